```python
import math
import jax, jax.numpy as jnp
from jax import lax
import numpy as np

D_MODEL = 1024
BATCH = 8
SEQ = 2048
DEPTH = 2

GRID_W = 64
CTX_LEN = 256
RMS_EPS = 1e-6
N_MOD = 9
D_FF = 2816
POOL_WINDOWS = (2, 4, 8, 16)
POOL_GROUPS = len(POOL_WINDOWS)
POOL_DIM = D_MODEL // 2
POOL_GROUP_DIM = POOL_DIM // POOL_GROUPS
MLA_HEADS = D_MODEL // 128
QK_NOPE_DIM = 64
QK_ROPE_DIM = 32
QK_HEAD_DIM = QK_NOPE_DIM + QK_ROPE_DIM
V_HEAD_DIM = 64
Q_LORA_RANK = 768
KV_LORA_RANK = 256
ROPE_AXIS_DIM = QK_ROPE_DIM // 2
ROPE_THETA = 10000.0
ATTN_SCALE = 1.0 / math.sqrt(QK_HEAD_DIM)
QBLOCK = 128
AB_IN_DIM = POOL_DIM + Q_LORA_RANK + KV_LORA_RANK + QK_ROPE_DIM
AB_OUT_DIM = POOL_DIM + MLA_HEADS * V_HEAD_DIM
CONV_DIM = D_MODEL
CONV_WIDTH = 3
N_EVEN = (DEPTH + 1) // 2
N_ODD = DEPTH // 2

kernel_name = 'hybrid_pool_mla_shortconv_dit_block'


def rmsnorm(x, g):
    xf = x.astype(jnp.float32)
    y = xf * lax.rsqrt(jnp.mean(xf * xf, axis=-1, keepdims=True) + RMS_EPS)
    return (y * g.astype(jnp.float32)).astype(x.dtype)


def modulation(cond, w, b):
    m = jax.nn.silu(cond) @ w + b
    return m.reshape(cond.shape[0], 1, N_MOD, cond.shape[-1])


def adaln(s, gain, m, k):
    shift, scale, gate = m[:, :, 3 * k], m[:, :, 3 * k + 1], m[:, :, 3 * k + 2]
    return rmsnorm(s, gain) * (1 + scale) + shift, gate


def swiglu(u, wg, wu, wd):
    return (jax.nn.silu(u @ wg) * (u @ wu)) @ wd


def ffn_half(s, m, k, gain, wg, wu, wd):
    u, gate = adaln(s, gain, m, k)
    return s + 0.5 * gate * swiglu(u, wg, wu, wd)


def multiscale_pool(u, w_grp, scale):
    b, l, _ = u.shape
    uf = u.astype(jnp.float32).reshape(b, l, POOL_GROUPS, POOL_GROUP_DIM)
    cs = jnp.concatenate([jnp.zeros_like(uf[:, :1]), jnp.cumsum(uf, axis=1)], axis=1)
    t = jnp.arange(l)
    outs = []
    for gi, w in enumerate(POOL_WINDOWS):
        lo = jnp.clip(t - w // 2, 0, l - 1)
        hi = jnp.clip(t + (w - w // 2 - 1), 0, l - 1)
        win_sum = cs[:, hi + 1, gi] - cs[:, lo, gi]
        cnt = (hi - lo + 1).astype(jnp.float32)[None, :, None]
        outs.append(win_sum / cnt - uf[:, :, gi])
    p = jnp.stack(outs, axis=2).astype(u.dtype)
    y = jnp.einsum('blgc,gcd->blgd', p, w_grp).reshape(b, l, POOL_DIM)
    return y * scale


def axial_rope_tables(length):
    rows = length // GRID_W
    row = jnp.repeat(jnp.arange(rows), GRID_W).astype(jnp.float32)
    col = jnp.tile(jnp.arange(GRID_W), rows).astype(jnp.float32)
    freqs = jnp.power(ROPE_THETA, -jnp.arange(0, ROPE_AXIS_DIM, 2, dtype=jnp.float32) / ROPE_AXIS_DIM)
    ang_r = (row[:, None] * freqs)[:, None, :]
    ang_c = (col[:, None] * freqs)[:, None, :]
    return (jnp.cos(ang_r), jnp.sin(ang_r), jnp.cos(ang_c), jnp.sin(ang_c))


def rot_half(z, cos, sin):
    h = z.shape[-1] // 2
    z1, z2 = z[..., :h], z[..., h:]
    return jnp.concatenate([z1 * cos - z2 * sin, z2 * cos + z1 * sin], axis=-1)


def apply_axial_rope(z, tabs):
    cos_r, sin_r, cos_c, sin_c = (tb.astype(z.dtype) for tb in tabs)
    return jnp.concatenate([rot_half(z[..., :ROPE_AXIS_DIM], cos_r, sin_r),
                            rot_half(z[..., ROPE_AXIS_DIM:], cos_c, sin_c)], axis=-1)


def mla_queries(cq, q_norm_g, w_uq):
    b, l, _ = cq.shape
    q = (rmsnorm(cq, q_norm_g) @ w_uq).reshape(b, l, MLA_HEADS, QK_HEAD_DIM)
    return q[..., :QK_NOPE_DIM], q[..., QK_NOPE_DIM:]


def mla_keys_values(ckv, kv_norm_g, w_ukv):
    b, l, _ = ckv.shape
    kv = (rmsnorm(ckv, kv_norm_g) @ w_ukv).reshape(b, l, MLA_HEADS, QK_NOPE_DIM + V_HEAD_DIM)
    return kv[..., :QK_NOPE_DIM], kv[..., QK_NOPE_DIM:]


def mla_attention(q_nope, q_rope, k_nope, k_rope, v):
    b, l, h, _ = q_nope.shape
    nb = l // QBLOCK

    def to_blocks(z):
        return z.reshape(b, nb, QBLOCK, h, z.shape[-1]).swapaxes(0, 1)

    def block(args):
        qn, qr = args
        s = (jnp.einsum('bqhd,bkhd->bhqk', qn, k_nope, preferred_element_type=jnp.float32)
             + jnp.einsum('bqhr,bkr->bhqk', qr, k_rope, preferred_element_type=jnp.float32))
        p = jax.nn.softmax(s * ATTN_SCALE, axis=-1).astype(v.dtype)
        return jnp.einsum('bhqk,bkhd->bqhd', p, v)

    o = lax.map(block, (to_blocks(q_nope), to_blocks(q_rope)))
    return o.swapaxes(0, 1).reshape(b, l, h * V_HEAD_DIM)


def pool_mla_mixer(uh, ug, tabs, w_in, pool_w, pool_scale, q_norm_g, w_uq, kv_norm_g, w_ukv, w_out, ctx_out):
    cuts = [POOL_DIM, POOL_DIM + Q_LORA_RANK, POOL_DIM + Q_LORA_RANK + KV_LORA_RANK]
    pool_h, cq_h, ckv_h, kr_h = jnp.split(uh @ w_in, cuts, axis=-1)
    pool_g, cq_g, ckv_g, kr_g = jnp.split(ug @ w_in, cuts, axis=-1)
    qn_h, qr_h = mla_queries(cq_h, q_norm_g, w_uq)
    qr_h = apply_axial_rope(qr_h, tabs)
    kn_h, v_h = mla_keys_values(ckv_h, kv_norm_g, w_ukv)
    kr_h = apply_axial_rope(kr_h[:, :, None, :], tabs)[:, :, 0]
    kn_g, v_g = mla_keys_values(ckv_g, kv_norm_g, w_ukv)
    attn_h = mla_attention(qn_h, qr_h,
                           jnp.concatenate([kn_h, kn_g], axis=1),
                           jnp.concatenate([kr_h, kr_g], axis=1),
                           jnp.concatenate([v_h, v_g], axis=1))
    out_h = jnp.concatenate([multiscale_pool(pool_h, pool_w, pool_scale), attn_h], axis=-1) @ w_out
    out_g = None
    if ctx_out:
        qn_g, qr_g = mla_queries(cq_g, q_norm_g, w_uq)
        attn_g = mla_attention(qn_g, qr_g, kn_g, kr_g, v_g)
        out_g = jnp.concatenate([multiscale_pool(pool_g, pool_w, pool_scale), attn_g], axis=-1) @ w_out
    return out_h, out_g


def short_conv_mixer(u, w_in, conv_w, w_out):
    b_gate, c_gate, val = jnp.split(u @ w_in, 3, axis=-1)
    z = c_gate * val
    z = lax.conv_general_dilated(z, conv_w[:, None, :].astype(z.dtype), window_strides=(1,),
                                 padding=((CONV_WIDTH // 2, CONV_WIDTH // 2),),
                                 dimension_numbers=('NWC', 'WIO', 'NWC'),
                                 feature_group_count=z.shape[-1])
    return (b_gate * z) @ w_out


def _fwd_setup_inputs(seed: int = 0) -> dict:
    key = jax.random.key(seed)
    ks = jax.random.split(key, 24)

    def nrm(k, shape, scale):
        return jax.random.normal(k, shape, jnp.float32) * scale

    D, F = D_MODEL, D_FF
    return {
        'x': nrm(ks[0], (BATCH, SEQ, D), 1.0),
        'c': nrm(ks[1], (BATCH, D), 1.0),
        'ctx': nrm(ks[2], (BATCH, CTX_LEN, D), 1.0),
        'c_ctx': nrm(ks[3], (D,), 1.0),
        'norm_g': 1.0 + nrm(ks[4], (DEPTH, 3, D), 0.02),
        'w_mod': nrm(ks[5], (DEPTH, D, N_MOD * D), 0.5 * D ** -0.5),
        'b_mod': nrm(ks[6], (DEPTH, N_MOD * D), 0.02),
        'ffn_w_gate': nrm(ks[7], (DEPTH, 2, D, F), D ** -0.5),
        'ffn_w_up': nrm(ks[8], (DEPTH, 2, D, F), D ** -0.5),
        'ffn_w_down': nrm(ks[9], (DEPTH, 2, F, D), F ** -0.5),
        'ab_w_in': nrm(ks[10], (N_EVEN, D, AB_IN_DIM), D ** -0.5),
        'pool_w': nrm(ks[11], (N_EVEN, POOL_GROUPS, POOL_GROUP_DIM, POOL_GROUP_DIM), POOL_GROUP_DIM ** -0.5),
        'pool_scale': 1.0 + nrm(ks[12], (N_EVEN, POOL_DIM), 0.1),
        'q_norm_g': 1.0 + nrm(ks[13], (N_EVEN, Q_LORA_RANK), 0.02),
        'w_uq': nrm(ks[14], (N_EVEN, Q_LORA_RANK, MLA_HEADS * QK_HEAD_DIM), Q_LORA_RANK ** -0.5),
        'kv_norm_g': 1.0 + nrm(ks[15], (N_EVEN, KV_LORA_RANK), 0.02),
        'w_ukv': nrm(ks[16], (N_EVEN, KV_LORA_RANK, MLA_HEADS * (QK_NOPE_DIM + V_HEAD_DIM)), KV_LORA_RANK ** -0.5),
        'ab_w_out': nrm(ks[17], (N_EVEN, AB_OUT_DIM, D), AB_OUT_DIM ** -0.5),
        'conv_w_in': nrm(ks[18], (N_ODD, D, 3 * CONV_DIM), D ** -0.5),
        'conv_w': nrm(ks[19], (N_ODD, CONV_WIDTH, CONV_DIM), CONV_WIDTH ** -0.5),
        'conv_w_out': nrm(ks[20], (N_ODD, CONV_DIM, D), CONV_DIM ** -0.5),
        'final_norm_g': 1.0 + nrm(ks[21], (D,), 0.02),
    }


def _fwd_reference(x, c, ctx, c_ctx, norm_g, w_mod, b_mod, ffn_w_gate, ffn_w_up, ffn_w_down,
              ab_w_in, pool_w, pool_scale, q_norm_g, w_uq, kv_norm_g, w_ukv, ab_w_out,
              conv_w_in, conv_w, conv_w_out, final_norm_g):
    tabs = axial_rope_tables(x.shape[1])
    h, g = x, ctx
    cond_g = c_ctx[None, :]
    for i in range(DEPTH):
        last = i == DEPTH - 1
        even = i % 2 == 0
        j = i // 2
        ctx_out = not last
        need_g = even or ctx_out
        m_h = modulation(c, w_mod[i], b_mod[i])
        h = ffn_half(h, m_h, 0, norm_g[i, 0], ffn_w_gate[i, 0], ffn_w_up[i, 0], ffn_w_down[i, 0])
        if need_g:
            m_g = modulation(cond_g, w_mod[i], b_mod[i])
            g = ffn_half(g, m_g, 0, norm_g[i, 0], ffn_w_gate[i, 0], ffn_w_up[i, 0], ffn_w_down[i, 0])
        uh, gate_h = adaln(h, norm_g[i, 1], m_h, 1)
        out_g = None
        if even:
            ug, gate_g = adaln(g, norm_g[i, 1], m_g, 1)
            out_h, out_g = pool_mla_mixer(uh, ug, tabs, ab_w_in[j], pool_w[j], pool_scale[j],
                                          q_norm_g[j], w_uq[j], kv_norm_g[j], w_ukv[j], ab_w_out[j], ctx_out)
        else:
            out_h = short_conv_mixer(uh, conv_w_in[j], conv_w[j], conv_w_out[j])
            if ctx_out:
                ug, gate_g = adaln(g, norm_g[i, 1], m_g, 1)
                out_g = short_conv_mixer(ug, conv_w_in[j], conv_w[j], conv_w_out[j])
        h = h + gate_h * out_h
        h = ffn_half(h, m_h, 2, norm_g[i, 2], ffn_w_gate[i, 1], ffn_w_up[i, 1], ffn_w_down[i, 1])
        if ctx_out:
            g = g + gate_g * out_g
            g = ffn_half(g, m_g, 2, norm_g[i, 2], ffn_w_gate[i, 1], ffn_w_up[i, 1], ffn_w_down[i, 1])
    return rmsnorm(h, final_norm_g)


import jax as _jax
import jax.numpy as _jnp

TWIN_FORMAT = 'train_step'
FWD_PARAMS = ['x', 'c', 'ctx', 'c_ctx', 'norm_g', 'w_mod', 'b_mod', 'ffn_w_gate', 'ffn_w_up', 'ffn_w_down', 'ab_w_in', 'pool_w', 'pool_scale', 'q_norm_g', 'w_uq', 'kv_norm_g', 'w_ukv', 'ab_w_out', 'conv_w_in', 'conv_w', 'conv_w_out', 'final_norm_g']
TWIN_WEIGHTS = ['c_ctx', 'norm_g', 'w_mod', 'b_mod', 'ffn_w_gate', 'ffn_w_up', 'ffn_w_down', 'ab_w_in', 'pool_w', 'pool_scale', 'q_norm_g', 'w_uq', 'kv_norm_g', 'w_ukv', 'ab_w_out', 'conv_w_in', 'conv_w', 'conv_w_out', 'final_norm_g']
TWIN_DIFF_INPUT = 'x'
TWIN_INPUTS = ['x', 'c', 'ctx', 'c_ctx', 'norm_g', 'w_mod', 'b_mod', 'ffn_w_gate', 'ffn_w_up', 'ffn_w_down', 'ab_w_in', 'pool_w', 'pool_scale', 'q_norm_g', 'w_uq', 'kv_norm_g', 'w_ukv', 'ab_w_out', 'conv_w_in', 'conv_w', 'conv_w_out', 'final_norm_g', 'loss_target', 'm_c_ctx', 'm_norm_g', 'm_w_mod', 'm_b_mod', 'm_ffn_w_gate', 'm_ffn_w_up', 'm_ffn_w_down', 'm_ab_w_in', 'm_pool_w', 'm_pool_scale', 'm_q_norm_g', 'm_w_uq', 'm_kv_norm_g', 'm_w_ukv', 'm_ab_w_out', 'm_conv_w_in', 'm_conv_w', 'm_conv_w_out', 'm_final_norm_g', 'v_c_ctx', 'v_norm_g', 'v_w_mod', 'v_b_mod', 'v_ffn_w_gate', 'v_ffn_w_up', 'v_ffn_w_down', 'v_ab_w_in', 'v_pool_w', 'v_pool_scale', 'v_q_norm_g', 'v_w_uq', 'v_kv_norm_g', 'v_w_ukv', 'v_ab_w_out', 'v_conv_w_in', 'v_conv_w', 'v_conv_w_out', 'v_final_norm_g']
TWIN_OUTPUTS = ['loss', 'grad_x', 'grad_c_ctx', 'grad_norm_g', 'grad_w_mod', 'grad_b_mod', 'grad_ffn_w_gate', 'grad_ffn_w_up', 'grad_ffn_w_down', 'grad_ab_w_in', 'grad_pool_w', 'grad_pool_scale', 'grad_q_norm_g', 'grad_w_uq', 'grad_kv_norm_g', 'grad_w_ukv', 'grad_ab_w_out', 'grad_conv_w_in', 'grad_conv_w', 'grad_conv_w_out', 'grad_final_norm_g', 'delta_c_ctx', 'delta_norm_g', 'delta_w_mod', 'delta_b_mod', 'delta_ffn_w_gate', 'delta_ffn_w_up', 'delta_ffn_w_down', 'delta_ab_w_in', 'delta_pool_w', 'delta_pool_scale', 'delta_q_norm_g', 'delta_w_uq', 'delta_kv_norm_g', 'delta_w_ukv', 'delta_ab_w_out', 'delta_conv_w_in', 'delta_conv_w', 'delta_conv_w_out', 'delta_final_norm_g', 'new_m_c_ctx', 'new_m_norm_g', 'new_m_w_mod', 'new_m_b_mod', 'new_m_ffn_w_gate', 'new_m_ffn_w_up', 'new_m_ffn_w_down', 'new_m_ab_w_in', 'new_m_pool_w', 'new_m_pool_scale', 'new_m_q_norm_g', 'new_m_w_uq', 'new_m_kv_norm_g', 'new_m_w_ukv', 'new_m_ab_w_out', 'new_m_conv_w_in', 'new_m_conv_w', 'new_m_conv_w_out', 'new_m_final_norm_g', 'new_v_c_ctx', 'new_v_norm_g', 'new_v_w_mod', 'new_v_b_mod', 'new_v_ffn_w_gate', 'new_v_ffn_w_up', 'new_v_ffn_w_down', 'new_v_ab_w_in', 'new_v_pool_w', 'new_v_pool_scale', 'new_v_q_norm_g', 'new_v_w_uq', 'new_v_kv_norm_g', 'new_v_w_ukv', 'new_v_ab_w_out', 'new_v_conv_w_in', 'new_v_conv_w', 'new_v_conv_w_out', 'new_v_final_norm_g']
TWIN_LEAF_KINDS = {'loss': 'loss', 'grad_x': 'grad_x', 'grad_c_ctx': 'grad_w', 'grad_norm_g': 'grad_w', 'grad_w_mod': 'grad_w', 'grad_b_mod': 'grad_w', 'grad_ffn_w_gate': 'grad_w', 'grad_ffn_w_up': 'grad_w', 'grad_ffn_w_down': 'grad_w', 'grad_ab_w_in': 'grad_w', 'grad_pool_w': 'grad_w', 'grad_pool_scale': 'grad_w', 'grad_q_norm_g': 'grad_w', 'grad_w_uq': 'grad_w', 'grad_kv_norm_g': 'grad_w', 'grad_w_ukv': 'grad_w', 'grad_ab_w_out': 'grad_w', 'grad_conv_w_in': 'grad_w', 'grad_conv_w': 'grad_w', 'grad_conv_w_out': 'grad_w', 'grad_final_norm_g': 'grad_w', 'delta_c_ctx': 'delta_w', 'delta_norm_g': 'delta_w', 'delta_w_mod': 'delta_w', 'delta_b_mod': 'delta_w', 'delta_ffn_w_gate': 'delta_w', 'delta_ffn_w_up': 'delta_w', 'delta_ffn_w_down': 'delta_w', 'delta_ab_w_in': 'delta_w', 'delta_pool_w': 'delta_w', 'delta_pool_scale': 'delta_w', 'delta_q_norm_g': 'delta_w', 'delta_w_uq': 'delta_w', 'delta_kv_norm_g': 'delta_w', 'delta_w_ukv': 'delta_w', 'delta_ab_w_out': 'delta_w', 'delta_conv_w_in': 'delta_w', 'delta_conv_w': 'delta_w', 'delta_conv_w_out': 'delta_w', 'delta_final_norm_g': 'delta_w', 'new_m_c_ctx': 'new_m', 'new_m_norm_g': 'new_m', 'new_m_w_mod': 'new_m', 'new_m_b_mod': 'new_m', 'new_m_ffn_w_gate': 'new_m', 'new_m_ffn_w_up': 'new_m', 'new_m_ffn_w_down': 'new_m', 'new_m_ab_w_in': 'new_m', 'new_m_pool_w': 'new_m', 'new_m_pool_scale': 'new_m', 'new_m_q_norm_g': 'new_m', 'new_m_w_uq': 'new_m', 'new_m_kv_norm_g': 'new_m', 'new_m_w_ukv': 'new_m', 'new_m_ab_w_out': 'new_m', 'new_m_conv_w_in': 'new_m', 'new_m_conv_w': 'new_m', 'new_m_conv_w_out': 'new_m', 'new_m_final_norm_g': 'new_m', 'new_v_c_ctx': 'new_v', 'new_v_norm_g': 'new_v', 'new_v_w_mod': 'new_v', 'new_v_b_mod': 'new_v', 'new_v_ffn_w_gate': 'new_v', 'new_v_ffn_w_up': 'new_v', 'new_v_ffn_w_down': 'new_v', 'new_v_ab_w_in': 'new_v', 'new_v_pool_w': 'new_v', 'new_v_pool_scale': 'new_v', 'new_v_q_norm_g': 'new_v', 'new_v_w_uq': 'new_v', 'new_v_kv_norm_g': 'new_v', 'new_v_w_ukv': 'new_v', 'new_v_ab_w_out': 'new_v', 'new_v_conv_w_in': 'new_v', 'new_v_conv_w': 'new_v', 'new_v_conv_w_out': 'new_v', 'new_v_final_norm_g': 'new_v'}


def _forward(args):
    return _fwd_reference(*[args[k] for k in FWD_PARAMS])


def _output_shape():
    out = _jax.eval_shape(lambda: _forward(_fwd_setup_inputs(0)))
    return out.shape, out.dtype

N_MICROBATCH = 1
ADAM_LR = 0.001
ADAM_B1 = 0.9
ADAM_B2 = 0.999
ADAM_EPS = 1e-08
ADAM_WD = 0.01
ADAM_STEP = 10
PER_EXAMPLE_BATCH_AXIS = {'x': 0, 'c': 0, 'ctx': 0, 'loss_target': 0}
SHARED_INPUTS = []
_WEIGHT_DTYPES = {'c_ctx': _jnp.float32, 'norm_g': _jnp.float32, 'w_mod': _jnp.float32, 'b_mod': _jnp.float32, 'ffn_w_gate': _jnp.float32, 'ffn_w_up': _jnp.float32, 'ffn_w_down': _jnp.float32, 'ab_w_in': _jnp.float32, 'pool_w': _jnp.float32, 'pool_scale': _jnp.float32, 'q_norm_g': _jnp.float32, 'w_uq': _jnp.float32, 'kv_norm_g': _jnp.float32, 'w_ukv': _jnp.float32, 'ab_w_out': _jnp.float32, 'conv_w_in': _jnp.float32, 'conv_w': _jnp.float32, 'conv_w_out': _jnp.float32, 'final_norm_g': _jnp.float32}
MOMENT_SCALE = {'c_ctx': 4.109286e-03, 'norm_g': 3.585826e-02, 'w_mod': 3.539739e-02, 'b_mod': 5.697261e-02, 'ffn_w_gate': 9.122495e-03, 'ffn_w_up': 8.835459e-03, 'ffn_w_down': 1.463230e-02, 'ab_w_in': 2.360231e-02, 'pool_w': 3.972119e-02, 'pool_scale': 3.862030e-02, 'q_norm_g': 3.491966e-03, 'w_uq': 3.447800e-03, 'kv_norm_g': 1.809060e-02, 'w_ukv': 8.193505e-03, 'ab_w_out': 2.898105e-02, 'conv_w_in': 4.365015e-02, 'conv_w': 4.444111e-02, 'conv_w_out': 4.399267e-02, 'final_norm_g': 1.605710e+01}


def _to_microbatches(a, axis):
    t = _jnp.moveaxis(a, axis, 0)
    t = t.reshape((N_MICROBATCH, t.shape[0] // N_MICROBATCH) + t.shape[1:])
    return _jnp.moveaxis(t, 1, axis + 1)


def setup_inputs(seed: int = 0) -> dict:
    inp = _fwd_setup_inputs(seed)
    key = _jax.random.fold_in(_jax.random.key(seed), 7919)
    shape, _ = _output_shape()
    out = dict(inp)
    out["loss_target"] = _jax.random.normal(_jax.random.fold_in(key, 0), shape, _jnp.float32)
    for i, name in enumerate(TWIN_WEIGHTS):
        w = inp[name].astype(_jnp.float32)
        if MOMENT_SCALE is None:
            s = _jnp.sqrt(_jnp.mean(_jnp.square(w)) + 1e-30)
        else:
            s = MOMENT_SCALE[name]
        km, kv = _jax.random.split(_jax.random.fold_in(key, i + 1))
        out[name] = w
        out["m_" + name] = s * _jax.random.normal(km, w.shape, _jnp.float32)
        out["v_" + name] = (s * s) * _jax.random.uniform(kv, w.shape, _jnp.float32, 0.5, 1.5)
    if N_MICROBATCH > 1:
        for name, axis in PER_EXAMPLE_BATCH_AXIS.items():
            out[name] = _to_microbatches(out[name], axis)
    return {'x': out['x'], 'c': out['c'], 'ctx': out['ctx'], 'c_ctx': out['c_ctx'], 'norm_g': out['norm_g'], 'w_mod': out['w_mod'], 'b_mod': out['b_mod'], 'ffn_w_gate': out['ffn_w_gate'], 'ffn_w_up': out['ffn_w_up'], 'ffn_w_down': out['ffn_w_down'], 'ab_w_in': out['ab_w_in'], 'pool_w': out['pool_w'], 'pool_scale': out['pool_scale'], 'q_norm_g': out['q_norm_g'], 'w_uq': out['w_uq'], 'kv_norm_g': out['kv_norm_g'], 'w_ukv': out['w_ukv'], 'ab_w_out': out['ab_w_out'], 'conv_w_in': out['conv_w_in'], 'conv_w': out['conv_w'], 'conv_w_out': out['conv_w_out'], 'final_norm_g': out['final_norm_g'], 'loss_target': out['loss_target'], 'm_c_ctx': out['m_c_ctx'], 'm_norm_g': out['m_norm_g'], 'm_w_mod': out['m_w_mod'], 'm_b_mod': out['m_b_mod'], 'm_ffn_w_gate': out['m_ffn_w_gate'], 'm_ffn_w_up': out['m_ffn_w_up'], 'm_ffn_w_down': out['m_ffn_w_down'], 'm_ab_w_in': out['m_ab_w_in'], 'm_pool_w': out['m_pool_w'], 'm_pool_scale': out['m_pool_scale'], 'm_q_norm_g': out['m_q_norm_g'], 'm_w_uq': out['m_w_uq'], 'm_kv_norm_g': out['m_kv_norm_g'], 'm_w_ukv': out['m_w_ukv'], 'm_ab_w_out': out['m_ab_w_out'], 'm_conv_w_in': out['m_conv_w_in'], 'm_conv_w': out['m_conv_w'], 'm_conv_w_out': out['m_conv_w_out'], 'm_final_norm_g': out['m_final_norm_g'], 'v_c_ctx': out['v_c_ctx'], 'v_norm_g': out['v_norm_g'], 'v_w_mod': out['v_w_mod'], 'v_b_mod': out['v_b_mod'], 'v_ffn_w_gate': out['v_ffn_w_gate'], 'v_ffn_w_up': out['v_ffn_w_up'], 'v_ffn_w_down': out['v_ffn_w_down'], 'v_ab_w_in': out['v_ab_w_in'], 'v_pool_w': out['v_pool_w'], 'v_pool_scale': out['v_pool_scale'], 'v_q_norm_g': out['v_q_norm_g'], 'v_w_uq': out['v_w_uq'], 'v_kv_norm_g': out['v_kv_norm_g'], 'v_w_ukv': out['v_w_ukv'], 'v_ab_w_out': out['v_ab_w_out'], 'v_conv_w_in': out['v_conv_w_in'], 'v_conv_w': out['v_conv_w'], 'v_conv_w_out': out['v_conv_w_out'], 'v_final_norm_g': out['v_final_norm_g']}


def _loss(weights, diff, rest, loss_target):
    with _jax.named_scope("forward"):
        args = {**rest, TWIN_DIFF_INPUT: diff, **{k: w.astype(_WEIGHT_DTYPES[k]) for k, w in weights.items()}}
        y = _forward(args)
    with _jax.named_scope("loss_head"):
        err = _jnp.square(y.astype(_jnp.float32) - loss_target)
        return 0.5 * _jnp.sum(_jnp.mean(err, axis=-1)) if err.ndim else 0.5 * err


def _adamw(w, g, m, v):
    m = ADAM_B1 * m + (1.0 - ADAM_B1) * g
    v = ADAM_B2 * v + (1.0 - ADAM_B2) * _jnp.square(g)
    m_hat = m / (1.0 - ADAM_B1 ** ADAM_STEP)
    v_hat = v / (1.0 - ADAM_B2 ** ADAM_STEP)
    delta = -ADAM_LR * (m_hat / (_jnp.sqrt(v_hat) + ADAM_EPS) + ADAM_WD * w)
    return delta, m, v


def reference(x, c, ctx, c_ctx, norm_g, w_mod, b_mod, ffn_w_gate, ffn_w_up, ffn_w_down, ab_w_in, pool_w, pool_scale, q_norm_g, w_uq, kv_norm_g, w_ukv, ab_w_out, conv_w_in, conv_w, conv_w_out, final_norm_g, loss_target, m_c_ctx, m_norm_g, m_w_mod, m_b_mod, m_ffn_w_gate, m_ffn_w_up, m_ffn_w_down, m_ab_w_in, m_pool_w, m_pool_scale, m_q_norm_g, m_w_uq, m_kv_norm_g, m_w_ukv, m_ab_w_out, m_conv_w_in, m_conv_w, m_conv_w_out, m_final_norm_g, v_c_ctx, v_norm_g, v_w_mod, v_b_mod, v_ffn_w_gate, v_ffn_w_up, v_ffn_w_down, v_ab_w_in, v_pool_w, v_pool_scale, v_q_norm_g, v_w_uq, v_kv_norm_g, v_w_ukv, v_ab_w_out, v_conv_w_in, v_conv_w, v_conv_w_out, v_final_norm_g):
    given = dict(x=x, c=c, ctx=ctx, c_ctx=c_ctx, norm_g=norm_g, w_mod=w_mod, b_mod=b_mod, ffn_w_gate=ffn_w_gate, ffn_w_up=ffn_w_up, ffn_w_down=ffn_w_down, ab_w_in=ab_w_in, pool_w=pool_w, pool_scale=pool_scale, q_norm_g=q_norm_g, w_uq=w_uq, kv_norm_g=kv_norm_g, w_ukv=w_ukv, ab_w_out=ab_w_out, conv_w_in=conv_w_in, conv_w=conv_w, conv_w_out=conv_w_out, final_norm_g=final_norm_g, loss_target=loss_target, m_c_ctx=m_c_ctx, m_norm_g=m_norm_g, m_w_mod=m_w_mod, m_b_mod=m_b_mod, m_ffn_w_gate=m_ffn_w_gate, m_ffn_w_up=m_ffn_w_up, m_ffn_w_down=m_ffn_w_down, m_ab_w_in=m_ab_w_in, m_pool_w=m_pool_w, m_pool_scale=m_pool_scale, m_q_norm_g=m_q_norm_g, m_w_uq=m_w_uq, m_kv_norm_g=m_kv_norm_g, m_w_ukv=m_w_ukv, m_ab_w_out=m_ab_w_out, m_conv_w_in=m_conv_w_in, m_conv_w=m_conv_w, m_conv_w_out=m_conv_w_out, m_final_norm_g=m_final_norm_g, v_c_ctx=v_c_ctx, v_norm_g=v_norm_g, v_w_mod=v_w_mod, v_b_mod=v_b_mod, v_ffn_w_gate=v_ffn_w_gate, v_ffn_w_up=v_ffn_w_up, v_ffn_w_down=v_ffn_w_down, v_ab_w_in=v_ab_w_in, v_pool_w=v_pool_w, v_pool_scale=v_pool_scale, v_q_norm_g=v_q_norm_g, v_w_uq=v_w_uq, v_kv_norm_g=v_kv_norm_g, v_w_ukv=v_w_ukv, v_ab_w_out=v_ab_w_out, v_conv_w_in=v_conv_w_in, v_conv_w=v_conv_w, v_conv_w_out=v_conv_w_out, v_final_norm_g=v_final_norm_g)
    weights = {n: given[n] for n in TWIN_WEIGHTS}
    shared = {n: given[n] for n in SHARED_INPUTS}
    per_example = {n: given[n] for n in ['x', 'c', 'ctx']}
    grad_fn = _jax.value_and_grad(_loss, argnums=(0, 1))

    def one_microbatch(ex, loss_target):
        ex = dict(ex)
        diff = ex.pop(TWIN_DIFF_INPUT)
        return grad_fn(weights, diff, {**shared, **ex}, loss_target)

    if N_MICROBATCH == 1:
        loss, (grad_w, grad_x) = one_microbatch(per_example, given["loss_target"])
    else:
        def body(carry, xs):
            loss_sum, grad_sum = carry
            l_k, (gw_k, gx_k) = one_microbatch(xs[0], xs[1])
            with _jax.named_scope("update"):
                return (loss_sum + l_k, _jax.tree.map(_jnp.add, grad_sum, gw_k)), gx_k

        init = (_jnp.zeros((), _jnp.float32), _jax.tree.map(_jnp.zeros_like, weights))
        (loss, grad_w), grad_x = _jax.lax.scan(body, init, (per_example, given["loss_target"]))
    with _jax.named_scope("update"):
        delta_w, new_m, new_v = {}, {}, {}
        for n in TWIN_WEIGHTS:
            delta_w[n], new_m[n], new_v[n] = _adamw(weights[n], grad_w[n], given["m_" + n], given["v_" + n])
    return (loss, grad_x, *[grad_w[n] for n in TWIN_WEIGHTS], *[delta_w[n] for n in TWIN_WEIGHTS],
            *[new_m[n] for n in TWIN_WEIGHTS], *[new_v[n] for n in TWIN_WEIGHTS])
```

```python
import functools
import math

import jax
import jax.numpy as jnp
from jax import lax
from jax.experimental import pallas as pl
from jax.experimental.pallas import tpu as pltpu

F32 = jnp.float32
BF16 = jnp.bfloat16
MESH = pl.DeviceIdType.MESH
SDS = jax.ShapeDtypeStruct

N_DEV = 8
D_MODEL = 1024
N_MOD = 9
D_FF = 2816
POOL_WINDOWS = (2, 4, 8, 16)
POOL_DIM = 512
POOL_GROUP_DIM = 128
HEADS = 8
QK_NOPE = 64
QK_ROPE = 32
QK_HEAD = QK_NOPE + QK_ROPE
V_HEAD = 64
Q_RANK = 768
KV_RANK = 256
GRID_W = 64
ROPE_THETA = 10000.0
RMS_EPS = 1e-6
ATTN_SCALE = 1.0 / math.sqrt(QK_HEAD)
HEAD_PAD = 128
POOL_PAD = 16
PA_POOL, PA_CQ, PA_KV = 0, 768, 1536
PA_KV_W = 384
PA_W = PA_KV + PA_KV_W

ADAM_LR, ADAM_B1, ADAM_B2, ADAM_EPS, ADAM_WD, ADAM_STEP = 0.001, 0.9, 0.999, 1e-08, 0.01, 10

VMEM_LIMIT_BYTES = 56 * 1024 * 1024

NN = ((1,), (0,))
NT = ((1,), (1,))
TN = ((0,), (0,))


def _cparams():
    return pltpu.CompilerParams(vmem_limit_bytes=VMEM_LIMIT_BYTES)


def _dot(a, b, dims):
    return lax.dot_general(a, b, (dims, ((), ())), preferred_element_type=F32)


def _tile(n, cap, mult=8):
    t = (min(cap, n) // mult) * mult
    while t >= mult:
        if n % t == 0:
            return t
        t -= mult
    return n


def _colsum(x):
    return jnp.sum(x, axis=0, keepdims=True)


def _rms(x):
    r = lax.rsqrt(jnp.mean(x * x, axis=-1, keepdims=True) + RMS_EPS)
    return x * r, r


def _rms_bwd(n, r, dn):
    return r * (dn - n * jnp.mean(dn * n, axis=-1, keepdims=True))


def _rowwise(name, fn, t_rows, tm, n_lat, rows, vecs, outs, accs):
    nt = t_rows // tm
    nlt = n_lat // tm
    n_groups = 2 if nlt < nt else 1

    def grp(i):
        return jnp.where(i >= nlt, 1, 0) if n_groups == 2 else 0

    in_specs = [pl.BlockSpec((tm, w), functools.partial(lambda i, cb: (i, cb), cb=cb)) for (_, w, cb) in rows]
    in_specs += [pl.BlockSpec((1,) + v.shape[1:], lambda i: (grp(i), 0, 0)) for v in vecs]
    out_specs = [pl.BlockSpec((tm, w), lambda i: (i, 0)) for (w, _) in outs]
    out_specs += [pl.BlockSpec((1, 1, w), lambda i: (grp(i), 0, 0)) for w in accs]
    out_shape = [SDS((t_rows, w), dt) for (w, dt) in outs] + [SDS((n_groups, 1, w), F32) for w in accs]
    n_r, n_v, n_o = len(rows), len(vecs), len(outs)

    def body(*refs):
        row_vals = [r[...] for r in refs[:n_r]]
        vec_vals = [v[0] for v in refs[n_r:n_r + n_v]]
        out_refs = refs[n_r + n_v:n_r + n_v + n_o]
        acc_refs = refs[n_r + n_v + n_o:]
        out_vals, acc_vals = fn(row_vals, vec_vals)
        for o_ref, o in zip(out_refs, out_vals):
            o_ref[...] = o.astype(o_ref.dtype)
        if acc_refs:
            i = pl.program_id(0)
            first = (i == 0) | (i == nlt) if n_groups == 2 else i == 0

            @pl.when(first)
            def _():
                for a_ref, a in zip(acc_refs, acc_vals):
                    a_ref[0] = a

            @pl.when(jnp.logical_not(first))
            def _():
                for a_ref, a in zip(acc_refs, acc_vals):
                    a_ref[0] += a

    res = pl.pallas_call(
        body, name=name, grid=(nt,), in_specs=in_specs, out_specs=out_specs, out_shape=out_shape,
        compiler_params=_cparams(),
    )(*[r[0] for r in rows], *vecs)
    return res[:n_o], res[n_o:]


def _mm(name, pairs, mode, out_dtype, tm_cap=512, tn_cap=512, weights_outer=False, bias=None):
    a0, b0 = pairs[0]
    if mode == "nn":
        m, n, dims = a0.shape[0], b0.shape[1], NN
    elif mode == "nt":
        m, n, dims = a0.shape[0], b0.shape[0], NT
    else:
        m, n, dims = a0.shape[1], b0.shape[1], TN
    tm, tn = _tile(m, tm_cap, 128 if mode == "tn" else 16), _tile(n, tn_cap, 128)
    if weights_outer:
        grid = (n // tn, m // tm)
        ij = lambda g0, g1: (g1, g0)
    else:
        grid = (m // tm, n // tn)
        ij = lambda g0, g1: (g0, g1)

    def a_spec(a):
        if mode == "tn":
            return pl.BlockSpec((a.shape[0], tm), lambda g0, g1: (0, ij(g0, g1)[0]))
        return pl.BlockSpec((tm, a.shape[1]), lambda g0, g1: (ij(g0, g1)[0], 0))

    def b_spec(b):
        if mode == "nt":
            return pl.BlockSpec((tn, b.shape[1]), lambda g0, g1: (ij(g0, g1)[1], 0))
        return pl.BlockSpec((b.shape[0], tn), lambda g0, g1: (0, ij(g0, g1)[1]))

    in_specs, flat = [], []
    for a, b in pairs:
        in_specs += [a_spec(a), b_spec(b)]
        flat += [a, b]
    if bias is not None:
        in_specs.append(pl.BlockSpec((1, tn), lambda g0, g1: (0, ij(g0, g1)[1])))
        flat.append(bias)
    n_pairs = len(pairs)

    def body(*refs):
        acc = None
        for p in range(n_pairs):
            t = _dot(refs[2 * p][...], refs[2 * p + 1][...], dims)
            acc = t if acc is None else acc + t
        if bias is not None:
            acc = acc + refs[2 * n_pairs][...]
        refs[-1][...] = acc.astype(refs[-1].dtype)

    return pl.pallas_call(
        body, name=name, grid=grid, in_specs=in_specs,
        out_specs=pl.BlockSpec((tm, tn), lambda g0, g1: ij(g0, g1)),
        out_shape=SDS((m, n), out_dtype), compiler_params=_cparams(),
    )(*flat)


def _mm_resid(name, a, b, s, mg, k, coef, n_lat, tn_cap=512):
    t_rows, n = a.shape[0], b.shape[1]
    tm = _tile(math.gcd(n_lat, t_rows), 256, 16)
    tn = _tile(n, tn_cap, 128)
    nlt = n_lat // tm
    n_groups = 2 if nlt < t_rows // tm else 1

    def grp(i):
        return jnp.where(i >= nlt, 1, 0) if n_groups == 2 else 0

    def body(a_ref, b_ref, s_ref, mg_ref, so_ref, o_ref):
        o = _dot(a_ref[...], b_ref[...], NN)
        gate = mg_ref[0, 3 * k + 2:3 * k + 3, :]
        o_ref[...] = o
        so_ref[...] = s_ref[...] + (coef * gate) * o

    return pl.pallas_call(
        body, name=name, grid=(n // tn, t_rows // tm),
        in_specs=[pl.BlockSpec((tm, a.shape[1]), lambda j, i: (i, 0)),
                  pl.BlockSpec((b.shape[0], tn), lambda j, i: (0, j)),
                  pl.BlockSpec((tm, tn), lambda j, i: (i, j)),
                  pl.BlockSpec((1, mg.shape[1], tn), lambda j, i: (grp(i), 0, j))],
        out_specs=[pl.BlockSpec((tm, tn), lambda j, i: (i, j)), pl.BlockSpec((tm, tn), lambda j, i: (i, j))],
        out_shape=[SDS((t_rows, n), F32), SDS((t_rows, n), F32)], compiler_params=_cparams(),
    )(a, b, s, mg)


def _ffn_up(name, u, wg_t, wu_t):
    t_rows, f = u.shape[0], wg_t.shape[0]
    tm, tn = _tile(t_rows, 1024, 16), _tile(f, 256, 128)

    def body(u_ref, wg_ref, wu_ref, a_ref, b_ref, h_ref):
        uu = u_ref[...]
        a = _dot(uu, wg_ref[...], NT)
        b = _dot(uu, wu_ref[...], NT)
        a_ref[...] = a.astype(BF16)
        b_ref[...] = b.astype(BF16)
        h_ref[...] = (a * jax.nn.sigmoid(a) * b).astype(BF16)

    w_spec = pl.BlockSpec((tn, u.shape[1]), lambda i, j: (j, 0))
    o_spec = pl.BlockSpec((tm, tn), lambda i, j: (i, j))
    return pl.pallas_call(
        body, name=name, grid=(t_rows // tm, f // tn),
        in_specs=[pl.BlockSpec((tm, u.shape[1]), lambda i, j: (i, 0)), w_spec, w_spec],
        out_specs=[o_spec, o_spec, o_spec], out_shape=[SDS((t_rows, f), BF16)] * 3, compiler_params=_cparams(),
    )(u, wg_t, wu_t)


def _ffn_dact(name, do, wd, a, b):
    t_rows, f = do.shape[0], wd.shape[0]
    tm, tn = _tile(t_rows, 1024, 16), _tile(f, 256, 128)

    def body(do_ref, wd_ref, a_ref, b_ref, da_ref, db_ref):
        dh = _dot(do_ref[...], wd_ref[...], NT)
        av = a_ref[...].astype(F32)
        bv = b_ref[...].astype(F32)
        sg = jax.nn.sigmoid(av)
        da_ref[...] = (dh * bv * (sg * (1.0 + av * (1.0 - sg)))).astype(BF16)
        db_ref[...] = (dh * (av * sg)).astype(BF16)

    t_spec = pl.BlockSpec((tm, tn), lambda i, j: (i, j))
    return pl.pallas_call(
        body, name=name, grid=(t_rows // tm, f // tn),
        in_specs=[pl.BlockSpec((tm, do.shape[1]), lambda i, j: (i, 0)),
                  pl.BlockSpec((tn, wd.shape[1]), lambda i, j: (j, 0)), t_spec, t_spec],
        out_specs=[t_spec, t_spec], out_shape=[SDS((t_rows, f), BF16)] * 2, compiler_params=_cparams(),
    )(do, wd, a, b)


def _row_tm(t_rows, n_lat):
    return _tile(math.gcd(t_rows, n_lat), 256, 16)


def _adaln_fwd(name, s, mg, k, n_lat):
    t_rows = s.shape[0]

    def fn(rv, vv):
        m = vv[0]
        n, _ = _rms(rv[0])
        u = (n * m[9 + k:10 + k]) * (1.0 + m[3 * k + 1:3 * k + 2]) + m[3 * k:3 * k + 1]
        return [u], []

    (u,), _ = _rowwise(name, fn, t_rows, _row_tm(t_rows, n_lat), n_lat, [(s, D_MODEL, 0)], [mg], [(D_MODEL, BF16)], [])
    return u


def _adaln_bwd(name, s, du, ds_out, mg, k, n_lat):
    t_rows = s.shape[0]

    def fn(rv, vv):
        m = vv[0]
        gain, scale = m[9 + k:10 + k], m[3 * k + 1:3 * k + 2]
        n, r = _rms(rv[0])
        d_u = rv[1]
        dxn = d_u * (1.0 + scale)
        ds = _rms_bwd(n, r, dxn * gain)
        return [rv[2] + ds], [_colsum(d_u), _colsum(d_u * (n * gain)), _colsum(dxn * n)]

    (ds_in,), accs = _rowwise(name, fn, t_rows, _row_tm(t_rows, n_lat), n_lat,
                              [(s, D_MODEL, 0), (du, D_MODEL, 0), (ds_out, D_MODEL, 0)], [mg],
                              [(D_MODEL, F32)], [D_MODEL] * 3)
    return ds_in, accs


def _gate_bwd(name, ds_out, o, mg, k, coef, n_lat):
    t_rows = o.shape[0]

    def fn(rv, vv):
        gate = vv[0][3 * k + 2:3 * k + 3]
        d = coef * rv[0]
        return [d * gate], [_colsum(d * rv[1])]

    (do,), (dgate,) = _rowwise(name, fn, t_rows, _row_tm(t_rows, n_lat), n_lat,
                               [(ds_out, D_MODEL, 0), (o, D_MODEL, 0)], [mg], [(D_MODEL, BF16)], [D_MODEL])
    return do, dgate


def _rmsnorm_fwd(name, x, width, colblk, gain, t_rows):
    def fn(rv, vv):
        n, _ = _rms(rv[0])
        return [n * vv[0]], []

    (y,), _ = _rowwise(name, fn, t_rows, _tile(t_rows, 256, 16), t_rows, [(x, width, colblk)],
                       [gain.reshape(1, 1, width)], [(width, BF16)], [])
    return y


def _rmsnorm_bwd(name, x, width, colblk, dy, gain, t_rows):
    def fn(rv, vv):
        n, r = _rms(rv[0])
        return [_rms_bwd(n, r, rv[1] * vv[0])], [_colsum(rv[1] * n)]

    (dx,), (dgain,) = _rowwise(name, fn, t_rows, _tile(t_rows, 256, 16), t_rows,
                               [(x, width, colblk), (dy, width, 0)], [gain.reshape(1, 1, width)],
                               [(width, F32)], [width])
    return dx, dgain


def _final_loss(name, h, target, gain):
    t_rows = h.shape[0]
    inv_d = 1.0 / D_MODEL

    def fn(rv, vv):
        g = vv[0]
        n, r = _rms(rv[0])
        e = n * g - rv[1]
        dy = e * inv_d
        return [_rms_bwd(n, r, dy * g)], [_colsum(e * e), _colsum(dy * n)]

    (dh,), (sq, dgain) = _rowwise(name, fn, t_rows, _tile(t_rows, 256, 16), t_rows,
                                  [(h, D_MODEL, 0), (target, D_MODEL, 0)], [gain.reshape(1, 1, D_MODEL)],
                                  [(D_MODEL, F32)], [D_MODEL, D_MODEL])
    return dh, sq, dgain


def _rope(name, z, width, colblk, cos, sin, perm, backward, out_dtype):
    t_rows = cos.shape[0]

    def body(z_ref, c_ref, s_ref, p_ref, o_ref):
        zz = z_ref[...]
        pre = zz * s_ref[...] if backward else zz
        hi = pre.astype(BF16)
        lo = (pre - hi.astype(F32)).astype(BF16)
        rot = _dot(hi, p_ref[...], NN) + _dot(lo, p_ref[...], NN)
        if not backward:
            rot = rot * s_ref[...]
        o_ref[...] = (zz * c_ref[...] + rot).astype(o_ref.dtype)

    tm = _tile(t_rows, 256, 16)
    t_spec = pl.BlockSpec((tm, width), lambda i: (i, 0))
    return pl.pallas_call(
        body, name=name, grid=(t_rows // tm,),
        in_specs=[pl.BlockSpec((tm, width), lambda i: (i, colblk)), t_spec, t_spec,
                  pl.BlockSpec((width, width), lambda i: (0, 0))],
        out_specs=t_spec, out_shape=SDS((t_rows, width), out_dtype), compiler_params=_cparams(),
    )(z, cos, sin, perm)


def _window_sum(x, w, transposed):
    n_rows = x.shape[0]
    zeros = jnp.zeros((POOL_PAD, x.shape[1]), F32)
    y = jnp.concatenate([zeros, x, zeros], axis=0)
    total = n_rows + 2 * POOL_PAD
    if transposed:
        y = y + pltpu.roll(y, total - 1, 0)
    else:
        y = y + pltpu.roll(y, 1, 0)
    step = 1
    while 2 * step < w:
        y = pltpu.roll(y, step, 0) + pltpu.roll(y, total - step, 0)
        step *= 2
    return y[POOL_PAD:POOL_PAD + n_rows]


def _window_count(n_rows, w):
    t = lax.broadcasted_iota(jnp.int32, (n_rows, 1), 0)
    lo = jnp.maximum(t - w // 2, 0)
    hi = jnp.minimum(t + (w - w // 2 - 1), n_rows - 1)
    return (hi - lo + 1).astype(F32)


def _pool_fwd(name, proj, n_rows, w_grp, scale):
    def body(x_ref, w_ref, sc_ref, y_ref, p_ref):
        for g, w in enumerate(POOL_WINDOWS):
            cols = slice(g * POOL_GROUP_DIM, (g + 1) * POOL_GROUP_DIM)
            x = x_ref[:, cols]
            p = _window_sum(x, w, False) * (1.0 / _window_count(n_rows, w)) - x
            pb = p.astype(BF16)
            p_ref[:, cols] = pb
            y_ref[:, cols] = (_dot(pb, w_ref[g], NN) * sc_ref[:, cols]).astype(BF16)

    blk = pl.BlockSpec((n_rows, POOL_DIM), lambda i: (0, 0))
    return pl.pallas_call(
        body, name=name, grid=(1,),
        in_specs=[blk, pl.BlockSpec(w_grp.shape, lambda i: (0, 0, 0)), pl.BlockSpec((1, POOL_DIM), lambda i: (0, 0))],
        out_specs=[blk, blk], out_shape=[SDS((n_rows, POOL_DIM), BF16)] * 2, compiler_params=_cparams(),
    )(proj, w_grp, scale)


def _pool_bwd(name, dcat, n_rows, p, w_grp, scale):
    def body(dy_ref, p_ref, w_ref, sc_ref, dx_ref, dw_ref, dsc_ref):
        for g, w in enumerate(POOL_WINDOWS):
            cols = slice(g * POOL_GROUP_DIM, (g + 1) * POOL_GROUP_DIM)
            dy = dy_ref[:, cols]
            pb = p_ref[:, cols]
            pw = _dot(pb, w_ref[g], NN)
            dsc_ref[:, cols] = _colsum(dy * pw)
            dpw = (dy * sc_ref[:, cols]).astype(BF16)
            dw_ref[g] = _dot(pb, dpw, TN)
            dp = _dot(dpw, w_ref[g], NT)
            dx_ref[:, cols] = _window_sum(dp * (1.0 / _window_count(n_rows, w)), w, True) - dp

    blk = pl.BlockSpec((n_rows, POOL_DIM), lambda i: (0, 0))
    w_spec = pl.BlockSpec(w_grp.shape, lambda i: (0, 0, 0))
    v_spec = pl.BlockSpec((1, POOL_DIM), lambda i: (0, 0))
    return pl.pallas_call(
        body, name=name, grid=(1,), in_specs=[blk, blk, w_spec, v_spec], out_specs=[blk, w_spec, v_spec],
        out_shape=[SDS((n_rows, POOL_DIM), F32), SDS(w_grp.shape, F32), SDS((1, POOL_DIM), F32)],
        compiler_params=_cparams(),
    )(dcat, p, w_grp, scale)


def _attn_fwd(name, q, k, v):
    h, n_q, _ = q.shape
    n_k = k.shape[1]
    tq = _tile(n_q, 256, 16)

    def body(q_ref, k_ref, v_ref, o_ref, lse_ref):
        s = _dot(q_ref[...], k_ref[...], NT) * ATTN_SCALE
        m = jnp.max(s, axis=-1, keepdims=True)
        e = jnp.exp(s - m)
        l = jnp.sum(e, axis=-1, keepdims=True)
        p = (e * (1.0 / l)).astype(BF16)
        o_ref[...] = _dot(p, v_ref[...], NN).astype(BF16)
        lse_ref[...] = m + jnp.log(l)

    return pl.pallas_call(
        body, name=name, grid=(h, n_q // tq),
        in_specs=[pl.BlockSpec((None, tq, HEAD_PAD), lambda hh, i: (hh, i, 0)),
                  pl.BlockSpec((None, n_k, HEAD_PAD), lambda hh, i: (hh, 0, 0)),
                  pl.BlockSpec((None, n_k, V_HEAD), lambda hh, i: (hh, 0, 0))],
        out_specs=[pl.BlockSpec((None, tq, V_HEAD), lambda hh, i: (hh, i, 0)),
                   pl.BlockSpec((None, tq, 1), lambda hh, i: (hh, i, 0))],
        out_shape=[SDS((h, n_q, V_HEAD), BF16), SDS((h, n_q, 1), F32)], compiler_params=_cparams(),
    )(q, k, v)


def _attn_bwd(name, q, k, v, o, lse, do):
    h, n_q, _ = q.shape
    n_k = k.shape[1]
    tq = _tile(n_q, 256, 16)

    def body(q_ref, k_ref, v_ref, o_ref, lse_ref, do_ref, dq_ref, dk_ref, dv_ref, dks_ref):
        hh, i = pl.program_id(0), pl.program_id(1)
        qq, kk, dd = q_ref[...], k_ref[...], do_ref[...]
        s = _dot(qq, kk, NT) * ATTN_SCALE
        p = jnp.exp(s - lse_ref[...])
        dp = _dot(dd, v_ref[...], NT)
        delta = jnp.sum(dd.astype(F32) * o_ref[...].astype(F32), axis=-1, keepdims=True)
        ds = (p * (dp - delta) * ATTN_SCALE).astype(BF16)
        dq_ref[...] = _dot(ds, kk, NN)
        dk = _dot(ds, qq, TN)
        dv = _dot(p.astype(BF16), dd, TN)

        @pl.when(i == 0)
        def _():
            dk_ref[...] = dk
            dv_ref[...] = dv

        @pl.when(i > 0)
        def _():
            dk_ref[...] += dk
            dv_ref[...] += dv

        @pl.when((i == 0) & (hh == 0))
        def _():
            dks_ref[...] = dk

        @pl.when((i > 0) | (hh > 0))
        def _():
            dks_ref[...] += dk

    q_spec = pl.BlockSpec((None, tq, HEAD_PAD), lambda hh, i: (hh, i, 0))
    k_spec = pl.BlockSpec((None, n_k, HEAD_PAD), lambda hh, i: (hh, 0, 0))
    v_spec = pl.BlockSpec((None, n_k, V_HEAD), lambda hh, i: (hh, 0, 0))
    o_spec = pl.BlockSpec((None, tq, V_HEAD), lambda hh, i: (hh, i, 0))
    return pl.pallas_call(
        body, name=name, grid=(h, n_q // tq),
        in_specs=[q_spec, k_spec, v_spec, o_spec, pl.BlockSpec((None, tq, 1), lambda hh, i: (hh, i, 0)), o_spec],
        out_specs=[q_spec, k_spec, v_spec, pl.BlockSpec((n_k, HEAD_PAD), lambda hh, i: (0, 0))],
        out_shape=[SDS((h, n_q, HEAD_PAD), F32), SDS((h, n_k, HEAD_PAD), F32), SDS((h, n_k, V_HEAD), F32),
                   SDS((n_k, HEAD_PAD), F32)],
        compiler_params=_cparams(),
    )(q, k, v, o, lse, do)


CONV_COLS = 256


def _shift_rows(x, d):
    n_rows = x.shape[0]
    t = lax.broadcasted_iota(jnp.int32, (n_rows, 1), 0)
    if d > 0:
        return jnp.where(t >= d, pltpu.roll(x, d, 0), 0.0)
    return jnp.where(t < n_rows + d, pltpu.roll(x, n_rows + d, 0), 0.0)


def _conv_fwd(name, z3, conv_w):
    n_rows = z3.shape[0]
    nb = D_MODEL // CONV_COLS

    def body(b_ref, c_ref, v_ref, w_ref, y_ref):
        z = c_ref[...] * v_ref[...]
        zc = w_ref[0:1, :] * _shift_rows(z, 1) + w_ref[1:2, :] * z + w_ref[2:3, :] * _shift_rows(z, -1)
        y_ref[...] = (b_ref[...] * zc).astype(BF16)

    def part(k):
        return pl.BlockSpec((n_rows, CONV_COLS), lambda j: (0, k * nb + j))

    return pl.pallas_call(
        body, name=name, grid=(nb,),
        in_specs=[part(0), part(1), part(2), pl.BlockSpec((3, CONV_COLS), lambda j: (0, j))],
        out_specs=pl.BlockSpec((n_rows, CONV_COLS), lambda j: (0, j)),
        out_shape=SDS((n_rows, D_MODEL), BF16), compiler_params=_cparams(),
    )(z3, z3, z3, conv_w)


def _conv_bwd(name, dy, z3, conv_w):
    n_rows = z3.shape[0]
    nb = D_MODEL // CONV_COLS

    def body(dy_ref, b_ref, c_ref, v_ref, w_ref, db_ref, dc_ref, dv_ref, dw_ref):
        c, v, d_y = c_ref[...], v_ref[...], dy_ref[...]
        z = c * v
        z_dn, z_up = _shift_rows(z, 1), _shift_rows(z, -1)
        zc = w_ref[0:1, :] * z_dn + w_ref[1:2, :] * z + w_ref[2:3, :] * z_up
        db_ref[...] = (d_y * zc).astype(BF16)
        dzc = d_y * b_ref[...]
        dz = w_ref[0:1, :] * _shift_rows(dzc, -1) + w_ref[1:2, :] * dzc + w_ref[2:3, :] * _shift_rows(dzc, 1)
        dc_ref[...] = (dz * v).astype(BF16)
        dv_ref[...] = (dz * c).astype(BF16)
        dw_ref[0:1, :] = _colsum(dzc * z_dn)
        dw_ref[1:2, :] = _colsum(dzc * z)
        dw_ref[2:3, :] = _colsum(dzc * z_up)

    def part(k):
        return pl.BlockSpec((n_rows, CONV_COLS), lambda j: (0, k * nb + j))

    col = pl.BlockSpec((n_rows, CONV_COLS), lambda j: (0, j))
    w_spec = pl.BlockSpec((3, CONV_COLS), lambda j: (0, j))
    return pl.pallas_call(
        body, name=name, grid=(nb,), in_specs=[col, part(0), part(1), part(2), w_spec],
        out_specs=[col, col, col, w_spec],
        out_shape=[SDS((n_rows, D_MODEL), BF16)] * 3 + [SDS((3, D_MODEL), F32)], compiler_params=_cparams(),
    )(dy, z3, z3, z3, conv_w)


def _silu_rows(name, x):
    def body(x_ref, s_ref, d_ref):
        xx = x_ref[...]
        sg = jax.nn.sigmoid(xx)
        s_ref[...] = (xx * sg).astype(BF16)
        d_ref[...] = sg * (1.0 + xx * (1.0 - sg))

    return pl.pallas_call(body, name=name, out_shape=[SDS(x.shape, BF16), SDS(x.shape, F32)])(x)


def _sum_rows(name, x, scale=None):
    r, n = x.shape
    tn = _tile(n, 8192, 128)

    def body(*refs):
        acc = jnp.sum(refs[0][...].astype(F32), axis=0, keepdims=True)
        if scale is not None:
            acc = acc * refs[1][...]
        refs[-1][...] = acc

    in_specs = [pl.BlockSpec((r, tn), lambda j: (0, j))]
    args = [x]
    if scale is not None:
        in_specs.append(pl.BlockSpec((1, tn), lambda j: (0, j)))
        args.append(scale)
    return pl.pallas_call(body, name=name, grid=(n // tn,), in_specs=in_specs,
                          out_specs=pl.BlockSpec((1, tn), lambda j: (0, j)), out_shape=SDS((1, n), F32))(*args)


def _sum_slots(name, x):
    n_slots, r, c = x.shape
    tr = _tile(r, 432, 16)

    def body(x_ref, o_ref):
        acc = x_ref[0].astype(F32)
        for sl in range(1, n_slots):
            acc = acc + x_ref[sl].astype(F32)
        o_ref[...] = acc

    return pl.pallas_call(body, name=name, grid=(r // tr,),
                          in_specs=[pl.BlockSpec((n_slots, tr, c), lambda i: (0, i, 0))],
                          out_specs=pl.BlockSpec((tr, c), lambda i: (i, 0)), out_shape=SDS((r, c), F32),
                          compiler_params=_cparams())(x)


def _adamw(name, w, g, m, v):
    shape = w.shape
    cols = shape[-1]
    rows = w.size // cols
    tr = _tile(rows, 512, 8)
    bc1 = 1.0 - ADAM_B1 ** ADAM_STEP
    bc2 = 1.0 - ADAM_B2 ** ADAM_STEP

    def body(w_ref, g_ref, m_ref, v_ref, d_ref, nm_ref, nv_ref):
        gg = g_ref[...]
        nm = ADAM_B1 * m_ref[...] + (1.0 - ADAM_B1) * gg
        nv = ADAM_B2 * v_ref[...] + (1.0 - ADAM_B2) * (gg * gg)
        nm_ref[...] = nm
        nv_ref[...] = nv
        d_ref[...] = -ADAM_LR * ((nm / bc1) / (jnp.sqrt(nv / bc2) + ADAM_EPS) + ADAM_WD * w_ref[...])

    spec = pl.BlockSpec((tr, cols), lambda i: (i, 0))
    outs = pl.pallas_call(body, name=name, grid=(rows // tr,), in_specs=[spec] * 4, out_specs=[spec] * 3,
                          out_shape=[SDS((rows, cols), F32)] * 3, compiler_params=_cparams())(
        w.reshape(rows, cols), g.reshape(rows, cols), m.reshape(rows, cols), v.reshape(rows, cols))
    return tuple(t.reshape(shape) for t in outs)


def _exchange(name, x, scatter):
    blk = x.shape[1:] if scatter else x.shape

    def body(x_ref, out_ref, send_sems, recv_sems, local_sem):
        mx, my, mc = lax.axis_index("x"), lax.axis_index("y"), lax.axis_index("c")
        me = 4 * mx + 2 * my + mc
        own = pltpu.make_async_copy(x_ref.at[me] if scatter else x_ref, out_ref.at[me], local_sem)
        own.start()
        copies = []
        for kk in range(1, N_DEV):
            px = jnp.bitwise_xor(mx, (kk >> 2) & 1)
            py = jnp.bitwise_xor(my, (kk >> 1) & 1)
            pc = jnp.bitwise_xor(mc, kk & 1)
            peer = 4 * px + 2 * py + pc
            send = pltpu.make_async_remote_copy(
                src_ref=x_ref.at[peer] if scatter else x_ref, dst_ref=out_ref.at[me],
                send_sem=send_sems.at[kk - 1], recv_sem=recv_sems.at[kk - 1],
                device_id=(px, py, pc), device_id_type=MESH)
            send.start()
            arrival = pltpu.make_async_remote_copy(
                src_ref=x_ref.at[peer] if scatter else x_ref, dst_ref=out_ref.at[peer],
                send_sem=send_sems.at[kk - 1], recv_sem=recv_sems.at[kk - 1],
                device_id=(px, py, pc), device_id_type=MESH)
            copies.append((send, arrival))
        for send, arrival in copies:
            arrival.wait_recv()
            send.wait_send()
        own.wait()

    return pl.pallas_call(
        body, name=name, out_shape=SDS((N_DEV,) + tuple(blk), x.dtype),
        in_specs=[pl.BlockSpec(memory_space=pl.ANY)], out_specs=pl.BlockSpec(memory_space=pl.ANY),
        scratch_shapes=[pltpu.SemaphoreType.DMA((N_DEV - 1,)), pltpu.SemaphoreType.DMA((N_DEV - 1,)),
                        pltpu.SemaphoreType.DMA],
    )(x)


def _rope_perm(width, starts):
    half = QK_ROPE // 4
    src, dst, sign = [], [], []
    for s0 in starts:
        for base in (s0, s0 + 2 * half):
            for i in range(half):
                src += [base + half + i, base + i]
                dst += [base + i, base + half + i]
                sign += [-1.0, 1.0]
    p = jnp.zeros((width, width), F32).at[jnp.array(src), jnp.array(dst)].set(jnp.array(sign, F32))
    return p


def _rope_tables(n_lat, t_rows, width, starts):
    half = QK_ROPE // 4
    pos = jnp.arange(n_lat)
    freqs = jnp.power(ROPE_THETA, -jnp.arange(0, 2 * half, 2, dtype=F32) / (2 * half))
    ang_r = (pos // GRID_W).astype(F32)[:, None] * freqs
    ang_c = (pos % GRID_W).astype(F32)[:, None] * freqs
    ang = jnp.concatenate([ang_r, ang_r, ang_c, ang_c], axis=-1)
    cos = jnp.ones((t_rows, width), F32)
    sin = jnp.zeros((t_rows, width), F32)
    for s0 in starts:
        cos = cos.at[:n_lat, s0:s0 + QK_ROPE].set(jnp.cos(ang))
        sin = sin.at[:n_lat, s0:s0 + QK_ROPE].set(jnp.sin(ang))
    return cos, sin


def _ffn_half_fwd(tag, s, mg, k, wg_t, wu_t, wd, coef, n_lat):
    u = _adaln_fwd(f"{tag}_adaln", s, mg, k, n_lat)
    a, b, hid = _ffn_up(f"{tag}_up", u, wg_t, wu_t)
    s_out, o = _mm_resid(f"{tag}_down", hid, wd, s, mg, k, coef, n_lat)
    return s_out, (s, u, a, b, hid, o)


def _ffn_half_bwd(tag, ds_out, saved, mg, k, wg_t, wu_t, wd, coef, n_lat):
    s, u, a, b, hid, o = saved
    do, dgate = _gate_bwd(f"{tag}_dgate", ds_out, o, mg, k, coef, n_lat)
    da, db = _ffn_dact(f"{tag}_dact", do, wd, a, b)
    dwd = _mm(f"{tag}_dwd", [(hid, do)], "tn", BF16, 512, 512)
    dwg_t = _mm(f"{tag}_dwg", [(da, u)], "tn", BF16, 512, 512)
    dwu_t = _mm(f"{tag}_dwu", [(db, u)], "tn", BF16, 512, 512)
    du = _mm(f"{tag}_du", [(da, wg_t), (db, wu_t)], "nn", F32, 384, 512)
    ds_in, (dshift, dscale, dgain) = _adaln_bwd(f"{tag}_dadaln", s, du, ds_out, mg, k, n_lat)
    return ds_in, dict(wg_t=dwg_t, wu_t=dwu_t, wd=dwd, shift=dshift, scale=dscale, gate=dgate, gain=dgain)


def _mod_grad(parts, n_groups):
    rows = []
    zero = jnp.zeros((n_groups, 1, D_MODEL), F32)
    for k in range(3):
        for nm in ("shift", "scale", "gate"):
            t = parts[k].get(nm, zero)
            if t.shape[0] < n_groups:
                t = jnp.concatenate([t, jnp.zeros((n_groups - t.shape[0], 1, D_MODEL), F32)], axis=0)
            rows.append(t)
    return jnp.concatenate(rows, axis=1).reshape(n_groups, N_MOD * D_MODEL)


def _local_step(x, ctx, target, mod_h, mod_g, norm_g, fw, pool_w, pool_scale, q_norm_g, kv_norm_g, conv_w,
                final_norm_g):
    n_lat, n_ctx = x.shape[0], ctx.shape[0]
    t_all = n_lat + n_ctx
    mg0 = jnp.stack([jnp.concatenate([mod_h[0], norm_g[0]], axis=0), jnp.concatenate([mod_g, norm_g[0]], axis=0)])
    mg1 = jnp.concatenate([mod_h[1], norm_g[1]], axis=0)[None]

    s0 = jnp.concatenate([x, ctx], axis=0)
    s1, sv_f00 = _ffn_half_fwd("l0f0", s0, mg0, 0, fw["gate_t"][0], fw["up_t"][0], fw["down"][0], 0.5, n_lat)

    ua = _adaln_fwd("l0m_adaln", s1, mg0, 1, n_lat)
    proj = _mm("l0m_proj", [(ua, fw["in_t"])], "nt", F32, 768, 384)
    pool_y, pool_p = _pool_fwd("l0m_pool", proj, n_lat, pool_w.astype(BF16), pool_scale)
    nq = _rmsnorm_fwd("l0m_qnorm", proj, Q_RANK, PA_CQ // Q_RANK, q_norm_g, n_lat)
    q_lin = _mm("l0m_q", [(nq, fw["uq"])], "nn", F32, 512, 768)
    q_starts = [hh * QK_HEAD + QK_NOPE for hh in range(HEADS)]
    cos_q, sin_q = _rope_tables(n_lat, n_lat, Q_RANK, q_starts)
    perm_q = _rope_perm(Q_RANK, q_starts)
    q_rot = _rope("l0m_qrope", q_lin, Q_RANK, 0, cos_q, sin_q, perm_q.astype(BF16), False, BF16)
    cos_k, sin_k = _rope_tables(n_lat, t_all, PA_KV_W, [KV_RANK])
    perm_k = _rope_perm(PA_KV_W, [KV_RANK])
    kvr = _rope("l0m_krope", proj, PA_KV_W, PA_KV // PA_KV_W, cos_k, sin_k, perm_k.astype(BF16), False, F32)
    nkv = _rmsnorm_fwd("l0m_kvnorm", kvr, KV_RANK, 0, kv_norm_g, t_all)
    kv = _mm("l0m_kv", [(nkv, fw["ukv_t"])], "nt", BF16, 768, 512)
    qh = jnp.pad(q_rot.reshape(n_lat, HEADS, QK_HEAD), ((0, 0), (0, 0), (0, HEAD_PAD - QK_HEAD))).transpose(1, 0, 2)
    kvh = kv.reshape(t_all, HEADS, QK_NOPE + V_HEAD)
    k_rope = jnp.broadcast_to(kvr[:, None, KV_RANK:KV_RANK + QK_ROPE].astype(BF16), (t_all, HEADS, QK_ROPE))
    kh = jnp.concatenate([kvh[:, :, :QK_NOPE], k_rope, jnp.zeros((t_all, HEADS, HEAD_PAD - QK_HEAD), BF16)],
                         axis=-1).transpose(1, 0, 2)
    vh = kvh[:, :, QK_NOPE:].transpose(1, 0, 2)
    oh, lse = _attn_fwd("l0m_attn", qh, kh, vh)
    cat = jnp.concatenate([pool_y, oh.transpose(1, 0, 2).reshape(n_lat, HEADS * V_HEAD)], axis=-1)
    h1 = s1[:n_lat]
    h2, mix_o = _mm_resid("l0m_out", cat, fw["ab_out"], h1, mg0[:1], 1, 1.0, n_lat)

    h3, sv_f01 = _ffn_half_fwd("l0f1", h2, mg0[:1], 2, fw["gate_t"][1], fw["up_t"][1], fw["down"][1], 0.5, n_lat)

    h4, sv_f10 = _ffn_half_fwd("l1f0", h3, mg1, 0, fw["gate_t"][2], fw["up_t"][2], fw["down"][2], 0.5, n_lat)
    uc = _adaln_fwd("l1m_adaln", h4, mg1, 1, n_lat)
    z3 = _mm("l1m_in", [(uc, fw["cin_t"])], "nt", F32, 512, 512)
    yc = _conv_fwd("l1m_conv", z3, conv_w)
    h5, conv_o = _mm_resid("l1m_out", yc, fw["c_out"], h4, mg1, 1, 1.0, n_lat)
    h6, sv_f11 = _ffn_half_fwd("l1f1", h5, mg1, 2, fw["gate_t"][3], fw["up_t"][3], fw["down"][3], 0.5, n_lat)

    dh6, sq_cols, d_final_g = _final_loss("loss_head", h6, target, final_norm_g)
    g = {}
    dh5, g["f11"] = _ffn_half_bwd("l1f1", dh6, sv_f11, mg1, 2, fw["gate_t"][3], fw["up_t"][3], fw["down"][3], 0.5, n_lat)

    do_c, dgate_c = _gate_bwd("l1m_dgate", dh5, conv_o, mg1, 1, 1.0, n_lat)
    dyc = _mm("l1m_dy", [(do_c, fw["c_out"])], "nt", F32, 512, 512)
    d_c_out = _mm("l1m_dwout", [(yc, do_c)], "tn", BF16, 512, 512)
    db_, dc_, dv_, d_conv_w = _conv_bwd("l1m_dconv", dyc, z3, conv_w)
    dz3 = jnp.concatenate([db_, dc_, dv_], axis=-1)
    d_cin_t = _mm("l1m_dwin", [(dz3, uc)], "tn", BF16, 512, 512)
    duc = _mm("l1m_du", [(dz3, fw["cin_t"])], "nn", F32, 512, 512)
    dh4, (dsh_c, dsc_c, dgn_c) = _adaln_bwd("l1m_dadaln", h4, duc, dh5, mg1, 1, n_lat)
    dh3, g["f10"] = _ffn_half_bwd("l1f0", dh4, sv_f10, mg1, 0, fw["gate_t"][2], fw["up_t"][2], fw["down"][2], 0.5, n_lat)

    dh2, g["f01"] = _ffn_half_bwd("l0f1", dh3, sv_f01, mg0[:1], 2, fw["gate_t"][1], fw["up_t"][1], fw["down"][1], 0.5, n_lat)

    do_a, dgate_a = _gate_bwd("l0m_dgate", dh2, mix_o, mg0[:1], 1, 1.0, n_lat)
    dcat = _mm("l0m_dcat", [(do_a, fw["ab_out"])], "nt", F32, 512, 512)
    d_ab_out = _mm("l0m_dwout", [(cat, do_a)], "tn", BF16, 512, 512)
    d_pool_x, d_pool_w, d_pool_scale = _pool_bwd("l0m_dpool", dcat, n_lat, pool_p, pool_w.astype(BF16), pool_scale)
    doh = dcat[:, POOL_DIM:].reshape(n_lat, HEADS, V_HEAD).transpose(1, 0, 2).astype(BF16)
    dqh, dkh, dvh, dk_sum = _attn_bwd("l0m_dattn", qh, kh, vh, oh, lse, doh)
    dq_rot = dqh[:, :, :QK_HEAD].transpose(1, 0, 2).reshape(n_lat, Q_RANK)
    dq_lin = _rope("l0m_dqrope", dq_rot, Q_RANK, 0, cos_q, sin_q, perm_q.T.astype(BF16), True, BF16)
    d_uq = _mm("l0m_dwuq", [(nq, dq_lin)], "tn", BF16, 768, 768)
    dnq = _mm("l0m_dnq", [(dq_lin, fw["uq"])], "nt", F32, 512, 768)
    dcq, d_q_norm_g = _rmsnorm_bwd("l0m_dqnorm", proj, Q_RANK, PA_CQ // Q_RANK, dnq, q_norm_g, n_lat)
    dkv = jnp.concatenate([dkh[:, :, :QK_NOPE], dvh], axis=-1).transpose(1, 0, 2).reshape(t_all, HEADS * HEAD_PAD)
    dkv = dkv.astype(BF16)
    dnkv = _mm("l0m_dnkv", [(dkv, fw["ukv_t"])], "nn", F32, 768, 256)
    d_ukv_t = _mm("l0m_dwukv", [(dkv, nkv)], "tn", BF16, 512, 256)
    dckv, d_kv_norm_g = _rmsnorm_bwd("l0m_dkvnorm", kvr, KV_RANK, 0, dnkv, kv_norm_g, t_all)
    dkvr = jnp.concatenate([dckv, dk_sum[:, QK_NOPE:QK_HEAD],
                            jnp.zeros((t_all, PA_KV_W - KV_RANK - QK_ROPE), F32)], axis=-1)
    dpb = _rope("l0m_dkrope", dkvr, PA_KV_W, 0, cos_k, sin_k, perm_k.T.astype(BF16), True, F32)
    dproj_lat = jnp.concatenate([d_pool_x, jnp.zeros((n_lat, PA_CQ - POOL_DIM), F32), dcq, dpb[:n_lat]], axis=-1)
    dproj_ctx = jnp.concatenate([jnp.zeros((n_ctx, PA_KV), F32), dpb[n_lat:]], axis=-1)
    dproj = jnp.concatenate([dproj_lat, dproj_ctx], axis=0).astype(BF16)
    d_in_t = _mm("l0m_dwin", [(dproj, ua)], "tn", BF16, 640, 512)
    dua = _mm("l0m_du", [(dproj, fw["in_t"])], "nn", F32, 768, 512)
    dh2_all = jnp.concatenate([dh2, jnp.zeros((n_ctx, D_MODEL), F32)], axis=0)
    ds1, (dsh_a, dsc_a, dgn_a) = _adaln_bwd("l0m_dadaln", s1, dua, dh2_all, mg0, 1, n_lat)
    ds0, g["f00"] = _ffn_half_bwd("l0f0", ds1, sv_f00, mg0, 0, fw["gate_t"][0], fw["up_t"][0], fw["down"][0], 0.5, n_lat)

    dmod0 = _mod_grad([g["f00"], dict(shift=dsh_a, scale=dsc_a, gate=dgate_a), g["f01"]], 2)
    dmod1 = _mod_grad([g["f10"], dict(shift=dsh_c, scale=dsc_c, gate=dgate_c), g["f11"]], 1)
    d_norm_g = jnp.stack([
        jnp.concatenate([jnp.sum(g["f00"]["gain"], axis=0), jnp.sum(dgn_a, axis=0), g["f01"]["gain"][0]], axis=0),
        jnp.concatenate([g["f10"]["gain"][0], dgn_c[0], g["f11"]["gain"][0]], axis=0)])
    grads = dict(
        gate_t=[g["f00"]["wg_t"], g["f01"]["wg_t"], g["f10"]["wg_t"], g["f11"]["wg_t"]],
        up_t=[g["f00"]["wu_t"], g["f01"]["wu_t"], g["f10"]["wu_t"], g["f11"]["wu_t"]],
        down=[g["f00"]["wd"], g["f01"]["wd"], g["f10"]["wd"], g["f11"]["wd"]],
        in_t=d_in_t, uq=d_uq, ukv_t=d_ukv_t, ab_out=d_ab_out, cin_t=d_cin_t, c_out=d_c_out,
        pool_w=d_pool_w, pool_scale=d_pool_scale, q_norm_g=d_q_norm_g[0], kv_norm_g=d_kv_norm_g[0],
        conv_w=d_conv_w, final_norm_g=d_final_g[0], norm_g=d_norm_g,
        mod_h=jnp.stack([dmod0[0], dmod1[0]]), mod_g=dmod0[1])
    return sq_cols, ds0[:n_lat], grads


PACK_COLS = 1024
PACK_ROW_MULT = 16


def _pack_layout(shards):
    layout, off = {}, 0
    for name, (r, cc) in shards:
        n = r * cc // PACK_COLS
        layout[name] = (off, n, r, cc)
        off += -(-n // PACK_ROW_MULT) * PACK_ROW_MULT
    return layout, off


def _pack(parts, layout, total, lead):
    pieces, pos = [], 0
    for name, (off, n, r, cc) in layout.items():
        if off > pos:
            pieces.append(jnp.zeros(lead + (off - pos, PACK_COLS), parts[name].dtype))
        pieces.append(parts[name].reshape(lead + (n, PACK_COLS)))
        pos = off + n
    if total > pos:
        pieces.append(jnp.zeros(lead + (total - pos, PACK_COLS), pieces[0].dtype))
    return jnp.concatenate(pieces, axis=len(lead))


def _adam_all(names, weights, grads, moms, vels):
    deltas, new_m, new_v = [], [], []
    for nm, w, g, m, v in zip(names, weights, grads, moms, vels):
        d, a, b = _adamw(f"adamw_{nm}", w, g.reshape(w.shape), m, v)
        deltas.append(d)
        new_m.append(a)
        new_v.append(b)
    return deltas, new_m, new_v


WEIGHT_NAMES = ("c_ctx", "norm_g", "w_mod", "b_mod", "ffn_w_gate", "ffn_w_up", "ffn_w_down", "ab_w_in", "pool_w",
                "pool_scale", "q_norm_g", "w_uq", "kv_norm_g", "w_ukv", "ab_w_out", "conv_w_in", "conv_w",
                "conv_w_out", "final_norm_g")


def kernel(x, c, ctx, c_ctx, norm_g, w_mod, b_mod, ffn_w_gate, ffn_w_up, ffn_w_down, ab_w_in, pool_w, pool_scale, q_norm_g, w_uq, kv_norm_g, w_ukv, ab_w_out, conv_w_in, conv_w, conv_w_out, final_norm_g, loss_target, m_c_ctx, m_norm_g, m_w_mod, m_b_mod, m_ffn_w_gate, m_ffn_w_up, m_ffn_w_down, m_ab_w_in, m_pool_w, m_pool_scale, m_q_norm_g, m_w_uq, m_kv_norm_g, m_w_ukv, m_ab_w_out, m_conv_w_in, m_conv_w, m_conv_w_out, m_final_norm_g, v_c_ctx, v_norm_g, v_w_mod, v_b_mod, v_ffn_w_gate, v_ffn_w_up, v_ffn_w_down, v_ab_w_in, v_pool_w, v_pool_scale, v_q_norm_g, v_w_uq, v_kv_norm_g, v_w_ukv, v_ab_w_out, v_conv_w_in, v_conv_w, v_conv_w_out, v_final_norm_g):
    weights = (c_ctx, norm_g, w_mod, b_mod, ffn_w_gate, ffn_w_up, ffn_w_down, ab_w_in, pool_w, pool_scale, q_norm_g,
               w_uq, kv_norm_g, w_ukv, ab_w_out, conv_w_in, conv_w, conv_w_out, final_norm_g)
    moms = (m_c_ctx, m_norm_g, m_w_mod, m_b_mod, m_ffn_w_gate, m_ffn_w_up, m_ffn_w_down, m_ab_w_in, m_pool_w,
            m_pool_scale, m_q_norm_g, m_w_uq, m_kv_norm_g, m_w_ukv, m_ab_w_out, m_conv_w_in, m_conv_w, m_conv_w_out,
            m_final_norm_g)
    vels = (v_c_ctx, v_norm_g, v_w_mod, v_b_mod, v_ffn_w_gate, v_ffn_w_up, v_ffn_w_down, v_ab_w_in, v_pool_w,
            v_pool_scale, v_q_norm_g, v_w_uq, v_kv_norm_g, v_w_ukv, v_ab_w_out, v_conv_w_in, v_conv_w, v_conv_w_out,
            v_final_norm_g)
    me = 4 * lax.axis_index("x") + 2 * lax.axis_index("y") + lax.axis_index("c")
    n_lat, n_ctx = x.shape[1], ctx.shape[1]
    d = D_MODEL
    mod_cols = w_mod.shape[-1]
    ng_sh, cw_sh = norm_g.shape[-1], conv_w.shape[-1]

    small = jnp.concatenate([c.reshape(-1), norm_g.reshape(-1), conv_w.reshape(-1)])
    small_n = -(-small.shape[0] // 1024) * 1024
    small = jnp.pad(small, (0, small_n - small.shape[0])).reshape(small_n // 128, 128)
    small_all = _exchange("gather_small", small, False).reshape(N_DEV, small_n)
    c_all = small_all[:, :d]
    o1 = d + 6 * ng_sh
    norm_g_full = small_all[:, d:o1].reshape(N_DEV, 2, 3, ng_sh).transpose(1, 2, 0, 3).reshape(2, 3, d)
    conv_w_full = small_all[:, o1:o1 + 3 * cw_sh].reshape(N_DEV, 3, cw_sh).transpose(1, 0, 2).reshape(3, d)

    local = {}
    for i in range(4):
        local[f"gate_t{i}"] = ffn_w_gate[i // 2, i % 2].T
        local[f"up_t{i}"] = ffn_w_up[i // 2, i % 2].T
        local[f"down{i}"] = ffn_w_down[i // 2, i % 2]
    local["in_t"] = ab_w_in[0].T
    local["uq"] = w_uq[0]
    local["ukv_t"] = w_ukv[0].T
    local["ab_out"] = ab_w_out[0]
    local["cin_t"] = conv_w_in[0].T
    local["c_out"] = conv_w_out[0]
    layout, total = _pack_layout([(nm, a.shape) for nm, a in local.items()])
    packed = _pack({nm: a.astype(BF16) for nm, a in local.items()}, layout, total, ())
    gathered = _exchange("gather_weights", packed, False)

    def whole(nm):
        off, n, r, cc = layout[nm]
        return gathered[:, off:off + n, :].reshape(N_DEV * r, cc)

    in_t = whole("in_t")
    kv_rows = KV_RANK + QK_ROPE
    in_t_pad = jnp.concatenate([
        in_t[:POOL_DIM], jnp.zeros((PA_CQ - POOL_DIM, d), BF16), in_t[POOL_DIM:POOL_DIM + Q_RANK],
        in_t[POOL_DIM + Q_RANK:], jnp.zeros((PA_KV_W - kv_rows, d), BF16)], axis=0)
    fw = dict(gate_t=[whole(f"gate_t{i}") for i in range(4)], up_t=[whole(f"up_t{i}") for i in range(4)],
              down=[whole(f"down{i}") for i in range(4)], in_t=in_t_pad, uq=whole("uq"), ukv_t=whole("ukv_t"),
              ab_out=whole("ab_out"), cin_t=whole("cin_t"), c_out=whole("c_out"))

    cond = jnp.concatenate([c_all, jnp.broadcast_to(c_ctx[None, :], (N_DEV, d))], axis=0)
    sil, dsil = _silu_rows("mod_silu", cond)
    w_mod_b = w_mod.astype(BF16)
    b_sh = lax.dynamic_slice(b_mod, (0, me * mod_cols), (2, mod_cols))
    m_part = jnp.stack([_mm(f"mod_fwd{l}", [(sil, w_mod_b[l])], "nn", F32, 16, 384, bias=b_sh[l:l + 1])
                        for l in range(2)], axis=1)
    m_all = _exchange("gather_mod", m_part.reshape(-1, 128), False).reshape(N_DEV, 2 * N_DEV, 2, mod_cols)
    m_mine = lax.dynamic_index_in_dim(m_all, me, axis=1, keepdims=False)
    mod_h = m_mine.transpose(1, 0, 2).reshape(2, N_MOD, d)
    mod_g = m_all[:, N_DEV, 0, :].reshape(N_MOD, d)

    sq_cols, grad_x, g = _local_step(x[0], ctx[0], loss_target[0], mod_h, mod_g, norm_g_full, fw, pool_w[0],
                                     pool_scale, q_norm_g, kv_norm_g, conv_w_full, final_norm_g)
    loss = lax.psum(0.5 * jnp.sum(sq_cols) / d, ("x", "y", "c"))

    dm = jnp.stack([g["mod_h"], jnp.stack([g["mod_g"], jnp.zeros_like(g["mod_g"])])])
    dm_all = _exchange("gather_dmod", dm.reshape(-1, 128), False).reshape(N_DEV, 2, 2, N_MOD * d)
    grad_b_mod = _sum_rows("dmod_bias", dm_all.reshape(2 * N_DEV, 2 * N_MOD * d)).reshape(2, N_MOD * d)
    dm_sh = lax.dynamic_slice(dm_all, (0, 0, 0, me * mod_cols), (N_DEV, 2, 2, mod_cols))
    gw_mod, cctx_parts = [], []
    for l in range(2):
        dm_l = dm_sh[:, :, l, :].transpose(1, 0, 2).reshape(2 * N_DEV, mod_cols).astype(BF16)
        gw_mod.append(_mm(f"mod_dw{l}", [(sil, dm_l)], "tn", F32, 512, 384))
        dm_ctx = jnp.concatenate([dm_l[N_DEV:], jnp.zeros((N_DEV, mod_cols), BF16)], axis=0)
        cctx_parts.append(_mm(f"mod_dcond{l}", [(dm_ctx, w_mod_b[l])], "nt", F32, 16, 512))
    grad_w_mod = jnp.stack(gw_mod)
    cctx_part = _sum_rows("mod_dcond_sum", jnp.concatenate(cctx_parts, axis=0))

    small_g = jnp.concatenate([g["pool_w"].reshape(-1), g["pool_scale"].reshape(-1), g["q_norm_g"].reshape(-1),
                               g["kv_norm_g"].reshape(-1), g["final_norm_g"].reshape(-1), g["norm_g"].reshape(-1),
                               g["conv_w"].reshape(-1), cctx_part.reshape(-1)])
    sizes = [pool_w.size, pool_scale.size, q_norm_g.size, kv_norm_g.size, d, 6 * d, 3 * d, d]
    sg_n = -(-small_g.shape[0] // 1024) * 1024
    small_g = jnp.pad(small_g, (0, sg_n - small_g.shape[0]))
    sg_all = _exchange("gather_small_grads", small_g.reshape(-1, 128), False).reshape(N_DEV, sg_n)
    scale_vec = jnp.concatenate([jnp.ones((1, sum(sizes[:-1])), F32), dsil[N_DEV:N_DEV + 1],
                                 jnp.ones((1, sg_n - sum(sizes)), F32)], axis=1)
    sg = _sum_rows("small_grads_sum", sg_all, scale_vec)[0]
    cuts, pos = [], 0
    for sz in sizes:
        cuts.append(sg[pos:pos + sz])
        pos += sz
    g_pool_w, g_pool_scale, g_q_norm, g_kv_norm, g_final, g_norm_full, g_conv_full, g_c_ctx = cuts
    grad_norm_g = lax.dynamic_slice(g_norm_full.reshape(2, 3, d), (0, 0, me * ng_sh), (2, 3, ng_sh))
    grad_conv_w = lax.dynamic_slice(g_conv_full.reshape(3, d), (0, me * cw_sh), (3, cw_sh))[None]

    in_rows = POOL_DIM + Q_RANK
    d_in_t = jnp.concatenate([g["in_t"][:POOL_DIM], g["in_t"][PA_CQ:PA_CQ + Q_RANK],
                              g["in_t"][PA_KV:PA_KV + kv_rows]], axis=0)
    full_g = {}
    for i in range(4):
        full_g[f"gate_t{i}"] = g["gate_t"][i]
        full_g[f"up_t{i}"] = g["up_t"][i]
        full_g[f"down{i}"] = g["down"][i]
    full_g.update(in_t=d_in_t, uq=g["uq"], ukv_t=g["ukv_t"], ab_out=g["ab_out"], cin_t=g["cin_t"], c_out=g["c_out"])
    parts = {nm: full_g[nm].reshape(N_DEV, layout[nm][2], layout[nm][3]) for nm in layout}
    g_packed = _pack(parts, layout, total, (N_DEV,))
    g_slots = _exchange("scatter_grads", g_packed, True)
    g_red = _sum_slots("reduce_grads", g_slots)

    def mine(nm):
        off, n, r, cc = layout[nm]
        return g_red[off:off + n].reshape(r, cc)

    grad_gate = jnp.stack([mine(f"gate_t{i}").T for i in range(4)]).reshape(ffn_w_gate.shape)
    grad_up = jnp.stack([mine(f"up_t{i}").T for i in range(4)]).reshape(ffn_w_up.shape)
    grad_down = jnp.stack([mine(f"down{i}") for i in range(4)]).reshape(ffn_w_down.shape)
    grads = (g_c_ctx, grad_norm_g, grad_w_mod, grad_b_mod, grad_gate, grad_up, grad_down, mine("in_t").T[None],
             g_pool_w.reshape(pool_w.shape), g_pool_scale.reshape(pool_scale.shape), g_q_norm.reshape(q_norm_g.shape),
             mine("uq")[None], g_kv_norm.reshape(kv_norm_g.shape), mine("ukv_t").T[None], mine("ab_out")[None],
             mine("cin_t").T[None], grad_conv_w, mine("c_out")[None], g_final)
    grads = tuple(gr.reshape(w.shape) for gr, w in zip(grads, weights))
    deltas, new_m, new_v = _adam_all(WEIGHT_NAMES, weights, grads, moms, vels)
    return (loss, grad_x[None], *grads, *deltas, *new_m, *new_v)
```

```python
import functools
import math

import jax
import jax.numpy as jnp
import numpy as np
from jax import lax
from jax.experimental import pallas as pl
from jax.experimental.pallas import tpu as pltpu

F32 = jnp.float32
BF16 = jnp.bfloat16
MESH = pl.DeviceIdType.MESH
SDS = jax.ShapeDtypeStruct

N_DEV = 8
D_MODEL = 1024
N_MOD = 9
D_FF = 2816
POOL_WINDOWS = (2, 4, 8, 16)
POOL_DIM = 512
POOL_GROUP_DIM = 128
HEADS = 8
QK_NOPE = 64
QK_ROPE = 32
QK_HEAD = QK_NOPE + QK_ROPE
V_HEAD = 64
Q_RANK = 768
KV_RANK = 256
GRID_W = 64
ROPE_THETA = 10000.0
RMS_EPS = 1e-6
ATTN_SCALE = 1.0 / math.sqrt(QK_HEAD)
HEAD_PAD = 128
POOL_PAD = 16
PA_POOL, PA_CQ, PA_KV = 0, 768, 1536
PA_KV_W = 384
PA_W = PA_KV + PA_KV_W

ADAM_LR, ADAM_B1, ADAM_B2, ADAM_EPS, ADAM_WD, ADAM_STEP = 0.001, 0.9, 0.999, 1e-08, 0.01, 10

VMEM_LIMIT_BYTES = 56 * 1024 * 1024

NN = ((1,), (0,))
NT = ((1,), (1,))
TN = ((0,), (0,))


def _cparams():
    return pltpu.CompilerParams(vmem_limit_bytes=VMEM_LIMIT_BYTES)


def _dot(a, b, dims):
    return lax.dot_general(a, b, (dims, ((), ())), preferred_element_type=F32)


def _tile(n, cap, mult=8):
    t = (min(cap, n) // mult) * mult
    while t >= mult:
        if n % t == 0:
            return t
        t -= mult
    return n


def _colsum(x):
    return jnp.sum(x, axis=0, keepdims=True)


def _rms(x):
    r = lax.rsqrt(jnp.mean(x * x, axis=-1, keepdims=True) + RMS_EPS)
    return x * r, r


def _rms_bwd(n, r, dn):
    return r * (dn - n * jnp.mean(dn * n, axis=-1, keepdims=True))


def _rowwise(name, fn, t_rows, tm, n_lat, rows, vecs, outs, accs):
    nt = t_rows // tm
    nlt = n_lat // tm
    n_groups = 2 if nlt < nt else 1

    def grp(i):
        return jnp.where(i >= nlt, 1, 0) if n_groups == 2 else 0

    in_specs = [pl.BlockSpec((tm, w), functools.partial(lambda i, cb: (i, cb), cb=cb)) for (_, w, cb) in rows]
    in_specs += [pl.BlockSpec((1,) + v.shape[1:], lambda i: (grp(i), 0, 0)) for v in vecs]
    out_specs = [pl.BlockSpec((tm, w), lambda i: (i, 0)) for (w, _) in outs]
    out_specs += [pl.BlockSpec((1, 1, w), lambda i: (grp(i), 0, 0)) for w in accs]
    out_shape = [SDS((t_rows, w), dt) for (w, dt) in outs] + [SDS((n_groups, 1, w), F32) for w in accs]
    n_r, n_v, n_o = len(rows), len(vecs), len(outs)

    def body(*refs):
        row_vals = [r[...] for r in refs[:n_r]]
        vec_vals = [v[0] for v in refs[n_r:n_r + n_v]]
        out_refs = refs[n_r + n_v:n_r + n_v + n_o]
        acc_refs = refs[n_r + n_v + n_o:]
        out_vals, acc_vals = fn(row_vals, vec_vals)
        for o_ref, o in zip(out_refs, out_vals):
            o_ref[...] = o.astype(o_ref.dtype)
        if acc_refs:
            i = pl.program_id(0)
            first = (i == 0) | (i == nlt) if n_groups == 2 else i == 0

            @pl.when(first)
            def _():
                for a_ref, a in zip(acc_refs, acc_vals):
                    a_ref[0] = a

            @pl.when(jnp.logical_not(first))
            def _():
                for a_ref, a in zip(acc_refs, acc_vals):
                    a_ref[0] += a

    res = pl.pallas_call(
        body, name=name, grid=(nt,), in_specs=in_specs, out_specs=out_specs, out_shape=out_shape,
        compiler_params=_cparams(),
    )(*[r[0] for r in rows], *vecs)
    return res[:n_o], res[n_o:]


def _mm(name, pairs, mode, out_dtype, tm_cap=512, tn_cap=512, weights_outer=False, bias=None):
    a0, b0 = pairs[0]
    if mode == "nn":
        m, n, dims = a0.shape[0], b0.shape[1], NN
    elif mode == "nt":
        m, n, dims = a0.shape[0], b0.shape[0], NT
    else:
        m, n, dims = a0.shape[1], b0.shape[1], TN
    tm, tn = _tile(m, tm_cap, 128 if mode == "tn" else 16), _tile(n, tn_cap, 128)
    if weights_outer:
        grid = (n // tn, m // tm)
        ij = lambda g0, g1: (g1, g0)
    else:
        grid = (m // tm, n // tn)
        ij = lambda g0, g1: (g0, g1)

    def a_spec(a):
        if mode == "tn":
            return pl.BlockSpec((a.shape[0], tm), lambda g0, g1: (0, ij(g0, g1)[0]))
        return pl.BlockSpec((tm, a.shape[1]), lambda g0, g1: (ij(g0, g1)[0], 0))

    def b_spec(b):
        if mode == "nt":
            return pl.BlockSpec((tn, b.shape[1]), lambda g0, g1: (ij(g0, g1)[1], 0))
        return pl.BlockSpec((b.shape[0], tn), lambda g0, g1: (0, ij(g0, g1)[1]))

    in_specs, flat = [], []
    for a, b in pairs:
        in_specs += [a_spec(a), b_spec(b)]
        flat += [a, b]
    if bias is not None:
        in_specs.append(pl.BlockSpec((1, tn), lambda g0, g1: (0, ij(g0, g1)[1])))
        flat.append(bias)
    n_pairs = len(pairs)

    def body(*refs):
        acc = None
        for p in range(n_pairs):
            t = _dot(refs[2 * p][...], refs[2 * p + 1][...], dims)
            acc = t if acc is None else acc + t
        if bias is not None:
            acc = acc + refs[2 * n_pairs][...]
        refs[-1][...] = acc.astype(refs[-1].dtype)

    return pl.pallas_call(
        body, name=name, grid=grid, in_specs=in_specs,
        out_specs=pl.BlockSpec((tm, tn), lambda g0, g1: ij(g0, g1)),
        out_shape=SDS((m, n), out_dtype), compiler_params=_cparams(),
    )(*flat)


def _mm_resid(name, a, b, s, mg, k, coef, n_lat, tn_cap=512):
    t_rows, n = a.shape[0], b.shape[1]
    tm = _tile(math.gcd(n_lat, t_rows), 256, 16)
    tn = _tile(n, tn_cap, 128)
    nlt = n_lat // tm
    n_groups = 2 if nlt < t_rows // tm else 1

    def grp(i):
        return jnp.where(i >= nlt, 1, 0) if n_groups == 2 else 0

    def body(a_ref, b_ref, s_ref, mg_ref, so_ref, o_ref):
        o = _dot(a_ref[...], b_ref[...], NN)
        gate = mg_ref[0, 3 * k + 2:3 * k + 3, :]
        o_ref[...] = o
        so_ref[...] = s_ref[...] + (coef * gate) * o

    return pl.pallas_call(
        body, name=name, grid=(n // tn, t_rows // tm),
        in_specs=[pl.BlockSpec((tm, a.shape[1]), lambda j, i: (i, 0)),
                  pl.BlockSpec((b.shape[0], tn), lambda j, i: (0, j)),
                  pl.BlockSpec((tm, tn), lambda j, i: (i, j)),
                  pl.BlockSpec((1, mg.shape[1], tn), lambda j, i: (grp(i), 0, j))],
        out_specs=[pl.BlockSpec((tm, tn), lambda j, i: (i, j)), pl.BlockSpec((tm, tn), lambda j, i: (i, j))],
        out_shape=[SDS((t_rows, n), F32), SDS((t_rows, n), F32)], compiler_params=_cparams(),
    )(a, b, s, mg)


def _ffn_up(name, u, wg_t, wu_t):
    t_rows, f = u.shape[0], wg_t.shape[0]
    tm, tn = _tile(t_rows, 1024, 16), _tile(f, 256, 128)

    def body(u_ref, wg_ref, wu_ref, a_ref, b_ref, h_ref):
        uu = u_ref[...]
        a = _dot(uu, wg_ref[...], NT)
        b = _dot(uu, wu_ref[...], NT)
        a_ref[...] = a.astype(BF16)
        b_ref[...] = b.astype(BF16)
        h_ref[...] = (a * jax.nn.sigmoid(a) * b).astype(BF16)

    w_spec = pl.BlockSpec((tn, u.shape[1]), lambda i, j: (j, 0))
    o_spec = pl.BlockSpec((tm, tn), lambda i, j: (i, j))
    return pl.pallas_call(
        body, name=name, grid=(t_rows // tm, f // tn),
        in_specs=[pl.BlockSpec((tm, u.shape[1]), lambda i, j: (i, 0)), w_spec, w_spec],
        out_specs=[o_spec, o_spec, o_spec], out_shape=[SDS((t_rows, f), BF16)] * 3, compiler_params=_cparams(),
    )(u, wg_t, wu_t)


def _ffn_dact(name, do, wd, a, b):
    t_rows, f = do.shape[0], wd.shape[0]
    tm, tn = _tile(t_rows, 1024, 16), _tile(f, 256, 128)

    def body(do_ref, wd_ref, a_ref, b_ref, da_ref, db_ref):
        dh = _dot(do_ref[...], wd_ref[...], NT)
        av = a_ref[...].astype(F32)
        bv = b_ref[...].astype(F32)
        sg = jax.nn.sigmoid(av)
        da_ref[...] = (dh * bv * (sg * (1.0 + av * (1.0 - sg)))).astype(BF16)
        db_ref[...] = (dh * (av * sg)).astype(BF16)

    t_spec = pl.BlockSpec((tm, tn), lambda i, j: (i, j))
    return pl.pallas_call(
        body, name=name, grid=(t_rows // tm, f // tn),
        in_specs=[pl.BlockSpec((tm, do.shape[1]), lambda i, j: (i, 0)),
                  pl.BlockSpec((tn, wd.shape[1]), lambda i, j: (j, 0)), t_spec, t_spec],
        out_specs=[t_spec, t_spec], out_shape=[SDS((t_rows, f), BF16)] * 2, compiler_params=_cparams(),
    )(do, wd, a, b)


def _row_tm(t_rows, n_lat):
    return _tile(math.gcd(t_rows, n_lat), 256, 16)


def _adaln_fwd(name, s, mg, k, n_lat):
    t_rows = s.shape[0]

    def fn(rv, vv):
        m = vv[0]
        n, _ = _rms(rv[0])
        u = (n * m[9 + k:10 + k]) * (1.0 + m[3 * k + 1:3 * k + 2]) + m[3 * k:3 * k + 1]
        return [u], []

    (u,), _ = _rowwise(name, fn, t_rows, _row_tm(t_rows, n_lat), n_lat, [(s, D_MODEL, 0)], [mg], [(D_MODEL, BF16)], [])
    return u


def _adaln_bwd(name, s, du, ds_out, mg, k, n_lat):
    t_rows = s.shape[0]

    def fn(rv, vv):
        m = vv[0]
        gain, scale = m[9 + k:10 + k], m[3 * k + 1:3 * k + 2]
        n, r = _rms(rv[0])
        d_u = rv[1]
        dxn = d_u * (1.0 + scale)
        ds = _rms_bwd(n, r, dxn * gain)
        return [rv[2] + ds], [_colsum(d_u), _colsum(d_u * (n * gain)), _colsum(dxn * n)]

    (ds_in,), accs = _rowwise(name, fn, t_rows, _row_tm(t_rows, n_lat), n_lat,
                              [(s, D_MODEL, 0), (du, D_MODEL, 0), (ds_out, D_MODEL, 0)], [mg],
                              [(D_MODEL, F32)], [D_MODEL] * 3)
    return ds_in, accs


def _gate_bwd(name, ds_out, o, mg, k, coef, n_lat):
    t_rows = o.shape[0]

    def fn(rv, vv):
        gate = vv[0][3 * k + 2:3 * k + 3]
        d = coef * rv[0]
        return [d * gate], [_colsum(d * rv[1])]

    (do,), (dgate,) = _rowwise(name, fn, t_rows, _row_tm(t_rows, n_lat), n_lat,
                               [(ds_out, D_MODEL, 0), (o, D_MODEL, 0)], [mg], [(D_MODEL, BF16)], [D_MODEL])
    return do, dgate


def _rmsnorm_fwd(name, x, width, colblk, gain, t_rows):
    def fn(rv, vv):
        n, _ = _rms(rv[0])
        return [n * vv[0]], []

    (y,), _ = _rowwise(name, fn, t_rows, _tile(t_rows, 256, 16), t_rows, [(x, width, colblk)],
                       [gain.reshape(1, 1, width)], [(width, BF16)], [])
    return y


def _rmsnorm_bwd(name, x, width, colblk, dy, gain, t_rows):
    def fn(rv, vv):
        n, r = _rms(rv[0])
        return [_rms_bwd(n, r, rv[1] * vv[0])], [_colsum(rv[1] * n)]

    (dx,), (dgain,) = _rowwise(name, fn, t_rows, _tile(t_rows, 256, 16), t_rows,
                               [(x, width, colblk), (dy, width, 0)], [gain.reshape(1, 1, width)],
                               [(width, F32)], [width])
    return dx, dgain


def _final_loss(name, h, target, gain):
    t_rows = h.shape[0]
    inv_d = 1.0 / D_MODEL

    def fn(rv, vv):
        g = vv[0]
        n, r = _rms(rv[0])
        e = n * g - rv[1]
        dy = e * inv_d
        return [_rms_bwd(n, r, dy * g)], [_colsum(e * e), _colsum(dy * n)]

    (dh,), (sq, dgain) = _rowwise(name, fn, t_rows, _tile(t_rows, 256, 16), t_rows,
                                  [(h, D_MODEL, 0), (target, D_MODEL, 0)], [gain.reshape(1, 1, D_MODEL)],
                                  [(D_MODEL, F32)], [D_MODEL, D_MODEL])
    return dh, sq, dgain


def _rope(name, z, width, colblk, cos, sin, perm, backward, out_dtype):
    t_rows = cos.shape[0]

    def body(z_ref, c_ref, s_ref, p_ref, o_ref):
        zz = z_ref[...]
        pre = zz * s_ref[...] if backward else zz
        hi = pre.astype(BF16)
        lo = (pre - hi.astype(F32)).astype(BF16)
        rot = _dot(hi, p_ref[...], NN) + _dot(lo, p_ref[...], NN)
        if not backward:
            rot = rot * s_ref[...]
        o_ref[...] = (zz * c_ref[...] + rot).astype(o_ref.dtype)

    tm = _tile(t_rows, 256, 16)
    t_spec = pl.BlockSpec((tm, width), lambda i: (i, 0))
    return pl.pallas_call(
        body, name=name, grid=(t_rows // tm,),
        in_specs=[pl.BlockSpec((tm, width), lambda i: (i, colblk)), t_spec, t_spec,
                  pl.BlockSpec((width, width), lambda i: (0, 0))],
        out_specs=t_spec, out_shape=SDS((t_rows, width), out_dtype), compiler_params=_cparams(),
    )(z, cos, sin, perm)


def _window_sum(x, w, transposed):
    n_rows = x.shape[0]
    zeros = jnp.zeros((POOL_PAD, x.shape[1]), F32)
    y = jnp.concatenate([zeros, x, zeros], axis=0)
    total = n_rows + 2 * POOL_PAD
    if transposed:
        y = y + pltpu.roll(y, total - 1, 0)
    else:
        y = y + pltpu.roll(y, 1, 0)
    step = 1
    while 2 * step < w:
        y = pltpu.roll(y, step, 0) + pltpu.roll(y, total - step, 0)
        step *= 2
    return y[POOL_PAD:POOL_PAD + n_rows]


def _window_count(n_rows, w):
    t = lax.broadcasted_iota(jnp.int32, (n_rows, 1), 0)
    lo = jnp.maximum(t - w // 2, 0)
    hi = jnp.minimum(t + (w - w // 2 - 1), n_rows - 1)
    return (hi - lo + 1).astype(F32)


def _pool_fwd(name, proj, n_rows, w_grp, scale):
    def body(x_ref, w_ref, sc_ref, y_ref, p_ref):
        for g, w in enumerate(POOL_WINDOWS):
            cols = slice(g * POOL_GROUP_DIM, (g + 1) * POOL_GROUP_DIM)
            x = x_ref[:, cols]
            p = _window_sum(x, w, False) * (1.0 / _window_count(n_rows, w)) - x
            pb = p.astype(BF16)
            p_ref[:, cols] = pb
            y_ref[:, cols] = (_dot(pb, w_ref[g], NN) * sc_ref[:, cols]).astype(BF16)

    blk = pl.BlockSpec((n_rows, POOL_DIM), lambda i: (0, 0))
    return pl.pallas_call(
        body, name=name, grid=(1,),
        in_specs=[blk, pl.BlockSpec(w_grp.shape, lambda i: (0, 0, 0)), pl.BlockSpec((1, POOL_DIM), lambda i: (0, 0))],
        out_specs=[blk, blk], out_shape=[SDS((n_rows, POOL_DIM), BF16)] * 2, compiler_params=_cparams(),
    )(proj, w_grp, scale)


def _pool_bwd(name, dcat, n_rows, p, w_grp, scale):
    def body(dy_ref, p_ref, w_ref, sc_ref, dx_ref, dw_ref, dsc_ref):
        for g, w in enumerate(POOL_WINDOWS):
            cols = slice(g * POOL_GROUP_DIM, (g + 1) * POOL_GROUP_DIM)
            dy = dy_ref[:, cols]
            pb = p_ref[:, cols]
            pw = _dot(pb, w_ref[g], NN)
            dsc_ref[:, cols] = _colsum(dy * pw)
            dpw = (dy * sc_ref[:, cols]).astype(BF16)
            dw_ref[g] = _dot(pb, dpw, TN)
            dp = _dot(dpw, w_ref[g], NT)
            dx_ref[:, cols] = _window_sum(dp * (1.0 / _window_count(n_rows, w)), w, True) - dp

    blk = pl.BlockSpec((n_rows, POOL_DIM), lambda i: (0, 0))
    w_spec = pl.BlockSpec(w_grp.shape, lambda i: (0, 0, 0))
    v_spec = pl.BlockSpec((1, POOL_DIM), lambda i: (0, 0))
    return pl.pallas_call(
        body, name=name, grid=(1,), in_specs=[blk, blk, w_spec, v_spec], out_specs=[blk, w_spec, v_spec],
        out_shape=[SDS((n_rows, POOL_DIM), F32), SDS(w_grp.shape, F32), SDS((1, POOL_DIM), F32)],
        compiler_params=_cparams(),
    )(dcat, p, w_grp, scale)


def _attn_fwd(name, q, k, v):
    h, n_q, _ = q.shape
    n_k = k.shape[1]
    tq = _tile(n_q, 256, 16)

    def body(q_ref, k_ref, v_ref, o_ref, lse_ref):
        s = _dot(q_ref[...], k_ref[...], NT) * ATTN_SCALE
        m = jnp.max(s, axis=-1, keepdims=True)
        e = jnp.exp(s - m)
        l = jnp.sum(e, axis=-1, keepdims=True)
        p = (e * (1.0 / l)).astype(BF16)
        o_ref[...] = _dot(p, v_ref[...], NN).astype(BF16)
        lse_ref[...] = m + jnp.log(l)

    return pl.pallas_call(
        body, name=name, grid=(h, n_q // tq),
        in_specs=[pl.BlockSpec((None, tq, HEAD_PAD), lambda hh, i: (hh, i, 0)),
                  pl.BlockSpec((None, n_k, HEAD_PAD), lambda hh, i: (hh, 0, 0)),
                  pl.BlockSpec((None, n_k, V_HEAD), lambda hh, i: (hh, 0, 0))],
        out_specs=[pl.BlockSpec((None, tq, V_HEAD), lambda hh, i: (hh, i, 0)),
                   pl.BlockSpec((None, tq, 1), lambda hh, i: (hh, i, 0))],
        out_shape=[SDS((h, n_q, V_HEAD), BF16), SDS((h, n_q, 1), F32)], compiler_params=_cparams(),
    )(q, k, v)


def _attn_bwd(name, q, k, v, o, lse, do):
    h, n_q, _ = q.shape
    n_k = k.shape[1]
    tq = _tile(n_q, 256, 16)

    def body(q_ref, k_ref, v_ref, o_ref, lse_ref, do_ref, dq_ref, dk_ref, dv_ref, dks_ref):
        hh, i = pl.program_id(0), pl.program_id(1)
        qq, kk, dd = q_ref[...], k_ref[...], do_ref[...]
        s = _dot(qq, kk, NT) * ATTN_SCALE
        p = jnp.exp(s - lse_ref[...])
        dp = _dot(dd, v_ref[...], NT)
        delta = jnp.sum(dd.astype(F32) * o_ref[...].astype(F32), axis=-1, keepdims=True)
        ds = (p * (dp - delta) * ATTN_SCALE).astype(BF16)
        dq_ref[...] = _dot(ds, kk, NN)
        dk = _dot(ds, qq, TN)
        dv = _dot(p.astype(BF16), dd, TN)

        @pl.when(i == 0)
        def _():
            dk_ref[...] = dk
            dv_ref[...] = dv

        @pl.when(i > 0)
        def _():
            dk_ref[...] += dk
            dv_ref[...] += dv

        @pl.when((i == 0) & (hh == 0))
        def _():
            dks_ref[...] = dk

        @pl.when((i > 0) | (hh > 0))
        def _():
            dks_ref[...] += dk

    q_spec = pl.BlockSpec((None, tq, HEAD_PAD), lambda hh, i: (hh, i, 0))
    k_spec = pl.BlockSpec((None, n_k, HEAD_PAD), lambda hh, i: (hh, 0, 0))
    v_spec = pl.BlockSpec((None, n_k, V_HEAD), lambda hh, i: (hh, 0, 0))
    o_spec = pl.BlockSpec((None, tq, V_HEAD), lambda hh, i: (hh, i, 0))
    return pl.pallas_call(
        body, name=name, grid=(h, n_q // tq),
        in_specs=[q_spec, k_spec, v_spec, o_spec, pl.BlockSpec((None, tq, 1), lambda hh, i: (hh, i, 0)), o_spec],
        out_specs=[q_spec, k_spec, v_spec, pl.BlockSpec((n_k, HEAD_PAD), lambda hh, i: (0, 0))],
        out_shape=[SDS((h, n_q, HEAD_PAD), F32), SDS((h, n_k, HEAD_PAD), F32), SDS((h, n_k, V_HEAD), F32),
                   SDS((n_k, HEAD_PAD), F32)],
        compiler_params=_cparams(),
    )(q, k, v, o, lse, do)


CONV_COLS = 256


def _shift_rows(x, d):
    n_rows = x.shape[0]
    t = lax.broadcasted_iota(jnp.int32, (n_rows, 1), 0)
    if d > 0:
        return jnp.where(t >= d, pltpu.roll(x, d, 0), 0.0)
    return jnp.where(t < n_rows + d, pltpu.roll(x, n_rows + d, 0), 0.0)


def _conv_fwd(name, z3, conv_w):
    n_rows = z3.shape[0]
    nb = D_MODEL // CONV_COLS

    def body(b_ref, c_ref, v_ref, w_ref, y_ref):
        z = c_ref[...] * v_ref[...]
        zc = w_ref[0:1, :] * _shift_rows(z, 1) + w_ref[1:2, :] * z + w_ref[2:3, :] * _shift_rows(z, -1)
        y_ref[...] = (b_ref[...] * zc).astype(BF16)

    def part(k):
        return pl.BlockSpec((n_rows, CONV_COLS), lambda j: (0, k * nb + j))

    return pl.pallas_call(
        body, name=name, grid=(nb,),
        in_specs=[part(0), part(1), part(2), pl.BlockSpec((3, CONV_COLS), lambda j: (0, j))],
        out_specs=pl.BlockSpec((n_rows, CONV_COLS), lambda j: (0, j)),
        out_shape=SDS((n_rows, D_MODEL), BF16), compiler_params=_cparams(),
    )(z3, z3, z3, conv_w)


def _conv_bwd(name, dy, z3, conv_w):
    n_rows = z3.shape[0]
    nb = D_MODEL // CONV_COLS

    def body(dy_ref, b_ref, c_ref, v_ref, w_ref, db_ref, dc_ref, dv_ref, dw_ref):
        c, v, d_y = c_ref[...], v_ref[...], dy_ref[...]
        z = c * v
        z_dn, z_up = _shift_rows(z, 1), _shift_rows(z, -1)
        zc = w_ref[0:1, :] * z_dn + w_ref[1:2, :] * z + w_ref[2:3, :] * z_up
        db_ref[...] = (d_y * zc).astype(BF16)
        dzc = d_y * b_ref[...]
        dz = w_ref[0:1, :] * _shift_rows(dzc, -1) + w_ref[1:2, :] * dzc + w_ref[2:3, :] * _shift_rows(dzc, 1)
        dc_ref[...] = (dz * v).astype(BF16)
        dv_ref[...] = (dz * c).astype(BF16)
        dw_ref[0:1, :] = _colsum(dzc * z_dn)
        dw_ref[1:2, :] = _colsum(dzc * z)
        dw_ref[2:3, :] = _colsum(dzc * z_up)

    def part(k):
        return pl.BlockSpec((n_rows, CONV_COLS), lambda j: (0, k * nb + j))

    col = pl.BlockSpec((n_rows, CONV_COLS), lambda j: (0, j))
    w_spec = pl.BlockSpec((3, CONV_COLS), lambda j: (0, j))
    return pl.pallas_call(
        body, name=name, grid=(nb,), in_specs=[col, part(0), part(1), part(2), w_spec],
        out_specs=[col, col, col, w_spec],
        out_shape=[SDS((n_rows, D_MODEL), BF16)] * 3 + [SDS((3, D_MODEL), F32)], compiler_params=_cparams(),
    )(dy, z3, z3, z3, conv_w)


def _silu_rows(name, x):
    def body(x_ref, s_ref, d_ref):
        xx = x_ref[...]
        sg = jax.nn.sigmoid(xx)
        s_ref[...] = (xx * sg).astype(BF16)
        d_ref[...] = sg * (1.0 + xx * (1.0 - sg))

    return pl.pallas_call(body, name=name, out_shape=[SDS(x.shape, BF16), SDS(x.shape, F32)])(x)


def _sum_rows(name, x, scale=None):
    r, n = x.shape
    tn = _tile(n, 8192, 128)

    def body(*refs):
        acc = jnp.sum(refs[0][...].astype(F32), axis=0, keepdims=True)
        if scale is not None:
            acc = acc * refs[1][...]
        refs[-1][...] = acc

    in_specs = [pl.BlockSpec((r, tn), lambda j: (0, j))]
    args = [x]
    if scale is not None:
        in_specs.append(pl.BlockSpec((1, tn), lambda j: (0, j)))
        args.append(scale)
    return pl.pallas_call(body, name=name, grid=(n // tn,), in_specs=in_specs,
                          out_specs=pl.BlockSpec((1, tn), lambda j: (0, j)), out_shape=SDS((1, n), F32))(*args)


def _sum_slots(name, x):
    n_slots, r, c = x.shape
    tr = _tile(r, 432, 16)

    def body(x_ref, o_ref):
        acc = x_ref[0].astype(F32)
        for sl in range(1, n_slots):
            acc = acc + x_ref[sl].astype(F32)
        o_ref[...] = acc

    return pl.pallas_call(body, name=name, grid=(r // tr,),
                          in_specs=[pl.BlockSpec((n_slots, tr, c), lambda i: (0, i, 0))],
                          out_specs=pl.BlockSpec((tr, c), lambda i: (i, 0)), out_shape=SDS((r, c), F32),
                          compiler_params=_cparams())(x)


def _adamw(name, w, g, m, v):
    shape = w.shape
    cols = shape[-1]
    rows = w.size // cols
    tr = _tile(rows, 512, 8)
    bc1 = 1.0 - ADAM_B1 ** ADAM_STEP
    bc2 = 1.0 - ADAM_B2 ** ADAM_STEP

    def body(w_ref, g_ref, m_ref, v_ref, d_ref, nm_ref, nv_ref):
        gg = g_ref[...]
        nm = ADAM_B1 * m_ref[...] + (1.0 - ADAM_B1) * gg
        nv = ADAM_B2 * v_ref[...] + (1.0 - ADAM_B2) * (gg * gg)
        nm_ref[...] = nm
        nv_ref[...] = nv
        d_ref[...] = -ADAM_LR * ((nm / bc1) / (jnp.sqrt(nv / bc2) + ADAM_EPS) + ADAM_WD * w_ref[...])

    spec = pl.BlockSpec((tr, cols), lambda i: (i, 0))
    outs = pl.pallas_call(body, name=name, grid=(rows // tr,), in_specs=[spec] * 4, out_specs=[spec] * 3,
                          out_shape=[SDS((rows, cols), F32)] * 3, compiler_params=_cparams())(
        w.reshape(rows, cols), g.reshape(rows, cols), m.reshape(rows, cols), v.reshape(rows, cols))
    return tuple(t.reshape(shape) for t in outs)


def _exchange(name, x, scatter):
    blk = x.shape[1:] if scatter else x.shape

    def body(x_ref, out_ref, send_sems, recv_sems, local_sem):
        mx, my, mc = lax.axis_index("x"), lax.axis_index("y"), lax.axis_index("c")
        me = 4 * mx + 2 * my + mc
        own = pltpu.make_async_copy(x_ref.at[me] if scatter else x_ref, out_ref.at[me], local_sem)
        own.start()
        copies = []
        for kk in range(1, N_DEV):
            px = jnp.bitwise_xor(mx, (kk >> 2) & 1)
            py = jnp.bitwise_xor(my, (kk >> 1) & 1)
            pc = jnp.bitwise_xor(mc, kk & 1)
            peer = 4 * px + 2 * py + pc
            send = pltpu.make_async_remote_copy(
                src_ref=x_ref.at[peer] if scatter else x_ref, dst_ref=out_ref.at[me],
                send_sem=send_sems.at[kk - 1], recv_sem=recv_sems.at[kk - 1],
                device_id=(px, py, pc), device_id_type=MESH)
            send.start()
            arrival = pltpu.make_async_remote_copy(
                src_ref=x_ref.at[peer] if scatter else x_ref, dst_ref=out_ref.at[peer],
                send_sem=send_sems.at[kk - 1], recv_sem=recv_sems.at[kk - 1],
                device_id=(px, py, pc), device_id_type=MESH)
            copies.append((send, arrival))
        for send, arrival in copies:
            arrival.wait_recv()
            send.wait_send()
        own.wait()

    return pl.pallas_call(
        body, name=name, out_shape=SDS((N_DEV,) + tuple(blk), x.dtype),
        in_specs=[pl.BlockSpec(memory_space=pl.ANY)], out_specs=pl.BlockSpec(memory_space=pl.ANY),
        scratch_shapes=[pltpu.SemaphoreType.DMA((N_DEV - 1,)), pltpu.SemaphoreType.DMA((N_DEV - 1,)),
                        pltpu.SemaphoreType.DMA],
    )(x)


def _rope_perm(pre, reps, post):
    half = QK_ROPE // 4
    width = reps * (pre + QK_ROPE) + post
    p = np.zeros((width, width), np.float32)
    for rep in range(reps):
        s0 = rep * (pre + QK_ROPE) + pre
        for base in (s0, s0 + 2 * half):
            for i in range(half):
                p[base + half + i, base + i] = -1.0
                p[base + i, base + half + i] = 1.0
    return p


def _rope_tables(n_lat, t_rows, pre, reps, post):
    half = QK_ROPE // 4
    pos = jnp.arange(n_lat)
    freqs = jnp.power(ROPE_THETA, -jnp.arange(0, 2 * half, 2, dtype=F32) / (2 * half))
    ang_r = (pos // GRID_W).astype(F32)[:, None] * freqs
    ang_c = (pos % GRID_W).astype(F32)[:, None] * freqs
    ang = jnp.concatenate([ang_r, ang_r, ang_c, ang_c], axis=-1)

    def table(fn, plain):
        slot = jnp.concatenate([jnp.full((n_lat, pre), plain, F32), fn(ang)], axis=-1)
        t = jnp.concatenate([jnp.tile(slot, (1, reps)), jnp.full((n_lat, post), plain, F32)], axis=-1)
        return jnp.concatenate([t, jnp.full((t_rows - n_lat, t.shape[1]), plain, F32)], axis=0)

    return table(jnp.cos, 1.0), table(jnp.sin, 0.0)


def _ffn_half_fwd(tag, s, mg, k, feed, i, coef, n_lat):
    u = _adaln_fwd(f"{tag}_adaln", s, mg, k, n_lat)
    wg_t, wu_t = feed.weights(f"{tag}_up", [f"gate_t{i}", f"up_t{i}"], u)
    a, b, hid = _ffn_up(f"{tag}_up", u, wg_t, wu_t)
    (wd,) = feed.weights(f"{tag}_down", [f"down{i}"], hid)
    s_out, o = _mm_resid(f"{tag}_down", hid, wd, s, mg, k, coef, n_lat)
    return s_out, (s, u, a, b, hid, o, wg_t, wu_t, wd)


def _ffn_half_bwd(tag, ds_out, saved, mg, k, feed, i, coef, n_lat):
    s, u, a, b, hid, o, wg_t, wu_t, wd = saved
    do, dgate = _gate_bwd(f"{tag}_dgate", ds_out, o, mg, k, coef, n_lat)
    da, db = _ffn_dact(f"{tag}_dact", do, wd, a, b)
    dwd = _mm(f"{tag}_dwd", [(hid, do)], "tn", BF16, 512, 512)
    dwg_t = _mm(f"{tag}_dwg", [(da, u)], "tn", BF16, 512, 512)
    dwu_t = _mm(f"{tag}_dwu", [(db, u)], "tn", BF16, 512, 512)
    token = feed.grads(tag, {f"down{i}": dwd, f"gate_t{i}": dwg_t, f"up_t{i}": dwu_t})
    du = _mm(f"{tag}_du", [(da, wg_t), (db, wu_t)], "nn", F32, 384, 512, bias=_after(token))
    ds_in, (dshift, dscale, dgain) = _adaln_bwd(f"{tag}_dadaln", s, du, ds_out, mg, k, n_lat)
    return ds_in, dict(shift=dshift, scale=dscale, gate=dgate, gain=dgain)


def _after(token):
    return jnp.zeros((1, D_MODEL), F32) + token


def _mod_grad(parts, n_groups):
    rows = []
    zero = jnp.zeros((n_groups, 1, D_MODEL), F32)
    for k in range(3):
        for nm in ("shift", "scale", "gate"):
            t = parts[k].get(nm, zero)
            if t.shape[0] < n_groups:
                t = jnp.concatenate([t, jnp.zeros((n_groups - t.shape[0], 1, D_MODEL), F32)], axis=0)
            rows.append(t)
    return jnp.concatenate(rows, axis=1).reshape(n_groups, N_MOD * D_MODEL)


def _local_step(x, ctx, target, mod_h, mod_g, norm_g, feed, pool_w, pool_scale, q_norm_g, kv_norm_g, conv_w,
                final_norm_g):
    n_lat, n_ctx = x.shape[0], ctx.shape[0]
    t_all = n_lat + n_ctx
    mg0 = jnp.stack([jnp.concatenate([mod_h[0], norm_g[0]], axis=0), jnp.concatenate([mod_g, norm_g[0]], axis=0)])
    mg1 = jnp.concatenate([mod_h[1], norm_g[1]], axis=0)[None]

    s0 = jnp.concatenate([x, ctx], axis=0) + feed.start_token()
    s1, sv_f00 = _ffn_half_fwd("l0f0", s0, mg0, 0, feed, 0, 0.5, n_lat)

    ua = _adaln_fwd("l0m_adaln", s1, mg0, 1, n_lat)
    w_in, w_uq, w_ukv_t, w_ab_out = feed.weights("l0m", ["in_t", "uq", "ukv_t", "ab_out"], ua)
    kv_rows = KV_RANK + QK_ROPE
    w_in_t = jnp.concatenate([
        w_in[:POOL_DIM], jnp.zeros((PA_CQ - POOL_DIM, D_MODEL), BF16), w_in[POOL_DIM:POOL_DIM + Q_RANK],
        w_in[POOL_DIM + Q_RANK:], jnp.zeros((PA_KV_W - kv_rows, D_MODEL), BF16)], axis=0)
    proj = _mm("l0m_proj", [(ua, w_in_t)], "nt", F32, 768, 384)
    pool_y, pool_p = _pool_fwd("l0m_pool", proj, n_lat, pool_w.astype(BF16), pool_scale)
    nq = _rmsnorm_fwd("l0m_qnorm", proj, Q_RANK, PA_CQ // Q_RANK, q_norm_g, n_lat)
    q_lin = _mm("l0m_q", [(nq, w_uq)], "nn", F32, 512, 768)
    cos_q, sin_q = _rope_tables(n_lat, n_lat, QK_NOPE, HEADS, 0)
    perm_q = _rope_perm(QK_NOPE, HEADS, 0)
    q_rot = _rope("l0m_qrope", q_lin, Q_RANK, 0, cos_q, sin_q, jnp.asarray(perm_q, BF16), False, BF16)
    cos_k, sin_k = _rope_tables(n_lat, t_all, KV_RANK, 1, PA_KV_W - kv_rows)
    perm_k = _rope_perm(KV_RANK, 1, PA_KV_W - kv_rows)
    kvr = _rope("l0m_krope", proj, PA_KV_W, PA_KV // PA_KV_W, cos_k, sin_k, jnp.asarray(perm_k, BF16), False, F32)
    nkv = _rmsnorm_fwd("l0m_kvnorm", kvr, KV_RANK, 0, kv_norm_g, t_all)
    kv = _mm("l0m_kv", [(nkv, w_ukv_t)], "nt", BF16, 768, 512)
    qh = jnp.pad(q_rot.reshape(n_lat, HEADS, QK_HEAD), ((0, 0), (0, 0), (0, HEAD_PAD - QK_HEAD))).transpose(1, 0, 2)
    kvh = kv.reshape(t_all, HEADS, QK_NOPE + V_HEAD)
    k_rope = jnp.broadcast_to(kvr[:, None, KV_RANK:KV_RANK + QK_ROPE].astype(BF16), (t_all, HEADS, QK_ROPE))
    kh = jnp.concatenate([kvh[:, :, :QK_NOPE], k_rope, jnp.zeros((t_all, HEADS, HEAD_PAD - QK_HEAD), BF16)],
                         axis=-1).transpose(1, 0, 2)
    vh = kvh[:, :, QK_NOPE:].transpose(1, 0, 2)
    oh, lse = _attn_fwd("l0m_attn", qh, kh, vh)
    cat = jnp.concatenate([pool_y, oh.transpose(1, 0, 2).reshape(n_lat, HEADS * V_HEAD)], axis=-1)
    h1 = s1[:n_lat]
    h2, mix_o = _mm_resid("l0m_out", cat, w_ab_out, h1, mg0[:1], 1, 1.0, n_lat)

    h3, sv_f01 = _ffn_half_fwd("l0f1", h2, mg0[:1], 2, feed, 1, 0.5, n_lat)

    h4, sv_f10 = _ffn_half_fwd("l1f0", h3, mg1, 0, feed, 2, 0.5, n_lat)
    uc = _adaln_fwd("l1m_adaln", h4, mg1, 1, n_lat)
    w_cin_t, w_c_out = feed.weights("l1m", ["cin_t", "c_out"], uc)
    z3 = _mm("l1m_in", [(uc, w_cin_t)], "nt", F32, 512, 512)
    yc = _conv_fwd("l1m_conv", z3, conv_w)
    h5, conv_o = _mm_resid("l1m_out", yc, w_c_out, h4, mg1, 1, 1.0, n_lat)
    h6, sv_f11 = _ffn_half_fwd("l1f1", h5, mg1, 2, feed, 3, 0.5, n_lat)

    dh6, sq_cols, d_final_g = _final_loss("loss_head", h6, target, final_norm_g)
    g = {}
    dh5, g["f11"] = _ffn_half_bwd("l1f1", dh6, sv_f11, mg1, 2, feed, 3, 0.5, n_lat)

    do_c, dgate_c = _gate_bwd("l1m_dgate", dh5, conv_o, mg1, 1, 1.0, n_lat)
    dyc = _mm("l1m_dy", [(do_c, w_c_out)], "nt", F32, 512, 512)
    d_c_out = _mm("l1m_dwout", [(yc, do_c)], "tn", BF16, 512, 512)
    db_, dc_, dv_, d_conv_w = _conv_bwd("l1m_dconv", dyc, z3, conv_w)
    dz3 = jnp.concatenate([db_, dc_, dv_], axis=-1)
    d_cin_t = _mm("l1m_dwin", [(dz3, uc)], "tn", BF16, 512, 512)
    token = feed.grads("l1m", {"c_out": d_c_out, "cin_t": d_cin_t})
    duc = _mm("l1m_du", [(dz3, w_cin_t)], "nn", F32, 512, 512, bias=_after(token))
    dh4, (dsh_c, dsc_c, dgn_c) = _adaln_bwd("l1m_dadaln", h4, duc, dh5, mg1, 1, n_lat)
    dh3, g["f10"] = _ffn_half_bwd("l1f0", dh4, sv_f10, mg1, 0, feed, 2, 0.5, n_lat)

    dh2, g["f01"] = _ffn_half_bwd("l0f1", dh3, sv_f01, mg0[:1], 2, feed, 1, 0.5, n_lat)

    do_a, dgate_a = _gate_bwd("l0m_dgate", dh2, mix_o, mg0[:1], 1, 1.0, n_lat)
    dcat = _mm("l0m_dcat", [(do_a, w_ab_out)], "nt", F32, 512, 512)
    d_ab_out = _mm("l0m_dwout", [(cat, do_a)], "tn", BF16, 512, 512)
    d_pool_x, d_pool_w, d_pool_scale = _pool_bwd("l0m_dpool", dcat, n_lat, pool_p, pool_w.astype(BF16), pool_scale)
    doh = dcat[:, POOL_DIM:].reshape(n_lat, HEADS, V_HEAD).transpose(1, 0, 2).astype(BF16)
    dqh, dkh, dvh, dk_sum = _attn_bwd("l0m_dattn", qh, kh, vh, oh, lse, doh)
    dq_rot = dqh[:, :, :QK_HEAD].transpose(1, 0, 2).reshape(n_lat, Q_RANK)
    dq_lin = _rope("l0m_dqrope", dq_rot, Q_RANK, 0, cos_q, sin_q, jnp.asarray(perm_q.T, BF16), True, BF16)
    d_uq = _mm("l0m_dwuq", [(nq, dq_lin)], "tn", BF16, 768, 768)
    dnq = _mm("l0m_dnq", [(dq_lin, w_uq)], "nt", F32, 512, 768)
    dcq, d_q_norm_g = _rmsnorm_bwd("l0m_dqnorm", proj, Q_RANK, PA_CQ // Q_RANK, dnq, q_norm_g, n_lat)
    dkv = jnp.concatenate([dkh[:, :, :QK_NOPE], dvh], axis=-1).transpose(1, 0, 2).reshape(t_all, HEADS * HEAD_PAD)
    dkv = dkv.astype(BF16)
    dnkv = _mm("l0m_dnkv", [(dkv, w_ukv_t)], "nn", F32, 768, 256)
    d_ukv_t = _mm("l0m_dwukv", [(dkv, nkv)], "tn", BF16, 512, 256)
    dckv, d_kv_norm_g = _rmsnorm_bwd("l0m_dkvnorm", kvr, KV_RANK, 0, dnkv, kv_norm_g, t_all)
    dkvr = jnp.concatenate([dckv, dk_sum[:, QK_NOPE:QK_HEAD],
                            jnp.zeros((t_all, PA_KV_W - KV_RANK - QK_ROPE), F32)], axis=-1)
    dpb = _rope("l0m_dkrope", dkvr, PA_KV_W, 0, cos_k, sin_k, jnp.asarray(perm_k.T, BF16), True, F32)
    dproj_lat = jnp.concatenate([d_pool_x, jnp.zeros((n_lat, PA_CQ - POOL_DIM), F32), dcq, dpb[:n_lat]], axis=-1)
    dproj_ctx = jnp.concatenate([jnp.zeros((n_ctx, PA_KV), F32), dpb[n_lat:]], axis=-1)
    dproj = jnp.concatenate([dproj_lat, dproj_ctx], axis=0).astype(BF16)
    d_in_pad = _mm("l0m_dwin", [(dproj, ua)], "tn", BF16, 640, 512)
    d_in_t = jnp.concatenate([d_in_pad[:POOL_DIM], d_in_pad[PA_CQ:PA_CQ + Q_RANK],
                              d_in_pad[PA_KV:PA_KV + kv_rows]], axis=0)
    token = feed.grads("l0m", {"ab_out": d_ab_out, "uq": d_uq, "ukv_t": d_ukv_t, "in_t": d_in_t})
    dua = _mm("l0m_du", [(dproj, w_in_t)], "nn", F32, 768, 512, bias=_after(token))
    dh2_all = jnp.concatenate([dh2, jnp.zeros((n_ctx, D_MODEL), F32)], axis=0)
    ds1, (dsh_a, dsc_a, dgn_a) = _adaln_bwd("l0m_dadaln", s1, dua, dh2_all, mg0, 1, n_lat)
    ds0, g["f00"] = _ffn_half_bwd("l0f0", ds1, sv_f00, mg0, 0, feed, 0, 0.5, n_lat)

    dmod0 = _mod_grad([g["f00"], dict(shift=dsh_a, scale=dsc_a, gate=dgate_a), g["f01"]], 2)
    dmod1 = _mod_grad([g["f10"], dict(shift=dsh_c, scale=dsc_c, gate=dgate_c), g["f11"]], 1)
    d_norm_g = jnp.stack([
        jnp.concatenate([jnp.sum(g["f00"]["gain"], axis=0), jnp.sum(dgn_a, axis=0), g["f01"]["gain"][0]], axis=0),
        jnp.concatenate([g["f10"]["gain"][0], dgn_c[0], g["f11"]["gain"][0]], axis=0)])
    grads = dict(
        pool_w=d_pool_w, pool_scale=d_pool_scale, q_norm_g=d_q_norm_g[0], kv_norm_g=d_kv_norm_g[0],
        conv_w=d_conv_w, final_norm_g=d_final_g[0], norm_g=d_norm_g,
        mod_h=jnp.stack([dmod0[0], dmod1[0]]), mod_g=dmod0[1])
    return sq_cols, ds0, grads


HBM_SPEC = pl.BlockSpec(memory_space=pltpu.HBM)
SEM_SPEC = pl.BlockSpec(memory_space=pltpu.SEMAPHORE)
ANY_SPEC = pl.BlockSpec(memory_space=pl.ANY)
SIDE_EFFECT = pltpu.SideEffectType.DATAFLOW_SIDE_EFFECTING
N_PEERS = N_DEV - 1


def _mesh_place():
    mx, my, mc = lax.axis_index("x"), lax.axis_index("y"), lax.axis_index("c")
    return mx, my, mc, 4 * mx + 2 * my + mc


def _peer(place, kk):
    mx, my, mc, _ = place
    px = jnp.bitwise_xor(mx, (kk >> 2) & 1)
    py = jnp.bitwise_xor(my, (kk >> 1) & 1)
    pc = jnp.bitwise_xor(mc, kk & 1)
    return (px, py, pc), 4 * px + 2 * py + pc


def _hbm(a):
    return pltpu.with_memory_space_constraint(a, pltpu.HBM)


def _own_slots(name, srcs, scatter, after):
    n = len(srcs)

    def body(*refs):
        src, land, sems = refs[:n], refs[n + 1:2 * n + 1], refs[2 * n + 1]
        me = _mesh_place()[3]
        copies = [pltpu.make_async_copy(src[a].at[me] if scatter else src[a], land[a].at[me], sems.at[a])
                  for a in range(n)]
        for cp in copies:
            cp.start()
        for cp in copies:
            cp.wait()

    out_shape = [SDS(s.shape if scatter else (N_DEV,) + s.shape, s.dtype) for s in srcs]
    return pl.pallas_call(
        body, name=name, out_shape=out_shape, in_specs=[ANY_SPEC] * (n + 1), out_specs=[ANY_SPEC] * n,
        scratch_shapes=[pltpu.SemaphoreType.DMA((n,))],
    )(*srcs, after)


def _exchange_start(name, srcs, lands, scatter):
    n = len(srcs)

    def body(*refs):
        src, land = refs[:n], refs[n:2 * n]
        send_sems, recv_sems, token = refs[2 * n], refs[2 * n + 1], refs[-1]
        place = _mesh_place()
        for a in range(n):
            for kk in range(1, N_DEV):
                dev, peer = _peer(place, kk)
                pltpu.make_async_remote_copy(
                    src_ref=src[a].at[peer] if scatter else src[a], dst_ref=land[a].at[place[3]],
                    send_sem=send_sems.at[a * N_PEERS + kk - 1], recv_sem=recv_sems.at[a * N_PEERS + kk - 1],
                    device_id=dev, device_id_type=MESH).start()
        token[...] = jnp.zeros_like(token)

    thru = [pltpu.HBM(t.shape, t.dtype) for t in (*srcs, *lands)]
    res = pl.pallas_call(
        body, name=name,
        out_shape=(pltpu.SemaphoreType.DMA((n * N_PEERS,)), pltpu.SemaphoreType.DMA((n * N_PEERS,)), *thru,
                   SDS((8, 128), F32)),
        in_specs=[HBM_SPEC] * (2 * n),
        out_specs=(SEM_SPEC, SEM_SPEC, *([HBM_SPEC] * (2 * n)), pl.BlockSpec(memory_space=pltpu.VMEM)),
        input_output_aliases={i: 2 + i for i in range(2 * n)},
        compiler_params=pltpu.CompilerParams(has_side_effects=SIDE_EFFECT),
    )(*[_hbm(s) for s in srcs], *[_hbm(t) for t in lands])
    return res[0], res[1], list(res[2:2 + n]), list(res[2 + n:2 + 2 * n]), res[-1]


def _exchange_wait(name, send_sems, recv_sems, srcs, lands, places, scatter, after):
    n = len(srcs)

    def body(*refs):
        src, land = refs[:n], refs[n:2 * n]
        send, recv = refs[2 * n], refs[2 * n + 1]
        place = _mesh_place()
        for a in range(n):
            for kk in range(1, N_DEV):
                dev, peer = _peer(place, kk)
                cp = pltpu.make_async_remote_copy(
                    src_ref=src[a].at[peer] if scatter else src[a], dst_ref=land[a].at[peer],
                    send_sem=send.at[places[a] * N_PEERS + kk - 1], recv_sem=recv.at[places[a] * N_PEERS + kk - 1],
                    device_id=dev, device_id_type=MESH)
                cp.wait_send()
                cp.wait_recv()

    thru = [pltpu.HBM(t.shape, t.dtype) for t in (*srcs, *lands)]
    res = pl.pallas_call(
        body, name=name, out_shape=tuple(thru),
        in_specs=[HBM_SPEC] * (2 * n) + [SEM_SPEC, SEM_SPEC, ANY_SPEC], out_specs=tuple([HBM_SPEC] * (2 * n)),
        input_output_aliases={i: i for i in range(2 * n)},
        compiler_params=pltpu.CompilerParams(has_side_effects=SIDE_EFFECT),
    )(*srcs, *lands, send_sems, recv_sems, after)
    return list(res[n:])


class _Feed:
    def __init__(self, shards, after):
        self.names = list(shards)
        srcs = [shards[nm] for nm in self.names]
        lands = _own_slots("gather_own", srcs, False, after)
        self.send, self.recv, self.srcs, self.lands, self.token = _exchange_start("gather_start", srcs, lands, False)
        self.pending = []

    def start_token(self):
        return self.token[0, 0]

    def weights(self, tag, names, after):
        places = [self.names.index(nm) for nm in names]
        got = _exchange_wait(f"gather_wait_{tag}", self.send, self.recv, [self.srcs[i] for i in places],
                             [self.lands[i] for i in places], places, False, after)
        return [t.reshape((N_DEV * t.shape[1],) + t.shape[2:]) for t in got]

    def grads(self, tag, full):
        names = list(full)
        srcs = [full[nm].reshape((N_DEV, full[nm].shape[0] // N_DEV) + full[nm].shape[1:]) for nm in names]
        lands = _own_slots(f"scatter_own_{tag}", srcs, True, srcs[0])
        send, recv, srcs, lands, token = _exchange_start(f"scatter_start_{tag}", srcs, lands, True)
        self.pending.append((tag, names, send, recv, srcs, lands))
        return token[0, 0]

    def finish(self, after):
        out = {}
        for tag, names, send, recv, srcs, lands in self.pending:
            got = _exchange_wait(f"scatter_wait_{tag}", send, recv, srcs, lands, list(range(len(names))), True, after)
            for nm, slots in zip(names, got):
                out[nm] = _sum_slots(f"reduce_{nm}", slots)
        return out


def _adam_all(names, weights, grads, moms, vels):
    deltas, new_m, new_v = [], [], []
    for nm, w, g, m, v in zip(names, weights, grads, moms, vels):
        d, a, b = _adamw(f"adamw_{nm}", w, g.reshape(w.shape), m, v)
        deltas.append(d)
        new_m.append(a)
        new_v.append(b)
    return deltas, new_m, new_v


WEIGHT_NAMES = ("c_ctx", "norm_g", "w_mod", "b_mod", "ffn_w_gate", "ffn_w_up", "ffn_w_down", "ab_w_in", "pool_w",
                "pool_scale", "q_norm_g", "w_uq", "kv_norm_g", "w_ukv", "ab_w_out", "conv_w_in", "conv_w",
                "conv_w_out", "final_norm_g")


def kernel(x, c, ctx, c_ctx, norm_g, w_mod, b_mod, ffn_w_gate, ffn_w_up, ffn_w_down, ab_w_in, pool_w, pool_scale, q_norm_g, w_uq, kv_norm_g, w_ukv, ab_w_out, conv_w_in, conv_w, conv_w_out, final_norm_g, loss_target, m_c_ctx, m_norm_g, m_w_mod, m_b_mod, m_ffn_w_gate, m_ffn_w_up, m_ffn_w_down, m_ab_w_in, m_pool_w, m_pool_scale, m_q_norm_g, m_w_uq, m_kv_norm_g, m_w_ukv, m_ab_w_out, m_conv_w_in, m_conv_w, m_conv_w_out, m_final_norm_g, v_c_ctx, v_norm_g, v_w_mod, v_b_mod, v_ffn_w_gate, v_ffn_w_up, v_ffn_w_down, v_ab_w_in, v_pool_w, v_pool_scale, v_q_norm_g, v_w_uq, v_kv_norm_g, v_w_ukv, v_ab_w_out, v_conv_w_in, v_conv_w, v_conv_w_out, v_final_norm_g):
    weights = (c_ctx, norm_g, w_mod, b_mod, ffn_w_gate, ffn_w_up, ffn_w_down, ab_w_in, pool_w, pool_scale, q_norm_g,
               w_uq, kv_norm_g, w_ukv, ab_w_out, conv_w_in, conv_w, conv_w_out, final_norm_g)
    moms = (m_c_ctx, m_norm_g, m_w_mod, m_b_mod, m_ffn_w_gate, m_ffn_w_up, m_ffn_w_down, m_ab_w_in, m_pool_w,
            m_pool_scale, m_q_norm_g, m_w_uq, m_kv_norm_g, m_w_ukv, m_ab_w_out, m_conv_w_in, m_conv_w, m_conv_w_out,
            m_final_norm_g)
    vels = (v_c_ctx, v_norm_g, v_w_mod, v_b_mod, v_ffn_w_gate, v_ffn_w_up, v_ffn_w_down, v_ab_w_in, v_pool_w,
            v_pool_scale, v_q_norm_g, v_w_uq, v_kv_norm_g, v_w_ukv, v_ab_w_out, v_conv_w_in, v_conv_w, v_conv_w_out,
            v_final_norm_g)
    me = 4 * lax.axis_index("x") + 2 * lax.axis_index("y") + lax.axis_index("c")
    n_lat, n_ctx = x.shape[1], ctx.shape[1]
    d = D_MODEL
    mod_cols = w_mod.shape[-1]
    ng_sh, cw_sh = norm_g.shape[-1], conv_w.shape[-1]

    small = jnp.concatenate([c.reshape(-1), norm_g.reshape(-1), conv_w.reshape(-1)])
    small_n = -(-small.shape[0] // 1024) * 1024
    small = jnp.pad(small, (0, small_n - small.shape[0])).reshape(small_n // 128, 128)
    small_all = _exchange("gather_small", small, False).reshape(N_DEV, small_n)
    c_all = small_all[:, :d]
    o1 = d + 6 * ng_sh
    norm_g_full = small_all[:, d:o1].reshape(N_DEV, 2, 3, ng_sh).transpose(1, 2, 0, 3).reshape(2, 3, d)
    conv_w_full = small_all[:, o1:o1 + 3 * cw_sh].reshape(N_DEV, 3, cw_sh).transpose(1, 0, 2).reshape(3, d)

    cond = jnp.concatenate([c_all, jnp.broadcast_to(c_ctx[None, :], (N_DEV, d))], axis=0)
    sil, dsil = _silu_rows("mod_silu", cond)
    w_mod_b = w_mod.astype(BF16)
    b_sh = lax.dynamic_slice(b_mod, (0, me * mod_cols), (2, mod_cols))
    m_part = jnp.stack([_mm(f"mod_fwd{l}", [(sil, w_mod_b[l])], "nn", F32, 16, 384, bias=b_sh[l:l + 1])
                        for l in range(2)], axis=1)
    m_all = _exchange("gather_mod", m_part.reshape(-1, 128), False).reshape(N_DEV, 2 * N_DEV, 2, mod_cols)
    m_mine = lax.dynamic_index_in_dim(m_all, me, axis=1, keepdims=False)
    mod_h = m_mine.transpose(1, 0, 2).reshape(2, N_MOD, d)
    mod_g = m_all[:, N_DEV, 0, :].reshape(N_MOD, d)

    def ffn_shards(i):
        return {f"gate_t{i}": ffn_w_gate[i // 2, i % 2].T, f"up_t{i}": ffn_w_up[i // 2, i % 2].T,
                f"down{i}": ffn_w_down[i // 2, i % 2]}

    local = {**ffn_shards(0), "in_t": ab_w_in[0].T, "uq": w_uq[0], "ukv_t": w_ukv[0].T, "ab_out": ab_w_out[0],
             **ffn_shards(1), **ffn_shards(2), "cin_t": conv_w_in[0].T, "c_out": conv_w_out[0], **ffn_shards(3)}
    feed = _Feed({nm: a.astype(BF16) for nm, a in local.items()}, m_all)

    sq_cols, ds0, g = _local_step(x[0], ctx[0], loss_target[0], mod_h, mod_g, norm_g_full, feed, pool_w[0],
                                  pool_scale, q_norm_g, kv_norm_g, conv_w_full, final_norm_g)
    grad_x = ds0[:n_lat]
    loss = lax.psum(0.5 * jnp.sum(sq_cols) / d, ("x", "y", "c"))
    red = feed.finish(ds0)

    dm = jnp.stack([g["mod_h"], jnp.stack([g["mod_g"], jnp.zeros_like(g["mod_g"])])])
    dm_all = _exchange("gather_dmod", dm.reshape(-1, 128), False).reshape(N_DEV, 2, 2, N_MOD * d)
    grad_b_mod = _sum_rows("dmod_bias", dm_all.reshape(2 * N_DEV, 2 * N_MOD * d)).reshape(2, N_MOD * d)
    dm_sh = lax.dynamic_slice(dm_all, (0, 0, 0, me * mod_cols), (N_DEV, 2, 2, mod_cols))
    gw_mod, cctx_parts = [], []
    for l in range(2):
        dm_l = dm_sh[:, :, l, :].transpose(1, 0, 2).reshape(2 * N_DEV, mod_cols).astype(BF16)
        gw_mod.append(_mm(f"mod_dw{l}", [(sil, dm_l)], "tn", F32, 512, 384))
        dm_ctx = jnp.concatenate([dm_l[N_DEV:], jnp.zeros((N_DEV, mod_cols), BF16)], axis=0)
        cctx_parts.append(_mm(f"mod_dcond{l}", [(dm_ctx, w_mod_b[l])], "nt", F32, 16, 512))
    grad_w_mod = jnp.stack(gw_mod)
    cctx_part = _sum_rows("mod_dcond_sum", jnp.concatenate(cctx_parts, axis=0))

    small_g = jnp.concatenate([g["pool_w"].reshape(-1), g["pool_scale"].reshape(-1), g["q_norm_g"].reshape(-1),
                               g["kv_norm_g"].reshape(-1), g["final_norm_g"].reshape(-1), g["norm_g"].reshape(-1),
                               g["conv_w"].reshape(-1), cctx_part.reshape(-1)])
    sizes = [pool_w.size, pool_scale.size, q_norm_g.size, kv_norm_g.size, d, 6 * d, 3 * d, d]
    sg_n = -(-small_g.shape[0] // 1024) * 1024
    small_g = jnp.pad(small_g, (0, sg_n - small_g.shape[0]))
    sg_all = _exchange("gather_small_grads", small_g.reshape(-1, 128), False).reshape(N_DEV, sg_n)
    scale_vec = jnp.concatenate([jnp.ones((1, sum(sizes[:-1])), F32), dsil[N_DEV:N_DEV + 1],
                                 jnp.ones((1, sg_n - sum(sizes)), F32)], axis=1)
    sg = _sum_rows("small_grads_sum", sg_all, scale_vec)[0]
    cuts, pos = [], 0
    for sz in sizes:
        cuts.append(sg[pos:pos + sz])
        pos += sz
    g_pool_w, g_pool_scale, g_q_norm, g_kv_norm, g_final, g_norm_full, g_conv_full, g_c_ctx = cuts
    grad_norm_g = lax.dynamic_slice(g_norm_full.reshape(2, 3, d), (0, 0, me * ng_sh), (2, 3, ng_sh))
    grad_conv_w = lax.dynamic_slice(g_conv_full.reshape(3, d), (0, me * cw_sh), (3, cw_sh))[None]

    def mine(nm):
        return red[nm]

    grad_gate = jnp.stack([mine(f"gate_t{i}").T for i in range(4)]).reshape(ffn_w_gate.shape)
    grad_up = jnp.stack([mine(f"up_t{i}").T for i in range(4)]).reshape(ffn_w_up.shape)
    grad_down = jnp.stack([mine(f"down{i}") for i in range(4)]).reshape(ffn_w_down.shape)
    grads = (g_c_ctx, grad_norm_g, grad_w_mod, grad_b_mod, grad_gate, grad_up, grad_down, mine("in_t").T[None],
             g_pool_w.reshape(pool_w.shape), g_pool_scale.reshape(pool_scale.shape), g_q_norm.reshape(q_norm_g.shape),
             mine("uq")[None], g_kv_norm.reshape(kv_norm_g.shape), mine("ukv_t").T[None], mine("ab_out")[None],
             mine("cin_t").T[None], grad_conv_w, mine("c_out")[None], g_final)
    grads = tuple(gr.reshape(w.shape) for gr, w in zip(grads, weights))
    deltas, new_m, new_v = _adam_all(WEIGHT_NAMES, weights, grads, moms, vels)
    return (loss, grad_x[None], *grads, *deltas, *new_m, *new_v)
```

```python
import functools
import math

import jax
import jax.numpy as jnp
import numpy as np
from jax import lax
from jax.experimental import pallas as pl
from jax.experimental.pallas import tpu as pltpu

F32 = jnp.float32
BF16 = jnp.bfloat16
MESH = pl.DeviceIdType.MESH
SDS = jax.ShapeDtypeStruct

N_DEV = 8
D_MODEL = 1024
N_MOD = 9
D_FF = 2816
POOL_WINDOWS = (2, 4, 8, 16)
POOL_DIM = 512
POOL_GROUP_DIM = 128
HEADS = 8
QK_NOPE = 64
QK_ROPE = 32
QK_HEAD = QK_NOPE + QK_ROPE
V_HEAD = 64
Q_RANK = 768
KV_RANK = 256
GRID_W = 64
ROPE_THETA = 10000.0
RMS_EPS = 1e-6
ATTN_SCALE = 1.0 / math.sqrt(QK_HEAD)
HEAD_PAD = 128
POOL_PAD = 16
PA_POOL, PA_CQ, PA_KV = 0, 768, 1536
PA_KV_W = 384
PA_W = PA_KV + PA_KV_W

ADAM_LR, ADAM_B1, ADAM_B2, ADAM_EPS, ADAM_WD, ADAM_STEP = 0.001, 0.9, 0.999, 1e-08, 0.01, 10

VMEM_LIMIT_BYTES = 56 * 1024 * 1024

NN = ((1,), (0,))
NT = ((1,), (1,))
TN = ((0,), (0,))


def _cparams():
    return pltpu.CompilerParams(vmem_limit_bytes=VMEM_LIMIT_BYTES)


def _dot(a, b, dims):
    return lax.dot_general(a, b, (dims, ((), ())), preferred_element_type=F32)


def _tile(n, cap, mult=8):
    t = (min(cap, n) // mult) * mult
    while t >= mult:
        if n % t == 0:
            return t
        t -= mult
    return n


def _colsum(x):
    return jnp.sum(x, axis=0, keepdims=True)


def _rms(x):
    r = lax.rsqrt(jnp.mean(x * x, axis=-1, keepdims=True) + RMS_EPS)
    return x * r, r


def _rms_bwd(n, r, dn):
    return r * (dn - n * jnp.mean(dn * n, axis=-1, keepdims=True))


def _rowwise(name, fn, t_rows, tm, n_lat, rows, vecs, outs, accs):
    nt = t_rows // tm
    nlt = n_lat // tm
    n_groups = 2 if nlt < nt else 1

    def grp(i):
        return jnp.where(i >= nlt, 1, 0) if n_groups == 2 else 0

    in_specs = [pl.BlockSpec((tm, w), functools.partial(lambda i, cb: (i, cb), cb=cb)) for (_, w, cb) in rows]
    in_specs += [pl.BlockSpec((1,) + v.shape[1:], lambda i: (grp(i), 0, 0)) for v in vecs]
    out_specs = [pl.BlockSpec((tm, w), lambda i: (i, 0)) for (w, _) in outs]
    out_specs += [pl.BlockSpec((1, 1, w), lambda i: (grp(i), 0, 0)) for w in accs]
    out_shape = [SDS((t_rows, w), dt) for (w, dt) in outs] + [SDS((n_groups, 1, w), F32) for w in accs]
    n_r, n_v, n_o = len(rows), len(vecs), len(outs)

    def body(*refs):
        row_vals = [r[...] for r in refs[:n_r]]
        vec_vals = [v[0] for v in refs[n_r:n_r + n_v]]
        out_refs = refs[n_r + n_v:n_r + n_v + n_o]
        acc_refs = refs[n_r + n_v + n_o:]
        out_vals, acc_vals = fn(row_vals, vec_vals)
        for o_ref, o in zip(out_refs, out_vals):
            o_ref[...] = o.astype(o_ref.dtype)
        if acc_refs:
            i = pl.program_id(0)
            first = (i == 0) | (i == nlt) if n_groups == 2 else i == 0

            @pl.when(first)
            def _():
                for a_ref, a in zip(acc_refs, acc_vals):
                    a_ref[0] = a

            @pl.when(jnp.logical_not(first))
            def _():
                for a_ref, a in zip(acc_refs, acc_vals):
                    a_ref[0] += a

    res = pl.pallas_call(
        body, name=name, grid=(nt,), in_specs=in_specs, out_specs=out_specs, out_shape=out_shape,
        compiler_params=_cparams(),
    )(*[r[0] for r in rows], *vecs)
    return res[:n_o], res[n_o:]


def _mm(name, pairs, mode, out_dtype, tm_cap=512, tn_cap=512, weights_outer=False, bias=None):
    a0, b0 = pairs[0]
    if mode == "nn":
        m, n, dims = a0.shape[0], b0.shape[1], NN
    elif mode == "nt":
        m, n, dims = a0.shape[0], b0.shape[0], NT
    else:
        m, n, dims = a0.shape[1], b0.shape[1], TN
    tm, tn = _tile(m, tm_cap, 128 if mode == "tn" else 16), _tile(n, tn_cap, 128)
    if weights_outer:
        grid = (n // tn, m // tm)
        ij = lambda g0, g1: (g1, g0)
    else:
        grid = (m // tm, n // tn)
        ij = lambda g0, g1: (g0, g1)

    def a_spec(a):
        if mode == "tn":
            return pl.BlockSpec((a.shape[0], tm), lambda g0, g1: (0, ij(g0, g1)[0]))
        return pl.BlockSpec((tm, a.shape[1]), lambda g0, g1: (ij(g0, g1)[0], 0))

    def b_spec(b):
        if mode == "nt":
            return pl.BlockSpec((tn, b.shape[1]), lambda g0, g1: (ij(g0, g1)[1], 0))
        return pl.BlockSpec((b.shape[0], tn), lambda g0, g1: (0, ij(g0, g1)[1]))

    in_specs, flat = [], []
    for a, b in pairs:
        in_specs += [a_spec(a), b_spec(b)]
        flat += [a, b]
    if bias is not None:
        in_specs.append(pl.BlockSpec((1, tn), lambda g0, g1: (0, ij(g0, g1)[1])))
        flat.append(bias)
    n_pairs = len(pairs)

    def body(*refs):
        acc = None
        for p in range(n_pairs):
            t = _dot(refs[2 * p][...], refs[2 * p + 1][...], dims)
            acc = t if acc is None else acc + t
        if bias is not None:
            acc = acc + refs[2 * n_pairs][...]
        refs[-1][...] = acc.astype(refs[-1].dtype)

    return pl.pallas_call(
        body, name=name, grid=grid, in_specs=in_specs,
        out_specs=pl.BlockSpec((tm, tn), lambda g0, g1: ij(g0, g1)),
        out_shape=SDS((m, n), out_dtype), compiler_params=_cparams(),
    )(*flat)


def _mm_resid(name, a, b, s, mg, k, coef, n_lat, tn_cap=512):
    t_rows, n = a.shape[0], b.shape[1]
    tm = _tile(math.gcd(n_lat, t_rows), 256, 16)
    tn = _tile(n, tn_cap, 128)
    nlt = n_lat // tm
    n_groups = 2 if nlt < t_rows // tm else 1

    def grp(i):
        return jnp.where(i >= nlt, 1, 0) if n_groups == 2 else 0

    def body(a_ref, b_ref, s_ref, mg_ref, so_ref, o_ref):
        o = _dot(a_ref[...], b_ref[...], NN)
        gate = mg_ref[0, 3 * k + 2:3 * k + 3, :]
        o_ref[...] = o
        so_ref[...] = s_ref[...] + (coef * gate) * o

    return pl.pallas_call(
        body, name=name, grid=(n // tn, t_rows // tm),
        in_specs=[pl.BlockSpec((tm, a.shape[1]), lambda j, i: (i, 0)),
                  pl.BlockSpec((b.shape[0], tn), lambda j, i: (0, j)),
                  pl.BlockSpec((tm, tn), lambda j, i: (i, j)),
                  pl.BlockSpec((1, mg.shape[1], tn), lambda j, i: (grp(i), 0, j))],
        out_specs=[pl.BlockSpec((tm, tn), lambda j, i: (i, j)), pl.BlockSpec((tm, tn), lambda j, i: (i, j))],
        out_shape=[SDS((t_rows, n), F32), SDS((t_rows, n), F32)], compiler_params=_cparams(),
    )(a, b, s, mg)


def _ffn_up(name, u, wg_t, wu_t):
    t_rows, f = u.shape[0], wg_t.shape[0]
    tm, tn = _tile(t_rows, 1024, 16), _tile(f, 256, 128)

    def body(u_ref, wg_ref, wu_ref, a_ref, b_ref, h_ref):
        uu = u_ref[...]
        a = _dot(uu, wg_ref[...], NT)
        b = _dot(uu, wu_ref[...], NT)
        a_ref[...] = a.astype(BF16)
        b_ref[...] = b.astype(BF16)
        h_ref[...] = (a * jax.nn.sigmoid(a) * b).astype(BF16)

    w_spec = pl.BlockSpec((tn, u.shape[1]), lambda i, j: (j, 0))
    o_spec = pl.BlockSpec((tm, tn), lambda i, j: (i, j))
    return pl.pallas_call(
        body, name=name, grid=(t_rows // tm, f // tn),
        in_specs=[pl.BlockSpec((tm, u.shape[1]), lambda i, j: (i, 0)), w_spec, w_spec],
        out_specs=[o_spec, o_spec, o_spec], out_shape=[SDS((t_rows, f), BF16)] * 3, compiler_params=_cparams(),
    )(u, wg_t, wu_t)


def _ffn_dact(name, do, wd, a, b):
    t_rows, f = do.shape[0], wd.shape[0]
    tm, tn = _tile(t_rows, 1024, 16), _tile(f, 256, 128)

    def body(do_ref, wd_ref, a_ref, b_ref, da_ref, db_ref):
        dh = _dot(do_ref[...], wd_ref[...], NT)
        av = a_ref[...].astype(F32)
        bv = b_ref[...].astype(F32)
        sg = jax.nn.sigmoid(av)
        da_ref[...] = (dh * bv * (sg * (1.0 + av * (1.0 - sg)))).astype(BF16)
        db_ref[...] = (dh * (av * sg)).astype(BF16)

    t_spec = pl.BlockSpec((tm, tn), lambda i, j: (i, j))
    return pl.pallas_call(
        body, name=name, grid=(t_rows // tm, f // tn),
        in_specs=[pl.BlockSpec((tm, do.shape[1]), lambda i, j: (i, 0)),
                  pl.BlockSpec((tn, wd.shape[1]), lambda i, j: (j, 0)), t_spec, t_spec],
        out_specs=[t_spec, t_spec], out_shape=[SDS((t_rows, f), BF16)] * 2, compiler_params=_cparams(),
    )(do, wd, a, b)


def _row_tm(t_rows, n_lat):
    return _tile(math.gcd(t_rows, n_lat), 256, 16)


def _adaln_fwd(name, s, mg, k, n_lat):
    t_rows = s.shape[0]

    def fn(rv, vv):
        m = vv[0]
        n, _ = _rms(rv[0])
        u = (n * m[9 + k:10 + k]) * (1.0 + m[3 * k + 1:3 * k + 2]) + m[3 * k:3 * k + 1]
        return [u], []

    (u,), _ = _rowwise(name, fn, t_rows, _row_tm(t_rows, n_lat), n_lat, [(s, D_MODEL, 0)], [mg], [(D_MODEL, BF16)], [])
    return u


def _adaln_bwd(name, s, du, ds_out, mg, k, n_lat):
    t_rows = s.shape[0]

    def fn(rv, vv):
        m = vv[0]
        gain, scale = m[9 + k:10 + k], m[3 * k + 1:3 * k + 2]
        n, r = _rms(rv[0])
        d_u = rv[1]
        dxn = d_u * (1.0 + scale)
        ds = _rms_bwd(n, r, dxn * gain)
        return [rv[2] + ds], [_colsum(d_u), _colsum(d_u * (n * gain)), _colsum(dxn * n)]

    (ds_in,), accs = _rowwise(name, fn, t_rows, _row_tm(t_rows, n_lat), n_lat,
                              [(s, D_MODEL, 0), (du, D_MODEL, 0), (ds_out, D_MODEL, 0)], [mg],
                              [(D_MODEL, F32)], [D_MODEL] * 3)
    return ds_in, accs


def _gate_bwd(name, ds_out, o, mg, k, coef, n_lat):
    t_rows = o.shape[0]

    def fn(rv, vv):
        gate = vv[0][3 * k + 2:3 * k + 3]
        d = coef * rv[0]
        return [d * gate], [_colsum(d * rv[1])]

    (do,), (dgate,) = _rowwise(name, fn, t_rows, _row_tm(t_rows, n_lat), n_lat,
                               [(ds_out, D_MODEL, 0), (o, D_MODEL, 0)], [mg], [(D_MODEL, BF16)], [D_MODEL])
    return do, dgate


def _rmsnorm_fwd(name, x, width, colblk, gain, t_rows):
    def fn(rv, vv):
        n, _ = _rms(rv[0])
        return [n * vv[0]], []

    (y,), _ = _rowwise(name, fn, t_rows, _tile(t_rows, 256, 16), t_rows, [(x, width, colblk)],
                       [gain.reshape(1, 1, width)], [(width, BF16)], [])
    return y


def _rmsnorm_bwd(name, x, width, colblk, dy, gain, t_rows):
    def fn(rv, vv):
        n, r = _rms(rv[0])
        return [_rms_bwd(n, r, rv[1] * vv[0])], [_colsum(rv[1] * n)]

    (dx,), (dgain,) = _rowwise(name, fn, t_rows, _tile(t_rows, 256, 16), t_rows,
                               [(x, width, colblk), (dy, width, 0)], [gain.reshape(1, 1, width)],
                               [(width, F32)], [width])
    return dx, dgain


def _final_loss(name, h, target, gain):
    t_rows = h.shape[0]
    inv_d = 1.0 / D_MODEL

    def fn(rv, vv):
        g = vv[0]
        n, r = _rms(rv[0])
        e = n * g - rv[1]
        dy = e * inv_d
        return [_rms_bwd(n, r, dy * g)], [_colsum(e * e), _colsum(dy * n)]

    (dh,), (sq, dgain) = _rowwise(name, fn, t_rows, _tile(t_rows, 256, 16), t_rows,
                                  [(h, D_MODEL, 0), (target, D_MODEL, 0)], [gain.reshape(1, 1, D_MODEL)],
                                  [(D_MODEL, F32)], [D_MODEL, D_MODEL])
    return dh, sq, dgain


def _rope(name, z, width, colblk, cos, sin, perm, backward, out_dtype):
    t_rows = cos.shape[0]

    def body(z_ref, c_ref, s_ref, p_ref, o_ref):
        zz = z_ref[...]
        pre = zz * s_ref[...] if backward else zz
        hi = pre.astype(BF16)
        lo = (pre - hi.astype(F32)).astype(BF16)
        rot = _dot(hi, p_ref[...], NN) + _dot(lo, p_ref[...], NN)
        if not backward:
            rot = rot * s_ref[...]
        o_ref[...] = (zz * c_ref[...] + rot).astype(o_ref.dtype)

    tm = _tile(t_rows, 256, 16)
    t_spec = pl.BlockSpec((tm, width), lambda i: (i, 0))
    return pl.pallas_call(
        body, name=name, grid=(t_rows // tm,),
        in_specs=[pl.BlockSpec((tm, width), lambda i: (i, colblk)), t_spec, t_spec,
                  pl.BlockSpec((width, width), lambda i: (0, 0))],
        out_specs=t_spec, out_shape=SDS((t_rows, width), out_dtype), compiler_params=_cparams(),
    )(z, cos, sin, perm)


def _window_sum(x, w, transposed):
    n_rows = x.shape[0]
    zeros = jnp.zeros((POOL_PAD, x.shape[1]), F32)
    y = jnp.concatenate([zeros, x, zeros], axis=0)
    total = n_rows + 2 * POOL_PAD
    if transposed:
        y = y + pltpu.roll(y, total - 1, 0)
    else:
        y = y + pltpu.roll(y, 1, 0)
    step = 1
    while 2 * step < w:
        y = pltpu.roll(y, step, 0) + pltpu.roll(y, total - step, 0)
        step *= 2
    return y[POOL_PAD:POOL_PAD + n_rows]


def _window_count(n_rows, w):
    t = lax.broadcasted_iota(jnp.int32, (n_rows, 1), 0)
    lo = jnp.maximum(t - w // 2, 0)
    hi = jnp.minimum(t + (w - w // 2 - 1), n_rows - 1)
    return (hi - lo + 1).astype(F32)


def _pool_fwd(name, proj, n_rows, w_grp, scale):
    def body(x_ref, w_ref, sc_ref, y_ref, p_ref):
        for g, w in enumerate(POOL_WINDOWS):
            cols = slice(g * POOL_GROUP_DIM, (g + 1) * POOL_GROUP_DIM)
            x = x_ref[:, cols]
            p = _window_sum(x, w, False) * (1.0 / _window_count(n_rows, w)) - x
            pb = p.astype(BF16)
            p_ref[:, cols] = pb
            y_ref[:, cols] = (_dot(pb, w_ref[g], NN) * sc_ref[:, cols]).astype(BF16)

    blk = pl.BlockSpec((n_rows, POOL_DIM), lambda i: (0, 0))
    return pl.pallas_call(
        body, name=name, grid=(1,),
        in_specs=[blk, pl.BlockSpec(w_grp.shape, lambda i: (0, 0, 0)), pl.BlockSpec((1, POOL_DIM), lambda i: (0, 0))],
        out_specs=[blk, blk], out_shape=[SDS((n_rows, POOL_DIM), BF16)] * 2, compiler_params=_cparams(),
    )(proj, w_grp, scale)


def _pool_bwd(name, dcat, n_rows, p, w_grp, scale):
    def body(dy_ref, p_ref, w_ref, sc_ref, dx_ref, dw_ref, dsc_ref):
        for g, w in enumerate(POOL_WINDOWS):
            cols = slice(g * POOL_GROUP_DIM, (g + 1) * POOL_GROUP_DIM)
            dy = dy_ref[:, cols]
            pb = p_ref[:, cols]
            pw = _dot(pb, w_ref[g], NN)
            dsc_ref[:, cols] = _colsum(dy * pw)
            dpw = (dy * sc_ref[:, cols]).astype(BF16)
            dw_ref[g] = _dot(pb, dpw, TN)
            dp = _dot(dpw, w_ref[g], NT)
            dx_ref[:, cols] = _window_sum(dp * (1.0 / _window_count(n_rows, w)), w, True) - dp

    blk = pl.BlockSpec((n_rows, POOL_DIM), lambda i: (0, 0))
    w_spec = pl.BlockSpec(w_grp.shape, lambda i: (0, 0, 0))
    v_spec = pl.BlockSpec((1, POOL_DIM), lambda i: (0, 0))
    return pl.pallas_call(
        body, name=name, grid=(1,), in_specs=[blk, blk, w_spec, v_spec], out_specs=[blk, w_spec, v_spec],
        out_shape=[SDS((n_rows, POOL_DIM), F32), SDS(w_grp.shape, F32), SDS((1, POOL_DIM), F32)],
        compiler_params=_cparams(),
    )(dcat, p, w_grp, scale)


def _attn_fwd(name, q, k, v):
    h, n_q, _ = q.shape
    n_k = k.shape[1]
    tq = _tile(n_q, 256, 16)

    def body(q_ref, k_ref, v_ref, o_ref, lse_ref):
        s = _dot(q_ref[...], k_ref[...], NT) * ATTN_SCALE
        m = jnp.max(s, axis=-1, keepdims=True)
        e = jnp.exp(s - m)
        l = jnp.sum(e, axis=-1, keepdims=True)
        p = (e * (1.0 / l)).astype(BF16)
        o_ref[...] = _dot(p, v_ref[...], NN).astype(BF16)
        lse_ref[...] = m + jnp.log(l)

    return pl.pallas_call(
        body, name=name, grid=(h, n_q // tq),
        in_specs=[pl.BlockSpec((None, tq, HEAD_PAD), lambda hh, i: (hh, i, 0)),
                  pl.BlockSpec((None, n_k, HEAD_PAD), lambda hh, i: (hh, 0, 0)),
                  pl.BlockSpec((None, n_k, V_HEAD), lambda hh, i: (hh, 0, 0))],
        out_specs=[pl.BlockSpec((None, tq, V_HEAD), lambda hh, i: (hh, i, 0)),
                   pl.BlockSpec((None, tq, 1), lambda hh, i: (hh, i, 0))],
        out_shape=[SDS((h, n_q, V_HEAD), BF16), SDS((h, n_q, 1), F32)], compiler_params=_cparams(),
    )(q, k, v)


def _attn_bwd(name, q, k, v, o, lse, do):
    h, n_q, _ = q.shape
    n_k = k.shape[1]
    tq = _tile(n_q, 256, 16)

    def body(q_ref, k_ref, v_ref, o_ref, lse_ref, do_ref, dq_ref, dk_ref, dv_ref, dks_ref):
        hh, i = pl.program_id(0), pl.program_id(1)
        qq, kk, dd = q_ref[...], k_ref[...], do_ref[...]
        s = _dot(qq, kk, NT) * ATTN_SCALE
        p = jnp.exp(s - lse_ref[...])
        dp = _dot(dd, v_ref[...], NT)
        delta = jnp.sum(dd.astype(F32) * o_ref[...].astype(F32), axis=-1, keepdims=True)
        ds = (p * (dp - delta) * ATTN_SCALE).astype(BF16)
        dq_ref[...] = _dot(ds, kk, NN)
        dk = _dot(ds, qq, TN)
        dv = _dot(p.astype(BF16), dd, TN)

        @pl.when(i == 0)
        def _():
            dk_ref[...] = dk
            dv_ref[...] = dv

        @pl.when(i > 0)
        def _():
            dk_ref[...] += dk
            dv_ref[...] += dv

        @pl.when((i == 0) & (hh == 0))
        def _():
            dks_ref[...] = dk

        @pl.when((i > 0) | (hh > 0))
        def _():
            dks_ref[...] += dk

    q_spec = pl.BlockSpec((None, tq, HEAD_PAD), lambda hh, i: (hh, i, 0))
    k_spec = pl.BlockSpec((None, n_k, HEAD_PAD), lambda hh, i: (hh, 0, 0))
    v_spec = pl.BlockSpec((None, n_k, V_HEAD), lambda hh, i: (hh, 0, 0))
    o_spec = pl.BlockSpec((None, tq, V_HEAD), lambda hh, i: (hh, i, 0))
    return pl.pallas_call(
        body, name=name, grid=(h, n_q // tq),
        in_specs=[q_spec, k_spec, v_spec, o_spec, pl.BlockSpec((None, tq, 1), lambda hh, i: (hh, i, 0)), o_spec],
        out_specs=[q_spec, k_spec, v_spec, pl.BlockSpec((n_k, HEAD_PAD), lambda hh, i: (0, 0))],
        out_shape=[SDS((h, n_q, HEAD_PAD), F32), SDS((h, n_k, HEAD_PAD), F32), SDS((h, n_k, V_HEAD), F32),
                   SDS((n_k, HEAD_PAD), F32)],
        compiler_params=_cparams(),
    )(q, k, v, o, lse, do)


CONV_COLS = 256


def _shift_rows(x, d):
    n_rows = x.shape[0]
    t = lax.broadcasted_iota(jnp.int32, (n_rows, 1), 0)
    if d > 0:
        return jnp.where(t >= d, pltpu.roll(x, d, 0), 0.0)
    return jnp.where(t < n_rows + d, pltpu.roll(x, n_rows + d, 0), 0.0)


def _conv_fwd(name, z3, conv_w):
    n_rows = z3.shape[0]
    nb = D_MODEL // CONV_COLS

    def body(b_ref, c_ref, v_ref, w_ref, y_ref):
        z = c_ref[...] * v_ref[...]
        zc = w_ref[0:1, :] * _shift_rows(z, 1) + w_ref[1:2, :] * z + w_ref[2:3, :] * _shift_rows(z, -1)
        y_ref[...] = (b_ref[...] * zc).astype(BF16)

    def part(k):
        return pl.BlockSpec((n_rows, CONV_COLS), lambda j: (0, k * nb + j))

    return pl.pallas_call(
        body, name=name, grid=(nb,),
        in_specs=[part(0), part(1), part(2), pl.BlockSpec((3, CONV_COLS), lambda j: (0, j))],
        out_specs=pl.BlockSpec((n_rows, CONV_COLS), lambda j: (0, j)),
        out_shape=SDS((n_rows, D_MODEL), BF16), compiler_params=_cparams(),
    )(z3, z3, z3, conv_w)


def _conv_bwd(name, dy, z3, conv_w):
    n_rows = z3.shape[0]
    nb = D_MODEL // CONV_COLS

    def body(dy_ref, b_ref, c_ref, v_ref, w_ref, db_ref, dc_ref, dv_ref, dw_ref):
        c, v, d_y = c_ref[...], v_ref[...], dy_ref[...]
        z = c * v
        z_dn, z_up = _shift_rows(z, 1), _shift_rows(z, -1)
        zc = w_ref[0:1, :] * z_dn + w_ref[1:2, :] * z + w_ref[2:3, :] * z_up
        db_ref[...] = (d_y * zc).astype(BF16)
        dzc = d_y * b_ref[...]
        dz = w_ref[0:1, :] * _shift_rows(dzc, -1) + w_ref[1:2, :] * dzc + w_ref[2:3, :] * _shift_rows(dzc, 1)
        dc_ref[...] = (dz * v).astype(BF16)
        dv_ref[...] = (dz * c).astype(BF16)
        dw_ref[0:1, :] = _colsum(dzc * z_dn)
        dw_ref[1:2, :] = _colsum(dzc * z)
        dw_ref[2:3, :] = _colsum(dzc * z_up)

    def part(k):
        return pl.BlockSpec((n_rows, CONV_COLS), lambda j: (0, k * nb + j))

    col = pl.BlockSpec((n_rows, CONV_COLS), lambda j: (0, j))
    w_spec = pl.BlockSpec((3, CONV_COLS), lambda j: (0, j))
    return pl.pallas_call(
        body, name=name, grid=(nb,), in_specs=[col, part(0), part(1), part(2), w_spec],
        out_specs=[col, col, col, w_spec],
        out_shape=[SDS((n_rows, D_MODEL), BF16)] * 3 + [SDS((3, D_MODEL), F32)], compiler_params=_cparams(),
    )(dy, z3, z3, z3, conv_w)


def _silu_rows(name, x):
    def body(x_ref, s_ref, d_ref):
        xx = x_ref[...]
        sg = jax.nn.sigmoid(xx)
        s_ref[...] = (xx * sg).astype(BF16)
        d_ref[...] = sg * (1.0 + xx * (1.0 - sg))

    return pl.pallas_call(body, name=name, out_shape=[SDS(x.shape, BF16), SDS(x.shape, F32)])(x)


def _sum_rows(name, x, scale=None):
    r, n = x.shape
    tn = _tile(n, 8192, 128)

    def body(*refs):
        acc = jnp.sum(refs[0][...].astype(F32), axis=0, keepdims=True)
        if scale is not None:
            acc = acc * refs[1][...]
        refs[-1][...] = acc

    in_specs = [pl.BlockSpec((r, tn), lambda j: (0, j))]
    args = [x]
    if scale is not None:
        in_specs.append(pl.BlockSpec((1, tn), lambda j: (0, j)))
        args.append(scale)
    return pl.pallas_call(body, name=name, grid=(n // tn,), in_specs=in_specs,
                          out_specs=pl.BlockSpec((1, tn), lambda j: (0, j)), out_shape=SDS((1, n), F32))(*args)


def _sum_slots(name, x):
    n_slots, r, c = x.shape
    tr = _tile(r, 432, 16)

    def body(x_ref, o_ref):
        acc = x_ref[0].astype(F32)
        for sl in range(1, n_slots):
            acc = acc + x_ref[sl].astype(F32)
        o_ref[...] = acc

    return pl.pallas_call(body, name=name, grid=(r // tr,),
                          in_specs=[pl.BlockSpec((n_slots, tr, c), lambda i: (0, i, 0))],
                          out_specs=pl.BlockSpec((tr, c), lambda i: (i, 0)), out_shape=SDS((r, c), F32),
                          compiler_params=_cparams())(x)


def _adamw(name, w, g, m, v):
    shape = w.shape
    cols = shape[-1]
    rows = w.size // cols
    tr = _tile(rows, 512, 8)
    bc1 = 1.0 - ADAM_B1 ** ADAM_STEP
    bc2 = 1.0 - ADAM_B2 ** ADAM_STEP

    def body(w_ref, g_ref, m_ref, v_ref, d_ref, nm_ref, nv_ref):
        gg = g_ref[...]
        nm = ADAM_B1 * m_ref[...] + (1.0 - ADAM_B1) * gg
        nv = ADAM_B2 * v_ref[...] + (1.0 - ADAM_B2) * (gg * gg)
        nm_ref[...] = nm
        nv_ref[...] = nv
        d_ref[...] = -ADAM_LR * ((nm / bc1) / (jnp.sqrt(nv / bc2) + ADAM_EPS) + ADAM_WD * w_ref[...])

    spec = pl.BlockSpec((tr, cols), lambda i: (i, 0))
    outs = pl.pallas_call(body, name=name, grid=(rows // tr,), in_specs=[spec] * 4, out_specs=[spec] * 3,
                          out_shape=[SDS((rows, cols), F32)] * 3, compiler_params=_cparams())(
        w.reshape(rows, cols), g.reshape(rows, cols), m.reshape(rows, cols), v.reshape(rows, cols))
    return tuple(t.reshape(shape) for t in outs)


def _exchange(name, x, scatter):
    blk = x.shape[1:] if scatter else x.shape

    def body(x_ref, out_ref, send_sems, recv_sems, local_sem):
        mx, my, mc = lax.axis_index("x"), lax.axis_index("y"), lax.axis_index("c")
        me = 4 * mx + 2 * my + mc
        own = pltpu.make_async_copy(x_ref.at[me] if scatter else x_ref, out_ref.at[me], local_sem)
        own.start()
        copies = []
        for kk in range(1, N_DEV):
            px = jnp.bitwise_xor(mx, (kk >> 2) & 1)
            py = jnp.bitwise_xor(my, (kk >> 1) & 1)
            pc = jnp.bitwise_xor(mc, kk & 1)
            peer = 4 * px + 2 * py + pc
            send = pltpu.make_async_remote_copy(
                src_ref=x_ref.at[peer] if scatter else x_ref, dst_ref=out_ref.at[me],
                send_sem=send_sems.at[kk - 1], recv_sem=recv_sems.at[kk - 1],
                device_id=(px, py, pc), device_id_type=MESH)
            send.start()
            arrival = pltpu.make_async_remote_copy(
                src_ref=x_ref.at[peer] if scatter else x_ref, dst_ref=out_ref.at[peer],
                send_sem=send_sems.at[kk - 1], recv_sem=recv_sems.at[kk - 1],
                device_id=(px, py, pc), device_id_type=MESH)
            copies.append((send, arrival))
        for send, arrival in copies:
            arrival.wait_recv()
            send.wait_send()
        own.wait()

    return pl.pallas_call(
        body, name=name, out_shape=SDS((N_DEV,) + tuple(blk), x.dtype),
        in_specs=[pl.BlockSpec(memory_space=pl.ANY)], out_specs=pl.BlockSpec(memory_space=pl.ANY),
        scratch_shapes=[pltpu.SemaphoreType.DMA((N_DEV - 1,)), pltpu.SemaphoreType.DMA((N_DEV - 1,)),
                        pltpu.SemaphoreType.DMA],
    )(x)


def _rope_perm(pre, reps, post):
    half = QK_ROPE // 4
    width = reps * (pre + QK_ROPE) + post
    p = np.zeros((width, width), np.float32)
    for rep in range(reps):
        s0 = rep * (pre + QK_ROPE) + pre
        for base in (s0, s0 + 2 * half):
            for i in range(half):
                p[base + half + i, base + i] = -1.0
                p[base + i, base + half + i] = 1.0
    return p


def _rope_tables(n_lat, t_rows, pre, reps, post):
    half = QK_ROPE // 4
    pos = jnp.arange(n_lat)
    freqs = jnp.power(ROPE_THETA, -jnp.arange(0, 2 * half, 2, dtype=F32) / (2 * half))
    ang_r = (pos // GRID_W).astype(F32)[:, None] * freqs
    ang_c = (pos % GRID_W).astype(F32)[:, None] * freqs
    ang = jnp.concatenate([ang_r, ang_r, ang_c, ang_c], axis=-1)

    def table(fn, plain):
        slot = jnp.concatenate([jnp.full((n_lat, pre), plain, F32), fn(ang)], axis=-1)
        t = jnp.concatenate([jnp.tile(slot, (1, reps)), jnp.full((n_lat, post), plain, F32)], axis=-1)
        return jnp.concatenate([t, jnp.full((t_rows - n_lat, t.shape[1]), plain, F32)], axis=0)

    return table(jnp.cos, 1.0), table(jnp.sin, 0.0)


def _ffn_half_fwd(tag, s, mg, k, feed, i, coef, n_lat):
    u = _adaln_fwd(f"{tag}_adaln", s, mg, k, n_lat)
    wg_t, wu_t = feed.weights(f"{tag}_up", [f"gate_t{i}", f"up_t{i}"], u)
    a, b, hid = _ffn_up(f"{tag}_up", u, wg_t, wu_t)
    (wd,) = feed.weights(f"{tag}_down", [f"down{i}"], hid)
    s_out, o = _mm_resid(f"{tag}_down", hid, wd, s, mg, k, coef, n_lat)
    return s_out, (s, u, a, b, hid, o, wg_t, wu_t, wd)


def _ffn_half_bwd(tag, ds_out, saved, mg, k, feed, i, coef, n_lat):
    s, u, a, b, hid, o, wg_t, wu_t, wd = saved
    do, dgate = _gate_bwd(f"{tag}_dgate", ds_out, o, mg, k, coef, n_lat)
    da, db = _ffn_dact(f"{tag}_dact", do, wd, a, b)
    dwd = _mm(f"{tag}_dwd", [(hid, do)], "tn", BF16, 512, 512)
    dwg_t = _mm(f"{tag}_dwg", [(da, u)], "tn", BF16, 512, 512)
    dwu_t = _mm(f"{tag}_dwu", [(db, u)], "tn", BF16, 512, 512)
    token = feed.grads(tag, {f"down{i}": dwd, f"gate_t{i}": dwg_t, f"up_t{i}": dwu_t})
    du = _mm(f"{tag}_du", [(da, wg_t), (db, wu_t)], "nn", F32, 384, 512, bias=_after(token))
    ds_in, (dshift, dscale, dgain) = _adaln_bwd(f"{tag}_dadaln", s, du, ds_out, mg, k, n_lat)
    return ds_in, dict(shift=dshift, scale=dscale, gate=dgate, gain=dgain)


def _after(token):
    return jnp.zeros((1, D_MODEL), F32) + token


def _mod_grad(parts, n_groups):
    rows = []
    zero = jnp.zeros((n_groups, 1, D_MODEL), F32)
    for k in range(3):
        for nm in ("shift", "scale", "gate"):
            t = parts[k].get(nm, zero)
            if t.shape[0] < n_groups:
                t = jnp.concatenate([t, jnp.zeros((n_groups - t.shape[0], 1, D_MODEL), F32)], axis=0)
            rows.append(t)
    return jnp.concatenate(rows, axis=1).reshape(n_groups, N_MOD * D_MODEL)


def _local_step(x, ctx, target, mod_h, mod_g, norm_g, feed, pool_w, pool_scale, q_norm_g, kv_norm_g, conv_w,
                final_norm_g):
    n_lat, n_ctx = x.shape[0], ctx.shape[0]
    t_all = n_lat + n_ctx
    mg0 = jnp.stack([jnp.concatenate([mod_h[0], norm_g[0]], axis=0), jnp.concatenate([mod_g, norm_g[0]], axis=0)])
    mg1 = jnp.concatenate([mod_h[1], norm_g[1]], axis=0)[None]

    s0 = jnp.concatenate([x, ctx], axis=0) + feed.start_token()
    s1, sv_f00 = _ffn_half_fwd("l0f0", s0, mg0, 0, feed, 0, 0.5, n_lat)

    ua = _adaln_fwd("l0m_adaln", s1, mg0, 1, n_lat)
    w_in, w_uq, w_ukv_t, w_ab_out = feed.weights("l0m", ["in_t", "uq", "ukv_t", "ab_out"], ua)
    kv_rows = KV_RANK + QK_ROPE
    w_in_t = jnp.concatenate([
        w_in[:POOL_DIM], jnp.zeros((PA_CQ - POOL_DIM, D_MODEL), BF16), w_in[POOL_DIM:POOL_DIM + Q_RANK],
        w_in[POOL_DIM + Q_RANK:], jnp.zeros((PA_KV_W - kv_rows, D_MODEL), BF16)], axis=0)
    proj = _mm("l0m_proj", [(ua, w_in_t)], "nt", F32, 768, 384)
    pool_y, pool_p = _pool_fwd("l0m_pool", proj, n_lat, pool_w.astype(BF16), pool_scale)
    nq = _rmsnorm_fwd("l0m_qnorm", proj, Q_RANK, PA_CQ // Q_RANK, q_norm_g, n_lat)
    q_lin = _mm("l0m_q", [(nq, w_uq)], "nn", F32, 512, 768)
    cos_q, sin_q = _rope_tables(n_lat, n_lat, QK_NOPE, HEADS, 0)
    perm_q = _rope_perm(QK_NOPE, HEADS, 0)
    q_rot = _rope("l0m_qrope", q_lin, Q_RANK, 0, cos_q, sin_q, jnp.asarray(perm_q, BF16), False, BF16)
    cos_k, sin_k = _rope_tables(n_lat, t_all, KV_RANK, 1, PA_KV_W - kv_rows)
    perm_k = _rope_perm(KV_RANK, 1, PA_KV_W - kv_rows)
    kvr = _rope("l0m_krope", proj, PA_KV_W, PA_KV // PA_KV_W, cos_k, sin_k, jnp.asarray(perm_k, BF16), False, F32)
    nkv = _rmsnorm_fwd("l0m_kvnorm", kvr, KV_RANK, 0, kv_norm_g, t_all)
    kv = _mm("l0m_kv", [(nkv, w_ukv_t)], "nt", BF16, 768, 512)
    qh = jnp.pad(q_rot.reshape(n_lat, HEADS, QK_HEAD), ((0, 0), (0, 0), (0, HEAD_PAD - QK_HEAD))).transpose(1, 0, 2)
    kvh = kv.reshape(t_all, HEADS, QK_NOPE + V_HEAD)
    k_rope = jnp.broadcast_to(kvr[:, None, KV_RANK:KV_RANK + QK_ROPE].astype(BF16), (t_all, HEADS, QK_ROPE))
    kh = jnp.concatenate([kvh[:, :, :QK_NOPE], k_rope, jnp.zeros((t_all, HEADS, HEAD_PAD - QK_HEAD), BF16)],
                         axis=-1).transpose(1, 0, 2)
    vh = kvh[:, :, QK_NOPE:].transpose(1, 0, 2)
    oh, lse = _attn_fwd("l0m_attn", qh, kh, vh)
    cat = jnp.concatenate([pool_y, oh.transpose(1, 0, 2).reshape(n_lat, HEADS * V_HEAD)], axis=-1)
    h1 = s1[:n_lat]
    h2, mix_o = _mm_resid("l0m_out", cat, w_ab_out, h1, mg0[:1], 1, 1.0, n_lat)

    h3, sv_f01 = _ffn_half_fwd("l0f1", h2, mg0[:1], 2, feed, 1, 0.5, n_lat)

    h4, sv_f10 = _ffn_half_fwd("l1f0", h3, mg1, 0, feed, 2, 0.5, n_lat)
    uc = _adaln_fwd("l1m_adaln", h4, mg1, 1, n_lat)
    w_cin_t, w_c_out = feed.weights("l1m", ["cin_t", "c_out"], uc)
    z3 = _mm("l1m_in", [(uc, w_cin_t)], "nt", F32, 512, 512)
    yc = _conv_fwd("l1m_conv", z3, conv_w)
    h5, conv_o = _mm_resid("l1m_out", yc, w_c_out, h4, mg1, 1, 1.0, n_lat)
    h6, sv_f11 = _ffn_half_fwd("l1f1", h5, mg1, 2, feed, 3, 0.5, n_lat)

    dh6, sq_cols, d_final_g = _final_loss("loss_head", h6, target, final_norm_g)
    g = {}
    dh5, g["f11"] = _ffn_half_bwd("l1f1", dh6, sv_f11, mg1, 2, feed, 3, 0.5, n_lat)

    do_c, dgate_c = _gate_bwd("l1m_dgate", dh5, conv_o, mg1, 1, 1.0, n_lat)
    dyc = _mm("l1m_dy", [(do_c, w_c_out)], "nt", F32, 512, 512)
    d_c_out = _mm("l1m_dwout", [(yc, do_c)], "tn", BF16, 512, 512)
    db_, dc_, dv_, d_conv_w = _conv_bwd("l1m_dconv", dyc, z3, conv_w)
    dz3 = jnp.concatenate([db_, dc_, dv_], axis=-1)
    d_cin_t = _mm("l1m_dwin", [(dz3, uc)], "tn", BF16, 512, 512)
    token = feed.grads("l1m", {"c_out": d_c_out, "cin_t": d_cin_t})
    duc = _mm("l1m_du", [(dz3, w_cin_t)], "nn", F32, 512, 512, bias=_after(token))
    dh4, (dsh_c, dsc_c, dgn_c) = _adaln_bwd("l1m_dadaln", h4, duc, dh5, mg1, 1, n_lat)
    dh3, g["f10"] = _ffn_half_bwd("l1f0", dh4, sv_f10, mg1, 0, feed, 2, 0.5, n_lat)

    dh2, g["f01"] = _ffn_half_bwd("l0f1", dh3, sv_f01, mg0[:1], 2, feed, 1, 0.5, n_lat)

    do_a, dgate_a = _gate_bwd("l0m_dgate", dh2, mix_o, mg0[:1], 1, 1.0, n_lat)
    dcat = _mm("l0m_dcat", [(do_a, w_ab_out)], "nt", F32, 512, 512)
    d_ab_out = _mm("l0m_dwout", [(cat, do_a)], "tn", BF16, 512, 512)
    d_pool_x, d_pool_w, d_pool_scale = _pool_bwd("l0m_dpool", dcat, n_lat, pool_p, pool_w.astype(BF16), pool_scale)
    doh = dcat[:, POOL_DIM:].reshape(n_lat, HEADS, V_HEAD).transpose(1, 0, 2).astype(BF16)
    dqh, dkh, dvh, dk_sum = _attn_bwd("l0m_dattn", qh, kh, vh, oh, lse, doh)
    dq_rot = dqh[:, :, :QK_HEAD].transpose(1, 0, 2).reshape(n_lat, Q_RANK)
    dq_lin = _rope("l0m_dqrope", dq_rot, Q_RANK, 0, cos_q, sin_q, jnp.asarray(perm_q.T, BF16), True, BF16)
    d_uq = _mm("l0m_dwuq", [(nq, dq_lin)], "tn", BF16, 768, 768)
    dnq = _mm("l0m_dnq", [(dq_lin, w_uq)], "nt", F32, 512, 768)
    dcq, d_q_norm_g = _rmsnorm_bwd("l0m_dqnorm", proj, Q_RANK, PA_CQ // Q_RANK, dnq, q_norm_g, n_lat)
    dkv = jnp.concatenate([dkh[:, :, :QK_NOPE], dvh], axis=-1).transpose(1, 0, 2).reshape(t_all, HEADS * HEAD_PAD)
    dkv = dkv.astype(BF16)
    dnkv = _mm("l0m_dnkv", [(dkv, w_ukv_t)], "nn", F32, 768, 256)
    d_ukv_t = _mm("l0m_dwukv", [(dkv, nkv)], "tn", BF16, 512, 256)
    dckv, d_kv_norm_g = _rmsnorm_bwd("l0m_dkvnorm", kvr, KV_RANK, 0, dnkv, kv_norm_g, t_all)
    dkvr = jnp.concatenate([dckv, dk_sum[:, QK_NOPE:QK_HEAD],
                            jnp.zeros((t_all, PA_KV_W - KV_RANK - QK_ROPE), F32)], axis=-1)
    dpb = _rope("l0m_dkrope", dkvr, PA_KV_W, 0, cos_k, sin_k, jnp.asarray(perm_k.T, BF16), True, F32)
    dproj_lat = jnp.concatenate([d_pool_x, jnp.zeros((n_lat, PA_CQ - POOL_DIM), F32), dcq, dpb[:n_lat]], axis=-1)
    dproj_ctx = jnp.concatenate([jnp.zeros((n_ctx, PA_KV), F32), dpb[n_lat:]], axis=-1)
    dproj = jnp.concatenate([dproj_lat, dproj_ctx], axis=0).astype(BF16)
    d_in_pad = _mm("l0m_dwin", [(dproj, ua)], "tn", BF16, 640, 512)
    d_in_t = jnp.concatenate([d_in_pad[:POOL_DIM], d_in_pad[PA_CQ:PA_CQ + Q_RANK],
                              d_in_pad[PA_KV:PA_KV + kv_rows]], axis=0)
    token = feed.grads("l0m", {"ab_out": d_ab_out, "uq": d_uq, "ukv_t": d_ukv_t, "in_t": d_in_t})
    dua = _mm("l0m_du", [(dproj, w_in_t)], "nn", F32, 768, 512, bias=_after(token))
    dh2_all = jnp.concatenate([dh2, jnp.zeros((n_ctx, D_MODEL), F32)], axis=0)
    ds1, (dsh_a, dsc_a, dgn_a) = _adaln_bwd("l0m_dadaln", s1, dua, dh2_all, mg0, 1, n_lat)
    ds0, g["f00"] = _ffn_half_bwd("l0f0", ds1, sv_f00, mg0, 0, feed, 0, 0.5, n_lat)

    dmod0 = _mod_grad([g["f00"], dict(shift=dsh_a, scale=dsc_a, gate=dgate_a), g["f01"]], 2)
    dmod1 = _mod_grad([g["f10"], dict(shift=dsh_c, scale=dsc_c, gate=dgate_c), g["f11"]], 1)
    d_norm_g = jnp.stack([
        jnp.concatenate([jnp.sum(g["f00"]["gain"], axis=0), jnp.sum(dgn_a, axis=0), g["f01"]["gain"][0]], axis=0),
        jnp.concatenate([g["f10"]["gain"][0], dgn_c[0], g["f11"]["gain"][0]], axis=0)])
    grads = dict(
        pool_w=d_pool_w, pool_scale=d_pool_scale, q_norm_g=d_q_norm_g[0], kv_norm_g=d_kv_norm_g[0],
        conv_w=d_conv_w, final_norm_g=d_final_g[0], norm_g=d_norm_g,
        mod_h=jnp.stack([dmod0[0], dmod1[0]]), mod_g=dmod0[1])
    return sq_cols, ds0, grads


HBM_SPEC = pl.BlockSpec(memory_space=pltpu.HBM)
SEM_SPEC = pl.BlockSpec(memory_space=pltpu.SEMAPHORE)
ANY_SPEC = pl.BlockSpec(memory_space=pl.ANY)
SIDE_EFFECT = pltpu.SideEffectType.DATAFLOW_SIDE_EFFECTING
N_PEERS = N_DEV - 1


def _mesh_place():
    mx, my, mc = lax.axis_index("x"), lax.axis_index("y"), lax.axis_index("c")
    return mx, my, mc, 4 * mx + 2 * my + mc


def _peer(place, kk):
    mx, my, mc, _ = place
    px = jnp.bitwise_xor(mx, (kk >> 2) & 1)
    py = jnp.bitwise_xor(my, (kk >> 1) & 1)
    pc = jnp.bitwise_xor(mc, kk & 1)
    return (px, py, pc), 4 * px + 2 * py + pc


def _hbm(a):
    return pltpu.with_memory_space_constraint(a, pltpu.HBM)


def _landing(block, me):
    zone = lax.empty((N_DEV,) + block.shape, block.dtype)
    return lax.dynamic_update_slice(zone, block[None], (me,) + (0,) * block.ndim)


def _exchange_start(name, srcs, lands, scatter, after):
    n = len(srcs)

    def body(*refs):
        src, land = refs[:n], refs[n:2 * n]
        send_sems, recv_sems, token = refs[2 * n + 1], refs[2 * n + 2], refs[-1]
        place = _mesh_place()
        for a in range(n):
            for kk in range(1, N_DEV):
                dev, peer = _peer(place, kk)
                pltpu.make_async_remote_copy(
                    src_ref=src[a].at[peer] if scatter else src[a], dst_ref=land[a].at[place[3]],
                    send_sem=send_sems.at[a * N_PEERS + kk - 1], recv_sem=recv_sems.at[a * N_PEERS + kk - 1],
                    device_id=dev, device_id_type=MESH).start()
        token[...] = jnp.zeros_like(token)

    thru = [pltpu.HBM(t.shape, t.dtype) for t in (*srcs, *lands)]
    res = pl.pallas_call(
        body, name=name,
        out_shape=(pltpu.SemaphoreType.DMA((n * N_PEERS,)), pltpu.SemaphoreType.DMA((n * N_PEERS,)), *thru,
                   SDS((8, 128), F32)),
        in_specs=[HBM_SPEC] * (2 * n) + [ANY_SPEC],
        out_specs=(SEM_SPEC, SEM_SPEC, *([HBM_SPEC] * (2 * n)), pl.BlockSpec(memory_space=pltpu.VMEM)),
        input_output_aliases={i: 2 + i for i in range(2 * n)},
        compiler_params=pltpu.CompilerParams(has_side_effects=SIDE_EFFECT),
    )(*[_hbm(s) for s in srcs], *[_hbm(t) for t in lands], after)
    return res[0], res[1], list(res[2:2 + n]), list(res[2 + n:2 + 2 * n]), res[-1]


def _exchange_wait(name, send_sems, recv_sems, srcs, lands, places, scatter, after):
    n = len(srcs)

    def body(*refs):
        src, land = refs[:n], refs[n:2 * n]
        send, recv = refs[2 * n], refs[2 * n + 1]
        place = _mesh_place()
        for a in range(n):
            for kk in range(1, N_DEV):
                dev, peer = _peer(place, kk)
                cp = pltpu.make_async_remote_copy(
                    src_ref=src[a].at[peer] if scatter else src[a], dst_ref=land[a].at[peer],
                    send_sem=send.at[places[a] * N_PEERS + kk - 1], recv_sem=recv.at[places[a] * N_PEERS + kk - 1],
                    device_id=dev, device_id_type=MESH)
                cp.wait_send()
                cp.wait_recv()

    thru = [pltpu.HBM(t.shape, t.dtype) for t in (*srcs, *lands)]
    res = pl.pallas_call(
        body, name=name, out_shape=tuple(thru),
        in_specs=[HBM_SPEC] * (2 * n) + [SEM_SPEC, SEM_SPEC, ANY_SPEC], out_specs=tuple([HBM_SPEC] * (2 * n)),
        input_output_aliases={i: i for i in range(2 * n)},
        compiler_params=pltpu.CompilerParams(has_side_effects=SIDE_EFFECT),
    )(*srcs, *lands, send_sems, recv_sems, after)
    return list(res[n:])


class _Feed:
    def __init__(self, shards, me, after):
        self.names, self.me = list(shards), me
        srcs = [shards[nm] for nm in self.names]
        lands = [_landing(s, me) for s in srcs]
        self.send, self.recv, self.srcs, self.lands, self.token = _exchange_start(
            "gather_start", srcs, lands, False, after)
        self.pending = []

    def start_token(self):
        return self.token[0, 0]

    def weights(self, tag, names, after):
        places = [self.names.index(nm) for nm in names]
        got = _exchange_wait(f"gather_wait_{tag}", self.send, self.recv, [self.srcs[i] for i in places],
                             [self.lands[i] for i in places], places, False, after)
        return [t.reshape((N_DEV * t.shape[1],) + t.shape[2:]) for t in got]

    def grads(self, tag, full):
        names = list(full)
        srcs = [full[nm].reshape((N_DEV, full[nm].shape[0] // N_DEV) + full[nm].shape[1:]) for nm in names]
        lands = [_landing(lax.dynamic_index_in_dim(s, self.me, 0, keepdims=False), self.me) for s in srcs]
        send, recv, srcs, lands, token = _exchange_start(f"scatter_start_{tag}", srcs, lands, True, srcs[0])
        self.pending.append((tag, names, send, recv, srcs, lands))
        return token[0, 0]

    def finish(self, after):
        out = {}
        for tag, names, send, recv, srcs, lands in self.pending:
            got = _exchange_wait(f"scatter_wait_{tag}", send, recv, srcs, lands, list(range(len(names))), True, after)
            for nm, slots in zip(names, got):
                out[nm] = _sum_slots(f"reduce_{nm}", slots)
        return out


def _adam_all(names, weights, grads, moms, vels):
    deltas, new_m, new_v = [], [], []
    for nm, w, g, m, v in zip(names, weights, grads, moms, vels):
        d, a, b = _adamw(f"adamw_{nm}", w, g.reshape(w.shape), m, v)
        deltas.append(d)
        new_m.append(a)
        new_v.append(b)
    return deltas, new_m, new_v


WEIGHT_NAMES = ("c_ctx", "norm_g", "w_mod", "b_mod", "ffn_w_gate", "ffn_w_up", "ffn_w_down", "ab_w_in", "pool_w",
                "pool_scale", "q_norm_g", "w_uq", "kv_norm_g", "w_ukv", "ab_w_out", "conv_w_in", "conv_w",
                "conv_w_out", "final_norm_g")


def kernel(x, c, ctx, c_ctx, norm_g, w_mod, b_mod, ffn_w_gate, ffn_w_up, ffn_w_down, ab_w_in, pool_w, pool_scale, q_norm_g, w_uq, kv_norm_g, w_ukv, ab_w_out, conv_w_in, conv_w, conv_w_out, final_norm_g, loss_target, m_c_ctx, m_norm_g, m_w_mod, m_b_mod, m_ffn_w_gate, m_ffn_w_up, m_ffn_w_down, m_ab_w_in, m_pool_w, m_pool_scale, m_q_norm_g, m_w_uq, m_kv_norm_g, m_w_ukv, m_ab_w_out, m_conv_w_in, m_conv_w, m_conv_w_out, m_final_norm_g, v_c_ctx, v_norm_g, v_w_mod, v_b_mod, v_ffn_w_gate, v_ffn_w_up, v_ffn_w_down, v_ab_w_in, v_pool_w, v_pool_scale, v_q_norm_g, v_w_uq, v_kv_norm_g, v_w_ukv, v_ab_w_out, v_conv_w_in, v_conv_w, v_conv_w_out, v_final_norm_g):
    weights = (c_ctx, norm_g, w_mod, b_mod, ffn_w_gate, ffn_w_up, ffn_w_down, ab_w_in, pool_w, pool_scale, q_norm_g,
               w_uq, kv_norm_g, w_ukv, ab_w_out, conv_w_in, conv_w, conv_w_out, final_norm_g)
    moms = (m_c_ctx, m_norm_g, m_w_mod, m_b_mod, m_ffn_w_gate, m_ffn_w_up, m_ffn_w_down, m_ab_w_in, m_pool_w,
            m_pool_scale, m_q_norm_g, m_w_uq, m_kv_norm_g, m_w_ukv, m_ab_w_out, m_conv_w_in, m_conv_w, m_conv_w_out,
            m_final_norm_g)
    vels = (v_c_ctx, v_norm_g, v_w_mod, v_b_mod, v_ffn_w_gate, v_ffn_w_up, v_ffn_w_down, v_ab_w_in, v_pool_w,
            v_pool_scale, v_q_norm_g, v_w_uq, v_kv_norm_g, v_w_ukv, v_ab_w_out, v_conv_w_in, v_conv_w, v_conv_w_out,
            v_final_norm_g)
    me = 4 * lax.axis_index("x") + 2 * lax.axis_index("y") + lax.axis_index("c")
    n_lat, n_ctx = x.shape[1], ctx.shape[1]
    d = D_MODEL
    mod_cols = w_mod.shape[-1]
    ng_sh, cw_sh = norm_g.shape[-1], conv_w.shape[-1]

    small = jnp.concatenate([c.reshape(-1), norm_g.reshape(-1), conv_w.reshape(-1)])
    small_n = -(-small.shape[0] // 1024) * 1024
    small = jnp.pad(small, (0, small_n - small.shape[0])).reshape(small_n // 128, 128)
    small_all = _exchange("gather_small", small, False).reshape(N_DEV, small_n)
    c_all = small_all[:, :d]
    o1 = d + 6 * ng_sh
    norm_g_full = small_all[:, d:o1].reshape(N_DEV, 2, 3, ng_sh).transpose(1, 2, 0, 3).reshape(2, 3, d)
    conv_w_full = small_all[:, o1:o1 + 3 * cw_sh].reshape(N_DEV, 3, cw_sh).transpose(1, 0, 2).reshape(3, d)

    cond = jnp.concatenate([c_all, jnp.broadcast_to(c_ctx[None, :], (N_DEV, d))], axis=0)
    sil, dsil = _silu_rows("mod_silu", cond)
    w_mod_b = w_mod.astype(BF16)
    b_sh = lax.dynamic_slice(b_mod, (0, me * mod_cols), (2, mod_cols))
    m_part = jnp.stack([_mm(f"mod_fwd{l}", [(sil, w_mod_b[l])], "nn", F32, 16, 384, bias=b_sh[l:l + 1])
                        for l in range(2)], axis=1)
    m_all = _exchange("gather_mod", m_part.reshape(-1, 128), False).reshape(N_DEV, 2 * N_DEV, 2, mod_cols)
    m_mine = lax.dynamic_index_in_dim(m_all, me, axis=1, keepdims=False)
    mod_h = m_mine.transpose(1, 0, 2).reshape(2, N_MOD, d)
    mod_g = m_all[:, N_DEV, 0, :].reshape(N_MOD, d)

    def ffn_shards(i):
        return {f"gate_t{i}": ffn_w_gate[i // 2, i % 2].T, f"up_t{i}": ffn_w_up[i // 2, i % 2].T,
                f"down{i}": ffn_w_down[i // 2, i % 2]}

    local = {**ffn_shards(0), "in_t": ab_w_in[0].T, "uq": w_uq[0], "ukv_t": w_ukv[0].T, "ab_out": ab_w_out[0],
             **ffn_shards(1), **ffn_shards(2), "cin_t": conv_w_in[0].T, "c_out": conv_w_out[0], **ffn_shards(3)}
    feed = _Feed({nm: a.astype(BF16) for nm, a in local.items()}, me, m_all)

    sq_cols, ds0, g = _local_step(x[0], ctx[0], loss_target[0], mod_h, mod_g, norm_g_full, feed, pool_w[0],
                                  pool_scale, q_norm_g, kv_norm_g, conv_w_full, final_norm_g)
    grad_x = ds0[:n_lat]
    loss = lax.psum(0.5 * jnp.sum(sq_cols) / d, ("x", "y", "c"))
    red = feed.finish(ds0)

    dm = jnp.stack([g["mod_h"], jnp.stack([g["mod_g"], jnp.zeros_like(g["mod_g"])])])
    dm_all = _exchange("gather_dmod", dm.reshape(-1, 128), False).reshape(N_DEV, 2, 2, N_MOD * d)
    grad_b_mod = _sum_rows("dmod_bias", dm_all.reshape(2 * N_DEV, 2 * N_MOD * d)).reshape(2, N_MOD * d)
    dm_sh = lax.dynamic_slice(dm_all, (0, 0, 0, me * mod_cols), (N_DEV, 2, 2, mod_cols))
    gw_mod, cctx_parts = [], []
    for l in range(2):
        dm_l = dm_sh[:, :, l, :].transpose(1, 0, 2).reshape(2 * N_DEV, mod_cols).astype(BF16)
        gw_mod.append(_mm(f"mod_dw{l}", [(sil, dm_l)], "tn", F32, 512, 384))
        dm_ctx = jnp.concatenate([dm_l[N_DEV:], jnp.zeros((N_DEV, mod_cols), BF16)], axis=0)
        cctx_parts.append(_mm(f"mod_dcond{l}", [(dm_ctx, w_mod_b[l])], "nt", F32, 16, 512))
    grad_w_mod = jnp.stack(gw_mod)
    cctx_part = _sum_rows("mod_dcond_sum", jnp.concatenate(cctx_parts, axis=0))

    small_g = jnp.concatenate([g["pool_w"].reshape(-1), g["pool_scale"].reshape(-1), g["q_norm_g"].reshape(-1),
                               g["kv_norm_g"].reshape(-1), g["final_norm_g"].reshape(-1), g["norm_g"].reshape(-1),
                               g["conv_w"].reshape(-1), cctx_part.reshape(-1)])
    sizes = [pool_w.size, pool_scale.size, q_norm_g.size, kv_norm_g.size, d, 6 * d, 3 * d, d]
    sg_n = -(-small_g.shape[0] // 1024) * 1024
    small_g = jnp.pad(small_g, (0, sg_n - small_g.shape[0]))
    sg_all = _exchange("gather_small_grads", small_g.reshape(-1, 128), False).reshape(N_DEV, sg_n)
    scale_vec = jnp.concatenate([jnp.ones((1, sum(sizes[:-1])), F32), dsil[N_DEV:N_DEV + 1],
                                 jnp.ones((1, sg_n - sum(sizes)), F32)], axis=1)
    sg = _sum_rows("small_grads_sum", sg_all, scale_vec)[0]
    cuts, pos = [], 0
    for sz in sizes:
        cuts.append(sg[pos:pos + sz])
        pos += sz
    g_pool_w, g_pool_scale, g_q_norm, g_kv_norm, g_final, g_norm_full, g_conv_full, g_c_ctx = cuts
    grad_norm_g = lax.dynamic_slice(g_norm_full.reshape(2, 3, d), (0, 0, me * ng_sh), (2, 3, ng_sh))
    grad_conv_w = lax.dynamic_slice(g_conv_full.reshape(3, d), (0, me * cw_sh), (3, cw_sh))[None]

    def mine(nm):
        return red[nm]

    grad_gate = jnp.stack([mine(f"gate_t{i}").T for i in range(4)]).reshape(ffn_w_gate.shape)
    grad_up = jnp.stack([mine(f"up_t{i}").T for i in range(4)]).reshape(ffn_w_up.shape)
    grad_down = jnp.stack([mine(f"down{i}") for i in range(4)]).reshape(ffn_w_down.shape)
    grads = (g_c_ctx, grad_norm_g, grad_w_mod, grad_b_mod, grad_gate, grad_up, grad_down, mine("in_t").T[None],
             g_pool_w.reshape(pool_w.shape), g_pool_scale.reshape(pool_scale.shape), g_q_norm.reshape(q_norm_g.shape),
             mine("uq")[None], g_kv_norm.reshape(kv_norm_g.shape), mine("ukv_t").T[None], mine("ab_out")[None],
             mine("cin_t").T[None], grad_conv_w, mine("c_out")[None], g_final)
    grads = tuple(gr.reshape(w.shape) for gr, w in zip(grads, weights))
    deltas, new_m, new_v = _adam_all(WEIGHT_NAMES, weights, grads, moms, vels)
    return (loss, grad_x[None], *grads, *deltas, *new_m, *new_v)
```

```python
import functools
import math

import jax
import jax.numpy as jnp
import numpy as np
from jax import lax
from jax.experimental import pallas as pl
from jax.experimental.pallas import tpu as pltpu

F32 = jnp.float32
BF16 = jnp.bfloat16
MESH = pl.DeviceIdType.MESH
SDS = jax.ShapeDtypeStruct

N_DEV = 8
D_MODEL = 1024
N_MOD = 9
D_FF = 2816
POOL_WINDOWS = (2, 4, 8, 16)
POOL_DIM = 512
POOL_GROUP_DIM = 128
HEADS = 8
QK_NOPE = 64
QK_ROPE = 32
QK_HEAD = QK_NOPE + QK_ROPE
V_HEAD = 64
Q_RANK = 768
KV_RANK = 256
GRID_W = 64
ROPE_THETA = 10000.0
RMS_EPS = 1e-6
ATTN_SCALE = 1.0 / math.sqrt(QK_HEAD)
HEAD_PAD = 128
POOL_PAD = 16
PA_POOL, PA_CQ, PA_KV = 0, 768, 1536
PA_KV_W = 384
PA_W = PA_KV + PA_KV_W

ADAM_LR, ADAM_B1, ADAM_B2, ADAM_EPS, ADAM_WD, ADAM_STEP = 0.001, 0.9, 0.999, 1e-08, 0.01, 10

VMEM_LIMIT_BYTES = 56 * 1024 * 1024

NN = ((1,), (0,))
NT = ((1,), (1,))
TN = ((0,), (0,))


def _cparams():
    return pltpu.CompilerParams(vmem_limit_bytes=VMEM_LIMIT_BYTES)


def _dot(a, b, dims):
    return lax.dot_general(a, b, (dims, ((), ())), preferred_element_type=F32)


def _tile(n, cap, mult=8):
    t = (min(cap, n) // mult) * mult
    while t >= mult:
        if n % t == 0:
            return t
        t -= mult
    return n


def _colsum(x):
    return jnp.sum(x, axis=0, keepdims=True)


def _rms(x):
    r = lax.rsqrt(jnp.mean(x * x, axis=-1, keepdims=True) + RMS_EPS)
    return x * r, r


def _rms_bwd(n, r, dn):
    return r * (dn - n * jnp.mean(dn * n, axis=-1, keepdims=True))


def _rowwise(name, fn, t_rows, tm, n_lat, rows, vecs, outs, accs):
    nt = t_rows // tm
    nlt = n_lat // tm
    n_groups = 2 if nlt < nt else 1

    def grp(i):
        return jnp.where(i >= nlt, 1, 0) if n_groups == 2 else 0

    in_specs = [pl.BlockSpec((tm, w), functools.partial(lambda i, cb: (i, cb), cb=cb)) for (_, w, cb) in rows]
    in_specs += [pl.BlockSpec((1,) + v.shape[1:], lambda i: (grp(i), 0, 0)) for v in vecs]
    out_specs = [pl.BlockSpec((tm, w), lambda i: (i, 0)) for (w, _) in outs]
    out_specs += [pl.BlockSpec((1, 1, w), lambda i: (grp(i), 0, 0)) for w in accs]
    out_shape = [SDS((t_rows, w), dt) for (w, dt) in outs] + [SDS((n_groups, 1, w), F32) for w in accs]
    n_r, n_v, n_o = len(rows), len(vecs), len(outs)

    def body(*refs):
        row_vals = [r[...] for r in refs[:n_r]]
        vec_vals = [v[0] for v in refs[n_r:n_r + n_v]]
        out_refs = refs[n_r + n_v:n_r + n_v + n_o]
        acc_refs = refs[n_r + n_v + n_o:]
        out_vals, acc_vals = fn(row_vals, vec_vals)
        for o_ref, o in zip(out_refs, out_vals):
            o_ref[...] = o.astype(o_ref.dtype)
        if acc_refs:
            i = pl.program_id(0)
            first = (i == 0) | (i == nlt) if n_groups == 2 else i == 0

            @pl.when(first)
            def _():
                for a_ref, a in zip(acc_refs, acc_vals):
                    a_ref[0] = a

            @pl.when(jnp.logical_not(first))
            def _():
                for a_ref, a in zip(acc_refs, acc_vals):
                    a_ref[0] += a

    res = pl.pallas_call(
        body, name=name, grid=(nt,), in_specs=in_specs, out_specs=out_specs, out_shape=out_shape,
        compiler_params=_cparams(),
    )(*[r[0] for r in rows], *vecs)
    return res[:n_o], res[n_o:]


RESIDENT_BYTES = 12 * 1024 * 1024


def _mm(name, pairs, mode, out_dtype, tm_cap=256, tn_cap=512, bias=None):
    a0, b0 = pairs[0]
    if mode == "nn":
        m, n, dims = a0.shape[0], b0.shape[1], NN
    elif mode == "nt":
        m, n, dims = a0.shape[0], b0.shape[0], NT
    else:
        m, n, dims = a0.shape[1], b0.shape[1], TN
    b_bytes = sum(b.size * b.dtype.itemsize for _, b in pairs)
    tn = n if b_bytes <= RESIDENT_BYTES else _tile(n, tn_cap, 128)
    tm = _tile(m, tm_cap, 128 if mode == "tn" else 16)

    def a_spec(a):
        if mode == "tn":
            return pl.BlockSpec((a.shape[0], tm), lambda i, j: (0, i))
        return pl.BlockSpec((tm, a.shape[1]), lambda i, j: (i, 0))

    def b_spec(b):
        if mode == "nt":
            return pl.BlockSpec((tn, b.shape[1]), lambda i, j: (j, 0))
        return pl.BlockSpec((b.shape[0], tn), lambda i, j: (0, j))

    in_specs, flat = [], []
    for a, b in pairs:
        in_specs += [a_spec(a), b_spec(b)]
        flat += [a, b]
    if bias is not None:
        in_specs.append(pl.BlockSpec((1, tn), lambda i, j: (0, j)))
        flat.append(bias)
    n_pairs = len(pairs)

    def body(*refs):
        acc = None
        for p in range(n_pairs):
            t = _dot(refs[2 * p][...], refs[2 * p + 1][...], dims)
            acc = t if acc is None else acc + t
        if bias is not None:
            acc = acc + refs[2 * n_pairs][...]
        refs[-1][...] = acc.astype(refs[-1].dtype)

    return pl.pallas_call(
        body, name=name, grid=(m // tm, n // tn), in_specs=in_specs,
        out_specs=pl.BlockSpec((tm, tn), lambda i, j: (i, j)),
        out_shape=SDS((m, n), out_dtype), compiler_params=_cparams(),
    )(*flat)


def _mm_resid(name, a, b, s, mg, k, coef, n_lat):
    t_rows, n = a.shape[0], b.shape[1]
    tm = _tile(math.gcd(n_lat, t_rows), 256, 16)
    nlt = n_lat // tm
    n_groups = 2 if nlt < t_rows // tm else 1

    def grp(i):
        return jnp.where(i >= nlt, 1, 0) if n_groups == 2 else 0

    def body(a_ref, b_ref, s_ref, mg_ref, so_ref, o_ref):
        o = _dot(a_ref[...], b_ref[...], NN)
        gate = mg_ref[0, 3 * k + 2:3 * k + 3, :]
        o_ref[...] = o
        so_ref[...] = s_ref[...] + (coef * gate) * o

    row = pl.BlockSpec((tm, n), lambda i: (i, 0))
    return pl.pallas_call(
        body, name=name, grid=(t_rows // tm,),
        in_specs=[pl.BlockSpec((tm, a.shape[1]), lambda i: (i, 0)), pl.BlockSpec(b.shape, lambda i: (0, 0)), row,
                  pl.BlockSpec((1, mg.shape[1], n), lambda i: (grp(i), 0, 0))],
        out_specs=[row, row], out_shape=[SDS((t_rows, n), F32), SDS((t_rows, n), F32)], compiler_params=_cparams(),
    )(a, b, s, mg)


def _ffn_up(name, u, wg_t, wu_t):
    t_rows, f = u.shape[0], wg_t.shape[0]
    tm = _tile(t_rows, 256, 16)

    def body(u_ref, wg_ref, wu_ref, a_ref, b_ref, h_ref):
        uu = u_ref[...]
        a = _dot(uu, wg_ref[...], NT)
        b = _dot(uu, wu_ref[...], NT)
        a_ref[...] = a.astype(BF16)
        b_ref[...] = b.astype(BF16)
        h_ref[...] = (a * jax.nn.sigmoid(a) * b).astype(BF16)

    w_spec = pl.BlockSpec(wg_t.shape, lambda i: (0, 0))
    o_spec = pl.BlockSpec((tm, f), lambda i: (i, 0))
    return pl.pallas_call(
        body, name=name, grid=(t_rows // tm,),
        in_specs=[pl.BlockSpec((tm, u.shape[1]), lambda i: (i, 0)), w_spec, w_spec],
        out_specs=[o_spec, o_spec, o_spec], out_shape=[SDS((t_rows, f), BF16)] * 3, compiler_params=_cparams(),
    )(u, wg_t, wu_t)


def _ffn_dact(name, do, wd, a, b):
    t_rows, f = do.shape[0], wd.shape[0]
    tm = _tile(t_rows, 256, 16)

    def body(do_ref, wd_ref, a_ref, b_ref, da_ref, db_ref):
        dh = _dot(do_ref[...], wd_ref[...], NT)
        av = a_ref[...].astype(F32)
        bv = b_ref[...].astype(F32)
        sg = jax.nn.sigmoid(av)
        da_ref[...] = (dh * bv * (sg * (1.0 + av * (1.0 - sg)))).astype(BF16)
        db_ref[...] = (dh * (av * sg)).astype(BF16)

    t_spec = pl.BlockSpec((tm, f), lambda i: (i, 0))
    return pl.pallas_call(
        body, name=name, grid=(t_rows // tm,),
        in_specs=[pl.BlockSpec((tm, do.shape[1]), lambda i: (i, 0)), pl.BlockSpec(wd.shape, lambda i: (0, 0)),
                  t_spec, t_spec],
        out_specs=[t_spec, t_spec], out_shape=[SDS((t_rows, f), BF16)] * 2, compiler_params=_cparams(),
    )(do, wd, a, b)


def _row_tm(t_rows, n_lat):
    return _tile(math.gcd(t_rows, n_lat), 256, 16)


def _adaln_fwd(name, s, mg, k, n_lat):
    t_rows = s.shape[0]

    def fn(rv, vv):
        m = vv[0]
        n, _ = _rms(rv[0])
        u = (n * m[9 + k:10 + k]) * (1.0 + m[3 * k + 1:3 * k + 2]) + m[3 * k:3 * k + 1]
        return [u], []

    (u,), _ = _rowwise(name, fn, t_rows, _row_tm(t_rows, n_lat), n_lat, [(s, D_MODEL, 0)], [mg], [(D_MODEL, BF16)], [])
    return u


def _adaln_bwd(name, s, du, ds_out, mg, k, n_lat):
    t_rows = s.shape[0]

    def fn(rv, vv):
        m = vv[0]
        gain, scale = m[9 + k:10 + k], m[3 * k + 1:3 * k + 2]
        n, r = _rms(rv[0])
        d_u = rv[1]
        dxn = d_u * (1.0 + scale)
        ds = _rms_bwd(n, r, dxn * gain)
        return [rv[2] + ds], [_colsum(d_u), _colsum(d_u * (n * gain)), _colsum(dxn * n)]

    (ds_in,), accs = _rowwise(name, fn, t_rows, _row_tm(t_rows, n_lat), n_lat,
                              [(s, D_MODEL, 0), (du, D_MODEL, 0), (ds_out, D_MODEL, 0)], [mg],
                              [(D_MODEL, F32)], [D_MODEL] * 3)
    return ds_in, accs


def _gate_bwd(name, ds_out, o, mg, k, coef, n_lat):
    t_rows = o.shape[0]

    def fn(rv, vv):
        gate = vv[0][3 * k + 2:3 * k + 3]
        d = coef * rv[0]
        return [d * gate], [_colsum(d * rv[1])]

    (do,), (dgate,) = _rowwise(name, fn, t_rows, _row_tm(t_rows, n_lat), n_lat,
                               [(ds_out, D_MODEL, 0), (o, D_MODEL, 0)], [mg], [(D_MODEL, BF16)], [D_MODEL])
    return do, dgate


def _rmsnorm_fwd(name, x, width, colblk, gain, t_rows):
    def fn(rv, vv):
        n, _ = _rms(rv[0])
        return [n * vv[0]], []

    (y,), _ = _rowwise(name, fn, t_rows, _tile(t_rows, 256, 16), t_rows, [(x, width, colblk)],
                       [gain.reshape(1, 1, width)], [(width, BF16)], [])
    return y


def _rmsnorm_bwd(name, x, width, colblk, dy, gain, t_rows):
    def fn(rv, vv):
        n, r = _rms(rv[0])
        return [_rms_bwd(n, r, rv[1] * vv[0])], [_colsum(rv[1] * n)]

    (dx,), (dgain,) = _rowwise(name, fn, t_rows, _tile(t_rows, 256, 16), t_rows,
                               [(x, width, colblk), (dy, width, 0)], [gain.reshape(1, 1, width)],
                               [(width, F32)], [width])
    return dx, dgain


def _final_loss(name, h, target, gain):
    t_rows = h.shape[0]
    inv_d = 1.0 / D_MODEL

    def fn(rv, vv):
        g = vv[0]
        n, r = _rms(rv[0])
        e = n * g - rv[1]
        dy = e * inv_d
        return [_rms_bwd(n, r, dy * g)], [_colsum(e * e), _colsum(dy * n)]

    (dh,), (sq, dgain) = _rowwise(name, fn, t_rows, _tile(t_rows, 256, 16), t_rows,
                                  [(h, D_MODEL, 0), (target, D_MODEL, 0)], [gain.reshape(1, 1, D_MODEL)],
                                  [(D_MODEL, F32)], [D_MODEL, D_MODEL])
    return dh, sq, dgain


def _rope(name, z, width, colblk, cos, sin, perm, backward, out_dtype):
    t_rows = cos.shape[0]

    def body(z_ref, c_ref, s_ref, p_ref, o_ref):
        zz = z_ref[...]
        pre = zz * s_ref[...] if backward else zz
        hi = pre.astype(BF16)
        lo = (pre - hi.astype(F32)).astype(BF16)
        rot = _dot(hi, p_ref[...], NN) + _dot(lo, p_ref[...], NN)
        if not backward:
            rot = rot * s_ref[...]
        o_ref[...] = (zz * c_ref[...] + rot).astype(o_ref.dtype)

    tm = _tile(t_rows, 256, 16)
    t_spec = pl.BlockSpec((tm, width), lambda i: (i, 0))
    return pl.pallas_call(
        body, name=name, grid=(t_rows // tm,),
        in_specs=[pl.BlockSpec((tm, width), lambda i: (i, colblk)), t_spec, t_spec,
                  pl.BlockSpec((width, width), lambda i: (0, 0))],
        out_specs=t_spec, out_shape=SDS((t_rows, width), out_dtype), compiler_params=_cparams(),
    )(z, cos, sin, perm)


def _window_sum(x, w, transposed):
    n_rows = x.shape[0]
    zeros = jnp.zeros((POOL_PAD, x.shape[1]), F32)
    y = jnp.concatenate([zeros, x, zeros], axis=0)
    total = n_rows + 2 * POOL_PAD
    if transposed:
        y = y + pltpu.roll(y, total - 1, 0)
    else:
        y = y + pltpu.roll(y, 1, 0)
    step = 1
    while 2 * step < w:
        y = pltpu.roll(y, step, 0) + pltpu.roll(y, total - step, 0)
        step *= 2
    return y[POOL_PAD:POOL_PAD + n_rows]


def _window_count(n_rows, w):
    t = lax.broadcasted_iota(jnp.int32, (n_rows, 1), 0)
    lo = jnp.maximum(t - w // 2, 0)
    hi = jnp.minimum(t + (w - w // 2 - 1), n_rows - 1)
    return (hi - lo + 1).astype(F32)


def _pool_fwd(name, proj, n_rows, w_grp, scale):
    def body(x_ref, w_ref, sc_ref, y_ref, p_ref):
        for g, w in enumerate(POOL_WINDOWS):
            cols = slice(g * POOL_GROUP_DIM, (g + 1) * POOL_GROUP_DIM)
            x = x_ref[:, cols]
            p = _window_sum(x, w, False) * (1.0 / _window_count(n_rows, w)) - x
            pb = p.astype(BF16)
            p_ref[:, cols] = pb
            y_ref[:, cols] = (_dot(pb, w_ref[g], NN) * sc_ref[:, cols]).astype(BF16)

    blk = pl.BlockSpec((n_rows, POOL_DIM), lambda i: (0, 0))
    return pl.pallas_call(
        body, name=name, grid=(1,),
        in_specs=[blk, pl.BlockSpec(w_grp.shape, lambda i: (0, 0, 0)), pl.BlockSpec((1, POOL_DIM), lambda i: (0, 0))],
        out_specs=[blk, blk], out_shape=[SDS((n_rows, POOL_DIM), BF16)] * 2, compiler_params=_cparams(),
    )(proj, w_grp, scale)


def _pool_bwd(name, dcat, n_rows, p, w_grp, scale):
    def body(dy_ref, p_ref, w_ref, sc_ref, dx_ref, dw_ref, dsc_ref):
        for g, w in enumerate(POOL_WINDOWS):
            cols = slice(g * POOL_GROUP_DIM, (g + 1) * POOL_GROUP_DIM)
            dy = dy_ref[:, cols]
            pb = p_ref[:, cols]
            pw = _dot(pb, w_ref[g], NN)
            dsc_ref[:, cols] = _colsum(dy * pw)
            dpw = (dy * sc_ref[:, cols]).astype(BF16)
            dw_ref[g] = _dot(pb, dpw, TN)
            dp = _dot(dpw, w_ref[g], NT)
            dx_ref[:, cols] = _window_sum(dp * (1.0 / _window_count(n_rows, w)), w, True) - dp

    blk = pl.BlockSpec((n_rows, POOL_DIM), lambda i: (0, 0))
    w_spec = pl.BlockSpec(w_grp.shape, lambda i: (0, 0, 0))
    v_spec = pl.BlockSpec((1, POOL_DIM), lambda i: (0, 0))
    return pl.pallas_call(
        body, name=name, grid=(1,), in_specs=[blk, blk, w_spec, v_spec], out_specs=[blk, w_spec, v_spec],
        out_shape=[SDS((n_rows, POOL_DIM), F32), SDS(w_grp.shape, F32), SDS((1, POOL_DIM), F32)],
        compiler_params=_cparams(),
    )(dcat, p, w_grp, scale)


def _attn_fwd(name, q, k, v):
    h, n_q, _ = q.shape
    n_k = k.shape[1]
    tq = _tile(n_q, 256, 16)

    def body(q_ref, k_ref, v_ref, o_ref, lse_ref):
        s = _dot(q_ref[...], k_ref[...], NT) * ATTN_SCALE
        m = jnp.max(s, axis=-1, keepdims=True)
        e = jnp.exp(s - m)
        l = jnp.sum(e, axis=-1, keepdims=True)
        p = (e * (1.0 / l)).astype(BF16)
        o_ref[...] = _dot(p, v_ref[...], NN).astype(BF16)
        lse_ref[...] = m + jnp.log(l)

    return pl.pallas_call(
        body, name=name, grid=(h, n_q // tq),
        in_specs=[pl.BlockSpec((None, tq, HEAD_PAD), lambda hh, i: (hh, i, 0)),
                  pl.BlockSpec((None, n_k, HEAD_PAD), lambda hh, i: (hh, 0, 0)),
                  pl.BlockSpec((None, n_k, V_HEAD), lambda hh, i: (hh, 0, 0))],
        out_specs=[pl.BlockSpec((None, tq, V_HEAD), lambda hh, i: (hh, i, 0)),
                   pl.BlockSpec((None, tq, 1), lambda hh, i: (hh, i, 0))],
        out_shape=[SDS((h, n_q, V_HEAD), BF16), SDS((h, n_q, 1), F32)], compiler_params=_cparams(),
    )(q, k, v)


def _attn_bwd(name, q, k, v, o, lse, do):
    h, n_q, _ = q.shape
    n_k = k.shape[1]
    tq = _tile(n_q, 256, 16)

    def body(q_ref, k_ref, v_ref, o_ref, lse_ref, do_ref, dq_ref, dk_ref, dv_ref, dks_ref):
        hh, i = pl.program_id(0), pl.program_id(1)
        qq, kk, dd = q_ref[...], k_ref[...], do_ref[...]
        s = _dot(qq, kk, NT) * ATTN_SCALE
        p = jnp.exp(s - lse_ref[...])
        dp = _dot(dd, v_ref[...], NT)
        delta = jnp.sum(dd.astype(F32) * o_ref[...].astype(F32), axis=-1, keepdims=True)
        ds = (p * (dp - delta) * ATTN_SCALE).astype(BF16)
        dq_ref[...] = _dot(ds, kk, NN)
        dk = _dot(ds, qq, TN)
        dv = _dot(p.astype(BF16), dd, TN)

        @pl.when(i == 0)
        def _():
            dk_ref[...] = dk
            dv_ref[...] = dv

        @pl.when(i > 0)
        def _():
            dk_ref[...] += dk
            dv_ref[...] += dv

        @pl.when((i == 0) & (hh == 0))
        def _():
            dks_ref[...] = dk

        @pl.when((i > 0) | (hh > 0))
        def _():
            dks_ref[...] += dk

    q_spec = pl.BlockSpec((None, tq, HEAD_PAD), lambda hh, i: (hh, i, 0))
    k_spec = pl.BlockSpec((None, n_k, HEAD_PAD), lambda hh, i: (hh, 0, 0))
    v_spec = pl.BlockSpec((None, n_k, V_HEAD), lambda hh, i: (hh, 0, 0))
    o_spec = pl.BlockSpec((None, tq, V_HEAD), lambda hh, i: (hh, i, 0))
    return pl.pallas_call(
        body, name=name, grid=(h, n_q // tq),
        in_specs=[q_spec, k_spec, v_spec, o_spec, pl.BlockSpec((None, tq, 1), lambda hh, i: (hh, i, 0)), o_spec],
        out_specs=[q_spec, k_spec, v_spec, pl.BlockSpec((n_k, HEAD_PAD), lambda hh, i: (0, 0))],
        out_shape=[SDS((h, n_q, HEAD_PAD), F32), SDS((h, n_k, HEAD_PAD), F32), SDS((h, n_k, V_HEAD), F32),
                   SDS((n_k, HEAD_PAD), F32)],
        compiler_params=_cparams(),
    )(q, k, v, o, lse, do)


CONV_COLS = 256


def _shift_rows(x, d):
    n_rows = x.shape[0]
    t = lax.broadcasted_iota(jnp.int32, (n_rows, 1), 0)
    if d > 0:
        return jnp.where(t >= d, pltpu.roll(x, d, 0), 0.0)
    return jnp.where(t < n_rows + d, pltpu.roll(x, n_rows + d, 0), 0.0)


def _conv_fwd(name, z3, conv_w):
    n_rows = z3.shape[0]
    nb = D_MODEL // CONV_COLS

    def body(b_ref, c_ref, v_ref, w_ref, y_ref):
        z = c_ref[...] * v_ref[...]
        zc = w_ref[0:1, :] * _shift_rows(z, 1) + w_ref[1:2, :] * z + w_ref[2:3, :] * _shift_rows(z, -1)
        y_ref[...] = (b_ref[...] * zc).astype(BF16)

    def part(k):
        return pl.BlockSpec((n_rows, CONV_COLS), lambda j: (0, k * nb + j))

    return pl.pallas_call(
        body, name=name, grid=(nb,),
        in_specs=[part(0), part(1), part(2), pl.BlockSpec((3, CONV_COLS), lambda j: (0, j))],
        out_specs=pl.BlockSpec((n_rows, CONV_COLS), lambda j: (0, j)),
        out_shape=SDS((n_rows, D_MODEL), BF16), compiler_params=_cparams(),
    )(z3, z3, z3, conv_w)


def _conv_bwd(name, dy, z3, conv_w):
    n_rows = z3.shape[0]
    nb = D_MODEL // CONV_COLS

    def body(dy_ref, b_ref, c_ref, v_ref, w_ref, db_ref, dc_ref, dv_ref, dw_ref):
        c, v, d_y = c_ref[...], v_ref[...], dy_ref[...]
        z = c * v
        z_dn, z_up = _shift_rows(z, 1), _shift_rows(z, -1)
        zc = w_ref[0:1, :] * z_dn + w_ref[1:2, :] * z + w_ref[2:3, :] * z_up
        db_ref[...] = (d_y * zc).astype(BF16)
        dzc = d_y * b_ref[...]
        dz = w_ref[0:1, :] * _shift_rows(dzc, -1) + w_ref[1:2, :] * dzc + w_ref[2:3, :] * _shift_rows(dzc, 1)
        dc_ref[...] = (dz * v).astype(BF16)
        dv_ref[...] = (dz * c).astype(BF16)
        dw_ref[0:1, :] = _colsum(dzc * z_dn)
        dw_ref[1:2, :] = _colsum(dzc * z)
        dw_ref[2:3, :] = _colsum(dzc * z_up)

    def part(k):
        return pl.BlockSpec((n_rows, CONV_COLS), lambda j: (0, k * nb + j))

    col = pl.BlockSpec((n_rows, CONV_COLS), lambda j: (0, j))
    w_spec = pl.BlockSpec((3, CONV_COLS), lambda j: (0, j))
    return pl.pallas_call(
        body, name=name, grid=(nb,), in_specs=[col, part(0), part(1), part(2), w_spec],
        out_specs=[col, col, col, w_spec],
        out_shape=[SDS((n_rows, D_MODEL), BF16)] * 3 + [SDS((3, D_MODEL), F32)], compiler_params=_cparams(),
    )(dy, z3, z3, z3, conv_w)


def _silu_rows(name, x):
    def body(x_ref, s_ref, d_ref):
        xx = x_ref[...]
        sg = jax.nn.sigmoid(xx)
        s_ref[...] = (xx * sg).astype(BF16)
        d_ref[...] = sg * (1.0 + xx * (1.0 - sg))

    return pl.pallas_call(body, name=name, out_shape=[SDS(x.shape, BF16), SDS(x.shape, F32)])(x)


def _sum_rows(name, x, scale=None):
    r, n = x.shape
    tn = _tile(n, 8192, 128)

    def body(*refs):
        acc = jnp.sum(refs[0][...].astype(F32), axis=0, keepdims=True)
        if scale is not None:
            acc = acc * refs[1][...]
        refs[-1][...] = acc

    in_specs = [pl.BlockSpec((r, tn), lambda j: (0, j))]
    args = [x]
    if scale is not None:
        in_specs.append(pl.BlockSpec((1, tn), lambda j: (0, j)))
        args.append(scale)
    return pl.pallas_call(body, name=name, grid=(n // tn,), in_specs=in_specs,
                          out_specs=pl.BlockSpec((1, tn), lambda j: (0, j)), out_shape=SDS((1, n), F32))(*args)


def _sum_slots(name, x):
    n_slots, r, c = x.shape
    tr = _tile(r, 432, 16)

    def body(x_ref, o_ref):
        acc = x_ref[0].astype(F32)
        for sl in range(1, n_slots):
            acc = acc + x_ref[sl].astype(F32)
        o_ref[...] = acc

    return pl.pallas_call(body, name=name, grid=(r // tr,),
                          in_specs=[pl.BlockSpec((n_slots, tr, c), lambda i: (0, i, 0))],
                          out_specs=pl.BlockSpec((tr, c), lambda i: (i, 0)), out_shape=SDS((r, c), F32),
                          compiler_params=_cparams())(x)


def _adamw(name, w, g, m, v):
    shape = w.shape
    cols = shape[-1]
    rows = w.size // cols
    tr = _tile(rows, 512, 8)
    bc1 = 1.0 - ADAM_B1 ** ADAM_STEP
    bc2 = 1.0 - ADAM_B2 ** ADAM_STEP

    def body(w_ref, g_ref, m_ref, v_ref, d_ref, nm_ref, nv_ref):
        gg = g_ref[...]
        nm = ADAM_B1 * m_ref[...] + (1.0 - ADAM_B1) * gg
        nv = ADAM_B2 * v_ref[...] + (1.0 - ADAM_B2) * (gg * gg)
        nm_ref[...] = nm
        nv_ref[...] = nv
        d_ref[...] = -ADAM_LR * ((nm / bc1) / (jnp.sqrt(nv / bc2) + ADAM_EPS) + ADAM_WD * w_ref[...])

    spec = pl.BlockSpec((tr, cols), lambda i: (i, 0))
    outs = pl.pallas_call(body, name=name, grid=(rows // tr,), in_specs=[spec] * 4, out_specs=[spec] * 3,
                          out_shape=[SDS((rows, cols), F32)] * 3, compiler_params=_cparams())(
        w.reshape(rows, cols), g.reshape(rows, cols), m.reshape(rows, cols), v.reshape(rows, cols))
    return tuple(t.reshape(shape) for t in outs)


def _exchange(name, x, scatter):
    blk = x.shape[1:] if scatter else x.shape

    def body(x_ref, out_ref, send_sems, recv_sems, local_sem):
        mx, my, mc = lax.axis_index("x"), lax.axis_index("y"), lax.axis_index("c")
        me = 4 * mx + 2 * my + mc
        own = pltpu.make_async_copy(x_ref.at[me] if scatter else x_ref, out_ref.at[me], local_sem)
        own.start()
        copies = []
        for kk in range(1, N_DEV):
            px = jnp.bitwise_xor(mx, (kk >> 2) & 1)
            py = jnp.bitwise_xor(my, (kk >> 1) & 1)
            pc = jnp.bitwise_xor(mc, kk & 1)
            peer = 4 * px + 2 * py + pc
            send = pltpu.make_async_remote_copy(
                src_ref=x_ref.at[peer] if scatter else x_ref, dst_ref=out_ref.at[me],
                send_sem=send_sems.at[kk - 1], recv_sem=recv_sems.at[kk - 1],
                device_id=(px, py, pc), device_id_type=MESH)
            send.start()
            arrival = pltpu.make_async_remote_copy(
                src_ref=x_ref.at[peer] if scatter else x_ref, dst_ref=out_ref.at[peer],
                send_sem=send_sems.at[kk - 1], recv_sem=recv_sems.at[kk - 1],
                device_id=(px, py, pc), device_id_type=MESH)
            copies.append((send, arrival))
        for send, arrival in copies:
            arrival.wait_recv()
            send.wait_send()
        own.wait()

    return pl.pallas_call(
        body, name=name, out_shape=SDS((N_DEV,) + tuple(blk), x.dtype),
        in_specs=[pl.BlockSpec(memory_space=pl.ANY)], out_specs=pl.BlockSpec(memory_space=pl.ANY),
        scratch_shapes=[pltpu.SemaphoreType.DMA((N_DEV - 1,)), pltpu.SemaphoreType.DMA((N_DEV - 1,)),
                        pltpu.SemaphoreType.DMA],
    )(x)


def _rope_perm(pre, reps, post):
    half = QK_ROPE // 4
    width = reps * (pre + QK_ROPE) + post
    p = np.zeros((width, width), np.float32)
    for rep in range(reps):
        s0 = rep * (pre + QK_ROPE) + pre
        for base in (s0, s0 + 2 * half):
            for i in range(half):
                p[base + half + i, base + i] = -1.0
                p[base + i, base + half + i] = 1.0
    return p


def _rope_tables(n_lat, t_rows, pre, reps, post):
    half = QK_ROPE // 4
    pos = jnp.arange(n_lat)
    freqs = jnp.power(ROPE_THETA, -jnp.arange(0, 2 * half, 2, dtype=F32) / (2 * half))
    ang_r = (pos // GRID_W).astype(F32)[:, None] * freqs
    ang_c = (pos % GRID_W).astype(F32)[:, None] * freqs
    ang = jnp.concatenate([ang_r, ang_r, ang_c, ang_c], axis=-1)

    def table(fn, plain):
        slot = jnp.concatenate([jnp.full((n_lat, pre), plain, F32), fn(ang)], axis=-1)
        t = jnp.concatenate([jnp.tile(slot, (1, reps)), jnp.full((n_lat, post), plain, F32)], axis=-1)
        return jnp.concatenate([t, jnp.full((t_rows - n_lat, t.shape[1]), plain, F32)], axis=0)

    return table(jnp.cos, 1.0), table(jnp.sin, 0.0)


def _ffn_half_fwd(tag, s, mg, k, feed, i, coef, n_lat):
    u = _adaln_fwd(f"{tag}_adaln", s, mg, k, n_lat)
    wg_t, wu_t = feed.weights(f"{tag}_up", [f"gate_t{i}", f"up_t{i}"], u)
    a, b, hid = _ffn_up(f"{tag}_up", u, wg_t, wu_t)
    (wd,) = feed.weights(f"{tag}_down", [f"down{i}"], hid)
    s_out, o = _mm_resid(f"{tag}_down", hid, wd, s, mg, k, coef, n_lat)
    return s_out, (s, u, a, b, hid, o, wg_t, wu_t, wd)


def _ffn_half_bwd(tag, ds_out, saved, mg, k, feed, i, coef, n_lat):
    s, u, a, b, hid, o, wg_t, wu_t, wd = saved
    do, dgate = _gate_bwd(f"{tag}_dgate", ds_out, o, mg, k, coef, n_lat)
    da, db = _ffn_dact(f"{tag}_dact", do, wd, a, b)
    dwd = _mm(f"{tag}_dwd", [(hid, do)], "tn", BF16)
    dwg_t = _mm(f"{tag}_dwg", [(da, u)], "tn", BF16)
    dwu_t = _mm(f"{tag}_dwu", [(db, u)], "tn", BF16)
    token = feed.grads(tag, {f"down{i}": dwd, f"gate_t{i}": dwg_t, f"up_t{i}": dwu_t})
    du = _mm(f"{tag}_du", [(da, wg_t), (db, wu_t)], "nn", F32, 384, 512, bias=_after(token))
    ds_in, (dshift, dscale, dgain) = _adaln_bwd(f"{tag}_dadaln", s, du, ds_out, mg, k, n_lat)
    return ds_in, dict(shift=dshift, scale=dscale, gate=dgate, gain=dgain)


def _after(token):
    return jnp.zeros((1, D_MODEL), F32) + token


def _mod_grad(parts, n_groups):
    rows = []
    zero = jnp.zeros((n_groups, 1, D_MODEL), F32)
    for k in range(3):
        for nm in ("shift", "scale", "gate"):
            t = parts[k].get(nm, zero)
            if t.shape[0] < n_groups:
                t = jnp.concatenate([t, jnp.zeros((n_groups - t.shape[0], 1, D_MODEL), F32)], axis=0)
            rows.append(t)
    return jnp.concatenate(rows, axis=1).reshape(n_groups, N_MOD * D_MODEL)


def _local_step(x, ctx, target, mod_h, mod_g, norm_g, feed, pool_w, pool_scale, q_norm_g, kv_norm_g, conv_w,
                final_norm_g):
    n_lat, n_ctx = x.shape[0], ctx.shape[0]
    t_all = n_lat + n_ctx
    mg0 = jnp.stack([jnp.concatenate([mod_h[0], norm_g[0]], axis=0), jnp.concatenate([mod_g, norm_g[0]], axis=0)])
    mg1 = jnp.concatenate([mod_h[1], norm_g[1]], axis=0)[None]

    s0 = jnp.concatenate([x, ctx], axis=0) + feed.start_token()
    s1, sv_f00 = _ffn_half_fwd("l0f0", s0, mg0, 0, feed, 0, 0.5, n_lat)

    ua = _adaln_fwd("l0m_adaln", s1, mg0, 1, n_lat)
    w_in, w_uq, w_ukv_t, w_ab_out = feed.weights("l0m", ["in_t", "uq", "ukv_t", "ab_out"], ua)
    kv_rows = KV_RANK + QK_ROPE
    w_in_t = jnp.concatenate([
        w_in[:POOL_DIM], jnp.zeros((PA_CQ - POOL_DIM, D_MODEL), BF16), w_in[POOL_DIM:POOL_DIM + Q_RANK],
        w_in[POOL_DIM + Q_RANK:], jnp.zeros((PA_KV_W - kv_rows, D_MODEL), BF16)], axis=0)
    proj = _mm("l0m_proj", [(ua, w_in_t)], "nt", F32, 768, 384)
    pool_y, pool_p = _pool_fwd("l0m_pool", proj, n_lat, pool_w.astype(BF16), pool_scale)
    nq = _rmsnorm_fwd("l0m_qnorm", proj, Q_RANK, PA_CQ // Q_RANK, q_norm_g, n_lat)
    q_lin = _mm("l0m_q", [(nq, w_uq)], "nn", F32, 512, 768)
    cos_q, sin_q = _rope_tables(n_lat, n_lat, QK_NOPE, HEADS, 0)
    perm_q = _rope_perm(QK_NOPE, HEADS, 0)
    q_rot = _rope("l0m_qrope", q_lin, Q_RANK, 0, cos_q, sin_q, jnp.asarray(perm_q, BF16), False, BF16)
    cos_k, sin_k = _rope_tables(n_lat, t_all, KV_RANK, 1, PA_KV_W - kv_rows)
    perm_k = _rope_perm(KV_RANK, 1, PA_KV_W - kv_rows)
    kvr = _rope("l0m_krope", proj, PA_KV_W, PA_KV // PA_KV_W, cos_k, sin_k, jnp.asarray(perm_k, BF16), False, F32)
    nkv = _rmsnorm_fwd("l0m_kvnorm", kvr, KV_RANK, 0, kv_norm_g, t_all)
    kv = _mm("l0m_kv", [(nkv, w_ukv_t)], "nt", BF16, 768, 512)
    qh = jnp.pad(q_rot.reshape(n_lat, HEADS, QK_HEAD), ((0, 0), (0, 0), (0, HEAD_PAD - QK_HEAD))).transpose(1, 0, 2)
    kvh = kv.reshape(t_all, HEADS, QK_NOPE + V_HEAD)
    k_rope = jnp.broadcast_to(kvr[:, None, KV_RANK:KV_RANK + QK_ROPE].astype(BF16), (t_all, HEADS, QK_ROPE))
    kh = jnp.concatenate([kvh[:, :, :QK_NOPE], k_rope, jnp.zeros((t_all, HEADS, HEAD_PAD - QK_HEAD), BF16)],
                         axis=-1).transpose(1, 0, 2)
    vh = kvh[:, :, QK_NOPE:].transpose(1, 0, 2)
    oh, lse = _attn_fwd("l0m_attn", qh, kh, vh)
    cat = jnp.concatenate([pool_y, oh.transpose(1, 0, 2).reshape(n_lat, HEADS * V_HEAD)], axis=-1)
    h1 = s1[:n_lat]
    h2, mix_o = _mm_resid("l0m_out", cat, w_ab_out, h1, mg0[:1], 1, 1.0, n_lat)

    h3, sv_f01 = _ffn_half_fwd("l0f1", h2, mg0[:1], 2, feed, 1, 0.5, n_lat)

    h4, sv_f10 = _ffn_half_fwd("l1f0", h3, mg1, 0, feed, 2, 0.5, n_lat)
    uc = _adaln_fwd("l1m_adaln", h4, mg1, 1, n_lat)
    w_cin_t, w_c_out = feed.weights("l1m", ["cin_t", "c_out"], uc)
    z3 = _mm("l1m_in", [(uc, w_cin_t)], "nt", F32)
    yc = _conv_fwd("l1m_conv", z3, conv_w)
    h5, conv_o = _mm_resid("l1m_out", yc, w_c_out, h4, mg1, 1, 1.0, n_lat)
    h6, sv_f11 = _ffn_half_fwd("l1f1", h5, mg1, 2, feed, 3, 0.5, n_lat)

    dh6, sq_cols, d_final_g = _final_loss("loss_head", h6, target, final_norm_g)
    g = {}
    dh5, g["f11"] = _ffn_half_bwd("l1f1", dh6, sv_f11, mg1, 2, feed, 3, 0.5, n_lat)

    do_c, dgate_c = _gate_bwd("l1m_dgate", dh5, conv_o, mg1, 1, 1.0, n_lat)
    dyc = _mm("l1m_dy", [(do_c, w_c_out)], "nt", F32)
    d_c_out = _mm("l1m_dwout", [(yc, do_c)], "tn", BF16)
    db_, dc_, dv_, d_conv_w = _conv_bwd("l1m_dconv", dyc, z3, conv_w)
    dz3 = jnp.concatenate([db_, dc_, dv_], axis=-1)
    d_cin_t = _mm("l1m_dwin", [(dz3, uc)], "tn", BF16)
    token = feed.grads("l1m", {"c_out": d_c_out, "cin_t": d_cin_t})
    duc = _mm("l1m_du", [(dz3, w_cin_t)], "nn", F32, bias=_after(token))
    dh4, (dsh_c, dsc_c, dgn_c) = _adaln_bwd("l1m_dadaln", h4, duc, dh5, mg1, 1, n_lat)
    dh3, g["f10"] = _ffn_half_bwd("l1f0", dh4, sv_f10, mg1, 0, feed, 2, 0.5, n_lat)

    dh2, g["f01"] = _ffn_half_bwd("l0f1", dh3, sv_f01, mg0[:1], 2, feed, 1, 0.5, n_lat)

    do_a, dgate_a = _gate_bwd("l0m_dgate", dh2, mix_o, mg0[:1], 1, 1.0, n_lat)
    dcat = _mm("l0m_dcat", [(do_a, w_ab_out)], "nt", F32)
    d_ab_out = _mm("l0m_dwout", [(cat, do_a)], "tn", BF16)
    d_pool_x, d_pool_w, d_pool_scale = _pool_bwd("l0m_dpool", dcat, n_lat, pool_p, pool_w.astype(BF16), pool_scale)
    doh = dcat[:, POOL_DIM:].reshape(n_lat, HEADS, V_HEAD).transpose(1, 0, 2).astype(BF16)
    dqh, dkh, dvh, dk_sum = _attn_bwd("l0m_dattn", qh, kh, vh, oh, lse, doh)
    dq_rot = dqh[:, :, :QK_HEAD].transpose(1, 0, 2).reshape(n_lat, Q_RANK)
    dq_lin = _rope("l0m_dqrope", dq_rot, Q_RANK, 0, cos_q, sin_q, jnp.asarray(perm_q.T, BF16), True, BF16)
    d_uq = _mm("l0m_dwuq", [(nq, dq_lin)], "tn", BF16, 768, 768)
    dnq = _mm("l0m_dnq", [(dq_lin, w_uq)], "nt", F32, 512, 768)
    dcq, d_q_norm_g = _rmsnorm_bwd("l0m_dqnorm", proj, Q_RANK, PA_CQ // Q_RANK, dnq, q_norm_g, n_lat)
    dkv = jnp.concatenate([dkh[:, :, :QK_NOPE], dvh], axis=-1).transpose(1, 0, 2).reshape(t_all, HEADS * HEAD_PAD)
    dkv = dkv.astype(BF16)
    dnkv = _mm("l0m_dnkv", [(dkv, w_ukv_t)], "nn", F32, 768, 256)
    d_ukv_t = _mm("l0m_dwukv", [(dkv, nkv)], "tn", BF16, 512, 256)
    dckv, d_kv_norm_g = _rmsnorm_bwd("l0m_dkvnorm", kvr, KV_RANK, 0, dnkv, kv_norm_g, t_all)
    dkvr = jnp.concatenate([dckv, dk_sum[:, QK_NOPE:QK_HEAD],
                            jnp.zeros((t_all, PA_KV_W - KV_RANK - QK_ROPE), F32)], axis=-1)
    dpb = _rope("l0m_dkrope", dkvr, PA_KV_W, 0, cos_k, sin_k, jnp.asarray(perm_k.T, BF16), True, F32)
    dproj_lat = jnp.concatenate([d_pool_x, jnp.zeros((n_lat, PA_CQ - POOL_DIM), F32), dcq, dpb[:n_lat]], axis=-1)
    dproj_ctx = jnp.concatenate([jnp.zeros((n_ctx, PA_KV), F32), dpb[n_lat:]], axis=-1)
    dproj = jnp.concatenate([dproj_lat, dproj_ctx], axis=0).astype(BF16)
    d_in_pad = _mm("l0m_dwin", [(dproj, ua)], "tn", BF16, 640, 512)
    d_in_t = jnp.concatenate([d_in_pad[:POOL_DIM], d_in_pad[PA_CQ:PA_CQ + Q_RANK],
                              d_in_pad[PA_KV:PA_KV + kv_rows]], axis=0)
    token = feed.grads("l0m", {"ab_out": d_ab_out, "uq": d_uq, "ukv_t": d_ukv_t, "in_t": d_in_t})
    dua = _mm("l0m_du", [(dproj, w_in_t)], "nn", F32, 768, 512, bias=_after(token))
    dh2_all = jnp.concatenate([dh2, jnp.zeros((n_ctx, D_MODEL), F32)], axis=0)
    ds1, (dsh_a, dsc_a, dgn_a) = _adaln_bwd("l0m_dadaln", s1, dua, dh2_all, mg0, 1, n_lat)
    ds0, g["f00"] = _ffn_half_bwd("l0f0", ds1, sv_f00, mg0, 0, feed, 0, 0.5, n_lat)

    dmod0 = _mod_grad([g["f00"], dict(shift=dsh_a, scale=dsc_a, gate=dgate_a), g["f01"]], 2)
    dmod1 = _mod_grad([g["f10"], dict(shift=dsh_c, scale=dsc_c, gate=dgate_c), g["f11"]], 1)
    d_norm_g = jnp.stack([
        jnp.concatenate([jnp.sum(g["f00"]["gain"], axis=0), jnp.sum(dgn_a, axis=0), g["f01"]["gain"][0]], axis=0),
        jnp.concatenate([g["f10"]["gain"][0], dgn_c[0], g["f11"]["gain"][0]], axis=0)])
    grads = dict(
        pool_w=d_pool_w, pool_scale=d_pool_scale, q_norm_g=d_q_norm_g[0], kv_norm_g=d_kv_norm_g[0],
        conv_w=d_conv_w, final_norm_g=d_final_g[0], norm_g=d_norm_g,
        mod_h=jnp.stack([dmod0[0], dmod1[0]]), mod_g=dmod0[1])
    return sq_cols, ds0, grads


HBM_SPEC = pl.BlockSpec(memory_space=pltpu.HBM)
SEM_SPEC = pl.BlockSpec(memory_space=pltpu.SEMAPHORE)
ANY_SPEC = pl.BlockSpec(memory_space=pl.ANY)
SIDE_EFFECT = pltpu.SideEffectType.DATAFLOW_SIDE_EFFECTING
N_PEERS = N_DEV - 1


def _mesh_place():
    mx, my, mc = lax.axis_index("x"), lax.axis_index("y"), lax.axis_index("c")
    return mx, my, mc, 4 * mx + 2 * my + mc


def _peer(place, kk):
    mx, my, mc, _ = place
    px = jnp.bitwise_xor(mx, (kk >> 2) & 1)
    py = jnp.bitwise_xor(my, (kk >> 1) & 1)
    pc = jnp.bitwise_xor(mc, kk & 1)
    return (px, py, pc), 4 * px + 2 * py + pc


def _hbm(a):
    return pltpu.with_memory_space_constraint(a, pltpu.HBM)


def _landing(block, me):
    zone = lax.empty((N_DEV,) + block.shape, block.dtype)
    return lax.dynamic_update_slice(zone, block[None], (me,) + (0,) * block.ndim)


def _exchange_start(name, srcs, lands, scatter, after):
    n = len(srcs)

    def body(*refs):
        src, land = refs[:n], refs[n:2 * n]
        send_sems, recv_sems, token = refs[2 * n + 1], refs[2 * n + 2], refs[-1]
        place = _mesh_place()
        for a in range(n):
            for kk in range(1, N_DEV):
                dev, peer = _peer(place, kk)
                pltpu.make_async_remote_copy(
                    src_ref=src[a].at[peer] if scatter else src[a], dst_ref=land[a].at[place[3]],
                    send_sem=send_sems.at[a * N_PEERS + kk - 1], recv_sem=recv_sems.at[a * N_PEERS + kk - 1],
                    device_id=dev, device_id_type=MESH).start()
        token[...] = jnp.zeros_like(token)

    thru = [pltpu.HBM(t.shape, t.dtype) for t in (*srcs, *lands)]
    res = pl.pallas_call(
        body, name=name,
        out_shape=(pltpu.SemaphoreType.DMA((n * N_PEERS,)), pltpu.SemaphoreType.DMA((n * N_PEERS,)), *thru,
                   SDS((8, 128), F32)),
        in_specs=[HBM_SPEC] * (2 * n) + [ANY_SPEC],
        out_specs=(SEM_SPEC, SEM_SPEC, *([HBM_SPEC] * (2 * n)), pl.BlockSpec(memory_space=pltpu.VMEM)),
        input_output_aliases={i: 2 + i for i in range(2 * n)},
        compiler_params=pltpu.CompilerParams(has_side_effects=SIDE_EFFECT),
    )(*[_hbm(s) for s in srcs], *[_hbm(t) for t in lands], after)
    return res[0], res[1], list(res[2:2 + n]), list(res[2 + n:2 + 2 * n]), res[-1]


def _exchange_wait(name, send_sems, recv_sems, srcs, lands, places, scatter, after):
    n = len(srcs)

    def body(*refs):
        src, land = refs[:n], refs[n:2 * n]
        send, recv = refs[2 * n], refs[2 * n + 1]
        place = _mesh_place()
        for a in range(n):
            for kk in range(1, N_DEV):
                dev, peer = _peer(place, kk)
                cp = pltpu.make_async_remote_copy(
                    src_ref=src[a].at[peer] if scatter else src[a], dst_ref=land[a].at[peer],
                    send_sem=send.at[places[a] * N_PEERS + kk - 1], recv_sem=recv.at[places[a] * N_PEERS + kk - 1],
                    device_id=dev, device_id_type=MESH)
                cp.wait_send()
                cp.wait_recv()

    thru = [pltpu.HBM(t.shape, t.dtype) for t in (*srcs, *lands)]
    res = pl.pallas_call(
        body, name=name, out_shape=tuple(thru),
        in_specs=[HBM_SPEC] * (2 * n) + [SEM_SPEC, SEM_SPEC, ANY_SPEC], out_specs=tuple([HBM_SPEC] * (2 * n)),
        input_output_aliases={i: i for i in range(2 * n)},
        compiler_params=pltpu.CompilerParams(has_side_effects=SIDE_EFFECT),
    )(*srcs, *lands, send_sems, recv_sems, after)
    return list(res[n:])


class _Feed:
    def __init__(self, shards, me, after):
        self.names, self.me = list(shards), me
        srcs = [shards[nm] for nm in self.names]
        lands = [_landing(s, me) for s in srcs]
        self.send, self.recv, self.srcs, self.lands, self.token = _exchange_start(
            "gather_start", srcs, lands, False, after)
        self.pending = []

    def start_token(self):
        return self.token[0, 0]

    def weights(self, tag, names, after):
        places = [self.names.index(nm) for nm in names]
        got = _exchange_wait(f"gather_wait_{tag}", self.send, self.recv, [self.srcs[i] for i in places],
                             [self.lands[i] for i in places], places, False, after)
        return [t.reshape((N_DEV * t.shape[1],) + t.shape[2:]) for t in got]

    def grads(self, tag, full):
        names = list(full)
        srcs = [full[nm].reshape((N_DEV, full[nm].shape[0] // N_DEV) + full[nm].shape[1:]) for nm in names]
        lands = [_landing(lax.dynamic_index_in_dim(s, self.me, 0, keepdims=False), self.me) for s in srcs]
        send, recv, srcs, lands, token = _exchange_start(f"scatter_start_{tag}", srcs, lands, True, srcs[0])
        self.pending.append((tag, names, send, recv, srcs, lands))
        return token[0, 0]

    def finish(self, after):
        out = {}
        for tag, names, send, recv, srcs, lands in self.pending:
            got = _exchange_wait(f"scatter_wait_{tag}", send, recv, srcs, lands, list(range(len(names))), True, after)
            for nm, slots in zip(names, got):
                out[nm] = _sum_slots(f"reduce_{nm}", slots)
        return out


def _adam_all(names, weights, grads, moms, vels):
    deltas, new_m, new_v = [], [], []
    for nm, w, g, m, v in zip(names, weights, grads, moms, vels):
        d, a, b = _adamw(f"adamw_{nm}", w, g.reshape(w.shape), m, v)
        deltas.append(d)
        new_m.append(a)
        new_v.append(b)
    return deltas, new_m, new_v


WEIGHT_NAMES = ("c_ctx", "norm_g", "w_mod", "b_mod", "ffn_w_gate", "ffn_w_up", "ffn_w_down", "ab_w_in", "pool_w",
                "pool_scale", "q_norm_g", "w_uq", "kv_norm_g", "w_ukv", "ab_w_out", "conv_w_in", "conv_w",
                "conv_w_out", "final_norm_g")


def kernel(x, c, ctx, c_ctx, norm_g, w_mod, b_mod, ffn_w_gate, ffn_w_up, ffn_w_down, ab_w_in, pool_w, pool_scale, q_norm_g, w_uq, kv_norm_g, w_ukv, ab_w_out, conv_w_in, conv_w, conv_w_out, final_norm_g, loss_target, m_c_ctx, m_norm_g, m_w_mod, m_b_mod, m_ffn_w_gate, m_ffn_w_up, m_ffn_w_down, m_ab_w_in, m_pool_w, m_pool_scale, m_q_norm_g, m_w_uq, m_kv_norm_g, m_w_ukv, m_ab_w_out, m_conv_w_in, m_conv_w, m_conv_w_out, m_final_norm_g, v_c_ctx, v_norm_g, v_w_mod, v_b_mod, v_ffn_w_gate, v_ffn_w_up, v_ffn_w_down, v_ab_w_in, v_pool_w, v_pool_scale, v_q_norm_g, v_w_uq, v_kv_norm_g, v_w_ukv, v_ab_w_out, v_conv_w_in, v_conv_w, v_conv_w_out, v_final_norm_g):
    weights = (c_ctx, norm_g, w_mod, b_mod, ffn_w_gate, ffn_w_up, ffn_w_down, ab_w_in, pool_w, pool_scale, q_norm_g,
               w_uq, kv_norm_g, w_ukv, ab_w_out, conv_w_in, conv_w, conv_w_out, final_norm_g)
    moms = (m_c_ctx, m_norm_g, m_w_mod, m_b_mod, m_ffn_w_gate, m_ffn_w_up, m_ffn_w_down, m_ab_w_in, m_pool_w,
            m_pool_scale, m_q_norm_g, m_w_uq, m_kv_norm_g, m_w_ukv, m_ab_w_out, m_conv_w_in, m_conv_w, m_conv_w_out,
            m_final_norm_g)
    vels = (v_c_ctx, v_norm_g, v_w_mod, v_b_mod, v_ffn_w_gate, v_ffn_w_up, v_ffn_w_down, v_ab_w_in, v_pool_w,
            v_pool_scale, v_q_norm_g, v_w_uq, v_kv_norm_g, v_w_ukv, v_ab_w_out, v_conv_w_in, v_conv_w, v_conv_w_out,
            v_final_norm_g)
    me = 4 * lax.axis_index("x") + 2 * lax.axis_index("y") + lax.axis_index("c")
    n_lat, n_ctx = x.shape[1], ctx.shape[1]
    d = D_MODEL
    mod_cols = w_mod.shape[-1]
    ng_sh, cw_sh = norm_g.shape[-1], conv_w.shape[-1]

    small = jnp.concatenate([c.reshape(-1), norm_g.reshape(-1), conv_w.reshape(-1)])
    small_n = -(-small.shape[0] // 1024) * 1024
    small = jnp.pad(small, (0, small_n - small.shape[0])).reshape(small_n // 128, 128)
    small_all = _exchange("gather_small", small, False).reshape(N_DEV, small_n)
    c_all = small_all[:, :d]
    o1 = d + 6 * ng_sh
    norm_g_full = small_all[:, d:o1].reshape(N_DEV, 2, 3, ng_sh).transpose(1, 2, 0, 3).reshape(2, 3, d)
    conv_w_full = small_all[:, o1:o1 + 3 * cw_sh].reshape(N_DEV, 3, cw_sh).transpose(1, 0, 2).reshape(3, d)

    cond = jnp.concatenate([c_all, jnp.broadcast_to(c_ctx[None, :], (N_DEV, d))], axis=0)
    sil, dsil = _silu_rows("mod_silu", cond)
    w_mod_b = w_mod.astype(BF16)
    b_sh = lax.dynamic_slice(b_mod, (0, me * mod_cols), (2, mod_cols))
    m_part = jnp.stack([_mm(f"mod_fwd{l}", [(sil, w_mod_b[l])], "nn", F32, 16, 384, bias=b_sh[l:l + 1])
                        for l in range(2)], axis=1)
    m_all = _exchange("gather_mod", m_part.reshape(-1, 128), False).reshape(N_DEV, 2 * N_DEV, 2, mod_cols)
    m_mine = lax.dynamic_index_in_dim(m_all, me, axis=1, keepdims=False)
    mod_h = m_mine.transpose(1, 0, 2).reshape(2, N_MOD, d)
    mod_g = m_all[:, N_DEV, 0, :].reshape(N_MOD, d)

    def ffn_shards(i):
        return {f"gate_t{i}": ffn_w_gate[i // 2, i % 2].T, f"up_t{i}": ffn_w_up[i // 2, i % 2].T,
                f"down{i}": ffn_w_down[i // 2, i % 2]}

    local = {**ffn_shards(0), "in_t": ab_w_in[0].T, "uq": w_uq[0], "ukv_t": w_ukv[0].T, "ab_out": ab_w_out[0],
             **ffn_shards(1), **ffn_shards(2), "cin_t": conv_w_in[0].T, "c_out": conv_w_out[0], **ffn_shards(3)}
    feed = _Feed({nm: a.astype(BF16) for nm, a in local.items()}, me, m_all)

    sq_cols, ds0, g = _local_step(x[0], ctx[0], loss_target[0], mod_h, mod_g, norm_g_full, feed, pool_w[0],
                                  pool_scale, q_norm_g, kv_norm_g, conv_w_full, final_norm_g)
    grad_x = ds0[:n_lat]
    loss = lax.psum(0.5 * jnp.sum(sq_cols) / d, ("x", "y", "c"))
    red = feed.finish(ds0)

    dm = jnp.stack([g["mod_h"], jnp.stack([g["mod_g"], jnp.zeros_like(g["mod_g"])])])
    dm_all = _exchange("gather_dmod", dm.reshape(-1, 128), False).reshape(N_DEV, 2, 2, N_MOD * d)
    grad_b_mod = _sum_rows("dmod_bias", dm_all.reshape(2 * N_DEV, 2 * N_MOD * d)).reshape(2, N_MOD * d)
    dm_sh = lax.dynamic_slice(dm_all, (0, 0, 0, me * mod_cols), (N_DEV, 2, 2, mod_cols))
    gw_mod, cctx_parts = [], []
    for l in range(2):
        dm_l = dm_sh[:, :, l, :].transpose(1, 0, 2).reshape(2 * N_DEV, mod_cols).astype(BF16)
        gw_mod.append(_mm(f"mod_dw{l}", [(sil, dm_l)], "tn", F32, 512, 384))
        dm_ctx = jnp.concatenate([dm_l[N_DEV:], jnp.zeros((N_DEV, mod_cols), BF16)], axis=0)
        cctx_parts.append(_mm(f"mod_dcond{l}", [(dm_ctx, w_mod_b[l])], "nt", F32, 16, 512))
    grad_w_mod = jnp.stack(gw_mod)
    cctx_part = _sum_rows("mod_dcond_sum", jnp.concatenate(cctx_parts, axis=0))

    small_g = jnp.concatenate([g["pool_w"].reshape(-1), g["pool_scale"].reshape(-1), g["q_norm_g"].reshape(-1),
                               g["kv_norm_g"].reshape(-1), g["final_norm_g"].reshape(-1), g["norm_g"].reshape(-1),
                               g["conv_w"].reshape(-1), cctx_part.reshape(-1)])
    sizes = [pool_w.size, pool_scale.size, q_norm_g.size, kv_norm_g.size, d, 6 * d, 3 * d, d]
    sg_n = -(-small_g.shape[0] // 1024) * 1024
    small_g = jnp.pad(small_g, (0, sg_n - small_g.shape[0]))
    sg_all = _exchange("gather_small_grads", small_g.reshape(-1, 128), False).reshape(N_DEV, sg_n)
    scale_vec = jnp.concatenate([jnp.ones((1, sum(sizes[:-1])), F32), dsil[N_DEV:N_DEV + 1],
                                 jnp.ones((1, sg_n - sum(sizes)), F32)], axis=1)
    sg = _sum_rows("small_grads_sum", sg_all, scale_vec)[0]
    cuts, pos = [], 0
    for sz in sizes:
        cuts.append(sg[pos:pos + sz])
        pos += sz
    g_pool_w, g_pool_scale, g_q_norm, g_kv_norm, g_final, g_norm_full, g_conv_full, g_c_ctx = cuts
    grad_norm_g = lax.dynamic_slice(g_norm_full.reshape(2, 3, d), (0, 0, me * ng_sh), (2, 3, ng_sh))
    grad_conv_w = lax.dynamic_slice(g_conv_full.reshape(3, d), (0, me * cw_sh), (3, cw_sh))[None]

    def mine(nm):
        return red[nm]

    grad_gate = jnp.stack([mine(f"gate_t{i}").T for i in range(4)]).reshape(ffn_w_gate.shape)
    grad_up = jnp.stack([mine(f"up_t{i}").T for i in range(4)]).reshape(ffn_w_up.shape)
    grad_down = jnp.stack([mine(f"down{i}") for i in range(4)]).reshape(ffn_w_down.shape)
    grads = (g_c_ctx, grad_norm_g, grad_w_mod, grad_b_mod, grad_gate, grad_up, grad_down, mine("in_t").T[None],
             g_pool_w.reshape(pool_w.shape), g_pool_scale.reshape(pool_scale.shape), g_q_norm.reshape(q_norm_g.shape),
             mine("uq")[None], g_kv_norm.reshape(kv_norm_g.shape), mine("ukv_t").T[None], mine("ab_out")[None],
             mine("cin_t").T[None], grad_conv_w, mine("c_out")[None], g_final)
    grads = tuple(gr.reshape(w.shape) for gr, w in zip(grads, weights))
    deltas, new_m, new_v = _adam_all(WEIGHT_NAMES, weights, grads, moms, vels)
    return (loss, grad_x[None], *grads, *deltas, *new_m, *new_v)
```

```python
import functools
import math

import jax
import jax.numpy as jnp
import numpy as np
from jax import lax
from jax.experimental import pallas as pl
from jax.experimental.pallas import tpu as pltpu

F32 = jnp.float32
BF16 = jnp.bfloat16
MESH = pl.DeviceIdType.MESH
SDS = jax.ShapeDtypeStruct

N_DEV = 8
D_MODEL = 1024
N_MOD = 9
D_FF = 2816
POOL_WINDOWS = (2, 4, 8, 16)
POOL_DIM = 512
POOL_GROUP_DIM = 128
HEADS = 8
QK_NOPE = 64
QK_ROPE = 32
QK_HEAD = QK_NOPE + QK_ROPE
V_HEAD = 64
Q_RANK = 768
KV_RANK = 256
GRID_W = 64
ROPE_THETA = 10000.0
RMS_EPS = 1e-6
ATTN_SCALE = 1.0 / math.sqrt(QK_HEAD)
HEAD_PAD = 128
POOL_PAD = 16
PA_POOL, PA_CQ, PA_KV = 0, 768, 1536
PA_KV_W = 384
PA_W = PA_KV + PA_KV_W

ADAM_LR, ADAM_B1, ADAM_B2, ADAM_EPS, ADAM_WD, ADAM_STEP = 0.001, 0.9, 0.999, 1e-08, 0.01, 10

VMEM_LIMIT_BYTES = 56 * 1024 * 1024

NN = ((1,), (0,))
NT = ((1,), (1,))
TN = ((0,), (0,))


def _cparams():
    return pltpu.CompilerParams(vmem_limit_bytes=VMEM_LIMIT_BYTES)


def _dot(a, b, dims):
    return lax.dot_general(a, b, (dims, ((), ())), preferred_element_type=F32)


def _tile(n, cap, mult=8):
    t = (min(cap, n) // mult) * mult
    while t >= mult:
        if n % t == 0:
            return t
        t -= mult
    return n


def _colsum(x):
    return jnp.sum(x, axis=0, keepdims=True)


def _rms(x):
    r = lax.rsqrt(jnp.mean(x * x, axis=-1, keepdims=True) + RMS_EPS)
    return x * r, r


def _rms_bwd(n, r, dn):
    return r * (dn - n * jnp.mean(dn * n, axis=-1, keepdims=True))


def _rowwise(name, fn, t_rows, tm, n_lat, rows, vecs, outs, accs):
    nt = t_rows // tm
    nlt = n_lat // tm
    n_groups = 2 if nlt < nt else 1

    def grp(i):
        return jnp.where(i >= nlt, 1, 0) if n_groups == 2 else 0

    in_specs = [pl.BlockSpec((tm, w), functools.partial(lambda i, cb: (i, cb), cb=cb)) for (_, w, cb) in rows]
    in_specs += [pl.BlockSpec((1,) + v.shape[1:], lambda i: (grp(i), 0, 0)) for v in vecs]
    out_specs = [pl.BlockSpec((tm, w), lambda i: (i, 0)) for (w, _) in outs]
    out_specs += [pl.BlockSpec((1, 1, w), lambda i: (grp(i), 0, 0)) for w in accs]
    out_shape = [SDS((t_rows, w), dt) for (w, dt) in outs] + [SDS((n_groups, 1, w), F32) for w in accs]
    n_r, n_v, n_o = len(rows), len(vecs), len(outs)

    def body(*refs):
        row_vals = [r[...] for r in refs[:n_r]]
        vec_vals = [v[0] for v in refs[n_r:n_r + n_v]]
        out_refs = refs[n_r + n_v:n_r + n_v + n_o]
        acc_refs = refs[n_r + n_v + n_o:]
        out_vals, acc_vals = fn(row_vals, vec_vals)
        for o_ref, o in zip(out_refs, out_vals):
            o_ref[...] = o.astype(o_ref.dtype)
        if acc_refs:
            i = pl.program_id(0)
            first = (i == 0) | (i == nlt) if n_groups == 2 else i == 0

            @pl.when(first)
            def _():
                for a_ref, a in zip(acc_refs, acc_vals):
                    a_ref[0] = a

            @pl.when(jnp.logical_not(first))
            def _():
                for a_ref, a in zip(acc_refs, acc_vals):
                    a_ref[0] += a

    res = pl.pallas_call(
        body, name=name, grid=(nt,), in_specs=in_specs, out_specs=out_specs, out_shape=out_shape,
        compiler_params=_cparams(),
    )(*[r[0] for r in rows], *vecs)
    return res[:n_o], res[n_o:]


RESIDENT_BYTES = 12 * 1024 * 1024


def _mm(name, pairs, mode, out_dtype, tm_cap=256, tn_cap=512, bias=None):
    a0, b0 = pairs[0]
    if mode == "nn":
        m, n, dims = a0.shape[0], b0.shape[1], NN
    elif mode == "nt":
        m, n, dims = a0.shape[0], b0.shape[0], NT
    else:
        m, n, dims = a0.shape[1], b0.shape[1], TN
    b_bytes = sum(b.size * b.dtype.itemsize for _, b in pairs)
    tn = n if b_bytes <= RESIDENT_BYTES else _tile(n, tn_cap, 128)
    tm = _tile(m, tm_cap, 128 if mode == "tn" else 16)

    def a_spec(a):
        if mode == "tn":
            return pl.BlockSpec((a.shape[0], tm), lambda i, j: (0, i))
        return pl.BlockSpec((tm, a.shape[1]), lambda i, j: (i, 0))

    def b_spec(b):
        if mode == "nt":
            return pl.BlockSpec((tn, b.shape[1]), lambda i, j: (j, 0))
        return pl.BlockSpec((b.shape[0], tn), lambda i, j: (0, j))

    in_specs, flat = [], []
    for a, b in pairs:
        in_specs += [a_spec(a), b_spec(b)]
        flat += [a, b]
    if bias is not None:
        in_specs.append(pl.BlockSpec((1, tn), lambda i, j: (0, j)))
        flat.append(bias)
    n_pairs = len(pairs)

    def body(*refs):
        acc = None
        for p in range(n_pairs):
            t = _dot(refs[2 * p][...], refs[2 * p + 1][...], dims)
            acc = t if acc is None else acc + t
        if bias is not None:
            acc = acc + refs[2 * n_pairs][...]
        refs[-1][...] = acc.astype(refs[-1].dtype)

    return pl.pallas_call(
        body, name=name, grid=(m // tm, n // tn), in_specs=in_specs,
        out_specs=pl.BlockSpec((tm, tn), lambda i, j: (i, j)),
        out_shape=SDS((m, n), out_dtype), compiler_params=_cparams(),
    )(*flat)


def _mm_resid(name, a, b, s, mg, k, coef, n_lat):
    t_rows, n = a.shape[0], b.shape[1]
    tm = _tile(math.gcd(n_lat, t_rows), 256, 16)
    nlt = n_lat // tm
    n_groups = 2 if nlt < t_rows // tm else 1

    def grp(i):
        return jnp.where(i >= nlt, 1, 0) if n_groups == 2 else 0

    def body(a_ref, b_ref, s_ref, mg_ref, so_ref, o_ref):
        o = _dot(a_ref[...], b_ref[...], NN)
        gate = mg_ref[0, 3 * k + 2:3 * k + 3, :]
        o_ref[...] = o
        so_ref[...] = s_ref[...] + (coef * gate) * o

    row = pl.BlockSpec((tm, n), lambda i: (i, 0))
    return pl.pallas_call(
        body, name=name, grid=(t_rows // tm,),
        in_specs=[pl.BlockSpec((tm, a.shape[1]), lambda i: (i, 0)), pl.BlockSpec(b.shape, lambda i: (0, 0)), row,
                  pl.BlockSpec((1, mg.shape[1], n), lambda i: (grp(i), 0, 0))],
        out_specs=[row, row], out_shape=[SDS((t_rows, n), F32), SDS((t_rows, n), F32)], compiler_params=_cparams(),
    )(a, b, s, mg)


def _ffn_up(name, u, wg_t, wu_t):
    t_rows, f = u.shape[0], wg_t.shape[0]
    tm = _tile(t_rows, 256, 16)

    def body(u_ref, wg_ref, wu_ref, a_ref, b_ref, h_ref):
        uu = u_ref[...]
        a = _dot(uu, wg_ref[...], NT)
        b = _dot(uu, wu_ref[...], NT)
        a_ref[...] = a.astype(BF16)
        b_ref[...] = b.astype(BF16)
        h_ref[...] = (a * jax.nn.sigmoid(a) * b).astype(BF16)

    w_spec = pl.BlockSpec(wg_t.shape, lambda i: (0, 0))
    o_spec = pl.BlockSpec((tm, f), lambda i: (i, 0))
    return pl.pallas_call(
        body, name=name, grid=(t_rows // tm,),
        in_specs=[pl.BlockSpec((tm, u.shape[1]), lambda i: (i, 0)), w_spec, w_spec],
        out_specs=[o_spec, o_spec, o_spec], out_shape=[SDS((t_rows, f), BF16)] * 3, compiler_params=_cparams(),
    )(u, wg_t, wu_t)


def _ffn_dact(name, do, wd, a, b):
    t_rows, f = do.shape[0], wd.shape[0]
    tm = _tile(t_rows, 256, 16)

    def body(do_ref, wd_ref, a_ref, b_ref, da_ref, db_ref):
        dh = _dot(do_ref[...], wd_ref[...], NT)
        av = a_ref[...].astype(F32)
        bv = b_ref[...].astype(F32)
        sg = jax.nn.sigmoid(av)
        da_ref[...] = (dh * bv * (sg * (1.0 + av * (1.0 - sg)))).astype(BF16)
        db_ref[...] = (dh * (av * sg)).astype(BF16)

    t_spec = pl.BlockSpec((tm, f), lambda i: (i, 0))
    return pl.pallas_call(
        body, name=name, grid=(t_rows // tm,),
        in_specs=[pl.BlockSpec((tm, do.shape[1]), lambda i: (i, 0)), pl.BlockSpec(wd.shape, lambda i: (0, 0)),
                  t_spec, t_spec],
        out_specs=[t_spec, t_spec], out_shape=[SDS((t_rows, f), BF16)] * 2, compiler_params=_cparams(),
    )(do, wd, a, b)


def _row_tm(t_rows, n_lat):
    return _tile(math.gcd(t_rows, n_lat), 256, 16)


def _adaln_fwd(name, s, mg, k, n_lat):
    t_rows = s.shape[0]

    def fn(rv, vv):
        m = vv[0]
        n, _ = _rms(rv[0])
        u = (n * m[9 + k:10 + k]) * (1.0 + m[3 * k + 1:3 * k + 2]) + m[3 * k:3 * k + 1]
        return [u], []

    (u,), _ = _rowwise(name, fn, t_rows, _row_tm(t_rows, n_lat), n_lat, [(s, D_MODEL, 0)], [mg], [(D_MODEL, BF16)], [])
    return u


def _adaln_bwd(name, s, du, ds_out, mg, k, n_lat):
    t_rows = s.shape[0]

    def fn(rv, vv):
        m = vv[0]
        gain, scale = m[9 + k:10 + k], m[3 * k + 1:3 * k + 2]
        n, r = _rms(rv[0])
        d_u = rv[1]
        dxn = d_u * (1.0 + scale)
        ds = _rms_bwd(n, r, dxn * gain)
        return [rv[2] + ds], [_colsum(d_u), _colsum(d_u * (n * gain)), _colsum(dxn * n)]

    (ds_in,), accs = _rowwise(name, fn, t_rows, _row_tm(t_rows, n_lat), n_lat,
                              [(s, D_MODEL, 0), (du, D_MODEL, 0), (ds_out, D_MODEL, 0)], [mg],
                              [(D_MODEL, F32)], [D_MODEL] * 3)
    return ds_in, accs


def _gate_bwd(name, ds_out, o, mg, k, coef, n_lat):
    t_rows = o.shape[0]

    def fn(rv, vv):
        gate = vv[0][3 * k + 2:3 * k + 3]
        d = coef * rv[0]
        return [d * gate], [_colsum(d * rv[1])]

    (do,), (dgate,) = _rowwise(name, fn, t_rows, _row_tm(t_rows, n_lat), n_lat,
                               [(ds_out, D_MODEL, 0), (o, D_MODEL, 0)], [mg], [(D_MODEL, BF16)], [D_MODEL])
    return do, dgate


def _rmsnorm_fwd(name, x, width, colblk, gain, t_rows):
    def fn(rv, vv):
        n, _ = _rms(rv[0])
        return [n * vv[0]], []

    (y,), _ = _rowwise(name, fn, t_rows, _tile(t_rows, 256, 16), t_rows, [(x, width, colblk)],
                       [gain.reshape(1, 1, width)], [(width, BF16)], [])
    return y


def _rmsnorm_bwd(name, x, width, colblk, dy, gain, t_rows):
    def fn(rv, vv):
        n, r = _rms(rv[0])
        return [_rms_bwd(n, r, rv[1] * vv[0])], [_colsum(rv[1] * n)]

    (dx,), (dgain,) = _rowwise(name, fn, t_rows, _tile(t_rows, 256, 16), t_rows,
                               [(x, width, colblk), (dy, width, 0)], [gain.reshape(1, 1, width)],
                               [(width, F32)], [width])
    return dx, dgain


def _final_loss(name, h, target, gain):
    t_rows = h.shape[0]
    inv_d = 1.0 / D_MODEL

    def fn(rv, vv):
        g = vv[0]
        n, r = _rms(rv[0])
        e = n * g - rv[1]
        dy = e * inv_d
        return [_rms_bwd(n, r, dy * g)], [_colsum(e * e), _colsum(dy * n)]

    (dh,), (sq, dgain) = _rowwise(name, fn, t_rows, _tile(t_rows, 256, 16), t_rows,
                                  [(h, D_MODEL, 0), (target, D_MODEL, 0)], [gain.reshape(1, 1, D_MODEL)],
                                  [(D_MODEL, F32)], [D_MODEL, D_MODEL])
    return dh, sq, dgain


def _rope(name, z, width, colblk, cos, sin, perm, backward, out_dtype):
    t_rows = cos.shape[0]

    def body(z_ref, c_ref, s_ref, p_ref, o_ref):
        zz = z_ref[...]
        pre = zz * s_ref[...] if backward else zz
        hi = pre.astype(BF16)
        lo = (pre - hi.astype(F32)).astype(BF16)
        rot = _dot(hi, p_ref[...], NN) + _dot(lo, p_ref[...], NN)
        if not backward:
            rot = rot * s_ref[...]
        o_ref[...] = (zz * c_ref[...] + rot).astype(o_ref.dtype)

    tm = _tile(t_rows, 256, 16)
    t_spec = pl.BlockSpec((tm, width), lambda i: (i, 0))
    return pl.pallas_call(
        body, name=name, grid=(t_rows // tm,),
        in_specs=[pl.BlockSpec((tm, width), lambda i: (i, colblk)), t_spec, t_spec,
                  pl.BlockSpec((width, width), lambda i: (0, 0))],
        out_specs=t_spec, out_shape=SDS((t_rows, width), out_dtype), compiler_params=_cparams(),
    )(z, cos, sin, perm)


def _window_sum(x, w, transposed):
    n_rows = x.shape[0]
    zeros = jnp.zeros((POOL_PAD, x.shape[1]), F32)
    y = jnp.concatenate([zeros, x, zeros], axis=0)
    total = n_rows + 2 * POOL_PAD
    if transposed:
        y = y + pltpu.roll(y, total - 1, 0)
    else:
        y = y + pltpu.roll(y, 1, 0)
    step = 1
    while 2 * step < w:
        y = pltpu.roll(y, step, 0) + pltpu.roll(y, total - step, 0)
        step *= 2
    return y[POOL_PAD:POOL_PAD + n_rows]


def _window_count(n_rows, w):
    t = lax.broadcasted_iota(jnp.int32, (n_rows, 1), 0)
    lo = jnp.maximum(t - w // 2, 0)
    hi = jnp.minimum(t + (w - w // 2 - 1), n_rows - 1)
    return (hi - lo + 1).astype(F32)


def _pool_fwd(name, proj, n_rows, w_grp, scale):
    def body(x_ref, w_ref, sc_ref, y_ref, p_ref):
        for g, w in enumerate(POOL_WINDOWS):
            cols = slice(g * POOL_GROUP_DIM, (g + 1) * POOL_GROUP_DIM)
            x = x_ref[:, cols]
            p = _window_sum(x, w, False) * (1.0 / _window_count(n_rows, w)) - x
            pb = p.astype(BF16)
            p_ref[:, cols] = pb
            y_ref[:, cols] = (_dot(pb, w_ref[g], NN) * sc_ref[:, cols]).astype(BF16)

    blk = pl.BlockSpec((n_rows, POOL_DIM), lambda i: (0, 0))
    return pl.pallas_call(
        body, name=name, grid=(1,),
        in_specs=[blk, pl.BlockSpec(w_grp.shape, lambda i: (0, 0, 0)), pl.BlockSpec((1, POOL_DIM), lambda i: (0, 0))],
        out_specs=[blk, blk], out_shape=[SDS((n_rows, POOL_DIM), BF16)] * 2, compiler_params=_cparams(),
    )(proj, w_grp, scale)


def _pool_bwd(name, dcat, n_rows, p, w_grp, scale):
    def body(dy_ref, p_ref, w_ref, sc_ref, dx_ref, dw_ref, dsc_ref):
        for g, w in enumerate(POOL_WINDOWS):
            cols = slice(g * POOL_GROUP_DIM, (g + 1) * POOL_GROUP_DIM)
            dy = dy_ref[:, cols]
            pb = p_ref[:, cols]
            pw = _dot(pb, w_ref[g], NN)
            dsc_ref[:, cols] = _colsum(dy * pw)
            dpw = (dy * sc_ref[:, cols]).astype(BF16)
            dw_ref[g] = _dot(pb, dpw, TN)
            dp = _dot(dpw, w_ref[g], NT)
            dx_ref[:, cols] = _window_sum(dp * (1.0 / _window_count(n_rows, w)), w, True) - dp

    blk = pl.BlockSpec((n_rows, POOL_DIM), lambda i: (0, 0))
    w_spec = pl.BlockSpec(w_grp.shape, lambda i: (0, 0, 0))
    v_spec = pl.BlockSpec((1, POOL_DIM), lambda i: (0, 0))
    return pl.pallas_call(
        body, name=name, grid=(1,), in_specs=[blk, blk, w_spec, v_spec], out_specs=[blk, w_spec, v_spec],
        out_shape=[SDS((n_rows, POOL_DIM), F32), SDS(w_grp.shape, F32), SDS((1, POOL_DIM), F32)],
        compiler_params=_cparams(),
    )(dcat, p, w_grp, scale)


def _attn_fwd(name, q, k, v):
    h, n_q, _ = q.shape
    n_k = k.shape[1]
    tq = _tile(n_q, 256, 16)

    def body(q_ref, k_ref, v_ref, o_ref, lse_ref):
        s = _dot(q_ref[...], k_ref[...], NT) * ATTN_SCALE
        m = jnp.max(s, axis=-1, keepdims=True)
        e = jnp.exp(s - m)
        l = jnp.sum(e, axis=-1, keepdims=True)
        p = (e * (1.0 / l)).astype(BF16)
        o_ref[...] = _dot(p, v_ref[...], NN).astype(BF16)
        lse_ref[...] = m + jnp.log(l)

    return pl.pallas_call(
        body, name=name, grid=(h, n_q // tq),
        in_specs=[pl.BlockSpec((None, tq, HEAD_PAD), lambda hh, i: (hh, i, 0)),
                  pl.BlockSpec((None, n_k, HEAD_PAD), lambda hh, i: (hh, 0, 0)),
                  pl.BlockSpec((None, n_k, V_HEAD), lambda hh, i: (hh, 0, 0))],
        out_specs=[pl.BlockSpec((None, tq, V_HEAD), lambda hh, i: (hh, i, 0)),
                   pl.BlockSpec((None, tq, 1), lambda hh, i: (hh, i, 0))],
        out_shape=[SDS((h, n_q, V_HEAD), BF16), SDS((h, n_q, 1), F32)], compiler_params=_cparams(),
    )(q, k, v)


def _attn_bwd(name, q, k, v, o, lse, do):
    h, n_q, _ = q.shape
    n_k = k.shape[1]
    tq = _tile(n_q, 256, 16)

    def body(q_ref, k_ref, v_ref, o_ref, lse_ref, do_ref, dq_ref, dk_ref, dv_ref, dks_ref):
        hh, i = pl.program_id(0), pl.program_id(1)
        qq, kk, dd = q_ref[...], k_ref[...], do_ref[...]
        s = _dot(qq, kk, NT) * ATTN_SCALE
        p = jnp.exp(s - lse_ref[...])
        dp = _dot(dd, v_ref[...], NT)
        delta = jnp.sum(dd.astype(F32) * o_ref[...].astype(F32), axis=-1, keepdims=True)
        ds = (p * (dp - delta) * ATTN_SCALE).astype(BF16)
        dq_ref[...] = _dot(ds, kk, NN)
        dk = _dot(ds, qq, TN)
        dv = _dot(p.astype(BF16), dd, TN)

        @pl.when(i == 0)
        def _():
            dk_ref[...] = dk
            dv_ref[...] = dv

        @pl.when(i > 0)
        def _():
            dk_ref[...] += dk
            dv_ref[...] += dv

        @pl.when((i == 0) & (hh == 0))
        def _():
            dks_ref[...] = dk

        @pl.when((i > 0) | (hh > 0))
        def _():
            dks_ref[...] += dk

    q_spec = pl.BlockSpec((None, tq, HEAD_PAD), lambda hh, i: (hh, i, 0))
    k_spec = pl.BlockSpec((None, n_k, HEAD_PAD), lambda hh, i: (hh, 0, 0))
    v_spec = pl.BlockSpec((None, n_k, V_HEAD), lambda hh, i: (hh, 0, 0))
    o_spec = pl.BlockSpec((None, tq, V_HEAD), lambda hh, i: (hh, i, 0))
    return pl.pallas_call(
        body, name=name, grid=(h, n_q // tq),
        in_specs=[q_spec, k_spec, v_spec, o_spec, pl.BlockSpec((None, tq, 1), lambda hh, i: (hh, i, 0)), o_spec],
        out_specs=[q_spec, k_spec, v_spec, pl.BlockSpec((n_k, HEAD_PAD), lambda hh, i: (0, 0))],
        out_shape=[SDS((h, n_q, HEAD_PAD), F32), SDS((h, n_k, HEAD_PAD), F32), SDS((h, n_k, V_HEAD), F32),
                   SDS((n_k, HEAD_PAD), F32)],
        compiler_params=_cparams(),
    )(q, k, v, o, lse, do)


CONV_COLS = 256


def _shift_rows(x, d):
    n_rows = x.shape[0]
    t = lax.broadcasted_iota(jnp.int32, (n_rows, 1), 0)
    if d > 0:
        return jnp.where(t >= d, pltpu.roll(x, d, 0), 0.0)
    return jnp.where(t < n_rows + d, pltpu.roll(x, n_rows + d, 0), 0.0)


def _conv_fwd(name, z3, conv_w):
    n_rows = z3.shape[0]
    nb = D_MODEL // CONV_COLS

    def body(b_ref, c_ref, v_ref, w_ref, y_ref):
        z = c_ref[...] * v_ref[...]
        zc = w_ref[0:1, :] * _shift_rows(z, 1) + w_ref[1:2, :] * z + w_ref[2:3, :] * _shift_rows(z, -1)
        y_ref[...] = (b_ref[...] * zc).astype(BF16)

    def part(k):
        return pl.BlockSpec((n_rows, CONV_COLS), lambda j: (0, k * nb + j))

    return pl.pallas_call(
        body, name=name, grid=(nb,),
        in_specs=[part(0), part(1), part(2), pl.BlockSpec((3, CONV_COLS), lambda j: (0, j))],
        out_specs=pl.BlockSpec((n_rows, CONV_COLS), lambda j: (0, j)),
        out_shape=SDS((n_rows, D_MODEL), BF16), compiler_params=_cparams(),
    )(z3, z3, z3, conv_w)


def _conv_bwd(name, dy, z3, conv_w):
    n_rows = z3.shape[0]
    nb = D_MODEL // CONV_COLS

    def body(dy_ref, b_ref, c_ref, v_ref, w_ref, db_ref, dc_ref, dv_ref, dw_ref):
        c, v, d_y = c_ref[...], v_ref[...], dy_ref[...]
        z = c * v
        z_dn, z_up = _shift_rows(z, 1), _shift_rows(z, -1)
        zc = w_ref[0:1, :] * z_dn + w_ref[1:2, :] * z + w_ref[2:3, :] * z_up
        db_ref[...] = (d_y * zc).astype(BF16)
        dzc = d_y * b_ref[...]
        dz = w_ref[0:1, :] * _shift_rows(dzc, -1) + w_ref[1:2, :] * dzc + w_ref[2:3, :] * _shift_rows(dzc, 1)
        dc_ref[...] = (dz * v).astype(BF16)
        dv_ref[...] = (dz * c).astype(BF16)
        dw_ref[0:1, :] = _colsum(dzc * z_dn)
        dw_ref[1:2, :] = _colsum(dzc * z)
        dw_ref[2:3, :] = _colsum(dzc * z_up)

    def part(k):
        return pl.BlockSpec((n_rows, CONV_COLS), lambda j: (0, k * nb + j))

    col = pl.BlockSpec((n_rows, CONV_COLS), lambda j: (0, j))
    w_spec = pl.BlockSpec((3, CONV_COLS), lambda j: (0, j))
    return pl.pallas_call(
        body, name=name, grid=(nb,), in_specs=[col, part(0), part(1), part(2), w_spec],
        out_specs=[col, col, col, w_spec],
        out_shape=[SDS((n_rows, D_MODEL), BF16)] * 3 + [SDS((3, D_MODEL), F32)], compiler_params=_cparams(),
    )(dy, z3, z3, z3, conv_w)


def _silu_rows(name, x):
    def body(x_ref, s_ref, d_ref):
        xx = x_ref[...]
        sg = jax.nn.sigmoid(xx)
        s_ref[...] = (xx * sg).astype(BF16)
        d_ref[...] = sg * (1.0 + xx * (1.0 - sg))

    return pl.pallas_call(body, name=name, out_shape=[SDS(x.shape, BF16), SDS(x.shape, F32)])(x)


def _sum_rows(name, x, scale=None):
    r, n = x.shape
    tn = _tile(n, 8192, 128)

    def body(*refs):
        acc = jnp.sum(refs[0][...].astype(F32), axis=0, keepdims=True)
        if scale is not None:
            acc = acc * refs[1][...]
        refs[-1][...] = acc

    in_specs = [pl.BlockSpec((r, tn), lambda j: (0, j))]
    args = [x]
    if scale is not None:
        in_specs.append(pl.BlockSpec((1, tn), lambda j: (0, j)))
        args.append(scale)
    return pl.pallas_call(body, name=name, grid=(n // tn,), in_specs=in_specs,
                          out_specs=pl.BlockSpec((1, tn), lambda j: (0, j)), out_shape=SDS((1, n), F32))(*args)


def _sum_slots(name, x):
    n_slots, r, c = x.shape
    tr = _tile(r, 432, 16)

    def body(x_ref, o_ref):
        acc = x_ref[0].astype(F32)
        for sl in range(1, n_slots):
            acc = acc + x_ref[sl].astype(F32)
        o_ref[...] = acc

    return pl.pallas_call(body, name=name, grid=(r // tr,),
                          in_specs=[pl.BlockSpec((n_slots, tr, c), lambda i: (0, i, 0))],
                          out_specs=pl.BlockSpec((tr, c), lambda i: (i, 0)), out_shape=SDS((r, c), F32),
                          compiler_params=_cparams())(x)


def _adamw(name, w, g, m, v):
    shape = w.shape
    cols = shape[-1]
    rows = w.size // cols
    tr = _tile(rows, 512, 8)
    bc1 = 1.0 - ADAM_B1 ** ADAM_STEP
    bc2 = 1.0 - ADAM_B2 ** ADAM_STEP

    def body(w_ref, g_ref, m_ref, v_ref, d_ref, nm_ref, nv_ref):
        gg = g_ref[...]
        nm = ADAM_B1 * m_ref[...] + (1.0 - ADAM_B1) * gg
        nv = ADAM_B2 * v_ref[...] + (1.0 - ADAM_B2) * (gg * gg)
        nm_ref[...] = nm
        nv_ref[...] = nv
        d_ref[...] = -ADAM_LR * ((nm / bc1) / (jnp.sqrt(nv / bc2) + ADAM_EPS) + ADAM_WD * w_ref[...])

    spec = pl.BlockSpec((tr, cols), lambda i: (i, 0))
    outs = pl.pallas_call(body, name=name, grid=(rows // tr,), in_specs=[spec] * 4, out_specs=[spec] * 3,
                          out_shape=[SDS((rows, cols), F32)] * 3, compiler_params=_cparams())(
        w.reshape(rows, cols), g.reshape(rows, cols), m.reshape(rows, cols), v.reshape(rows, cols))
    return tuple(t.reshape(shape) for t in outs)


def _exchange(name, x, scatter):
    blk = x.shape[1:] if scatter else x.shape

    def body(x_ref, out_ref, send_sems, recv_sems, local_sem):
        mx, my, mc = lax.axis_index("x"), lax.axis_index("y"), lax.axis_index("c")
        me = 4 * mx + 2 * my + mc
        own = pltpu.make_async_copy(x_ref.at[me] if scatter else x_ref, out_ref.at[me], local_sem)
        own.start()
        copies = []
        for kk in range(1, N_DEV):
            px = jnp.bitwise_xor(mx, (kk >> 2) & 1)
            py = jnp.bitwise_xor(my, (kk >> 1) & 1)
            pc = jnp.bitwise_xor(mc, kk & 1)
            peer = 4 * px + 2 * py + pc
            send = pltpu.make_async_remote_copy(
                src_ref=x_ref.at[peer] if scatter else x_ref, dst_ref=out_ref.at[me],
                send_sem=send_sems.at[kk - 1], recv_sem=recv_sems.at[kk - 1],
                device_id=(px, py, pc), device_id_type=MESH)
            send.start()
            arrival = pltpu.make_async_remote_copy(
                src_ref=x_ref.at[peer] if scatter else x_ref, dst_ref=out_ref.at[peer],
                send_sem=send_sems.at[kk - 1], recv_sem=recv_sems.at[kk - 1],
                device_id=(px, py, pc), device_id_type=MESH)
            copies.append((send, arrival))
        for send, arrival in copies:
            arrival.wait_recv()
            send.wait_send()
        own.wait()

    return pl.pallas_call(
        body, name=name, out_shape=SDS((N_DEV,) + tuple(blk), x.dtype),
        in_specs=[pl.BlockSpec(memory_space=pl.ANY)], out_specs=pl.BlockSpec(memory_space=pl.ANY),
        scratch_shapes=[pltpu.SemaphoreType.DMA((N_DEV - 1,)), pltpu.SemaphoreType.DMA((N_DEV - 1,)),
                        pltpu.SemaphoreType.DMA],
    )(x)


def _rope_perm(pre, reps, post):
    half = QK_ROPE // 4
    width = reps * (pre + QK_ROPE) + post
    p = np.zeros((width, width), np.float32)
    for rep in range(reps):
        s0 = rep * (pre + QK_ROPE) + pre
        for base in (s0, s0 + 2 * half):
            for i in range(half):
                p[base + half + i, base + i] = -1.0
                p[base + i, base + half + i] = 1.0
    return p


def _rope_tables(n_lat, t_rows, pre, reps, post):
    half = QK_ROPE // 4
    pos = jnp.arange(n_lat)
    freqs = jnp.power(ROPE_THETA, -jnp.arange(0, 2 * half, 2, dtype=F32) / (2 * half))
    ang_r = (pos // GRID_W).astype(F32)[:, None] * freqs
    ang_c = (pos % GRID_W).astype(F32)[:, None] * freqs
    ang = jnp.concatenate([ang_r, ang_r, ang_c, ang_c], axis=-1)

    def table(fn, plain):
        slot = jnp.concatenate([jnp.full((n_lat, pre), plain, F32), fn(ang)], axis=-1)
        t = jnp.concatenate([jnp.tile(slot, (1, reps)), jnp.full((n_lat, post), plain, F32)], axis=-1)
        return jnp.concatenate([t, jnp.full((t_rows - n_lat, t.shape[1]), plain, F32)], axis=0)

    return table(jnp.cos, 1.0), table(jnp.sin, 0.0)


def _ffn_half_fwd(tag, s, mg, k, feed, i, coef, n_lat):
    u = _adaln_fwd(f"{tag}_adaln", s, mg, k, n_lat)
    wg_t, wu_t = feed.weights(f"{tag}_up", [f"gate_t{i}", f"up_t{i}"], u)
    a, b, hid = _ffn_up(f"{tag}_up", u, wg_t, wu_t)
    (wd,) = feed.weights(f"{tag}_down", [f"down{i}"], hid)
    s_out, o = _mm_resid(f"{tag}_down", hid, wd, s, mg, k, coef, n_lat)
    return s_out, (s, u, a, b, hid, o, wg_t, wu_t, wd)


def _ffn_half_bwd(tag, ds_out, saved, mg, k, feed, i, coef, n_lat):
    s, u, a, b, hid, o, wg_t, wu_t, wd = saved
    do, dgate = _gate_bwd(f"{tag}_dgate", ds_out, o, mg, k, coef, n_lat)
    da, db = _ffn_dact(f"{tag}_dact", do, wd, a, b)
    dwd = _mm(f"{tag}_dwd", [(hid, do)], "tn", BF16)
    dwg_t = _mm(f"{tag}_dwg", [(da, u)], "tn", BF16)
    dwu_t = _mm(f"{tag}_dwu", [(db, u)], "tn", BF16)
    token = feed.grads(tag, {f"down{i}": dwd, f"gate_t{i}": dwg_t, f"up_t{i}": dwu_t})
    du = _mm(f"{tag}_du", [(da, wg_t), (db, wu_t)], "nn", F32, 384, 512, bias=_after(token))
    ds_in, (dshift, dscale, dgain) = _adaln_bwd(f"{tag}_dadaln", s, du, ds_out, mg, k, n_lat)
    return ds_in, dict(shift=dshift, scale=dscale, gate=dgate, gain=dgain)


def _after(token):
    return jnp.zeros((1, D_MODEL), F32) + token


def _mod_grad(parts, n_groups):
    rows = []
    zero = jnp.zeros((n_groups, 1, D_MODEL), F32)
    for k in range(3):
        for nm in ("shift", "scale", "gate"):
            t = parts[k].get(nm, zero)
            if t.shape[0] < n_groups:
                t = jnp.concatenate([t, jnp.zeros((n_groups - t.shape[0], 1, D_MODEL), F32)], axis=0)
            rows.append(t)
    return jnp.concatenate(rows, axis=1).reshape(n_groups, N_MOD * D_MODEL)


def _local_step(x, ctx, target, mod_h, mod_g, norm_g, feed, pool_w, pool_scale, q_norm_g, kv_norm_g, conv_w,
                final_norm_g):
    n_lat, n_ctx = x.shape[0], ctx.shape[0]
    t_all = n_lat + n_ctx
    mg0 = jnp.stack([jnp.concatenate([mod_h[0], norm_g[0]], axis=0), jnp.concatenate([mod_g, norm_g[0]], axis=0)])
    mg1 = jnp.concatenate([mod_h[1], norm_g[1]], axis=0)[None]

    s0 = jnp.concatenate([x, ctx], axis=0) + feed.start_token()
    s1, sv_f00 = _ffn_half_fwd("l0f0", s0, mg0, 0, feed, 0, 0.5, n_lat)

    ua = _adaln_fwd("l0m_adaln", s1, mg0, 1, n_lat)
    w_in, w_uq, w_ukv_t, w_ab_out = feed.weights("l0m", ["in_t", "uq", "ukv_t", "ab_out"], ua)
    kv_rows = KV_RANK + QK_ROPE
    w_in_t = jnp.concatenate([
        w_in[:POOL_DIM], jnp.zeros((PA_CQ - POOL_DIM, D_MODEL), BF16), w_in[POOL_DIM:POOL_DIM + Q_RANK],
        w_in[POOL_DIM + Q_RANK:], jnp.zeros((PA_KV_W - kv_rows, D_MODEL), BF16)], axis=0)
    proj = _mm("l0m_proj", [(ua, w_in_t)], "nt", F32, 768, 384)
    pool_y, pool_p = _pool_fwd("l0m_pool", proj, n_lat, pool_w.astype(BF16), pool_scale)
    nq = _rmsnorm_fwd("l0m_qnorm", proj, Q_RANK, PA_CQ // Q_RANK, q_norm_g, n_lat)
    q_lin = _mm("l0m_q", [(nq, w_uq)], "nn", F32, 512, 768)
    cos_q, sin_q = _rope_tables(n_lat, n_lat, QK_NOPE, HEADS, 0)
    perm_q = _rope_perm(QK_NOPE, HEADS, 0)
    q_rot = _rope("l0m_qrope", q_lin, Q_RANK, 0, cos_q, sin_q, jnp.asarray(perm_q, BF16), False, BF16)
    cos_k, sin_k = _rope_tables(n_lat, t_all, KV_RANK, 1, PA_KV_W - kv_rows)
    perm_k = _rope_perm(KV_RANK, 1, PA_KV_W - kv_rows)
    kvr = _rope("l0m_krope", proj, PA_KV_W, PA_KV // PA_KV_W, cos_k, sin_k, jnp.asarray(perm_k, BF16), False, F32)
    nkv = _rmsnorm_fwd("l0m_kvnorm", kvr, KV_RANK, 0, kv_norm_g, t_all)
    kv = _mm("l0m_kv", [(nkv, w_ukv_t)], "nt", BF16, 768, 512)
    qh = jnp.pad(q_rot.reshape(n_lat, HEADS, QK_HEAD), ((0, 0), (0, 0), (0, HEAD_PAD - QK_HEAD))).transpose(1, 0, 2)
    kvh = kv.reshape(t_all, HEADS, QK_NOPE + V_HEAD)
    k_rope = jnp.broadcast_to(kvr[:, None, KV_RANK:KV_RANK + QK_ROPE].astype(BF16), (t_all, HEADS, QK_ROPE))
    kh = jnp.concatenate([kvh[:, :, :QK_NOPE], k_rope, jnp.zeros((t_all, HEADS, HEAD_PAD - QK_HEAD), BF16)],
                         axis=-1).transpose(1, 0, 2)
    vh = kvh[:, :, QK_NOPE:].transpose(1, 0, 2)
    oh, lse = _attn_fwd("l0m_attn", qh, kh, vh)
    cat = jnp.concatenate([pool_y, oh.transpose(1, 0, 2).reshape(n_lat, HEADS * V_HEAD)], axis=-1)
    h1 = s1[:n_lat]
    h2, mix_o = _mm_resid("l0m_out", cat, w_ab_out, h1, mg0[:1], 1, 1.0, n_lat)

    h3, sv_f01 = _ffn_half_fwd("l0f1", h2, mg0[:1], 2, feed, 1, 0.5, n_lat)

    h4, sv_f10 = _ffn_half_fwd("l1f0", h3, mg1, 0, feed, 2, 0.5, n_lat)
    uc = _adaln_fwd("l1m_adaln", h4, mg1, 1, n_lat)
    w_cin_t, w_c_out = feed.weights("l1m", ["cin_t", "c_out"], uc)
    z3 = _mm("l1m_in", [(uc, w_cin_t)], "nt", F32)
    yc = _conv_fwd("l1m_conv", z3, conv_w)
    h5, conv_o = _mm_resid("l1m_out", yc, w_c_out, h4, mg1, 1, 1.0, n_lat)
    h6, sv_f11 = _ffn_half_fwd("l1f1", h5, mg1, 2, feed, 3, 0.5, n_lat)

    dh6, sq_cols, d_final_g = _final_loss("loss_head", h6, target, final_norm_g)
    g = {}
    dh5, g["f11"] = _ffn_half_bwd("l1f1", dh6, sv_f11, mg1, 2, feed, 3, 0.5, n_lat)

    do_c, dgate_c = _gate_bwd("l1m_dgate", dh5, conv_o, mg1, 1, 1.0, n_lat)
    dyc = _mm("l1m_dy", [(do_c, w_c_out)], "nt", F32)
    d_c_out = _mm("l1m_dwout", [(yc, do_c)], "tn", BF16)
    db_, dc_, dv_, d_conv_w = _conv_bwd("l1m_dconv", dyc, z3, conv_w)
    dz3 = jnp.concatenate([db_, dc_, dv_], axis=-1)
    d_cin_t = _mm("l1m_dwin", [(dz3, uc)], "tn", BF16)
    token = feed.grads("l1m", {"c_out": d_c_out, "cin_t": d_cin_t})
    duc = _mm("l1m_du", [(dz3, w_cin_t)], "nn", F32, bias=_after(token))
    dh4, (dsh_c, dsc_c, dgn_c) = _adaln_bwd("l1m_dadaln", h4, duc, dh5, mg1, 1, n_lat)
    dh3, g["f10"] = _ffn_half_bwd("l1f0", dh4, sv_f10, mg1, 0, feed, 2, 0.5, n_lat)

    dh2, g["f01"] = _ffn_half_bwd("l0f1", dh3, sv_f01, mg0[:1], 2, feed, 1, 0.5, n_lat)

    do_a, dgate_a = _gate_bwd("l0m_dgate", dh2, mix_o, mg0[:1], 1, 1.0, n_lat)
    dcat = _mm("l0m_dcat", [(do_a, w_ab_out)], "nt", F32)
    d_ab_out = _mm("l0m_dwout", [(cat, do_a)], "tn", BF16)
    d_pool_x, d_pool_w, d_pool_scale = _pool_bwd("l0m_dpool", dcat, n_lat, pool_p, pool_w.astype(BF16), pool_scale)
    doh = dcat[:, POOL_DIM:].reshape(n_lat, HEADS, V_HEAD).transpose(1, 0, 2).astype(BF16)
    dqh, dkh, dvh, dk_sum = _attn_bwd("l0m_dattn", qh, kh, vh, oh, lse, doh)
    dq_rot = dqh[:, :, :QK_HEAD].transpose(1, 0, 2).reshape(n_lat, Q_RANK)
    dq_lin = _rope("l0m_dqrope", dq_rot, Q_RANK, 0, cos_q, sin_q, jnp.asarray(perm_q.T, BF16), True, BF16)
    d_uq = _mm("l0m_dwuq", [(nq, dq_lin)], "tn", BF16, 768, 768)
    dnq = _mm("l0m_dnq", [(dq_lin, w_uq)], "nt", F32, 512, 768)
    dcq, d_q_norm_g = _rmsnorm_bwd("l0m_dqnorm", proj, Q_RANK, PA_CQ // Q_RANK, dnq, q_norm_g, n_lat)
    dkv = jnp.concatenate([dkh[:, :, :QK_NOPE], dvh], axis=-1).transpose(1, 0, 2).reshape(t_all, HEADS * HEAD_PAD)
    dkv = dkv.astype(BF16)
    dnkv = _mm("l0m_dnkv", [(dkv, w_ukv_t)], "nn", F32, 768, 256)
    d_ukv_t = _mm("l0m_dwukv", [(dkv, nkv)], "tn", BF16, 512, 256)
    dckv, d_kv_norm_g = _rmsnorm_bwd("l0m_dkvnorm", kvr, KV_RANK, 0, dnkv, kv_norm_g, t_all)
    dkvr = jnp.concatenate([dckv, dk_sum[:, QK_NOPE:QK_HEAD],
                            jnp.zeros((t_all, PA_KV_W - KV_RANK - QK_ROPE), F32)], axis=-1)
    dpb = _rope("l0m_dkrope", dkvr, PA_KV_W, 0, cos_k, sin_k, jnp.asarray(perm_k.T, BF16), True, F32)
    dproj_lat = jnp.concatenate([d_pool_x, jnp.zeros((n_lat, PA_CQ - POOL_DIM), F32), dcq, dpb[:n_lat]], axis=-1)
    dproj_ctx = jnp.concatenate([jnp.zeros((n_ctx, PA_KV), F32), dpb[n_lat:]], axis=-1)
    dproj = jnp.concatenate([dproj_lat, dproj_ctx], axis=0).astype(BF16)
    d_in_pad = _mm("l0m_dwin", [(dproj, ua)], "tn", BF16, 640, 512)
    d_in_t = jnp.concatenate([d_in_pad[:POOL_DIM], d_in_pad[PA_CQ:PA_CQ + Q_RANK],
                              d_in_pad[PA_KV:PA_KV + kv_rows]], axis=0)
    token = feed.grads("l0m", {"ab_out": d_ab_out, "uq": d_uq, "ukv_t": d_ukv_t, "in_t": d_in_t})
    dua = _mm("l0m_du", [(dproj, w_in_t)], "nn", F32, 768, 512, bias=_after(token))
    dh2_all = jnp.concatenate([dh2, jnp.zeros((n_ctx, D_MODEL), F32)], axis=0)
    ds1, (dsh_a, dsc_a, dgn_a) = _adaln_bwd("l0m_dadaln", s1, dua, dh2_all, mg0, 1, n_lat)
    ds0, g["f00"] = _ffn_half_bwd("l0f0", ds1, sv_f00, mg0, 0, feed, 0, 0.5, n_lat)

    dmod0 = _mod_grad([g["f00"], dict(shift=dsh_a, scale=dsc_a, gate=dgate_a), g["f01"]], 2)
    dmod1 = _mod_grad([g["f10"], dict(shift=dsh_c, scale=dsc_c, gate=dgate_c), g["f11"]], 1)
    d_norm_g = jnp.stack([
        jnp.concatenate([jnp.sum(g["f00"]["gain"], axis=0), jnp.sum(dgn_a, axis=0), g["f01"]["gain"][0]], axis=0),
        jnp.concatenate([g["f10"]["gain"][0], dgn_c[0], g["f11"]["gain"][0]], axis=0)])
    grads = dict(
        pool_w=d_pool_w, pool_scale=d_pool_scale, q_norm_g=d_q_norm_g[0], kv_norm_g=d_kv_norm_g[0],
        conv_w=d_conv_w, final_norm_g=d_final_g[0], norm_g=d_norm_g,
        mod_h=jnp.stack([dmod0[0], dmod1[0]]), mod_g=dmod0[1])
    return sq_cols, ds0, grads


HBM_SPEC = pl.BlockSpec(memory_space=pltpu.HBM)
SEM_SPEC = pl.BlockSpec(memory_space=pltpu.SEMAPHORE)
ANY_SPEC = pl.BlockSpec(memory_space=pl.ANY)
SIDE_EFFECT = pltpu.SideEffectType.DATAFLOW_SIDE_EFFECTING
N_PEERS = N_DEV - 1


def _mesh_place():
    mx, my, mc = lax.axis_index("x"), lax.axis_index("y"), lax.axis_index("c")
    return mx, my, mc, 4 * mx + 2 * my + mc


def _peer(place, kk):
    mx, my, mc, _ = place
    px = jnp.bitwise_xor(mx, (kk >> 2) & 1)
    py = jnp.bitwise_xor(my, (kk >> 1) & 1)
    pc = jnp.bitwise_xor(mc, kk & 1)
    return (px, py, pc), 4 * px + 2 * py + pc


def _hbm(a):
    return pltpu.with_memory_space_constraint(a, pltpu.HBM)


def _landing(block, me):
    zone = lax.empty((N_DEV,) + block.shape, block.dtype)
    return lax.dynamic_update_slice(zone, block[None], (me,) + (0,) * block.ndim)


ALL_PEERS = tuple(range(1, N_DEV))
SIBLING = 1
CHIP_PEERS = (2, 4, 6)
RELAYED = (3, 5, 7)


def _exchange_start(name, srcs, lands, scatter, after, peers=ALL_PEERS):
    n = len(srcs)

    def body(*refs):
        src, land = refs[:n], refs[n:2 * n]
        send_sems, recv_sems, token = refs[2 * n + 1], refs[2 * n + 2], refs[-1]
        place = _mesh_place()
        for a in range(n):
            for kk in peers:
                dev, peer = _peer(place, kk)
                pltpu.make_async_remote_copy(
                    src_ref=src[a].at[peer] if scatter else src[a], dst_ref=land[a].at[place[3]],
                    send_sem=send_sems.at[a * N_PEERS + kk - 1], recv_sem=recv_sems.at[a * N_PEERS + kk - 1],
                    device_id=dev, device_id_type=MESH).start()
        token[...] = jnp.zeros_like(token)

    thru = [pltpu.HBM(t.shape, t.dtype) for t in (*srcs, *lands)]
    res = pl.pallas_call(
        body, name=name,
        out_shape=(pltpu.SemaphoreType.DMA((n * N_PEERS,)), pltpu.SemaphoreType.DMA((n * N_PEERS,)), *thru,
                   SDS((8, 128), F32)),
        in_specs=[HBM_SPEC] * (2 * n) + [ANY_SPEC],
        out_specs=(SEM_SPEC, SEM_SPEC, *([HBM_SPEC] * (2 * n)), pl.BlockSpec(memory_space=pltpu.VMEM)),
        input_output_aliases={i: 2 + i for i in range(2 * n)},
        compiler_params=pltpu.CompilerParams(has_side_effects=SIDE_EFFECT),
    )(*[_hbm(s) for s in srcs], *[_hbm(t) for t in lands], after)
    return res[0], res[1], list(res[2:2 + n]), list(res[2 + n:2 + 2 * n]), res[-1]


def _exchange_wait(name, send_sems, recv_sems, srcs, lands, places, scatter, after):
    n = len(srcs)

    def body(*refs):
        src, land = refs[:n], refs[n:2 * n]
        send, recv = refs[2 * n], refs[2 * n + 1]
        place = _mesh_place()
        for a in range(n):
            for kk in range(1, N_DEV):
                dev, peer = _peer(place, kk)
                cp = pltpu.make_async_remote_copy(
                    src_ref=src[a].at[peer] if scatter else src[a], dst_ref=land[a].at[peer],
                    send_sem=send.at[places[a] * N_PEERS + kk - 1], recv_sem=recv.at[places[a] * N_PEERS + kk - 1],
                    device_id=dev, device_id_type=MESH)
                cp.wait_send()
                cp.wait_recv()

    thru = [pltpu.HBM(t.shape, t.dtype) for t in (*srcs, *lands)]
    res = pl.pallas_call(
        body, name=name, out_shape=tuple(thru),
        in_specs=[HBM_SPEC] * (2 * n) + [SEM_SPEC, SEM_SPEC, ANY_SPEC], out_specs=tuple([HBM_SPEC] * (2 * n)),
        input_output_aliases={i: i for i in range(2 * n)},
        compiler_params=pltpu.CompilerParams(has_side_effects=SIDE_EFFECT),
    )(*srcs, *lands, send_sems, recv_sems, after)
    return list(res[n:])


def _gather_relay(name, send1, recv1, lands, places, after):
    n = len(lands)

    def body(*refs):
        land, s1, r1 = refs[:n], refs[n], refs[n + 1]
        s2, r2 = refs[n + 3], refs[n + 4]
        place = _mesh_place()
        sibling = _peer(place, SIBLING)[0]
        for a in range(n):
            for j, kk in enumerate(CHIP_PEERS):
                dev, origin = _peer(place, kk)
                block = land[a].at[origin]
                pltpu.make_async_remote_copy(
                    src_ref=block, dst_ref=block, send_sem=s1.at[places[a] * N_PEERS + kk - 1],
                    recv_sem=r1.at[places[a] * N_PEERS + kk - 1], device_id=dev, device_id_type=MESH).wait_recv()
                pltpu.make_async_remote_copy(
                    src_ref=block, dst_ref=block, send_sem=s2.at[a * 3 + j], recv_sem=r2.at[a * 3 + j],
                    device_id=sibling, device_id_type=MESH).start()

    res = pl.pallas_call(
        body, name=name,
        out_shape=(pltpu.SemaphoreType.DMA((3 * n,)), pltpu.SemaphoreType.DMA((3 * n,)),
                   *[pltpu.HBM(t.shape, t.dtype) for t in lands]),
        in_specs=[HBM_SPEC] * n + [SEM_SPEC, SEM_SPEC, ANY_SPEC],
        out_specs=(SEM_SPEC, SEM_SPEC, *([HBM_SPEC] * n)),
        input_output_aliases={i: 2 + i for i in range(n)},
        compiler_params=pltpu.CompilerParams(has_side_effects=SIDE_EFFECT),
    )(*lands, send1, recv1, after)
    return res[0], res[1], list(res[2:])


def _gather_wait(name, send1, recv1, send2, recv2, srcs, lands, places, after):
    n = len(lands)

    def body(*refs):
        src, land = refs[:n], refs[n:2 * n]
        s1, r1, s2, r2 = refs[2 * n:2 * n + 4]
        place = _mesh_place()
        for a in range(n):
            for kk in (SIBLING,) + CHIP_PEERS:
                dev, origin = _peer(place, kk)
                first = pltpu.make_async_remote_copy(
                    src_ref=src[a], dst_ref=land[a].at[origin], send_sem=s1.at[places[a] * N_PEERS + kk - 1],
                    recv_sem=r1.at[places[a] * N_PEERS + kk - 1], device_id=dev, device_id_type=MESH)
                first.wait_send()
                if kk == SIBLING:
                    first.wait_recv()
            for j, kk in enumerate(CHIP_PEERS):
                dev, origin = _peer(place, kk + 1)
                relay = pltpu.make_async_remote_copy(
                    src_ref=src[a], dst_ref=land[a].at[origin], send_sem=s2.at[a * 3 + j], recv_sem=r2.at[a * 3 + j],
                    device_id=dev, device_id_type=MESH)
                relay.wait_send()
                relay.wait_recv()

    arrays = (*srcs, *lands)
    res = pl.pallas_call(
        body, name=name, out_shape=tuple(pltpu.HBM(t.shape, t.dtype) for t in arrays),
        in_specs=[HBM_SPEC] * (2 * n) + [SEM_SPEC] * 4 + [ANY_SPEC], out_specs=tuple([HBM_SPEC] * (2 * n)),
        input_output_aliases={i: i for i in range(2 * n)},
        compiler_params=pltpu.CompilerParams(has_side_effects=SIDE_EFFECT),
    )(*arrays, send1, recv1, send2, recv2, after)
    return list(res[n:])


class _Feed:
    def __init__(self, shards, groups, me, after):
        self.names, self.me, self.groups, self.pos = list(shards), me, groups, 0
        srcs = [shards[nm] for nm in self.names]
        lands = [_landing(s, me) for s in srcs]
        self.send, self.recv, self.srcs, self.lands, self.token = _exchange_start(
            "gather_start", srcs, lands, False, after, (SIBLING,) + CHIP_PEERS)
        self.relay = self._relay("gather_relay_first", groups[0], self.token)
        self.pending = []

    def _relay(self, name, names, after):
        places = [self.names.index(nm) for nm in names]
        send2, recv2, lands = _gather_relay(name, self.send, self.recv, [self.lands[i] for i in places], places, after)
        for i, t in zip(places, lands):
            self.lands[i] = t
        return send2, recv2

    def start_token(self):
        return self.token[0, 0]

    def weights(self, tag, names, after):
        assert names == self.groups[self.pos], (names, self.groups[self.pos])
        send2, recv2 = self.relay
        if self.pos + 1 < len(self.groups):
            nxt = self.groups[self.pos + 1]
            self.relay = self._relay(f"gather_relay_{tag}", nxt, after)
            after = self.lands[self.names.index(nxt[0])]
        places = [self.names.index(nm) for nm in names]
        got = _gather_wait(f"gather_wait_{tag}", self.send, self.recv, send2, recv2, [self.srcs[i] for i in places],
                           [self.lands[i] for i in places], places, after)
        self.pos += 1
        return [t.reshape((N_DEV * t.shape[1],) + t.shape[2:]) for t in got]

    def grads(self, tag, full):
        names = list(full)
        srcs = [full[nm].reshape((N_DEV, full[nm].shape[0] // N_DEV) + full[nm].shape[1:]) for nm in names]
        lands = [_landing(lax.dynamic_index_in_dim(s, self.me, 0, keepdims=False), self.me) for s in srcs]
        send, recv, srcs, lands, token = _exchange_start(f"scatter_start_{tag}", srcs, lands, True, srcs[0])
        self.pending.append((tag, names, send, recv, srcs, lands))
        return token[0, 0]

    def finish(self, after):
        out = {}
        for tag, names, send, recv, srcs, lands in self.pending:
            got = _exchange_wait(f"scatter_wait_{tag}", send, recv, srcs, lands, list(range(len(names))), True, after)
            for nm, slots in zip(names, got):
                out[nm] = _sum_slots(f"reduce_{nm}", slots)
        return out


def _adam_all(names, weights, grads, moms, vels):
    deltas, new_m, new_v = [], [], []
    for nm, w, g, m, v in zip(names, weights, grads, moms, vels):
        d, a, b = _adamw(f"adamw_{nm}", w, g.reshape(w.shape), m, v)
        deltas.append(d)
        new_m.append(a)
        new_v.append(b)
    return deltas, new_m, new_v


WEIGHT_NAMES = ("c_ctx", "norm_g", "w_mod", "b_mod", "ffn_w_gate", "ffn_w_up", "ffn_w_down", "ab_w_in", "pool_w",
                "pool_scale", "q_norm_g", "w_uq", "kv_norm_g", "w_ukv", "ab_w_out", "conv_w_in", "conv_w",
                "conv_w_out", "final_norm_g")


def kernel(x, c, ctx, c_ctx, norm_g, w_mod, b_mod, ffn_w_gate, ffn_w_up, ffn_w_down, ab_w_in, pool_w, pool_scale, q_norm_g, w_uq, kv_norm_g, w_ukv, ab_w_out, conv_w_in, conv_w, conv_w_out, final_norm_g, loss_target, m_c_ctx, m_norm_g, m_w_mod, m_b_mod, m_ffn_w_gate, m_ffn_w_up, m_ffn_w_down, m_ab_w_in, m_pool_w, m_pool_scale, m_q_norm_g, m_w_uq, m_kv_norm_g, m_w_ukv, m_ab_w_out, m_conv_w_in, m_conv_w, m_conv_w_out, m_final_norm_g, v_c_ctx, v_norm_g, v_w_mod, v_b_mod, v_ffn_w_gate, v_ffn_w_up, v_ffn_w_down, v_ab_w_in, v_pool_w, v_pool_scale, v_q_norm_g, v_w_uq, v_kv_norm_g, v_w_ukv, v_ab_w_out, v_conv_w_in, v_conv_w, v_conv_w_out, v_final_norm_g):
    weights = (c_ctx, norm_g, w_mod, b_mod, ffn_w_gate, ffn_w_up, ffn_w_down, ab_w_in, pool_w, pool_scale, q_norm_g,
               w_uq, kv_norm_g, w_ukv, ab_w_out, conv_w_in, conv_w, conv_w_out, final_norm_g)
    moms = (m_c_ctx, m_norm_g, m_w_mod, m_b_mod, m_ffn_w_gate, m_ffn_w_up, m_ffn_w_down, m_ab_w_in, m_pool_w,
            m_pool_scale, m_q_norm_g, m_w_uq, m_kv_norm_g, m_w_ukv, m_ab_w_out, m_conv_w_in, m_conv_w, m_conv_w_out,
            m_final_norm_g)
    vels = (v_c_ctx, v_norm_g, v_w_mod, v_b_mod, v_ffn_w_gate, v_ffn_w_up, v_ffn_w_down, v_ab_w_in, v_pool_w,
            v_pool_scale, v_q_norm_g, v_w_uq, v_kv_norm_g, v_w_ukv, v_ab_w_out, v_conv_w_in, v_conv_w, v_conv_w_out,
            v_final_norm_g)
    me = 4 * lax.axis_index("x") + 2 * lax.axis_index("y") + lax.axis_index("c")
    n_lat, n_ctx = x.shape[1], ctx.shape[1]
    d = D_MODEL
    mod_cols = w_mod.shape[-1]
    ng_sh, cw_sh = norm_g.shape[-1], conv_w.shape[-1]

    small = jnp.concatenate([c.reshape(-1), norm_g.reshape(-1), conv_w.reshape(-1)])
    small_n = -(-small.shape[0] // 1024) * 1024
    small = jnp.pad(small, (0, small_n - small.shape[0])).reshape(small_n // 128, 128)
    small_all = _exchange("gather_small", small, False).reshape(N_DEV, small_n)
    c_all = small_all[:, :d]
    o1 = d + 6 * ng_sh
    norm_g_full = small_all[:, d:o1].reshape(N_DEV, 2, 3, ng_sh).transpose(1, 2, 0, 3).reshape(2, 3, d)
    conv_w_full = small_all[:, o1:o1 + 3 * cw_sh].reshape(N_DEV, 3, cw_sh).transpose(1, 0, 2).reshape(3, d)

    cond = jnp.concatenate([c_all, jnp.broadcast_to(c_ctx[None, :], (N_DEV, d))], axis=0)
    sil, dsil = _silu_rows("mod_silu", cond)
    w_mod_b = w_mod.astype(BF16)
    b_sh = lax.dynamic_slice(b_mod, (0, me * mod_cols), (2, mod_cols))
    m_part = jnp.stack([_mm(f"mod_fwd{l}", [(sil, w_mod_b[l])], "nn", F32, 16, 384, bias=b_sh[l:l + 1])
                        for l in range(2)], axis=1)
    m_all = _exchange("gather_mod", m_part.reshape(-1, 128), False).reshape(N_DEV, 2 * N_DEV, 2, mod_cols)
    m_mine = lax.dynamic_index_in_dim(m_all, me, axis=1, keepdims=False)
    mod_h = m_mine.transpose(1, 0, 2).reshape(2, N_MOD, d)
    mod_g = m_all[:, N_DEV, 0, :].reshape(N_MOD, d)

    def ffn_shards(i):
        return {f"gate_t{i}": ffn_w_gate[i // 2, i % 2].T, f"up_t{i}": ffn_w_up[i // 2, i % 2].T,
                f"down{i}": ffn_w_down[i // 2, i % 2]}

    local = {**ffn_shards(0), "in_t": ab_w_in[0].T, "uq": w_uq[0], "ukv_t": w_ukv[0].T, "ab_out": ab_w_out[0],
             **ffn_shards(1), **ffn_shards(2), "cin_t": conv_w_in[0].T, "c_out": conv_w_out[0], **ffn_shards(3)}
    ffn_groups = [[[f"gate_t{i}", f"up_t{i}"], [f"down{i}"]] for i in range(4)]
    groups = [*ffn_groups[0], ["in_t", "uq", "ukv_t", "ab_out"], *ffn_groups[1], *ffn_groups[2], ["cin_t", "c_out"],
              *ffn_groups[3]]
    feed = _Feed({nm: a.astype(BF16) for nm, a in local.items()}, groups, me, m_all)

    sq_cols, ds0, g = _local_step(x[0], ctx[0], loss_target[0], mod_h, mod_g, norm_g_full, feed, pool_w[0],
                                  pool_scale, q_norm_g, kv_norm_g, conv_w_full, final_norm_g)
    grad_x = ds0[:n_lat]
    loss = lax.psum(0.5 * jnp.sum(sq_cols) / d, ("x", "y", "c"))
    red = feed.finish(ds0)

    dm = jnp.stack([g["mod_h"], jnp.stack([g["mod_g"], jnp.zeros_like(g["mod_g"])])])
    dm_all = _exchange("gather_dmod", dm.reshape(-1, 128), False).reshape(N_DEV, 2, 2, N_MOD * d)
    grad_b_mod = _sum_rows("dmod_bias", dm_all.reshape(2 * N_DEV, 2 * N_MOD * d)).reshape(2, N_MOD * d)
    dm_sh = lax.dynamic_slice(dm_all, (0, 0, 0, me * mod_cols), (N_DEV, 2, 2, mod_cols))
    gw_mod, cctx_parts = [], []
    for l in range(2):
        dm_l = dm_sh[:, :, l, :].transpose(1, 0, 2).reshape(2 * N_DEV, mod_cols).astype(BF16)
        gw_mod.append(_mm(f"mod_dw{l}", [(sil, dm_l)], "tn", F32, 512, 384))
        dm_ctx = jnp.concatenate([dm_l[N_DEV:], jnp.zeros((N_DEV, mod_cols), BF16)], axis=0)
        cctx_parts.append(_mm(f"mod_dcond{l}", [(dm_ctx, w_mod_b[l])], "nt", F32, 16, 512))
    grad_w_mod = jnp.stack(gw_mod)
    cctx_part = _sum_rows("mod_dcond_sum", jnp.concatenate(cctx_parts, axis=0))

    small_g = jnp.concatenate([g["pool_w"].reshape(-1), g["pool_scale"].reshape(-1), g["q_norm_g"].reshape(-1),
                               g["kv_norm_g"].reshape(-1), g["final_norm_g"].reshape(-1), g["norm_g"].reshape(-1),
                               g["conv_w"].reshape(-1), cctx_part.reshape(-1)])
    sizes = [pool_w.size, pool_scale.size, q_norm_g.size, kv_norm_g.size, d, 6 * d, 3 * d, d]
    sg_n = -(-small_g.shape[0] // 1024) * 1024
    small_g = jnp.pad(small_g, (0, sg_n - small_g.shape[0]))
    sg_all = _exchange("gather_small_grads", small_g.reshape(-1, 128), False).reshape(N_DEV, sg_n)
    scale_vec = jnp.concatenate([jnp.ones((1, sum(sizes[:-1])), F32), dsil[N_DEV:N_DEV + 1],
                                 jnp.ones((1, sg_n - sum(sizes)), F32)], axis=1)
    sg = _sum_rows("small_grads_sum", sg_all, scale_vec)[0]
    cuts, pos = [], 0
    for sz in sizes:
        cuts.append(sg[pos:pos + sz])
        pos += sz
    g_pool_w, g_pool_scale, g_q_norm, g_kv_norm, g_final, g_norm_full, g_conv_full, g_c_ctx = cuts
    grad_norm_g = lax.dynamic_slice(g_norm_full.reshape(2, 3, d), (0, 0, me * ng_sh), (2, 3, ng_sh))
    grad_conv_w = lax.dynamic_slice(g_conv_full.reshape(3, d), (0, me * cw_sh), (3, cw_sh))[None]

    def mine(nm):
        return red[nm]

    grad_gate = jnp.stack([mine(f"gate_t{i}").T for i in range(4)]).reshape(ffn_w_gate.shape)
    grad_up = jnp.stack([mine(f"up_t{i}").T for i in range(4)]).reshape(ffn_w_up.shape)
    grad_down = jnp.stack([mine(f"down{i}") for i in range(4)]).reshape(ffn_w_down.shape)
    grads = (g_c_ctx, grad_norm_g, grad_w_mod, grad_b_mod, grad_gate, grad_up, grad_down, mine("in_t").T[None],
             g_pool_w.reshape(pool_w.shape), g_pool_scale.reshape(pool_scale.shape), g_q_norm.reshape(q_norm_g.shape),
             mine("uq")[None], g_kv_norm.reshape(kv_norm_g.shape), mine("ukv_t").T[None], mine("ab_out")[None],
             mine("cin_t").T[None], grad_conv_w, mine("c_out")[None], g_final)
    grads = tuple(gr.reshape(w.shape) for gr, w in zip(grads, weights))
    deltas, new_m, new_v = _adam_all(WEIGHT_NAMES, weights, grads, moms, vels)
    return (loss, grad_x[None], *grads, *deltas, *new_m, *new_v)
```

```python
import functools
import math

import jax
import jax.numpy as jnp
import numpy as np
from jax import lax
from jax.experimental import pallas as pl
from jax.experimental.pallas import tpu as pltpu

F32 = jnp.float32
BF16 = jnp.bfloat16
MESH = pl.DeviceIdType.MESH
SDS = jax.ShapeDtypeStruct

N_DEV = 8
D_MODEL = 1024
N_MOD = 9
D_FF = 2816
POOL_WINDOWS = (2, 4, 8, 16)
POOL_DIM = 512
POOL_GROUP_DIM = 128
HEADS = 8
QK_NOPE = 64
QK_ROPE = 32
QK_HEAD = QK_NOPE + QK_ROPE
V_HEAD = 64
Q_RANK = 768
KV_RANK = 256
GRID_W = 64
ROPE_THETA = 10000.0
RMS_EPS = 1e-6
ATTN_SCALE = 1.0 / math.sqrt(QK_HEAD)
HEAD_PAD = 128
POOL_PAD = 16
PA_POOL, PA_CQ, PA_KV = 0, 768, 1536
PA_KV_W = 384
PA_W = PA_KV + PA_KV_W

ADAM_LR, ADAM_B1, ADAM_B2, ADAM_EPS, ADAM_WD, ADAM_STEP = 0.001, 0.9, 0.999, 1e-08, 0.01, 10

VMEM_LIMIT_BYTES = 56 * 1024 * 1024

NN = ((1,), (0,))
NT = ((1,), (1,))
TN = ((0,), (0,))


def _cparams():
    return pltpu.CompilerParams(vmem_limit_bytes=VMEM_LIMIT_BYTES)


def _dot(a, b, dims):
    return lax.dot_general(a, b, (dims, ((), ())), preferred_element_type=F32)


def _tile(n, cap, mult=8):
    t = (min(cap, n) // mult) * mult
    while t >= mult:
        if n % t == 0:
            return t
        t -= mult
    return n


def _colsum(x):
    return jnp.sum(x, axis=0, keepdims=True)


def _rms(x):
    r = lax.rsqrt(jnp.mean(x * x, axis=-1, keepdims=True) + RMS_EPS)
    return x * r, r


def _rms_bwd(n, r, dn):
    return r * (dn - n * jnp.mean(dn * n, axis=-1, keepdims=True))


def _rowwise(name, fn, t_rows, tm, n_lat, rows, vecs, outs, accs):
    nt = t_rows // tm
    nlt = n_lat // tm
    n_groups = 2 if nlt < nt else 1

    def grp(i):
        return jnp.where(i >= nlt, 1, 0) if n_groups == 2 else 0

    in_specs = [pl.BlockSpec((tm, w), functools.partial(lambda i, cb: (i, cb), cb=cb)) for (_, w, cb) in rows]
    in_specs += [pl.BlockSpec((1,) + v.shape[1:], lambda i: (grp(i), 0, 0)) for v in vecs]
    out_specs = [pl.BlockSpec((tm, w), lambda i: (i, 0)) for (w, _) in outs]
    out_specs += [pl.BlockSpec((1, 1, w), lambda i: (grp(i), 0, 0)) for w in accs]
    out_shape = [SDS((t_rows, w), dt) for (w, dt) in outs] + [SDS((n_groups, 1, w), F32) for w in accs]
    n_r, n_v, n_o = len(rows), len(vecs), len(outs)

    def body(*refs):
        row_vals = [r[...] for r in refs[:n_r]]
        vec_vals = [v[0] for v in refs[n_r:n_r + n_v]]
        out_refs = refs[n_r + n_v:n_r + n_v + n_o]
        acc_refs = refs[n_r + n_v + n_o:]
        out_vals, acc_vals = fn(row_vals, vec_vals)
        for o_ref, o in zip(out_refs, out_vals):
            o_ref[...] = o.astype(o_ref.dtype)
        if acc_refs:
            i = pl.program_id(0)
            first = (i == 0) | (i == nlt) if n_groups == 2 else i == 0

            @pl.when(first)
            def _():
                for a_ref, a in zip(acc_refs, acc_vals):
                    a_ref[0] = a

            @pl.when(jnp.logical_not(first))
            def _():
                for a_ref, a in zip(acc_refs, acc_vals):
                    a_ref[0] += a

    res = pl.pallas_call(
        body, name=name, grid=(nt,), in_specs=in_specs, out_specs=out_specs, out_shape=out_shape,
        compiler_params=_cparams(),
    )(*[r[0] for r in rows], *vecs)
    return res[:n_o], res[n_o:]


RESIDENT_BYTES = 12 * 1024 * 1024


def _mm(name, pairs, mode, out_dtype, tm_cap=256, tn_cap=512, bias=None):
    a0, b0 = pairs[0]
    if mode == "nn":
        m, n, dims = a0.shape[0], b0.shape[1], NN
    elif mode == "nt":
        m, n, dims = a0.shape[0], b0.shape[0], NT
    else:
        m, n, dims = a0.shape[1], b0.shape[1], TN
    b_bytes = sum(b.size * b.dtype.itemsize for _, b in pairs)
    tn = n if b_bytes <= RESIDENT_BYTES else _tile(n, tn_cap, 128)
    tm = _tile(m, tm_cap, 128 if mode == "tn" else 16)

    def a_spec(a):
        if mode == "tn":
            return pl.BlockSpec((a.shape[0], tm), lambda i, j: (0, i))
        return pl.BlockSpec((tm, a.shape[1]), lambda i, j: (i, 0))

    def b_spec(b):
        if mode == "nt":
            return pl.BlockSpec((tn, b.shape[1]), lambda i, j: (j, 0))
        return pl.BlockSpec((b.shape[0], tn), lambda i, j: (0, j))

    in_specs, flat = [], []
    for a, b in pairs:
        in_specs += [a_spec(a), b_spec(b)]
        flat += [a, b]
    if bias is not None:
        in_specs.append(pl.BlockSpec((1, tn), lambda i, j: (0, j)))
        flat.append(bias)
    n_pairs = len(pairs)

    def body(*refs):
        acc = None
        for p in range(n_pairs):
            t = _dot(refs[2 * p][...], refs[2 * p + 1][...], dims)
            acc = t if acc is None else acc + t
        if bias is not None:
            acc = acc + refs[2 * n_pairs][...]
        refs[-1][...] = acc.astype(refs[-1].dtype)

    return pl.pallas_call(
        body, name=name, grid=(m // tm, n // tn), in_specs=in_specs,
        out_specs=pl.BlockSpec((tm, tn), lambda i, j: (i, j)),
        out_shape=SDS((m, n), out_dtype), compiler_params=_cparams(),
    )(*flat)


def _mm_resid(name, a, b, s, mg, k, coef, n_lat):
    t_rows, n = a.shape[0], b.shape[1]
    tm = _tile(math.gcd(n_lat, t_rows), 256, 16)
    nlt = n_lat // tm
    n_groups = 2 if nlt < t_rows // tm else 1

    def grp(i):
        return jnp.where(i >= nlt, 1, 0) if n_groups == 2 else 0

    def body(a_ref, b_ref, s_ref, mg_ref, so_ref, o_ref):
        o = _dot(a_ref[...], b_ref[...], NN)
        gate = mg_ref[0, 3 * k + 2:3 * k + 3, :]
        o_ref[...] = o
        so_ref[...] = s_ref[...] + (coef * gate) * o

    row = pl.BlockSpec((tm, n), lambda i: (i, 0))
    return pl.pallas_call(
        body, name=name, grid=(t_rows // tm,),
        in_specs=[pl.BlockSpec((tm, a.shape[1]), lambda i: (i, 0)), pl.BlockSpec(b.shape, lambda i: (0, 0)), row,
                  pl.BlockSpec((1, mg.shape[1], n), lambda i: (grp(i), 0, 0))],
        out_specs=[row, row], out_shape=[SDS((t_rows, n), F32), SDS((t_rows, n), F32)], compiler_params=_cparams(),
    )(a, b, s, mg)


def _ffn_up(name, u, wg_t, wu_t):
    t_rows, f = u.shape[0], wg_t.shape[0]
    tm = _tile(t_rows, 256, 16)

    def body(u_ref, wg_ref, wu_ref, a_ref, b_ref, h_ref):
        uu = u_ref[...]
        a = _dot(uu, wg_ref[...], NT)
        b = _dot(uu, wu_ref[...], NT)
        sg = jax.nn.sigmoid(a)
        act = a * sg
        a_ref[...] = (b * (sg * (1.0 + a * (1.0 - sg)))).astype(BF16)
        b_ref[...] = act.astype(BF16)
        h_ref[...] = (act * b).astype(BF16)

    w_spec = pl.BlockSpec(wg_t.shape, lambda i: (0, 0))
    o_spec = pl.BlockSpec((tm, f), lambda i: (i, 0))
    return pl.pallas_call(
        body, name=name, grid=(t_rows // tm,),
        in_specs=[pl.BlockSpec((tm, u.shape[1]), lambda i: (i, 0)), w_spec, w_spec],
        out_specs=[o_spec, o_spec, o_spec], out_shape=[SDS((t_rows, f), BF16)] * 3, compiler_params=_cparams(),
    )(u, wg_t, wu_t)


def _ffn_dact(name, do, wd, a, b):
    t_rows, f = do.shape[0], wd.shape[0]
    tm = _tile(t_rows, 256, 16)

    def body(do_ref, wd_ref, a_ref, b_ref, da_ref, db_ref):
        dh = _dot(do_ref[...], wd_ref[...], NT)
        da_ref[...] = (dh * a_ref[...].astype(F32)).astype(BF16)
        db_ref[...] = (dh * b_ref[...].astype(F32)).astype(BF16)

    t_spec = pl.BlockSpec((tm, f), lambda i: (i, 0))
    return pl.pallas_call(
        body, name=name, grid=(t_rows // tm,),
        in_specs=[pl.BlockSpec((tm, do.shape[1]), lambda i: (i, 0)), pl.BlockSpec(wd.shape, lambda i: (0, 0)),
                  t_spec, t_spec],
        out_specs=[t_spec, t_spec], out_shape=[SDS((t_rows, f), BF16)] * 2, compiler_params=_cparams(),
    )(do, wd, a, b)


def _row_tm(t_rows, n_lat):
    return _tile(math.gcd(t_rows, n_lat), 256, 16)


def _adaln_fwd(name, s, mg, k, n_lat):
    t_rows = s.shape[0]

    def fn(rv, vv):
        m = vv[0]
        n, _ = _rms(rv[0])
        u = (n * m[9 + k:10 + k]) * (1.0 + m[3 * k + 1:3 * k + 2]) + m[3 * k:3 * k + 1]
        return [u], []

    (u,), _ = _rowwise(name, fn, t_rows, _row_tm(t_rows, n_lat), n_lat, [(s, D_MODEL, 0)], [mg], [(D_MODEL, BF16)], [])
    return u


def _adaln_bwd(name, s, du, ds_out, mg, k, n_lat):
    t_rows = s.shape[0]

    def fn(rv, vv):
        m = vv[0]
        gain, scale = m[9 + k:10 + k], m[3 * k + 1:3 * k + 2]
        n, r = _rms(rv[0])
        d_u = rv[1]
        dxn = d_u * (1.0 + scale)
        ds = _rms_bwd(n, r, dxn * gain)
        return [rv[2] + ds], [_colsum(d_u), _colsum(d_u * (n * gain)), _colsum(dxn * n)]

    (ds_in,), accs = _rowwise(name, fn, t_rows, _row_tm(t_rows, n_lat), n_lat,
                              [(s, D_MODEL, 0), (du, D_MODEL, 0), (ds_out, D_MODEL, 0)], [mg],
                              [(D_MODEL, F32)], [D_MODEL] * 3)
    return ds_in, accs


def _gate_bwd(name, ds_out, o, mg, k, coef, n_lat):
    t_rows = o.shape[0]

    def fn(rv, vv):
        gate = vv[0][3 * k + 2:3 * k + 3]
        d = coef * rv[0]
        return [d * gate], [_colsum(d * rv[1])]

    (do,), (dgate,) = _rowwise(name, fn, t_rows, _row_tm(t_rows, n_lat), n_lat,
                               [(ds_out, D_MODEL, 0), (o, D_MODEL, 0)], [mg], [(D_MODEL, BF16)], [D_MODEL])
    return do, dgate


def _rmsnorm_fwd(name, x, width, colblk, gain, t_rows):
    def fn(rv, vv):
        n, _ = _rms(rv[0])
        return [n * vv[0]], []

    (y,), _ = _rowwise(name, fn, t_rows, _tile(t_rows, 256, 16), t_rows, [(x, width, colblk)],
                       [gain.reshape(1, 1, width)], [(width, BF16)], [])
    return y


def _rmsnorm_bwd(name, x, width, colblk, dy, gain, t_rows):
    def fn(rv, vv):
        n, r = _rms(rv[0])
        return [_rms_bwd(n, r, rv[1] * vv[0])], [_colsum(rv[1] * n)]

    (dx,), (dgain,) = _rowwise(name, fn, t_rows, _tile(t_rows, 256, 16), t_rows,
                               [(x, width, colblk), (dy, width, 0)], [gain.reshape(1, 1, width)],
                               [(width, F32)], [width])
    return dx, dgain


def _final_loss(name, h, target, gain):
    t_rows = h.shape[0]
    inv_d = 1.0 / D_MODEL

    def fn(rv, vv):
        g = vv[0]
        n, r = _rms(rv[0])
        e = n * g - rv[1]
        dy = e * inv_d
        return [_rms_bwd(n, r, dy * g)], [_colsum(e * e), _colsum(dy * n)]

    (dh,), (sq, dgain) = _rowwise(name, fn, t_rows, _tile(t_rows, 256, 16), t_rows,
                                  [(h, D_MODEL, 0), (target, D_MODEL, 0)], [gain.reshape(1, 1, D_MODEL)],
                                  [(D_MODEL, F32)], [D_MODEL, D_MODEL])
    return dh, sq, dgain


def _rope(name, z, width, colblk, cos, sin, perm, backward, out_dtype):
    t_rows = cos.shape[0]

    def body(z_ref, c_ref, s_ref, p_ref, o_ref):
        zz = z_ref[...]
        pre = zz * s_ref[...] if backward else zz
        hi = pre.astype(BF16)
        lo = (pre - hi.astype(F32)).astype(BF16)
        rot = _dot(hi, p_ref[...], NN) + _dot(lo, p_ref[...], NN)
        if not backward:
            rot = rot * s_ref[...]
        o_ref[...] = (zz * c_ref[...] + rot).astype(o_ref.dtype)

    tm = _tile(t_rows, 256, 16)
    t_spec = pl.BlockSpec((tm, width), lambda i: (i, 0))
    return pl.pallas_call(
        body, name=name, grid=(t_rows // tm,),
        in_specs=[pl.BlockSpec((tm, width), lambda i: (i, colblk)), t_spec, t_spec,
                  pl.BlockSpec((width, width), lambda i: (0, 0))],
        out_specs=t_spec, out_shape=SDS((t_rows, width), out_dtype), compiler_params=_cparams(),
    )(z, cos, sin, perm)


def _window_sum(x, w, transposed):
    n_rows = x.shape[0]
    zeros = jnp.zeros((POOL_PAD, x.shape[1]), F32)
    y = jnp.concatenate([zeros, x, zeros], axis=0)
    total = n_rows + 2 * POOL_PAD
    if transposed:
        y = y + pltpu.roll(y, total - 1, 0)
    else:
        y = y + pltpu.roll(y, 1, 0)
    step = 1
    while 2 * step < w:
        y = pltpu.roll(y, step, 0) + pltpu.roll(y, total - step, 0)
        step *= 2
    return y[POOL_PAD:POOL_PAD + n_rows]


def _window_count(n_rows, w):
    t = lax.broadcasted_iota(jnp.int32, (n_rows, 1), 0)
    lo = jnp.maximum(t - w // 2, 0)
    hi = jnp.minimum(t + (w - w // 2 - 1), n_rows - 1)
    return (hi - lo + 1).astype(F32)


def _pool_fwd(name, proj, n_rows, w_grp, scale):
    def body(x_ref, w_ref, sc_ref, y_ref, p_ref):
        for g, w in enumerate(POOL_WINDOWS):
            cols = slice(g * POOL_GROUP_DIM, (g + 1) * POOL_GROUP_DIM)
            x = x_ref[:, cols]
            p = _window_sum(x, w, False) * (1.0 / _window_count(n_rows, w)) - x
            pb = p.astype(BF16)
            p_ref[:, cols] = pb
            y_ref[:, cols] = (_dot(pb, w_ref[g], NN) * sc_ref[:, cols]).astype(BF16)

    blk = pl.BlockSpec((n_rows, POOL_DIM), lambda i: (0, 0))
    return pl.pallas_call(
        body, name=name, grid=(1,),
        in_specs=[blk, pl.BlockSpec(w_grp.shape, lambda i: (0, 0, 0)), pl.BlockSpec((1, POOL_DIM), lambda i: (0, 0))],
        out_specs=[blk, blk], out_shape=[SDS((n_rows, POOL_DIM), BF16)] * 2, compiler_params=_cparams(),
    )(proj, w_grp, scale)


def _pool_bwd(name, dcat, n_rows, p, w_grp, scale):
    def body(dy_ref, p_ref, w_ref, sc_ref, dx_ref, dw_ref, dsc_ref):
        for g, w in enumerate(POOL_WINDOWS):
            cols = slice(g * POOL_GROUP_DIM, (g + 1) * POOL_GROUP_DIM)
            dy = dy_ref[:, cols]
            pb = p_ref[:, cols]
            pw = _dot(pb, w_ref[g], NN)
            dsc_ref[:, cols] = _colsum(dy * pw)
            dpw = (dy * sc_ref[:, cols]).astype(BF16)
            dw_ref[g] = _dot(pb, dpw, TN)
            dp = _dot(dpw, w_ref[g], NT)
            dx_ref[:, cols] = _window_sum(dp * (1.0 / _window_count(n_rows, w)), w, True) - dp

    blk = pl.BlockSpec((n_rows, POOL_DIM), lambda i: (0, 0))
    w_spec = pl.BlockSpec(w_grp.shape, lambda i: (0, 0, 0))
    v_spec = pl.BlockSpec((1, POOL_DIM), lambda i: (0, 0))
    return pl.pallas_call(
        body, name=name, grid=(1,), in_specs=[blk, blk, w_spec, v_spec], out_specs=[blk, w_spec, v_spec],
        out_shape=[SDS((n_rows, POOL_DIM), F32), SDS(w_grp.shape, F32), SDS((1, POOL_DIM), F32)],
        compiler_params=_cparams(),
    )(dcat, p, w_grp, scale)


def _attn_fwd(name, q, k, v):
    h, n_q, _ = q.shape
    n_k = k.shape[1]
    tq = _tile(n_q, 256, 16)

    def body(q_ref, k_ref, v_ref, o_ref, lse_ref):
        s = _dot(q_ref[...], k_ref[...], NT) * ATTN_SCALE
        m = jnp.max(s, axis=-1, keepdims=True)
        e = jnp.exp(s - m)
        l = jnp.sum(e, axis=-1, keepdims=True)
        p = (e * (1.0 / l)).astype(BF16)
        o_ref[...] = _dot(p, v_ref[...], NN).astype(BF16)
        lse_ref[...] = m + jnp.log(l)

    return pl.pallas_call(
        body, name=name, grid=(h, n_q // tq),
        in_specs=[pl.BlockSpec((None, tq, HEAD_PAD), lambda hh, i: (hh, i, 0)),
                  pl.BlockSpec((None, n_k, HEAD_PAD), lambda hh, i: (hh, 0, 0)),
                  pl.BlockSpec((None, n_k, V_HEAD), lambda hh, i: (hh, 0, 0))],
        out_specs=[pl.BlockSpec((None, tq, V_HEAD), lambda hh, i: (hh, i, 0)),
                   pl.BlockSpec((None, tq, 1), lambda hh, i: (hh, i, 0))],
        out_shape=[SDS((h, n_q, V_HEAD), BF16), SDS((h, n_q, 1), F32)], compiler_params=_cparams(),
    )(q, k, v)


def _attn_bwd(name, q, k, v, o, lse, do):
    h, n_q, _ = q.shape
    n_k = k.shape[1]
    tq = _tile(n_q, 256, 16)

    def body(q_ref, k_ref, v_ref, o_ref, lse_ref, do_ref, dq_ref, dk_ref, dv_ref, dks_ref):
        hh, i = pl.program_id(0), pl.program_id(1)
        qq, kk, dd = q_ref[...], k_ref[...], do_ref[...]
        s = _dot(qq, kk, NT) * ATTN_SCALE
        p = jnp.exp(s - lse_ref[...])
        dp = _dot(dd, v_ref[...], NT)
        delta = jnp.sum(dd.astype(F32) * o_ref[...].astype(F32), axis=-1, keepdims=True)
        ds = (p * (dp - delta) * ATTN_SCALE).astype(BF16)
        dq_ref[...] = _dot(ds, kk, NN)
        dk = _dot(ds, qq, TN)
        dv = _dot(p.astype(BF16), dd, TN)

        @pl.when(i == 0)
        def _():
            dk_ref[...] = dk
            dv_ref[...] = dv

        @pl.when(i > 0)
        def _():
            dk_ref[...] += dk
            dv_ref[...] += dv

        @pl.when((i == 0) & (hh == 0))
        def _():
            dks_ref[...] = dk

        @pl.when((i > 0) | (hh > 0))
        def _():
            dks_ref[...] += dk

    q_spec = pl.BlockSpec((None, tq, HEAD_PAD), lambda hh, i: (hh, i, 0))
    k_spec = pl.BlockSpec((None, n_k, HEAD_PAD), lambda hh, i: (hh, 0, 0))
    v_spec = pl.BlockSpec((None, n_k, V_HEAD), lambda hh, i: (hh, 0, 0))
    o_spec = pl.BlockSpec((None, tq, V_HEAD), lambda hh, i: (hh, i, 0))
    return pl.pallas_call(
        body, name=name, grid=(h, n_q // tq),
        in_specs=[q_spec, k_spec, v_spec, o_spec, pl.BlockSpec((None, tq, 1), lambda hh, i: (hh, i, 0)), o_spec],
        out_specs=[q_spec, k_spec, v_spec, pl.BlockSpec((n_k, HEAD_PAD), lambda hh, i: (0, 0))],
        out_shape=[SDS((h, n_q, HEAD_PAD), F32), SDS((h, n_k, HEAD_PAD), F32), SDS((h, n_k, V_HEAD), F32),
                   SDS((n_k, HEAD_PAD), F32)],
        compiler_params=_cparams(),
    )(q, k, v, o, lse, do)


CONV_COLS = 256


def _shift_rows(x, d):
    n_rows = x.shape[0]
    t = lax.broadcasted_iota(jnp.int32, (n_rows, 1), 0)
    if d > 0:
        return jnp.where(t >= d, pltpu.roll(x, d, 0), 0.0)
    return jnp.where(t < n_rows + d, pltpu.roll(x, n_rows + d, 0), 0.0)


def _conv_fwd(name, z3, conv_w):
    n_rows = z3.shape[0]
    nb = D_MODEL // CONV_COLS

    def body(b_ref, c_ref, v_ref, w_ref, y_ref):
        z = c_ref[...] * v_ref[...]
        zc = w_ref[0:1, :] * _shift_rows(z, 1) + w_ref[1:2, :] * z + w_ref[2:3, :] * _shift_rows(z, -1)
        y_ref[...] = (b_ref[...] * zc).astype(BF16)

    def part(k):
        return pl.BlockSpec((n_rows, CONV_COLS), lambda j: (0, k * nb + j))

    return pl.pallas_call(
        body, name=name, grid=(nb,),
        in_specs=[part(0), part(1), part(2), pl.BlockSpec((3, CONV_COLS), lambda j: (0, j))],
        out_specs=pl.BlockSpec((n_rows, CONV_COLS), lambda j: (0, j)),
        out_shape=SDS((n_rows, D_MODEL), BF16), compiler_params=_cparams(),
    )(z3, z3, z3, conv_w)


def _conv_bwd(name, dy, z3, conv_w):
    n_rows = z3.shape[0]
    nb = D_MODEL // CONV_COLS

    def body(dy_ref, b_ref, c_ref, v_ref, w_ref, db_ref, dc_ref, dv_ref, dw_ref):
        c, v, d_y = c_ref[...], v_ref[...], dy_ref[...]
        z = c * v
        z_dn, z_up = _shift_rows(z, 1), _shift_rows(z, -1)
        zc = w_ref[0:1, :] * z_dn + w_ref[1:2, :] * z + w_ref[2:3, :] * z_up
        db_ref[...] = (d_y * zc).astype(BF16)
        dzc = d_y * b_ref[...]
        dz = w_ref[0:1, :] * _shift_rows(dzc, -1) + w_ref[1:2, :] * dzc + w_ref[2:3, :] * _shift_rows(dzc, 1)
        dc_ref[...] = (dz * v).astype(BF16)
        dv_ref[...] = (dz * c).astype(BF16)
        dw_ref[0:1, :] = _colsum(dzc * z_dn)
        dw_ref[1:2, :] = _colsum(dzc * z)
        dw_ref[2:3, :] = _colsum(dzc * z_up)

    def part(k):
        return pl.BlockSpec((n_rows, CONV_COLS), lambda j: (0, k * nb + j))

    col = pl.BlockSpec((n_rows, CONV_COLS), lambda j: (0, j))
    w_spec = pl.BlockSpec((3, CONV_COLS), lambda j: (0, j))
    return pl.pallas_call(
        body, name=name, grid=(nb,), in_specs=[col, part(0), part(1), part(2), w_spec],
        out_specs=[col, col, col, w_spec],
        out_shape=[SDS((n_rows, D_MODEL), BF16)] * 3 + [SDS((3, D_MODEL), F32)], compiler_params=_cparams(),
    )(dy, z3, z3, z3, conv_w)


def _silu_rows(name, x):
    def body(x_ref, s_ref, d_ref):
        xx = x_ref[...]
        sg = jax.nn.sigmoid(xx)
        s_ref[...] = (xx * sg).astype(BF16)
        d_ref[...] = sg * (1.0 + xx * (1.0 - sg))

    return pl.pallas_call(body, name=name, out_shape=[SDS(x.shape, BF16), SDS(x.shape, F32)])(x)


def _sum_rows(name, x, scale=None):
    r, n = x.shape
    tn = _tile(n, 8192, 128)

    def body(*refs):
        acc = jnp.sum(refs[0][...].astype(F32), axis=0, keepdims=True)
        if scale is not None:
            acc = acc * refs[1][...]
        refs[-1][...] = acc

    in_specs = [pl.BlockSpec((r, tn), lambda j: (0, j))]
    args = [x]
    if scale is not None:
        in_specs.append(pl.BlockSpec((1, tn), lambda j: (0, j)))
        args.append(scale)
    return pl.pallas_call(body, name=name, grid=(n // tn,), in_specs=in_specs,
                          out_specs=pl.BlockSpec((1, tn), lambda j: (0, j)), out_shape=SDS((1, n), F32))(*args)


def _sum_slots(name, x):
    n_slots, r, c = x.shape
    tr = _tile(r, 432, 16)

    def body(x_ref, o_ref):
        acc = x_ref[0].astype(F32)
        for sl in range(1, n_slots):
            acc = acc + x_ref[sl].astype(F32)
        o_ref[...] = acc

    return pl.pallas_call(body, name=name, grid=(r // tr,),
                          in_specs=[pl.BlockSpec((n_slots, tr, c), lambda i: (0, i, 0))],
                          out_specs=pl.BlockSpec((tr, c), lambda i: (i, 0)), out_shape=SDS((r, c), F32),
                          compiler_params=_cparams())(x)


def _adamw(name, w, g, m, v):
    shape = w.shape
    cols = shape[-1]
    rows = w.size // cols
    tr = _tile(rows, 512, 8)
    bc1 = 1.0 - ADAM_B1 ** ADAM_STEP
    bc2 = 1.0 - ADAM_B2 ** ADAM_STEP

    def body(w_ref, g_ref, m_ref, v_ref, d_ref, nm_ref, nv_ref):
        gg = g_ref[...]
        nm = ADAM_B1 * m_ref[...] + (1.0 - ADAM_B1) * gg
        nv = ADAM_B2 * v_ref[...] + (1.0 - ADAM_B2) * (gg * gg)
        nm_ref[...] = nm
        nv_ref[...] = nv
        d_ref[...] = -ADAM_LR * ((nm / bc1) / (jnp.sqrt(nv / bc2) + ADAM_EPS) + ADAM_WD * w_ref[...])

    spec = pl.BlockSpec((tr, cols), lambda i: (i, 0))
    outs = pl.pallas_call(body, name=name, grid=(rows // tr,), in_specs=[spec] * 4, out_specs=[spec] * 3,
                          out_shape=[SDS((rows, cols), F32)] * 3, compiler_params=_cparams())(
        w.reshape(rows, cols), g.reshape(rows, cols), m.reshape(rows, cols), v.reshape(rows, cols))
    return tuple(t.reshape(shape) for t in outs)


def _exchange(name, x, scatter):
    blk = x.shape[1:] if scatter else x.shape

    def body(x_ref, out_ref, send_sems, recv_sems, local_sem):
        mx, my, mc = lax.axis_index("x"), lax.axis_index("y"), lax.axis_index("c")
        me = 4 * mx + 2 * my + mc
        own = pltpu.make_async_copy(x_ref.at[me] if scatter else x_ref, out_ref.at[me], local_sem)
        own.start()
        copies = []
        for kk in range(1, N_DEV):
            px = jnp.bitwise_xor(mx, (kk >> 2) & 1)
            py = jnp.bitwise_xor(my, (kk >> 1) & 1)
            pc = jnp.bitwise_xor(mc, kk & 1)
            peer = 4 * px + 2 * py + pc
            send = pltpu.make_async_remote_copy(
                src_ref=x_ref.at[peer] if scatter else x_ref, dst_ref=out_ref.at[me],
                send_sem=send_sems.at[kk - 1], recv_sem=recv_sems.at[kk - 1],
                device_id=(px, py, pc), device_id_type=MESH)
            send.start()
            arrival = pltpu.make_async_remote_copy(
                src_ref=x_ref.at[peer] if scatter else x_ref, dst_ref=out_ref.at[peer],
                send_sem=send_sems.at[kk - 1], recv_sem=recv_sems.at[kk - 1],
                device_id=(px, py, pc), device_id_type=MESH)
            copies.append((send, arrival))
        for send, arrival in copies:
            arrival.wait_recv()
            send.wait_send()
        own.wait()

    return pl.pallas_call(
        body, name=name, out_shape=SDS((N_DEV,) + tuple(blk), x.dtype),
        in_specs=[pl.BlockSpec(memory_space=pl.ANY)], out_specs=pl.BlockSpec(memory_space=pl.ANY),
        scratch_shapes=[pltpu.SemaphoreType.DMA((N_DEV - 1,)), pltpu.SemaphoreType.DMA((N_DEV - 1,)),
                        pltpu.SemaphoreType.DMA],
    )(x)


def _rope_perm(pre, reps, post):
    half = QK_ROPE // 4
    width = reps * (pre + QK_ROPE) + post
    p = np.zeros((width, width), np.float32)
    for rep in range(reps):
        s0 = rep * (pre + QK_ROPE) + pre
        for base in (s0, s0 + 2 * half):
            for i in range(half):
                p[base + half + i, base + i] = -1.0
                p[base + i, base + half + i] = 1.0
    return p


def _rope_tables(n_lat, t_rows, pre, reps, post):
    half = QK_ROPE // 4
    pos = jnp.arange(n_lat)
    freqs = jnp.power(ROPE_THETA, -jnp.arange(0, 2 * half, 2, dtype=F32) / (2 * half))
    ang_r = (pos // GRID_W).astype(F32)[:, None] * freqs
    ang_c = (pos % GRID_W).astype(F32)[:, None] * freqs
    ang = jnp.concatenate([ang_r, ang_r, ang_c, ang_c], axis=-1)

    def table(fn, plain):
        slot = jnp.concatenate([jnp.full((n_lat, pre), plain, F32), fn(ang)], axis=-1)
        t = jnp.concatenate([jnp.tile(slot, (1, reps)), jnp.full((n_lat, post), plain, F32)], axis=-1)
        return jnp.concatenate([t, jnp.full((t_rows - n_lat, t.shape[1]), plain, F32)], axis=0)

    return table(jnp.cos, 1.0), table(jnp.sin, 0.0)


def _ffn_half_fwd(tag, s, mg, k, feed, i, coef, n_lat):
    u = _adaln_fwd(f"{tag}_adaln", s, mg, k, n_lat)
    wg_t, wu_t = feed.weights(f"{tag}_up", [f"gate_t{i}", f"up_t{i}"], u)
    a, b, hid = _ffn_up(f"{tag}_up", u, wg_t, wu_t)
    (wd,) = feed.weights(f"{tag}_down", [f"down{i}"], hid)
    s_out, o = _mm_resid(f"{tag}_down", hid, wd, s, mg, k, coef, n_lat)
    return s_out, (s, u, a, b, hid, o, wg_t, wu_t, wd)


def _ffn_half_bwd(tag, ds_out, saved, mg, k, feed, i, coef, n_lat):
    s, u, a, b, hid, o, wg_t, wu_t, wd = saved
    do, dgate = _gate_bwd(f"{tag}_dgate", ds_out, o, mg, k, coef, n_lat)
    da, db = _ffn_dact(f"{tag}_dact", do, wd, a, b)
    dwd = _mm(f"{tag}_dwd", [(hid, do)], "tn", BF16)
    dwg_t = _mm(f"{tag}_dwg", [(da, u)], "tn", BF16)
    dwu_t = _mm(f"{tag}_dwu", [(db, u)], "tn", BF16)
    token = feed.grads(tag, {f"down{i}": dwd, f"gate_t{i}": dwg_t, f"up_t{i}": dwu_t})
    du = _mm(f"{tag}_du", [(da, wg_t), (db, wu_t)], "nn", F32, 384, 512, bias=_after(token))
    ds_in, (dshift, dscale, dgain) = _adaln_bwd(f"{tag}_dadaln", s, du, ds_out, mg, k, n_lat)
    return ds_in, dict(shift=dshift, scale=dscale, gate=dgate, gain=dgain)


def _after(token):
    return jnp.zeros((1, D_MODEL), F32) + token


def _mod_grad(parts, n_groups):
    rows = []
    zero = jnp.zeros((n_groups, 1, D_MODEL), F32)
    for k in range(3):
        for nm in ("shift", "scale", "gate"):
            t = parts[k].get(nm, zero)
            if t.shape[0] < n_groups:
                t = jnp.concatenate([t, jnp.zeros((n_groups - t.shape[0], 1, D_MODEL), F32)], axis=0)
            rows.append(t)
    return jnp.concatenate(rows, axis=1).reshape(n_groups, N_MOD * D_MODEL)


def _local_step(x, ctx, target, mod_h, mod_g, norm_g, feed, pool_w, pool_scale, q_norm_g, kv_norm_g, conv_w,
                final_norm_g):
    n_lat, n_ctx = x.shape[0], ctx.shape[0]
    t_all = n_lat + n_ctx
    mg0 = jnp.stack([jnp.concatenate([mod_h[0], norm_g[0]], axis=0), jnp.concatenate([mod_g, norm_g[0]], axis=0)])
    mg1 = jnp.concatenate([mod_h[1], norm_g[1]], axis=0)[None]

    s0 = jnp.concatenate([x, ctx], axis=0) + feed.start_token()
    s1, sv_f00 = _ffn_half_fwd("l0f0", s0, mg0, 0, feed, 0, 0.5, n_lat)

    ua = _adaln_fwd("l0m_adaln", s1, mg0, 1, n_lat)
    w_in, w_uq, w_ukv_t, w_ab_out = feed.weights("l0m", ["in_t", "uq", "ukv_t", "ab_out"], ua)
    kv_rows = KV_RANK + QK_ROPE
    w_in_t = jnp.concatenate([
        w_in[:POOL_DIM], jnp.zeros((PA_CQ - POOL_DIM, D_MODEL), BF16), w_in[POOL_DIM:POOL_DIM + Q_RANK],
        w_in[POOL_DIM + Q_RANK:], jnp.zeros((PA_KV_W - kv_rows, D_MODEL), BF16)], axis=0)
    proj = _mm("l0m_proj", [(ua, w_in_t)], "nt", F32, 768, 384)
    pool_y, pool_p = _pool_fwd("l0m_pool", proj, n_lat, pool_w.astype(BF16), pool_scale)
    nq = _rmsnorm_fwd("l0m_qnorm", proj, Q_RANK, PA_CQ // Q_RANK, q_norm_g, n_lat)
    q_lin = _mm("l0m_q", [(nq, w_uq)], "nn", F32, 512, 768)
    cos_q, sin_q = _rope_tables(n_lat, n_lat, QK_NOPE, HEADS, 0)
    perm_q = _rope_perm(QK_NOPE, HEADS, 0)
    q_rot = _rope("l0m_qrope", q_lin, Q_RANK, 0, cos_q, sin_q, jnp.asarray(perm_q, BF16), False, BF16)
    cos_k, sin_k = _rope_tables(n_lat, t_all, KV_RANK, 1, PA_KV_W - kv_rows)
    perm_k = _rope_perm(KV_RANK, 1, PA_KV_W - kv_rows)
    kvr = _rope("l0m_krope", proj, PA_KV_W, PA_KV // PA_KV_W, cos_k, sin_k, jnp.asarray(perm_k, BF16), False, F32)
    nkv = _rmsnorm_fwd("l0m_kvnorm", kvr, KV_RANK, 0, kv_norm_g, t_all)
    kv = _mm("l0m_kv", [(nkv, w_ukv_t)], "nt", BF16, 768, 512)
    qh = jnp.pad(q_rot.reshape(n_lat, HEADS, QK_HEAD), ((0, 0), (0, 0), (0, HEAD_PAD - QK_HEAD))).transpose(1, 0, 2)
    kvh = kv.reshape(t_all, HEADS, QK_NOPE + V_HEAD)
    k_rope = jnp.broadcast_to(kvr[:, None, KV_RANK:KV_RANK + QK_ROPE].astype(BF16), (t_all, HEADS, QK_ROPE))
    kh = jnp.concatenate([kvh[:, :, :QK_NOPE], k_rope, jnp.zeros((t_all, HEADS, HEAD_PAD - QK_HEAD), BF16)],
                         axis=-1).transpose(1, 0, 2)
    vh = kvh[:, :, QK_NOPE:].transpose(1, 0, 2)
    oh, lse = _attn_fwd("l0m_attn", qh, kh, vh)
    cat = jnp.concatenate([pool_y, oh.transpose(1, 0, 2).reshape(n_lat, HEADS * V_HEAD)], axis=-1)
    h1 = s1[:n_lat]
    h2, mix_o = _mm_resid("l0m_out", cat, w_ab_out, h1, mg0[:1], 1, 1.0, n_lat)

    h3, sv_f01 = _ffn_half_fwd("l0f1", h2, mg0[:1], 2, feed, 1, 0.5, n_lat)

    h4, sv_f10 = _ffn_half_fwd("l1f0", h3, mg1, 0, feed, 2, 0.5, n_lat)
    uc = _adaln_fwd("l1m_adaln", h4, mg1, 1, n_lat)
    w_cin_t, w_c_out = feed.weights("l1m", ["cin_t", "c_out"], uc)
    z3 = _mm("l1m_in", [(uc, w_cin_t)], "nt", F32)
    yc = _conv_fwd("l1m_conv", z3, conv_w)
    h5, conv_o = _mm_resid("l1m_out", yc, w_c_out, h4, mg1, 1, 1.0, n_lat)
    h6, sv_f11 = _ffn_half_fwd("l1f1", h5, mg1, 2, feed, 3, 0.5, n_lat)

    dh6, sq_cols, d_final_g = _final_loss("loss_head", h6, target, final_norm_g)
    g = {}
    dh5, g["f11"] = _ffn_half_bwd("l1f1", dh6, sv_f11, mg1, 2, feed, 3, 0.5, n_lat)

    do_c, dgate_c = _gate_bwd("l1m_dgate", dh5, conv_o, mg1, 1, 1.0, n_lat)
    dyc = _mm("l1m_dy", [(do_c, w_c_out)], "nt", F32)
    d_c_out = _mm("l1m_dwout", [(yc, do_c)], "tn", BF16)
    db_, dc_, dv_, d_conv_w = _conv_bwd("l1m_dconv", dyc, z3, conv_w)
    dz3 = jnp.concatenate([db_, dc_, dv_], axis=-1)
    d_cin_t = _mm("l1m_dwin", [(dz3, uc)], "tn", BF16)
    token = feed.grads("l1m", {"c_out": d_c_out, "cin_t": d_cin_t})
    duc = _mm("l1m_du", [(dz3, w_cin_t)], "nn", F32, bias=_after(token))
    dh4, (dsh_c, dsc_c, dgn_c) = _adaln_bwd("l1m_dadaln", h4, duc, dh5, mg1, 1, n_lat)
    dh3, g["f10"] = _ffn_half_bwd("l1f0", dh4, sv_f10, mg1, 0, feed, 2, 0.5, n_lat)

    dh2, g["f01"] = _ffn_half_bwd("l0f1", dh3, sv_f01, mg0[:1], 2, feed, 1, 0.5, n_lat)

    do_a, dgate_a = _gate_bwd("l0m_dgate", dh2, mix_o, mg0[:1], 1, 1.0, n_lat)
    dcat = _mm("l0m_dcat", [(do_a, w_ab_out)], "nt", F32)
    d_ab_out = _mm("l0m_dwout", [(cat, do_a)], "tn", BF16)
    d_pool_x, d_pool_w, d_pool_scale = _pool_bwd("l0m_dpool", dcat, n_lat, pool_p, pool_w.astype(BF16), pool_scale)
    doh = dcat[:, POOL_DIM:].reshape(n_lat, HEADS, V_HEAD).transpose(1, 0, 2).astype(BF16)
    dqh, dkh, dvh, dk_sum = _attn_bwd("l0m_dattn", qh, kh, vh, oh, lse, doh)
    dq_rot = dqh[:, :, :QK_HEAD].transpose(1, 0, 2).reshape(n_lat, Q_RANK)
    dq_lin = _rope("l0m_dqrope", dq_rot, Q_RANK, 0, cos_q, sin_q, jnp.asarray(perm_q.T, BF16), True, BF16)
    d_uq = _mm("l0m_dwuq", [(nq, dq_lin)], "tn", BF16, 768, 768)
    dnq = _mm("l0m_dnq", [(dq_lin, w_uq)], "nt", F32, 512, 768)
    dcq, d_q_norm_g = _rmsnorm_bwd("l0m_dqnorm", proj, Q_RANK, PA_CQ // Q_RANK, dnq, q_norm_g, n_lat)
    dkv = jnp.concatenate([dkh[:, :, :QK_NOPE], dvh], axis=-1).transpose(1, 0, 2).reshape(t_all, HEADS * HEAD_PAD)
    dkv = dkv.astype(BF16)
    dnkv = _mm("l0m_dnkv", [(dkv, w_ukv_t)], "nn", F32, 768, 256)
    d_ukv_t = _mm("l0m_dwukv", [(dkv, nkv)], "tn", BF16, 512, 256)
    dckv, d_kv_norm_g = _rmsnorm_bwd("l0m_dkvnorm", kvr, KV_RANK, 0, dnkv, kv_norm_g, t_all)
    dkvr = jnp.concatenate([dckv, dk_sum[:, QK_NOPE:QK_HEAD],
                            jnp.zeros((t_all, PA_KV_W - KV_RANK - QK_ROPE), F32)], axis=-1)
    dpb = _rope("l0m_dkrope", dkvr, PA_KV_W, 0, cos_k, sin_k, jnp.asarray(perm_k.T, BF16), True, F32)
    dproj_lat = jnp.concatenate([d_pool_x, jnp.zeros((n_lat, PA_CQ - POOL_DIM), F32), dcq, dpb[:n_lat]], axis=-1)
    dproj_ctx = jnp.concatenate([jnp.zeros((n_ctx, PA_KV), F32), dpb[n_lat:]], axis=-1)
    dproj = jnp.concatenate([dproj_lat, dproj_ctx], axis=0).astype(BF16)
    d_in_pad = _mm("l0m_dwin", [(dproj, ua)], "tn", BF16, 640, 512)
    d_in_t = jnp.concatenate([d_in_pad[:POOL_DIM], d_in_pad[PA_CQ:PA_CQ + Q_RANK],
                              d_in_pad[PA_KV:PA_KV + kv_rows]], axis=0)
    token = feed.grads("l0m", {"ab_out": d_ab_out, "uq": d_uq, "ukv_t": d_ukv_t, "in_t": d_in_t})
    dua = _mm("l0m_du", [(dproj, w_in_t)], "nn", F32, 768, 512, bias=_after(token))
    dh2_all = jnp.concatenate([dh2, jnp.zeros((n_ctx, D_MODEL), F32)], axis=0)
    ds1, (dsh_a, dsc_a, dgn_a) = _adaln_bwd("l0m_dadaln", s1, dua, dh2_all, mg0, 1, n_lat)
    ds0, g["f00"] = _ffn_half_bwd("l0f0", ds1, sv_f00, mg0, 0, feed, 0, 0.5, n_lat)

    dmod0 = _mod_grad([g["f00"], dict(shift=dsh_a, scale=dsc_a, gate=dgate_a), g["f01"]], 2)
    dmod1 = _mod_grad([g["f10"], dict(shift=dsh_c, scale=dsc_c, gate=dgate_c), g["f11"]], 1)
    d_norm_g = jnp.stack([
        jnp.concatenate([jnp.sum(g["f00"]["gain"], axis=0), jnp.sum(dgn_a, axis=0), g["f01"]["gain"][0]], axis=0),
        jnp.concatenate([g["f10"]["gain"][0], dgn_c[0], g["f11"]["gain"][0]], axis=0)])
    grads = dict(
        pool_w=d_pool_w, pool_scale=d_pool_scale, q_norm_g=d_q_norm_g[0], kv_norm_g=d_kv_norm_g[0],
        conv_w=d_conv_w, final_norm_g=d_final_g[0], norm_g=d_norm_g,
        mod_h=jnp.stack([dmod0[0], dmod1[0]]), mod_g=dmod0[1])
    return sq_cols, ds0, grads


HBM_SPEC = pl.BlockSpec(memory_space=pltpu.HBM)
SEM_SPEC = pl.BlockSpec(memory_space=pltpu.SEMAPHORE)
ANY_SPEC = pl.BlockSpec(memory_space=pl.ANY)
SIDE_EFFECT = pltpu.SideEffectType.DATAFLOW_SIDE_EFFECTING
N_PEERS = N_DEV - 1


def _mesh_place():
    mx, my, mc = lax.axis_index("x"), lax.axis_index("y"), lax.axis_index("c")
    return mx, my, mc, 4 * mx + 2 * my + mc


def _peer(place, kk):
    mx, my, mc, _ = place
    px = jnp.bitwise_xor(mx, (kk >> 2) & 1)
    py = jnp.bitwise_xor(my, (kk >> 1) & 1)
    pc = jnp.bitwise_xor(mc, kk & 1)
    return (px, py, pc), 4 * px + 2 * py + pc


def _hbm(a):
    return pltpu.with_memory_space_constraint(a, pltpu.HBM)


def _landing(block, me):
    zone = lax.empty((N_DEV,) + block.shape, block.dtype)
    return lax.dynamic_update_slice(zone, block[None], (me,) + (0,) * block.ndim)


ALL_PEERS = tuple(range(1, N_DEV))
SIBLING = 1
CHIP_PEERS = (2, 4, 6)
RELAYED = (3, 5, 7)


def _exchange_start(name, srcs, lands, scatter, after, peers=ALL_PEERS):
    n = len(srcs)

    def body(*refs):
        src, land = refs[:n], refs[n:2 * n]
        send_sems, recv_sems, token = refs[2 * n + 1], refs[2 * n + 2], refs[-1]
        place = _mesh_place()
        for a in range(n):
            for kk in peers:
                dev, peer = _peer(place, kk)
                pltpu.make_async_remote_copy(
                    src_ref=src[a].at[peer] if scatter else src[a], dst_ref=land[a].at[place[3]],
                    send_sem=send_sems.at[a * N_PEERS + kk - 1], recv_sem=recv_sems.at[a * N_PEERS + kk - 1],
                    device_id=dev, device_id_type=MESH).start()
        token[...] = jnp.zeros_like(token)

    thru = [pltpu.HBM(t.shape, t.dtype) for t in (*srcs, *lands)]
    res = pl.pallas_call(
        body, name=name,
        out_shape=(pltpu.SemaphoreType.DMA((n * N_PEERS,)), pltpu.SemaphoreType.DMA((n * N_PEERS,)), *thru,
                   SDS((8, 128), F32)),
        in_specs=[HBM_SPEC] * (2 * n) + [ANY_SPEC],
        out_specs=(SEM_SPEC, SEM_SPEC, *([HBM_SPEC] * (2 * n)), pl.BlockSpec(memory_space=pltpu.VMEM)),
        input_output_aliases={i: 2 + i for i in range(2 * n)},
        compiler_params=pltpu.CompilerParams(has_side_effects=SIDE_EFFECT),
    )(*[_hbm(s) for s in srcs], *[_hbm(t) for t in lands], after)
    return res[0], res[1], list(res[2:2 + n]), list(res[2 + n:2 + 2 * n]), res[-1]


def _exchange_wait(name, send_sems, recv_sems, srcs, lands, places, scatter, after):
    n = len(srcs)

    def body(*refs):
        src, land = refs[:n], refs[n:2 * n]
        send, recv = refs[2 * n], refs[2 * n + 1]
        place = _mesh_place()
        for a in range(n):
            for kk in range(1, N_DEV):
                dev, peer = _peer(place, kk)
                cp = pltpu.make_async_remote_copy(
                    src_ref=src[a].at[peer] if scatter else src[a], dst_ref=land[a].at[peer],
                    send_sem=send.at[places[a] * N_PEERS + kk - 1], recv_sem=recv.at[places[a] * N_PEERS + kk - 1],
                    device_id=dev, device_id_type=MESH)
                cp.wait_send()
                cp.wait_recv()

    thru = [pltpu.HBM(t.shape, t.dtype) for t in (*srcs, *lands)]
    res = pl.pallas_call(
        body, name=name, out_shape=tuple(thru),
        in_specs=[HBM_SPEC] * (2 * n) + [SEM_SPEC, SEM_SPEC, ANY_SPEC], out_specs=tuple([HBM_SPEC] * (2 * n)),
        input_output_aliases={i: i for i in range(2 * n)},
        compiler_params=pltpu.CompilerParams(has_side_effects=SIDE_EFFECT),
    )(*srcs, *lands, send_sems, recv_sems, after)
    return list(res[n:])


def _gather_relay(name, send1, recv1, lands, places, after):
    n = len(lands)

    def body(*refs):
        land, s1, r1 = refs[:n], refs[n], refs[n + 1]
        s2, r2 = refs[n + 3], refs[n + 4]
        place = _mesh_place()
        sibling = _peer(place, SIBLING)[0]
        for a in range(n):
            for j, kk in enumerate(CHIP_PEERS):
                dev, origin = _peer(place, kk)
                block = land[a].at[origin]
                pltpu.make_async_remote_copy(
                    src_ref=block, dst_ref=block, send_sem=s1.at[places[a] * N_PEERS + kk - 1],
                    recv_sem=r1.at[places[a] * N_PEERS + kk - 1], device_id=dev, device_id_type=MESH).wait_recv()
                pltpu.make_async_remote_copy(
                    src_ref=block, dst_ref=block, send_sem=s2.at[a * 3 + j], recv_sem=r2.at[a * 3 + j],
                    device_id=sibling, device_id_type=MESH).start()

    res = pl.pallas_call(
        body, name=name,
        out_shape=(pltpu.SemaphoreType.DMA((3 * n,)), pltpu.SemaphoreType.DMA((3 * n,)),
                   *[pltpu.HBM(t.shape, t.dtype) for t in lands]),
        in_specs=[HBM_SPEC] * n + [SEM_SPEC, SEM_SPEC, ANY_SPEC],
        out_specs=(SEM_SPEC, SEM_SPEC, *([HBM_SPEC] * n)),
        input_output_aliases={i: 2 + i for i in range(n)},
        compiler_params=pltpu.CompilerParams(has_side_effects=SIDE_EFFECT),
    )(*lands, send1, recv1, after)
    return res[0], res[1], list(res[2:])


def _gather_wait(name, send1, recv1, send2, recv2, srcs, lands, places, after):
    n = len(lands)

    def body(*refs):
        src, land = refs[:n], refs[n:2 * n]
        s1, r1, s2, r2 = refs[2 * n:2 * n + 4]
        place = _mesh_place()
        for a in range(n):
            for kk in (SIBLING,) + CHIP_PEERS:
                dev, origin = _peer(place, kk)
                first = pltpu.make_async_remote_copy(
                    src_ref=src[a], dst_ref=land[a].at[origin], send_sem=s1.at[places[a] * N_PEERS + kk - 1],
                    recv_sem=r1.at[places[a] * N_PEERS + kk - 1], device_id=dev, device_id_type=MESH)
                first.wait_send()
                if kk == SIBLING:
                    first.wait_recv()
            for j, kk in enumerate(CHIP_PEERS):
                dev, origin = _peer(place, kk + 1)
                relay = pltpu.make_async_remote_copy(
                    src_ref=src[a], dst_ref=land[a].at[origin], send_sem=s2.at[a * 3 + j], recv_sem=r2.at[a * 3 + j],
                    device_id=dev, device_id_type=MESH)
                relay.wait_send()
                relay.wait_recv()

    arrays = (*srcs, *lands)
    res = pl.pallas_call(
        body, name=name, out_shape=tuple(pltpu.HBM(t.shape, t.dtype) for t in arrays),
        in_specs=[HBM_SPEC] * (2 * n) + [SEM_SPEC] * 4 + [ANY_SPEC], out_specs=tuple([HBM_SPEC] * (2 * n)),
        input_output_aliases={i: i for i in range(2 * n)},
        compiler_params=pltpu.CompilerParams(has_side_effects=SIDE_EFFECT),
    )(*arrays, send1, recv1, send2, recv2, after)
    return list(res[n:])


class _Feed:
    def __init__(self, shards, groups, me, after):
        self.names, self.me, self.groups, self.pos = list(shards), me, groups, 0
        srcs = [shards[nm] for nm in self.names]
        lands = [_landing(s, me) for s in srcs]
        self.send, self.recv, self.srcs, self.lands, self.token = _exchange_start(
            "gather_start", srcs, lands, False, after, (SIBLING,) + CHIP_PEERS)
        self.relay = self._relay("gather_relay_first", groups[0], self.token)
        self.pending = []

    def _relay(self, name, names, after):
        places = [self.names.index(nm) for nm in names]
        send2, recv2, lands = _gather_relay(name, self.send, self.recv, [self.lands[i] for i in places], places, after)
        for i, t in zip(places, lands):
            self.lands[i] = t
        return send2, recv2

    def start_token(self):
        return self.token[0, 0]

    def weights(self, tag, names, after):
        assert names == self.groups[self.pos], (names, self.groups[self.pos])
        send2, recv2 = self.relay
        if self.pos + 1 < len(self.groups):
            nxt = self.groups[self.pos + 1]
            self.relay = self._relay(f"gather_relay_{tag}", nxt, after)
            after = self.lands[self.names.index(nxt[0])]
        places = [self.names.index(nm) for nm in names]
        got = _gather_wait(f"gather_wait_{tag}", self.send, self.recv, send2, recv2, [self.srcs[i] for i in places],
                           [self.lands[i] for i in places], places, after)
        self.pos += 1
        return [t.reshape((N_DEV * t.shape[1],) + t.shape[2:]) for t in got]

    def grads(self, tag, full):
        names = list(full)
        srcs = [full[nm].reshape((N_DEV, full[nm].shape[0] // N_DEV) + full[nm].shape[1:]) for nm in names]
        lands = [_landing(lax.dynamic_index_in_dim(s, self.me, 0, keepdims=False), self.me) for s in srcs]
        send, recv, srcs, lands, token = _exchange_start(f"scatter_start_{tag}", srcs, lands, True, srcs[0])
        self.pending.append((tag, names, send, recv, srcs, lands))
        return token[0, 0]

    def finish(self, after):
        out = {}
        for tag, names, send, recv, srcs, lands in self.pending:
            got = _exchange_wait(f"scatter_wait_{tag}", send, recv, srcs, lands, list(range(len(names))), True, after)
            for nm, slots in zip(names, got):
                out[nm] = _sum_slots(f"reduce_{nm}", slots)
        return out


def _adam_all(names, weights, grads, moms, vels, transposed):
    deltas, new_m, new_v = [], [], []
    for nm, w, g, m, v in zip(names, weights, grads, moms, vels):
        if nm in transposed:
            swap = lambda t: jnp.swapaxes(t, -1, -2)
            d, a, b = (swap(t) for t in _adamw(f"adamw_{nm}", swap(w), g, swap(m), swap(v)))
        else:
            d, a, b = _adamw(f"adamw_{nm}", w, g.reshape(w.shape), m, v)
        deltas.append(d)
        new_m.append(a)
        new_v.append(b)
    return deltas, new_m, new_v


WEIGHT_NAMES = ("c_ctx", "norm_g", "w_mod", "b_mod", "ffn_w_gate", "ffn_w_up", "ffn_w_down", "ab_w_in", "pool_w",
                "pool_scale", "q_norm_g", "w_uq", "kv_norm_g", "w_ukv", "ab_w_out", "conv_w_in", "conv_w",
                "conv_w_out", "final_norm_g")


def kernel(x, c, ctx, c_ctx, norm_g, w_mod, b_mod, ffn_w_gate, ffn_w_up, ffn_w_down, ab_w_in, pool_w, pool_scale, q_norm_g, w_uq, kv_norm_g, w_ukv, ab_w_out, conv_w_in, conv_w, conv_w_out, final_norm_g, loss_target, m_c_ctx, m_norm_g, m_w_mod, m_b_mod, m_ffn_w_gate, m_ffn_w_up, m_ffn_w_down, m_ab_w_in, m_pool_w, m_pool_scale, m_q_norm_g, m_w_uq, m_kv_norm_g, m_w_ukv, m_ab_w_out, m_conv_w_in, m_conv_w, m_conv_w_out, m_final_norm_g, v_c_ctx, v_norm_g, v_w_mod, v_b_mod, v_ffn_w_gate, v_ffn_w_up, v_ffn_w_down, v_ab_w_in, v_pool_w, v_pool_scale, v_q_norm_g, v_w_uq, v_kv_norm_g, v_w_ukv, v_ab_w_out, v_conv_w_in, v_conv_w, v_conv_w_out, v_final_norm_g):
    weights = (c_ctx, norm_g, w_mod, b_mod, ffn_w_gate, ffn_w_up, ffn_w_down, ab_w_in, pool_w, pool_scale, q_norm_g,
               w_uq, kv_norm_g, w_ukv, ab_w_out, conv_w_in, conv_w, conv_w_out, final_norm_g)
    moms = (m_c_ctx, m_norm_g, m_w_mod, m_b_mod, m_ffn_w_gate, m_ffn_w_up, m_ffn_w_down, m_ab_w_in, m_pool_w,
            m_pool_scale, m_q_norm_g, m_w_uq, m_kv_norm_g, m_w_ukv, m_ab_w_out, m_conv_w_in, m_conv_w, m_conv_w_out,
            m_final_norm_g)
    vels = (v_c_ctx, v_norm_g, v_w_mod, v_b_mod, v_ffn_w_gate, v_ffn_w_up, v_ffn_w_down, v_ab_w_in, v_pool_w,
            v_pool_scale, v_q_norm_g, v_w_uq, v_kv_norm_g, v_w_ukv, v_ab_w_out, v_conv_w_in, v_conv_w, v_conv_w_out,
            v_final_norm_g)
    me = 4 * lax.axis_index("x") + 2 * lax.axis_index("y") + lax.axis_index("c")
    n_lat, n_ctx = x.shape[1], ctx.shape[1]
    d = D_MODEL
    mod_cols = w_mod.shape[-1]
    ng_sh, cw_sh = norm_g.shape[-1], conv_w.shape[-1]

    small = jnp.concatenate([c.reshape(-1), norm_g.reshape(-1), conv_w.reshape(-1)])
    small_n = -(-small.shape[0] // 1024) * 1024
    small = jnp.pad(small, (0, small_n - small.shape[0])).reshape(small_n // 128, 128)
    small_all = _exchange("gather_small", small, False).reshape(N_DEV, small_n)
    c_all = small_all[:, :d]
    o1 = d + 6 * ng_sh
    norm_g_full = small_all[:, d:o1].reshape(N_DEV, 2, 3, ng_sh).transpose(1, 2, 0, 3).reshape(2, 3, d)
    conv_w_full = small_all[:, o1:o1 + 3 * cw_sh].reshape(N_DEV, 3, cw_sh).transpose(1, 0, 2).reshape(3, d)

    cond = jnp.concatenate([c_all, jnp.broadcast_to(c_ctx[None, :], (N_DEV, d))], axis=0)
    sil, dsil = _silu_rows("mod_silu", cond)
    w_mod_b = w_mod.astype(BF16)
    b_sh = lax.dynamic_slice(b_mod, (0, me * mod_cols), (2, mod_cols))
    m_part = jnp.stack([_mm(f"mod_fwd{l}", [(sil, w_mod_b[l])], "nn", F32, 16, 384, bias=b_sh[l:l + 1])
                        for l in range(2)], axis=1)
    m_all = _exchange("gather_mod", m_part.reshape(-1, 128), False).reshape(N_DEV, 2 * N_DEV, 2, mod_cols)
    m_mine = lax.dynamic_index_in_dim(m_all, me, axis=1, keepdims=False)
    mod_h = m_mine.transpose(1, 0, 2).reshape(2, N_MOD, d)
    mod_g = m_all[:, N_DEV, 0, :].reshape(N_MOD, d)

    def ffn_shards(i):
        return {f"gate_t{i}": ffn_w_gate[i // 2, i % 2].T, f"up_t{i}": ffn_w_up[i // 2, i % 2].T,
                f"down{i}": ffn_w_down[i // 2, i % 2]}

    local = {**ffn_shards(0), "in_t": ab_w_in[0].T, "uq": w_uq[0], "ukv_t": w_ukv[0].T, "ab_out": ab_w_out[0],
             **ffn_shards(1), **ffn_shards(2), "cin_t": conv_w_in[0].T, "c_out": conv_w_out[0], **ffn_shards(3)}
    ffn_groups = [[[f"gate_t{i}", f"up_t{i}"], [f"down{i}"]] for i in range(4)]
    groups = [*ffn_groups[0], ["in_t", "uq", "ukv_t", "ab_out"], *ffn_groups[1], *ffn_groups[2], ["cin_t", "c_out"],
              *ffn_groups[3]]
    feed = _Feed({nm: a.astype(BF16) for nm, a in local.items()}, groups, me, m_all)

    sq_cols, ds0, g = _local_step(x[0], ctx[0], loss_target[0], mod_h, mod_g, norm_g_full, feed, pool_w[0],
                                  pool_scale, q_norm_g, kv_norm_g, conv_w_full, final_norm_g)
    grad_x = ds0[:n_lat]
    loss = lax.psum(0.5 * jnp.sum(sq_cols) / d, ("x", "y", "c"))
    red = feed.finish(ds0)

    dm = jnp.stack([g["mod_h"], jnp.stack([g["mod_g"], jnp.zeros_like(g["mod_g"])])])
    dm_all = _exchange("gather_dmod", dm.reshape(-1, 128), False).reshape(N_DEV, 2, 2, N_MOD * d)
    grad_b_mod = _sum_rows("dmod_bias", dm_all.reshape(2 * N_DEV, 2 * N_MOD * d)).reshape(2, N_MOD * d)
    dm_sh = lax.dynamic_slice(dm_all, (0, 0, 0, me * mod_cols), (N_DEV, 2, 2, mod_cols))
    gw_mod, cctx_parts = [], []
    for l in range(2):
        dm_l = dm_sh[:, :, l, :].transpose(1, 0, 2).reshape(2 * N_DEV, mod_cols).astype(BF16)
        gw_mod.append(_mm(f"mod_dw{l}", [(sil, dm_l)], "tn", F32, 512, 384))
        dm_ctx = jnp.concatenate([dm_l[N_DEV:], jnp.zeros((N_DEV, mod_cols), BF16)], axis=0)
        cctx_parts.append(_mm(f"mod_dcond{l}", [(dm_ctx, w_mod_b[l])], "nt", F32, 16, 512))
    grad_w_mod = jnp.stack(gw_mod)
    cctx_part = _sum_rows("mod_dcond_sum", jnp.concatenate(cctx_parts, axis=0))

    small_g = jnp.concatenate([g["pool_w"].reshape(-1), g["pool_scale"].reshape(-1), g["q_norm_g"].reshape(-1),
                               g["kv_norm_g"].reshape(-1), g["final_norm_g"].reshape(-1), g["norm_g"].reshape(-1),
                               g["conv_w"].reshape(-1), cctx_part.reshape(-1)])
    sizes = [pool_w.size, pool_scale.size, q_norm_g.size, kv_norm_g.size, d, 6 * d, 3 * d, d]
    sg_n = -(-small_g.shape[0] // 1024) * 1024
    small_g = jnp.pad(small_g, (0, sg_n - small_g.shape[0]))
    sg_all = _exchange("gather_small_grads", small_g.reshape(-1, 128), False).reshape(N_DEV, sg_n)
    scale_vec = jnp.concatenate([jnp.ones((1, sum(sizes[:-1])), F32), dsil[N_DEV:N_DEV + 1],
                                 jnp.ones((1, sg_n - sum(sizes)), F32)], axis=1)
    sg = _sum_rows("small_grads_sum", sg_all, scale_vec)[0]
    cuts, pos = [], 0
    for sz in sizes:
        cuts.append(sg[pos:pos + sz])
        pos += sz
    g_pool_w, g_pool_scale, g_q_norm, g_kv_norm, g_final, g_norm_full, g_conv_full, g_c_ctx = cuts
    grad_norm_g = lax.dynamic_slice(g_norm_full.reshape(2, 3, d), (0, 0, me * ng_sh), (2, 3, ng_sh))
    grad_conv_w = lax.dynamic_slice(g_conv_full.reshape(3, d), (0, me * cw_sh), (3, cw_sh))[None]

    def mine(nm):
        return red[nm]

    ffn_t = ffn_w_gate.shape[:2] + (ffn_w_gate.shape[3], ffn_w_gate.shape[2])
    transposed = {"ffn_w_gate": jnp.stack([mine(f"gate_t{i}") for i in range(4)]).reshape(ffn_t),
                  "ffn_w_up": jnp.stack([mine(f"up_t{i}") for i in range(4)]).reshape(ffn_t),
                  "ab_w_in": mine("in_t")[None]}
    grad_down = jnp.stack([mine(f"down{i}") for i in range(4)]).reshape(ffn_w_down.shape)
    grads = (g_c_ctx, grad_norm_g, grad_w_mod, grad_b_mod, jnp.swapaxes(transposed["ffn_w_gate"], -1, -2),
             jnp.swapaxes(transposed["ffn_w_up"], -1, -2), grad_down, jnp.swapaxes(transposed["ab_w_in"], -1, -2),
             g_pool_w.reshape(pool_w.shape), g_pool_scale.reshape(pool_scale.shape), g_q_norm.reshape(q_norm_g.shape),
             mine("uq")[None], g_kv_norm.reshape(kv_norm_g.shape), mine("ukv_t").T[None], mine("ab_out")[None],
             mine("cin_t").T[None], grad_conv_w, mine("c_out")[None], g_final)
    grads = tuple(gr.reshape(w.shape) for gr, w in zip(grads, weights))
    grads_in = tuple(transposed.get(nm, gr) for nm, gr in zip(WEIGHT_NAMES, grads))
    deltas, new_m, new_v = _adam_all(WEIGHT_NAMES, weights, grads_in, moms, vels, transposed)
    return (loss, grad_x[None], *grads, *deltas, *new_m, *new_v)
```

```python
import functools
import math

import jax
import jax.numpy as jnp
import numpy as np
from jax import lax
from jax.experimental import pallas as pl
from jax.experimental.pallas import tpu as pltpu

F32 = jnp.float32
BF16 = jnp.bfloat16
MESH = pl.DeviceIdType.MESH
SDS = jax.ShapeDtypeStruct

N_DEV = 8
D_MODEL = 1024
N_MOD = 9
D_FF = 2816
POOL_WINDOWS = (2, 4, 8, 16)
POOL_DIM = 512
POOL_GROUP_DIM = 128
HEADS = 8
QK_NOPE = 64
QK_ROPE = 32
QK_HEAD = QK_NOPE + QK_ROPE
V_HEAD = 64
Q_RANK = 768
KV_RANK = 256
GRID_W = 64
ROPE_THETA = 10000.0
RMS_EPS = 1e-6
ATTN_SCALE = 1.0 / math.sqrt(QK_HEAD)
HEAD_PAD = 128
POOL_PAD = 16
PA_POOL, PA_CQ, PA_KV = 0, 768, 1536
PA_KV_W = 384
PA_W = PA_KV + PA_KV_W

ADAM_LR, ADAM_B1, ADAM_B2, ADAM_EPS, ADAM_WD, ADAM_STEP = 0.001, 0.9, 0.999, 1e-08, 0.01, 10

VMEM_LIMIT_BYTES = 56 * 1024 * 1024

NN = ((1,), (0,))
NT = ((1,), (1,))
TN = ((0,), (0,))


def _cparams():
    return pltpu.CompilerParams(vmem_limit_bytes=VMEM_LIMIT_BYTES)


def _dot(a, b, dims):
    return lax.dot_general(a, b, (dims, ((), ())), preferred_element_type=F32)


def _tile(n, cap, mult=8):
    t = (min(cap, n) // mult) * mult
    while t >= mult:
        if n % t == 0:
            return t
        t -= mult
    return n


def _colsum(x):
    return jnp.sum(x, axis=0, keepdims=True)


def _rms(x):
    r = lax.rsqrt(jnp.mean(x * x, axis=-1, keepdims=True) + RMS_EPS)
    return x * r, r


def _rms_bwd(n, r, dn):
    return r * (dn - n * jnp.mean(dn * n, axis=-1, keepdims=True))


def _rowwise(name, fn, t_rows, tm, n_lat, rows, vecs, outs, accs):
    nt = t_rows // tm
    nlt = n_lat // tm
    n_groups = 2 if nlt < nt else 1

    def grp(i):
        return jnp.where(i >= nlt, 1, 0) if n_groups == 2 else 0

    in_specs = [pl.BlockSpec((tm, w), functools.partial(lambda i, cb: (i, cb), cb=cb)) for (_, w, cb) in rows]
    in_specs += [pl.BlockSpec((1,) + v.shape[1:], lambda i: (grp(i), 0, 0)) for v in vecs]
    out_specs = [pl.BlockSpec((tm, w), lambda i: (i, 0)) for (w, _) in outs]
    out_specs += [pl.BlockSpec((1, 1, w), lambda i: (grp(i), 0, 0)) for w in accs]
    out_shape = [SDS((t_rows, w), dt) for (w, dt) in outs] + [SDS((n_groups, 1, w), F32) for w in accs]
    n_r, n_v, n_o = len(rows), len(vecs), len(outs)

    def body(*refs):
        row_vals = [r[...] for r in refs[:n_r]]
        vec_vals = [v[0] for v in refs[n_r:n_r + n_v]]
        out_refs = refs[n_r + n_v:n_r + n_v + n_o]
        acc_refs = refs[n_r + n_v + n_o:]
        out_vals, acc_vals = fn(row_vals, vec_vals)
        for o_ref, o in zip(out_refs, out_vals):
            o_ref[...] = o.astype(o_ref.dtype)
        if acc_refs:
            i = pl.program_id(0)
            first = (i == 0) | (i == nlt) if n_groups == 2 else i == 0

            @pl.when(first)
            def _():
                for a_ref, a in zip(acc_refs, acc_vals):
                    a_ref[0] = a

            @pl.when(jnp.logical_not(first))
            def _():
                for a_ref, a in zip(acc_refs, acc_vals):
                    a_ref[0] += a

    res = pl.pallas_call(
        body, name=name, grid=(nt,), in_specs=in_specs, out_specs=out_specs, out_shape=out_shape,
        compiler_params=_cparams(),
    )(*[r[0] for r in rows], *vecs)
    return res[:n_o], res[n_o:]


RESIDENT_BYTES = 12 * 1024 * 1024


def _mm(name, pairs, mode, out_dtype, tm_cap=256, tn_cap=512, bias=None):
    a0, b0 = pairs[0]
    if mode == "nn":
        m, n, dims = a0.shape[0], b0.shape[1], NN
    elif mode == "nt":
        m, n, dims = a0.shape[0], b0.shape[0], NT
    else:
        m, n, dims = a0.shape[1], b0.shape[1], TN
    b_bytes = sum(b.size * b.dtype.itemsize for _, b in pairs)
    tn = n if b_bytes <= RESIDENT_BYTES else _tile(n, tn_cap, 128)
    tm = _tile(m, tm_cap, 128 if mode == "tn" else 16)

    def a_spec(a):
        if mode == "tn":
            return pl.BlockSpec((a.shape[0], tm), lambda i, j: (0, i))
        return pl.BlockSpec((tm, a.shape[1]), lambda i, j: (i, 0))

    def b_spec(b):
        if mode == "nt":
            return pl.BlockSpec((tn, b.shape[1]), lambda i, j: (j, 0))
        return pl.BlockSpec((b.shape[0], tn), lambda i, j: (0, j))

    in_specs, flat = [], []
    for a, b in pairs:
        in_specs += [a_spec(a), b_spec(b)]
        flat += [a, b]
    if bias is not None:
        in_specs.append(pl.BlockSpec((1, tn), lambda i, j: (0, j)))
        flat.append(bias)
    n_pairs = len(pairs)

    def body(*refs):
        acc = None
        for p in range(n_pairs):
            t = _dot(refs[2 * p][...], refs[2 * p + 1][...], dims)
            acc = t if acc is None else acc + t
        if bias is not None:
            acc = acc + refs[2 * n_pairs][...]
        refs[-1][...] = acc.astype(refs[-1].dtype)

    return pl.pallas_call(
        body, name=name, grid=(m // tm, n // tn), in_specs=in_specs,
        out_specs=pl.BlockSpec((tm, tn), lambda i, j: (i, j)),
        out_shape=SDS((m, n), out_dtype), compiler_params=_cparams(),
    )(*flat)


def _mm_resid(name, a, b, s, mg, k, coef, n_lat):
    t_rows, n = a.shape[0], b.shape[1]
    tm = _tile(math.gcd(n_lat, t_rows), 256, 16)
    nlt = n_lat // tm
    n_groups = 2 if nlt < t_rows // tm else 1

    def grp(i):
        return jnp.where(i >= nlt, 1, 0) if n_groups == 2 else 0

    def body(a_ref, b_ref, s_ref, mg_ref, so_ref, o_ref):
        o = _dot(a_ref[...], b_ref[...], NN)
        gate = mg_ref[0, 3 * k + 2:3 * k + 3, :]
        o_ref[...] = o
        so_ref[...] = s_ref[...] + (coef * gate) * o

    row = pl.BlockSpec((tm, n), lambda i: (i, 0))
    return pl.pallas_call(
        body, name=name, grid=(t_rows // tm,),
        in_specs=[pl.BlockSpec((tm, a.shape[1]), lambda i: (i, 0)), pl.BlockSpec(b.shape, lambda i: (0, 0)), row,
                  pl.BlockSpec((1, mg.shape[1], n), lambda i: (grp(i), 0, 0))],
        out_specs=[row, row], out_shape=[SDS((t_rows, n), F32), SDS((t_rows, n), F32)], compiler_params=_cparams(),
    )(a, b, s, mg)


def _ffn_up(name, u, wg_t, wu_t):
    t_rows, f = u.shape[0], wg_t.shape[0]
    tm = _tile(t_rows, 256, 16)

    def body(u_ref, wg_ref, wu_ref, a_ref, b_ref, h_ref):
        uu = u_ref[...]
        a = _dot(uu, wg_ref[...], NT)
        b = _dot(uu, wu_ref[...], NT)
        sg = jax.nn.sigmoid(a)
        act = a * sg
        a_ref[...] = (b * (sg * (1.0 + a * (1.0 - sg)))).astype(BF16)
        b_ref[...] = act.astype(BF16)
        h_ref[...] = (act * b).astype(BF16)

    w_spec = pl.BlockSpec(wg_t.shape, lambda i: (0, 0))
    o_spec = pl.BlockSpec((tm, f), lambda i: (i, 0))
    return pl.pallas_call(
        body, name=name, grid=(t_rows // tm,),
        in_specs=[pl.BlockSpec((tm, u.shape[1]), lambda i: (i, 0)), w_spec, w_spec],
        out_specs=[o_spec, o_spec, o_spec], out_shape=[SDS((t_rows, f), BF16)] * 3, compiler_params=_cparams(),
    )(u, wg_t, wu_t)


def _ffn_dact(name, do, wd, a, b):
    t_rows, f = do.shape[0], wd.shape[0]
    tm = _tile(t_rows, 256, 16)

    def body(do_ref, wd_ref, a_ref, b_ref, da_ref, db_ref):
        dh = _dot(do_ref[...], wd_ref[...], NT)
        da_ref[...] = (dh * a_ref[...].astype(F32)).astype(BF16)
        db_ref[...] = (dh * b_ref[...].astype(F32)).astype(BF16)

    t_spec = pl.BlockSpec((tm, f), lambda i: (i, 0))
    return pl.pallas_call(
        body, name=name, grid=(t_rows // tm,),
        in_specs=[pl.BlockSpec((tm, do.shape[1]), lambda i: (i, 0)), pl.BlockSpec(wd.shape, lambda i: (0, 0)),
                  t_spec, t_spec],
        out_specs=[t_spec, t_spec], out_shape=[SDS((t_rows, f), BF16)] * 2, compiler_params=_cparams(),
    )(do, wd, a, b)


def _row_tm(t_rows, n_lat):
    return _tile(math.gcd(t_rows, n_lat), 256, 16)


def _adaln_fwd(name, s, mg, k, n_lat):
    t_rows = s.shape[0]

    def fn(rv, vv):
        m = vv[0]
        n, _ = _rms(rv[0])
        u = (n * m[9 + k:10 + k]) * (1.0 + m[3 * k + 1:3 * k + 2]) + m[3 * k:3 * k + 1]
        return [u], []

    (u,), _ = _rowwise(name, fn, t_rows, _row_tm(t_rows, n_lat), n_lat, [(s, D_MODEL, 0)], [mg], [(D_MODEL, BF16)], [])
    return u


def _adaln_bwd(name, s, du, ds_out, mg, k, n_lat):
    t_rows = s.shape[0]

    def fn(rv, vv):
        m = vv[0]
        gain, scale = m[9 + k:10 + k], m[3 * k + 1:3 * k + 2]
        n, r = _rms(rv[0])
        d_u = rv[1]
        dxn = d_u * (1.0 + scale)
        ds = _rms_bwd(n, r, dxn * gain)
        return [rv[2] + ds], [_colsum(d_u), _colsum(d_u * (n * gain)), _colsum(dxn * n)]

    (ds_in,), accs = _rowwise(name, fn, t_rows, _row_tm(t_rows, n_lat), n_lat,
                              [(s, D_MODEL, 0), (du, D_MODEL, 0), (ds_out, D_MODEL, 0)], [mg],
                              [(D_MODEL, F32)], [D_MODEL] * 3)
    return ds_in, accs


def _gate_bwd(name, ds_out, o, mg, k, coef, n_lat):
    t_rows = o.shape[0]

    def fn(rv, vv):
        gate = vv[0][3 * k + 2:3 * k + 3]
        d = coef * rv[0]
        return [d * gate], [_colsum(d * rv[1])]

    (do,), (dgate,) = _rowwise(name, fn, t_rows, _row_tm(t_rows, n_lat), n_lat,
                               [(ds_out, D_MODEL, 0), (o, D_MODEL, 0)], [mg], [(D_MODEL, BF16)], [D_MODEL])
    return do, dgate


def _rmsnorm_fwd(name, x, width, colblk, gain, t_rows):
    def fn(rv, vv):
        n, _ = _rms(rv[0])
        return [n * vv[0]], []

    (y,), _ = _rowwise(name, fn, t_rows, _tile(t_rows, 256, 16), t_rows, [(x, width, colblk)],
                       [gain.reshape(1, 1, width)], [(width, BF16)], [])
    return y


def _rmsnorm_bwd(name, x, width, colblk, dy, gain, t_rows):
    def fn(rv, vv):
        n, r = _rms(rv[0])
        return [_rms_bwd(n, r, rv[1] * vv[0])], [_colsum(rv[1] * n)]

    (dx,), (dgain,) = _rowwise(name, fn, t_rows, _tile(t_rows, 256, 16), t_rows,
                               [(x, width, colblk), (dy, width, 0)], [gain.reshape(1, 1, width)],
                               [(width, F32)], [width])
    return dx, dgain


def _final_loss(name, h, target, gain):
    t_rows = h.shape[0]
    inv_d = 1.0 / D_MODEL

    def fn(rv, vv):
        g = vv[0]
        n, r = _rms(rv[0])
        e = n * g - rv[1]
        dy = e * inv_d
        return [_rms_bwd(n, r, dy * g)], [_colsum(e * e), _colsum(dy * n)]

    (dh,), (sq, dgain) = _rowwise(name, fn, t_rows, _tile(t_rows, 256, 16), t_rows,
                                  [(h, D_MODEL, 0), (target, D_MODEL, 0)], [gain.reshape(1, 1, D_MODEL)],
                                  [(D_MODEL, F32)], [D_MODEL, D_MODEL])
    return dh, sq, dgain


def _rope(name, z, width, colblk, cos, sin, perm, backward, out_dtype):
    t_rows = cos.shape[0]

    def body(z_ref, c_ref, s_ref, p_ref, o_ref):
        zz = z_ref[...]
        pre = zz * s_ref[...] if backward else zz
        hi = pre.astype(BF16)
        lo = (pre - hi.astype(F32)).astype(BF16)
        rot = _dot(hi, p_ref[...], NN) + _dot(lo, p_ref[...], NN)
        if not backward:
            rot = rot * s_ref[...]
        o_ref[...] = (zz * c_ref[...] + rot).astype(o_ref.dtype)

    tm = _tile(t_rows, 256, 16)
    t_spec = pl.BlockSpec((tm, width), lambda i: (i, 0))
    return pl.pallas_call(
        body, name=name, grid=(t_rows // tm,),
        in_specs=[pl.BlockSpec((tm, width), lambda i: (i, colblk)), t_spec, t_spec,
                  pl.BlockSpec((width, width), lambda i: (0, 0))],
        out_specs=t_spec, out_shape=SDS((t_rows, width), out_dtype), compiler_params=_cparams(),
    )(z, cos, sin, perm)


def _window_sum(x, w, transposed):
    n_rows = x.shape[0]
    zeros = jnp.zeros((POOL_PAD, x.shape[1]), F32)
    y = jnp.concatenate([zeros, x, zeros], axis=0)
    total = n_rows + 2 * POOL_PAD
    if transposed:
        y = y + pltpu.roll(y, total - 1, 0)
    else:
        y = y + pltpu.roll(y, 1, 0)
    step = 1
    while 2 * step < w:
        y = pltpu.roll(y, step, 0) + pltpu.roll(y, total - step, 0)
        step *= 2
    return y[POOL_PAD:POOL_PAD + n_rows]


def _window_count(n_rows, w):
    t = lax.broadcasted_iota(jnp.int32, (n_rows, 1), 0)
    lo = jnp.maximum(t - w // 2, 0)
    hi = jnp.minimum(t + (w - w // 2 - 1), n_rows - 1)
    return (hi - lo + 1).astype(F32)


def _pool_fwd(name, proj, n_rows, w_grp, scale):
    def body(x_ref, w_ref, sc_ref, y_ref, p_ref):
        for g, w in enumerate(POOL_WINDOWS):
            cols = slice(g * POOL_GROUP_DIM, (g + 1) * POOL_GROUP_DIM)
            x = x_ref[:, cols]
            p = _window_sum(x, w, False) * (1.0 / _window_count(n_rows, w)) - x
            pb = p.astype(BF16)
            p_ref[:, cols] = pb
            y_ref[:, cols] = (_dot(pb, w_ref[g], NN) * sc_ref[:, cols]).astype(BF16)

    blk = pl.BlockSpec((n_rows, POOL_DIM), lambda i: (0, 0))
    return pl.pallas_call(
        body, name=name, grid=(1,),
        in_specs=[blk, pl.BlockSpec(w_grp.shape, lambda i: (0, 0, 0)), pl.BlockSpec((1, POOL_DIM), lambda i: (0, 0))],
        out_specs=[blk, blk], out_shape=[SDS((n_rows, POOL_DIM), BF16)] * 2, compiler_params=_cparams(),
    )(proj, w_grp, scale)


def _pool_bwd(name, dcat, n_rows, p, w_grp, scale):
    def body(dy_ref, p_ref, w_ref, sc_ref, dx_ref, dw_ref, dsc_ref):
        for g, w in enumerate(POOL_WINDOWS):
            cols = slice(g * POOL_GROUP_DIM, (g + 1) * POOL_GROUP_DIM)
            dy = dy_ref[:, cols]
            pb = p_ref[:, cols]
            pw = _dot(pb, w_ref[g], NN)
            dsc_ref[:, cols] = _colsum(dy * pw)
            dpw = (dy * sc_ref[:, cols]).astype(BF16)
            dw_ref[g] = _dot(pb, dpw, TN)
            dp = _dot(dpw, w_ref[g], NT)
            dx_ref[:, cols] = _window_sum(dp * (1.0 / _window_count(n_rows, w)), w, True) - dp

    blk = pl.BlockSpec((n_rows, POOL_DIM), lambda i: (0, 0))
    w_spec = pl.BlockSpec(w_grp.shape, lambda i: (0, 0, 0))
    v_spec = pl.BlockSpec((1, POOL_DIM), lambda i: (0, 0))
    return pl.pallas_call(
        body, name=name, grid=(1,), in_specs=[blk, blk, w_spec, v_spec], out_specs=[blk, w_spec, v_spec],
        out_shape=[SDS((n_rows, POOL_DIM), F32), SDS(w_grp.shape, F32), SDS((1, POOL_DIM), F32)],
        compiler_params=_cparams(),
    )(dcat, p, w_grp, scale)


def _attn_fwd(name, q, k, v):
    h, n_q, _ = q.shape
    n_k = k.shape[1]
    tq = _tile(n_q, 256, 16)

    def body(q_ref, k_ref, v_ref, o_ref, lse_ref):
        s = _dot(q_ref[...], k_ref[...], NT) * ATTN_SCALE
        m = jnp.max(s, axis=-1, keepdims=True)
        e = jnp.exp(s - m)
        l = jnp.sum(e, axis=-1, keepdims=True)
        p = (e * (1.0 / l)).astype(BF16)
        o_ref[...] = _dot(p, v_ref[...], NN).astype(BF16)
        lse_ref[...] = m + jnp.log(l)

    return pl.pallas_call(
        body, name=name, grid=(h, n_q // tq),
        in_specs=[pl.BlockSpec((None, tq, HEAD_PAD), lambda hh, i: (hh, i, 0)),
                  pl.BlockSpec((None, n_k, HEAD_PAD), lambda hh, i: (hh, 0, 0)),
                  pl.BlockSpec((None, n_k, V_HEAD), lambda hh, i: (hh, 0, 0))],
        out_specs=[pl.BlockSpec((None, tq, V_HEAD), lambda hh, i: (hh, i, 0)),
                   pl.BlockSpec((None, tq, 1), lambda hh, i: (hh, i, 0))],
        out_shape=[SDS((h, n_q, V_HEAD), BF16), SDS((h, n_q, 1), F32)], compiler_params=_cparams(),
    )(q, k, v)


def _attn_bwd(name, q, k, v, o, lse, do):
    h, n_q, _ = q.shape
    n_k = k.shape[1]
    tq = _tile(n_q, 256, 16)

    def body(q_ref, k_ref, v_ref, o_ref, lse_ref, do_ref, dq_ref, dk_ref, dv_ref, dks_ref):
        hh, i = pl.program_id(0), pl.program_id(1)
        qq, kk, dd = q_ref[...], k_ref[...], do_ref[...]
        s = _dot(qq, kk, NT) * ATTN_SCALE
        p = jnp.exp(s - lse_ref[...])
        dp = _dot(dd, v_ref[...], NT)
        delta = jnp.sum(dd.astype(F32) * o_ref[...].astype(F32), axis=-1, keepdims=True)
        ds = (p * (dp - delta) * ATTN_SCALE).astype(BF16)
        dq_ref[...] = _dot(ds, kk, NN)
        dk = _dot(ds, qq, TN)
        dv = _dot(p.astype(BF16), dd, TN)

        @pl.when(i == 0)
        def _():
            dk_ref[...] = dk
            dv_ref[...] = dv

        @pl.when(i > 0)
        def _():
            dk_ref[...] += dk
            dv_ref[...] += dv

        @pl.when((i == 0) & (hh == 0))
        def _():
            dks_ref[...] = dk

        @pl.when((i > 0) | (hh > 0))
        def _():
            dks_ref[...] += dk

    q_spec = pl.BlockSpec((None, tq, HEAD_PAD), lambda hh, i: (hh, i, 0))
    k_spec = pl.BlockSpec((None, n_k, HEAD_PAD), lambda hh, i: (hh, 0, 0))
    v_spec = pl.BlockSpec((None, n_k, V_HEAD), lambda hh, i: (hh, 0, 0))
    o_spec = pl.BlockSpec((None, tq, V_HEAD), lambda hh, i: (hh, i, 0))
    return pl.pallas_call(
        body, name=name, grid=(h, n_q // tq),
        in_specs=[q_spec, k_spec, v_spec, o_spec, pl.BlockSpec((None, tq, 1), lambda hh, i: (hh, i, 0)), o_spec],
        out_specs=[q_spec, k_spec, v_spec, pl.BlockSpec((n_k, HEAD_PAD), lambda hh, i: (0, 0))],
        out_shape=[SDS((h, n_q, HEAD_PAD), F32), SDS((h, n_k, HEAD_PAD), F32), SDS((h, n_k, V_HEAD), F32),
                   SDS((n_k, HEAD_PAD), F32)],
        compiler_params=_cparams(),
    )(q, k, v, o, lse, do)


CONV_COLS = 256


def _shift_rows(x, d):
    n_rows = x.shape[0]
    t = lax.broadcasted_iota(jnp.int32, (n_rows, 1), 0)
    if d > 0:
        return jnp.where(t >= d, pltpu.roll(x, d, 0), 0.0)
    return jnp.where(t < n_rows + d, pltpu.roll(x, n_rows + d, 0), 0.0)


def _conv_fwd(name, z3, conv_w):
    n_rows = z3.shape[0]
    nb = D_MODEL // CONV_COLS

    def body(b_ref, c_ref, v_ref, w_ref, y_ref):
        z = c_ref[...] * v_ref[...]
        zc = w_ref[0:1, :] * _shift_rows(z, 1) + w_ref[1:2, :] * z + w_ref[2:3, :] * _shift_rows(z, -1)
        y_ref[...] = (b_ref[...] * zc).astype(BF16)

    def part(k):
        return pl.BlockSpec((n_rows, CONV_COLS), lambda j: (0, k * nb + j))

    return pl.pallas_call(
        body, name=name, grid=(nb,),
        in_specs=[part(0), part(1), part(2), pl.BlockSpec((3, CONV_COLS), lambda j: (0, j))],
        out_specs=pl.BlockSpec((n_rows, CONV_COLS), lambda j: (0, j)),
        out_shape=SDS((n_rows, D_MODEL), BF16), compiler_params=_cparams(),
    )(z3, z3, z3, conv_w)


def _conv_bwd(name, dy, z3, conv_w):
    n_rows = z3.shape[0]
    nb = D_MODEL // CONV_COLS

    def body(dy_ref, b_ref, c_ref, v_ref, w_ref, db_ref, dc_ref, dv_ref, dw_ref):
        c, v, d_y = c_ref[...], v_ref[...], dy_ref[...]
        z = c * v
        z_dn, z_up = _shift_rows(z, 1), _shift_rows(z, -1)
        zc = w_ref[0:1, :] * z_dn + w_ref[1:2, :] * z + w_ref[2:3, :] * z_up
        db_ref[...] = (d_y * zc).astype(BF16)
        dzc = d_y * b_ref[...]
        dz = w_ref[0:1, :] * _shift_rows(dzc, -1) + w_ref[1:2, :] * dzc + w_ref[2:3, :] * _shift_rows(dzc, 1)
        dc_ref[...] = (dz * v).astype(BF16)
        dv_ref[...] = (dz * c).astype(BF16)
        dw_ref[0:1, :] = _colsum(dzc * z_dn)
        dw_ref[1:2, :] = _colsum(dzc * z)
        dw_ref[2:3, :] = _colsum(dzc * z_up)

    def part(k):
        return pl.BlockSpec((n_rows, CONV_COLS), lambda j: (0, k * nb + j))

    col = pl.BlockSpec((n_rows, CONV_COLS), lambda j: (0, j))
    w_spec = pl.BlockSpec((3, CONV_COLS), lambda j: (0, j))
    return pl.pallas_call(
        body, name=name, grid=(nb,), in_specs=[col, part(0), part(1), part(2), w_spec],
        out_specs=[col, col, col, w_spec],
        out_shape=[SDS((n_rows, D_MODEL), BF16)] * 3 + [SDS((3, D_MODEL), F32)], compiler_params=_cparams(),
    )(dy, z3, z3, z3, conv_w)


def _silu_rows(name, x):
    def body(x_ref, s_ref, d_ref):
        xx = x_ref[...]
        sg = jax.nn.sigmoid(xx)
        s_ref[...] = (xx * sg).astype(BF16)
        d_ref[...] = sg * (1.0 + xx * (1.0 - sg))

    return pl.pallas_call(body, name=name, out_shape=[SDS(x.shape, BF16), SDS(x.shape, F32)])(x)


def _sum_rows(name, x, scale=None):
    r, n = x.shape
    tn = _tile(n, 8192, 128)

    def body(*refs):
        acc = jnp.sum(refs[0][...].astype(F32), axis=0, keepdims=True)
        if scale is not None:
            acc = acc * refs[1][...]
        refs[-1][...] = acc

    in_specs = [pl.BlockSpec((r, tn), lambda j: (0, j))]
    args = [x]
    if scale is not None:
        in_specs.append(pl.BlockSpec((1, tn), lambda j: (0, j)))
        args.append(scale)
    return pl.pallas_call(body, name=name, grid=(n // tn,), in_specs=in_specs,
                          out_specs=pl.BlockSpec((1, tn), lambda j: (0, j)), out_shape=SDS((1, n), F32))(*args)


def _sum_slots(name, x):
    n_slots, r, c = x.shape
    tr = _tile(r, 432, 16)

    def body(x_ref, o_ref):
        acc = x_ref[0].astype(F32)
        for sl in range(1, n_slots):
            acc = acc + x_ref[sl].astype(F32)
        o_ref[...] = acc

    return pl.pallas_call(body, name=name, grid=(r // tr,),
                          in_specs=[pl.BlockSpec((n_slots, tr, c), lambda i: (0, i, 0))],
                          out_specs=pl.BlockSpec((tr, c), lambda i: (i, 0)), out_shape=SDS((r, c), F32),
                          compiler_params=_cparams())(x)


def _adamw(name, w, g, m, v):
    shape = w.shape
    cols = shape[-1]
    rows = w.size // cols
    tr = _tile(rows, 512, 8)
    bc1 = 1.0 - ADAM_B1 ** ADAM_STEP
    bc2 = 1.0 - ADAM_B2 ** ADAM_STEP

    def body(w_ref, g_ref, m_ref, v_ref, d_ref, nm_ref, nv_ref):
        gg = g_ref[...]
        nm = ADAM_B1 * m_ref[...] + (1.0 - ADAM_B1) * gg
        nv = ADAM_B2 * v_ref[...] + (1.0 - ADAM_B2) * (gg * gg)
        nm_ref[...] = nm
        nv_ref[...] = nv
        d_ref[...] = -ADAM_LR * ((nm / bc1) / (jnp.sqrt(nv / bc2) + ADAM_EPS) + ADAM_WD * w_ref[...])

    spec = pl.BlockSpec((tr, cols), lambda i: (i, 0))
    outs = pl.pallas_call(body, name=name, grid=(rows // tr,), in_specs=[spec] * 4, out_specs=[spec] * 3,
                          out_shape=[SDS((rows, cols), F32)] * 3, compiler_params=_cparams())(
        w.reshape(rows, cols), g.reshape(rows, cols), m.reshape(rows, cols), v.reshape(rows, cols))
    return tuple(t.reshape(shape) for t in outs)


def _exchange(name, x, scatter, after=None):
    blk = x.shape[1:] if scatter else x.shape
    extra = [] if after is None else [after]

    def body(x_ref, *rest):
        out_ref, send_sems, recv_sems, local_sem = rest[len(extra):]
        mx, my, mc = lax.axis_index("x"), lax.axis_index("y"), lax.axis_index("c")
        me = 4 * mx + 2 * my + mc
        own = pltpu.make_async_copy(x_ref.at[me] if scatter else x_ref, out_ref.at[me], local_sem)
        own.start()
        copies = []
        for kk in range(1, N_DEV):
            px = jnp.bitwise_xor(mx, (kk >> 2) & 1)
            py = jnp.bitwise_xor(my, (kk >> 1) & 1)
            pc = jnp.bitwise_xor(mc, kk & 1)
            peer = 4 * px + 2 * py + pc
            send = pltpu.make_async_remote_copy(
                src_ref=x_ref.at[peer] if scatter else x_ref, dst_ref=out_ref.at[me],
                send_sem=send_sems.at[kk - 1], recv_sem=recv_sems.at[kk - 1],
                device_id=(px, py, pc), device_id_type=MESH)
            send.start()
            arrival = pltpu.make_async_remote_copy(
                src_ref=x_ref.at[peer] if scatter else x_ref, dst_ref=out_ref.at[peer],
                send_sem=send_sems.at[kk - 1], recv_sem=recv_sems.at[kk - 1],
                device_id=(px, py, pc), device_id_type=MESH)
            copies.append((send, arrival))
        for send, arrival in copies:
            arrival.wait_recv()
            send.wait_send()
        own.wait()

    return pl.pallas_call(
        body, name=name, out_shape=SDS((N_DEV,) + tuple(blk), x.dtype),
        in_specs=[pl.BlockSpec(memory_space=pl.ANY)] * (1 + len(extra)), out_specs=pl.BlockSpec(memory_space=pl.ANY),
        scratch_shapes=[pltpu.SemaphoreType.DMA((N_DEV - 1,)), pltpu.SemaphoreType.DMA((N_DEV - 1,)),
                        pltpu.SemaphoreType.DMA],
    )(x, *extra)


def _rope_perm(pre, reps, post):
    half = QK_ROPE // 4
    width = reps * (pre + QK_ROPE) + post
    p = np.zeros((width, width), np.float32)
    for rep in range(reps):
        s0 = rep * (pre + QK_ROPE) + pre
        for base in (s0, s0 + 2 * half):
            for i in range(half):
                p[base + half + i, base + i] = -1.0
                p[base + i, base + half + i] = 1.0
    return p


def _rope_tables(n_lat, t_rows, pre, reps, post):
    half = QK_ROPE // 4
    pos = jnp.arange(n_lat)
    freqs = jnp.power(ROPE_THETA, -jnp.arange(0, 2 * half, 2, dtype=F32) / (2 * half))
    ang_r = (pos // GRID_W).astype(F32)[:, None] * freqs
    ang_c = (pos % GRID_W).astype(F32)[:, None] * freqs
    ang = jnp.concatenate([ang_r, ang_r, ang_c, ang_c], axis=-1)

    def table(fn, plain):
        slot = jnp.concatenate([jnp.full((n_lat, pre), plain, F32), fn(ang)], axis=-1)
        t = jnp.concatenate([jnp.tile(slot, (1, reps)), jnp.full((n_lat, post), plain, F32)], axis=-1)
        return jnp.concatenate([t, jnp.full((t_rows - n_lat, t.shape[1]), plain, F32)], axis=0)

    return table(jnp.cos, 1.0), table(jnp.sin, 0.0)


def _ffn_half_fwd(tag, s, mg, k, feed, i, coef, n_lat):
    u = _adaln_fwd(f"{tag}_adaln", s, mg, k, n_lat)
    wg_t, wu_t = feed.weights(f"{tag}_up", [f"gate_t{i}", f"up_t{i}"], u)
    a, b, hid = _ffn_up(f"{tag}_up", u, wg_t, wu_t)
    (wd,) = feed.weights(f"{tag}_down", [f"down{i}"], hid)
    s_out, o = _mm_resid(f"{tag}_down", hid, wd, s, mg, k, coef, n_lat)
    return s_out, (s, u, a, b, hid, o, wg_t, wu_t, wd)


def _ffn_half_bwd(tag, ds_out, saved, mg, k, feed, i, coef, n_lat):
    s, u, a, b, hid, o, wg_t, wu_t, wd = saved
    do, dgate = _gate_bwd(f"{tag}_dgate", ds_out, o, mg, k, coef, n_lat)
    da, db = _ffn_dact(f"{tag}_dact", do, wd, a, b)
    dwd = _mm(f"{tag}_dwd", [(hid, do)], "tn", BF16)
    dwg_t = _mm(f"{tag}_dwg", [(da, u)], "tn", BF16)
    dwu_t = _mm(f"{tag}_dwu", [(db, u)], "tn", BF16)
    token = feed.grads(tag, {f"down{i}": dwd, f"gate_t{i}": dwg_t, f"up_t{i}": dwu_t})
    du = _mm(f"{tag}_du", [(da, wg_t), (db, wu_t)], "nn", F32, 384, 512, bias=_after(token))
    ds_in, (dshift, dscale, dgain) = _adaln_bwd(f"{tag}_dadaln", s, du, ds_out, mg, k, n_lat)
    return ds_in, dict(shift=dshift, scale=dscale, gate=dgate, gain=dgain)


def _after(token):
    return jnp.zeros((1, D_MODEL), F32) + token


def _mod_grad(parts, n_groups):
    rows = []
    zero = jnp.zeros((n_groups, 1, D_MODEL), F32)
    for k in range(3):
        for nm in ("shift", "scale", "gate"):
            t = parts[k].get(nm, zero)
            if t.shape[0] < n_groups:
                t = jnp.concatenate([t, jnp.zeros((n_groups - t.shape[0], 1, D_MODEL), F32)], axis=0)
            rows.append(t)
    return jnp.concatenate(rows, axis=1).reshape(n_groups, N_MOD * D_MODEL)


def _local_step(x, ctx, target, mod_h, mod_g, norm_g, feed, pool_w, pool_scale, q_norm_g, kv_norm_g, conv_w,
                final_norm_g):
    n_lat, n_ctx = x.shape[0], ctx.shape[0]
    t_all = n_lat + n_ctx
    mg0 = jnp.stack([jnp.concatenate([mod_h[0], norm_g[0]], axis=0), jnp.concatenate([mod_g, norm_g[0]], axis=0)])
    mg1 = jnp.concatenate([mod_h[1], norm_g[1]], axis=0)[None]

    s0 = jnp.concatenate([x, ctx], axis=0) + feed.start_token()
    s1, sv_f00 = _ffn_half_fwd("l0f0", s0, mg0, 0, feed, 0, 0.5, n_lat)

    ua = _adaln_fwd("l0m_adaln", s1, mg0, 1, n_lat)
    w_in, w_uq, w_ukv_t, w_ab_out = feed.weights("l0m", ["in_t", "uq", "ukv_t", "ab_out"], ua)
    kv_rows = KV_RANK + QK_ROPE
    w_in_t = jnp.concatenate([
        w_in[:POOL_DIM], jnp.zeros((PA_CQ - POOL_DIM, D_MODEL), BF16), w_in[POOL_DIM:POOL_DIM + Q_RANK],
        w_in[POOL_DIM + Q_RANK:], jnp.zeros((PA_KV_W - kv_rows, D_MODEL), BF16)], axis=0)
    proj = _mm("l0m_proj", [(ua, w_in_t)], "nt", F32, 768, 384)
    pool_y, pool_p = _pool_fwd("l0m_pool", proj, n_lat, pool_w.astype(BF16), pool_scale)
    nq = _rmsnorm_fwd("l0m_qnorm", proj, Q_RANK, PA_CQ // Q_RANK, q_norm_g, n_lat)
    q_lin = _mm("l0m_q", [(nq, w_uq)], "nn", F32, 512, 768)
    cos_q, sin_q = _rope_tables(n_lat, n_lat, QK_NOPE, HEADS, 0)
    perm_q = _rope_perm(QK_NOPE, HEADS, 0)
    q_rot = _rope("l0m_qrope", q_lin, Q_RANK, 0, cos_q, sin_q, jnp.asarray(perm_q, BF16), False, BF16)
    cos_k, sin_k = _rope_tables(n_lat, t_all, KV_RANK, 1, PA_KV_W - kv_rows)
    perm_k = _rope_perm(KV_RANK, 1, PA_KV_W - kv_rows)
    kvr = _rope("l0m_krope", proj, PA_KV_W, PA_KV // PA_KV_W, cos_k, sin_k, jnp.asarray(perm_k, BF16), False, F32)
    nkv = _rmsnorm_fwd("l0m_kvnorm", kvr, KV_RANK, 0, kv_norm_g, t_all)
    kv = _mm("l0m_kv", [(nkv, w_ukv_t)], "nt", BF16, 768, 512)
    qh = jnp.pad(q_rot.reshape(n_lat, HEADS, QK_HEAD), ((0, 0), (0, 0), (0, HEAD_PAD - QK_HEAD))).transpose(1, 0, 2)
    kvh = kv.reshape(t_all, HEADS, QK_NOPE + V_HEAD)
    k_rope = jnp.broadcast_to(kvr[:, None, KV_RANK:KV_RANK + QK_ROPE].astype(BF16), (t_all, HEADS, QK_ROPE))
    kh = jnp.concatenate([kvh[:, :, :QK_NOPE], k_rope, jnp.zeros((t_all, HEADS, HEAD_PAD - QK_HEAD), BF16)],
                         axis=-1).transpose(1, 0, 2)
    vh = kvh[:, :, QK_NOPE:].transpose(1, 0, 2)
    oh, lse = _attn_fwd("l0m_attn", qh, kh, vh)
    cat = jnp.concatenate([pool_y, oh.transpose(1, 0, 2).reshape(n_lat, HEADS * V_HEAD)], axis=-1)
    h1 = s1[:n_lat]
    h2, mix_o = _mm_resid("l0m_out", cat, w_ab_out, h1, mg0[:1], 1, 1.0, n_lat)

    h3, sv_f01 = _ffn_half_fwd("l0f1", h2, mg0[:1], 2, feed, 1, 0.5, n_lat)

    h4, sv_f10 = _ffn_half_fwd("l1f0", h3, mg1, 0, feed, 2, 0.5, n_lat)
    uc = _adaln_fwd("l1m_adaln", h4, mg1, 1, n_lat)
    w_cin_t, w_c_out = feed.weights("l1m", ["cin_t", "c_out"], uc)
    z3 = _mm("l1m_in", [(uc, w_cin_t)], "nt", F32)
    yc = _conv_fwd("l1m_conv", z3, conv_w)
    h5, conv_o = _mm_resid("l1m_out", yc, w_c_out, h4, mg1, 1, 1.0, n_lat)
    h6, sv_f11 = _ffn_half_fwd("l1f1", h5, mg1, 2, feed, 3, 0.5, n_lat)

    dh6, sq_cols, d_final_g = _final_loss("loss_head", h6, target, final_norm_g)
    g = {}
    dh5, g["f11"] = _ffn_half_bwd("l1f1", dh6, sv_f11, mg1, 2, feed, 3, 0.5, n_lat)

    do_c, dgate_c = _gate_bwd("l1m_dgate", dh5, conv_o, mg1, 1, 1.0, n_lat)
    dyc = _mm("l1m_dy", [(do_c, w_c_out)], "nt", F32)
    d_c_out = _mm("l1m_dwout", [(yc, do_c)], "tn", BF16)
    db_, dc_, dv_, d_conv_w = _conv_bwd("l1m_dconv", dyc, z3, conv_w)
    dz3 = jnp.concatenate([db_, dc_, dv_], axis=-1)
    d_cin_t = _mm("l1m_dwin", [(dz3, uc)], "tn", BF16)
    token = feed.grads("l1m", {"c_out": d_c_out, "cin_t": d_cin_t})
    duc = _mm("l1m_du", [(dz3, w_cin_t)], "nn", F32, bias=_after(token))
    dh4, (dsh_c, dsc_c, dgn_c) = _adaln_bwd("l1m_dadaln", h4, duc, dh5, mg1, 1, n_lat)
    dh3, g["f10"] = _ffn_half_bwd("l1f0", dh4, sv_f10, mg1, 0, feed, 2, 0.5, n_lat)

    dh2, g["f01"] = _ffn_half_bwd("l0f1", dh3, sv_f01, mg0[:1], 2, feed, 1, 0.5, n_lat)

    do_a, dgate_a = _gate_bwd("l0m_dgate", dh2, mix_o, mg0[:1], 1, 1.0, n_lat)
    dcat = _mm("l0m_dcat", [(do_a, w_ab_out)], "nt", F32)
    d_ab_out = _mm("l0m_dwout", [(cat, do_a)], "tn", BF16)
    d_pool_x, d_pool_w, d_pool_scale = _pool_bwd("l0m_dpool", dcat, n_lat, pool_p, pool_w.astype(BF16), pool_scale)
    doh = dcat[:, POOL_DIM:].reshape(n_lat, HEADS, V_HEAD).transpose(1, 0, 2).astype(BF16)
    dqh, dkh, dvh, dk_sum = _attn_bwd("l0m_dattn", qh, kh, vh, oh, lse, doh)
    dq_rot = dqh[:, :, :QK_HEAD].transpose(1, 0, 2).reshape(n_lat, Q_RANK)
    dq_lin = _rope("l0m_dqrope", dq_rot, Q_RANK, 0, cos_q, sin_q, jnp.asarray(perm_q.T, BF16), True, BF16)
    d_uq = _mm("l0m_dwuq", [(nq, dq_lin)], "tn", BF16, 768, 768)
    dnq = _mm("l0m_dnq", [(dq_lin, w_uq)], "nt", F32, 512, 768)
    dcq, d_q_norm_g = _rmsnorm_bwd("l0m_dqnorm", proj, Q_RANK, PA_CQ // Q_RANK, dnq, q_norm_g, n_lat)
    dkv = jnp.concatenate([dkh[:, :, :QK_NOPE], dvh], axis=-1).transpose(1, 0, 2).reshape(t_all, HEADS * HEAD_PAD)
    dkv = dkv.astype(BF16)
    dnkv = _mm("l0m_dnkv", [(dkv, w_ukv_t)], "nn", F32, 768, 256)
    d_ukv_t = _mm("l0m_dwukv", [(dkv, nkv)], "tn", BF16, 512, 256)
    dckv, d_kv_norm_g = _rmsnorm_bwd("l0m_dkvnorm", kvr, KV_RANK, 0, dnkv, kv_norm_g, t_all)
    dkvr = jnp.concatenate([dckv, dk_sum[:, QK_NOPE:QK_HEAD],
                            jnp.zeros((t_all, PA_KV_W - KV_RANK - QK_ROPE), F32)], axis=-1)
    dpb = _rope("l0m_dkrope", dkvr, PA_KV_W, 0, cos_k, sin_k, jnp.asarray(perm_k.T, BF16), True, F32)
    dproj_lat = jnp.concatenate([d_pool_x, jnp.zeros((n_lat, PA_CQ - POOL_DIM), F32), dcq, dpb[:n_lat]], axis=-1)
    dproj_ctx = jnp.concatenate([jnp.zeros((n_ctx, PA_KV), F32), dpb[n_lat:]], axis=-1)
    dproj = jnp.concatenate([dproj_lat, dproj_ctx], axis=0).astype(BF16)
    d_in_pad = _mm("l0m_dwin", [(dproj, ua)], "tn", BF16, 640, 512)
    d_in_t = jnp.concatenate([d_in_pad[:POOL_DIM], d_in_pad[PA_CQ:PA_CQ + Q_RANK],
                              d_in_pad[PA_KV:PA_KV + kv_rows]], axis=0)
    token = feed.grads("l0m", {"ab_out": d_ab_out, "uq": d_uq, "ukv_t": d_ukv_t, "in_t": d_in_t})
    dua = _mm("l0m_du", [(dproj, w_in_t)], "nn", F32, 768, 512, bias=_after(token))
    dh2_all = jnp.concatenate([dh2, jnp.zeros((n_ctx, D_MODEL), F32)], axis=0)
    ds1, (dsh_a, dsc_a, dgn_a) = _adaln_bwd("l0m_dadaln", s1, dua, dh2_all, mg0, 1, n_lat)
    ds0, g["f00"] = _ffn_half_bwd("l0f0", ds1, sv_f00, mg0, 0, feed, 0, 0.5, n_lat)

    dmod0 = _mod_grad([g["f00"], dict(shift=dsh_a, scale=dsc_a, gate=dgate_a), g["f01"]], 2)
    dmod1 = _mod_grad([g["f10"], dict(shift=dsh_c, scale=dsc_c, gate=dgate_c), g["f11"]], 1)
    d_norm_g = jnp.stack([
        jnp.concatenate([jnp.sum(g["f00"]["gain"], axis=0), jnp.sum(dgn_a, axis=0), g["f01"]["gain"][0]], axis=0),
        jnp.concatenate([g["f10"]["gain"][0], dgn_c[0], g["f11"]["gain"][0]], axis=0)])
    grads = dict(
        pool_w=d_pool_w, pool_scale=d_pool_scale, q_norm_g=d_q_norm_g[0], kv_norm_g=d_kv_norm_g[0],
        conv_w=d_conv_w, final_norm_g=d_final_g[0], norm_g=d_norm_g,
        mod_h=jnp.stack([dmod0[0], dmod1[0]]), mod_g=dmod0[1])
    return sq_cols, ds0, grads


HBM_SPEC = pl.BlockSpec(memory_space=pltpu.HBM)
SEM_SPEC = pl.BlockSpec(memory_space=pltpu.SEMAPHORE)
ANY_SPEC = pl.BlockSpec(memory_space=pl.ANY)
SIDE_EFFECT = pltpu.SideEffectType.DATAFLOW_SIDE_EFFECTING
N_PEERS = N_DEV - 1


def _mesh_place():
    mx, my, mc = lax.axis_index("x"), lax.axis_index("y"), lax.axis_index("c")
    return mx, my, mc, 4 * mx + 2 * my + mc


def _peer(place, kk):
    mx, my, mc, _ = place
    px = jnp.bitwise_xor(mx, (kk >> 2) & 1)
    py = jnp.bitwise_xor(my, (kk >> 1) & 1)
    pc = jnp.bitwise_xor(mc, kk & 1)
    return (px, py, pc), 4 * px + 2 * py + pc


def _hbm(a):
    return pltpu.with_memory_space_constraint(a, pltpu.HBM)


def _landing(block, me):
    zone = lax.empty((N_DEV,) + block.shape, block.dtype)
    return lax.dynamic_update_slice(zone, block[None], (me,) + (0,) * block.ndim)


ALL_PEERS = tuple(range(1, N_DEV))
SIBLING = 1
CHIP_PEERS = (2, 4, 6)
RELAYED = (3, 5, 7)


def _exchange_start(name, srcs, lands, scatter, after, peers=ALL_PEERS):
    n = len(srcs)

    def body(*refs):
        src, land = refs[:n], refs[n:2 * n]
        send_sems, recv_sems, token = refs[2 * n + 1], refs[2 * n + 2], refs[-1]
        place = _mesh_place()
        for a in range(n):
            for kk in peers:
                dev, peer = _peer(place, kk)
                pltpu.make_async_remote_copy(
                    src_ref=src[a].at[peer] if scatter else src[a], dst_ref=land[a].at[place[3]],
                    send_sem=send_sems.at[a * N_PEERS + kk - 1], recv_sem=recv_sems.at[a * N_PEERS + kk - 1],
                    device_id=dev, device_id_type=MESH).start()
        token[...] = jnp.zeros_like(token)

    thru = [pltpu.HBM(t.shape, t.dtype) for t in (*srcs, *lands)]
    res = pl.pallas_call(
        body, name=name,
        out_shape=(pltpu.SemaphoreType.DMA((n * N_PEERS,)), pltpu.SemaphoreType.DMA((n * N_PEERS,)), *thru,
                   SDS((8, 128), F32)),
        in_specs=[HBM_SPEC] * (2 * n) + [ANY_SPEC],
        out_specs=(SEM_SPEC, SEM_SPEC, *([HBM_SPEC] * (2 * n)), pl.BlockSpec(memory_space=pltpu.VMEM)),
        input_output_aliases={i: 2 + i for i in range(2 * n)},
        compiler_params=pltpu.CompilerParams(has_side_effects=SIDE_EFFECT),
    )(*[_hbm(s) for s in srcs], *[_hbm(t) for t in lands], after)
    return res[0], res[1], list(res[2:2 + n]), list(res[2 + n:2 + 2 * n]), res[-1]


def _exchange_wait(name, send_sems, recv_sems, srcs, lands, places, scatter, after):
    n = len(srcs)

    def body(*refs):
        src, land = refs[:n], refs[n:2 * n]
        send, recv = refs[2 * n], refs[2 * n + 1]
        place = _mesh_place()
        for a in range(n):
            for kk in range(1, N_DEV):
                dev, peer = _peer(place, kk)
                cp = pltpu.make_async_remote_copy(
                    src_ref=src[a].at[peer] if scatter else src[a], dst_ref=land[a].at[peer],
                    send_sem=send.at[places[a] * N_PEERS + kk - 1], recv_sem=recv.at[places[a] * N_PEERS + kk - 1],
                    device_id=dev, device_id_type=MESH)
                cp.wait_send()
                cp.wait_recv()

    thru = [pltpu.HBM(t.shape, t.dtype) for t in (*srcs, *lands)]
    res = pl.pallas_call(
        body, name=name, out_shape=tuple(thru),
        in_specs=[HBM_SPEC] * (2 * n) + [SEM_SPEC, SEM_SPEC, ANY_SPEC], out_specs=tuple([HBM_SPEC] * (2 * n)),
        input_output_aliases={i: i for i in range(2 * n)},
        compiler_params=pltpu.CompilerParams(has_side_effects=SIDE_EFFECT),
    )(*srcs, *lands, send_sems, recv_sems, after)
    return list(res[n:])


def _gather_relay(name, send1, recv1, lands, places, after):
    n = len(lands)

    def body(*refs):
        land, s1, r1 = refs[:n], refs[n], refs[n + 1]
        s2, r2 = refs[n + 3], refs[n + 4]
        place = _mesh_place()
        sibling = _peer(place, SIBLING)[0]
        for a in range(n):
            for j, kk in enumerate(CHIP_PEERS):
                dev, origin = _peer(place, kk)
                block = land[a].at[origin]
                pltpu.make_async_remote_copy(
                    src_ref=block, dst_ref=block, send_sem=s1.at[places[a] * N_PEERS + kk - 1],
                    recv_sem=r1.at[places[a] * N_PEERS + kk - 1], device_id=dev, device_id_type=MESH).wait_recv()
                pltpu.make_async_remote_copy(
                    src_ref=block, dst_ref=block, send_sem=s2.at[a * 3 + j], recv_sem=r2.at[a * 3 + j],
                    device_id=sibling, device_id_type=MESH).start()

    res = pl.pallas_call(
        body, name=name,
        out_shape=(pltpu.SemaphoreType.DMA((3 * n,)), pltpu.SemaphoreType.DMA((3 * n,)),
                   *[pltpu.HBM(t.shape, t.dtype) for t in lands]),
        in_specs=[HBM_SPEC] * n + [SEM_SPEC, SEM_SPEC, ANY_SPEC],
        out_specs=(SEM_SPEC, SEM_SPEC, *([HBM_SPEC] * n)),
        input_output_aliases={i: 2 + i for i in range(n)},
        compiler_params=pltpu.CompilerParams(has_side_effects=SIDE_EFFECT),
    )(*lands, send1, recv1, after)
    return res[0], res[1], list(res[2:])


def _gather_wait(name, send1, recv1, send2, recv2, srcs, lands, places, after):
    n = len(lands)

    def body(*refs):
        src, land = refs[:n], refs[n:2 * n]
        s1, r1, s2, r2 = refs[2 * n:2 * n + 4]
        place = _mesh_place()
        for a in range(n):
            for kk in (SIBLING,) + CHIP_PEERS:
                dev, origin = _peer(place, kk)
                first = pltpu.make_async_remote_copy(
                    src_ref=src[a], dst_ref=land[a].at[origin], send_sem=s1.at[places[a] * N_PEERS + kk - 1],
                    recv_sem=r1.at[places[a] * N_PEERS + kk - 1], device_id=dev, device_id_type=MESH)
                first.wait_send()
                if kk == SIBLING:
                    first.wait_recv()
            for j, kk in enumerate(CHIP_PEERS):
                dev, origin = _peer(place, kk + 1)
                relay = pltpu.make_async_remote_copy(
                    src_ref=src[a], dst_ref=land[a].at[origin], send_sem=s2.at[a * 3 + j], recv_sem=r2.at[a * 3 + j],
                    device_id=dev, device_id_type=MESH)
                relay.wait_send()
                relay.wait_recv()

    arrays = (*srcs, *lands)
    res = pl.pallas_call(
        body, name=name, out_shape=tuple(pltpu.HBM(t.shape, t.dtype) for t in arrays),
        in_specs=[HBM_SPEC] * (2 * n) + [SEM_SPEC] * 4 + [ANY_SPEC], out_specs=tuple([HBM_SPEC] * (2 * n)),
        input_output_aliases={i: i for i in range(2 * n)},
        compiler_params=pltpu.CompilerParams(has_side_effects=SIDE_EFFECT),
    )(*arrays, send1, recv1, send2, recv2, after)
    return list(res[n:])


class _Feed:
    def __init__(self, shards, groups, me, after):
        self.names, self.me, self.groups, self.pos = list(shards), me, groups, 0
        srcs = [shards[nm] for nm in self.names]
        lands = [_landing(s, me) for s in srcs]
        self.send, self.recv, self.srcs, self.lands, self.token = _exchange_start(
            "gather_start", srcs, lands, False, after, (SIBLING,) + CHIP_PEERS)
        self.relay = self._relay("gather_relay_first", groups[0], self.token)
        self.pending = []

    def _relay(self, name, names, after):
        places = [self.names.index(nm) for nm in names]
        send2, recv2, lands = _gather_relay(name, self.send, self.recv, [self.lands[i] for i in places], places, after)
        for i, t in zip(places, lands):
            self.lands[i] = t
        return send2, recv2

    def start_token(self):
        return self.token[0, 0]

    def weights(self, tag, names, after):
        assert names == self.groups[self.pos], (names, self.groups[self.pos])
        send2, recv2 = self.relay
        if self.pos + 1 < len(self.groups):
            nxt = self.groups[self.pos + 1]
            self.relay = self._relay(f"gather_relay_{tag}", nxt, after)
            after = self.lands[self.names.index(nxt[0])]
        places = [self.names.index(nm) for nm in names]
        got = _gather_wait(f"gather_wait_{tag}", self.send, self.recv, send2, recv2, [self.srcs[i] for i in places],
                           [self.lands[i] for i in places], places, after)
        self.pos += 1
        return [t.reshape((N_DEV * t.shape[1],) + t.shape[2:]) for t in got]

    def grads(self, tag, full):
        names = list(full)
        srcs = [full[nm].reshape((N_DEV, full[nm].shape[0] // N_DEV) + full[nm].shape[1:]) for nm in names]
        lands = [_landing(lax.dynamic_index_in_dim(s, self.me, 0, keepdims=False), self.me) for s in srcs]
        send, recv, srcs, lands, token = _exchange_start(f"scatter_start_{tag}", srcs, lands, True, srcs[0])
        self.pending.append((tag, names, send, recv, srcs, lands))
        return token[0, 0]

    def collect(self, tags, after):
        out = {}
        for tag, names, send, recv, srcs, lands in self.pending:
            if tag not in tags:
                continue
            got = _exchange_wait(f"scatter_wait_{tag}", send, recv, srcs, lands, list(range(len(names))), True, after)
            for nm, slots in zip(names, got):
                out[nm] = _sum_slots(f"reduce_{nm}", slots)
        return out


def _adamw_math(w, gg, m, v):
    nm = ADAM_B1 * m + (1.0 - ADAM_B1) * gg
    nv = ADAM_B2 * v + (1.0 - ADAM_B2) * (gg * gg)
    bc1 = 1.0 - ADAM_B1 ** ADAM_STEP
    bc2 = 1.0 - ADAM_B2 ** ADAM_STEP
    return -ADAM_LR * ((nm / bc1) / (jnp.sqrt(nv / bc2) + ADAM_EPS) + ADAM_WD * w), nm, nv


def _adamw_part(name, i, w, g, m, v, prev):
    n_parts, rows, cols = w.shape
    tr = _tile(rows, 512, 8)
    if prev is None:
        prev = tuple(lax.empty(w.shape, F32) for _ in range(4))

    def body(w_ref, g_ref, m_ref, v_ref, *rest):
        go_ref, d_ref, nm_ref, nv_ref = rest[4:]
        gg = g_ref[...]
        d, nm, nv = _adamw_math(w_ref[...], gg, m_ref[...], v_ref[...])
        go_ref[...] = gg
        d_ref[...] = d
        nm_ref[...] = nm
        nv_ref[...] = nv

    part = pl.BlockSpec((None, tr, cols), lambda r: (i, r, 0))
    return pl.pallas_call(
        body, name=name, grid=(rows // tr,),
        in_specs=[part, pl.BlockSpec((tr, cols), lambda r: (r, 0)), part, part] + [ANY_SPEC] * 4,
        out_specs=[part] * 4, out_shape=[SDS(w.shape, F32)] * 4,
        input_output_aliases={4 + k: k for k in range(4)}, compiler_params=_cparams(),
    )(w, g, m, v, *prev)


WEIGHT_NAMES = ("c_ctx", "norm_g", "w_mod", "b_mod", "ffn_w_gate", "ffn_w_up", "ffn_w_down", "ab_w_in", "pool_w",
                "pool_scale", "q_norm_g", "w_uq", "kv_norm_g", "w_ukv", "ab_w_out", "conv_w_in", "conv_w",
                "conv_w_out", "final_norm_g")


def kernel(x, c, ctx, c_ctx, norm_g, w_mod, b_mod, ffn_w_gate, ffn_w_up, ffn_w_down, ab_w_in, pool_w, pool_scale, q_norm_g, w_uq, kv_norm_g, w_ukv, ab_w_out, conv_w_in, conv_w, conv_w_out, final_norm_g, loss_target, m_c_ctx, m_norm_g, m_w_mod, m_b_mod, m_ffn_w_gate, m_ffn_w_up, m_ffn_w_down, m_ab_w_in, m_pool_w, m_pool_scale, m_q_norm_g, m_w_uq, m_kv_norm_g, m_w_ukv, m_ab_w_out, m_conv_w_in, m_conv_w, m_conv_w_out, m_final_norm_g, v_c_ctx, v_norm_g, v_w_mod, v_b_mod, v_ffn_w_gate, v_ffn_w_up, v_ffn_w_down, v_ab_w_in, v_pool_w, v_pool_scale, v_q_norm_g, v_w_uq, v_kv_norm_g, v_w_ukv, v_ab_w_out, v_conv_w_in, v_conv_w, v_conv_w_out, v_final_norm_g):
    weights = (c_ctx, norm_g, w_mod, b_mod, ffn_w_gate, ffn_w_up, ffn_w_down, ab_w_in, pool_w, pool_scale, q_norm_g,
               w_uq, kv_norm_g, w_ukv, ab_w_out, conv_w_in, conv_w, conv_w_out, final_norm_g)
    moms = (m_c_ctx, m_norm_g, m_w_mod, m_b_mod, m_ffn_w_gate, m_ffn_w_up, m_ffn_w_down, m_ab_w_in, m_pool_w,
            m_pool_scale, m_q_norm_g, m_w_uq, m_kv_norm_g, m_w_ukv, m_ab_w_out, m_conv_w_in, m_conv_w, m_conv_w_out,
            m_final_norm_g)
    vels = (v_c_ctx, v_norm_g, v_w_mod, v_b_mod, v_ffn_w_gate, v_ffn_w_up, v_ffn_w_down, v_ab_w_in, v_pool_w,
            v_pool_scale, v_q_norm_g, v_w_uq, v_kv_norm_g, v_w_ukv, v_ab_w_out, v_conv_w_in, v_conv_w, v_conv_w_out,
            v_final_norm_g)
    me = 4 * lax.axis_index("x") + 2 * lax.axis_index("y") + lax.axis_index("c")
    n_lat, n_ctx = x.shape[1], ctx.shape[1]
    d = D_MODEL
    mod_cols = w_mod.shape[-1]
    ng_sh, cw_sh = norm_g.shape[-1], conv_w.shape[-1]

    small = jnp.concatenate([c.reshape(-1), norm_g.reshape(-1), conv_w.reshape(-1)])
    small_n = -(-small.shape[0] // 1024) * 1024
    small = jnp.pad(small, (0, small_n - small.shape[0])).reshape(small_n // 128, 128)
    small_all = _exchange("gather_small", small, False).reshape(N_DEV, small_n)
    c_all = small_all[:, :d]
    o1 = d + 6 * ng_sh
    norm_g_full = small_all[:, d:o1].reshape(N_DEV, 2, 3, ng_sh).transpose(1, 2, 0, 3).reshape(2, 3, d)
    conv_w_full = small_all[:, o1:o1 + 3 * cw_sh].reshape(N_DEV, 3, cw_sh).transpose(1, 0, 2).reshape(3, d)

    cond = jnp.concatenate([c_all, jnp.broadcast_to(c_ctx[None, :], (N_DEV, d))], axis=0)
    sil, dsil = _silu_rows("mod_silu", cond)
    w_mod_b = w_mod.astype(BF16)
    b_sh = lax.dynamic_slice(b_mod, (0, me * mod_cols), (2, mod_cols))
    m_part = jnp.stack([_mm(f"mod_fwd{l}", [(sil, w_mod_b[l])], "nn", F32, 16, 384, bias=b_sh[l:l + 1])
                        for l in range(2)], axis=1)
    m_all = _exchange("gather_mod", m_part.reshape(-1, 128), False).reshape(N_DEV, 2 * N_DEV, 2, mod_cols)
    m_mine = lax.dynamic_index_in_dim(m_all, me, axis=1, keepdims=False)
    mod_h = m_mine.transpose(1, 0, 2).reshape(2, N_MOD, d)
    mod_g = m_all[:, N_DEV, 0, :].reshape(N_MOD, d)

    def ffn_shards(i):
        return {f"gate_t{i}": ffn_w_gate[i // 2, i % 2].T, f"up_t{i}": ffn_w_up[i // 2, i % 2].T,
                f"down{i}": ffn_w_down[i // 2, i % 2]}

    local = {**ffn_shards(0), "in_t": ab_w_in[0].T, "uq": w_uq[0], "ukv_t": w_ukv[0].T, "ab_out": ab_w_out[0],
             **ffn_shards(1), **ffn_shards(2), "cin_t": conv_w_in[0].T, "c_out": conv_w_out[0], **ffn_shards(3)}
    ffn_groups = [[[f"gate_t{i}", f"up_t{i}"], [f"down{i}"]] for i in range(4)]
    groups = [*ffn_groups[0], ["in_t", "uq", "ukv_t", "ab_out"], *ffn_groups[1], *ffn_groups[2], ["cin_t", "c_out"],
              *ffn_groups[3]]
    feed = _Feed({nm: a.astype(BF16) for nm, a in local.items()}, groups, me, m_all)

    sq_cols, ds0, g = _local_step(x[0], ctx[0], loss_target[0], mod_h, mod_g, norm_g_full, feed, pool_w[0],
                                  pool_scale, q_norm_g, kv_norm_g, conv_w_full, final_norm_g)
    grad_x = ds0[:n_lat]
    loss = lax.psum(0.5 * jnp.sum(sq_cols) / d, ("x", "y", "c"))
    w_of, m_of, v_of = (dict(zip(WEIGHT_NAMES, t)) for t in (weights, moms, vels))
    results = {}

    def update(nm, grad, view=lambda t: t):
        outs = _adamw(f"adamw_{nm}", view(w_of[nm]), grad.reshape(view(w_of[nm]).shape), view(m_of[nm]), view(v_of[nm]))
        results[nm] = tuple(view(t) for t in (grad.reshape(view(w_of[nm]).shape), *outs))

    def swap(t):
        return jnp.swapaxes(t, -1, -2)

    early = feed.collect(["l1f1", "l1m", "l1f0", "l0f1", "l0m"], ds0)
    update("ab_w_in", early["in_t"], swap)
    update("w_uq", early["uq"])
    update("w_ukv", early["ukv_t"].T)
    update("ab_w_out", early["ab_out"])
    update("conv_w_in", early["cin_t"].T)
    update("conv_w_out", early["c_out"])
    ffn = {}
    for nm, prefix, view in (("ffn_w_gate", "gate_t", swap), ("ffn_w_up", "up_t", swap),
                             ("ffn_w_down", "down", lambda t: t)):
        w4, m4, v4 = (view(t).reshape((4,) + view(t).shape[-2:]) for t in (w_of[nm], m_of[nm], v_of[nm]))
        prev = None
        for i in (3, 2, 1):
            prev = _adamw_part(f"adamw_{nm}{i}", i, w4, early[f"{prefix}{i}"], m4, v4, prev)
        ffn[nm] = (prefix, view, w4, m4, v4, prev)
    late = feed.collect(["l0f0"], ffn["ffn_w_down"][5][1])
    for nm, (prefix, view, w4, m4, v4, prev) in ffn.items():
        outs = _adamw_part(f"adamw_{nm}0", 0, w4, late[f"{prefix}0"], m4, v4, prev)
        results[nm] = tuple(view(t.reshape(view(w_of[nm]).shape)) for t in outs)

    dm = jnp.stack([g["mod_h"], jnp.stack([g["mod_g"], jnp.zeros_like(g["mod_g"])])])
    dm_all = _exchange("gather_dmod", dm.reshape(-1, 128), False, late["down0"]).reshape(N_DEV, 2, 2, N_MOD * d)
    grad_b_mod = _sum_rows("dmod_bias", dm_all.reshape(2 * N_DEV, 2 * N_MOD * d)).reshape(2, N_MOD * d)
    dm_sh = lax.dynamic_slice(dm_all, (0, 0, 0, me * mod_cols), (N_DEV, 2, 2, mod_cols))
    gw_mod, cctx_parts = [], []
    for l in range(2):
        dm_l = dm_sh[:, :, l, :].transpose(1, 0, 2).reshape(2 * N_DEV, mod_cols).astype(BF16)
        gw_mod.append(_mm(f"mod_dw{l}", [(sil, dm_l)], "tn", F32, 512, 384))
        dm_ctx = jnp.concatenate([dm_l[N_DEV:], jnp.zeros((N_DEV, mod_cols), BF16)], axis=0)
        cctx_parts.append(_mm(f"mod_dcond{l}", [(dm_ctx, w_mod_b[l])], "nt", F32, 16, 512))
    cctx_part = _sum_rows("mod_dcond_sum", jnp.concatenate(cctx_parts, axis=0))
    update("w_mod", jnp.stack(gw_mod))
    update("b_mod", grad_b_mod)

    small_g = jnp.concatenate([g["pool_w"].reshape(-1), g["pool_scale"].reshape(-1), g["q_norm_g"].reshape(-1),
                               g["kv_norm_g"].reshape(-1), g["final_norm_g"].reshape(-1), g["norm_g"].reshape(-1),
                               g["conv_w"].reshape(-1), cctx_part.reshape(-1)])
    sizes = [pool_w.size, pool_scale.size, q_norm_g.size, kv_norm_g.size, d, 6 * d, 3 * d, d]
    sg_n = -(-small_g.shape[0] // 1024) * 1024
    small_g = jnp.pad(small_g, (0, sg_n - small_g.shape[0]))
    sg_all = _exchange("gather_small_grads", small_g.reshape(-1, 128), False).reshape(N_DEV, sg_n)
    scale_vec = jnp.concatenate([jnp.ones((1, sum(sizes[:-1])), F32), dsil[N_DEV:N_DEV + 1],
                                 jnp.ones((1, sg_n - sum(sizes)), F32)], axis=1)
    sg = _sum_rows("small_grads_sum", sg_all, scale_vec)[0]
    cuts, pos = [], 0
    for sz in sizes:
        cuts.append(sg[pos:pos + sz])
        pos += sz
    g_pool_w, g_pool_scale, g_q_norm, g_kv_norm, g_final, g_norm_full, g_conv_full, g_c_ctx = cuts
    update("c_ctx", g_c_ctx)
    update("norm_g", lax.dynamic_slice(g_norm_full.reshape(2, 3, d), (0, 0, me * ng_sh), (2, 3, ng_sh)))
    update("conv_w", lax.dynamic_slice(g_conv_full.reshape(3, d), (0, me * cw_sh), (3, cw_sh)))
    update("pool_w", g_pool_w)
    update("pool_scale", g_pool_scale)
    update("q_norm_g", g_q_norm)
    update("kv_norm_g", g_kv_norm)
    update("final_norm_g", g_final)
    outs = [results[nm] for nm in WEIGHT_NAMES]
    return (loss, grad_x[None], *[o[0] for o in outs], *[o[1] for o in outs], *[o[2] for o in outs],
            *[o[3] for o in outs])
```

```python
import functools
import math

import jax
import jax.numpy as jnp
import numpy as np
from jax import lax
from jax.experimental import pallas as pl
from jax.experimental.pallas import tpu as pltpu

F32 = jnp.float32
BF16 = jnp.bfloat16
MESH = pl.DeviceIdType.MESH
SDS = jax.ShapeDtypeStruct

N_DEV = 8
D_MODEL = 1024
N_MOD = 9
D_FF = 2816
POOL_WINDOWS = (2, 4, 8, 16)
POOL_DIM = 512
POOL_GROUP_DIM = 128
HEADS = 8
QK_NOPE = 64
QK_ROPE = 32
QK_HEAD = QK_NOPE + QK_ROPE
V_HEAD = 64
Q_RANK = 768
KV_RANK = 256
GRID_W = 64
ROPE_THETA = 10000.0
RMS_EPS = 1e-6
ATTN_SCALE = 1.0 / math.sqrt(QK_HEAD)
HEAD_PAD = 128
POOL_PAD = 16
PA_POOL, PA_CQ, PA_KV = 0, 768, 1536
PA_KV_W = 384
PA_W = PA_KV + PA_KV_W

ADAM_LR, ADAM_B1, ADAM_B2, ADAM_EPS, ADAM_WD, ADAM_STEP = 0.001, 0.9, 0.999, 1e-08, 0.01, 10

VMEM_LIMIT_BYTES = 56 * 1024 * 1024

NN = ((1,), (0,))
NT = ((1,), (1,))
TN = ((0,), (0,))


def _cparams():
    return pltpu.CompilerParams(vmem_limit_bytes=VMEM_LIMIT_BYTES)


def _dot(a, b, dims):
    return lax.dot_general(a, b, (dims, ((), ())), preferred_element_type=F32)


def _tile(n, cap, mult=8):
    t = (min(cap, n) // mult) * mult
    while t >= mult:
        if n % t == 0:
            return t
        t -= mult
    return n


def _colsum(x):
    return jnp.sum(x, axis=0, keepdims=True)


def _rms(x):
    r = lax.rsqrt(jnp.mean(x * x, axis=-1, keepdims=True) + RMS_EPS)
    return x * r, r


def _rms_bwd(n, r, dn):
    return r * (dn - n * jnp.mean(dn * n, axis=-1, keepdims=True))


def _rowwise(name, fn, t_rows, tm, n_lat, rows, vecs, outs, accs):
    nt = t_rows // tm
    nlt = n_lat // tm
    n_groups = 2 if nlt < nt else 1

    def grp(i):
        return jnp.where(i >= nlt, 1, 0) if n_groups == 2 else 0

    in_specs = [pl.BlockSpec((tm, w), functools.partial(lambda i, cb: (i, cb), cb=cb)) for (_, w, cb) in rows]
    in_specs += [pl.BlockSpec((1,) + v.shape[1:], lambda i: (grp(i), 0, 0)) for v in vecs]
    out_specs = [pl.BlockSpec((tm, w), lambda i: (i, 0)) for (w, _) in outs]
    out_specs += [pl.BlockSpec((1, 1, w), lambda i: (grp(i), 0, 0)) for w in accs]
    out_shape = [SDS((t_rows, w), dt) for (w, dt) in outs] + [SDS((n_groups, 1, w), F32) for w in accs]
    n_r, n_v, n_o = len(rows), len(vecs), len(outs)

    def body(*refs):
        row_vals = [r[...] for r in refs[:n_r]]
        vec_vals = [v[0] for v in refs[n_r:n_r + n_v]]
        out_refs = refs[n_r + n_v:n_r + n_v + n_o]
        acc_refs = refs[n_r + n_v + n_o:]
        out_vals, acc_vals = fn(row_vals, vec_vals)
        for o_ref, o in zip(out_refs, out_vals):
            o_ref[...] = o.astype(o_ref.dtype)
        if acc_refs:
            i = pl.program_id(0)
            first = (i == 0) | (i == nlt) if n_groups == 2 else i == 0

            @pl.when(first)
            def _():
                for a_ref, a in zip(acc_refs, acc_vals):
                    a_ref[0] = a

            @pl.when(jnp.logical_not(first))
            def _():
                for a_ref, a in zip(acc_refs, acc_vals):
                    a_ref[0] += a

    res = pl.pallas_call(
        body, name=name, grid=(nt,), in_specs=in_specs, out_specs=out_specs, out_shape=out_shape,
        compiler_params=_cparams(),
    )(*[r[0] for r in rows], *vecs)
    return res[:n_o], res[n_o:]


RESIDENT_BYTES = 12 * 1024 * 1024


def _mm(name, pairs, mode, out_dtype, tm_cap=256, tn_cap=512, bias=None):
    a0, b0 = pairs[0]
    if mode == "nn":
        m, n, dims = a0.shape[0], b0.shape[1], NN
    elif mode == "nt":
        m, n, dims = a0.shape[0], b0.shape[0], NT
    else:
        m, n, dims = a0.shape[1], b0.shape[1], TN
    b_bytes = sum(b.size * b.dtype.itemsize for _, b in pairs)
    tn = n if b_bytes <= RESIDENT_BYTES else _tile(n, tn_cap, 128)
    tm = _tile(m, tm_cap, 128 if mode == "tn" else 16)

    def a_spec(a):
        if mode == "tn":
            return pl.BlockSpec((a.shape[0], tm), lambda i, j: (0, i))
        return pl.BlockSpec((tm, a.shape[1]), lambda i, j: (i, 0))

    def b_spec(b):
        if mode == "nt":
            return pl.BlockSpec((tn, b.shape[1]), lambda i, j: (j, 0))
        return pl.BlockSpec((b.shape[0], tn), lambda i, j: (0, j))

    in_specs, flat = [], []
    for a, b in pairs:
        in_specs += [a_spec(a), b_spec(b)]
        flat += [a, b]
    if bias is not None:
        in_specs.append(pl.BlockSpec((1, tn), lambda i, j: (0, j)))
        flat.append(bias)
    n_pairs = len(pairs)

    def body(*refs):
        acc = None
        for p in range(n_pairs):
            t = _dot(refs[2 * p][...], refs[2 * p + 1][...], dims)
            acc = t if acc is None else acc + t
        if bias is not None:
            acc = acc + refs[2 * n_pairs][...]
        refs[-1][...] = acc.astype(refs[-1].dtype)

    return pl.pallas_call(
        body, name=name, grid=(m // tm, n // tn), in_specs=in_specs,
        out_specs=pl.BlockSpec((tm, tn), lambda i, j: (i, j)),
        out_shape=SDS((m, n), out_dtype), compiler_params=_cparams(),
    )(*flat)


def _mm_resid(name, a, b, s, mg, k, coef, n_lat):
    t_rows, n = a.shape[0], b.shape[1]
    tm = _tile(math.gcd(n_lat, t_rows), 256, 16)
    nlt = n_lat // tm
    n_groups = 2 if nlt < t_rows // tm else 1

    def grp(i):
        return jnp.where(i >= nlt, 1, 0) if n_groups == 2 else 0

    def body(a_ref, b_ref, s_ref, mg_ref, so_ref, o_ref):
        o = _dot(a_ref[...], b_ref[...], NN)
        gate = mg_ref[0, 3 * k + 2:3 * k + 3, :]
        o_ref[...] = o
        so_ref[...] = s_ref[...] + (coef * gate) * o

    row = pl.BlockSpec((tm, n), lambda i: (i, 0))
    return pl.pallas_call(
        body, name=name, grid=(t_rows // tm,),
        in_specs=[pl.BlockSpec((tm, a.shape[1]), lambda i: (i, 0)), pl.BlockSpec(b.shape, lambda i: (0, 0)), row,
                  pl.BlockSpec((1, mg.shape[1], n), lambda i: (grp(i), 0, 0))],
        out_specs=[row, row], out_shape=[SDS((t_rows, n), F32), SDS((t_rows, n), F32)], compiler_params=_cparams(),
    )(a, b, s, mg)


def _ffn_up(name, u, wg_t, wu_t):
    t_rows, f = u.shape[0], wg_t.shape[0]
    tm = _tile(t_rows, 256, 16)

    def body(u_ref, wg_ref, wu_ref, a_ref, b_ref, h_ref):
        uu = u_ref[...]
        a = _dot(uu, wg_ref[...], NT)
        b = _dot(uu, wu_ref[...], NT)
        sg = jax.nn.sigmoid(a)
        act = a * sg
        a_ref[...] = (b * (sg * (1.0 + a * (1.0 - sg)))).astype(BF16)
        b_ref[...] = act.astype(BF16)
        h_ref[...] = (act * b).astype(BF16)

    w_spec = pl.BlockSpec(wg_t.shape, lambda i: (0, 0))
    o_spec = pl.BlockSpec((tm, f), lambda i: (i, 0))
    return pl.pallas_call(
        body, name=name, grid=(t_rows // tm,),
        in_specs=[pl.BlockSpec((tm, u.shape[1]), lambda i: (i, 0)), w_spec, w_spec],
        out_specs=[o_spec, o_spec, o_spec], out_shape=[SDS((t_rows, f), BF16)] * 3, compiler_params=_cparams(),
    )(u, wg_t, wu_t)


def _ffn_dact(name, do, wd, a, b):
    t_rows, f = do.shape[0], wd.shape[0]
    tm = _tile(t_rows, 256, 16)

    def body(do_ref, wd_ref, a_ref, b_ref, da_ref, db_ref):
        dh = _dot(do_ref[...], wd_ref[...], NT)
        da_ref[...] = (dh * a_ref[...].astype(F32)).astype(BF16)
        db_ref[...] = (dh * b_ref[...].astype(F32)).astype(BF16)

    t_spec = pl.BlockSpec((tm, f), lambda i: (i, 0))
    return pl.pallas_call(
        body, name=name, grid=(t_rows // tm,),
        in_specs=[pl.BlockSpec((tm, do.shape[1]), lambda i: (i, 0)), pl.BlockSpec(wd.shape, lambda i: (0, 0)),
                  t_spec, t_spec],
        out_specs=[t_spec, t_spec], out_shape=[SDS((t_rows, f), BF16)] * 2, compiler_params=_cparams(),
    )(do, wd, a, b)


def _row_tm(t_rows, n_lat):
    return _tile(math.gcd(t_rows, n_lat), 256, 16)


def _adaln_fwd(name, s, mg, k, n_lat):
    t_rows = s.shape[0]

    def fn(rv, vv):
        m = vv[0]
        n, _ = _rms(rv[0])
        u = (n * m[9 + k:10 + k]) * (1.0 + m[3 * k + 1:3 * k + 2]) + m[3 * k:3 * k + 1]
        return [u], []

    (u,), _ = _rowwise(name, fn, t_rows, _row_tm(t_rows, n_lat), n_lat, [(s, D_MODEL, 0)], [mg], [(D_MODEL, BF16)], [])
    return u


def _adaln_bwd(name, s, du, ds_out, mg, k, n_lat):
    t_rows = s.shape[0]

    def fn(rv, vv):
        m = vv[0]
        gain, scale = m[9 + k:10 + k], m[3 * k + 1:3 * k + 2]
        n, r = _rms(rv[0])
        d_u = rv[1]
        dxn = d_u * (1.0 + scale)
        ds = _rms_bwd(n, r, dxn * gain)
        return [rv[2] + ds], [_colsum(d_u), _colsum(d_u * (n * gain)), _colsum(dxn * n)]

    (ds_in,), accs = _rowwise(name, fn, t_rows, _row_tm(t_rows, n_lat), n_lat,
                              [(s, D_MODEL, 0), (du, D_MODEL, 0), (ds_out, D_MODEL, 0)], [mg],
                              [(D_MODEL, F32)], [D_MODEL] * 3)
    return ds_in, accs


def _gate_bwd(name, ds_out, o, mg, k, coef, n_lat):
    t_rows = o.shape[0]

    def fn(rv, vv):
        gate = vv[0][3 * k + 2:3 * k + 3]
        d = coef * rv[0]
        return [d * gate], [_colsum(d * rv[1])]

    (do,), (dgate,) = _rowwise(name, fn, t_rows, _row_tm(t_rows, n_lat), n_lat,
                               [(ds_out, D_MODEL, 0), (o, D_MODEL, 0)], [mg], [(D_MODEL, BF16)], [D_MODEL])
    return do, dgate


def _rmsnorm_fwd(name, x, width, colblk, gain, t_rows):
    def fn(rv, vv):
        n, _ = _rms(rv[0])
        return [n * vv[0]], []

    (y,), _ = _rowwise(name, fn, t_rows, _tile(t_rows, 256, 16), t_rows, [(x, width, colblk)],
                       [gain.reshape(1, 1, width)], [(width, BF16)], [])
    return y


def _rmsnorm_bwd(name, x, width, colblk, dy, gain, t_rows):
    def fn(rv, vv):
        n, r = _rms(rv[0])
        return [_rms_bwd(n, r, rv[1] * vv[0])], [_colsum(rv[1] * n)]

    (dx,), (dgain,) = _rowwise(name, fn, t_rows, _tile(t_rows, 256, 16), t_rows,
                               [(x, width, colblk), (dy, width, 0)], [gain.reshape(1, 1, width)],
                               [(width, F32)], [width])
    return dx, dgain


def _final_loss(name, h, target, gain):
    t_rows = h.shape[0]
    inv_d = 1.0 / D_MODEL

    def fn(rv, vv):
        g = vv[0]
        n, r = _rms(rv[0])
        e = n * g - rv[1]
        dy = e * inv_d
        return [_rms_bwd(n, r, dy * g)], [_colsum(e * e), _colsum(dy * n)]

    (dh,), (sq, dgain) = _rowwise(name, fn, t_rows, _tile(t_rows, 256, 16), t_rows,
                                  [(h, D_MODEL, 0), (target, D_MODEL, 0)], [gain.reshape(1, 1, D_MODEL)],
                                  [(D_MODEL, F32)], [D_MODEL, D_MODEL])
    return dh, sq, dgain


def _rope(name, z, width, colblk, cos, sin, perm, backward, out_dtype):
    t_rows = cos.shape[0]

    def body(z_ref, c_ref, s_ref, p_ref, o_ref):
        zz = z_ref[...]
        pre = zz * s_ref[...] if backward else zz
        hi = pre.astype(BF16)
        lo = (pre - hi.astype(F32)).astype(BF16)
        rot = _dot(hi, p_ref[...], NN) + _dot(lo, p_ref[...], NN)
        if not backward:
            rot = rot * s_ref[...]
        o_ref[...] = (zz * c_ref[...] + rot).astype(o_ref.dtype)

    tm = _tile(t_rows, 256, 16)
    t_spec = pl.BlockSpec((tm, width), lambda i: (i, 0))
    return pl.pallas_call(
        body, name=name, grid=(t_rows // tm,),
        in_specs=[pl.BlockSpec((tm, width), lambda i: (i, colblk)), t_spec, t_spec,
                  pl.BlockSpec((width, width), lambda i: (0, 0))],
        out_specs=t_spec, out_shape=SDS((t_rows, width), out_dtype), compiler_params=_cparams(),
    )(z, cos, sin, perm)


def _window_sum(x, w, transposed):
    n_rows = x.shape[0]
    zeros = jnp.zeros((POOL_PAD, x.shape[1]), F32)
    y = jnp.concatenate([zeros, x, zeros], axis=0)
    total = n_rows + 2 * POOL_PAD
    if transposed:
        y = y + pltpu.roll(y, total - 1, 0)
    else:
        y = y + pltpu.roll(y, 1, 0)
    step = 1
    while 2 * step < w:
        y = pltpu.roll(y, step, 0) + pltpu.roll(y, total - step, 0)
        step *= 2
    return y[POOL_PAD:POOL_PAD + n_rows]


def _window_count(n_rows, w):
    t = lax.broadcasted_iota(jnp.int32, (n_rows, 1), 0)
    lo = jnp.maximum(t - w // 2, 0)
    hi = jnp.minimum(t + (w - w // 2 - 1), n_rows - 1)
    return (hi - lo + 1).astype(F32)


def _pool_fwd(name, proj, n_rows, w_grp, scale):
    def body(x_ref, w_ref, sc_ref, y_ref, p_ref):
        for g, w in enumerate(POOL_WINDOWS):
            cols = slice(g * POOL_GROUP_DIM, (g + 1) * POOL_GROUP_DIM)
            x = x_ref[:, cols]
            p = _window_sum(x, w, False) * (1.0 / _window_count(n_rows, w)) - x
            pb = p.astype(BF16)
            p_ref[:, cols] = pb
            y_ref[:, cols] = (_dot(pb, w_ref[g], NN) * sc_ref[:, cols]).astype(BF16)

    blk = pl.BlockSpec((n_rows, POOL_DIM), lambda i: (0, 0))
    return pl.pallas_call(
        body, name=name, grid=(1,),
        in_specs=[blk, pl.BlockSpec(w_grp.shape, lambda i: (0, 0, 0)), pl.BlockSpec((1, POOL_DIM), lambda i: (0, 0))],
        out_specs=[blk, blk], out_shape=[SDS((n_rows, POOL_DIM), BF16)] * 2, compiler_params=_cparams(),
    )(proj, w_grp, scale)


def _pool_bwd(name, dcat, n_rows, p, w_grp, scale):
    def body(dy_ref, p_ref, w_ref, sc_ref, dx_ref, dw_ref, dsc_ref):
        for g, w in enumerate(POOL_WINDOWS):
            cols = slice(g * POOL_GROUP_DIM, (g + 1) * POOL_GROUP_DIM)
            dy = dy_ref[:, cols]
            pb = p_ref[:, cols]
            pw = _dot(pb, w_ref[g], NN)
            dsc_ref[:, cols] = _colsum(dy * pw)
            dpw = (dy * sc_ref[:, cols]).astype(BF16)
            dw_ref[g] = _dot(pb, dpw, TN)
            dp = _dot(dpw, w_ref[g], NT)
            dx_ref[:, cols] = _window_sum(dp * (1.0 / _window_count(n_rows, w)), w, True) - dp

    blk = pl.BlockSpec((n_rows, POOL_DIM), lambda i: (0, 0))
    w_spec = pl.BlockSpec(w_grp.shape, lambda i: (0, 0, 0))
    v_spec = pl.BlockSpec((1, POOL_DIM), lambda i: (0, 0))
    return pl.pallas_call(
        body, name=name, grid=(1,), in_specs=[blk, blk, w_spec, v_spec], out_specs=[blk, w_spec, v_spec],
        out_shape=[SDS((n_rows, POOL_DIM), F32), SDS(w_grp.shape, F32), SDS((1, POOL_DIM), F32)],
        compiler_params=_cparams(),
    )(dcat, p, w_grp, scale)


def _attn_fwd(name, q, k, v):
    h, n_q, _ = q.shape
    n_k = k.shape[1]
    tq = _tile(n_q, 256, 16)

    def body(q_ref, k_ref, v_ref, o_ref, lse_ref):
        s = _dot(q_ref[...], k_ref[...], NT) * ATTN_SCALE
        m = jnp.max(s, axis=-1, keepdims=True)
        e = jnp.exp(s - m)
        l = jnp.sum(e, axis=-1, keepdims=True)
        p = (e * (1.0 / l)).astype(BF16)
        o_ref[...] = _dot(p, v_ref[...], NN).astype(BF16)
        lse_ref[...] = m + jnp.log(l)

    return pl.pallas_call(
        body, name=name, grid=(h, n_q // tq),
        in_specs=[pl.BlockSpec((None, tq, HEAD_PAD), lambda hh, i: (hh, i, 0)),
                  pl.BlockSpec((None, n_k, HEAD_PAD), lambda hh, i: (hh, 0, 0)),
                  pl.BlockSpec((None, n_k, V_HEAD), lambda hh, i: (hh, 0, 0))],
        out_specs=[pl.BlockSpec((None, tq, V_HEAD), lambda hh, i: (hh, i, 0)),
                   pl.BlockSpec((None, tq, 1), lambda hh, i: (hh, i, 0))],
        out_shape=[SDS((h, n_q, V_HEAD), BF16), SDS((h, n_q, 1), F32)], compiler_params=_cparams(),
    )(q, k, v)


def _attn_bwd(name, q, k, v, o, lse, do):
    h, n_q, _ = q.shape
    n_k = k.shape[1]
    tq = _tile(n_q, 256, 16)

    def body(q_ref, k_ref, v_ref, o_ref, lse_ref, do_ref, dq_ref, dk_ref, dv_ref, dks_ref):
        hh, i = pl.program_id(0), pl.program_id(1)
        qq, kk, dd = q_ref[...], k_ref[...], do_ref[...]
        s = _dot(qq, kk, NT) * ATTN_SCALE
        p = jnp.exp(s - lse_ref[...])
        dp = _dot(dd, v_ref[...], NT)
        delta = jnp.sum(dd.astype(F32) * o_ref[...].astype(F32), axis=-1, keepdims=True)
        ds = (p * (dp - delta) * ATTN_SCALE).astype(BF16)
        dq_ref[...] = _dot(ds, kk, NN)
        dk = _dot(ds, qq, TN)
        dv = _dot(p.astype(BF16), dd, TN)

        @pl.when(i == 0)
        def _():
            dk_ref[...] = dk
            dv_ref[...] = dv

        @pl.when(i > 0)
        def _():
            dk_ref[...] += dk
            dv_ref[...] += dv

        @pl.when((i == 0) & (hh == 0))
        def _():
            dks_ref[...] = dk

        @pl.when((i > 0) | (hh > 0))
        def _():
            dks_ref[...] += dk

    q_spec = pl.BlockSpec((None, tq, HEAD_PAD), lambda hh, i: (hh, i, 0))
    k_spec = pl.BlockSpec((None, n_k, HEAD_PAD), lambda hh, i: (hh, 0, 0))
    v_spec = pl.BlockSpec((None, n_k, V_HEAD), lambda hh, i: (hh, 0, 0))
    o_spec = pl.BlockSpec((None, tq, V_HEAD), lambda hh, i: (hh, i, 0))
    return pl.pallas_call(
        body, name=name, grid=(h, n_q // tq),
        in_specs=[q_spec, k_spec, v_spec, o_spec, pl.BlockSpec((None, tq, 1), lambda hh, i: (hh, i, 0)), o_spec],
        out_specs=[q_spec, k_spec, v_spec, pl.BlockSpec((n_k, HEAD_PAD), lambda hh, i: (0, 0))],
        out_shape=[SDS((h, n_q, HEAD_PAD), F32), SDS((h, n_k, HEAD_PAD), F32), SDS((h, n_k, V_HEAD), F32),
                   SDS((n_k, HEAD_PAD), F32)],
        compiler_params=_cparams(),
    )(q, k, v, o, lse, do)


CONV_COLS = 256


def _shift_rows(x, d):
    n_rows = x.shape[0]
    t = lax.broadcasted_iota(jnp.int32, (n_rows, 1), 0)
    if d > 0:
        return jnp.where(t >= d, pltpu.roll(x, d, 0), 0.0)
    return jnp.where(t < n_rows + d, pltpu.roll(x, n_rows + d, 0), 0.0)


def _conv_fwd(name, z3, conv_w):
    n_rows = z3.shape[0]
    nb = D_MODEL // CONV_COLS

    def body(b_ref, c_ref, v_ref, w_ref, y_ref):
        z = c_ref[...] * v_ref[...]
        zc = w_ref[0:1, :] * _shift_rows(z, 1) + w_ref[1:2, :] * z + w_ref[2:3, :] * _shift_rows(z, -1)
        y_ref[...] = (b_ref[...] * zc).astype(BF16)

    def part(k):
        return pl.BlockSpec((n_rows, CONV_COLS), lambda j: (0, k * nb + j))

    return pl.pallas_call(
        body, name=name, grid=(nb,),
        in_specs=[part(0), part(1), part(2), pl.BlockSpec((3, CONV_COLS), lambda j: (0, j))],
        out_specs=pl.BlockSpec((n_rows, CONV_COLS), lambda j: (0, j)),
        out_shape=SDS((n_rows, D_MODEL), BF16), compiler_params=_cparams(),
    )(z3, z3, z3, conv_w)


def _conv_bwd(name, dy, z3, conv_w):
    n_rows = z3.shape[0]
    nb = D_MODEL // CONV_COLS

    def body(dy_ref, b_ref, c_ref, v_ref, w_ref, db_ref, dc_ref, dv_ref, dw_ref):
        c, v, d_y = c_ref[...], v_ref[...], dy_ref[...]
        z = c * v
        z_dn, z_up = _shift_rows(z, 1), _shift_rows(z, -1)
        zc = w_ref[0:1, :] * z_dn + w_ref[1:2, :] * z + w_ref[2:3, :] * z_up
        db_ref[...] = (d_y * zc).astype(BF16)
        dzc = d_y * b_ref[...]
        dz = w_ref[0:1, :] * _shift_rows(dzc, -1) + w_ref[1:2, :] * dzc + w_ref[2:3, :] * _shift_rows(dzc, 1)
        dc_ref[...] = (dz * v).astype(BF16)
        dv_ref[...] = (dz * c).astype(BF16)
        dw_ref[0:1, :] = _colsum(dzc * z_dn)
        dw_ref[1:2, :] = _colsum(dzc * z)
        dw_ref[2:3, :] = _colsum(dzc * z_up)

    def part(k):
        return pl.BlockSpec((n_rows, CONV_COLS), lambda j: (0, k * nb + j))

    col = pl.BlockSpec((n_rows, CONV_COLS), lambda j: (0, j))
    w_spec = pl.BlockSpec((3, CONV_COLS), lambda j: (0, j))
    return pl.pallas_call(
        body, name=name, grid=(nb,), in_specs=[col, part(0), part(1), part(2), w_spec],
        out_specs=[col, col, col, w_spec],
        out_shape=[SDS((n_rows, D_MODEL), BF16)] * 3 + [SDS((3, D_MODEL), F32)], compiler_params=_cparams(),
    )(dy, z3, z3, z3, conv_w)


def _silu_rows(name, x):
    def body(x_ref, s_ref, d_ref):
        xx = x_ref[...]
        sg = jax.nn.sigmoid(xx)
        s_ref[...] = (xx * sg).astype(BF16)
        d_ref[...] = sg * (1.0 + xx * (1.0 - sg))

    return pl.pallas_call(body, name=name, out_shape=[SDS(x.shape, BF16), SDS(x.shape, F32)])(x)


def _sum_rows(name, x, scale=None):
    r, n = x.shape
    tn = _tile(n, 8192, 128)

    def body(*refs):
        acc = jnp.sum(refs[0][...].astype(F32), axis=0, keepdims=True)
        if scale is not None:
            acc = acc * refs[1][...]
        refs[-1][...] = acc

    in_specs = [pl.BlockSpec((r, tn), lambda j: (0, j))]
    args = [x]
    if scale is not None:
        in_specs.append(pl.BlockSpec((1, tn), lambda j: (0, j)))
        args.append(scale)
    return pl.pallas_call(body, name=name, grid=(n // tn,), in_specs=in_specs,
                          out_specs=pl.BlockSpec((1, tn), lambda j: (0, j)), out_shape=SDS((1, n), F32))(*args)


def _sum_slots(name, x):
    n_slots, r, c = x.shape
    tr = _tile(r, 432, 16)

    def body(x_ref, o_ref):
        acc = x_ref[0].astype(F32)
        for sl in range(1, n_slots):
            acc = acc + x_ref[sl].astype(F32)
        o_ref[...] = acc

    return pl.pallas_call(body, name=name, grid=(r // tr,),
                          in_specs=[pl.BlockSpec((n_slots, tr, c), lambda i: (0, i, 0))],
                          out_specs=pl.BlockSpec((tr, c), lambda i: (i, 0)), out_shape=SDS((r, c), F32),
                          compiler_params=_cparams())(x)


def _adamw(name, w, g, m, v):
    shape = w.shape
    cols = shape[-1]
    rows = w.size // cols
    tr = _tile(rows, 512, 8)
    bc1 = 1.0 - ADAM_B1 ** ADAM_STEP
    bc2 = 1.0 - ADAM_B2 ** ADAM_STEP

    def body(w_ref, g_ref, m_ref, v_ref, d_ref, nm_ref, nv_ref):
        gg = g_ref[...]
        nm = ADAM_B1 * m_ref[...] + (1.0 - ADAM_B1) * gg
        nv = ADAM_B2 * v_ref[...] + (1.0 - ADAM_B2) * (gg * gg)
        nm_ref[...] = nm
        nv_ref[...] = nv
        d_ref[...] = -ADAM_LR * ((nm / bc1) / (jnp.sqrt(nv / bc2) + ADAM_EPS) + ADAM_WD * w_ref[...])

    spec = pl.BlockSpec((tr, cols), lambda i: (i, 0))
    outs = pl.pallas_call(body, name=name, grid=(rows // tr,), in_specs=[spec] * 4, out_specs=[spec] * 3,
                          out_shape=[SDS((rows, cols), F32)] * 3, compiler_params=_cparams())(
        w.reshape(rows, cols), g.reshape(rows, cols), m.reshape(rows, cols), v.reshape(rows, cols))
    return tuple(t.reshape(shape) for t in outs)


def _exchange(name, x, scatter, after=None):
    blk = x.shape[1:] if scatter else x.shape
    extra = [] if after is None else [after]

    def body(x_ref, *rest):
        out_ref, send_sems, recv_sems, local_sem = rest[len(extra):]
        mx, my, mc = lax.axis_index("x"), lax.axis_index("y"), lax.axis_index("c")
        me = 4 * mx + 2 * my + mc
        own = pltpu.make_async_copy(x_ref.at[me] if scatter else x_ref, out_ref.at[me], local_sem)
        own.start()
        copies = []
        for kk in range(1, N_DEV):
            px = jnp.bitwise_xor(mx, (kk >> 2) & 1)
            py = jnp.bitwise_xor(my, (kk >> 1) & 1)
            pc = jnp.bitwise_xor(mc, kk & 1)
            peer = 4 * px + 2 * py + pc
            send = pltpu.make_async_remote_copy(
                src_ref=x_ref.at[peer] if scatter else x_ref, dst_ref=out_ref.at[me],
                send_sem=send_sems.at[kk - 1], recv_sem=recv_sems.at[kk - 1],
                device_id=(px, py, pc), device_id_type=MESH)
            send.start()
            arrival = pltpu.make_async_remote_copy(
                src_ref=x_ref.at[peer] if scatter else x_ref, dst_ref=out_ref.at[peer],
                send_sem=send_sems.at[kk - 1], recv_sem=recv_sems.at[kk - 1],
                device_id=(px, py, pc), device_id_type=MESH)
            copies.append((send, arrival))
        for send, arrival in copies:
            arrival.wait_recv()
            send.wait_send()
        own.wait()

    return pl.pallas_call(
        body, name=name, out_shape=SDS((N_DEV,) + tuple(blk), x.dtype),
        in_specs=[pl.BlockSpec(memory_space=pl.ANY)] * (1 + len(extra)), out_specs=pl.BlockSpec(memory_space=pl.ANY),
        scratch_shapes=[pltpu.SemaphoreType.DMA((N_DEV - 1,)), pltpu.SemaphoreType.DMA((N_DEV - 1,)),
                        pltpu.SemaphoreType.DMA],
    )(x, *extra)


def _rope_perm(pre, reps, post):
    half = QK_ROPE // 4
    width = reps * (pre + QK_ROPE) + post
    p = np.zeros((width, width), np.float32)
    for rep in range(reps):
        s0 = rep * (pre + QK_ROPE) + pre
        for base in (s0, s0 + 2 * half):
            for i in range(half):
                p[base + half + i, base + i] = -1.0
                p[base + i, base + half + i] = 1.0
    return p


def _rope_tables(n_lat, t_rows, pre, reps, post):
    half = QK_ROPE // 4
    pos = jnp.arange(n_lat)
    freqs = jnp.power(ROPE_THETA, -jnp.arange(0, 2 * half, 2, dtype=F32) / (2 * half))
    ang_r = (pos // GRID_W).astype(F32)[:, None] * freqs
    ang_c = (pos % GRID_W).astype(F32)[:, None] * freqs
    ang = jnp.concatenate([ang_r, ang_r, ang_c, ang_c], axis=-1)

    def table(fn, plain):
        slot = jnp.concatenate([jnp.full((n_lat, pre), plain, F32), fn(ang)], axis=-1)
        t = jnp.concatenate([jnp.tile(slot, (1, reps)), jnp.full((n_lat, post), plain, F32)], axis=-1)
        return jnp.concatenate([t, jnp.full((t_rows - n_lat, t.shape[1]), plain, F32)], axis=0)

    return table(jnp.cos, 1.0), table(jnp.sin, 0.0)


def _ffn_half_fwd(tag, s, mg, k, feed, i, coef, n_lat):
    u = _adaln_fwd(f"{tag}_adaln", s, mg, k, n_lat)
    wg_t, wu_t = feed.weights(f"{tag}_up", [f"gate_t{i}", f"up_t{i}"], u)
    a, b, hid = _ffn_up(f"{tag}_up", u, wg_t, wu_t)
    (wd,) = feed.weights(f"{tag}_down", [f"down{i}"], hid)
    s_out, o = _mm_resid(f"{tag}_down", hid, wd, s, mg, k, coef, n_lat)
    return s_out, (s, u, a, b, hid, o, wg_t, wu_t, wd)


def _ffn_half_bwd(tag, ds_out, saved, mg, k, feed, i, coef, n_lat):
    s, u, a, b, hid, o, wg_t, wu_t, wd = saved
    do, dgate = _gate_bwd(f"{tag}_dgate", ds_out, o, mg, k, coef, n_lat)
    da, db = _ffn_dact(f"{tag}_dact", do, wd, a, b)
    dwd = _mm(f"{tag}_dwd", [(hid, do)], "tn", BF16)
    dwg_t = _mm(f"{tag}_dwg", [(da, u)], "tn", BF16)
    dwu_t = _mm(f"{tag}_dwu", [(db, u)], "tn", BF16)
    token = feed.grads(tag, {f"down{i}": dwd, f"gate_t{i}": dwg_t, f"up_t{i}": dwu_t})
    du = _mm(f"{tag}_du", [(da, wg_t), (db, wu_t)], "nn", F32, 384, 512, bias=_after(token))
    ds_in, (dshift, dscale, dgain) = _adaln_bwd(f"{tag}_dadaln", s, du, ds_out, mg, k, n_lat)
    return ds_in, dict(shift=dshift, scale=dscale, gate=dgate, gain=dgain)


def _after(token):
    return jnp.zeros((1, D_MODEL), F32) + token


def _mod_grad(parts, n_groups):
    rows = []
    zero = jnp.zeros((n_groups, 1, D_MODEL), F32)
    for k in range(3):
        for nm in ("shift", "scale", "gate"):
            t = parts[k].get(nm, zero)
            if t.shape[0] < n_groups:
                t = jnp.concatenate([t, jnp.zeros((n_groups - t.shape[0], 1, D_MODEL), F32)], axis=0)
            rows.append(t)
    return jnp.concatenate(rows, axis=1).reshape(n_groups, N_MOD * D_MODEL)


def _local_step(x, ctx, target, mod_h, mod_g, norm_g, feed, pool_w, pool_scale, q_norm_g, kv_norm_g, conv_w,
                final_norm_g):
    n_lat, n_ctx = x.shape[0], ctx.shape[0]
    t_all = n_lat + n_ctx
    mg0 = jnp.stack([jnp.concatenate([mod_h[0], norm_g[0]], axis=0), jnp.concatenate([mod_g, norm_g[0]], axis=0)])
    mg1 = jnp.concatenate([mod_h[1], norm_g[1]], axis=0)[None]

    s0 = jnp.concatenate([x, ctx], axis=0) + feed.start_token()
    s1, sv_f00 = _ffn_half_fwd("l0f0", s0, mg0, 0, feed, 0, 0.5, n_lat)

    ua = _adaln_fwd("l0m_adaln", s1, mg0, 1, n_lat)
    w_in, w_uq, w_ukv_t, w_ab_out = feed.weights("l0m", ["in_t", "uq", "ukv_t", "ab_out"], ua)
    kv_rows = KV_RANK + QK_ROPE
    w_in_t = jnp.concatenate([
        w_in[:POOL_DIM], jnp.zeros((PA_CQ - POOL_DIM, D_MODEL), BF16), w_in[POOL_DIM:POOL_DIM + Q_RANK],
        w_in[POOL_DIM + Q_RANK:], jnp.zeros((PA_KV_W - kv_rows, D_MODEL), BF16)], axis=0)
    proj = _mm("l0m_proj", [(ua, w_in_t)], "nt", F32, 768, 384)
    pool_y, pool_p = _pool_fwd("l0m_pool", proj, n_lat, pool_w.astype(BF16), pool_scale)
    nq = _rmsnorm_fwd("l0m_qnorm", proj, Q_RANK, PA_CQ // Q_RANK, q_norm_g, n_lat)
    q_lin = _mm("l0m_q", [(nq, w_uq)], "nn", F32, 512, 768)
    cos_q, sin_q = _rope_tables(n_lat, n_lat, QK_NOPE, HEADS, 0)
    perm_q = _rope_perm(QK_NOPE, HEADS, 0)
    q_rot = _rope("l0m_qrope", q_lin, Q_RANK, 0, cos_q, sin_q, jnp.asarray(perm_q, BF16), False, BF16)
    cos_k, sin_k = _rope_tables(n_lat, t_all, KV_RANK, 1, PA_KV_W - kv_rows)
    perm_k = _rope_perm(KV_RANK, 1, PA_KV_W - kv_rows)
    kvr = _rope("l0m_krope", proj, PA_KV_W, PA_KV // PA_KV_W, cos_k, sin_k, jnp.asarray(perm_k, BF16), False, F32)
    nkv = _rmsnorm_fwd("l0m_kvnorm", kvr, KV_RANK, 0, kv_norm_g, t_all)
    kv = _mm("l0m_kv", [(nkv, w_ukv_t)], "nt", BF16, 768, 512)
    qh = jnp.pad(q_rot.reshape(n_lat, HEADS, QK_HEAD), ((0, 0), (0, 0), (0, HEAD_PAD - QK_HEAD))).transpose(1, 0, 2)
    kvh = kv.reshape(t_all, HEADS, QK_NOPE + V_HEAD)
    k_rope = jnp.broadcast_to(kvr[:, None, KV_RANK:KV_RANK + QK_ROPE].astype(BF16), (t_all, HEADS, QK_ROPE))
    kh = jnp.concatenate([kvh[:, :, :QK_NOPE], k_rope, jnp.zeros((t_all, HEADS, HEAD_PAD - QK_HEAD), BF16)],
                         axis=-1).transpose(1, 0, 2)
    vh = kvh[:, :, QK_NOPE:].transpose(1, 0, 2)
    oh, lse = _attn_fwd("l0m_attn", qh, kh, vh)
    cat = jnp.concatenate([pool_y, oh.transpose(1, 0, 2).reshape(n_lat, HEADS * V_HEAD)], axis=-1)
    h1 = s1[:n_lat]
    h2, mix_o = _mm_resid("l0m_out", cat, w_ab_out, h1, mg0[:1], 1, 1.0, n_lat)

    h3, sv_f01 = _ffn_half_fwd("l0f1", h2, mg0[:1], 2, feed, 1, 0.5, n_lat)

    h4, sv_f10 = _ffn_half_fwd("l1f0", h3, mg1, 0, feed, 2, 0.5, n_lat)
    uc = _adaln_fwd("l1m_adaln", h4, mg1, 1, n_lat)
    w_cin_t, w_c_out = feed.weights("l1m", ["cin_t", "c_out"], uc)
    z3 = _mm("l1m_in", [(uc, w_cin_t)], "nt", F32)
    yc = _conv_fwd("l1m_conv", z3, conv_w)
    h5, conv_o = _mm_resid("l1m_out", yc, w_c_out, h4, mg1, 1, 1.0, n_lat)
    h6, sv_f11 = _ffn_half_fwd("l1f1", h5, mg1, 2, feed, 3, 0.5, n_lat)

    dh6, sq_cols, d_final_g = _final_loss("loss_head", h6, target, final_norm_g)
    g = {}
    dh5, g["f11"] = _ffn_half_bwd("l1f1", dh6, sv_f11, mg1, 2, feed, 3, 0.5, n_lat)

    do_c, dgate_c = _gate_bwd("l1m_dgate", dh5, conv_o, mg1, 1, 1.0, n_lat)
    dyc = _mm("l1m_dy", [(do_c, w_c_out)], "nt", F32)
    d_c_out = _mm("l1m_dwout", [(yc, do_c)], "tn", BF16)
    db_, dc_, dv_, d_conv_w = _conv_bwd("l1m_dconv", dyc, z3, conv_w)
    dz3 = jnp.concatenate([db_, dc_, dv_], axis=-1)
    d_cin_t = _mm("l1m_dwin", [(dz3, uc)], "tn", BF16)
    token = feed.grads("l1m", {"c_out": d_c_out, "cin_t": d_cin_t})
    duc = _mm("l1m_du", [(dz3, w_cin_t)], "nn", F32, bias=_after(token))
    dh4, (dsh_c, dsc_c, dgn_c) = _adaln_bwd("l1m_dadaln", h4, duc, dh5, mg1, 1, n_lat)
    dh3, g["f10"] = _ffn_half_bwd("l1f0", dh4, sv_f10, mg1, 0, feed, 2, 0.5, n_lat)

    dh2, g["f01"] = _ffn_half_bwd("l0f1", dh3, sv_f01, mg0[:1], 2, feed, 1, 0.5, n_lat)

    do_a, dgate_a = _gate_bwd("l0m_dgate", dh2, mix_o, mg0[:1], 1, 1.0, n_lat)
    dcat = _mm("l0m_dcat", [(do_a, w_ab_out)], "nt", F32)
    d_ab_out = _mm("l0m_dwout", [(cat, do_a)], "tn", BF16)
    d_pool_x, d_pool_w, d_pool_scale = _pool_bwd("l0m_dpool", dcat, n_lat, pool_p, pool_w.astype(BF16), pool_scale)
    doh = dcat[:, POOL_DIM:].reshape(n_lat, HEADS, V_HEAD).transpose(1, 0, 2).astype(BF16)
    dqh, dkh, dvh, dk_sum = _attn_bwd("l0m_dattn", qh, kh, vh, oh, lse, doh)
    dq_rot = dqh[:, :, :QK_HEAD].transpose(1, 0, 2).reshape(n_lat, Q_RANK)
    dq_lin = _rope("l0m_dqrope", dq_rot, Q_RANK, 0, cos_q, sin_q, jnp.asarray(perm_q.T, BF16), True, BF16)
    d_uq = _mm("l0m_dwuq", [(nq, dq_lin)], "tn", BF16, 768, 768)
    dnq = _mm("l0m_dnq", [(dq_lin, w_uq)], "nt", F32, 512, 768)
    dcq, d_q_norm_g = _rmsnorm_bwd("l0m_dqnorm", proj, Q_RANK, PA_CQ // Q_RANK, dnq, q_norm_g, n_lat)
    dkv = jnp.concatenate([dkh[:, :, :QK_NOPE], dvh], axis=-1).transpose(1, 0, 2).reshape(t_all, HEADS * HEAD_PAD)
    dkv = dkv.astype(BF16)
    dnkv = _mm("l0m_dnkv", [(dkv, w_ukv_t)], "nn", F32, 768, 256)
    d_ukv_t = _mm("l0m_dwukv", [(dkv, nkv)], "tn", BF16, 512, 256)
    dckv, d_kv_norm_g = _rmsnorm_bwd("l0m_dkvnorm", kvr, KV_RANK, 0, dnkv, kv_norm_g, t_all)
    dkvr = jnp.concatenate([dckv, dk_sum[:, QK_NOPE:QK_HEAD],
                            jnp.zeros((t_all, PA_KV_W - KV_RANK - QK_ROPE), F32)], axis=-1)
    dpb = _rope("l0m_dkrope", dkvr, PA_KV_W, 0, cos_k, sin_k, jnp.asarray(perm_k.T, BF16), True, F32)
    dproj_lat = jnp.concatenate([d_pool_x, jnp.zeros((n_lat, PA_CQ - POOL_DIM), F32), dcq, dpb[:n_lat]], axis=-1)
    dproj_ctx = jnp.concatenate([jnp.zeros((n_ctx, PA_KV), F32), dpb[n_lat:]], axis=-1)
    dproj = jnp.concatenate([dproj_lat, dproj_ctx], axis=0).astype(BF16)
    d_in_pad = _mm("l0m_dwin", [(dproj, ua)], "tn", BF16, 640, 512)
    d_in_t = jnp.concatenate([d_in_pad[:POOL_DIM], d_in_pad[PA_CQ:PA_CQ + Q_RANK],
                              d_in_pad[PA_KV:PA_KV + kv_rows]], axis=0)
    token = feed.grads("l0m", {"ab_out": d_ab_out, "uq": d_uq, "ukv_t": d_ukv_t, "in_t": d_in_t})
    dua = _mm("l0m_du", [(dproj, w_in_t)], "nn", F32, 768, 512, bias=_after(token))
    dh2_all = jnp.concatenate([dh2, jnp.zeros((n_ctx, D_MODEL), F32)], axis=0)
    ds1, (dsh_a, dsc_a, dgn_a) = _adaln_bwd("l0m_dadaln", s1, dua, dh2_all, mg0, 1, n_lat)
    ds0, g["f00"] = _ffn_half_bwd("l0f0", ds1, sv_f00, mg0, 0, feed, 0, 0.5, n_lat)

    dmod0 = _mod_grad([g["f00"], dict(shift=dsh_a, scale=dsc_a, gate=dgate_a), g["f01"]], 2)
    dmod1 = _mod_grad([g["f10"], dict(shift=dsh_c, scale=dsc_c, gate=dgate_c), g["f11"]], 1)
    d_norm_g = jnp.stack([
        jnp.concatenate([jnp.sum(g["f00"]["gain"], axis=0), jnp.sum(dgn_a, axis=0), g["f01"]["gain"][0]], axis=0),
        jnp.concatenate([g["f10"]["gain"][0], dgn_c[0], g["f11"]["gain"][0]], axis=0)])
    grads = dict(
        pool_w=d_pool_w, pool_scale=d_pool_scale, q_norm_g=d_q_norm_g[0], kv_norm_g=d_kv_norm_g[0],
        conv_w=d_conv_w, final_norm_g=d_final_g[0], norm_g=d_norm_g,
        mod_h=jnp.stack([dmod0[0], dmod1[0]]), mod_g=dmod0[1])
    return sq_cols, ds0, grads


HBM_SPEC = pl.BlockSpec(memory_space=pltpu.HBM)
SEM_SPEC = pl.BlockSpec(memory_space=pltpu.SEMAPHORE)
ANY_SPEC = pl.BlockSpec(memory_space=pl.ANY)
SIDE_EFFECT = pltpu.SideEffectType.DATAFLOW_SIDE_EFFECTING
N_PEERS = N_DEV - 1


def _mesh_place():
    mx, my, mc = lax.axis_index("x"), lax.axis_index("y"), lax.axis_index("c")
    return mx, my, mc, 4 * mx + 2 * my + mc


def _peer(place, kk):
    mx, my, mc, _ = place
    px = jnp.bitwise_xor(mx, (kk >> 2) & 1)
    py = jnp.bitwise_xor(my, (kk >> 1) & 1)
    pc = jnp.bitwise_xor(mc, kk & 1)
    return (px, py, pc), 4 * px + 2 * py + pc


def _hbm(a):
    return pltpu.with_memory_space_constraint(a, pltpu.HBM)


def _landing(block, me):
    zone = lax.empty((N_DEV,) + block.shape, block.dtype)
    return lax.dynamic_update_slice(zone, block[None], (me,) + (0,) * block.ndim)


ALL_PEERS = tuple(range(1, N_DEV))
SIBLING = 1
CHIP_PEERS = (2, 4, 6)
RELAYED = (3, 5, 7)


def _exchange_start(name, srcs, lands, scatter, after, peers=ALL_PEERS):
    n = len(srcs)

    def body(*refs):
        src, land = refs[:n], refs[n:2 * n]
        send_sems, recv_sems, token = refs[2 * n + 1], refs[2 * n + 2], refs[-1]
        place = _mesh_place()
        for a in range(n):
            for kk in peers:
                dev, peer = _peer(place, kk)
                pltpu.make_async_remote_copy(
                    src_ref=src[a].at[peer] if scatter else src[a], dst_ref=land[a].at[place[3]],
                    send_sem=send_sems.at[a * N_PEERS + kk - 1], recv_sem=recv_sems.at[a * N_PEERS + kk - 1],
                    device_id=dev, device_id_type=MESH).start()
        token[...] = jnp.zeros_like(token)

    thru = [pltpu.HBM(t.shape, t.dtype) for t in (*srcs, *lands)]
    res = pl.pallas_call(
        body, name=name,
        out_shape=(pltpu.SemaphoreType.DMA((n * N_PEERS,)), pltpu.SemaphoreType.DMA((n * N_PEERS,)), *thru,
                   SDS((8, 128), F32)),
        in_specs=[HBM_SPEC] * (2 * n) + [ANY_SPEC],
        out_specs=(SEM_SPEC, SEM_SPEC, *([HBM_SPEC] * (2 * n)), pl.BlockSpec(memory_space=pltpu.VMEM)),
        input_output_aliases={i: 2 + i for i in range(2 * n)},
        compiler_params=pltpu.CompilerParams(has_side_effects=SIDE_EFFECT),
    )(*[_hbm(s) for s in srcs], *[_hbm(t) for t in lands], after)
    return res[0], res[1], list(res[2:2 + n]), list(res[2 + n:2 + 2 * n]), res[-1]


def _exchange_wait(name, send_sems, recv_sems, srcs, lands, places, scatter, after):
    n = len(srcs)

    def body(*refs):
        src, land = refs[:n], refs[n:2 * n]
        send, recv = refs[2 * n], refs[2 * n + 1]
        place = _mesh_place()
        for a in range(n):
            for kk in range(1, N_DEV):
                dev, peer = _peer(place, kk)
                cp = pltpu.make_async_remote_copy(
                    src_ref=src[a].at[peer] if scatter else src[a], dst_ref=land[a].at[peer],
                    send_sem=send.at[places[a] * N_PEERS + kk - 1], recv_sem=recv.at[places[a] * N_PEERS + kk - 1],
                    device_id=dev, device_id_type=MESH)
                cp.wait_send()
                cp.wait_recv()

    thru = [pltpu.HBM(t.shape, t.dtype) for t in (*srcs, *lands)]
    res = pl.pallas_call(
        body, name=name, out_shape=tuple(thru),
        in_specs=[HBM_SPEC] * (2 * n) + [SEM_SPEC, SEM_SPEC] + [ANY_SPEC] * len(after),
        out_specs=tuple([HBM_SPEC] * (2 * n)), input_output_aliases={i: i for i in range(2 * n)},
        compiler_params=pltpu.CompilerParams(has_side_effects=SIDE_EFFECT),
    )(*srcs, *lands, send_sems, recv_sems, *after)
    return list(res[n:])


def _gather_relay(name, send1, recv1, lands, places, after):
    n = len(lands)

    def body(*refs):
        land, s1, r1 = refs[:n], refs[n], refs[n + 1]
        s2, r2 = refs[n + 3], refs[n + 4]
        place = _mesh_place()
        sibling = _peer(place, SIBLING)[0]
        for a in range(n):
            for j, kk in enumerate(CHIP_PEERS):
                dev, origin = _peer(place, kk)
                block = land[a].at[origin]
                pltpu.make_async_remote_copy(
                    src_ref=block, dst_ref=block, send_sem=s1.at[places[a] * N_PEERS + kk - 1],
                    recv_sem=r1.at[places[a] * N_PEERS + kk - 1], device_id=dev, device_id_type=MESH).wait_recv()
                pltpu.make_async_remote_copy(
                    src_ref=block, dst_ref=block, send_sem=s2.at[a * 3 + j], recv_sem=r2.at[a * 3 + j],
                    device_id=sibling, device_id_type=MESH).start()

    res = pl.pallas_call(
        body, name=name,
        out_shape=(pltpu.SemaphoreType.DMA((3 * n,)), pltpu.SemaphoreType.DMA((3 * n,)),
                   *[pltpu.HBM(t.shape, t.dtype) for t in lands]),
        in_specs=[HBM_SPEC] * n + [SEM_SPEC, SEM_SPEC, ANY_SPEC],
        out_specs=(SEM_SPEC, SEM_SPEC, *([HBM_SPEC] * n)),
        input_output_aliases={i: 2 + i for i in range(n)},
        compiler_params=pltpu.CompilerParams(has_side_effects=SIDE_EFFECT),
    )(*lands, send1, recv1, after)
    return res[0], res[1], list(res[2:])


def _gather_wait(name, send1, recv1, send2, recv2, srcs, lands, places, after):
    n = len(lands)

    def body(*refs):
        src, land = refs[:n], refs[n:2 * n]
        s1, r1, s2, r2 = refs[2 * n:2 * n + 4]
        place = _mesh_place()
        for a in range(n):
            for kk in (SIBLING,) + CHIP_PEERS:
                dev, origin = _peer(place, kk)
                first = pltpu.make_async_remote_copy(
                    src_ref=src[a], dst_ref=land[a].at[origin], send_sem=s1.at[places[a] * N_PEERS + kk - 1],
                    recv_sem=r1.at[places[a] * N_PEERS + kk - 1], device_id=dev, device_id_type=MESH)
                first.wait_send()
                if kk == SIBLING:
                    first.wait_recv()
            for j, kk in enumerate(CHIP_PEERS):
                dev, origin = _peer(place, kk + 1)
                relay = pltpu.make_async_remote_copy(
                    src_ref=src[a], dst_ref=land[a].at[origin], send_sem=s2.at[a * 3 + j], recv_sem=r2.at[a * 3 + j],
                    device_id=dev, device_id_type=MESH)
                relay.wait_send()
                relay.wait_recv()

    arrays = (*srcs, *lands)
    res = pl.pallas_call(
        body, name=name, out_shape=tuple(pltpu.HBM(t.shape, t.dtype) for t in arrays),
        in_specs=[HBM_SPEC] * (2 * n) + [SEM_SPEC] * 4 + [ANY_SPEC], out_specs=tuple([HBM_SPEC] * (2 * n)),
        input_output_aliases={i: i for i in range(2 * n)},
        compiler_params=pltpu.CompilerParams(has_side_effects=SIDE_EFFECT),
    )(*arrays, send1, recv1, send2, recv2, after)
    return list(res[n:])


class _Feed:
    def __init__(self, shards, groups, me, after):
        self.names, self.me, self.groups, self.pos = list(shards), me, groups, 0
        srcs = [shards[nm] for nm in self.names]
        lands = [_landing(s, me) for s in srcs]
        self.send, self.recv, self.srcs, self.lands, self.token = _exchange_start(
            "gather_start", srcs, lands, False, after, (SIBLING,) + CHIP_PEERS)
        self.relay = self._relay("gather_relay_first", groups[0], self.token)
        self.pending = []

    def _relay(self, name, names, after):
        places = [self.names.index(nm) for nm in names]
        send2, recv2, lands = _gather_relay(name, self.send, self.recv, [self.lands[i] for i in places], places, after)
        for i, t in zip(places, lands):
            self.lands[i] = t
        return send2, recv2

    def start_token(self):
        return self.token[0, 0]

    def weights(self, tag, names, after):
        assert names == self.groups[self.pos], (names, self.groups[self.pos])
        send2, recv2 = self.relay
        if self.pos + 1 < len(self.groups):
            nxt = self.groups[self.pos + 1]
            self.relay = self._relay(f"gather_relay_{tag}", nxt, after)
            after = self.lands[self.names.index(nxt[0])]
        places = [self.names.index(nm) for nm in names]
        got = _gather_wait(f"gather_wait_{tag}", self.send, self.recv, send2, recv2, [self.srcs[i] for i in places],
                           [self.lands[i] for i in places], places, after)
        self.pos += 1
        return [t.reshape((N_DEV * t.shape[1],) + t.shape[2:]) for t in got]

    def grads(self, tag, full):
        names = list(full)
        srcs = [full[nm].reshape((N_DEV, full[nm].shape[0] // N_DEV) + full[nm].shape[1:]) for nm in names]
        lands = [_landing(lax.dynamic_index_in_dim(s, self.me, 0, keepdims=False), self.me) for s in srcs]
        send, recv, srcs, lands, token = _exchange_start(f"scatter_start_{tag}", srcs, lands, True, srcs[0])
        self.pending.append((tag, names, send, recv, srcs, lands))
        return token[0, 0]

    def collect(self, tags, after):
        out = {}
        for tag, names, send, recv, srcs, lands in self.pending:
            if tag not in tags:
                continue
            got = _exchange_wait(f"scatter_wait_{tag}", send, recv, srcs, lands, list(range(len(names))), True, after)
            for nm, slots in zip(names, got):
                out[nm] = _sum_slots(f"reduce_{nm}", slots)
        return out


def _adamw_math(w, gg, m, v):
    nm = ADAM_B1 * m + (1.0 - ADAM_B1) * gg
    nv = ADAM_B2 * v + (1.0 - ADAM_B2) * (gg * gg)
    bc1 = 1.0 - ADAM_B1 ** ADAM_STEP
    bc2 = 1.0 - ADAM_B2 ** ADAM_STEP
    return -ADAM_LR * ((nm / bc1) / (jnp.sqrt(nv / bc2) + ADAM_EPS) + ADAM_WD * w), nm, nv


def _adamw_part(name, i, w, g, m, v, prev):
    n_parts, rows, cols = w.shape
    tr = _tile(rows, 512, 8)
    if prev is None:
        prev = tuple(lax.empty(w.shape, F32) for _ in range(4))

    def body(w_ref, g_ref, m_ref, v_ref, *rest):
        go_ref, d_ref, nm_ref, nv_ref = rest[4:]
        gg = g_ref[...]
        d, nm, nv = _adamw_math(w_ref[...], gg, m_ref[...], v_ref[...])
        go_ref[...] = gg
        d_ref[...] = d
        nm_ref[...] = nm
        nv_ref[...] = nv

    part = pl.BlockSpec((None, tr, cols), lambda r: (i, r, 0))
    return pl.pallas_call(
        body, name=name, grid=(rows // tr,),
        in_specs=[part, pl.BlockSpec((tr, cols), lambda r: (r, 0)), part, part] + [ANY_SPEC] * 4,
        out_specs=[part] * 4, out_shape=[SDS(w.shape, F32)] * 4,
        input_output_aliases={4 + k: k for k in range(4)}, compiler_params=_cparams(),
    )(w, g, m, v, *prev)


WEIGHT_NAMES = ("c_ctx", "norm_g", "w_mod", "b_mod", "ffn_w_gate", "ffn_w_up", "ffn_w_down", "ab_w_in", "pool_w",
                "pool_scale", "q_norm_g", "w_uq", "kv_norm_g", "w_ukv", "ab_w_out", "conv_w_in", "conv_w",
                "conv_w_out", "final_norm_g")


def kernel(x, c, ctx, c_ctx, norm_g, w_mod, b_mod, ffn_w_gate, ffn_w_up, ffn_w_down, ab_w_in, pool_w, pool_scale, q_norm_g, w_uq, kv_norm_g, w_ukv, ab_w_out, conv_w_in, conv_w, conv_w_out, final_norm_g, loss_target, m_c_ctx, m_norm_g, m_w_mod, m_b_mod, m_ffn_w_gate, m_ffn_w_up, m_ffn_w_down, m_ab_w_in, m_pool_w, m_pool_scale, m_q_norm_g, m_w_uq, m_kv_norm_g, m_w_ukv, m_ab_w_out, m_conv_w_in, m_conv_w, m_conv_w_out, m_final_norm_g, v_c_ctx, v_norm_g, v_w_mod, v_b_mod, v_ffn_w_gate, v_ffn_w_up, v_ffn_w_down, v_ab_w_in, v_pool_w, v_pool_scale, v_q_norm_g, v_w_uq, v_kv_norm_g, v_w_ukv, v_ab_w_out, v_conv_w_in, v_conv_w, v_conv_w_out, v_final_norm_g):
    weights = (c_ctx, norm_g, w_mod, b_mod, ffn_w_gate, ffn_w_up, ffn_w_down, ab_w_in, pool_w, pool_scale, q_norm_g,
               w_uq, kv_norm_g, w_ukv, ab_w_out, conv_w_in, conv_w, conv_w_out, final_norm_g)
    moms = (m_c_ctx, m_norm_g, m_w_mod, m_b_mod, m_ffn_w_gate, m_ffn_w_up, m_ffn_w_down, m_ab_w_in, m_pool_w,
            m_pool_scale, m_q_norm_g, m_w_uq, m_kv_norm_g, m_w_ukv, m_ab_w_out, m_conv_w_in, m_conv_w, m_conv_w_out,
            m_final_norm_g)
    vels = (v_c_ctx, v_norm_g, v_w_mod, v_b_mod, v_ffn_w_gate, v_ffn_w_up, v_ffn_w_down, v_ab_w_in, v_pool_w,
            v_pool_scale, v_q_norm_g, v_w_uq, v_kv_norm_g, v_w_ukv, v_ab_w_out, v_conv_w_in, v_conv_w, v_conv_w_out,
            v_final_norm_g)
    me = 4 * lax.axis_index("x") + 2 * lax.axis_index("y") + lax.axis_index("c")
    n_lat, n_ctx = x.shape[1], ctx.shape[1]
    d = D_MODEL
    mod_cols = w_mod.shape[-1]
    ng_sh, cw_sh = norm_g.shape[-1], conv_w.shape[-1]

    small = jnp.concatenate([c.reshape(-1), norm_g.reshape(-1), conv_w.reshape(-1)])
    small_n = -(-small.shape[0] // 1024) * 1024
    small = jnp.pad(small, (0, small_n - small.shape[0])).reshape(small_n // 128, 128)
    small_all = _exchange("gather_small", small, False).reshape(N_DEV, small_n)
    c_all = small_all[:, :d]
    o1 = d + 6 * ng_sh
    norm_g_full = small_all[:, d:o1].reshape(N_DEV, 2, 3, ng_sh).transpose(1, 2, 0, 3).reshape(2, 3, d)
    conv_w_full = small_all[:, o1:o1 + 3 * cw_sh].reshape(N_DEV, 3, cw_sh).transpose(1, 0, 2).reshape(3, d)

    cond = jnp.concatenate([c_all, jnp.broadcast_to(c_ctx[None, :], (N_DEV, d))], axis=0)
    sil, dsil = _silu_rows("mod_silu", cond)
    w_mod_b = w_mod.astype(BF16)
    b_sh = lax.dynamic_slice(b_mod, (0, me * mod_cols), (2, mod_cols))
    m_part = jnp.stack([_mm(f"mod_fwd{l}", [(sil, w_mod_b[l])], "nn", F32, 16, 384, bias=b_sh[l:l + 1])
                        for l in range(2)], axis=1)
    m_all = _exchange("gather_mod", m_part.reshape(-1, 128), False).reshape(N_DEV, 2 * N_DEV, 2, mod_cols)
    m_mine = lax.dynamic_index_in_dim(m_all, me, axis=1, keepdims=False)
    mod_h = m_mine.transpose(1, 0, 2).reshape(2, N_MOD, d)
    mod_g = m_all[:, N_DEV, 0, :].reshape(N_MOD, d)

    def ffn_shards(i):
        return {f"gate_t{i}": ffn_w_gate[i // 2, i % 2].T, f"up_t{i}": ffn_w_up[i // 2, i % 2].T,
                f"down{i}": ffn_w_down[i // 2, i % 2]}

    local = {**ffn_shards(0), "in_t": ab_w_in[0].T, "uq": w_uq[0], "ukv_t": w_ukv[0].T, "ab_out": ab_w_out[0],
             **ffn_shards(1), **ffn_shards(2), "cin_t": conv_w_in[0].T, "c_out": conv_w_out[0], **ffn_shards(3)}
    ffn_groups = [[[f"gate_t{i}", f"up_t{i}"], [f"down{i}"]] for i in range(4)]
    groups = [*ffn_groups[0], ["in_t", "uq", "ukv_t", "ab_out"], *ffn_groups[1], *ffn_groups[2], ["cin_t", "c_out"],
              *ffn_groups[3]]
    feed = _Feed({nm: a.astype(BF16) for nm, a in local.items()}, groups, me, m_all)

    sq_cols, ds0, g = _local_step(x[0], ctx[0], loss_target[0], mod_h, mod_g, norm_g_full, feed, pool_w[0],
                                  pool_scale, q_norm_g, kv_norm_g, conv_w_full, final_norm_g)
    grad_x = ds0[:n_lat]
    loss = lax.psum(0.5 * jnp.sum(sq_cols) / d, ("x", "y", "c"))
    w_of, m_of, v_of = (dict(zip(WEIGHT_NAMES, t)) for t in (weights, moms, vels))
    results = {}

    def update(nm, grad, view=lambda t: t):
        outs = _adamw(f"adamw_{nm}", view(w_of[nm]), grad.reshape(view(w_of[nm]).shape), view(m_of[nm]), view(v_of[nm]))
        results[nm] = tuple(view(t) for t in (grad.reshape(view(w_of[nm]).shape), *outs))

    def swap(t):
        return jnp.swapaxes(t, -1, -2)

    early = feed.collect(["l1f1", "l1m", "l1f0", "l0f1", "l0m"], [ds0])
    update("ab_w_in", early["in_t"], swap)
    update("w_uq", early["uq"])
    update("w_ukv", early["ukv_t"].T)
    update("ab_w_out", early["ab_out"])
    update("conv_w_in", early["cin_t"].T)
    update("conv_w_out", early["c_out"])
    ffn = {}
    for nm, prefix, view in (("ffn_w_gate", "gate_t", swap), ("ffn_w_up", "up_t", swap),
                             ("ffn_w_down", "down", lambda t: t)):
        w4, m4, v4 = (view(t).reshape((4,) + view(t).shape[-2:]) for t in (w_of[nm], m_of[nm], v_of[nm]))
        prev = None
        for i in (3, 2, 1):
            prev = _adamw_part(f"adamw_{nm}{i}", i, w4, early[f"{prefix}{i}"], m4, v4, prev)
        ffn[nm] = (prefix, view, w4, m4, v4, prev)
    done_early = [results[nm][1] for nm in results] + [state[5][1] for state in ffn.values()]
    late = feed.collect(["l0f0"], done_early)
    for nm, (prefix, view, w4, m4, v4, prev) in ffn.items():
        outs = _adamw_part(f"adamw_{nm}0", 0, w4, late[f"{prefix}0"], m4, v4, prev)
        results[nm] = tuple(view(t.reshape(view(w_of[nm]).shape)) for t in outs)

    dm = jnp.stack([g["mod_h"], jnp.stack([g["mod_g"], jnp.zeros_like(g["mod_g"])])])
    dm_all = _exchange("gather_dmod", dm.reshape(-1, 128), False, late["down0"]).reshape(N_DEV, 2, 2, N_MOD * d)
    grad_b_mod = _sum_rows("dmod_bias", dm_all.reshape(2 * N_DEV, 2 * N_MOD * d)).reshape(2, N_MOD * d)
    dm_sh = lax.dynamic_slice(dm_all, (0, 0, 0, me * mod_cols), (N_DEV, 2, 2, mod_cols))
    gw_mod, cctx_parts = [], []
    for l in range(2):
        dm_l = dm_sh[:, :, l, :].transpose(1, 0, 2).reshape(2 * N_DEV, mod_cols).astype(BF16)
        gw_mod.append(_mm(f"mod_dw{l}", [(sil, dm_l)], "tn", F32, 512, 384))
        dm_ctx = jnp.concatenate([dm_l[N_DEV:], jnp.zeros((N_DEV, mod_cols), BF16)], axis=0)
        cctx_parts.append(_mm(f"mod_dcond{l}", [(dm_ctx, w_mod_b[l])], "nt", F32, 16, 512))
    cctx_part = _sum_rows("mod_dcond_sum", jnp.concatenate(cctx_parts, axis=0))
    update("w_mod", jnp.stack(gw_mod))
    update("b_mod", grad_b_mod)

    small_g = jnp.concatenate([g["pool_w"].reshape(-1), g["pool_scale"].reshape(-1), g["q_norm_g"].reshape(-1),
                               g["kv_norm_g"].reshape(-1), g["final_norm_g"].reshape(-1), g["norm_g"].reshape(-1),
                               g["conv_w"].reshape(-1), cctx_part.reshape(-1)])
    sizes = [pool_w.size, pool_scale.size, q_norm_g.size, kv_norm_g.size, d, 6 * d, 3 * d, d]
    sg_n = -(-small_g.shape[0] // 1024) * 1024
    small_g = jnp.pad(small_g, (0, sg_n - small_g.shape[0]))
    sg_all = _exchange("gather_small_grads", small_g.reshape(-1, 128), False).reshape(N_DEV, sg_n)
    scale_vec = jnp.concatenate([jnp.ones((1, sum(sizes[:-1])), F32), dsil[N_DEV:N_DEV + 1],
                                 jnp.ones((1, sg_n - sum(sizes)), F32)], axis=1)
    sg = _sum_rows("small_grads_sum", sg_all, scale_vec)[0]
    cuts, pos = [], 0
    for sz in sizes:
        cuts.append(sg[pos:pos + sz])
        pos += sz
    g_pool_w, g_pool_scale, g_q_norm, g_kv_norm, g_final, g_norm_full, g_conv_full, g_c_ctx = cuts
    update("c_ctx", g_c_ctx)
    update("norm_g", lax.dynamic_slice(g_norm_full.reshape(2, 3, d), (0, 0, me * ng_sh), (2, 3, ng_sh)))
    update("conv_w", lax.dynamic_slice(g_conv_full.reshape(3, d), (0, me * cw_sh), (3, cw_sh)))
    update("pool_w", g_pool_w)
    update("pool_scale", g_pool_scale)
    update("q_norm_g", g_q_norm)
    update("kv_norm_g", g_kv_norm)
    update("final_norm_g", g_final)
    outs = [results[nm] for nm in WEIGHT_NAMES]
    return (loss, grad_x[None], *[o[0] for o in outs], *[o[1] for o in outs], *[o[2] for o in outs],
            *[o[3] for o in outs])
```

```python
import functools
import math

import jax
import jax.numpy as jnp
import numpy as np
from jax import lax
from jax.experimental import pallas as pl
from jax.experimental.pallas import tpu as pltpu

F32 = jnp.float32
BF16 = jnp.bfloat16
MESH = pl.DeviceIdType.MESH
SDS = jax.ShapeDtypeStruct

N_DEV = 8
D_MODEL = 1024
N_MOD = 9
D_FF = 2816
POOL_WINDOWS = (2, 4, 8, 16)
POOL_DIM = 512
POOL_GROUP_DIM = 128
HEADS = 8
QK_NOPE = 64
QK_ROPE = 32
QK_HEAD = QK_NOPE + QK_ROPE
V_HEAD = 64
Q_RANK = 768
KV_RANK = 256
GRID_W = 64
ROPE_THETA = 10000.0
RMS_EPS = 1e-6
ATTN_SCALE = 1.0 / math.sqrt(QK_HEAD)
HEAD_PAD = 128
POOL_PAD = 16
PA_POOL, PA_CQ, PA_KV = 0, 768, 1536
PA_KV_W = 384
PA_W = PA_KV + PA_KV_W

ADAM_LR, ADAM_B1, ADAM_B2, ADAM_EPS, ADAM_WD, ADAM_STEP = 0.001, 0.9, 0.999, 1e-08, 0.01, 10

VMEM_LIMIT_BYTES = 56 * 1024 * 1024

NN = ((1,), (0,))
NT = ((1,), (1,))
TN = ((0,), (0,))


def _cparams():
    return pltpu.CompilerParams(vmem_limit_bytes=VMEM_LIMIT_BYTES)


def _dot(a, b, dims):
    return lax.dot_general(a, b, (dims, ((), ())), preferred_element_type=F32)


def _tile(n, cap, mult=8):
    t = (min(cap, n) // mult) * mult
    while t >= mult:
        if n % t == 0:
            return t
        t -= mult
    return n


def _colsum(x):
    return jnp.sum(x, axis=0, keepdims=True)


def _rms(x):
    r = lax.rsqrt(jnp.mean(x * x, axis=-1, keepdims=True) + RMS_EPS)
    return x * r, r


def _rms_bwd(n, r, dn):
    return r * (dn - n * jnp.mean(dn * n, axis=-1, keepdims=True))


def _rowwise(name, fn, t_rows, tm, n_lat, rows, vecs, outs, accs):
    nt = t_rows // tm
    nlt = n_lat // tm
    n_groups = 2 if nlt < nt else 1

    def grp(i):
        return jnp.where(i >= nlt, 1, 0) if n_groups == 2 else 0

    in_specs = [pl.BlockSpec((tm, w), functools.partial(lambda i, cb: (i, cb), cb=cb)) for (_, w, cb) in rows]
    in_specs += [pl.BlockSpec((1,) + v.shape[1:], lambda i: (grp(i), 0, 0)) for v in vecs]
    out_specs = [pl.BlockSpec((tm, w), lambda i: (i, 0)) for (w, _) in outs]
    out_specs += [pl.BlockSpec((1, 1, w), lambda i: (grp(i), 0, 0)) for w in accs]
    out_shape = [SDS((t_rows, w), dt) for (w, dt) in outs] + [SDS((n_groups, 1, w), F32) for w in accs]
    n_r, n_v, n_o = len(rows), len(vecs), len(outs)

    def body(*refs):
        row_vals = [r[...] for r in refs[:n_r]]
        vec_vals = [v[0] for v in refs[n_r:n_r + n_v]]
        out_refs = refs[n_r + n_v:n_r + n_v + n_o]
        acc_refs = refs[n_r + n_v + n_o:]
        out_vals, acc_vals = fn(row_vals, vec_vals)
        for o_ref, o in zip(out_refs, out_vals):
            o_ref[...] = o.astype(o_ref.dtype)
        if acc_refs:
            i = pl.program_id(0)
            first = (i == 0) | (i == nlt) if n_groups == 2 else i == 0

            @pl.when(first)
            def _():
                for a_ref, a in zip(acc_refs, acc_vals):
                    a_ref[0] = a

            @pl.when(jnp.logical_not(first))
            def _():
                for a_ref, a in zip(acc_refs, acc_vals):
                    a_ref[0] += a

    res = pl.pallas_call(
        body, name=name, grid=(nt,), in_specs=in_specs, out_specs=out_specs, out_shape=out_shape,
        compiler_params=_cparams(),
    )(*[r[0] for r in rows], *vecs)
    return res[:n_o], res[n_o:]


RESIDENT_BYTES = 12 * 1024 * 1024


def _mm(name, pairs, mode, out_dtype, tm_cap=256, tn_cap=512, bias=None):
    a0, b0 = pairs[0]
    if mode == "nn":
        m, n, dims = a0.shape[0], b0.shape[1], NN
    elif mode == "nt":
        m, n, dims = a0.shape[0], b0.shape[0], NT
    else:
        m, n, dims = a0.shape[1], b0.shape[1], TN
    b_bytes = sum(b.size * b.dtype.itemsize for _, b in pairs)
    tn = n if b_bytes <= RESIDENT_BYTES else _tile(n, tn_cap, 128)
    tm = _tile(m, tm_cap, 128 if mode == "tn" else 16)

    def a_spec(a):
        if mode == "tn":
            return pl.BlockSpec((a.shape[0], tm), lambda i, j: (0, i))
        return pl.BlockSpec((tm, a.shape[1]), lambda i, j: (i, 0))

    def b_spec(b):
        if mode == "nt":
            return pl.BlockSpec((tn, b.shape[1]), lambda i, j: (j, 0))
        return pl.BlockSpec((b.shape[0], tn), lambda i, j: (0, j))

    in_specs, flat = [], []
    for a, b in pairs:
        in_specs += [a_spec(a), b_spec(b)]
        flat += [a, b]
    if bias is not None:
        in_specs.append(pl.BlockSpec((1, tn), lambda i, j: (0, j)))
        flat.append(bias)
    n_pairs = len(pairs)

    def body(*refs):
        acc = None
        for p in range(n_pairs):
            t = _dot(refs[2 * p][...], refs[2 * p + 1][...], dims)
            acc = t if acc is None else acc + t
        if bias is not None:
            acc = acc + refs[2 * n_pairs][...]
        refs[-1][...] = acc.astype(refs[-1].dtype)

    return pl.pallas_call(
        body, name=name, grid=(m // tm, n // tn), in_specs=in_specs,
        out_specs=pl.BlockSpec((tm, tn), lambda i, j: (i, j)),
        out_shape=SDS((m, n), out_dtype), compiler_params=_cparams(),
    )(*flat)


def _mm_resid(name, a, b, s, mg, k, coef, n_lat):
    t_rows, n = a.shape[0], b.shape[1]
    tm = _tile(math.gcd(n_lat, t_rows), 256, 16)
    nlt = n_lat // tm
    n_groups = 2 if nlt < t_rows // tm else 1

    def grp(i):
        return jnp.where(i >= nlt, 1, 0) if n_groups == 2 else 0

    def body(a_ref, b_ref, s_ref, mg_ref, so_ref, o_ref):
        o = _dot(a_ref[...], b_ref[...], NN)
        gate = mg_ref[0, 3 * k + 2:3 * k + 3, :]
        o_ref[...] = o
        so_ref[...] = s_ref[...] + (coef * gate) * o

    row = pl.BlockSpec((tm, n), lambda i: (i, 0))
    return pl.pallas_call(
        body, name=name, grid=(t_rows // tm,),
        in_specs=[pl.BlockSpec((tm, a.shape[1]), lambda i: (i, 0)), pl.BlockSpec(b.shape, lambda i: (0, 0)), row,
                  pl.BlockSpec((1, mg.shape[1], n), lambda i: (grp(i), 0, 0))],
        out_specs=[row, row], out_shape=[SDS((t_rows, n), F32), SDS((t_rows, n), F32)], compiler_params=_cparams(),
    )(a, b, s, mg)


def _ffn_up(name, u, wg_t, wu_t):
    t_rows, f = u.shape[0], wg_t.shape[0]
    tm = _tile(t_rows, 256, 16)

    def body(u_ref, wg_ref, wu_ref, a_ref, b_ref, h_ref):
        uu = u_ref[...]
        a = _dot(uu, wg_ref[...], NT)
        b = _dot(uu, wu_ref[...], NT)
        sg = jax.nn.sigmoid(a)
        act = a * sg
        a_ref[...] = (b * (sg * (1.0 + a * (1.0 - sg)))).astype(BF16)
        b_ref[...] = act.astype(BF16)
        h_ref[...] = (act * b).astype(BF16)

    w_spec = pl.BlockSpec(wg_t.shape, lambda i: (0, 0))
    o_spec = pl.BlockSpec((tm, f), lambda i: (i, 0))
    return pl.pallas_call(
        body, name=name, grid=(t_rows // tm,),
        in_specs=[pl.BlockSpec((tm, u.shape[1]), lambda i: (i, 0)), w_spec, w_spec],
        out_specs=[o_spec, o_spec, o_spec], out_shape=[SDS((t_rows, f), BF16)] * 3, compiler_params=_cparams(),
    )(u, wg_t, wu_t)


def _ffn_dact(name, do, wd, a, b):
    t_rows, f = do.shape[0], wd.shape[0]
    tm = _tile(t_rows, 256, 16)

    def body(do_ref, wd_ref, a_ref, b_ref, da_ref, db_ref):
        dh = _dot(do_ref[...], wd_ref[...], NT)
        da_ref[...] = (dh * a_ref[...].astype(F32)).astype(BF16)
        db_ref[...] = (dh * b_ref[...].astype(F32)).astype(BF16)

    t_spec = pl.BlockSpec((tm, f), lambda i: (i, 0))
    return pl.pallas_call(
        body, name=name, grid=(t_rows // tm,),
        in_specs=[pl.BlockSpec((tm, do.shape[1]), lambda i: (i, 0)), pl.BlockSpec(wd.shape, lambda i: (0, 0)),
                  t_spec, t_spec],
        out_specs=[t_spec, t_spec], out_shape=[SDS((t_rows, f), BF16)] * 2, compiler_params=_cparams(),
    )(do, wd, a, b)


def _row_tm(t_rows, n_lat):
    return _tile(math.gcd(t_rows, n_lat), 256, 16)


def _adaln_fwd(name, s, mg, k, n_lat):
    t_rows = s.shape[0]

    def fn(rv, vv):
        m = vv[0]
        n, _ = _rms(rv[0])
        u = (n * m[9 + k:10 + k]) * (1.0 + m[3 * k + 1:3 * k + 2]) + m[3 * k:3 * k + 1]
        return [u], []

    (u,), _ = _rowwise(name, fn, t_rows, _row_tm(t_rows, n_lat), n_lat, [(s, D_MODEL, 0)], [mg], [(D_MODEL, BF16)], [])
    return u


def _adaln_bwd(name, s, du, ds_out, mg, k, n_lat):
    t_rows = s.shape[0]

    def fn(rv, vv):
        m = vv[0]
        gain, scale = m[9 + k:10 + k], m[3 * k + 1:3 * k + 2]
        n, r = _rms(rv[0])
        d_u = rv[1]
        dxn = d_u * (1.0 + scale)
        ds = _rms_bwd(n, r, dxn * gain)
        return [rv[2] + ds], [_colsum(d_u), _colsum(d_u * (n * gain)), _colsum(dxn * n)]

    (ds_in,), accs = _rowwise(name, fn, t_rows, _row_tm(t_rows, n_lat), n_lat,
                              [(s, D_MODEL, 0), (du, D_MODEL, 0), (ds_out, D_MODEL, 0)], [mg],
                              [(D_MODEL, F32)], [D_MODEL] * 3)
    return ds_in, accs


def _gate_bwd(name, ds_out, o, mg, k, coef, n_lat):
    t_rows = o.shape[0]

    def fn(rv, vv):
        gate = vv[0][3 * k + 2:3 * k + 3]
        d = coef * rv[0]
        return [d * gate], [_colsum(d * rv[1])]

    (do,), (dgate,) = _rowwise(name, fn, t_rows, _row_tm(t_rows, n_lat), n_lat,
                               [(ds_out, D_MODEL, 0), (o, D_MODEL, 0)], [mg], [(D_MODEL, BF16)], [D_MODEL])
    return do, dgate


def _rmsnorm_fwd(name, x, width, colblk, gain, t_rows):
    def fn(rv, vv):
        n, _ = _rms(rv[0])
        return [n * vv[0]], []

    (y,), _ = _rowwise(name, fn, t_rows, _tile(t_rows, 256, 16), t_rows, [(x, width, colblk)],
                       [gain.reshape(1, 1, width)], [(width, BF16)], [])
    return y


def _rmsnorm_bwd(name, x, width, colblk, dy, gain, t_rows):
    def fn(rv, vv):
        n, r = _rms(rv[0])
        return [_rms_bwd(n, r, rv[1] * vv[0])], [_colsum(rv[1] * n)]

    (dx,), (dgain,) = _rowwise(name, fn, t_rows, _tile(t_rows, 256, 16), t_rows,
                               [(x, width, colblk), (dy, width, 0)], [gain.reshape(1, 1, width)],
                               [(width, F32)], [width])
    return dx, dgain


def _final_loss(name, h, target, gain):
    t_rows = h.shape[0]
    inv_d = 1.0 / D_MODEL

    def fn(rv, vv):
        g = vv[0]
        n, r = _rms(rv[0])
        e = n * g - rv[1]
        dy = e * inv_d
        return [_rms_bwd(n, r, dy * g)], [_colsum(e * e), _colsum(dy * n)]

    (dh,), (sq, dgain) = _rowwise(name, fn, t_rows, _tile(t_rows, 256, 16), t_rows,
                                  [(h, D_MODEL, 0), (target, D_MODEL, 0)], [gain.reshape(1, 1, D_MODEL)],
                                  [(D_MODEL, F32)], [D_MODEL, D_MODEL])
    return dh, sq, dgain


def _rope(name, z, width, colblk, cos, sin, perm, backward, out_dtype):
    t_rows = cos.shape[0]

    def body(z_ref, c_ref, s_ref, p_ref, o_ref):
        zz = z_ref[...]
        pre = zz * s_ref[...] if backward else zz
        hi = pre.astype(BF16)
        lo = (pre - hi.astype(F32)).astype(BF16)
        rot = _dot(hi, p_ref[...], NN) + _dot(lo, p_ref[...], NN)
        if not backward:
            rot = rot * s_ref[...]
        o_ref[...] = (zz * c_ref[...] + rot).astype(o_ref.dtype)

    tm = _tile(t_rows, 256, 16)
    t_spec = pl.BlockSpec((tm, width), lambda i: (i, 0))
    return pl.pallas_call(
        body, name=name, grid=(t_rows // tm,),
        in_specs=[pl.BlockSpec((tm, width), lambda i: (i, colblk)), t_spec, t_spec,
                  pl.BlockSpec((width, width), lambda i: (0, 0))],
        out_specs=t_spec, out_shape=SDS((t_rows, width), out_dtype), compiler_params=_cparams(),
    )(z, cos, sin, perm)


def _window_sum(x, w, transposed):
    n_rows = x.shape[0]
    zeros = jnp.zeros((POOL_PAD, x.shape[1]), F32)
    y = jnp.concatenate([zeros, x, zeros], axis=0)
    total = n_rows + 2 * POOL_PAD
    if transposed:
        y = y + pltpu.roll(y, total - 1, 0)
    else:
        y = y + pltpu.roll(y, 1, 0)
    step = 1
    while 2 * step < w:
        y = pltpu.roll(y, step, 0) + pltpu.roll(y, total - step, 0)
        step *= 2
    return y[POOL_PAD:POOL_PAD + n_rows]


def _window_count(n_rows, w):
    t = lax.broadcasted_iota(jnp.int32, (n_rows, 1), 0)
    lo = jnp.maximum(t - w // 2, 0)
    hi = jnp.minimum(t + (w - w // 2 - 1), n_rows - 1)
    return (hi - lo + 1).astype(F32)


def _pool_fwd(name, proj, n_rows, w_grp, scale):
    def body(x_ref, w_ref, sc_ref, y_ref, p_ref):
        for g, w in enumerate(POOL_WINDOWS):
            cols = slice(g * POOL_GROUP_DIM, (g + 1) * POOL_GROUP_DIM)
            x = x_ref[:, cols]
            p = _window_sum(x, w, False) * (1.0 / _window_count(n_rows, w)) - x
            pb = p.astype(BF16)
            p_ref[:, cols] = pb
            y_ref[:, cols] = (_dot(pb, w_ref[g], NN) * sc_ref[:, cols]).astype(BF16)

    blk = pl.BlockSpec((n_rows, POOL_DIM), lambda i: (0, 0))
    return pl.pallas_call(
        body, name=name, grid=(1,),
        in_specs=[blk, pl.BlockSpec(w_grp.shape, lambda i: (0, 0, 0)), pl.BlockSpec((1, POOL_DIM), lambda i: (0, 0))],
        out_specs=[blk, blk], out_shape=[SDS((n_rows, POOL_DIM), BF16)] * 2, compiler_params=_cparams(),
    )(proj, w_grp, scale)


def _pool_bwd(name, dcat, n_rows, p, w_grp, scale):
    def body(dy_ref, p_ref, w_ref, sc_ref, dx_ref, dw_ref, dsc_ref):
        for g, w in enumerate(POOL_WINDOWS):
            cols = slice(g * POOL_GROUP_DIM, (g + 1) * POOL_GROUP_DIM)
            dy = dy_ref[:, cols]
            pb = p_ref[:, cols]
            pw = _dot(pb, w_ref[g], NN)
            dsc_ref[:, cols] = _colsum(dy * pw)
            dpw = (dy * sc_ref[:, cols]).astype(BF16)
            dw_ref[g] = _dot(pb, dpw, TN)
            dp = _dot(dpw, w_ref[g], NT)
            dx_ref[:, cols] = _window_sum(dp * (1.0 / _window_count(n_rows, w)), w, True) - dp

    blk = pl.BlockSpec((n_rows, POOL_DIM), lambda i: (0, 0))
    w_spec = pl.BlockSpec(w_grp.shape, lambda i: (0, 0, 0))
    v_spec = pl.BlockSpec((1, POOL_DIM), lambda i: (0, 0))
    return pl.pallas_call(
        body, name=name, grid=(1,), in_specs=[blk, blk, w_spec, v_spec], out_specs=[blk, w_spec, v_spec],
        out_shape=[SDS((n_rows, POOL_DIM), F32), SDS(w_grp.shape, F32), SDS((1, POOL_DIM), F32)],
        compiler_params=_cparams(),
    )(dcat, p, w_grp, scale)


def _attn_fwd(name, q, k, v):
    h, n_q, _ = q.shape
    n_k = k.shape[1]
    tq = _tile(n_q, 256, 16)

    def body(q_ref, k_ref, v_ref, o_ref, lse_ref):
        s = _dot(q_ref[...], k_ref[...], NT) * ATTN_SCALE
        m = jnp.max(s, axis=-1, keepdims=True)
        e = jnp.exp(s - m)
        l = jnp.sum(e, axis=-1, keepdims=True)
        p = (e * (1.0 / l)).astype(BF16)
        o_ref[...] = _dot(p, v_ref[...], NN).astype(BF16)
        lse_ref[...] = m + jnp.log(l)

    return pl.pallas_call(
        body, name=name, grid=(h, n_q // tq),
        in_specs=[pl.BlockSpec((None, tq, HEAD_PAD), lambda hh, i: (hh, i, 0)),
                  pl.BlockSpec((None, n_k, HEAD_PAD), lambda hh, i: (hh, 0, 0)),
                  pl.BlockSpec((None, n_k, V_HEAD), lambda hh, i: (hh, 0, 0))],
        out_specs=[pl.BlockSpec((None, tq, V_HEAD), lambda hh, i: (hh, i, 0)),
                   pl.BlockSpec((None, tq, 1), lambda hh, i: (hh, i, 0))],
        out_shape=[SDS((h, n_q, V_HEAD), BF16), SDS((h, n_q, 1), F32)], compiler_params=_cparams(),
    )(q, k, v)


def _attn_bwd(name, q, k, v, o, lse, do):
    h, n_q, _ = q.shape
    n_k = k.shape[1]
    tq = _tile(n_q, 256, 16)

    def body(q_ref, k_ref, v_ref, o_ref, lse_ref, do_ref, dq_ref, dk_ref, dv_ref, dks_ref):
        hh, i = pl.program_id(0), pl.program_id(1)
        qq, kk, dd = q_ref[...], k_ref[...], do_ref[...]
        s = _dot(qq, kk, NT) * ATTN_SCALE
        p = jnp.exp(s - lse_ref[...])
        dp = _dot(dd, v_ref[...], NT)
        delta = jnp.sum(dd.astype(F32) * o_ref[...].astype(F32), axis=-1, keepdims=True)
        ds = (p * (dp - delta) * ATTN_SCALE).astype(BF16)
        dq_ref[...] = _dot(ds, kk, NN)
        dk = _dot(ds, qq, TN)
        dv = _dot(p.astype(BF16), dd, TN)

        @pl.when(i == 0)
        def _():
            dk_ref[...] = dk
            dv_ref[...] = dv

        @pl.when(i > 0)
        def _():
            dk_ref[...] += dk
            dv_ref[...] += dv

        @pl.when((i == 0) & (hh == 0))
        def _():
            dks_ref[...] = dk

        @pl.when((i > 0) | (hh > 0))
        def _():
            dks_ref[...] += dk

    q_spec = pl.BlockSpec((None, tq, HEAD_PAD), lambda hh, i: (hh, i, 0))
    k_spec = pl.BlockSpec((None, n_k, HEAD_PAD), lambda hh, i: (hh, 0, 0))
    v_spec = pl.BlockSpec((None, n_k, V_HEAD), lambda hh, i: (hh, 0, 0))
    o_spec = pl.BlockSpec((None, tq, V_HEAD), lambda hh, i: (hh, i, 0))
    return pl.pallas_call(
        body, name=name, grid=(h, n_q // tq),
        in_specs=[q_spec, k_spec, v_spec, o_spec, pl.BlockSpec((None, tq, 1), lambda hh, i: (hh, i, 0)), o_spec],
        out_specs=[q_spec, k_spec, v_spec, pl.BlockSpec((n_k, HEAD_PAD), lambda hh, i: (0, 0))],
        out_shape=[SDS((h, n_q, HEAD_PAD), F32), SDS((h, n_k, HEAD_PAD), F32), SDS((h, n_k, V_HEAD), F32),
                   SDS((n_k, HEAD_PAD), F32)],
        compiler_params=_cparams(),
    )(q, k, v, o, lse, do)


CONV_COLS = 256


def _shift_rows(x, d):
    n_rows = x.shape[0]
    t = lax.broadcasted_iota(jnp.int32, (n_rows, 1), 0)
    if d > 0:
        return jnp.where(t >= d, pltpu.roll(x, d, 0), 0.0)
    return jnp.where(t < n_rows + d, pltpu.roll(x, n_rows + d, 0), 0.0)


def _conv_fwd(name, z3, conv_w):
    n_rows = z3.shape[0]
    nb = D_MODEL // CONV_COLS

    def body(b_ref, c_ref, v_ref, w_ref, y_ref):
        z = c_ref[...] * v_ref[...]
        zc = w_ref[0:1, :] * _shift_rows(z, 1) + w_ref[1:2, :] * z + w_ref[2:3, :] * _shift_rows(z, -1)
        y_ref[...] = (b_ref[...] * zc).astype(BF16)

    def part(k):
        return pl.BlockSpec((n_rows, CONV_COLS), lambda j: (0, k * nb + j))

    return pl.pallas_call(
        body, name=name, grid=(nb,),
        in_specs=[part(0), part(1), part(2), pl.BlockSpec((3, CONV_COLS), lambda j: (0, j))],
        out_specs=pl.BlockSpec((n_rows, CONV_COLS), lambda j: (0, j)),
        out_shape=SDS((n_rows, D_MODEL), BF16), compiler_params=_cparams(),
    )(z3, z3, z3, conv_w)


def _conv_bwd(name, dy, z3, conv_w):
    n_rows = z3.shape[0]
    nb = D_MODEL // CONV_COLS

    def body(dy_ref, b_ref, c_ref, v_ref, w_ref, db_ref, dc_ref, dv_ref, dw_ref):
        c, v, d_y = c_ref[...], v_ref[...], dy_ref[...]
        z = c * v
        z_dn, z_up = _shift_rows(z, 1), _shift_rows(z, -1)
        zc = w_ref[0:1, :] * z_dn + w_ref[1:2, :] * z + w_ref[2:3, :] * z_up
        db_ref[...] = (d_y * zc).astype(BF16)
        dzc = d_y * b_ref[...]
        dz = w_ref[0:1, :] * _shift_rows(dzc, -1) + w_ref[1:2, :] * dzc + w_ref[2:3, :] * _shift_rows(dzc, 1)
        dc_ref[...] = (dz * v).astype(BF16)
        dv_ref[...] = (dz * c).astype(BF16)
        dw_ref[0:1, :] = _colsum(dzc * z_dn)
        dw_ref[1:2, :] = _colsum(dzc * z)
        dw_ref[2:3, :] = _colsum(dzc * z_up)

    def part(k):
        return pl.BlockSpec((n_rows, CONV_COLS), lambda j: (0, k * nb + j))

    col = pl.BlockSpec((n_rows, CONV_COLS), lambda j: (0, j))
    w_spec = pl.BlockSpec((3, CONV_COLS), lambda j: (0, j))
    return pl.pallas_call(
        body, name=name, grid=(nb,), in_specs=[col, part(0), part(1), part(2), w_spec],
        out_specs=[col, col, col, w_spec],
        out_shape=[SDS((n_rows, D_MODEL), BF16)] * 3 + [SDS((3, D_MODEL), F32)], compiler_params=_cparams(),
    )(dy, z3, z3, z3, conv_w)


def _silu_rows(name, x):
    def body(x_ref, s_ref, d_ref):
        xx = x_ref[...]
        sg = jax.nn.sigmoid(xx)
        s_ref[...] = (xx * sg).astype(BF16)
        d_ref[...] = sg * (1.0 + xx * (1.0 - sg))

    return pl.pallas_call(body, name=name, out_shape=[SDS(x.shape, BF16), SDS(x.shape, F32)])(x)


def _sum_rows(name, x, scale=None):
    r, n = x.shape
    tn = _tile(n, 8192, 128)

    def body(*refs):
        acc = jnp.sum(refs[0][...].astype(F32), axis=0, keepdims=True)
        if scale is not None:
            acc = acc * refs[1][...]
        refs[-1][...] = acc

    in_specs = [pl.BlockSpec((r, tn), lambda j: (0, j))]
    args = [x]
    if scale is not None:
        in_specs.append(pl.BlockSpec((1, tn), lambda j: (0, j)))
        args.append(scale)
    return pl.pallas_call(body, name=name, grid=(n // tn,), in_specs=in_specs,
                          out_specs=pl.BlockSpec((1, tn), lambda j: (0, j)), out_shape=SDS((1, n), F32))(*args)


def _sum_slots(name, x):
    n_slots, r, c = x.shape
    tr = _tile(r, 432, 16)

    def body(x_ref, o_ref):
        acc = x_ref[0].astype(F32)
        for sl in range(1, n_slots):
            acc = acc + x_ref[sl].astype(F32)
        o_ref[...] = acc

    return pl.pallas_call(body, name=name, grid=(r // tr,),
                          in_specs=[pl.BlockSpec((n_slots, tr, c), lambda i: (0, i, 0))],
                          out_specs=pl.BlockSpec((tr, c), lambda i: (i, 0)), out_shape=SDS((r, c), F32),
                          compiler_params=_cparams())(x)


def _adamw(name, w, g, m, v):
    shape = w.shape
    cols = shape[-1]
    rows = w.size // cols
    tr = _tile(rows, 512, 8)
    bc1 = 1.0 - ADAM_B1 ** ADAM_STEP
    bc2 = 1.0 - ADAM_B2 ** ADAM_STEP

    def body(w_ref, g_ref, m_ref, v_ref, d_ref, nm_ref, nv_ref):
        gg = g_ref[...]
        nm = ADAM_B1 * m_ref[...] + (1.0 - ADAM_B1) * gg
        nv = ADAM_B2 * v_ref[...] + (1.0 - ADAM_B2) * (gg * gg)
        nm_ref[...] = nm
        nv_ref[...] = nv
        d_ref[...] = -ADAM_LR * ((nm / bc1) / (jnp.sqrt(nv / bc2) + ADAM_EPS) + ADAM_WD * w_ref[...])

    spec = pl.BlockSpec((tr, cols), lambda i: (i, 0))
    outs = pl.pallas_call(body, name=name, grid=(rows // tr,), in_specs=[spec] * 4, out_specs=[spec] * 3,
                          out_shape=[SDS((rows, cols), F32)] * 3, compiler_params=_cparams())(
        w.reshape(rows, cols), g.reshape(rows, cols), m.reshape(rows, cols), v.reshape(rows, cols))
    return tuple(t.reshape(shape) for t in outs)


def _exchange(name, x, scatter, after=None):
    blk = x.shape[1:] if scatter else x.shape
    extra = [] if after is None else [after]

    def body(x_ref, *rest):
        out_ref, send_sems, recv_sems, local_sem = rest[len(extra):]
        mx, my, mc = lax.axis_index("x"), lax.axis_index("y"), lax.axis_index("c")
        me = 4 * mx + 2 * my + mc
        own = pltpu.make_async_copy(x_ref.at[me] if scatter else x_ref, out_ref.at[me], local_sem)
        own.start()
        copies = []
        for kk in range(1, N_DEV):
            px = jnp.bitwise_xor(mx, (kk >> 2) & 1)
            py = jnp.bitwise_xor(my, (kk >> 1) & 1)
            pc = jnp.bitwise_xor(mc, kk & 1)
            peer = 4 * px + 2 * py + pc
            send = pltpu.make_async_remote_copy(
                src_ref=x_ref.at[peer] if scatter else x_ref, dst_ref=out_ref.at[me],
                send_sem=send_sems.at[kk - 1], recv_sem=recv_sems.at[kk - 1],
                device_id=(px, py, pc), device_id_type=MESH)
            send.start()
            arrival = pltpu.make_async_remote_copy(
                src_ref=x_ref.at[peer] if scatter else x_ref, dst_ref=out_ref.at[peer],
                send_sem=send_sems.at[kk - 1], recv_sem=recv_sems.at[kk - 1],
                device_id=(px, py, pc), device_id_type=MESH)
            copies.append((send, arrival))
        for send, arrival in copies:
            arrival.wait_recv()
            send.wait_send()
        own.wait()

    return pl.pallas_call(
        body, name=name, out_shape=SDS((N_DEV,) + tuple(blk), x.dtype),
        in_specs=[pl.BlockSpec(memory_space=pl.ANY)] * (1 + len(extra)), out_specs=pl.BlockSpec(memory_space=pl.ANY),
        scratch_shapes=[pltpu.SemaphoreType.DMA((N_DEV - 1,)), pltpu.SemaphoreType.DMA((N_DEV - 1,)),
                        pltpu.SemaphoreType.DMA],
    )(x, *extra)


def _rope_perm(pre, reps, post):
    half = QK_ROPE // 4
    width = reps * (pre + QK_ROPE) + post
    p = np.zeros((width, width), np.float32)
    for rep in range(reps):
        s0 = rep * (pre + QK_ROPE) + pre
        for base in (s0, s0 + 2 * half):
            for i in range(half):
                p[base + half + i, base + i] = -1.0
                p[base + i, base + half + i] = 1.0
    return p


def _rope_tables(n_lat, t_rows, pre, reps, post):
    half = QK_ROPE // 4
    pos = jnp.arange(n_lat)
    freqs = jnp.power(ROPE_THETA, -jnp.arange(0, 2 * half, 2, dtype=F32) / (2 * half))
    ang_r = (pos // GRID_W).astype(F32)[:, None] * freqs
    ang_c = (pos % GRID_W).astype(F32)[:, None] * freqs
    ang = jnp.concatenate([ang_r, ang_r, ang_c, ang_c], axis=-1)

    def table(fn, plain):
        slot = jnp.concatenate([jnp.full((n_lat, pre), plain, F32), fn(ang)], axis=-1)
        t = jnp.concatenate([jnp.tile(slot, (1, reps)), jnp.full((n_lat, post), plain, F32)], axis=-1)
        return jnp.concatenate([t, jnp.full((t_rows - n_lat, t.shape[1]), plain, F32)], axis=0)

    return table(jnp.cos, 1.0), table(jnp.sin, 0.0)


def _ffn_half_fwd(tag, s, mg, k, feed, i, coef, n_lat):
    u = _adaln_fwd(f"{tag}_adaln", s, mg, k, n_lat)
    wg_t, wu_t = feed.weights(f"{tag}_up", [f"gate_t{i}", f"up_t{i}"], u)
    a, b, hid = _ffn_up(f"{tag}_up", u, wg_t, wu_t)
    (wd,) = feed.weights(f"{tag}_down", [f"down{i}"], hid)
    s_out, o = _mm_resid(f"{tag}_down", hid, wd, s, mg, k, coef, n_lat)
    return s_out, (s, u, a, b, hid, o, wg_t, wu_t, wd)


def _ffn_half_bwd(tag, ds_out, saved, mg, k, feed, i, coef, n_lat):
    s, u, a, b, hid, o, wg_t, wu_t, wd = saved
    do, dgate = _gate_bwd(f"{tag}_dgate", ds_out, o, mg, k, coef, n_lat)
    da, db = _ffn_dact(f"{tag}_dact", do, wd, a, b)
    dwd = _mm(f"{tag}_dwd", [(hid, do)], "tn", BF16)
    dwg_t = _mm(f"{tag}_dwg", [(da, u)], "tn", BF16)
    dwu_t = _mm(f"{tag}_dwu", [(db, u)], "tn", BF16)
    token = feed.grads(tag, {f"down{i}": dwd, f"gate_t{i}": dwg_t, f"up_t{i}": dwu_t})
    du = _mm(f"{tag}_du", [(da, wg_t), (db, wu_t)], "nn", F32, 384, 512, bias=_after(token))
    ds_in, (dshift, dscale, dgain) = _adaln_bwd(f"{tag}_dadaln", s, du, ds_out, mg, k, n_lat)
    return ds_in, dict(shift=dshift, scale=dscale, gate=dgate, gain=dgain)


def _after(token):
    return jnp.zeros((1, D_MODEL), F32) + token


def _mod_grad(parts, n_groups):
    rows = []
    zero = jnp.zeros((n_groups, 1, D_MODEL), F32)
    for k in range(3):
        for nm in ("shift", "scale", "gate"):
            t = parts[k].get(nm, zero)
            if t.shape[0] < n_groups:
                t = jnp.concatenate([t, jnp.zeros((n_groups - t.shape[0], 1, D_MODEL), F32)], axis=0)
            rows.append(t)
    return jnp.concatenate(rows, axis=1).reshape(n_groups, N_MOD * D_MODEL)


def _local_step(x, ctx, target, mod_h, mod_g, norm_g, feed, pool_w, pool_scale, q_norm_g, kv_norm_g, conv_w,
                final_norm_g):
    n_lat, n_ctx = x.shape[0], ctx.shape[0]
    t_all = n_lat + n_ctx
    mg0 = jnp.stack([jnp.concatenate([mod_h[0], norm_g[0]], axis=0), jnp.concatenate([mod_g, norm_g[0]], axis=0)])
    mg1 = jnp.concatenate([mod_h[1], norm_g[1]], axis=0)[None]

    s0 = jnp.concatenate([x, ctx], axis=0) + feed.start_token()
    s1, sv_f00 = _ffn_half_fwd("l0f0", s0, mg0, 0, feed, 0, 0.5, n_lat)

    ua = _adaln_fwd("l0m_adaln", s1, mg0, 1, n_lat)
    w_in, w_uq, w_ukv_t, w_ab_out = feed.weights("l0m", ["in_t", "uq", "ukv_t", "ab_out"], ua)
    kv_rows = KV_RANK + QK_ROPE
    w_in_t = jnp.concatenate([
        w_in[:POOL_DIM], jnp.zeros((PA_CQ - POOL_DIM, D_MODEL), BF16), w_in[POOL_DIM:POOL_DIM + Q_RANK],
        w_in[POOL_DIM + Q_RANK:], jnp.zeros((PA_KV_W - kv_rows, D_MODEL), BF16)], axis=0)
    proj = _mm("l0m_proj", [(ua, w_in_t)], "nt", F32, 768, 384)
    pool_y, pool_p = _pool_fwd("l0m_pool", proj, n_lat, pool_w.astype(BF16), pool_scale)
    nq = _rmsnorm_fwd("l0m_qnorm", proj, Q_RANK, PA_CQ // Q_RANK, q_norm_g, n_lat)
    q_lin = _mm("l0m_q", [(nq, w_uq)], "nn", F32, 512, 768)
    cos_q, sin_q = _rope_tables(n_lat, n_lat, QK_NOPE, HEADS, 0)
    perm_q = _rope_perm(QK_NOPE, HEADS, 0)
    q_rot = _rope("l0m_qrope", q_lin, Q_RANK, 0, cos_q, sin_q, jnp.asarray(perm_q, BF16), False, BF16)
    cos_k, sin_k = _rope_tables(n_lat, t_all, KV_RANK, 1, PA_KV_W - kv_rows)
    perm_k = _rope_perm(KV_RANK, 1, PA_KV_W - kv_rows)
    kvr = _rope("l0m_krope", proj, PA_KV_W, PA_KV // PA_KV_W, cos_k, sin_k, jnp.asarray(perm_k, BF16), False, F32)
    nkv = _rmsnorm_fwd("l0m_kvnorm", kvr, KV_RANK, 0, kv_norm_g, t_all)
    kv = _mm("l0m_kv", [(nkv, w_ukv_t)], "nt", BF16, 768, 512)
    qh = jnp.pad(q_rot.reshape(n_lat, HEADS, QK_HEAD), ((0, 0), (0, 0), (0, HEAD_PAD - QK_HEAD))).transpose(1, 0, 2)
    kvh = kv.reshape(t_all, HEADS, QK_NOPE + V_HEAD)
    k_rope = jnp.broadcast_to(kvr[:, None, KV_RANK:KV_RANK + QK_ROPE].astype(BF16), (t_all, HEADS, QK_ROPE))
    kh = jnp.concatenate([kvh[:, :, :QK_NOPE], k_rope, jnp.zeros((t_all, HEADS, HEAD_PAD - QK_HEAD), BF16)],
                         axis=-1).transpose(1, 0, 2)
    vh = kvh[:, :, QK_NOPE:].transpose(1, 0, 2)
    oh, lse = _attn_fwd("l0m_attn", qh, kh, vh)
    cat = jnp.concatenate([pool_y, oh.transpose(1, 0, 2).reshape(n_lat, HEADS * V_HEAD)], axis=-1)
    h1 = s1[:n_lat]
    h2, mix_o = _mm_resid("l0m_out", cat, w_ab_out, h1, mg0[:1], 1, 1.0, n_lat)

    h3, sv_f01 = _ffn_half_fwd("l0f1", h2, mg0[:1], 2, feed, 1, 0.5, n_lat)

    h4, sv_f10 = _ffn_half_fwd("l1f0", h3, mg1, 0, feed, 2, 0.5, n_lat)
    uc = _adaln_fwd("l1m_adaln", h4, mg1, 1, n_lat)
    w_cin_t, w_c_out = feed.weights("l1m", ["cin_t", "c_out"], uc)
    z3 = _mm("l1m_in", [(uc, w_cin_t)], "nt", F32)
    yc = _conv_fwd("l1m_conv", z3, conv_w)
    h5, conv_o = _mm_resid("l1m_out", yc, w_c_out, h4, mg1, 1, 1.0, n_lat)
    h6, sv_f11 = _ffn_half_fwd("l1f1", h5, mg1, 2, feed, 3, 0.5, n_lat)

    dh6, sq_cols, d_final_g = _final_loss("loss_head", h6, target, final_norm_g)
    g = {}
    dh5, g["f11"] = _ffn_half_bwd("l1f1", dh6, sv_f11, mg1, 2, feed, 3, 0.5, n_lat)

    do_c, dgate_c = _gate_bwd("l1m_dgate", dh5, conv_o, mg1, 1, 1.0, n_lat)
    dyc = _mm("l1m_dy", [(do_c, w_c_out)], "nt", F32)
    d_c_out = _mm("l1m_dwout", [(yc, do_c)], "tn", BF16)
    db_, dc_, dv_, d_conv_w = _conv_bwd("l1m_dconv", dyc, z3, conv_w)
    dz3 = jnp.concatenate([db_, dc_, dv_], axis=-1)
    d_cin_t = _mm("l1m_dwin", [(dz3, uc)], "tn", BF16)
    token = feed.grads("l1m", {"c_out": d_c_out, "cin_t": d_cin_t})
    duc = _mm("l1m_du", [(dz3, w_cin_t)], "nn", F32, bias=_after(token))
    dh4, (dsh_c, dsc_c, dgn_c) = _adaln_bwd("l1m_dadaln", h4, duc, dh5, mg1, 1, n_lat)
    dh3, g["f10"] = _ffn_half_bwd("l1f0", dh4, sv_f10, mg1, 0, feed, 2, 0.5, n_lat)

    dh2, g["f01"] = _ffn_half_bwd("l0f1", dh3, sv_f01, mg0[:1], 2, feed, 1, 0.5, n_lat)

    do_a, dgate_a = _gate_bwd("l0m_dgate", dh2, mix_o, mg0[:1], 1, 1.0, n_lat)
    dcat = _mm("l0m_dcat", [(do_a, w_ab_out)], "nt", F32)
    d_ab_out = _mm("l0m_dwout", [(cat, do_a)], "tn", BF16)
    d_pool_x, d_pool_w, d_pool_scale = _pool_bwd("l0m_dpool", dcat, n_lat, pool_p, pool_w.astype(BF16), pool_scale)
    doh = dcat[:, POOL_DIM:].reshape(n_lat, HEADS, V_HEAD).transpose(1, 0, 2).astype(BF16)
    dqh, dkh, dvh, dk_sum = _attn_bwd("l0m_dattn", qh, kh, vh, oh, lse, doh)
    dq_rot = dqh[:, :, :QK_HEAD].transpose(1, 0, 2).reshape(n_lat, Q_RANK)
    dq_lin = _rope("l0m_dqrope", dq_rot, Q_RANK, 0, cos_q, sin_q, jnp.asarray(perm_q.T, BF16), True, BF16)
    d_uq = _mm("l0m_dwuq", [(nq, dq_lin)], "tn", BF16, 768, 768)
    dnq = _mm("l0m_dnq", [(dq_lin, w_uq)], "nt", F32, 512, 768)
    dcq, d_q_norm_g = _rmsnorm_bwd("l0m_dqnorm", proj, Q_RANK, PA_CQ // Q_RANK, dnq, q_norm_g, n_lat)
    dkv = jnp.concatenate([dkh[:, :, :QK_NOPE], dvh], axis=-1).transpose(1, 0, 2).reshape(t_all, HEADS * HEAD_PAD)
    dkv = dkv.astype(BF16)
    dnkv = _mm("l0m_dnkv", [(dkv, w_ukv_t)], "nn", F32, 768, 256)
    d_ukv_t = _mm("l0m_dwukv", [(dkv, nkv)], "tn", BF16, 512, 256)
    dckv, d_kv_norm_g = _rmsnorm_bwd("l0m_dkvnorm", kvr, KV_RANK, 0, dnkv, kv_norm_g, t_all)
    dkvr = jnp.concatenate([dckv, dk_sum[:, QK_NOPE:QK_HEAD],
                            jnp.zeros((t_all, PA_KV_W - KV_RANK - QK_ROPE), F32)], axis=-1)
    dpb = _rope("l0m_dkrope", dkvr, PA_KV_W, 0, cos_k, sin_k, jnp.asarray(perm_k.T, BF16), True, F32)
    dproj_lat = jnp.concatenate([d_pool_x, jnp.zeros((n_lat, PA_CQ - POOL_DIM), F32), dcq, dpb[:n_lat]], axis=-1)
    dproj_ctx = jnp.concatenate([jnp.zeros((n_ctx, PA_KV), F32), dpb[n_lat:]], axis=-1)
    dproj = jnp.concatenate([dproj_lat, dproj_ctx], axis=0).astype(BF16)
    d_in_pad = _mm("l0m_dwin", [(dproj, ua)], "tn", BF16, 640, 512)
    d_in_t = jnp.concatenate([d_in_pad[:POOL_DIM], d_in_pad[PA_CQ:PA_CQ + Q_RANK],
                              d_in_pad[PA_KV:PA_KV + kv_rows]], axis=0)
    token = feed.grads("l0m", {"ab_out": d_ab_out, "uq": d_uq, "ukv_t": d_ukv_t, "in_t": d_in_t})
    dua = _mm("l0m_du", [(dproj, w_in_t)], "nn", F32, 768, 512, bias=_after(token))
    dh2_all = jnp.concatenate([dh2, jnp.zeros((n_ctx, D_MODEL), F32)], axis=0)
    ds1, (dsh_a, dsc_a, dgn_a) = _adaln_bwd("l0m_dadaln", s1, dua, dh2_all, mg0, 1, n_lat)
    ds0, g["f00"] = _ffn_half_bwd("l0f0", ds1, sv_f00, mg0, 0, feed, 0, 0.5, n_lat)

    dmod0 = _mod_grad([g["f00"], dict(shift=dsh_a, scale=dsc_a, gate=dgate_a), g["f01"]], 2)
    dmod1 = _mod_grad([g["f10"], dict(shift=dsh_c, scale=dsc_c, gate=dgate_c), g["f11"]], 1)
    d_norm_g = jnp.stack([
        jnp.concatenate([jnp.sum(g["f00"]["gain"], axis=0), jnp.sum(dgn_a, axis=0), g["f01"]["gain"][0]], axis=0),
        jnp.concatenate([g["f10"]["gain"][0], dgn_c[0], g["f11"]["gain"][0]], axis=0)])
    grads = dict(
        pool_w=d_pool_w, pool_scale=d_pool_scale, q_norm_g=d_q_norm_g[0], kv_norm_g=d_kv_norm_g[0],
        conv_w=d_conv_w, final_norm_g=d_final_g[0], norm_g=d_norm_g,
        mod_h=jnp.stack([dmod0[0], dmod1[0]]), mod_g=dmod0[1])
    return sq_cols, ds0, grads


HBM_SPEC = pl.BlockSpec(memory_space=pltpu.HBM)
SEM_SPEC = pl.BlockSpec(memory_space=pltpu.SEMAPHORE)
ANY_SPEC = pl.BlockSpec(memory_space=pl.ANY)
SIDE_EFFECT = pltpu.SideEffectType.DATAFLOW_SIDE_EFFECTING
N_PEERS = N_DEV - 1


def _mesh_place():
    mx, my, mc = lax.axis_index("x"), lax.axis_index("y"), lax.axis_index("c")
    return mx, my, mc, 4 * mx + 2 * my + mc


def _peer(place, kk):
    mx, my, mc, _ = place
    px = jnp.bitwise_xor(mx, (kk >> 2) & 1)
    py = jnp.bitwise_xor(my, (kk >> 1) & 1)
    pc = jnp.bitwise_xor(mc, kk & 1)
    return (px, py, pc), 4 * px + 2 * py + pc


def _hbm(a):
    return pltpu.with_memory_space_constraint(a, pltpu.HBM)


def _landing(block, me):
    zone = lax.empty((N_DEV,) + block.shape, block.dtype)
    return lax.dynamic_update_slice(zone, block[None], (me,) + (0,) * block.ndim)


ALL_PEERS = tuple(range(1, N_DEV))
SIBLING = 1
CHIP_PEERS = (2, 4, 6)
RELAYED = (3, 5, 7)


def _exchange_start(name, srcs, lands, scatter, after, peers=ALL_PEERS):
    n = len(srcs)

    def body(*refs):
        src, land = refs[:n], refs[n:2 * n]
        send_sems, recv_sems, token = refs[2 * n + 1], refs[2 * n + 2], refs[-1]
        place = _mesh_place()
        for a in range(n):
            for kk in peers:
                dev, peer = _peer(place, kk)
                pltpu.make_async_remote_copy(
                    src_ref=src[a].at[peer] if scatter else src[a], dst_ref=land[a].at[place[3]],
                    send_sem=send_sems.at[a * N_PEERS + kk - 1], recv_sem=recv_sems.at[a * N_PEERS + kk - 1],
                    device_id=dev, device_id_type=MESH).start()
        token[...] = jnp.zeros_like(token)

    thru = [pltpu.HBM(t.shape, t.dtype) for t in (*srcs, *lands)]
    res = pl.pallas_call(
        body, name=name,
        out_shape=(pltpu.SemaphoreType.DMA((n * N_PEERS,)), pltpu.SemaphoreType.DMA((n * N_PEERS,)), *thru,
                   SDS((8, 128), F32)),
        in_specs=[HBM_SPEC] * (2 * n) + [ANY_SPEC],
        out_specs=(SEM_SPEC, SEM_SPEC, *([HBM_SPEC] * (2 * n)), pl.BlockSpec(memory_space=pltpu.VMEM)),
        input_output_aliases={i: 2 + i for i in range(2 * n)},
        compiler_params=pltpu.CompilerParams(has_side_effects=SIDE_EFFECT),
    )(*[_hbm(s) for s in srcs], *[_hbm(t) for t in lands], after)
    return res[0], res[1], list(res[2:2 + n]), list(res[2 + n:2 + 2 * n]), res[-1]


def _exchange_wait(name, send_sems, recv_sems, srcs, lands, places, scatter, after):
    n = len(srcs)

    def body(*refs):
        src, land = refs[:n], refs[n:2 * n]
        send, recv = refs[2 * n], refs[2 * n + 1]
        place = _mesh_place()
        for a in range(n):
            for kk in range(1, N_DEV):
                dev, peer = _peer(place, kk)
                cp = pltpu.make_async_remote_copy(
                    src_ref=src[a].at[peer] if scatter else src[a], dst_ref=land[a].at[peer],
                    send_sem=send.at[places[a] * N_PEERS + kk - 1], recv_sem=recv.at[places[a] * N_PEERS + kk - 1],
                    device_id=dev, device_id_type=MESH)
                cp.wait_send()
                cp.wait_recv()

    thru = [pltpu.HBM(t.shape, t.dtype) for t in (*srcs, *lands)]
    res = pl.pallas_call(
        body, name=name, out_shape=tuple(thru),
        in_specs=[HBM_SPEC] * (2 * n) + [SEM_SPEC, SEM_SPEC] + [ANY_SPEC] * len(after),
        out_specs=tuple([HBM_SPEC] * (2 * n)), input_output_aliases={i: i for i in range(2 * n)},
        compiler_params=pltpu.CompilerParams(has_side_effects=SIDE_EFFECT),
    )(*srcs, *lands, send_sems, recv_sems, *after)
    return list(res[n:])


def _gather_relay(name, send1, recv1, lands, places, after):
    n = len(lands)

    def body(*refs):
        land, s1, r1 = refs[:n], refs[n], refs[n + 1]
        s2, r2 = refs[n + 3], refs[n + 4]
        place = _mesh_place()
        sibling = _peer(place, SIBLING)[0]
        for a in range(n):
            for j, kk in enumerate(CHIP_PEERS):
                dev, origin = _peer(place, kk)
                block = land[a].at[origin]
                pltpu.make_async_remote_copy(
                    src_ref=block, dst_ref=block, send_sem=s1.at[places[a] * N_PEERS + kk - 1],
                    recv_sem=r1.at[places[a] * N_PEERS + kk - 1], device_id=dev, device_id_type=MESH).wait_recv()
                pltpu.make_async_remote_copy(
                    src_ref=block, dst_ref=block, send_sem=s2.at[a * 3 + j], recv_sem=r2.at[a * 3 + j],
                    device_id=sibling, device_id_type=MESH).start()

    res = pl.pallas_call(
        body, name=name,
        out_shape=(pltpu.SemaphoreType.DMA((3 * n,)), pltpu.SemaphoreType.DMA((3 * n,)),
                   *[pltpu.HBM(t.shape, t.dtype) for t in lands]),
        in_specs=[HBM_SPEC] * n + [SEM_SPEC, SEM_SPEC, ANY_SPEC],
        out_specs=(SEM_SPEC, SEM_SPEC, *([HBM_SPEC] * n)),
        input_output_aliases={i: 2 + i for i in range(n)},
        compiler_params=pltpu.CompilerParams(has_side_effects=SIDE_EFFECT),
    )(*lands, send1, recv1, after)
    return res[0], res[1], list(res[2:])


def _gather_wait(name, send1, recv1, send2, recv2, srcs, lands, places, after):
    n = len(lands)

    def body(*refs):
        src, land = refs[:n], refs[n:2 * n]
        s1, r1, s2, r2 = refs[2 * n:2 * n + 4]
        place = _mesh_place()
        for a in range(n):
            for kk in (SIBLING,) + CHIP_PEERS:
                dev, origin = _peer(place, kk)
                first = pltpu.make_async_remote_copy(
                    src_ref=src[a], dst_ref=land[a].at[origin], send_sem=s1.at[places[a] * N_PEERS + kk - 1],
                    recv_sem=r1.at[places[a] * N_PEERS + kk - 1], device_id=dev, device_id_type=MESH)
                first.wait_send()
                if kk == SIBLING:
                    first.wait_recv()
            for j, kk in enumerate(CHIP_PEERS):
                dev, origin = _peer(place, kk + 1)
                relay = pltpu.make_async_remote_copy(
                    src_ref=src[a], dst_ref=land[a].at[origin], send_sem=s2.at[a * 3 + j], recv_sem=r2.at[a * 3 + j],
                    device_id=dev, device_id_type=MESH)
                relay.wait_send()
                relay.wait_recv()

    arrays = (*srcs, *lands)
    res = pl.pallas_call(
        body, name=name, out_shape=tuple(pltpu.HBM(t.shape, t.dtype) for t in arrays),
        in_specs=[HBM_SPEC] * (2 * n) + [SEM_SPEC] * 4 + [ANY_SPEC], out_specs=tuple([HBM_SPEC] * (2 * n)),
        input_output_aliases={i: i for i in range(2 * n)},
        compiler_params=pltpu.CompilerParams(has_side_effects=SIDE_EFFECT),
    )(*arrays, send1, recv1, send2, recv2, after)
    return list(res[n:])


class _Feed:
    def __init__(self, shards, groups, me):
        self.shards, self.groups, self.me, self.pos = shards, groups, me, 0
        self.sems, self.srcs, self.lands = {}, {}, {}
        self.pending = []

    def start(self, tag, names, after):
        srcs = [self.shards[nm] for nm in names]
        lands = [_landing(s, self.me) for s in srcs]
        send, recv, srcs, lands, self.token = _exchange_start(
            f"gather_start_{tag}", srcs, lands, False, after, (SIBLING,) + CHIP_PEERS)
        for i, nm in enumerate(names):
            self.sems[nm], self.srcs[nm], self.lands[nm] = (send, recv, i), srcs[i], lands[i]
        return self.token

    def relay_first(self, after):
        self.relay = self._relay("gather_relay_first", self.groups[0], after)

    def _relay(self, name, names, after):
        send, recv, _ = self.sems[names[0]]
        places = [self.sems[nm][2] for nm in names]
        send2, recv2, lands = _gather_relay(name, send, recv, [self.lands[nm] for nm in names], places, after)
        for nm, t in zip(names, lands):
            self.lands[nm] = t
        return send2, recv2

    def start_token(self):
        return self.token[0, 0]

    def weights(self, tag, names, after):
        assert names == self.groups[self.pos], (names, self.groups[self.pos])
        send2, recv2 = self.relay
        if self.pos + 1 < len(self.groups):
            nxt = self.groups[self.pos + 1]
            self.relay = self._relay(f"gather_relay_{tag}", nxt, after)
            after = self.lands[nxt[0]]
        send, recv, _ = self.sems[names[0]]
        got = _gather_wait(f"gather_wait_{tag}", send, recv, send2, recv2, [self.srcs[nm] for nm in names],
                           [self.lands[nm] for nm in names], [self.sems[nm][2] for nm in names], after)
        self.pos += 1
        return [t.reshape((N_DEV * t.shape[1],) + t.shape[2:]) for t in got]

    def grads(self, tag, full):
        names = list(full)
        srcs = [full[nm].reshape((N_DEV, full[nm].shape[0] // N_DEV) + full[nm].shape[1:]) for nm in names]
        lands = [_landing(lax.dynamic_index_in_dim(s, self.me, 0, keepdims=False), self.me) for s in srcs]
        send, recv, srcs, lands, token = _exchange_start(f"scatter_start_{tag}", srcs, lands, True, srcs[0])
        self.pending.append((tag, names, send, recv, srcs, lands))
        return token[0, 0]

    def collect(self, tags, after):
        out = {}
        for tag, names, send, recv, srcs, lands in self.pending:
            if tag not in tags:
                continue
            got = _exchange_wait(f"scatter_wait_{tag}", send, recv, srcs, lands, list(range(len(names))), True, after)
            for nm, slots in zip(names, got):
                out[nm] = _sum_slots(f"reduce_{nm}", slots)
        return out


def _adamw_math(w, gg, m, v):
    nm = ADAM_B1 * m + (1.0 - ADAM_B1) * gg
    nv = ADAM_B2 * v + (1.0 - ADAM_B2) * (gg * gg)
    bc1 = 1.0 - ADAM_B1 ** ADAM_STEP
    bc2 = 1.0 - ADAM_B2 ** ADAM_STEP
    return -ADAM_LR * ((nm / bc1) / (jnp.sqrt(nv / bc2) + ADAM_EPS) + ADAM_WD * w), nm, nv


def _adamw_part(name, i, w, g, m, v, prev):
    n_parts, rows, cols = w.shape
    tr = _tile(rows, 512, 8)
    if prev is None:
        prev = tuple(lax.empty(w.shape, F32) for _ in range(4))

    def body(w_ref, g_ref, m_ref, v_ref, *rest):
        go_ref, d_ref, nm_ref, nv_ref = rest[4:]
        gg = g_ref[...]
        d, nm, nv = _adamw_math(w_ref[...], gg, m_ref[...], v_ref[...])
        go_ref[...] = gg
        d_ref[...] = d
        nm_ref[...] = nm
        nv_ref[...] = nv

    part = pl.BlockSpec((None, tr, cols), lambda r: (i, r, 0))
    return pl.pallas_call(
        body, name=name, grid=(rows // tr,),
        in_specs=[part, pl.BlockSpec((tr, cols), lambda r: (r, 0)), part, part] + [ANY_SPEC] * 4,
        out_specs=[part] * 4, out_shape=[SDS(w.shape, F32)] * 4,
        input_output_aliases={4 + k: k for k in range(4)}, compiler_params=_cparams(),
    )(w, g, m, v, *prev)


WEIGHT_NAMES = ("c_ctx", "norm_g", "w_mod", "b_mod", "ffn_w_gate", "ffn_w_up", "ffn_w_down", "ab_w_in", "pool_w",
                "pool_scale", "q_norm_g", "w_uq", "kv_norm_g", "w_ukv", "ab_w_out", "conv_w_in", "conv_w",
                "conv_w_out", "final_norm_g")


def kernel(x, c, ctx, c_ctx, norm_g, w_mod, b_mod, ffn_w_gate, ffn_w_up, ffn_w_down, ab_w_in, pool_w, pool_scale, q_norm_g, w_uq, kv_norm_g, w_ukv, ab_w_out, conv_w_in, conv_w, conv_w_out, final_norm_g, loss_target, m_c_ctx, m_norm_g, m_w_mod, m_b_mod, m_ffn_w_gate, m_ffn_w_up, m_ffn_w_down, m_ab_w_in, m_pool_w, m_pool_scale, m_q_norm_g, m_w_uq, m_kv_norm_g, m_w_ukv, m_ab_w_out, m_conv_w_in, m_conv_w, m_conv_w_out, m_final_norm_g, v_c_ctx, v_norm_g, v_w_mod, v_b_mod, v_ffn_w_gate, v_ffn_w_up, v_ffn_w_down, v_ab_w_in, v_pool_w, v_pool_scale, v_q_norm_g, v_w_uq, v_kv_norm_g, v_w_ukv, v_ab_w_out, v_conv_w_in, v_conv_w, v_conv_w_out, v_final_norm_g):
    weights = (c_ctx, norm_g, w_mod, b_mod, ffn_w_gate, ffn_w_up, ffn_w_down, ab_w_in, pool_w, pool_scale, q_norm_g,
               w_uq, kv_norm_g, w_ukv, ab_w_out, conv_w_in, conv_w, conv_w_out, final_norm_g)
    moms = (m_c_ctx, m_norm_g, m_w_mod, m_b_mod, m_ffn_w_gate, m_ffn_w_up, m_ffn_w_down, m_ab_w_in, m_pool_w,
            m_pool_scale, m_q_norm_g, m_w_uq, m_kv_norm_g, m_w_ukv, m_ab_w_out, m_conv_w_in, m_conv_w, m_conv_w_out,
            m_final_norm_g)
    vels = (v_c_ctx, v_norm_g, v_w_mod, v_b_mod, v_ffn_w_gate, v_ffn_w_up, v_ffn_w_down, v_ab_w_in, v_pool_w,
            v_pool_scale, v_q_norm_g, v_w_uq, v_kv_norm_g, v_w_ukv, v_ab_w_out, v_conv_w_in, v_conv_w, v_conv_w_out,
            v_final_norm_g)
    me = 4 * lax.axis_index("x") + 2 * lax.axis_index("y") + lax.axis_index("c")
    n_lat, n_ctx = x.shape[1], ctx.shape[1]
    d = D_MODEL
    mod_cols = w_mod.shape[-1]
    ng_sh, cw_sh = norm_g.shape[-1], conv_w.shape[-1]

    def ffn_shards(i):
        return {f"gate_t{i}": ffn_w_gate[i // 2, i % 2].T, f"up_t{i}": ffn_w_up[i // 2, i % 2].T,
                f"down{i}": ffn_w_down[i // 2, i % 2]}

    local = {**ffn_shards(0), "in_t": ab_w_in[0].T, "uq": w_uq[0], "ukv_t": w_ukv[0].T, "ab_out": ab_w_out[0],
             **ffn_shards(1), **ffn_shards(2), "cin_t": conv_w_in[0].T, "c_out": conv_w_out[0], **ffn_shards(3)}
    ffn_groups = [[[f"gate_t{i}", f"up_t{i}"], [f"down{i}"]] for i in range(4)]
    groups = [*ffn_groups[0], ["in_t", "uq", "ukv_t", "ab_out"], *ffn_groups[1], *ffn_groups[2], ["cin_t", "c_out"],
              *ffn_groups[3]]
    feed = _Feed({nm: a.astype(BF16) for nm, a in local.items()}, groups, me)
    first_started = feed.start("first", groups[0] + groups[1], c)

    small = jnp.concatenate([c.reshape(-1), norm_g.reshape(-1), conv_w.reshape(-1)])
    small_n = -(-small.shape[0] // 1024) * 1024
    small = jnp.pad(small, (0, small_n - small.shape[0])).reshape(small_n // 128, 128)
    small_all = _exchange("gather_small", small, False, first_started).reshape(N_DEV, small_n)
    c_all = small_all[:, :d]
    o1 = d + 6 * ng_sh
    norm_g_full = small_all[:, d:o1].reshape(N_DEV, 2, 3, ng_sh).transpose(1, 2, 0, 3).reshape(2, 3, d)
    conv_w_full = small_all[:, o1:o1 + 3 * cw_sh].reshape(N_DEV, 3, cw_sh).transpose(1, 0, 2).reshape(3, d)

    cond = jnp.concatenate([c_all, jnp.broadcast_to(c_ctx[None, :], (N_DEV, d))], axis=0)
    sil, dsil = _silu_rows("mod_silu", cond)
    w_mod_b = w_mod.astype(BF16)
    b_sh = lax.dynamic_slice(b_mod, (0, me * mod_cols), (2, mod_cols))
    m_part = jnp.stack([_mm(f"mod_fwd{l}", [(sil, w_mod_b[l])], "nn", F32, 16, 384, bias=b_sh[l:l + 1])
                        for l in range(2)], axis=1)
    m_all = _exchange("gather_mod", m_part.reshape(-1, 128), False).reshape(N_DEV, 2 * N_DEV, 2, mod_cols)
    m_mine = lax.dynamic_index_in_dim(m_all, me, axis=1, keepdims=False)
    mod_h = m_mine.transpose(1, 0, 2).reshape(2, N_MOD, d)
    mod_g = m_all[:, N_DEV, 0, :].reshape(N_MOD, d)

    feed.start("rest", [nm for grp in groups[2:] for nm in grp], m_all)
    feed.relay_first(feed.token)

    sq_cols, ds0, g = _local_step(x[0], ctx[0], loss_target[0], mod_h, mod_g, norm_g_full, feed, pool_w[0],
                                  pool_scale, q_norm_g, kv_norm_g, conv_w_full, final_norm_g)
    grad_x = ds0[:n_lat]
    loss = lax.psum(0.5 * jnp.sum(sq_cols) / d, ("x", "y", "c"))
    w_of, m_of, v_of = (dict(zip(WEIGHT_NAMES, t)) for t in (weights, moms, vels))
    results = {}

    def update(nm, grad, view=lambda t: t):
        outs = _adamw(f"adamw_{nm}", view(w_of[nm]), grad.reshape(view(w_of[nm]).shape), view(m_of[nm]), view(v_of[nm]))
        results[nm] = tuple(view(t) for t in (grad.reshape(view(w_of[nm]).shape), *outs))

    def swap(t):
        return jnp.swapaxes(t, -1, -2)

    early = feed.collect(["l1f1", "l1m", "l1f0", "l0f1", "l0m"], [ds0])
    update("ab_w_in", early["in_t"], swap)
    update("w_uq", early["uq"])
    update("w_ukv", early["ukv_t"].T)
    update("ab_w_out", early["ab_out"])
    update("conv_w_in", early["cin_t"].T)
    update("conv_w_out", early["c_out"])
    ffn = {}
    for nm, prefix, view in (("ffn_w_gate", "gate_t", swap), ("ffn_w_up", "up_t", swap),
                             ("ffn_w_down", "down", lambda t: t)):
        w4, m4, v4 = (view(t).reshape((4,) + view(t).shape[-2:]) for t in (w_of[nm], m_of[nm], v_of[nm]))
        prev = None
        for i in (3, 2, 1):
            prev = _adamw_part(f"adamw_{nm}{i}", i, w4, early[f"{prefix}{i}"], m4, v4, prev)
        ffn[nm] = (prefix, view, w4, m4, v4, prev)
    done_early = [results[nm][1] for nm in results] + [state[5][1] for state in ffn.values()]
    late = feed.collect(["l0f0"], done_early)
    for nm, (prefix, view, w4, m4, v4, prev) in ffn.items():
        outs = _adamw_part(f"adamw_{nm}0", 0, w4, late[f"{prefix}0"], m4, v4, prev)
        results[nm] = tuple(view(t.reshape(view(w_of[nm]).shape)) for t in outs)

    dm = jnp.stack([g["mod_h"], jnp.stack([g["mod_g"], jnp.zeros_like(g["mod_g"])])])
    dm_all = _exchange("gather_dmod", dm.reshape(-1, 128), False, late["down0"]).reshape(N_DEV, 2, 2, N_MOD * d)
    grad_b_mod = _sum_rows("dmod_bias", dm_all.reshape(2 * N_DEV, 2 * N_MOD * d)).reshape(2, N_MOD * d)
    dm_sh = lax.dynamic_slice(dm_all, (0, 0, 0, me * mod_cols), (N_DEV, 2, 2, mod_cols))
    gw_mod, cctx_parts = [], []
    for l in range(2):
        dm_l = dm_sh[:, :, l, :].transpose(1, 0, 2).reshape(2 * N_DEV, mod_cols).astype(BF16)
        gw_mod.append(_mm(f"mod_dw{l}", [(sil, dm_l)], "tn", F32, 512, 384))
        dm_ctx = jnp.concatenate([dm_l[N_DEV:], jnp.zeros((N_DEV, mod_cols), BF16)], axis=0)
        cctx_parts.append(_mm(f"mod_dcond{l}", [(dm_ctx, w_mod_b[l])], "nt", F32, 16, 512))
    cctx_part = _sum_rows("mod_dcond_sum", jnp.concatenate(cctx_parts, axis=0))
    update("w_mod", jnp.stack(gw_mod))
    update("b_mod", grad_b_mod)

    small_g = jnp.concatenate([g["pool_w"].reshape(-1), g["pool_scale"].reshape(-1), g["q_norm_g"].reshape(-1),
                               g["kv_norm_g"].reshape(-1), g["final_norm_g"].reshape(-1), g["norm_g"].reshape(-1),
                               g["conv_w"].reshape(-1), cctx_part.reshape(-1)])
    sizes = [pool_w.size, pool_scale.size, q_norm_g.size, kv_norm_g.size, d, 6 * d, 3 * d, d]
    sg_n = -(-small_g.shape[0] // 1024) * 1024
    small_g = jnp.pad(small_g, (0, sg_n - small_g.shape[0]))
    sg_all = _exchange("gather_small_grads", small_g.reshape(-1, 128), False).reshape(N_DEV, sg_n)
    scale_vec = jnp.concatenate([jnp.ones((1, sum(sizes[:-1])), F32), dsil[N_DEV:N_DEV + 1],
                                 jnp.ones((1, sg_n - sum(sizes)), F32)], axis=1)
    sg = _sum_rows("small_grads_sum", sg_all, scale_vec)[0]
    cuts, pos = [], 0
    for sz in sizes:
        cuts.append(sg[pos:pos + sz])
        pos += sz
    g_pool_w, g_pool_scale, g_q_norm, g_kv_norm, g_final, g_norm_full, g_conv_full, g_c_ctx = cuts
    update("c_ctx", g_c_ctx)
    update("norm_g", lax.dynamic_slice(g_norm_full.reshape(2, 3, d), (0, 0, me * ng_sh), (2, 3, ng_sh)))
    update("conv_w", lax.dynamic_slice(g_conv_full.reshape(3, d), (0, me * cw_sh), (3, cw_sh)))
    update("pool_w", g_pool_w)
    update("pool_scale", g_pool_scale)
    update("q_norm_g", g_q_norm)
    update("kv_norm_g", g_kv_norm)
    update("final_norm_g", g_final)
    outs = [results[nm] for nm in WEIGHT_NAMES]
    return (loss, grad_x[None], *[o[0] for o in outs], *[o[1] for o in outs], *[o[2] for o in outs],
            *[o[3] for o in outs])
```

```python
import functools
import math

import jax
import jax.numpy as jnp
import numpy as np
from jax import lax
from jax.experimental import pallas as pl
from jax.experimental.pallas import tpu as pltpu

F32 = jnp.float32
BF16 = jnp.bfloat16
MESH = pl.DeviceIdType.MESH
SDS = jax.ShapeDtypeStruct

N_DEV = 8
D_MODEL = 1024
N_MOD = 9
D_FF = 2816
POOL_WINDOWS = (2, 4, 8, 16)
POOL_DIM = 512
POOL_GROUP_DIM = 128
HEADS = 8
QK_NOPE = 64
QK_ROPE = 32
QK_HEAD = QK_NOPE + QK_ROPE
V_HEAD = 64
Q_RANK = 768
KV_RANK = 256
GRID_W = 64
ROPE_THETA = 10000.0
RMS_EPS = 1e-6
ATTN_SCALE = 1.0 / math.sqrt(QK_HEAD)
HEAD_PAD = 128
POOL_PAD = 16
PA_POOL, PA_CQ, PA_KV = 0, 768, 1536
PA_KV_W = 384
PA_W = PA_KV + PA_KV_W

ADAM_LR, ADAM_B1, ADAM_B2, ADAM_EPS, ADAM_WD, ADAM_STEP = 0.001, 0.9, 0.999, 1e-08, 0.01, 10

VMEM_LIMIT_BYTES = 56 * 1024 * 1024

NN = ((1,), (0,))
NT = ((1,), (1,))
TN = ((0,), (0,))


def _cparams():
    return pltpu.CompilerParams(vmem_limit_bytes=VMEM_LIMIT_BYTES)


def _dot(a, b, dims):
    return lax.dot_general(a, b, (dims, ((), ())), preferred_element_type=F32)


def _tile(n, cap, mult=8):
    t = (min(cap, n) // mult) * mult
    while t >= mult:
        if n % t == 0:
            return t
        t -= mult
    return n


def _colsum(x):
    return jnp.sum(x, axis=0, keepdims=True)


def _rms(x):
    r = lax.rsqrt(jnp.mean(x * x, axis=-1, keepdims=True) + RMS_EPS)
    return x * r, r


def _rms_bwd(n, r, dn):
    return r * (dn - n * jnp.mean(dn * n, axis=-1, keepdims=True))


def _rowwise(name, fn, t_rows, tm, n_lat, rows, vecs, outs, accs):
    nt = t_rows // tm
    nlt = n_lat // tm
    n_groups = 2 if nlt < nt else 1

    def grp(i):
        return jnp.where(i >= nlt, 1, 0) if n_groups == 2 else 0

    in_specs = [pl.BlockSpec((tm, w), functools.partial(lambda i, cb: (i, cb), cb=cb)) for (_, w, cb) in rows]
    in_specs += [pl.BlockSpec((1,) + v.shape[1:], lambda i: (grp(i), 0, 0)) for v in vecs]
    out_specs = [pl.BlockSpec((tm, w), lambda i: (i, 0)) for (w, _) in outs]
    out_specs += [pl.BlockSpec((1, 1, w), lambda i: (grp(i), 0, 0)) for w in accs]
    out_shape = [SDS((t_rows, w), dt) for (w, dt) in outs] + [SDS((n_groups, 1, w), F32) for w in accs]
    n_r, n_v, n_o = len(rows), len(vecs), len(outs)

    def body(*refs):
        row_vals = [r[...] for r in refs[:n_r]]
        vec_vals = [v[0] for v in refs[n_r:n_r + n_v]]
        out_refs = refs[n_r + n_v:n_r + n_v + n_o]
        acc_refs = refs[n_r + n_v + n_o:]
        out_vals, acc_vals = fn(row_vals, vec_vals)
        for o_ref, o in zip(out_refs, out_vals):
            o_ref[...] = o.astype(o_ref.dtype)
        if acc_refs:
            i = pl.program_id(0)
            first = (i == 0) | (i == nlt) if n_groups == 2 else i == 0

            @pl.when(first)
            def _():
                for a_ref, a in zip(acc_refs, acc_vals):
                    a_ref[0] = a

            @pl.when(jnp.logical_not(first))
            def _():
                for a_ref, a in zip(acc_refs, acc_vals):
                    a_ref[0] += a

    res = pl.pallas_call(
        body, name=name, grid=(nt,), in_specs=in_specs, out_specs=out_specs, out_shape=out_shape,
        compiler_params=_cparams(),
    )(*[r[0] for r in rows], *vecs)
    return res[:n_o], res[n_o:]


RESIDENT_BYTES = 12 * 1024 * 1024


def _mm(name, pairs, mode, out_dtype, tm_cap=256, tn_cap=512, bias=None):
    a0, b0 = pairs[0]
    if mode == "nn":
        m, n, dims = a0.shape[0], b0.shape[1], NN
    elif mode == "nt":
        m, n, dims = a0.shape[0], b0.shape[0], NT
    else:
        m, n, dims = a0.shape[1], b0.shape[1], TN
    b_bytes = sum(b.size * b.dtype.itemsize for _, b in pairs)
    tn = n if b_bytes <= RESIDENT_BYTES else _tile(n, tn_cap, 128)
    tm = _tile(m, tm_cap, 128 if mode == "tn" else 16)

    def a_spec(a):
        if mode == "tn":
            return pl.BlockSpec((a.shape[0], tm), lambda i, j: (0, i))
        return pl.BlockSpec((tm, a.shape[1]), lambda i, j: (i, 0))

    def b_spec(b):
        if mode == "nt":
            return pl.BlockSpec((tn, b.shape[1]), lambda i, j: (j, 0))
        return pl.BlockSpec((b.shape[0], tn), lambda i, j: (0, j))

    in_specs, flat = [], []
    for a, b in pairs:
        in_specs += [a_spec(a), b_spec(b)]
        flat += [a, b]
    if bias is not None:
        in_specs.append(pl.BlockSpec((1, tn), lambda i, j: (0, j)))
        flat.append(bias)
    n_pairs = len(pairs)

    def body(*refs):
        acc = None
        for p in range(n_pairs):
            t = _dot(refs[2 * p][...], refs[2 * p + 1][...], dims)
            acc = t if acc is None else acc + t
        if bias is not None:
            acc = acc + refs[2 * n_pairs][...]
        refs[-1][...] = acc.astype(refs[-1].dtype)

    return pl.pallas_call(
        body, name=name, grid=(m // tm, n // tn), in_specs=in_specs,
        out_specs=pl.BlockSpec((tm, tn), lambda i, j: (i, j)),
        out_shape=SDS((m, n), out_dtype), compiler_params=_cparams(),
    )(*flat)


def _mm_resid(name, a, b, s, mg, k, coef, n_lat):
    t_rows, n = a.shape[0], b.shape[1]
    tm = _tile(math.gcd(n_lat, t_rows), 256, 16)
    nlt = n_lat // tm
    n_groups = 2 if nlt < t_rows // tm else 1

    def grp(i):
        return jnp.where(i >= nlt, 1, 0) if n_groups == 2 else 0

    def body(a_ref, b_ref, s_ref, mg_ref, so_ref, o_ref):
        o = _dot(a_ref[...], b_ref[...], NN)
        gate = mg_ref[0, 3 * k + 2:3 * k + 3, :]
        o_ref[...] = o.astype(BF16)
        so_ref[...] = s_ref[...] + (coef * gate) * o

    row = pl.BlockSpec((tm, n), lambda i: (i, 0))
    return pl.pallas_call(
        body, name=name, grid=(t_rows // tm,),
        in_specs=[pl.BlockSpec((tm, a.shape[1]), lambda i: (i, 0)), pl.BlockSpec(b.shape, lambda i: (0, 0)), row,
                  pl.BlockSpec((1, mg.shape[1], n), lambda i: (grp(i), 0, 0))],
        out_specs=[row, row], out_shape=[SDS((t_rows, n), F32), SDS((t_rows, n), BF16)], compiler_params=_cparams(),
    )(a, b, s, mg)


def _ffn_up(name, u, wg_t, wu_t):
    t_rows, f = u.shape[0], wg_t.shape[0]
    tm = _tile(t_rows, 256, 16)

    def body(u_ref, wg_ref, wu_ref, a_ref, b_ref, h_ref):
        uu = u_ref[...]
        a = _dot(uu, wg_ref[...], NT)
        b = _dot(uu, wu_ref[...], NT)
        sg = jax.nn.sigmoid(a)
        act = a * sg
        a_ref[...] = (b * (sg * (1.0 + a * (1.0 - sg)))).astype(BF16)
        b_ref[...] = act.astype(BF16)
        h_ref[...] = (act * b).astype(BF16)

    w_spec = pl.BlockSpec(wg_t.shape, lambda i: (0, 0))
    o_spec = pl.BlockSpec((tm, f), lambda i: (i, 0))
    return pl.pallas_call(
        body, name=name, grid=(t_rows // tm,),
        in_specs=[pl.BlockSpec((tm, u.shape[1]), lambda i: (i, 0)), w_spec, w_spec],
        out_specs=[o_spec, o_spec, o_spec], out_shape=[SDS((t_rows, f), BF16)] * 3, compiler_params=_cparams(),
    )(u, wg_t, wu_t)


def _ffn_dact(name, do, wd, a, b):
    t_rows, f = do.shape[0], wd.shape[0]
    tm = _tile(t_rows, 256, 16)

    def body(do_ref, wd_ref, a_ref, b_ref, da_ref, db_ref):
        dh = _dot(do_ref[...], wd_ref[...], NT)
        da_ref[...] = (dh * a_ref[...].astype(F32)).astype(BF16)
        db_ref[...] = (dh * b_ref[...].astype(F32)).astype(BF16)

    t_spec = pl.BlockSpec((tm, f), lambda i: (i, 0))
    return pl.pallas_call(
        body, name=name, grid=(t_rows // tm,),
        in_specs=[pl.BlockSpec((tm, do.shape[1]), lambda i: (i, 0)), pl.BlockSpec(wd.shape, lambda i: (0, 0)),
                  t_spec, t_spec],
        out_specs=[t_spec, t_spec], out_shape=[SDS((t_rows, f), BF16)] * 2, compiler_params=_cparams(),
    )(do, wd, a, b)


def _row_tm(t_rows, n_lat):
    return _tile(math.gcd(t_rows, n_lat), 256, 16)


def _adaln_fwd(name, s, mg, k, n_lat):
    t_rows = s.shape[0]

    def fn(rv, vv):
        m = vv[0]
        n, _ = _rms(rv[0])
        u = (n * m[9 + k:10 + k]) * (1.0 + m[3 * k + 1:3 * k + 2]) + m[3 * k:3 * k + 1]
        return [u], []

    (u,), _ = _rowwise(name, fn, t_rows, _row_tm(t_rows, n_lat), n_lat, [(s, D_MODEL, 0)], [mg], [(D_MODEL, BF16)], [])
    return u


def _adaln_bwd(name, s, du, ds_out, mg, k, n_lat):
    t_rows = s.shape[0]

    def fn(rv, vv):
        m = vv[0]
        gain, scale = m[9 + k:10 + k], m[3 * k + 1:3 * k + 2]
        n, r = _rms(rv[0])
        d_u = rv[1].astype(F32)
        dxn = d_u * (1.0 + scale)
        ds = _rms_bwd(n, r, dxn * gain)
        return [rv[2] + ds], [_colsum(d_u), _colsum(d_u * (n * gain)), _colsum(dxn * n)]

    (ds_in,), accs = _rowwise(name, fn, t_rows, _row_tm(t_rows, n_lat), n_lat,
                              [(s, D_MODEL, 0), (du, D_MODEL, 0), (ds_out, D_MODEL, 0)], [mg],
                              [(D_MODEL, F32)], [D_MODEL] * 3)
    return ds_in, accs


def _gate_bwd(name, ds_out, o, mg, k, coef, n_lat):
    t_rows = o.shape[0]

    def fn(rv, vv):
        gate = vv[0][3 * k + 2:3 * k + 3]
        d = coef * rv[0]
        return [d * gate], [_colsum(d * rv[1].astype(F32))]

    (do,), (dgate,) = _rowwise(name, fn, t_rows, _row_tm(t_rows, n_lat), n_lat,
                               [(ds_out, D_MODEL, 0), (o, D_MODEL, 0)], [mg], [(D_MODEL, BF16)], [D_MODEL])
    return do, dgate


def _rmsnorm_fwd(name, x, width, colblk, gain, t_rows):
    def fn(rv, vv):
        n, _ = _rms(rv[0])
        return [n * vv[0]], []

    (y,), _ = _rowwise(name, fn, t_rows, _tile(t_rows, 256, 16), t_rows, [(x, width, colblk)],
                       [gain.reshape(1, 1, width)], [(width, BF16)], [])
    return y


def _rmsnorm_bwd(name, x, width, colblk, dy, gain, t_rows):
    def fn(rv, vv):
        n, r = _rms(rv[0])
        return [_rms_bwd(n, r, rv[1] * vv[0])], [_colsum(rv[1] * n)]

    (dx,), (dgain,) = _rowwise(name, fn, t_rows, _tile(t_rows, 256, 16), t_rows,
                               [(x, width, colblk), (dy, width, 0)], [gain.reshape(1, 1, width)],
                               [(width, F32)], [width])
    return dx, dgain


def _final_loss(name, h, target, gain):
    t_rows = h.shape[0]
    inv_d = 1.0 / D_MODEL

    def fn(rv, vv):
        g = vv[0]
        n, r = _rms(rv[0])
        e = n * g - rv[1]
        dy = e * inv_d
        return [_rms_bwd(n, r, dy * g)], [_colsum(e * e), _colsum(dy * n)]

    (dh,), (sq, dgain) = _rowwise(name, fn, t_rows, _tile(t_rows, 256, 16), t_rows,
                                  [(h, D_MODEL, 0), (target, D_MODEL, 0)], [gain.reshape(1, 1, D_MODEL)],
                                  [(D_MODEL, F32)], [D_MODEL, D_MODEL])
    return dh, sq, dgain


def _rope(name, z, width, colblk, cos, sin, perm, backward, out_dtype):
    t_rows = cos.shape[0]

    def body(z_ref, c_ref, s_ref, p_ref, o_ref):
        zz = z_ref[...]
        pre = zz * s_ref[...] if backward else zz
        hi = pre.astype(BF16)
        lo = (pre - hi.astype(F32)).astype(BF16)
        rot = _dot(hi, p_ref[...], NN) + _dot(lo, p_ref[...], NN)
        if not backward:
            rot = rot * s_ref[...]
        o_ref[...] = (zz * c_ref[...] + rot).astype(o_ref.dtype)

    tm = _tile(t_rows, 256, 16)
    t_spec = pl.BlockSpec((tm, width), lambda i: (i, 0))
    return pl.pallas_call(
        body, name=name, grid=(t_rows // tm,),
        in_specs=[pl.BlockSpec((tm, width), lambda i: (i, colblk)), t_spec, t_spec,
                  pl.BlockSpec((width, width), lambda i: (0, 0))],
        out_specs=t_spec, out_shape=SDS((t_rows, width), out_dtype), compiler_params=_cparams(),
    )(z, cos, sin, perm)


def _window_sum(x, w, transposed):
    n_rows = x.shape[0]
    zeros = jnp.zeros((POOL_PAD, x.shape[1]), F32)
    y = jnp.concatenate([zeros, x, zeros], axis=0)
    total = n_rows + 2 * POOL_PAD
    if transposed:
        y = y + pltpu.roll(y, total - 1, 0)
    else:
        y = y + pltpu.roll(y, 1, 0)
    step = 1
    while 2 * step < w:
        y = pltpu.roll(y, step, 0) + pltpu.roll(y, total - step, 0)
        step *= 2
    return y[POOL_PAD:POOL_PAD + n_rows]


def _window_count(n_rows, w):
    t = lax.broadcasted_iota(jnp.int32, (n_rows, 1), 0)
    lo = jnp.maximum(t - w // 2, 0)
    hi = jnp.minimum(t + (w - w // 2 - 1), n_rows - 1)
    return (hi - lo + 1).astype(F32)


def _pool_fwd(name, proj, n_rows, w_grp, scale):
    def body(x_ref, w_ref, sc_ref, y_ref, p_ref):
        for g, w in enumerate(POOL_WINDOWS):
            cols = slice(g * POOL_GROUP_DIM, (g + 1) * POOL_GROUP_DIM)
            x = x_ref[:, cols]
            p = _window_sum(x, w, False) * (1.0 / _window_count(n_rows, w)) - x
            pb = p.astype(BF16)
            p_ref[:, cols] = pb
            y_ref[:, cols] = (_dot(pb, w_ref[g], NN) * sc_ref[:, cols]).astype(BF16)

    blk = pl.BlockSpec((n_rows, POOL_DIM), lambda i: (0, 0))
    return pl.pallas_call(
        body, name=name, grid=(1,),
        in_specs=[blk, pl.BlockSpec(w_grp.shape, lambda i: (0, 0, 0)), pl.BlockSpec((1, POOL_DIM), lambda i: (0, 0))],
        out_specs=[blk, blk], out_shape=[SDS((n_rows, POOL_DIM), BF16)] * 2, compiler_params=_cparams(),
    )(proj, w_grp, scale)


def _pool_bwd(name, dcat, n_rows, p, w_grp, scale):
    def body(dy_ref, p_ref, w_ref, sc_ref, dx_ref, dw_ref, dsc_ref):
        for g, w in enumerate(POOL_WINDOWS):
            cols = slice(g * POOL_GROUP_DIM, (g + 1) * POOL_GROUP_DIM)
            dy = dy_ref[:, cols]
            pb = p_ref[:, cols]
            pw = _dot(pb, w_ref[g], NN)
            dsc_ref[:, cols] = _colsum(dy * pw)
            dpw = (dy * sc_ref[:, cols]).astype(BF16)
            dw_ref[g] = _dot(pb, dpw, TN)
            dp = _dot(dpw, w_ref[g], NT)
            dx_ref[:, cols] = _window_sum(dp * (1.0 / _window_count(n_rows, w)), w, True) - dp

    blk = pl.BlockSpec((n_rows, POOL_DIM), lambda i: (0, 0))
    w_spec = pl.BlockSpec(w_grp.shape, lambda i: (0, 0, 0))
    v_spec = pl.BlockSpec((1, POOL_DIM), lambda i: (0, 0))
    return pl.pallas_call(
        body, name=name, grid=(1,), in_specs=[blk, blk, w_spec, v_spec], out_specs=[blk, w_spec, v_spec],
        out_shape=[SDS((n_rows, POOL_DIM), F32), SDS(w_grp.shape, F32), SDS((1, POOL_DIM), F32)],
        compiler_params=_cparams(),
    )(dcat, p, w_grp, scale)


def _attn_fwd(name, q, k, v):
    h, n_q, _ = q.shape
    n_k = k.shape[1]
    tq = _tile(n_q, 256, 16)

    def body(q_ref, k_ref, v_ref, o_ref, lse_ref):
        s = _dot(q_ref[...], k_ref[...], NT) * ATTN_SCALE
        m = jnp.max(s, axis=-1, keepdims=True)
        e = jnp.exp(s - m)
        l = jnp.sum(e, axis=-1, keepdims=True)
        p = (e * (1.0 / l)).astype(BF16)
        o_ref[...] = _dot(p, v_ref[...], NN).astype(BF16)
        lse_ref[...] = m + jnp.log(l)

    return pl.pallas_call(
        body, name=name, grid=(h, n_q // tq),
        in_specs=[pl.BlockSpec((None, tq, HEAD_PAD), lambda hh, i: (hh, i, 0)),
                  pl.BlockSpec((None, n_k, HEAD_PAD), lambda hh, i: (hh, 0, 0)),
                  pl.BlockSpec((None, n_k, V_HEAD), lambda hh, i: (hh, 0, 0))],
        out_specs=[pl.BlockSpec((None, tq, V_HEAD), lambda hh, i: (hh, i, 0)),
                   pl.BlockSpec((None, tq, 1), lambda hh, i: (hh, i, 0))],
        out_shape=[SDS((h, n_q, V_HEAD), BF16), SDS((h, n_q, 1), F32)], compiler_params=_cparams(),
    )(q, k, v)


def _attn_bwd(name, q, k, v, o, lse, do):
    h, n_q, _ = q.shape
    n_k = k.shape[1]
    tq = _tile(n_q, 256, 16)

    def body(q_ref, k_ref, v_ref, o_ref, lse_ref, do_ref, dq_ref, dk_ref, dv_ref, dks_ref):
        hh, i = pl.program_id(0), pl.program_id(1)
        qq, kk, dd = q_ref[...], k_ref[...], do_ref[...]
        s = _dot(qq, kk, NT) * ATTN_SCALE
        p = jnp.exp(s - lse_ref[...])
        dp = _dot(dd, v_ref[...], NT)
        delta = jnp.sum(dd.astype(F32) * o_ref[...].astype(F32), axis=-1, keepdims=True)
        ds = (p * (dp - delta) * ATTN_SCALE).astype(BF16)
        dq_ref[...] = _dot(ds, kk, NN)
        dk = _dot(ds, qq, TN)
        dv = _dot(p.astype(BF16), dd, TN)

        @pl.when(i == 0)
        def _():
            dk_ref[...] = dk
            dv_ref[...] = dv

        @pl.when(i > 0)
        def _():
            dk_ref[...] += dk
            dv_ref[...] += dv

        @pl.when((i == 0) & (hh == 0))
        def _():
            dks_ref[...] = dk

        @pl.when((i > 0) | (hh > 0))
        def _():
            dks_ref[...] += dk

    q_spec = pl.BlockSpec((None, tq, HEAD_PAD), lambda hh, i: (hh, i, 0))
    k_spec = pl.BlockSpec((None, n_k, HEAD_PAD), lambda hh, i: (hh, 0, 0))
    v_spec = pl.BlockSpec((None, n_k, V_HEAD), lambda hh, i: (hh, 0, 0))
    o_spec = pl.BlockSpec((None, tq, V_HEAD), lambda hh, i: (hh, i, 0))
    return pl.pallas_call(
        body, name=name, grid=(h, n_q // tq),
        in_specs=[q_spec, k_spec, v_spec, o_spec, pl.BlockSpec((None, tq, 1), lambda hh, i: (hh, i, 0)), o_spec],
        out_specs=[q_spec, k_spec, v_spec, pl.BlockSpec((n_k, HEAD_PAD), lambda hh, i: (0, 0))],
        out_shape=[SDS((h, n_q, HEAD_PAD), F32), SDS((h, n_k, HEAD_PAD), F32), SDS((h, n_k, V_HEAD), F32),
                   SDS((n_k, HEAD_PAD), F32)],
        compiler_params=_cparams(),
    )(q, k, v, o, lse, do)


CONV_COLS = 256


def _shift_rows(x, d):
    n_rows = x.shape[0]
    t = lax.broadcasted_iota(jnp.int32, (n_rows, 1), 0)
    if d > 0:
        return jnp.where(t >= d, pltpu.roll(x, d, 0), 0.0)
    return jnp.where(t < n_rows + d, pltpu.roll(x, n_rows + d, 0), 0.0)


def _conv_fwd(name, z3, conv_w):
    n_rows = z3.shape[0]
    nb = D_MODEL // CONV_COLS

    def body(b_ref, c_ref, v_ref, w_ref, y_ref):
        z = c_ref[...] * v_ref[...]
        zc = w_ref[0:1, :] * _shift_rows(z, 1) + w_ref[1:2, :] * z + w_ref[2:3, :] * _shift_rows(z, -1)
        y_ref[...] = (b_ref[...] * zc).astype(BF16)

    def part(k):
        return pl.BlockSpec((n_rows, CONV_COLS), lambda j: (0, k * nb + j))

    return pl.pallas_call(
        body, name=name, grid=(nb,),
        in_specs=[part(0), part(1), part(2), pl.BlockSpec((3, CONV_COLS), lambda j: (0, j))],
        out_specs=pl.BlockSpec((n_rows, CONV_COLS), lambda j: (0, j)),
        out_shape=SDS((n_rows, D_MODEL), BF16), compiler_params=_cparams(),
    )(z3, z3, z3, conv_w)


def _conv_bwd(name, dy, z3, conv_w):
    n_rows = z3.shape[0]
    nb = D_MODEL // CONV_COLS

    def body(dy_ref, b_ref, c_ref, v_ref, w_ref, db_ref, dc_ref, dv_ref, dw_ref):
        c, v, d_y = c_ref[...], v_ref[...], dy_ref[...]
        z = c * v
        z_dn, z_up = _shift_rows(z, 1), _shift_rows(z, -1)
        zc = w_ref[0:1, :] * z_dn + w_ref[1:2, :] * z + w_ref[2:3, :] * z_up
        db_ref[...] = (d_y * zc).astype(BF16)
        dzc = d_y * b_ref[...]
        dz = w_ref[0:1, :] * _shift_rows(dzc, -1) + w_ref[1:2, :] * dzc + w_ref[2:3, :] * _shift_rows(dzc, 1)
        dc_ref[...] = (dz * v).astype(BF16)
        dv_ref[...] = (dz * c).astype(BF16)
        dw_ref[0:1, :] = _colsum(dzc * z_dn)
        dw_ref[1:2, :] = _colsum(dzc * z)
        dw_ref[2:3, :] = _colsum(dzc * z_up)

    def part(k):
        return pl.BlockSpec((n_rows, CONV_COLS), lambda j: (0, k * nb + j))

    col = pl.BlockSpec((n_rows, CONV_COLS), lambda j: (0, j))
    w_spec = pl.BlockSpec((3, CONV_COLS), lambda j: (0, j))
    return pl.pallas_call(
        body, name=name, grid=(nb,), in_specs=[col, part(0), part(1), part(2), w_spec],
        out_specs=[col, col, col, w_spec],
        out_shape=[SDS((n_rows, D_MODEL), BF16)] * 3 + [SDS((3, D_MODEL), F32)], compiler_params=_cparams(),
    )(dy, z3, z3, z3, conv_w)


def _silu_rows(name, x):
    def body(x_ref, s_ref, d_ref):
        xx = x_ref[...]
        sg = jax.nn.sigmoid(xx)
        s_ref[...] = (xx * sg).astype(BF16)
        d_ref[...] = sg * (1.0 + xx * (1.0 - sg))

    return pl.pallas_call(body, name=name, out_shape=[SDS(x.shape, BF16), SDS(x.shape, F32)])(x)


def _sum_rows(name, x, scale=None):
    r, n = x.shape
    tn = _tile(n, 8192, 128)

    def body(*refs):
        acc = jnp.sum(refs[0][...].astype(F32), axis=0, keepdims=True)
        if scale is not None:
            acc = acc * refs[1][...]
        refs[-1][...] = acc

    in_specs = [pl.BlockSpec((r, tn), lambda j: (0, j))]
    args = [x]
    if scale is not None:
        in_specs.append(pl.BlockSpec((1, tn), lambda j: (0, j)))
        args.append(scale)
    return pl.pallas_call(body, name=name, grid=(n // tn,), in_specs=in_specs,
                          out_specs=pl.BlockSpec((1, tn), lambda j: (0, j)), out_shape=SDS((1, n), F32))(*args)


def _sum_slots(name, x):
    n_slots, r, c = x.shape
    tr = _tile(r, 432, 16)

    def body(x_ref, o_ref):
        acc = x_ref[0].astype(F32)
        for sl in range(1, n_slots):
            acc = acc + x_ref[sl].astype(F32)
        o_ref[...] = acc

    return pl.pallas_call(body, name=name, grid=(r // tr,),
                          in_specs=[pl.BlockSpec((n_slots, tr, c), lambda i: (0, i, 0))],
                          out_specs=pl.BlockSpec((tr, c), lambda i: (i, 0)), out_shape=SDS((r, c), F32),
                          compiler_params=_cparams())(x)


def _adamw(name, w, g, m, v):
    shape = w.shape
    cols = shape[-1]
    rows = w.size // cols
    tr = _tile(rows, 512, 8)
    bc1 = 1.0 - ADAM_B1 ** ADAM_STEP
    bc2 = 1.0 - ADAM_B2 ** ADAM_STEP

    def body(w_ref, g_ref, m_ref, v_ref, d_ref, nm_ref, nv_ref):
        gg = g_ref[...]
        nm = ADAM_B1 * m_ref[...] + (1.0 - ADAM_B1) * gg
        nv = ADAM_B2 * v_ref[...] + (1.0 - ADAM_B2) * (gg * gg)
        nm_ref[...] = nm
        nv_ref[...] = nv
        d_ref[...] = -ADAM_LR * ((nm / bc1) / (jnp.sqrt(nv / bc2) + ADAM_EPS) + ADAM_WD * w_ref[...])

    spec = pl.BlockSpec((tr, cols), lambda i: (i, 0))
    outs = pl.pallas_call(body, name=name, grid=(rows // tr,), in_specs=[spec] * 4, out_specs=[spec] * 3,
                          out_shape=[SDS((rows, cols), F32)] * 3, compiler_params=_cparams())(
        w.reshape(rows, cols), g.reshape(rows, cols), m.reshape(rows, cols), v.reshape(rows, cols))
    return tuple(t.reshape(shape) for t in outs)


def _exchange(name, x, scatter, after=None):
    blk = x.shape[1:] if scatter else x.shape
    extra = [] if after is None else [after]

    def body(x_ref, *rest):
        out_ref, send_sems, recv_sems, local_sem = rest[len(extra):]
        mx, my, mc = lax.axis_index("x"), lax.axis_index("y"), lax.axis_index("c")
        me = 4 * mx + 2 * my + mc
        own = pltpu.make_async_copy(x_ref.at[me] if scatter else x_ref, out_ref.at[me], local_sem)
        own.start()
        copies = []
        for kk in range(1, N_DEV):
            px = jnp.bitwise_xor(mx, (kk >> 2) & 1)
            py = jnp.bitwise_xor(my, (kk >> 1) & 1)
            pc = jnp.bitwise_xor(mc, kk & 1)
            peer = 4 * px + 2 * py + pc
            send = pltpu.make_async_remote_copy(
                src_ref=x_ref.at[peer] if scatter else x_ref, dst_ref=out_ref.at[me],
                send_sem=send_sems.at[kk - 1], recv_sem=recv_sems.at[kk - 1],
                device_id=(px, py, pc), device_id_type=MESH)
            send.start()
            arrival = pltpu.make_async_remote_copy(
                src_ref=x_ref.at[peer] if scatter else x_ref, dst_ref=out_ref.at[peer],
                send_sem=send_sems.at[kk - 1], recv_sem=recv_sems.at[kk - 1],
                device_id=(px, py, pc), device_id_type=MESH)
            copies.append((send, arrival))
        for send, arrival in copies:
            arrival.wait_recv()
            send.wait_send()
        own.wait()

    return pl.pallas_call(
        body, name=name, out_shape=SDS((N_DEV,) + tuple(blk), x.dtype),
        in_specs=[pl.BlockSpec(memory_space=pl.ANY)] * (1 + len(extra)), out_specs=pl.BlockSpec(memory_space=pl.ANY),
        scratch_shapes=[pltpu.SemaphoreType.DMA((N_DEV - 1,)), pltpu.SemaphoreType.DMA((N_DEV - 1,)),
                        pltpu.SemaphoreType.DMA],
    )(x, *extra)


def _rope_perm(pre, reps, post):
    half = QK_ROPE // 4
    width = reps * (pre + QK_ROPE) + post
    p = np.zeros((width, width), np.float32)
    for rep in range(reps):
        s0 = rep * (pre + QK_ROPE) + pre
        for base in (s0, s0 + 2 * half):
            for i in range(half):
                p[base + half + i, base + i] = -1.0
                p[base + i, base + half + i] = 1.0
    return p


def _rope_tables(n_lat, t_rows, pre, reps, post):
    half = QK_ROPE // 4
    pos = jnp.arange(n_lat)
    freqs = jnp.power(ROPE_THETA, -jnp.arange(0, 2 * half, 2, dtype=F32) / (2 * half))
    ang_r = (pos // GRID_W).astype(F32)[:, None] * freqs
    ang_c = (pos % GRID_W).astype(F32)[:, None] * freqs
    ang = jnp.concatenate([ang_r, ang_r, ang_c, ang_c], axis=-1)

    def table(fn, plain):
        slot = jnp.concatenate([jnp.full((n_lat, pre), plain, F32), fn(ang)], axis=-1)
        t = jnp.concatenate([jnp.tile(slot, (1, reps)), jnp.full((n_lat, post), plain, F32)], axis=-1)
        return jnp.concatenate([t, jnp.full((t_rows - n_lat, t.shape[1]), plain, F32)], axis=0)

    return table(jnp.cos, 1.0), table(jnp.sin, 0.0)


def _ffn_half_fwd(tag, s, mg, k, feed, i, coef, n_lat):
    u = _adaln_fwd(f"{tag}_adaln", s, mg, k, n_lat)
    wg_t, wu_t = feed.weights(f"{tag}_up", [f"gate_t{i}", f"up_t{i}"], u)
    a, b, hid = _ffn_up(f"{tag}_up", u, wg_t, wu_t)
    (wd,) = feed.weights(f"{tag}_down", [f"down{i}"], hid)
    s_out, o = _mm_resid(f"{tag}_down", hid, wd, s, mg, k, coef, n_lat)
    return s_out, (s, u, a, b, hid, o, wg_t, wu_t, wd)


def _ffn_half_bwd(tag, ds_out, saved, mg, k, feed, i, coef, n_lat):
    s, u, a, b, hid, o, wg_t, wu_t, wd = saved
    do, dgate = _gate_bwd(f"{tag}_dgate", ds_out, o, mg, k, coef, n_lat)
    da, db = _ffn_dact(f"{tag}_dact", do, wd, a, b)
    dwd = _mm(f"{tag}_dwd", [(hid, do)], "tn", BF16)
    dwg_t = _mm(f"{tag}_dwg", [(da, u)], "tn", BF16)
    dwu_t = _mm(f"{tag}_dwu", [(db, u)], "tn", BF16)
    token = feed.grads(tag, {f"down{i}": dwd, f"gate_t{i}": dwg_t, f"up_t{i}": dwu_t})
    du = _mm(f"{tag}_du", [(da, wg_t), (db, wu_t)], "nn", BF16, 384, 512, bias=_after(token))
    ds_in, (dshift, dscale, dgain) = _adaln_bwd(f"{tag}_dadaln", s, du, ds_out, mg, k, n_lat)
    return ds_in, dict(shift=dshift, scale=dscale, gate=dgate, gain=dgain)


def _after(token):
    return jnp.zeros((1, D_MODEL), F32) + token


def _mod_grad(parts, n_groups):
    rows = []
    zero = jnp.zeros((n_groups, 1, D_MODEL), F32)
    for k in range(3):
        for nm in ("shift", "scale", "gate"):
            t = parts[k].get(nm, zero)
            if t.shape[0] < n_groups:
                t = jnp.concatenate([t, jnp.zeros((n_groups - t.shape[0], 1, D_MODEL), F32)], axis=0)
            rows.append(t)
    return jnp.concatenate(rows, axis=1).reshape(n_groups, N_MOD * D_MODEL)


def _local_step(x, ctx, target, mod_h, mod_g, norm_g, feed, pool_w, pool_scale, q_norm_g, kv_norm_g, conv_w,
                final_norm_g):
    n_lat, n_ctx = x.shape[0], ctx.shape[0]
    t_all = n_lat + n_ctx
    mg0 = jnp.stack([jnp.concatenate([mod_h[0], norm_g[0]], axis=0), jnp.concatenate([mod_g, norm_g[0]], axis=0)])
    mg1 = jnp.concatenate([mod_h[1], norm_g[1]], axis=0)[None]

    s0 = jnp.concatenate([x, ctx], axis=0) + feed.start_token()
    s1, sv_f00 = _ffn_half_fwd("l0f0", s0, mg0, 0, feed, 0, 0.5, n_lat)

    ua = _adaln_fwd("l0m_adaln", s1, mg0, 1, n_lat)
    w_in, w_uq, w_ukv_t, w_ab_out = feed.weights("l0m", ["in_t", "uq", "ukv_t", "ab_out"], ua)
    kv_rows = KV_RANK + QK_ROPE
    w_in_t = jnp.concatenate([
        w_in[:POOL_DIM], jnp.zeros((PA_CQ - POOL_DIM, D_MODEL), BF16), w_in[POOL_DIM:POOL_DIM + Q_RANK],
        w_in[POOL_DIM + Q_RANK:], jnp.zeros((PA_KV_W - kv_rows, D_MODEL), BF16)], axis=0)
    proj = _mm("l0m_proj", [(ua, w_in_t)], "nt", F32, 768, 384)
    pool_y, pool_p = _pool_fwd("l0m_pool", proj, n_lat, pool_w.astype(BF16), pool_scale)
    nq = _rmsnorm_fwd("l0m_qnorm", proj, Q_RANK, PA_CQ // Q_RANK, q_norm_g, n_lat)
    q_lin = _mm("l0m_q", [(nq, w_uq)], "nn", F32, 512, 768)
    cos_q, sin_q = _rope_tables(n_lat, n_lat, QK_NOPE, HEADS, 0)
    perm_q = _rope_perm(QK_NOPE, HEADS, 0)
    q_rot = _rope("l0m_qrope", q_lin, Q_RANK, 0, cos_q, sin_q, jnp.asarray(perm_q, BF16), False, BF16)
    cos_k, sin_k = _rope_tables(n_lat, t_all, KV_RANK, 1, PA_KV_W - kv_rows)
    perm_k = _rope_perm(KV_RANK, 1, PA_KV_W - kv_rows)
    kvr = _rope("l0m_krope", proj, PA_KV_W, PA_KV // PA_KV_W, cos_k, sin_k, jnp.asarray(perm_k, BF16), False, F32)
    nkv = _rmsnorm_fwd("l0m_kvnorm", kvr, KV_RANK, 0, kv_norm_g, t_all)
    kv = _mm("l0m_kv", [(nkv, w_ukv_t)], "nt", BF16, 768, 512)
    qh = jnp.pad(q_rot.reshape(n_lat, HEADS, QK_HEAD), ((0, 0), (0, 0), (0, HEAD_PAD - QK_HEAD))).transpose(1, 0, 2)
    kvh = kv.reshape(t_all, HEADS, QK_NOPE + V_HEAD)
    k_rope = jnp.broadcast_to(kvr[:, None, KV_RANK:KV_RANK + QK_ROPE].astype(BF16), (t_all, HEADS, QK_ROPE))
    kh = jnp.concatenate([kvh[:, :, :QK_NOPE], k_rope, jnp.zeros((t_all, HEADS, HEAD_PAD - QK_HEAD), BF16)],
                         axis=-1).transpose(1, 0, 2)
    vh = kvh[:, :, QK_NOPE:].transpose(1, 0, 2)
    oh, lse = _attn_fwd("l0m_attn", qh, kh, vh)
    cat = jnp.concatenate([pool_y, oh.transpose(1, 0, 2).reshape(n_lat, HEADS * V_HEAD)], axis=-1)
    h1 = s1[:n_lat]
    h2, mix_o = _mm_resid("l0m_out", cat, w_ab_out, h1, mg0[:1], 1, 1.0, n_lat)

    h3, sv_f01 = _ffn_half_fwd("l0f1", h2, mg0[:1], 2, feed, 1, 0.5, n_lat)

    h4, sv_f10 = _ffn_half_fwd("l1f0", h3, mg1, 0, feed, 2, 0.5, n_lat)
    uc = _adaln_fwd("l1m_adaln", h4, mg1, 1, n_lat)
    w_cin_t, w_c_out = feed.weights("l1m", ["cin_t", "c_out"], uc)
    z3 = _mm("l1m_in", [(uc, w_cin_t)], "nt", F32)
    yc = _conv_fwd("l1m_conv", z3, conv_w)
    h5, conv_o = _mm_resid("l1m_out", yc, w_c_out, h4, mg1, 1, 1.0, n_lat)
    h6, sv_f11 = _ffn_half_fwd("l1f1", h5, mg1, 2, feed, 3, 0.5, n_lat)

    dh6, sq_cols, d_final_g = _final_loss("loss_head", h6, target, final_norm_g)
    g = {}
    dh5, g["f11"] = _ffn_half_bwd("l1f1", dh6, sv_f11, mg1, 2, feed, 3, 0.5, n_lat)

    do_c, dgate_c = _gate_bwd("l1m_dgate", dh5, conv_o, mg1, 1, 1.0, n_lat)
    dyc = _mm("l1m_dy", [(do_c, w_c_out)], "nt", F32)
    d_c_out = _mm("l1m_dwout", [(yc, do_c)], "tn", BF16)
    db_, dc_, dv_, d_conv_w = _conv_bwd("l1m_dconv", dyc, z3, conv_w)
    dz3 = jnp.concatenate([db_, dc_, dv_], axis=-1)
    d_cin_t = _mm("l1m_dwin", [(dz3, uc)], "tn", BF16)
    token = feed.grads("l1m", {"c_out": d_c_out, "cin_t": d_cin_t})
    duc = _mm("l1m_du", [(dz3, w_cin_t)], "nn", BF16, bias=_after(token))
    dh4, (dsh_c, dsc_c, dgn_c) = _adaln_bwd("l1m_dadaln", h4, duc, dh5, mg1, 1, n_lat)
    dh3, g["f10"] = _ffn_half_bwd("l1f0", dh4, sv_f10, mg1, 0, feed, 2, 0.5, n_lat)

    dh2, g["f01"] = _ffn_half_bwd("l0f1", dh3, sv_f01, mg0[:1], 2, feed, 1, 0.5, n_lat)

    do_a, dgate_a = _gate_bwd("l0m_dgate", dh2, mix_o, mg0[:1], 1, 1.0, n_lat)
    dcat = _mm("l0m_dcat", [(do_a, w_ab_out)], "nt", F32)
    d_ab_out = _mm("l0m_dwout", [(cat, do_a)], "tn", BF16)
    d_pool_x, d_pool_w, d_pool_scale = _pool_bwd("l0m_dpool", dcat, n_lat, pool_p, pool_w.astype(BF16), pool_scale)
    doh = dcat[:, POOL_DIM:].reshape(n_lat, HEADS, V_HEAD).transpose(1, 0, 2).astype(BF16)
    dqh, dkh, dvh, dk_sum = _attn_bwd("l0m_dattn", qh, kh, vh, oh, lse, doh)
    dq_rot = dqh[:, :, :QK_HEAD].transpose(1, 0, 2).reshape(n_lat, Q_RANK)
    dq_lin = _rope("l0m_dqrope", dq_rot, Q_RANK, 0, cos_q, sin_q, jnp.asarray(perm_q.T, BF16), True, BF16)
    d_uq = _mm("l0m_dwuq", [(nq, dq_lin)], "tn", BF16, 768, 768)
    dnq = _mm("l0m_dnq", [(dq_lin, w_uq)], "nt", F32, 512, 768)
    dcq, d_q_norm_g = _rmsnorm_bwd("l0m_dqnorm", proj, Q_RANK, PA_CQ // Q_RANK, dnq, q_norm_g, n_lat)
    dkv = jnp.concatenate([dkh[:, :, :QK_NOPE], dvh], axis=-1).transpose(1, 0, 2).reshape(t_all, HEADS * HEAD_PAD)
    dkv = dkv.astype(BF16)
    dnkv = _mm("l0m_dnkv", [(dkv, w_ukv_t)], "nn", F32, 768, 256)
    d_ukv_t = _mm("l0m_dwukv", [(dkv, nkv)], "tn", BF16, 512, 256)
    dckv, d_kv_norm_g = _rmsnorm_bwd("l0m_dkvnorm", kvr, KV_RANK, 0, dnkv, kv_norm_g, t_all)
    dkvr = jnp.concatenate([dckv, dk_sum[:, QK_NOPE:QK_HEAD],
                            jnp.zeros((t_all, PA_KV_W - KV_RANK - QK_ROPE), F32)], axis=-1)
    dpb = _rope("l0m_dkrope", dkvr, PA_KV_W, 0, cos_k, sin_k, jnp.asarray(perm_k.T, BF16), True, F32)
    dproj_lat = jnp.concatenate([d_pool_x, jnp.zeros((n_lat, PA_CQ - POOL_DIM), F32), dcq, dpb[:n_lat]], axis=-1)
    dproj_ctx = jnp.concatenate([jnp.zeros((n_ctx, PA_KV), F32), dpb[n_lat:]], axis=-1)
    dproj = jnp.concatenate([dproj_lat, dproj_ctx], axis=0).astype(BF16)
    d_in_pad = _mm("l0m_dwin", [(dproj, ua)], "tn", BF16, 640, 512)
    d_in_t = jnp.concatenate([d_in_pad[:POOL_DIM], d_in_pad[PA_CQ:PA_CQ + Q_RANK],
                              d_in_pad[PA_KV:PA_KV + kv_rows]], axis=0)
    token = feed.grads("l0m", {"ab_out": d_ab_out, "uq": d_uq, "ukv_t": d_ukv_t, "in_t": d_in_t})
    dua = _mm("l0m_du", [(dproj, w_in_t)], "nn", BF16, 768, 512, bias=_after(token))
    dh2_all = jnp.concatenate([dh2, jnp.zeros((n_ctx, D_MODEL), F32)], axis=0)
    ds1, (dsh_a, dsc_a, dgn_a) = _adaln_bwd("l0m_dadaln", s1, dua, dh2_all, mg0, 1, n_lat)
    ds0, g["f00"] = _ffn_half_bwd("l0f0", ds1, sv_f00, mg0, 0, feed, 0, 0.5, n_lat)

    dmod0 = _mod_grad([g["f00"], dict(shift=dsh_a, scale=dsc_a, gate=dgate_a), g["f01"]], 2)
    dmod1 = _mod_grad([g["f10"], dict(shift=dsh_c, scale=dsc_c, gate=dgate_c), g["f11"]], 1)
    d_norm_g = jnp.stack([
        jnp.concatenate([jnp.sum(g["f00"]["gain"], axis=0), jnp.sum(dgn_a, axis=0), g["f01"]["gain"][0]], axis=0),
        jnp.concatenate([g["f10"]["gain"][0], dgn_c[0], g["f11"]["gain"][0]], axis=0)])
    grads = dict(
        pool_w=d_pool_w, pool_scale=d_pool_scale, q_norm_g=d_q_norm_g[0], kv_norm_g=d_kv_norm_g[0],
        conv_w=d_conv_w, final_norm_g=d_final_g[0], norm_g=d_norm_g,
        mod_h=jnp.stack([dmod0[0], dmod1[0]]), mod_g=dmod0[1])
    return sq_cols, ds0, grads


HBM_SPEC = pl.BlockSpec(memory_space=pltpu.HBM)
SEM_SPEC = pl.BlockSpec(memory_space=pltpu.SEMAPHORE)
ANY_SPEC = pl.BlockSpec(memory_space=pl.ANY)
SIDE_EFFECT = pltpu.SideEffectType.DATAFLOW_SIDE_EFFECTING
N_PEERS = N_DEV - 1


def _mesh_place():
    mx, my, mc = lax.axis_index("x"), lax.axis_index("y"), lax.axis_index("c")
    return mx, my, mc, 4 * mx + 2 * my + mc


def _peer(place, kk):
    mx, my, mc, _ = place
    px = jnp.bitwise_xor(mx, (kk >> 2) & 1)
    py = jnp.bitwise_xor(my, (kk >> 1) & 1)
    pc = jnp.bitwise_xor(mc, kk & 1)
    return (px, py, pc), 4 * px + 2 * py + pc


def _hbm(a):
    return pltpu.with_memory_space_constraint(a, pltpu.HBM)


def _landing(block, me):
    zone = lax.empty((N_DEV,) + block.shape, block.dtype)
    return lax.dynamic_update_slice(zone, block[None], (me,) + (0,) * block.ndim)


ALL_PEERS = tuple(range(1, N_DEV))
SIBLING = 1
CHIP_PEERS = (2, 4, 6)
RELAYED = (3, 5, 7)


def _exchange_start(name, srcs, lands, scatter, after, peers=ALL_PEERS):
    n = len(srcs)

    def body(*refs):
        src, land = refs[:n], refs[n:2 * n]
        send_sems, recv_sems, token = refs[2 * n + 1], refs[2 * n + 2], refs[-1]
        place = _mesh_place()
        for a in range(n):
            for kk in peers:
                dev, peer = _peer(place, kk)
                pltpu.make_async_remote_copy(
                    src_ref=src[a].at[peer] if scatter else src[a], dst_ref=land[a].at[place[3]],
                    send_sem=send_sems.at[a * N_PEERS + kk - 1], recv_sem=recv_sems.at[a * N_PEERS + kk - 1],
                    device_id=dev, device_id_type=MESH).start()
        token[...] = jnp.zeros_like(token)

    thru = [pltpu.HBM(t.shape, t.dtype) for t in (*srcs, *lands)]
    res = pl.pallas_call(
        body, name=name,
        out_shape=(pltpu.SemaphoreType.DMA((n * N_PEERS,)), pltpu.SemaphoreType.DMA((n * N_PEERS,)), *thru,
                   SDS((8, 128), F32)),
        in_specs=[HBM_SPEC] * (2 * n) + [ANY_SPEC],
        out_specs=(SEM_SPEC, SEM_SPEC, *([HBM_SPEC] * (2 * n)), pl.BlockSpec(memory_space=pltpu.VMEM)),
        input_output_aliases={i: 2 + i for i in range(2 * n)},
        compiler_params=pltpu.CompilerParams(has_side_effects=SIDE_EFFECT),
    )(*[_hbm(s) for s in srcs], *[_hbm(t) for t in lands], after)
    return res[0], res[1], list(res[2:2 + n]), list(res[2 + n:2 + 2 * n]), res[-1]


def _exchange_wait(name, send_sems, recv_sems, srcs, lands, places, scatter, after):
    n = len(srcs)

    def body(*refs):
        src, land = refs[:n], refs[n:2 * n]
        send, recv = refs[2 * n], refs[2 * n + 1]
        place = _mesh_place()
        for a in range(n):
            for kk in range(1, N_DEV):
                dev, peer = _peer(place, kk)
                cp = pltpu.make_async_remote_copy(
                    src_ref=src[a].at[peer] if scatter else src[a], dst_ref=land[a].at[peer],
                    send_sem=send.at[places[a] * N_PEERS + kk - 1], recv_sem=recv.at[places[a] * N_PEERS + kk - 1],
                    device_id=dev, device_id_type=MESH)
                cp.wait_send()
                cp.wait_recv()

    thru = [pltpu.HBM(t.shape, t.dtype) for t in (*srcs, *lands)]
    res = pl.pallas_call(
        body, name=name, out_shape=tuple(thru),
        in_specs=[HBM_SPEC] * (2 * n) + [SEM_SPEC, SEM_SPEC] + [ANY_SPEC] * len(after),
        out_specs=tuple([HBM_SPEC] * (2 * n)), input_output_aliases={i: i for i in range(2 * n)},
        compiler_params=pltpu.CompilerParams(has_side_effects=SIDE_EFFECT),
    )(*srcs, *lands, send_sems, recv_sems, *after)
    return list(res[n:])


def _gather_relay(name, send1, recv1, lands, places, after):
    n = len(lands)

    def body(*refs):
        land, s1, r1 = refs[:n], refs[n], refs[n + 1]
        s2, r2 = refs[n + 3], refs[n + 4]
        place = _mesh_place()
        sibling = _peer(place, SIBLING)[0]
        for a in range(n):
            for j, kk in enumerate(CHIP_PEERS):
                dev, origin = _peer(place, kk)
                block = land[a].at[origin]
                pltpu.make_async_remote_copy(
                    src_ref=block, dst_ref=block, send_sem=s1.at[places[a] * N_PEERS + kk - 1],
                    recv_sem=r1.at[places[a] * N_PEERS + kk - 1], device_id=dev, device_id_type=MESH).wait_recv()
                pltpu.make_async_remote_copy(
                    src_ref=block, dst_ref=block, send_sem=s2.at[a * 3 + j], recv_sem=r2.at[a * 3 + j],
                    device_id=sibling, device_id_type=MESH).start()

    res = pl.pallas_call(
        body, name=name,
        out_shape=(pltpu.SemaphoreType.DMA((3 * n,)), pltpu.SemaphoreType.DMA((3 * n,)),
                   *[pltpu.HBM(t.shape, t.dtype) for t in lands]),
        in_specs=[HBM_SPEC] * n + [SEM_SPEC, SEM_SPEC, ANY_SPEC],
        out_specs=(SEM_SPEC, SEM_SPEC, *([HBM_SPEC] * n)),
        input_output_aliases={i: 2 + i for i in range(n)},
        compiler_params=pltpu.CompilerParams(has_side_effects=SIDE_EFFECT),
    )(*lands, send1, recv1, after)
    return res[0], res[1], list(res[2:])


def _gather_wait(name, send1, recv1, send2, recv2, srcs, lands, places, after):
    n = len(lands)

    def body(*refs):
        src, land = refs[:n], refs[n:2 * n]
        s1, r1, s2, r2 = refs[2 * n:2 * n + 4]
        place = _mesh_place()
        for a in range(n):
            for kk in (SIBLING,) + CHIP_PEERS:
                dev, origin = _peer(place, kk)
                first = pltpu.make_async_remote_copy(
                    src_ref=src[a], dst_ref=land[a].at[origin], send_sem=s1.at[places[a] * N_PEERS + kk - 1],
                    recv_sem=r1.at[places[a] * N_PEERS + kk - 1], device_id=dev, device_id_type=MESH)
                first.wait_send()
                if kk == SIBLING:
                    first.wait_recv()
            for j, kk in enumerate(CHIP_PEERS):
                dev, origin = _peer(place, kk + 1)
                relay = pltpu.make_async_remote_copy(
                    src_ref=src[a], dst_ref=land[a].at[origin], send_sem=s2.at[a * 3 + j], recv_sem=r2.at[a * 3 + j],
                    device_id=dev, device_id_type=MESH)
                relay.wait_send()
                relay.wait_recv()

    arrays = (*srcs, *lands)
    res = pl.pallas_call(
        body, name=name, out_shape=tuple(pltpu.HBM(t.shape, t.dtype) for t in arrays),
        in_specs=[HBM_SPEC] * (2 * n) + [SEM_SPEC] * 4 + [ANY_SPEC], out_specs=tuple([HBM_SPEC] * (2 * n)),
        input_output_aliases={i: i for i in range(2 * n)},
        compiler_params=pltpu.CompilerParams(has_side_effects=SIDE_EFFECT),
    )(*arrays, send1, recv1, send2, recv2, after)
    return list(res[n:])


class _Feed:
    def __init__(self, shards, groups, me):
        self.shards, self.groups, self.me, self.pos = shards, groups, me, 0
        self.sems, self.srcs, self.lands = {}, {}, {}
        self.pending = []

    def start(self, tag, names, after):
        srcs = [self.shards[nm] for nm in names]
        lands = [_landing(s, self.me) for s in srcs]
        send, recv, srcs, lands, self.token = _exchange_start(
            f"gather_start_{tag}", srcs, lands, False, after, (SIBLING,) + CHIP_PEERS)
        for i, nm in enumerate(names):
            self.sems[nm], self.srcs[nm], self.lands[nm] = (send, recv, i), srcs[i], lands[i]
        return self.token

    def relay_first(self, after):
        self.relay = self._relay("gather_relay_first", self.groups[0], after)

    def _relay(self, name, names, after):
        send, recv, _ = self.sems[names[0]]
        places = [self.sems[nm][2] for nm in names]
        send2, recv2, lands = _gather_relay(name, send, recv, [self.lands[nm] for nm in names], places, after)
        for nm, t in zip(names, lands):
            self.lands[nm] = t
        return send2, recv2

    def start_token(self):
        return self.token[0, 0]

    def weights(self, tag, names, after):
        assert names == self.groups[self.pos], (names, self.groups[self.pos])
        send2, recv2 = self.relay
        if self.pos + 1 < len(self.groups):
            nxt = self.groups[self.pos + 1]
            self.relay = self._relay(f"gather_relay_{tag}", nxt, after)
            after = self.lands[nxt[0]]
        send, recv, _ = self.sems[names[0]]
        got = _gather_wait(f"gather_wait_{tag}", send, recv, send2, recv2, [self.srcs[nm] for nm in names],
                           [self.lands[nm] for nm in names], [self.sems[nm][2] for nm in names], after)
        self.pos += 1
        return [t.reshape((N_DEV * t.shape[1],) + t.shape[2:]) for t in got]

    def grads(self, tag, full):
        names = list(full)
        srcs = [full[nm].reshape((N_DEV, full[nm].shape[0] // N_DEV) + full[nm].shape[1:]) for nm in names]
        lands = [_landing(lax.dynamic_index_in_dim(s, self.me, 0, keepdims=False), self.me) for s in srcs]
        send, recv, srcs, lands, token = _exchange_start(f"scatter_start_{tag}", srcs, lands, True, srcs[0])
        self.pending.append((tag, names, send, recv, srcs, lands))
        return token[0, 0]

    def collect(self, tags, after, keep_slots=()):
        out = {}
        for tag, names, send, recv, srcs, lands in self.pending:
            if tag not in tags:
                continue
            got = _exchange_wait(f"scatter_wait_{tag}", send, recv, srcs, lands, list(range(len(names))), True, after)
            for nm, slots in zip(names, got):
                out[nm] = slots if nm.startswith(tuple(keep_slots)) else _sum_slots(f"reduce_{nm}", slots)
        return out


def _adamw_math(w, gg, m, v):
    nm = ADAM_B1 * m + (1.0 - ADAM_B1) * gg
    nv = ADAM_B2 * v + (1.0 - ADAM_B2) * (gg * gg)
    bc1 = 1.0 - ADAM_B1 ** ADAM_STEP
    bc2 = 1.0 - ADAM_B2 ** ADAM_STEP
    return -ADAM_LR * ((nm / bc1) / (jnp.sqrt(nv / bc2) + ADAM_EPS) + ADAM_WD * w), nm, nv


def _adamw_part(name, i, w, slots, m, v, prev):
    n_parts, rows, cols = w.shape
    tr = _tile(rows, 256, 16)
    if prev is None:
        prev = tuple(lax.empty(w.shape, F32) for _ in range(4))

    def body(w_ref, g_ref, m_ref, v_ref, *rest):
        go_ref, d_ref, nm_ref, nv_ref = rest[4:]
        gg = g_ref[0].astype(F32)
        for sl in range(1, N_DEV):
            gg = gg + g_ref[sl].astype(F32)
        d, nm, nv = _adamw_math(w_ref[...], gg, m_ref[...], v_ref[...])
        go_ref[...] = gg
        d_ref[...] = d
        nm_ref[...] = nm
        nv_ref[...] = nv

    part = pl.BlockSpec((None, tr, cols), lambda r: (i, r, 0))
    return pl.pallas_call(
        body, name=name, grid=(rows // tr,),
        in_specs=[part, pl.BlockSpec((N_DEV, tr, cols), lambda r: (0, r, 0)), part, part] + [ANY_SPEC] * 4,
        out_specs=[part] * 4, out_shape=[SDS(w.shape, F32)] * 4,
        input_output_aliases={4 + k: k for k in range(4)}, compiler_params=_cparams(),
    )(w, slots, m, v, *prev)


WEIGHT_NAMES = ("c_ctx", "norm_g", "w_mod", "b_mod", "ffn_w_gate", "ffn_w_up", "ffn_w_down", "ab_w_in", "pool_w",
                "pool_scale", "q_norm_g", "w_uq", "kv_norm_g", "w_ukv", "ab_w_out", "conv_w_in", "conv_w",
                "conv_w_out", "final_norm_g")


def kernel(x, c, ctx, c_ctx, norm_g, w_mod, b_mod, ffn_w_gate, ffn_w_up, ffn_w_down, ab_w_in, pool_w, pool_scale, q_norm_g, w_uq, kv_norm_g, w_ukv, ab_w_out, conv_w_in, conv_w, conv_w_out, final_norm_g, loss_target, m_c_ctx, m_norm_g, m_w_mod, m_b_mod, m_ffn_w_gate, m_ffn_w_up, m_ffn_w_down, m_ab_w_in, m_pool_w, m_pool_scale, m_q_norm_g, m_w_uq, m_kv_norm_g, m_w_ukv, m_ab_w_out, m_conv_w_in, m_conv_w, m_conv_w_out, m_final_norm_g, v_c_ctx, v_norm_g, v_w_mod, v_b_mod, v_ffn_w_gate, v_ffn_w_up, v_ffn_w_down, v_ab_w_in, v_pool_w, v_pool_scale, v_q_norm_g, v_w_uq, v_kv_norm_g, v_w_ukv, v_ab_w_out, v_conv_w_in, v_conv_w, v_conv_w_out, v_final_norm_g):
    weights = (c_ctx, norm_g, w_mod, b_mod, ffn_w_gate, ffn_w_up, ffn_w_down, ab_w_in, pool_w, pool_scale, q_norm_g,
               w_uq, kv_norm_g, w_ukv, ab_w_out, conv_w_in, conv_w, conv_w_out, final_norm_g)
    moms = (m_c_ctx, m_norm_g, m_w_mod, m_b_mod, m_ffn_w_gate, m_ffn_w_up, m_ffn_w_down, m_ab_w_in, m_pool_w,
            m_pool_scale, m_q_norm_g, m_w_uq, m_kv_norm_g, m_w_ukv, m_ab_w_out, m_conv_w_in, m_conv_w, m_conv_w_out,
            m_final_norm_g)
    vels = (v_c_ctx, v_norm_g, v_w_mod, v_b_mod, v_ffn_w_gate, v_ffn_w_up, v_ffn_w_down, v_ab_w_in, v_pool_w,
            v_pool_scale, v_q_norm_g, v_w_uq, v_kv_norm_g, v_w_ukv, v_ab_w_out, v_conv_w_in, v_conv_w, v_conv_w_out,
            v_final_norm_g)
    me = 4 * lax.axis_index("x") + 2 * lax.axis_index("y") + lax.axis_index("c")
    n_lat, n_ctx = x.shape[1], ctx.shape[1]
    d = D_MODEL
    mod_cols = w_mod.shape[-1]
    ng_sh, cw_sh = norm_g.shape[-1], conv_w.shape[-1]

    def ffn_shards(i):
        return {f"gate_t{i}": ffn_w_gate[i // 2, i % 2].T, f"up_t{i}": ffn_w_up[i // 2, i % 2].T,
                f"down{i}": ffn_w_down[i // 2, i % 2]}

    local = {**ffn_shards(0), "in_t": ab_w_in[0].T, "uq": w_uq[0], "ukv_t": w_ukv[0].T, "ab_out": ab_w_out[0],
             **ffn_shards(1), **ffn_shards(2), "cin_t": conv_w_in[0].T, "c_out": conv_w_out[0], **ffn_shards(3)}
    ffn_groups = [[[f"gate_t{i}", f"up_t{i}"], [f"down{i}"]] for i in range(4)]
    groups = [*ffn_groups[0], ["in_t", "uq", "ukv_t", "ab_out"], *ffn_groups[1], *ffn_groups[2], ["cin_t", "c_out"],
              *ffn_groups[3]]
    feed = _Feed({nm: a.astype(BF16) for nm, a in local.items()}, groups, me)

    small = jnp.concatenate([c.reshape(-1), norm_g.reshape(-1), conv_w.reshape(-1)])
    small_n = -(-small.shape[0] // 1024) * 1024
    small = jnp.pad(small, (0, small_n - small.shape[0])).reshape(small_n // 128, 128)
    small_all = _exchange("gather_small", small, False).reshape(N_DEV, small_n)
    c_all = small_all[:, :d]
    o1 = d + 6 * ng_sh
    norm_g_full = small_all[:, d:o1].reshape(N_DEV, 2, 3, ng_sh).transpose(1, 2, 0, 3).reshape(2, 3, d)
    conv_w_full = small_all[:, o1:o1 + 3 * cw_sh].reshape(N_DEV, 3, cw_sh).transpose(1, 0, 2).reshape(3, d)

    cond = jnp.concatenate([c_all, jnp.broadcast_to(c_ctx[None, :], (N_DEV, d))], axis=0)
    sil, dsil = _silu_rows("mod_silu", cond)
    w_mod_b = w_mod.astype(BF16)
    b_sh = lax.dynamic_slice(b_mod, (0, me * mod_cols), (2, mod_cols))
    m_part = jnp.stack([_mm(f"mod_fwd{l}", [(sil, w_mod_b[l])], "nn", F32, 16, 384, bias=b_sh[l:l + 1])
                        for l in range(2)], axis=1)
    m_all = _exchange("gather_mod", m_part.reshape(-1, 128), False).reshape(N_DEV, 2 * N_DEV, 2, mod_cols)
    m_mine = lax.dynamic_index_in_dim(m_all, me, axis=1, keepdims=False)
    mod_h = m_mine.transpose(1, 0, 2).reshape(2, N_MOD, d)
    mod_g = m_all[:, N_DEV, 0, :].reshape(N_MOD, d)

    feed.start("all", [nm for grp in groups for nm in grp], m_all)
    feed.relay_first(feed.token)

    sq_cols, ds0, g = _local_step(x[0], ctx[0], loss_target[0], mod_h, mod_g, norm_g_full, feed, pool_w[0],
                                  pool_scale, q_norm_g, kv_norm_g, conv_w_full, final_norm_g)
    grad_x = ds0[:n_lat]
    loss = lax.psum(0.5 * jnp.sum(sq_cols) / d, ("x", "y", "c"))
    w_of, m_of, v_of = (dict(zip(WEIGHT_NAMES, t)) for t in (weights, moms, vels))
    results = {}

    def update(nm, grad, view=lambda t: t):
        outs = _adamw(f"adamw_{nm}", view(w_of[nm]), grad.reshape(view(w_of[nm]).shape), view(m_of[nm]), view(v_of[nm]))
        results[nm] = tuple(view(t) for t in (grad.reshape(view(w_of[nm]).shape), *outs))

    def swap(t):
        return jnp.swapaxes(t, -1, -2)

    stacked = ("gate_t", "up_t", "down")
    early = feed.collect(["l1f1", "l1m", "l1f0", "l0f1", "l0m"], [ds0], stacked)
    update("ab_w_in", early["in_t"], swap)
    update("w_uq", early["uq"])
    update("w_ukv", early["ukv_t"].T)
    update("ab_w_out", early["ab_out"])
    update("conv_w_in", early["cin_t"].T)
    update("conv_w_out", early["c_out"])
    ffn = {}
    for nm, prefix, view in (("ffn_w_gate", "gate_t", swap), ("ffn_w_up", "up_t", swap),
                             ("ffn_w_down", "down", lambda t: t)):
        w4, m4, v4 = (view(t).reshape((4,) + view(t).shape[-2:]) for t in (w_of[nm], m_of[nm], v_of[nm]))
        prev = None
        for i in (3, 2, 1):
            prev = _adamw_part(f"adamw_{nm}{i}", i, w4, early[f"{prefix}{i}"], m4, v4, prev)
        ffn[nm] = (prefix, view, w4, m4, v4, prev)
    done_early = [results[nm][1] for nm in results] + [state[5][1] for state in ffn.values()]
    late = feed.collect(["l0f0"], done_early, stacked)
    for nm, (prefix, view, w4, m4, v4, prev) in ffn.items():
        outs = _adamw_part(f"adamw_{nm}0", 0, w4, late[f"{prefix}0"], m4, v4, prev)
        results[nm] = tuple(view(t.reshape(view(w_of[nm]).shape)) for t in outs)

    dm = jnp.stack([g["mod_h"], jnp.stack([g["mod_g"], jnp.zeros_like(g["mod_g"])])])
    dm_all = _exchange("gather_dmod", dm.reshape(-1, 128), False, results["ffn_w_down"][1]).reshape(N_DEV, 2, 2, N_MOD * d)
    grad_b_mod = _sum_rows("dmod_bias", dm_all.reshape(2 * N_DEV, 2 * N_MOD * d)).reshape(2, N_MOD * d)
    dm_sh = lax.dynamic_slice(dm_all, (0, 0, 0, me * mod_cols), (N_DEV, 2, 2, mod_cols))
    gw_mod, cctx_parts = [], []
    for l in range(2):
        dm_l = dm_sh[:, :, l, :].transpose(1, 0, 2).reshape(2 * N_DEV, mod_cols).astype(BF16)
        gw_mod.append(_mm(f"mod_dw{l}", [(sil, dm_l)], "tn", F32, 512, 384))
        dm_ctx = jnp.concatenate([dm_l[N_DEV:], jnp.zeros((N_DEV, mod_cols), BF16)], axis=0)
        cctx_parts.append(_mm(f"mod_dcond{l}", [(dm_ctx, w_mod_b[l])], "nt", F32, 16, 512))
    cctx_part = _sum_rows("mod_dcond_sum", jnp.concatenate(cctx_parts, axis=0))
    update("w_mod", jnp.stack(gw_mod))
    update("b_mod", grad_b_mod)

    small_g = jnp.concatenate([g["pool_w"].reshape(-1), g["pool_scale"].reshape(-1), g["q_norm_g"].reshape(-1),
                               g["kv_norm_g"].reshape(-1), g["final_norm_g"].reshape(-1), g["norm_g"].reshape(-1),
                               g["conv_w"].reshape(-1), cctx_part.reshape(-1)])
    sizes = [pool_w.size, pool_scale.size, q_norm_g.size, kv_norm_g.size, d, 6 * d, 3 * d, d]
    sg_n = -(-small_g.shape[0] // 1024) * 1024
    small_g = jnp.pad(small_g, (0, sg_n - small_g.shape[0]))
    sg_all = _exchange("gather_small_grads", small_g.reshape(-1, 128), False).reshape(N_DEV, sg_n)
    scale_vec = jnp.concatenate([jnp.ones((1, sum(sizes[:-1])), F32), dsil[N_DEV:N_DEV + 1],
                                 jnp.ones((1, sg_n - sum(sizes)), F32)], axis=1)
    sg = _sum_rows("small_grads_sum", sg_all, scale_vec)[0]
    cuts, pos = [], 0
    for sz in sizes:
        cuts.append(sg[pos:pos + sz])
        pos += sz
    g_pool_w, g_pool_scale, g_q_norm, g_kv_norm, g_final, g_norm_full, g_conv_full, g_c_ctx = cuts
    update("c_ctx", g_c_ctx)
    update("norm_g", lax.dynamic_slice(g_norm_full.reshape(2, 3, d), (0, 0, me * ng_sh), (2, 3, ng_sh)))
    update("conv_w", lax.dynamic_slice(g_conv_full.reshape(3, d), (0, me * cw_sh), (3, cw_sh)))
    update("pool_w", g_pool_w)
    update("pool_scale", g_pool_scale)
    update("q_norm_g", g_q_norm)
    update("kv_norm_g", g_kv_norm)
    update("final_norm_g", g_final)
    outs = [results[nm] for nm in WEIGHT_NAMES]
    return (loss, grad_x[None], *[o[0] for o in outs], *[o[1] for o in outs], *[o[2] for o in outs],
            *[o[3] for o in outs])
```

```python
import functools
import math

import jax
import jax.numpy as jnp
import numpy as np
from jax import lax
from jax.experimental import pallas as pl
from jax.experimental.pallas import tpu as pltpu

F32 = jnp.float32
BF16 = jnp.bfloat16
MESH = pl.DeviceIdType.MESH
SDS = jax.ShapeDtypeStruct

N_DEV = 8
D_MODEL = 1024
N_MOD = 9
D_FF = 2816
POOL_WINDOWS = (2, 4, 8, 16)
POOL_DIM = 512
POOL_GROUP_DIM = 128
HEADS = 8
QK_NOPE = 64
QK_ROPE = 32
QK_HEAD = QK_NOPE + QK_ROPE
V_HEAD = 64
Q_RANK = 768
KV_RANK = 256
GRID_W = 64
ROPE_THETA = 10000.0
RMS_EPS = 1e-6
ATTN_SCALE = 1.0 / math.sqrt(QK_HEAD)
HEAD_PAD = 128
POOL_PAD = 16
PA_POOL, PA_CQ, PA_KV = 0, 768, 1536
PA_KV_W = 384
PA_W = PA_KV + PA_KV_W

ADAM_LR, ADAM_B1, ADAM_B2, ADAM_EPS, ADAM_WD, ADAM_STEP = 0.001, 0.9, 0.999, 1e-08, 0.01, 10

VMEM_LIMIT_BYTES = 56 * 1024 * 1024

NN = ((1,), (0,))
NT = ((1,), (1,))
TN = ((0,), (0,))


def _cparams():
    return pltpu.CompilerParams(vmem_limit_bytes=VMEM_LIMIT_BYTES)


def _dot(a, b, dims):
    return lax.dot_general(a, b, (dims, ((), ())), preferred_element_type=F32)


def _tile(n, cap, mult=8):
    t = (min(cap, n) // mult) * mult
    while t >= mult:
        if n % t == 0:
            return t
        t -= mult
    return n


def _colsum(x):
    return jnp.sum(x, axis=0, keepdims=True)


def _rms(x):
    r = lax.rsqrt(jnp.mean(x * x, axis=-1, keepdims=True) + RMS_EPS)
    return x * r, r


def _rms_bwd(n, r, dn):
    return r * (dn - n * jnp.mean(dn * n, axis=-1, keepdims=True))


def _rowwise(name, fn, t_rows, tm, n_lat, rows, vecs, outs, accs):
    nt = t_rows // tm
    nlt = n_lat // tm
    n_groups = 2 if nlt < nt else 1

    def grp(i):
        return jnp.where(i >= nlt, 1, 0) if n_groups == 2 else 0

    in_specs = [pl.BlockSpec((tm, w), functools.partial(lambda i, cb: (i, cb), cb=cb)) for (_, w, cb) in rows]
    in_specs += [pl.BlockSpec((1,) + v.shape[1:], lambda i: (grp(i), 0, 0)) for v in vecs]
    out_specs = [pl.BlockSpec((tm, w), lambda i: (i, 0)) for (w, _) in outs]
    out_specs += [pl.BlockSpec((1, 1, w), lambda i: (grp(i), 0, 0)) for w in accs]
    out_shape = [SDS((t_rows, w), dt) for (w, dt) in outs] + [SDS((n_groups, 1, w), F32) for w in accs]
    n_r, n_v, n_o = len(rows), len(vecs), len(outs)

    def body(*refs):
        row_vals = [r[...] for r in refs[:n_r]]
        vec_vals = [v[0] for v in refs[n_r:n_r + n_v]]
        out_refs = refs[n_r + n_v:n_r + n_v + n_o]
        acc_refs = refs[n_r + n_v + n_o:]
        out_vals, acc_vals = fn(row_vals, vec_vals)
        for o_ref, o in zip(out_refs, out_vals):
            o_ref[...] = o.astype(o_ref.dtype)
        if acc_refs:
            i = pl.program_id(0)
            first = (i == 0) | (i == nlt) if n_groups == 2 else i == 0

            @pl.when(first)
            def _():
                for a_ref, a in zip(acc_refs, acc_vals):
                    a_ref[0] = a

            @pl.when(jnp.logical_not(first))
            def _():
                for a_ref, a in zip(acc_refs, acc_vals):
                    a_ref[0] += a

    res = pl.pallas_call(
        body, name=name, grid=(nt,), in_specs=in_specs, out_specs=out_specs, out_shape=out_shape,
        compiler_params=_cparams(),
    )(*[r[0] for r in rows], *vecs)
    return res[:n_o], res[n_o:]


RESIDENT_BYTES = 12 * 1024 * 1024


def _mm(name, pairs, mode, out_dtype, tm_cap=256, tn_cap=512, bias=None):
    a0, b0 = pairs[0]
    if mode == "nn":
        m, n, dims = a0.shape[0], b0.shape[1], NN
    elif mode == "nt":
        m, n, dims = a0.shape[0], b0.shape[0], NT
    else:
        m, n, dims = a0.shape[1], b0.shape[1], TN
    b_bytes = sum(b.size * b.dtype.itemsize for _, b in pairs)
    tn = n if b_bytes <= RESIDENT_BYTES else _tile(n, tn_cap, 128)
    tm = _tile(m, tm_cap, 128 if mode == "tn" else 16)

    def a_spec(a):
        if mode == "tn":
            return pl.BlockSpec((a.shape[0], tm), lambda i, j: (0, i))
        return pl.BlockSpec((tm, a.shape[1]), lambda i, j: (i, 0))

    def b_spec(b):
        if mode == "nt":
            return pl.BlockSpec((tn, b.shape[1]), lambda i, j: (j, 0))
        return pl.BlockSpec((b.shape[0], tn), lambda i, j: (0, j))

    in_specs, flat = [], []
    for a, b in pairs:
        in_specs += [a_spec(a), b_spec(b)]
        flat += [a, b]
    if bias is not None:
        in_specs.append(pl.BlockSpec((1, tn), lambda i, j: (0, j)))
        flat.append(bias)
    n_pairs = len(pairs)

    def body(*refs):
        acc = None
        for p in range(n_pairs):
            t = _dot(refs[2 * p][...], refs[2 * p + 1][...], dims)
            acc = t if acc is None else acc + t
        if bias is not None:
            acc = acc + refs[2 * n_pairs][...]
        refs[-1][...] = acc.astype(refs[-1].dtype)

    return pl.pallas_call(
        body, name=name, grid=(m // tm, n // tn), in_specs=in_specs,
        out_specs=pl.BlockSpec((tm, tn), lambda i, j: (i, j)),
        out_shape=SDS((m, n), out_dtype), compiler_params=_cparams(),
    )(*flat)


def _mm_resid(name, a, b, s, mg, k, coef, n_lat):
    t_rows, n = a.shape[0], b.shape[1]
    tm = _tile(math.gcd(n_lat, t_rows), 256, 16)
    nlt = n_lat // tm
    n_groups = 2 if nlt < t_rows // tm else 1

    def grp(i):
        return jnp.where(i >= nlt, 1, 0) if n_groups == 2 else 0

    def body(a_ref, b_ref, s_ref, mg_ref, so_ref, o_ref):
        o = _dot(a_ref[...], b_ref[...], NN)
        gate = mg_ref[0, 3 * k + 2:3 * k + 3, :]
        o_ref[...] = o.astype(BF16)
        so_ref[...] = s_ref[...] + (coef * gate) * o

    row = pl.BlockSpec((tm, n), lambda i: (i, 0))
    return pl.pallas_call(
        body, name=name, grid=(t_rows // tm,),
        in_specs=[pl.BlockSpec((tm, a.shape[1]), lambda i: (i, 0)), pl.BlockSpec(b.shape, lambda i: (0, 0)), row,
                  pl.BlockSpec((1, mg.shape[1], n), lambda i: (grp(i), 0, 0))],
        out_specs=[row, row], out_shape=[SDS((t_rows, n), F32), SDS((t_rows, n), BF16)], compiler_params=_cparams(),
    )(a, b, s, mg)


def _groups(t_rows, tm, n_lat):
    nlt = n_lat // tm
    if nlt < t_rows // tm:
        return 2, (lambda i: jnp.where(i >= nlt, 1, 0)), (lambda i: (i == 0) | (i == nlt))
    return 1, (lambda i: 0), (lambda i: i == 0)


def _accumulate(acc_refs, vals, first):
    @pl.when(first)
    def _():
        for r, v in zip(acc_refs, vals):
            r[0] = v

    @pl.when(jnp.logical_not(first))
    def _():
        for r, v in zip(acc_refs, vals):
            r[0] += v


def _adaln_math(s, m, k):
    n, _ = _rms(s)
    return (n * m[9 + k:10 + k]) * (1.0 + m[3 * k + 1:3 * k + 2]) + m[3 * k:3 * k + 1]


def _ffn_up(name, s, mg, k, n_lat, wg_t, wu_t):
    t_rows, f = s.shape[0], wg_t.shape[0]
    tm = _row_tm(t_rows, n_lat)
    _, grp, _ = _groups(t_rows, tm, n_lat)

    def body(s_ref, mg_ref, wg_ref, wu_ref, u_ref, a_ref, b_ref, h_ref):
        uu = _adaln_math(s_ref[...], mg_ref[0], k).astype(BF16)
        u_ref[...] = uu
        a = _dot(uu, wg_ref[...], NT)
        b = _dot(uu, wu_ref[...], NT)
        sg = jax.nn.sigmoid(a)
        act = a * sg
        a_ref[...] = (b * (sg * (1.0 + a * (1.0 - sg)))).astype(BF16)
        b_ref[...] = act.astype(BF16)
        h_ref[...] = (act * b).astype(BF16)

    w_spec = pl.BlockSpec(wg_t.shape, lambda i: (0, 0))
    o_spec = pl.BlockSpec((tm, f), lambda i: (i, 0))
    row = pl.BlockSpec((tm, s.shape[1]), lambda i: (i, 0))
    return pl.pallas_call(
        body, name=name, grid=(t_rows // tm,),
        in_specs=[row, pl.BlockSpec((1,) + mg.shape[1:], lambda i: (grp(i), 0, 0)), w_spec, w_spec],
        out_specs=[row, o_spec, o_spec, o_spec],
        out_shape=[SDS(s.shape, BF16)] + [SDS((t_rows, f), BF16)] * 3, compiler_params=_cparams(),
    )(s, mg, wg_t, wu_t)


def _ffn_dact(name, ds_out, o, mg, k, coef, n_lat, wd, a, b):
    t_rows, f = ds_out.shape[0], wd.shape[0]
    tm = _row_tm(t_rows, n_lat)
    n_groups, grp, first = _groups(t_rows, tm, n_lat)
    d = ds_out.shape[1]

    def body(ds_ref, o_ref, mg_ref, wd_ref, a_ref, b_ref, do_ref, da_ref, db_ref, dg_ref):
        dd = coef * ds_ref[...]
        do = (dd * mg_ref[0, 3 * k + 2:3 * k + 3, :]).astype(BF16)
        do_ref[...] = do
        _accumulate([dg_ref], [_colsum(dd * o_ref[...].astype(F32))], first(pl.program_id(0)))
        dh = _dot(do, wd_ref[...], NT)
        da_ref[...] = (dh * a_ref[...].astype(F32)).astype(BF16)
        db_ref[...] = (dh * b_ref[...].astype(F32)).astype(BF16)

    row = pl.BlockSpec((tm, d), lambda i: (i, 0))
    t_spec = pl.BlockSpec((tm, f), lambda i: (i, 0))
    return pl.pallas_call(
        body, name=name, grid=(t_rows // tm,),
        in_specs=[row, row, pl.BlockSpec((1,) + mg.shape[1:], lambda i: (grp(i), 0, 0)),
                  pl.BlockSpec(wd.shape, lambda i: (0, 0)), t_spec, t_spec],
        out_specs=[row, t_spec, t_spec, pl.BlockSpec((1, 1, d), lambda i: (grp(i), 0, 0))],
        out_shape=[SDS((t_rows, d), BF16), SDS((t_rows, f), BF16), SDS((t_rows, f), BF16), SDS((n_groups, 1, d), F32)],
        compiler_params=_cparams(),
    )(ds_out, o, mg, wd, a, b)


def _ffn_du(name, da, db, wg_t, wu_t, s, ds_out, mg, k, n_lat, after):
    t_rows, d = s.shape
    tm = _row_tm(t_rows, n_lat)
    n_groups, grp, first = _groups(t_rows, tm, n_lat)

    def body(da_ref, db_ref, wg_ref, wu_ref, s_ref, ds_ref, mg_ref, z_ref, out_ref, dsh_ref, dsc_ref, dgn_ref):
        d_u = _dot(da_ref[...], wg_ref[...], NN) + _dot(db_ref[...], wu_ref[...], NN) + z_ref[...]
        m = mg_ref[0]
        gain, scale = m[9 + k:10 + k], m[3 * k + 1:3 * k + 2]
        n, r = _rms(s_ref[...])
        dxn = d_u * (1.0 + scale)
        out_ref[...] = ds_ref[...] + _rms_bwd(n, r, dxn * gain)
        _accumulate([dsh_ref, dsc_ref, dgn_ref], [_colsum(d_u), _colsum(d_u * (n * gain)), _colsum(dxn * n)],
                    first(pl.program_id(0)))

    row = pl.BlockSpec((tm, d), lambda i: (i, 0))
    wide = pl.BlockSpec((tm, da.shape[1]), lambda i: (i, 0))
    w_spec = pl.BlockSpec(wg_t.shape, lambda i: (0, 0))
    acc = pl.BlockSpec((1, 1, d), lambda i: (grp(i), 0, 0))
    res = pl.pallas_call(
        body, name=name, grid=(t_rows // tm,),
        in_specs=[wide, wide, w_spec, w_spec, row, row, pl.BlockSpec((1,) + mg.shape[1:], lambda i: (grp(i), 0, 0)),
                  pl.BlockSpec((1, d), lambda i: (0, 0))],
        out_specs=[row, acc, acc, acc],
        out_shape=[SDS((t_rows, d), F32)] + [SDS((n_groups, 1, d), F32)] * 3, compiler_params=_cparams(),
    )(da, db, wg_t, wu_t, s, ds_out, mg, after)
    return res[0], res[1:]


def _row_tm(t_rows, n_lat):
    return _tile(math.gcd(t_rows, n_lat), 256, 16)


def _adaln_fwd(name, s, mg, k, n_lat):
    t_rows = s.shape[0]

    def fn(rv, vv):
        m = vv[0]
        n, _ = _rms(rv[0])
        u = (n * m[9 + k:10 + k]) * (1.0 + m[3 * k + 1:3 * k + 2]) + m[3 * k:3 * k + 1]
        return [u], []

    (u,), _ = _rowwise(name, fn, t_rows, _row_tm(t_rows, n_lat), n_lat, [(s, D_MODEL, 0)], [mg], [(D_MODEL, BF16)], [])
    return u


def _adaln_bwd(name, s, du, ds_out, mg, k, n_lat):
    t_rows = s.shape[0]

    def fn(rv, vv):
        m = vv[0]
        gain, scale = m[9 + k:10 + k], m[3 * k + 1:3 * k + 2]
        n, r = _rms(rv[0])
        d_u = rv[1].astype(F32)
        dxn = d_u * (1.0 + scale)
        ds = _rms_bwd(n, r, dxn * gain)
        return [rv[2] + ds], [_colsum(d_u), _colsum(d_u * (n * gain)), _colsum(dxn * n)]

    (ds_in,), accs = _rowwise(name, fn, t_rows, _row_tm(t_rows, n_lat), n_lat,
                              [(s, D_MODEL, 0), (du, D_MODEL, 0), (ds_out, D_MODEL, 0)], [mg],
                              [(D_MODEL, F32)], [D_MODEL] * 3)
    return ds_in, accs


def _gate_bwd(name, ds_out, o, mg, k, coef, n_lat):
    t_rows = o.shape[0]

    def fn(rv, vv):
        gate = vv[0][3 * k + 2:3 * k + 3]
        d = coef * rv[0]
        return [d * gate], [_colsum(d * rv[1].astype(F32))]

    (do,), (dgate,) = _rowwise(name, fn, t_rows, _row_tm(t_rows, n_lat), n_lat,
                               [(ds_out, D_MODEL, 0), (o, D_MODEL, 0)], [mg], [(D_MODEL, BF16)], [D_MODEL])
    return do, dgate


def _rmsnorm_fwd(name, x, width, colblk, gain, t_rows):
    def fn(rv, vv):
        n, _ = _rms(rv[0])
        return [n * vv[0]], []

    (y,), _ = _rowwise(name, fn, t_rows, _tile(t_rows, 256, 16), t_rows, [(x, width, colblk)],
                       [gain.reshape(1, 1, width)], [(width, BF16)], [])
    return y


def _rmsnorm_bwd(name, x, width, colblk, dy, gain, t_rows):
    def fn(rv, vv):
        n, r = _rms(rv[0])
        return [_rms_bwd(n, r, rv[1] * vv[0])], [_colsum(rv[1] * n)]

    (dx,), (dgain,) = _rowwise(name, fn, t_rows, _tile(t_rows, 256, 16), t_rows,
                               [(x, width, colblk), (dy, width, 0)], [gain.reshape(1, 1, width)],
                               [(width, F32)], [width])
    return dx, dgain


def _final_loss(name, h, target, gain):
    t_rows = h.shape[0]
    inv_d = 1.0 / D_MODEL

    def fn(rv, vv):
        g = vv[0]
        n, r = _rms(rv[0])
        e = n * g - rv[1]
        dy = e * inv_d
        return [_rms_bwd(n, r, dy * g)], [_colsum(e * e), _colsum(dy * n)]

    (dh,), (sq, dgain) = _rowwise(name, fn, t_rows, _tile(t_rows, 256, 16), t_rows,
                                  [(h, D_MODEL, 0), (target, D_MODEL, 0)], [gain.reshape(1, 1, D_MODEL)],
                                  [(D_MODEL, F32)], [D_MODEL, D_MODEL])
    return dh, sq, dgain


def _rope(name, z, width, colblk, cos, sin, perm, backward, out_dtype):
    t_rows = cos.shape[0]

    def body(z_ref, c_ref, s_ref, p_ref, o_ref):
        zz = z_ref[...]
        pre = zz * s_ref[...] if backward else zz
        hi = pre.astype(BF16)
        lo = (pre - hi.astype(F32)).astype(BF16)
        rot = _dot(hi, p_ref[...], NN) + _dot(lo, p_ref[...], NN)
        if not backward:
            rot = rot * s_ref[...]
        o_ref[...] = (zz * c_ref[...] + rot).astype(o_ref.dtype)

    tm = _tile(t_rows, 256, 16)
    t_spec = pl.BlockSpec((tm, width), lambda i: (i, 0))
    return pl.pallas_call(
        body, name=name, grid=(t_rows // tm,),
        in_specs=[pl.BlockSpec((tm, width), lambda i: (i, colblk)), t_spec, t_spec,
                  pl.BlockSpec((width, width), lambda i: (0, 0))],
        out_specs=t_spec, out_shape=SDS((t_rows, width), out_dtype), compiler_params=_cparams(),
    )(z, cos, sin, perm)


def _window_sum(x, w, transposed):
    n_rows = x.shape[0]
    zeros = jnp.zeros((POOL_PAD, x.shape[1]), F32)
    y = jnp.concatenate([zeros, x, zeros], axis=0)
    total = n_rows + 2 * POOL_PAD
    if transposed:
        y = y + pltpu.roll(y, total - 1, 0)
    else:
        y = y + pltpu.roll(y, 1, 0)
    step = 1
    while 2 * step < w:
        y = pltpu.roll(y, step, 0) + pltpu.roll(y, total - step, 0)
        step *= 2
    return y[POOL_PAD:POOL_PAD + n_rows]


def _window_count(n_rows, w):
    t = lax.broadcasted_iota(jnp.int32, (n_rows, 1), 0)
    lo = jnp.maximum(t - w // 2, 0)
    hi = jnp.minimum(t + (w - w // 2 - 1), n_rows - 1)
    return (hi - lo + 1).astype(F32)


def _pool_fwd(name, proj, n_rows, w_grp, scale):
    def body(x_ref, w_ref, sc_ref, y_ref, p_ref):
        for g, w in enumerate(POOL_WINDOWS):
            cols = slice(g * POOL_GROUP_DIM, (g + 1) * POOL_GROUP_DIM)
            x = x_ref[:, cols]
            p = _window_sum(x, w, False) * (1.0 / _window_count(n_rows, w)) - x
            pb = p.astype(BF16)
            p_ref[:, cols] = pb
            y_ref[:, cols] = (_dot(pb, w_ref[g], NN) * sc_ref[:, cols]).astype(BF16)

    blk = pl.BlockSpec((n_rows, POOL_DIM), lambda i: (0, 0))
    return pl.pallas_call(
        body, name=name, grid=(1,),
        in_specs=[blk, pl.BlockSpec(w_grp.shape, lambda i: (0, 0, 0)), pl.BlockSpec((1, POOL_DIM), lambda i: (0, 0))],
        out_specs=[blk, blk], out_shape=[SDS((n_rows, POOL_DIM), BF16)] * 2, compiler_params=_cparams(),
    )(proj, w_grp, scale)


def _pool_bwd(name, dcat, n_rows, p, w_grp, scale):
    def body(dy_ref, p_ref, w_ref, sc_ref, dx_ref, dw_ref, dsc_ref):
        for g, w in enumerate(POOL_WINDOWS):
            cols = slice(g * POOL_GROUP_DIM, (g + 1) * POOL_GROUP_DIM)
            dy = dy_ref[:, cols]
            pb = p_ref[:, cols]
            pw = _dot(pb, w_ref[g], NN)
            dsc_ref[:, cols] = _colsum(dy * pw)
            dpw = (dy * sc_ref[:, cols]).astype(BF16)
            dw_ref[g] = _dot(pb, dpw, TN)
            dp = _dot(dpw, w_ref[g], NT)
            dx_ref[:, cols] = _window_sum(dp * (1.0 / _window_count(n_rows, w)), w, True) - dp

    blk = pl.BlockSpec((n_rows, POOL_DIM), lambda i: (0, 0))
    w_spec = pl.BlockSpec(w_grp.shape, lambda i: (0, 0, 0))
    v_spec = pl.BlockSpec((1, POOL_DIM), lambda i: (0, 0))
    return pl.pallas_call(
        body, name=name, grid=(1,), in_specs=[blk, blk, w_spec, v_spec], out_specs=[blk, w_spec, v_spec],
        out_shape=[SDS((n_rows, POOL_DIM), F32), SDS(w_grp.shape, F32), SDS((1, POOL_DIM), F32)],
        compiler_params=_cparams(),
    )(dcat, p, w_grp, scale)


def _attn_fwd(name, q, k, v):
    h, n_q, _ = q.shape
    n_k = k.shape[1]
    tq = _tile(n_q, 256, 16)

    def body(q_ref, k_ref, v_ref, o_ref, lse_ref):
        s = _dot(q_ref[...], k_ref[...], NT) * ATTN_SCALE
        m = jnp.max(s, axis=-1, keepdims=True)
        e = jnp.exp(s - m)
        l = jnp.sum(e, axis=-1, keepdims=True)
        p = (e * (1.0 / l)).astype(BF16)
        o_ref[...] = _dot(p, v_ref[...], NN).astype(BF16)
        lse_ref[...] = m + jnp.log(l)

    return pl.pallas_call(
        body, name=name, grid=(h, n_q // tq),
        in_specs=[pl.BlockSpec((None, tq, HEAD_PAD), lambda hh, i: (hh, i, 0)),
                  pl.BlockSpec((None, n_k, HEAD_PAD), lambda hh, i: (hh, 0, 0)),
                  pl.BlockSpec((None, n_k, V_HEAD), lambda hh, i: (hh, 0, 0))],
        out_specs=[pl.BlockSpec((None, tq, V_HEAD), lambda hh, i: (hh, i, 0)),
                   pl.BlockSpec((None, tq, 1), lambda hh, i: (hh, i, 0))],
        out_shape=[SDS((h, n_q, V_HEAD), BF16), SDS((h, n_q, 1), F32)], compiler_params=_cparams(),
    )(q, k, v)


def _attn_bwd(name, q, k, v, o, lse, do):
    h, n_q, _ = q.shape
    n_k = k.shape[1]
    tq = _tile(n_q, 256, 16)

    def body(q_ref, k_ref, v_ref, o_ref, lse_ref, do_ref, dq_ref, dk_ref, dv_ref, dks_ref):
        hh, i = pl.program_id(0), pl.program_id(1)
        qq, kk, dd = q_ref[...], k_ref[...], do_ref[...]
        s = _dot(qq, kk, NT) * ATTN_SCALE
        p = jnp.exp(s - lse_ref[...])
        dp = _dot(dd, v_ref[...], NT)
        delta = jnp.sum(dd.astype(F32) * o_ref[...].astype(F32), axis=-1, keepdims=True)
        ds = (p * (dp - delta) * ATTN_SCALE).astype(BF16)
        dq_ref[...] = _dot(ds, kk, NN)
        dk = _dot(ds, qq, TN)
        dv = _dot(p.astype(BF16), dd, TN)

        @pl.when(i == 0)
        def _():
            dk_ref[...] = dk
            dv_ref[...] = dv

        @pl.when(i > 0)
        def _():
            dk_ref[...] += dk
            dv_ref[...] += dv

        @pl.when((i == 0) & (hh == 0))
        def _():
            dks_ref[...] = dk

        @pl.when((i > 0) | (hh > 0))
        def _():
            dks_ref[...] += dk

    q_spec = pl.BlockSpec((None, tq, HEAD_PAD), lambda hh, i: (hh, i, 0))
    k_spec = pl.BlockSpec((None, n_k, HEAD_PAD), lambda hh, i: (hh, 0, 0))
    v_spec = pl.BlockSpec((None, n_k, V_HEAD), lambda hh, i: (hh, 0, 0))
    o_spec = pl.BlockSpec((None, tq, V_HEAD), lambda hh, i: (hh, i, 0))
    return pl.pallas_call(
        body, name=name, grid=(h, n_q // tq),
        in_specs=[q_spec, k_spec, v_spec, o_spec, pl.BlockSpec((None, tq, 1), lambda hh, i: (hh, i, 0)), o_spec],
        out_specs=[q_spec, k_spec, v_spec, pl.BlockSpec((n_k, HEAD_PAD), lambda hh, i: (0, 0))],
        out_shape=[SDS((h, n_q, HEAD_PAD), F32), SDS((h, n_k, HEAD_PAD), F32), SDS((h, n_k, V_HEAD), F32),
                   SDS((n_k, HEAD_PAD), F32)],
        compiler_params=_cparams(),
    )(q, k, v, o, lse, do)


CONV_COLS = 256


def _shift_rows(x, d):
    n_rows = x.shape[0]
    t = lax.broadcasted_iota(jnp.int32, (n_rows, 1), 0)
    if d > 0:
        return jnp.where(t >= d, pltpu.roll(x, d, 0), 0.0)
    return jnp.where(t < n_rows + d, pltpu.roll(x, n_rows + d, 0), 0.0)


def _conv_fwd(name, z3, conv_w):
    n_rows = z3.shape[0]
    nb = D_MODEL // CONV_COLS

    def body(b_ref, c_ref, v_ref, w_ref, y_ref):
        z = c_ref[...] * v_ref[...]
        zc = w_ref[0:1, :] * _shift_rows(z, 1) + w_ref[1:2, :] * z + w_ref[2:3, :] * _shift_rows(z, -1)
        y_ref[...] = (b_ref[...] * zc).astype(BF16)

    def part(k):
        return pl.BlockSpec((n_rows, CONV_COLS), lambda j: (0, k * nb + j))

    return pl.pallas_call(
        body, name=name, grid=(nb,),
        in_specs=[part(0), part(1), part(2), pl.BlockSpec((3, CONV_COLS), lambda j: (0, j))],
        out_specs=pl.BlockSpec((n_rows, CONV_COLS), lambda j: (0, j)),
        out_shape=SDS((n_rows, D_MODEL), BF16), compiler_params=_cparams(),
    )(z3, z3, z3, conv_w)


def _conv_bwd(name, dy, z3, conv_w):
    n_rows = z3.shape[0]
    nb = D_MODEL // CONV_COLS

    def body(dy_ref, b_ref, c_ref, v_ref, w_ref, db_ref, dc_ref, dv_ref, dw_ref):
        c, v, d_y = c_ref[...], v_ref[...], dy_ref[...]
        z = c * v
        z_dn, z_up = _shift_rows(z, 1), _shift_rows(z, -1)
        zc = w_ref[0:1, :] * z_dn + w_ref[1:2, :] * z + w_ref[2:3, :] * z_up
        db_ref[...] = (d_y * zc).astype(BF16)
        dzc = d_y * b_ref[...]
        dz = w_ref[0:1, :] * _shift_rows(dzc, -1) + w_ref[1:2, :] * dzc + w_ref[2:3, :] * _shift_rows(dzc, 1)
        dc_ref[...] = (dz * v).astype(BF16)
        dv_ref[...] = (dz * c).astype(BF16)
        dw_ref[0:1, :] = _colsum(dzc * z_dn)
        dw_ref[1:2, :] = _colsum(dzc * z)
        dw_ref[2:3, :] = _colsum(dzc * z_up)

    def part(k):
        return pl.BlockSpec((n_rows, CONV_COLS), lambda j: (0, k * nb + j))

    col = pl.BlockSpec((n_rows, CONV_COLS), lambda j: (0, j))
    w_spec = pl.BlockSpec((3, CONV_COLS), lambda j: (0, j))
    return pl.pallas_call(
        body, name=name, grid=(nb,), in_specs=[col, part(0), part(1), part(2), w_spec],
        out_specs=[col, col, col, w_spec],
        out_shape=[SDS((n_rows, D_MODEL), BF16)] * 3 + [SDS((3, D_MODEL), F32)], compiler_params=_cparams(),
    )(dy, z3, z3, z3, conv_w)


def _silu_rows(name, x):
    def body(x_ref, s_ref, d_ref):
        xx = x_ref[...]
        sg = jax.nn.sigmoid(xx)
        s_ref[...] = (xx * sg).astype(BF16)
        d_ref[...] = sg * (1.0 + xx * (1.0 - sg))

    return pl.pallas_call(body, name=name, out_shape=[SDS(x.shape, BF16), SDS(x.shape, F32)])(x)


def _sum_rows(name, x, scale=None):
    r, n = x.shape
    tn = _tile(n, 8192, 128)

    def body(*refs):
        acc = jnp.sum(refs[0][...].astype(F32), axis=0, keepdims=True)
        if scale is not None:
            acc = acc * refs[1][...]
        refs[-1][...] = acc

    in_specs = [pl.BlockSpec((r, tn), lambda j: (0, j))]
    args = [x]
    if scale is not None:
        in_specs.append(pl.BlockSpec((1, tn), lambda j: (0, j)))
        args.append(scale)
    return pl.pallas_call(body, name=name, grid=(n // tn,), in_specs=in_specs,
                          out_specs=pl.BlockSpec((1, tn), lambda j: (0, j)), out_shape=SDS((1, n), F32))(*args)


def _sum_slots(name, x):
    n_slots, r, c = x.shape
    tr = _tile(r, 432, 16)

    def body(x_ref, o_ref):
        acc = x_ref[0].astype(F32)
        for sl in range(1, n_slots):
            acc = acc + x_ref[sl].astype(F32)
        o_ref[...] = acc

    return pl.pallas_call(body, name=name, grid=(r // tr,),
                          in_specs=[pl.BlockSpec((n_slots, tr, c), lambda i: (0, i, 0))],
                          out_specs=pl.BlockSpec((tr, c), lambda i: (i, 0)), out_shape=SDS((r, c), F32),
                          compiler_params=_cparams())(x)


def _adamw(name, w, g, m, v):
    shape = w.shape
    cols = shape[-1]
    rows = w.size // cols
    tr = _tile(rows, 512, 8)
    bc1 = 1.0 - ADAM_B1 ** ADAM_STEP
    bc2 = 1.0 - ADAM_B2 ** ADAM_STEP

    def body(w_ref, g_ref, m_ref, v_ref, d_ref, nm_ref, nv_ref):
        gg = g_ref[...]
        nm = ADAM_B1 * m_ref[...] + (1.0 - ADAM_B1) * gg
        nv = ADAM_B2 * v_ref[...] + (1.0 - ADAM_B2) * (gg * gg)
        nm_ref[...] = nm
        nv_ref[...] = nv
        d_ref[...] = -ADAM_LR * ((nm / bc1) / (jnp.sqrt(nv / bc2) + ADAM_EPS) + ADAM_WD * w_ref[...])

    spec = pl.BlockSpec((tr, cols), lambda i: (i, 0))
    outs = pl.pallas_call(body, name=name, grid=(rows // tr,), in_specs=[spec] * 4, out_specs=[spec] * 3,
                          out_shape=[SDS((rows, cols), F32)] * 3, compiler_params=_cparams())(
        w.reshape(rows, cols), g.reshape(rows, cols), m.reshape(rows, cols), v.reshape(rows, cols))
    return tuple(t.reshape(shape) for t in outs)


def _exchange(name, x, scatter, after=None):
    blk = x.shape[1:] if scatter else x.shape
    extra = [] if after is None else [after]

    def body(x_ref, *rest):
        out_ref, send_sems, recv_sems, local_sem = rest[len(extra):]
        mx, my, mc = lax.axis_index("x"), lax.axis_index("y"), lax.axis_index("c")
        me = 4 * mx + 2 * my + mc
        own = pltpu.make_async_copy(x_ref.at[me] if scatter else x_ref, out_ref.at[me], local_sem)
        own.start()
        copies = []
        for kk in range(1, N_DEV):
            px = jnp.bitwise_xor(mx, (kk >> 2) & 1)
            py = jnp.bitwise_xor(my, (kk >> 1) & 1)
            pc = jnp.bitwise_xor(mc, kk & 1)
            peer = 4 * px + 2 * py + pc
            send = pltpu.make_async_remote_copy(
                src_ref=x_ref.at[peer] if scatter else x_ref, dst_ref=out_ref.at[me],
                send_sem=send_sems.at[kk - 1], recv_sem=recv_sems.at[kk - 1],
                device_id=(px, py, pc), device_id_type=MESH)
            send.start()
            arrival = pltpu.make_async_remote_copy(
                src_ref=x_ref.at[peer] if scatter else x_ref, dst_ref=out_ref.at[peer],
                send_sem=send_sems.at[kk - 1], recv_sem=recv_sems.at[kk - 1],
                device_id=(px, py, pc), device_id_type=MESH)
            copies.append((send, arrival))
        for send, arrival in copies:
            arrival.wait_recv()
            send.wait_send()
        own.wait()

    return pl.pallas_call(
        body, name=name, out_shape=SDS((N_DEV,) + tuple(blk), x.dtype),
        in_specs=[pl.BlockSpec(memory_space=pl.ANY)] * (1 + len(extra)), out_specs=pl.BlockSpec(memory_space=pl.ANY),
        scratch_shapes=[pltpu.SemaphoreType.DMA((N_DEV - 1,)), pltpu.SemaphoreType.DMA((N_DEV - 1,)),
                        pltpu.SemaphoreType.DMA],
    )(x, *extra)


def _rope_perm(pre, reps, post):
    half = QK_ROPE // 4
    width = reps * (pre + QK_ROPE) + post
    p = np.zeros((width, width), np.float32)
    for rep in range(reps):
        s0 = rep * (pre + QK_ROPE) + pre
        for base in (s0, s0 + 2 * half):
            for i in range(half):
                p[base + half + i, base + i] = -1.0
                p[base + i, base + half + i] = 1.0
    return p


def _rope_tables(n_lat, t_rows, pre, reps, post):
    half = QK_ROPE // 4
    pos = jnp.arange(n_lat)
    freqs = jnp.power(ROPE_THETA, -jnp.arange(0, 2 * half, 2, dtype=F32) / (2 * half))
    ang_r = (pos // GRID_W).astype(F32)[:, None] * freqs
    ang_c = (pos % GRID_W).astype(F32)[:, None] * freqs
    ang = jnp.concatenate([ang_r, ang_r, ang_c, ang_c], axis=-1)

    def table(fn, plain):
        slot = jnp.concatenate([jnp.full((n_lat, pre), plain, F32), fn(ang)], axis=-1)
        t = jnp.concatenate([jnp.tile(slot, (1, reps)), jnp.full((n_lat, post), plain, F32)], axis=-1)
        return jnp.concatenate([t, jnp.full((t_rows - n_lat, t.shape[1]), plain, F32)], axis=0)

    return table(jnp.cos, 1.0), table(jnp.sin, 0.0)


def _ffn_half_fwd(tag, s, mg, k, feed, i, coef, n_lat):
    wg_t, wu_t = feed.weights(f"{tag}_up", [f"gate_t{i}", f"up_t{i}"], s)
    u, a, b, hid = _ffn_up(f"{tag}_up", s, mg, k, n_lat, wg_t, wu_t)
    (wd,) = feed.weights(f"{tag}_down", [f"down{i}"], hid)
    s_out, o = _mm_resid(f"{tag}_down", hid, wd, s, mg, k, coef, n_lat)
    return s_out, (s, u, a, b, hid, o, wg_t, wu_t, wd)


def _ffn_half_bwd(tag, ds_out, saved, mg, k, feed, i, coef, n_lat):
    s, u, a, b, hid, o, wg_t, wu_t, wd = saved
    do, da, db, dgate = _ffn_dact(f"{tag}_dact", ds_out, o, mg, k, coef, n_lat, wd, a, b)
    dwd = _mm(f"{tag}_dwd", [(hid, do)], "tn", BF16)
    dwg_t = _mm(f"{tag}_dwg", [(da, u)], "tn", BF16)
    dwu_t = _mm(f"{tag}_dwu", [(db, u)], "tn", BF16)
    token = feed.grads(tag, {f"down{i}": dwd, f"gate_t{i}": dwg_t, f"up_t{i}": dwu_t})
    ds_in, (dshift, dscale, dgain) = _ffn_du(f"{tag}_du", da, db, wg_t, wu_t, s, ds_out, mg, k, n_lat, _after(token))
    return ds_in, dict(shift=dshift, scale=dscale, gate=dgate, gain=dgain)


def _after(token):
    return jnp.zeros((1, D_MODEL), F32) + token


def _mod_grad(parts, n_groups):
    rows = []
    zero = jnp.zeros((n_groups, 1, D_MODEL), F32)
    for k in range(3):
        for nm in ("shift", "scale", "gate"):
            t = parts[k].get(nm, zero)
            if t.shape[0] < n_groups:
                t = jnp.concatenate([t, jnp.zeros((n_groups - t.shape[0], 1, D_MODEL), F32)], axis=0)
            rows.append(t)
    return jnp.concatenate(rows, axis=1).reshape(n_groups, N_MOD * D_MODEL)


def _local_step(x, ctx, target, mod_h, mod_g, norm_g, feed, pool_w, pool_scale, q_norm_g, kv_norm_g, conv_w,
                final_norm_g):
    n_lat, n_ctx = x.shape[0], ctx.shape[0]
    t_all = n_lat + n_ctx
    mg0 = jnp.stack([jnp.concatenate([mod_h[0], norm_g[0]], axis=0), jnp.concatenate([mod_g, norm_g[0]], axis=0)])
    mg1 = jnp.concatenate([mod_h[1], norm_g[1]], axis=0)[None]

    s0 = jnp.concatenate([x, ctx], axis=0) + feed.start_token()
    s1, sv_f00 = _ffn_half_fwd("l0f0", s0, mg0, 0, feed, 0, 0.5, n_lat)

    ua = _adaln_fwd("l0m_adaln", s1, mg0, 1, n_lat)
    w_in, w_uq, w_ukv_t, w_ab_out = feed.weights("l0m", ["in_t", "uq", "ukv_t", "ab_out"], ua)
    kv_rows = KV_RANK + QK_ROPE
    w_in_t = jnp.concatenate([
        w_in[:POOL_DIM], jnp.zeros((PA_CQ - POOL_DIM, D_MODEL), BF16), w_in[POOL_DIM:POOL_DIM + Q_RANK],
        w_in[POOL_DIM + Q_RANK:], jnp.zeros((PA_KV_W - kv_rows, D_MODEL), BF16)], axis=0)
    proj = _mm("l0m_proj", [(ua, w_in_t)], "nt", F32, 768, 384)
    pool_y, pool_p = _pool_fwd("l0m_pool", proj, n_lat, pool_w.astype(BF16), pool_scale)
    nq = _rmsnorm_fwd("l0m_qnorm", proj, Q_RANK, PA_CQ // Q_RANK, q_norm_g, n_lat)
    q_lin = _mm("l0m_q", [(nq, w_uq)], "nn", F32, 512, 768)
    cos_q, sin_q = _rope_tables(n_lat, n_lat, QK_NOPE, HEADS, 0)
    perm_q = _rope_perm(QK_NOPE, HEADS, 0)
    q_rot = _rope("l0m_qrope", q_lin, Q_RANK, 0, cos_q, sin_q, jnp.asarray(perm_q, BF16), False, BF16)
    cos_k, sin_k = _rope_tables(n_lat, t_all, KV_RANK, 1, PA_KV_W - kv_rows)
    perm_k = _rope_perm(KV_RANK, 1, PA_KV_W - kv_rows)
    kvr = _rope("l0m_krope", proj, PA_KV_W, PA_KV // PA_KV_W, cos_k, sin_k, jnp.asarray(perm_k, BF16), False, F32)
    nkv = _rmsnorm_fwd("l0m_kvnorm", kvr, KV_RANK, 0, kv_norm_g, t_all)
    kv = _mm("l0m_kv", [(nkv, w_ukv_t)], "nt", BF16, 768, 512)
    qh = jnp.pad(q_rot.reshape(n_lat, HEADS, QK_HEAD), ((0, 0), (0, 0), (0, HEAD_PAD - QK_HEAD))).transpose(1, 0, 2)
    kvh = kv.reshape(t_all, HEADS, QK_NOPE + V_HEAD)
    k_rope = jnp.broadcast_to(kvr[:, None, KV_RANK:KV_RANK + QK_ROPE].astype(BF16), (t_all, HEADS, QK_ROPE))
    kh = jnp.concatenate([kvh[:, :, :QK_NOPE], k_rope, jnp.zeros((t_all, HEADS, HEAD_PAD - QK_HEAD), BF16)],
                         axis=-1).transpose(1, 0, 2)
    vh = kvh[:, :, QK_NOPE:].transpose(1, 0, 2)
    oh, lse = _attn_fwd("l0m_attn", qh, kh, vh)
    cat = jnp.concatenate([pool_y, oh.transpose(1, 0, 2).reshape(n_lat, HEADS * V_HEAD)], axis=-1)
    h1 = s1[:n_lat]
    h2, mix_o = _mm_resid("l0m_out", cat, w_ab_out, h1, mg0[:1], 1, 1.0, n_lat)

    h3, sv_f01 = _ffn_half_fwd("l0f1", h2, mg0[:1], 2, feed, 1, 0.5, n_lat)

    h4, sv_f10 = _ffn_half_fwd("l1f0", h3, mg1, 0, feed, 2, 0.5, n_lat)
    uc = _adaln_fwd("l1m_adaln", h4, mg1, 1, n_lat)
    w_cin_t, w_c_out = feed.weights("l1m", ["cin_t", "c_out"], uc)
    z3 = _mm("l1m_in", [(uc, w_cin_t)], "nt", F32)
    yc = _conv_fwd("l1m_conv", z3, conv_w)
    h5, conv_o = _mm_resid("l1m_out", yc, w_c_out, h4, mg1, 1, 1.0, n_lat)
    h6, sv_f11 = _ffn_half_fwd("l1f1", h5, mg1, 2, feed, 3, 0.5, n_lat)

    dh6, sq_cols, d_final_g = _final_loss("loss_head", h6, target, final_norm_g)
    g = {}
    dh5, g["f11"] = _ffn_half_bwd("l1f1", dh6, sv_f11, mg1, 2, feed, 3, 0.5, n_lat)

    do_c, dgate_c = _gate_bwd("l1m_dgate", dh5, conv_o, mg1, 1, 1.0, n_lat)
    dyc = _mm("l1m_dy", [(do_c, w_c_out)], "nt", F32)
    d_c_out = _mm("l1m_dwout", [(yc, do_c)], "tn", BF16)
    db_, dc_, dv_, d_conv_w = _conv_bwd("l1m_dconv", dyc, z3, conv_w)
    dz3 = jnp.concatenate([db_, dc_, dv_], axis=-1)
    d_cin_t = _mm("l1m_dwin", [(dz3, uc)], "tn", BF16)
    token = feed.grads("l1m", {"c_out": d_c_out, "cin_t": d_cin_t})
    duc = _mm("l1m_du", [(dz3, w_cin_t)], "nn", BF16, bias=_after(token))
    dh4, (dsh_c, dsc_c, dgn_c) = _adaln_bwd("l1m_dadaln", h4, duc, dh5, mg1, 1, n_lat)
    dh3, g["f10"] = _ffn_half_bwd("l1f0", dh4, sv_f10, mg1, 0, feed, 2, 0.5, n_lat)

    dh2, g["f01"] = _ffn_half_bwd("l0f1", dh3, sv_f01, mg0[:1], 2, feed, 1, 0.5, n_lat)

    do_a, dgate_a = _gate_bwd("l0m_dgate", dh2, mix_o, mg0[:1], 1, 1.0, n_lat)
    dcat = _mm("l0m_dcat", [(do_a, w_ab_out)], "nt", F32)
    d_ab_out = _mm("l0m_dwout", [(cat, do_a)], "tn", BF16)
    d_pool_x, d_pool_w, d_pool_scale = _pool_bwd("l0m_dpool", dcat, n_lat, pool_p, pool_w.astype(BF16), pool_scale)
    doh = dcat[:, POOL_DIM:].reshape(n_lat, HEADS, V_HEAD).transpose(1, 0, 2).astype(BF16)
    dqh, dkh, dvh, dk_sum = _attn_bwd("l0m_dattn", qh, kh, vh, oh, lse, doh)
    dq_rot = dqh[:, :, :QK_HEAD].transpose(1, 0, 2).reshape(n_lat, Q_RANK)
    dq_lin = _rope("l0m_dqrope", dq_rot, Q_RANK, 0, cos_q, sin_q, jnp.asarray(perm_q.T, BF16), True, BF16)
    d_uq = _mm("l0m_dwuq", [(nq, dq_lin)], "tn", BF16, 768, 768)
    dnq = _mm("l0m_dnq", [(dq_lin, w_uq)], "nt", F32, 512, 768)
    dcq, d_q_norm_g = _rmsnorm_bwd("l0m_dqnorm", proj, Q_RANK, PA_CQ // Q_RANK, dnq, q_norm_g, n_lat)
    dkv = jnp.concatenate([dkh[:, :, :QK_NOPE], dvh], axis=-1).transpose(1, 0, 2).reshape(t_all, HEADS * HEAD_PAD)
    dkv = dkv.astype(BF16)
    dnkv = _mm("l0m_dnkv", [(dkv, w_ukv_t)], "nn", F32, 768, 256)
    d_ukv_t = _mm("l0m_dwukv", [(dkv, nkv)], "tn", BF16, 512, 256)
    dckv, d_kv_norm_g = _rmsnorm_bwd("l0m_dkvnorm", kvr, KV_RANK, 0, dnkv, kv_norm_g, t_all)
    dkvr = jnp.concatenate([dckv, dk_sum[:, QK_NOPE:QK_HEAD],
                            jnp.zeros((t_all, PA_KV_W - KV_RANK - QK_ROPE), F32)], axis=-1)
    dpb = _rope("l0m_dkrope", dkvr, PA_KV_W, 0, cos_k, sin_k, jnp.asarray(perm_k.T, BF16), True, F32)
    dproj_lat = jnp.concatenate([d_pool_x, jnp.zeros((n_lat, PA_CQ - POOL_DIM), F32), dcq, dpb[:n_lat]], axis=-1)
    dproj_ctx = jnp.concatenate([jnp.zeros((n_ctx, PA_KV), F32), dpb[n_lat:]], axis=-1)
    dproj = jnp.concatenate([dproj_lat, dproj_ctx], axis=0).astype(BF16)
    d_in_pad = _mm("l0m_dwin", [(dproj, ua)], "tn", BF16, 640, 512)
    d_in_t = jnp.concatenate([d_in_pad[:POOL_DIM], d_in_pad[PA_CQ:PA_CQ + Q_RANK],
                              d_in_pad[PA_KV:PA_KV + kv_rows]], axis=0)
    token = feed.grads("l0m", {"ab_out": d_ab_out, "uq": d_uq, "ukv_t": d_ukv_t, "in_t": d_in_t})
    dua = _mm("l0m_du", [(dproj, w_in_t)], "nn", BF16, 768, 512, bias=_after(token))
    dh2_all = jnp.concatenate([dh2, jnp.zeros((n_ctx, D_MODEL), F32)], axis=0)
    ds1, (dsh_a, dsc_a, dgn_a) = _adaln_bwd("l0m_dadaln", s1, dua, dh2_all, mg0, 1, n_lat)
    ds0, g["f00"] = _ffn_half_bwd("l0f0", ds1, sv_f00, mg0, 0, feed, 0, 0.5, n_lat)

    dmod0 = _mod_grad([g["f00"], dict(shift=dsh_a, scale=dsc_a, gate=dgate_a), g["f01"]], 2)
    dmod1 = _mod_grad([g["f10"], dict(shift=dsh_c, scale=dsc_c, gate=dgate_c), g["f11"]], 1)
    d_norm_g = jnp.stack([
        jnp.concatenate([jnp.sum(g["f00"]["gain"], axis=0), jnp.sum(dgn_a, axis=0), g["f01"]["gain"][0]], axis=0),
        jnp.concatenate([g["f10"]["gain"][0], dgn_c[0], g["f11"]["gain"][0]], axis=0)])
    grads = dict(
        pool_w=d_pool_w, pool_scale=d_pool_scale, q_norm_g=d_q_norm_g[0], kv_norm_g=d_kv_norm_g[0],
        conv_w=d_conv_w, final_norm_g=d_final_g[0], norm_g=d_norm_g,
        mod_h=jnp.stack([dmod0[0], dmod1[0]]), mod_g=dmod0[1])
    return sq_cols, ds0, grads


HBM_SPEC = pl.BlockSpec(memory_space=pltpu.HBM)
SEM_SPEC = pl.BlockSpec(memory_space=pltpu.SEMAPHORE)
ANY_SPEC = pl.BlockSpec(memory_space=pl.ANY)
SIDE_EFFECT = pltpu.SideEffectType.DATAFLOW_SIDE_EFFECTING
N_PEERS = N_DEV - 1


def _mesh_place():
    mx, my, mc = lax.axis_index("x"), lax.axis_index("y"), lax.axis_index("c")
    return mx, my, mc, 4 * mx + 2 * my + mc


def _peer(place, kk):
    mx, my, mc, _ = place
    px = jnp.bitwise_xor(mx, (kk >> 2) & 1)
    py = jnp.bitwise_xor(my, (kk >> 1) & 1)
    pc = jnp.bitwise_xor(mc, kk & 1)
    return (px, py, pc), 4 * px + 2 * py + pc


def _hbm(a):
    return pltpu.with_memory_space_constraint(a, pltpu.HBM)


def _landing(block, me):
    zone = lax.empty((N_DEV,) + block.shape, block.dtype)
    return lax.dynamic_update_slice(zone, block[None], (me,) + (0,) * block.ndim)


ALL_PEERS = tuple(range(1, N_DEV))
SIBLING = 1
CHIP_PEERS = (2, 4, 6)
RELAYED = (3, 5, 7)


def _exchange_start(name, srcs, lands, scatter, after, peers=ALL_PEERS):
    n = len(srcs)

    def body(*refs):
        src, land = refs[:n], refs[n:2 * n]
        send_sems, recv_sems, token = refs[2 * n + 1], refs[2 * n + 2], refs[-1]
        place = _mesh_place()
        for a in range(n):
            for kk in peers:
                dev, peer = _peer(place, kk)
                pltpu.make_async_remote_copy(
                    src_ref=src[a].at[peer] if scatter else src[a], dst_ref=land[a].at[place[3]],
                    send_sem=send_sems.at[a * N_PEERS + kk - 1], recv_sem=recv_sems.at[a * N_PEERS + kk - 1],
                    device_id=dev, device_id_type=MESH).start()
        token[...] = jnp.zeros_like(token)

    thru = [pltpu.HBM(t.shape, t.dtype) for t in (*srcs, *lands)]
    res = pl.pallas_call(
        body, name=name,
        out_shape=(pltpu.SemaphoreType.DMA((n * N_PEERS,)), pltpu.SemaphoreType.DMA((n * N_PEERS,)), *thru,
                   SDS((8, 128), F32)),
        in_specs=[HBM_SPEC] * (2 * n) + [ANY_SPEC],
        out_specs=(SEM_SPEC, SEM_SPEC, *([HBM_SPEC] * (2 * n)), pl.BlockSpec(memory_space=pltpu.VMEM)),
        input_output_aliases={i: 2 + i for i in range(2 * n)},
        compiler_params=pltpu.CompilerParams(has_side_effects=SIDE_EFFECT),
    )(*[_hbm(s) for s in srcs], *[_hbm(t) for t in lands], after)
    return res[0], res[1], list(res[2:2 + n]), list(res[2 + n:2 + 2 * n]), res[-1]


def _exchange_wait(name, send_sems, recv_sems, srcs, lands, places, scatter, after):
    n = len(srcs)

    def body(*refs):
        src, land = refs[:n], refs[n:2 * n]
        send, recv = refs[2 * n], refs[2 * n + 1]
        place = _mesh_place()
        for a in range(n):
            for kk in range(1, N_DEV):
                dev, peer = _peer(place, kk)
                cp = pltpu.make_async_remote_copy(
                    src_ref=src[a].at[peer] if scatter else src[a], dst_ref=land[a].at[peer],
                    send_sem=send.at[places[a] * N_PEERS + kk - 1], recv_sem=recv.at[places[a] * N_PEERS + kk - 1],
                    device_id=dev, device_id_type=MESH)
                cp.wait_send()
                cp.wait_recv()

    thru = [pltpu.HBM(t.shape, t.dtype) for t in (*srcs, *lands)]
    res = pl.pallas_call(
        body, name=name, out_shape=tuple(thru),
        in_specs=[HBM_SPEC] * (2 * n) + [SEM_SPEC, SEM_SPEC] + [ANY_SPEC] * len(after),
        out_specs=tuple([HBM_SPEC] * (2 * n)), input_output_aliases={i: i for i in range(2 * n)},
        compiler_params=pltpu.CompilerParams(has_side_effects=SIDE_EFFECT),
    )(*srcs, *lands, send_sems, recv_sems, *after)
    return list(res[n:])


def _gather_relay(name, send1, recv1, lands, places, after):
    n = len(lands)

    def body(*refs):
        land, s1, r1 = refs[:n], refs[n], refs[n + 1]
        s2, r2 = refs[n + 3], refs[n + 4]
        place = _mesh_place()
        sibling = _peer(place, SIBLING)[0]
        for a in range(n):
            for j, kk in enumerate(CHIP_PEERS):
                dev, origin = _peer(place, kk)
                block = land[a].at[origin]
                pltpu.make_async_remote_copy(
                    src_ref=block, dst_ref=block, send_sem=s1.at[places[a] * N_PEERS + kk - 1],
                    recv_sem=r1.at[places[a] * N_PEERS + kk - 1], device_id=dev, device_id_type=MESH).wait_recv()
                pltpu.make_async_remote_copy(
                    src_ref=block, dst_ref=block, send_sem=s2.at[a * 3 + j], recv_sem=r2.at[a * 3 + j],
                    device_id=sibling, device_id_type=MESH).start()

    res = pl.pallas_call(
        body, name=name,
        out_shape=(pltpu.SemaphoreType.DMA((3 * n,)), pltpu.SemaphoreType.DMA((3 * n,)),
                   *[pltpu.HBM(t.shape, t.dtype) for t in lands]),
        in_specs=[HBM_SPEC] * n + [SEM_SPEC, SEM_SPEC, ANY_SPEC],
        out_specs=(SEM_SPEC, SEM_SPEC, *([HBM_SPEC] * n)),
        input_output_aliases={i: 2 + i for i in range(n)},
        compiler_params=pltpu.CompilerParams(has_side_effects=SIDE_EFFECT),
    )(*lands, send1, recv1, after)
    return res[0], res[1], list(res[2:])


def _gather_wait(name, send1, recv1, send2, recv2, srcs, lands, places, after):
    n = len(lands)

    def body(*refs):
        src, land = refs[:n], refs[n:2 * n]
        s1, r1, s2, r2 = refs[2 * n:2 * n + 4]
        place = _mesh_place()
        for a in range(n):
            for kk in (SIBLING,) + CHIP_PEERS:
                dev, origin = _peer(place, kk)
                first = pltpu.make_async_remote_copy(
                    src_ref=src[a], dst_ref=land[a].at[origin], send_sem=s1.at[places[a] * N_PEERS + kk - 1],
                    recv_sem=r1.at[places[a] * N_PEERS + kk - 1], device_id=dev, device_id_type=MESH)
                first.wait_send()
                if kk == SIBLING:
                    first.wait_recv()
            for j, kk in enumerate(CHIP_PEERS):
                dev, origin = _peer(place, kk + 1)
                relay = pltpu.make_async_remote_copy(
                    src_ref=src[a], dst_ref=land[a].at[origin], send_sem=s2.at[a * 3 + j], recv_sem=r2.at[a * 3 + j],
                    device_id=dev, device_id_type=MESH)
                relay.wait_send()
                relay.wait_recv()

    arrays = (*srcs, *lands)
    res = pl.pallas_call(
        body, name=name, out_shape=tuple(pltpu.HBM(t.shape, t.dtype) for t in arrays),
        in_specs=[HBM_SPEC] * (2 * n) + [SEM_SPEC] * 4 + [ANY_SPEC], out_specs=tuple([HBM_SPEC] * (2 * n)),
        input_output_aliases={i: i for i in range(2 * n)},
        compiler_params=pltpu.CompilerParams(has_side_effects=SIDE_EFFECT),
    )(*arrays, send1, recv1, send2, recv2, after)
    return list(res[n:])


class _Feed:
    def __init__(self, shards, groups, me):
        self.shards, self.groups, self.me, self.pos = shards, groups, me, 0
        self.sems, self.srcs, self.lands = {}, {}, {}
        self.pending = []

    def start(self, tag, names, after):
        srcs = [self.shards[nm] for nm in names]
        lands = [_landing(s, self.me) for s in srcs]
        send, recv, srcs, lands, self.token = _exchange_start(
            f"gather_start_{tag}", srcs, lands, False, after, (SIBLING,) + CHIP_PEERS)
        for i, nm in enumerate(names):
            self.sems[nm], self.srcs[nm], self.lands[nm] = (send, recv, i), srcs[i], lands[i]
        return self.token

    def relay_first(self, after):
        self.relay = self._relay("gather_relay_first", self.groups[0], after)

    def _relay(self, name, names, after):
        send, recv, _ = self.sems[names[0]]
        places = [self.sems[nm][2] for nm in names]
        send2, recv2, lands = _gather_relay(name, send, recv, [self.lands[nm] for nm in names], places, after)
        for nm, t in zip(names, lands):
            self.lands[nm] = t
        return send2, recv2

    def start_token(self):
        return self.token[0, 0]

    def weights(self, tag, names, after):
        assert names == self.groups[self.pos], (names, self.groups[self.pos])
        send2, recv2 = self.relay
        if self.pos + 1 < len(self.groups):
            nxt = self.groups[self.pos + 1]
            self.relay = self._relay(f"gather_relay_{tag}", nxt, after)
            after = self.lands[nxt[0]]
        send, recv, _ = self.sems[names[0]]
        got = _gather_wait(f"gather_wait_{tag}", send, recv, send2, recv2, [self.srcs[nm] for nm in names],
                           [self.lands[nm] for nm in names], [self.sems[nm][2] for nm in names], after)
        self.pos += 1
        return [t.reshape((N_DEV * t.shape[1],) + t.shape[2:]) for t in got]

    def grads(self, tag, full):
        names = list(full)
        srcs = [full[nm].reshape((N_DEV, full[nm].shape[0] // N_DEV) + full[nm].shape[1:]) for nm in names]
        lands = [_landing(lax.dynamic_index_in_dim(s, self.me, 0, keepdims=False), self.me) for s in srcs]
        send, recv, srcs, lands, token = _exchange_start(f"scatter_start_{tag}", srcs, lands, True, srcs[0])
        self.pending.append((tag, names, send, recv, srcs, lands))
        return token[0, 0]

    def collect(self, tags, after, keep_slots=()):
        out = {}
        for tag, names, send, recv, srcs, lands in self.pending:
            if tag not in tags:
                continue
            got = _exchange_wait(f"scatter_wait_{tag}", send, recv, srcs, lands, list(range(len(names))), True, after)
            for nm, slots in zip(names, got):
                out[nm] = slots if nm.startswith(tuple(keep_slots)) else _sum_slots(f"reduce_{nm}", slots)
        return out


def _adamw_math(w, gg, m, v):
    nm = ADAM_B1 * m + (1.0 - ADAM_B1) * gg
    nv = ADAM_B2 * v + (1.0 - ADAM_B2) * (gg * gg)
    bc1 = 1.0 - ADAM_B1 ** ADAM_STEP
    bc2 = 1.0 - ADAM_B2 ** ADAM_STEP
    return -ADAM_LR * ((nm / bc1) / (jnp.sqrt(nv / bc2) + ADAM_EPS) + ADAM_WD * w), nm, nv


def _adamw_part(name, i, w, slots, m, v, prev):
    n_parts, rows, cols = w.shape
    tr = _tile(rows, 256, 16)
    if prev is None:
        prev = tuple(lax.empty(w.shape, F32) for _ in range(4))

    def body(w_ref, g_ref, m_ref, v_ref, *rest):
        go_ref, d_ref, nm_ref, nv_ref = rest[4:]
        gg = g_ref[0].astype(F32)
        for sl in range(1, N_DEV):
            gg = gg + g_ref[sl].astype(F32)
        d, nm, nv = _adamw_math(w_ref[...], gg, m_ref[...], v_ref[...])
        go_ref[...] = gg
        d_ref[...] = d
        nm_ref[...] = nm
        nv_ref[...] = nv

    part = pl.BlockSpec((None, tr, cols), lambda r: (i, r, 0))
    return pl.pallas_call(
        body, name=name, grid=(rows // tr,),
        in_specs=[part, pl.BlockSpec((N_DEV, tr, cols), lambda r: (0, r, 0)), part, part] + [ANY_SPEC] * 4,
        out_specs=[part] * 4, out_shape=[SDS(w.shape, F32)] * 4,
        input_output_aliases={4 + k: k for k in range(4)}, compiler_params=_cparams(),
    )(w, slots, m, v, *prev)


WEIGHT_NAMES = ("c_ctx", "norm_g", "w_mod", "b_mod", "ffn_w_gate", "ffn_w_up", "ffn_w_down", "ab_w_in", "pool_w",
                "pool_scale", "q_norm_g", "w_uq", "kv_norm_g", "w_ukv", "ab_w_out", "conv_w_in", "conv_w",
                "conv_w_out", "final_norm_g")


def kernel(x, c, ctx, c_ctx, norm_g, w_mod, b_mod, ffn_w_gate, ffn_w_up, ffn_w_down, ab_w_in, pool_w, pool_scale, q_norm_g, w_uq, kv_norm_g, w_ukv, ab_w_out, conv_w_in, conv_w, conv_w_out, final_norm_g, loss_target, m_c_ctx, m_norm_g, m_w_mod, m_b_mod, m_ffn_w_gate, m_ffn_w_up, m_ffn_w_down, m_ab_w_in, m_pool_w, m_pool_scale, m_q_norm_g, m_w_uq, m_kv_norm_g, m_w_ukv, m_ab_w_out, m_conv_w_in, m_conv_w, m_conv_w_out, m_final_norm_g, v_c_ctx, v_norm_g, v_w_mod, v_b_mod, v_ffn_w_gate, v_ffn_w_up, v_ffn_w_down, v_ab_w_in, v_pool_w, v_pool_scale, v_q_norm_g, v_w_uq, v_kv_norm_g, v_w_ukv, v_ab_w_out, v_conv_w_in, v_conv_w, v_conv_w_out, v_final_norm_g):
    weights = (c_ctx, norm_g, w_mod, b_mod, ffn_w_gate, ffn_w_up, ffn_w_down, ab_w_in, pool_w, pool_scale, q_norm_g,
               w_uq, kv_norm_g, w_ukv, ab_w_out, conv_w_in, conv_w, conv_w_out, final_norm_g)
    moms = (m_c_ctx, m_norm_g, m_w_mod, m_b_mod, m_ffn_w_gate, m_ffn_w_up, m_ffn_w_down, m_ab_w_in, m_pool_w,
            m_pool_scale, m_q_norm_g, m_w_uq, m_kv_norm_g, m_w_ukv, m_ab_w_out, m_conv_w_in, m_conv_w, m_conv_w_out,
            m_final_norm_g)
    vels = (v_c_ctx, v_norm_g, v_w_mod, v_b_mod, v_ffn_w_gate, v_ffn_w_up, v_ffn_w_down, v_ab_w_in, v_pool_w,
            v_pool_scale, v_q_norm_g, v_w_uq, v_kv_norm_g, v_w_ukv, v_ab_w_out, v_conv_w_in, v_conv_w, v_conv_w_out,
            v_final_norm_g)
    me = 4 * lax.axis_index("x") + 2 * lax.axis_index("y") + lax.axis_index("c")
    n_lat, n_ctx = x.shape[1], ctx.shape[1]
    d = D_MODEL
    mod_cols = w_mod.shape[-1]
    ng_sh, cw_sh = norm_g.shape[-1], conv_w.shape[-1]

    def ffn_shards(i):
        return {f"gate_t{i}": ffn_w_gate[i // 2, i % 2].T, f"up_t{i}": ffn_w_up[i // 2, i % 2].T,
                f"down{i}": ffn_w_down[i // 2, i % 2]}

    local = {**ffn_shards(0), "in_t": ab_w_in[0].T, "uq": w_uq[0], "ukv_t": w_ukv[0].T, "ab_out": ab_w_out[0],
             **ffn_shards(1), **ffn_shards(2), "cin_t": conv_w_in[0].T, "c_out": conv_w_out[0], **ffn_shards(3)}
    ffn_groups = [[[f"gate_t{i}", f"up_t{i}"], [f"down{i}"]] for i in range(4)]
    groups = [*ffn_groups[0], ["in_t", "uq", "ukv_t", "ab_out"], *ffn_groups[1], *ffn_groups[2], ["cin_t", "c_out"],
              *ffn_groups[3]]
    feed = _Feed({nm: a.astype(BF16) for nm, a in local.items()}, groups, me)

    small = jnp.concatenate([c.reshape(-1), norm_g.reshape(-1), conv_w.reshape(-1)])
    small_n = -(-small.shape[0] // 1024) * 1024
    small = jnp.pad(small, (0, small_n - small.shape[0])).reshape(small_n // 128, 128)
    small_all = _exchange("gather_small", small, False).reshape(N_DEV, small_n)
    c_all = small_all[:, :d]
    o1 = d + 6 * ng_sh
    norm_g_full = small_all[:, d:o1].reshape(N_DEV, 2, 3, ng_sh).transpose(1, 2, 0, 3).reshape(2, 3, d)
    conv_w_full = small_all[:, o1:o1 + 3 * cw_sh].reshape(N_DEV, 3, cw_sh).transpose(1, 0, 2).reshape(3, d)

    cond = jnp.concatenate([c_all, jnp.broadcast_to(c_ctx[None, :], (N_DEV, d))], axis=0)
    sil, dsil = _silu_rows("mod_silu", cond)
    w_mod_b = w_mod.astype(BF16)
    b_sh = lax.dynamic_slice(b_mod, (0, me * mod_cols), (2, mod_cols))
    m_part = jnp.stack([_mm(f"mod_fwd{l}", [(sil, w_mod_b[l])], "nn", F32, 16, 384, bias=b_sh[l:l + 1])
                        for l in range(2)], axis=1)
    m_all = _exchange("gather_mod", m_part.reshape(-1, 128), False).reshape(N_DEV, 2 * N_DEV, 2, mod_cols)
    m_mine = lax.dynamic_index_in_dim(m_all, me, axis=1, keepdims=False)
    mod_h = m_mine.transpose(1, 0, 2).reshape(2, N_MOD, d)
    mod_g = m_all[:, N_DEV, 0, :].reshape(N_MOD, d)

    feed.start("all", [nm for grp in groups for nm in grp], m_all)
    feed.relay_first(feed.token)

    sq_cols, ds0, g = _local_step(x[0], ctx[0], loss_target[0], mod_h, mod_g, norm_g_full, feed, pool_w[0],
                                  pool_scale, q_norm_g, kv_norm_g, conv_w_full, final_norm_g)
    grad_x = ds0[:n_lat]
    loss = lax.psum(0.5 * jnp.sum(sq_cols) / d, ("x", "y", "c"))
    w_of, m_of, v_of = (dict(zip(WEIGHT_NAMES, t)) for t in (weights, moms, vels))
    results = {}

    def update(nm, grad, view=lambda t: t):
        outs = _adamw(f"adamw_{nm}", view(w_of[nm]), grad.reshape(view(w_of[nm]).shape), view(m_of[nm]), view(v_of[nm]))
        results[nm] = tuple(view(t) for t in (grad.reshape(view(w_of[nm]).shape), *outs))

    def swap(t):
        return jnp.swapaxes(t, -1, -2)

    stacked = ("gate_t", "up_t", "down")
    early = feed.collect(["l1f1", "l1m", "l1f0", "l0f1", "l0m"], [ds0], stacked)
    update("ab_w_in", early["in_t"], swap)
    update("w_uq", early["uq"])
    update("w_ukv", early["ukv_t"].T)
    update("ab_w_out", early["ab_out"])
    update("conv_w_in", early["cin_t"].T)
    update("conv_w_out", early["c_out"])
    ffn = {}
    for nm, prefix, view in (("ffn_w_gate", "gate_t", swap), ("ffn_w_up", "up_t", swap),
                             ("ffn_w_down", "down", lambda t: t)):
        w4, m4, v4 = (view(t).reshape((4,) + view(t).shape[-2:]) for t in (w_of[nm], m_of[nm], v_of[nm]))
        prev = None
        for i in (3, 2, 1):
            prev = _adamw_part(f"adamw_{nm}{i}", i, w4, early[f"{prefix}{i}"], m4, v4, prev)
        ffn[nm] = (prefix, view, w4, m4, v4, prev)
    done_early = [results[nm][1] for nm in results] + [state[5][1] for state in ffn.values()]
    late = feed.collect(["l0f0"], done_early, stacked)
    for nm, (prefix, view, w4, m4, v4, prev) in ffn.items():
        outs = _adamw_part(f"adamw_{nm}0", 0, w4, late[f"{prefix}0"], m4, v4, prev)
        results[nm] = tuple(view(t.reshape(view(w_of[nm]).shape)) for t in outs)

    dm = jnp.stack([g["mod_h"], jnp.stack([g["mod_g"], jnp.zeros_like(g["mod_g"])])])
    dm_all = _exchange("gather_dmod", dm.reshape(-1, 128), False, results["ffn_w_down"][1]).reshape(N_DEV, 2, 2, N_MOD * d)
    grad_b_mod = _sum_rows("dmod_bias", dm_all.reshape(2 * N_DEV, 2 * N_MOD * d)).reshape(2, N_MOD * d)
    dm_sh = lax.dynamic_slice(dm_all, (0, 0, 0, me * mod_cols), (N_DEV, 2, 2, mod_cols))
    gw_mod, cctx_parts = [], []
    for l in range(2):
        dm_l = dm_sh[:, :, l, :].transpose(1, 0, 2).reshape(2 * N_DEV, mod_cols).astype(BF16)
        gw_mod.append(_mm(f"mod_dw{l}", [(sil, dm_l)], "tn", F32, 512, 384))
        dm_ctx = jnp.concatenate([dm_l[N_DEV:], jnp.zeros((N_DEV, mod_cols), BF16)], axis=0)
        cctx_parts.append(_mm(f"mod_dcond{l}", [(dm_ctx, w_mod_b[l])], "nt", F32, 16, 512))
    cctx_part = _sum_rows("mod_dcond_sum", jnp.concatenate(cctx_parts, axis=0))
    update("w_mod", jnp.stack(gw_mod))
    update("b_mod", grad_b_mod)

    small_g = jnp.concatenate([g["pool_w"].reshape(-1), g["pool_scale"].reshape(-1), g["q_norm_g"].reshape(-1),
                               g["kv_norm_g"].reshape(-1), g["final_norm_g"].reshape(-1), g["norm_g"].reshape(-1),
                               g["conv_w"].reshape(-1), cctx_part.reshape(-1)])
    sizes = [pool_w.size, pool_scale.size, q_norm_g.size, kv_norm_g.size, d, 6 * d, 3 * d, d]
    sg_n = -(-small_g.shape[0] // 1024) * 1024
    small_g = jnp.pad(small_g, (0, sg_n - small_g.shape[0]))
    sg_all = _exchange("gather_small_grads", small_g.reshape(-1, 128), False).reshape(N_DEV, sg_n)
    scale_vec = jnp.concatenate([jnp.ones((1, sum(sizes[:-1])), F32), dsil[N_DEV:N_DEV + 1],
                                 jnp.ones((1, sg_n - sum(sizes)), F32)], axis=1)
    sg = _sum_rows("small_grads_sum", sg_all, scale_vec)[0]
    cuts, pos = [], 0
    for sz in sizes:
        cuts.append(sg[pos:pos + sz])
        pos += sz
    g_pool_w, g_pool_scale, g_q_norm, g_kv_norm, g_final, g_norm_full, g_conv_full, g_c_ctx = cuts
    update("c_ctx", g_c_ctx)
    update("norm_g", lax.dynamic_slice(g_norm_full.reshape(2, 3, d), (0, 0, me * ng_sh), (2, 3, ng_sh)))
    update("conv_w", lax.dynamic_slice(g_conv_full.reshape(3, d), (0, me * cw_sh), (3, cw_sh)))
    update("pool_w", g_pool_w)
    update("pool_scale", g_pool_scale)
    update("q_norm_g", g_q_norm)
    update("kv_norm_g", g_kv_norm)
    update("final_norm_g", g_final)
    outs = [results[nm] for nm in WEIGHT_NAMES]
    return (loss, grad_x[None], *[o[0] for o in outs], *[o[1] for o in outs], *[o[2] for o in outs],
            *[o[3] for o in outs])
```

```python
import functools
import math

import jax
import jax.numpy as jnp
import numpy as np
from jax import lax
from jax.experimental import pallas as pl
from jax.experimental.pallas import tpu as pltpu

F32 = jnp.float32
BF16 = jnp.bfloat16
MESH = pl.DeviceIdType.MESH
SDS = jax.ShapeDtypeStruct

N_DEV = 8
D_MODEL = 1024
N_MOD = 9
D_FF = 2816
POOL_WINDOWS = (2, 4, 8, 16)
POOL_DIM = 512
POOL_GROUP_DIM = 128
HEADS = 8
QK_NOPE = 64
QK_ROPE = 32
QK_HEAD = QK_NOPE + QK_ROPE
V_HEAD = 64
Q_RANK = 768
KV_RANK = 256
GRID_W = 64
ROPE_THETA = 10000.0
RMS_EPS = 1e-6
ATTN_SCALE = 1.0 / math.sqrt(QK_HEAD)
HEAD_PAD = 128
POOL_PAD = 16
PA_POOL, PA_CQ, PA_KV = 0, 768, 1536
PA_KV_W = 384
PA_W = PA_KV + PA_KV_W

ADAM_LR, ADAM_B1, ADAM_B2, ADAM_EPS, ADAM_WD, ADAM_STEP = 0.001, 0.9, 0.999, 1e-08, 0.01, 10

VMEM_LIMIT_BYTES = 56 * 1024 * 1024

NN = ((1,), (0,))
NT = ((1,), (1,))
TN = ((0,), (0,))


def _cparams():
    return pltpu.CompilerParams(vmem_limit_bytes=VMEM_LIMIT_BYTES)


def _dot(a, b, dims):
    return lax.dot_general(a, b, (dims, ((), ())), preferred_element_type=F32)


def _tile(n, cap, mult=8):
    t = (min(cap, n) // mult) * mult
    while t >= mult:
        if n % t == 0:
            return t
        t -= mult
    return n


def _colsum(x):
    return jnp.sum(x, axis=0, keepdims=True)


def _rms(x):
    r = lax.rsqrt(jnp.mean(x * x, axis=-1, keepdims=True) + RMS_EPS)
    return x * r, r


def _rms_bwd(n, r, dn):
    return r * (dn - n * jnp.mean(dn * n, axis=-1, keepdims=True))


def _rowwise(name, fn, t_rows, tm, n_lat, rows, vecs, outs, accs):
    nt = t_rows // tm
    nlt = n_lat // tm
    n_groups = 2 if nlt < nt else 1

    def grp(i):
        return jnp.where(i >= nlt, 1, 0) if n_groups == 2 else 0

    in_specs = [pl.BlockSpec((tm, w), functools.partial(lambda i, cb: (i, cb), cb=cb)) for (_, w, cb) in rows]
    in_specs += [pl.BlockSpec((1,) + v.shape[1:], lambda i: (grp(i), 0, 0)) for v in vecs]
    out_specs = [pl.BlockSpec((tm, w), lambda i: (i, 0)) for (w, _) in outs]
    out_specs += [pl.BlockSpec((1, 1, w), lambda i: (grp(i), 0, 0)) for w in accs]
    out_shape = [SDS((t_rows, w), dt) for (w, dt) in outs] + [SDS((n_groups, 1, w), F32) for w in accs]
    n_r, n_v, n_o = len(rows), len(vecs), len(outs)

    def body(*refs):
        row_vals = [r[...] for r in refs[:n_r]]
        vec_vals = [v[0] for v in refs[n_r:n_r + n_v]]
        out_refs = refs[n_r + n_v:n_r + n_v + n_o]
        acc_refs = refs[n_r + n_v + n_o:]
        out_vals, acc_vals = fn(row_vals, vec_vals)
        for o_ref, o in zip(out_refs, out_vals):
            o_ref[...] = o.astype(o_ref.dtype)
        if acc_refs:
            i = pl.program_id(0)
            first = (i == 0) | (i == nlt) if n_groups == 2 else i == 0

            @pl.when(first)
            def _():
                for a_ref, a in zip(acc_refs, acc_vals):
                    a_ref[0] = a

            @pl.when(jnp.logical_not(first))
            def _():
                for a_ref, a in zip(acc_refs, acc_vals):
                    a_ref[0] += a

    res = pl.pallas_call(
        body, name=name, grid=(nt,), in_specs=in_specs, out_specs=out_specs, out_shape=out_shape,
        compiler_params=_cparams(),
    )(*[r[0] for r in rows], *vecs)
    return res[:n_o], res[n_o:]


RESIDENT_BYTES = 12 * 1024 * 1024


def _mm(name, pairs, mode, out_dtype, tm_cap=256, tn_cap=512, bias=None):
    a0, b0 = pairs[0]
    if mode == "nn":
        m, n, dims = a0.shape[0], b0.shape[1], NN
    elif mode == "nt":
        m, n, dims = a0.shape[0], b0.shape[0], NT
    else:
        m, n, dims = a0.shape[1], b0.shape[1], TN
    b_bytes = sum(b.size * b.dtype.itemsize for _, b in pairs)
    tn = n if b_bytes <= RESIDENT_BYTES else _tile(n, tn_cap, 128)
    tm = _tile(m, tm_cap, 128 if mode == "tn" else 16)

    def a_spec(a):
        if mode == "tn":
            return pl.BlockSpec((a.shape[0], tm), lambda i, j: (0, i))
        return pl.BlockSpec((tm, a.shape[1]), lambda i, j: (i, 0))

    def b_spec(b):
        if mode == "nt":
            return pl.BlockSpec((tn, b.shape[1]), lambda i, j: (j, 0))
        return pl.BlockSpec((b.shape[0], tn), lambda i, j: (0, j))

    in_specs, flat = [], []
    for a, b in pairs:
        in_specs += [a_spec(a), b_spec(b)]
        flat += [a, b]
    if bias is not None:
        in_specs.append(pl.BlockSpec((1, tn), lambda i, j: (0, j)))
        flat.append(bias)
    n_pairs = len(pairs)

    def body(*refs):
        acc = None
        for p in range(n_pairs):
            t = _dot(refs[2 * p][...], refs[2 * p + 1][...], dims)
            acc = t if acc is None else acc + t
        if bias is not None:
            acc = acc + refs[2 * n_pairs][...]
        refs[-1][...] = acc.astype(refs[-1].dtype)

    return pl.pallas_call(
        body, name=name, grid=(m // tm, n // tn), in_specs=in_specs,
        out_specs=pl.BlockSpec((tm, tn), lambda i, j: (i, j)),
        out_shape=SDS((m, n), out_dtype), compiler_params=_cparams(),
    )(*flat)


def _mm_resid(name, a, b, s, mg, k, coef, n_lat):
    t_rows, n = a.shape[0], b.shape[1]
    tm = _tile(math.gcd(n_lat, t_rows), 256, 16)
    nlt = n_lat // tm
    n_groups = 2 if nlt < t_rows // tm else 1

    def grp(i):
        return jnp.where(i >= nlt, 1, 0) if n_groups == 2 else 0

    def body(a_ref, b_ref, s_ref, mg_ref, so_ref, o_ref):
        o = _dot(a_ref[...], b_ref[...], NN)
        gate = mg_ref[0, 3 * k + 2:3 * k + 3, :]
        o_ref[...] = o.astype(BF16)
        so_ref[...] = s_ref[...] + (coef * gate) * o

    row = pl.BlockSpec((tm, n), lambda i: (i, 0))
    return pl.pallas_call(
        body, name=name, grid=(t_rows // tm,),
        in_specs=[pl.BlockSpec((tm, a.shape[1]), lambda i: (i, 0)), pl.BlockSpec(b.shape, lambda i: (0, 0)), row,
                  pl.BlockSpec((1, mg.shape[1], n), lambda i: (grp(i), 0, 0))],
        out_specs=[row, row], out_shape=[SDS((t_rows, n), F32), SDS((t_rows, n), BF16)], compiler_params=_cparams(),
    )(a, b, s, mg)


def _groups(t_rows, tm, n_lat):
    nlt = n_lat // tm
    if nlt < t_rows // tm:
        return 2, (lambda i: jnp.where(i >= nlt, 1, 0)), (lambda i: (i == 0) | (i == nlt))
    return 1, (lambda i: 0), (lambda i: i == 0)


def _accumulate(acc_refs, vals, first):
    @pl.when(first)
    def _():
        for r, v in zip(acc_refs, vals):
            r[0] = v

    @pl.when(jnp.logical_not(first))
    def _():
        for r, v in zip(acc_refs, vals):
            r[0] += v


def _adaln_math(s, m, k):
    n, _ = _rms(s)
    return (n * m[9 + k:10 + k]) * (1.0 + m[3 * k + 1:3 * k + 2]) + m[3 * k:3 * k + 1]


def _ffn_up(name, s, mg, k, n_lat, wg_t, wu_t):
    t_rows, f = s.shape[0], wg_t.shape[0]
    tm = _row_tm(t_rows, n_lat)
    _, grp, _ = _groups(t_rows, tm, n_lat)

    def body(s_ref, mg_ref, wg_ref, wu_ref, u_ref, a_ref, b_ref, h_ref):
        uu = _adaln_math(s_ref[...], mg_ref[0], k).astype(BF16)
        u_ref[...] = uu
        a = _dot(uu, wg_ref[...], NT)
        b = _dot(uu, wu_ref[...], NT)
        sg = jax.nn.sigmoid(a)
        act = a * sg
        a_ref[...] = (b * (sg * (1.0 + a * (1.0 - sg)))).astype(BF16)
        b_ref[...] = act.astype(BF16)
        h_ref[...] = (act * b).astype(BF16)

    w_spec = pl.BlockSpec(wg_t.shape, lambda i: (0, 0))
    o_spec = pl.BlockSpec((tm, f), lambda i: (i, 0))
    row = pl.BlockSpec((tm, s.shape[1]), lambda i: (i, 0))
    return pl.pallas_call(
        body, name=name, grid=(t_rows // tm,),
        in_specs=[row, pl.BlockSpec((1,) + mg.shape[1:], lambda i: (grp(i), 0, 0)), w_spec, w_spec],
        out_specs=[row, o_spec, o_spec, o_spec],
        out_shape=[SDS(s.shape, BF16)] + [SDS((t_rows, f), BF16)] * 3, compiler_params=_cparams(),
    )(s, mg, wg_t, wu_t)


def _ffn_dact(name, ds_out, o, mg, k, coef, n_lat, wd, a, b):
    t_rows, f = ds_out.shape[0], wd.shape[0]
    tm = _row_tm(t_rows, n_lat)
    n_groups, grp, first = _groups(t_rows, tm, n_lat)
    d = ds_out.shape[1]

    def body(ds_ref, o_ref, mg_ref, wd_ref, a_ref, b_ref, do_ref, da_ref, db_ref, dg_ref):
        dd = coef * ds_ref[...]
        do = (dd * mg_ref[0, 3 * k + 2:3 * k + 3, :]).astype(BF16)
        do_ref[...] = do
        _accumulate([dg_ref], [_colsum(dd * o_ref[...].astype(F32))], first(pl.program_id(0)))
        dh = _dot(do, wd_ref[...], NT)
        da_ref[...] = (dh * a_ref[...].astype(F32)).astype(BF16)
        db_ref[...] = (dh * b_ref[...].astype(F32)).astype(BF16)

    row = pl.BlockSpec((tm, d), lambda i: (i, 0))
    t_spec = pl.BlockSpec((tm, f), lambda i: (i, 0))
    return pl.pallas_call(
        body, name=name, grid=(t_rows // tm,),
        in_specs=[row, row, pl.BlockSpec((1,) + mg.shape[1:], lambda i: (grp(i), 0, 0)),
                  pl.BlockSpec(wd.shape, lambda i: (0, 0)), t_spec, t_spec],
        out_specs=[row, t_spec, t_spec, pl.BlockSpec((1, 1, d), lambda i: (grp(i), 0, 0))],
        out_shape=[SDS((t_rows, d), BF16), SDS((t_rows, f), BF16), SDS((t_rows, f), BF16), SDS((n_groups, 1, d), F32)],
        compiler_params=_cparams(),
    )(ds_out, o, mg, wd, a, b)


def _du_adaln(name, pairs, s, ds_out, mg, k, n_lat, after):
    t_rows, d = s.shape
    tm = _row_tm(t_rows, n_lat)
    n_groups, grp, first = _groups(t_rows, tm, n_lat)
    n_pairs = len(pairs)

    def body(*refs):
        s_ref, ds_ref, mg_ref, z_ref, out_ref, dsh_ref, dsc_ref, dgn_ref = refs[2 * n_pairs:]
        d_u = z_ref[...]
        for p in range(n_pairs):
            d_u = d_u + _dot(refs[p][...], refs[n_pairs + p][...], NN)
        m = mg_ref[0]
        gain, scale = m[9 + k:10 + k], m[3 * k + 1:3 * k + 2]
        n, r = _rms(s_ref[...])
        dxn = d_u * (1.0 + scale)
        out_ref[...] = ds_ref[...] + _rms_bwd(n, r, dxn * gain)
        _accumulate([dsh_ref, dsc_ref, dgn_ref], [_colsum(d_u), _colsum(d_u * (n * gain)), _colsum(dxn * n)],
                    first(pl.program_id(0)))

    row = pl.BlockSpec((tm, d), lambda i: (i, 0))
    acc = pl.BlockSpec((1, 1, d), lambda i: (grp(i), 0, 0))
    res = pl.pallas_call(
        body, name=name, grid=(t_rows // tm,),
        in_specs=[pl.BlockSpec((tm, a.shape[1]), lambda i: (i, 0)) for a, _ in pairs]
        + [pl.BlockSpec(w.shape, lambda i: (0, 0)) for _, w in pairs]
        + [row, row, pl.BlockSpec((1,) + mg.shape[1:], lambda i: (grp(i), 0, 0)), pl.BlockSpec((1, d), lambda i: (0, 0))],
        out_specs=[row, acc, acc, acc],
        out_shape=[SDS((t_rows, d), F32)] + [SDS((n_groups, 1, d), F32)] * 3, compiler_params=_cparams(),
    )(*[a for a, _ in pairs], *[w for _, w in pairs], s, ds_out, mg, after)
    return res[0], res[1:]


def _adaln_mm(name, s, mg, k, n_lat, w_t):
    rows, d = s.shape
    tm = _row_tm(rows, n_lat)
    _, grp, _ = _groups(rows, tm, n_lat)
    n = w_t.shape[0]

    def body(s_ref, mg_ref, w_ref, u_ref, y_ref):
        uu = _adaln_math(s_ref[...], mg_ref[0], k).astype(BF16)
        u_ref[...] = uu
        y_ref[...] = _dot(uu, w_ref[...], NT)

    row = pl.BlockSpec((tm, d), lambda i: (i, 0))
    return pl.pallas_call(
        body, name=name, grid=(rows // tm,),
        in_specs=[row, pl.BlockSpec((1,) + mg.shape[1:], lambda i: (grp(i), 0, 0)), pl.BlockSpec(w_t.shape, lambda i: (0, 0))],
        out_specs=[row, pl.BlockSpec((tm, n), lambda i: (i, 0))],
        out_shape=[SDS((rows, d), BF16), SDS((rows, n), F32)], compiler_params=_cparams(),
    )(s, mg, w_t)


def _gate_mm(name, ds_out, o, mg, k, coef, n_lat, w):
    t_rows, d = ds_out.shape
    tm = _row_tm(t_rows, n_lat)
    n_groups, grp, first = _groups(t_rows, tm, n_lat)
    n = w.shape[0]

    def body(ds_ref, o_ref, mg_ref, w_ref, do_ref, y_ref, dg_ref):
        dd = coef * ds_ref[...]
        do = (dd * mg_ref[0, 3 * k + 2:3 * k + 3, :]).astype(BF16)
        do_ref[...] = do
        _accumulate([dg_ref], [_colsum(dd * o_ref[...].astype(F32))], first(pl.program_id(0)))
        y_ref[...] = _dot(do, w_ref[...], NT)

    row = pl.BlockSpec((tm, d), lambda i: (i, 0))
    return pl.pallas_call(
        body, name=name, grid=(t_rows // tm,),
        in_specs=[row, row, pl.BlockSpec((1,) + mg.shape[1:], lambda i: (grp(i), 0, 0)), pl.BlockSpec(w.shape, lambda i: (0, 0))],
        out_specs=[row, pl.BlockSpec((tm, n), lambda i: (i, 0)), pl.BlockSpec((1, 1, d), lambda i: (grp(i), 0, 0))],
        out_shape=[SDS((t_rows, d), BF16), SDS((t_rows, n), F32), SDS((n_groups, 1, d), F32)],
        compiler_params=_cparams(),
    )(ds_out, o, mg, w)


def _row_tm(t_rows, n_lat):
    return _tile(math.gcd(t_rows, n_lat), 256, 16)


def _rmsnorm_fwd(name, x, width, colblk, gain, t_rows):
    def fn(rv, vv):
        n, _ = _rms(rv[0])
        return [n * vv[0]], []

    (y,), _ = _rowwise(name, fn, t_rows, _tile(t_rows, 256, 16), t_rows, [(x, width, colblk)],
                       [gain.reshape(1, 1, width)], [(width, BF16)], [])
    return y


def _rmsnorm_bwd(name, x, width, colblk, dy, gain, t_rows):
    def fn(rv, vv):
        n, r = _rms(rv[0])
        return [_rms_bwd(n, r, rv[1] * vv[0])], [_colsum(rv[1] * n)]

    (dx,), (dgain,) = _rowwise(name, fn, t_rows, _tile(t_rows, 256, 16), t_rows,
                               [(x, width, colblk), (dy, width, 0)], [gain.reshape(1, 1, width)],
                               [(width, F32)], [width])
    return dx, dgain


def _final_loss(name, h, target, gain):
    t_rows = h.shape[0]
    inv_d = 1.0 / D_MODEL

    def fn(rv, vv):
        g = vv[0]
        n, r = _rms(rv[0])
        e = n * g - rv[1]
        dy = e * inv_d
        return [_rms_bwd(n, r, dy * g)], [_colsum(e * e), _colsum(dy * n)]

    (dh,), (sq, dgain) = _rowwise(name, fn, t_rows, _tile(t_rows, 256, 16), t_rows,
                                  [(h, D_MODEL, 0), (target, D_MODEL, 0)], [gain.reshape(1, 1, D_MODEL)],
                                  [(D_MODEL, F32)], [D_MODEL, D_MODEL])
    return dh, sq, dgain


def _rope(name, z, width, colblk, cos, sin, perm, backward, out_dtype):
    t_rows = cos.shape[0]

    def body(z_ref, c_ref, s_ref, p_ref, o_ref):
        zz = z_ref[...]
        pre = zz * s_ref[...] if backward else zz
        hi = pre.astype(BF16)
        lo = (pre - hi.astype(F32)).astype(BF16)
        rot = _dot(hi, p_ref[...], NN) + _dot(lo, p_ref[...], NN)
        if not backward:
            rot = rot * s_ref[...]
        o_ref[...] = (zz * c_ref[...] + rot).astype(o_ref.dtype)

    tm = _tile(t_rows, 256, 16)
    t_spec = pl.BlockSpec((tm, width), lambda i: (i, 0))
    return pl.pallas_call(
        body, name=name, grid=(t_rows // tm,),
        in_specs=[pl.BlockSpec((tm, width), lambda i: (i, colblk)), t_spec, t_spec,
                  pl.BlockSpec((width, width), lambda i: (0, 0))],
        out_specs=t_spec, out_shape=SDS((t_rows, width), out_dtype), compiler_params=_cparams(),
    )(z, cos, sin, perm)


def _window_sum(x, w, transposed):
    n_rows = x.shape[0]
    zeros = jnp.zeros((POOL_PAD, x.shape[1]), F32)
    y = jnp.concatenate([zeros, x, zeros], axis=0)
    total = n_rows + 2 * POOL_PAD
    if transposed:
        y = y + pltpu.roll(y, total - 1, 0)
    else:
        y = y + pltpu.roll(y, 1, 0)
    step = 1
    while 2 * step < w:
        y = pltpu.roll(y, step, 0) + pltpu.roll(y, total - step, 0)
        step *= 2
    return y[POOL_PAD:POOL_PAD + n_rows]


def _window_count(n_rows, w):
    t = lax.broadcasted_iota(jnp.int32, (n_rows, 1), 0)
    lo = jnp.maximum(t - w // 2, 0)
    hi = jnp.minimum(t + (w - w // 2 - 1), n_rows - 1)
    return (hi - lo + 1).astype(F32)


def _pool_fwd(name, proj, n_rows, w_grp, scale):
    def body(x_ref, w_ref, sc_ref, y_ref, p_ref):
        for g, w in enumerate(POOL_WINDOWS):
            cols = slice(g * POOL_GROUP_DIM, (g + 1) * POOL_GROUP_DIM)
            x = x_ref[:, cols]
            p = _window_sum(x, w, False) * (1.0 / _window_count(n_rows, w)) - x
            pb = p.astype(BF16)
            p_ref[:, cols] = pb
            y_ref[:, cols] = (_dot(pb, w_ref[g], NN) * sc_ref[:, cols]).astype(BF16)

    blk = pl.BlockSpec((n_rows, POOL_DIM), lambda i: (0, 0))
    return pl.pallas_call(
        body, name=name, grid=(1,),
        in_specs=[blk, pl.BlockSpec(w_grp.shape, lambda i: (0, 0, 0)), pl.BlockSpec((1, POOL_DIM), lambda i: (0, 0))],
        out_specs=[blk, blk], out_shape=[SDS((n_rows, POOL_DIM), BF16)] * 2, compiler_params=_cparams(),
    )(proj, w_grp, scale)


def _pool_bwd(name, dcat, n_rows, p, w_grp, scale):
    def body(dy_ref, p_ref, w_ref, sc_ref, dx_ref, dw_ref, dsc_ref):
        for g, w in enumerate(POOL_WINDOWS):
            cols = slice(g * POOL_GROUP_DIM, (g + 1) * POOL_GROUP_DIM)
            dy = dy_ref[:, cols]
            pb = p_ref[:, cols]
            pw = _dot(pb, w_ref[g], NN)
            dsc_ref[:, cols] = _colsum(dy * pw)
            dpw = (dy * sc_ref[:, cols]).astype(BF16)
            dw_ref[g] = _dot(pb, dpw, TN)
            dp = _dot(dpw, w_ref[g], NT)
            dx_ref[:, cols] = _window_sum(dp * (1.0 / _window_count(n_rows, w)), w, True) - dp

    blk = pl.BlockSpec((n_rows, POOL_DIM), lambda i: (0, 0))
    w_spec = pl.BlockSpec(w_grp.shape, lambda i: (0, 0, 0))
    v_spec = pl.BlockSpec((1, POOL_DIM), lambda i: (0, 0))
    return pl.pallas_call(
        body, name=name, grid=(1,), in_specs=[blk, blk, w_spec, v_spec], out_specs=[blk, w_spec, v_spec],
        out_shape=[SDS((n_rows, POOL_DIM), F32), SDS(w_grp.shape, F32), SDS((1, POOL_DIM), F32)],
        compiler_params=_cparams(),
    )(dcat, p, w_grp, scale)


def _attn_fwd(name, q, k, v):
    h, n_q, _ = q.shape
    n_k = k.shape[1]
    tq = _tile(n_q, 256, 16)

    def body(q_ref, k_ref, v_ref, o_ref, lse_ref):
        s = _dot(q_ref[...], k_ref[...], NT) * ATTN_SCALE
        m = jnp.max(s, axis=-1, keepdims=True)
        e = jnp.exp(s - m)
        l = jnp.sum(e, axis=-1, keepdims=True)
        p = (e * (1.0 / l)).astype(BF16)
        o_ref[...] = _dot(p, v_ref[...], NN).astype(BF16)
        lse_ref[...] = m + jnp.log(l)

    return pl.pallas_call(
        body, name=name, grid=(h, n_q // tq),
        in_specs=[pl.BlockSpec((None, tq, HEAD_PAD), lambda hh, i: (hh, i, 0)),
                  pl.BlockSpec((None, n_k, HEAD_PAD), lambda hh, i: (hh, 0, 0)),
                  pl.BlockSpec((None, n_k, V_HEAD), lambda hh, i: (hh, 0, 0))],
        out_specs=[pl.BlockSpec((None, tq, V_HEAD), lambda hh, i: (hh, i, 0)),
                   pl.BlockSpec((None, tq, 1), lambda hh, i: (hh, i, 0))],
        out_shape=[SDS((h, n_q, V_HEAD), BF16), SDS((h, n_q, 1), F32)], compiler_params=_cparams(),
    )(q, k, v)


def _attn_bwd(name, q, k, v, o, lse, do):
    h, n_q, _ = q.shape
    n_k = k.shape[1]
    tq = _tile(n_q, 256, 16)

    def body(q_ref, k_ref, v_ref, o_ref, lse_ref, do_ref, dq_ref, dk_ref, dv_ref, dks_ref):
        hh, i = pl.program_id(0), pl.program_id(1)
        qq, kk, dd = q_ref[...], k_ref[...], do_ref[...]
        s = _dot(qq, kk, NT) * ATTN_SCALE
        p = jnp.exp(s - lse_ref[...])
        dp = _dot(dd, v_ref[...], NT)
        delta = jnp.sum(dd.astype(F32) * o_ref[...].astype(F32), axis=-1, keepdims=True)
        ds = (p * (dp - delta) * ATTN_SCALE).astype(BF16)
        dq_ref[...] = _dot(ds, kk, NN)
        dk = _dot(ds, qq, TN)
        dv = _dot(p.astype(BF16), dd, TN)

        @pl.when(i == 0)
        def _():
            dk_ref[...] = dk
            dv_ref[...] = dv

        @pl.when(i > 0)
        def _():
            dk_ref[...] += dk
            dv_ref[...] += dv

        @pl.when((i == 0) & (hh == 0))
        def _():
            dks_ref[...] = dk

        @pl.when((i > 0) | (hh > 0))
        def _():
            dks_ref[...] += dk

    q_spec = pl.BlockSpec((None, tq, HEAD_PAD), lambda hh, i: (hh, i, 0))
    k_spec = pl.BlockSpec((None, n_k, HEAD_PAD), lambda hh, i: (hh, 0, 0))
    v_spec = pl.BlockSpec((None, n_k, V_HEAD), lambda hh, i: (hh, 0, 0))
    o_spec = pl.BlockSpec((None, tq, V_HEAD), lambda hh, i: (hh, i, 0))
    return pl.pallas_call(
        body, name=name, grid=(h, n_q // tq),
        in_specs=[q_spec, k_spec, v_spec, o_spec, pl.BlockSpec((None, tq, 1), lambda hh, i: (hh, i, 0)), o_spec],
        out_specs=[q_spec, k_spec, v_spec, pl.BlockSpec((n_k, HEAD_PAD), lambda hh, i: (0, 0))],
        out_shape=[SDS((h, n_q, HEAD_PAD), F32), SDS((h, n_k, HEAD_PAD), F32), SDS((h, n_k, V_HEAD), F32),
                   SDS((n_k, HEAD_PAD), F32)],
        compiler_params=_cparams(),
    )(q, k, v, o, lse, do)


CONV_COLS = 256


def _shift_rows(x, d):
    n_rows = x.shape[0]
    t = lax.broadcasted_iota(jnp.int32, (n_rows, 1), 0)
    if d > 0:
        return jnp.where(t >= d, pltpu.roll(x, d, 0), 0.0)
    return jnp.where(t < n_rows + d, pltpu.roll(x, n_rows + d, 0), 0.0)


def _conv_fwd(name, z3, conv_w):
    n_rows = z3.shape[0]
    nb = D_MODEL // CONV_COLS

    def body(b_ref, c_ref, v_ref, w_ref, y_ref):
        z = c_ref[...] * v_ref[...]
        zc = w_ref[0:1, :] * _shift_rows(z, 1) + w_ref[1:2, :] * z + w_ref[2:3, :] * _shift_rows(z, -1)
        y_ref[...] = (b_ref[...] * zc).astype(BF16)

    def part(k):
        return pl.BlockSpec((n_rows, CONV_COLS), lambda j: (0, k * nb + j))

    return pl.pallas_call(
        body, name=name, grid=(nb,),
        in_specs=[part(0), part(1), part(2), pl.BlockSpec((3, CONV_COLS), lambda j: (0, j))],
        out_specs=pl.BlockSpec((n_rows, CONV_COLS), lambda j: (0, j)),
        out_shape=SDS((n_rows, D_MODEL), BF16), compiler_params=_cparams(),
    )(z3, z3, z3, conv_w)


def _conv_bwd(name, dy, z3, conv_w):
    n_rows = z3.shape[0]
    nb = D_MODEL // CONV_COLS

    def body(dy_ref, b_ref, c_ref, v_ref, w_ref, db_ref, dc_ref, dv_ref, dw_ref):
        c, v, d_y = c_ref[...], v_ref[...], dy_ref[...]
        z = c * v
        z_dn, z_up = _shift_rows(z, 1), _shift_rows(z, -1)
        zc = w_ref[0:1, :] * z_dn + w_ref[1:2, :] * z + w_ref[2:3, :] * z_up
        db_ref[...] = (d_y * zc).astype(BF16)
        dzc = d_y * b_ref[...]
        dz = w_ref[0:1, :] * _shift_rows(dzc, -1) + w_ref[1:2, :] * dzc + w_ref[2:3, :] * _shift_rows(dzc, 1)
        dc_ref[...] = (dz * v).astype(BF16)
        dv_ref[...] = (dz * c).astype(BF16)
        dw_ref[0:1, :] = _colsum(dzc * z_dn)
        dw_ref[1:2, :] = _colsum(dzc * z)
        dw_ref[2:3, :] = _colsum(dzc * z_up)

    def part(k):
        return pl.BlockSpec((n_rows, CONV_COLS), lambda j: (0, k * nb + j))

    col = pl.BlockSpec((n_rows, CONV_COLS), lambda j: (0, j))
    w_spec = pl.BlockSpec((3, CONV_COLS), lambda j: (0, j))
    return pl.pallas_call(
        body, name=name, grid=(nb,), in_specs=[col, part(0), part(1), part(2), w_spec],
        out_specs=[col, col, col, w_spec],
        out_shape=[SDS((n_rows, D_MODEL), BF16)] * 3 + [SDS((3, D_MODEL), F32)], compiler_params=_cparams(),
    )(dy, z3, z3, z3, conv_w)


def _silu_rows(name, x):
    def body(x_ref, s_ref, d_ref):
        xx = x_ref[...]
        sg = jax.nn.sigmoid(xx)
        s_ref[...] = (xx * sg).astype(BF16)
        d_ref[...] = sg * (1.0 + xx * (1.0 - sg))

    return pl.pallas_call(body, name=name, out_shape=[SDS(x.shape, BF16), SDS(x.shape, F32)])(x)


def _sum_rows(name, x, scale=None):
    r, n = x.shape
    tn = _tile(n, 8192, 128)

    def body(*refs):
        acc = jnp.sum(refs[0][...].astype(F32), axis=0, keepdims=True)
        if scale is not None:
            acc = acc * refs[1][...]
        refs[-1][...] = acc

    in_specs = [pl.BlockSpec((r, tn), lambda j: (0, j))]
    args = [x]
    if scale is not None:
        in_specs.append(pl.BlockSpec((1, tn), lambda j: (0, j)))
        args.append(scale)
    return pl.pallas_call(body, name=name, grid=(n // tn,), in_specs=in_specs,
                          out_specs=pl.BlockSpec((1, tn), lambda j: (0, j)), out_shape=SDS((1, n), F32))(*args)


def _sum_slots(name, x):
    n_slots, r, c = x.shape
    tr = _tile(r, 432, 16)

    def body(x_ref, o_ref):
        acc = x_ref[0].astype(F32)
        for sl in range(1, n_slots):
            acc = acc + x_ref[sl].astype(F32)
        o_ref[...] = acc

    return pl.pallas_call(body, name=name, grid=(r // tr,),
                          in_specs=[pl.BlockSpec((n_slots, tr, c), lambda i: (0, i, 0))],
                          out_specs=pl.BlockSpec((tr, c), lambda i: (i, 0)), out_shape=SDS((r, c), F32),
                          compiler_params=_cparams())(x)


def _adamw(name, w, g, m, v):
    shape = w.shape
    cols = shape[-1]
    rows = w.size // cols
    tr = _tile(rows, 512, 8)
    bc1 = 1.0 - ADAM_B1 ** ADAM_STEP
    bc2 = 1.0 - ADAM_B2 ** ADAM_STEP

    def body(w_ref, g_ref, m_ref, v_ref, d_ref, nm_ref, nv_ref):
        gg = g_ref[...]
        nm = ADAM_B1 * m_ref[...] + (1.0 - ADAM_B1) * gg
        nv = ADAM_B2 * v_ref[...] + (1.0 - ADAM_B2) * (gg * gg)
        nm_ref[...] = nm
        nv_ref[...] = nv
        d_ref[...] = -ADAM_LR * ((nm / bc1) / (jnp.sqrt(nv / bc2) + ADAM_EPS) + ADAM_WD * w_ref[...])

    spec = pl.BlockSpec((tr, cols), lambda i: (i, 0))
    outs = pl.pallas_call(body, name=name, grid=(rows // tr,), in_specs=[spec] * 4, out_specs=[spec] * 3,
                          out_shape=[SDS((rows, cols), F32)] * 3, compiler_params=_cparams())(
        w.reshape(rows, cols), g.reshape(rows, cols), m.reshape(rows, cols), v.reshape(rows, cols))
    return tuple(t.reshape(shape) for t in outs)


def _exchange(name, x, scatter, after=None):
    blk = x.shape[1:] if scatter else x.shape
    extra = [] if after is None else [after]

    def body(x_ref, *rest):
        out_ref, send_sems, recv_sems, local_sem = rest[len(extra):]
        mx, my, mc = lax.axis_index("x"), lax.axis_index("y"), lax.axis_index("c")
        me = 4 * mx + 2 * my + mc
        own = pltpu.make_async_copy(x_ref.at[me] if scatter else x_ref, out_ref.at[me], local_sem)
        own.start()
        copies = []
        for kk in range(1, N_DEV):
            px = jnp.bitwise_xor(mx, (kk >> 2) & 1)
            py = jnp.bitwise_xor(my, (kk >> 1) & 1)
            pc = jnp.bitwise_xor(mc, kk & 1)
            peer = 4 * px + 2 * py + pc
            send = pltpu.make_async_remote_copy(
                src_ref=x_ref.at[peer] if scatter else x_ref, dst_ref=out_ref.at[me],
                send_sem=send_sems.at[kk - 1], recv_sem=recv_sems.at[kk - 1],
                device_id=(px, py, pc), device_id_type=MESH)
            send.start()
            arrival = pltpu.make_async_remote_copy(
                src_ref=x_ref.at[peer] if scatter else x_ref, dst_ref=out_ref.at[peer],
                send_sem=send_sems.at[kk - 1], recv_sem=recv_sems.at[kk - 1],
                device_id=(px, py, pc), device_id_type=MESH)
            copies.append((send, arrival))
        for send, arrival in copies:
            arrival.wait_recv()
            send.wait_send()
        own.wait()

    return pl.pallas_call(
        body, name=name, out_shape=SDS((N_DEV,) + tuple(blk), x.dtype),
        in_specs=[pl.BlockSpec(memory_space=pl.ANY)] * (1 + len(extra)), out_specs=pl.BlockSpec(memory_space=pl.ANY),
        scratch_shapes=[pltpu.SemaphoreType.DMA((N_DEV - 1,)), pltpu.SemaphoreType.DMA((N_DEV - 1,)),
                        pltpu.SemaphoreType.DMA],
    )(x, *extra)


def _rope_perm(pre, reps, post):
    half = QK_ROPE // 4
    width = reps * (pre + QK_ROPE) + post
    p = np.zeros((width, width), np.float32)
    for rep in range(reps):
        s0 = rep * (pre + QK_ROPE) + pre
        for base in (s0, s0 + 2 * half):
            for i in range(half):
                p[base + half + i, base + i] = -1.0
                p[base + i, base + half + i] = 1.0
    return p


def _rope_tables(n_lat, t_rows, pre, reps, post):
    half = QK_ROPE // 4
    pos = jnp.arange(n_lat)
    freqs = jnp.power(ROPE_THETA, -jnp.arange(0, 2 * half, 2, dtype=F32) / (2 * half))
    ang_r = (pos // GRID_W).astype(F32)[:, None] * freqs
    ang_c = (pos % GRID_W).astype(F32)[:, None] * freqs
    ang = jnp.concatenate([ang_r, ang_r, ang_c, ang_c], axis=-1)

    def table(fn, plain):
        slot = jnp.concatenate([jnp.full((n_lat, pre), plain, F32), fn(ang)], axis=-1)
        t = jnp.concatenate([jnp.tile(slot, (1, reps)), jnp.full((n_lat, post), plain, F32)], axis=-1)
        return jnp.concatenate([t, jnp.full((t_rows - n_lat, t.shape[1]), plain, F32)], axis=0)

    return table(jnp.cos, 1.0), table(jnp.sin, 0.0)


def _ffn_half_fwd(tag, s, mg, k, feed, i, coef, n_lat):
    wg_t, wu_t = feed.weights(f"{tag}_up", [f"gate_t{i}", f"up_t{i}"], s)
    u, a, b, hid = _ffn_up(f"{tag}_up", s, mg, k, n_lat, wg_t, wu_t)
    (wd,) = feed.weights(f"{tag}_down", [f"down{i}"], hid)
    s_out, o = _mm_resid(f"{tag}_down", hid, wd, s, mg, k, coef, n_lat)
    return s_out, (s, u, a, b, hid, o, wg_t, wu_t, wd)


def _ffn_half_bwd(tag, ds_out, saved, mg, k, feed, i, coef, n_lat):
    s, u, a, b, hid, o, wg_t, wu_t, wd = saved
    do, da, db, dgate = _ffn_dact(f"{tag}_dact", ds_out, o, mg, k, coef, n_lat, wd, a, b)
    dwd = _mm(f"{tag}_dwd", [(hid, do)], "tn", BF16)
    dwg_t = _mm(f"{tag}_dwg", [(da, u)], "tn", BF16)
    dwu_t = _mm(f"{tag}_dwu", [(db, u)], "tn", BF16)
    token = feed.grads(tag, {f"down{i}": dwd, f"gate_t{i}": dwg_t, f"up_t{i}": dwu_t})
    ds_in, (dshift, dscale, dgain) = _du_adaln(f"{tag}_du", [(da, wg_t), (db, wu_t)], s, ds_out, mg, k, n_lat,
                                               _after(token))
    return ds_in, dict(shift=dshift, scale=dscale, gate=dgate, gain=dgain)


def _after(token):
    return jnp.zeros((1, D_MODEL), F32) + token


def _mod_grad(parts, n_groups):
    rows = []
    zero = jnp.zeros((n_groups, 1, D_MODEL), F32)
    for k in range(3):
        for nm in ("shift", "scale", "gate"):
            t = parts[k].get(nm, zero)
            if t.shape[0] < n_groups:
                t = jnp.concatenate([t, jnp.zeros((n_groups - t.shape[0], 1, D_MODEL), F32)], axis=0)
            rows.append(t)
    return jnp.concatenate(rows, axis=1).reshape(n_groups, N_MOD * D_MODEL)


def _local_step(x, ctx, target, mod_h, mod_g, norm_g, feed, pool_w, pool_scale, q_norm_g, kv_norm_g, conv_w,
                final_norm_g):
    n_lat, n_ctx = x.shape[0], ctx.shape[0]
    t_all = n_lat + n_ctx
    mg0 = jnp.stack([jnp.concatenate([mod_h[0], norm_g[0]], axis=0), jnp.concatenate([mod_g, norm_g[0]], axis=0)])
    mg1 = jnp.concatenate([mod_h[1], norm_g[1]], axis=0)[None]

    s0 = jnp.concatenate([x, ctx], axis=0)
    s1, sv_f00 = _ffn_half_fwd("l0f0", s0, mg0, 0, feed, 0, 0.5, n_lat)

    w_in, w_uq, w_ukv_t, w_ab_out = feed.weights("l0m", ["in_t", "uq", "ukv_t", "ab_out"], s1)
    kv_rows = KV_RANK + QK_ROPE
    w_in_t = jnp.concatenate([
        w_in[:POOL_DIM], jnp.zeros((PA_CQ - POOL_DIM, D_MODEL), BF16), w_in[POOL_DIM:POOL_DIM + Q_RANK],
        w_in[POOL_DIM + Q_RANK:], jnp.zeros((PA_KV_W - kv_rows, D_MODEL), BF16)], axis=0)
    ua, proj = _adaln_mm("l0m_proj", s1, mg0, 1, n_lat, w_in_t)
    pool_y, pool_p = _pool_fwd("l0m_pool", proj, n_lat, pool_w.astype(BF16), pool_scale)
    nq = _rmsnorm_fwd("l0m_qnorm", proj, Q_RANK, PA_CQ // Q_RANK, q_norm_g, n_lat)
    q_lin = _mm("l0m_q", [(nq, w_uq)], "nn", F32, 512, 768)
    cos_q, sin_q = _rope_tables(n_lat, n_lat, QK_NOPE, HEADS, 0)
    perm_q = _rope_perm(QK_NOPE, HEADS, 0)
    q_rot = _rope("l0m_qrope", q_lin, Q_RANK, 0, cos_q, sin_q, jnp.asarray(perm_q, BF16), False, BF16)
    cos_k, sin_k = _rope_tables(n_lat, t_all, KV_RANK, 1, PA_KV_W - kv_rows)
    perm_k = _rope_perm(KV_RANK, 1, PA_KV_W - kv_rows)
    kvr = _rope("l0m_krope", proj, PA_KV_W, PA_KV // PA_KV_W, cos_k, sin_k, jnp.asarray(perm_k, BF16), False, F32)
    nkv = _rmsnorm_fwd("l0m_kvnorm", kvr, KV_RANK, 0, kv_norm_g, t_all)
    kv = _mm("l0m_kv", [(nkv, w_ukv_t)], "nt", BF16, 768, 512)
    qh = jnp.pad(q_rot.reshape(n_lat, HEADS, QK_HEAD), ((0, 0), (0, 0), (0, HEAD_PAD - QK_HEAD))).transpose(1, 0, 2)
    kvh = kv.reshape(t_all, HEADS, QK_NOPE + V_HEAD)
    k_rope = jnp.broadcast_to(kvr[:, None, KV_RANK:KV_RANK + QK_ROPE].astype(BF16), (t_all, HEADS, QK_ROPE))
    kh = jnp.concatenate([kvh[:, :, :QK_NOPE], k_rope, jnp.zeros((t_all, HEADS, HEAD_PAD - QK_HEAD), BF16)],
                         axis=-1).transpose(1, 0, 2)
    vh = kvh[:, :, QK_NOPE:].transpose(1, 0, 2)
    oh, lse = _attn_fwd("l0m_attn", qh, kh, vh)
    cat = jnp.concatenate([pool_y, oh.transpose(1, 0, 2).reshape(n_lat, HEADS * V_HEAD)], axis=-1)
    h1 = s1[:n_lat]
    h2, mix_o = _mm_resid("l0m_out", cat, w_ab_out, h1, mg0[:1], 1, 1.0, n_lat)

    h3, sv_f01 = _ffn_half_fwd("l0f1", h2, mg0[:1], 2, feed, 1, 0.5, n_lat)

    h4, sv_f10 = _ffn_half_fwd("l1f0", h3, mg1, 0, feed, 2, 0.5, n_lat)
    w_cin_t, w_c_out = feed.weights("l1m", ["cin_t", "c_out"], h4)
    uc, z3 = _adaln_mm("l1m_in", h4, mg1, 1, n_lat, w_cin_t)
    yc = _conv_fwd("l1m_conv", z3, conv_w)
    h5, conv_o = _mm_resid("l1m_out", yc, w_c_out, h4, mg1, 1, 1.0, n_lat)
    h6, sv_f11 = _ffn_half_fwd("l1f1", h5, mg1, 2, feed, 3, 0.5, n_lat)

    dh6, sq_cols, d_final_g = _final_loss("loss_head", h6, target, final_norm_g)
    g = {}
    dh5, g["f11"] = _ffn_half_bwd("l1f1", dh6, sv_f11, mg1, 2, feed, 3, 0.5, n_lat)

    do_c, dyc, dgate_c = _gate_mm("l1m_dy", dh5, conv_o, mg1, 1, 1.0, n_lat, w_c_out)
    d_c_out = _mm("l1m_dwout", [(yc, do_c)], "tn", BF16)
    db_, dc_, dv_, d_conv_w = _conv_bwd("l1m_dconv", dyc, z3, conv_w)
    dz3 = jnp.concatenate([db_, dc_, dv_], axis=-1)
    d_cin_t = _mm("l1m_dwin", [(dz3, uc)], "tn", BF16)
    token = feed.grads("l1m", {"c_out": d_c_out, "cin_t": d_cin_t})
    dh4, (dsh_c, dsc_c, dgn_c) = _du_adaln("l1m_du", [(dz3, w_cin_t)], h4, dh5, mg1, 1, n_lat, _after(token))
    dh3, g["f10"] = _ffn_half_bwd("l1f0", dh4, sv_f10, mg1, 0, feed, 2, 0.5, n_lat)

    dh2, g["f01"] = _ffn_half_bwd("l0f1", dh3, sv_f01, mg0[:1], 2, feed, 1, 0.5, n_lat)

    do_a, dcat, dgate_a = _gate_mm("l0m_dcat", dh2, mix_o, mg0[:1], 1, 1.0, n_lat, w_ab_out)
    d_ab_out = _mm("l0m_dwout", [(cat, do_a)], "tn", BF16)
    d_pool_x, d_pool_w, d_pool_scale = _pool_bwd("l0m_dpool", dcat, n_lat, pool_p, pool_w.astype(BF16), pool_scale)
    doh = dcat[:, POOL_DIM:].reshape(n_lat, HEADS, V_HEAD).transpose(1, 0, 2).astype(BF16)
    dqh, dkh, dvh, dk_sum = _attn_bwd("l0m_dattn", qh, kh, vh, oh, lse, doh)
    dq_rot = dqh[:, :, :QK_HEAD].transpose(1, 0, 2).reshape(n_lat, Q_RANK)
    dq_lin = _rope("l0m_dqrope", dq_rot, Q_RANK, 0, cos_q, sin_q, jnp.asarray(perm_q.T, BF16), True, BF16)
    d_uq = _mm("l0m_dwuq", [(nq, dq_lin)], "tn", BF16, 768, 768)
    dnq = _mm("l0m_dnq", [(dq_lin, w_uq)], "nt", F32, 512, 768)
    dcq, d_q_norm_g = _rmsnorm_bwd("l0m_dqnorm", proj, Q_RANK, PA_CQ // Q_RANK, dnq, q_norm_g, n_lat)
    dkv = jnp.concatenate([dkh[:, :, :QK_NOPE], dvh], axis=-1).transpose(1, 0, 2).reshape(t_all, HEADS * HEAD_PAD)
    dkv = dkv.astype(BF16)
    dnkv = _mm("l0m_dnkv", [(dkv, w_ukv_t)], "nn", F32, 768, 256)
    d_ukv_t = _mm("l0m_dwukv", [(dkv, nkv)], "tn", BF16, 512, 256)
    dckv, d_kv_norm_g = _rmsnorm_bwd("l0m_dkvnorm", kvr, KV_RANK, 0, dnkv, kv_norm_g, t_all)
    dkvr = jnp.concatenate([dckv, dk_sum[:, QK_NOPE:QK_HEAD],
                            jnp.zeros((t_all, PA_KV_W - KV_RANK - QK_ROPE), F32)], axis=-1)
    dpb = _rope("l0m_dkrope", dkvr, PA_KV_W, 0, cos_k, sin_k, jnp.asarray(perm_k.T, BF16), True, F32)
    dproj_lat = jnp.concatenate([d_pool_x, jnp.zeros((n_lat, PA_CQ - POOL_DIM), F32), dcq, dpb[:n_lat]], axis=-1)
    dproj_ctx = jnp.concatenate([jnp.zeros((n_ctx, PA_KV), F32), dpb[n_lat:]], axis=-1)
    dproj = jnp.concatenate([dproj_lat, dproj_ctx], axis=0).astype(BF16)
    d_in_pad = _mm("l0m_dwin", [(dproj, ua)], "tn", BF16, 640, 512)
    d_in_t = jnp.concatenate([d_in_pad[:POOL_DIM], d_in_pad[PA_CQ:PA_CQ + Q_RANK],
                              d_in_pad[PA_KV:PA_KV + kv_rows]], axis=0)
    token = feed.grads("l0m", {"ab_out": d_ab_out, "uq": d_uq, "ukv_t": d_ukv_t, "in_t": d_in_t})
    dh2_all = jnp.concatenate([dh2, jnp.zeros((n_ctx, D_MODEL), F32)], axis=0)
    ds1, (dsh_a, dsc_a, dgn_a) = _du_adaln("l0m_du", [(dproj, w_in_t)], s1, dh2_all, mg0, 1, n_lat, _after(token))
    ds0, g["f00"] = _ffn_half_bwd("l0f0", ds1, sv_f00, mg0, 0, feed, 0, 0.5, n_lat)

    dmod0 = _mod_grad([g["f00"], dict(shift=dsh_a, scale=dsc_a, gate=dgate_a), g["f01"]], 2)
    dmod1 = _mod_grad([g["f10"], dict(shift=dsh_c, scale=dsc_c, gate=dgate_c), g["f11"]], 1)
    d_norm_g = jnp.stack([
        jnp.concatenate([jnp.sum(g["f00"]["gain"], axis=0), jnp.sum(dgn_a, axis=0), g["f01"]["gain"][0]], axis=0),
        jnp.concatenate([g["f10"]["gain"][0], dgn_c[0], g["f11"]["gain"][0]], axis=0)])
    grads = dict(
        pool_w=d_pool_w, pool_scale=d_pool_scale, q_norm_g=d_q_norm_g[0], kv_norm_g=d_kv_norm_g[0],
        conv_w=d_conv_w, final_norm_g=d_final_g[0], norm_g=d_norm_g,
        mod_h=jnp.stack([dmod0[0], dmod1[0]]), mod_g=dmod0[1])
    return sq_cols, ds0, grads


HBM_SPEC = pl.BlockSpec(memory_space=pltpu.HBM)
SEM_SPEC = pl.BlockSpec(memory_space=pltpu.SEMAPHORE)
ANY_SPEC = pl.BlockSpec(memory_space=pl.ANY)
SIDE_EFFECT = pltpu.SideEffectType.DATAFLOW_SIDE_EFFECTING
N_PEERS = N_DEV - 1


def _mesh_place():
    mx, my, mc = lax.axis_index("x"), lax.axis_index("y"), lax.axis_index("c")
    return mx, my, mc, 4 * mx + 2 * my + mc


def _peer(place, kk):
    mx, my, mc, _ = place
    px = jnp.bitwise_xor(mx, (kk >> 2) & 1)
    py = jnp.bitwise_xor(my, (kk >> 1) & 1)
    pc = jnp.bitwise_xor(mc, kk & 1)
    return (px, py, pc), 4 * px + 2 * py + pc


def _hbm(a):
    return pltpu.with_memory_space_constraint(a, pltpu.HBM)


def _landing(block, me):
    zone = lax.empty((N_DEV,) + block.shape, block.dtype)
    return lax.dynamic_update_slice(zone, block[None], (me,) + (0,) * block.ndim)


ALL_PEERS = tuple(range(1, N_DEV))
SIBLING = 1
CHIP_PEERS = (2, 4, 6)
RELAYED = (3, 5, 7)


def _exchange_start(name, srcs, lands, scatter, after, peers=ALL_PEERS):
    n = len(srcs)

    def body(*refs):
        src, land = refs[:n], refs[n:2 * n]
        send_sems, recv_sems, token = refs[2 * n + 1], refs[2 * n + 2], refs[-1]
        place = _mesh_place()
        for a in range(n):
            for kk in peers:
                dev, peer = _peer(place, kk)
                pltpu.make_async_remote_copy(
                    src_ref=src[a].at[peer] if scatter else src[a], dst_ref=land[a].at[place[3]],
                    send_sem=send_sems.at[a * N_PEERS + kk - 1], recv_sem=recv_sems.at[a * N_PEERS + kk - 1],
                    device_id=dev, device_id_type=MESH).start()
        token[...] = jnp.zeros_like(token)

    thru = [pltpu.HBM(t.shape, t.dtype) for t in (*srcs, *lands)]
    res = pl.pallas_call(
        body, name=name,
        out_shape=(pltpu.SemaphoreType.DMA((n * N_PEERS,)), pltpu.SemaphoreType.DMA((n * N_PEERS,)), *thru,
                   SDS((8, 128), F32)),
        in_specs=[HBM_SPEC] * (2 * n) + [ANY_SPEC],
        out_specs=(SEM_SPEC, SEM_SPEC, *([HBM_SPEC] * (2 * n)), pl.BlockSpec(memory_space=pltpu.VMEM)),
        input_output_aliases={i: 2 + i for i in range(2 * n)},
        compiler_params=pltpu.CompilerParams(has_side_effects=SIDE_EFFECT),
    )(*[_hbm(s) for s in srcs], *[_hbm(t) for t in lands], after)
    return res[0], res[1], list(res[2:2 + n]), list(res[2 + n:2 + 2 * n]), res[-1]


def _exchange_wait(name, send_sems, recv_sems, srcs, lands, places, scatter, after):
    n = len(srcs)

    def body(*refs):
        src, land = refs[:n], refs[n:2 * n]
        send, recv = refs[2 * n], refs[2 * n + 1]
        place = _mesh_place()
        for a in range(n):
            for kk in range(1, N_DEV):
                dev, peer = _peer(place, kk)
                cp = pltpu.make_async_remote_copy(
                    src_ref=src[a].at[peer] if scatter else src[a], dst_ref=land[a].at[peer],
                    send_sem=send.at[places[a] * N_PEERS + kk - 1], recv_sem=recv.at[places[a] * N_PEERS + kk - 1],
                    device_id=dev, device_id_type=MESH)
                cp.wait_send()
                cp.wait_recv()

    thru = [pltpu.HBM(t.shape, t.dtype) for t in (*srcs, *lands)]
    res = pl.pallas_call(
        body, name=name, out_shape=tuple(thru),
        in_specs=[HBM_SPEC] * (2 * n) + [SEM_SPEC, SEM_SPEC] + [ANY_SPEC] * len(after),
        out_specs=tuple([HBM_SPEC] * (2 * n)), input_output_aliases={i: i for i in range(2 * n)},
        compiler_params=pltpu.CompilerParams(has_side_effects=SIDE_EFFECT),
    )(*srcs, *lands, send_sems, recv_sems, *after)
    return list(res[n:])


def _gather_relay(name, send1, recv1, lands, places, after):
    n = len(lands)

    def body(*refs):
        land, s1, r1 = refs[:n], refs[n], refs[n + 1]
        s2, r2 = refs[n + 3], refs[n + 4]
        place = _mesh_place()
        sibling = _peer(place, SIBLING)[0]
        for a in range(n):
            for j, kk in enumerate(CHIP_PEERS):
                dev, origin = _peer(place, kk)
                block = land[a].at[origin]
                pltpu.make_async_remote_copy(
                    src_ref=block, dst_ref=block, send_sem=s1.at[places[a] * N_PEERS + kk - 1],
                    recv_sem=r1.at[places[a] * N_PEERS + kk - 1], device_id=dev, device_id_type=MESH).wait_recv()
                pltpu.make_async_remote_copy(
                    src_ref=block, dst_ref=block, send_sem=s2.at[a * 3 + j], recv_sem=r2.at[a * 3 + j],
                    device_id=sibling, device_id_type=MESH).start()

    res = pl.pallas_call(
        body, name=name,
        out_shape=(pltpu.SemaphoreType.DMA((3 * n,)), pltpu.SemaphoreType.DMA((3 * n,)),
                   *[pltpu.HBM(t.shape, t.dtype) for t in lands]),
        in_specs=[HBM_SPEC] * n + [SEM_SPEC, SEM_SPEC, ANY_SPEC],
        out_specs=(SEM_SPEC, SEM_SPEC, *([HBM_SPEC] * n)),
        input_output_aliases={i: 2 + i for i in range(n)},
        compiler_params=pltpu.CompilerParams(has_side_effects=SIDE_EFFECT),
    )(*lands, send1, recv1, after)
    return res[0], res[1], list(res[2:])


def _gather_wait(name, send1, recv1, send2, recv2, srcs, lands, places, after):
    n = len(lands)

    def body(*refs):
        src, land = refs[:n], refs[n:2 * n]
        s1, r1, s2, r2 = refs[2 * n:2 * n + 4]
        place = _mesh_place()
        for a in range(n):
            for kk in (SIBLING,) + CHIP_PEERS:
                dev, origin = _peer(place, kk)
                first = pltpu.make_async_remote_copy(
                    src_ref=src[a], dst_ref=land[a].at[origin], send_sem=s1.at[places[a] * N_PEERS + kk - 1],
                    recv_sem=r1.at[places[a] * N_PEERS + kk - 1], device_id=dev, device_id_type=MESH)
                first.wait_send()
                if kk == SIBLING:
                    first.wait_recv()
            for j, kk in enumerate(CHIP_PEERS):
                dev, origin = _peer(place, kk + 1)
                relay = pltpu.make_async_remote_copy(
                    src_ref=src[a], dst_ref=land[a].at[origin], send_sem=s2.at[a * 3 + j], recv_sem=r2.at[a * 3 + j],
                    device_id=dev, device_id_type=MESH)
                relay.wait_send()
                relay.wait_recv()

    arrays = (*srcs, *lands)
    res = pl.pallas_call(
        body, name=name, out_shape=tuple(pltpu.HBM(t.shape, t.dtype) for t in arrays),
        in_specs=[HBM_SPEC] * (2 * n) + [SEM_SPEC] * 4 + [ANY_SPEC], out_specs=tuple([HBM_SPEC] * (2 * n)),
        input_output_aliases={i: i for i in range(2 * n)},
        compiler_params=pltpu.CompilerParams(has_side_effects=SIDE_EFFECT),
    )(*arrays, send1, recv1, send2, recv2, after)
    return list(res[n:])


class _Feed:
    def __init__(self, shards, groups, me):
        self.shards, self.groups, self.me, self.pos = shards, groups, me, 0
        self.sems, self.srcs, self.lands = {}, {}, {}
        self.pending = []

    def start(self, tag, names, after):
        srcs = [self.shards[nm] for nm in names]
        lands = [_landing(s, self.me) for s in srcs]
        send, recv, srcs, lands, self.token = _exchange_start(
            f"gather_start_{tag}", srcs, lands, False, after, (SIBLING,) + CHIP_PEERS)
        for i, nm in enumerate(names):
            self.sems[nm], self.srcs[nm], self.lands[nm] = (send, recv, i), srcs[i], lands[i]
        return self.token

    def relay_first(self, after):
        self.relay = self._relay("gather_relay_first", self.groups[0], after)

    def _relay(self, name, names, after):
        send, recv, _ = self.sems[names[0]]
        places = [self.sems[nm][2] for nm in names]
        send2, recv2, lands = _gather_relay(name, send, recv, [self.lands[nm] for nm in names], places, after)
        for nm, t in zip(names, lands):
            self.lands[nm] = t
        return send2, recv2

    def start_token(self):
        return self.token[0, 0]

    def weights(self, tag, names, after):
        assert names == self.groups[self.pos], (names, self.groups[self.pos])
        send2, recv2 = self.relay
        if self.pos + 1 < len(self.groups):
            nxt = self.groups[self.pos + 1]
            self.relay = self._relay(f"gather_relay_{tag}", nxt, after)
            after = self.lands[nxt[0]]
        send, recv, _ = self.sems[names[0]]
        got = _gather_wait(f"gather_wait_{tag}", send, recv, send2, recv2, [self.srcs[nm] for nm in names],
                           [self.lands[nm] for nm in names], [self.sems[nm][2] for nm in names], after)
        self.pos += 1
        return [t.reshape((N_DEV * t.shape[1],) + t.shape[2:]) for t in got]

    def grads(self, tag, full):
        names = list(full)
        srcs = [full[nm].reshape((N_DEV, full[nm].shape[0] // N_DEV) + full[nm].shape[1:]) for nm in names]
        lands = [_landing(lax.dynamic_index_in_dim(s, self.me, 0, keepdims=False), self.me) for s in srcs]
        send, recv, srcs, lands, token = _exchange_start(f"scatter_start_{tag}", srcs, lands, True, srcs[0])
        self.pending.append((tag, names, send, recv, srcs, lands))
        return token[0, 0]

    def collect(self, tags, after, keep_slots=()):
        out = {}
        for tag, names, send, recv, srcs, lands in self.pending:
            if tag not in tags:
                continue
            got = _exchange_wait(f"scatter_wait_{tag}", send, recv, srcs, lands, list(range(len(names))), True, after)
            for nm, slots in zip(names, got):
                out[nm] = slots if nm.startswith(tuple(keep_slots)) else _sum_slots(f"reduce_{nm}", slots)
        return out


def _adamw_math(w, gg, m, v):
    nm = ADAM_B1 * m + (1.0 - ADAM_B1) * gg
    nv = ADAM_B2 * v + (1.0 - ADAM_B2) * (gg * gg)
    bc1 = 1.0 - ADAM_B1 ** ADAM_STEP
    bc2 = 1.0 - ADAM_B2 ** ADAM_STEP
    return -ADAM_LR * ((nm / bc1) / (jnp.sqrt(nv / bc2) + ADAM_EPS) + ADAM_WD * w), nm, nv


def _adamw_part(name, i, w, slots, m, v, prev):
    n_parts, rows, cols = w.shape
    tr = _tile(rows, 256, 16)
    if prev is None:
        prev = tuple(lax.empty(w.shape, F32) for _ in range(4))

    def body(w_ref, g_ref, m_ref, v_ref, *rest):
        go_ref, d_ref, nm_ref, nv_ref = rest[4:]
        gg = g_ref[0].astype(F32)
        for sl in range(1, N_DEV):
            gg = gg + g_ref[sl].astype(F32)
        d, nm, nv = _adamw_math(w_ref[...], gg, m_ref[...], v_ref[...])
        go_ref[...] = gg
        d_ref[...] = d
        nm_ref[...] = nm
        nv_ref[...] = nv

    part = pl.BlockSpec((None, tr, cols), lambda r: (i, r, 0))
    return pl.pallas_call(
        body, name=name, grid=(rows // tr,),
        in_specs=[part, pl.BlockSpec((N_DEV, tr, cols), lambda r: (0, r, 0)), part, part] + [ANY_SPEC] * 4,
        out_specs=[part] * 4, out_shape=[SDS(w.shape, F32)] * 4,
        input_output_aliases={4 + k: k for k in range(4)}, compiler_params=_cparams(),
    )(w, slots, m, v, *prev)


WEIGHT_NAMES = ("c_ctx", "norm_g", "w_mod", "b_mod", "ffn_w_gate", "ffn_w_up", "ffn_w_down", "ab_w_in", "pool_w",
                "pool_scale", "q_norm_g", "w_uq", "kv_norm_g", "w_ukv", "ab_w_out", "conv_w_in", "conv_w",
                "conv_w_out", "final_norm_g")


def kernel(x, c, ctx, c_ctx, norm_g, w_mod, b_mod, ffn_w_gate, ffn_w_up, ffn_w_down, ab_w_in, pool_w, pool_scale, q_norm_g, w_uq, kv_norm_g, w_ukv, ab_w_out, conv_w_in, conv_w, conv_w_out, final_norm_g, loss_target, m_c_ctx, m_norm_g, m_w_mod, m_b_mod, m_ffn_w_gate, m_ffn_w_up, m_ffn_w_down, m_ab_w_in, m_pool_w, m_pool_scale, m_q_norm_g, m_w_uq, m_kv_norm_g, m_w_ukv, m_ab_w_out, m_conv_w_in, m_conv_w, m_conv_w_out, m_final_norm_g, v_c_ctx, v_norm_g, v_w_mod, v_b_mod, v_ffn_w_gate, v_ffn_w_up, v_ffn_w_down, v_ab_w_in, v_pool_w, v_pool_scale, v_q_norm_g, v_w_uq, v_kv_norm_g, v_w_ukv, v_ab_w_out, v_conv_w_in, v_conv_w, v_conv_w_out, v_final_norm_g):
    weights = (c_ctx, norm_g, w_mod, b_mod, ffn_w_gate, ffn_w_up, ffn_w_down, ab_w_in, pool_w, pool_scale, q_norm_g,
               w_uq, kv_norm_g, w_ukv, ab_w_out, conv_w_in, conv_w, conv_w_out, final_norm_g)
    moms = (m_c_ctx, m_norm_g, m_w_mod, m_b_mod, m_ffn_w_gate, m_ffn_w_up, m_ffn_w_down, m_ab_w_in, m_pool_w,
            m_pool_scale, m_q_norm_g, m_w_uq, m_kv_norm_g, m_w_ukv, m_ab_w_out, m_conv_w_in, m_conv_w, m_conv_w_out,
            m_final_norm_g)
    vels = (v_c_ctx, v_norm_g, v_w_mod, v_b_mod, v_ffn_w_gate, v_ffn_w_up, v_ffn_w_down, v_ab_w_in, v_pool_w,
            v_pool_scale, v_q_norm_g, v_w_uq, v_kv_norm_g, v_w_ukv, v_ab_w_out, v_conv_w_in, v_conv_w, v_conv_w_out,
            v_final_norm_g)
    me = 4 * lax.axis_index("x") + 2 * lax.axis_index("y") + lax.axis_index("c")
    n_lat, n_ctx = x.shape[1], ctx.shape[1]
    d = D_MODEL
    mod_cols = w_mod.shape[-1]
    ng_sh, cw_sh = norm_g.shape[-1], conv_w.shape[-1]

    def ffn_shards(i):
        return {f"gate_t{i}": ffn_w_gate[i // 2, i % 2].T, f"up_t{i}": ffn_w_up[i // 2, i % 2].T,
                f"down{i}": ffn_w_down[i // 2, i % 2]}

    local = {**ffn_shards(0), "in_t": ab_w_in[0].T, "uq": w_uq[0], "ukv_t": w_ukv[0].T, "ab_out": ab_w_out[0],
             **ffn_shards(1), **ffn_shards(2), "cin_t": conv_w_in[0].T, "c_out": conv_w_out[0], **ffn_shards(3)}
    ffn_groups = [[[f"gate_t{i}", f"up_t{i}"], [f"down{i}"]] for i in range(4)]
    groups = [*ffn_groups[0], ["in_t", "uq", "ukv_t", "ab_out"], *ffn_groups[1], *ffn_groups[2], ["cin_t", "c_out"],
              *ffn_groups[3]]
    feed = _Feed({nm: a.astype(BF16) for nm, a in local.items()}, groups, me)

    small = jnp.concatenate([c.reshape(-1), norm_g.reshape(-1), conv_w.reshape(-1)])
    small_n = -(-small.shape[0] // 1024) * 1024
    small = jnp.pad(small, (0, small_n - small.shape[0])).reshape(small_n // 128, 128)
    small_all = _exchange("gather_small", small, False).reshape(N_DEV, small_n)
    c_all = small_all[:, :d]
    o1 = d + 6 * ng_sh
    norm_g_full = small_all[:, d:o1].reshape(N_DEV, 2, 3, ng_sh).transpose(1, 2, 0, 3).reshape(2, 3, d)
    conv_w_full = small_all[:, o1:o1 + 3 * cw_sh].reshape(N_DEV, 3, cw_sh).transpose(1, 0, 2).reshape(3, d)

    cond = jnp.concatenate([c_all, jnp.broadcast_to(c_ctx[None, :], (N_DEV, d))], axis=0)
    sil, dsil = _silu_rows("mod_silu", cond)
    w_mod_b = w_mod.astype(BF16)
    b_sh = lax.dynamic_slice(b_mod, (0, me * mod_cols), (2, mod_cols))
    m_part = jnp.stack([_mm(f"mod_fwd{l}", [(sil, w_mod_b[l])], "nn", F32, 16, 384, bias=b_sh[l:l + 1])
                        for l in range(2)], axis=1)
    m_all = _exchange("gather_mod", m_part.reshape(-1, 128), False).reshape(N_DEV, 2 * N_DEV, 2, mod_cols)
    m_mine = lax.dynamic_index_in_dim(m_all, me, axis=1, keepdims=False)
    mod_h = m_mine.transpose(1, 0, 2).reshape(2, N_MOD, d)
    mod_g = m_all[:, N_DEV, 0, :].reshape(N_MOD, d)

    feed.start("all", [nm for grp in groups for nm in grp], m_all)
    feed.relay_first(feed.token)

    sq_cols, ds0, g = _local_step(x[0], ctx[0], loss_target[0], mod_h, mod_g, norm_g_full, feed, pool_w[0],
                                  pool_scale, q_norm_g, kv_norm_g, conv_w_full, final_norm_g)
    grad_x = ds0[:n_lat]
    loss = lax.psum(0.5 * jnp.sum(sq_cols) / d, ("x", "y", "c"))
    w_of, m_of, v_of = (dict(zip(WEIGHT_NAMES, t)) for t in (weights, moms, vels))
    results = {}

    def update(nm, grad, view=lambda t: t):
        outs = _adamw(f"adamw_{nm}", view(w_of[nm]), grad.reshape(view(w_of[nm]).shape), view(m_of[nm]), view(v_of[nm]))
        results[nm] = tuple(view(t) for t in (grad.reshape(view(w_of[nm]).shape), *outs))

    def swap(t):
        return jnp.swapaxes(t, -1, -2)

    stacked = ("gate_t", "up_t", "down")
    early = feed.collect(["l1f1", "l1m", "l1f0", "l0f1", "l0m"], [ds0], stacked)
    update("ab_w_in", early["in_t"], swap)
    update("w_uq", early["uq"])
    update("w_ukv", early["ukv_t"].T)
    update("ab_w_out", early["ab_out"])
    update("conv_w_in", early["cin_t"].T)
    update("conv_w_out", early["c_out"])
    ffn = {}
    for nm, prefix, view in (("ffn_w_gate", "gate_t", swap), ("ffn_w_up", "up_t", swap),
                             ("ffn_w_down", "down", lambda t: t)):
        w4, m4, v4 = (view(t).reshape((4,) + view(t).shape[-2:]) for t in (w_of[nm], m_of[nm], v_of[nm]))
        prev = None
        for i in (3, 2, 1):
            prev = _adamw_part(f"adamw_{nm}{i}", i, w4, early[f"{prefix}{i}"], m4, v4, prev)
        ffn[nm] = (prefix, view, w4, m4, v4, prev)
    done_early = [results[nm][1] for nm in results] + [state[5][1] for state in ffn.values()]
    late = feed.collect(["l0f0"], done_early, stacked)
    for nm, (prefix, view, w4, m4, v4, prev) in ffn.items():
        outs = _adamw_part(f"adamw_{nm}0", 0, w4, late[f"{prefix}0"], m4, v4, prev)
        results[nm] = tuple(view(t.reshape(view(w_of[nm]).shape)) for t in outs)

    dm = jnp.stack([g["mod_h"], jnp.stack([g["mod_g"], jnp.zeros_like(g["mod_g"])])])
    dm_all = _exchange("gather_dmod", dm.reshape(-1, 128), False, results["ffn_w_down"][1]).reshape(N_DEV, 2, 2, N_MOD * d)
    grad_b_mod = _sum_rows("dmod_bias", dm_all.reshape(2 * N_DEV, 2 * N_MOD * d)).reshape(2, N_MOD * d)
    dm_sh = lax.dynamic_slice(dm_all, (0, 0, 0, me * mod_cols), (N_DEV, 2, 2, mod_cols))
    gw_mod, cctx_parts = [], []
    for l in range(2):
        dm_l = dm_sh[:, :, l, :].transpose(1, 0, 2).reshape(2 * N_DEV, mod_cols).astype(BF16)
        gw_mod.append(_mm(f"mod_dw{l}", [(sil, dm_l)], "tn", F32, 512, 384))
        dm_ctx = jnp.concatenate([dm_l[N_DEV:], jnp.zeros((N_DEV, mod_cols), BF16)], axis=0)
        cctx_parts.append(_mm(f"mod_dcond{l}", [(dm_ctx, w_mod_b[l])], "nt", F32, 16, 512))
    cctx_part = _sum_rows("mod_dcond_sum", jnp.concatenate(cctx_parts, axis=0))
    update("w_mod", jnp.stack(gw_mod))
    update("b_mod", grad_b_mod)

    small_g = jnp.concatenate([g["pool_w"].reshape(-1), g["pool_scale"].reshape(-1), g["q_norm_g"].reshape(-1),
                               g["kv_norm_g"].reshape(-1), g["final_norm_g"].reshape(-1), g["norm_g"].reshape(-1),
                               g["conv_w"].reshape(-1), cctx_part.reshape(-1)])
    sizes = [pool_w.size, pool_scale.size, q_norm_g.size, kv_norm_g.size, d, 6 * d, 3 * d, d]
    sg_n = -(-small_g.shape[0] // 1024) * 1024
    small_g = jnp.pad(small_g, (0, sg_n - small_g.shape[0]))
    sg_all = _exchange("gather_small_grads", small_g.reshape(-1, 128), False).reshape(N_DEV, sg_n)
    scale_vec = jnp.concatenate([jnp.ones((1, sum(sizes[:-1])), F32), dsil[N_DEV:N_DEV + 1],
                                 jnp.ones((1, sg_n - sum(sizes)), F32)], axis=1)
    sg = _sum_rows("small_grads_sum", sg_all, scale_vec)[0]
    cuts, pos = [], 0
    for sz in sizes:
        cuts.append(sg[pos:pos + sz])
        pos += sz
    g_pool_w, g_pool_scale, g_q_norm, g_kv_norm, g_final, g_norm_full, g_conv_full, g_c_ctx = cuts
    update("c_ctx", g_c_ctx)
    update("norm_g", lax.dynamic_slice(g_norm_full.reshape(2, 3, d), (0, 0, me * ng_sh), (2, 3, ng_sh)))
    update("conv_w", lax.dynamic_slice(g_conv_full.reshape(3, d), (0, me * cw_sh), (3, cw_sh)))
    update("pool_w", g_pool_w)
    update("pool_scale", g_pool_scale)
    update("q_norm_g", g_q_norm)
    update("kv_norm_g", g_kv_norm)
    update("final_norm_g", g_final)
    outs = [results[nm] for nm in WEIGHT_NAMES]
    return (loss, grad_x[None], *[o[0] for o in outs], *[o[1] for o in outs], *[o[2] for o in outs],
            *[o[3] for o in outs])
```

```python
import functools
import math

import jax
import jax.numpy as jnp
import numpy as np
from jax import lax
from jax.experimental import pallas as pl
from jax.experimental.pallas import tpu as pltpu

F32 = jnp.float32
BF16 = jnp.bfloat16
MESH = pl.DeviceIdType.MESH
SDS = jax.ShapeDtypeStruct

N_DEV = 8
D_MODEL = 1024
N_MOD = 9
D_FF = 2816
POOL_WINDOWS = (2, 4, 8, 16)
POOL_DIM = 512
POOL_GROUP_DIM = 128
HEADS = 8
QK_NOPE = 64
QK_ROPE = 32
QK_HEAD = QK_NOPE + QK_ROPE
V_HEAD = 64
Q_RANK = 768
KV_RANK = 256
GRID_W = 64
ROPE_THETA = 10000.0
RMS_EPS = 1e-6
ATTN_SCALE = 1.0 / math.sqrt(QK_HEAD)
HEAD_PAD = 128
POOL_PAD = 16
PA_POOL, PA_CQ, PA_KV = 0, 768, 1536
PA_KV_W = 384
PA_W = PA_KV + PA_KV_W

ADAM_LR, ADAM_B1, ADAM_B2, ADAM_EPS, ADAM_WD, ADAM_STEP = 0.001, 0.9, 0.999, 1e-08, 0.01, 10

VMEM_LIMIT_BYTES = 56 * 1024 * 1024

NN = ((1,), (0,))
NT = ((1,), (1,))
TN = ((0,), (0,))


def _cparams():
    return pltpu.CompilerParams(vmem_limit_bytes=VMEM_LIMIT_BYTES)


def _dot(a, b, dims):
    return lax.dot_general(a, b, (dims, ((), ())), preferred_element_type=F32)


def _tile(n, cap, mult=8):
    t = (min(cap, n) // mult) * mult
    while t >= mult:
        if n % t == 0:
            return t
        t -= mult
    return n


def _colsum(x):
    return jnp.sum(x, axis=0, keepdims=True)


def _rms(x):
    r = lax.rsqrt(jnp.mean(x * x, axis=-1, keepdims=True) + RMS_EPS)
    return x * r, r


def _rms_bwd(n, r, dn):
    return r * (dn - n * jnp.mean(dn * n, axis=-1, keepdims=True))


def _rowwise(name, fn, t_rows, tm, n_lat, rows, vecs, outs, accs):
    nt = t_rows // tm
    nlt = n_lat // tm
    n_groups = 2 if nlt < nt else 1

    def grp(i):
        return jnp.where(i >= nlt, 1, 0) if n_groups == 2 else 0

    in_specs = [pl.BlockSpec((tm, w), functools.partial(lambda i, cb: (i, cb), cb=cb)) for (_, w, cb) in rows]
    in_specs += [pl.BlockSpec((1,) + v.shape[1:], lambda i: (grp(i), 0, 0)) for v in vecs]
    out_specs = [pl.BlockSpec((tm, w), lambda i: (i, 0)) for (w, _) in outs]
    out_specs += [pl.BlockSpec((1, 1, w), lambda i: (grp(i), 0, 0)) for w in accs]
    out_shape = [SDS((t_rows, w), dt) for (w, dt) in outs] + [SDS((n_groups, 1, w), F32) for w in accs]
    n_r, n_v, n_o = len(rows), len(vecs), len(outs)

    def body(*refs):
        row_vals = [r[...] for r in refs[:n_r]]
        vec_vals = [v[0] for v in refs[n_r:n_r + n_v]]
        out_refs = refs[n_r + n_v:n_r + n_v + n_o]
        acc_refs = refs[n_r + n_v + n_o:]
        out_vals, acc_vals = fn(row_vals, vec_vals)
        for o_ref, o in zip(out_refs, out_vals):
            o_ref[...] = o.astype(o_ref.dtype)
        if acc_refs:
            i = pl.program_id(0)
            first = (i == 0) | (i == nlt) if n_groups == 2 else i == 0

            @pl.when(first)
            def _():
                for a_ref, a in zip(acc_refs, acc_vals):
                    a_ref[0] = a

            @pl.when(jnp.logical_not(first))
            def _():
                for a_ref, a in zip(acc_refs, acc_vals):
                    a_ref[0] += a

    res = pl.pallas_call(
        body, name=name, grid=(nt,), in_specs=in_specs, out_specs=out_specs, out_shape=out_shape,
        compiler_params=_cparams(),
    )(*[r[0] for r in rows], *vecs)
    return res[:n_o], res[n_o:]


RESIDENT_BYTES = 12 * 1024 * 1024


def _mm(name, pairs, mode, out_dtype, tm_cap=256, tn_cap=512, bias=None):
    a0, b0 = pairs[0]
    if mode == "nn":
        m, n, dims = a0.shape[0], b0.shape[1], NN
    elif mode == "nt":
        m, n, dims = a0.shape[0], b0.shape[0], NT
    else:
        m, n, dims = a0.shape[1], b0.shape[1], TN
    b_bytes = sum(b.size * b.dtype.itemsize for _, b in pairs)
    tn = n if b_bytes <= RESIDENT_BYTES else _tile(n, tn_cap, 128)
    tm = _tile(m, tm_cap, 128 if mode == "tn" else 16)

    def a_spec(a):
        if mode == "tn":
            return pl.BlockSpec((a.shape[0], tm), lambda i, j: (0, i))
        return pl.BlockSpec((tm, a.shape[1]), lambda i, j: (i, 0))

    def b_spec(b):
        if mode == "nt":
            return pl.BlockSpec((tn, b.shape[1]), lambda i, j: (j, 0))
        return pl.BlockSpec((b.shape[0], tn), lambda i, j: (0, j))

    in_specs, flat = [], []
    for a, b in pairs:
        in_specs += [a_spec(a), b_spec(b)]
        flat += [a, b]
    if bias is not None:
        in_specs.append(pl.BlockSpec((1, tn), lambda i, j: (0, j)))
        flat.append(bias)
    n_pairs = len(pairs)

    def body(*refs):
        acc = None
        for p in range(n_pairs):
            t = _dot(refs[2 * p][...], refs[2 * p + 1][...], dims)
            acc = t if acc is None else acc + t
        if bias is not None:
            acc = acc + refs[2 * n_pairs][...]
        refs[-1][...] = acc.astype(refs[-1].dtype)

    return pl.pallas_call(
        body, name=name, grid=(m // tm, n // tn), in_specs=in_specs,
        out_specs=pl.BlockSpec((tm, tn), lambda i, j: (i, j)),
        out_shape=SDS((m, n), out_dtype), compiler_params=_cparams(),
    )(*flat)


def _mm_resid(name, a, b, s, mg, k, coef, n_lat):
    t_rows, n = a.shape[0], b.shape[1]
    tm = _tile(math.gcd(n_lat, t_rows), 256, 16)
    nlt = n_lat // tm
    n_groups = 2 if nlt < t_rows // tm else 1

    def grp(i):
        return jnp.where(i >= nlt, 1, 0) if n_groups == 2 else 0

    def body(a_ref, b_ref, s_ref, mg_ref, so_ref, o_ref):
        o = _dot(a_ref[...], b_ref[...], NN)
        gate = mg_ref[0, 3 * k + 2:3 * k + 3, :]
        o_ref[...] = o.astype(BF16)
        so_ref[...] = s_ref[...] + (coef * gate) * o

    row = pl.BlockSpec((tm, n), lambda i: (i, 0))
    return pl.pallas_call(
        body, name=name, grid=(t_rows // tm,),
        in_specs=[pl.BlockSpec((tm, a.shape[1]), lambda i: (i, 0)), pl.BlockSpec(b.shape, lambda i: (0, 0)), row,
                  pl.BlockSpec((1, mg.shape[1], n), lambda i: (grp(i), 0, 0))],
        out_specs=[row, row], out_shape=[SDS((t_rows, n), F32), SDS((t_rows, n), BF16)], compiler_params=_cparams(),
    )(a, b, s, mg)


def _groups(t_rows, tm, n_lat):
    nlt = n_lat // tm
    if nlt < t_rows // tm:
        return 2, (lambda i: jnp.where(i >= nlt, 1, 0)), (lambda i: (i == 0) | (i == nlt))
    return 1, (lambda i: 0), (lambda i: i == 0)


def _accumulate(acc_refs, vals, first):
    @pl.when(first)
    def _():
        for r, v in zip(acc_refs, vals):
            r[0] = v

    @pl.when(jnp.logical_not(first))
    def _():
        for r, v in zip(acc_refs, vals):
            r[0] += v


def _adaln_math(s, m, k):
    n, _ = _rms(s)
    return (n * m[9 + k:10 + k]) * (1.0 + m[3 * k + 1:3 * k + 2]) + m[3 * k:3 * k + 1]


def _ffn_up(name, s, mg, k, n_lat, wg_t, wu_t):
    t_rows, f = s.shape[0], wg_t.shape[0]
    tm = _row_tm(t_rows, n_lat)
    _, grp, _ = _groups(t_rows, tm, n_lat)

    def body(s_ref, mg_ref, wg_ref, wu_ref, u_ref, a_ref, b_ref, h_ref):
        uu = _adaln_math(s_ref[...], mg_ref[0], k).astype(BF16)
        u_ref[...] = uu
        a = _dot(uu, wg_ref[...], NT)
        b = _dot(uu, wu_ref[...], NT)
        sg = jax.nn.sigmoid(a)
        act = a * sg
        a_ref[...] = (b * (sg * (1.0 + a * (1.0 - sg)))).astype(BF16)
        b_ref[...] = act.astype(BF16)
        h_ref[...] = (act * b).astype(BF16)

    w_spec = pl.BlockSpec(wg_t.shape, lambda i: (0, 0))
    o_spec = pl.BlockSpec((tm, f), lambda i: (i, 0))
    row = pl.BlockSpec((tm, s.shape[1]), lambda i: (i, 0))
    return pl.pallas_call(
        body, name=name, grid=(t_rows // tm,),
        in_specs=[row, pl.BlockSpec((1,) + mg.shape[1:], lambda i: (grp(i), 0, 0)), w_spec, w_spec],
        out_specs=[row, o_spec, o_spec, o_spec],
        out_shape=[SDS(s.shape, BF16)] + [SDS((t_rows, f), BF16)] * 3, compiler_params=_cparams(),
    )(s, mg, wg_t, wu_t)


def _ffn_dact(name, ds_out, o, mg, k, coef, n_lat, wd, a, b):
    t_rows, f = ds_out.shape[0], wd.shape[0]
    tm = _row_tm(t_rows, n_lat)
    n_groups, grp, first = _groups(t_rows, tm, n_lat)
    d = ds_out.shape[1]

    def body(ds_ref, o_ref, mg_ref, wd_ref, a_ref, b_ref, do_ref, da_ref, db_ref, dg_ref):
        dd = coef * ds_ref[...]
        do = (dd * mg_ref[0, 3 * k + 2:3 * k + 3, :]).astype(BF16)
        do_ref[...] = do
        _accumulate([dg_ref], [_colsum(dd * o_ref[...].astype(F32))], first(pl.program_id(0)))
        dh = _dot(do, wd_ref[...], NT)
        da_ref[...] = (dh * a_ref[...].astype(F32)).astype(BF16)
        db_ref[...] = (dh * b_ref[...].astype(F32)).astype(BF16)

    row = pl.BlockSpec((tm, d), lambda i: (i, 0))
    t_spec = pl.BlockSpec((tm, f), lambda i: (i, 0))
    return pl.pallas_call(
        body, name=name, grid=(t_rows // tm,),
        in_specs=[row, row, pl.BlockSpec((1,) + mg.shape[1:], lambda i: (grp(i), 0, 0)),
                  pl.BlockSpec(wd.shape, lambda i: (0, 0)), t_spec, t_spec],
        out_specs=[row, t_spec, t_spec, pl.BlockSpec((1, 1, d), lambda i: (grp(i), 0, 0))],
        out_shape=[SDS((t_rows, d), BF16), SDS((t_rows, f), BF16), SDS((t_rows, f), BF16), SDS((n_groups, 1, d), F32)],
        compiler_params=_cparams(),
    )(ds_out, o, mg, wd, a, b)


def _du_adaln(name, pairs, s, ds_out, mg, k, n_lat, after):
    t_rows, d = s.shape
    tm = _row_tm(t_rows, n_lat)
    n_groups, grp, first = _groups(t_rows, tm, n_lat)
    n_pairs = len(pairs)

    def body(*refs):
        s_ref, ds_ref, mg_ref, z_ref, out_ref, dsh_ref, dsc_ref, dgn_ref = refs[2 * n_pairs:]
        d_u = z_ref[...]
        for p in range(n_pairs):
            d_u = d_u + _dot(refs[p][...], refs[n_pairs + p][...], NN)
        m = mg_ref[0]
        gain, scale = m[9 + k:10 + k], m[3 * k + 1:3 * k + 2]
        n, r = _rms(s_ref[...])
        dxn = d_u * (1.0 + scale)
        out_ref[...] = ds_ref[...] + _rms_bwd(n, r, dxn * gain)
        _accumulate([dsh_ref, dsc_ref, dgn_ref], [_colsum(d_u), _colsum(d_u * (n * gain)), _colsum(dxn * n)],
                    first(pl.program_id(0)))

    row = pl.BlockSpec((tm, d), lambda i: (i, 0))
    acc = pl.BlockSpec((1, 1, d), lambda i: (grp(i), 0, 0))
    res = pl.pallas_call(
        body, name=name, grid=(t_rows // tm,),
        in_specs=[pl.BlockSpec((tm, a.shape[1]), lambda i: (i, 0)) for a, _ in pairs]
        + [pl.BlockSpec(w.shape, lambda i: (0, 0)) for _, w in pairs]
        + [row, row, pl.BlockSpec((1,) + mg.shape[1:], lambda i: (grp(i), 0, 0)), pl.BlockSpec((1, d), lambda i: (0, 0))],
        out_specs=[row, acc, acc, acc],
        out_shape=[SDS((t_rows, d), F32)] + [SDS((n_groups, 1, d), F32)] * 3, compiler_params=_cparams(),
    )(*[a for a, _ in pairs], *[w for _, w in pairs], s, ds_out, mg, after)
    return res[0], res[1:]


def _adaln_mm(name, s, mg, k, n_lat, w_t):
    rows, d = s.shape
    tm = _row_tm(rows, n_lat)
    _, grp, _ = _groups(rows, tm, n_lat)
    n = w_t.shape[0]

    def body(s_ref, mg_ref, w_ref, u_ref, y_ref):
        uu = _adaln_math(s_ref[...], mg_ref[0], k).astype(BF16)
        u_ref[...] = uu
        y_ref[...] = _dot(uu, w_ref[...], NT)

    row = pl.BlockSpec((tm, d), lambda i: (i, 0))
    return pl.pallas_call(
        body, name=name, grid=(rows // tm,),
        in_specs=[row, pl.BlockSpec((1,) + mg.shape[1:], lambda i: (grp(i), 0, 0)), pl.BlockSpec(w_t.shape, lambda i: (0, 0))],
        out_specs=[row, pl.BlockSpec((tm, n), lambda i: (i, 0))],
        out_shape=[SDS((rows, d), BF16), SDS((rows, n), F32)], compiler_params=_cparams(),
    )(s, mg, w_t)


def _gate_mm(name, ds_out, o, mg, k, coef, n_lat, w):
    t_rows, d = ds_out.shape
    tm = _row_tm(t_rows, n_lat)
    n_groups, grp, first = _groups(t_rows, tm, n_lat)
    n = w.shape[0]

    def body(ds_ref, o_ref, mg_ref, w_ref, do_ref, y_ref, dg_ref):
        dd = coef * ds_ref[...]
        do = (dd * mg_ref[0, 3 * k + 2:3 * k + 3, :]).astype(BF16)
        do_ref[...] = do
        _accumulate([dg_ref], [_colsum(dd * o_ref[...].astype(F32))], first(pl.program_id(0)))
        y_ref[...] = _dot(do, w_ref[...], NT)

    row = pl.BlockSpec((tm, d), lambda i: (i, 0))
    return pl.pallas_call(
        body, name=name, grid=(t_rows // tm,),
        in_specs=[row, row, pl.BlockSpec((1,) + mg.shape[1:], lambda i: (grp(i), 0, 0)), pl.BlockSpec(w.shape, lambda i: (0, 0))],
        out_specs=[row, pl.BlockSpec((tm, n), lambda i: (i, 0)), pl.BlockSpec((1, 1, d), lambda i: (grp(i), 0, 0))],
        out_shape=[SDS((t_rows, d), BF16), SDS((t_rows, n), F32), SDS((n_groups, 1, d), F32)],
        compiler_params=_cparams(),
    )(ds_out, o, mg, w)


def _row_tm(t_rows, n_lat):
    return _tile(math.gcd(t_rows, n_lat), 256, 16)


def _rmsnorm_fwd(name, x, width, colblk, gain, t_rows):
    def fn(rv, vv):
        n, _ = _rms(rv[0])
        return [n * vv[0]], []

    (y,), _ = _rowwise(name, fn, t_rows, _tile(t_rows, 256, 16), t_rows, [(x, width, colblk)],
                       [gain.reshape(1, 1, width)], [(width, BF16)], [])
    return y


def _rmsnorm_bwd(name, x, width, colblk, dy, gain, t_rows):
    def fn(rv, vv):
        n, r = _rms(rv[0])
        return [_rms_bwd(n, r, rv[1] * vv[0])], [_colsum(rv[1] * n)]

    (dx,), (dgain,) = _rowwise(name, fn, t_rows, _tile(t_rows, 256, 16), t_rows,
                               [(x, width, colblk), (dy, width, 0)], [gain.reshape(1, 1, width)],
                               [(width, F32)], [width])
    return dx, dgain


def _final_loss(name, h, target, gain):
    t_rows = h.shape[0]
    inv_d = 1.0 / D_MODEL

    def fn(rv, vv):
        g = vv[0]
        n, r = _rms(rv[0])
        e = n * g - rv[1]
        dy = e * inv_d
        return [_rms_bwd(n, r, dy * g)], [_colsum(e * e), _colsum(dy * n)]

    (dh,), (sq, dgain) = _rowwise(name, fn, t_rows, _tile(t_rows, 256, 16), t_rows,
                                  [(h, D_MODEL, 0), (target, D_MODEL, 0)], [gain.reshape(1, 1, D_MODEL)],
                                  [(D_MODEL, F32)], [D_MODEL, D_MODEL])
    return dh, sq, dgain


def _rope(name, z, width, colblk, cos, sin, perm, backward, out_dtype):
    t_rows = cos.shape[0]

    def body(z_ref, c_ref, s_ref, p_ref, o_ref):
        zz = z_ref[...]
        pre = zz * s_ref[...] if backward else zz
        hi = pre.astype(BF16)
        lo = (pre - hi.astype(F32)).astype(BF16)
        rot = _dot(hi, p_ref[...], NN) + _dot(lo, p_ref[...], NN)
        if not backward:
            rot = rot * s_ref[...]
        o_ref[...] = (zz * c_ref[...] + rot).astype(o_ref.dtype)

    tm = _tile(t_rows, 256, 16)
    t_spec = pl.BlockSpec((tm, width), lambda i: (i, 0))
    return pl.pallas_call(
        body, name=name, grid=(t_rows // tm,),
        in_specs=[pl.BlockSpec((tm, width), lambda i: (i, colblk)), t_spec, t_spec,
                  pl.BlockSpec((width, width), lambda i: (0, 0))],
        out_specs=t_spec, out_shape=SDS((t_rows, width), out_dtype), compiler_params=_cparams(),
    )(z, cos, sin, perm)


def _window_sum(x, w, transposed):
    n_rows = x.shape[0]
    zeros = jnp.zeros((POOL_PAD, x.shape[1]), F32)
    y = jnp.concatenate([zeros, x, zeros], axis=0)
    total = n_rows + 2 * POOL_PAD
    if transposed:
        y = y + pltpu.roll(y, total - 1, 0)
    else:
        y = y + pltpu.roll(y, 1, 0)
    step = 1
    while 2 * step < w:
        y = pltpu.roll(y, step, 0) + pltpu.roll(y, total - step, 0)
        step *= 2
    return y[POOL_PAD:POOL_PAD + n_rows]


def _window_count(n_rows, w):
    t = lax.broadcasted_iota(jnp.int32, (n_rows, 1), 0)
    lo = jnp.maximum(t - w // 2, 0)
    hi = jnp.minimum(t + (w - w // 2 - 1), n_rows - 1)
    return (hi - lo + 1).astype(F32)


def _pool_fwd(name, proj, n_rows, w_grp, scale):
    def body(x_ref, w_ref, sc_ref, y_ref, p_ref):
        for g, w in enumerate(POOL_WINDOWS):
            cols = slice(g * POOL_GROUP_DIM, (g + 1) * POOL_GROUP_DIM)
            x = x_ref[:, cols]
            p = _window_sum(x, w, False) * (1.0 / _window_count(n_rows, w)) - x
            pb = p.astype(BF16)
            p_ref[:, cols] = pb
            y_ref[:, cols] = (_dot(pb, w_ref[g], NN) * sc_ref[:, cols]).astype(BF16)

    blk = pl.BlockSpec((n_rows, POOL_DIM), lambda i: (0, 0))
    return pl.pallas_call(
        body, name=name, grid=(1,),
        in_specs=[blk, pl.BlockSpec(w_grp.shape, lambda i: (0, 0, 0)), pl.BlockSpec((1, POOL_DIM), lambda i: (0, 0))],
        out_specs=[blk, blk], out_shape=[SDS((n_rows, POOL_DIM), BF16)] * 2, compiler_params=_cparams(),
    )(proj, w_grp, scale)


def _pool_bwd(name, dcat, n_rows, p, w_grp, scale):
    def body(dy_ref, p_ref, w_ref, sc_ref, dx_ref, dw_ref, dsc_ref):
        for g, w in enumerate(POOL_WINDOWS):
            cols = slice(g * POOL_GROUP_DIM, (g + 1) * POOL_GROUP_DIM)
            dy = dy_ref[:, cols]
            pb = p_ref[:, cols]
            pw = _dot(pb, w_ref[g], NN)
            dsc_ref[:, cols] = _colsum(dy * pw)
            dpw = (dy * sc_ref[:, cols]).astype(BF16)
            dw_ref[g] = _dot(pb, dpw, TN)
            dp = _dot(dpw, w_ref[g], NT)
            dx_ref[:, cols] = _window_sum(dp * (1.0 / _window_count(n_rows, w)), w, True) - dp

    blk = pl.BlockSpec((n_rows, POOL_DIM), lambda i: (0, 0))
    w_spec = pl.BlockSpec(w_grp.shape, lambda i: (0, 0, 0))
    v_spec = pl.BlockSpec((1, POOL_DIM), lambda i: (0, 0))
    return pl.pallas_call(
        body, name=name, grid=(1,), in_specs=[blk, blk, w_spec, v_spec], out_specs=[blk, w_spec, v_spec],
        out_shape=[SDS((n_rows, POOL_DIM), F32), SDS(w_grp.shape, F32), SDS((1, POOL_DIM), F32)],
        compiler_params=_cparams(),
    )(dcat, p, w_grp, scale)


def _attn_fwd(name, q, k, v):
    h, n_q, _ = q.shape
    n_k = k.shape[1]
    tq = _tile(n_q, 256, 16)

    def body(q_ref, k_ref, v_ref, o_ref, lse_ref):
        s = _dot(q_ref[...], k_ref[...], NT) * ATTN_SCALE
        m = jnp.max(s, axis=-1, keepdims=True)
        e = jnp.exp(s - m)
        l = jnp.sum(e, axis=-1, keepdims=True)
        p = (e * (1.0 / l)).astype(BF16)
        o_ref[...] = _dot(p, v_ref[...], NN).astype(BF16)
        lse_ref[...] = m + jnp.log(l)

    return pl.pallas_call(
        body, name=name, grid=(h, n_q // tq),
        in_specs=[pl.BlockSpec((None, tq, HEAD_PAD), lambda hh, i: (hh, i, 0)),
                  pl.BlockSpec((None, n_k, HEAD_PAD), lambda hh, i: (hh, 0, 0)),
                  pl.BlockSpec((None, n_k, V_HEAD), lambda hh, i: (hh, 0, 0))],
        out_specs=[pl.BlockSpec((None, tq, V_HEAD), lambda hh, i: (hh, i, 0)),
                   pl.BlockSpec((None, tq, 1), lambda hh, i: (hh, i, 0))],
        out_shape=[SDS((h, n_q, V_HEAD), BF16), SDS((h, n_q, 1), F32)], compiler_params=_cparams(),
    )(q, k, v)


def _attn_bwd(name, q, k, v, o, lse, do):
    h, n_q, _ = q.shape
    n_k = k.shape[1]
    tq = _tile(n_q, 256, 16)

    def body(q_ref, k_ref, v_ref, o_ref, lse_ref, do_ref, dq_ref, dk_ref, dv_ref, dks_ref):
        hh, i = pl.program_id(0), pl.program_id(1)
        qq, kk, dd = q_ref[...], k_ref[...], do_ref[...]
        s = _dot(qq, kk, NT) * ATTN_SCALE
        p = jnp.exp(s - lse_ref[...])
        dp = _dot(dd, v_ref[...], NT)
        delta = jnp.sum(dd.astype(F32) * o_ref[...].astype(F32), axis=-1, keepdims=True)
        ds = (p * (dp - delta) * ATTN_SCALE).astype(BF16)
        dq_ref[...] = _dot(ds, kk, NN)
        dk = _dot(ds, qq, TN)
        dv = _dot(p.astype(BF16), dd, TN)

        @pl.when(i == 0)
        def _():
            dk_ref[...] = dk
            dv_ref[...] = dv

        @pl.when(i > 0)
        def _():
            dk_ref[...] += dk
            dv_ref[...] += dv

        @pl.when((i == 0) & (hh == 0))
        def _():
            dks_ref[...] = dk

        @pl.when((i > 0) | (hh > 0))
        def _():
            dks_ref[...] += dk

    q_spec = pl.BlockSpec((None, tq, HEAD_PAD), lambda hh, i: (hh, i, 0))
    k_spec = pl.BlockSpec((None, n_k, HEAD_PAD), lambda hh, i: (hh, 0, 0))
    v_spec = pl.BlockSpec((None, n_k, V_HEAD), lambda hh, i: (hh, 0, 0))
    o_spec = pl.BlockSpec((None, tq, V_HEAD), lambda hh, i: (hh, i, 0))
    return pl.pallas_call(
        body, name=name, grid=(h, n_q // tq),
        in_specs=[q_spec, k_spec, v_spec, o_spec, pl.BlockSpec((None, tq, 1), lambda hh, i: (hh, i, 0)), o_spec],
        out_specs=[q_spec, k_spec, v_spec, pl.BlockSpec((n_k, HEAD_PAD), lambda hh, i: (0, 0))],
        out_shape=[SDS((h, n_q, HEAD_PAD), F32), SDS((h, n_k, HEAD_PAD), F32), SDS((h, n_k, V_HEAD), F32),
                   SDS((n_k, HEAD_PAD), F32)],
        compiler_params=_cparams(),
    )(q, k, v, o, lse, do)


CONV_COLS = 256


def _shift_rows(x, d):
    n_rows = x.shape[0]
    t = lax.broadcasted_iota(jnp.int32, (n_rows, 1), 0)
    if d > 0:
        return jnp.where(t >= d, pltpu.roll(x, d, 0), 0.0)
    return jnp.where(t < n_rows + d, pltpu.roll(x, n_rows + d, 0), 0.0)


def _conv_fwd(name, z3, conv_w):
    n_rows = z3.shape[0]
    nb = D_MODEL // CONV_COLS

    def body(b_ref, c_ref, v_ref, w_ref, y_ref):
        z = c_ref[...] * v_ref[...]
        zc = w_ref[0:1, :] * _shift_rows(z, 1) + w_ref[1:2, :] * z + w_ref[2:3, :] * _shift_rows(z, -1)
        y_ref[...] = (b_ref[...] * zc).astype(BF16)

    def part(k):
        return pl.BlockSpec((n_rows, CONV_COLS), lambda j: (0, k * nb + j))

    return pl.pallas_call(
        body, name=name, grid=(nb,),
        in_specs=[part(0), part(1), part(2), pl.BlockSpec((3, CONV_COLS), lambda j: (0, j))],
        out_specs=pl.BlockSpec((n_rows, CONV_COLS), lambda j: (0, j)),
        out_shape=SDS((n_rows, D_MODEL), BF16), compiler_params=_cparams(),
    )(z3, z3, z3, conv_w)


def _conv_bwd(name, dy, z3, conv_w):
    n_rows = z3.shape[0]
    nb = D_MODEL // CONV_COLS

    def body(dy_ref, b_ref, c_ref, v_ref, w_ref, db_ref, dc_ref, dv_ref, dw_ref):
        c, v, d_y = c_ref[...], v_ref[...], dy_ref[...]
        z = c * v
        z_dn, z_up = _shift_rows(z, 1), _shift_rows(z, -1)
        zc = w_ref[0:1, :] * z_dn + w_ref[1:2, :] * z + w_ref[2:3, :] * z_up
        db_ref[...] = (d_y * zc).astype(BF16)
        dzc = d_y * b_ref[...]
        dz = w_ref[0:1, :] * _shift_rows(dzc, -1) + w_ref[1:2, :] * dzc + w_ref[2:3, :] * _shift_rows(dzc, 1)
        dc_ref[...] = (dz * v).astype(BF16)
        dv_ref[...] = (dz * c).astype(BF16)
        dw_ref[0:1, :] = _colsum(dzc * z_dn)
        dw_ref[1:2, :] = _colsum(dzc * z)
        dw_ref[2:3, :] = _colsum(dzc * z_up)

    def part(k):
        return pl.BlockSpec((n_rows, CONV_COLS), lambda j: (0, k * nb + j))

    col = pl.BlockSpec((n_rows, CONV_COLS), lambda j: (0, j))
    w_spec = pl.BlockSpec((3, CONV_COLS), lambda j: (0, j))
    return pl.pallas_call(
        body, name=name, grid=(nb,), in_specs=[col, part(0), part(1), part(2), w_spec],
        out_specs=[col, col, col, w_spec],
        out_shape=[SDS((n_rows, D_MODEL), BF16)] * 3 + [SDS((3, D_MODEL), F32)], compiler_params=_cparams(),
    )(dy, z3, z3, z3, conv_w)


def _silu_rows(name, x):
    def body(x_ref, s_ref, d_ref):
        xx = x_ref[...]
        sg = jax.nn.sigmoid(xx)
        s_ref[...] = (xx * sg).astype(BF16)
        d_ref[...] = sg * (1.0 + xx * (1.0 - sg))

    return pl.pallas_call(body, name=name, out_shape=[SDS(x.shape, BF16), SDS(x.shape, F32)])(x)


def _sum_rows(name, x, scale=None):
    r, n = x.shape
    tn = _tile(n, 8192, 128)

    def body(*refs):
        acc = jnp.sum(refs[0][...].astype(F32), axis=0, keepdims=True)
        if scale is not None:
            acc = acc * refs[1][...]
        refs[-1][...] = acc

    in_specs = [pl.BlockSpec((r, tn), lambda j: (0, j))]
    args = [x]
    if scale is not None:
        in_specs.append(pl.BlockSpec((1, tn), lambda j: (0, j)))
        args.append(scale)
    return pl.pallas_call(body, name=name, grid=(n // tn,), in_specs=in_specs,
                          out_specs=pl.BlockSpec((1, tn), lambda j: (0, j)), out_shape=SDS((1, n), F32))(*args)


def _sum_slots(name, x):
    n_slots, r, c = x.shape
    tr = _tile(r, 432, 16)

    def body(x_ref, o_ref):
        acc = x_ref[0].astype(F32)
        for sl in range(1, n_slots):
            acc = acc + x_ref[sl].astype(F32)
        o_ref[...] = acc

    return pl.pallas_call(body, name=name, grid=(r // tr,),
                          in_specs=[pl.BlockSpec((n_slots, tr, c), lambda i: (0, i, 0))],
                          out_specs=pl.BlockSpec((tr, c), lambda i: (i, 0)), out_shape=SDS((r, c), F32),
                          compiler_params=_cparams())(x)


def _adamw(name, w, g, m, v):
    shape = w.shape
    cols = shape[-1]
    rows = w.size // cols
    tr = _tile(rows, 512, 8)
    bc1 = 1.0 - ADAM_B1 ** ADAM_STEP
    bc2 = 1.0 - ADAM_B2 ** ADAM_STEP

    def body(w_ref, g_ref, m_ref, v_ref, d_ref, nm_ref, nv_ref):
        gg = g_ref[...]
        nm = ADAM_B1 * m_ref[...] + (1.0 - ADAM_B1) * gg
        nv = ADAM_B2 * v_ref[...] + (1.0 - ADAM_B2) * (gg * gg)
        nm_ref[...] = nm
        nv_ref[...] = nv
        d_ref[...] = -ADAM_LR * ((nm / bc1) / (jnp.sqrt(nv / bc2) + ADAM_EPS) + ADAM_WD * w_ref[...])

    spec = pl.BlockSpec((tr, cols), lambda i: (i, 0))
    outs = pl.pallas_call(body, name=name, grid=(rows // tr,), in_specs=[spec] * 4, out_specs=[spec] * 3,
                          out_shape=[SDS((rows, cols), F32)] * 3, compiler_params=_cparams())(
        w.reshape(rows, cols), g.reshape(rows, cols), m.reshape(rows, cols), v.reshape(rows, cols))
    return tuple(t.reshape(shape) for t in outs)


def _exchange(name, x, scatter, after=None):
    blk = x.shape[1:] if scatter else x.shape
    extra = [] if after is None else [after]

    def body(x_ref, *rest):
        out_ref, send_sems, recv_sems, local_sem = rest[len(extra):]
        mx, my, mc = lax.axis_index("x"), lax.axis_index("y"), lax.axis_index("c")
        me = 4 * mx + 2 * my + mc
        own = pltpu.make_async_copy(x_ref.at[me] if scatter else x_ref, out_ref.at[me], local_sem)
        own.start()
        copies = []
        for kk in range(1, N_DEV):
            px = jnp.bitwise_xor(mx, (kk >> 2) & 1)
            py = jnp.bitwise_xor(my, (kk >> 1) & 1)
            pc = jnp.bitwise_xor(mc, kk & 1)
            peer = 4 * px + 2 * py + pc
            send = pltpu.make_async_remote_copy(
                src_ref=x_ref.at[peer] if scatter else x_ref, dst_ref=out_ref.at[me],
                send_sem=send_sems.at[kk - 1], recv_sem=recv_sems.at[kk - 1],
                device_id=(px, py, pc), device_id_type=MESH)
            send.start()
            arrival = pltpu.make_async_remote_copy(
                src_ref=x_ref.at[peer] if scatter else x_ref, dst_ref=out_ref.at[peer],
                send_sem=send_sems.at[kk - 1], recv_sem=recv_sems.at[kk - 1],
                device_id=(px, py, pc), device_id_type=MESH)
            copies.append((send, arrival))
        for send, arrival in copies:
            arrival.wait_recv()
            send.wait_send()
        own.wait()

    return pl.pallas_call(
        body, name=name, out_shape=SDS((N_DEV,) + tuple(blk), x.dtype),
        in_specs=[pl.BlockSpec(memory_space=pl.ANY)] * (1 + len(extra)), out_specs=pl.BlockSpec(memory_space=pl.ANY),
        scratch_shapes=[pltpu.SemaphoreType.DMA((N_DEV - 1,)), pltpu.SemaphoreType.DMA((N_DEV - 1,)),
                        pltpu.SemaphoreType.DMA],
    )(x, *extra)


def _rope_perm(pre, reps, post):
    half = QK_ROPE // 4
    width = reps * (pre + QK_ROPE) + post
    p = np.zeros((width, width), np.float32)
    for rep in range(reps):
        s0 = rep * (pre + QK_ROPE) + pre
        for base in (s0, s0 + 2 * half):
            for i in range(half):
                p[base + half + i, base + i] = -1.0
                p[base + i, base + half + i] = 1.0
    return p


def _rope_tables(n_lat, t_rows, pre, reps, post):
    half = QK_ROPE // 4
    pos = jnp.arange(n_lat)
    freqs = jnp.power(ROPE_THETA, -jnp.arange(0, 2 * half, 2, dtype=F32) / (2 * half))
    ang_r = (pos // GRID_W).astype(F32)[:, None] * freqs
    ang_c = (pos % GRID_W).astype(F32)[:, None] * freqs
    ang = jnp.concatenate([ang_r, ang_r, ang_c, ang_c], axis=-1)

    def table(fn, plain):
        slot = jnp.concatenate([jnp.full((n_lat, pre), plain, F32), fn(ang)], axis=-1)
        t = jnp.concatenate([jnp.tile(slot, (1, reps)), jnp.full((n_lat, post), plain, F32)], axis=-1)
        return jnp.concatenate([t, jnp.full((t_rows - n_lat, t.shape[1]), plain, F32)], axis=0)

    return table(jnp.cos, 1.0), table(jnp.sin, 0.0)


def _ffn_half_fwd(tag, s, mg, k, feed, i, coef, n_lat):
    wg_t, wu_t = feed.weights(f"{tag}_up", [f"gate_t{i}", f"up_t{i}"], s)
    u, a, b, hid = _ffn_up(f"{tag}_up", s, mg, k, n_lat, wg_t, wu_t)
    (wd,) = feed.weights(f"{tag}_down", [f"down{i}"], hid)
    s_out, o = _mm_resid(f"{tag}_down", hid, wd, s, mg, k, coef, n_lat)
    return s_out, (s, u, a, b, hid, o, wg_t, wu_t, wd)


def _ffn_half_bwd(tag, ds_out, saved, mg, k, feed, i, coef, n_lat):
    s, u, a, b, hid, o, wg_t, wu_t, wd = saved
    do, da, db, dgate = _ffn_dact(f"{tag}_dact", ds_out, o, mg, k, coef, n_lat, wd, a, b)
    dwd = _mm(f"{tag}_dwd", [(hid, do)], "tn", BF16)
    dwg_t = _mm(f"{tag}_dwg", [(da, u)], "tn", BF16)
    dwu_t = _mm(f"{tag}_dwu", [(db, u)], "tn", BF16)
    token = feed.grads(tag, {f"down{i}": dwd, f"gate_t{i}": dwg_t, f"up_t{i}": dwu_t})
    ds_in, (dshift, dscale, dgain) = _du_adaln(f"{tag}_du", [(da, wg_t), (db, wu_t)], s, ds_out, mg, k, n_lat,
                                               _after(token))
    return ds_in, dict(shift=dshift, scale=dscale, gate=dgate, gain=dgain)


def _after(token):
    return jnp.zeros((1, D_MODEL), F32) + token


def _mod_grad(parts, n_groups):
    rows = []
    zero = jnp.zeros((n_groups, 1, D_MODEL), F32)
    for k in range(3):
        for nm in ("shift", "scale", "gate"):
            t = parts[k].get(nm, zero)
            if t.shape[0] < n_groups:
                t = jnp.concatenate([t, jnp.zeros((n_groups - t.shape[0], 1, D_MODEL), F32)], axis=0)
            rows.append(t)
    return jnp.concatenate(rows, axis=1).reshape(n_groups, N_MOD * D_MODEL)


def _local_step(x, ctx, target, mod_h, mod_g, norm_g, feed, pool_w, pool_scale, q_norm_g, kv_norm_g, conv_w,
                final_norm_g):
    n_lat, n_ctx = x.shape[0], ctx.shape[0]
    t_all = n_lat + n_ctx
    mg0 = jnp.stack([jnp.concatenate([mod_h[0], norm_g[0]], axis=0), jnp.concatenate([mod_g, norm_g[0]], axis=0)])
    mg1 = jnp.concatenate([mod_h[1], norm_g[1]], axis=0)[None]

    s0 = jnp.concatenate([x, ctx], axis=0)
    s1, sv_f00 = _ffn_half_fwd("l0f0", s0, mg0, 0, feed, 0, 0.5, n_lat)

    w_in, w_uq, w_ukv_t, w_ab_out = feed.weights("l0m", ["in_t", "uq", "ukv_t", "ab_out"], s1)
    kv_rows = KV_RANK + QK_ROPE
    w_in_t = jnp.concatenate([
        w_in[:POOL_DIM], jnp.zeros((PA_CQ - POOL_DIM, D_MODEL), BF16), w_in[POOL_DIM:POOL_DIM + Q_RANK],
        w_in[POOL_DIM + Q_RANK:], jnp.zeros((PA_KV_W - kv_rows, D_MODEL), BF16)], axis=0)
    ua, proj = _adaln_mm("l0m_proj", s1, mg0, 1, n_lat, w_in_t)
    pool_y, pool_p = _pool_fwd("l0m_pool", proj, n_lat, pool_w.astype(BF16), pool_scale)
    nq = _rmsnorm_fwd("l0m_qnorm", proj, Q_RANK, PA_CQ // Q_RANK, q_norm_g, n_lat)
    q_lin = _mm("l0m_q", [(nq, w_uq)], "nn", F32, 512, 768)
    cos_q, sin_q = _rope_tables(n_lat, n_lat, QK_NOPE, HEADS, 0)
    perm_q = _rope_perm(QK_NOPE, HEADS, 0)
    q_rot = _rope("l0m_qrope", q_lin, Q_RANK, 0, cos_q, sin_q, jnp.asarray(perm_q, BF16), False, BF16)
    cos_k, sin_k = _rope_tables(n_lat, t_all, KV_RANK, 1, PA_KV_W - kv_rows)
    perm_k = _rope_perm(KV_RANK, 1, PA_KV_W - kv_rows)
    kvr = _rope("l0m_krope", proj, PA_KV_W, PA_KV // PA_KV_W, cos_k, sin_k, jnp.asarray(perm_k, BF16), False, F32)
    nkv = _rmsnorm_fwd("l0m_kvnorm", kvr, KV_RANK, 0, kv_norm_g, t_all)
    kv = _mm("l0m_kv", [(nkv, w_ukv_t)], "nt", BF16, 768, 512)
    qh = jnp.pad(q_rot.reshape(n_lat, HEADS, QK_HEAD), ((0, 0), (0, 0), (0, HEAD_PAD - QK_HEAD))).transpose(1, 0, 2)
    kvh = kv.reshape(t_all, HEADS, QK_NOPE + V_HEAD)
    k_rope = jnp.broadcast_to(kvr[:, None, KV_RANK:KV_RANK + QK_ROPE].astype(BF16), (t_all, HEADS, QK_ROPE))
    kh = jnp.concatenate([kvh[:, :, :QK_NOPE], k_rope, jnp.zeros((t_all, HEADS, HEAD_PAD - QK_HEAD), BF16)],
                         axis=-1).transpose(1, 0, 2)
    vh = kvh[:, :, QK_NOPE:].transpose(1, 0, 2)
    oh, lse = _attn_fwd("l0m_attn", qh, kh, vh)
    cat = jnp.concatenate([pool_y, oh.transpose(1, 0, 2).reshape(n_lat, HEADS * V_HEAD)], axis=-1)
    h1 = s1[:n_lat]
    h2, mix_o = _mm_resid("l0m_out", cat, w_ab_out, h1, mg0[:1], 1, 1.0, n_lat)

    h3, sv_f01 = _ffn_half_fwd("l0f1", h2, mg0[:1], 2, feed, 1, 0.5, n_lat)

    h4, sv_f10 = _ffn_half_fwd("l1f0", h3, mg1, 0, feed, 2, 0.5, n_lat)
    w_cin_t, w_c_out = feed.weights("l1m", ["cin_t", "c_out"], h4)
    uc, z3 = _adaln_mm("l1m_in", h4, mg1, 1, n_lat, w_cin_t)
    yc = _conv_fwd("l1m_conv", z3, conv_w)
    h5, conv_o = _mm_resid("l1m_out", yc, w_c_out, h4, mg1, 1, 1.0, n_lat)
    h6, sv_f11 = _ffn_half_fwd("l1f1", h5, mg1, 2, feed, 3, 0.5, n_lat)

    dh6, sq_cols, d_final_g = _final_loss("loss_head", h6, target, final_norm_g)
    g = {}
    dh5, g["f11"] = _ffn_half_bwd("l1f1", dh6, sv_f11, mg1, 2, feed, 3, 0.5, n_lat)

    do_c, dyc, dgate_c = _gate_mm("l1m_dy", dh5, conv_o, mg1, 1, 1.0, n_lat, w_c_out)
    d_c_out = _mm("l1m_dwout", [(yc, do_c)], "tn", BF16)
    db_, dc_, dv_, d_conv_w = _conv_bwd("l1m_dconv", dyc, z3, conv_w)
    dz3 = jnp.concatenate([db_, dc_, dv_], axis=-1)
    d_cin_t = _mm("l1m_dwin", [(dz3, uc)], "tn", BF16)
    token = feed.grads("l1m", {"c_out": d_c_out, "cin_t": d_cin_t})
    dh4, (dsh_c, dsc_c, dgn_c) = _du_adaln("l1m_du", [(dz3, w_cin_t)], h4, dh5, mg1, 1, n_lat, _after(token))
    dh3, g["f10"] = _ffn_half_bwd("l1f0", dh4, sv_f10, mg1, 0, feed, 2, 0.5, n_lat)

    dh2, g["f01"] = _ffn_half_bwd("l0f1", dh3, sv_f01, mg0[:1], 2, feed, 1, 0.5, n_lat)

    do_a, dcat, dgate_a = _gate_mm("l0m_dcat", dh2, mix_o, mg0[:1], 1, 1.0, n_lat, w_ab_out)
    d_ab_out = _mm("l0m_dwout", [(cat, do_a)], "tn", BF16)
    d_pool_x, d_pool_w, d_pool_scale = _pool_bwd("l0m_dpool", dcat, n_lat, pool_p, pool_w.astype(BF16), pool_scale)
    doh = dcat[:, POOL_DIM:].reshape(n_lat, HEADS, V_HEAD).transpose(1, 0, 2).astype(BF16)
    dqh, dkh, dvh, dk_sum = _attn_bwd("l0m_dattn", qh, kh, vh, oh, lse, doh)
    dq_rot = dqh[:, :, :QK_HEAD].transpose(1, 0, 2).reshape(n_lat, Q_RANK)
    dq_lin = _rope("l0m_dqrope", dq_rot, Q_RANK, 0, cos_q, sin_q, jnp.asarray(perm_q.T, BF16), True, BF16)
    d_uq = _mm("l0m_dwuq", [(nq, dq_lin)], "tn", BF16, 768, 768)
    dnq = _mm("l0m_dnq", [(dq_lin, w_uq)], "nt", F32, 512, 768)
    dcq, d_q_norm_g = _rmsnorm_bwd("l0m_dqnorm", proj, Q_RANK, PA_CQ // Q_RANK, dnq, q_norm_g, n_lat)
    dkv = jnp.concatenate([dkh[:, :, :QK_NOPE], dvh], axis=-1).transpose(1, 0, 2).reshape(t_all, HEADS * HEAD_PAD)
    dkv = dkv.astype(BF16)
    dnkv = _mm("l0m_dnkv", [(dkv, w_ukv_t)], "nn", F32, 768, 256)
    d_ukv_t = _mm("l0m_dwukv", [(dkv, nkv)], "tn", BF16, 512, 256)
    dckv, d_kv_norm_g = _rmsnorm_bwd("l0m_dkvnorm", kvr, KV_RANK, 0, dnkv, kv_norm_g, t_all)
    dkvr = jnp.concatenate([dckv, dk_sum[:, QK_NOPE:QK_HEAD],
                            jnp.zeros((t_all, PA_KV_W - KV_RANK - QK_ROPE), F32)], axis=-1)
    dpb = _rope("l0m_dkrope", dkvr, PA_KV_W, 0, cos_k, sin_k, jnp.asarray(perm_k.T, BF16), True, F32)
    dproj_lat = jnp.concatenate([d_pool_x, jnp.zeros((n_lat, PA_CQ - POOL_DIM), F32), dcq, dpb[:n_lat]], axis=-1)
    dproj_ctx = jnp.concatenate([jnp.zeros((n_ctx, PA_KV), F32), dpb[n_lat:]], axis=-1)
    dproj = jnp.concatenate([dproj_lat, dproj_ctx], axis=0).astype(BF16)
    d_in_pad = _mm("l0m_dwin", [(dproj, ua)], "tn", BF16, 640, 512)
    d_in_t = jnp.concatenate([d_in_pad[:POOL_DIM], d_in_pad[PA_CQ:PA_CQ + Q_RANK],
                              d_in_pad[PA_KV:PA_KV + kv_rows]], axis=0)
    token = feed.grads("l0m", {"ab_out": d_ab_out, "uq": d_uq, "ukv_t": d_ukv_t, "in_t": d_in_t})
    dh2_all = jnp.concatenate([dh2, jnp.zeros((n_ctx, D_MODEL), F32)], axis=0)
    ds1, (dsh_a, dsc_a, dgn_a) = _du_adaln("l0m_du", [(dproj, w_in_t)], s1, dh2_all, mg0, 1, n_lat, _after(token))
    ds0, g["f00"] = _ffn_half_bwd("l0f0", ds1, sv_f00, mg0, 0, feed, 0, 0.5, n_lat)

    dmod0 = _mod_grad([g["f00"], dict(shift=dsh_a, scale=dsc_a, gate=dgate_a), g["f01"]], 2)
    dmod1 = _mod_grad([g["f10"], dict(shift=dsh_c, scale=dsc_c, gate=dgate_c), g["f11"]], 1)
    d_norm_g = jnp.stack([
        jnp.concatenate([jnp.sum(g["f00"]["gain"], axis=0), jnp.sum(dgn_a, axis=0), g["f01"]["gain"][0]], axis=0),
        jnp.concatenate([g["f10"]["gain"][0], dgn_c[0], g["f11"]["gain"][0]], axis=0)])
    grads = dict(
        pool_w=d_pool_w, pool_scale=d_pool_scale, q_norm_g=d_q_norm_g[0], kv_norm_g=d_kv_norm_g[0],
        conv_w=d_conv_w, final_norm_g=d_final_g[0], norm_g=d_norm_g,
        mod_h=jnp.stack([dmod0[0], dmod1[0]]), mod_g=dmod0[1])
    return sq_cols, ds0, grads


HBM_SPEC = pl.BlockSpec(memory_space=pltpu.HBM)
SEM_SPEC = pl.BlockSpec(memory_space=pltpu.SEMAPHORE)
ANY_SPEC = pl.BlockSpec(memory_space=pl.ANY)
SIDE_EFFECT = pltpu.SideEffectType.DATAFLOW_SIDE_EFFECTING
N_PEERS = N_DEV - 1


def _mesh_place():
    mx, my, mc = lax.axis_index("x"), lax.axis_index("y"), lax.axis_index("c")
    return mx, my, mc, 4 * mx + 2 * my + mc


def _peer(place, kk):
    mx, my, mc, _ = place
    px = jnp.bitwise_xor(mx, (kk >> 2) & 1)
    py = jnp.bitwise_xor(my, (kk >> 1) & 1)
    pc = jnp.bitwise_xor(mc, kk & 1)
    return (px, py, pc), 4 * px + 2 * py + pc


def _hbm(a):
    return pltpu.with_memory_space_constraint(a, pltpu.HBM)


def _landing(block, me):
    zone = lax.empty((N_DEV,) + block.shape, block.dtype)
    return lax.dynamic_update_slice(zone, block[None], (me,) + (0,) * block.ndim)


ALL_PEERS = tuple(range(1, N_DEV))
SIBLING = 1
CHIP_PEERS = (2, 4, 6)
RELAYED = (3, 5, 7)


def _exchange_start(name, srcs, lands, scatter, after, peers=ALL_PEERS):
    n = len(srcs)

    def body(*refs):
        src, land = refs[:n], refs[n:2 * n]
        send_sems, recv_sems, token = refs[2 * n + 1], refs[2 * n + 2], refs[-1]
        place = _mesh_place()
        for a in range(n):
            for kk in peers:
                dev, peer = _peer(place, kk)
                pltpu.make_async_remote_copy(
                    src_ref=src[a].at[peer] if scatter else src[a], dst_ref=land[a].at[place[3]],
                    send_sem=send_sems.at[a * N_PEERS + kk - 1], recv_sem=recv_sems.at[a * N_PEERS + kk - 1],
                    device_id=dev, device_id_type=MESH).start()
        token[...] = jnp.zeros_like(token)

    thru = [pltpu.HBM(t.shape, t.dtype) for t in (*srcs, *lands)]
    res = pl.pallas_call(
        body, name=name,
        out_shape=(pltpu.SemaphoreType.DMA((n * N_PEERS,)), pltpu.SemaphoreType.DMA((n * N_PEERS,)), *thru,
                   SDS((8, 128), F32)),
        in_specs=[HBM_SPEC] * (2 * n) + [ANY_SPEC],
        out_specs=(SEM_SPEC, SEM_SPEC, *([HBM_SPEC] * (2 * n)), pl.BlockSpec(memory_space=pltpu.VMEM)),
        input_output_aliases={i: 2 + i for i in range(2 * n)},
        compiler_params=pltpu.CompilerParams(has_side_effects=SIDE_EFFECT),
    )(*[_hbm(s) for s in srcs], *[_hbm(t) for t in lands], after)
    return res[0], res[1], list(res[2:2 + n]), list(res[2 + n:2 + 2 * n]), res[-1]


def _exchange_wait(name, send_sems, recv_sems, srcs, lands, places, scatter, after):
    n = len(srcs)

    def body(*refs):
        src, land = refs[:n], refs[n:2 * n]
        send, recv = refs[2 * n], refs[2 * n + 1]
        place = _mesh_place()
        for a in range(n):
            for kk in range(1, N_DEV):
                dev, peer = _peer(place, kk)
                cp = pltpu.make_async_remote_copy(
                    src_ref=src[a].at[peer] if scatter else src[a], dst_ref=land[a].at[peer],
                    send_sem=send.at[places[a] * N_PEERS + kk - 1], recv_sem=recv.at[places[a] * N_PEERS + kk - 1],
                    device_id=dev, device_id_type=MESH)
                cp.wait_send()
                cp.wait_recv()

    thru = [pltpu.HBM(t.shape, t.dtype) for t in (*srcs, *lands)]
    res = pl.pallas_call(
        body, name=name, out_shape=tuple(thru),
        in_specs=[HBM_SPEC] * (2 * n) + [SEM_SPEC, SEM_SPEC] + [ANY_SPEC] * len(after),
        out_specs=tuple([HBM_SPEC] * (2 * n)), input_output_aliases={i: i for i in range(2 * n)},
        compiler_params=pltpu.CompilerParams(has_side_effects=SIDE_EFFECT),
    )(*srcs, *lands, send_sems, recv_sems, *after)
    return list(res[n:])


def _gather_relay(name, send1, recv1, lands, places, after):
    n = len(lands)

    def body(*refs):
        land, s1, r1 = refs[:n], refs[n], refs[n + 1]
        s2, r2 = refs[n + 3], refs[n + 4]
        place = _mesh_place()
        sibling = _peer(place, SIBLING)[0]
        for a in range(n):
            for j, kk in enumerate(CHIP_PEERS):
                dev, origin = _peer(place, kk)
                block = land[a].at[origin]
                pltpu.make_async_remote_copy(
                    src_ref=block, dst_ref=block, send_sem=s1.at[places[a] * N_PEERS + kk - 1],
                    recv_sem=r1.at[places[a] * N_PEERS + kk - 1], device_id=dev, device_id_type=MESH).wait_recv()
                pltpu.make_async_remote_copy(
                    src_ref=block, dst_ref=block, send_sem=s2.at[a * 3 + j], recv_sem=r2.at[a * 3 + j],
                    device_id=sibling, device_id_type=MESH).start()

    res = pl.pallas_call(
        body, name=name,
        out_shape=(pltpu.SemaphoreType.DMA((3 * n,)), pltpu.SemaphoreType.DMA((3 * n,)),
                   *[pltpu.HBM(t.shape, t.dtype) for t in lands]),
        in_specs=[HBM_SPEC] * n + [SEM_SPEC, SEM_SPEC, ANY_SPEC],
        out_specs=(SEM_SPEC, SEM_SPEC, *([HBM_SPEC] * n)),
        input_output_aliases={i: 2 + i for i in range(n)},
        compiler_params=pltpu.CompilerParams(has_side_effects=SIDE_EFFECT),
    )(*lands, send1, recv1, after)
    return res[0], res[1], list(res[2:])


def _gather_wait(name, send1, recv1, send2, recv2, srcs, lands, places, after):
    n = len(lands)

    def body(*refs):
        src, land = refs[:n], refs[n:2 * n]
        s1, r1, s2, r2 = refs[2 * n:2 * n + 4]
        place = _mesh_place()
        for a in range(n):
            for kk in (SIBLING,) + CHIP_PEERS:
                dev, origin = _peer(place, kk)
                first = pltpu.make_async_remote_copy(
                    src_ref=src[a], dst_ref=land[a].at[origin], send_sem=s1.at[places[a] * N_PEERS + kk - 1],
                    recv_sem=r1.at[places[a] * N_PEERS + kk - 1], device_id=dev, device_id_type=MESH)
                first.wait_send()
                if kk == SIBLING:
                    first.wait_recv()
            for j, kk in enumerate(CHIP_PEERS):
                dev, origin = _peer(place, kk + 1)
                relay = pltpu.make_async_remote_copy(
                    src_ref=src[a], dst_ref=land[a].at[origin], send_sem=s2.at[a * 3 + j], recv_sem=r2.at[a * 3 + j],
                    device_id=dev, device_id_type=MESH)
                relay.wait_send()
                relay.wait_recv()

    arrays = (*srcs, *lands)
    res = pl.pallas_call(
        body, name=name, out_shape=tuple(pltpu.HBM(t.shape, t.dtype) for t in arrays),
        in_specs=[HBM_SPEC] * (2 * n) + [SEM_SPEC] * 4 + [ANY_SPEC], out_specs=tuple([HBM_SPEC] * (2 * n)),
        input_output_aliases={i: i for i in range(2 * n)},
        compiler_params=pltpu.CompilerParams(has_side_effects=SIDE_EFFECT),
    )(*arrays, send1, recv1, send2, recv2, after)
    return list(res[n:])


class _Feed:
    def __init__(self, shards, groups, me):
        self.shards, self.groups, self.me, self.pos = shards, groups, me, 0
        self.sems, self.srcs, self.lands = {}, {}, {}
        self.relays = {}
        self.pending = []

    def start(self, tag, names, after):
        srcs = [self.shards[nm] for nm in names]
        lands = [_landing(s, self.me) for s in srcs]
        send, recv, srcs, lands, self.token = _exchange_start(
            f"gather_start_{tag}", srcs, lands, False, after, (SIBLING,) + CHIP_PEERS)
        for i, nm in enumerate(names):
            self.sems[nm], self.srcs[nm], self.lands[nm] = (send, recv, i), srcs[i], lands[i]
        return self.token

    def _relay(self, gi, after):
        names = self.groups[gi]
        if gi not in self.relays:
            send, recv, _ = self.sems[names[0]]
            places = [self.sems[nm][2] for nm in names]
            send2, recv2, lands = _gather_relay(f"gather_relay_{gi}", send, recv, [self.lands[nm] for nm in names],
                                                places, after)
            for nm, t in zip(names, lands):
                self.lands[nm] = t
            self.relays[gi] = (send2, recv2)
            after = lands[0]
        return after

    def weights(self, tag, names, after):
        gi = self.pos
        assert names == self.groups[gi], (names, self.groups[gi])
        if gi == 0:
            after = self.token
        self._relay(gi, after)
        if 1 <= gi < len(self.groups) - 1:
            after = self._relay(gi + 1, after)
        send2, recv2 = self.relays[gi]
        send, recv, _ = self.sems[names[0]]
        got = _gather_wait(f"gather_wait_{tag}", send, recv, send2, recv2, [self.srcs[nm] for nm in names],
                           [self.lands[nm] for nm in names], [self.sems[nm][2] for nm in names], after)
        self.pos += 1
        return [t.reshape((N_DEV * t.shape[1],) + t.shape[2:]) for t in got]

    def grads(self, tag, full):
        names = list(full)
        srcs = [full[nm].reshape((N_DEV, full[nm].shape[0] // N_DEV) + full[nm].shape[1:]) for nm in names]
        lands = [_landing(lax.dynamic_index_in_dim(s, self.me, 0, keepdims=False), self.me) for s in srcs]
        send, recv, srcs, lands, token = _exchange_start(f"scatter_start_{tag}", srcs, lands, True, srcs[0])
        self.pending.append((tag, names, send, recv, srcs, lands))
        return token[0, 0]

    def collect(self, tags, after, keep_slots=()):
        out = {}
        for tag, names, send, recv, srcs, lands in self.pending:
            if tag not in tags:
                continue
            got = _exchange_wait(f"scatter_wait_{tag}", send, recv, srcs, lands, list(range(len(names))), True, after)
            for nm, slots in zip(names, got):
                out[nm] = slots if nm.startswith(tuple(keep_slots)) else _sum_slots(f"reduce_{nm}", slots)
        return out


def _adamw_math(w, gg, m, v):
    nm = ADAM_B1 * m + (1.0 - ADAM_B1) * gg
    nv = ADAM_B2 * v + (1.0 - ADAM_B2) * (gg * gg)
    bc1 = 1.0 - ADAM_B1 ** ADAM_STEP
    bc2 = 1.0 - ADAM_B2 ** ADAM_STEP
    return -ADAM_LR * ((nm / bc1) / (jnp.sqrt(nv / bc2) + ADAM_EPS) + ADAM_WD * w), nm, nv


def _adamw_part(name, i, w, slots, m, v, prev):
    n_parts, rows, cols = w.shape
    tr = _tile(rows, 256, 16)
    if prev is None:
        prev = tuple(lax.empty(w.shape, F32) for _ in range(4))

    def body(w_ref, g_ref, m_ref, v_ref, *rest):
        go_ref, d_ref, nm_ref, nv_ref = rest[4:]
        gg = g_ref[0].astype(F32)
        for sl in range(1, N_DEV):
            gg = gg + g_ref[sl].astype(F32)
        d, nm, nv = _adamw_math(w_ref[...], gg, m_ref[...], v_ref[...])
        go_ref[...] = gg
        d_ref[...] = d
        nm_ref[...] = nm
        nv_ref[...] = nv

    part = pl.BlockSpec((None, tr, cols), lambda r: (i, r, 0))
    return pl.pallas_call(
        body, name=name, grid=(rows // tr,),
        in_specs=[part, pl.BlockSpec((N_DEV, tr, cols), lambda r: (0, r, 0)), part, part] + [ANY_SPEC] * 4,
        out_specs=[part] * 4, out_shape=[SDS(w.shape, F32)] * 4,
        input_output_aliases={4 + k: k for k in range(4)}, compiler_params=_cparams(),
    )(w, slots, m, v, *prev)


WEIGHT_NAMES = ("c_ctx", "norm_g", "w_mod", "b_mod", "ffn_w_gate", "ffn_w_up", "ffn_w_down", "ab_w_in", "pool_w",
                "pool_scale", "q_norm_g", "w_uq", "kv_norm_g", "w_ukv", "ab_w_out", "conv_w_in", "conv_w",
                "conv_w_out", "final_norm_g")


def kernel(x, c, ctx, c_ctx, norm_g, w_mod, b_mod, ffn_w_gate, ffn_w_up, ffn_w_down, ab_w_in, pool_w, pool_scale, q_norm_g, w_uq, kv_norm_g, w_ukv, ab_w_out, conv_w_in, conv_w, conv_w_out, final_norm_g, loss_target, m_c_ctx, m_norm_g, m_w_mod, m_b_mod, m_ffn_w_gate, m_ffn_w_up, m_ffn_w_down, m_ab_w_in, m_pool_w, m_pool_scale, m_q_norm_g, m_w_uq, m_kv_norm_g, m_w_ukv, m_ab_w_out, m_conv_w_in, m_conv_w, m_conv_w_out, m_final_norm_g, v_c_ctx, v_norm_g, v_w_mod, v_b_mod, v_ffn_w_gate, v_ffn_w_up, v_ffn_w_down, v_ab_w_in, v_pool_w, v_pool_scale, v_q_norm_g, v_w_uq, v_kv_norm_g, v_w_ukv, v_ab_w_out, v_conv_w_in, v_conv_w, v_conv_w_out, v_final_norm_g):
    weights = (c_ctx, norm_g, w_mod, b_mod, ffn_w_gate, ffn_w_up, ffn_w_down, ab_w_in, pool_w, pool_scale, q_norm_g,
               w_uq, kv_norm_g, w_ukv, ab_w_out, conv_w_in, conv_w, conv_w_out, final_norm_g)
    moms = (m_c_ctx, m_norm_g, m_w_mod, m_b_mod, m_ffn_w_gate, m_ffn_w_up, m_ffn_w_down, m_ab_w_in, m_pool_w,
            m_pool_scale, m_q_norm_g, m_w_uq, m_kv_norm_g, m_w_ukv, m_ab_w_out, m_conv_w_in, m_conv_w, m_conv_w_out,
            m_final_norm_g)
    vels = (v_c_ctx, v_norm_g, v_w_mod, v_b_mod, v_ffn_w_gate, v_ffn_w_up, v_ffn_w_down, v_ab_w_in, v_pool_w,
            v_pool_scale, v_q_norm_g, v_w_uq, v_kv_norm_g, v_w_ukv, v_ab_w_out, v_conv_w_in, v_conv_w, v_conv_w_out,
            v_final_norm_g)
    me = 4 * lax.axis_index("x") + 2 * lax.axis_index("y") + lax.axis_index("c")
    n_lat, n_ctx = x.shape[1], ctx.shape[1]
    d = D_MODEL
    mod_cols = w_mod.shape[-1]
    ng_sh, cw_sh = norm_g.shape[-1], conv_w.shape[-1]

    def ffn_shards(i):
        return {f"gate_t{i}": ffn_w_gate[i // 2, i % 2].T, f"up_t{i}": ffn_w_up[i // 2, i % 2].T,
                f"down{i}": ffn_w_down[i // 2, i % 2]}

    local = {**ffn_shards(0), "in_t": ab_w_in[0].T, "uq": w_uq[0], "ukv_t": w_ukv[0].T, "ab_out": ab_w_out[0],
             **ffn_shards(1), **ffn_shards(2), "cin_t": conv_w_in[0].T, "c_out": conv_w_out[0], **ffn_shards(3)}
    ffn_groups = [[[f"gate_t{i}", f"up_t{i}"], [f"down{i}"]] for i in range(4)]
    groups = [*ffn_groups[0], ["in_t", "uq", "ukv_t", "ab_out"], *ffn_groups[1], *ffn_groups[2], ["cin_t", "c_out"],
              *ffn_groups[3]]
    feed = _Feed({nm: a.astype(BF16) for nm, a in local.items()}, groups, me)

    small = jnp.concatenate([c.reshape(-1), norm_g.reshape(-1), conv_w.reshape(-1)])
    small_n = -(-small.shape[0] // 1024) * 1024
    small = jnp.pad(small, (0, small_n - small.shape[0])).reshape(small_n // 128, 128)
    small_all = _exchange("gather_small", small, False).reshape(N_DEV, small_n)
    c_all = small_all[:, :d]
    o1 = d + 6 * ng_sh
    norm_g_full = small_all[:, d:o1].reshape(N_DEV, 2, 3, ng_sh).transpose(1, 2, 0, 3).reshape(2, 3, d)
    conv_w_full = small_all[:, o1:o1 + 3 * cw_sh].reshape(N_DEV, 3, cw_sh).transpose(1, 0, 2).reshape(3, d)

    cond = jnp.concatenate([c_all, jnp.broadcast_to(c_ctx[None, :], (N_DEV, d))], axis=0)
    sil, dsil = _silu_rows("mod_silu", cond)
    w_mod_b = w_mod.astype(BF16)
    b_sh = lax.dynamic_slice(b_mod, (0, me * mod_cols), (2, mod_cols))
    m_part = jnp.stack([_mm(f"mod_fwd{l}", [(sil, w_mod_b[l])], "nn", F32, 16, 384, bias=b_sh[l:l + 1])
                        for l in range(2)], axis=1)
    m_all = _exchange("gather_mod", m_part.reshape(-1, 128), False).reshape(N_DEV, 2 * N_DEV, 2, mod_cols)
    m_mine = lax.dynamic_index_in_dim(m_all, me, axis=1, keepdims=False)
    mod_h = m_mine.transpose(1, 0, 2).reshape(2, N_MOD, d)
    mod_g = m_all[:, N_DEV, 0, :].reshape(N_MOD, d)

    first = feed.start("first", [nm for grp in groups[:3] for nm in grp], m_all)
    feed.start("rest", [nm for grp in groups[3:] for nm in grp], first)

    sq_cols, ds0, g = _local_step(x[0], ctx[0], loss_target[0], mod_h, mod_g, norm_g_full, feed, pool_w[0],
                                  pool_scale, q_norm_g, kv_norm_g, conv_w_full, final_norm_g)
    grad_x = ds0[:n_lat]
    loss = lax.psum(0.5 * jnp.sum(sq_cols) / d, ("x", "y", "c"))
    w_of, m_of, v_of = (dict(zip(WEIGHT_NAMES, t)) for t in (weights, moms, vels))
    results = {}

    def update(nm, grad, view=lambda t: t):
        outs = _adamw(f"adamw_{nm}", view(w_of[nm]), grad.reshape(view(w_of[nm]).shape), view(m_of[nm]), view(v_of[nm]))
        results[nm] = tuple(view(t) for t in (grad.reshape(view(w_of[nm]).shape), *outs))

    def swap(t):
        return jnp.swapaxes(t, -1, -2)

    stacked = ("gate_t", "up_t", "down")
    early = feed.collect(["l1f1", "l1m", "l1f0", "l0f1", "l0m"], [ds0], stacked)
    update("ab_w_in", early["in_t"], swap)
    update("w_uq", early["uq"])
    update("w_ukv", early["ukv_t"].T)
    update("ab_w_out", early["ab_out"])
    update("conv_w_in", early["cin_t"].T)
    update("conv_w_out", early["c_out"])
    ffn = {}
    for nm, prefix, view in (("ffn_w_gate", "gate_t", swap), ("ffn_w_up", "up_t", swap),
                             ("ffn_w_down", "down", lambda t: t)):
        w4, m4, v4 = (view(t).reshape((4,) + view(t).shape[-2:]) for t in (w_of[nm], m_of[nm], v_of[nm]))
        prev = None
        for i in (3, 2, 1):
            prev = _adamw_part(f"adamw_{nm}{i}", i, w4, early[f"{prefix}{i}"], m4, v4, prev)
        ffn[nm] = (prefix, view, w4, m4, v4, prev)
    done_early = [results[nm][1] for nm in results] + [state[5][1] for state in ffn.values()]
    late = feed.collect(["l0f0"], done_early, stacked)
    for nm, (prefix, view, w4, m4, v4, prev) in ffn.items():
        outs = _adamw_part(f"adamw_{nm}0", 0, w4, late[f"{prefix}0"], m4, v4, prev)
        results[nm] = tuple(view(t.reshape(view(w_of[nm]).shape)) for t in outs)

    dm = jnp.stack([g["mod_h"], jnp.stack([g["mod_g"], jnp.zeros_like(g["mod_g"])])])
    dm_all = _exchange("gather_dmod", dm.reshape(-1, 128), False, results["ffn_w_down"][1]).reshape(N_DEV, 2, 2, N_MOD * d)
    grad_b_mod = _sum_rows("dmod_bias", dm_all.reshape(2 * N_DEV, 2 * N_MOD * d)).reshape(2, N_MOD * d)
    dm_sh = lax.dynamic_slice(dm_all, (0, 0, 0, me * mod_cols), (N_DEV, 2, 2, mod_cols))
    gw_mod, cctx_parts = [], []
    for l in range(2):
        dm_l = dm_sh[:, :, l, :].transpose(1, 0, 2).reshape(2 * N_DEV, mod_cols).astype(BF16)
        gw_mod.append(_mm(f"mod_dw{l}", [(sil, dm_l)], "tn", F32, 512, 384))
        dm_ctx = jnp.concatenate([dm_l[N_DEV:], jnp.zeros((N_DEV, mod_cols), BF16)], axis=0)
        cctx_parts.append(_mm(f"mod_dcond{l}", [(dm_ctx, w_mod_b[l])], "nt", F32, 16, 512))
    cctx_part = _sum_rows("mod_dcond_sum", jnp.concatenate(cctx_parts, axis=0))
    update("w_mod", jnp.stack(gw_mod))
    update("b_mod", grad_b_mod)

    small_g = jnp.concatenate([g["pool_w"].reshape(-1), g["pool_scale"].reshape(-1), g["q_norm_g"].reshape(-1),
                               g["kv_norm_g"].reshape(-1), g["final_norm_g"].reshape(-1), g["norm_g"].reshape(-1),
                               g["conv_w"].reshape(-1), cctx_part.reshape(-1)])
    sizes = [pool_w.size, pool_scale.size, q_norm_g.size, kv_norm_g.size, d, 6 * d, 3 * d, d]
    sg_n = -(-small_g.shape[0] // 1024) * 1024
    small_g = jnp.pad(small_g, (0, sg_n - small_g.shape[0]))
    sg_all = _exchange("gather_small_grads", small_g.reshape(-1, 128), False).reshape(N_DEV, sg_n)
    scale_vec = jnp.concatenate([jnp.ones((1, sum(sizes[:-1])), F32), dsil[N_DEV:N_DEV + 1],
                                 jnp.ones((1, sg_n - sum(sizes)), F32)], axis=1)
    sg = _sum_rows("small_grads_sum", sg_all, scale_vec)[0]
    cuts, pos = [], 0
    for sz in sizes:
        cuts.append(sg[pos:pos + sz])
        pos += sz
    g_pool_w, g_pool_scale, g_q_norm, g_kv_norm, g_final, g_norm_full, g_conv_full, g_c_ctx = cuts
    update("c_ctx", g_c_ctx)
    update("norm_g", lax.dynamic_slice(g_norm_full.reshape(2, 3, d), (0, 0, me * ng_sh), (2, 3, ng_sh)))
    update("conv_w", lax.dynamic_slice(g_conv_full.reshape(3, d), (0, me * cw_sh), (3, cw_sh)))
    update("pool_w", g_pool_w)
    update("pool_scale", g_pool_scale)
    update("q_norm_g", g_q_norm)
    update("kv_norm_g", g_kv_norm)
    update("final_norm_g", g_final)
    outs = [results[nm] for nm in WEIGHT_NAMES]
    return (loss, grad_x[None], *[o[0] for o in outs], *[o[1] for o in outs], *[o[2] for o in outs],
            *[o[3] for o in outs])
```

```python
import functools
import math

import jax
import jax.numpy as jnp
import numpy as np
from jax import lax
from jax.experimental import pallas as pl
from jax.experimental.pallas import tpu as pltpu

F32 = jnp.float32
BF16 = jnp.bfloat16
MESH = pl.DeviceIdType.MESH
SDS = jax.ShapeDtypeStruct

N_DEV = 8
D_MODEL = 1024
N_MOD = 9
D_FF = 2816
POOL_WINDOWS = (2, 4, 8, 16)
POOL_DIM = 512
POOL_GROUP_DIM = 128
HEADS = 8
QK_NOPE = 64
QK_ROPE = 32
QK_HEAD = QK_NOPE + QK_ROPE
V_HEAD = 64
Q_RANK = 768
KV_RANK = 256
GRID_W = 64
ROPE_THETA = 10000.0
RMS_EPS = 1e-6
ATTN_SCALE = 1.0 / math.sqrt(QK_HEAD)
HEAD_PAD = 128
POOL_PAD = 16
PA_POOL, PA_CQ, PA_KV = 0, 768, 1536
PA_KV_W = 384
PA_W = PA_KV + PA_KV_W

ADAM_LR, ADAM_B1, ADAM_B2, ADAM_EPS, ADAM_WD, ADAM_STEP = 0.001, 0.9, 0.999, 1e-08, 0.01, 10

VMEM_LIMIT_BYTES = 56 * 1024 * 1024

NN = ((1,), (0,))
NT = ((1,), (1,))
TN = ((0,), (0,))


def _cparams():
    return pltpu.CompilerParams(vmem_limit_bytes=VMEM_LIMIT_BYTES)


def _dot(a, b, dims):
    return lax.dot_general(a, b, (dims, ((), ())), preferred_element_type=F32)


def _tile(n, cap, mult=8):
    t = (min(cap, n) // mult) * mult
    while t >= mult:
        if n % t == 0:
            return t
        t -= mult
    return n


def _colsum(x):
    return jnp.sum(x, axis=0, keepdims=True)


def _rms(x):
    r = lax.rsqrt(jnp.mean(x * x, axis=-1, keepdims=True) + RMS_EPS)
    return x * r, r


def _rms_bwd(n, r, dn):
    return r * (dn - n * jnp.mean(dn * n, axis=-1, keepdims=True))


def _rowwise(name, fn, t_rows, tm, n_lat, rows, vecs, outs, accs):
    nt = t_rows // tm
    nlt = n_lat // tm
    n_groups = 2 if nlt < nt else 1

    def grp(i):
        return jnp.where(i >= nlt, 1, 0) if n_groups == 2 else 0

    in_specs = [pl.BlockSpec((tm, w), functools.partial(lambda i, cb: (i, cb), cb=cb)) for (_, w, cb) in rows]
    in_specs += [pl.BlockSpec((1,) + v.shape[1:], lambda i: (grp(i), 0, 0)) for v in vecs]
    out_specs = [pl.BlockSpec((tm, w), lambda i: (i, 0)) for (w, _) in outs]
    out_specs += [pl.BlockSpec((1, 1, w), lambda i: (grp(i), 0, 0)) for w in accs]
    out_shape = [SDS((t_rows, w), dt) for (w, dt) in outs] + [SDS((n_groups, 1, w), F32) for w in accs]
    n_r, n_v, n_o = len(rows), len(vecs), len(outs)

    def body(*refs):
        row_vals = [r[...] for r in refs[:n_r]]
        vec_vals = [v[0] for v in refs[n_r:n_r + n_v]]
        out_refs = refs[n_r + n_v:n_r + n_v + n_o]
        acc_refs = refs[n_r + n_v + n_o:]
        out_vals, acc_vals = fn(row_vals, vec_vals)
        for o_ref, o in zip(out_refs, out_vals):
            o_ref[...] = o.astype(o_ref.dtype)
        if acc_refs:
            i = pl.program_id(0)
            first = (i == 0) | (i == nlt) if n_groups == 2 else i == 0

            @pl.when(first)
            def _():
                for a_ref, a in zip(acc_refs, acc_vals):
                    a_ref[0] = a

            @pl.when(jnp.logical_not(first))
            def _():
                for a_ref, a in zip(acc_refs, acc_vals):
                    a_ref[0] += a

    res = pl.pallas_call(
        body, name=name, grid=(nt,), in_specs=in_specs, out_specs=out_specs, out_shape=out_shape,
        compiler_params=_cparams(),
    )(*[r[0] for r in rows], *vecs)
    return res[:n_o], res[n_o:]


RESIDENT_BYTES = 12 * 1024 * 1024


def _mm(name, pairs, mode, out_dtype, tm_cap=256, tn_cap=512, bias=None):
    a0, b0 = pairs[0]
    if mode == "nn":
        m, n, dims = a0.shape[0], b0.shape[1], NN
    elif mode == "nt":
        m, n, dims = a0.shape[0], b0.shape[0], NT
    else:
        m, n, dims = a0.shape[1], b0.shape[1], TN
    b_bytes = sum(b.size * b.dtype.itemsize for _, b in pairs)
    tn = n if b_bytes <= RESIDENT_BYTES else _tile(n, tn_cap, 128)
    tm = _tile(m, tm_cap, 128 if mode == "tn" else 16)

    def a_spec(a):
        if mode == "tn":
            return pl.BlockSpec((a.shape[0], tm), lambda i, j: (0, i))
        return pl.BlockSpec((tm, a.shape[1]), lambda i, j: (i, 0))

    def b_spec(b):
        if mode == "nt":
            return pl.BlockSpec((tn, b.shape[1]), lambda i, j: (j, 0))
        return pl.BlockSpec((b.shape[0], tn), lambda i, j: (0, j))

    in_specs, flat = [], []
    for a, b in pairs:
        in_specs += [a_spec(a), b_spec(b)]
        flat += [a, b]
    if bias is not None:
        in_specs.append(pl.BlockSpec((1, tn), lambda i, j: (0, j)))
        flat.append(bias)
    n_pairs = len(pairs)

    def body(*refs):
        acc = None
        for p in range(n_pairs):
            t = _dot(refs[2 * p][...], refs[2 * p + 1][...], dims)
            acc = t if acc is None else acc + t
        if bias is not None:
            acc = acc + refs[2 * n_pairs][...]
        refs[-1][...] = acc.astype(refs[-1].dtype)

    return pl.pallas_call(
        body, name=name, grid=(m // tm, n // tn), in_specs=in_specs,
        out_specs=pl.BlockSpec((tm, tn), lambda i, j: (i, j)),
        out_shape=SDS((m, n), out_dtype), compiler_params=_cparams(),
    )(*flat)


def _mm_resid(name, a, b, s, mg, k, coef, n_lat):
    t_rows, n = a.shape[0], b.shape[1]
    tm = _tile(math.gcd(n_lat, t_rows), 256, 16)
    nlt = n_lat // tm
    n_groups = 2 if nlt < t_rows // tm else 1

    def grp(i):
        return jnp.where(i >= nlt, 1, 0) if n_groups == 2 else 0

    def body(a_ref, b_ref, s_ref, mg_ref, so_ref, o_ref):
        o = _dot(a_ref[...], b_ref[...], NN)
        gate = mg_ref[0, 3 * k + 2:3 * k + 3, :]
        o_ref[...] = o.astype(BF16)
        so_ref[...] = s_ref[...] + (coef * gate) * o

    row = pl.BlockSpec((tm, n), lambda i: (i, 0))
    return pl.pallas_call(
        body, name=name, grid=(t_rows // tm,),
        in_specs=[pl.BlockSpec((tm, a.shape[1]), lambda i: (i, 0)), pl.BlockSpec(b.shape, lambda i: (0, 0)), row,
                  pl.BlockSpec((1, mg.shape[1], n), lambda i: (grp(i), 0, 0))],
        out_specs=[row, row], out_shape=[SDS((t_rows, n), F32), SDS((t_rows, n), BF16)], compiler_params=_cparams(),
    )(a, b, s, mg)


def _groups(t_rows, tm, n_lat):
    nlt = n_lat // tm
    if nlt < t_rows // tm:
        return 2, (lambda i: jnp.where(i >= nlt, 1, 0)), (lambda i: (i == 0) | (i == nlt))
    return 1, (lambda i: 0), (lambda i: i == 0)


def _accumulate(acc_refs, vals, first):
    @pl.when(first)
    def _():
        for r, v in zip(acc_refs, vals):
            r[0] = v

    @pl.when(jnp.logical_not(first))
    def _():
        for r, v in zip(acc_refs, vals):
            r[0] += v


def _adaln_math(s, m, k):
    n, _ = _rms(s)
    return (n * m[9 + k:10 + k]) * (1.0 + m[3 * k + 1:3 * k + 2]) + m[3 * k:3 * k + 1]


def _ffn_up(name, s, mg, k, n_lat, wg_t, wu_t):
    t_rows, f = s.shape[0], wg_t.shape[0]
    tm = _row_tm(t_rows, n_lat)
    _, grp, _ = _groups(t_rows, tm, n_lat)

    def body(s_ref, mg_ref, wg_ref, wu_ref, u_ref, a_ref, b_ref, h_ref):
        uu = _adaln_math(s_ref[...], mg_ref[0], k).astype(BF16)
        u_ref[...] = uu
        a = _dot(uu, wg_ref[...], NT)
        b = _dot(uu, wu_ref[...], NT)
        sg = jax.nn.sigmoid(a)
        act = a * sg
        a_ref[...] = (b * (sg * (1.0 + a * (1.0 - sg)))).astype(BF16)
        b_ref[...] = act.astype(BF16)
        h_ref[...] = (act * b).astype(BF16)

    w_spec = pl.BlockSpec(wg_t.shape, lambda i: (0, 0))
    o_spec = pl.BlockSpec((tm, f), lambda i: (i, 0))
    row = pl.BlockSpec((tm, s.shape[1]), lambda i: (i, 0))
    return pl.pallas_call(
        body, name=name, grid=(t_rows // tm,),
        in_specs=[row, pl.BlockSpec((1,) + mg.shape[1:], lambda i: (grp(i), 0, 0)), w_spec, w_spec],
        out_specs=[row, o_spec, o_spec, o_spec],
        out_shape=[SDS(s.shape, BF16)] + [SDS((t_rows, f), BF16)] * 3, compiler_params=_cparams(),
    )(s, mg, wg_t, wu_t)


def _ffn_dact(name, ds_out, o, mg, k, coef, n_lat, wd, a, b):
    t_rows, f = ds_out.shape[0], wd.shape[0]
    tm = _row_tm(t_rows, n_lat)
    n_groups, grp, first = _groups(t_rows, tm, n_lat)
    d = ds_out.shape[1]

    def body(ds_ref, o_ref, mg_ref, wd_ref, a_ref, b_ref, do_ref, da_ref, db_ref, dg_ref):
        dd = coef * ds_ref[...]
        do = (dd * mg_ref[0, 3 * k + 2:3 * k + 3, :]).astype(BF16)
        do_ref[...] = do
        _accumulate([dg_ref], [_colsum(dd * o_ref[...].astype(F32))], first(pl.program_id(0)))
        dh = _dot(do, wd_ref[...], NT)
        da_ref[...] = (dh * a_ref[...].astype(F32)).astype(BF16)
        db_ref[...] = (dh * b_ref[...].astype(F32)).astype(BF16)

    row = pl.BlockSpec((tm, d), lambda i: (i, 0))
    t_spec = pl.BlockSpec((tm, f), lambda i: (i, 0))
    return pl.pallas_call(
        body, name=name, grid=(t_rows // tm,),
        in_specs=[row, row, pl.BlockSpec((1,) + mg.shape[1:], lambda i: (grp(i), 0, 0)),
                  pl.BlockSpec(wd.shape, lambda i: (0, 0)), t_spec, t_spec],
        out_specs=[row, t_spec, t_spec, pl.BlockSpec((1, 1, d), lambda i: (grp(i), 0, 0))],
        out_shape=[SDS((t_rows, d), BF16), SDS((t_rows, f), BF16), SDS((t_rows, f), BF16), SDS((n_groups, 1, d), F32)],
        compiler_params=_cparams(),
    )(ds_out, o, mg, wd, a, b)


def _du_adaln(name, pairs, s, ds_out, mg, k, n_lat, after):
    t_rows, d = s.shape
    tm = _row_tm(t_rows, n_lat)
    n_groups, grp, first = _groups(t_rows, tm, n_lat)
    n_pairs = len(pairs)

    def body(*refs):
        s_ref, ds_ref, mg_ref, z_ref, out_ref, dsh_ref, dsc_ref, dgn_ref = refs[2 * n_pairs:]
        d_u = z_ref[...]
        for p in range(n_pairs):
            d_u = d_u + _dot(refs[p][...], refs[n_pairs + p][...], NN)
        m = mg_ref[0]
        gain, scale = m[9 + k:10 + k], m[3 * k + 1:3 * k + 2]
        n, r = _rms(s_ref[...])
        dxn = d_u * (1.0 + scale)
        out_ref[...] = ds_ref[...] + _rms_bwd(n, r, dxn * gain)
        _accumulate([dsh_ref, dsc_ref, dgn_ref], [_colsum(d_u), _colsum(d_u * (n * gain)), _colsum(dxn * n)],
                    first(pl.program_id(0)))

    row = pl.BlockSpec((tm, d), lambda i: (i, 0))
    acc = pl.BlockSpec((1, 1, d), lambda i: (grp(i), 0, 0))
    res = pl.pallas_call(
        body, name=name, grid=(t_rows // tm,),
        in_specs=[pl.BlockSpec((tm, a.shape[1]), lambda i: (i, 0)) for a, _ in pairs]
        + [pl.BlockSpec(w.shape, lambda i: (0, 0)) for _, w in pairs]
        + [row, row, pl.BlockSpec((1,) + mg.shape[1:], lambda i: (grp(i), 0, 0)), pl.BlockSpec((1, d), lambda i: (0, 0))],
        out_specs=[row, acc, acc, acc],
        out_shape=[SDS((t_rows, d), F32)] + [SDS((n_groups, 1, d), F32)] * 3, compiler_params=_cparams(),
    )(*[a for a, _ in pairs], *[w for _, w in pairs], s, ds_out, mg, after)
    return res[0], res[1:]


def _adaln_mm(name, s, mg, k, n_lat, w_t):
    rows, d = s.shape
    tm = _row_tm(rows, n_lat)
    _, grp, _ = _groups(rows, tm, n_lat)
    n = w_t.shape[0]

    def body(s_ref, mg_ref, w_ref, u_ref, y_ref):
        uu = _adaln_math(s_ref[...], mg_ref[0], k).astype(BF16)
        u_ref[...] = uu
        y_ref[...] = _dot(uu, w_ref[...], NT)

    row = pl.BlockSpec((tm, d), lambda i: (i, 0))
    return pl.pallas_call(
        body, name=name, grid=(rows // tm,),
        in_specs=[row, pl.BlockSpec((1,) + mg.shape[1:], lambda i: (grp(i), 0, 0)), pl.BlockSpec(w_t.shape, lambda i: (0, 0))],
        out_specs=[row, pl.BlockSpec((tm, n), lambda i: (i, 0))],
        out_shape=[SDS((rows, d), BF16), SDS((rows, n), F32)], compiler_params=_cparams(),
    )(s, mg, w_t)


def _gate_mm(name, ds_out, o, mg, k, coef, n_lat, w):
    t_rows, d = ds_out.shape
    tm = _row_tm(t_rows, n_lat)
    n_groups, grp, first = _groups(t_rows, tm, n_lat)
    n = w.shape[0]

    def body(ds_ref, o_ref, mg_ref, w_ref, do_ref, y_ref, dg_ref):
        dd = coef * ds_ref[...]
        do = (dd * mg_ref[0, 3 * k + 2:3 * k + 3, :]).astype(BF16)
        do_ref[...] = do
        _accumulate([dg_ref], [_colsum(dd * o_ref[...].astype(F32))], first(pl.program_id(0)))
        y_ref[...] = _dot(do, w_ref[...], NT)

    row = pl.BlockSpec((tm, d), lambda i: (i, 0))
    return pl.pallas_call(
        body, name=name, grid=(t_rows // tm,),
        in_specs=[row, row, pl.BlockSpec((1,) + mg.shape[1:], lambda i: (grp(i), 0, 0)), pl.BlockSpec(w.shape, lambda i: (0, 0))],
        out_specs=[row, pl.BlockSpec((tm, n), lambda i: (i, 0)), pl.BlockSpec((1, 1, d), lambda i: (grp(i), 0, 0))],
        out_shape=[SDS((t_rows, d), BF16), SDS((t_rows, n), F32), SDS((n_groups, 1, d), F32)],
        compiler_params=_cparams(),
    )(ds_out, o, mg, w)


def _row_tm(t_rows, n_lat):
    return _tile(math.gcd(t_rows, n_lat), 256, 16)


def _rmsnorm_fwd(name, x, width, colblk, gain, t_rows):
    def fn(rv, vv):
        n, _ = _rms(rv[0])
        return [n * vv[0]], []

    (y,), _ = _rowwise(name, fn, t_rows, _tile(t_rows, 256, 16), t_rows, [(x, width, colblk)],
                       [gain.reshape(1, 1, width)], [(width, BF16)], [])
    return y


def _rmsnorm_bwd(name, x, width, colblk, dy, gain, t_rows, out_dtype=F32):
    def fn(rv, vv):
        n, r = _rms(rv[0])
        return [_rms_bwd(n, r, rv[1] * vv[0])], [_colsum(rv[1] * n)]

    (dx,), (dgain,) = _rowwise(name, fn, t_rows, _tile(t_rows, 256, 16), t_rows,
                               [(x, width, colblk), (dy, width, 0)], [gain.reshape(1, 1, width)],
                               [(width, out_dtype)], [width])
    return dx, dgain


def _final_loss(name, h, target, gain):
    t_rows = h.shape[0]
    inv_d = 1.0 / D_MODEL

    def fn(rv, vv):
        g = vv[0]
        n, r = _rms(rv[0])
        e = n * g - rv[1]
        dy = e * inv_d
        return [_rms_bwd(n, r, dy * g)], [_colsum(e * e), _colsum(dy * n)]

    (dh,), (sq, dgain) = _rowwise(name, fn, t_rows, _tile(t_rows, 256, 16), t_rows,
                                  [(h, D_MODEL, 0), (target, D_MODEL, 0)], [gain.reshape(1, 1, D_MODEL)],
                                  [(D_MODEL, F32)], [D_MODEL, D_MODEL])
    return dh, sq, dgain


def _rope(name, z, width, colblk, cos, sin, perm, backward, out_dtype):
    t_rows = cos.shape[0]

    def body(z_ref, c_ref, s_ref, p_ref, o_ref):
        zz = z_ref[...]
        pre = zz * s_ref[...] if backward else zz
        hi = pre.astype(BF16)
        lo = (pre - hi.astype(F32)).astype(BF16)
        rot = _dot(hi, p_ref[...], NN) + _dot(lo, p_ref[...], NN)
        if not backward:
            rot = rot * s_ref[...]
        o_ref[...] = (zz * c_ref[...] + rot).astype(o_ref.dtype)

    tm = _tile(t_rows, 256, 16)
    t_spec = pl.BlockSpec((tm, width), lambda i: (i, 0))
    return pl.pallas_call(
        body, name=name, grid=(t_rows // tm,),
        in_specs=[pl.BlockSpec((tm, width), lambda i: (i, colblk)), t_spec, t_spec,
                  pl.BlockSpec((width, width), lambda i: (0, 0))],
        out_specs=t_spec, out_shape=SDS((t_rows, width), out_dtype), compiler_params=_cparams(),
    )(z, cos, sin, perm)


def _window_sum(x, w, transposed):
    n_rows = x.shape[0]
    zeros = jnp.zeros((POOL_PAD, x.shape[1]), F32)
    y = jnp.concatenate([zeros, x, zeros], axis=0)
    total = n_rows + 2 * POOL_PAD
    if transposed:
        y = y + pltpu.roll(y, total - 1, 0)
    else:
        y = y + pltpu.roll(y, 1, 0)
    step = 1
    while 2 * step < w:
        y = pltpu.roll(y, step, 0) + pltpu.roll(y, total - step, 0)
        step *= 2
    return y[POOL_PAD:POOL_PAD + n_rows]


def _window_count(n_rows, w):
    t = lax.broadcasted_iota(jnp.int32, (n_rows, 1), 0)
    lo = jnp.maximum(t - w // 2, 0)
    hi = jnp.minimum(t + (w - w // 2 - 1), n_rows - 1)
    return (hi - lo + 1).astype(F32)


def _pool_fwd(name, proj, n_rows, w_grp, scale):
    def body(x_ref, w_ref, sc_ref, y_ref, p_ref):
        for g, w in enumerate(POOL_WINDOWS):
            cols = slice(g * POOL_GROUP_DIM, (g + 1) * POOL_GROUP_DIM)
            x = x_ref[:, cols]
            p = _window_sum(x, w, False) * (1.0 / _window_count(n_rows, w)) - x
            pb = p.astype(BF16)
            p_ref[:, cols] = pb
            y_ref[:, cols] = (_dot(pb, w_ref[g], NN) * sc_ref[:, cols]).astype(BF16)

    blk = pl.BlockSpec((n_rows, POOL_DIM), lambda i: (0, 0))
    return pl.pallas_call(
        body, name=name, grid=(1,),
        in_specs=[blk, pl.BlockSpec(w_grp.shape, lambda i: (0, 0, 0)), pl.BlockSpec((1, POOL_DIM), lambda i: (0, 0))],
        out_specs=[blk, blk], out_shape=[SDS((n_rows, POOL_DIM), BF16)] * 2, compiler_params=_cparams(),
    )(proj, w_grp, scale)


def _pool_bwd(name, dcat, n_rows, p, w_grp, scale):
    def body(dy_ref, p_ref, w_ref, sc_ref, dx_ref, dw_ref, dsc_ref):
        for g, w in enumerate(POOL_WINDOWS):
            cols = slice(g * POOL_GROUP_DIM, (g + 1) * POOL_GROUP_DIM)
            dy = dy_ref[:, cols]
            pb = p_ref[:, cols]
            pw = _dot(pb, w_ref[g], NN)
            dsc_ref[:, cols] = _colsum(dy * pw)
            dpw = (dy * sc_ref[:, cols]).astype(BF16)
            dw_ref[g] = _dot(pb, dpw, TN)
            dp = _dot(dpw, w_ref[g], NT)
            dx_ref[:, cols] = (_window_sum(dp * (1.0 / _window_count(n_rows, w)), w, True) - dp).astype(BF16)

    blk = pl.BlockSpec((n_rows, POOL_DIM), lambda i: (0, 0))
    w_spec = pl.BlockSpec(w_grp.shape, lambda i: (0, 0, 0))
    v_spec = pl.BlockSpec((1, POOL_DIM), lambda i: (0, 0))
    return pl.pallas_call(
        body, name=name, grid=(1,), in_specs=[blk, blk, w_spec, v_spec], out_specs=[blk, w_spec, v_spec],
        out_shape=[SDS((n_rows, POOL_DIM), BF16), SDS(w_grp.shape, F32), SDS((1, POOL_DIM), F32)],
        compiler_params=_cparams(),
    )(dcat, p, w_grp, scale)


def _attn_fwd(name, q, k, v):
    h, n_q, _ = q.shape
    n_k = k.shape[1]
    tq = _tile(n_q, 256, 16)

    def body(q_ref, k_ref, v_ref, o_ref, lse_ref):
        s = _dot(q_ref[...], k_ref[...], NT) * ATTN_SCALE
        m = jnp.max(s, axis=-1, keepdims=True)
        e = jnp.exp(s - m)
        l = jnp.sum(e, axis=-1, keepdims=True)
        p = (e * (1.0 / l)).astype(BF16)
        o_ref[...] = _dot(p, v_ref[...], NN).astype(BF16)
        lse_ref[...] = m + jnp.log(l)

    return pl.pallas_call(
        body, name=name, grid=(h, n_q // tq),
        in_specs=[pl.BlockSpec((None, tq, HEAD_PAD), lambda hh, i: (hh, i, 0)),
                  pl.BlockSpec((None, n_k, HEAD_PAD), lambda hh, i: (hh, 0, 0)),
                  pl.BlockSpec((None, n_k, V_HEAD), lambda hh, i: (hh, 0, 0))],
        out_specs=[pl.BlockSpec((None, tq, V_HEAD), lambda hh, i: (hh, i, 0)),
                   pl.BlockSpec((None, tq, 1), lambda hh, i: (hh, i, 0))],
        out_shape=[SDS((h, n_q, V_HEAD), BF16), SDS((h, n_q, 1), F32)], compiler_params=_cparams(),
    )(q, k, v)


def _attn_bwd(name, q, k, v, o, lse, do):
    h, n_q, _ = q.shape
    n_k = k.shape[1]
    tq = _tile(n_q, 256, 16)

    def body(q_ref, k_ref, v_ref, o_ref, lse_ref, do_ref, dq_ref, dk_ref, dv_ref, dks_ref):
        hh, i = pl.program_id(0), pl.program_id(1)
        qq, kk, dd = q_ref[...], k_ref[...], do_ref[...]
        s = _dot(qq, kk, NT) * ATTN_SCALE
        p = jnp.exp(s - lse_ref[...])
        dp = _dot(dd, v_ref[...], NT)
        delta = jnp.sum(dd.astype(F32) * o_ref[...].astype(F32), axis=-1, keepdims=True)
        ds = (p * (dp - delta) * ATTN_SCALE).astype(BF16)
        dq_ref[...] = _dot(ds, kk, NN)
        dk = _dot(ds, qq, TN)
        dv = _dot(p.astype(BF16), dd, TN)

        @pl.when(i == 0)
        def _():
            dk_ref[...] = dk
            dv_ref[...] = dv

        @pl.when(i > 0)
        def _():
            dk_ref[...] += dk
            dv_ref[...] += dv

        @pl.when((i == 0) & (hh == 0))
        def _():
            dks_ref[...] = dk

        @pl.when((i > 0) | (hh > 0))
        def _():
            dks_ref[...] += dk

    q_spec = pl.BlockSpec((None, tq, HEAD_PAD), lambda hh, i: (hh, i, 0))
    k_spec = pl.BlockSpec((None, n_k, HEAD_PAD), lambda hh, i: (hh, 0, 0))
    v_spec = pl.BlockSpec((None, n_k, V_HEAD), lambda hh, i: (hh, 0, 0))
    o_spec = pl.BlockSpec((None, tq, V_HEAD), lambda hh, i: (hh, i, 0))
    return pl.pallas_call(
        body, name=name, grid=(h, n_q // tq),
        in_specs=[q_spec, k_spec, v_spec, o_spec, pl.BlockSpec((None, tq, 1), lambda hh, i: (hh, i, 0)), o_spec],
        out_specs=[q_spec, k_spec, v_spec, pl.BlockSpec((n_k, HEAD_PAD), lambda hh, i: (0, 0))],
        out_shape=[SDS((h, n_q, HEAD_PAD), F32), SDS((h, n_k, HEAD_PAD), F32), SDS((h, n_k, V_HEAD), F32),
                   SDS((n_k, HEAD_PAD), F32)],
        compiler_params=_cparams(),
    )(q, k, v, o, lse, do)


CONV_COLS = 256


def _shift_rows(x, d):
    n_rows = x.shape[0]
    t = lax.broadcasted_iota(jnp.int32, (n_rows, 1), 0)
    if d > 0:
        return jnp.where(t >= d, pltpu.roll(x, d, 0), 0.0)
    return jnp.where(t < n_rows + d, pltpu.roll(x, n_rows + d, 0), 0.0)


def _conv_fwd(name, z3, conv_w):
    n_rows = z3.shape[0]
    nb = D_MODEL // CONV_COLS

    def body(b_ref, c_ref, v_ref, w_ref, y_ref):
        z = c_ref[...] * v_ref[...]
        zc = w_ref[0:1, :] * _shift_rows(z, 1) + w_ref[1:2, :] * z + w_ref[2:3, :] * _shift_rows(z, -1)
        y_ref[...] = (b_ref[...] * zc).astype(BF16)

    def part(k):
        return pl.BlockSpec((n_rows, CONV_COLS), lambda j: (0, k * nb + j))

    return pl.pallas_call(
        body, name=name, grid=(nb,),
        in_specs=[part(0), part(1), part(2), pl.BlockSpec((3, CONV_COLS), lambda j: (0, j))],
        out_specs=pl.BlockSpec((n_rows, CONV_COLS), lambda j: (0, j)),
        out_shape=SDS((n_rows, D_MODEL), BF16), compiler_params=_cparams(),
    )(z3, z3, z3, conv_w)


def _conv_bwd(name, dy, z3, conv_w):
    n_rows = z3.shape[0]
    nb = D_MODEL // CONV_COLS

    def body(dy_ref, b_ref, c_ref, v_ref, w_ref, db_ref, dc_ref, dv_ref, dw_ref):
        c, v, d_y = c_ref[...], v_ref[...], dy_ref[...]
        z = c * v
        z_dn, z_up = _shift_rows(z, 1), _shift_rows(z, -1)
        zc = w_ref[0:1, :] * z_dn + w_ref[1:2, :] * z + w_ref[2:3, :] * z_up
        db_ref[...] = (d_y * zc).astype(BF16)
        dzc = d_y * b_ref[...]
        dz = w_ref[0:1, :] * _shift_rows(dzc, -1) + w_ref[1:2, :] * dzc + w_ref[2:3, :] * _shift_rows(dzc, 1)
        dc_ref[...] = (dz * v).astype(BF16)
        dv_ref[...] = (dz * c).astype(BF16)
        dw_ref[0:1, :] = _colsum(dzc * z_dn)
        dw_ref[1:2, :] = _colsum(dzc * z)
        dw_ref[2:3, :] = _colsum(dzc * z_up)

    def part(k):
        return pl.BlockSpec((n_rows, CONV_COLS), lambda j: (0, k * nb + j))

    col = pl.BlockSpec((n_rows, CONV_COLS), lambda j: (0, j))
    w_spec = pl.BlockSpec((3, CONV_COLS), lambda j: (0, j))
    return pl.pallas_call(
        body, name=name, grid=(nb,), in_specs=[col, part(0), part(1), part(2), w_spec],
        out_specs=[col, col, col, w_spec],
        out_shape=[SDS((n_rows, D_MODEL), BF16)] * 3 + [SDS((3, D_MODEL), F32)], compiler_params=_cparams(),
    )(dy, z3, z3, z3, conv_w)


def _silu_rows(name, x):
    def body(x_ref, s_ref, d_ref):
        xx = x_ref[...]
        sg = jax.nn.sigmoid(xx)
        s_ref[...] = (xx * sg).astype(BF16)
        d_ref[...] = sg * (1.0 + xx * (1.0 - sg))

    return pl.pallas_call(body, name=name, out_shape=[SDS(x.shape, BF16), SDS(x.shape, F32)])(x)


def _sum_rows(name, x, scale=None):
    r, n = x.shape
    tn = _tile(n, 8192, 128)

    def body(*refs):
        acc = jnp.sum(refs[0][...].astype(F32), axis=0, keepdims=True)
        if scale is not None:
            acc = acc * refs[1][...]
        refs[-1][...] = acc

    in_specs = [pl.BlockSpec((r, tn), lambda j: (0, j))]
    args = [x]
    if scale is not None:
        in_specs.append(pl.BlockSpec((1, tn), lambda j: (0, j)))
        args.append(scale)
    return pl.pallas_call(body, name=name, grid=(n // tn,), in_specs=in_specs,
                          out_specs=pl.BlockSpec((1, tn), lambda j: (0, j)), out_shape=SDS((1, n), F32))(*args)


def _sum_slots(name, x):
    n_slots, r, c = x.shape
    tr = _tile(r, 432, 16)

    def body(x_ref, o_ref):
        acc = x_ref[0].astype(F32)
        for sl in range(1, n_slots):
            acc = acc + x_ref[sl].astype(F32)
        o_ref[...] = acc

    return pl.pallas_call(body, name=name, grid=(r // tr,),
                          in_specs=[pl.BlockSpec((n_slots, tr, c), lambda i: (0, i, 0))],
                          out_specs=pl.BlockSpec((tr, c), lambda i: (i, 0)), out_shape=SDS((r, c), F32),
                          compiler_params=_cparams())(x)


def _adamw(name, w, g, m, v):
    shape = w.shape
    cols = shape[-1]
    rows = w.size // cols
    tr = _tile(rows, 512, 8)
    bc1 = 1.0 - ADAM_B1 ** ADAM_STEP
    bc2 = 1.0 - ADAM_B2 ** ADAM_STEP

    def body(w_ref, g_ref, m_ref, v_ref, d_ref, nm_ref, nv_ref):
        gg = g_ref[...]
        nm = ADAM_B1 * m_ref[...] + (1.0 - ADAM_B1) * gg
        nv = ADAM_B2 * v_ref[...] + (1.0 - ADAM_B2) * (gg * gg)
        nm_ref[...] = nm
        nv_ref[...] = nv
        d_ref[...] = -ADAM_LR * ((nm / bc1) / (jnp.sqrt(nv / bc2) + ADAM_EPS) + ADAM_WD * w_ref[...])

    spec = pl.BlockSpec((tr, cols), lambda i: (i, 0))
    outs = pl.pallas_call(body, name=name, grid=(rows // tr,), in_specs=[spec] * 4, out_specs=[spec] * 3,
                          out_shape=[SDS((rows, cols), F32)] * 3, compiler_params=_cparams())(
        w.reshape(rows, cols), g.reshape(rows, cols), m.reshape(rows, cols), v.reshape(rows, cols))
    return tuple(t.reshape(shape) for t in outs)


def _exchange(name, x, scatter, after=None):
    blk = x.shape[1:] if scatter else x.shape
    extra = [] if after is None else [after]

    def body(x_ref, *rest):
        out_ref, send_sems, recv_sems, local_sem = rest[len(extra):]
        mx, my, mc = lax.axis_index("x"), lax.axis_index("y"), lax.axis_index("c")
        me = 4 * mx + 2 * my + mc
        own = pltpu.make_async_copy(x_ref.at[me] if scatter else x_ref, out_ref.at[me], local_sem)
        own.start()
        copies = []
        for kk in range(1, N_DEV):
            px = jnp.bitwise_xor(mx, (kk >> 2) & 1)
            py = jnp.bitwise_xor(my, (kk >> 1) & 1)
            pc = jnp.bitwise_xor(mc, kk & 1)
            peer = 4 * px + 2 * py + pc
            send = pltpu.make_async_remote_copy(
                src_ref=x_ref.at[peer] if scatter else x_ref, dst_ref=out_ref.at[me],
                send_sem=send_sems.at[kk - 1], recv_sem=recv_sems.at[kk - 1],
                device_id=(px, py, pc), device_id_type=MESH)
            send.start()
            arrival = pltpu.make_async_remote_copy(
                src_ref=x_ref.at[peer] if scatter else x_ref, dst_ref=out_ref.at[peer],
                send_sem=send_sems.at[kk - 1], recv_sem=recv_sems.at[kk - 1],
                device_id=(px, py, pc), device_id_type=MESH)
            copies.append((send, arrival))
        for send, arrival in copies:
            arrival.wait_recv()
            send.wait_send()
        own.wait()

    return pl.pallas_call(
        body, name=name, out_shape=SDS((N_DEV,) + tuple(blk), x.dtype),
        in_specs=[pl.BlockSpec(memory_space=pl.ANY)] * (1 + len(extra)), out_specs=pl.BlockSpec(memory_space=pl.ANY),
        scratch_shapes=[pltpu.SemaphoreType.DMA((N_DEV - 1,)), pltpu.SemaphoreType.DMA((N_DEV - 1,)),
                        pltpu.SemaphoreType.DMA],
    )(x, *extra)


def _rope_perm(pre, reps, post):
    half = QK_ROPE // 4
    width = reps * (pre + QK_ROPE) + post
    p = np.zeros((width, width), np.float32)
    for rep in range(reps):
        s0 = rep * (pre + QK_ROPE) + pre
        for base in (s0, s0 + 2 * half):
            for i in range(half):
                p[base + half + i, base + i] = -1.0
                p[base + i, base + half + i] = 1.0
    return p


def _rope_tables(n_lat, t_rows, pre, reps, post):
    half = QK_ROPE // 4
    pos = jnp.arange(n_lat)
    freqs = jnp.power(ROPE_THETA, -jnp.arange(0, 2 * half, 2, dtype=F32) / (2 * half))
    ang_r = (pos // GRID_W).astype(F32)[:, None] * freqs
    ang_c = (pos % GRID_W).astype(F32)[:, None] * freqs
    ang = jnp.concatenate([ang_r, ang_r, ang_c, ang_c], axis=-1)

    def table(fn, plain):
        slot = jnp.concatenate([jnp.full((n_lat, pre), plain, F32), fn(ang)], axis=-1)
        t = jnp.concatenate([jnp.tile(slot, (1, reps)), jnp.full((n_lat, post), plain, F32)], axis=-1)
        return jnp.concatenate([t, jnp.full((t_rows - n_lat, t.shape[1]), plain, F32)], axis=0)

    return table(jnp.cos, 1.0), table(jnp.sin, 0.0)


def _ffn_half_fwd(tag, s, mg, k, feed, i, coef, n_lat):
    wg_t, wu_t = feed.weights(f"{tag}_up", [f"gate_t{i}", f"up_t{i}"], s)
    u, a, b, hid = _ffn_up(f"{tag}_up", s, mg, k, n_lat, wg_t, wu_t)
    (wd,) = feed.weights(f"{tag}_down", [f"down{i}"], hid)
    s_out, o = _mm_resid(f"{tag}_down", hid, wd, s, mg, k, coef, n_lat)
    return s_out, (s, u, a, b, hid, o, wg_t, wu_t, wd)


def _ffn_half_bwd(tag, ds_out, saved, mg, k, feed, i, coef, n_lat):
    s, u, a, b, hid, o, wg_t, wu_t, wd = saved
    do, da, db, dgate = _ffn_dact(f"{tag}_dact", ds_out, o, mg, k, coef, n_lat, wd, a, b)
    dwd = _mm(f"{tag}_dwd", [(hid, do)], "tn", BF16)
    dwg_t = _mm(f"{tag}_dwg", [(da, u)], "tn", BF16)
    dwu_t = _mm(f"{tag}_dwu", [(db, u)], "tn", BF16)
    token = feed.grads(tag, {f"down{i}": dwd, f"gate_t{i}": dwg_t, f"up_t{i}": dwu_t})
    ds_in, (dshift, dscale, dgain) = _du_adaln(f"{tag}_du", [(da, wg_t), (db, wu_t)], s, ds_out, mg, k, n_lat,
                                               _after(token))
    return ds_in, dict(shift=dshift, scale=dscale, gate=dgate, gain=dgain)


def _after(token):
    return jnp.zeros((1, D_MODEL), F32) + token


def _mod_grad(parts, n_groups):
    rows = []
    zero = jnp.zeros((n_groups, 1, D_MODEL), F32)
    for k in range(3):
        for nm in ("shift", "scale", "gate"):
            t = parts[k].get(nm, zero)
            if t.shape[0] < n_groups:
                t = jnp.concatenate([t, jnp.zeros((n_groups - t.shape[0], 1, D_MODEL), F32)], axis=0)
            rows.append(t)
    return jnp.concatenate(rows, axis=1).reshape(n_groups, N_MOD * D_MODEL)


def _local_step(x, ctx, target, mod_h, mod_g, norm_g, feed, pool_w, pool_scale, q_norm_g, kv_norm_g, conv_w,
                final_norm_g):
    n_lat, n_ctx = x.shape[0], ctx.shape[0]
    t_all = n_lat + n_ctx
    mg0 = jnp.stack([jnp.concatenate([mod_h[0], norm_g[0]], axis=0), jnp.concatenate([mod_g, norm_g[0]], axis=0)])
    mg1 = jnp.concatenate([mod_h[1], norm_g[1]], axis=0)[None]

    s0 = jnp.concatenate([x, ctx], axis=0)
    s1, sv_f00 = _ffn_half_fwd("l0f0", s0, mg0, 0, feed, 0, 0.5, n_lat)

    w_in, w_uq, w_ukv_t, w_ab_out = feed.weights("l0m", ["in_t", "uq", "ukv_t", "ab_out"], s1)
    kv_rows = KV_RANK + QK_ROPE
    w_in_t = jnp.concatenate([
        w_in[:POOL_DIM], jnp.zeros((PA_CQ - POOL_DIM, D_MODEL), BF16), w_in[POOL_DIM:POOL_DIM + Q_RANK],
        w_in[POOL_DIM + Q_RANK:], jnp.zeros((PA_KV_W - kv_rows, D_MODEL), BF16)], axis=0)
    ua, proj = _adaln_mm("l0m_proj", s1, mg0, 1, n_lat, w_in_t)
    pool_y, pool_p = _pool_fwd("l0m_pool", proj, n_lat, pool_w.astype(BF16), pool_scale)
    nq = _rmsnorm_fwd("l0m_qnorm", proj, Q_RANK, PA_CQ // Q_RANK, q_norm_g, n_lat)
    q_lin = _mm("l0m_q", [(nq, w_uq)], "nn", F32, 512, 768)
    cos_q, sin_q = _rope_tables(n_lat, n_lat, QK_NOPE, HEADS, 0)
    perm_q = _rope_perm(QK_NOPE, HEADS, 0)
    q_rot = _rope("l0m_qrope", q_lin, Q_RANK, 0, cos_q, sin_q, jnp.asarray(perm_q, BF16), False, BF16)
    cos_k, sin_k = _rope_tables(n_lat, t_all, KV_RANK, 1, PA_KV_W - kv_rows)
    perm_k = _rope_perm(KV_RANK, 1, PA_KV_W - kv_rows)
    kvr = _rope("l0m_krope", proj, PA_KV_W, PA_KV // PA_KV_W, cos_k, sin_k, jnp.asarray(perm_k, BF16), False, F32)
    nkv = _rmsnorm_fwd("l0m_kvnorm", kvr, KV_RANK, 0, kv_norm_g, t_all)
    kv = _mm("l0m_kv", [(nkv, w_ukv_t)], "nt", BF16, 768, 512)
    qh = jnp.pad(q_rot.reshape(n_lat, HEADS, QK_HEAD), ((0, 0), (0, 0), (0, HEAD_PAD - QK_HEAD))).transpose(1, 0, 2)
    kvh = kv.reshape(t_all, HEADS, QK_NOPE + V_HEAD)
    k_rope = jnp.broadcast_to(kvr[:, None, KV_RANK:KV_RANK + QK_ROPE].astype(BF16), (t_all, HEADS, QK_ROPE))
    kh = jnp.concatenate([kvh[:, :, :QK_NOPE], k_rope, jnp.zeros((t_all, HEADS, HEAD_PAD - QK_HEAD), BF16)],
                         axis=-1).transpose(1, 0, 2)
    vh = kvh[:, :, QK_NOPE:].transpose(1, 0, 2)
    oh, lse = _attn_fwd("l0m_attn", qh, kh, vh)
    cat = jnp.concatenate([pool_y, oh.transpose(1, 0, 2).reshape(n_lat, HEADS * V_HEAD)], axis=-1)
    h1 = s1[:n_lat]
    h2, mix_o = _mm_resid("l0m_out", cat, w_ab_out, h1, mg0[:1], 1, 1.0, n_lat)

    h3, sv_f01 = _ffn_half_fwd("l0f1", h2, mg0[:1], 2, feed, 1, 0.5, n_lat)

    h4, sv_f10 = _ffn_half_fwd("l1f0", h3, mg1, 0, feed, 2, 0.5, n_lat)
    w_cin_t, w_c_out = feed.weights("l1m", ["cin_t", "c_out"], h4)
    uc, z3 = _adaln_mm("l1m_in", h4, mg1, 1, n_lat, w_cin_t)
    yc = _conv_fwd("l1m_conv", z3, conv_w)
    h5, conv_o = _mm_resid("l1m_out", yc, w_c_out, h4, mg1, 1, 1.0, n_lat)
    h6, sv_f11 = _ffn_half_fwd("l1f1", h5, mg1, 2, feed, 3, 0.5, n_lat)

    dh6, sq_cols, d_final_g = _final_loss("loss_head", h6, target, final_norm_g)
    g = {}
    dh5, g["f11"] = _ffn_half_bwd("l1f1", dh6, sv_f11, mg1, 2, feed, 3, 0.5, n_lat)

    do_c, dyc, dgate_c = _gate_mm("l1m_dy", dh5, conv_o, mg1, 1, 1.0, n_lat, w_c_out)
    d_c_out = _mm("l1m_dwout", [(yc, do_c)], "tn", BF16)
    db_, dc_, dv_, d_conv_w = _conv_bwd("l1m_dconv", dyc, z3, conv_w)
    dz3 = jnp.concatenate([db_, dc_, dv_], axis=-1)
    d_cin_t = _mm("l1m_dwin", [(dz3, uc)], "tn", BF16)
    token = feed.grads("l1m", {"c_out": d_c_out, "cin_t": d_cin_t})
    dh4, (dsh_c, dsc_c, dgn_c) = _du_adaln("l1m_du", [(dz3, w_cin_t)], h4, dh5, mg1, 1, n_lat, _after(token))
    dh3, g["f10"] = _ffn_half_bwd("l1f0", dh4, sv_f10, mg1, 0, feed, 2, 0.5, n_lat)

    dh2, g["f01"] = _ffn_half_bwd("l0f1", dh3, sv_f01, mg0[:1], 2, feed, 1, 0.5, n_lat)

    do_a, dcat, dgate_a = _gate_mm("l0m_dcat", dh2, mix_o, mg0[:1], 1, 1.0, n_lat, w_ab_out)
    d_ab_out = _mm("l0m_dwout", [(cat, do_a)], "tn", BF16)
    d_pool_x, d_pool_w, d_pool_scale = _pool_bwd("l0m_dpool", dcat, n_lat, pool_p, pool_w.astype(BF16), pool_scale)
    doh = dcat[:, POOL_DIM:].reshape(n_lat, HEADS, V_HEAD).transpose(1, 0, 2).astype(BF16)
    dqh, dkh, dvh, dk_sum = _attn_bwd("l0m_dattn", qh, kh, vh, oh, lse, doh)
    dq_rot = dqh[:, :, :QK_HEAD].transpose(1, 0, 2).reshape(n_lat, Q_RANK)
    dq_lin = _rope("l0m_dqrope", dq_rot, Q_RANK, 0, cos_q, sin_q, jnp.asarray(perm_q.T, BF16), True, BF16)
    d_uq = _mm("l0m_dwuq", [(nq, dq_lin)], "tn", BF16, 768, 768)
    dnq = _mm("l0m_dnq", [(dq_lin, w_uq)], "nt", F32, 512, 768)
    dcq, d_q_norm_g = _rmsnorm_bwd("l0m_dqnorm", proj, Q_RANK, PA_CQ // Q_RANK, dnq, q_norm_g, n_lat, BF16)
    dkv = jnp.concatenate([dkh[:, :, :QK_NOPE], dvh], axis=-1).transpose(1, 0, 2).reshape(t_all, HEADS * HEAD_PAD)
    dkv = dkv.astype(BF16)
    dnkv = _mm("l0m_dnkv", [(dkv, w_ukv_t)], "nn", F32, 768, 256)
    d_ukv_t = _mm("l0m_dwukv", [(dkv, nkv)], "tn", BF16, 512, 256)
    dckv, d_kv_norm_g = _rmsnorm_bwd("l0m_dkvnorm", kvr, KV_RANK, 0, dnkv, kv_norm_g, t_all)
    dkvr = jnp.concatenate([dckv, dk_sum[:, QK_NOPE:QK_HEAD],
                            jnp.zeros((t_all, PA_KV_W - KV_RANK - QK_ROPE), F32)], axis=-1)
    dpb = _rope("l0m_dkrope", dkvr, PA_KV_W, 0, cos_k, sin_k, jnp.asarray(perm_k.T, BF16), True, BF16)
    dproj_lat = jnp.concatenate([d_pool_x, jnp.zeros((n_lat, PA_CQ - POOL_DIM), BF16), dcq, dpb[:n_lat]], axis=-1)
    dproj_ctx = jnp.concatenate([jnp.zeros((n_ctx, PA_KV), BF16), dpb[n_lat:]], axis=-1)
    dproj = jnp.concatenate([dproj_lat, dproj_ctx], axis=0)
    d_in_pad = _mm("l0m_dwin", [(dproj, ua)], "tn", BF16, 640, 512)
    d_in_t = jnp.concatenate([d_in_pad[:POOL_DIM], d_in_pad[PA_CQ:PA_CQ + Q_RANK],
                              d_in_pad[PA_KV:PA_KV + kv_rows]], axis=0)
    token = feed.grads("l0m", {"ab_out": d_ab_out, "uq": d_uq, "ukv_t": d_ukv_t, "in_t": d_in_t})
    dh2_all = jnp.concatenate([dh2, jnp.zeros((n_ctx, D_MODEL), F32)], axis=0)
    ds1, (dsh_a, dsc_a, dgn_a) = _du_adaln("l0m_du", [(dproj, w_in_t)], s1, dh2_all, mg0, 1, n_lat, _after(token))
    ds0, g["f00"] = _ffn_half_bwd("l0f0", ds1, sv_f00, mg0, 0, feed, 0, 0.5, n_lat)

    dmod0 = _mod_grad([g["f00"], dict(shift=dsh_a, scale=dsc_a, gate=dgate_a), g["f01"]], 2)
    dmod1 = _mod_grad([g["f10"], dict(shift=dsh_c, scale=dsc_c, gate=dgate_c), g["f11"]], 1)
    d_norm_g = jnp.stack([
        jnp.concatenate([jnp.sum(g["f00"]["gain"], axis=0), jnp.sum(dgn_a, axis=0), g["f01"]["gain"][0]], axis=0),
        jnp.concatenate([g["f10"]["gain"][0], dgn_c[0], g["f11"]["gain"][0]], axis=0)])
    grads = dict(
        pool_w=d_pool_w, pool_scale=d_pool_scale, q_norm_g=d_q_norm_g[0], kv_norm_g=d_kv_norm_g[0],
        conv_w=d_conv_w, final_norm_g=d_final_g[0], norm_g=d_norm_g,
        mod_h=jnp.stack([dmod0[0], dmod1[0]]), mod_g=dmod0[1])
    return sq_cols, ds0, grads


HBM_SPEC = pl.BlockSpec(memory_space=pltpu.HBM)
SEM_SPEC = pl.BlockSpec(memory_space=pltpu.SEMAPHORE)
ANY_SPEC = pl.BlockSpec(memory_space=pl.ANY)
SIDE_EFFECT = pltpu.SideEffectType.DATAFLOW_SIDE_EFFECTING
N_PEERS = N_DEV - 1


def _mesh_place():
    mx, my, mc = lax.axis_index("x"), lax.axis_index("y"), lax.axis_index("c")
    return mx, my, mc, 4 * mx + 2 * my + mc


def _peer(place, kk):
    mx, my, mc, _ = place
    px = jnp.bitwise_xor(mx, (kk >> 2) & 1)
    py = jnp.bitwise_xor(my, (kk >> 1) & 1)
    pc = jnp.bitwise_xor(mc, kk & 1)
    return (px, py, pc), 4 * px + 2 * py + pc


def _hbm(a):
    return pltpu.with_memory_space_constraint(a, pltpu.HBM)


def _landing(block, me):
    zone = lax.empty((N_DEV,) + block.shape, block.dtype)
    return lax.dynamic_update_slice(zone, block[None], (me,) + (0,) * block.ndim)


ALL_PEERS = tuple(range(1, N_DEV))
SIBLING = 1
CHIP_PEERS = (2, 4, 6)
RELAYED = (3, 5, 7)


def _exchange_start(name, srcs, lands, scatter, after, peers=ALL_PEERS):
    n = len(srcs)
    extra = [] if after is None else [after]

    def body(*refs):
        src, land = refs[:n], refs[n:2 * n]
        send_sems, recv_sems, token = refs[2 * n + len(extra)], refs[2 * n + len(extra) + 1], refs[-1]
        place = _mesh_place()
        for a in range(n):
            for kk in peers:
                dev, peer = _peer(place, kk)
                pltpu.make_async_remote_copy(
                    src_ref=src[a].at[peer] if scatter else src[a], dst_ref=land[a].at[place[3]],
                    send_sem=send_sems.at[a * N_PEERS + kk - 1], recv_sem=recv_sems.at[a * N_PEERS + kk - 1],
                    device_id=dev, device_id_type=MESH).start()
        token[...] = jnp.zeros_like(token)

    thru = [pltpu.HBM(t.shape, t.dtype) for t in (*srcs, *lands)]
    res = pl.pallas_call(
        body, name=name,
        out_shape=(pltpu.SemaphoreType.DMA((n * N_PEERS,)), pltpu.SemaphoreType.DMA((n * N_PEERS,)), *thru,
                   SDS((8, 128), F32)),
        in_specs=[HBM_SPEC] * (2 * n) + [ANY_SPEC] * len(extra),
        out_specs=(SEM_SPEC, SEM_SPEC, *([HBM_SPEC] * (2 * n)), pl.BlockSpec(memory_space=pltpu.VMEM)),
        input_output_aliases={i: 2 + i for i in range(2 * n)},
        compiler_params=pltpu.CompilerParams(has_side_effects=SIDE_EFFECT),
    )(*[_hbm(s) for s in srcs], *[_hbm(t) for t in lands], *extra)
    return res[0], res[1], list(res[2:2 + n]), list(res[2 + n:2 + 2 * n]), res[-1]


def _exchange_wait(name, send_sems, recv_sems, srcs, lands, places, scatter, after):
    n = len(srcs)

    def body(*refs):
        src, land = refs[:n], refs[n:2 * n]
        send, recv = refs[2 * n], refs[2 * n + 1]
        place = _mesh_place()
        for a in range(n):
            for kk in range(1, N_DEV):
                dev, peer = _peer(place, kk)
                cp = pltpu.make_async_remote_copy(
                    src_ref=src[a].at[peer] if scatter else src[a], dst_ref=land[a].at[peer],
                    send_sem=send.at[places[a] * N_PEERS + kk - 1], recv_sem=recv.at[places[a] * N_PEERS + kk - 1],
                    device_id=dev, device_id_type=MESH)
                cp.wait_send()
                cp.wait_recv()

    thru = [pltpu.HBM(t.shape, t.dtype) for t in (*srcs, *lands)]
    res = pl.pallas_call(
        body, name=name, out_shape=tuple(thru),
        in_specs=[HBM_SPEC] * (2 * n) + [SEM_SPEC, SEM_SPEC] + [ANY_SPEC] * len(after),
        out_specs=tuple([HBM_SPEC] * (2 * n)), input_output_aliases={i: i for i in range(2 * n)},
        compiler_params=pltpu.CompilerParams(has_side_effects=SIDE_EFFECT),
    )(*srcs, *lands, send_sems, recv_sems, *after)
    return list(res[n:])


def _gather_relay(name, send1, recv1, lands, places, after):
    n = len(lands)

    def body(*refs):
        land, s1, r1 = refs[:n], refs[n], refs[n + 1]
        s2, r2 = refs[n + 3], refs[n + 4]
        place = _mesh_place()
        sibling = _peer(place, SIBLING)[0]
        for a in range(n):
            for j, kk in enumerate(CHIP_PEERS):
                dev, origin = _peer(place, kk)
                block = land[a].at[origin]
                pltpu.make_async_remote_copy(
                    src_ref=block, dst_ref=block, send_sem=s1.at[places[a] * N_PEERS + kk - 1],
                    recv_sem=r1.at[places[a] * N_PEERS + kk - 1], device_id=dev, device_id_type=MESH).wait_recv()
                pltpu.make_async_remote_copy(
                    src_ref=block, dst_ref=block, send_sem=s2.at[a * 3 + j], recv_sem=r2.at[a * 3 + j],
                    device_id=sibling, device_id_type=MESH).start()

    res = pl.pallas_call(
        body, name=name,
        out_shape=(pltpu.SemaphoreType.DMA((3 * n,)), pltpu.SemaphoreType.DMA((3 * n,)),
                   *[pltpu.HBM(t.shape, t.dtype) for t in lands]),
        in_specs=[HBM_SPEC] * n + [SEM_SPEC, SEM_SPEC, ANY_SPEC],
        out_specs=(SEM_SPEC, SEM_SPEC, *([HBM_SPEC] * n)),
        input_output_aliases={i: 2 + i for i in range(n)},
        compiler_params=pltpu.CompilerParams(has_side_effects=SIDE_EFFECT),
    )(*lands, send1, recv1, after)
    return res[0], res[1], list(res[2:])


def _gather_wait(name, send1, recv1, send2, recv2, srcs, lands, places, after):
    n = len(lands)

    def body(*refs):
        src, land = refs[:n], refs[n:2 * n]
        s1, r1, s2, r2 = refs[2 * n:2 * n + 4]
        place = _mesh_place()
        for a in range(n):
            for kk in (SIBLING,) + CHIP_PEERS:
                dev, origin = _peer(place, kk)
                first = pltpu.make_async_remote_copy(
                    src_ref=src[a], dst_ref=land[a].at[origin], send_sem=s1.at[places[a] * N_PEERS + kk - 1],
                    recv_sem=r1.at[places[a] * N_PEERS + kk - 1], device_id=dev, device_id_type=MESH)
                first.wait_send()
                if kk == SIBLING:
                    first.wait_recv()
            for j, kk in enumerate(CHIP_PEERS):
                dev, origin = _peer(place, kk + 1)
                relay = pltpu.make_async_remote_copy(
                    src_ref=src[a], dst_ref=land[a].at[origin], send_sem=s2.at[a * 3 + j], recv_sem=r2.at[a * 3 + j],
                    device_id=dev, device_id_type=MESH)
                relay.wait_send()
                relay.wait_recv()

    arrays = (*srcs, *lands)
    res = pl.pallas_call(
        body, name=name, out_shape=tuple(pltpu.HBM(t.shape, t.dtype) for t in arrays),
        in_specs=[HBM_SPEC] * (2 * n) + [SEM_SPEC] * 4 + [ANY_SPEC], out_specs=tuple([HBM_SPEC] * (2 * n)),
        input_output_aliases={i: i for i in range(2 * n)},
        compiler_params=pltpu.CompilerParams(has_side_effects=SIDE_EFFECT),
    )(*arrays, send1, recv1, send2, recv2, after)
    return list(res[n:])


class _Feed:
    def __init__(self, shards, groups, me):
        self.shards, self.groups, self.me, self.pos = shards, groups, me, 0
        self.sems, self.srcs, self.lands = {}, {}, {}
        self.relays = {}
        self.pending = []

    def start(self, tag, names, after):
        srcs = [self.shards[nm] for nm in names]
        lands = [_landing(s, self.me) for s in srcs]
        send, recv, srcs, lands, self.token = _exchange_start(
            f"gather_start_{tag}", srcs, lands, False, after, (SIBLING,) + CHIP_PEERS)
        for i, nm in enumerate(names):
            self.sems[nm], self.srcs[nm], self.lands[nm] = (send, recv, i), srcs[i], lands[i]
        return self.token

    def _relay(self, gi, after):
        names = self.groups[gi]
        if gi not in self.relays:
            send, recv, _ = self.sems[names[0]]
            places = [self.sems[nm][2] for nm in names]
            send2, recv2, lands = _gather_relay(f"gather_relay_{gi}", send, recv, [self.lands[nm] for nm in names],
                                                places, after)
            for nm, t in zip(names, lands):
                self.lands[nm] = t
            self.relays[gi] = (send2, recv2)
            after = lands[0]
        return after

    def weights(self, tag, names, after):
        gi = self.pos
        assert names == self.groups[gi], (names, self.groups[gi])
        if gi == 0:
            after = self.token
        self._relay(gi, after)
        if 1 <= gi < len(self.groups) - 1:
            after = self._relay(gi + 1, after)
        send2, recv2 = self.relays[gi]
        send, recv, _ = self.sems[names[0]]
        got = _gather_wait(f"gather_wait_{tag}", send, recv, send2, recv2, [self.srcs[nm] for nm in names],
                           [self.lands[nm] for nm in names], [self.sems[nm][2] for nm in names], after)
        self.pos += 1
        return [t.reshape((N_DEV * t.shape[1],) + t.shape[2:]) for t in got]

    def grads(self, tag, full):
        names = list(full)
        srcs = [full[nm].reshape((N_DEV, full[nm].shape[0] // N_DEV) + full[nm].shape[1:]) for nm in names]
        lands = [_landing(lax.dynamic_index_in_dim(s, self.me, 0, keepdims=False), self.me) for s in srcs]
        send, recv, srcs, lands, token = _exchange_start(f"scatter_start_{tag}", srcs, lands, True, None)
        self.pending.append((tag, names, send, recv, srcs, lands))
        return token[0, 0]

    def collect(self, tags, after, keep_slots=()):
        out = {}
        for tag, names, send, recv, srcs, lands in self.pending:
            if tag not in tags:
                continue
            got = _exchange_wait(f"scatter_wait_{tag}", send, recv, srcs, lands, list(range(len(names))), True, after)
            for nm, slots in zip(names, got):
                out[nm] = slots if nm.startswith(tuple(keep_slots)) else _sum_slots(f"reduce_{nm}", slots)
        return out


def _adamw_math(w, gg, m, v):
    nm = ADAM_B1 * m + (1.0 - ADAM_B1) * gg
    nv = ADAM_B2 * v + (1.0 - ADAM_B2) * (gg * gg)
    bc1 = 1.0 - ADAM_B1 ** ADAM_STEP
    bc2 = 1.0 - ADAM_B2 ** ADAM_STEP
    return -ADAM_LR * ((nm / bc1) / (jnp.sqrt(nv / bc2) + ADAM_EPS) + ADAM_WD * w), nm, nv


def _adamw_part(name, i, w, slots, m, v, prev):
    n_parts, rows, cols = w.shape
    tr = _tile(rows, 256, 16)
    if prev is None:
        prev = tuple(lax.empty(w.shape, F32) for _ in range(4))

    def body(w_ref, g_ref, m_ref, v_ref, *rest):
        go_ref, d_ref, nm_ref, nv_ref = rest[4:]
        gg = g_ref[0].astype(F32)
        for sl in range(1, N_DEV):
            gg = gg + g_ref[sl].astype(F32)
        d, nm, nv = _adamw_math(w_ref[...], gg, m_ref[...], v_ref[...])
        go_ref[...] = gg
        d_ref[...] = d
        nm_ref[...] = nm
        nv_ref[...] = nv

    part = pl.BlockSpec((None, tr, cols), lambda r: (i, r, 0))
    return pl.pallas_call(
        body, name=name, grid=(rows // tr,),
        in_specs=[part, pl.BlockSpec((N_DEV, tr, cols), lambda r: (0, r, 0)), part, part] + [ANY_SPEC] * 4,
        out_specs=[part] * 4, out_shape=[SDS(w.shape, F32)] * 4,
        input_output_aliases={4 + k: k for k in range(4)}, compiler_params=_cparams(),
    )(w, slots, m, v, *prev)


WEIGHT_NAMES = ("c_ctx", "norm_g", "w_mod", "b_mod", "ffn_w_gate", "ffn_w_up", "ffn_w_down", "ab_w_in", "pool_w",
                "pool_scale", "q_norm_g", "w_uq", "kv_norm_g", "w_ukv", "ab_w_out", "conv_w_in", "conv_w",
                "conv_w_out", "final_norm_g")


def kernel(x, c, ctx, c_ctx, norm_g, w_mod, b_mod, ffn_w_gate, ffn_w_up, ffn_w_down, ab_w_in, pool_w, pool_scale, q_norm_g, w_uq, kv_norm_g, w_ukv, ab_w_out, conv_w_in, conv_w, conv_w_out, final_norm_g, loss_target, m_c_ctx, m_norm_g, m_w_mod, m_b_mod, m_ffn_w_gate, m_ffn_w_up, m_ffn_w_down, m_ab_w_in, m_pool_w, m_pool_scale, m_q_norm_g, m_w_uq, m_kv_norm_g, m_w_ukv, m_ab_w_out, m_conv_w_in, m_conv_w, m_conv_w_out, m_final_norm_g, v_c_ctx, v_norm_g, v_w_mod, v_b_mod, v_ffn_w_gate, v_ffn_w_up, v_ffn_w_down, v_ab_w_in, v_pool_w, v_pool_scale, v_q_norm_g, v_w_uq, v_kv_norm_g, v_w_ukv, v_ab_w_out, v_conv_w_in, v_conv_w, v_conv_w_out, v_final_norm_g):
    weights = (c_ctx, norm_g, w_mod, b_mod, ffn_w_gate, ffn_w_up, ffn_w_down, ab_w_in, pool_w, pool_scale, q_norm_g,
               w_uq, kv_norm_g, w_ukv, ab_w_out, conv_w_in, conv_w, conv_w_out, final_norm_g)
    moms = (m_c_ctx, m_norm_g, m_w_mod, m_b_mod, m_ffn_w_gate, m_ffn_w_up, m_ffn_w_down, m_ab_w_in, m_pool_w,
            m_pool_scale, m_q_norm_g, m_w_uq, m_kv_norm_g, m_w_ukv, m_ab_w_out, m_conv_w_in, m_conv_w, m_conv_w_out,
            m_final_norm_g)
    vels = (v_c_ctx, v_norm_g, v_w_mod, v_b_mod, v_ffn_w_gate, v_ffn_w_up, v_ffn_w_down, v_ab_w_in, v_pool_w,
            v_pool_scale, v_q_norm_g, v_w_uq, v_kv_norm_g, v_w_ukv, v_ab_w_out, v_conv_w_in, v_conv_w, v_conv_w_out,
            v_final_norm_g)
    me = 4 * lax.axis_index("x") + 2 * lax.axis_index("y") + lax.axis_index("c")
    n_lat, n_ctx = x.shape[1], ctx.shape[1]
    d = D_MODEL
    mod_cols = w_mod.shape[-1]
    ng_sh, cw_sh = norm_g.shape[-1], conv_w.shape[-1]

    def ffn_shards(i):
        return {f"gate_t{i}": ffn_w_gate[i // 2, i % 2].T, f"up_t{i}": ffn_w_up[i // 2, i % 2].T,
                f"down{i}": ffn_w_down[i // 2, i % 2]}

    local = {**ffn_shards(0), "in_t": ab_w_in[0].T, "uq": w_uq[0], "ukv_t": w_ukv[0].T, "ab_out": ab_w_out[0],
             **ffn_shards(1), **ffn_shards(2), "cin_t": conv_w_in[0].T, "c_out": conv_w_out[0], **ffn_shards(3)}
    ffn_groups = [[[f"gate_t{i}", f"up_t{i}"], [f"down{i}"]] for i in range(4)]
    groups = [*ffn_groups[0], ["in_t", "uq", "ukv_t", "ab_out"], *ffn_groups[1], *ffn_groups[2], ["cin_t", "c_out"],
              *ffn_groups[3]]
    feed = _Feed({nm: a.astype(BF16) for nm, a in local.items()}, groups, me)

    small = jnp.concatenate([c.reshape(-1), norm_g.reshape(-1), conv_w.reshape(-1)])
    small_n = -(-small.shape[0] // 1024) * 1024
    small = jnp.pad(small, (0, small_n - small.shape[0])).reshape(small_n // 128, 128)
    small_all = _exchange("gather_small", small, False).reshape(N_DEV, small_n)
    c_all = small_all[:, :d]
    o1 = d + 6 * ng_sh
    norm_g_full = small_all[:, d:o1].reshape(N_DEV, 2, 3, ng_sh).transpose(1, 2, 0, 3).reshape(2, 3, d)
    conv_w_full = small_all[:, o1:o1 + 3 * cw_sh].reshape(N_DEV, 3, cw_sh).transpose(1, 0, 2).reshape(3, d)

    cond = jnp.concatenate([c_all, jnp.broadcast_to(c_ctx[None, :], (N_DEV, d))], axis=0)
    sil, dsil = _silu_rows("mod_silu", cond)
    w_mod_b = w_mod.astype(BF16)
    b_sh = lax.dynamic_slice(b_mod, (0, me * mod_cols), (2, mod_cols))
    m_part = jnp.stack([_mm(f"mod_fwd{l}", [(sil, w_mod_b[l])], "nn", F32, 16, 384, bias=b_sh[l:l + 1])
                        for l in range(2)], axis=1)
    m_all = _exchange("gather_mod", m_part.reshape(-1, 128), False).reshape(N_DEV, 2 * N_DEV, 2, mod_cols)
    m_mine = lax.dynamic_index_in_dim(m_all, me, axis=1, keepdims=False)
    mod_h = m_mine.transpose(1, 0, 2).reshape(2, N_MOD, d)
    mod_g = m_all[:, N_DEV, 0, :].reshape(N_MOD, d)

    first = feed.start("first", [nm for grp in groups[:3] for nm in grp], m_all)
    feed.start("rest", [nm for grp in groups[3:] for nm in grp], first)

    sq_cols, ds0, g = _local_step(x[0], ctx[0], loss_target[0], mod_h, mod_g, norm_g_full, feed, pool_w[0],
                                  pool_scale, q_norm_g, kv_norm_g, conv_w_full, final_norm_g)
    grad_x = ds0[:n_lat]
    loss = lax.psum(0.5 * jnp.sum(sq_cols) / d, ("x", "y", "c"))
    w_of, m_of, v_of = (dict(zip(WEIGHT_NAMES, t)) for t in (weights, moms, vels))
    results = {}

    def update(nm, grad, view=lambda t: t):
        outs = _adamw(f"adamw_{nm}", view(w_of[nm]), grad.reshape(view(w_of[nm]).shape), view(m_of[nm]), view(v_of[nm]))
        results[nm] = tuple(view(t) for t in (grad.reshape(view(w_of[nm]).shape), *outs))

    def swap(t):
        return jnp.swapaxes(t, -1, -2)

    stacked = ("gate_t", "up_t", "down")
    early = feed.collect(["l1f1", "l1m", "l1f0", "l0f1", "l0m"], [ds0], stacked)
    update("ab_w_in", early["in_t"], swap)
    update("w_uq", early["uq"])
    update("w_ukv", early["ukv_t"].T)
    update("ab_w_out", early["ab_out"])
    update("conv_w_in", early["cin_t"].T)
    update("conv_w_out", early["c_out"])
    ffn = {}
    for nm, prefix, view in (("ffn_w_gate", "gate_t", swap), ("ffn_w_up", "up_t", swap),
                             ("ffn_w_down", "down", lambda t: t)):
        w4, m4, v4 = (view(t).reshape((4,) + view(t).shape[-2:]) for t in (w_of[nm], m_of[nm], v_of[nm]))
        prev = None
        for i in (3, 2, 1):
            prev = _adamw_part(f"adamw_{nm}{i}", i, w4, early[f"{prefix}{i}"], m4, v4, prev)
        ffn[nm] = (prefix, view, w4, m4, v4, prev)
    done_early = [results[nm][1] for nm in results] + [state[5][1] for state in ffn.values()]
    late = feed.collect(["l0f0"], done_early, stacked)
    for nm, (prefix, view, w4, m4, v4, prev) in ffn.items():
        outs = _adamw_part(f"adamw_{nm}0", 0, w4, late[f"{prefix}0"], m4, v4, prev)
        results[nm] = tuple(view(t.reshape(view(w_of[nm]).shape)) for t in outs)

    dm = jnp.stack([g["mod_h"], jnp.stack([g["mod_g"], jnp.zeros_like(g["mod_g"])])])
    dm_all = _exchange("gather_dmod", dm.reshape(-1, 128), False, results["ffn_w_down"][1]).reshape(N_DEV, 2, 2, N_MOD * d)
    grad_b_mod = _sum_rows("dmod_bias", dm_all.reshape(2 * N_DEV, 2 * N_MOD * d)).reshape(2, N_MOD * d)
    dm_sh = lax.dynamic_slice(dm_all, (0, 0, 0, me * mod_cols), (N_DEV, 2, 2, mod_cols))
    gw_mod, cctx_parts = [], []
    for l in range(2):
        dm_l = dm_sh[:, :, l, :].transpose(1, 0, 2).reshape(2 * N_DEV, mod_cols).astype(BF16)
        gw_mod.append(_mm(f"mod_dw{l}", [(sil, dm_l)], "tn", F32, 512, 384))
        dm_ctx = jnp.concatenate([dm_l[N_DEV:], jnp.zeros((N_DEV, mod_cols), BF16)], axis=0)
        cctx_parts.append(_mm(f"mod_dcond{l}", [(dm_ctx, w_mod_b[l])], "nt", F32, 16, 512))
    cctx_part = _sum_rows("mod_dcond_sum", jnp.concatenate(cctx_parts, axis=0))
    update("w_mod", jnp.stack(gw_mod))
    update("b_mod", grad_b_mod)

    small_g = jnp.concatenate([g["pool_w"].reshape(-1), g["pool_scale"].reshape(-1), g["q_norm_g"].reshape(-1),
                               g["kv_norm_g"].reshape(-1), g["final_norm_g"].reshape(-1), g["norm_g"].reshape(-1),
                               g["conv_w"].reshape(-1), cctx_part.reshape(-1)])
    sizes = [pool_w.size, pool_scale.size, q_norm_g.size, kv_norm_g.size, d, 6 * d, 3 * d, d]
    sg_n = -(-small_g.shape[0] // 1024) * 1024
    small_g = jnp.pad(small_g, (0, sg_n - small_g.shape[0]))
    sg_all = _exchange("gather_small_grads", small_g.reshape(-1, 128), False).reshape(N_DEV, sg_n)
    scale_vec = jnp.concatenate([jnp.ones((1, sum(sizes[:-1])), F32), dsil[N_DEV:N_DEV + 1],
                                 jnp.ones((1, sg_n - sum(sizes)), F32)], axis=1)
    sg = _sum_rows("small_grads_sum", sg_all, scale_vec)[0]
    cuts, pos = [], 0
    for sz in sizes:
        cuts.append(sg[pos:pos + sz])
        pos += sz
    g_pool_w, g_pool_scale, g_q_norm, g_kv_norm, g_final, g_norm_full, g_conv_full, g_c_ctx = cuts
    update("c_ctx", g_c_ctx)
    update("norm_g", lax.dynamic_slice(g_norm_full.reshape(2, 3, d), (0, 0, me * ng_sh), (2, 3, ng_sh)))
    update("conv_w", lax.dynamic_slice(g_conv_full.reshape(3, d), (0, me * cw_sh), (3, cw_sh)))
    update("pool_w", g_pool_w)
    update("pool_scale", g_pool_scale)
    update("q_norm_g", g_q_norm)
    update("kv_norm_g", g_kv_norm)
    update("final_norm_g", g_final)
    outs = [results[nm] for nm in WEIGHT_NAMES]
    return (loss, grad_x[None], *[o[0] for o in outs], *[o[1] for o in outs], *[o[2] for o in outs],
            *[o[3] for o in outs])
```

```python
import functools
import math

import jax
import jax.numpy as jnp
import numpy as np
from jax import lax
from jax.experimental import pallas as pl
from jax.experimental.pallas import tpu as pltpu

F32 = jnp.float32
BF16 = jnp.bfloat16
MESH = pl.DeviceIdType.MESH
SDS = jax.ShapeDtypeStruct

N_DEV = 8
D_MODEL = 1024
N_MOD = 9
D_FF = 2816
POOL_WINDOWS = (2, 4, 8, 16)
POOL_DIM = 512
POOL_GROUP_DIM = 128
HEADS = 8
QK_NOPE = 64
QK_ROPE = 32
QK_HEAD = QK_NOPE + QK_ROPE
V_HEAD = 64
Q_RANK = 768
KV_RANK = 256
GRID_W = 64
ROPE_THETA = 10000.0
RMS_EPS = 1e-6
ATTN_SCALE = 1.0 / math.sqrt(QK_HEAD)
HEAD_PAD = 128
POOL_PAD = 16
PA_POOL, PA_CQ, PA_KV = 0, 768, 1536
PA_KV_W = 384
PA_W = PA_KV + PA_KV_W

ADAM_LR, ADAM_B1, ADAM_B2, ADAM_EPS, ADAM_WD, ADAM_STEP = 0.001, 0.9, 0.999, 1e-08, 0.01, 10

VMEM_LIMIT_BYTES = 56 * 1024 * 1024

NN = ((1,), (0,))
NT = ((1,), (1,))
TN = ((0,), (0,))


def _cparams():
    return pltpu.CompilerParams(vmem_limit_bytes=VMEM_LIMIT_BYTES)


def _dot(a, b, dims):
    return lax.dot_general(a, b, (dims, ((), ())), preferred_element_type=F32)


def _tile(n, cap, mult=8):
    t = (min(cap, n) // mult) * mult
    while t >= mult:
        if n % t == 0:
            return t
        t -= mult
    return n


def _colsum(x):
    return jnp.sum(x, axis=0, keepdims=True)


def _rms(x):
    r = lax.rsqrt(jnp.mean(x * x, axis=-1, keepdims=True) + RMS_EPS)
    return x * r, r


def _rms_bwd(n, r, dn):
    return r * (dn - n * jnp.mean(dn * n, axis=-1, keepdims=True))


def _rowwise(name, fn, t_rows, tm, n_lat, rows, vecs, outs, accs):
    nt = t_rows // tm
    nlt = n_lat // tm
    n_groups = 2 if nlt < nt else 1

    def grp(i):
        return jnp.where(i >= nlt, 1, 0) if n_groups == 2 else 0

    in_specs = [pl.BlockSpec((tm, w), functools.partial(lambda i, cb: (i, cb), cb=cb)) for (_, w, cb) in rows]
    in_specs += [pl.BlockSpec((1,) + v.shape[1:], lambda i: (grp(i), 0, 0)) for v in vecs]
    out_specs = [pl.BlockSpec((tm, w), lambda i: (i, 0)) for (w, _) in outs]
    out_specs += [pl.BlockSpec((1, 1, w), lambda i: (grp(i), 0, 0)) for w in accs]
    out_shape = [SDS((t_rows, w), dt) for (w, dt) in outs] + [SDS((n_groups, 1, w), F32) for w in accs]
    n_r, n_v, n_o = len(rows), len(vecs), len(outs)

    def body(*refs):
        row_vals = [r[...] for r in refs[:n_r]]
        vec_vals = [v[0] for v in refs[n_r:n_r + n_v]]
        out_refs = refs[n_r + n_v:n_r + n_v + n_o]
        acc_refs = refs[n_r + n_v + n_o:]
        out_vals, acc_vals = fn(row_vals, vec_vals)
        for o_ref, o in zip(out_refs, out_vals):
            o_ref[...] = o.astype(o_ref.dtype)
        if acc_refs:
            i = pl.program_id(0)
            first = (i == 0) | (i == nlt) if n_groups == 2 else i == 0

            @pl.when(first)
            def _():
                for a_ref, a in zip(acc_refs, acc_vals):
                    a_ref[0] = a

            @pl.when(jnp.logical_not(first))
            def _():
                for a_ref, a in zip(acc_refs, acc_vals):
                    a_ref[0] += a

    res = pl.pallas_call(
        body, name=name, grid=(nt,), in_specs=in_specs, out_specs=out_specs, out_shape=out_shape,
        compiler_params=_cparams(),
    )(*[r[0] for r in rows], *vecs)
    return res[:n_o], res[n_o:]


RESIDENT_BYTES = 12 * 1024 * 1024


def _mm(name, pairs, mode, out_dtype, tm_cap=256, tn_cap=512, bias=None):
    a0, b0 = pairs[0]
    if mode == "nn":
        m, n, dims = a0.shape[0], b0.shape[1], NN
    elif mode == "nt":
        m, n, dims = a0.shape[0], b0.shape[0], NT
    else:
        m, n, dims = a0.shape[1], b0.shape[1], TN
    b_bytes = sum(b.size * b.dtype.itemsize for _, b in pairs)
    tn = n if b_bytes <= RESIDENT_BYTES else _tile(n, tn_cap, 128)
    tm = _tile(m, tm_cap, 128 if mode == "tn" else 16)

    def a_spec(a):
        if mode == "tn":
            return pl.BlockSpec((a.shape[0], tm), lambda i, j: (0, i))
        return pl.BlockSpec((tm, a.shape[1]), lambda i, j: (i, 0))

    def b_spec(b):
        if mode == "nt":
            return pl.BlockSpec((tn, b.shape[1]), lambda i, j: (j, 0))
        return pl.BlockSpec((b.shape[0], tn), lambda i, j: (0, j))

    in_specs, flat = [], []
    for a, b in pairs:
        in_specs += [a_spec(a), b_spec(b)]
        flat += [a, b]
    if bias is not None:
        in_specs.append(pl.BlockSpec((1, tn), lambda i, j: (0, j)))
        flat.append(bias)
    n_pairs = len(pairs)

    def body(*refs):
        acc = None
        for p in range(n_pairs):
            t = _dot(refs[2 * p][...], refs[2 * p + 1][...], dims)
            acc = t if acc is None else acc + t
        if bias is not None:
            acc = acc + refs[2 * n_pairs][...]
        refs[-1][...] = acc.astype(refs[-1].dtype)

    return pl.pallas_call(
        body, name=name, grid=(m // tm, n // tn), in_specs=in_specs,
        out_specs=pl.BlockSpec((tm, tn), lambda i, j: (i, j)),
        out_shape=SDS((m, n), out_dtype), compiler_params=_cparams(),
    )(*flat)


def _mm_resid(name, a, b, s, mg, k, coef, n_lat):
    t_rows, n = a.shape[0], b.shape[1]
    tm = _tile(math.gcd(n_lat, t_rows), 256, 16)
    nlt = n_lat // tm
    n_groups = 2 if nlt < t_rows // tm else 1

    def grp(i):
        return jnp.where(i >= nlt, 1, 0) if n_groups == 2 else 0

    def body(a_ref, b_ref, s_ref, mg_ref, so_ref, o_ref):
        o = _dot(a_ref[...], b_ref[...], NN)
        gate = mg_ref[0, 3 * k + 2:3 * k + 3, :]
        o_ref[...] = o.astype(BF16)
        so_ref[...] = s_ref[...] + (coef * gate) * o

    row = pl.BlockSpec((tm, n), lambda i: (i, 0))
    return pl.pallas_call(
        body, name=name, grid=(t_rows // tm,),
        in_specs=[pl.BlockSpec((tm, a.shape[1]), lambda i: (i, 0)), pl.BlockSpec(b.shape, lambda i: (0, 0)), row,
                  pl.BlockSpec((1, mg.shape[1], n), lambda i: (grp(i), 0, 0))],
        out_specs=[row, row], out_shape=[SDS((t_rows, n), F32), SDS((t_rows, n), BF16)], compiler_params=_cparams(),
    )(a, b, s, mg)


def _groups(t_rows, tm, n_lat):
    nlt = n_lat // tm
    if nlt < t_rows // tm:
        return 2, (lambda i: jnp.where(i >= nlt, 1, 0)), (lambda i: (i == 0) | (i == nlt))
    return 1, (lambda i: 0), (lambda i: i == 0)


def _accumulate(acc_refs, vals, first):
    @pl.when(first)
    def _():
        for r, v in zip(acc_refs, vals):
            r[0] = v

    @pl.when(jnp.logical_not(first))
    def _():
        for r, v in zip(acc_refs, vals):
            r[0] += v


def _adaln_math(s, m, k):
    n, _ = _rms(s)
    return (n * m[9 + k:10 + k]) * (1.0 + m[3 * k + 1:3 * k + 2]) + m[3 * k:3 * k + 1]


def _ffn_up(name, s, mg, k, n_lat, wg_t, wu_t):
    t_rows, f = s.shape[0], wg_t.shape[0]
    tm = _row_tm(t_rows, n_lat)
    _, grp, _ = _groups(t_rows, tm, n_lat)

    def body(s_ref, mg_ref, wg_ref, wu_ref, u_ref, a_ref, b_ref, h_ref):
        uu = _adaln_math(s_ref[...], mg_ref[0], k).astype(BF16)
        u_ref[...] = uu
        a = _dot(uu, wg_ref[...], NT)
        b = _dot(uu, wu_ref[...], NT)
        sg = jax.nn.sigmoid(a)
        act = a * sg
        a_ref[...] = (b * (sg * (1.0 + a * (1.0 - sg)))).astype(BF16)
        b_ref[...] = act.astype(BF16)
        h_ref[...] = (act * b).astype(BF16)

    w_spec = pl.BlockSpec(wg_t.shape, lambda i: (0, 0))
    o_spec = pl.BlockSpec((tm, f), lambda i: (i, 0))
    row = pl.BlockSpec((tm, s.shape[1]), lambda i: (i, 0))
    return pl.pallas_call(
        body, name=name, grid=(t_rows // tm,),
        in_specs=[row, pl.BlockSpec((1,) + mg.shape[1:], lambda i: (grp(i), 0, 0)), w_spec, w_spec],
        out_specs=[row, o_spec, o_spec, o_spec],
        out_shape=[SDS(s.shape, BF16)] + [SDS((t_rows, f), BF16)] * 3, compiler_params=_cparams(),
    )(s, mg, wg_t, wu_t)


def _ffn_dact(name, ds_out, o, mg, k, coef, n_lat, wd, a, b):
    t_rows, f = ds_out.shape[0], wd.shape[0]
    tm = _row_tm(t_rows, n_lat)
    n_groups, grp, first = _groups(t_rows, tm, n_lat)
    d = ds_out.shape[1]

    def body(ds_ref, o_ref, mg_ref, wd_ref, a_ref, b_ref, do_ref, da_ref, db_ref, dg_ref):
        dd = coef * ds_ref[...]
        do = (dd * mg_ref[0, 3 * k + 2:3 * k + 3, :]).astype(BF16)
        do_ref[...] = do
        _accumulate([dg_ref], [_colsum(dd * o_ref[...].astype(F32))], first(pl.program_id(0)))
        dh = _dot(do, wd_ref[...], NT)
        da_ref[...] = (dh * a_ref[...].astype(F32)).astype(BF16)
        db_ref[...] = (dh * b_ref[...].astype(F32)).astype(BF16)

    row = pl.BlockSpec((tm, d), lambda i: (i, 0))
    t_spec = pl.BlockSpec((tm, f), lambda i: (i, 0))
    return pl.pallas_call(
        body, name=name, grid=(t_rows // tm,),
        in_specs=[row, row, pl.BlockSpec((1,) + mg.shape[1:], lambda i: (grp(i), 0, 0)),
                  pl.BlockSpec(wd.shape, lambda i: (0, 0)), t_spec, t_spec],
        out_specs=[row, t_spec, t_spec, pl.BlockSpec((1, 1, d), lambda i: (grp(i), 0, 0))],
        out_shape=[SDS((t_rows, d), BF16), SDS((t_rows, f), BF16), SDS((t_rows, f), BF16), SDS((n_groups, 1, d), F32)],
        compiler_params=_cparams(),
    )(ds_out, o, mg, wd, a, b)


def _du_adaln(name, pairs, s, ds_out, mg, k, n_lat, after):
    t_rows, d = s.shape
    tm = _row_tm(t_rows, n_lat)
    n_groups, grp, first = _groups(t_rows, tm, n_lat)
    n_pairs = len(pairs)

    def body(*refs):
        s_ref, ds_ref, mg_ref, z_ref, out_ref, dsh_ref, dsc_ref, dgn_ref = refs[2 * n_pairs:]
        d_u = z_ref[...]
        for p in range(n_pairs):
            d_u = d_u + _dot(refs[p][...], refs[n_pairs + p][...], NN)
        m = mg_ref[0]
        gain, scale = m[9 + k:10 + k], m[3 * k + 1:3 * k + 2]
        n, r = _rms(s_ref[...])
        dxn = d_u * (1.0 + scale)
        out_ref[...] = ds_ref[...] + _rms_bwd(n, r, dxn * gain)
        _accumulate([dsh_ref, dsc_ref, dgn_ref], [_colsum(d_u), _colsum(d_u * (n * gain)), _colsum(dxn * n)],
                    first(pl.program_id(0)))

    row = pl.BlockSpec((tm, d), lambda i: (i, 0))
    acc = pl.BlockSpec((1, 1, d), lambda i: (grp(i), 0, 0))
    res = pl.pallas_call(
        body, name=name, grid=(t_rows // tm,),
        in_specs=[pl.BlockSpec((tm, a.shape[1]), lambda i: (i, 0)) for a, _ in pairs]
        + [pl.BlockSpec(w.shape, lambda i: (0, 0)) for _, w in pairs]
        + [row, row, pl.BlockSpec((1,) + mg.shape[1:], lambda i: (grp(i), 0, 0)), pl.BlockSpec((1, d), lambda i: (0, 0))],
        out_specs=[row, acc, acc, acc],
        out_shape=[SDS((t_rows, d), F32)] + [SDS((n_groups, 1, d), F32)] * 3, compiler_params=_cparams(),
    )(*[a for a, _ in pairs], *[w for _, w in pairs], s, ds_out, mg, after)
    return res[0], res[1:]


def _adaln_mm(name, s, mg, k, n_lat, w_t):
    rows, d = s.shape
    tm = _row_tm(rows, n_lat)
    _, grp, _ = _groups(rows, tm, n_lat)
    n = w_t.shape[0]

    def body(s_ref, mg_ref, w_ref, u_ref, y_ref):
        uu = _adaln_math(s_ref[...], mg_ref[0], k).astype(BF16)
        u_ref[...] = uu
        y_ref[...] = _dot(uu, w_ref[...], NT)

    row = pl.BlockSpec((tm, d), lambda i: (i, 0))
    return pl.pallas_call(
        body, name=name, grid=(rows // tm,),
        in_specs=[row, pl.BlockSpec((1,) + mg.shape[1:], lambda i: (grp(i), 0, 0)), pl.BlockSpec(w_t.shape, lambda i: (0, 0))],
        out_specs=[row, pl.BlockSpec((tm, n), lambda i: (i, 0))],
        out_shape=[SDS((rows, d), BF16), SDS((rows, n), F32)], compiler_params=_cparams(),
    )(s, mg, w_t)


def _gate_mm(name, ds_out, o, mg, k, coef, n_lat, w):
    t_rows, d = ds_out.shape
    tm = _row_tm(t_rows, n_lat)
    n_groups, grp, first = _groups(t_rows, tm, n_lat)
    n = w.shape[0]

    def body(ds_ref, o_ref, mg_ref, w_ref, do_ref, y_ref, dg_ref):
        dd = coef * ds_ref[...]
        do = (dd * mg_ref[0, 3 * k + 2:3 * k + 3, :]).astype(BF16)
        do_ref[...] = do
        _accumulate([dg_ref], [_colsum(dd * o_ref[...].astype(F32))], first(pl.program_id(0)))
        y_ref[...] = _dot(do, w_ref[...], NT)

    row = pl.BlockSpec((tm, d), lambda i: (i, 0))
    return pl.pallas_call(
        body, name=name, grid=(t_rows // tm,),
        in_specs=[row, row, pl.BlockSpec((1,) + mg.shape[1:], lambda i: (grp(i), 0, 0)), pl.BlockSpec(w.shape, lambda i: (0, 0))],
        out_specs=[row, pl.BlockSpec((tm, n), lambda i: (i, 0)), pl.BlockSpec((1, 1, d), lambda i: (grp(i), 0, 0))],
        out_shape=[SDS((t_rows, d), BF16), SDS((t_rows, n), F32), SDS((n_groups, 1, d), F32)],
        compiler_params=_cparams(),
    )(ds_out, o, mg, w)


def _row_tm(t_rows, n_lat):
    return _tile(math.gcd(t_rows, n_lat), 256, 16)


def _rmsnorm_fwd(name, x, width, colblk, gain, t_rows):
    def fn(rv, vv):
        n, _ = _rms(rv[0])
        return [n * vv[0]], []

    (y,), _ = _rowwise(name, fn, t_rows, _tile(t_rows, 256, 16), t_rows, [(x, width, colblk)],
                       [gain.reshape(1, 1, width)], [(width, BF16)], [])
    return y


def _rmsnorm_bwd(name, x, width, colblk, dy, gain, t_rows, out_dtype=F32):
    def fn(rv, vv):
        n, r = _rms(rv[0])
        return [_rms_bwd(n, r, rv[1] * vv[0])], [_colsum(rv[1] * n)]

    (dx,), (dgain,) = _rowwise(name, fn, t_rows, _tile(t_rows, 256, 16), t_rows,
                               [(x, width, colblk), (dy, width, 0)], [gain.reshape(1, 1, width)],
                               [(width, out_dtype)], [width])
    return dx, dgain


def _final_loss(name, h, target, gain):
    t_rows = h.shape[0]
    inv_d = 1.0 / D_MODEL

    def fn(rv, vv):
        g = vv[0]
        n, r = _rms(rv[0])
        e = n * g - rv[1]
        dy = e * inv_d
        return [_rms_bwd(n, r, dy * g)], [_colsum(e * e), _colsum(dy * n)]

    (dh,), (sq, dgain) = _rowwise(name, fn, t_rows, _tile(t_rows, 256, 16), t_rows,
                                  [(h, D_MODEL, 0), (target, D_MODEL, 0)], [gain.reshape(1, 1, D_MODEL)],
                                  [(D_MODEL, F32)], [D_MODEL, D_MODEL])
    return dh, sq, dgain


def _rope(name, z, width, colblk, cos, sin, perm, backward, out_dtype):
    t_rows = cos.shape[0]

    def body(z_ref, c_ref, s_ref, p_ref, o_ref):
        zz = z_ref[...]
        pre = zz * s_ref[...] if backward else zz
        hi = pre.astype(BF16)
        lo = (pre - hi.astype(F32)).astype(BF16)
        rot = _dot(hi, p_ref[...], NN) + _dot(lo, p_ref[...], NN)
        if not backward:
            rot = rot * s_ref[...]
        o_ref[...] = (zz * c_ref[...] + rot).astype(o_ref.dtype)

    tm = _tile(t_rows, 256, 16)
    t_spec = pl.BlockSpec((tm, width), lambda i: (i, 0))
    return pl.pallas_call(
        body, name=name, grid=(t_rows // tm,),
        in_specs=[pl.BlockSpec((tm, width), lambda i: (i, colblk)), t_spec, t_spec,
                  pl.BlockSpec((width, width), lambda i: (0, 0))],
        out_specs=t_spec, out_shape=SDS((t_rows, width), out_dtype), compiler_params=_cparams(),
    )(z, cos, sin, perm)


def _window_sum(x, w, transposed):
    n_rows = x.shape[0]
    zeros = jnp.zeros((POOL_PAD, x.shape[1]), F32)
    y = jnp.concatenate([zeros, x, zeros], axis=0)
    total = n_rows + 2 * POOL_PAD
    if transposed:
        y = y + pltpu.roll(y, total - 1, 0)
    else:
        y = y + pltpu.roll(y, 1, 0)
    step = 1
    while 2 * step < w:
        y = pltpu.roll(y, step, 0) + pltpu.roll(y, total - step, 0)
        step *= 2
    return y[POOL_PAD:POOL_PAD + n_rows]


def _window_count(n_rows, w):
    t = lax.broadcasted_iota(jnp.int32, (n_rows, 1), 0)
    lo = jnp.maximum(t - w // 2, 0)
    hi = jnp.minimum(t + (w - w // 2 - 1), n_rows - 1)
    return (hi - lo + 1).astype(F32)


def _pool_fwd(name, proj, n_rows, w_grp, scale):
    def body(x_ref, w_ref, sc_ref, y_ref, p_ref):
        for g, w in enumerate(POOL_WINDOWS):
            cols = slice(g * POOL_GROUP_DIM, (g + 1) * POOL_GROUP_DIM)
            x = x_ref[:, cols]
            p = _window_sum(x, w, False) * (1.0 / _window_count(n_rows, w)) - x
            pb = p.astype(BF16)
            p_ref[:, cols] = pb
            y_ref[:, cols] = (_dot(pb, w_ref[g], NN) * sc_ref[:, cols]).astype(BF16)

    blk = pl.BlockSpec((n_rows, POOL_DIM), lambda i: (0, 0))
    return pl.pallas_call(
        body, name=name, grid=(1,),
        in_specs=[blk, pl.BlockSpec(w_grp.shape, lambda i: (0, 0, 0)), pl.BlockSpec((1, POOL_DIM), lambda i: (0, 0))],
        out_specs=[blk, blk], out_shape=[SDS((n_rows, POOL_DIM), BF16)] * 2, compiler_params=_cparams(),
    )(proj, w_grp, scale)


def _pool_bwd(name, dcat, n_rows, p, w_grp, scale):
    def body(dy_ref, p_ref, w_ref, sc_ref, dx_ref, dw_ref, dsc_ref):
        for g, w in enumerate(POOL_WINDOWS):
            cols = slice(g * POOL_GROUP_DIM, (g + 1) * POOL_GROUP_DIM)
            dy = dy_ref[:, cols]
            pb = p_ref[:, cols]
            pw = _dot(pb, w_ref[g], NN)
            dsc_ref[:, cols] = _colsum(dy * pw)
            dpw = (dy * sc_ref[:, cols]).astype(BF16)
            dw_ref[g] = _dot(pb, dpw, TN)
            dp = _dot(dpw, w_ref[g], NT)
            dx_ref[:, cols] = (_window_sum(dp * (1.0 / _window_count(n_rows, w)), w, True) - dp).astype(BF16)

    blk = pl.BlockSpec((n_rows, POOL_DIM), lambda i: (0, 0))
    w_spec = pl.BlockSpec(w_grp.shape, lambda i: (0, 0, 0))
    v_spec = pl.BlockSpec((1, POOL_DIM), lambda i: (0, 0))
    return pl.pallas_call(
        body, name=name, grid=(1,), in_specs=[blk, blk, w_spec, v_spec], out_specs=[blk, w_spec, v_spec],
        out_shape=[SDS((n_rows, POOL_DIM), BF16), SDS(w_grp.shape, F32), SDS((1, POOL_DIM), F32)],
        compiler_params=_cparams(),
    )(dcat, p, w_grp, scale)


def _attn_fwd(name, q, k, v):
    h, n_q, _ = q.shape
    n_k = k.shape[1]
    tq = _tile(n_q, 256, 16)

    def body(q_ref, k_ref, v_ref, o_ref, lse_ref):
        s = _dot(q_ref[...], k_ref[...], NT) * ATTN_SCALE
        m = jnp.max(s, axis=-1, keepdims=True)
        e = jnp.exp(s - m)
        l = jnp.sum(e, axis=-1, keepdims=True)
        p = (e * (1.0 / l)).astype(BF16)
        o_ref[...] = _dot(p, v_ref[...], NN).astype(BF16)
        lse_ref[...] = m + jnp.log(l)

    return pl.pallas_call(
        body, name=name, grid=(h, n_q // tq),
        in_specs=[pl.BlockSpec((None, tq, HEAD_PAD), lambda hh, i: (hh, i, 0)),
                  pl.BlockSpec((None, n_k, HEAD_PAD), lambda hh, i: (hh, 0, 0)),
                  pl.BlockSpec((None, n_k, V_HEAD), lambda hh, i: (hh, 0, 0))],
        out_specs=[pl.BlockSpec((None, tq, V_HEAD), lambda hh, i: (hh, i, 0)),
                   pl.BlockSpec((None, tq, 1), lambda hh, i: (hh, i, 0))],
        out_shape=[SDS((h, n_q, V_HEAD), BF16), SDS((h, n_q, 1), F32)], compiler_params=_cparams(),
    )(q, k, v)


def _attn_bwd(name, q, k, v, o, lse, do):
    h, n_q, _ = q.shape
    n_k = k.shape[1]
    tq = _tile(n_q, 256, 16)

    def body(q_ref, k_ref, v_ref, o_ref, lse_ref, do_ref, dq_ref, dk_ref, dv_ref, dks_ref):
        hh, i = pl.program_id(0), pl.program_id(1)
        qq, kk, dd = q_ref[...], k_ref[...], do_ref[...]
        s = _dot(qq, kk, NT) * ATTN_SCALE
        p = jnp.exp(s - lse_ref[...])
        dp = _dot(dd, v_ref[...], NT)
        delta = jnp.sum(dd.astype(F32) * o_ref[...].astype(F32), axis=-1, keepdims=True)
        ds = (p * (dp - delta) * ATTN_SCALE).astype(BF16)
        dq_ref[...] = _dot(ds, kk, NN)
        dk = _dot(ds, qq, TN)
        dv = _dot(p.astype(BF16), dd, TN)

        @pl.when(i == 0)
        def _():
            dk_ref[...] = dk
            dv_ref[...] = dv

        @pl.when(i > 0)
        def _():
            dk_ref[...] += dk
            dv_ref[...] += dv

        @pl.when((i == 0) & (hh == 0))
        def _():
            dks_ref[...] = dk

        @pl.when((i > 0) | (hh > 0))
        def _():
            dks_ref[...] += dk

    q_spec = pl.BlockSpec((None, tq, HEAD_PAD), lambda hh, i: (hh, i, 0))
    k_spec = pl.BlockSpec((None, n_k, HEAD_PAD), lambda hh, i: (hh, 0, 0))
    v_spec = pl.BlockSpec((None, n_k, V_HEAD), lambda hh, i: (hh, 0, 0))
    o_spec = pl.BlockSpec((None, tq, V_HEAD), lambda hh, i: (hh, i, 0))
    return pl.pallas_call(
        body, name=name, grid=(h, n_q // tq),
        in_specs=[q_spec, k_spec, v_spec, o_spec, pl.BlockSpec((None, tq, 1), lambda hh, i: (hh, i, 0)), o_spec],
        out_specs=[q_spec, k_spec, v_spec, pl.BlockSpec((n_k, HEAD_PAD), lambda hh, i: (0, 0))],
        out_shape=[SDS((h, n_q, HEAD_PAD), F32), SDS((h, n_k, HEAD_PAD), F32), SDS((h, n_k, V_HEAD), F32),
                   SDS((n_k, HEAD_PAD), F32)],
        compiler_params=_cparams(),
    )(q, k, v, o, lse, do)


CONV_COLS = 256


def _shift_rows(x, d):
    n_rows = x.shape[0]
    t = lax.broadcasted_iota(jnp.int32, (n_rows, 1), 0)
    if d > 0:
        return jnp.where(t >= d, pltpu.roll(x, d, 0), 0.0)
    return jnp.where(t < n_rows + d, pltpu.roll(x, n_rows + d, 0), 0.0)


def _conv_fwd(name, z3, conv_w):
    n_rows = z3.shape[0]
    nb = D_MODEL // CONV_COLS

    def body(b_ref, c_ref, v_ref, w_ref, y_ref):
        z = c_ref[...] * v_ref[...]
        zc = w_ref[0:1, :] * _shift_rows(z, 1) + w_ref[1:2, :] * z + w_ref[2:3, :] * _shift_rows(z, -1)
        y_ref[...] = (b_ref[...] * zc).astype(BF16)

    def part(k):
        return pl.BlockSpec((n_rows, CONV_COLS), lambda j: (0, k * nb + j))

    return pl.pallas_call(
        body, name=name, grid=(nb,),
        in_specs=[part(0), part(1), part(2), pl.BlockSpec((3, CONV_COLS), lambda j: (0, j))],
        out_specs=pl.BlockSpec((n_rows, CONV_COLS), lambda j: (0, j)),
        out_shape=SDS((n_rows, D_MODEL), BF16), compiler_params=_cparams(),
    )(z3, z3, z3, conv_w)


def _conv_bwd(name, dy, z3, conv_w):
    n_rows = z3.shape[0]
    nb = D_MODEL // CONV_COLS

    def body(dy_ref, b_ref, c_ref, v_ref, w_ref, db_ref, dc_ref, dv_ref, dw_ref):
        c, v, d_y = c_ref[...], v_ref[...], dy_ref[...]
        z = c * v
        z_dn, z_up = _shift_rows(z, 1), _shift_rows(z, -1)
        zc = w_ref[0:1, :] * z_dn + w_ref[1:2, :] * z + w_ref[2:3, :] * z_up
        db_ref[...] = (d_y * zc).astype(BF16)
        dzc = d_y * b_ref[...]
        dz = w_ref[0:1, :] * _shift_rows(dzc, -1) + w_ref[1:2, :] * dzc + w_ref[2:3, :] * _shift_rows(dzc, 1)
        dc_ref[...] = (dz * v).astype(BF16)
        dv_ref[...] = (dz * c).astype(BF16)
        dw_ref[0:1, :] = _colsum(dzc * z_dn)
        dw_ref[1:2, :] = _colsum(dzc * z)
        dw_ref[2:3, :] = _colsum(dzc * z_up)

    def part(k):
        return pl.BlockSpec((n_rows, CONV_COLS), lambda j: (0, k * nb + j))

    col = pl.BlockSpec((n_rows, CONV_COLS), lambda j: (0, j))
    w_spec = pl.BlockSpec((3, CONV_COLS), lambda j: (0, j))
    return pl.pallas_call(
        body, name=name, grid=(nb,), in_specs=[col, part(0), part(1), part(2), w_spec],
        out_specs=[col, col, col, w_spec],
        out_shape=[SDS((n_rows, D_MODEL), BF16)] * 3 + [SDS((3, D_MODEL), F32)], compiler_params=_cparams(),
    )(dy, z3, z3, z3, conv_w)


def _silu_rows(name, x):
    def body(x_ref, s_ref, d_ref):
        xx = x_ref[...]
        sg = jax.nn.sigmoid(xx)
        s_ref[...] = (xx * sg).astype(BF16)
        d_ref[...] = sg * (1.0 + xx * (1.0 - sg))

    return pl.pallas_call(body, name=name, out_shape=[SDS(x.shape, BF16), SDS(x.shape, F32)])(x)


def _sum_rows(name, x, scale=None):
    r, n = x.shape
    tn = _tile(n, 8192, 128)

    def body(*refs):
        acc = jnp.sum(refs[0][...].astype(F32), axis=0, keepdims=True)
        if scale is not None:
            acc = acc * refs[1][...]
        refs[-1][...] = acc

    in_specs = [pl.BlockSpec((r, tn), lambda j: (0, j))]
    args = [x]
    if scale is not None:
        in_specs.append(pl.BlockSpec((1, tn), lambda j: (0, j)))
        args.append(scale)
    return pl.pallas_call(body, name=name, grid=(n // tn,), in_specs=in_specs,
                          out_specs=pl.BlockSpec((1, tn), lambda j: (0, j)), out_shape=SDS((1, n), F32))(*args)


def _me_operand(me):
    return jnp.reshape(me, (1,)).astype(jnp.int32)


def _sum_slots(name, slots, src, me):
    n_slots, r, c = slots.shape
    tr = _tile(r, 432, 16)

    def body(me_ref, own_ref, x_ref, o_ref):
        acc = own_ref[...].astype(F32)
        for sl in range(n_slots):
            acc = acc + x_ref[sl].astype(F32)
        o_ref[...] = acc

    grid_spec = pltpu.PrefetchScalarGridSpec(
        num_scalar_prefetch=1, grid=(r // tr,),
        in_specs=[pl.BlockSpec((None, tr, c), lambda i, me_ref: (me_ref[0], i, 0)),
                  pl.BlockSpec((n_slots, tr, c), lambda i, me_ref: (0, i, 0))],
        out_specs=pl.BlockSpec((tr, c), lambda i, me_ref: (i, 0)))
    return pl.pallas_call(body, name=name, grid_spec=grid_spec, out_shape=SDS((r, c), F32),
                          compiler_params=_cparams())(_me_operand(me), src, slots)


def _adamw(name, w, g, m, v):
    shape = w.shape
    cols = shape[-1]
    rows = w.size // cols
    tr = _tile(rows, 512, 8)
    bc1 = 1.0 - ADAM_B1 ** ADAM_STEP
    bc2 = 1.0 - ADAM_B2 ** ADAM_STEP

    def body(w_ref, g_ref, m_ref, v_ref, d_ref, nm_ref, nv_ref):
        gg = g_ref[...]
        nm = ADAM_B1 * m_ref[...] + (1.0 - ADAM_B1) * gg
        nv = ADAM_B2 * v_ref[...] + (1.0 - ADAM_B2) * (gg * gg)
        nm_ref[...] = nm
        nv_ref[...] = nv
        d_ref[...] = -ADAM_LR * ((nm / bc1) / (jnp.sqrt(nv / bc2) + ADAM_EPS) + ADAM_WD * w_ref[...])

    spec = pl.BlockSpec((tr, cols), lambda i: (i, 0))
    outs = pl.pallas_call(body, name=name, grid=(rows // tr,), in_specs=[spec] * 4, out_specs=[spec] * 3,
                          out_shape=[SDS((rows, cols), F32)] * 3, compiler_params=_cparams())(
        w.reshape(rows, cols), g.reshape(rows, cols), m.reshape(rows, cols), v.reshape(rows, cols))
    return tuple(t.reshape(shape) for t in outs)


def _exchange(name, x, scatter, after=None):
    blk = x.shape[1:] if scatter else x.shape
    extra = [] if after is None else [after]

    def body(x_ref, *rest):
        out_ref, send_sems, recv_sems, local_sem = rest[len(extra):]
        mx, my, mc = lax.axis_index("x"), lax.axis_index("y"), lax.axis_index("c")
        me = 4 * mx + 2 * my + mc
        own = pltpu.make_async_copy(x_ref.at[me] if scatter else x_ref, out_ref.at[me], local_sem)
        own.start()
        copies = []
        for kk in range(1, N_DEV):
            px = jnp.bitwise_xor(mx, (kk >> 2) & 1)
            py = jnp.bitwise_xor(my, (kk >> 1) & 1)
            pc = jnp.bitwise_xor(mc, kk & 1)
            peer = 4 * px + 2 * py + pc
            send = pltpu.make_async_remote_copy(
                src_ref=x_ref.at[peer] if scatter else x_ref, dst_ref=out_ref.at[me],
                send_sem=send_sems.at[kk - 1], recv_sem=recv_sems.at[kk - 1],
                device_id=(px, py, pc), device_id_type=MESH)
            send.start()
            arrival = pltpu.make_async_remote_copy(
                src_ref=x_ref.at[peer] if scatter else x_ref, dst_ref=out_ref.at[peer],
                send_sem=send_sems.at[kk - 1], recv_sem=recv_sems.at[kk - 1],
                device_id=(px, py, pc), device_id_type=MESH)
            copies.append((send, arrival))
        for send, arrival in copies:
            arrival.wait_recv()
            send.wait_send()
        own.wait()

    return pl.pallas_call(
        body, name=name, out_shape=SDS((N_DEV,) + tuple(blk), x.dtype),
        in_specs=[pl.BlockSpec(memory_space=pl.ANY)] * (1 + len(extra)), out_specs=pl.BlockSpec(memory_space=pl.ANY),
        scratch_shapes=[pltpu.SemaphoreType.DMA((N_DEV - 1,)), pltpu.SemaphoreType.DMA((N_DEV - 1,)),
                        pltpu.SemaphoreType.DMA],
    )(x, *extra)


def _rope_perm(pre, reps, post):
    half = QK_ROPE // 4
    width = reps * (pre + QK_ROPE) + post
    p = np.zeros((width, width), np.float32)
    for rep in range(reps):
        s0 = rep * (pre + QK_ROPE) + pre
        for base in (s0, s0 + 2 * half):
            for i in range(half):
                p[base + half + i, base + i] = -1.0
                p[base + i, base + half + i] = 1.0
    return p


def _rope_tables(n_lat, t_rows, pre, reps, post):
    half = QK_ROPE // 4
    pos = jnp.arange(n_lat)
    freqs = jnp.power(ROPE_THETA, -jnp.arange(0, 2 * half, 2, dtype=F32) / (2 * half))
    ang_r = (pos // GRID_W).astype(F32)[:, None] * freqs
    ang_c = (pos % GRID_W).astype(F32)[:, None] * freqs
    ang = jnp.concatenate([ang_r, ang_r, ang_c, ang_c], axis=-1)

    def table(fn, plain):
        slot = jnp.concatenate([jnp.full((n_lat, pre), plain, F32), fn(ang)], axis=-1)
        t = jnp.concatenate([jnp.tile(slot, (1, reps)), jnp.full((n_lat, post), plain, F32)], axis=-1)
        return jnp.concatenate([t, jnp.full((t_rows - n_lat, t.shape[1]), plain, F32)], axis=0)

    return table(jnp.cos, 1.0), table(jnp.sin, 0.0)


def _ffn_half_fwd(tag, s, mg, k, feed, i, coef, n_lat):
    wg_t, wu_t = feed.weights(f"{tag}_up", [f"gate_t{i}", f"up_t{i}"], s)
    u, a, b, hid = _ffn_up(f"{tag}_up", s, mg, k, n_lat, wg_t, wu_t)
    (wd,) = feed.weights(f"{tag}_down", [f"down{i}"], hid)
    s_out, o = _mm_resid(f"{tag}_down", hid, wd, s, mg, k, coef, n_lat)
    return s_out, (s, u, a, b, hid, o, wg_t, wu_t, wd)


def _ffn_half_bwd(tag, ds_out, saved, mg, k, feed, i, coef, n_lat):
    s, u, a, b, hid, o, wg_t, wu_t, wd = saved
    do, da, db, dgate = _ffn_dact(f"{tag}_dact", ds_out, o, mg, k, coef, n_lat, wd, a, b)
    dwd = _mm(f"{tag}_dwd", [(hid, do)], "tn", BF16)
    dwg_t = _mm(f"{tag}_dwg", [(da, u)], "tn", BF16)
    dwu_t = _mm(f"{tag}_dwu", [(db, u)], "tn", BF16)
    token = feed.grads(tag, {f"down{i}": dwd, f"gate_t{i}": dwg_t, f"up_t{i}": dwu_t})
    ds_in, (dshift, dscale, dgain) = _du_adaln(f"{tag}_du", [(da, wg_t), (db, wu_t)], s, ds_out, mg, k, n_lat,
                                               _after(token))
    return ds_in, dict(shift=dshift, scale=dscale, gate=dgate, gain=dgain)


def _after(token):
    return jnp.zeros((1, D_MODEL), F32) + token


def _mod_grad(parts, n_groups):
    rows = []
    zero = jnp.zeros((n_groups, 1, D_MODEL), F32)
    for k in range(3):
        for nm in ("shift", "scale", "gate"):
            t = parts[k].get(nm, zero)
            if t.shape[0] < n_groups:
                t = jnp.concatenate([t, jnp.zeros((n_groups - t.shape[0], 1, D_MODEL), F32)], axis=0)
            rows.append(t)
    return jnp.concatenate(rows, axis=1).reshape(n_groups, N_MOD * D_MODEL)


def _local_step(x, ctx, target, mod_h, mod_g, norm_g, feed, pool_w, pool_scale, q_norm_g, kv_norm_g, conv_w,
                final_norm_g):
    n_lat, n_ctx = x.shape[0], ctx.shape[0]
    t_all = n_lat + n_ctx
    mg0 = jnp.stack([jnp.concatenate([mod_h[0], norm_g[0]], axis=0), jnp.concatenate([mod_g, norm_g[0]], axis=0)])
    mg1 = jnp.concatenate([mod_h[1], norm_g[1]], axis=0)[None]

    s0 = jnp.concatenate([x, ctx], axis=0)
    s1, sv_f00 = _ffn_half_fwd("l0f0", s0, mg0, 0, feed, 0, 0.5, n_lat)

    w_in, w_uq, w_ukv_t, w_ab_out = feed.weights("l0m", ["in_t", "uq", "ukv_t", "ab_out"], s1)
    kv_rows = KV_RANK + QK_ROPE
    w_in_t = jnp.concatenate([
        w_in[:POOL_DIM], jnp.zeros((PA_CQ - POOL_DIM, D_MODEL), BF16), w_in[POOL_DIM:POOL_DIM + Q_RANK],
        w_in[POOL_DIM + Q_RANK:], jnp.zeros((PA_KV_W - kv_rows, D_MODEL), BF16)], axis=0)
    ua, proj = _adaln_mm("l0m_proj", s1, mg0, 1, n_lat, w_in_t)
    pool_y, pool_p = _pool_fwd("l0m_pool", proj, n_lat, pool_w.astype(BF16), pool_scale)
    nq = _rmsnorm_fwd("l0m_qnorm", proj, Q_RANK, PA_CQ // Q_RANK, q_norm_g, n_lat)
    q_lin = _mm("l0m_q", [(nq, w_uq)], "nn", F32, 512, 768)
    cos_q, sin_q = _rope_tables(n_lat, n_lat, QK_NOPE, HEADS, 0)
    perm_q = _rope_perm(QK_NOPE, HEADS, 0)
    q_rot = _rope("l0m_qrope", q_lin, Q_RANK, 0, cos_q, sin_q, jnp.asarray(perm_q, BF16), False, BF16)
    cos_k, sin_k = _rope_tables(n_lat, t_all, KV_RANK, 1, PA_KV_W - kv_rows)
    perm_k = _rope_perm(KV_RANK, 1, PA_KV_W - kv_rows)
    kvr = _rope("l0m_krope", proj, PA_KV_W, PA_KV // PA_KV_W, cos_k, sin_k, jnp.asarray(perm_k, BF16), False, F32)
    nkv = _rmsnorm_fwd("l0m_kvnorm", kvr, KV_RANK, 0, kv_norm_g, t_all)
    kv = _mm("l0m_kv", [(nkv, w_ukv_t)], "nt", BF16, 768, 512)
    qh = jnp.pad(q_rot.reshape(n_lat, HEADS, QK_HEAD), ((0, 0), (0, 0), (0, HEAD_PAD - QK_HEAD))).transpose(1, 0, 2)
    kvh = kv.reshape(t_all, HEADS, QK_NOPE + V_HEAD)
    k_rope = jnp.broadcast_to(kvr[:, None, KV_RANK:KV_RANK + QK_ROPE].astype(BF16), (t_all, HEADS, QK_ROPE))
    kh = jnp.concatenate([kvh[:, :, :QK_NOPE], k_rope, jnp.zeros((t_all, HEADS, HEAD_PAD - QK_HEAD), BF16)],
                         axis=-1).transpose(1, 0, 2)
    vh = kvh[:, :, QK_NOPE:].transpose(1, 0, 2)
    oh, lse = _attn_fwd("l0m_attn", qh, kh, vh)
    cat = jnp.concatenate([pool_y, oh.transpose(1, 0, 2).reshape(n_lat, HEADS * V_HEAD)], axis=-1)
    h1 = s1[:n_lat]
    h2, mix_o = _mm_resid("l0m_out", cat, w_ab_out, h1, mg0[:1], 1, 1.0, n_lat)

    h3, sv_f01 = _ffn_half_fwd("l0f1", h2, mg0[:1], 2, feed, 1, 0.5, n_lat)

    h4, sv_f10 = _ffn_half_fwd("l1f0", h3, mg1, 0, feed, 2, 0.5, n_lat)
    w_cin_t, w_c_out = feed.weights("l1m", ["cin_t", "c_out"], h4)
    uc, z3 = _adaln_mm("l1m_in", h4, mg1, 1, n_lat, w_cin_t)
    yc = _conv_fwd("l1m_conv", z3, conv_w)
    h5, conv_o = _mm_resid("l1m_out", yc, w_c_out, h4, mg1, 1, 1.0, n_lat)
    h6, sv_f11 = _ffn_half_fwd("l1f1", h5, mg1, 2, feed, 3, 0.5, n_lat)

    dh6, sq_cols, d_final_g = _final_loss("loss_head", h6, target, final_norm_g)
    g = {}
    dh5, g["f11"] = _ffn_half_bwd("l1f1", dh6, sv_f11, mg1, 2, feed, 3, 0.5, n_lat)

    do_c, dyc, dgate_c = _gate_mm("l1m_dy", dh5, conv_o, mg1, 1, 1.0, n_lat, w_c_out)
    d_c_out = _mm("l1m_dwout", [(yc, do_c)], "tn", BF16)
    db_, dc_, dv_, d_conv_w = _conv_bwd("l1m_dconv", dyc, z3, conv_w)
    dz3 = jnp.concatenate([db_, dc_, dv_], axis=-1)
    d_cin_t = _mm("l1m_dwin", [(dz3, uc)], "tn", BF16)
    token = feed.grads("l1m", {"c_out": d_c_out, "cin_t": d_cin_t})
    dh4, (dsh_c, dsc_c, dgn_c) = _du_adaln("l1m_du", [(dz3, w_cin_t)], h4, dh5, mg1, 1, n_lat, _after(token))
    dh3, g["f10"] = _ffn_half_bwd("l1f0", dh4, sv_f10, mg1, 0, feed, 2, 0.5, n_lat)

    dh2, g["f01"] = _ffn_half_bwd("l0f1", dh3, sv_f01, mg0[:1], 2, feed, 1, 0.5, n_lat)

    do_a, dcat, dgate_a = _gate_mm("l0m_dcat", dh2, mix_o, mg0[:1], 1, 1.0, n_lat, w_ab_out)
    d_ab_out = _mm("l0m_dwout", [(cat, do_a)], "tn", BF16)
    d_pool_x, d_pool_w, d_pool_scale = _pool_bwd("l0m_dpool", dcat, n_lat, pool_p, pool_w.astype(BF16), pool_scale)
    doh = dcat[:, POOL_DIM:].reshape(n_lat, HEADS, V_HEAD).transpose(1, 0, 2).astype(BF16)
    dqh, dkh, dvh, dk_sum = _attn_bwd("l0m_dattn", qh, kh, vh, oh, lse, doh)
    dq_rot = dqh[:, :, :QK_HEAD].transpose(1, 0, 2).reshape(n_lat, Q_RANK)
    dq_lin = _rope("l0m_dqrope", dq_rot, Q_RANK, 0, cos_q, sin_q, jnp.asarray(perm_q.T, BF16), True, BF16)
    d_uq = _mm("l0m_dwuq", [(nq, dq_lin)], "tn", BF16, 768, 768)
    dnq = _mm("l0m_dnq", [(dq_lin, w_uq)], "nt", F32, 512, 768)
    dcq, d_q_norm_g = _rmsnorm_bwd("l0m_dqnorm", proj, Q_RANK, PA_CQ // Q_RANK, dnq, q_norm_g, n_lat, BF16)
    dkv = jnp.concatenate([dkh[:, :, :QK_NOPE], dvh], axis=-1).transpose(1, 0, 2).reshape(t_all, HEADS * HEAD_PAD)
    dkv = dkv.astype(BF16)
    dnkv = _mm("l0m_dnkv", [(dkv, w_ukv_t)], "nn", F32, 768, 256)
    d_ukv_t = _mm("l0m_dwukv", [(dkv, nkv)], "tn", BF16, 512, 256)
    dckv, d_kv_norm_g = _rmsnorm_bwd("l0m_dkvnorm", kvr, KV_RANK, 0, dnkv, kv_norm_g, t_all)
    dkvr = jnp.concatenate([dckv, dk_sum[:, QK_NOPE:QK_HEAD],
                            jnp.zeros((t_all, PA_KV_W - KV_RANK - QK_ROPE), F32)], axis=-1)
    dpb = _rope("l0m_dkrope", dkvr, PA_KV_W, 0, cos_k, sin_k, jnp.asarray(perm_k.T, BF16), True, BF16)
    dproj_lat = jnp.concatenate([d_pool_x, jnp.zeros((n_lat, PA_CQ - POOL_DIM), BF16), dcq, dpb[:n_lat]], axis=-1)
    dproj_ctx = jnp.concatenate([jnp.zeros((n_ctx, PA_KV), BF16), dpb[n_lat:]], axis=-1)
    dproj = jnp.concatenate([dproj_lat, dproj_ctx], axis=0)
    d_in_pad = _mm("l0m_dwin", [(dproj, ua)], "tn", BF16, 640, 512)
    d_in_t = jnp.concatenate([d_in_pad[:POOL_DIM], d_in_pad[PA_CQ:PA_CQ + Q_RANK],
                              d_in_pad[PA_KV:PA_KV + kv_rows]], axis=0)
    token = feed.grads("l0m", {"ab_out": d_ab_out, "uq": d_uq, "ukv_t": d_ukv_t, "in_t": d_in_t})
    dh2_all = jnp.concatenate([dh2, jnp.zeros((n_ctx, D_MODEL), F32)], axis=0)
    ds1, (dsh_a, dsc_a, dgn_a) = _du_adaln("l0m_du", [(dproj, w_in_t)], s1, dh2_all, mg0, 1, n_lat, _after(token))
    ds0, g["f00"] = _ffn_half_bwd("l0f0", ds1, sv_f00, mg0, 0, feed, 0, 0.5, n_lat)

    dmod0 = _mod_grad([g["f00"], dict(shift=dsh_a, scale=dsc_a, gate=dgate_a), g["f01"]], 2)
    dmod1 = _mod_grad([g["f10"], dict(shift=dsh_c, scale=dsc_c, gate=dgate_c), g["f11"]], 1)
    d_norm_g = jnp.stack([
        jnp.concatenate([jnp.sum(g["f00"]["gain"], axis=0), jnp.sum(dgn_a, axis=0), g["f01"]["gain"][0]], axis=0),
        jnp.concatenate([g["f10"]["gain"][0], dgn_c[0], g["f11"]["gain"][0]], axis=0)])
    grads = dict(
        pool_w=d_pool_w, pool_scale=d_pool_scale, q_norm_g=d_q_norm_g[0], kv_norm_g=d_kv_norm_g[0],
        conv_w=d_conv_w, final_norm_g=d_final_g[0], norm_g=d_norm_g,
        mod_h=jnp.stack([dmod0[0], dmod1[0]]), mod_g=dmod0[1])
    return sq_cols, ds0, grads


HBM_SPEC = pl.BlockSpec(memory_space=pltpu.HBM)
SEM_SPEC = pl.BlockSpec(memory_space=pltpu.SEMAPHORE)
ANY_SPEC = pl.BlockSpec(memory_space=pl.ANY)
SIDE_EFFECT = pltpu.SideEffectType.DATAFLOW_SIDE_EFFECTING
N_PEERS = N_DEV - 1


def _mesh_place():
    mx, my, mc = lax.axis_index("x"), lax.axis_index("y"), lax.axis_index("c")
    return mx, my, mc, 4 * mx + 2 * my + mc


def _peer(place, kk):
    mx, my, mc, _ = place
    px = jnp.bitwise_xor(mx, (kk >> 2) & 1)
    py = jnp.bitwise_xor(my, (kk >> 1) & 1)
    pc = jnp.bitwise_xor(mc, kk & 1)
    return (px, py, pc), 4 * px + 2 * py + pc


def _hbm(a):
    return pltpu.with_memory_space_constraint(a, pltpu.HBM)


def _landing(block, me):
    zone = lax.empty((N_DEV,) + block.shape, block.dtype)
    return lax.dynamic_update_slice(zone, block[None], (me,) + (0,) * block.ndim)


ALL_PEERS = tuple(range(1, N_DEV))
SIBLING = 1
CHIP_PEERS = (2, 4, 6)
RELAYED = (3, 5, 7)


def _exchange_start(name, srcs, lands, scatter, after, peers=ALL_PEERS):
    n = len(srcs)
    extra = [] if after is None else [after]

    def body(*refs):
        src, land = refs[:n], refs[n:2 * n]
        send_sems, recv_sems, token = refs[2 * n + len(extra)], refs[2 * n + len(extra) + 1], refs[-1]
        place = _mesh_place()
        for a in range(n):
            for kk in peers:
                dev, peer = _peer(place, kk)
                pltpu.make_async_remote_copy(
                    src_ref=src[a].at[peer] if scatter else src[a],
                    dst_ref=land[a].at[kk - 1] if scatter else land[a].at[place[3]],
                    send_sem=send_sems.at[a * N_PEERS + kk - 1], recv_sem=recv_sems.at[a * N_PEERS + kk - 1],
                    device_id=dev, device_id_type=MESH).start()
        token[...] = jnp.zeros_like(token)

    thru = [pltpu.HBM(t.shape, t.dtype) for t in (*srcs, *lands)]
    res = pl.pallas_call(
        body, name=name,
        out_shape=(pltpu.SemaphoreType.DMA((n * N_PEERS,)), pltpu.SemaphoreType.DMA((n * N_PEERS,)), *thru,
                   SDS((8, 128), F32)),
        in_specs=[HBM_SPEC] * (2 * n) + [ANY_SPEC] * len(extra),
        out_specs=(SEM_SPEC, SEM_SPEC, *([HBM_SPEC] * (2 * n)), pl.BlockSpec(memory_space=pltpu.VMEM)),
        input_output_aliases={i: 2 + i for i in range(2 * n)},
        compiler_params=pltpu.CompilerParams(has_side_effects=SIDE_EFFECT),
    )(*[_hbm(s) for s in srcs], *[_hbm(t) for t in lands], *extra)
    return res[0], res[1], list(res[2:2 + n]), list(res[2 + n:2 + 2 * n]), res[-1]


def _exchange_wait(name, send_sems, recv_sems, srcs, lands, places, scatter, after):
    n = len(srcs)

    def body(*refs):
        src, land = refs[:n], refs[n:2 * n]
        send, recv = refs[2 * n], refs[2 * n + 1]
        place = _mesh_place()
        for a in range(n):
            for kk in range(1, N_DEV):
                dev, peer = _peer(place, kk)
                cp = pltpu.make_async_remote_copy(
                    src_ref=src[a].at[peer] if scatter else src[a],
                    dst_ref=land[a].at[kk - 1] if scatter else land[a].at[peer],
                    send_sem=send.at[places[a] * N_PEERS + kk - 1], recv_sem=recv.at[places[a] * N_PEERS + kk - 1],
                    device_id=dev, device_id_type=MESH)
                cp.wait_send()
                cp.wait_recv()

    thru = [pltpu.HBM(t.shape, t.dtype) for t in (*srcs, *lands)]
    res = pl.pallas_call(
        body, name=name, out_shape=tuple(thru),
        in_specs=[HBM_SPEC] * (2 * n) + [SEM_SPEC, SEM_SPEC] + [ANY_SPEC] * len(after),
        out_specs=tuple([HBM_SPEC] * (2 * n)), input_output_aliases={i: i for i in range(2 * n)},
        compiler_params=pltpu.CompilerParams(has_side_effects=SIDE_EFFECT),
    )(*srcs, *lands, send_sems, recv_sems, *after)
    return list(res[:n]), list(res[n:])


def _gather_relay(name, send1, recv1, lands, places, after):
    n = len(lands)

    def body(*refs):
        land, s1, r1 = refs[:n], refs[n], refs[n + 1]
        s2, r2 = refs[n + 3], refs[n + 4]
        place = _mesh_place()
        sibling = _peer(place, SIBLING)[0]
        for a in range(n):
            for j, kk in enumerate(CHIP_PEERS):
                dev, origin = _peer(place, kk)
                block = land[a].at[origin]
                pltpu.make_async_remote_copy(
                    src_ref=block, dst_ref=block, send_sem=s1.at[places[a] * N_PEERS + kk - 1],
                    recv_sem=r1.at[places[a] * N_PEERS + kk - 1], device_id=dev, device_id_type=MESH).wait_recv()
                pltpu.make_async_remote_copy(
                    src_ref=block, dst_ref=block, send_sem=s2.at[a * 3 + j], recv_sem=r2.at[a * 3 + j],
                    device_id=sibling, device_id_type=MESH).start()

    res = pl.pallas_call(
        body, name=name,
        out_shape=(pltpu.SemaphoreType.DMA((3 * n,)), pltpu.SemaphoreType.DMA((3 * n,)),
                   *[pltpu.HBM(t.shape, t.dtype) for t in lands]),
        in_specs=[HBM_SPEC] * n + [SEM_SPEC, SEM_SPEC, ANY_SPEC],
        out_specs=(SEM_SPEC, SEM_SPEC, *([HBM_SPEC] * n)),
        input_output_aliases={i: 2 + i for i in range(n)},
        compiler_params=pltpu.CompilerParams(has_side_effects=SIDE_EFFECT),
    )(*lands, send1, recv1, after)
    return res[0], res[1], list(res[2:])


def _gather_wait(name, send1, recv1, send2, recv2, srcs, lands, places, after):
    n = len(lands)

    def body(*refs):
        src, land = refs[:n], refs[n:2 * n]
        s1, r1, s2, r2 = refs[2 * n:2 * n + 4]
        place = _mesh_place()
        for a in range(n):
            for kk in (SIBLING,) + CHIP_PEERS:
                dev, origin = _peer(place, kk)
                first = pltpu.make_async_remote_copy(
                    src_ref=src[a], dst_ref=land[a].at[origin], send_sem=s1.at[places[a] * N_PEERS + kk - 1],
                    recv_sem=r1.at[places[a] * N_PEERS + kk - 1], device_id=dev, device_id_type=MESH)
                first.wait_send()
                if kk == SIBLING:
                    first.wait_recv()
            for j, kk in enumerate(CHIP_PEERS):
                dev, origin = _peer(place, kk + 1)
                relay = pltpu.make_async_remote_copy(
                    src_ref=src[a], dst_ref=land[a].at[origin], send_sem=s2.at[a * 3 + j], recv_sem=r2.at[a * 3 + j],
                    device_id=dev, device_id_type=MESH)
                relay.wait_send()
                relay.wait_recv()

    arrays = (*srcs, *lands)
    res = pl.pallas_call(
        body, name=name, out_shape=tuple(pltpu.HBM(t.shape, t.dtype) for t in arrays),
        in_specs=[HBM_SPEC] * (2 * n) + [SEM_SPEC] * 4 + [ANY_SPEC], out_specs=tuple([HBM_SPEC] * (2 * n)),
        input_output_aliases={i: i for i in range(2 * n)},
        compiler_params=pltpu.CompilerParams(has_side_effects=SIDE_EFFECT),
    )(*arrays, send1, recv1, send2, recv2, after)
    return list(res[n:])


class _Feed:
    def __init__(self, shards, groups, me):
        self.shards, self.groups, self.me, self.pos = shards, groups, me, 0
        self.sems, self.srcs, self.lands = {}, {}, {}
        self.relays = {}
        self.pending = []

    def start(self, tag, names, after):
        srcs = [self.shards[nm] for nm in names]
        lands = [_landing(s, self.me) for s in srcs]
        send, recv, srcs, lands, self.token = _exchange_start(
            f"gather_start_{tag}", srcs, lands, False, after, (SIBLING,) + CHIP_PEERS)
        for i, nm in enumerate(names):
            self.sems[nm], self.srcs[nm], self.lands[nm] = (send, recv, i), srcs[i], lands[i]
        return self.token

    def _relay(self, gi, after):
        names = self.groups[gi]
        if gi not in self.relays:
            send, recv, _ = self.sems[names[0]]
            places = [self.sems[nm][2] for nm in names]
            send2, recv2, lands = _gather_relay(f"gather_relay_{gi}", send, recv, [self.lands[nm] for nm in names],
                                                places, after)
            for nm, t in zip(names, lands):
                self.lands[nm] = t
            self.relays[gi] = (send2, recv2)
            after = lands[0]
        return after

    def weights(self, tag, names, after):
        gi = self.pos
        assert names == self.groups[gi], (names, self.groups[gi])
        if gi == 0:
            after = self.token
        self._relay(gi, after)
        if 1 <= gi < len(self.groups) - 1:
            after = self._relay(gi + 1, after)
        send2, recv2 = self.relays[gi]
        send, recv, _ = self.sems[names[0]]
        got = _gather_wait(f"gather_wait_{tag}", send, recv, send2, recv2, [self.srcs[nm] for nm in names],
                           [self.lands[nm] for nm in names], [self.sems[nm][2] for nm in names], after)
        self.pos += 1
        return [t.reshape((N_DEV * t.shape[1],) + t.shape[2:]) for t in got]

    def grads(self, tag, full):
        names = list(full)
        srcs = [full[nm].reshape((N_DEV, full[nm].shape[0] // N_DEV) + full[nm].shape[1:]) for nm in names]
        lands = [lax.empty((N_PEERS,) + s.shape[1:], s.dtype) for s in srcs]
        send, recv, srcs, lands, token = _exchange_start(f"scatter_start_{tag}", srcs, lands, True, None)
        self.pending.append((tag, names, send, recv, srcs, lands))
        return token[0, 0]

    def collect(self, tags, after, keep_slots=()):
        out = {}
        for tag, names, send, recv, srcs, lands in self.pending:
            if tag not in tags:
                continue
            srcs, got = _exchange_wait(f"scatter_wait_{tag}", send, recv, srcs, lands, list(range(len(names))), True,
                                       after)
            for nm, slots, src in zip(names, got, srcs):
                out[nm] = ((slots, src) if nm.startswith(tuple(keep_slots))
                           else _sum_slots(f"reduce_{nm}", slots, src, self.me))
        return out


def _adamw_math(w, gg, m, v):
    nm = ADAM_B1 * m + (1.0 - ADAM_B1) * gg
    nv = ADAM_B2 * v + (1.0 - ADAM_B2) * (gg * gg)
    bc1 = 1.0 - ADAM_B1 ** ADAM_STEP
    bc2 = 1.0 - ADAM_B2 ** ADAM_STEP
    return -ADAM_LR * ((nm / bc1) / (jnp.sqrt(nv / bc2) + ADAM_EPS) + ADAM_WD * w), nm, nv


def _adamw_part(name, i, w, scattered, me, m, v, prev):
    n_parts, rows, cols = w.shape
    tr = _tile(rows, 256, 16)
    if prev is None:
        prev = tuple(lax.empty(w.shape, F32) for _ in range(4))

    slots, src = scattered

    def body(me_ref, w_ref, g_ref, own_ref, m_ref, v_ref, *rest):
        go_ref, d_ref, nm_ref, nv_ref = rest[4:]
        gg = own_ref[...].astype(F32)
        for sl in range(N_PEERS):
            gg = gg + g_ref[sl].astype(F32)
        d, nm, nv = _adamw_math(w_ref[...], gg, m_ref[...], v_ref[...])
        go_ref[...] = gg
        d_ref[...] = d
        nm_ref[...] = nm
        nv_ref[...] = nv

    part = pl.BlockSpec((None, tr, cols), lambda r, me_ref: (i, r, 0))
    grid_spec = pltpu.PrefetchScalarGridSpec(
        num_scalar_prefetch=1, grid=(rows // tr,),
        in_specs=[part, pl.BlockSpec((N_PEERS, tr, cols), lambda r, me_ref: (0, r, 0)),
                  pl.BlockSpec((None, tr, cols), lambda r, me_ref: (me_ref[0], r, 0)), part, part] + [ANY_SPEC] * 4,
        out_specs=[part] * 4)
    return pl.pallas_call(
        body, name=name, grid_spec=grid_spec, out_shape=[SDS(w.shape, F32)] * 4,
        input_output_aliases={6 + k: k for k in range(4)}, compiler_params=_cparams(),
    )(_me_operand(me), w, slots, src, m, v, *prev)


WEIGHT_NAMES = ("c_ctx", "norm_g", "w_mod", "b_mod", "ffn_w_gate", "ffn_w_up", "ffn_w_down", "ab_w_in", "pool_w",
                "pool_scale", "q_norm_g", "w_uq", "kv_norm_g", "w_ukv", "ab_w_out", "conv_w_in", "conv_w",
                "conv_w_out", "final_norm_g")


def kernel(x, c, ctx, c_ctx, norm_g, w_mod, b_mod, ffn_w_gate, ffn_w_up, ffn_w_down, ab_w_in, pool_w, pool_scale, q_norm_g, w_uq, kv_norm_g, w_ukv, ab_w_out, conv_w_in, conv_w, conv_w_out, final_norm_g, loss_target, m_c_ctx, m_norm_g, m_w_mod, m_b_mod, m_ffn_w_gate, m_ffn_w_up, m_ffn_w_down, m_ab_w_in, m_pool_w, m_pool_scale, m_q_norm_g, m_w_uq, m_kv_norm_g, m_w_ukv, m_ab_w_out, m_conv_w_in, m_conv_w, m_conv_w_out, m_final_norm_g, v_c_ctx, v_norm_g, v_w_mod, v_b_mod, v_ffn_w_gate, v_ffn_w_up, v_ffn_w_down, v_ab_w_in, v_pool_w, v_pool_scale, v_q_norm_g, v_w_uq, v_kv_norm_g, v_w_ukv, v_ab_w_out, v_conv_w_in, v_conv_w, v_conv_w_out, v_final_norm_g):
    weights = (c_ctx, norm_g, w_mod, b_mod, ffn_w_gate, ffn_w_up, ffn_w_down, ab_w_in, pool_w, pool_scale, q_norm_g,
               w_uq, kv_norm_g, w_ukv, ab_w_out, conv_w_in, conv_w, conv_w_out, final_norm_g)
    moms = (m_c_ctx, m_norm_g, m_w_mod, m_b_mod, m_ffn_w_gate, m_ffn_w_up, m_ffn_w_down, m_ab_w_in, m_pool_w,
            m_pool_scale, m_q_norm_g, m_w_uq, m_kv_norm_g, m_w_ukv, m_ab_w_out, m_conv_w_in, m_conv_w, m_conv_w_out,
            m_final_norm_g)
    vels = (v_c_ctx, v_norm_g, v_w_mod, v_b_mod, v_ffn_w_gate, v_ffn_w_up, v_ffn_w_down, v_ab_w_in, v_pool_w,
            v_pool_scale, v_q_norm_g, v_w_uq, v_kv_norm_g, v_w_ukv, v_ab_w_out, v_conv_w_in, v_conv_w, v_conv_w_out,
            v_final_norm_g)
    me = 4 * lax.axis_index("x") + 2 * lax.axis_index("y") + lax.axis_index("c")
    n_lat, n_ctx = x.shape[1], ctx.shape[1]
    d = D_MODEL
    mod_cols = w_mod.shape[-1]
    ng_sh, cw_sh = norm_g.shape[-1], conv_w.shape[-1]

    def ffn_shards(i):
        return {f"gate_t{i}": ffn_w_gate[i // 2, i % 2].T, f"up_t{i}": ffn_w_up[i // 2, i % 2].T,
                f"down{i}": ffn_w_down[i // 2, i % 2]}

    local = {**ffn_shards(0), "in_t": ab_w_in[0].T, "uq": w_uq[0], "ukv_t": w_ukv[0].T, "ab_out": ab_w_out[0],
             **ffn_shards(1), **ffn_shards(2), "cin_t": conv_w_in[0].T, "c_out": conv_w_out[0], **ffn_shards(3)}
    ffn_groups = [[[f"gate_t{i}", f"up_t{i}"], [f"down{i}"]] for i in range(4)]
    groups = [*ffn_groups[0], ["in_t", "uq", "ukv_t", "ab_out"], *ffn_groups[1], *ffn_groups[2], ["cin_t", "c_out"],
              *ffn_groups[3]]
    feed = _Feed({nm: a.astype(BF16) for nm, a in local.items()}, groups, me)

    small = jnp.concatenate([c.reshape(-1), norm_g.reshape(-1), conv_w.reshape(-1)])
    small_n = -(-small.shape[0] // 1024) * 1024
    small = jnp.pad(small, (0, small_n - small.shape[0])).reshape(small_n // 128, 128)
    small_all = _exchange("gather_small", small, False).reshape(N_DEV, small_n)
    c_all = small_all[:, :d]
    o1 = d + 6 * ng_sh
    norm_g_full = small_all[:, d:o1].reshape(N_DEV, 2, 3, ng_sh).transpose(1, 2, 0, 3).reshape(2, 3, d)
    conv_w_full = small_all[:, o1:o1 + 3 * cw_sh].reshape(N_DEV, 3, cw_sh).transpose(1, 0, 2).reshape(3, d)

    cond = jnp.concatenate([c_all, jnp.broadcast_to(c_ctx[None, :], (N_DEV, d))], axis=0)
    sil, dsil = _silu_rows("mod_silu", cond)
    w_mod_b = w_mod.astype(BF16)
    b_sh = lax.dynamic_slice(b_mod, (0, me * mod_cols), (2, mod_cols))
    m_part = jnp.stack([_mm(f"mod_fwd{l}", [(sil, w_mod_b[l])], "nn", F32, 16, 384, bias=b_sh[l:l + 1])
                        for l in range(2)], axis=1)
    m_all = _exchange("gather_mod", m_part.reshape(-1, 128), False).reshape(N_DEV, 2 * N_DEV, 2, mod_cols)
    m_mine = lax.dynamic_index_in_dim(m_all, me, axis=1, keepdims=False)
    mod_h = m_mine.transpose(1, 0, 2).reshape(2, N_MOD, d)
    mod_g = m_all[:, N_DEV, 0, :].reshape(N_MOD, d)

    first = feed.start("first", [nm for grp in groups[:3] for nm in grp], m_all)
    feed.start("rest", [nm for grp in groups[3:] for nm in grp], first)

    sq_cols, ds0, g = _local_step(x[0], ctx[0], loss_target[0], mod_h, mod_g, norm_g_full, feed, pool_w[0],
                                  pool_scale, q_norm_g, kv_norm_g, conv_w_full, final_norm_g)
    grad_x = ds0[:n_lat]
    loss = lax.psum(0.5 * jnp.sum(sq_cols) / d, ("x", "y", "c"))
    w_of, m_of, v_of = (dict(zip(WEIGHT_NAMES, t)) for t in (weights, moms, vels))
    results = {}

    def update(nm, grad, view=lambda t: t):
        outs = _adamw(f"adamw_{nm}", view(w_of[nm]), grad.reshape(view(w_of[nm]).shape), view(m_of[nm]), view(v_of[nm]))
        results[nm] = tuple(view(t) for t in (grad.reshape(view(w_of[nm]).shape), *outs))

    def swap(t):
        return jnp.swapaxes(t, -1, -2)

    stacked = ("gate_t", "up_t", "down")
    early = feed.collect(["l1f1", "l1m", "l1f0", "l0f1", "l0m"], [ds0], stacked)
    update("ab_w_in", early["in_t"], swap)
    update("w_uq", early["uq"])
    update("w_ukv", early["ukv_t"].T)
    update("ab_w_out", early["ab_out"])
    update("conv_w_in", early["cin_t"].T)
    update("conv_w_out", early["c_out"])
    ffn = {}
    for nm, prefix, view in (("ffn_w_gate", "gate_t", swap), ("ffn_w_up", "up_t", swap),
                             ("ffn_w_down", "down", lambda t: t)):
        w4, m4, v4 = (view(t).reshape((4,) + view(t).shape[-2:]) for t in (w_of[nm], m_of[nm], v_of[nm]))
        prev = None
        for i in (3, 2, 1):
            prev = _adamw_part(f"adamw_{nm}{i}", i, w4, early[f"{prefix}{i}"], me, m4, v4, prev)
        ffn[nm] = (prefix, view, w4, m4, v4, prev)
    done_early = [results[nm][1] for nm in results] + [state[5][1] for state in ffn.values()]
    late = feed.collect(["l0f0"], done_early, stacked)
    for nm, (prefix, view, w4, m4, v4, prev) in ffn.items():
        outs = _adamw_part(f"adamw_{nm}0", 0, w4, late[f"{prefix}0"], me, m4, v4, prev)
        results[nm] = tuple(view(t.reshape(view(w_of[nm]).shape)) for t in outs)

    dm = jnp.stack([g["mod_h"], jnp.stack([g["mod_g"], jnp.zeros_like(g["mod_g"])])])
    dm_all = _exchange("gather_dmod", dm.reshape(-1, 128), False, results["ffn_w_down"][1]).reshape(N_DEV, 2, 2, N_MOD * d)
    grad_b_mod = _sum_rows("dmod_bias", dm_all.reshape(2 * N_DEV, 2 * N_MOD * d)).reshape(2, N_MOD * d)
    dm_sh = lax.dynamic_slice(dm_all, (0, 0, 0, me * mod_cols), (N_DEV, 2, 2, mod_cols))
    gw_mod, cctx_parts = [], []
    for l in range(2):
        dm_l = dm_sh[:, :, l, :].transpose(1, 0, 2).reshape(2 * N_DEV, mod_cols).astype(BF16)
        gw_mod.append(_mm(f"mod_dw{l}", [(sil, dm_l)], "tn", F32, 512, 384))
        dm_ctx = jnp.concatenate([dm_l[N_DEV:], jnp.zeros((N_DEV, mod_cols), BF16)], axis=0)
        cctx_parts.append(_mm(f"mod_dcond{l}", [(dm_ctx, w_mod_b[l])], "nt", F32, 16, 512))
    cctx_part = _sum_rows("mod_dcond_sum", jnp.concatenate(cctx_parts, axis=0))
    update("w_mod", jnp.stack(gw_mod))
    update("b_mod", grad_b_mod)

    small_g = jnp.concatenate([g["pool_w"].reshape(-1), g["pool_scale"].reshape(-1), g["q_norm_g"].reshape(-1),
                               g["kv_norm_g"].reshape(-1), g["final_norm_g"].reshape(-1), g["norm_g"].reshape(-1),
                               g["conv_w"].reshape(-1), cctx_part.reshape(-1)])
    sizes = [pool_w.size, pool_scale.size, q_norm_g.size, kv_norm_g.size, d, 6 * d, 3 * d, d]
    sg_n = -(-small_g.shape[0] // 1024) * 1024
    small_g = jnp.pad(small_g, (0, sg_n - small_g.shape[0]))
    sg_all = _exchange("gather_small_grads", small_g.reshape(-1, 128), False).reshape(N_DEV, sg_n)
    scale_vec = jnp.concatenate([jnp.ones((1, sum(sizes[:-1])), F32), dsil[N_DEV:N_DEV + 1],
                                 jnp.ones((1, sg_n - sum(sizes)), F32)], axis=1)
    sg = _sum_rows("small_grads_sum", sg_all, scale_vec)[0]
    cuts, pos = [], 0
    for sz in sizes:
        cuts.append(sg[pos:pos + sz])
        pos += sz
    g_pool_w, g_pool_scale, g_q_norm, g_kv_norm, g_final, g_norm_full, g_conv_full, g_c_ctx = cuts
    update("c_ctx", g_c_ctx)
    update("norm_g", lax.dynamic_slice(g_norm_full.reshape(2, 3, d), (0, 0, me * ng_sh), (2, 3, ng_sh)))
    update("conv_w", lax.dynamic_slice(g_conv_full.reshape(3, d), (0, me * cw_sh), (3, cw_sh)))
    update("pool_w", g_pool_w)
    update("pool_scale", g_pool_scale)
    update("q_norm_g", g_q_norm)
    update("kv_norm_g", g_kv_norm)
    update("final_norm_g", g_final)
    outs = [results[nm] for nm in WEIGHT_NAMES]
    return (loss, grad_x[None], *[o[0] for o in outs], *[o[1] for o in outs], *[o[2] for o in outs],
            *[o[3] for o in outs])
```

```python
import functools
import math

import jax
import jax.numpy as jnp
import numpy as np
from jax import lax
from jax.experimental import pallas as pl
from jax.experimental.pallas import tpu as pltpu

F32 = jnp.float32
BF16 = jnp.bfloat16
MESH = pl.DeviceIdType.MESH
SDS = jax.ShapeDtypeStruct

N_DEV = 8
D_MODEL = 1024
N_MOD = 9
D_FF = 2816
POOL_WINDOWS = (2, 4, 8, 16)
POOL_DIM = 512
POOL_GROUP_DIM = 128
HEADS = 8
QK_NOPE = 64
QK_ROPE = 32
QK_HEAD = QK_NOPE + QK_ROPE
V_HEAD = 64
Q_RANK = 768
KV_RANK = 256
GRID_W = 64
ROPE_THETA = 10000.0
RMS_EPS = 1e-6
ATTN_SCALE = 1.0 / math.sqrt(QK_HEAD)
HEAD_PAD = 128
POOL_PAD = 16
PA_POOL, PA_CQ, PA_KV = 0, 768, 1536
PA_KV_W = 384
PA_W = PA_KV + PA_KV_W

ADAM_LR, ADAM_B1, ADAM_B2, ADAM_EPS, ADAM_WD, ADAM_STEP = 0.001, 0.9, 0.999, 1e-08, 0.01, 10

VMEM_LIMIT_BYTES = 56 * 1024 * 1024

NN = ((1,), (0,))
NT = ((1,), (1,))
TN = ((0,), (0,))


def _cparams():
    return pltpu.CompilerParams(vmem_limit_bytes=VMEM_LIMIT_BYTES)


def _dot(a, b, dims):
    return lax.dot_general(a, b, (dims, ((), ())), preferred_element_type=F32)


def _tile(n, cap, mult=8):
    t = (min(cap, n) // mult) * mult
    while t >= mult:
        if n % t == 0:
            return t
        t -= mult
    return n


def _colsum(x):
    return jnp.sum(x, axis=0, keepdims=True)


def _rms(x):
    r = lax.rsqrt(jnp.mean(x * x, axis=-1, keepdims=True) + RMS_EPS)
    return x * r, r


def _rms_bwd(n, r, dn):
    return r * (dn - n * jnp.mean(dn * n, axis=-1, keepdims=True))


def _rowwise(name, fn, t_rows, tm, n_lat, rows, vecs, outs, accs):
    nt = t_rows // tm
    nlt = n_lat // tm
    n_groups = 2 if nlt < nt else 1

    def grp(i):
        return jnp.where(i >= nlt, 1, 0) if n_groups == 2 else 0

    in_specs = [pl.BlockSpec((tm, w), functools.partial(lambda i, cb: (i, cb), cb=cb)) for (_, w, cb) in rows]
    in_specs += [pl.BlockSpec((1,) + v.shape[1:], lambda i: (grp(i), 0, 0)) for v in vecs]
    out_specs = [pl.BlockSpec((tm, w), lambda i: (i, 0)) for (w, _) in outs]
    out_specs += [pl.BlockSpec((1, 1, w), lambda i: (grp(i), 0, 0)) for w in accs]
    out_shape = [SDS((t_rows, w), dt) for (w, dt) in outs] + [SDS((n_groups, 1, w), F32) for w in accs]
    n_r, n_v, n_o = len(rows), len(vecs), len(outs)

    def body(*refs):
        row_vals = [r[...] for r in refs[:n_r]]
        vec_vals = [v[0] for v in refs[n_r:n_r + n_v]]
        out_refs = refs[n_r + n_v:n_r + n_v + n_o]
        acc_refs = refs[n_r + n_v + n_o:]
        out_vals, acc_vals = fn(row_vals, vec_vals)
        for o_ref, o in zip(out_refs, out_vals):
            o_ref[...] = o.astype(o_ref.dtype)
        if acc_refs:
            i = pl.program_id(0)
            first = (i == 0) | (i == nlt) if n_groups == 2 else i == 0

            @pl.when(first)
            def _():
                for a_ref, a in zip(acc_refs, acc_vals):
                    a_ref[0] = a

            @pl.when(jnp.logical_not(first))
            def _():
                for a_ref, a in zip(acc_refs, acc_vals):
                    a_ref[0] += a

    res = pl.pallas_call(
        body, name=name, grid=(nt,), in_specs=in_specs, out_specs=out_specs, out_shape=out_shape,
        compiler_params=_cparams(),
    )(*[r[0] for r in rows], *vecs)
    return res[:n_o], res[n_o:]


RESIDENT_BYTES = 12 * 1024 * 1024


def _mm(name, pairs, mode, out_dtype, tm_cap=256, tn_cap=512, bias=None):
    a0, b0 = pairs[0]
    if mode == "nn":
        m, n, dims = a0.shape[0], b0.shape[1], NN
    elif mode == "nt":
        m, n, dims = a0.shape[0], b0.shape[0], NT
    else:
        m, n, dims = a0.shape[1], b0.shape[1], TN
    b_bytes = sum(b.size * b.dtype.itemsize for _, b in pairs)
    tn = n if b_bytes <= RESIDENT_BYTES else _tile(n, tn_cap, 128)
    tm = _tile(m, tm_cap, 128 if mode == "tn" else 16)

    def a_spec(a):
        if mode == "tn":
            return pl.BlockSpec((a.shape[0], tm), lambda i, j: (0, i))
        return pl.BlockSpec((tm, a.shape[1]), lambda i, j: (i, 0))

    def b_spec(b):
        if mode == "nt":
            return pl.BlockSpec((tn, b.shape[1]), lambda i, j: (j, 0))
        return pl.BlockSpec((b.shape[0], tn), lambda i, j: (0, j))

    in_specs, flat = [], []
    for a, b in pairs:
        in_specs += [a_spec(a), b_spec(b)]
        flat += [a, b]
    if bias is not None:
        in_specs.append(pl.BlockSpec((1, tn), lambda i, j: (0, j)))
        flat.append(bias)
    n_pairs = len(pairs)

    def body(*refs):
        acc = None
        for p in range(n_pairs):
            t = _dot(refs[2 * p][...], refs[2 * p + 1][...], dims)
            acc = t if acc is None else acc + t
        if bias is not None:
            acc = acc + refs[2 * n_pairs][...]
        refs[-1][...] = acc.astype(refs[-1].dtype)

    return pl.pallas_call(
        body, name=name, grid=(m // tm, n // tn), in_specs=in_specs,
        out_specs=pl.BlockSpec((tm, tn), lambda i, j: (i, j)),
        out_shape=SDS((m, n), out_dtype), compiler_params=_cparams(),
    )(*flat)


def _mm_resid(name, a, b, s, mg, k, coef, n_lat):
    t_rows, n = a.shape[0], b.shape[1]
    tm = _tile(math.gcd(n_lat, t_rows), 256, 16)
    nlt = n_lat // tm
    n_groups = 2 if nlt < t_rows // tm else 1

    def grp(i):
        return jnp.where(i >= nlt, 1, 0) if n_groups == 2 else 0

    def body(a_ref, b_ref, s_ref, mg_ref, so_ref, o_ref):
        o = _dot(a_ref[...], b_ref[...], NN)
        gate = mg_ref[0, 3 * k + 2:3 * k + 3, :]
        o_ref[...] = o.astype(BF16)
        so_ref[...] = s_ref[...] + (coef * gate) * o

    row = pl.BlockSpec((tm, n), lambda i: (i, 0))
    return pl.pallas_call(
        body, name=name, grid=(t_rows // tm,),
        in_specs=[pl.BlockSpec((tm, a.shape[1]), lambda i: (i, 0)), pl.BlockSpec(b.shape, lambda i: (0, 0)), row,
                  pl.BlockSpec((1, mg.shape[1], n), lambda i: (grp(i), 0, 0))],
        out_specs=[row, row], out_shape=[SDS((t_rows, n), F32), SDS((t_rows, n), BF16)], compiler_params=_cparams(),
    )(a, b, s, mg)


def _dw_pair(name, a1, a2, b):
    kk, m = a1.shape
    n = b.shape[1]
    tm = _tile(m, 256, 128)

    def body(a1_ref, a2_ref, b_ref, o1_ref, o2_ref):
        bb = b_ref[...]
        o1_ref[...] = _dot(a1_ref[...], bb, TN).astype(BF16)
        o2_ref[...] = _dot(a2_ref[...], bb, TN).astype(BF16)

    col = pl.BlockSpec((kk, tm), lambda i: (0, i))
    out = pl.BlockSpec((tm, n), lambda i: (i, 0))
    return pl.pallas_call(
        body, name=name, grid=(m // tm,), in_specs=[col, col, pl.BlockSpec(b.shape, lambda i: (0, 0))],
        out_specs=[out, out], out_shape=[SDS((m, n), BF16)] * 2, compiler_params=_cparams(),
    )(a1, a2, b)


def _groups(t_rows, tm, n_lat):
    nlt = n_lat // tm
    if nlt < t_rows // tm:
        return 2, (lambda i: jnp.where(i >= nlt, 1, 0)), (lambda i: (i == 0) | (i == nlt))
    return 1, (lambda i: 0), (lambda i: i == 0)


def _accumulate(acc_refs, vals, first):
    @pl.when(first)
    def _():
        for r, v in zip(acc_refs, vals):
            r[0] = v

    @pl.when(jnp.logical_not(first))
    def _():
        for r, v in zip(acc_refs, vals):
            r[0] += v


def _adaln_math(s, m, k):
    n, _ = _rms(s)
    return (n * m[9 + k:10 + k]) * (1.0 + m[3 * k + 1:3 * k + 2]) + m[3 * k:3 * k + 1]


def _ffn_up(name, s, mg, k, n_lat, wg_t, wu_t):
    t_rows, f = s.shape[0], wg_t.shape[0]
    tm = _row_tm(t_rows, n_lat)
    _, grp, _ = _groups(t_rows, tm, n_lat)

    def body(s_ref, mg_ref, wg_ref, wu_ref, u_ref, a_ref, b_ref, h_ref):
        uu = _adaln_math(s_ref[...], mg_ref[0], k).astype(BF16)
        u_ref[...] = uu
        a = _dot(uu, wg_ref[...], NT)
        b = _dot(uu, wu_ref[...], NT)
        sg = jax.nn.sigmoid(a)
        act = a * sg
        a_ref[...] = (b * (sg * (1.0 + a * (1.0 - sg)))).astype(BF16)
        b_ref[...] = act.astype(BF16)
        h_ref[...] = (act * b).astype(BF16)

    w_spec = pl.BlockSpec(wg_t.shape, lambda i: (0, 0))
    o_spec = pl.BlockSpec((tm, f), lambda i: (i, 0))
    row = pl.BlockSpec((tm, s.shape[1]), lambda i: (i, 0))
    return pl.pallas_call(
        body, name=name, grid=(t_rows // tm,),
        in_specs=[row, pl.BlockSpec((1,) + mg.shape[1:], lambda i: (grp(i), 0, 0)), w_spec, w_spec],
        out_specs=[row, o_spec, o_spec, o_spec],
        out_shape=[SDS(s.shape, BF16)] + [SDS((t_rows, f), BF16)] * 3, compiler_params=_cparams(),
    )(s, mg, wg_t, wu_t)


def _ffn_dact(name, ds_out, o, mg, k, coef, n_lat, wd, a, b):
    t_rows, f = ds_out.shape[0], wd.shape[0]
    tm = _row_tm(t_rows, n_lat)
    n_groups, grp, first = _groups(t_rows, tm, n_lat)
    d = ds_out.shape[1]

    def body(ds_ref, o_ref, mg_ref, wd_ref, a_ref, b_ref, do_ref, da_ref, db_ref, dg_ref):
        dd = coef * ds_ref[...]
        do = (dd * mg_ref[0, 3 * k + 2:3 * k + 3, :]).astype(BF16)
        do_ref[...] = do
        _accumulate([dg_ref], [_colsum(dd * o_ref[...].astype(F32))], first(pl.program_id(0)))
        dh = _dot(do, wd_ref[...], NT)
        da_ref[...] = (dh * a_ref[...].astype(F32)).astype(BF16)
        db_ref[...] = (dh * b_ref[...].astype(F32)).astype(BF16)

    row = pl.BlockSpec((tm, d), lambda i: (i, 0))
    t_spec = pl.BlockSpec((tm, f), lambda i: (i, 0))
    return pl.pallas_call(
        body, name=name, grid=(t_rows // tm,),
        in_specs=[row, row, pl.BlockSpec((1,) + mg.shape[1:], lambda i: (grp(i), 0, 0)),
                  pl.BlockSpec(wd.shape, lambda i: (0, 0)), t_spec, t_spec],
        out_specs=[row, t_spec, t_spec, pl.BlockSpec((1, 1, d), lambda i: (grp(i), 0, 0))],
        out_shape=[SDS((t_rows, d), BF16), SDS((t_rows, f), BF16), SDS((t_rows, f), BF16), SDS((n_groups, 1, d), F32)],
        compiler_params=_cparams(),
    )(ds_out, o, mg, wd, a, b)


def _du_adaln(name, pairs, s, ds_out, mg, k, n_lat, after, out_rows=None):
    t_rows, d = s.shape
    tm = _row_tm(t_rows, n_lat)
    n_groups, grp, first = _groups(t_rows, tm, n_lat)
    n_pairs = len(pairs)
    nt, n_ds, n_out = t_rows // tm, ds_out.shape[0] // tm, (out_rows or t_rows) // tm

    def body(*refs):
        s_ref, ds_ref, mg_ref, z_ref, out_ref, dsh_ref, dsc_ref, dgn_ref = refs[2 * n_pairs:]
        i = pl.program_id(0)
        d_u = z_ref[...]
        for p in range(n_pairs):
            d_u = d_u + _dot(refs[p][...], refs[n_pairs + p][...], NN)
        m = mg_ref[0]
        gain, scale = m[9 + k:10 + k], m[3 * k + 1:3 * k + 2]
        n, r = _rms(s_ref[...])
        dxn = d_u * (1.0 + scale)
        ds_in = _rms_bwd(n, r, dxn * gain)
        ds_in = ds_in + (ds_ref[...] if n_ds == nt else jnp.where(i < n_ds, ds_ref[...], 0.0))
        if n_out == nt:
            out_ref[...] = ds_in
        else:
            @pl.when(i < n_out)
            def _():
                out_ref[...] = ds_in
        _accumulate([dsh_ref, dsc_ref, dgn_ref], [_colsum(d_u), _colsum(d_u * (n * gain)), _colsum(dxn * n)], first(i))

    row = pl.BlockSpec((tm, d), lambda i: (i, 0))
    acc = pl.BlockSpec((1, 1, d), lambda i: (grp(i), 0, 0))
    res = pl.pallas_call(
        body, name=name, grid=(t_rows // tm,),
        in_specs=[pl.BlockSpec((tm, a.shape[1]), lambda i: (i, 0)) for a, _ in pairs]
        + [pl.BlockSpec(w.shape, lambda i: (0, 0)) for _, w in pairs]
        + [row, pl.BlockSpec((tm, d), lambda i: (jnp.minimum(i, n_ds - 1), 0)),
           pl.BlockSpec((1,) + mg.shape[1:], lambda i: (grp(i), 0, 0)), pl.BlockSpec((1, d), lambda i: (0, 0))],
        out_specs=[pl.BlockSpec((tm, d), lambda i: (jnp.minimum(i, n_out - 1), 0)), acc, acc, acc],
        out_shape=[SDS((n_out * tm, d), F32)] + [SDS((n_groups, 1, d), F32)] * 3, compiler_params=_cparams(),
    )(*[a for a, _ in pairs], *[w for _, w in pairs], s, ds_out, mg, after)
    return res[0], res[1:]


def _adaln_mm(name, s, mg, k, n_lat, w_t):
    rows, d = s.shape
    tm = _row_tm(rows, n_lat)
    _, grp, _ = _groups(rows, tm, n_lat)
    n = w_t.shape[0]

    def body(s_ref, mg_ref, w_ref, u_ref, y_ref):
        uu = _adaln_math(s_ref[...], mg_ref[0], k).astype(BF16)
        u_ref[...] = uu
        y_ref[...] = _dot(uu, w_ref[...], NT)

    row = pl.BlockSpec((tm, d), lambda i: (i, 0))
    return pl.pallas_call(
        body, name=name, grid=(rows // tm,),
        in_specs=[row, pl.BlockSpec((1,) + mg.shape[1:], lambda i: (grp(i), 0, 0)), pl.BlockSpec(w_t.shape, lambda i: (0, 0))],
        out_specs=[row, pl.BlockSpec((tm, n), lambda i: (i, 0))],
        out_shape=[SDS((rows, d), BF16), SDS((rows, n), F32)], compiler_params=_cparams(),
    )(s, mg, w_t)


def _gate_mm(name, ds_out, o, mg, k, coef, n_lat, w):
    t_rows, d = ds_out.shape
    tm = _row_tm(t_rows, n_lat)
    n_groups, grp, first = _groups(t_rows, tm, n_lat)
    n = w.shape[0]

    def body(ds_ref, o_ref, mg_ref, w_ref, do_ref, y_ref, dg_ref):
        dd = coef * ds_ref[...]
        do = (dd * mg_ref[0, 3 * k + 2:3 * k + 3, :]).astype(BF16)
        do_ref[...] = do
        _accumulate([dg_ref], [_colsum(dd * o_ref[...].astype(F32))], first(pl.program_id(0)))
        y_ref[...] = _dot(do, w_ref[...], NT)

    row = pl.BlockSpec((tm, d), lambda i: (i, 0))
    return pl.pallas_call(
        body, name=name, grid=(t_rows // tm,),
        in_specs=[row, row, pl.BlockSpec((1,) + mg.shape[1:], lambda i: (grp(i), 0, 0)), pl.BlockSpec(w.shape, lambda i: (0, 0))],
        out_specs=[row, pl.BlockSpec((tm, n), lambda i: (i, 0)), pl.BlockSpec((1, 1, d), lambda i: (grp(i), 0, 0))],
        out_shape=[SDS((t_rows, d), BF16), SDS((t_rows, n), F32), SDS((n_groups, 1, d), F32)],
        compiler_params=_cparams(),
    )(ds_out, o, mg, w)


def _row_tm(t_rows, n_lat):
    return _tile(math.gcd(t_rows, n_lat), 256, 16)


def _rmsnorm_fwd(name, x, width, colblk, gain, t_rows):
    def fn(rv, vv):
        n, _ = _rms(rv[0])
        return [n * vv[0]], []

    (y,), _ = _rowwise(name, fn, t_rows, _tile(t_rows, 256, 16), t_rows, [(x, width, colblk)],
                       [gain.reshape(1, 1, width)], [(width, BF16)], [])
    return y


def _rmsnorm_bwd(name, x, width, colblk, dy, gain, t_rows, out_dtype=F32):
    def fn(rv, vv):
        n, r = _rms(rv[0])
        return [_rms_bwd(n, r, rv[1] * vv[0])], [_colsum(rv[1] * n)]

    (dx,), (dgain,) = _rowwise(name, fn, t_rows, _tile(t_rows, 256, 16), t_rows,
                               [(x, width, colblk), (dy, width, 0)], [gain.reshape(1, 1, width)],
                               [(width, out_dtype)], [width])
    return dx, dgain


def _final_loss(name, h, target, gain):
    t_rows = h.shape[0]
    inv_d = 1.0 / D_MODEL

    def fn(rv, vv):
        g = vv[0]
        n, r = _rms(rv[0])
        e = n * g - rv[1]
        dy = e * inv_d
        return [_rms_bwd(n, r, dy * g)], [_colsum(e * e), _colsum(dy * n)]

    (dh,), (sq, dgain) = _rowwise(name, fn, t_rows, _tile(t_rows, 256, 16), t_rows,
                                  [(h, D_MODEL, 0), (target, D_MODEL, 0)], [gain.reshape(1, 1, D_MODEL)],
                                  [(D_MODEL, F32)], [D_MODEL, D_MODEL])
    return dh, sq, dgain


def _rope(name, z, width, colblk, cos, sin, perm, backward, out_dtype):
    t_rows = cos.shape[0]

    def body(z_ref, c_ref, s_ref, p_ref, o_ref):
        zz = z_ref[...]
        pre = zz * s_ref[...] if backward else zz
        hi = pre.astype(BF16)
        lo = (pre - hi.astype(F32)).astype(BF16)
        rot = _dot(hi, p_ref[...], NN) + _dot(lo, p_ref[...], NN)
        if not backward:
            rot = rot * s_ref[...]
        o_ref[...] = (zz * c_ref[...] + rot).astype(o_ref.dtype)

    tm = _tile(t_rows, 256, 16)
    t_spec = pl.BlockSpec((tm, width), lambda i: (i, 0))
    return pl.pallas_call(
        body, name=name, grid=(t_rows // tm,),
        in_specs=[pl.BlockSpec((tm, width), lambda i: (i, colblk)), t_spec, t_spec,
                  pl.BlockSpec((width, width), lambda i: (0, 0))],
        out_specs=t_spec, out_shape=SDS((t_rows, width), out_dtype), compiler_params=_cparams(),
    )(z, cos, sin, perm)


def _window_sum(x, w, transposed):
    n_rows = x.shape[0]
    zeros = jnp.zeros((POOL_PAD, x.shape[1]), F32)
    y = jnp.concatenate([zeros, x, zeros], axis=0)
    total = n_rows + 2 * POOL_PAD
    if transposed:
        y = y + pltpu.roll(y, total - 1, 0)
    else:
        y = y + pltpu.roll(y, 1, 0)
    step = 1
    while 2 * step < w:
        y = pltpu.roll(y, step, 0) + pltpu.roll(y, total - step, 0)
        step *= 2
    return y[POOL_PAD:POOL_PAD + n_rows]


def _window_count(n_rows, w):
    t = lax.broadcasted_iota(jnp.int32, (n_rows, 1), 0)
    lo = jnp.maximum(t - w // 2, 0)
    hi = jnp.minimum(t + (w - w // 2 - 1), n_rows - 1)
    return (hi - lo + 1).astype(F32)


def _pool_fwd(name, proj, n_rows, w_grp, scale):
    def body(x_ref, w_ref, sc_ref, y_ref, p_ref):
        for g, w in enumerate(POOL_WINDOWS):
            cols = slice(g * POOL_GROUP_DIM, (g + 1) * POOL_GROUP_DIM)
            x = x_ref[:, cols]
            p = _window_sum(x, w, False) * (1.0 / _window_count(n_rows, w)) - x
            pb = p.astype(BF16)
            p_ref[:, cols] = pb
            y_ref[:, cols] = (_dot(pb, w_ref[g], NN) * sc_ref[:, cols]).astype(BF16)

    blk = pl.BlockSpec((n_rows, POOL_DIM), lambda i: (0, 0))
    return pl.pallas_call(
        body, name=name, grid=(1,),
        in_specs=[blk, pl.BlockSpec(w_grp.shape, lambda i: (0, 0, 0)), pl.BlockSpec((1, POOL_DIM), lambda i: (0, 0))],
        out_specs=[blk, blk], out_shape=[SDS((n_rows, POOL_DIM), BF16)] * 2, compiler_params=_cparams(),
    )(proj, w_grp, scale)


def _pool_bwd(name, dcat, n_rows, p, w_grp, scale):
    def body(dy_ref, p_ref, w_ref, sc_ref, dx_ref, dw_ref, dsc_ref):
        for g, w in enumerate(POOL_WINDOWS):
            cols = slice(g * POOL_GROUP_DIM, (g + 1) * POOL_GROUP_DIM)
            dy = dy_ref[:, cols]
            pb = p_ref[:, cols]
            pw = _dot(pb, w_ref[g], NN)
            dsc_ref[:, cols] = _colsum(dy * pw)
            dpw = (dy * sc_ref[:, cols]).astype(BF16)
            dw_ref[g] = _dot(pb, dpw, TN)
            dp = _dot(dpw, w_ref[g], NT)
            dx_ref[:, cols] = (_window_sum(dp * (1.0 / _window_count(n_rows, w)), w, True) - dp).astype(BF16)

    blk = pl.BlockSpec((n_rows, POOL_DIM), lambda i: (0, 0))
    w_spec = pl.BlockSpec(w_grp.shape, lambda i: (0, 0, 0))
    v_spec = pl.BlockSpec((1, POOL_DIM), lambda i: (0, 0))
    return pl.pallas_call(
        body, name=name, grid=(1,), in_specs=[blk, blk, w_spec, v_spec], out_specs=[blk, w_spec, v_spec],
        out_shape=[SDS((n_rows, POOL_DIM), BF16), SDS(w_grp.shape, F32), SDS((1, POOL_DIM), F32)],
        compiler_params=_cparams(),
    )(dcat, p, w_grp, scale)


def _attn_fwd(name, q, k, v):
    h, n_q, _ = q.shape
    n_k = k.shape[1]
    tq = _tile(n_q, 256, 16)

    def body(q_ref, k_ref, v_ref, o_ref, lse_ref):
        s = _dot(q_ref[...], k_ref[...], NT) * ATTN_SCALE
        m = jnp.max(s, axis=-1, keepdims=True)
        e = jnp.exp(s - m)
        l = jnp.sum(e, axis=-1, keepdims=True)
        p = (e * (1.0 / l)).astype(BF16)
        o_ref[...] = _dot(p, v_ref[...], NN).astype(BF16)
        lse_ref[...] = m + jnp.log(l)

    return pl.pallas_call(
        body, name=name, grid=(h, n_q // tq),
        in_specs=[pl.BlockSpec((None, tq, HEAD_PAD), lambda hh, i: (hh, i, 0)),
                  pl.BlockSpec((None, n_k, HEAD_PAD), lambda hh, i: (hh, 0, 0)),
                  pl.BlockSpec((None, n_k, V_HEAD), lambda hh, i: (hh, 0, 0))],
        out_specs=[pl.BlockSpec((None, tq, V_HEAD), lambda hh, i: (hh, i, 0)),
                   pl.BlockSpec((None, tq, 1), lambda hh, i: (hh, i, 0))],
        out_shape=[SDS((h, n_q, V_HEAD), BF16), SDS((h, n_q, 1), F32)], compiler_params=_cparams(),
    )(q, k, v)


def _attn_bwd(name, q, k, v, o, lse, do):
    h, n_q, _ = q.shape
    n_k = k.shape[1]
    tq = _tile(n_q, 256, 16)

    def body(q_ref, k_ref, v_ref, o_ref, lse_ref, do_ref, dq_ref, dk_ref, dv_ref):
        i = pl.program_id(1)
        qq, kk, dd = q_ref[...], k_ref[...], do_ref[...]
        s = _dot(qq, kk, NT) * ATTN_SCALE
        p = jnp.exp(s - lse_ref[...])
        dp = _dot(dd, v_ref[...], NT)
        delta = jnp.sum(dd.astype(F32) * o_ref[...].astype(F32), axis=-1, keepdims=True)
        ds = (p * (dp - delta) * ATTN_SCALE).astype(BF16)
        dq_ref[...] = _dot(ds, kk, NN)
        dk = _dot(ds, qq, TN)
        dv = _dot(p.astype(BF16), dd, TN)

        @pl.when(i == 0)
        def _():
            dk_ref[...] = dk
            dv_ref[...] = dv

        @pl.when(i > 0)
        def _():
            dk_ref[...] += dk
            dv_ref[...] += dv

    q_spec = pl.BlockSpec((None, tq, HEAD_PAD), lambda hh, i: (hh, i, 0))
    k_spec = pl.BlockSpec((None, n_k, HEAD_PAD), lambda hh, i: (hh, 0, 0))
    v_spec = pl.BlockSpec((None, n_k, V_HEAD), lambda hh, i: (hh, 0, 0))
    o_spec = pl.BlockSpec((None, tq, V_HEAD), lambda hh, i: (hh, i, 0))
    return pl.pallas_call(
        body, name=name, grid=(h, n_q // tq),
        in_specs=[q_spec, k_spec, v_spec, o_spec, pl.BlockSpec((None, tq, 1), lambda hh, i: (hh, i, 0)), o_spec],
        out_specs=[q_spec, k_spec, v_spec],
        out_shape=[SDS((h, n_q, HEAD_PAD), F32), SDS((h, n_k, HEAD_PAD), F32), SDS((h, n_k, V_HEAD), F32)],
        compiler_params=_cparams(),
    )(q, k, v, o, lse, do)


CONV_COLS = 256


def _shift_rows(x, d):
    n_rows = x.shape[0]
    t = lax.broadcasted_iota(jnp.int32, (n_rows, 1), 0)
    if d > 0:
        return jnp.where(t >= d, pltpu.roll(x, d, 0), 0.0)
    return jnp.where(t < n_rows + d, pltpu.roll(x, n_rows + d, 0), 0.0)


def _conv_fwd(name, z3, conv_w):
    n_rows = z3.shape[0]
    nb = D_MODEL // CONV_COLS

    def body(b_ref, c_ref, v_ref, w_ref, y_ref):
        z = c_ref[...] * v_ref[...]
        zc = w_ref[0:1, :] * _shift_rows(z, 1) + w_ref[1:2, :] * z + w_ref[2:3, :] * _shift_rows(z, -1)
        y_ref[...] = (b_ref[...] * zc).astype(BF16)

    def part(k):
        return pl.BlockSpec((n_rows, CONV_COLS), lambda j: (0, k * nb + j))

    return pl.pallas_call(
        body, name=name, grid=(nb,),
        in_specs=[part(0), part(1), part(2), pl.BlockSpec((3, CONV_COLS), lambda j: (0, j))],
        out_specs=pl.BlockSpec((n_rows, CONV_COLS), lambda j: (0, j)),
        out_shape=SDS((n_rows, D_MODEL), BF16), compiler_params=_cparams(),
    )(z3, z3, z3, conv_w)


def _conv_bwd(name, dy, z3, conv_w):
    n_rows = z3.shape[0]
    nb = D_MODEL // CONV_COLS

    def body(dy_ref, b_ref, c_ref, v_ref, w_ref, db_ref, dc_ref, dv_ref, dw_ref):
        c, v, d_y = c_ref[...], v_ref[...], dy_ref[...]
        z = c * v
        z_dn, z_up = _shift_rows(z, 1), _shift_rows(z, -1)
        zc = w_ref[0:1, :] * z_dn + w_ref[1:2, :] * z + w_ref[2:3, :] * z_up
        db_ref[...] = (d_y * zc).astype(BF16)
        dzc = d_y * b_ref[...]
        dz = w_ref[0:1, :] * _shift_rows(dzc, -1) + w_ref[1:2, :] * dzc + w_ref[2:3, :] * _shift_rows(dzc, 1)
        dc_ref[...] = (dz * v).astype(BF16)
        dv_ref[...] = (dz * c).astype(BF16)
        dw_ref[0:1, :] = _colsum(dzc * z_dn)
        dw_ref[1:2, :] = _colsum(dzc * z)
        dw_ref[2:3, :] = _colsum(dzc * z_up)

    def part(k):
        return pl.BlockSpec((n_rows, CONV_COLS), lambda j: (0, k * nb + j))

    col = pl.BlockSpec((n_rows, CONV_COLS), lambda j: (0, j))
    w_spec = pl.BlockSpec((3, CONV_COLS), lambda j: (0, j))
    return pl.pallas_call(
        body, name=name, grid=(nb,), in_specs=[col, part(0), part(1), part(2), w_spec],
        out_specs=[col, col, col, w_spec],
        out_shape=[SDS((n_rows, D_MODEL), BF16)] * 3 + [SDS((3, D_MODEL), F32)], compiler_params=_cparams(),
    )(dy, z3, z3, z3, conv_w)


def _silu_rows(name, x):
    def body(x_ref, s_ref, d_ref):
        xx = x_ref[...]
        sg = jax.nn.sigmoid(xx)
        s_ref[...] = (xx * sg).astype(BF16)
        d_ref[...] = sg * (1.0 + xx * (1.0 - sg))

    return pl.pallas_call(body, name=name, out_shape=[SDS(x.shape, BF16), SDS(x.shape, F32)])(x)


def _sum_rows(name, x, scale=None):
    r, n = x.shape
    tn = _tile(n, 8192, 128)

    def body(*refs):
        acc = jnp.sum(refs[0][...].astype(F32), axis=0, keepdims=True)
        if scale is not None:
            acc = acc * refs[1][...]
        refs[-1][...] = acc

    in_specs = [pl.BlockSpec((r, tn), lambda j: (0, j))]
    args = [x]
    if scale is not None:
        in_specs.append(pl.BlockSpec((1, tn), lambda j: (0, j)))
        args.append(scale)
    return pl.pallas_call(body, name=name, grid=(n // tn,), in_specs=in_specs,
                          out_specs=pl.BlockSpec((1, tn), lambda j: (0, j)), out_shape=SDS((1, n), F32))(*args)


def _me_operand(me):
    return jnp.reshape(me, (1,)).astype(jnp.int32)


def _sum_slots(name, slots, src, me):
    n_slots, r, c = slots.shape
    tr = _tile(r, 432, 16)

    def body(me_ref, own_ref, x_ref, o_ref):
        acc = own_ref[...].astype(F32)
        for sl in range(n_slots):
            acc = acc + x_ref[sl].astype(F32)
        o_ref[...] = acc

    grid_spec = pltpu.PrefetchScalarGridSpec(
        num_scalar_prefetch=1, grid=(r // tr,),
        in_specs=[pl.BlockSpec((None, tr, c), lambda i, me_ref: (me_ref[0], i, 0)),
                  pl.BlockSpec((n_slots, tr, c), lambda i, me_ref: (0, i, 0))],
        out_specs=pl.BlockSpec((tr, c), lambda i, me_ref: (i, 0)))
    return pl.pallas_call(body, name=name, grid_spec=grid_spec, out_shape=SDS((r, c), F32),
                          compiler_params=_cparams())(_me_operand(me), src, slots)


def _adamw(name, w, g, m, v):
    shape = w.shape
    cols = shape[-1]
    rows = w.size // cols
    tr = _tile(rows, 512, 8)
    bc1 = 1.0 - ADAM_B1 ** ADAM_STEP
    bc2 = 1.0 - ADAM_B2 ** ADAM_STEP

    def body(w_ref, g_ref, m_ref, v_ref, d_ref, nm_ref, nv_ref):
        gg = g_ref[...]
        nm = ADAM_B1 * m_ref[...] + (1.0 - ADAM_B1) * gg
        nv = ADAM_B2 * v_ref[...] + (1.0 - ADAM_B2) * (gg * gg)
        nm_ref[...] = nm
        nv_ref[...] = nv
        d_ref[...] = -ADAM_LR * ((nm / bc1) / (jnp.sqrt(nv / bc2) + ADAM_EPS) + ADAM_WD * w_ref[...])

    spec = pl.BlockSpec((tr, cols), lambda i: (i, 0))
    outs = pl.pallas_call(body, name=name, grid=(rows // tr,), in_specs=[spec] * 4, out_specs=[spec] * 3,
                          out_shape=[SDS((rows, cols), F32)] * 3, compiler_params=_cparams())(
        w.reshape(rows, cols), g.reshape(rows, cols), m.reshape(rows, cols), v.reshape(rows, cols))
    return tuple(t.reshape(shape) for t in outs)


def _exchange(name, x, scatter, after=None):
    blk = x.shape[1:] if scatter else x.shape
    extra = [] if after is None else [after]

    def body(x_ref, *rest):
        out_ref, send_sems, recv_sems, local_sem = rest[len(extra):]
        mx, my, mc = lax.axis_index("x"), lax.axis_index("y"), lax.axis_index("c")
        me = 4 * mx + 2 * my + mc
        own = pltpu.make_async_copy(x_ref.at[me] if scatter else x_ref, out_ref.at[me], local_sem)
        own.start()
        copies = []
        for kk in range(1, N_DEV):
            px = jnp.bitwise_xor(mx, (kk >> 2) & 1)
            py = jnp.bitwise_xor(my, (kk >> 1) & 1)
            pc = jnp.bitwise_xor(mc, kk & 1)
            peer = 4 * px + 2 * py + pc
            send = pltpu.make_async_remote_copy(
                src_ref=x_ref.at[peer] if scatter else x_ref, dst_ref=out_ref.at[me],
                send_sem=send_sems.at[kk - 1], recv_sem=recv_sems.at[kk - 1],
                device_id=(px, py, pc), device_id_type=MESH)
            send.start()
            arrival = pltpu.make_async_remote_copy(
                src_ref=x_ref.at[peer] if scatter else x_ref, dst_ref=out_ref.at[peer],
                send_sem=send_sems.at[kk - 1], recv_sem=recv_sems.at[kk - 1],
                device_id=(px, py, pc), device_id_type=MESH)
            copies.append((send, arrival))
        for send, arrival in copies:
            arrival.wait_recv()
            send.wait_send()
        own.wait()

    return pl.pallas_call(
        body, name=name, out_shape=SDS((N_DEV,) + tuple(blk), x.dtype),
        in_specs=[pl.BlockSpec(memory_space=pl.ANY)] * (1 + len(extra)), out_specs=pl.BlockSpec(memory_space=pl.ANY),
        scratch_shapes=[pltpu.SemaphoreType.DMA((N_DEV - 1,)), pltpu.SemaphoreType.DMA((N_DEV - 1,)),
                        pltpu.SemaphoreType.DMA],
    )(x, *extra)


def _rope_perm(pre, reps, post):
    half = QK_ROPE // 4
    width = reps * (pre + QK_ROPE) + post
    p = np.zeros((width, width), np.float32)
    for rep in range(reps):
        s0 = rep * (pre + QK_ROPE) + pre
        for base in (s0, s0 + 2 * half):
            for i in range(half):
                p[base + half + i, base + i] = -1.0
                p[base + i, base + half + i] = 1.0
    return p


def _rope_tables(n_lat, t_rows, pre, reps, post):
    half = QK_ROPE // 4
    pos = jnp.arange(n_lat)
    freqs = jnp.power(ROPE_THETA, -jnp.arange(0, 2 * half, 2, dtype=F32) / (2 * half))
    ang_r = (pos // GRID_W).astype(F32)[:, None] * freqs
    ang_c = (pos % GRID_W).astype(F32)[:, None] * freqs
    ang = jnp.concatenate([ang_r, ang_r, ang_c, ang_c], axis=-1)

    def table(fn, plain):
        slot = jnp.concatenate([jnp.full((n_lat, pre), plain, F32), fn(ang)], axis=-1)
        t = jnp.concatenate([jnp.tile(slot, (1, reps)), jnp.full((n_lat, post), plain, F32)], axis=-1)
        return jnp.concatenate([t, jnp.full((t_rows - n_lat, t.shape[1]), plain, F32)], axis=0)

    return table(jnp.cos, 1.0), table(jnp.sin, 0.0)


def _ffn_half_fwd(tag, s, mg, k, feed, i, coef, n_lat):
    wg_t, wu_t = feed.weights(f"{tag}_up", [f"gate_t{i}", f"up_t{i}"], s)
    u, a, b, hid = _ffn_up(f"{tag}_up", s, mg, k, n_lat, wg_t, wu_t)
    (wd,) = feed.weights(f"{tag}_down", [f"down{i}"], hid)
    s_out, o = _mm_resid(f"{tag}_down", hid, wd, s, mg, k, coef, n_lat)
    return s_out, (s, u, a, b, hid, o, wg_t, wu_t, wd)


def _ffn_half_bwd(tag, ds_out, saved, mg, k, feed, i, coef, n_lat, out_rows=None):
    s, u, a, b, hid, o, wg_t, wu_t, wd = saved
    do, da, db, dgate = _ffn_dact(f"{tag}_dact", ds_out, o, mg, k, coef, n_lat, wd, a, b)
    dwd = _mm(f"{tag}_dwd", [(hid, do)], "tn", BF16)
    dwg_t, dwu_t = _dw_pair(f"{tag}_dwgu", da, db, u)
    token = feed.grads(tag, {f"down{i}": dwd, f"gate_t{i}": dwg_t, f"up_t{i}": dwu_t})
    ds_in, (dshift, dscale, dgain) = _du_adaln(f"{tag}_du", [(da, wg_t), (db, wu_t)], s, ds_out, mg, k, n_lat,
                                               _after(token), out_rows)
    return ds_in, dict(shift=dshift, scale=dscale, gate=dgate, gain=dgain)


def _after(token):
    return jnp.zeros((1, D_MODEL), F32) + token


def _mod_grad(parts, n_groups):
    rows = []
    zero = jnp.zeros((n_groups, 1, D_MODEL), F32)
    for k in range(3):
        for nm in ("shift", "scale", "gate"):
            t = parts[k].get(nm, zero)
            if t.shape[0] < n_groups:
                t = jnp.concatenate([t, jnp.zeros((n_groups - t.shape[0], 1, D_MODEL), F32)], axis=0)
            rows.append(t)
    return jnp.concatenate(rows, axis=1).reshape(n_groups, N_MOD * D_MODEL)


def _local_step(x, ctx, target, mod_h, mod_g, norm_g, feed, pool_w, pool_scale, q_norm_g, kv_norm_g, conv_w,
                final_norm_g):
    n_lat, n_ctx = x.shape[0], ctx.shape[0]
    t_all = n_lat + n_ctx
    mg0 = jnp.stack([jnp.concatenate([mod_h[0], norm_g[0]], axis=0), jnp.concatenate([mod_g, norm_g[0]], axis=0)])
    mg1 = jnp.concatenate([mod_h[1], norm_g[1]], axis=0)[None]

    s0 = jnp.concatenate([x, ctx], axis=0)
    s1, sv_f00 = _ffn_half_fwd("l0f0", s0, mg0, 0, feed, 0, 0.5, n_lat)

    w_in, w_uq, w_ukv_t, w_ab_out = feed.weights("l0m", ["in_t", "uq", "ukv_t", "ab_out"], s1)
    kv_rows = KV_RANK + QK_ROPE
    w_in_t = jnp.concatenate([
        w_in[:POOL_DIM], jnp.zeros((PA_CQ - POOL_DIM, D_MODEL), BF16), w_in[POOL_DIM:POOL_DIM + Q_RANK],
        w_in[POOL_DIM + Q_RANK:], jnp.zeros((PA_KV_W - kv_rows, D_MODEL), BF16)], axis=0)
    ua, proj = _adaln_mm("l0m_proj", s1, mg0, 1, n_lat, w_in_t)
    pool_y, pool_p = _pool_fwd("l0m_pool", proj, n_lat, pool_w.astype(BF16), pool_scale)
    nq = _rmsnorm_fwd("l0m_qnorm", proj, Q_RANK, PA_CQ // Q_RANK, q_norm_g, n_lat)
    q_lin = _mm("l0m_q", [(nq, w_uq)], "nn", F32, 512, 768)
    cos_q, sin_q = _rope_tables(n_lat, n_lat, QK_NOPE, HEADS, 0)
    perm_q = _rope_perm(QK_NOPE, HEADS, 0)
    q_rot = _rope("l0m_qrope", q_lin, Q_RANK, 0, cos_q, sin_q, jnp.asarray(perm_q, BF16), False, BF16)
    cos_k, sin_k = _rope_tables(n_lat, t_all, KV_RANK, 1, PA_KV_W - kv_rows)
    perm_k = _rope_perm(KV_RANK, 1, PA_KV_W - kv_rows)
    kvr = _rope("l0m_krope", proj, PA_KV_W, PA_KV // PA_KV_W, cos_k, sin_k, jnp.asarray(perm_k, BF16), False, F32)
    nkv = _rmsnorm_fwd("l0m_kvnorm", kvr, KV_RANK, 0, kv_norm_g, t_all)
    kv = _mm("l0m_kv", [(nkv, w_ukv_t)], "nt", BF16, 768, 512)
    qh = jnp.pad(q_rot.reshape(n_lat, HEADS, QK_HEAD), ((0, 0), (0, 0), (0, HEAD_PAD - QK_HEAD))).transpose(1, 0, 2)
    kvh = kv.reshape(t_all, HEADS, QK_NOPE + V_HEAD)
    k_rope = jnp.broadcast_to(kvr[:, None, KV_RANK:KV_RANK + QK_ROPE].astype(BF16), (t_all, HEADS, QK_ROPE))
    kh = jnp.concatenate([kvh[:, :, :QK_NOPE], k_rope, jnp.zeros((t_all, HEADS, HEAD_PAD - QK_HEAD), BF16)],
                         axis=-1).transpose(1, 0, 2)
    vh = kvh[:, :, QK_NOPE:].transpose(1, 0, 2)
    oh, lse = _attn_fwd("l0m_attn", qh, kh, vh)
    cat = jnp.concatenate([pool_y, oh.transpose(1, 0, 2).reshape(n_lat, HEADS * V_HEAD)], axis=-1)
    h1 = s1[:n_lat]
    h2, mix_o = _mm_resid("l0m_out", cat, w_ab_out, h1, mg0[:1], 1, 1.0, n_lat)

    h3, sv_f01 = _ffn_half_fwd("l0f1", h2, mg0[:1], 2, feed, 1, 0.5, n_lat)

    h4, sv_f10 = _ffn_half_fwd("l1f0", h3, mg1, 0, feed, 2, 0.5, n_lat)
    w_cin_t, w_c_out = feed.weights("l1m", ["cin_t", "c_out"], h4)
    uc, z3 = _adaln_mm("l1m_in", h4, mg1, 1, n_lat, w_cin_t)
    yc = _conv_fwd("l1m_conv", z3, conv_w)
    h5, conv_o = _mm_resid("l1m_out", yc, w_c_out, h4, mg1, 1, 1.0, n_lat)
    h6, sv_f11 = _ffn_half_fwd("l1f1", h5, mg1, 2, feed, 3, 0.5, n_lat)

    dh6, sq_cols, d_final_g = _final_loss("loss_head", h6, target, final_norm_g)
    g = {}
    dh5, g["f11"] = _ffn_half_bwd("l1f1", dh6, sv_f11, mg1, 2, feed, 3, 0.5, n_lat)

    do_c, dyc, dgate_c = _gate_mm("l1m_dy", dh5, conv_o, mg1, 1, 1.0, n_lat, w_c_out)
    d_c_out = _mm("l1m_dwout", [(yc, do_c)], "tn", BF16)
    db_, dc_, dv_, d_conv_w = _conv_bwd("l1m_dconv", dyc, z3, conv_w)
    dz3 = jnp.concatenate([db_, dc_, dv_], axis=-1)
    d_cin_t = _mm("l1m_dwin", [(dz3, uc)], "tn", BF16)
    token = feed.grads("l1m", {"c_out": d_c_out, "cin_t": d_cin_t})
    dh4, (dsh_c, dsc_c, dgn_c) = _du_adaln("l1m_du", [(dz3, w_cin_t)], h4, dh5, mg1, 1, n_lat, _after(token))
    dh3, g["f10"] = _ffn_half_bwd("l1f0", dh4, sv_f10, mg1, 0, feed, 2, 0.5, n_lat)

    dh2, g["f01"] = _ffn_half_bwd("l0f1", dh3, sv_f01, mg0[:1], 2, feed, 1, 0.5, n_lat)

    do_a, dcat, dgate_a = _gate_mm("l0m_dcat", dh2, mix_o, mg0[:1], 1, 1.0, n_lat, w_ab_out)
    d_ab_out = _mm("l0m_dwout", [(cat, do_a)], "tn", BF16)
    d_pool_x, d_pool_w, d_pool_scale = _pool_bwd("l0m_dpool", dcat, n_lat, pool_p, pool_w.astype(BF16), pool_scale)
    doh = dcat[:, POOL_DIM:].reshape(n_lat, HEADS, V_HEAD).transpose(1, 0, 2).astype(BF16)
    dqh, dkh, dvh = _attn_bwd("l0m_dattn", qh, kh, vh, oh, lse, doh)
    dk_sum = _sum_rows("l0m_dksum", dkh.reshape(HEADS, t_all * HEAD_PAD)).reshape(t_all, HEAD_PAD)
    dq_rot = dqh[:, :, :QK_HEAD].transpose(1, 0, 2).reshape(n_lat, Q_RANK)
    dq_lin = _rope("l0m_dqrope", dq_rot, Q_RANK, 0, cos_q, sin_q, jnp.asarray(perm_q.T, BF16), True, BF16)
    d_uq = _mm("l0m_dwuq", [(nq, dq_lin)], "tn", BF16, 768, 768)
    dnq = _mm("l0m_dnq", [(dq_lin, w_uq)], "nt", F32, 512, 768)
    dcq, d_q_norm_g = _rmsnorm_bwd("l0m_dqnorm", proj, Q_RANK, PA_CQ // Q_RANK, dnq, q_norm_g, n_lat, BF16)
    dkv = jnp.concatenate([dkh[:, :, :QK_NOPE], dvh], axis=-1).transpose(1, 0, 2).reshape(t_all, HEADS * HEAD_PAD)
    dkv = dkv.astype(BF16)
    dnkv = _mm("l0m_dnkv", [(dkv, w_ukv_t)], "nn", F32, 768, 256)
    d_ukv_t = _mm("l0m_dwukv", [(dkv, nkv)], "tn", BF16, 512, 256)
    dckv, d_kv_norm_g = _rmsnorm_bwd("l0m_dkvnorm", kvr, KV_RANK, 0, dnkv, kv_norm_g, t_all)
    dkvr = jnp.concatenate([dckv, dk_sum[:, QK_NOPE:QK_HEAD],
                            jnp.zeros((t_all, PA_KV_W - KV_RANK - QK_ROPE), F32)], axis=-1)
    dpb = _rope("l0m_dkrope", dkvr, PA_KV_W, 0, cos_k, sin_k, jnp.asarray(perm_k.T, BF16), True, BF16)
    dproj_lat = jnp.concatenate([d_pool_x, jnp.zeros((n_lat, PA_CQ - POOL_DIM), BF16), dcq, dpb[:n_lat]], axis=-1)
    dproj_ctx = jnp.concatenate([jnp.zeros((n_ctx, PA_KV), BF16), dpb[n_lat:]], axis=-1)
    dproj = jnp.concatenate([dproj_lat, dproj_ctx], axis=0)
    d_in_pad = _mm("l0m_dwin", [(dproj, ua)], "tn", BF16, 640, 512)
    d_in_t = jnp.concatenate([d_in_pad[:POOL_DIM], d_in_pad[PA_CQ:PA_CQ + Q_RANK],
                              d_in_pad[PA_KV:PA_KV + kv_rows]], axis=0)
    token = feed.grads("l0m", {"ab_out": d_ab_out, "uq": d_uq, "ukv_t": d_ukv_t, "in_t": d_in_t})
    ds1, (dsh_a, dsc_a, dgn_a) = _du_adaln("l0m_du", [(dproj, w_in_t)], s1, dh2, mg0, 1, n_lat, _after(token))
    grad_x, g["f00"] = _ffn_half_bwd("l0f0", ds1, sv_f00, mg0, 0, feed, 0, 0.5, n_lat, out_rows=n_lat)

    dmod0 = _mod_grad([g["f00"], dict(shift=dsh_a, scale=dsc_a, gate=dgate_a), g["f01"]], 2)
    dmod1 = _mod_grad([g["f10"], dict(shift=dsh_c, scale=dsc_c, gate=dgate_c), g["f11"]], 1)
    d_norm_g = jnp.stack([
        jnp.concatenate([jnp.sum(g["f00"]["gain"], axis=0), jnp.sum(dgn_a, axis=0), g["f01"]["gain"][0]], axis=0),
        jnp.concatenate([g["f10"]["gain"][0], dgn_c[0], g["f11"]["gain"][0]], axis=0)])
    grads = dict(
        pool_w=d_pool_w, pool_scale=d_pool_scale, q_norm_g=d_q_norm_g[0], kv_norm_g=d_kv_norm_g[0],
        conv_w=d_conv_w, final_norm_g=d_final_g[0], norm_g=d_norm_g,
        mod_h=jnp.stack([dmod0[0], dmod1[0]]), mod_g=dmod0[1])
    return sq_cols, grad_x, grads


HBM_SPEC = pl.BlockSpec(memory_space=pltpu.HBM)
SEM_SPEC = pl.BlockSpec(memory_space=pltpu.SEMAPHORE)
ANY_SPEC = pl.BlockSpec(memory_space=pl.ANY)
SIDE_EFFECT = pltpu.SideEffectType.DATAFLOW_SIDE_EFFECTING
N_PEERS = N_DEV - 1


def _mesh_place():
    mx, my, mc = lax.axis_index("x"), lax.axis_index("y"), lax.axis_index("c")
    return mx, my, mc, 4 * mx + 2 * my + mc


def _peer(place, kk):
    mx, my, mc, _ = place
    px = jnp.bitwise_xor(mx, (kk >> 2) & 1)
    py = jnp.bitwise_xor(my, (kk >> 1) & 1)
    pc = jnp.bitwise_xor(mc, kk & 1)
    return (px, py, pc), 4 * px + 2 * py + pc


def _hbm(a):
    return pltpu.with_memory_space_constraint(a, pltpu.HBM)


def _landing(block, me):
    zone = lax.empty((N_DEV,) + block.shape, block.dtype)
    return lax.dynamic_update_slice(zone, block[None], (me,) + (0,) * block.ndim)


ALL_PEERS = tuple(range(1, N_DEV))
SIBLING = 1
CHIP_PEERS = (2, 4, 6)
RELAYED = (3, 5, 7)


def _exchange_start(name, srcs, lands, scatter, after, peers=ALL_PEERS):
    n = len(srcs)
    extra = [] if after is None else [after]

    def body(*refs):
        src, land = refs[:n], refs[n:2 * n]
        send_sems, recv_sems, token = refs[2 * n + len(extra)], refs[2 * n + len(extra) + 1], refs[-1]
        place = _mesh_place()
        for a in range(n):
            for kk in peers:
                dev, peer = _peer(place, kk)
                pltpu.make_async_remote_copy(
                    src_ref=src[a].at[peer] if scatter else src[a],
                    dst_ref=land[a].at[kk - 1] if scatter else land[a].at[place[3]],
                    send_sem=send_sems.at[a * N_PEERS + kk - 1], recv_sem=recv_sems.at[a * N_PEERS + kk - 1],
                    device_id=dev, device_id_type=MESH).start()
        token[...] = jnp.zeros_like(token)

    thru = [pltpu.HBM(t.shape, t.dtype) for t in (*srcs, *lands)]
    res = pl.pallas_call(
        body, name=name,
        out_shape=(pltpu.SemaphoreType.DMA((n * N_PEERS,)), pltpu.SemaphoreType.DMA((n * N_PEERS,)), *thru,
                   SDS((8, 128), F32)),
        in_specs=[HBM_SPEC] * (2 * n) + [ANY_SPEC] * len(extra),
        out_specs=(SEM_SPEC, SEM_SPEC, *([HBM_SPEC] * (2 * n)), pl.BlockSpec(memory_space=pltpu.VMEM)),
        input_output_aliases={i: 2 + i for i in range(2 * n)},
        compiler_params=pltpu.CompilerParams(has_side_effects=SIDE_EFFECT),
    )(*[_hbm(s) for s in srcs], *[_hbm(t) for t in lands], *extra)
    return res[0], res[1], list(res[2:2 + n]), list(res[2 + n:2 + 2 * n]), res[-1]


def _exchange_wait(name, send_sems, recv_sems, srcs, lands, places, scatter, after):
    n = len(srcs)

    def body(*refs):
        src, land = refs[:n], refs[n:2 * n]
        send, recv = refs[2 * n], refs[2 * n + 1]
        place = _mesh_place()
        for a in range(n):
            for kk in range(1, N_DEV):
                dev, peer = _peer(place, kk)
                cp = pltpu.make_async_remote_copy(
                    src_ref=src[a].at[peer] if scatter else src[a],
                    dst_ref=land[a].at[kk - 1] if scatter else land[a].at[peer],
                    send_sem=send.at[places[a] * N_PEERS + kk - 1], recv_sem=recv.at[places[a] * N_PEERS + kk - 1],
                    device_id=dev, device_id_type=MESH)
                cp.wait_send()
                cp.wait_recv()

    thru = [pltpu.HBM(t.shape, t.dtype) for t in (*srcs, *lands)]
    res = pl.pallas_call(
        body, name=name, out_shape=tuple(thru),
        in_specs=[HBM_SPEC] * (2 * n) + [SEM_SPEC, SEM_SPEC] + [ANY_SPEC] * len(after),
        out_specs=tuple([HBM_SPEC] * (2 * n)), input_output_aliases={i: i for i in range(2 * n)},
        compiler_params=pltpu.CompilerParams(has_side_effects=SIDE_EFFECT),
    )(*srcs, *lands, send_sems, recv_sems, *after)
    return list(res[:n]), list(res[n:])


def _gather_relay(name, send1, recv1, lands, places, after):
    n = len(lands)

    def body(*refs):
        land, s1, r1 = refs[:n], refs[n], refs[n + 1]
        s2, r2 = refs[n + 3], refs[n + 4]
        place = _mesh_place()
        sibling = _peer(place, SIBLING)[0]
        for a in range(n):
            for j, kk in enumerate(CHIP_PEERS):
                dev, origin = _peer(place, kk)
                block = land[a].at[origin]
                pltpu.make_async_remote_copy(
                    src_ref=block, dst_ref=block, send_sem=s1.at[places[a] * N_PEERS + kk - 1],
                    recv_sem=r1.at[places[a] * N_PEERS + kk - 1], device_id=dev, device_id_type=MESH).wait_recv()
                pltpu.make_async_remote_copy(
                    src_ref=block, dst_ref=block, send_sem=s2.at[a * 3 + j], recv_sem=r2.at[a * 3 + j],
                    device_id=sibling, device_id_type=MESH).start()

    res = pl.pallas_call(
        body, name=name,
        out_shape=(pltpu.SemaphoreType.DMA((3 * n,)), pltpu.SemaphoreType.DMA((3 * n,)),
                   *[pltpu.HBM(t.shape, t.dtype) for t in lands]),
        in_specs=[HBM_SPEC] * n + [SEM_SPEC, SEM_SPEC, ANY_SPEC],
        out_specs=(SEM_SPEC, SEM_SPEC, *([HBM_SPEC] * n)),
        input_output_aliases={i: 2 + i for i in range(n)},
        compiler_params=pltpu.CompilerParams(has_side_effects=SIDE_EFFECT),
    )(*lands, send1, recv1, after)
    return res[0], res[1], list(res[2:])


def _gather_wait(name, send1, recv1, send2, recv2, srcs, lands, places, after):
    n = len(lands)

    def body(*refs):
        src, land = refs[:n], refs[n:2 * n]
        s1, r1, s2, r2 = refs[2 * n:2 * n + 4]
        place = _mesh_place()
        for a in range(n):
            for kk in (SIBLING,) + CHIP_PEERS:
                dev, origin = _peer(place, kk)
                first = pltpu.make_async_remote_copy(
                    src_ref=src[a], dst_ref=land[a].at[origin], send_sem=s1.at[places[a] * N_PEERS + kk - 1],
                    recv_sem=r1.at[places[a] * N_PEERS + kk - 1], device_id=dev, device_id_type=MESH)
                first.wait_send()
                if kk == SIBLING:
                    first.wait_recv()
            for j, kk in enumerate(CHIP_PEERS):
                dev, origin = _peer(place, kk + 1)
                relay = pltpu.make_async_remote_copy(
                    src_ref=src[a], dst_ref=land[a].at[origin], send_sem=s2.at[a * 3 + j], recv_sem=r2.at[a * 3 + j],
                    device_id=dev, device_id_type=MESH)
                relay.wait_send()
                relay.wait_recv()

    arrays = (*srcs, *lands)
    res = pl.pallas_call(
        body, name=name, out_shape=tuple(pltpu.HBM(t.shape, t.dtype) for t in arrays),
        in_specs=[HBM_SPEC] * (2 * n) + [SEM_SPEC] * 4 + [ANY_SPEC], out_specs=tuple([HBM_SPEC] * (2 * n)),
        input_output_aliases={i: i for i in range(2 * n)},
        compiler_params=pltpu.CompilerParams(has_side_effects=SIDE_EFFECT),
    )(*arrays, send1, recv1, send2, recv2, after)
    return list(res[n:])


class _Feed:
    def __init__(self, shards, groups, me):
        self.shards, self.groups, self.me, self.pos = shards, groups, me, 0
        self.sems, self.srcs, self.lands = {}, {}, {}
        self.relays = {}
        self.pending = []

    def start(self, tag, names, after):
        srcs = [self.shards[nm] for nm in names]
        lands = [_landing(s, self.me) for s in srcs]
        send, recv, srcs, lands, self.token = _exchange_start(
            f"gather_start_{tag}", srcs, lands, False, after, (SIBLING,) + CHIP_PEERS)
        for i, nm in enumerate(names):
            self.sems[nm], self.srcs[nm], self.lands[nm] = (send, recv, i), srcs[i], lands[i]
        return self.token

    def _relay(self, gi, after):
        names = self.groups[gi]
        if gi not in self.relays:
            send, recv, _ = self.sems[names[0]]
            places = [self.sems[nm][2] for nm in names]
            send2, recv2, lands = _gather_relay(f"gather_relay_{gi}", send, recv, [self.lands[nm] for nm in names],
                                                places, after)
            for nm, t in zip(names, lands):
                self.lands[nm] = t
            self.relays[gi] = (send2, recv2)
            after = lands[0]
        return after

    def weights(self, tag, names, after):
        gi = self.pos
        assert names == self.groups[gi], (names, self.groups[gi])
        if gi == 0:
            after = self.token
        self._relay(gi, after)
        if 1 <= gi < len(self.groups) - 1:
            after = self._relay(gi + 1, after)
        send2, recv2 = self.relays[gi]
        send, recv, _ = self.sems[names[0]]
        got = _gather_wait(f"gather_wait_{tag}", send, recv, send2, recv2, [self.srcs[nm] for nm in names],
                           [self.lands[nm] for nm in names], [self.sems[nm][2] for nm in names], after)
        self.pos += 1
        return [t.reshape((N_DEV * t.shape[1],) + t.shape[2:]) for t in got]

    def grads(self, tag, full):
        names = list(full)
        srcs = [full[nm].reshape((N_DEV, full[nm].shape[0] // N_DEV) + full[nm].shape[1:]) for nm in names]
        lands = [lax.empty((N_PEERS,) + s.shape[1:], s.dtype) for s in srcs]
        send, recv, srcs, lands, token = _exchange_start(f"scatter_start_{tag}", srcs, lands, True, None)
        self.pending.append((tag, names, send, recv, srcs, lands))
        return token[0, 0]

    def collect(self, tags, after, keep_slots=()):
        out = {}
        for tag, names, send, recv, srcs, lands in self.pending:
            if tag not in tags:
                continue
            srcs, got = _exchange_wait(f"scatter_wait_{tag}", send, recv, srcs, lands, list(range(len(names))), True,
                                       after)
            for nm, slots, src in zip(names, got, srcs):
                out[nm] = ((slots, src) if nm.startswith(tuple(keep_slots))
                           else _sum_slots(f"reduce_{nm}", slots, src, self.me))
        return out


def _adamw_math(w, gg, m, v):
    nm = ADAM_B1 * m + (1.0 - ADAM_B1) * gg
    nv = ADAM_B2 * v + (1.0 - ADAM_B2) * (gg * gg)
    bc1 = 1.0 - ADAM_B1 ** ADAM_STEP
    bc2 = 1.0 - ADAM_B2 ** ADAM_STEP
    return -ADAM_LR * ((nm / bc1) / (jnp.sqrt(nv / bc2) + ADAM_EPS) + ADAM_WD * w), nm, nv


def _adamw_part(name, i, w, scattered, me, m, v, prev):
    n_parts, rows, cols = w.shape
    tr = _tile(rows, 256, 16)
    if prev is None:
        prev = tuple(lax.empty(w.shape, F32) for _ in range(4))

    slots, src = scattered

    def body(me_ref, w_ref, g_ref, own_ref, m_ref, v_ref, *rest):
        go_ref, d_ref, nm_ref, nv_ref = rest[4:]
        gg = own_ref[...].astype(F32)
        for sl in range(N_PEERS):
            gg = gg + g_ref[sl].astype(F32)
        d, nm, nv = _adamw_math(w_ref[...], gg, m_ref[...], v_ref[...])
        go_ref[...] = gg
        d_ref[...] = d
        nm_ref[...] = nm
        nv_ref[...] = nv

    part = pl.BlockSpec((None, tr, cols), lambda r, me_ref: (i, r, 0))
    grid_spec = pltpu.PrefetchScalarGridSpec(
        num_scalar_prefetch=1, grid=(rows // tr,),
        in_specs=[part, pl.BlockSpec((N_PEERS, tr, cols), lambda r, me_ref: (0, r, 0)),
                  pl.BlockSpec((None, tr, cols), lambda r, me_ref: (me_ref[0], r, 0)), part, part] + [ANY_SPEC] * 4,
        out_specs=[part] * 4)
    return pl.pallas_call(
        body, name=name, grid_spec=grid_spec, out_shape=[SDS(w.shape, F32)] * 4,
        input_output_aliases={6 + k: k for k in range(4)}, compiler_params=_cparams(),
    )(_me_operand(me), w, slots, src, m, v, *prev)


WEIGHT_NAMES = ("c_ctx", "norm_g", "w_mod", "b_mod", "ffn_w_gate", "ffn_w_up", "ffn_w_down", "ab_w_in", "pool_w",
                "pool_scale", "q_norm_g", "w_uq", "kv_norm_g", "w_ukv", "ab_w_out", "conv_w_in", "conv_w",
                "conv_w_out", "final_norm_g")


def kernel(x, c, ctx, c_ctx, norm_g, w_mod, b_mod, ffn_w_gate, ffn_w_up, ffn_w_down, ab_w_in, pool_w, pool_scale, q_norm_g, w_uq, kv_norm_g, w_ukv, ab_w_out, conv_w_in, conv_w, conv_w_out, final_norm_g, loss_target, m_c_ctx, m_norm_g, m_w_mod, m_b_mod, m_ffn_w_gate, m_ffn_w_up, m_ffn_w_down, m_ab_w_in, m_pool_w, m_pool_scale, m_q_norm_g, m_w_uq, m_kv_norm_g, m_w_ukv, m_ab_w_out, m_conv_w_in, m_conv_w, m_conv_w_out, m_final_norm_g, v_c_ctx, v_norm_g, v_w_mod, v_b_mod, v_ffn_w_gate, v_ffn_w_up, v_ffn_w_down, v_ab_w_in, v_pool_w, v_pool_scale, v_q_norm_g, v_w_uq, v_kv_norm_g, v_w_ukv, v_ab_w_out, v_conv_w_in, v_conv_w, v_conv_w_out, v_final_norm_g):
    weights = (c_ctx, norm_g, w_mod, b_mod, ffn_w_gate, ffn_w_up, ffn_w_down, ab_w_in, pool_w, pool_scale, q_norm_g,
               w_uq, kv_norm_g, w_ukv, ab_w_out, conv_w_in, conv_w, conv_w_out, final_norm_g)
    moms = (m_c_ctx, m_norm_g, m_w_mod, m_b_mod, m_ffn_w_gate, m_ffn_w_up, m_ffn_w_down, m_ab_w_in, m_pool_w,
            m_pool_scale, m_q_norm_g, m_w_uq, m_kv_norm_g, m_w_ukv, m_ab_w_out, m_conv_w_in, m_conv_w, m_conv_w_out,
            m_final_norm_g)
    vels = (v_c_ctx, v_norm_g, v_w_mod, v_b_mod, v_ffn_w_gate, v_ffn_w_up, v_ffn_w_down, v_ab_w_in, v_pool_w,
            v_pool_scale, v_q_norm_g, v_w_uq, v_kv_norm_g, v_w_ukv, v_ab_w_out, v_conv_w_in, v_conv_w, v_conv_w_out,
            v_final_norm_g)
    me = 4 * lax.axis_index("x") + 2 * lax.axis_index("y") + lax.axis_index("c")
    n_lat, n_ctx = x.shape[1], ctx.shape[1]
    d = D_MODEL
    mod_cols = w_mod.shape[-1]
    ng_sh, cw_sh = norm_g.shape[-1], conv_w.shape[-1]

    def ffn_shards(i):
        return {f"gate_t{i}": ffn_w_gate[i // 2, i % 2].T, f"up_t{i}": ffn_w_up[i // 2, i % 2].T,
                f"down{i}": ffn_w_down[i // 2, i % 2]}

    local = {**ffn_shards(0), "in_t": ab_w_in[0].T, "uq": w_uq[0], "ukv_t": w_ukv[0].T, "ab_out": ab_w_out[0],
             **ffn_shards(1), **ffn_shards(2), "cin_t": conv_w_in[0].T, "c_out": conv_w_out[0], **ffn_shards(3)}
    ffn_groups = [[[f"gate_t{i}", f"up_t{i}"], [f"down{i}"]] for i in range(4)]
    groups = [*ffn_groups[0], ["in_t", "uq", "ukv_t", "ab_out"], *ffn_groups[1], *ffn_groups[2], ["cin_t", "c_out"],
              *ffn_groups[3]]
    feed = _Feed({nm: a.astype(BF16) for nm, a in local.items()}, groups, me)

    small = jnp.concatenate([c.reshape(-1), norm_g.reshape(-1), conv_w.reshape(-1)])
    small_n = -(-small.shape[0] // 1024) * 1024
    small = jnp.pad(small, (0, small_n - small.shape[0])).reshape(small_n // 128, 128)
    small_all = _exchange("gather_small", small, False).reshape(N_DEV, small_n)
    c_all = small_all[:, :d]
    o1 = d + 6 * ng_sh
    norm_g_full = small_all[:, d:o1].reshape(N_DEV, 2, 3, ng_sh).transpose(1, 2, 0, 3).reshape(2, 3, d)
    conv_w_full = small_all[:, o1:o1 + 3 * cw_sh].reshape(N_DEV, 3, cw_sh).transpose(1, 0, 2).reshape(3, d)

    cond = jnp.concatenate([c_all, jnp.broadcast_to(c_ctx[None, :], (N_DEV, d))], axis=0)
    sil, dsil = _silu_rows("mod_silu", cond)
    w_mod_b = w_mod.astype(BF16)
    b_sh = lax.dynamic_slice(b_mod, (0, me * mod_cols), (2, mod_cols))
    m_part = jnp.stack([_mm(f"mod_fwd{l}", [(sil, w_mod_b[l])], "nn", F32, 16, 384, bias=b_sh[l:l + 1])
                        for l in range(2)], axis=1)
    m_all = _exchange("gather_mod", m_part.reshape(-1, 128), False).reshape(N_DEV, 2 * N_DEV, 2, mod_cols)
    m_mine = lax.dynamic_index_in_dim(m_all, me, axis=1, keepdims=False)
    mod_h = m_mine.transpose(1, 0, 2).reshape(2, N_MOD, d)
    mod_g = m_all[:, N_DEV, 0, :].reshape(N_MOD, d)

    first = feed.start("first", [nm for grp in groups[:3] for nm in grp], m_all)
    feed.start("rest", [nm for grp in groups[3:] for nm in grp], first)

    sq_cols, grad_x, g = _local_step(x[0], ctx[0], loss_target[0], mod_h, mod_g, norm_g_full, feed, pool_w[0],
                                  pool_scale, q_norm_g, kv_norm_g, conv_w_full, final_norm_g)
    loss = lax.psum(0.5 * jnp.sum(sq_cols) / d, ("x", "y", "c"))
    w_of, m_of, v_of = (dict(zip(WEIGHT_NAMES, t)) for t in (weights, moms, vels))
    results = {}

    def update(nm, grad, view=lambda t: t):
        outs = _adamw(f"adamw_{nm}", view(w_of[nm]), grad.reshape(view(w_of[nm]).shape), view(m_of[nm]), view(v_of[nm]))
        results[nm] = tuple(view(t) for t in (grad.reshape(view(w_of[nm]).shape), *outs))

    def swap(t):
        return jnp.swapaxes(t, -1, -2)

    stacked = ("gate_t", "up_t", "down")
    early = feed.collect(["l1f1", "l1m", "l1f0", "l0f1", "l0m"], [grad_x], stacked)
    update("ab_w_in", early["in_t"], swap)
    update("w_uq", early["uq"])
    update("w_ukv", early["ukv_t"].T)
    update("ab_w_out", early["ab_out"])
    update("conv_w_in", early["cin_t"].T)
    update("conv_w_out", early["c_out"])
    ffn = {}
    for nm, prefix, view in (("ffn_w_gate", "gate_t", swap), ("ffn_w_up", "up_t", swap),
                             ("ffn_w_down", "down", lambda t: t)):
        w4, m4, v4 = (view(t).reshape((4,) + view(t).shape[-2:]) for t in (w_of[nm], m_of[nm], v_of[nm]))
        prev = None
        for i in (3, 2, 1):
            prev = _adamw_part(f"adamw_{nm}{i}", i, w4, early[f"{prefix}{i}"], me, m4, v4, prev)
        ffn[nm] = (prefix, view, w4, m4, v4, prev)
    done_early = [results[nm][1] for nm in results] + [state[5][1] for state in ffn.values()]
    late = feed.collect(["l0f0"], done_early, stacked)
    for nm, (prefix, view, w4, m4, v4, prev) in ffn.items():
        outs = _adamw_part(f"adamw_{nm}0", 0, w4, late[f"{prefix}0"], me, m4, v4, prev)
        results[nm] = tuple(view(t.reshape(view(w_of[nm]).shape)) for t in outs)

    dm = jnp.stack([g["mod_h"], jnp.stack([g["mod_g"], jnp.zeros_like(g["mod_g"])])])
    dm_all = _exchange("gather_dmod", dm.reshape(-1, 128), False, results["ffn_w_down"][1]).reshape(N_DEV, 2, 2, N_MOD * d)
    grad_b_mod = _sum_rows("dmod_bias", dm_all.reshape(2 * N_DEV, 2 * N_MOD * d)).reshape(2, N_MOD * d)
    dm_sh = lax.dynamic_slice(dm_all, (0, 0, 0, me * mod_cols), (N_DEV, 2, 2, mod_cols))
    gw_mod, cctx_parts = [], []
    for l in range(2):
        dm_l = dm_sh[:, :, l, :].transpose(1, 0, 2).reshape(2 * N_DEV, mod_cols).astype(BF16)
        gw_mod.append(_mm(f"mod_dw{l}", [(sil, dm_l)], "tn", F32, 512, 384))
        dm_ctx = jnp.concatenate([dm_l[N_DEV:], jnp.zeros((N_DEV, mod_cols), BF16)], axis=0)
        cctx_parts.append(_mm(f"mod_dcond{l}", [(dm_ctx, w_mod_b[l])], "nt", F32, 16, 512))
    cctx_part = _sum_rows("mod_dcond_sum", jnp.concatenate(cctx_parts, axis=0))
    update("w_mod", jnp.stack(gw_mod))
    update("b_mod", grad_b_mod)

    small_g = jnp.concatenate([g["pool_w"].reshape(-1), g["pool_scale"].reshape(-1), g["q_norm_g"].reshape(-1),
                               g["kv_norm_g"].reshape(-1), g["final_norm_g"].reshape(-1), g["norm_g"].reshape(-1),
                               g["conv_w"].reshape(-1), cctx_part.reshape(-1)])
    sizes = [pool_w.size, pool_scale.size, q_norm_g.size, kv_norm_g.size, d, 6 * d, 3 * d, d]
    sg_n = -(-small_g.shape[0] // 1024) * 1024
    small_g = jnp.pad(small_g, (0, sg_n - small_g.shape[0]))
    sg_all = _exchange("gather_small_grads", small_g.reshape(-1, 128), False).reshape(N_DEV, sg_n)
    scale_vec = jnp.concatenate([jnp.ones((1, sum(sizes[:-1])), F32), dsil[N_DEV:N_DEV + 1],
                                 jnp.ones((1, sg_n - sum(sizes)), F32)], axis=1)
    sg = _sum_rows("small_grads_sum", sg_all, scale_vec)[0]
    cuts, pos = [], 0
    for sz in sizes:
        cuts.append(sg[pos:pos + sz])
        pos += sz
    g_pool_w, g_pool_scale, g_q_norm, g_kv_norm, g_final, g_norm_full, g_conv_full, g_c_ctx = cuts
    update("c_ctx", g_c_ctx)
    update("norm_g", lax.dynamic_slice(g_norm_full.reshape(2, 3, d), (0, 0, me * ng_sh), (2, 3, ng_sh)))
    update("conv_w", lax.dynamic_slice(g_conv_full.reshape(3, d), (0, me * cw_sh), (3, cw_sh)))
    update("pool_w", g_pool_w)
    update("pool_scale", g_pool_scale)
    update("q_norm_g", g_q_norm)
    update("kv_norm_g", g_kv_norm)
    update("final_norm_g", g_final)
    outs = [results[nm] for nm in WEIGHT_NAMES]
    return (loss, grad_x[None], *[o[0] for o in outs], *[o[1] for o in outs], *[o[2] for o in outs],
            *[o[3] for o in outs])
```

```python
import functools
import math

import jax
import jax.numpy as jnp
import numpy as np
from jax import lax
from jax.experimental import pallas as pl
from jax.experimental.pallas import tpu as pltpu

F32 = jnp.float32
BF16 = jnp.bfloat16
MESH = pl.DeviceIdType.MESH
SDS = jax.ShapeDtypeStruct

N_DEV = 8
D_MODEL = 1024
N_MOD = 9
D_FF = 2816
POOL_WINDOWS = (2, 4, 8, 16)
POOL_DIM = 512
POOL_GROUP_DIM = 128
HEADS = 8
QK_NOPE = 64
QK_ROPE = 32
QK_HEAD = QK_NOPE + QK_ROPE
V_HEAD = 64
Q_RANK = 768
KV_RANK = 256
GRID_W = 64
ROPE_THETA = 10000.0
RMS_EPS = 1e-6
ATTN_SCALE = 1.0 / math.sqrt(QK_HEAD)
HEAD_PAD = 128
POOL_PAD = 16
PA_POOL, PA_CQ, PA_KV = 0, 768, 1536
PA_KV_W = 384
PA_W = PA_KV + PA_KV_W

ADAM_LR, ADAM_B1, ADAM_B2, ADAM_EPS, ADAM_WD, ADAM_STEP = 0.001, 0.9, 0.999, 1e-08, 0.01, 10

VMEM_LIMIT_BYTES = 56 * 1024 * 1024

NN = ((1,), (0,))
NT = ((1,), (1,))
TN = ((0,), (0,))


def _cparams():
    return pltpu.CompilerParams(vmem_limit_bytes=VMEM_LIMIT_BYTES)


def _dot(a, b, dims):
    return lax.dot_general(a, b, (dims, ((), ())), preferred_element_type=F32)


def _tile(n, cap, mult=8):
    t = (min(cap, n) // mult) * mult
    while t >= mult:
        if n % t == 0:
            return t
        t -= mult
    return n


def _colsum(x):
    return jnp.sum(x, axis=0, keepdims=True)


def _rms(x):
    r = lax.rsqrt(jnp.mean(x * x, axis=-1, keepdims=True) + RMS_EPS)
    return x * r, r


def _rms_bwd(n, r, dn):
    return r * (dn - n * jnp.mean(dn * n, axis=-1, keepdims=True))


def _rowwise(name, fn, t_rows, tm, n_lat, rows, vecs, outs, accs):
    nt = t_rows // tm
    nlt = n_lat // tm
    n_groups = 2 if nlt < nt else 1

    def grp(i):
        return jnp.where(i >= nlt, 1, 0) if n_groups == 2 else 0

    in_specs = [pl.BlockSpec((tm, w), functools.partial(lambda i, cb: (i, cb), cb=cb)) for (_, w, cb) in rows]
    in_specs += [pl.BlockSpec((1,) + v.shape[1:], lambda i: (grp(i), 0, 0)) for v in vecs]
    out_specs = [pl.BlockSpec((tm, w), lambda i: (i, 0)) for (w, _) in outs]
    out_specs += [pl.BlockSpec((1, 1, w), lambda i: (grp(i), 0, 0)) for w in accs]
    out_shape = [SDS((t_rows, w), dt) for (w, dt) in outs] + [SDS((n_groups, 1, w), F32) for w in accs]
    n_r, n_v, n_o = len(rows), len(vecs), len(outs)

    def body(*refs):
        row_vals = [r[...] for r in refs[:n_r]]
        vec_vals = [v[0] for v in refs[n_r:n_r + n_v]]
        out_refs = refs[n_r + n_v:n_r + n_v + n_o]
        acc_refs = refs[n_r + n_v + n_o:]
        out_vals, acc_vals = fn(row_vals, vec_vals)
        for o_ref, o in zip(out_refs, out_vals):
            o_ref[...] = o.astype(o_ref.dtype)
        if acc_refs:
            i = pl.program_id(0)
            first = (i == 0) | (i == nlt) if n_groups == 2 else i == 0

            @pl.when(first)
            def _():
                for a_ref, a in zip(acc_refs, acc_vals):
                    a_ref[0] = a

            @pl.when(jnp.logical_not(first))
            def _():
                for a_ref, a in zip(acc_refs, acc_vals):
                    a_ref[0] += a

    res = pl.pallas_call(
        body, name=name, grid=(nt,), in_specs=in_specs, out_specs=out_specs, out_shape=out_shape,
        compiler_params=_cparams(),
    )(*[r[0] for r in rows], *vecs)
    return res[:n_o], res[n_o:]


RESIDENT_BYTES = 12 * 1024 * 1024


def _mm(name, pairs, mode, out_dtype, tm_cap=256, tn_cap=512, bias=None):
    a0, b0 = pairs[0]
    if mode == "nn":
        m, n, dims = a0.shape[0], b0.shape[1], NN
    elif mode == "nt":
        m, n, dims = a0.shape[0], b0.shape[0], NT
    else:
        m, n, dims = a0.shape[1], b0.shape[1], TN
    b_bytes = sum(b.size * b.dtype.itemsize for _, b in pairs)
    tn = n if b_bytes <= RESIDENT_BYTES else _tile(n, tn_cap, 128)
    tm = _tile(m, tm_cap, 128 if mode == "tn" else 16)

    def a_spec(a):
        if mode == "tn":
            return pl.BlockSpec((a.shape[0], tm), lambda i, j: (0, i))
        return pl.BlockSpec((tm, a.shape[1]), lambda i, j: (i, 0))

    def b_spec(b):
        if mode == "nt":
            return pl.BlockSpec((tn, b.shape[1]), lambda i, j: (j, 0))
        return pl.BlockSpec((b.shape[0], tn), lambda i, j: (0, j))

    in_specs, flat = [], []
    for a, b in pairs:
        in_specs += [a_spec(a), b_spec(b)]
        flat += [a, b]
    if bias is not None:
        in_specs.append(pl.BlockSpec((1, tn), lambda i, j: (0, j)))
        flat.append(bias)
    n_pairs = len(pairs)

    def body(*refs):
        acc = None
        for p in range(n_pairs):
            t = _dot(refs[2 * p][...], refs[2 * p + 1][...], dims)
            acc = t if acc is None else acc + t
        if bias is not None:
            acc = acc + refs[2 * n_pairs][...]
        refs[-1][...] = acc.astype(refs[-1].dtype)

    return pl.pallas_call(
        body, name=name, grid=(m // tm, n // tn), in_specs=in_specs,
        out_specs=pl.BlockSpec((tm, tn), lambda i, j: (i, j)),
        out_shape=SDS((m, n), out_dtype), compiler_params=_cparams(),
    )(*flat)


def _mm_resid(name, a, b, s, mg, k, coef, n_lat):
    t_rows, n = a.shape[0], b.shape[1]
    tm = _tile(math.gcd(n_lat, t_rows), 256, 16)
    nlt = n_lat // tm
    n_groups = 2 if nlt < t_rows // tm else 1

    def grp(i):
        return jnp.where(i >= nlt, 1, 0) if n_groups == 2 else 0

    def body(a_ref, b_ref, s_ref, mg_ref, so_ref, o_ref):
        o = _dot(a_ref[...], b_ref[...], NN)
        gate = mg_ref[0, 3 * k + 2:3 * k + 3, :]
        o_ref[...] = o.astype(BF16)
        so_ref[...] = s_ref[...] + (coef * gate) * o

    row = pl.BlockSpec((tm, n), lambda i: (i, 0))
    return pl.pallas_call(
        body, name=name, grid=(t_rows // tm,),
        in_specs=[pl.BlockSpec((tm, a.shape[1]), lambda i: (i, 0)), pl.BlockSpec(b.shape, lambda i: (0, 0)), row,
                  pl.BlockSpec((1, mg.shape[1], n), lambda i: (grp(i), 0, 0))],
        out_specs=[row, row], out_shape=[SDS((t_rows, n), F32), SDS((t_rows, n), BF16)], compiler_params=_cparams(),
    )(a, b, s, mg)


def _dw_pair(name, a1, a2, b):
    kk, m = a1.shape
    n = b.shape[1]
    tm = _tile(m, 256, 128)

    def body(a1_ref, a2_ref, b_ref, o1_ref, o2_ref):
        bb = b_ref[...]
        o1_ref[...] = _dot(a1_ref[...], bb, TN).astype(BF16)
        o2_ref[...] = _dot(a2_ref[...], bb, TN).astype(BF16)

    col = pl.BlockSpec((kk, tm), lambda i: (0, i))
    out = pl.BlockSpec((tm, n), lambda i: (i, 0))
    return pl.pallas_call(
        body, name=name, grid=(m // tm,), in_specs=[col, col, pl.BlockSpec(b.shape, lambda i: (0, 0))],
        out_specs=[out, out], out_shape=[SDS((m, n), BF16)] * 2, compiler_params=_cparams(),
    )(a1, a2, b)


def _groups(t_rows, tm, n_lat):
    nlt = n_lat // tm
    if nlt < t_rows // tm:
        return 2, (lambda i: jnp.where(i >= nlt, 1, 0)), (lambda i: (i == 0) | (i == nlt))
    return 1, (lambda i: 0), (lambda i: i == 0)


def _accumulate(acc_refs, vals, first):
    @pl.when(first)
    def _():
        for r, v in zip(acc_refs, vals):
            r[0] = v

    @pl.when(jnp.logical_not(first))
    def _():
        for r, v in zip(acc_refs, vals):
            r[0] += v


def _adaln_math(s, m, k):
    n, _ = _rms(s)
    return (n * m[9 + k:10 + k]) * (1.0 + m[3 * k + 1:3 * k + 2]) + m[3 * k:3 * k + 1]


def _ffn_up(name, s, mg, k, n_lat, wg_t, wu_t):
    t_rows, f = s.shape[0], wg_t.shape[0]
    tm = _row_tm(t_rows, n_lat)
    _, grp, _ = _groups(t_rows, tm, n_lat)

    def body(s_ref, mg_ref, wg_ref, wu_ref, u_ref, a_ref, b_ref, h_ref):
        uu = _adaln_math(s_ref[...], mg_ref[0], k).astype(BF16)
        u_ref[...] = uu
        a = _dot(uu, wg_ref[...], NT)
        b = _dot(uu, wu_ref[...], NT)
        sg = jax.nn.sigmoid(a)
        act = a * sg
        a_ref[...] = (b * (sg * (1.0 + a * (1.0 - sg)))).astype(BF16)
        b_ref[...] = act.astype(BF16)
        h_ref[...] = (act * b).astype(BF16)

    w_spec = pl.BlockSpec(wg_t.shape, lambda i: (0, 0))
    o_spec = pl.BlockSpec((tm, f), lambda i: (i, 0))
    row = pl.BlockSpec((tm, s.shape[1]), lambda i: (i, 0))
    return pl.pallas_call(
        body, name=name, grid=(t_rows // tm,),
        in_specs=[row, pl.BlockSpec((1,) + mg.shape[1:], lambda i: (grp(i), 0, 0)), w_spec, w_spec],
        out_specs=[row, o_spec, o_spec, o_spec],
        out_shape=[SDS(s.shape, BF16)] + [SDS((t_rows, f), BF16)] * 3, compiler_params=_cparams(),
    )(s, mg, wg_t, wu_t)


def _ffn_dact(name, ds_out, o, mg, k, coef, n_lat, wd, a, b):
    t_rows, f = ds_out.shape[0], wd.shape[0]
    tm = _row_tm(t_rows, n_lat)
    n_groups, grp, first = _groups(t_rows, tm, n_lat)
    d = ds_out.shape[1]

    def body(ds_ref, o_ref, mg_ref, wd_ref, a_ref, b_ref, do_ref, da_ref, db_ref, dg_ref):
        dd = coef * ds_ref[...]
        do = (dd * mg_ref[0, 3 * k + 2:3 * k + 3, :]).astype(BF16)
        do_ref[...] = do
        _accumulate([dg_ref], [_colsum(dd * o_ref[...].astype(F32))], first(pl.program_id(0)))
        dh = _dot(do, wd_ref[...], NT)
        da_ref[...] = (dh * a_ref[...].astype(F32)).astype(BF16)
        db_ref[...] = (dh * b_ref[...].astype(F32)).astype(BF16)

    row = pl.BlockSpec((tm, d), lambda i: (i, 0))
    t_spec = pl.BlockSpec((tm, f), lambda i: (i, 0))
    return pl.pallas_call(
        body, name=name, grid=(t_rows // tm,),
        in_specs=[row, row, pl.BlockSpec((1,) + mg.shape[1:], lambda i: (grp(i), 0, 0)),
                  pl.BlockSpec(wd.shape, lambda i: (0, 0)), t_spec, t_spec],
        out_specs=[row, t_spec, t_spec, pl.BlockSpec((1, 1, d), lambda i: (grp(i), 0, 0))],
        out_shape=[SDS((t_rows, d), BF16), SDS((t_rows, f), BF16), SDS((t_rows, f), BF16), SDS((n_groups, 1, d), F32)],
        compiler_params=_cparams(),
    )(ds_out, o, mg, wd, a, b)


def _du_adaln(name, pairs, s, ds_out, mg, k, n_lat, after, out_rows=None):
    t_rows, d = s.shape
    tm = _row_tm(t_rows, n_lat)
    n_groups, grp, first = _groups(t_rows, tm, n_lat)
    n_pairs = len(pairs)
    nt, n_ds, n_out = t_rows // tm, ds_out.shape[0] // tm, (out_rows or t_rows) // tm

    def body(*refs):
        s_ref, ds_ref, mg_ref, z_ref, out_ref, dsh_ref, dsc_ref, dgn_ref = refs[2 * n_pairs:]
        i = pl.program_id(0)
        d_u = z_ref[...]
        for p in range(n_pairs):
            d_u = d_u + _dot(refs[p][...], refs[n_pairs + p][...], NN)
        m = mg_ref[0]
        gain, scale = m[9 + k:10 + k], m[3 * k + 1:3 * k + 2]
        n, r = _rms(s_ref[...])
        dxn = d_u * (1.0 + scale)
        ds_in = _rms_bwd(n, r, dxn * gain)
        ds_in = ds_in + (ds_ref[...] if n_ds == nt else jnp.where(i < n_ds, ds_ref[...], 0.0))
        if n_out == nt:
            out_ref[...] = ds_in
        else:
            @pl.when(i < n_out)
            def _():
                out_ref[...] = ds_in
        _accumulate([dsh_ref, dsc_ref, dgn_ref], [_colsum(d_u), _colsum(d_u * (n * gain)), _colsum(dxn * n)], first(i))

    row = pl.BlockSpec((tm, d), lambda i: (i, 0))
    acc = pl.BlockSpec((1, 1, d), lambda i: (grp(i), 0, 0))
    res = pl.pallas_call(
        body, name=name, grid=(t_rows // tm,),
        in_specs=[pl.BlockSpec((tm, a.shape[1]), lambda i: (i, 0)) for a, _ in pairs]
        + [pl.BlockSpec(w.shape, lambda i: (0, 0)) for _, w in pairs]
        + [row, pl.BlockSpec((tm, d), lambda i: (jnp.minimum(i, n_ds - 1), 0)),
           pl.BlockSpec((1,) + mg.shape[1:], lambda i: (grp(i), 0, 0)), pl.BlockSpec((1, d), lambda i: (0, 0))],
        out_specs=[pl.BlockSpec((tm, d), lambda i: (jnp.minimum(i, n_out - 1), 0)), acc, acc, acc],
        out_shape=[SDS((n_out * tm, d), F32)] + [SDS((n_groups, 1, d), F32)] * 3, compiler_params=_cparams(),
    )(*[a for a, _ in pairs], *[w for _, w in pairs], s, ds_out, mg, after)
    return res[0], res[1:]


def _adaln_mm(name, s, mg, k, n_lat, w_t):
    rows, d = s.shape
    tm = _row_tm(rows, n_lat)
    _, grp, _ = _groups(rows, tm, n_lat)
    n = w_t.shape[0]

    def body(s_ref, mg_ref, w_ref, u_ref, y_ref):
        uu = _adaln_math(s_ref[...], mg_ref[0], k).astype(BF16)
        u_ref[...] = uu
        y_ref[...] = _dot(uu, w_ref[...], NT)

    row = pl.BlockSpec((tm, d), lambda i: (i, 0))
    return pl.pallas_call(
        body, name=name, grid=(rows // tm,),
        in_specs=[row, pl.BlockSpec((1,) + mg.shape[1:], lambda i: (grp(i), 0, 0)), pl.BlockSpec(w_t.shape, lambda i: (0, 0))],
        out_specs=[row, pl.BlockSpec((tm, n), lambda i: (i, 0))],
        out_shape=[SDS((rows, d), BF16), SDS((rows, n), F32)], compiler_params=_cparams(),
    )(s, mg, w_t)


def _gate_mm(name, ds_out, o, mg, k, coef, n_lat, w):
    t_rows, d = ds_out.shape
    tm = _row_tm(t_rows, n_lat)
    n_groups, grp, first = _groups(t_rows, tm, n_lat)
    n = w.shape[0]

    def body(ds_ref, o_ref, mg_ref, w_ref, do_ref, y_ref, dg_ref):
        dd = coef * ds_ref[...]
        do = (dd * mg_ref[0, 3 * k + 2:3 * k + 3, :]).astype(BF16)
        do_ref[...] = do
        _accumulate([dg_ref], [_colsum(dd * o_ref[...].astype(F32))], first(pl.program_id(0)))
        y_ref[...] = _dot(do, w_ref[...], NT)

    row = pl.BlockSpec((tm, d), lambda i: (i, 0))
    return pl.pallas_call(
        body, name=name, grid=(t_rows // tm,),
        in_specs=[row, row, pl.BlockSpec((1,) + mg.shape[1:], lambda i: (grp(i), 0, 0)), pl.BlockSpec(w.shape, lambda i: (0, 0))],
        out_specs=[row, pl.BlockSpec((tm, n), lambda i: (i, 0)), pl.BlockSpec((1, 1, d), lambda i: (grp(i), 0, 0))],
        out_shape=[SDS((t_rows, d), BF16), SDS((t_rows, n), F32), SDS((n_groups, 1, d), F32)],
        compiler_params=_cparams(),
    )(ds_out, o, mg, w)


def _row_tm(t_rows, n_lat):
    return _tile(math.gcd(t_rows, n_lat), 256, 16)


def _rmsnorm_fwd(name, x, width, colblk, gain, t_rows):
    def fn(rv, vv):
        n, _ = _rms(rv[0])
        return [n * vv[0]], []

    (y,), _ = _rowwise(name, fn, t_rows, _tile(t_rows, 256, 16), t_rows, [(x, width, colblk)],
                       [gain.reshape(1, 1, width)], [(width, BF16)], [])
    return y


def _rmsnorm_bwd(name, x, width, colblk, dy, gain, t_rows, out_dtype=F32):
    def fn(rv, vv):
        n, r = _rms(rv[0])
        return [_rms_bwd(n, r, rv[1] * vv[0])], [_colsum(rv[1] * n)]

    (dx,), (dgain,) = _rowwise(name, fn, t_rows, _tile(t_rows, 256, 16), t_rows,
                               [(x, width, colblk), (dy, width, 0)], [gain.reshape(1, 1, width)],
                               [(width, out_dtype)], [width])
    return dx, dgain


def _final_loss(name, h, target, gain):
    t_rows = h.shape[0]
    inv_d = 1.0 / D_MODEL

    def fn(rv, vv):
        g = vv[0]
        n, r = _rms(rv[0])
        e = n * g - rv[1]
        dy = e * inv_d
        return [_rms_bwd(n, r, dy * g)], [_colsum(e * e), _colsum(dy * n)]

    (dh,), (sq, dgain) = _rowwise(name, fn, t_rows, _tile(t_rows, 256, 16), t_rows,
                                  [(h, D_MODEL, 0), (target, D_MODEL, 0)], [gain.reshape(1, 1, D_MODEL)],
                                  [(D_MODEL, F32)], [D_MODEL, D_MODEL])
    return dh, sq, dgain


def _rope(name, z, width, colblk, cos, sin, perm, backward, out_dtype):
    t_rows = cos.shape[0]

    def body(z_ref, c_ref, s_ref, p_ref, o_ref):
        zz = z_ref[...]
        pre = zz * s_ref[...] if backward else zz
        hi = pre.astype(BF16)
        lo = (pre - hi.astype(F32)).astype(BF16)
        rot = _dot(hi, p_ref[...], NN) + _dot(lo, p_ref[...], NN)
        if not backward:
            rot = rot * s_ref[...]
        o_ref[...] = (zz * c_ref[...] + rot).astype(o_ref.dtype)

    tm = _tile(t_rows, 256, 16)
    t_spec = pl.BlockSpec((tm, width), lambda i: (i, 0))
    return pl.pallas_call(
        body, name=name, grid=(t_rows // tm,),
        in_specs=[pl.BlockSpec((tm, width), lambda i: (i, colblk)), t_spec, t_spec,
                  pl.BlockSpec((width, width), lambda i: (0, 0))],
        out_specs=t_spec, out_shape=SDS((t_rows, width), out_dtype), compiler_params=_cparams(),
    )(z, cos, sin, perm)


def _window_sum(x, w, transposed):
    n_rows = x.shape[0]
    zeros = jnp.zeros((POOL_PAD, x.shape[1]), F32)
    y = jnp.concatenate([zeros, x, zeros], axis=0)
    total = n_rows + 2 * POOL_PAD
    if transposed:
        y = y + pltpu.roll(y, total - 1, 0)
    else:
        y = y + pltpu.roll(y, 1, 0)
    step = 1
    while 2 * step < w:
        y = pltpu.roll(y, step, 0) + pltpu.roll(y, total - step, 0)
        step *= 2
    return y[POOL_PAD:POOL_PAD + n_rows]


def _window_count(n_rows, w):
    t = lax.broadcasted_iota(jnp.int32, (n_rows, 1), 0)
    lo = jnp.maximum(t - w // 2, 0)
    hi = jnp.minimum(t + (w - w // 2 - 1), n_rows - 1)
    return (hi - lo + 1).astype(F32)


def _pool_fwd(name, proj, n_rows, w_grp, scale):
    def body(x_ref, w_ref, sc_ref, y_ref, p_ref):
        for g, w in enumerate(POOL_WINDOWS):
            cols = slice(g * POOL_GROUP_DIM, (g + 1) * POOL_GROUP_DIM)
            x = x_ref[:, cols]
            p = _window_sum(x, w, False) * (1.0 / _window_count(n_rows, w)) - x
            pb = p.astype(BF16)
            p_ref[:, cols] = pb
            y_ref[:, cols] = (_dot(pb, w_ref[g], NN) * sc_ref[:, cols]).astype(BF16)

    blk = pl.BlockSpec((n_rows, POOL_DIM), lambda i: (0, 0))
    return pl.pallas_call(
        body, name=name, grid=(1,),
        in_specs=[blk, pl.BlockSpec(w_grp.shape, lambda i: (0, 0, 0)), pl.BlockSpec((1, POOL_DIM), lambda i: (0, 0))],
        out_specs=[blk, blk], out_shape=[SDS((n_rows, POOL_DIM), BF16)] * 2, compiler_params=_cparams(),
    )(proj, w_grp, scale)


def _pool_bwd(name, dcat, n_rows, p, w_grp, scale):
    def body(dy_ref, p_ref, w_ref, sc_ref, dx_ref, dw_ref, dsc_ref):
        for g, w in enumerate(POOL_WINDOWS):
            cols = slice(g * POOL_GROUP_DIM, (g + 1) * POOL_GROUP_DIM)
            dy = dy_ref[:, cols]
            pb = p_ref[:, cols]
            pw = _dot(pb, w_ref[g], NN)
            dsc_ref[:, cols] = _colsum(dy * pw)
            dpw = (dy * sc_ref[:, cols]).astype(BF16)
            dw_ref[g] = _dot(pb, dpw, TN)
            dp = _dot(dpw, w_ref[g], NT)
            dx_ref[:, cols] = (_window_sum(dp * (1.0 / _window_count(n_rows, w)), w, True) - dp).astype(BF16)

    blk = pl.BlockSpec((n_rows, POOL_DIM), lambda i: (0, 0))
    w_spec = pl.BlockSpec(w_grp.shape, lambda i: (0, 0, 0))
    v_spec = pl.BlockSpec((1, POOL_DIM), lambda i: (0, 0))
    return pl.pallas_call(
        body, name=name, grid=(1,), in_specs=[blk, blk, w_spec, v_spec], out_specs=[blk, w_spec, v_spec],
        out_shape=[SDS((n_rows, POOL_DIM), BF16), SDS(w_grp.shape, F32), SDS((1, POOL_DIM), F32)],
        compiler_params=_cparams(),
    )(dcat, p, w_grp, scale)


def _attn_fwd(name, q, k, v):
    h, n_q, _ = q.shape
    n_k = k.shape[1]
    tq = _tile(n_q, 256, 16)

    def body(q_ref, k_ref, v_ref, o_ref, lse_ref):
        s = _dot(q_ref[...], k_ref[...], NT) * ATTN_SCALE
        m = jnp.max(s, axis=-1, keepdims=True)
        e = jnp.exp(s - m)
        l = jnp.sum(e, axis=-1, keepdims=True)
        p = (e * (1.0 / l)).astype(BF16)
        o_ref[...] = _dot(p, v_ref[...], NN).astype(BF16)
        lse_ref[...] = m + jnp.log(l)

    return pl.pallas_call(
        body, name=name, grid=(h, n_q // tq),
        in_specs=[pl.BlockSpec((None, tq, HEAD_PAD), lambda hh, i: (hh, i, 0)),
                  pl.BlockSpec((None, n_k, HEAD_PAD), lambda hh, i: (hh, 0, 0)),
                  pl.BlockSpec((None, n_k, V_HEAD), lambda hh, i: (hh, 0, 0))],
        out_specs=[pl.BlockSpec((None, tq, V_HEAD), lambda hh, i: (hh, i, 0)),
                   pl.BlockSpec((None, tq, 1), lambda hh, i: (hh, i, 0))],
        out_shape=[SDS((h, n_q, V_HEAD), BF16), SDS((h, n_q, 1), F32)], compiler_params=_cparams(),
    )(q, k, v)


def _attn_bwd(name, q, k, v, o, lse, do):
    h, n_q, _ = q.shape
    n_k = k.shape[1]
    tq = _tile(n_q, 256, 16)

    def body(q_ref, k_ref, v_ref, o_ref, lse_ref, do_ref, dq_ref, dk_ref, dv_ref):
        i = pl.program_id(1)
        qq, kk, dd = q_ref[...], k_ref[...], do_ref[...]
        s = _dot(qq, kk, NT) * ATTN_SCALE
        p = jnp.exp(s - lse_ref[...])
        dp = _dot(dd, v_ref[...], NT)
        delta = jnp.sum(dd.astype(F32) * o_ref[...].astype(F32), axis=-1, keepdims=True)
        ds = (p * (dp - delta) * ATTN_SCALE).astype(BF16)
        dq_ref[...] = _dot(ds, kk, NN)
        dk = _dot(ds, qq, TN)
        dv = _dot(p.astype(BF16), dd, TN)

        @pl.when(i == 0)
        def _():
            dk_ref[...] = dk
            dv_ref[...] = dv

        @pl.when(i > 0)
        def _():
            dk_ref[...] += dk
            dv_ref[...] += dv

    q_spec = pl.BlockSpec((None, tq, HEAD_PAD), lambda hh, i: (hh, i, 0))
    k_spec = pl.BlockSpec((None, n_k, HEAD_PAD), lambda hh, i: (hh, 0, 0))
    v_spec = pl.BlockSpec((None, n_k, V_HEAD), lambda hh, i: (hh, 0, 0))
    o_spec = pl.BlockSpec((None, tq, V_HEAD), lambda hh, i: (hh, i, 0))
    return pl.pallas_call(
        body, name=name, grid=(h, n_q // tq),
        in_specs=[q_spec, k_spec, v_spec, o_spec, pl.BlockSpec((None, tq, 1), lambda hh, i: (hh, i, 0)), o_spec],
        out_specs=[q_spec, k_spec, v_spec],
        out_shape=[SDS((h, n_q, HEAD_PAD), F32), SDS((h, n_k, HEAD_PAD), F32), SDS((h, n_k, V_HEAD), F32)],
        compiler_params=_cparams(),
    )(q, k, v, o, lse, do)


CONV_COLS = 256


def _shift_rows(x, d):
    n_rows = x.shape[0]
    t = lax.broadcasted_iota(jnp.int32, (n_rows, 1), 0)
    if d > 0:
        return jnp.where(t >= d, pltpu.roll(x, d, 0), 0.0)
    return jnp.where(t < n_rows + d, pltpu.roll(x, n_rows + d, 0), 0.0)


def _conv_fwd(name, z3, conv_w):
    n_rows = z3.shape[0]
    nb = D_MODEL // CONV_COLS

    def body(b_ref, c_ref, v_ref, w_ref, y_ref):
        z = c_ref[...] * v_ref[...]
        zc = w_ref[0:1, :] * _shift_rows(z, 1) + w_ref[1:2, :] * z + w_ref[2:3, :] * _shift_rows(z, -1)
        y_ref[...] = (b_ref[...] * zc).astype(BF16)

    def part(k):
        return pl.BlockSpec((n_rows, CONV_COLS), lambda j: (0, k * nb + j))

    return pl.pallas_call(
        body, name=name, grid=(nb,),
        in_specs=[part(0), part(1), part(2), pl.BlockSpec((3, CONV_COLS), lambda j: (0, j))],
        out_specs=pl.BlockSpec((n_rows, CONV_COLS), lambda j: (0, j)),
        out_shape=SDS((n_rows, D_MODEL), BF16), compiler_params=_cparams(),
    )(z3, z3, z3, conv_w)


def _conv_bwd(name, dy, z3, conv_w):
    n_rows = z3.shape[0]
    nb = D_MODEL // CONV_COLS

    def body(dy_ref, b_ref, c_ref, v_ref, w_ref, db_ref, dc_ref, dv_ref, dw_ref):
        c, v, d_y = c_ref[...], v_ref[...], dy_ref[...]
        z = c * v
        z_dn, z_up = _shift_rows(z, 1), _shift_rows(z, -1)
        zc = w_ref[0:1, :] * z_dn + w_ref[1:2, :] * z + w_ref[2:3, :] * z_up
        db_ref[...] = (d_y * zc).astype(BF16)
        dzc = d_y * b_ref[...]
        dz = w_ref[0:1, :] * _shift_rows(dzc, -1) + w_ref[1:2, :] * dzc + w_ref[2:3, :] * _shift_rows(dzc, 1)
        dc_ref[...] = (dz * v).astype(BF16)
        dv_ref[...] = (dz * c).astype(BF16)
        dw_ref[0:1, :] = _colsum(dzc * z_dn)
        dw_ref[1:2, :] = _colsum(dzc * z)
        dw_ref[2:3, :] = _colsum(dzc * z_up)

    def part(k):
        return pl.BlockSpec((n_rows, CONV_COLS), lambda j: (0, k * nb + j))

    col = pl.BlockSpec((n_rows, CONV_COLS), lambda j: (0, j))
    w_spec = pl.BlockSpec((3, CONV_COLS), lambda j: (0, j))
    return pl.pallas_call(
        body, name=name, grid=(nb,), in_specs=[col, part(0), part(1), part(2), w_spec],
        out_specs=[col, col, col, w_spec],
        out_shape=[SDS((n_rows, D_MODEL), BF16)] * 3 + [SDS((3, D_MODEL), F32)], compiler_params=_cparams(),
    )(dy, z3, z3, z3, conv_w)


def _silu_rows(name, x):
    def body(x_ref, s_ref, d_ref):
        xx = x_ref[...]
        sg = jax.nn.sigmoid(xx)
        s_ref[...] = (xx * sg).astype(BF16)
        d_ref[...] = sg * (1.0 + xx * (1.0 - sg))

    return pl.pallas_call(body, name=name, out_shape=[SDS(x.shape, BF16), SDS(x.shape, F32)])(x)


def _sum_rows(name, x, scale=None):
    r, n = x.shape
    tn = _tile(n, 32768, 128)

    def body(*refs):
        acc = jnp.sum(refs[0][...].astype(F32), axis=0, keepdims=True)
        if scale is not None:
            acc = acc * refs[1][...]
        refs[-1][...] = acc

    in_specs = [pl.BlockSpec((r, tn), lambda j: (0, j))]
    args = [x]
    if scale is not None:
        in_specs.append(pl.BlockSpec((1, tn), lambda j: (0, j)))
        args.append(scale)
    return pl.pallas_call(body, name=name, grid=(n // tn,), in_specs=in_specs,
                          out_specs=pl.BlockSpec((1, tn), lambda j: (0, j)), out_shape=SDS((1, n), F32))(*args)


def _me_operand(me):
    return jnp.reshape(me, (1,)).astype(jnp.int32)


def _sum_slots(name, slots, src, me):
    n_slots, r, c = slots.shape
    tr = _tile(r, 432, 16)

    def body(me_ref, own_ref, x_ref, o_ref):
        acc = own_ref[...].astype(F32)
        for sl in range(n_slots):
            acc = acc + x_ref[sl].astype(F32)
        o_ref[...] = acc

    grid_spec = pltpu.PrefetchScalarGridSpec(
        num_scalar_prefetch=1, grid=(r // tr,),
        in_specs=[pl.BlockSpec((None, tr, c), lambda i, me_ref: (me_ref[0], i, 0)),
                  pl.BlockSpec((n_slots, tr, c), lambda i, me_ref: (0, i, 0))],
        out_specs=pl.BlockSpec((tr, c), lambda i, me_ref: (i, 0)))
    return pl.pallas_call(body, name=name, grid_spec=grid_spec, out_shape=SDS((r, c), F32),
                          compiler_params=_cparams())(_me_operand(me), src, slots)


def _adamw(name, w, g, m, v):
    shape = w.shape
    cols = shape[-1]
    rows = w.size // cols
    tr = _tile(rows, 512, 8)
    bc1 = 1.0 - ADAM_B1 ** ADAM_STEP
    bc2 = 1.0 - ADAM_B2 ** ADAM_STEP

    def body(w_ref, g_ref, m_ref, v_ref, d_ref, nm_ref, nv_ref):
        gg = g_ref[...]
        nm = ADAM_B1 * m_ref[...] + (1.0 - ADAM_B1) * gg
        nv = ADAM_B2 * v_ref[...] + (1.0 - ADAM_B2) * (gg * gg)
        nm_ref[...] = nm
        nv_ref[...] = nv
        d_ref[...] = -ADAM_LR * ((nm / bc1) / (jnp.sqrt(nv / bc2) + ADAM_EPS) + ADAM_WD * w_ref[...])

    spec = pl.BlockSpec((tr, cols), lambda i: (i, 0))
    outs = pl.pallas_call(body, name=name, grid=(rows // tr,), in_specs=[spec] * 4, out_specs=[spec] * 3,
                          out_shape=[SDS((rows, cols), F32)] * 3, compiler_params=_cparams())(
        w.reshape(rows, cols), g.reshape(rows, cols), m.reshape(rows, cols), v.reshape(rows, cols))
    return tuple(t.reshape(shape) for t in outs)


def _exchange(name, x, scatter, after=None):
    blk = x.shape[1:] if scatter else x.shape
    extra = [] if after is None else [after]

    def body(x_ref, *rest):
        out_ref, send_sems, recv_sems, local_sem = rest[len(extra):]
        mx, my, mc = lax.axis_index("x"), lax.axis_index("y"), lax.axis_index("c")
        me = 4 * mx + 2 * my + mc
        own = pltpu.make_async_copy(x_ref.at[me] if scatter else x_ref, out_ref.at[me], local_sem)
        own.start()
        copies = []
        for kk in range(1, N_DEV):
            px = jnp.bitwise_xor(mx, (kk >> 2) & 1)
            py = jnp.bitwise_xor(my, (kk >> 1) & 1)
            pc = jnp.bitwise_xor(mc, kk & 1)
            peer = 4 * px + 2 * py + pc
            send = pltpu.make_async_remote_copy(
                src_ref=x_ref.at[peer] if scatter else x_ref, dst_ref=out_ref.at[me],
                send_sem=send_sems.at[kk - 1], recv_sem=recv_sems.at[kk - 1],
                device_id=(px, py, pc), device_id_type=MESH)
            send.start()
            arrival = pltpu.make_async_remote_copy(
                src_ref=x_ref.at[peer] if scatter else x_ref, dst_ref=out_ref.at[peer],
                send_sem=send_sems.at[kk - 1], recv_sem=recv_sems.at[kk - 1],
                device_id=(px, py, pc), device_id_type=MESH)
            copies.append((send, arrival))
        for send, arrival in copies:
            arrival.wait_recv()
            send.wait_send()
        own.wait()

    return pl.pallas_call(
        body, name=name, out_shape=SDS((N_DEV,) + tuple(blk), x.dtype),
        in_specs=[pl.BlockSpec(memory_space=pl.ANY)] * (1 + len(extra)), out_specs=pl.BlockSpec(memory_space=pl.ANY),
        scratch_shapes=[pltpu.SemaphoreType.DMA((N_DEV - 1,)), pltpu.SemaphoreType.DMA((N_DEV - 1,)),
                        pltpu.SemaphoreType.DMA],
    )(x, *extra)


def _rope_perm(pre, reps, post):
    half = QK_ROPE // 4
    width = reps * (pre + QK_ROPE) + post
    p = np.zeros((width, width), np.float32)
    for rep in range(reps):
        s0 = rep * (pre + QK_ROPE) + pre
        for base in (s0, s0 + 2 * half):
            for i in range(half):
                p[base + half + i, base + i] = -1.0
                p[base + i, base + half + i] = 1.0
    return p


def _rope_tables(n_lat, t_rows, pre, reps, post):
    half = QK_ROPE // 4
    pos = jnp.arange(n_lat)
    freqs = jnp.power(ROPE_THETA, -jnp.arange(0, 2 * half, 2, dtype=F32) / (2 * half))
    ang_r = (pos // GRID_W).astype(F32)[:, None] * freqs
    ang_c = (pos % GRID_W).astype(F32)[:, None] * freqs
    ang = jnp.concatenate([ang_r, ang_r, ang_c, ang_c], axis=-1)

    def table(fn, plain):
        slot = jnp.concatenate([jnp.full((n_lat, pre), plain, F32), fn(ang)], axis=-1)
        t = jnp.concatenate([jnp.tile(slot, (1, reps)), jnp.full((n_lat, post), plain, F32)], axis=-1)
        return jnp.concatenate([t, jnp.full((t_rows - n_lat, t.shape[1]), plain, F32)], axis=0)

    return table(jnp.cos, 1.0), table(jnp.sin, 0.0)


def _ffn_half_fwd(tag, s, mg, k, feed, i, coef, n_lat):
    wg_t, wu_t = feed.weights(f"{tag}_up", [f"gate_t{i}", f"up_t{i}"], s)
    u, a, b, hid = _ffn_up(f"{tag}_up", s, mg, k, n_lat, wg_t, wu_t)
    (wd,) = feed.weights(f"{tag}_down", [f"down{i}"], hid)
    s_out, o = _mm_resid(f"{tag}_down", hid, wd, s, mg, k, coef, n_lat)
    return s_out, (s, u, a, b, hid, o, wg_t, wu_t, wd)


def _ffn_half_bwd(tag, ds_out, saved, mg, k, feed, i, coef, n_lat, out_rows=None):
    s, u, a, b, hid, o, wg_t, wu_t, wd = saved
    do, da, db, dgate = _ffn_dact(f"{tag}_dact", ds_out, o, mg, k, coef, n_lat, wd, a, b)
    dwd = _mm(f"{tag}_dwd", [(hid, do)], "tn", BF16)
    dwg_t, dwu_t = _dw_pair(f"{tag}_dwgu", da, db, u)
    token = feed.grads(tag, {f"down{i}": dwd, f"gate_t{i}": dwg_t, f"up_t{i}": dwu_t})
    ds_in, (dshift, dscale, dgain) = _du_adaln(f"{tag}_du", [(da, wg_t), (db, wu_t)], s, ds_out, mg, k, n_lat,
                                               _after(token), out_rows)
    return ds_in, dict(shift=dshift, scale=dscale, gate=dgate, gain=dgain)


def _after(token):
    return jnp.zeros((1, D_MODEL), F32) + token


def _mod_grad(parts, n_groups):
    rows = []
    zero = jnp.zeros((n_groups, 1, D_MODEL), F32)
    for k in range(3):
        for nm in ("shift", "scale", "gate"):
            t = parts[k].get(nm, zero)
            if t.shape[0] < n_groups:
                t = jnp.concatenate([t, jnp.zeros((n_groups - t.shape[0], 1, D_MODEL), F32)], axis=0)
            rows.append(t)
    return jnp.concatenate(rows, axis=1).reshape(n_groups, N_MOD * D_MODEL)


def _local_step(x, ctx, target, mod_h, mod_g, norm_g, feed, pool_w, pool_scale, q_norm_g, kv_norm_g, conv_w,
                final_norm_g):
    n_lat, n_ctx = x.shape[0], ctx.shape[0]
    t_all = n_lat + n_ctx
    mg0 = jnp.stack([jnp.concatenate([mod_h[0], norm_g[0]], axis=0), jnp.concatenate([mod_g, norm_g[0]], axis=0)])
    mg1 = jnp.concatenate([mod_h[1], norm_g[1]], axis=0)[None]

    s0 = jnp.concatenate([x, ctx], axis=0)
    s1, sv_f00 = _ffn_half_fwd("l0f0", s0, mg0, 0, feed, 0, 0.5, n_lat)

    w_in, w_uq, w_ukv_t, w_ab_out = feed.weights("l0m", ["in_t", "uq", "ukv_t", "ab_out"], s1)
    kv_rows = KV_RANK + QK_ROPE
    w_in_t = jnp.concatenate([
        w_in[:POOL_DIM], jnp.zeros((PA_CQ - POOL_DIM, D_MODEL), BF16), w_in[POOL_DIM:POOL_DIM + Q_RANK],
        w_in[POOL_DIM + Q_RANK:], jnp.zeros((PA_KV_W - kv_rows, D_MODEL), BF16)], axis=0)
    ua, proj = _adaln_mm("l0m_proj", s1, mg0, 1, n_lat, w_in_t)
    pool_y, pool_p = _pool_fwd("l0m_pool", proj, n_lat, pool_w.astype(BF16), pool_scale)
    nq = _rmsnorm_fwd("l0m_qnorm", proj, Q_RANK, PA_CQ // Q_RANK, q_norm_g, n_lat)
    q_lin = _mm("l0m_q", [(nq, w_uq)], "nn", F32, 512, 768)
    cos_q, sin_q = _rope_tables(n_lat, n_lat, QK_NOPE, HEADS, 0)
    perm_q = _rope_perm(QK_NOPE, HEADS, 0)
    q_rot = _rope("l0m_qrope", q_lin, Q_RANK, 0, cos_q, sin_q, jnp.asarray(perm_q, BF16), False, BF16)
    cos_k, sin_k = _rope_tables(n_lat, t_all, KV_RANK, 1, PA_KV_W - kv_rows)
    perm_k = _rope_perm(KV_RANK, 1, PA_KV_W - kv_rows)
    kvr = _rope("l0m_krope", proj, PA_KV_W, PA_KV // PA_KV_W, cos_k, sin_k, jnp.asarray(perm_k, BF16), False, F32)
    nkv = _rmsnorm_fwd("l0m_kvnorm", kvr, KV_RANK, 0, kv_norm_g, t_all)
    kv = _mm("l0m_kv", [(nkv, w_ukv_t)], "nt", BF16, 768, 512)
    qh = jnp.pad(q_rot.reshape(n_lat, HEADS, QK_HEAD), ((0, 0), (0, 0), (0, HEAD_PAD - QK_HEAD))).transpose(1, 0, 2)
    kvh = kv.reshape(t_all, HEADS, QK_NOPE + V_HEAD)
    k_rope = jnp.broadcast_to(kvr[:, None, KV_RANK:KV_RANK + QK_ROPE].astype(BF16), (t_all, HEADS, QK_ROPE))
    kh = jnp.concatenate([kvh[:, :, :QK_NOPE], k_rope, jnp.zeros((t_all, HEADS, HEAD_PAD - QK_HEAD), BF16)],
                         axis=-1).transpose(1, 0, 2)
    vh = kvh[:, :, QK_NOPE:].transpose(1, 0, 2)
    oh, lse = _attn_fwd("l0m_attn", qh, kh, vh)
    cat = jnp.concatenate([pool_y, oh.transpose(1, 0, 2).reshape(n_lat, HEADS * V_HEAD)], axis=-1)
    h1 = s1[:n_lat]
    h2, mix_o = _mm_resid("l0m_out", cat, w_ab_out, h1, mg0[:1], 1, 1.0, n_lat)

    h3, sv_f01 = _ffn_half_fwd("l0f1", h2, mg0[:1], 2, feed, 1, 0.5, n_lat)

    h4, sv_f10 = _ffn_half_fwd("l1f0", h3, mg1, 0, feed, 2, 0.5, n_lat)
    w_cin_t, w_c_out = feed.weights("l1m", ["cin_t", "c_out"], h4)
    uc, z3 = _adaln_mm("l1m_in", h4, mg1, 1, n_lat, w_cin_t)
    yc = _conv_fwd("l1m_conv", z3, conv_w)
    h5, conv_o = _mm_resid("l1m_out", yc, w_c_out, h4, mg1, 1, 1.0, n_lat)
    h6, sv_f11 = _ffn_half_fwd("l1f1", h5, mg1, 2, feed, 3, 0.5, n_lat)

    dh6, sq_cols, d_final_g = _final_loss("loss_head", h6, target, final_norm_g)
    g = {}
    dh5, g["f11"] = _ffn_half_bwd("l1f1", dh6, sv_f11, mg1, 2, feed, 3, 0.5, n_lat)

    do_c, dyc, dgate_c = _gate_mm("l1m_dy", dh5, conv_o, mg1, 1, 1.0, n_lat, w_c_out)
    d_c_out = _mm("l1m_dwout", [(yc, do_c)], "tn", BF16)
    db_, dc_, dv_, d_conv_w = _conv_bwd("l1m_dconv", dyc, z3, conv_w)
    dz3 = jnp.concatenate([db_, dc_, dv_], axis=-1)
    d_cin_t = _mm("l1m_dwin", [(dz3, uc)], "tn", BF16)
    token = feed.grads("l1m", {"c_out": d_c_out, "cin_t": d_cin_t})
    dh4, (dsh_c, dsc_c, dgn_c) = _du_adaln("l1m_du", [(dz3, w_cin_t)], h4, dh5, mg1, 1, n_lat, _after(token))
    dh3, g["f10"] = _ffn_half_bwd("l1f0", dh4, sv_f10, mg1, 0, feed, 2, 0.5, n_lat)

    dh2, g["f01"] = _ffn_half_bwd("l0f1", dh3, sv_f01, mg0[:1], 2, feed, 1, 0.5, n_lat)

    do_a, dcat, dgate_a = _gate_mm("l0m_dcat", dh2, mix_o, mg0[:1], 1, 1.0, n_lat, w_ab_out)
    d_ab_out = _mm("l0m_dwout", [(cat, do_a)], "tn", BF16)
    d_pool_x, d_pool_w, d_pool_scale = _pool_bwd("l0m_dpool", dcat, n_lat, pool_p, pool_w.astype(BF16), pool_scale)
    doh = dcat[:, POOL_DIM:].reshape(n_lat, HEADS, V_HEAD).transpose(1, 0, 2).astype(BF16)
    dqh, dkh, dvh = _attn_bwd("l0m_dattn", qh, kh, vh, oh, lse, doh)
    dk_rope = _sum_rows("l0m_dksum", dkh[:, :, QK_NOPE:QK_HEAD].reshape(HEADS, t_all * QK_ROPE))
    dq_rot = dqh[:, :, :QK_HEAD].transpose(1, 0, 2).reshape(n_lat, Q_RANK)
    dq_lin = _rope("l0m_dqrope", dq_rot, Q_RANK, 0, cos_q, sin_q, jnp.asarray(perm_q.T, BF16), True, BF16)
    d_uq = _mm("l0m_dwuq", [(nq, dq_lin)], "tn", BF16, 768, 768)
    dnq = _mm("l0m_dnq", [(dq_lin, w_uq)], "nt", F32, 512, 768)
    dcq, d_q_norm_g = _rmsnorm_bwd("l0m_dqnorm", proj, Q_RANK, PA_CQ // Q_RANK, dnq, q_norm_g, n_lat, BF16)
    dkv = jnp.concatenate([dkh[:, :, :QK_NOPE], dvh], axis=-1).transpose(1, 0, 2).reshape(t_all, HEADS * HEAD_PAD)
    dkv = dkv.astype(BF16)
    dnkv = _mm("l0m_dnkv", [(dkv, w_ukv_t)], "nn", F32, 768, 256)
    d_ukv_t = _mm("l0m_dwukv", [(dkv, nkv)], "tn", BF16, 512, 256)
    dckv, d_kv_norm_g = _rmsnorm_bwd("l0m_dkvnorm", kvr, KV_RANK, 0, dnkv, kv_norm_g, t_all)
    dkvr = jnp.concatenate([dckv, dk_rope.reshape(t_all, QK_ROPE),
                            jnp.zeros((t_all, PA_KV_W - KV_RANK - QK_ROPE), F32)], axis=-1)
    dpb = _rope("l0m_dkrope", dkvr, PA_KV_W, 0, cos_k, sin_k, jnp.asarray(perm_k.T, BF16), True, BF16)
    dproj_lat = jnp.concatenate([d_pool_x, jnp.zeros((n_lat, PA_CQ - POOL_DIM), BF16), dcq, dpb[:n_lat]], axis=-1)
    dproj_ctx = jnp.concatenate([jnp.zeros((n_ctx, PA_KV), BF16), dpb[n_lat:]], axis=-1)
    dproj = jnp.concatenate([dproj_lat, dproj_ctx], axis=0)
    d_in_pad = _mm("l0m_dwin", [(dproj, ua)], "tn", BF16, 640, 512)
    d_in_t = jnp.concatenate([d_in_pad[:POOL_DIM], d_in_pad[PA_CQ:PA_CQ + Q_RANK],
                              d_in_pad[PA_KV:PA_KV + kv_rows]], axis=0)
    token = feed.grads("l0m", {"ab_out": d_ab_out, "uq": d_uq, "ukv_t": d_ukv_t, "in_t": d_in_t})
    ds1, (dsh_a, dsc_a, dgn_a) = _du_adaln("l0m_du", [(dproj, w_in_t)], s1, dh2, mg0, 1, n_lat, _after(token))
    grad_x, g["f00"] = _ffn_half_bwd("l0f0", ds1, sv_f00, mg0, 0, feed, 0, 0.5, n_lat, out_rows=n_lat)

    dmod0 = _mod_grad([g["f00"], dict(shift=dsh_a, scale=dsc_a, gate=dgate_a), g["f01"]], 2)
    dmod1 = _mod_grad([g["f10"], dict(shift=dsh_c, scale=dsc_c, gate=dgate_c), g["f11"]], 1)
    d_norm_g = jnp.stack([
        jnp.concatenate([jnp.sum(g["f00"]["gain"], axis=0), jnp.sum(dgn_a, axis=0), g["f01"]["gain"][0]], axis=0),
        jnp.concatenate([g["f10"]["gain"][0], dgn_c[0], g["f11"]["gain"][0]], axis=0)])
    grads = dict(
        pool_w=d_pool_w, pool_scale=d_pool_scale, q_norm_g=d_q_norm_g[0], kv_norm_g=d_kv_norm_g[0],
        conv_w=d_conv_w, final_norm_g=d_final_g[0], norm_g=d_norm_g,
        mod_h=jnp.stack([dmod0[0], dmod1[0]]), mod_g=dmod0[1])
    return sq_cols, grad_x, grads


HBM_SPEC = pl.BlockSpec(memory_space=pltpu.HBM)
SEM_SPEC = pl.BlockSpec(memory_space=pltpu.SEMAPHORE)
ANY_SPEC = pl.BlockSpec(memory_space=pl.ANY)
SIDE_EFFECT = pltpu.SideEffectType.DATAFLOW_SIDE_EFFECTING
N_PEERS = N_DEV - 1


def _mesh_place():
    mx, my, mc = lax.axis_index("x"), lax.axis_index("y"), lax.axis_index("c")
    return mx, my, mc, 4 * mx + 2 * my + mc


def _peer(place, kk):
    mx, my, mc, _ = place
    px = jnp.bitwise_xor(mx, (kk >> 2) & 1)
    py = jnp.bitwise_xor(my, (kk >> 1) & 1)
    pc = jnp.bitwise_xor(mc, kk & 1)
    return (px, py, pc), 4 * px + 2 * py + pc


def _hbm(a):
    return pltpu.with_memory_space_constraint(a, pltpu.HBM)


def _landing(block, me):
    zone = lax.empty((N_DEV,) + block.shape, block.dtype)
    return lax.dynamic_update_slice(zone, block[None], (me,) + (0,) * block.ndim)


ALL_PEERS = tuple(range(1, N_DEV))
SIBLING = 1
CHIP_PEERS = (2, 4, 6)
RELAYED = (3, 5, 7)


def _exchange_start(name, srcs, lands, scatter, after, peers=ALL_PEERS):
    n = len(srcs)
    extra = [] if after is None else [after]

    def body(*refs):
        src, land = refs[:n], refs[n:2 * n]
        send_sems, recv_sems, token = refs[2 * n + len(extra)], refs[2 * n + len(extra) + 1], refs[-1]
        place = _mesh_place()
        for a in range(n):
            for kk in peers:
                dev, peer = _peer(place, kk)
                pltpu.make_async_remote_copy(
                    src_ref=src[a].at[peer] if scatter else src[a],
                    dst_ref=land[a].at[kk - 1] if scatter else land[a].at[place[3]],
                    send_sem=send_sems.at[a * N_PEERS + kk - 1], recv_sem=recv_sems.at[a * N_PEERS + kk - 1],
                    device_id=dev, device_id_type=MESH).start()
        token[...] = jnp.zeros_like(token)

    thru = [pltpu.HBM(t.shape, t.dtype) for t in (*srcs, *lands)]
    res = pl.pallas_call(
        body, name=name,
        out_shape=(pltpu.SemaphoreType.DMA((n * N_PEERS,)), pltpu.SemaphoreType.DMA((n * N_PEERS,)), *thru,
                   SDS((8, 128), F32)),
        in_specs=[HBM_SPEC] * (2 * n) + [ANY_SPEC] * len(extra),
        out_specs=(SEM_SPEC, SEM_SPEC, *([HBM_SPEC] * (2 * n)), pl.BlockSpec(memory_space=pltpu.VMEM)),
        input_output_aliases={i: 2 + i for i in range(2 * n)},
        compiler_params=pltpu.CompilerParams(has_side_effects=SIDE_EFFECT),
    )(*[_hbm(s) for s in srcs], *[_hbm(t) for t in lands], *extra)
    return res[0], res[1], list(res[2:2 + n]), list(res[2 + n:2 + 2 * n]), res[-1]


def _exchange_wait(name, send_sems, recv_sems, srcs, lands, places, scatter, after):
    n = len(srcs)

    def body(*refs):
        src, land = refs[:n], refs[n:2 * n]
        send, recv = refs[2 * n], refs[2 * n + 1]
        place = _mesh_place()
        for a in range(n):
            for kk in range(1, N_DEV):
                dev, peer = _peer(place, kk)
                cp = pltpu.make_async_remote_copy(
                    src_ref=src[a].at[peer] if scatter else src[a],
                    dst_ref=land[a].at[kk - 1] if scatter else land[a].at[peer],
                    send_sem=send.at[places[a] * N_PEERS + kk - 1], recv_sem=recv.at[places[a] * N_PEERS + kk - 1],
                    device_id=dev, device_id_type=MESH)
                cp.wait_send()
                cp.wait_recv()

    thru = [pltpu.HBM(t.shape, t.dtype) for t in (*srcs, *lands)]
    res = pl.pallas_call(
        body, name=name, out_shape=tuple(thru),
        in_specs=[HBM_SPEC] * (2 * n) + [SEM_SPEC, SEM_SPEC] + [ANY_SPEC] * len(after),
        out_specs=tuple([HBM_SPEC] * (2 * n)), input_output_aliases={i: i for i in range(2 * n)},
        compiler_params=pltpu.CompilerParams(has_side_effects=SIDE_EFFECT),
    )(*srcs, *lands, send_sems, recv_sems, *after)
    return list(res[:n]), list(res[n:])


def _gather_relay(name, send1, recv1, lands, places, after):
    n = len(lands)

    def body(*refs):
        land, s1, r1 = refs[:n], refs[n], refs[n + 1]
        s2, r2 = refs[n + 3], refs[n + 4]
        place = _mesh_place()
        sibling = _peer(place, SIBLING)[0]
        for a in range(n):
            for j, kk in enumerate(CHIP_PEERS):
                dev, origin = _peer(place, kk)
                block = land[a].at[origin]
                pltpu.make_async_remote_copy(
                    src_ref=block, dst_ref=block, send_sem=s1.at[places[a] * N_PEERS + kk - 1],
                    recv_sem=r1.at[places[a] * N_PEERS + kk - 1], device_id=dev, device_id_type=MESH).wait_recv()
                pltpu.make_async_remote_copy(
                    src_ref=block, dst_ref=block, send_sem=s2.at[a * 3 + j], recv_sem=r2.at[a * 3 + j],
                    device_id=sibling, device_id_type=MESH).start()

    res = pl.pallas_call(
        body, name=name,
        out_shape=(pltpu.SemaphoreType.DMA((3 * n,)), pltpu.SemaphoreType.DMA((3 * n,)),
                   *[pltpu.HBM(t.shape, t.dtype) for t in lands]),
        in_specs=[HBM_SPEC] * n + [SEM_SPEC, SEM_SPEC, ANY_SPEC],
        out_specs=(SEM_SPEC, SEM_SPEC, *([HBM_SPEC] * n)),
        input_output_aliases={i: 2 + i for i in range(n)},
        compiler_params=pltpu.CompilerParams(has_side_effects=SIDE_EFFECT),
    )(*lands, send1, recv1, after)
    return res[0], res[1], list(res[2:])


def _gather_wait(name, send1, recv1, send2, recv2, srcs, lands, places, after):
    n = len(lands)

    def body(*refs):
        src, land = refs[:n], refs[n:2 * n]
        s1, r1, s2, r2 = refs[2 * n:2 * n + 4]
        place = _mesh_place()
        for a in range(n):
            for kk in (SIBLING,) + CHIP_PEERS:
                dev, origin = _peer(place, kk)
                first = pltpu.make_async_remote_copy(
                    src_ref=src[a], dst_ref=land[a].at[origin], send_sem=s1.at[places[a] * N_PEERS + kk - 1],
                    recv_sem=r1.at[places[a] * N_PEERS + kk - 1], device_id=dev, device_id_type=MESH)
                first.wait_send()
                if kk == SIBLING:
                    first.wait_recv()
            for j, kk in enumerate(CHIP_PEERS):
                dev, origin = _peer(place, kk + 1)
                relay = pltpu.make_async_remote_copy(
                    src_ref=src[a], dst_ref=land[a].at[origin], send_sem=s2.at[a * 3 + j], recv_sem=r2.at[a * 3 + j],
                    device_id=dev, device_id_type=MESH)
                relay.wait_send()
                relay.wait_recv()

    arrays = (*srcs, *lands)
    res = pl.pallas_call(
        body, name=name, out_shape=tuple(pltpu.HBM(t.shape, t.dtype) for t in arrays),
        in_specs=[HBM_SPEC] * (2 * n) + [SEM_SPEC] * 4 + [ANY_SPEC], out_specs=tuple([HBM_SPEC] * (2 * n)),
        input_output_aliases={i: i for i in range(2 * n)},
        compiler_params=pltpu.CompilerParams(has_side_effects=SIDE_EFFECT),
    )(*arrays, send1, recv1, send2, recv2, after)
    return list(res[n:])


class _Feed:
    def __init__(self, shards, groups, me):
        self.shards, self.groups, self.me, self.pos = shards, groups, me, 0
        self.sems, self.srcs, self.lands = {}, {}, {}
        self.relays = {}
        self.pending = []

    def start(self, tag, names, after):
        srcs = [self.shards[nm] for nm in names]
        lands = [_landing(s, self.me) for s in srcs]
        send, recv, srcs, lands, self.token = _exchange_start(
            f"gather_start_{tag}", srcs, lands, False, after, (SIBLING,) + CHIP_PEERS)
        for i, nm in enumerate(names):
            self.sems[nm], self.srcs[nm], self.lands[nm] = (send, recv, i), srcs[i], lands[i]
        return self.token

    def _relay(self, gi, after):
        names = self.groups[gi]
        if gi not in self.relays:
            send, recv, _ = self.sems[names[0]]
            places = [self.sems[nm][2] for nm in names]
            send2, recv2, lands = _gather_relay(f"gather_relay_{gi}", send, recv, [self.lands[nm] for nm in names],
                                                places, after)
            for nm, t in zip(names, lands):
                self.lands[nm] = t
            self.relays[gi] = (send2, recv2)
            after = lands[0]
        return after

    def weights(self, tag, names, after):
        gi = self.pos
        assert names == self.groups[gi], (names, self.groups[gi])
        if gi == 0:
            after = self.token
        self._relay(gi, after)
        if 1 <= gi < len(self.groups) - 1:
            after = self._relay(gi + 1, after)
        send2, recv2 = self.relays[gi]
        send, recv, _ = self.sems[names[0]]
        got = _gather_wait(f"gather_wait_{tag}", send, recv, send2, recv2, [self.srcs[nm] for nm in names],
                           [self.lands[nm] for nm in names], [self.sems[nm][2] for nm in names], after)
        self.pos += 1
        return [t.reshape((N_DEV * t.shape[1],) + t.shape[2:]) for t in got]

    def grads(self, tag, full):
        names = list(full)
        srcs = [full[nm].reshape((N_DEV, full[nm].shape[0] // N_DEV) + full[nm].shape[1:]) for nm in names]
        lands = [lax.empty((N_PEERS,) + s.shape[1:], s.dtype) for s in srcs]
        send, recv, srcs, lands, token = _exchange_start(f"scatter_start_{tag}", srcs, lands, True, None)
        self.pending.append((tag, names, send, recv, srcs, lands))
        return token[0, 0]

    def collect(self, tags, after, keep_slots=()):
        out = {}
        for tag, names, send, recv, srcs, lands in self.pending:
            if tag not in tags:
                continue
            srcs, got = _exchange_wait(f"scatter_wait_{tag}", send, recv, srcs, lands, list(range(len(names))), True,
                                       after)
            for nm, slots, src in zip(names, got, srcs):
                out[nm] = ((slots, src) if nm.startswith(tuple(keep_slots))
                           else _sum_slots(f"reduce_{nm}", slots, src, self.me))
        return out


def _adamw_math(w, gg, m, v):
    nm = ADAM_B1 * m + (1.0 - ADAM_B1) * gg
    nv = ADAM_B2 * v + (1.0 - ADAM_B2) * (gg * gg)
    bc1 = 1.0 - ADAM_B1 ** ADAM_STEP
    bc2 = 1.0 - ADAM_B2 ** ADAM_STEP
    return -ADAM_LR * ((nm / bc1) / (jnp.sqrt(nv / bc2) + ADAM_EPS) + ADAM_WD * w), nm, nv


def _adamw_part(name, i, w, scattered, me, m, v, prev):
    n_parts, rows, cols = w.shape
    tr = _tile(rows, 256, 16)
    if prev is None:
        prev = tuple(lax.empty(w.shape, F32) for _ in range(4))

    slots, src = scattered

    def body(me_ref, w_ref, g_ref, own_ref, m_ref, v_ref, *rest):
        go_ref, d_ref, nm_ref, nv_ref = rest[4:]
        gg = own_ref[...].astype(F32)
        for sl in range(N_PEERS):
            gg = gg + g_ref[sl].astype(F32)
        d, nm, nv = _adamw_math(w_ref[...], gg, m_ref[...], v_ref[...])
        go_ref[...] = gg
        d_ref[...] = d
        nm_ref[...] = nm
        nv_ref[...] = nv

    part = pl.BlockSpec((None, tr, cols), lambda r, me_ref: (i, r, 0))
    grid_spec = pltpu.PrefetchScalarGridSpec(
        num_scalar_prefetch=1, grid=(rows // tr,),
        in_specs=[part, pl.BlockSpec((N_PEERS, tr, cols), lambda r, me_ref: (0, r, 0)),
                  pl.BlockSpec((None, tr, cols), lambda r, me_ref: (me_ref[0], r, 0)), part, part] + [ANY_SPEC] * 4,
        out_specs=[part] * 4)
    return pl.pallas_call(
        body, name=name, grid_spec=grid_spec, out_shape=[SDS(w.shape, F32)] * 4,
        input_output_aliases={6 + k: k for k in range(4)}, compiler_params=_cparams(),
    )(_me_operand(me), w, slots, src, m, v, *prev)


WEIGHT_NAMES = ("c_ctx", "norm_g", "w_mod", "b_mod", "ffn_w_gate", "ffn_w_up", "ffn_w_down", "ab_w_in", "pool_w",
                "pool_scale", "q_norm_g", "w_uq", "kv_norm_g", "w_ukv", "ab_w_out", "conv_w_in", "conv_w",
                "conv_w_out", "final_norm_g")


def kernel(x, c, ctx, c_ctx, norm_g, w_mod, b_mod, ffn_w_gate, ffn_w_up, ffn_w_down, ab_w_in, pool_w, pool_scale, q_norm_g, w_uq, kv_norm_g, w_ukv, ab_w_out, conv_w_in, conv_w, conv_w_out, final_norm_g, loss_target, m_c_ctx, m_norm_g, m_w_mod, m_b_mod, m_ffn_w_gate, m_ffn_w_up, m_ffn_w_down, m_ab_w_in, m_pool_w, m_pool_scale, m_q_norm_g, m_w_uq, m_kv_norm_g, m_w_ukv, m_ab_w_out, m_conv_w_in, m_conv_w, m_conv_w_out, m_final_norm_g, v_c_ctx, v_norm_g, v_w_mod, v_b_mod, v_ffn_w_gate, v_ffn_w_up, v_ffn_w_down, v_ab_w_in, v_pool_w, v_pool_scale, v_q_norm_g, v_w_uq, v_kv_norm_g, v_w_ukv, v_ab_w_out, v_conv_w_in, v_conv_w, v_conv_w_out, v_final_norm_g):
    weights = (c_ctx, norm_g, w_mod, b_mod, ffn_w_gate, ffn_w_up, ffn_w_down, ab_w_in, pool_w, pool_scale, q_norm_g,
               w_uq, kv_norm_g, w_ukv, ab_w_out, conv_w_in, conv_w, conv_w_out, final_norm_g)
    moms = (m_c_ctx, m_norm_g, m_w_mod, m_b_mod, m_ffn_w_gate, m_ffn_w_up, m_ffn_w_down, m_ab_w_in, m_pool_w,
            m_pool_scale, m_q_norm_g, m_w_uq, m_kv_norm_g, m_w_ukv, m_ab_w_out, m_conv_w_in, m_conv_w, m_conv_w_out,
            m_final_norm_g)
    vels = (v_c_ctx, v_norm_g, v_w_mod, v_b_mod, v_ffn_w_gate, v_ffn_w_up, v_ffn_w_down, v_ab_w_in, v_pool_w,
            v_pool_scale, v_q_norm_g, v_w_uq, v_kv_norm_g, v_w_ukv, v_ab_w_out, v_conv_w_in, v_conv_w, v_conv_w_out,
            v_final_norm_g)
    me = 4 * lax.axis_index("x") + 2 * lax.axis_index("y") + lax.axis_index("c")
    n_lat, n_ctx = x.shape[1], ctx.shape[1]
    d = D_MODEL
    mod_cols = w_mod.shape[-1]
    ng_sh, cw_sh = norm_g.shape[-1], conv_w.shape[-1]

    def ffn_shards(i):
        return {f"gate_t{i}": ffn_w_gate[i // 2, i % 2].T, f"up_t{i}": ffn_w_up[i // 2, i % 2].T,
                f"down{i}": ffn_w_down[i // 2, i % 2]}

    local = {**ffn_shards(0), "in_t": ab_w_in[0].T, "uq": w_uq[0], "ukv_t": w_ukv[0].T, "ab_out": ab_w_out[0],
             **ffn_shards(1), **ffn_shards(2), "cin_t": conv_w_in[0].T, "c_out": conv_w_out[0], **ffn_shards(3)}
    ffn_groups = [[[f"gate_t{i}", f"up_t{i}"], [f"down{i}"]] for i in range(4)]
    groups = [*ffn_groups[0], ["in_t", "uq", "ukv_t", "ab_out"], *ffn_groups[1], *ffn_groups[2], ["cin_t", "c_out"],
              *ffn_groups[3]]
    feed = _Feed({nm: a.astype(BF16) for nm, a in local.items()}, groups, me)

    small = jnp.concatenate([c.reshape(-1), norm_g.reshape(-1), conv_w.reshape(-1)])
    small_n = -(-small.shape[0] // 1024) * 1024
    small = jnp.pad(small, (0, small_n - small.shape[0])).reshape(small_n // 128, 128)
    small_all = _exchange("gather_small", small, False).reshape(N_DEV, small_n)
    c_all = small_all[:, :d]
    o1 = d + 6 * ng_sh
    norm_g_full = small_all[:, d:o1].reshape(N_DEV, 2, 3, ng_sh).transpose(1, 2, 0, 3).reshape(2, 3, d)
    conv_w_full = small_all[:, o1:o1 + 3 * cw_sh].reshape(N_DEV, 3, cw_sh).transpose(1, 0, 2).reshape(3, d)

    cond = jnp.concatenate([c_all, jnp.broadcast_to(c_ctx[None, :], (N_DEV, d))], axis=0)
    sil, dsil = _silu_rows("mod_silu", cond)
    w_mod_b = w_mod.astype(BF16)
    b_sh = lax.dynamic_slice(b_mod, (0, me * mod_cols), (2, mod_cols))
    m_part = jnp.stack([_mm(f"mod_fwd{l}", [(sil, w_mod_b[l])], "nn", F32, 16, 384, bias=b_sh[l:l + 1])
                        for l in range(2)], axis=1)
    m_all = _exchange("gather_mod", m_part.reshape(-1, 128), False).reshape(N_DEV, 2 * N_DEV, 2, mod_cols)
    m_mine = lax.dynamic_index_in_dim(m_all, me, axis=1, keepdims=False)
    mod_h = m_mine.transpose(1, 0, 2).reshape(2, N_MOD, d)
    mod_g = m_all[:, N_DEV, 0, :].reshape(N_MOD, d)

    first = feed.start("first", [nm for grp in groups[:3] for nm in grp], m_all)
    feed.start("rest", [nm for grp in groups[3:] for nm in grp], first)

    sq_cols, grad_x, g = _local_step(x[0], ctx[0], loss_target[0], mod_h, mod_g, norm_g_full, feed, pool_w[0],
                                  pool_scale, q_norm_g, kv_norm_g, conv_w_full, final_norm_g)
    w_of, m_of, v_of = (dict(zip(WEIGHT_NAMES, t)) for t in (weights, moms, vels))
    results = {}

    def update(nm, grad, view=lambda t: t):
        outs = _adamw(f"adamw_{nm}", view(w_of[nm]), grad.reshape(view(w_of[nm]).shape), view(m_of[nm]), view(v_of[nm]))
        results[nm] = tuple(view(t) for t in (grad.reshape(view(w_of[nm]).shape), *outs))

    def swap(t):
        return jnp.swapaxes(t, -1, -2)

    stacked = ("gate_t", "up_t", "down")
    early = feed.collect(["l1f1", "l1m", "l1f0", "l0f1", "l0m"], [grad_x], stacked)
    update("ab_w_in", early["in_t"], swap)
    update("w_uq", early["uq"])
    update("w_ukv", early["ukv_t"].T)
    update("ab_w_out", early["ab_out"])
    update("conv_w_in", early["cin_t"].T)
    update("conv_w_out", early["c_out"])
    ffn = {}
    for nm, prefix, view in (("ffn_w_gate", "gate_t", swap), ("ffn_w_up", "up_t", swap),
                             ("ffn_w_down", "down", lambda t: t)):
        w4, m4, v4 = (view(t).reshape((4,) + view(t).shape[-2:]) for t in (w_of[nm], m_of[nm], v_of[nm]))
        prev = None
        for i in (3, 2, 1):
            prev = _adamw_part(f"adamw_{nm}{i}", i, w4, early[f"{prefix}{i}"], me, m4, v4, prev)
        ffn[nm] = (prefix, view, w4, m4, v4, prev)
    done_early = [results[nm][1] for nm in results] + [state[5][1] for state in ffn.values()]
    late = feed.collect(["l0f0"], done_early, stacked)
    for nm, (prefix, view, w4, m4, v4, prev) in ffn.items():
        outs = _adamw_part(f"adamw_{nm}0", 0, w4, late[f"{prefix}0"], me, m4, v4, prev)
        results[nm] = tuple(view(t.reshape(view(w_of[nm]).shape)) for t in outs)

    dm = jnp.stack([g["mod_h"], jnp.stack([g["mod_g"], jnp.zeros_like(g["mod_g"])])])
    dm_all = _exchange("gather_dmod", dm.reshape(-1, 128), False, results["ffn_w_down"][1]).reshape(N_DEV, 2, 2, N_MOD * d)
    grad_b_mod = _sum_rows("dmod_bias", dm_all.reshape(2 * N_DEV, 2 * N_MOD * d)).reshape(2, N_MOD * d)
    dm_sh = lax.dynamic_slice(dm_all, (0, 0, 0, me * mod_cols), (N_DEV, 2, 2, mod_cols))
    gw_mod, cctx_parts = [], []
    for l in range(2):
        dm_l = dm_sh[:, :, l, :].transpose(1, 0, 2).reshape(2 * N_DEV, mod_cols).astype(BF16)
        gw_mod.append(_mm(f"mod_dw{l}", [(sil, dm_l)], "tn", F32, 512, 384))
        dm_ctx = jnp.concatenate([dm_l[N_DEV:], jnp.zeros((N_DEV, mod_cols), BF16)], axis=0)
        cctx_parts.append(_mm(f"mod_dcond{l}", [(dm_ctx, w_mod_b[l])], "nt", F32, 16, 512))
    cctx_part = _sum_rows("mod_dcond_sum", jnp.concatenate(cctx_parts, axis=0))
    update("w_mod", jnp.stack(gw_mod))
    update("b_mod", grad_b_mod)

    small_g = jnp.concatenate([g["pool_w"].reshape(-1), g["pool_scale"].reshape(-1), g["q_norm_g"].reshape(-1),
                               g["kv_norm_g"].reshape(-1), g["final_norm_g"].reshape(-1), g["norm_g"].reshape(-1),
                               g["conv_w"].reshape(-1), sq_cols.reshape(-1), cctx_part.reshape(-1)])
    sizes = [pool_w.size, pool_scale.size, q_norm_g.size, kv_norm_g.size, d, 6 * d, 3 * d, d, d]
    sg_n = -(-small_g.shape[0] // 1024) * 1024
    small_g = jnp.pad(small_g, (0, sg_n - small_g.shape[0]))
    sg_all = _exchange("gather_small_grads", small_g.reshape(-1, 128), False).reshape(N_DEV, sg_n)
    scale_vec = jnp.concatenate([jnp.ones((1, sum(sizes[:-1])), F32), dsil[N_DEV:N_DEV + 1],
                                 jnp.ones((1, sg_n - sum(sizes)), F32)], axis=1)
    sg = _sum_rows("small_grads_sum", sg_all, scale_vec)[0]
    cuts, pos = [], 0
    for sz in sizes:
        cuts.append(sg[pos:pos + sz])
        pos += sz
    g_pool_w, g_pool_scale, g_q_norm, g_kv_norm, g_final, g_norm_full, g_conv_full, sq_all, g_c_ctx = cuts
    loss = 0.5 * jnp.sum(sq_all) / d
    update("c_ctx", g_c_ctx)
    update("norm_g", lax.dynamic_slice(g_norm_full.reshape(2, 3, d), (0, 0, me * ng_sh), (2, 3, ng_sh)))
    update("conv_w", lax.dynamic_slice(g_conv_full.reshape(3, d), (0, me * cw_sh), (3, cw_sh)))
    update("pool_w", g_pool_w)
    update("pool_scale", g_pool_scale)
    update("q_norm_g", g_q_norm)
    update("kv_norm_g", g_kv_norm)
    update("final_norm_g", g_final)
    outs = [results[nm] for nm in WEIGHT_NAMES]
    return (loss, grad_x[None], *[o[0] for o in outs], *[o[1] for o in outs], *[o[2] for o in outs],
            *[o[3] for o in outs])
```

```python
import functools
import math

import jax
import jax.numpy as jnp
import numpy as np
from jax import lax
from jax.experimental import pallas as pl
from jax.experimental.pallas import tpu as pltpu

F32 = jnp.float32
BF16 = jnp.bfloat16
MESH = pl.DeviceIdType.MESH
SDS = jax.ShapeDtypeStruct

N_DEV = 8
D_MODEL = 1024
N_MOD = 9
D_FF = 2816
POOL_WINDOWS = (2, 4, 8, 16)
POOL_DIM = 512
POOL_GROUP_DIM = 128
HEADS = 8
QK_NOPE = 64
QK_ROPE = 32
QK_HEAD = QK_NOPE + QK_ROPE
V_HEAD = 64
Q_RANK = 768
KV_RANK = 256
GRID_W = 64
ROPE_THETA = 10000.0
RMS_EPS = 1e-6
ATTN_SCALE = 1.0 / math.sqrt(QK_HEAD)
HEAD_PAD = 128
POOL_PAD = 16
PA_POOL, PA_CQ, PA_KV = 0, 768, 1536
PA_KV_W = 384
PA_W = PA_KV + PA_KV_W

ADAM_LR, ADAM_B1, ADAM_B2, ADAM_EPS, ADAM_WD, ADAM_STEP = 0.001, 0.9, 0.999, 1e-08, 0.01, 10

VMEM_LIMIT_BYTES = 56 * 1024 * 1024

NN = ((1,), (0,))
NT = ((1,), (1,))
TN = ((0,), (0,))


def _cparams():
    return pltpu.CompilerParams(vmem_limit_bytes=VMEM_LIMIT_BYTES)


def _dot(a, b, dims):
    return lax.dot_general(a, b, (dims, ((), ())), preferred_element_type=F32)


def _tile(n, cap, mult=8):
    t = (min(cap, n) // mult) * mult
    while t >= mult:
        if n % t == 0:
            return t
        t -= mult
    return n


def _colsum(x):
    return jnp.sum(x, axis=0, keepdims=True)


def _rms(x):
    r = lax.rsqrt(jnp.mean(x * x, axis=-1, keepdims=True) + RMS_EPS)
    return x * r, r


def _rms_bwd(n, r, dn):
    return r * (dn - n * jnp.mean(dn * n, axis=-1, keepdims=True))


def _rowwise(name, fn, t_rows, tm, n_lat, rows, vecs, outs, accs):
    nt = t_rows // tm
    nlt = n_lat // tm
    n_groups = 2 if nlt < nt else 1

    def grp(i):
        return jnp.where(i >= nlt, 1, 0) if n_groups == 2 else 0

    in_specs = [pl.BlockSpec((tm, w), functools.partial(lambda i, cb: (i, cb), cb=cb)) for (_, w, cb) in rows]
    in_specs += [pl.BlockSpec((1,) + v.shape[1:], lambda i: (grp(i), 0, 0)) for v in vecs]
    out_specs = [pl.BlockSpec((tm, w), lambda i: (i, 0)) for (w, _) in outs]
    out_specs += [pl.BlockSpec((1, 1, w), lambda i: (grp(i), 0, 0)) for w in accs]
    out_shape = [SDS((t_rows, w), dt) for (w, dt) in outs] + [SDS((n_groups, 1, w), F32) for w in accs]
    n_r, n_v, n_o = len(rows), len(vecs), len(outs)

    def body(*refs):
        row_vals = [r[...] for r in refs[:n_r]]
        vec_vals = [v[0] for v in refs[n_r:n_r + n_v]]
        out_refs = refs[n_r + n_v:n_r + n_v + n_o]
        acc_refs = refs[n_r + n_v + n_o:]
        out_vals, acc_vals = fn(row_vals, vec_vals)
        for o_ref, o in zip(out_refs, out_vals):
            o_ref[...] = o.astype(o_ref.dtype)
        if acc_refs:
            i = pl.program_id(0)
            first = (i == 0) | (i == nlt) if n_groups == 2 else i == 0

            @pl.when(first)
            def _():
                for a_ref, a in zip(acc_refs, acc_vals):
                    a_ref[0] = a

            @pl.when(jnp.logical_not(first))
            def _():
                for a_ref, a in zip(acc_refs, acc_vals):
                    a_ref[0] += a

    res = pl.pallas_call(
        body, name=name, grid=(nt,), in_specs=in_specs, out_specs=out_specs, out_shape=out_shape,
        compiler_params=_cparams(),
    )(*[r[0] for r in rows], *vecs)
    return res[:n_o], res[n_o:]


RESIDENT_BYTES = 12 * 1024 * 1024


def _mm(name, pairs, mode, out_dtype, tm_cap=256, tn_cap=512, bias=None):
    a0, b0 = pairs[0]
    if mode == "nn":
        m, n, dims = a0.shape[0], b0.shape[1], NN
    elif mode == "nt":
        m, n, dims = a0.shape[0], b0.shape[0], NT
    else:
        m, n, dims = a0.shape[1], b0.shape[1], TN
    b_bytes = sum(b.size * b.dtype.itemsize for _, b in pairs)
    tn = n if b_bytes <= RESIDENT_BYTES else _tile(n, tn_cap, 128)
    tm = _tile(m, tm_cap, 128 if mode == "tn" else 16)

    def a_spec(a):
        if mode == "tn":
            return pl.BlockSpec((a.shape[0], tm), lambda i, j: (0, i))
        return pl.BlockSpec((tm, a.shape[1]), lambda i, j: (i, 0))

    def b_spec(b):
        if mode == "nt":
            return pl.BlockSpec((tn, b.shape[1]), lambda i, j: (j, 0))
        return pl.BlockSpec((b.shape[0], tn), lambda i, j: (0, j))

    in_specs, flat = [], []
    for a, b in pairs:
        in_specs += [a_spec(a), b_spec(b)]
        flat += [a, b]
    if bias is not None:
        in_specs.append(pl.BlockSpec((1, tn), lambda i, j: (0, j)))
        flat.append(bias)
    n_pairs = len(pairs)

    def body(*refs):
        acc = None
        for p in range(n_pairs):
            t = _dot(refs[2 * p][...], refs[2 * p + 1][...], dims)
            acc = t if acc is None else acc + t
        if bias is not None:
            acc = acc + refs[2 * n_pairs][...]
        refs[-1][...] = acc.astype(refs[-1].dtype)

    return pl.pallas_call(
        body, name=name, grid=(m // tm, n // tn), in_specs=in_specs,
        out_specs=pl.BlockSpec((tm, tn), lambda i, j: (i, j)),
        out_shape=SDS((m, n), out_dtype), compiler_params=_cparams(),
    )(*flat)


def _mm_resid(name, a, b, s, mg, k, coef, n_lat):
    t_rows, n = a.shape[0], b.shape[1]
    tm = _tile(math.gcd(n_lat, t_rows), 256, 16)
    nlt = n_lat // tm
    n_groups = 2 if nlt < t_rows // tm else 1

    def grp(i):
        return jnp.where(i >= nlt, 1, 0) if n_groups == 2 else 0

    def body(a_ref, b_ref, s_ref, mg_ref, so_ref, o_ref):
        o = _dot(a_ref[...], b_ref[...], NN)
        gate = mg_ref[0, 3 * k + 2:3 * k + 3, :]
        o_ref[...] = o.astype(BF16)
        so_ref[...] = s_ref[...] + (coef * gate) * o

    row = pl.BlockSpec((tm, n), lambda i: (i, 0))
    return pl.pallas_call(
        body, name=name, grid=(t_rows // tm,),
        in_specs=[pl.BlockSpec((tm, a.shape[1]), lambda i: (i, 0)), pl.BlockSpec(b.shape, lambda i: (0, 0)), row,
                  pl.BlockSpec((1, mg.shape[1], n), lambda i: (grp(i), 0, 0))],
        out_specs=[row, row], out_shape=[SDS((t_rows, n), F32), SDS((t_rows, n), BF16)], compiler_params=_cparams(),
    )(a, b, s, mg)


def _dw_pair(name, a1, a2, b):
    kk, m = a1.shape
    n = b.shape[1]
    tm = _tile(m, 256, 128)

    def body(a1_ref, a2_ref, b_ref, o1_ref, o2_ref):
        bb = b_ref[...]
        o1_ref[...] = _dot(a1_ref[...], bb, TN).astype(BF16)
        o2_ref[...] = _dot(a2_ref[...], bb, TN).astype(BF16)

    col = pl.BlockSpec((kk, tm), lambda i: (0, i))
    out = pl.BlockSpec((tm, n), lambda i: (i, 0))
    return pl.pallas_call(
        body, name=name, grid=(m // tm,), in_specs=[col, col, pl.BlockSpec(b.shape, lambda i: (0, 0))],
        out_specs=[out, out], out_shape=[SDS((m, n), BF16)] * 2, compiler_params=_cparams(),
    )(a1, a2, b)


def _groups(t_rows, tm, n_lat):
    nlt = n_lat // tm
    if nlt < t_rows // tm:
        return 2, (lambda i: jnp.where(i >= nlt, 1, 0)), (lambda i: (i == 0) | (i == nlt))
    return 1, (lambda i: 0), (lambda i: i == 0)


def _accumulate(acc_refs, vals, first):
    @pl.when(first)
    def _():
        for r, v in zip(acc_refs, vals):
            r[0] = v

    @pl.when(jnp.logical_not(first))
    def _():
        for r, v in zip(acc_refs, vals):
            r[0] += v


def _adaln_math(s, m, k):
    n, _ = _rms(s)
    return (n * m[9 + k:10 + k]) * (1.0 + m[3 * k + 1:3 * k + 2]) + m[3 * k:3 * k + 1]


def _ffn_up(name, s, mg, k, n_lat, wg_t, wu_t):
    t_rows, f = s.shape[0], wg_t.shape[0]
    tm = _row_tm(t_rows, n_lat)
    _, grp, _ = _groups(t_rows, tm, n_lat)

    def body(s_ref, mg_ref, wg_ref, wu_ref, u_ref, a_ref, b_ref, h_ref):
        uu = _adaln_math(s_ref[...], mg_ref[0], k).astype(BF16)
        u_ref[...] = uu
        a = _dot(uu, wg_ref[...], NT)
        b = _dot(uu, wu_ref[...], NT)
        sg = jax.nn.sigmoid(a)
        act = a * sg
        a_ref[...] = (b * (sg * (1.0 + a * (1.0 - sg)))).astype(BF16)
        b_ref[...] = act.astype(BF16)
        h_ref[...] = (act * b).astype(BF16)

    w_spec = pl.BlockSpec(wg_t.shape, lambda i: (0, 0))
    o_spec = pl.BlockSpec((tm, f), lambda i: (i, 0))
    row = pl.BlockSpec((tm, s.shape[1]), lambda i: (i, 0))
    return pl.pallas_call(
        body, name=name, grid=(t_rows // tm,),
        in_specs=[row, pl.BlockSpec((1,) + mg.shape[1:], lambda i: (grp(i), 0, 0)), w_spec, w_spec],
        out_specs=[row, o_spec, o_spec, o_spec],
        out_shape=[SDS(s.shape, BF16)] + [SDS((t_rows, f), BF16)] * 3, compiler_params=_cparams(),
    )(s, mg, wg_t, wu_t)


def _ffn_dact(name, ds_out, o, mg, k, coef, n_lat, wd, a, b):
    t_rows, f = ds_out.shape[0], wd.shape[0]
    tm = _row_tm(t_rows, n_lat)
    n_groups, grp, first = _groups(t_rows, tm, n_lat)
    d = ds_out.shape[1]

    def body(ds_ref, o_ref, mg_ref, wd_ref, a_ref, b_ref, do_ref, da_ref, db_ref, dg_ref):
        dd = coef * ds_ref[...]
        do = (dd * mg_ref[0, 3 * k + 2:3 * k + 3, :]).astype(BF16)
        do_ref[...] = do
        _accumulate([dg_ref], [_colsum(dd * o_ref[...].astype(F32))], first(pl.program_id(0)))
        dh = _dot(do, wd_ref[...], NT)
        da_ref[...] = (dh * a_ref[...].astype(F32)).astype(BF16)
        db_ref[...] = (dh * b_ref[...].astype(F32)).astype(BF16)

    row = pl.BlockSpec((tm, d), lambda i: (i, 0))
    t_spec = pl.BlockSpec((tm, f), lambda i: (i, 0))
    return pl.pallas_call(
        body, name=name, grid=(t_rows // tm,),
        in_specs=[row, row, pl.BlockSpec((1,) + mg.shape[1:], lambda i: (grp(i), 0, 0)),
                  pl.BlockSpec(wd.shape, lambda i: (0, 0)), t_spec, t_spec],
        out_specs=[row, t_spec, t_spec, pl.BlockSpec((1, 1, d), lambda i: (grp(i), 0, 0))],
        out_shape=[SDS((t_rows, d), BF16), SDS((t_rows, f), BF16), SDS((t_rows, f), BF16), SDS((n_groups, 1, d), F32)],
        compiler_params=_cparams(),
    )(ds_out, o, mg, wd, a, b)


def _du_adaln(name, pairs, s, ds_out, mg, k, n_lat, after, out_rows=None):
    t_rows, d = s.shape
    tm = _row_tm(t_rows, n_lat)
    n_groups, grp, first = _groups(t_rows, tm, n_lat)
    n_pairs = len(pairs)
    nt, n_ds, n_out = t_rows // tm, ds_out.shape[0] // tm, (out_rows or t_rows) // tm

    def body(*refs):
        s_ref, ds_ref, mg_ref, z_ref, out_ref, dsh_ref, dsc_ref, dgn_ref = refs[2 * n_pairs:]
        i = pl.program_id(0)
        d_u = z_ref[...]
        for p in range(n_pairs):
            d_u = d_u + _dot(refs[p][...], refs[n_pairs + p][...], NN)
        m = mg_ref[0]
        gain, scale = m[9 + k:10 + k], m[3 * k + 1:3 * k + 2]
        n, r = _rms(s_ref[...])
        dxn = d_u * (1.0 + scale)
        ds_in = _rms_bwd(n, r, dxn * gain)
        ds_in = ds_in + (ds_ref[...] if n_ds == nt else jnp.where(i < n_ds, ds_ref[...], 0.0))
        if n_out == nt:
            out_ref[...] = ds_in
        else:
            @pl.when(i < n_out)
            def _():
                out_ref[...] = ds_in
        _accumulate([dsh_ref, dsc_ref, dgn_ref], [_colsum(d_u), _colsum(d_u * (n * gain)), _colsum(dxn * n)], first(i))

    row = pl.BlockSpec((tm, d), lambda i: (i, 0))
    acc = pl.BlockSpec((1, 1, d), lambda i: (grp(i), 0, 0))
    res = pl.pallas_call(
        body, name=name, grid=(t_rows // tm,),
        in_specs=[pl.BlockSpec((tm, a.shape[1]), lambda i: (i, 0)) for a, _ in pairs]
        + [pl.BlockSpec(w.shape, lambda i: (0, 0)) for _, w in pairs]
        + [row, pl.BlockSpec((tm, d), lambda i: (jnp.minimum(i, n_ds - 1), 0)),
           pl.BlockSpec((1,) + mg.shape[1:], lambda i: (grp(i), 0, 0)), pl.BlockSpec((1, d), lambda i: (0, 0))],
        out_specs=[pl.BlockSpec((tm, d), lambda i: (jnp.minimum(i, n_out - 1), 0)), acc, acc, acc],
        out_shape=[SDS((n_out * tm, d), F32)] + [SDS((n_groups, 1, d), F32)] * 3, compiler_params=_cparams(),
    )(*[a for a, _ in pairs], *[w for _, w in pairs], s, ds_out, mg, after)
    return res[0], res[1:]


def _adaln_mm(name, s, mg, k, n_lat, w_t):
    rows, d = s.shape
    tm = _row_tm(rows, n_lat)
    _, grp, _ = _groups(rows, tm, n_lat)
    n = w_t.shape[0]

    def body(s_ref, mg_ref, w_ref, u_ref, y_ref):
        uu = _adaln_math(s_ref[...], mg_ref[0], k).astype(BF16)
        u_ref[...] = uu
        y_ref[...] = _dot(uu, w_ref[...], NT)

    row = pl.BlockSpec((tm, d), lambda i: (i, 0))
    return pl.pallas_call(
        body, name=name, grid=(rows // tm,),
        in_specs=[row, pl.BlockSpec((1,) + mg.shape[1:], lambda i: (grp(i), 0, 0)), pl.BlockSpec(w_t.shape, lambda i: (0, 0))],
        out_specs=[row, pl.BlockSpec((tm, n), lambda i: (i, 0))],
        out_shape=[SDS((rows, d), BF16), SDS((rows, n), F32)], compiler_params=_cparams(),
    )(s, mg, w_t)


def _gate_mm(name, ds_out, o, mg, k, coef, n_lat, w):
    t_rows, d = ds_out.shape
    tm = _row_tm(t_rows, n_lat)
    n_groups, grp, first = _groups(t_rows, tm, n_lat)
    n = w.shape[0]

    def body(ds_ref, o_ref, mg_ref, w_ref, do_ref, y_ref, dg_ref):
        dd = coef * ds_ref[...]
        do = (dd * mg_ref[0, 3 * k + 2:3 * k + 3, :]).astype(BF16)
        do_ref[...] = do
        _accumulate([dg_ref], [_colsum(dd * o_ref[...].astype(F32))], first(pl.program_id(0)))
        y_ref[...] = _dot(do, w_ref[...], NT)

    row = pl.BlockSpec((tm, d), lambda i: (i, 0))
    return pl.pallas_call(
        body, name=name, grid=(t_rows // tm,),
        in_specs=[row, row, pl.BlockSpec((1,) + mg.shape[1:], lambda i: (grp(i), 0, 0)), pl.BlockSpec(w.shape, lambda i: (0, 0))],
        out_specs=[row, pl.BlockSpec((tm, n), lambda i: (i, 0)), pl.BlockSpec((1, 1, d), lambda i: (grp(i), 0, 0))],
        out_shape=[SDS((t_rows, d), BF16), SDS((t_rows, n), F32), SDS((n_groups, 1, d), F32)],
        compiler_params=_cparams(),
    )(ds_out, o, mg, w)


def _row_tm(t_rows, n_lat):
    return _tile(math.gcd(t_rows, n_lat), 256, 16)


def _rmsnorm_fwd(name, x, width, colblk, gain, t_rows):
    def fn(rv, vv):
        n, _ = _rms(rv[0])
        return [n * vv[0]], []

    (y,), _ = _rowwise(name, fn, t_rows, _tile(t_rows, 256, 16), t_rows, [(x, width, colblk)],
                       [gain.reshape(1, 1, width)], [(width, BF16)], [])
    return y


def _rmsnorm_bwd(name, x, width, colblk, dy, gain, t_rows, out_dtype=F32):
    def fn(rv, vv):
        n, r = _rms(rv[0])
        return [_rms_bwd(n, r, rv[1] * vv[0])], [_colsum(rv[1] * n)]

    (dx,), (dgain,) = _rowwise(name, fn, t_rows, _tile(t_rows, 256, 16), t_rows,
                               [(x, width, colblk), (dy, width, 0)], [gain.reshape(1, 1, width)],
                               [(width, out_dtype)], [width])
    return dx, dgain


def _final_loss(name, h, target, gain):
    t_rows = h.shape[0]
    inv_d = 1.0 / D_MODEL

    def fn(rv, vv):
        g = vv[0]
        n, r = _rms(rv[0])
        e = n * g - rv[1]
        dy = e * inv_d
        return [_rms_bwd(n, r, dy * g)], [_colsum(e * e), _colsum(dy * n)]

    (dh,), (sq, dgain) = _rowwise(name, fn, t_rows, _tile(t_rows, 256, 16), t_rows,
                                  [(h, D_MODEL, 0), (target, D_MODEL, 0)], [gain.reshape(1, 1, D_MODEL)],
                                  [(D_MODEL, F32)], [D_MODEL, D_MODEL])
    return dh, sq, dgain


def _exact_dot(x, m_ref):
    hi = x.astype(BF16)
    lo = (x - hi.astype(F32)).astype(BF16)
    return _dot(hi, m_ref[...], NN) + _dot(lo, m_ref[...], NN)


def _rope(name, z, width, colblk, cos32, sin32, layout, backward, out_dtype):
    t_rows = cos32.shape[0]
    expand, plain, perm = layout

    def body(z_ref, c_ref, s_ref, e_ref, m_ref, p_ref, o_ref):
        zz = z_ref[...]
        cos = _exact_dot(c_ref[...], e_ref) + m_ref[...]
        sin = _exact_dot(s_ref[...], e_ref)
        rot = _exact_dot(zz * sin if backward else zz, p_ref)
        if not backward:
            rot = rot * sin
        o_ref[...] = (zz * cos + rot).astype(o_ref.dtype)

    tm = _tile(t_rows, 256, 16)
    t_spec = pl.BlockSpec((tm, width), lambda i: (i, 0))
    f_spec = pl.BlockSpec((tm, QK_ROPE), lambda i: (i, 0))
    return pl.pallas_call(
        body, name=name, grid=(t_rows // tm,),
        in_specs=[pl.BlockSpec((tm, width), lambda i: (i, colblk)), f_spec, f_spec,
                  pl.BlockSpec((QK_ROPE, width), lambda i: (0, 0)), pl.BlockSpec((1, width), lambda i: (0, 0)),
                  pl.BlockSpec((width, width), lambda i: (0, 0))],
        out_specs=t_spec, out_shape=SDS((t_rows, width), out_dtype), compiler_params=_cparams(),
    )(z, cos32, sin32, expand, plain, perm.T if backward else perm)


def _window_sum(x, w, transposed):
    n_rows = x.shape[0]
    zeros = jnp.zeros((POOL_PAD, x.shape[1]), F32)
    y = jnp.concatenate([zeros, x, zeros], axis=0)
    total = n_rows + 2 * POOL_PAD
    if transposed:
        y = y + pltpu.roll(y, total - 1, 0)
    else:
        y = y + pltpu.roll(y, 1, 0)
    step = 1
    while 2 * step < w:
        y = pltpu.roll(y, step, 0) + pltpu.roll(y, total - step, 0)
        step *= 2
    return y[POOL_PAD:POOL_PAD + n_rows]


def _window_count(n_rows, w):
    t = lax.broadcasted_iota(jnp.int32, (n_rows, 1), 0)
    lo = jnp.maximum(t - w // 2, 0)
    hi = jnp.minimum(t + (w - w // 2 - 1), n_rows - 1)
    return (hi - lo + 1).astype(F32)


def _pool_fwd(name, proj, n_rows, w_grp, scale):
    def body(x_ref, w_ref, sc_ref, y_ref, p_ref):
        for g, w in enumerate(POOL_WINDOWS):
            cols = slice(g * POOL_GROUP_DIM, (g + 1) * POOL_GROUP_DIM)
            x = x_ref[:, cols]
            p = _window_sum(x, w, False) * (1.0 / _window_count(n_rows, w)) - x
            pb = p.astype(BF16)
            p_ref[:, cols] = pb
            y_ref[:, cols] = (_dot(pb, w_ref[g], NN) * sc_ref[:, cols]).astype(BF16)

    blk = pl.BlockSpec((n_rows, POOL_DIM), lambda i: (0, 0))
    return pl.pallas_call(
        body, name=name, grid=(1,),
        in_specs=[blk, pl.BlockSpec(w_grp.shape, lambda i: (0, 0, 0)), pl.BlockSpec((1, POOL_DIM), lambda i: (0, 0))],
        out_specs=[blk, blk], out_shape=[SDS((n_rows, POOL_DIM), BF16)] * 2, compiler_params=_cparams(),
    )(proj, w_grp, scale)


def _pool_bwd(name, dcat, n_rows, p, w_grp, scale):
    def body(dy_ref, p_ref, w_ref, sc_ref, dx_ref, dw_ref, dsc_ref):
        for g, w in enumerate(POOL_WINDOWS):
            cols = slice(g * POOL_GROUP_DIM, (g + 1) * POOL_GROUP_DIM)
            dy = dy_ref[:, cols]
            pb = p_ref[:, cols]
            pw = _dot(pb, w_ref[g], NN)
            dsc_ref[:, cols] = _colsum(dy * pw)
            dpw = (dy * sc_ref[:, cols]).astype(BF16)
            dw_ref[g] = _dot(pb, dpw, TN)
            dp = _dot(dpw, w_ref[g], NT)
            dx_ref[:, cols] = (_window_sum(dp * (1.0 / _window_count(n_rows, w)), w, True) - dp).astype(BF16)

    blk = pl.BlockSpec((n_rows, POOL_DIM), lambda i: (0, 0))
    w_spec = pl.BlockSpec(w_grp.shape, lambda i: (0, 0, 0))
    v_spec = pl.BlockSpec((1, POOL_DIM), lambda i: (0, 0))
    return pl.pallas_call(
        body, name=name, grid=(1,), in_specs=[blk, blk, w_spec, v_spec], out_specs=[blk, w_spec, v_spec],
        out_shape=[SDS((n_rows, POOL_DIM), BF16), SDS(w_grp.shape, F32), SDS((1, POOL_DIM), F32)],
        compiler_params=_cparams(),
    )(dcat, p, w_grp, scale)


def _attn_fwd(name, q, k, v):
    h, n_q, _ = q.shape
    n_k = k.shape[1]
    tq = _tile(n_q, 256, 16)

    def body(q_ref, k_ref, v_ref, o_ref, lse_ref):
        s = _dot(q_ref[...], k_ref[...], NT) * ATTN_SCALE
        m = jnp.max(s, axis=-1, keepdims=True)
        e = jnp.exp(s - m)
        l = jnp.sum(e, axis=-1, keepdims=True)
        p = (e * (1.0 / l)).astype(BF16)
        o_ref[...] = _dot(p, v_ref[...], NN).astype(BF16)
        lse_ref[...] = m + jnp.log(l)

    return pl.pallas_call(
        body, name=name, grid=(h, n_q // tq),
        in_specs=[pl.BlockSpec((None, tq, HEAD_PAD), lambda hh, i: (hh, i, 0)),
                  pl.BlockSpec((None, n_k, HEAD_PAD), lambda hh, i: (hh, 0, 0)),
                  pl.BlockSpec((None, n_k, V_HEAD), lambda hh, i: (hh, 0, 0))],
        out_specs=[pl.BlockSpec((None, tq, V_HEAD), lambda hh, i: (hh, i, 0)),
                   pl.BlockSpec((None, tq, 1), lambda hh, i: (hh, i, 0))],
        out_shape=[SDS((h, n_q, V_HEAD), BF16), SDS((h, n_q, 1), F32)], compiler_params=_cparams(),
    )(q, k, v)


def _attn_bwd(name, q, k, v, o, lse, do):
    h, n_q, _ = q.shape
    n_k = k.shape[1]
    tq = _tile(n_q, 256, 16)

    def body(q_ref, k_ref, v_ref, o_ref, lse_ref, do_ref, dq_ref, dk_ref, dv_ref):
        i = pl.program_id(1)
        qq, kk, dd = q_ref[...], k_ref[...], do_ref[...]
        s = _dot(qq, kk, NT) * ATTN_SCALE
        p = jnp.exp(s - lse_ref[...])
        dp = _dot(dd, v_ref[...], NT)
        delta = jnp.sum(dd.astype(F32) * o_ref[...].astype(F32), axis=-1, keepdims=True)
        ds = (p * (dp - delta) * ATTN_SCALE).astype(BF16)
        dq_ref[...] = _dot(ds, kk, NN)
        dk = _dot(ds, qq, TN)
        dv = _dot(p.astype(BF16), dd, TN)

        @pl.when(i == 0)
        def _():
            dk_ref[...] = dk
            dv_ref[...] = dv

        @pl.when(i > 0)
        def _():
            dk_ref[...] += dk
            dv_ref[...] += dv

    q_spec = pl.BlockSpec((None, tq, HEAD_PAD), lambda hh, i: (hh, i, 0))
    k_spec = pl.BlockSpec((None, n_k, HEAD_PAD), lambda hh, i: (hh, 0, 0))
    v_spec = pl.BlockSpec((None, n_k, V_HEAD), lambda hh, i: (hh, 0, 0))
    o_spec = pl.BlockSpec((None, tq, V_HEAD), lambda hh, i: (hh, i, 0))
    return pl.pallas_call(
        body, name=name, grid=(h, n_q // tq),
        in_specs=[q_spec, k_spec, v_spec, o_spec, pl.BlockSpec((None, tq, 1), lambda hh, i: (hh, i, 0)), o_spec],
        out_specs=[q_spec, k_spec, v_spec],
        out_shape=[SDS((h, n_q, HEAD_PAD), F32), SDS((h, n_k, HEAD_PAD), F32), SDS((h, n_k, V_HEAD), F32)],
        compiler_params=_cparams(),
    )(q, k, v, o, lse, do)


CONV_COLS = 256


def _shift_rows(x, d):
    n_rows = x.shape[0]
    t = lax.broadcasted_iota(jnp.int32, (n_rows, 1), 0)
    if d > 0:
        return jnp.where(t >= d, pltpu.roll(x, d, 0), 0.0)
    return jnp.where(t < n_rows + d, pltpu.roll(x, n_rows + d, 0), 0.0)


def _conv_fwd(name, z3, conv_w):
    n_rows = z3.shape[0]
    nb = D_MODEL // CONV_COLS

    def body(b_ref, c_ref, v_ref, w_ref, y_ref):
        z = c_ref[...] * v_ref[...]
        zc = w_ref[0:1, :] * _shift_rows(z, 1) + w_ref[1:2, :] * z + w_ref[2:3, :] * _shift_rows(z, -1)
        y_ref[...] = (b_ref[...] * zc).astype(BF16)

    def part(k):
        return pl.BlockSpec((n_rows, CONV_COLS), lambda j: (0, k * nb + j))

    return pl.pallas_call(
        body, name=name, grid=(nb,),
        in_specs=[part(0), part(1), part(2), pl.BlockSpec((3, CONV_COLS), lambda j: (0, j))],
        out_specs=pl.BlockSpec((n_rows, CONV_COLS), lambda j: (0, j)),
        out_shape=SDS((n_rows, D_MODEL), BF16), compiler_params=_cparams(),
    )(z3, z3, z3, conv_w)


def _conv_bwd(name, dy, z3, conv_w):
    n_rows = z3.shape[0]
    nb = D_MODEL // CONV_COLS

    def body(dy_ref, b_ref, c_ref, v_ref, w_ref, db_ref, dc_ref, dv_ref, dw_ref):
        c, v, d_y = c_ref[...], v_ref[...], dy_ref[...]
        z = c * v
        z_dn, z_up = _shift_rows(z, 1), _shift_rows(z, -1)
        zc = w_ref[0:1, :] * z_dn + w_ref[1:2, :] * z + w_ref[2:3, :] * z_up
        db_ref[...] = (d_y * zc).astype(BF16)
        dzc = d_y * b_ref[...]
        dz = w_ref[0:1, :] * _shift_rows(dzc, -1) + w_ref[1:2, :] * dzc + w_ref[2:3, :] * _shift_rows(dzc, 1)
        dc_ref[...] = (dz * v).astype(BF16)
        dv_ref[...] = (dz * c).astype(BF16)
        dw_ref[0:1, :] = _colsum(dzc * z_dn)
        dw_ref[1:2, :] = _colsum(dzc * z)
        dw_ref[2:3, :] = _colsum(dzc * z_up)

    def part(k):
        return pl.BlockSpec((n_rows, CONV_COLS), lambda j: (0, k * nb + j))

    col = pl.BlockSpec((n_rows, CONV_COLS), lambda j: (0, j))
    w_spec = pl.BlockSpec((3, CONV_COLS), lambda j: (0, j))
    return pl.pallas_call(
        body, name=name, grid=(nb,), in_specs=[col, part(0), part(1), part(2), w_spec],
        out_specs=[col, col, col, w_spec],
        out_shape=[SDS((n_rows, D_MODEL), BF16)] * 3 + [SDS((3, D_MODEL), F32)], compiler_params=_cparams(),
    )(dy, z3, z3, z3, conv_w)


def _silu_rows(name, x):
    def body(x_ref, s_ref, d_ref):
        xx = x_ref[...]
        sg = jax.nn.sigmoid(xx)
        s_ref[...] = (xx * sg).astype(BF16)
        d_ref[...] = sg * (1.0 + xx * (1.0 - sg))

    return pl.pallas_call(body, name=name, out_shape=[SDS(x.shape, BF16), SDS(x.shape, F32)])(x)


def _sum_rows(name, x, scale=None):
    r, n = x.shape
    tn = _tile(n, 32768, 128)

    def body(*refs):
        acc = jnp.sum(refs[0][...].astype(F32), axis=0, keepdims=True)
        if scale is not None:
            acc = acc * refs[1][...]
        refs[-1][...] = acc

    in_specs = [pl.BlockSpec((r, tn), lambda j: (0, j))]
    args = [x]
    if scale is not None:
        in_specs.append(pl.BlockSpec((1, tn), lambda j: (0, j)))
        args.append(scale)
    return pl.pallas_call(body, name=name, grid=(n // tn,), in_specs=in_specs,
                          out_specs=pl.BlockSpec((1, tn), lambda j: (0, j)), out_shape=SDS((1, n), F32))(*args)


def _me_operand(me):
    return jnp.reshape(me, (1,)).astype(jnp.int32)


def _sum_slots(name, slots, src, me):
    n_slots, r, c = slots.shape
    tr = _tile(r, 432, 16)

    def body(me_ref, own_ref, x_ref, o_ref):
        acc = own_ref[...].astype(F32)
        for sl in range(n_slots):
            acc = acc + x_ref[sl].astype(F32)
        o_ref[...] = acc

    grid_spec = pltpu.PrefetchScalarGridSpec(
        num_scalar_prefetch=1, grid=(r // tr,),
        in_specs=[pl.BlockSpec((None, tr, c), lambda i, me_ref: (me_ref[0], i, 0)),
                  pl.BlockSpec((n_slots, tr, c), lambda i, me_ref: (0, i, 0))],
        out_specs=pl.BlockSpec((tr, c), lambda i, me_ref: (i, 0)))
    return pl.pallas_call(body, name=name, grid_spec=grid_spec, out_shape=SDS((r, c), F32),
                          compiler_params=_cparams())(_me_operand(me), src, slots)


def _adamw(name, w, g, m, v):
    shape = w.shape
    cols = shape[-1]
    rows = w.size // cols
    tr = _tile(rows, 512, 8)
    bc1 = 1.0 - ADAM_B1 ** ADAM_STEP
    bc2 = 1.0 - ADAM_B2 ** ADAM_STEP

    def body(w_ref, g_ref, m_ref, v_ref, d_ref, nm_ref, nv_ref):
        gg = g_ref[...]
        nm = ADAM_B1 * m_ref[...] + (1.0 - ADAM_B1) * gg
        nv = ADAM_B2 * v_ref[...] + (1.0 - ADAM_B2) * (gg * gg)
        nm_ref[...] = nm
        nv_ref[...] = nv
        d_ref[...] = -ADAM_LR * ((nm / bc1) / (jnp.sqrt(nv / bc2) + ADAM_EPS) + ADAM_WD * w_ref[...])

    spec = pl.BlockSpec((tr, cols), lambda i: (i, 0))
    outs = pl.pallas_call(body, name=name, grid=(rows // tr,), in_specs=[spec] * 4, out_specs=[spec] * 3,
                          out_shape=[SDS((rows, cols), F32)] * 3, compiler_params=_cparams())(
        w.reshape(rows, cols), g.reshape(rows, cols), m.reshape(rows, cols), v.reshape(rows, cols))
    return tuple(t.reshape(shape) for t in outs)


def _exchange(name, x, scatter, after=None):
    blk = x.shape[1:] if scatter else x.shape
    extra = [] if after is None else [after]

    def body(x_ref, *rest):
        out_ref, send_sems, recv_sems, local_sem = rest[len(extra):]
        mx, my, mc = lax.axis_index("x"), lax.axis_index("y"), lax.axis_index("c")
        me = 4 * mx + 2 * my + mc
        own = pltpu.make_async_copy(x_ref.at[me] if scatter else x_ref, out_ref.at[me], local_sem)
        own.start()
        copies = []
        for kk in range(1, N_DEV):
            px = jnp.bitwise_xor(mx, (kk >> 2) & 1)
            py = jnp.bitwise_xor(my, (kk >> 1) & 1)
            pc = jnp.bitwise_xor(mc, kk & 1)
            peer = 4 * px + 2 * py + pc
            send = pltpu.make_async_remote_copy(
                src_ref=x_ref.at[peer] if scatter else x_ref, dst_ref=out_ref.at[me],
                send_sem=send_sems.at[kk - 1], recv_sem=recv_sems.at[kk - 1],
                device_id=(px, py, pc), device_id_type=MESH)
            send.start()
            arrival = pltpu.make_async_remote_copy(
                src_ref=x_ref.at[peer] if scatter else x_ref, dst_ref=out_ref.at[peer],
                send_sem=send_sems.at[kk - 1], recv_sem=recv_sems.at[kk - 1],
                device_id=(px, py, pc), device_id_type=MESH)
            copies.append((send, arrival))
        for send, arrival in copies:
            arrival.wait_recv()
            send.wait_send()
        own.wait()

    return pl.pallas_call(
        body, name=name, out_shape=SDS((N_DEV,) + tuple(blk), x.dtype),
        in_specs=[pl.BlockSpec(memory_space=pl.ANY)] * (1 + len(extra)), out_specs=pl.BlockSpec(memory_space=pl.ANY),
        scratch_shapes=[pltpu.SemaphoreType.DMA((N_DEV - 1,)), pltpu.SemaphoreType.DMA((N_DEV - 1,)),
                        pltpu.SemaphoreType.DMA],
    )(x, *extra)


def _rope_perm(pre, reps, post):
    half = QK_ROPE // 4
    width = reps * (pre + QK_ROPE) + post
    p = np.zeros((width, width), np.float32)
    for rep in range(reps):
        s0 = rep * (pre + QK_ROPE) + pre
        for base in (s0, s0 + 2 * half):
            for i in range(half):
                p[base + half + i, base + i] = -1.0
                p[base + i, base + half + i] = 1.0
    return p


def _rope_layout(pre, reps, post):
    width = reps * (pre + QK_ROPE) + post
    expand = np.zeros((QK_ROPE, width), np.float32)
    plain = np.ones((1, width), np.float32)
    for rep in range(reps):
        s0 = rep * (pre + QK_ROPE) + pre
        expand[np.arange(QK_ROPE), s0 + np.arange(QK_ROPE)] = 1.0
        plain[0, s0:s0 + QK_ROPE] = 0.0
    return jnp.asarray(expand, BF16), jnp.asarray(plain, F32), jnp.asarray(_rope_perm(pre, reps, post), BF16)


def _rope_factors(n_lat, t_rows):
    half = QK_ROPE // 4
    pos = jnp.arange(n_lat)
    freqs = jnp.power(ROPE_THETA, -jnp.arange(0, 2 * half, 2, dtype=F32) / (2 * half))
    ang_r = (pos // GRID_W).astype(F32)[:, None] * freqs
    ang_c = (pos % GRID_W).astype(F32)[:, None] * freqs
    ang = jnp.concatenate([ang_r, ang_r, ang_c, ang_c], axis=-1)
    rest = t_rows - n_lat
    return (jnp.concatenate([jnp.cos(ang), jnp.ones((rest, QK_ROPE), F32)], axis=0),
            jnp.concatenate([jnp.sin(ang), jnp.zeros((rest, QK_ROPE), F32)], axis=0))


def _ffn_half_fwd(tag, s, mg, k, feed, i, coef, n_lat):
    wg_t, wu_t = feed.weights(f"{tag}_up", [f"gate_t{i}", f"up_t{i}"], s)
    u, a, b, hid = _ffn_up(f"{tag}_up", s, mg, k, n_lat, wg_t, wu_t)
    (wd,) = feed.weights(f"{tag}_down", [f"down{i}"], hid)
    s_out, o = _mm_resid(f"{tag}_down", hid, wd, s, mg, k, coef, n_lat)
    return s_out, (s, u, a, b, hid, o, wg_t, wu_t, wd)


def _ffn_half_bwd(tag, ds_out, saved, mg, k, feed, i, coef, n_lat, out_rows=None):
    s, u, a, b, hid, o, wg_t, wu_t, wd = saved
    do, da, db, dgate = _ffn_dact(f"{tag}_dact", ds_out, o, mg, k, coef, n_lat, wd, a, b)
    dwd = _mm(f"{tag}_dwd", [(hid, do)], "tn", BF16)
    dwg_t, dwu_t = _dw_pair(f"{tag}_dwgu", da, db, u)
    token = feed.grads(tag, {f"down{i}": dwd, f"gate_t{i}": dwg_t, f"up_t{i}": dwu_t})
    ds_in, (dshift, dscale, dgain) = _du_adaln(f"{tag}_du", [(da, wg_t), (db, wu_t)], s, ds_out, mg, k, n_lat,
                                               _after(token), out_rows)
    return ds_in, dict(shift=dshift, scale=dscale, gate=dgate, gain=dgain)


def _after(token):
    return jnp.zeros((1, D_MODEL), F32) + token


def _mod_grad(parts, n_groups):
    rows = []
    zero = jnp.zeros((n_groups, 1, D_MODEL), F32)
    for k in range(3):
        for nm in ("shift", "scale", "gate"):
            t = parts[k].get(nm, zero)
            if t.shape[0] < n_groups:
                t = jnp.concatenate([t, jnp.zeros((n_groups - t.shape[0], 1, D_MODEL), F32)], axis=0)
            rows.append(t)
    return jnp.concatenate(rows, axis=1).reshape(n_groups, N_MOD * D_MODEL)


def _local_step(x, ctx, target, mod_h, mod_g, norm_g, feed, pool_w, pool_scale, q_norm_g, kv_norm_g, conv_w,
                final_norm_g):
    n_lat, n_ctx = x.shape[0], ctx.shape[0]
    t_all = n_lat + n_ctx
    mg0 = jnp.stack([jnp.concatenate([mod_h[0], norm_g[0]], axis=0), jnp.concatenate([mod_g, norm_g[0]], axis=0)])
    mg1 = jnp.concatenate([mod_h[1], norm_g[1]], axis=0)[None]

    s0 = jnp.concatenate([x, ctx], axis=0)
    s1, sv_f00 = _ffn_half_fwd("l0f0", s0, mg0, 0, feed, 0, 0.5, n_lat)

    w_in, w_uq, w_ukv_t, w_ab_out = feed.weights("l0m", ["in_t", "uq", "ukv_t", "ab_out"], s1)
    kv_rows = KV_RANK + QK_ROPE
    w_in_t = jnp.concatenate([
        w_in[:POOL_DIM], jnp.zeros((PA_CQ - POOL_DIM, D_MODEL), BF16), w_in[POOL_DIM:POOL_DIM + Q_RANK],
        w_in[POOL_DIM + Q_RANK:], jnp.zeros((PA_KV_W - kv_rows, D_MODEL), BF16)], axis=0)
    ua, proj = _adaln_mm("l0m_proj", s1, mg0, 1, n_lat, w_in_t)
    pool_y, pool_p = _pool_fwd("l0m_pool", proj, n_lat, pool_w.astype(BF16), pool_scale)
    nq = _rmsnorm_fwd("l0m_qnorm", proj, Q_RANK, PA_CQ // Q_RANK, q_norm_g, n_lat)
    q_lin = _mm("l0m_q", [(nq, w_uq)], "nn", F32, 512, 768)
    cos32, sin32 = _rope_factors(n_lat, t_all)
    lay_q, lay_k = _rope_layout(QK_NOPE, HEADS, 0), _rope_layout(KV_RANK, 1, PA_KV_W - kv_rows)
    q_rot = _rope("l0m_qrope", q_lin, Q_RANK, 0, cos32[:n_lat], sin32[:n_lat], lay_q, False, BF16)
    kvr = _rope("l0m_krope", proj, PA_KV_W, PA_KV // PA_KV_W, cos32, sin32, lay_k, False, F32)
    nkv = _rmsnorm_fwd("l0m_kvnorm", kvr, KV_RANK, 0, kv_norm_g, t_all)
    kv = _mm("l0m_kv", [(nkv, w_ukv_t)], "nt", BF16, 768, 512)
    qh = jnp.pad(q_rot.reshape(n_lat, HEADS, QK_HEAD), ((0, 0), (0, 0), (0, HEAD_PAD - QK_HEAD))).transpose(1, 0, 2)
    kvh = kv.reshape(t_all, HEADS, QK_NOPE + V_HEAD)
    k_rope = jnp.broadcast_to(kvr[:, None, KV_RANK:KV_RANK + QK_ROPE].astype(BF16), (t_all, HEADS, QK_ROPE))
    kh = jnp.concatenate([kvh[:, :, :QK_NOPE], k_rope, jnp.zeros((t_all, HEADS, HEAD_PAD - QK_HEAD), BF16)],
                         axis=-1).transpose(1, 0, 2)
    vh = kvh[:, :, QK_NOPE:].transpose(1, 0, 2)
    oh, lse = _attn_fwd("l0m_attn", qh, kh, vh)
    cat = jnp.concatenate([pool_y, oh.transpose(1, 0, 2).reshape(n_lat, HEADS * V_HEAD)], axis=-1)
    h1 = s1[:n_lat]
    h2, mix_o = _mm_resid("l0m_out", cat, w_ab_out, h1, mg0[:1], 1, 1.0, n_lat)

    h3, sv_f01 = _ffn_half_fwd("l0f1", h2, mg0[:1], 2, feed, 1, 0.5, n_lat)

    h4, sv_f10 = _ffn_half_fwd("l1f0", h3, mg1, 0, feed, 2, 0.5, n_lat)
    w_cin_t, w_c_out = feed.weights("l1m", ["cin_t", "c_out"], h4)
    uc, z3 = _adaln_mm("l1m_in", h4, mg1, 1, n_lat, w_cin_t)
    yc = _conv_fwd("l1m_conv", z3, conv_w)
    h5, conv_o = _mm_resid("l1m_out", yc, w_c_out, h4, mg1, 1, 1.0, n_lat)
    h6, sv_f11 = _ffn_half_fwd("l1f1", h5, mg1, 2, feed, 3, 0.5, n_lat)

    dh6, sq_cols, d_final_g = _final_loss("loss_head", h6, target, final_norm_g)
    g = {}
    dh5, g["f11"] = _ffn_half_bwd("l1f1", dh6, sv_f11, mg1, 2, feed, 3, 0.5, n_lat)

    do_c, dyc, dgate_c = _gate_mm("l1m_dy", dh5, conv_o, mg1, 1, 1.0, n_lat, w_c_out)
    d_c_out = _mm("l1m_dwout", [(yc, do_c)], "tn", BF16)
    db_, dc_, dv_, d_conv_w = _conv_bwd("l1m_dconv", dyc, z3, conv_w)
    dz3 = jnp.concatenate([db_, dc_, dv_], axis=-1)
    d_cin_t = _mm("l1m_dwin", [(dz3, uc)], "tn", BF16)
    token = feed.grads("l1m", {"c_out": d_c_out, "cin_t": d_cin_t})
    dh4, (dsh_c, dsc_c, dgn_c) = _du_adaln("l1m_du", [(dz3, w_cin_t)], h4, dh5, mg1, 1, n_lat, _after(token))
    dh3, g["f10"] = _ffn_half_bwd("l1f0", dh4, sv_f10, mg1, 0, feed, 2, 0.5, n_lat)

    dh2, g["f01"] = _ffn_half_bwd("l0f1", dh3, sv_f01, mg0[:1], 2, feed, 1, 0.5, n_lat)

    do_a, dcat, dgate_a = _gate_mm("l0m_dcat", dh2, mix_o, mg0[:1], 1, 1.0, n_lat, w_ab_out)
    d_ab_out = _mm("l0m_dwout", [(cat, do_a)], "tn", BF16)
    d_pool_x, d_pool_w, d_pool_scale = _pool_bwd("l0m_dpool", dcat, n_lat, pool_p, pool_w.astype(BF16), pool_scale)
    doh = dcat[:, POOL_DIM:].reshape(n_lat, HEADS, V_HEAD).transpose(1, 0, 2).astype(BF16)
    dqh, dkh, dvh = _attn_bwd("l0m_dattn", qh, kh, vh, oh, lse, doh)
    dk_rope = _sum_rows("l0m_dksum", dkh[:, :, QK_NOPE:QK_HEAD].reshape(HEADS, t_all * QK_ROPE))
    dq_rot = dqh[:, :, :QK_HEAD].transpose(1, 0, 2).reshape(n_lat, Q_RANK)
    dq_lin = _rope("l0m_dqrope", dq_rot, Q_RANK, 0, cos32[:n_lat], sin32[:n_lat], lay_q, True, BF16)
    d_uq = _mm("l0m_dwuq", [(nq, dq_lin)], "tn", BF16, 768, 768)
    dnq = _mm("l0m_dnq", [(dq_lin, w_uq)], "nt", F32, 512, 768)
    dcq, d_q_norm_g = _rmsnorm_bwd("l0m_dqnorm", proj, Q_RANK, PA_CQ // Q_RANK, dnq, q_norm_g, n_lat, BF16)
    dkv = jnp.concatenate([dkh[:, :, :QK_NOPE], dvh], axis=-1).transpose(1, 0, 2).reshape(t_all, HEADS * HEAD_PAD)
    dkv = dkv.astype(BF16)
    dnkv = _mm("l0m_dnkv", [(dkv, w_ukv_t)], "nn", F32, 768, 256)
    d_ukv_t = _mm("l0m_dwukv", [(dkv, nkv)], "tn", BF16, 512, 256)
    dckv, d_kv_norm_g = _rmsnorm_bwd("l0m_dkvnorm", kvr, KV_RANK, 0, dnkv, kv_norm_g, t_all)
    dkvr = jnp.concatenate([dckv, dk_rope.reshape(t_all, QK_ROPE),
                            jnp.zeros((t_all, PA_KV_W - KV_RANK - QK_ROPE), F32)], axis=-1)
    dpb = _rope("l0m_dkrope", dkvr, PA_KV_W, 0, cos32, sin32, lay_k, True, BF16)
    dproj_lat = jnp.concatenate([d_pool_x, jnp.zeros((n_lat, PA_CQ - POOL_DIM), BF16), dcq, dpb[:n_lat]], axis=-1)
    dproj_ctx = jnp.concatenate([jnp.zeros((n_ctx, PA_KV), BF16), dpb[n_lat:]], axis=-1)
    dproj = jnp.concatenate([dproj_lat, dproj_ctx], axis=0)
    d_in_pad = _mm("l0m_dwin", [(dproj, ua)], "tn", BF16, 640, 512)
    d_in_t = jnp.concatenate([d_in_pad[:POOL_DIM], d_in_pad[PA_CQ:PA_CQ + Q_RANK],
                              d_in_pad[PA_KV:PA_KV + kv_rows]], axis=0)
    token = feed.grads("l0m", {"ab_out": d_ab_out, "uq": d_uq, "ukv_t": d_ukv_t, "in_t": d_in_t})
    ds1, (dsh_a, dsc_a, dgn_a) = _du_adaln("l0m_du", [(dproj, w_in_t)], s1, dh2, mg0, 1, n_lat, _after(token))
    grad_x, g["f00"] = _ffn_half_bwd("l0f0", ds1, sv_f00, mg0, 0, feed, 0, 0.5, n_lat, out_rows=n_lat)

    dmod0 = _mod_grad([g["f00"], dict(shift=dsh_a, scale=dsc_a, gate=dgate_a), g["f01"]], 2)
    dmod1 = _mod_grad([g["f10"], dict(shift=dsh_c, scale=dsc_c, gate=dgate_c), g["f11"]], 1)
    d_norm_g = jnp.stack([
        jnp.concatenate([jnp.sum(g["f00"]["gain"], axis=0), jnp.sum(dgn_a, axis=0), g["f01"]["gain"][0]], axis=0),
        jnp.concatenate([g["f10"]["gain"][0], dgn_c[0], g["f11"]["gain"][0]], axis=0)])
    grads = dict(
        pool_w=d_pool_w, pool_scale=d_pool_scale, q_norm_g=d_q_norm_g[0], kv_norm_g=d_kv_norm_g[0],
        conv_w=d_conv_w, final_norm_g=d_final_g[0], norm_g=d_norm_g,
        mod_h=jnp.stack([dmod0[0], dmod1[0]]), mod_g=dmod0[1])
    return sq_cols, grad_x, grads


HBM_SPEC = pl.BlockSpec(memory_space=pltpu.HBM)
SEM_SPEC = pl.BlockSpec(memory_space=pltpu.SEMAPHORE)
ANY_SPEC = pl.BlockSpec(memory_space=pl.ANY)
SIDE_EFFECT = pltpu.SideEffectType.DATAFLOW_SIDE_EFFECTING
N_PEERS = N_DEV - 1


def _mesh_place():
    mx, my, mc = lax.axis_index("x"), lax.axis_index("y"), lax.axis_index("c")
    return mx, my, mc, 4 * mx + 2 * my + mc


def _peer(place, kk):
    mx, my, mc, _ = place
    px = jnp.bitwise_xor(mx, (kk >> 2) & 1)
    py = jnp.bitwise_xor(my, (kk >> 1) & 1)
    pc = jnp.bitwise_xor(mc, kk & 1)
    return (px, py, pc), 4 * px + 2 * py + pc


def _hbm(a):
    return pltpu.with_memory_space_constraint(a, pltpu.HBM)


def _landing(block, me):
    zone = lax.empty((N_DEV,) + block.shape, block.dtype)
    return lax.dynamic_update_slice(zone, block[None], (me,) + (0,) * block.ndim)


ALL_PEERS = tuple(range(1, N_DEV))
SIBLING = 1
CHIP_PEERS = (2, 4, 6)
RELAYED = (3, 5, 7)


def _exchange_start(name, srcs, lands, scatter, after, peers=ALL_PEERS):
    n = len(srcs)
    extra = [] if after is None else [after]

    def body(*refs):
        src, land = refs[:n], refs[n:2 * n]
        send_sems, recv_sems, token = refs[2 * n + len(extra)], refs[2 * n + len(extra) + 1], refs[-1]
        place = _mesh_place()
        for a in range(n):
            for kk in peers:
                dev, peer = _peer(place, kk)
                pltpu.make_async_remote_copy(
                    src_ref=src[a].at[peer] if scatter else src[a],
                    dst_ref=land[a].at[kk - 1] if scatter else land[a].at[place[3]],
                    send_sem=send_sems.at[a * N_PEERS + kk - 1], recv_sem=recv_sems.at[a * N_PEERS + kk - 1],
                    device_id=dev, device_id_type=MESH).start()
        token[...] = jnp.zeros_like(token)

    thru = [pltpu.HBM(t.shape, t.dtype) for t in (*srcs, *lands)]
    res = pl.pallas_call(
        body, name=name,
        out_shape=(pltpu.SemaphoreType.DMA((n * N_PEERS,)), pltpu.SemaphoreType.DMA((n * N_PEERS,)), *thru,
                   SDS((8, 128), F32)),
        in_specs=[HBM_SPEC] * (2 * n) + [ANY_SPEC] * len(extra),
        out_specs=(SEM_SPEC, SEM_SPEC, *([HBM_SPEC] * (2 * n)), pl.BlockSpec(memory_space=pltpu.VMEM)),
        input_output_aliases={i: 2 + i for i in range(2 * n)},
        compiler_params=pltpu.CompilerParams(has_side_effects=SIDE_EFFECT),
    )(*[_hbm(s) for s in srcs], *[_hbm(t) for t in lands], *extra)
    return res[0], res[1], list(res[2:2 + n]), list(res[2 + n:2 + 2 * n]), res[-1]


def _exchange_wait(name, send_sems, recv_sems, srcs, lands, places, scatter, after):
    n = len(srcs)

    def body(*refs):
        src, land = refs[:n], refs[n:2 * n]
        send, recv = refs[2 * n], refs[2 * n + 1]
        place = _mesh_place()
        for a in range(n):
            for kk in range(1, N_DEV):
                dev, peer = _peer(place, kk)
                cp = pltpu.make_async_remote_copy(
                    src_ref=src[a].at[peer] if scatter else src[a],
                    dst_ref=land[a].at[kk - 1] if scatter else land[a].at[peer],
                    send_sem=send.at[places[a] * N_PEERS + kk - 1], recv_sem=recv.at[places[a] * N_PEERS + kk - 1],
                    device_id=dev, device_id_type=MESH)
                cp.wait_send()
                cp.wait_recv()

    thru = [pltpu.HBM(t.shape, t.dtype) for t in (*srcs, *lands)]
    res = pl.pallas_call(
        body, name=name, out_shape=tuple(thru),
        in_specs=[HBM_SPEC] * (2 * n) + [SEM_SPEC, SEM_SPEC] + [ANY_SPEC] * len(after),
        out_specs=tuple([HBM_SPEC] * (2 * n)), input_output_aliases={i: i for i in range(2 * n)},
        compiler_params=pltpu.CompilerParams(has_side_effects=SIDE_EFFECT),
    )(*srcs, *lands, send_sems, recv_sems, *after)
    return list(res[:n]), list(res[n:])


def _gather_relay(name, send1, recv1, lands, places, after):
    n = len(lands)

    def body(*refs):
        land, s1, r1 = refs[:n], refs[n], refs[n + 1]
        s2, r2 = refs[n + 3], refs[n + 4]
        place = _mesh_place()
        sibling = _peer(place, SIBLING)[0]
        for a in range(n):
            for j, kk in enumerate(CHIP_PEERS):
                dev, origin = _peer(place, kk)
                block = land[a].at[origin]
                pltpu.make_async_remote_copy(
                    src_ref=block, dst_ref=block, send_sem=s1.at[places[a] * N_PEERS + kk - 1],
                    recv_sem=r1.at[places[a] * N_PEERS + kk - 1], device_id=dev, device_id_type=MESH).wait_recv()
                pltpu.make_async_remote_copy(
                    src_ref=block, dst_ref=block, send_sem=s2.at[a * 3 + j], recv_sem=r2.at[a * 3 + j],
                    device_id=sibling, device_id_type=MESH).start()

    res = pl.pallas_call(
        body, name=name,
        out_shape=(pltpu.SemaphoreType.DMA((3 * n,)), pltpu.SemaphoreType.DMA((3 * n,)),
                   *[pltpu.HBM(t.shape, t.dtype) for t in lands]),
        in_specs=[HBM_SPEC] * n + [SEM_SPEC, SEM_SPEC, ANY_SPEC],
        out_specs=(SEM_SPEC, SEM_SPEC, *([HBM_SPEC] * n)),
        input_output_aliases={i: 2 + i for i in range(n)},
        compiler_params=pltpu.CompilerParams(has_side_effects=SIDE_EFFECT),
    )(*lands, send1, recv1, after)
    return res[0], res[1], list(res[2:])


def _gather_wait(name, send1, recv1, send2, recv2, srcs, lands, places, after):
    n = len(lands)

    def body(*refs):
        src, land = refs[:n], refs[n:2 * n]
        s1, r1, s2, r2 = refs[2 * n:2 * n + 4]
        place = _mesh_place()
        for a in range(n):
            for kk in (SIBLING,) + CHIP_PEERS:
                dev, origin = _peer(place, kk)
                first = pltpu.make_async_remote_copy(
                    src_ref=src[a], dst_ref=land[a].at[origin], send_sem=s1.at[places[a] * N_PEERS + kk - 1],
                    recv_sem=r1.at[places[a] * N_PEERS + kk - 1], device_id=dev, device_id_type=MESH)
                first.wait_send()
                if kk == SIBLING:
                    first.wait_recv()
            for j, kk in enumerate(CHIP_PEERS):
                dev, origin = _peer(place, kk + 1)
                relay = pltpu.make_async_remote_copy(
                    src_ref=src[a], dst_ref=land[a].at[origin], send_sem=s2.at[a * 3 + j], recv_sem=r2.at[a * 3 + j],
                    device_id=dev, device_id_type=MESH)
                relay.wait_send()
                relay.wait_recv()

    arrays = (*srcs, *lands)
    res = pl.pallas_call(
        body, name=name, out_shape=tuple(pltpu.HBM(t.shape, t.dtype) for t in arrays),
        in_specs=[HBM_SPEC] * (2 * n) + [SEM_SPEC] * 4 + [ANY_SPEC], out_specs=tuple([HBM_SPEC] * (2 * n)),
        input_output_aliases={i: i for i in range(2 * n)},
        compiler_params=pltpu.CompilerParams(has_side_effects=SIDE_EFFECT),
    )(*arrays, send1, recv1, send2, recv2, after)
    return list(res[n:])


class _Feed:
    def __init__(self, shards, groups, me):
        self.shards, self.groups, self.me, self.pos = shards, groups, me, 0
        self.sems, self.srcs, self.lands = {}, {}, {}
        self.relays = {}
        self.pending = []

    def start(self, tag, names, after):
        srcs = [self.shards[nm] for nm in names]
        lands = [_landing(s, self.me) for s in srcs]
        send, recv, srcs, lands, self.token = _exchange_start(
            f"gather_start_{tag}", srcs, lands, False, after, (SIBLING,) + CHIP_PEERS)
        for i, nm in enumerate(names):
            self.sems[nm], self.srcs[nm], self.lands[nm] = (send, recv, i), srcs[i], lands[i]
        return self.token

    def _relay(self, gi, after):
        names = self.groups[gi]
        if gi not in self.relays:
            send, recv, _ = self.sems[names[0]]
            places = [self.sems[nm][2] for nm in names]
            send2, recv2, lands = _gather_relay(f"gather_relay_{gi}", send, recv, [self.lands[nm] for nm in names],
                                                places, after)
            for nm, t in zip(names, lands):
                self.lands[nm] = t
            self.relays[gi] = (send2, recv2)
            after = lands[0]
        return after

    def weights(self, tag, names, after):
        gi = self.pos
        assert names == self.groups[gi], (names, self.groups[gi])
        if gi == 0:
            after = self.token
        self._relay(gi, after)
        if 1 <= gi < len(self.groups) - 1:
            after = self._relay(gi + 1, after)
        send2, recv2 = self.relays[gi]
        send, recv, _ = self.sems[names[0]]
        got = _gather_wait(f"gather_wait_{tag}", send, recv, send2, recv2, [self.srcs[nm] for nm in names],
                           [self.lands[nm] for nm in names], [self.sems[nm][2] for nm in names], after)
        self.pos += 1
        return [t.reshape((N_DEV * t.shape[1],) + t.shape[2:]) for t in got]

    def grads(self, tag, full):
        names = list(full)
        srcs = [full[nm].reshape((N_DEV, full[nm].shape[0] // N_DEV) + full[nm].shape[1:]) for nm in names]
        lands = [lax.empty((N_PEERS,) + s.shape[1:], s.dtype) for s in srcs]
        send, recv, srcs, lands, token = _exchange_start(f"scatter_start_{tag}", srcs, lands, True, None)
        self.pending.append((tag, names, send, recv, srcs, lands))
        return token[0, 0]

    def collect(self, tags, after, keep_slots=()):
        out = {}
        for tag, names, send, recv, srcs, lands in self.pending:
            if tag not in tags:
                continue
            srcs, got = _exchange_wait(f"scatter_wait_{tag}", send, recv, srcs, lands, list(range(len(names))), True,
                                       after)
            for nm, slots, src in zip(names, got, srcs):
                out[nm] = ((slots, src) if nm.startswith(tuple(keep_slots))
                           else _sum_slots(f"reduce_{nm}", slots, src, self.me))
        return out


def _adamw_math(w, gg, m, v):
    nm = ADAM_B1 * m + (1.0 - ADAM_B1) * gg
    nv = ADAM_B2 * v + (1.0 - ADAM_B2) * (gg * gg)
    bc1 = 1.0 - ADAM_B1 ** ADAM_STEP
    bc2 = 1.0 - ADAM_B2 ** ADAM_STEP
    return -ADAM_LR * ((nm / bc1) / (jnp.sqrt(nv / bc2) + ADAM_EPS) + ADAM_WD * w), nm, nv


def _adamw_part(name, i, w, scattered, me, m, v, prev):
    n_parts, rows, cols = w.shape
    tr = _tile(rows, 256, 16)
    if prev is None:
        prev = tuple(lax.empty(w.shape, F32) for _ in range(4))

    slots, src = scattered

    def body(me_ref, w_ref, g_ref, own_ref, m_ref, v_ref, *rest):
        go_ref, d_ref, nm_ref, nv_ref = rest[4:]
        gg = own_ref[...].astype(F32)
        for sl in range(N_PEERS):
            gg = gg + g_ref[sl].astype(F32)
        d, nm, nv = _adamw_math(w_ref[...], gg, m_ref[...], v_ref[...])
        go_ref[...] = gg
        d_ref[...] = d
        nm_ref[...] = nm
        nv_ref[...] = nv

    part = pl.BlockSpec((None, tr, cols), lambda r, me_ref: (i, r, 0))
    grid_spec = pltpu.PrefetchScalarGridSpec(
        num_scalar_prefetch=1, grid=(rows // tr,),
        in_specs=[part, pl.BlockSpec((N_PEERS, tr, cols), lambda r, me_ref: (0, r, 0)),
                  pl.BlockSpec((None, tr, cols), lambda r, me_ref: (me_ref[0], r, 0)), part, part] + [ANY_SPEC] * 4,
        out_specs=[part] * 4)
    return pl.pallas_call(
        body, name=name, grid_spec=grid_spec, out_shape=[SDS(w.shape, F32)] * 4,
        input_output_aliases={6 + k: k for k in range(4)}, compiler_params=_cparams(),
    )(_me_operand(me), w, slots, src, m, v, *prev)


WEIGHT_NAMES = ("c_ctx", "norm_g", "w_mod", "b_mod", "ffn_w_gate", "ffn_w_up", "ffn_w_down", "ab_w_in", "pool_w",
                "pool_scale", "q_norm_g", "w_uq", "kv_norm_g", "w_ukv", "ab_w_out", "conv_w_in", "conv_w",
                "conv_w_out", "final_norm_g")


def kernel(x, c, ctx, c_ctx, norm_g, w_mod, b_mod, ffn_w_gate, ffn_w_up, ffn_w_down, ab_w_in, pool_w, pool_scale, q_norm_g, w_uq, kv_norm_g, w_ukv, ab_w_out, conv_w_in, conv_w, conv_w_out, final_norm_g, loss_target, m_c_ctx, m_norm_g, m_w_mod, m_b_mod, m_ffn_w_gate, m_ffn_w_up, m_ffn_w_down, m_ab_w_in, m_pool_w, m_pool_scale, m_q_norm_g, m_w_uq, m_kv_norm_g, m_w_ukv, m_ab_w_out, m_conv_w_in, m_conv_w, m_conv_w_out, m_final_norm_g, v_c_ctx, v_norm_g, v_w_mod, v_b_mod, v_ffn_w_gate, v_ffn_w_up, v_ffn_w_down, v_ab_w_in, v_pool_w, v_pool_scale, v_q_norm_g, v_w_uq, v_kv_norm_g, v_w_ukv, v_ab_w_out, v_conv_w_in, v_conv_w, v_conv_w_out, v_final_norm_g):
    weights = (c_ctx, norm_g, w_mod, b_mod, ffn_w_gate, ffn_w_up, ffn_w_down, ab_w_in, pool_w, pool_scale, q_norm_g,
               w_uq, kv_norm_g, w_ukv, ab_w_out, conv_w_in, conv_w, conv_w_out, final_norm_g)
    moms = (m_c_ctx, m_norm_g, m_w_mod, m_b_mod, m_ffn_w_gate, m_ffn_w_up, m_ffn_w_down, m_ab_w_in, m_pool_w,
            m_pool_scale, m_q_norm_g, m_w_uq, m_kv_norm_g, m_w_ukv, m_ab_w_out, m_conv_w_in, m_conv_w, m_conv_w_out,
            m_final_norm_g)
    vels = (v_c_ctx, v_norm_g, v_w_mod, v_b_mod, v_ffn_w_gate, v_ffn_w_up, v_ffn_w_down, v_ab_w_in, v_pool_w,
            v_pool_scale, v_q_norm_g, v_w_uq, v_kv_norm_g, v_w_ukv, v_ab_w_out, v_conv_w_in, v_conv_w, v_conv_w_out,
            v_final_norm_g)
    me = 4 * lax.axis_index("x") + 2 * lax.axis_index("y") + lax.axis_index("c")
    n_lat, n_ctx = x.shape[1], ctx.shape[1]
    d = D_MODEL
    mod_cols = w_mod.shape[-1]
    ng_sh, cw_sh = norm_g.shape[-1], conv_w.shape[-1]

    def ffn_shards(i):
        return {f"gate_t{i}": ffn_w_gate[i // 2, i % 2].T, f"up_t{i}": ffn_w_up[i // 2, i % 2].T,
                f"down{i}": ffn_w_down[i // 2, i % 2]}

    local = {**ffn_shards(0), "in_t": ab_w_in[0].T, "uq": w_uq[0], "ukv_t": w_ukv[0].T, "ab_out": ab_w_out[0],
             **ffn_shards(1), **ffn_shards(2), "cin_t": conv_w_in[0].T, "c_out": conv_w_out[0], **ffn_shards(3)}
    ffn_groups = [[[f"gate_t{i}", f"up_t{i}"], [f"down{i}"]] for i in range(4)]
    groups = [*ffn_groups[0], ["in_t", "uq", "ukv_t", "ab_out"], *ffn_groups[1], *ffn_groups[2], ["cin_t", "c_out"],
              *ffn_groups[3]]
    feed = _Feed({nm: a.astype(BF16) for nm, a in local.items()}, groups, me)

    small = jnp.concatenate([c.reshape(-1), norm_g.reshape(-1), conv_w.reshape(-1)])
    small_n = -(-small.shape[0] // 1024) * 1024
    small = jnp.pad(small, (0, small_n - small.shape[0])).reshape(small_n // 128, 128)
    small_all = _exchange("gather_small", small, False).reshape(N_DEV, small_n)
    c_all = small_all[:, :d]
    o1 = d + 6 * ng_sh
    norm_g_full = small_all[:, d:o1].reshape(N_DEV, 2, 3, ng_sh).transpose(1, 2, 0, 3).reshape(2, 3, d)
    conv_w_full = small_all[:, o1:o1 + 3 * cw_sh].reshape(N_DEV, 3, cw_sh).transpose(1, 0, 2).reshape(3, d)

    cond = jnp.concatenate([c_all, jnp.broadcast_to(c_ctx[None, :], (N_DEV, d))], axis=0)
    sil, dsil = _silu_rows("mod_silu", cond)
    w_mod_b = w_mod.astype(BF16)
    b_sh = lax.dynamic_slice(b_mod, (0, me * mod_cols), (2, mod_cols))
    m_part = jnp.stack([_mm(f"mod_fwd{l}", [(sil, w_mod_b[l])], "nn", F32, 16, 384, bias=b_sh[l:l + 1])
                        for l in range(2)], axis=1)
    m_all = _exchange("gather_mod", m_part.reshape(-1, 128), False).reshape(N_DEV, 2 * N_DEV, 2, mod_cols)
    m_mine = lax.dynamic_index_in_dim(m_all, me, axis=1, keepdims=False)
    mod_h = m_mine.transpose(1, 0, 2).reshape(2, N_MOD, d)
    mod_g = m_all[:, N_DEV, 0, :].reshape(N_MOD, d)

    first = feed.start("first", [nm for grp in groups[:3] for nm in grp], m_all)
    feed.start("rest", [nm for grp in groups[3:] for nm in grp], first)

    sq_cols, grad_x, g = _local_step(x[0], ctx[0], loss_target[0], mod_h, mod_g, norm_g_full, feed, pool_w[0],
                                  pool_scale, q_norm_g, kv_norm_g, conv_w_full, final_norm_g)
    w_of, m_of, v_of = (dict(zip(WEIGHT_NAMES, t)) for t in (weights, moms, vels))
    results = {}

    def update(nm, grad, view=lambda t: t):
        outs = _adamw(f"adamw_{nm}", view(w_of[nm]), grad.reshape(view(w_of[nm]).shape), view(m_of[nm]), view(v_of[nm]))
        results[nm] = tuple(view(t) for t in (grad.reshape(view(w_of[nm]).shape), *outs))

    def swap(t):
        return jnp.swapaxes(t, -1, -2)

    stacked = ("gate_t", "up_t", "down")
    early = feed.collect(["l1f1", "l1m", "l1f0", "l0f1", "l0m"], [grad_x], stacked)
    update("ab_w_in", early["in_t"], swap)
    update("w_uq", early["uq"])
    update("w_ukv", early["ukv_t"].T)
    update("ab_w_out", early["ab_out"])
    update("conv_w_in", early["cin_t"].T)
    update("conv_w_out", early["c_out"])
    ffn = {}
    for nm, prefix, view in (("ffn_w_gate", "gate_t", swap), ("ffn_w_up", "up_t", swap),
                             ("ffn_w_down", "down", lambda t: t)):
        w4, m4, v4 = (view(t).reshape((4,) + view(t).shape[-2:]) for t in (w_of[nm], m_of[nm], v_of[nm]))
        prev = None
        for i in (3, 2, 1):
            prev = _adamw_part(f"adamw_{nm}{i}", i, w4, early[f"{prefix}{i}"], me, m4, v4, prev)
        ffn[nm] = (prefix, view, w4, m4, v4, prev)
    done_early = [results[nm][1] for nm in results] + [state[5][1] for state in ffn.values()]
    late = feed.collect(["l0f0"], done_early, stacked)
    for nm, (prefix, view, w4, m4, v4, prev) in ffn.items():
        outs = _adamw_part(f"adamw_{nm}0", 0, w4, late[f"{prefix}0"], me, m4, v4, prev)
        results[nm] = tuple(view(t.reshape(view(w_of[nm]).shape)) for t in outs)

    dm = jnp.stack([g["mod_h"], jnp.stack([g["mod_g"], jnp.zeros_like(g["mod_g"])])])
    dm_all = _exchange("gather_dmod", dm.reshape(-1, 128), False, results["ffn_w_down"][1]).reshape(N_DEV, 2, 2, N_MOD * d)
    grad_b_mod = _sum_rows("dmod_bias", dm_all.reshape(2 * N_DEV, 2 * N_MOD * d)).reshape(2, N_MOD * d)
    dm_sh = lax.dynamic_slice(dm_all, (0, 0, 0, me * mod_cols), (N_DEV, 2, 2, mod_cols))
    gw_mod, cctx_parts = [], []
    for l in range(2):
        dm_l = dm_sh[:, :, l, :].transpose(1, 0, 2).reshape(2 * N_DEV, mod_cols).astype(BF16)
        gw_mod.append(_mm(f"mod_dw{l}", [(sil, dm_l)], "tn", F32, 512, 384))
        dm_ctx = jnp.concatenate([dm_l[N_DEV:], jnp.zeros((N_DEV, mod_cols), BF16)], axis=0)
        cctx_parts.append(_mm(f"mod_dcond{l}", [(dm_ctx, w_mod_b[l])], "nt", F32, 16, 512))
    cctx_part = _sum_rows("mod_dcond_sum", jnp.concatenate(cctx_parts, axis=0))
    update("w_mod", jnp.stack(gw_mod))
    update("b_mod", grad_b_mod)

    small_g = jnp.concatenate([g["pool_w"].reshape(-1), g["pool_scale"].reshape(-1), g["q_norm_g"].reshape(-1),
                               g["kv_norm_g"].reshape(-1), g["final_norm_g"].reshape(-1), g["norm_g"].reshape(-1),
                               g["conv_w"].reshape(-1), sq_cols.reshape(-1), cctx_part.reshape(-1)])
    sizes = [pool_w.size, pool_scale.size, q_norm_g.size, kv_norm_g.size, d, 6 * d, 3 * d, d, d]
    sg_n = -(-small_g.shape[0] // 1024) * 1024
    small_g = jnp.pad(small_g, (0, sg_n - small_g.shape[0]))
    sg_all = _exchange("gather_small_grads", small_g.reshape(-1, 128), False).reshape(N_DEV, sg_n)
    scale_vec = jnp.concatenate([jnp.ones((1, sum(sizes[:-1])), F32), dsil[N_DEV:N_DEV + 1],
                                 jnp.ones((1, sg_n - sum(sizes)), F32)], axis=1)
    sg = _sum_rows("small_grads_sum", sg_all, scale_vec)[0]
    cuts, pos = [], 0
    for sz in sizes:
        cuts.append(sg[pos:pos + sz])
        pos += sz
    g_pool_w, g_pool_scale, g_q_norm, g_kv_norm, g_final, g_norm_full, g_conv_full, sq_all, g_c_ctx = cuts
    loss = 0.5 * jnp.sum(sq_all) / d
    update("c_ctx", g_c_ctx)
    update("norm_g", lax.dynamic_slice(g_norm_full.reshape(2, 3, d), (0, 0, me * ng_sh), (2, 3, ng_sh)))
    update("conv_w", lax.dynamic_slice(g_conv_full.reshape(3, d), (0, me * cw_sh), (3, cw_sh)))
    update("pool_w", g_pool_w)
    update("pool_scale", g_pool_scale)
    update("q_norm_g", g_q_norm)
    update("kv_norm_g", g_kv_norm)
    update("final_norm_g", g_final)
    outs = [results[nm] for nm in WEIGHT_NAMES]
    return (loss, grad_x[None], *[o[0] for o in outs], *[o[1] for o in outs], *[o[2] for o in outs],
            *[o[3] for o in outs])
```

```python
import functools
import math

import jax
import jax.numpy as jnp
import numpy as np
from jax import lax
from jax.experimental import pallas as pl
from jax.experimental.pallas import tpu as pltpu

F32 = jnp.float32
BF16 = jnp.bfloat16
MESH = pl.DeviceIdType.MESH
SDS = jax.ShapeDtypeStruct

N_DEV = 8
D_MODEL = 1024
N_MOD = 9
D_FF = 2816
POOL_WINDOWS = (2, 4, 8, 16)
POOL_DIM = 512
POOL_GROUP_DIM = 128
HEADS = 8
QK_NOPE = 64
QK_ROPE = 32
QK_HEAD = QK_NOPE + QK_ROPE
V_HEAD = 64
Q_RANK = 768
KV_RANK = 256
GRID_W = 64
ROPE_THETA = 10000.0
RMS_EPS = 1e-6
ATTN_SCALE = 1.0 / math.sqrt(QK_HEAD)
HEAD_PAD = 128
POOL_PAD = 16
PA_POOL, PA_CQ, PA_KV = 0, 768, 1536
PA_KV_W = 384
PA_W = PA_KV + PA_KV_W

ADAM_LR, ADAM_B1, ADAM_B2, ADAM_EPS, ADAM_WD, ADAM_STEP = 0.001, 0.9, 0.999, 1e-08, 0.01, 10

VMEM_LIMIT_BYTES = 56 * 1024 * 1024

NN = ((1,), (0,))
NT = ((1,), (1,))
TN = ((0,), (0,))


def _cparams():
    return pltpu.CompilerParams(vmem_limit_bytes=VMEM_LIMIT_BYTES)


def _dot(a, b, dims):
    return lax.dot_general(a, b, (dims, ((), ())), preferred_element_type=F32)


def _tile(n, cap, mult=8):
    t = (min(cap, n) // mult) * mult
    while t >= mult:
        if n % t == 0:
            return t
        t -= mult
    return n


def _colsum(x):
    return jnp.sum(x, axis=0, keepdims=True)


def _rms(x):
    r = lax.rsqrt(jnp.mean(x * x, axis=-1, keepdims=True) + RMS_EPS)
    return x * r, r


def _rms_bwd(n, r, dn):
    return r * (dn - n * jnp.mean(dn * n, axis=-1, keepdims=True))


def _rowwise(name, fn, t_rows, tm, n_lat, rows, vecs, outs, accs):
    nt = t_rows // tm
    nlt = n_lat // tm
    n_groups = 2 if nlt < nt else 1

    def grp(i):
        return jnp.where(i >= nlt, 1, 0) if n_groups == 2 else 0

    in_specs = [pl.BlockSpec((tm, w), functools.partial(lambda i, cb: (i, cb), cb=cb)) for (_, w, cb) in rows]
    in_specs += [pl.BlockSpec((1,) + v.shape[1:], lambda i: (grp(i), 0, 0)) for v in vecs]
    out_specs = [pl.BlockSpec((tm, w), lambda i: (i, 0)) for (w, _) in outs]
    out_specs += [pl.BlockSpec((1, 1, w), lambda i: (grp(i), 0, 0)) for w in accs]
    out_shape = [SDS((t_rows, w), dt) for (w, dt) in outs] + [SDS((n_groups, 1, w), F32) for w in accs]
    n_r, n_v, n_o = len(rows), len(vecs), len(outs)

    def body(*refs):
        row_vals = [r[...] for r in refs[:n_r]]
        vec_vals = [v[0] for v in refs[n_r:n_r + n_v]]
        out_refs = refs[n_r + n_v:n_r + n_v + n_o]
        acc_refs = refs[n_r + n_v + n_o:]
        out_vals, acc_vals = fn(row_vals, vec_vals)
        for o_ref, o in zip(out_refs, out_vals):
            o_ref[...] = o.astype(o_ref.dtype)
        if acc_refs:
            i = pl.program_id(0)
            first = (i == 0) | (i == nlt) if n_groups == 2 else i == 0

            @pl.when(first)
            def _():
                for a_ref, a in zip(acc_refs, acc_vals):
                    a_ref[0] = a

            @pl.when(jnp.logical_not(first))
            def _():
                for a_ref, a in zip(acc_refs, acc_vals):
                    a_ref[0] += a

    res = pl.pallas_call(
        body, name=name, grid=(nt,), in_specs=in_specs, out_specs=out_specs, out_shape=out_shape,
        compiler_params=_cparams(),
    )(*[r[0] for r in rows], *vecs)
    return res[:n_o], res[n_o:]


RESIDENT_BYTES = 12 * 1024 * 1024


def _mm(name, pairs, mode, out_dtype, tm_cap=256, tn_cap=512, bias=None):
    a0, b0 = pairs[0]
    if mode == "nn":
        m, n, dims = a0.shape[0], b0.shape[1], NN
    elif mode == "nt":
        m, n, dims = a0.shape[0], b0.shape[0], NT
    else:
        m, n, dims = a0.shape[1], b0.shape[1], TN
    b_bytes = sum(b.size * b.dtype.itemsize for _, b in pairs)
    tn = n if b_bytes <= RESIDENT_BYTES else _tile(n, tn_cap, 128)
    tm = _tile(m, tm_cap, 128 if mode == "tn" else 16)

    def a_spec(a):
        if mode == "tn":
            return pl.BlockSpec((a.shape[0], tm), lambda i, j: (0, i))
        return pl.BlockSpec((tm, a.shape[1]), lambda i, j: (i, 0))

    def b_spec(b):
        if mode == "nt":
            return pl.BlockSpec((tn, b.shape[1]), lambda i, j: (j, 0))
        return pl.BlockSpec((b.shape[0], tn), lambda i, j: (0, j))

    in_specs, flat = [], []
    for a, b in pairs:
        in_specs += [a_spec(a), b_spec(b)]
        flat += [a, b]
    if bias is not None:
        in_specs.append(pl.BlockSpec((1, tn), lambda i, j: (0, j)))
        flat.append(bias)
    n_pairs = len(pairs)

    def body(*refs):
        acc = None
        for p in range(n_pairs):
            t = _dot(refs[2 * p][...], refs[2 * p + 1][...], dims)
            acc = t if acc is None else acc + t
        if bias is not None:
            acc = acc + refs[2 * n_pairs][...]
        refs[-1][...] = acc.astype(refs[-1].dtype)

    return pl.pallas_call(
        body, name=name, grid=(m // tm, n // tn), in_specs=in_specs,
        out_specs=pl.BlockSpec((tm, tn), lambda i, j: (i, j)),
        out_shape=SDS((m, n), out_dtype), compiler_params=_cparams(),
    )(*flat)


def _mm_resid(name, pairs, s, mg, k, coef, n_lat):
    t_rows, n = s.shape
    n_pairs = len(pairs)
    tm = _tile(math.gcd(n_lat, t_rows), 256, 16)
    nlt = n_lat // tm
    n_groups = 2 if nlt < t_rows // tm else 1

    def grp(i):
        return jnp.where(i >= nlt, 1, 0) if n_groups == 2 else 0

    def body(*refs):
        s_ref, mg_ref, so_ref, o_ref = refs[2 * n_pairs:]
        o = _dot(refs[0][...], refs[n_pairs][...], NN)
        for p in range(1, n_pairs):
            o = o + _dot(refs[p][...], refs[n_pairs + p][...], NN)
        gate = mg_ref[0, 3 * k + 2:3 * k + 3, :]
        o_ref[...] = o.astype(BF16)
        so_ref[...] = s_ref[...] + (coef * gate) * o

    row = pl.BlockSpec((tm, n), lambda i: (i, 0))
    return pl.pallas_call(
        body, name=name, grid=(t_rows // tm,),
        in_specs=[pl.BlockSpec((tm, a.shape[1]), lambda i: (i, 0)) for a, _ in pairs]
        + [pl.BlockSpec(b.shape, lambda i: (0, 0)) for _, b in pairs]
        + [row, pl.BlockSpec((1, mg.shape[1], n), lambda i: (grp(i), 0, 0))],
        out_specs=[row, row], out_shape=[SDS((t_rows, n), F32), SDS((t_rows, n), BF16)], compiler_params=_cparams(),
    )(*[a for a, _ in pairs], *[b for _, b in pairs], s, mg)


def _dw_pair(name, a1, a2, b):
    kk, m = a1.shape
    n = b.shape[1]
    tm = _tile(m, 256, 128)

    def body(a1_ref, a2_ref, b_ref, o1_ref, o2_ref):
        bb = b_ref[...]
        o1_ref[...] = _dot(a1_ref[...], bb, TN).astype(BF16)
        o2_ref[...] = _dot(a2_ref[...], bb, TN).astype(BF16)

    col = pl.BlockSpec((kk, tm), lambda i: (0, i))
    out = pl.BlockSpec((tm, n), lambda i: (i, 0))
    return pl.pallas_call(
        body, name=name, grid=(m // tm,), in_specs=[col, col, pl.BlockSpec(b.shape, lambda i: (0, 0))],
        out_specs=[out, out], out_shape=[SDS((m, n), BF16)] * 2, compiler_params=_cparams(),
    )(a1, a2, b)


def _groups(t_rows, tm, n_lat):
    nlt = n_lat // tm
    if nlt < t_rows // tm:
        return 2, (lambda i: jnp.where(i >= nlt, 1, 0)), (lambda i: (i == 0) | (i == nlt))
    return 1, (lambda i: 0), (lambda i: i == 0)


def _accumulate(acc_refs, vals, first):
    @pl.when(first)
    def _():
        for r, v in zip(acc_refs, vals):
            r[0] = v

    @pl.when(jnp.logical_not(first))
    def _():
        for r, v in zip(acc_refs, vals):
            r[0] += v


def _adaln_math(s, m, k):
    n, _ = _rms(s)
    return (n * m[9 + k:10 + k]) * (1.0 + m[3 * k + 1:3 * k + 2]) + m[3 * k:3 * k + 1]


def _ffn_up(name, s, mg, k, n_lat, wg_t, wu_t):
    t_rows, f = s.shape[0], wg_t.shape[0]
    tm = _row_tm(t_rows, n_lat)
    _, grp, _ = _groups(t_rows, tm, n_lat)

    def body(s_ref, mg_ref, wg_ref, wu_ref, u_ref, a_ref, b_ref, h_ref):
        uu = _adaln_math(s_ref[...], mg_ref[0], k).astype(BF16)
        u_ref[...] = uu
        a = _dot(uu, wg_ref[...], NT)
        b = _dot(uu, wu_ref[...], NT)
        sg = jax.nn.sigmoid(a)
        act = a * sg
        a_ref[...] = (b * (sg * (1.0 + a * (1.0 - sg)))).astype(BF16)
        b_ref[...] = act.astype(BF16)
        h_ref[...] = (act * b).astype(BF16)

    w_spec = pl.BlockSpec(wg_t.shape, lambda i: (0, 0))
    o_spec = pl.BlockSpec((tm, f), lambda i: (i, 0))
    row = pl.BlockSpec((tm, s.shape[1]), lambda i: (i, 0))
    return pl.pallas_call(
        body, name=name, grid=(t_rows // tm,),
        in_specs=[row, pl.BlockSpec((1,) + mg.shape[1:], lambda i: (grp(i), 0, 0)), w_spec, w_spec],
        out_specs=[row, o_spec, o_spec, o_spec],
        out_shape=[SDS(s.shape, BF16)] + [SDS((t_rows, f), BF16)] * 3, compiler_params=_cparams(),
    )(s, mg, wg_t, wu_t)


def _ffn_dact(name, ds_out, o, mg, k, coef, n_lat, wd, a, b):
    t_rows, f = ds_out.shape[0], wd.shape[0]
    tm = _row_tm(t_rows, n_lat)
    n_groups, grp, first = _groups(t_rows, tm, n_lat)
    d = ds_out.shape[1]

    def body(ds_ref, o_ref, mg_ref, wd_ref, a_ref, b_ref, do_ref, da_ref, db_ref, dg_ref):
        dd = coef * ds_ref[...]
        do = (dd * mg_ref[0, 3 * k + 2:3 * k + 3, :]).astype(BF16)
        do_ref[...] = do
        _accumulate([dg_ref], [_colsum(dd * o_ref[...].astype(F32))], first(pl.program_id(0)))
        dh = _dot(do, wd_ref[...], NT)
        da_ref[...] = (dh * a_ref[...].astype(F32)).astype(BF16)
        db_ref[...] = (dh * b_ref[...].astype(F32)).astype(BF16)

    row = pl.BlockSpec((tm, d), lambda i: (i, 0))
    t_spec = pl.BlockSpec((tm, f), lambda i: (i, 0))
    return pl.pallas_call(
        body, name=name, grid=(t_rows // tm,),
        in_specs=[row, row, pl.BlockSpec((1,) + mg.shape[1:], lambda i: (grp(i), 0, 0)),
                  pl.BlockSpec(wd.shape, lambda i: (0, 0)), t_spec, t_spec],
        out_specs=[row, t_spec, t_spec, pl.BlockSpec((1, 1, d), lambda i: (grp(i), 0, 0))],
        out_shape=[SDS((t_rows, d), BF16), SDS((t_rows, f), BF16), SDS((t_rows, f), BF16), SDS((n_groups, 1, d), F32)],
        compiler_params=_cparams(),
    )(ds_out, o, mg, wd, a, b)


def _du_adaln(name, pairs, s, ds_out, mg, k, n_lat, after, out_rows=None):
    t_rows, d = s.shape
    tm = _row_tm(t_rows, n_lat)
    n_groups, grp, first = _groups(t_rows, tm, n_lat)
    n_pairs = len(pairs)
    nt, n_ds, n_out = t_rows // tm, ds_out.shape[0] // tm, (out_rows or t_rows) // tm

    def body(*refs):
        s_ref, ds_ref, mg_ref, z_ref, out_ref, dsh_ref, dsc_ref, dgn_ref = refs[2 * n_pairs:]
        i = pl.program_id(0)
        d_u = z_ref[...]
        for p in range(n_pairs):
            d_u = d_u + _dot(refs[p][...], refs[n_pairs + p][...], NN)
        m = mg_ref[0]
        gain, scale = m[9 + k:10 + k], m[3 * k + 1:3 * k + 2]
        n, r = _rms(s_ref[...])
        dxn = d_u * (1.0 + scale)
        ds_in = _rms_bwd(n, r, dxn * gain)
        ds_in = ds_in + (ds_ref[...] if n_ds == nt else jnp.where(i < n_ds, ds_ref[...], 0.0))
        if n_out == nt:
            out_ref[...] = ds_in
        else:
            @pl.when(i < n_out)
            def _():
                out_ref[...] = ds_in
        _accumulate([dsh_ref, dsc_ref, dgn_ref], [_colsum(d_u), _colsum(d_u * (n * gain)), _colsum(dxn * n)], first(i))

    row = pl.BlockSpec((tm, d), lambda i: (i, 0))
    acc = pl.BlockSpec((1, 1, d), lambda i: (grp(i), 0, 0))
    res = pl.pallas_call(
        body, name=name, grid=(t_rows // tm,),
        in_specs=[pl.BlockSpec((tm, a.shape[1]), lambda i: (i, 0)) for a, _ in pairs]
        + [pl.BlockSpec(w.shape, lambda i: (0, 0)) for _, w in pairs]
        + [row, pl.BlockSpec((tm, d), lambda i: (jnp.minimum(i, n_ds - 1), 0)),
           pl.BlockSpec((1,) + mg.shape[1:], lambda i: (grp(i), 0, 0)), pl.BlockSpec((1, d), lambda i: (0, 0))],
        out_specs=[pl.BlockSpec((tm, d), lambda i: (jnp.minimum(i, n_out - 1), 0)), acc, acc, acc],
        out_shape=[SDS((n_out * tm, d), F32)] + [SDS((n_groups, 1, d), F32)] * 3, compiler_params=_cparams(),
    )(*[a for a, _ in pairs], *[w for _, w in pairs], s, ds_out, mg, after)
    return res[0], res[1:]


def _adaln_mm(name, s, mg, k, n_lat, w_t):
    rows, d = s.shape
    tm = _row_tm(rows, n_lat)
    _, grp, _ = _groups(rows, tm, n_lat)
    n = w_t.shape[0]

    def body(s_ref, mg_ref, w_ref, u_ref, y_ref):
        uu = _adaln_math(s_ref[...], mg_ref[0], k).astype(BF16)
        u_ref[...] = uu
        y_ref[...] = _dot(uu, w_ref[...], NT)

    row = pl.BlockSpec((tm, d), lambda i: (i, 0))
    return pl.pallas_call(
        body, name=name, grid=(rows // tm,),
        in_specs=[row, pl.BlockSpec((1,) + mg.shape[1:], lambda i: (grp(i), 0, 0)), pl.BlockSpec(w_t.shape, lambda i: (0, 0))],
        out_specs=[row, pl.BlockSpec((tm, n), lambda i: (i, 0))],
        out_shape=[SDS((rows, d), BF16), SDS((rows, n), F32)], compiler_params=_cparams(),
    )(s, mg, w_t)


def _gate_mm(name, ds_out, o, mg, k, coef, n_lat, w):
    t_rows, d = ds_out.shape
    tm = _row_tm(t_rows, n_lat)
    n_groups, grp, first = _groups(t_rows, tm, n_lat)
    n = w.shape[0]

    def body(ds_ref, o_ref, mg_ref, w_ref, do_ref, y_ref, dg_ref):
        dd = coef * ds_ref[...]
        do = (dd * mg_ref[0, 3 * k + 2:3 * k + 3, :]).astype(BF16)
        do_ref[...] = do
        _accumulate([dg_ref], [_colsum(dd * o_ref[...].astype(F32))], first(pl.program_id(0)))
        y_ref[...] = _dot(do, w_ref[...], NT)

    row = pl.BlockSpec((tm, d), lambda i: (i, 0))
    return pl.pallas_call(
        body, name=name, grid=(t_rows // tm,),
        in_specs=[row, row, pl.BlockSpec((1,) + mg.shape[1:], lambda i: (grp(i), 0, 0)), pl.BlockSpec(w.shape, lambda i: (0, 0))],
        out_specs=[row, pl.BlockSpec((tm, n), lambda i: (i, 0)), pl.BlockSpec((1, 1, d), lambda i: (grp(i), 0, 0))],
        out_shape=[SDS((t_rows, d), BF16), SDS((t_rows, n), F32), SDS((n_groups, 1, d), F32)],
        compiler_params=_cparams(),
    )(ds_out, o, mg, w)


def _row_tm(t_rows, n_lat):
    return _tile(math.gcd(t_rows, n_lat), 256, 16)


def _rmsnorm_fwd(name, x, width, colblk, gain, t_rows):
    def fn(rv, vv):
        n, _ = _rms(rv[0])
        return [n * vv[0]], []

    (y,), _ = _rowwise(name, fn, t_rows, _tile(t_rows, 256, 16), t_rows, [(x, width, colblk)],
                       [gain.reshape(1, 1, width)], [(width, BF16)], [])
    return y


def _rmsnorm_bwd(name, x, width, colblk, dy, gain, t_rows, out_dtype=F32):
    def fn(rv, vv):
        n, r = _rms(rv[0])
        return [_rms_bwd(n, r, rv[1] * vv[0])], [_colsum(rv[1] * n)]

    (dx,), (dgain,) = _rowwise(name, fn, t_rows, _tile(t_rows, 256, 16), t_rows,
                               [(x, width, colblk), (dy, width, 0)], [gain.reshape(1, 1, width)],
                               [(width, out_dtype)], [width])
    return dx, dgain


def _final_loss(name, h, target, gain):
    t_rows = h.shape[0]
    inv_d = 1.0 / D_MODEL

    def fn(rv, vv):
        g = vv[0]
        n, r = _rms(rv[0])
        e = n * g - rv[1]
        dy = e * inv_d
        return [_rms_bwd(n, r, dy * g)], [_colsum(e * e), _colsum(dy * n)]

    (dh,), (sq, dgain) = _rowwise(name, fn, t_rows, _tile(t_rows, 256, 16), t_rows,
                                  [(h, D_MODEL, 0), (target, D_MODEL, 0)], [gain.reshape(1, 1, D_MODEL)],
                                  [(D_MODEL, F32)], [D_MODEL, D_MODEL])
    return dh, sq, dgain


def _exact_dot(x, m_ref):
    hi = x.astype(BF16)
    lo = (x - hi.astype(F32)).astype(BF16)
    return _dot(hi, m_ref[...], NN) + _dot(lo, m_ref[...], NN)


def _rope(name, z, width, colblk, cos32, sin32, layout, backward, out_dtype, remap=None):
    t_rows = cos32.shape[0]
    expand, plain, perm = layout
    w_in = remap.shape[1] if (remap is not None and backward) else width
    w_out = remap.shape[1] if (remap is not None and not backward) else width
    extra = [] if remap is None else [remap.T if backward else remap]

    def body(z_ref, c_ref, s_ref, e_ref, m_ref, p_ref, *rest):
        o_ref = rest[-1]
        zz = z_ref[...]
        if remap is not None and backward:
            zz = _exact_dot(zz, rest[0])
        cos = _exact_dot(c_ref[...], e_ref) + m_ref[...]
        sin = _exact_dot(s_ref[...], e_ref)
        rot = _exact_dot(zz * sin if backward else zz, p_ref)
        if not backward:
            rot = rot * sin
        res = zz * cos + rot
        if remap is not None and not backward:
            res = _dot(res.astype(BF16), rest[0][...], NN)
        o_ref[...] = res.astype(o_ref.dtype)

    tm = _tile(t_rows, 256, 16)
    f_spec = pl.BlockSpec((tm, QK_ROPE), lambda i: (i, 0))
    return pl.pallas_call(
        body, name=name, grid=(t_rows // tm,),
        in_specs=[pl.BlockSpec((tm, w_in), lambda i: (i, colblk)), f_spec, f_spec,
                  pl.BlockSpec((QK_ROPE, width), lambda i: (0, 0)), pl.BlockSpec((1, width), lambda i: (0, 0)),
                  pl.BlockSpec((width, width), lambda i: (0, 0))]
        + [pl.BlockSpec(e.shape, lambda i: (0, 0)) for e in extra],
        out_specs=pl.BlockSpec((tm, w_out), lambda i: (i, 0)), out_shape=SDS((t_rows, w_out), out_dtype),
        compiler_params=_cparams(),
    )(z, cos32, sin32, expand, plain, perm.T if backward else perm, *extra)


def _window_sum(x, w, transposed):
    n_rows = x.shape[0]
    zeros = jnp.zeros((POOL_PAD, x.shape[1]), F32)
    y = jnp.concatenate([zeros, x, zeros], axis=0)
    total = n_rows + 2 * POOL_PAD
    if transposed:
        y = y + pltpu.roll(y, total - 1, 0)
    else:
        y = y + pltpu.roll(y, 1, 0)
    step = 1
    while 2 * step < w:
        y = pltpu.roll(y, step, 0) + pltpu.roll(y, total - step, 0)
        step *= 2
    return y[POOL_PAD:POOL_PAD + n_rows]


def _window_count(n_rows, w):
    t = lax.broadcasted_iota(jnp.int32, (n_rows, 1), 0)
    lo = jnp.maximum(t - w // 2, 0)
    hi = jnp.minimum(t + (w - w // 2 - 1), n_rows - 1)
    return (hi - lo + 1).astype(F32)


def _pool_fwd(name, proj, n_rows, w_grp, scale):
    def body(x_ref, w_ref, sc_ref, y_ref, p_ref):
        for g, w in enumerate(POOL_WINDOWS):
            cols = slice(g * POOL_GROUP_DIM, (g + 1) * POOL_GROUP_DIM)
            x = x_ref[:, cols]
            p = _window_sum(x, w, False) * (1.0 / _window_count(n_rows, w)) - x
            pb = p.astype(BF16)
            p_ref[:, cols] = pb
            y_ref[:, cols] = (_dot(pb, w_ref[g], NN) * sc_ref[:, cols]).astype(BF16)

    blk = pl.BlockSpec((n_rows, POOL_DIM), lambda i: (0, 0))
    return pl.pallas_call(
        body, name=name, grid=(1,),
        in_specs=[blk, pl.BlockSpec(w_grp.shape, lambda i: (0, 0, 0)), pl.BlockSpec((1, POOL_DIM), lambda i: (0, 0))],
        out_specs=[blk, blk], out_shape=[SDS((n_rows, POOL_DIM), BF16)] * 2, compiler_params=_cparams(),
    )(proj, w_grp, scale)


def _pool_bwd(name, dcat, n_rows, p, w_grp, scale):
    def body(dy_ref, p_ref, w_ref, sc_ref, dx_ref, dw_ref, dsc_ref):
        for g, w in enumerate(POOL_WINDOWS):
            cols = slice(g * POOL_GROUP_DIM, (g + 1) * POOL_GROUP_DIM)
            dy = dy_ref[:, cols]
            pb = p_ref[:, cols]
            pw = _dot(pb, w_ref[g], NN)
            dsc_ref[:, cols] = _colsum(dy * pw)
            dpw = (dy * sc_ref[:, cols]).astype(BF16)
            dw_ref[g] = _dot(pb, dpw, TN)
            dp = _dot(dpw, w_ref[g], NT)
            dx_ref[:, cols] = (_window_sum(dp * (1.0 / _window_count(n_rows, w)), w, True) - dp).astype(BF16)

    blk = pl.BlockSpec((n_rows, POOL_DIM), lambda i: (0, 0))
    w_spec = pl.BlockSpec(w_grp.shape, lambda i: (0, 0, 0))
    v_spec = pl.BlockSpec((1, POOL_DIM), lambda i: (0, 0))
    return pl.pallas_call(
        body, name=name, grid=(1,), in_specs=[blk, blk, w_spec, v_spec], out_specs=[blk, w_spec, v_spec],
        out_shape=[SDS((n_rows, POOL_DIM), BF16), SDS(w_grp.shape, F32), SDS((1, POOL_DIM), F32)],
        compiler_params=_cparams(),
    )(dcat, p, w_grp, scale)


def _head_keys(kv_blk, k_rope):
    lane = lax.broadcasted_iota(jnp.int32, (1, HEAD_PAD), 1)
    return jnp.where(lane < QK_NOPE, kv_blk, k_rope)


def _attn_fwd(name, q, kv, k_rope, n_q):
    n_k = kv.shape[0]
    h = kv.shape[1] // HEAD_PAD
    tq = _tile(n_q, 256, 16)

    def body(q_ref, kv_ref, kr_ref, o_ref, lse_ref):
        kvb = kv_ref[...]
        s = _dot(q_ref[...], _head_keys(kvb, kr_ref[...]), NT) * ATTN_SCALE
        m = jnp.max(s, axis=-1, keepdims=True)
        e = jnp.exp(s - m)
        l = jnp.sum(e, axis=-1, keepdims=True)
        p = (e * (1.0 / l)).astype(BF16)
        lane = lax.broadcasted_iota(jnp.int32, (1, HEAD_PAD), 1)
        o_ref[...] = jnp.where(lane >= QK_NOPE, _dot(p, kvb, NN), 0.0).astype(BF16)
        lse_ref[...] = m + jnp.log(l)

    blk = pl.BlockSpec((tq, HEAD_PAD), lambda hh, i: (i, hh))
    return pl.pallas_call(
        body, name=name, grid=(h, n_q // tq),
        in_specs=[blk, pl.BlockSpec((n_k, HEAD_PAD), lambda hh, i: (0, hh)),
                  pl.BlockSpec((n_k, HEAD_PAD), lambda hh, i: (0, 0))],
        out_specs=[blk, pl.BlockSpec((None, tq, 1), lambda hh, i: (hh, i, 0))],
        out_shape=[SDS((n_q, h * HEAD_PAD), BF16), SDS((h, n_q, 1), F32)], compiler_params=_cparams(),
    )(q, kv, k_rope)


def _attn_bwd(name, q, kv, k_rope, o, lse, dy, dy_col0, n_q):
    n_k = kv.shape[0]
    h = kv.shape[1] // HEAD_PAD
    tq = _tile(n_q, 256, 16)
    n_i = n_q // tq

    def body(q_ref, kv_ref, kr_ref, o_ref, lse_ref, do_ref, dq_ref, dkv_ref, dkr_ref, acc_k, acc_v):
        hh, i = pl.program_id(0), pl.program_id(1)
        qq, kvb = q_ref[...], kv_ref[...]
        kk = _head_keys(kvb, kr_ref[...])
        d_o = do_ref[...]
        dd = d_o.astype(BF16)
        s = _dot(qq, kk, NT) * ATTN_SCALE
        p = jnp.exp(s - lse_ref[...])
        dp = _dot(dd, kvb, NT)
        delta = jnp.sum(d_o * o_ref[...].astype(F32), axis=-1, keepdims=True)
        ds = (p * (dp - delta) * ATTN_SCALE).astype(BF16)
        dq_ref[...] = _dot(ds, kk, NN)
        dk = _dot(ds, qq, TN)
        dv = _dot(p.astype(BF16), dd, TN)

        @pl.when(i == 0)
        def _():
            acc_k[...] = dk
            acc_v[...] = dv

        @pl.when(i > 0)
        def _():
            acc_k[...] += dk
            acc_v[...] += dv

        @pl.when(i == n_i - 1)
        def _():
            lane = lax.broadcasted_iota(jnp.int32, (1, HEAD_PAD), 1)
            dkv_ref[...] = jnp.where(lane < QK_NOPE, acc_k[...], acc_v[...]).astype(BF16)
            rope = jnp.where((lane >= QK_NOPE) & (lane < QK_HEAD), acc_k[...], 0.0)

            @pl.when(hh == 0)
            def _():
                dkr_ref[...] = rope

            @pl.when(hh > 0)
            def _():
                dkr_ref[...] += rope

    blk = pl.BlockSpec((tq, HEAD_PAD), lambda hh, i: (i, hh))
    kv_spec = pl.BlockSpec((n_k, HEAD_PAD), lambda hh, i: (0, hh))
    shared = pl.BlockSpec((n_k, HEAD_PAD), lambda hh, i: (0, 0))
    return pl.pallas_call(
        body, name=name, grid=(h, n_i),
        in_specs=[blk, kv_spec, shared, blk, pl.BlockSpec((None, tq, 1), lambda hh, i: (hh, i, 0)),
                  pl.BlockSpec((tq, HEAD_PAD), lambda hh, i: (i, dy_col0 + hh))],
        out_specs=[blk, kv_spec, shared],
        out_shape=[SDS((n_q, h * HEAD_PAD), F32), SDS((n_k, h * HEAD_PAD), BF16), SDS((n_k, HEAD_PAD), F32)],
        scratch_shapes=[pltpu.VMEM((n_k, HEAD_PAD), F32), pltpu.VMEM((n_k, HEAD_PAD), F32)],
        compiler_params=_cparams(),
    )(q, kv, k_rope, o, lse, dy)


CONV_COLS = 256


def _shift_rows(x, d):
    n_rows = x.shape[0]
    t = lax.broadcasted_iota(jnp.int32, (n_rows, 1), 0)
    if d > 0:
        return jnp.where(t >= d, pltpu.roll(x, d, 0), 0.0)
    return jnp.where(t < n_rows + d, pltpu.roll(x, n_rows + d, 0), 0.0)


def _conv_fwd(name, z3, conv_w):
    n_rows = z3.shape[0]
    nb = D_MODEL // CONV_COLS

    def body(b_ref, c_ref, v_ref, w_ref, y_ref):
        z = c_ref[...] * v_ref[...]
        zc = w_ref[0:1, :] * _shift_rows(z, 1) + w_ref[1:2, :] * z + w_ref[2:3, :] * _shift_rows(z, -1)
        y_ref[...] = (b_ref[...] * zc).astype(BF16)

    def part(k):
        return pl.BlockSpec((n_rows, CONV_COLS), lambda j: (0, k * nb + j))

    return pl.pallas_call(
        body, name=name, grid=(nb,),
        in_specs=[part(0), part(1), part(2), pl.BlockSpec((3, CONV_COLS), lambda j: (0, j))],
        out_specs=pl.BlockSpec((n_rows, CONV_COLS), lambda j: (0, j)),
        out_shape=SDS((n_rows, D_MODEL), BF16), compiler_params=_cparams(),
    )(z3, z3, z3, conv_w)


def _conv_bwd(name, dy, z3, conv_w):
    n_rows = z3.shape[0]
    nb = D_MODEL // CONV_COLS

    def body(dy_ref, b_ref, c_ref, v_ref, w_ref, db_ref, dc_ref, dv_ref, dw_ref):
        c, v, d_y = c_ref[...], v_ref[...], dy_ref[...]
        z = c * v
        z_dn, z_up = _shift_rows(z, 1), _shift_rows(z, -1)
        zc = w_ref[0:1, :] * z_dn + w_ref[1:2, :] * z + w_ref[2:3, :] * z_up
        db_ref[...] = (d_y * zc).astype(BF16)
        dzc = d_y * b_ref[...]
        dz = w_ref[0:1, :] * _shift_rows(dzc, -1) + w_ref[1:2, :] * dzc + w_ref[2:3, :] * _shift_rows(dzc, 1)
        dc_ref[...] = (dz * v).astype(BF16)
        dv_ref[...] = (dz * c).astype(BF16)
        dw_ref[0:1, :] = _colsum(dzc * z_dn)
        dw_ref[1:2, :] = _colsum(dzc * z)
        dw_ref[2:3, :] = _colsum(dzc * z_up)

    def part(k):
        return pl.BlockSpec((n_rows, CONV_COLS), lambda j: (0, k * nb + j))

    col = pl.BlockSpec((n_rows, CONV_COLS), lambda j: (0, j))
    w_spec = pl.BlockSpec((3, CONV_COLS), lambda j: (0, j))
    return pl.pallas_call(
        body, name=name, grid=(nb,), in_specs=[col, part(0), part(1), part(2), w_spec],
        out_specs=[col, col, col, w_spec],
        out_shape=[SDS((n_rows, D_MODEL), BF16)] * 3 + [SDS((3, D_MODEL), F32)], compiler_params=_cparams(),
    )(dy, z3, z3, z3, conv_w)


def _silu_rows(name, x):
    def body(x_ref, s_ref, d_ref):
        xx = x_ref[...]
        sg = jax.nn.sigmoid(xx)
        s_ref[...] = (xx * sg).astype(BF16)
        d_ref[...] = sg * (1.0 + xx * (1.0 - sg))

    return pl.pallas_call(body, name=name, out_shape=[SDS(x.shape, BF16), SDS(x.shape, F32)])(x)


def _sum_rows(name, x, scale=None):
    r, n = x.shape
    tn = _tile(n, 32768, 128)

    def body(*refs):
        acc = jnp.sum(refs[0][...].astype(F32), axis=0, keepdims=True)
        if scale is not None:
            acc = acc * refs[1][...]
        refs[-1][...] = acc

    in_specs = [pl.BlockSpec((r, tn), lambda j: (0, j))]
    args = [x]
    if scale is not None:
        in_specs.append(pl.BlockSpec((1, tn), lambda j: (0, j)))
        args.append(scale)
    return pl.pallas_call(body, name=name, grid=(n // tn,), in_specs=in_specs,
                          out_specs=pl.BlockSpec((1, tn), lambda j: (0, j)), out_shape=SDS((1, n), F32))(*args)


def _me_operand(me):
    return jnp.reshape(me, (1,)).astype(jnp.int32)


def _sum_slots(name, slots, src, me):
    n_slots, r, c = slots.shape
    tr = _tile(r, 432, 16)

    def body(me_ref, own_ref, x_ref, o_ref):
        acc = own_ref[...].astype(F32)
        for sl in range(n_slots):
            acc = acc + x_ref[sl].astype(F32)
        o_ref[...] = acc

    grid_spec = pltpu.PrefetchScalarGridSpec(
        num_scalar_prefetch=1, grid=(r // tr,),
        in_specs=[pl.BlockSpec((None, tr, c), lambda i, me_ref: (me_ref[0], i, 0)),
                  pl.BlockSpec((n_slots, tr, c), lambda i, me_ref: (0, i, 0))],
        out_specs=pl.BlockSpec((tr, c), lambda i, me_ref: (i, 0)))
    return pl.pallas_call(body, name=name, grid_spec=grid_spec, out_shape=SDS((r, c), F32),
                          compiler_params=_cparams())(_me_operand(me), src, slots)


def _adamw(name, w, g, m, v):
    shape = w.shape
    cols = shape[-1]
    rows = w.size // cols
    tr = _tile(rows, 512, 8)
    bc1 = 1.0 - ADAM_B1 ** ADAM_STEP
    bc2 = 1.0 - ADAM_B2 ** ADAM_STEP

    def body(w_ref, g_ref, m_ref, v_ref, d_ref, nm_ref, nv_ref):
        gg = g_ref[...]
        nm = ADAM_B1 * m_ref[...] + (1.0 - ADAM_B1) * gg
        nv = ADAM_B2 * v_ref[...] + (1.0 - ADAM_B2) * (gg * gg)
        nm_ref[...] = nm
        nv_ref[...] = nv
        d_ref[...] = -ADAM_LR * ((nm / bc1) / (jnp.sqrt(nv / bc2) + ADAM_EPS) + ADAM_WD * w_ref[...])

    spec = pl.BlockSpec((tr, cols), lambda i: (i, 0))
    outs = pl.pallas_call(body, name=name, grid=(rows // tr,), in_specs=[spec] * 4, out_specs=[spec] * 3,
                          out_shape=[SDS((rows, cols), F32)] * 3, compiler_params=_cparams())(
        w.reshape(rows, cols), g.reshape(rows, cols), m.reshape(rows, cols), v.reshape(rows, cols))
    return tuple(t.reshape(shape) for t in outs)


def _exchange(name, x, scatter, after=None):
    blk = x.shape[1:] if scatter else x.shape
    extra = [] if after is None else [after]

    def body(x_ref, *rest):
        out_ref, send_sems, recv_sems, local_sem = rest[len(extra):]
        mx, my, mc = lax.axis_index("x"), lax.axis_index("y"), lax.axis_index("c")
        me = 4 * mx + 2 * my + mc
        own = pltpu.make_async_copy(x_ref.at[me] if scatter else x_ref, out_ref.at[me], local_sem)
        own.start()
        copies = []
        for kk in range(1, N_DEV):
            px = jnp.bitwise_xor(mx, (kk >> 2) & 1)
            py = jnp.bitwise_xor(my, (kk >> 1) & 1)
            pc = jnp.bitwise_xor(mc, kk & 1)
            peer = 4 * px + 2 * py + pc
            send = pltpu.make_async_remote_copy(
                src_ref=x_ref.at[peer] if scatter else x_ref, dst_ref=out_ref.at[me],
                send_sem=send_sems.at[kk - 1], recv_sem=recv_sems.at[kk - 1],
                device_id=(px, py, pc), device_id_type=MESH)
            send.start()
            arrival = pltpu.make_async_remote_copy(
                src_ref=x_ref.at[peer] if scatter else x_ref, dst_ref=out_ref.at[peer],
                send_sem=send_sems.at[kk - 1], recv_sem=recv_sems.at[kk - 1],
                device_id=(px, py, pc), device_id_type=MESH)
            copies.append((send, arrival))
        for send, arrival in copies:
            arrival.wait_recv()
            send.wait_send()
        own.wait()

    return pl.pallas_call(
        body, name=name, out_shape=SDS((N_DEV,) + tuple(blk), x.dtype),
        in_specs=[pl.BlockSpec(memory_space=pl.ANY)] * (1 + len(extra)), out_specs=pl.BlockSpec(memory_space=pl.ANY),
        scratch_shapes=[pltpu.SemaphoreType.DMA((N_DEV - 1,)), pltpu.SemaphoreType.DMA((N_DEV - 1,)),
                        pltpu.SemaphoreType.DMA],
    )(x, *extra)


def _rope_perm(pre, reps, post):
    half = QK_ROPE // 4
    width = reps * (pre + QK_ROPE) + post
    p = np.zeros((width, width), np.float32)
    for rep in range(reps):
        s0 = rep * (pre + QK_ROPE) + pre
        for base in (s0, s0 + 2 * half):
            for i in range(half):
                p[base + half + i, base + i] = -1.0
                p[base + i, base + half + i] = 1.0
    return p


def _rope_layout(pre, reps, post):
    width = reps * (pre + QK_ROPE) + post
    expand = np.zeros((QK_ROPE, width), np.float32)
    plain = np.ones((1, width), np.float32)
    for rep in range(reps):
        s0 = rep * (pre + QK_ROPE) + pre
        expand[np.arange(QK_ROPE), s0 + np.arange(QK_ROPE)] = 1.0
        plain[0, s0:s0 + QK_ROPE] = 0.0
    return jnp.asarray(expand, BF16), jnp.asarray(plain, F32), jnp.asarray(_rope_perm(pre, reps, post), BF16)


def _head_spread():
    spread = np.zeros((HEADS * QK_HEAD, HEADS * HEAD_PAD), np.float32)
    for hh in range(HEADS):
        spread[hh * QK_HEAD + np.arange(QK_HEAD), hh * HEAD_PAD + np.arange(QK_HEAD)] = 1.0
    return jnp.asarray(spread, BF16)


def _rope_factors(n_lat, t_rows):
    half = QK_ROPE // 4
    pos = jnp.arange(n_lat)
    freqs = jnp.power(ROPE_THETA, -jnp.arange(0, 2 * half, 2, dtype=F32) / (2 * half))
    ang_r = (pos // GRID_W).astype(F32)[:, None] * freqs
    ang_c = (pos % GRID_W).astype(F32)[:, None] * freqs
    ang = jnp.concatenate([ang_r, ang_r, ang_c, ang_c], axis=-1)
    rest = t_rows - n_lat
    return (jnp.concatenate([jnp.cos(ang), jnp.ones((rest, QK_ROPE), F32)], axis=0),
            jnp.concatenate([jnp.sin(ang), jnp.zeros((rest, QK_ROPE), F32)], axis=0))


def _ffn_half_fwd(tag, s, mg, k, feed, i, coef, n_lat):
    wg_t, wu_t = feed.weights(f"{tag}_up", [f"gate_t{i}", f"up_t{i}"], s)
    u, a, b, hid = _ffn_up(f"{tag}_up", s, mg, k, n_lat, wg_t, wu_t)
    (wd,) = feed.weights(f"{tag}_down", [f"down{i}"], hid)
    s_out, o = _mm_resid(f"{tag}_down", [(hid, wd)], s, mg, k, coef, n_lat)
    return s_out, (s, u, a, b, hid, o, wg_t, wu_t, wd)


def _ffn_half_bwd(tag, ds_out, saved, mg, k, feed, i, coef, n_lat, out_rows=None):
    s, u, a, b, hid, o, wg_t, wu_t, wd = saved
    do, da, db, dgate = _ffn_dact(f"{tag}_dact", ds_out, o, mg, k, coef, n_lat, wd, a, b)
    dwd = _mm(f"{tag}_dwd", [(hid, do)], "tn", BF16)
    dwg_t, dwu_t = _dw_pair(f"{tag}_dwgu", da, db, u)
    token = feed.grads(tag, {f"down{i}": dwd, f"gate_t{i}": dwg_t, f"up_t{i}": dwu_t})
    ds_in, (dshift, dscale, dgain) = _du_adaln(f"{tag}_du", [(da, wg_t), (db, wu_t)], s, ds_out, mg, k, n_lat,
                                               _after(token), out_rows)
    return ds_in, dict(shift=dshift, scale=dscale, gate=dgate, gain=dgain)


def _after(token):
    return jnp.zeros((1, D_MODEL), F32) + token


def _mod_grad(parts, n_groups):
    rows = []
    zero = jnp.zeros((n_groups, 1, D_MODEL), F32)
    for k in range(3):
        for nm in ("shift", "scale", "gate"):
            t = parts[k].get(nm, zero)
            if t.shape[0] < n_groups:
                t = jnp.concatenate([t, jnp.zeros((n_groups - t.shape[0], 1, D_MODEL), F32)], axis=0)
            rows.append(t)
    return jnp.concatenate(rows, axis=1).reshape(n_groups, N_MOD * D_MODEL)


def _local_step(x, ctx, target, mod_h, mod_g, norm_g, feed, pool_w, pool_scale, q_norm_g, kv_norm_g, conv_w,
                final_norm_g):
    n_lat, n_ctx = x.shape[0], ctx.shape[0]
    t_all = n_lat + n_ctx
    mg0 = jnp.stack([jnp.concatenate([mod_h[0], norm_g[0]], axis=0), jnp.concatenate([mod_g, norm_g[0]], axis=0)])
    mg1 = jnp.concatenate([mod_h[1], norm_g[1]], axis=0)[None]

    s0 = jnp.concatenate([x, ctx], axis=0)
    s1, sv_f00 = _ffn_half_fwd("l0f0", s0, mg0, 0, feed, 0, 0.5, n_lat)

    w_in, w_uq, w_ukv_t, w_ab_out = feed.weights("l0m", ["in_t", "uq", "ukv_t", "ab_out"], s1)
    kv_rows = KV_RANK + QK_ROPE
    w_in_t = jnp.concatenate([
        w_in[:POOL_DIM], jnp.zeros((PA_CQ - POOL_DIM, D_MODEL), BF16), w_in[POOL_DIM:POOL_DIM + Q_RANK],
        w_in[POOL_DIM + Q_RANK:], jnp.zeros((PA_KV_W - kv_rows, D_MODEL), BF16)], axis=0)
    ua, proj = _adaln_mm("l0m_proj", s1, mg0, 1, n_lat, w_in_t)
    pool_y, pool_p = _pool_fwd("l0m_pool", proj, n_lat, pool_w.astype(BF16), pool_scale)
    nq = _rmsnorm_fwd("l0m_qnorm", proj, Q_RANK, PA_CQ // Q_RANK, q_norm_g, n_lat)
    q_lin = _mm("l0m_q", [(nq, w_uq)], "nn", F32, 512, 768)
    cos32, sin32 = _rope_factors(n_lat, t_all)
    lay_q, lay_k = _rope_layout(QK_NOPE, HEADS, 0), _rope_layout(KV_RANK, 1, PA_KV_W - kv_rows)
    spread = _head_spread()
    q_flat = _rope("l0m_qrope", q_lin, Q_RANK, 0, cos32[:n_lat], sin32[:n_lat], lay_q, False, BF16, spread)
    kvr = _rope("l0m_krope", proj, PA_KV_W, PA_KV // PA_KV_W, cos32, sin32, lay_k, False, F32)
    nkv = _rmsnorm_fwd("l0m_kvnorm", kvr, KV_RANK, 0, kv_norm_g, t_all)
    kv = _mm("l0m_kv", [(nkv, w_ukv_t)], "nt", BF16, 768, 512)
    k_rope = jnp.pad(kvr[:, KV_RANK:KV_RANK + QK_ROPE].astype(BF16), ((0, 0), (QK_NOPE, HEAD_PAD - QK_HEAD)))
    o_flat, lse = _attn_fwd("l0m_attn", q_flat, kv, k_rope, n_lat)
    w_o_pad = jnp.pad(w_ab_out[POOL_DIM:].reshape(HEADS, V_HEAD, D_MODEL),
                      ((0, 0), (HEAD_PAD - V_HEAD, 0), (0, 0))).reshape(HEADS * HEAD_PAD, D_MODEL)
    w_o_pool = w_ab_out[:POOL_DIM]
    h1 = s1[:n_lat]
    h2, mix_o = _mm_resid("l0m_out", [(pool_y, w_o_pool), (o_flat, w_o_pad)], h1, mg0[:1], 1, 1.0, n_lat)

    h3, sv_f01 = _ffn_half_fwd("l0f1", h2, mg0[:1], 2, feed, 1, 0.5, n_lat)

    h4, sv_f10 = _ffn_half_fwd("l1f0", h3, mg1, 0, feed, 2, 0.5, n_lat)
    w_cin_t, w_c_out = feed.weights("l1m", ["cin_t", "c_out"], h4)
    uc, z3 = _adaln_mm("l1m_in", h4, mg1, 1, n_lat, w_cin_t)
    yc = _conv_fwd("l1m_conv", z3, conv_w)
    h5, conv_o = _mm_resid("l1m_out", [(yc, w_c_out)], h4, mg1, 1, 1.0, n_lat)
    h6, sv_f11 = _ffn_half_fwd("l1f1", h5, mg1, 2, feed, 3, 0.5, n_lat)

    dh6, sq_cols, d_final_g = _final_loss("loss_head", h6, target, final_norm_g)
    g = {}
    dh5, g["f11"] = _ffn_half_bwd("l1f1", dh6, sv_f11, mg1, 2, feed, 3, 0.5, n_lat)

    do_c, dyc, dgate_c = _gate_mm("l1m_dy", dh5, conv_o, mg1, 1, 1.0, n_lat, w_c_out)
    d_c_out = _mm("l1m_dwout", [(yc, do_c)], "tn", BF16)
    db_, dc_, dv_, d_conv_w = _conv_bwd("l1m_dconv", dyc, z3, conv_w)
    dz3 = jnp.concatenate([db_, dc_, dv_], axis=-1)
    d_cin_t = _mm("l1m_dwin", [(dz3, uc)], "tn", BF16)
    token = feed.grads("l1m", {"c_out": d_c_out, "cin_t": d_cin_t})
    dh4, (dsh_c, dsc_c, dgn_c) = _du_adaln("l1m_du", [(dz3, w_cin_t)], h4, dh5, mg1, 1, n_lat, _after(token))
    dh3, g["f10"] = _ffn_half_bwd("l1f0", dh4, sv_f10, mg1, 0, feed, 2, 0.5, n_lat)

    dh2, g["f01"] = _ffn_half_bwd("l0f1", dh3, sv_f01, mg0[:1], 2, feed, 1, 0.5, n_lat)

    w_back = jnp.concatenate([w_o_pool, w_o_pad], axis=0)
    do_a, dcat, dgate_a = _gate_mm("l0m_dcat", dh2, mix_o, mg0[:1], 1, 1.0, n_lat, w_back)
    d_o_pad = _mm("l0m_dwout_a", [(o_flat, do_a)], "tn", BF16)
    d_ab_out = jnp.concatenate([
        _mm("l0m_dwout_p", [(pool_y, do_a)], "tn", BF16),
        d_o_pad.reshape(HEADS, HEAD_PAD, D_MODEL)[:, HEAD_PAD - V_HEAD:].reshape(HEADS * V_HEAD, D_MODEL)], axis=0)
    d_pool_x, d_pool_w, d_pool_scale = _pool_bwd("l0m_dpool", dcat, n_lat, pool_p, pool_w.astype(BF16), pool_scale)
    dq_flat, dkv, dk_rope = _attn_bwd("l0m_dattn", q_flat, kv, k_rope, o_flat, lse, dcat, POOL_DIM // HEAD_PAD, n_lat)
    dq_lin = _rope("l0m_dqrope", dq_flat, Q_RANK, 0, cos32[:n_lat], sin32[:n_lat], lay_q, True, BF16, spread)
    d_uq = _mm("l0m_dwuq", [(nq, dq_lin)], "tn", BF16, 768, 768)
    dnq = _mm("l0m_dnq", [(dq_lin, w_uq)], "nt", F32, 512, 768)
    dcq, d_q_norm_g = _rmsnorm_bwd("l0m_dqnorm", proj, Q_RANK, PA_CQ // Q_RANK, dnq, q_norm_g, n_lat, BF16)
    dnkv = _mm("l0m_dnkv", [(dkv, w_ukv_t)], "nn", F32, 768, 256)
    d_ukv_t = _mm("l0m_dwukv", [(dkv, nkv)], "tn", BF16, 512, 256)
    dckv, d_kv_norm_g = _rmsnorm_bwd("l0m_dkvnorm", kvr, KV_RANK, 0, dnkv, kv_norm_g, t_all)
    dkvr = jnp.concatenate([dckv, dk_rope[:, QK_NOPE:QK_HEAD],
                            jnp.zeros((t_all, PA_KV_W - KV_RANK - QK_ROPE), F32)], axis=-1)
    dpb = _rope("l0m_dkrope", dkvr, PA_KV_W, 0, cos32, sin32, lay_k, True, BF16)
    dproj_lat = jnp.concatenate([d_pool_x, jnp.zeros((n_lat, PA_CQ - POOL_DIM), BF16), dcq, dpb[:n_lat]], axis=-1)
    dproj_ctx = jnp.concatenate([jnp.zeros((n_ctx, PA_KV), BF16), dpb[n_lat:]], axis=-1)
    dproj = jnp.concatenate([dproj_lat, dproj_ctx], axis=0)
    d_in_pad = _mm("l0m_dwin", [(dproj, ua)], "tn", BF16, 640, 512)
    d_in_t = jnp.concatenate([d_in_pad[:POOL_DIM], d_in_pad[PA_CQ:PA_CQ + Q_RANK],
                              d_in_pad[PA_KV:PA_KV + kv_rows]], axis=0)
    token = feed.grads("l0m", {"ab_out": d_ab_out, "uq": d_uq, "ukv_t": d_ukv_t, "in_t": d_in_t})
    ds1, (dsh_a, dsc_a, dgn_a) = _du_adaln("l0m_du", [(dproj, w_in_t)], s1, dh2, mg0, 1, n_lat, _after(token))
    grad_x, g["f00"] = _ffn_half_bwd("l0f0", ds1, sv_f00, mg0, 0, feed, 0, 0.5, n_lat, out_rows=n_lat)

    dmod0 = _mod_grad([g["f00"], dict(shift=dsh_a, scale=dsc_a, gate=dgate_a), g["f01"]], 2)
    dmod1 = _mod_grad([g["f10"], dict(shift=dsh_c, scale=dsc_c, gate=dgate_c), g["f11"]], 1)
    d_norm_g = jnp.stack([
        jnp.concatenate([jnp.sum(g["f00"]["gain"], axis=0), jnp.sum(dgn_a, axis=0), g["f01"]["gain"][0]], axis=0),
        jnp.concatenate([g["f10"]["gain"][0], dgn_c[0], g["f11"]["gain"][0]], axis=0)])
    grads = dict(
        pool_w=d_pool_w, pool_scale=d_pool_scale, q_norm_g=d_q_norm_g[0], kv_norm_g=d_kv_norm_g[0],
        conv_w=d_conv_w, final_norm_g=d_final_g[0], norm_g=d_norm_g,
        mod_h=jnp.stack([dmod0[0], dmod1[0]]), mod_g=dmod0[1])
    return sq_cols, grad_x, grads


HBM_SPEC = pl.BlockSpec(memory_space=pltpu.HBM)
SEM_SPEC = pl.BlockSpec(memory_space=pltpu.SEMAPHORE)
ANY_SPEC = pl.BlockSpec(memory_space=pl.ANY)
SIDE_EFFECT = pltpu.SideEffectType.DATAFLOW_SIDE_EFFECTING
N_PEERS = N_DEV - 1


def _mesh_place():
    mx, my, mc = lax.axis_index("x"), lax.axis_index("y"), lax.axis_index("c")
    return mx, my, mc, 4 * mx + 2 * my + mc


def _peer(place, kk):
    mx, my, mc, _ = place
    px = jnp.bitwise_xor(mx, (kk >> 2) & 1)
    py = jnp.bitwise_xor(my, (kk >> 1) & 1)
    pc = jnp.bitwise_xor(mc, kk & 1)
    return (px, py, pc), 4 * px + 2 * py + pc


def _hbm(a):
    return pltpu.with_memory_space_constraint(a, pltpu.HBM)


def _landing(block, me):
    zone = lax.empty((N_DEV,) + block.shape, block.dtype)
    return lax.dynamic_update_slice(zone, block[None], (me,) + (0,) * block.ndim)


ALL_PEERS = tuple(range(1, N_DEV))
SIBLING = 1
CHIP_PEERS = (2, 4, 6)
RELAYED = (3, 5, 7)


def _exchange_start(name, srcs, lands, scatter, after, peers=ALL_PEERS):
    n = len(srcs)
    extra = [] if after is None else [after]

    def body(*refs):
        src, land = refs[:n], refs[n:2 * n]
        send_sems, recv_sems, token = refs[2 * n + len(extra)], refs[2 * n + len(extra) + 1], refs[-1]
        place = _mesh_place()
        for a in range(n):
            for kk in peers:
                dev, peer = _peer(place, kk)
                pltpu.make_async_remote_copy(
                    src_ref=src[a].at[peer] if scatter else src[a],
                    dst_ref=land[a].at[kk - 1] if scatter else land[a].at[place[3]],
                    send_sem=send_sems.at[a * N_PEERS + kk - 1], recv_sem=recv_sems.at[a * N_PEERS + kk - 1],
                    device_id=dev, device_id_type=MESH).start()
        token[...] = jnp.zeros_like(token)

    thru = [pltpu.HBM(t.shape, t.dtype) for t in (*srcs, *lands)]
    res = pl.pallas_call(
        body, name=name,
        out_shape=(pltpu.SemaphoreType.DMA((n * N_PEERS,)), pltpu.SemaphoreType.DMA((n * N_PEERS,)), *thru,
                   SDS((8, 128), F32)),
        in_specs=[HBM_SPEC] * (2 * n) + [ANY_SPEC] * len(extra),
        out_specs=(SEM_SPEC, SEM_SPEC, *([HBM_SPEC] * (2 * n)), pl.BlockSpec(memory_space=pltpu.VMEM)),
        input_output_aliases={i: 2 + i for i in range(2 * n)},
        compiler_params=pltpu.CompilerParams(has_side_effects=SIDE_EFFECT),
    )(*[_hbm(s) for s in srcs], *[_hbm(t) for t in lands], *extra)
    return res[0], res[1], list(res[2:2 + n]), list(res[2 + n:2 + 2 * n]), res[-1]


def _exchange_wait(name, send_sems, recv_sems, srcs, lands, places, scatter, after):
    n = len(srcs)

    def body(*refs):
        src, land = refs[:n], refs[n:2 * n]
        send, recv = refs[2 * n], refs[2 * n + 1]
        place = _mesh_place()
        for a in range(n):
            for kk in range(1, N_DEV):
                dev, peer = _peer(place, kk)
                cp = pltpu.make_async_remote_copy(
                    src_ref=src[a].at[peer] if scatter else src[a],
                    dst_ref=land[a].at[kk - 1] if scatter else land[a].at[peer],
                    send_sem=send.at[places[a] * N_PEERS + kk - 1], recv_sem=recv.at[places[a] * N_PEERS + kk - 1],
                    device_id=dev, device_id_type=MESH)
                cp.wait_send()
                cp.wait_recv()

    thru = [pltpu.HBM(t.shape, t.dtype) for t in (*srcs, *lands)]
    res = pl.pallas_call(
        body, name=name, out_shape=tuple(thru),
        in_specs=[HBM_SPEC] * (2 * n) + [SEM_SPEC, SEM_SPEC] + [ANY_SPEC] * len(after),
        out_specs=tuple([HBM_SPEC] * (2 * n)), input_output_aliases={i: i for i in range(2 * n)},
        compiler_params=pltpu.CompilerParams(has_side_effects=SIDE_EFFECT),
    )(*srcs, *lands, send_sems, recv_sems, *after)
    return list(res[:n]), list(res[n:])


def _gather_relay(name, send1, recv1, lands, places, after):
    n = len(lands)

    def body(*refs):
        land, s1, r1 = refs[:n], refs[n], refs[n + 1]
        s2, r2 = refs[n + 3], refs[n + 4]
        place = _mesh_place()
        sibling = _peer(place, SIBLING)[0]
        for a in range(n):
            for j, kk in enumerate(CHIP_PEERS):
                dev, origin = _peer(place, kk)
                block = land[a].at[origin]
                pltpu.make_async_remote_copy(
                    src_ref=block, dst_ref=block, send_sem=s1.at[places[a] * N_PEERS + kk - 1],
                    recv_sem=r1.at[places[a] * N_PEERS + kk - 1], device_id=dev, device_id_type=MESH).wait_recv()
                pltpu.make_async_remote_copy(
                    src_ref=block, dst_ref=block, send_sem=s2.at[a * 3 + j], recv_sem=r2.at[a * 3 + j],
                    device_id=sibling, device_id_type=MESH).start()

    res = pl.pallas_call(
        body, name=name,
        out_shape=(pltpu.SemaphoreType.DMA((3 * n,)), pltpu.SemaphoreType.DMA((3 * n,)),
                   *[pltpu.HBM(t.shape, t.dtype) for t in lands]),
        in_specs=[HBM_SPEC] * n + [SEM_SPEC, SEM_SPEC, ANY_SPEC],
        out_specs=(SEM_SPEC, SEM_SPEC, *([HBM_SPEC] * n)),
        input_output_aliases={i: 2 + i for i in range(n)},
        compiler_params=pltpu.CompilerParams(has_side_effects=SIDE_EFFECT),
    )(*lands, send1, recv1, after)
    return res[0], res[1], list(res[2:])


def _gather_wait(name, send1, recv1, send2, recv2, srcs, lands, places, after):
    n = len(lands)

    def body(*refs):
        src, land = refs[:n], refs[n:2 * n]
        s1, r1, s2, r2 = refs[2 * n:2 * n + 4]
        place = _mesh_place()
        for a in range(n):
            for kk in (SIBLING,) + CHIP_PEERS:
                dev, origin = _peer(place, kk)
                first = pltpu.make_async_remote_copy(
                    src_ref=src[a], dst_ref=land[a].at[origin], send_sem=s1.at[places[a] * N_PEERS + kk - 1],
                    recv_sem=r1.at[places[a] * N_PEERS + kk - 1], device_id=dev, device_id_type=MESH)
                first.wait_send()
                if kk == SIBLING:
                    first.wait_recv()
            for j, kk in enumerate(CHIP_PEERS):
                dev, origin = _peer(place, kk + 1)
                relay = pltpu.make_async_remote_copy(
                    src_ref=src[a], dst_ref=land[a].at[origin], send_sem=s2.at[a * 3 + j], recv_sem=r2.at[a * 3 + j],
                    device_id=dev, device_id_type=MESH)
                relay.wait_send()
                relay.wait_recv()

    arrays = (*srcs, *lands)
    res = pl.pallas_call(
        body, name=name, out_shape=tuple(pltpu.HBM(t.shape, t.dtype) for t in arrays),
        in_specs=[HBM_SPEC] * (2 * n) + [SEM_SPEC] * 4 + [ANY_SPEC], out_specs=tuple([HBM_SPEC] * (2 * n)),
        input_output_aliases={i: i for i in range(2 * n)},
        compiler_params=pltpu.CompilerParams(has_side_effects=SIDE_EFFECT),
    )(*arrays, send1, recv1, send2, recv2, after)
    return list(res[n:])


class _Feed:
    def __init__(self, shards, groups, me):
        self.shards, self.groups, self.me, self.pos = shards, groups, me, 0
        self.sems, self.srcs, self.lands = {}, {}, {}
        self.relays = {}
        self.pending = []

    def start(self, tag, names, after):
        srcs = [self.shards[nm] for nm in names]
        lands = [_landing(s, self.me) for s in srcs]
        send, recv, srcs, lands, self.token = _exchange_start(
            f"gather_start_{tag}", srcs, lands, False, after, (SIBLING,) + CHIP_PEERS)
        for i, nm in enumerate(names):
            self.sems[nm], self.srcs[nm], self.lands[nm] = (send, recv, i), srcs[i], lands[i]
        return self.token

    def _relay(self, gi, after):
        names = self.groups[gi]
        if gi not in self.relays:
            send, recv, _ = self.sems[names[0]]
            places = [self.sems[nm][2] for nm in names]
            send2, recv2, lands = _gather_relay(f"gather_relay_{gi}", send, recv, [self.lands[nm] for nm in names],
                                                places, after)
            for nm, t in zip(names, lands):
                self.lands[nm] = t
            self.relays[gi] = (send2, recv2)
            after = lands[0]
        return after

    def weights(self, tag, names, after):
        gi = self.pos
        assert names == self.groups[gi], (names, self.groups[gi])
        if gi == 0:
            after = self.token
        self._relay(gi, after)
        if 1 <= gi < len(self.groups) - 1:
            after = self._relay(gi + 1, after)
        send2, recv2 = self.relays[gi]
        send, recv, _ = self.sems[names[0]]
        got = _gather_wait(f"gather_wait_{tag}", send, recv, send2, recv2, [self.srcs[nm] for nm in names],
                           [self.lands[nm] for nm in names], [self.sems[nm][2] for nm in names], after)
        self.pos += 1
        return [t.reshape((N_DEV * t.shape[1],) + t.shape[2:]) for t in got]

    def grads(self, tag, full):
        names = list(full)
        srcs = [full[nm].reshape((N_DEV, full[nm].shape[0] // N_DEV) + full[nm].shape[1:]) for nm in names]
        lands = [lax.empty((N_PEERS,) + s.shape[1:], s.dtype) for s in srcs]
        send, recv, srcs, lands, token = _exchange_start(f"scatter_start_{tag}", srcs, lands, True, None)
        self.pending.append((tag, names, send, recv, srcs, lands))
        return token[0, 0]

    def collect(self, tags, after, keep_slots=()):
        out = {}
        for tag, names, send, recv, srcs, lands in self.pending:
            if tag not in tags:
                continue
            srcs, got = _exchange_wait(f"scatter_wait_{tag}", send, recv, srcs, lands, list(range(len(names))), True,
                                       after)
            for nm, slots, src in zip(names, got, srcs):
                out[nm] = ((slots, src) if nm.startswith(tuple(keep_slots))
                           else _sum_slots(f"reduce_{nm}", slots, src, self.me))
        return out


def _adamw_math(w, gg, m, v):
    nm = ADAM_B1 * m + (1.0 - ADAM_B1) * gg
    nv = ADAM_B2 * v + (1.0 - ADAM_B2) * (gg * gg)
    bc1 = 1.0 - ADAM_B1 ** ADAM_STEP
    bc2 = 1.0 - ADAM_B2 ** ADAM_STEP
    return -ADAM_LR * ((nm / bc1) / (jnp.sqrt(nv / bc2) + ADAM_EPS) + ADAM_WD * w), nm, nv


def _adamw_part(name, i, w, scattered, me, m, v, prev):
    n_parts, rows, cols = w.shape
    tr = _tile(rows, 256, 16)
    if prev is None:
        prev = tuple(lax.empty(w.shape, F32) for _ in range(4))

    slots, src = scattered

    def body(me_ref, w_ref, g_ref, own_ref, m_ref, v_ref, *rest):
        go_ref, d_ref, nm_ref, nv_ref = rest[4:]
        gg = own_ref[...].astype(F32)
        for sl in range(N_PEERS):
            gg = gg + g_ref[sl].astype(F32)
        d, nm, nv = _adamw_math(w_ref[...], gg, m_ref[...], v_ref[...])
        go_ref[...] = gg
        d_ref[...] = d
        nm_ref[...] = nm
        nv_ref[...] = nv

    part = pl.BlockSpec((None, tr, cols), lambda r, me_ref: (i, r, 0))
    grid_spec = pltpu.PrefetchScalarGridSpec(
        num_scalar_prefetch=1, grid=(rows // tr,),
        in_specs=[part, pl.BlockSpec((N_PEERS, tr, cols), lambda r, me_ref: (0, r, 0)),
                  pl.BlockSpec((None, tr, cols), lambda r, me_ref: (me_ref[0], r, 0)), part, part] + [ANY_SPEC] * 4,
        out_specs=[part] * 4)
    return pl.pallas_call(
        body, name=name, grid_spec=grid_spec, out_shape=[SDS(w.shape, F32)] * 4,
        input_output_aliases={6 + k: k for k in range(4)}, compiler_params=_cparams(),
    )(_me_operand(me), w, slots, src, m, v, *prev)


WEIGHT_NAMES = ("c_ctx", "norm_g", "w_mod", "b_mod", "ffn_w_gate", "ffn_w_up", "ffn_w_down", "ab_w_in", "pool_w",
                "pool_scale", "q_norm_g", "w_uq", "kv_norm_g", "w_ukv", "ab_w_out", "conv_w_in", "conv_w",
                "conv_w_out", "final_norm_g")


def kernel(x, c, ctx, c_ctx, norm_g, w_mod, b_mod, ffn_w_gate, ffn_w_up, ffn_w_down, ab_w_in, pool_w, pool_scale, q_norm_g, w_uq, kv_norm_g, w_ukv, ab_w_out, conv_w_in, conv_w, conv_w_out, final_norm_g, loss_target, m_c_ctx, m_norm_g, m_w_mod, m_b_mod, m_ffn_w_gate, m_ffn_w_up, m_ffn_w_down, m_ab_w_in, m_pool_w, m_pool_scale, m_q_norm_g, m_w_uq, m_kv_norm_g, m_w_ukv, m_ab_w_out, m_conv_w_in, m_conv_w, m_conv_w_out, m_final_norm_g, v_c_ctx, v_norm_g, v_w_mod, v_b_mod, v_ffn_w_gate, v_ffn_w_up, v_ffn_w_down, v_ab_w_in, v_pool_w, v_pool_scale, v_q_norm_g, v_w_uq, v_kv_norm_g, v_w_ukv, v_ab_w_out, v_conv_w_in, v_conv_w, v_conv_w_out, v_final_norm_g):
    weights = (c_ctx, norm_g, w_mod, b_mod, ffn_w_gate, ffn_w_up, ffn_w_down, ab_w_in, pool_w, pool_scale, q_norm_g,
               w_uq, kv_norm_g, w_ukv, ab_w_out, conv_w_in, conv_w, conv_w_out, final_norm_g)
    moms = (m_c_ctx, m_norm_g, m_w_mod, m_b_mod, m_ffn_w_gate, m_ffn_w_up, m_ffn_w_down, m_ab_w_in, m_pool_w,
            m_pool_scale, m_q_norm_g, m_w_uq, m_kv_norm_g, m_w_ukv, m_ab_w_out, m_conv_w_in, m_conv_w, m_conv_w_out,
            m_final_norm_g)
    vels = (v_c_ctx, v_norm_g, v_w_mod, v_b_mod, v_ffn_w_gate, v_ffn_w_up, v_ffn_w_down, v_ab_w_in, v_pool_w,
            v_pool_scale, v_q_norm_g, v_w_uq, v_kv_norm_g, v_w_ukv, v_ab_w_out, v_conv_w_in, v_conv_w, v_conv_w_out,
            v_final_norm_g)
    me = 4 * lax.axis_index("x") + 2 * lax.axis_index("y") + lax.axis_index("c")
    n_lat, n_ctx = x.shape[1], ctx.shape[1]
    d = D_MODEL
    mod_cols = w_mod.shape[-1]
    ng_sh, cw_sh = norm_g.shape[-1], conv_w.shape[-1]

    def ffn_shards(i):
        return {f"gate_t{i}": ffn_w_gate[i // 2, i % 2].T, f"up_t{i}": ffn_w_up[i // 2, i % 2].T,
                f"down{i}": ffn_w_down[i // 2, i % 2]}

    local = {**ffn_shards(0), "in_t": ab_w_in[0].T, "uq": w_uq[0], "ukv_t": w_ukv[0].T, "ab_out": ab_w_out[0],
             **ffn_shards(1), **ffn_shards(2), "cin_t": conv_w_in[0].T, "c_out": conv_w_out[0], **ffn_shards(3)}
    ffn_groups = [[[f"gate_t{i}", f"up_t{i}"], [f"down{i}"]] for i in range(4)]
    groups = [*ffn_groups[0], ["in_t", "uq", "ukv_t", "ab_out"], *ffn_groups[1], *ffn_groups[2], ["cin_t", "c_out"],
              *ffn_groups[3]]
    feed = _Feed({nm: a.astype(BF16) for nm, a in local.items()}, groups, me)

    small = jnp.concatenate([c.reshape(-1), norm_g.reshape(-1), conv_w.reshape(-1)])
    small_n = -(-small.shape[0] // 1024) * 1024
    small = jnp.pad(small, (0, small_n - small.shape[0])).reshape(small_n // 128, 128)
    small_all = _exchange("gather_small", small, False).reshape(N_DEV, small_n)
    c_all = small_all[:, :d]
    o1 = d + 6 * ng_sh
    norm_g_full = small_all[:, d:o1].reshape(N_DEV, 2, 3, ng_sh).transpose(1, 2, 0, 3).reshape(2, 3, d)
    conv_w_full = small_all[:, o1:o1 + 3 * cw_sh].reshape(N_DEV, 3, cw_sh).transpose(1, 0, 2).reshape(3, d)

    cond = jnp.concatenate([c_all, jnp.broadcast_to(c_ctx[None, :], (N_DEV, d))], axis=0)
    sil, dsil = _silu_rows("mod_silu", cond)
    w_mod_b = w_mod.astype(BF16)
    b_sh = lax.dynamic_slice(b_mod, (0, me * mod_cols), (2, mod_cols))
    m_part = jnp.stack([_mm(f"mod_fwd{l}", [(sil, w_mod_b[l])], "nn", F32, 16, 384, bias=b_sh[l:l + 1])
                        for l in range(2)], axis=1)
    m_all = _exchange("gather_mod", m_part.reshape(-1, 128), False).reshape(N_DEV, 2 * N_DEV, 2, mod_cols)
    m_mine = lax.dynamic_index_in_dim(m_all, me, axis=1, keepdims=False)
    mod_h = m_mine.transpose(1, 0, 2).reshape(2, N_MOD, d)
    mod_g = m_all[:, N_DEV, 0, :].reshape(N_MOD, d)

    first = feed.start("first", [nm for grp in groups[:3] for nm in grp], m_all)
    feed.start("rest", [nm for grp in groups[3:] for nm in grp], first)

    sq_cols, grad_x, g = _local_step(x[0], ctx[0], loss_target[0], mod_h, mod_g, norm_g_full, feed, pool_w[0],
                                  pool_scale, q_norm_g, kv_norm_g, conv_w_full, final_norm_g)
    w_of, m_of, v_of = (dict(zip(WEIGHT_NAMES, t)) for t in (weights, moms, vels))
    results = {}

    def update(nm, grad, view=lambda t: t):
        outs = _adamw(f"adamw_{nm}", view(w_of[nm]), grad.reshape(view(w_of[nm]).shape), view(m_of[nm]), view(v_of[nm]))
        results[nm] = tuple(view(t) for t in (grad.reshape(view(w_of[nm]).shape), *outs))

    def swap(t):
        return jnp.swapaxes(t, -1, -2)

    stacked = ("gate_t", "up_t", "down")
    early = feed.collect(["l1f1", "l1m", "l1f0", "l0f1", "l0m"], [grad_x], stacked)
    update("ab_w_in", early["in_t"], swap)
    update("w_uq", early["uq"])
    update("w_ukv", early["ukv_t"].T)
    update("ab_w_out", early["ab_out"])
    update("conv_w_in", early["cin_t"].T)
    update("conv_w_out", early["c_out"])
    ffn = {}
    for nm, prefix, view in (("ffn_w_gate", "gate_t", swap), ("ffn_w_up", "up_t", swap),
                             ("ffn_w_down", "down", lambda t: t)):
        w4, m4, v4 = (view(t).reshape((4,) + view(t).shape[-2:]) for t in (w_of[nm], m_of[nm], v_of[nm]))
        prev = None
        for i in (3, 2, 1):
            prev = _adamw_part(f"adamw_{nm}{i}", i, w4, early[f"{prefix}{i}"], me, m4, v4, prev)
        ffn[nm] = (prefix, view, w4, m4, v4, prev)
    done_early = [results[nm][1] for nm in results] + [state[5][1] for state in ffn.values()]
    late = feed.collect(["l0f0"], done_early, stacked)
    for nm, (prefix, view, w4, m4, v4, prev) in ffn.items():
        outs = _adamw_part(f"adamw_{nm}0", 0, w4, late[f"{prefix}0"], me, m4, v4, prev)
        results[nm] = tuple(view(t.reshape(view(w_of[nm]).shape)) for t in outs)

    dm = jnp.stack([g["mod_h"], jnp.stack([g["mod_g"], jnp.zeros_like(g["mod_g"])])])
    dm_all = _exchange("gather_dmod", dm.reshape(-1, 128), False, results["ffn_w_down"][1]).reshape(N_DEV, 2, 2, N_MOD * d)
    grad_b_mod = _sum_rows("dmod_bias", dm_all.reshape(2 * N_DEV, 2 * N_MOD * d)).reshape(2, N_MOD * d)
    dm_sh = lax.dynamic_slice(dm_all, (0, 0, 0, me * mod_cols), (N_DEV, 2, 2, mod_cols))
    gw_mod, cctx_parts = [], []
    for l in range(2):
        dm_l = dm_sh[:, :, l, :].transpose(1, 0, 2).reshape(2 * N_DEV, mod_cols).astype(BF16)
        gw_mod.append(_mm(f"mod_dw{l}", [(sil, dm_l)], "tn", F32, 512, 384))
        dm_ctx = jnp.concatenate([dm_l[N_DEV:], jnp.zeros((N_DEV, mod_cols), BF16)], axis=0)
        cctx_parts.append(_mm(f"mod_dcond{l}", [(dm_ctx, w_mod_b[l])], "nt", F32, 16, 512))
    cctx_part = _sum_rows("mod_dcond_sum", jnp.concatenate(cctx_parts, axis=0))
    update("w_mod", jnp.stack(gw_mod))
    update("b_mod", grad_b_mod)

    small_g = jnp.concatenate([g["pool_w"].reshape(-1), g["pool_scale"].reshape(-1), g["q_norm_g"].reshape(-1),
                               g["kv_norm_g"].reshape(-1), g["final_norm_g"].reshape(-1), g["norm_g"].reshape(-1),
                               g["conv_w"].reshape(-1), sq_cols.reshape(-1), cctx_part.reshape(-1)])
    sizes = [pool_w.size, pool_scale.size, q_norm_g.size, kv_norm_g.size, d, 6 * d, 3 * d, d, d]
    sg_n = -(-small_g.shape[0] // 1024) * 1024
    small_g = jnp.pad(small_g, (0, sg_n - small_g.shape[0]))
    sg_all = _exchange("gather_small_grads", small_g.reshape(-1, 128), False).reshape(N_DEV, sg_n)
    scale_vec = jnp.concatenate([jnp.ones((1, sum(sizes[:-1])), F32), dsil[N_DEV:N_DEV + 1],
                                 jnp.ones((1, sg_n - sum(sizes)), F32)], axis=1)
    sg = _sum_rows("small_grads_sum", sg_all, scale_vec)[0]
    cuts, pos = [], 0
    for sz in sizes:
        cuts.append(sg[pos:pos + sz])
        pos += sz
    g_pool_w, g_pool_scale, g_q_norm, g_kv_norm, g_final, g_norm_full, g_conv_full, sq_all, g_c_ctx = cuts
    loss = 0.5 * jnp.sum(sq_all) / d
    update("c_ctx", g_c_ctx)
    update("norm_g", lax.dynamic_slice(g_norm_full.reshape(2, 3, d), (0, 0, me * ng_sh), (2, 3, ng_sh)))
    update("conv_w", lax.dynamic_slice(g_conv_full.reshape(3, d), (0, me * cw_sh), (3, cw_sh)))
    update("pool_w", g_pool_w)
    update("pool_scale", g_pool_scale)
    update("q_norm_g", g_q_norm)
    update("kv_norm_g", g_kv_norm)
    update("final_norm_g", g_final)
    outs = [results[nm] for nm in WEIGHT_NAMES]
    return (loss, grad_x[None], *[o[0] for o in outs], *[o[1] for o in outs], *[o[2] for o in outs],
            *[o[3] for o in outs])
```

```python
import functools
import math

import jax
import jax.numpy as jnp
import numpy as np
from jax import lax
from jax.experimental import pallas as pl
from jax.experimental.pallas import tpu as pltpu

F32 = jnp.float32
BF16 = jnp.bfloat16
MESH = pl.DeviceIdType.MESH
SDS = jax.ShapeDtypeStruct

N_DEV = 8
D_MODEL = 1024
N_MOD = 9
D_FF = 2816
POOL_WINDOWS = (2, 4, 8, 16)
POOL_DIM = 512
POOL_GROUP_DIM = 128
HEADS = 8
QK_NOPE = 64
QK_ROPE = 32
QK_HEAD = QK_NOPE + QK_ROPE
V_HEAD = 64
Q_RANK = 768
KV_RANK = 256
GRID_W = 64
ROPE_THETA = 10000.0
RMS_EPS = 1e-6
ATTN_SCALE = 1.0 / math.sqrt(QK_HEAD)
HEAD_PAD = 128
POOL_PAD = 16
PA_POOL, PA_CQ, PA_KV = 0, 768, 1536
PA_KV_W = 384
PA_W = PA_KV + PA_KV_W

ADAM_LR, ADAM_B1, ADAM_B2, ADAM_EPS, ADAM_WD, ADAM_STEP = 0.001, 0.9, 0.999, 1e-08, 0.01, 10

VMEM_LIMIT_BYTES = 56 * 1024 * 1024

NN = ((1,), (0,))
NT = ((1,), (1,))
TN = ((0,), (0,))


def _cparams():
    return pltpu.CompilerParams(vmem_limit_bytes=VMEM_LIMIT_BYTES)


def _dot(a, b, dims):
    return lax.dot_general(a, b, (dims, ((), ())), preferred_element_type=F32)


def _tile(n, cap, mult=8):
    t = (min(cap, n) // mult) * mult
    while t >= mult:
        if n % t == 0:
            return t
        t -= mult
    return n


def _colsum(x):
    return jnp.sum(x, axis=0, keepdims=True)


def _rms(x):
    r = lax.rsqrt(jnp.mean(x * x, axis=-1, keepdims=True) + RMS_EPS)
    return x * r, r


def _rms_bwd(n, r, dn):
    return r * (dn - n * jnp.mean(dn * n, axis=-1, keepdims=True))


def _rowwise(name, fn, t_rows, tm, n_lat, rows, vecs, outs, accs, into=None):
    nt = t_rows // tm
    nlt = n_lat // tm
    n_groups = 2 if nlt < nt else 1

    def grp(i):
        return jnp.where(i >= nlt, 1, 0) if n_groups == 2 else 0

    in_specs = [pl.BlockSpec((tm, w), functools.partial(lambda i, cb: (i, cb), cb=cb)) for (_, w, cb) in rows]
    in_specs += [pl.BlockSpec((1,) + v.shape[1:], lambda i: (grp(i), 0, 0)) for v in vecs]
    out_specs = [pl.BlockSpec((tm, w), lambda i: (i, 0)) for (w, _) in outs]
    out_specs += [pl.BlockSpec((1, 1, w), lambda i: (grp(i), 0, 0)) for w in accs]
    out_shape = [SDS((t_rows, w), dt) for (w, dt) in outs] + [SDS((n_groups, 1, w), F32) for w in accs]
    n_r, n_v, n_o = len(rows), len(vecs), len(outs)
    extra, aliases = [], {}
    if into is not None:
        extra, aliases = [into[0]], {n_r + n_v: 0}
        in_specs.append(pl.BlockSpec(memory_space=pl.ANY))
        out_specs[0] = pl.BlockSpec((tm, outs[0][0]), lambda i: (i, into[1]))
        out_shape[0] = SDS(into[0].shape, into[0].dtype)
    n_in = n_r + n_v + len(extra)

    def body(*refs):
        row_vals = [r[...] for r in refs[:n_r]]
        vec_vals = [v[0] for v in refs[n_r:n_r + n_v]]
        out_refs = refs[n_in:n_in + n_o]
        acc_refs = refs[n_in + n_o:]
        out_vals, acc_vals = fn(row_vals, vec_vals)
        for o_ref, o in zip(out_refs, out_vals):
            o_ref[...] = o.astype(o_ref.dtype)
        if acc_refs:
            i = pl.program_id(0)
            first = (i == 0) | (i == nlt) if n_groups == 2 else i == 0

            @pl.when(first)
            def _():
                for a_ref, a in zip(acc_refs, acc_vals):
                    a_ref[0] = a

            @pl.when(jnp.logical_not(first))
            def _():
                for a_ref, a in zip(acc_refs, acc_vals):
                    a_ref[0] += a

    res = pl.pallas_call(
        body, name=name, grid=(nt,), in_specs=in_specs, out_specs=out_specs, out_shape=out_shape,
        input_output_aliases=aliases, compiler_params=_cparams(),
    )(*[r[0] for r in rows], *vecs, *extra)
    return res[:n_o], res[n_o:]


RESIDENT_BYTES = 12 * 1024 * 1024


def _mm(name, pairs, mode, out_dtype, tm_cap=256, tn_cap=512, bias=None):
    a0, b0 = pairs[0]
    if mode == "nn":
        m, n, dims = a0.shape[0], b0.shape[1], NN
    elif mode == "nt":
        m, n, dims = a0.shape[0], b0.shape[0], NT
    else:
        m, n, dims = a0.shape[1], b0.shape[1], TN
    b_bytes = sum(b.size * b.dtype.itemsize for _, b in pairs)
    tn = n if b_bytes <= RESIDENT_BYTES else _tile(n, tn_cap, 128)
    tm = _tile(m, tm_cap, 128 if mode == "tn" else 16)

    def a_spec(a):
        if mode == "tn":
            return pl.BlockSpec((a.shape[0], tm), lambda i, j: (0, i))
        return pl.BlockSpec((tm, a.shape[1]), lambda i, j: (i, 0))

    def b_spec(b):
        if mode == "nt":
            return pl.BlockSpec((tn, b.shape[1]), lambda i, j: (j, 0))
        return pl.BlockSpec((b.shape[0], tn), lambda i, j: (0, j))

    in_specs, flat = [], []
    for a, b in pairs:
        in_specs += [a_spec(a), b_spec(b)]
        flat += [a, b]
    if bias is not None:
        in_specs.append(pl.BlockSpec((1, tn), lambda i, j: (0, j)))
        flat.append(bias)
    n_pairs = len(pairs)

    def body(*refs):
        acc = None
        for p in range(n_pairs):
            t = _dot(refs[2 * p][...], refs[2 * p + 1][...], dims)
            acc = t if acc is None else acc + t
        if bias is not None:
            acc = acc + refs[2 * n_pairs][...]
        refs[-1][...] = acc.astype(refs[-1].dtype)

    return pl.pallas_call(
        body, name=name, grid=(m // tm, n // tn), in_specs=in_specs,
        out_specs=pl.BlockSpec((tm, tn), lambda i, j: (i, j)),
        out_shape=SDS((m, n), out_dtype), compiler_params=_cparams(),
    )(*flat)


def _mm_resid(name, pairs, s, mg, k, coef, n_lat):
    t_rows, n = s.shape
    n_pairs = len(pairs)
    tm = _tile(math.gcd(n_lat, t_rows), 256, 16)
    nlt = n_lat // tm
    n_groups = 2 if nlt < t_rows // tm else 1

    def grp(i):
        return jnp.where(i >= nlt, 1, 0) if n_groups == 2 else 0

    def body(*refs):
        s_ref, mg_ref, so_ref, o_ref = refs[2 * n_pairs:]
        o = _dot(refs[0][...], refs[n_pairs][...], NN)
        for p in range(1, n_pairs):
            o = o + _dot(refs[p][...], refs[n_pairs + p][...], NN)
        gate = mg_ref[0, 3 * k + 2:3 * k + 3, :]
        o_ref[...] = o.astype(BF16)
        so_ref[...] = s_ref[...] + (coef * gate) * o

    row = pl.BlockSpec((tm, n), lambda i: (i, 0))
    return pl.pallas_call(
        body, name=name, grid=(t_rows // tm,),
        in_specs=[pl.BlockSpec((tm, a.shape[1]), lambda i: (i, 0)) for a, _ in pairs]
        + [pl.BlockSpec(b.shape, lambda i: (0, 0)) for _, b in pairs]
        + [row, pl.BlockSpec((1, mg.shape[1], n), lambda i: (grp(i), 0, 0))],
        out_specs=[row, row], out_shape=[SDS((t_rows, n), F32), SDS((t_rows, n), BF16)], compiler_params=_cparams(),
    )(*[a for a, _ in pairs], *[b for _, b in pairs], s, mg)


def _dw_pair(name, a1, a2, b):
    kk, m = a1.shape
    n = b.shape[1]
    tm = _tile(m, 256, 128)

    def body(a1_ref, a2_ref, b_ref, o1_ref, o2_ref):
        bb = b_ref[...]
        o1_ref[...] = _dot(a1_ref[...], bb, TN).astype(BF16)
        o2_ref[...] = _dot(a2_ref[...], bb, TN).astype(BF16)

    col = pl.BlockSpec((kk, tm), lambda i: (0, i))
    out = pl.BlockSpec((tm, n), lambda i: (i, 0))
    return pl.pallas_call(
        body, name=name, grid=(m // tm,), in_specs=[col, col, pl.BlockSpec(b.shape, lambda i: (0, 0))],
        out_specs=[out, out], out_shape=[SDS((m, n), BF16)] * 2, compiler_params=_cparams(),
    )(a1, a2, b)


def _groups(t_rows, tm, n_lat):
    nlt = n_lat // tm
    if nlt < t_rows // tm:
        return 2, (lambda i: jnp.where(i >= nlt, 1, 0)), (lambda i: (i == 0) | (i == nlt))
    return 1, (lambda i: 0), (lambda i: i == 0)


def _accumulate(acc_refs, vals, first):
    @pl.when(first)
    def _():
        for r, v in zip(acc_refs, vals):
            r[0] = v

    @pl.when(jnp.logical_not(first))
    def _():
        for r, v in zip(acc_refs, vals):
            r[0] += v


def _adaln_math(s, m, k):
    n, _ = _rms(s)
    return (n * m[9 + k:10 + k]) * (1.0 + m[3 * k + 1:3 * k + 2]) + m[3 * k:3 * k + 1]


def _ffn_up(name, s, mg, k, n_lat, wg_t, wu_t):
    t_rows, f = s.shape[0], wg_t.shape[0]
    tm = _row_tm(t_rows, n_lat)
    _, grp, _ = _groups(t_rows, tm, n_lat)

    def body(s_ref, mg_ref, wg_ref, wu_ref, u_ref, a_ref, b_ref, h_ref):
        uu = _adaln_math(s_ref[...], mg_ref[0], k).astype(BF16)
        u_ref[...] = uu
        a = _dot(uu, wg_ref[...], NT)
        b = _dot(uu, wu_ref[...], NT)
        sg = jax.nn.sigmoid(a)
        act = a * sg
        a_ref[...] = (b * (sg * (1.0 + a * (1.0 - sg)))).astype(BF16)
        b_ref[...] = act.astype(BF16)
        h_ref[...] = (act * b).astype(BF16)

    w_spec = pl.BlockSpec(wg_t.shape, lambda i: (0, 0))
    o_spec = pl.BlockSpec((tm, f), lambda i: (i, 0))
    row = pl.BlockSpec((tm, s.shape[1]), lambda i: (i, 0))
    return pl.pallas_call(
        body, name=name, grid=(t_rows // tm,),
        in_specs=[row, pl.BlockSpec((1,) + mg.shape[1:], lambda i: (grp(i), 0, 0)), w_spec, w_spec],
        out_specs=[row, o_spec, o_spec, o_spec],
        out_shape=[SDS(s.shape, BF16)] + [SDS((t_rows, f), BF16)] * 3, compiler_params=_cparams(),
    )(s, mg, wg_t, wu_t)


def _ffn_dact(name, ds_out, o, mg, k, coef, n_lat, wd, a, b):
    t_rows, f = ds_out.shape[0], wd.shape[0]
    tm = _row_tm(t_rows, n_lat)
    n_groups, grp, first = _groups(t_rows, tm, n_lat)
    d = ds_out.shape[1]

    def body(ds_ref, o_ref, mg_ref, wd_ref, a_ref, b_ref, do_ref, da_ref, db_ref, dg_ref):
        dd = coef * ds_ref[...]
        do = (dd * mg_ref[0, 3 * k + 2:3 * k + 3, :]).astype(BF16)
        do_ref[...] = do
        _accumulate([dg_ref], [_colsum(dd * o_ref[...].astype(F32))], first(pl.program_id(0)))
        dh = _dot(do, wd_ref[...], NT)
        da_ref[...] = (dh * a_ref[...].astype(F32)).astype(BF16)
        db_ref[...] = (dh * b_ref[...].astype(F32)).astype(BF16)

    row = pl.BlockSpec((tm, d), lambda i: (i, 0))
    t_spec = pl.BlockSpec((tm, f), lambda i: (i, 0))
    return pl.pallas_call(
        body, name=name, grid=(t_rows // tm,),
        in_specs=[row, row, pl.BlockSpec((1,) + mg.shape[1:], lambda i: (grp(i), 0, 0)),
                  pl.BlockSpec(wd.shape, lambda i: (0, 0)), t_spec, t_spec],
        out_specs=[row, t_spec, t_spec, pl.BlockSpec((1, 1, d), lambda i: (grp(i), 0, 0))],
        out_shape=[SDS((t_rows, d), BF16), SDS((t_rows, f), BF16), SDS((t_rows, f), BF16), SDS((n_groups, 1, d), F32)],
        compiler_params=_cparams(),
    )(ds_out, o, mg, wd, a, b)


def _du_adaln(name, pairs, s, ds_out, mg, k, n_lat, after, out_rows=None):
    t_rows, d = s.shape
    tm = _row_tm(t_rows, n_lat)
    n_groups, grp, first = _groups(t_rows, tm, n_lat)
    n_pairs = len(pairs)
    nt, n_ds, n_out = t_rows // tm, ds_out.shape[0] // tm, (out_rows or t_rows) // tm

    def body(*refs):
        s_ref, ds_ref, mg_ref, z_ref, out_ref, dsh_ref, dsc_ref, dgn_ref = refs[2 * n_pairs:]
        i = pl.program_id(0)
        d_u = z_ref[...]
        for p in range(n_pairs):
            d_u = d_u + _dot(refs[p][...], refs[n_pairs + p][...], NN)
        m = mg_ref[0]
        gain, scale = m[9 + k:10 + k], m[3 * k + 1:3 * k + 2]
        n, r = _rms(s_ref[...])
        dxn = d_u * (1.0 + scale)
        ds_in = _rms_bwd(n, r, dxn * gain)
        ds_in = ds_in + (ds_ref[...] if n_ds == nt else jnp.where(i < n_ds, ds_ref[...], 0.0))
        if n_out == nt:
            out_ref[...] = ds_in
        else:
            @pl.when(i < n_out)
            def _():
                out_ref[...] = ds_in
        _accumulate([dsh_ref, dsc_ref, dgn_ref], [_colsum(d_u), _colsum(d_u * (n * gain)), _colsum(dxn * n)], first(i))

    row = pl.BlockSpec((tm, d), lambda i: (i, 0))
    acc = pl.BlockSpec((1, 1, d), lambda i: (grp(i), 0, 0))
    res = pl.pallas_call(
        body, name=name, grid=(t_rows // tm,),
        in_specs=[pl.BlockSpec((tm, a.shape[1]), lambda i: (i, 0)) for a, _ in pairs]
        + [pl.BlockSpec(w.shape, lambda i: (0, 0)) for _, w in pairs]
        + [row, pl.BlockSpec((tm, d), lambda i: (jnp.minimum(i, n_ds - 1), 0)),
           pl.BlockSpec((1,) + mg.shape[1:], lambda i: (grp(i), 0, 0)), pl.BlockSpec((1, d), lambda i: (0, 0))],
        out_specs=[pl.BlockSpec((tm, d), lambda i: (jnp.minimum(i, n_out - 1), 0)), acc, acc, acc],
        out_shape=[SDS((n_out * tm, d), F32)] + [SDS((n_groups, 1, d), F32)] * 3, compiler_params=_cparams(),
    )(*[a for a, _ in pairs], *[w for _, w in pairs], s, ds_out, mg, after)
    return res[0], res[1:]


def _adaln_mm(name, s, mg, k, n_lat, w_t):
    rows, d = s.shape
    tm = _row_tm(rows, n_lat)
    _, grp, _ = _groups(rows, tm, n_lat)
    n = w_t.shape[0]

    def body(s_ref, mg_ref, w_ref, u_ref, y_ref):
        uu = _adaln_math(s_ref[...], mg_ref[0], k).astype(BF16)
        u_ref[...] = uu
        y_ref[...] = _dot(uu, w_ref[...], NT)

    row = pl.BlockSpec((tm, d), lambda i: (i, 0))
    return pl.pallas_call(
        body, name=name, grid=(rows // tm,),
        in_specs=[row, pl.BlockSpec((1,) + mg.shape[1:], lambda i: (grp(i), 0, 0)), pl.BlockSpec(w_t.shape, lambda i: (0, 0))],
        out_specs=[row, pl.BlockSpec((tm, n), lambda i: (i, 0))],
        out_shape=[SDS((rows, d), BF16), SDS((rows, n), F32)], compiler_params=_cparams(),
    )(s, mg, w_t)


def _gate_mm(name, ds_out, o, mg, k, coef, n_lat, w):
    t_rows, d = ds_out.shape
    tm = _row_tm(t_rows, n_lat)
    n_groups, grp, first = _groups(t_rows, tm, n_lat)
    n = w.shape[0]

    def body(ds_ref, o_ref, mg_ref, w_ref, do_ref, y_ref, dg_ref):
        dd = coef * ds_ref[...]
        do = (dd * mg_ref[0, 3 * k + 2:3 * k + 3, :]).astype(BF16)
        do_ref[...] = do
        _accumulate([dg_ref], [_colsum(dd * o_ref[...].astype(F32))], first(pl.program_id(0)))
        y_ref[...] = _dot(do, w_ref[...], NT)

    row = pl.BlockSpec((tm, d), lambda i: (i, 0))
    return pl.pallas_call(
        body, name=name, grid=(t_rows // tm,),
        in_specs=[row, row, pl.BlockSpec((1,) + mg.shape[1:], lambda i: (grp(i), 0, 0)), pl.BlockSpec(w.shape, lambda i: (0, 0))],
        out_specs=[row, pl.BlockSpec((tm, n), lambda i: (i, 0)), pl.BlockSpec((1, 1, d), lambda i: (grp(i), 0, 0))],
        out_shape=[SDS((t_rows, d), BF16), SDS((t_rows, n), F32), SDS((n_groups, 1, d), F32)],
        compiler_params=_cparams(),
    )(ds_out, o, mg, w)


def _row_tm(t_rows, n_lat):
    return _tile(math.gcd(t_rows, n_lat), 256, 16)


def _rmsnorm_fwd(name, x, width, colblk, gain, t_rows):
    def fn(rv, vv):
        n, _ = _rms(rv[0])
        return [n * vv[0]], []

    (y,), _ = _rowwise(name, fn, t_rows, _tile(t_rows, 256, 16), t_rows, [(x, width, colblk)],
                       [gain.reshape(1, 1, width)], [(width, BF16)], [])
    return y


def _rmsnorm_bwd(name, x, width, colblk, dy, gain, t_rows, out_dtype=F32, into=None):
    def fn(rv, vv):
        n, r = _rms(rv[0])
        return [_rms_bwd(n, r, rv[1] * vv[0])], [_colsum(rv[1] * n)]

    (dx,), (dgain,) = _rowwise(name, fn, t_rows, _tile(t_rows, 256, 16), t_rows,
                               [(x, width, colblk), (dy, width, 0)], [gain.reshape(1, 1, width)],
                               [(width, out_dtype)], [width], into)
    return dx, dgain


def _final_loss(name, h, target, gain):
    t_rows = h.shape[0]
    inv_d = 1.0 / D_MODEL

    def fn(rv, vv):
        g = vv[0]
        n, r = _rms(rv[0])
        e = n * g - rv[1]
        dy = e * inv_d
        return [_rms_bwd(n, r, dy * g)], [_colsum(e * e), _colsum(dy * n)]

    (dh,), (sq, dgain) = _rowwise(name, fn, t_rows, _tile(t_rows, 256, 16), t_rows,
                                  [(h, D_MODEL, 0), (target, D_MODEL, 0)], [gain.reshape(1, 1, D_MODEL)],
                                  [(D_MODEL, F32)], [D_MODEL, D_MODEL])
    return dh, sq, dgain


def _exact_dot(x, m_ref):
    hi = x.astype(BF16)
    lo = (x - hi.astype(F32)).astype(BF16)
    return _dot(hi, m_ref[...], NN) + _dot(lo, m_ref[...], NN)


def _rope(name, z, width, colblk, cos32, sin32, layout, backward, out_dtype, remap=None, into=None):
    t_rows = cos32.shape[0]
    expand, plain, perm = layout
    w_in = remap.shape[1] if (remap is not None and backward) else width
    w_out = remap.shape[1] if (remap is not None and not backward) else width
    extra = [] if remap is None else [remap.T if backward else remap]
    dest = [] if into is None else [into[0]]

    def body(z_ref, c_ref, s_ref, e_ref, m_ref, p_ref, *rest):
        o_ref = rest[-1]
        zz = z_ref[...]
        if remap is not None and backward:
            zz = _exact_dot(zz, rest[0])
        cos = _exact_dot(c_ref[...], e_ref) + m_ref[...]
        sin = _exact_dot(s_ref[...], e_ref)
        rot = _exact_dot(zz * sin if backward else zz, p_ref)
        if not backward:
            rot = rot * sin
        res = zz * cos + rot
        if remap is not None and not backward:
            res = _dot(res.astype(BF16), rest[0][...], NN)
        o_ref[...] = res.astype(o_ref.dtype)

    tm = _tile(t_rows, 256, 16)
    f_spec = pl.BlockSpec((tm, QK_ROPE), lambda i: (i, 0))
    return pl.pallas_call(
        body, name=name, grid=(t_rows // tm,),
        in_specs=[pl.BlockSpec((tm, w_in), lambda i: (i, colblk)), f_spec, f_spec,
                  pl.BlockSpec((QK_ROPE, width), lambda i: (0, 0)), pl.BlockSpec((1, width), lambda i: (0, 0)),
                  pl.BlockSpec((width, width), lambda i: (0, 0))]
        + [pl.BlockSpec(e.shape, lambda i: (0, 0)) for e in extra] + [pl.BlockSpec(memory_space=pl.ANY)] * len(dest),
        out_specs=pl.BlockSpec((tm, w_out), lambda i: (i, 0 if into is None else into[1])),
        out_shape=SDS((t_rows, w_out), out_dtype) if into is None else SDS(into[0].shape, into[0].dtype),
        input_output_aliases={} if into is None else {6 + len(extra): 0}, compiler_params=_cparams(),
    )(z, cos32, sin32, expand, plain, perm.T if backward else perm, *extra, *dest)


def _window_sum(x, w, transposed):
    n_rows = x.shape[0]
    zeros = jnp.zeros((POOL_PAD, x.shape[1]), F32)
    y = jnp.concatenate([zeros, x, zeros], axis=0)
    total = n_rows + 2 * POOL_PAD
    if transposed:
        y = y + pltpu.roll(y, total - 1, 0)
    else:
        y = y + pltpu.roll(y, 1, 0)
    step = 1
    while 2 * step < w:
        y = pltpu.roll(y, step, 0) + pltpu.roll(y, total - step, 0)
        step *= 2
    return y[POOL_PAD:POOL_PAD + n_rows]


def _window_count(n_rows, w):
    t = lax.broadcasted_iota(jnp.int32, (n_rows, 1), 0)
    lo = jnp.maximum(t - w // 2, 0)
    hi = jnp.minimum(t + (w - w // 2 - 1), n_rows - 1)
    return (hi - lo + 1).astype(F32)


def _pool_fwd(name, proj, n_rows, w_grp, scale):
    def body(x_ref, w_ref, sc_ref, y_ref, p_ref):
        for g, w in enumerate(POOL_WINDOWS):
            cols = slice(g * POOL_GROUP_DIM, (g + 1) * POOL_GROUP_DIM)
            x = x_ref[:, cols]
            p = _window_sum(x, w, False) * (1.0 / _window_count(n_rows, w)) - x
            pb = p.astype(BF16)
            p_ref[:, cols] = pb
            y_ref[:, cols] = (_dot(pb, w_ref[g], NN) * sc_ref[:, cols]).astype(BF16)

    blk = pl.BlockSpec((n_rows, POOL_DIM), lambda i: (0, 0))
    return pl.pallas_call(
        body, name=name, grid=(1,),
        in_specs=[blk, pl.BlockSpec(w_grp.shape, lambda i: (0, 0, 0)), pl.BlockSpec((1, POOL_DIM), lambda i: (0, 0))],
        out_specs=[blk, blk], out_shape=[SDS((n_rows, POOL_DIM), BF16)] * 2, compiler_params=_cparams(),
    )(proj, w_grp, scale)


def _pool_bwd(name, dcat, n_rows, p, w_grp, scale, into):
    def body(dy_ref, p_ref, w_ref, sc_ref, into_ref, dx_ref, dw_ref, dsc_ref):
        for g, w in enumerate(POOL_WINDOWS):
            cols = slice(g * POOL_GROUP_DIM, (g + 1) * POOL_GROUP_DIM)
            dy = dy_ref[:, cols]
            pb = p_ref[:, cols]
            pw = _dot(pb, w_ref[g], NN)
            dsc_ref[:, cols] = _colsum(dy * pw)
            dpw = (dy * sc_ref[:, cols]).astype(BF16)
            dw_ref[g] = _dot(pb, dpw, TN)
            dp = _dot(dpw, w_ref[g], NT)
            dx_ref[:, cols] = (_window_sum(dp * (1.0 / _window_count(n_rows, w)), w, True) - dp).astype(BF16)

    blk = pl.BlockSpec((n_rows, POOL_DIM), lambda i: (0, 0))
    w_spec = pl.BlockSpec(w_grp.shape, lambda i: (0, 0, 0))
    v_spec = pl.BlockSpec((1, POOL_DIM), lambda i: (0, 0))
    return pl.pallas_call(
        body, name=name, grid=(1,), in_specs=[blk, blk, w_spec, v_spec, pl.BlockSpec(memory_space=pl.ANY)],
        out_specs=[blk, w_spec, v_spec],
        out_shape=[SDS(into.shape, into.dtype), SDS(w_grp.shape, F32), SDS((1, POOL_DIM), F32)],
        input_output_aliases={4: 0}, compiler_params=_cparams(),
    )(dcat, p, w_grp, scale, into)


def _head_keys(kv_blk, k_rope):
    lane = lax.broadcasted_iota(jnp.int32, (1, HEAD_PAD), 1)
    return jnp.where(lane < QK_NOPE, kv_blk, k_rope)


def _attn_fwd(name, q, kv, k_rope, n_q):
    n_k = kv.shape[0]
    h = kv.shape[1] // HEAD_PAD
    tq = _tile(n_q, 256, 16)

    def body(q_ref, kv_ref, kr_ref, o_ref, lse_ref):
        kvb = kv_ref[...]
        s = _dot(q_ref[...], _head_keys(kvb, kr_ref[...]), NT) * ATTN_SCALE
        m = jnp.max(s, axis=-1, keepdims=True)
        e = jnp.exp(s - m)
        l = jnp.sum(e, axis=-1, keepdims=True)
        p = (e * (1.0 / l)).astype(BF16)
        lane = lax.broadcasted_iota(jnp.int32, (1, HEAD_PAD), 1)
        o_ref[...] = jnp.where(lane >= QK_NOPE, _dot(p, kvb, NN), 0.0).astype(BF16)
        lse_ref[...] = m + jnp.log(l)

    blk = pl.BlockSpec((tq, HEAD_PAD), lambda hh, i: (i, hh))
    return pl.pallas_call(
        body, name=name, grid=(h, n_q // tq),
        in_specs=[blk, pl.BlockSpec((n_k, HEAD_PAD), lambda hh, i: (0, hh)),
                  pl.BlockSpec((n_k, HEAD_PAD), lambda hh, i: (0, 0))],
        out_specs=[blk, pl.BlockSpec((None, tq, 1), lambda hh, i: (hh, i, 0))],
        out_shape=[SDS((n_q, h * HEAD_PAD), BF16), SDS((h, n_q, 1), F32)], compiler_params=_cparams(),
    )(q, kv, k_rope)


def _attn_bwd(name, q, kv, k_rope, o, lse, dy, dy_col0, n_q):
    n_k = kv.shape[0]
    h = kv.shape[1] // HEAD_PAD
    tq = _tile(n_q, 256, 16)
    n_i = n_q // tq

    def body(q_ref, kv_ref, kr_ref, o_ref, lse_ref, do_ref, dq_ref, dkv_ref, dkr_ref, acc_k, acc_v):
        hh, i = pl.program_id(0), pl.program_id(1)
        qq, kvb = q_ref[...], kv_ref[...]
        kk = _head_keys(kvb, kr_ref[...])
        d_o = do_ref[...]
        dd = d_o.astype(BF16)
        s = _dot(qq, kk, NT) * ATTN_SCALE
        p = jnp.exp(s - lse_ref[...])
        dp = _dot(dd, kvb, NT)
        delta = jnp.sum(d_o * o_ref[...].astype(F32), axis=-1, keepdims=True)
        ds = (p * (dp - delta) * ATTN_SCALE).astype(BF16)
        dq_ref[...] = _dot(ds, kk, NN)
        dk = _dot(ds, qq, TN)
        dv = _dot(p.astype(BF16), dd, TN)

        @pl.when(i == 0)
        def _():
            acc_k[...] = dk
            acc_v[...] = dv

        @pl.when(i > 0)
        def _():
            acc_k[...] += dk
            acc_v[...] += dv

        @pl.when(i == n_i - 1)
        def _():
            lane = lax.broadcasted_iota(jnp.int32, (1, HEAD_PAD), 1)
            dkv_ref[...] = jnp.where(lane < QK_NOPE, acc_k[...], acc_v[...]).astype(BF16)
            rope = jnp.where((lane >= QK_NOPE) & (lane < QK_HEAD), acc_k[...], 0.0)

            @pl.when(hh == 0)
            def _():
                dkr_ref[...] = rope

            @pl.when(hh > 0)
            def _():
                dkr_ref[...] += rope

    blk = pl.BlockSpec((tq, HEAD_PAD), lambda hh, i: (i, hh))
    kv_spec = pl.BlockSpec((n_k, HEAD_PAD), lambda hh, i: (0, hh))
    shared = pl.BlockSpec((n_k, HEAD_PAD), lambda hh, i: (0, 0))
    return pl.pallas_call(
        body, name=name, grid=(h, n_i),
        in_specs=[blk, kv_spec, shared, blk, pl.BlockSpec((None, tq, 1), lambda hh, i: (hh, i, 0)),
                  pl.BlockSpec((tq, HEAD_PAD), lambda hh, i: (i, dy_col0 + hh))],
        out_specs=[blk, kv_spec, shared],
        out_shape=[SDS((n_q, h * HEAD_PAD), F32), SDS((n_k, h * HEAD_PAD), BF16), SDS((n_k, HEAD_PAD), F32)],
        scratch_shapes=[pltpu.VMEM((n_k, HEAD_PAD), F32), pltpu.VMEM((n_k, HEAD_PAD), F32)],
        compiler_params=_cparams(),
    )(q, kv, k_rope, o, lse, dy)


CONV_COLS = 256


def _shift_rows(x, d):
    n_rows = x.shape[0]
    t = lax.broadcasted_iota(jnp.int32, (n_rows, 1), 0)
    if d > 0:
        return jnp.where(t >= d, pltpu.roll(x, d, 0), 0.0)
    return jnp.where(t < n_rows + d, pltpu.roll(x, n_rows + d, 0), 0.0)


def _conv_fwd(name, z3, conv_w):
    n_rows = z3.shape[0]
    nb = D_MODEL // CONV_COLS

    def body(b_ref, c_ref, v_ref, w_ref, y_ref):
        z = c_ref[...] * v_ref[...]
        zc = w_ref[0:1, :] * _shift_rows(z, 1) + w_ref[1:2, :] * z + w_ref[2:3, :] * _shift_rows(z, -1)
        y_ref[...] = (b_ref[...] * zc).astype(BF16)

    def part(k):
        return pl.BlockSpec((n_rows, CONV_COLS), lambda j: (0, k * nb + j))

    return pl.pallas_call(
        body, name=name, grid=(nb,),
        in_specs=[part(0), part(1), part(2), pl.BlockSpec((3, CONV_COLS), lambda j: (0, j))],
        out_specs=pl.BlockSpec((n_rows, CONV_COLS), lambda j: (0, j)),
        out_shape=SDS((n_rows, D_MODEL), BF16), compiler_params=_cparams(),
    )(z3, z3, z3, conv_w)


def _conv_bwd(name, dy, z3, conv_w):
    n_rows = z3.shape[0]
    nb = D_MODEL // CONV_COLS

    def body(dy_ref, b_ref, c_ref, v_ref, w_ref, db_ref, dc_ref, dv_ref, dw_ref):
        c, v, d_y = c_ref[...], v_ref[...], dy_ref[...]
        z = c * v
        z_dn, z_up = _shift_rows(z, 1), _shift_rows(z, -1)
        zc = w_ref[0:1, :] * z_dn + w_ref[1:2, :] * z + w_ref[2:3, :] * z_up
        db_ref[...] = (d_y * zc).astype(BF16)
        dzc = d_y * b_ref[...]
        dz = w_ref[0:1, :] * _shift_rows(dzc, -1) + w_ref[1:2, :] * dzc + w_ref[2:3, :] * _shift_rows(dzc, 1)
        dc_ref[...] = (dz * v).astype(BF16)
        dv_ref[...] = (dz * c).astype(BF16)
        dw_ref[0:1, :] = _colsum(dzc * z_dn)
        dw_ref[1:2, :] = _colsum(dzc * z)
        dw_ref[2:3, :] = _colsum(dzc * z_up)

    def part(k):
        return pl.BlockSpec((n_rows, CONV_COLS), lambda j: (0, k * nb + j))

    col = pl.BlockSpec((n_rows, CONV_COLS), lambda j: (0, j))
    w_spec = pl.BlockSpec((3, CONV_COLS), lambda j: (0, j))
    return pl.pallas_call(
        body, name=name, grid=(nb,), in_specs=[col, part(0), part(1), part(2), w_spec],
        out_specs=[col, col, col, w_spec],
        out_shape=[SDS((n_rows, D_MODEL), BF16)] * 3 + [SDS((3, D_MODEL), F32)], compiler_params=_cparams(),
    )(dy, z3, z3, z3, conv_w)


def _silu_rows(name, x):
    def body(x_ref, s_ref, d_ref):
        xx = x_ref[...]
        sg = jax.nn.sigmoid(xx)
        s_ref[...] = (xx * sg).astype(BF16)
        d_ref[...] = sg * (1.0 + xx * (1.0 - sg))

    return pl.pallas_call(body, name=name, out_shape=[SDS(x.shape, BF16), SDS(x.shape, F32)])(x)


def _sum_rows(name, x, scale=None):
    r, n = x.shape
    tn = _tile(n, 32768, 128)

    def body(*refs):
        acc = jnp.sum(refs[0][...].astype(F32), axis=0, keepdims=True)
        if scale is not None:
            acc = acc * refs[1][...]
        refs[-1][...] = acc

    in_specs = [pl.BlockSpec((r, tn), lambda j: (0, j))]
    args = [x]
    if scale is not None:
        in_specs.append(pl.BlockSpec((1, tn), lambda j: (0, j)))
        args.append(scale)
    return pl.pallas_call(body, name=name, grid=(n // tn,), in_specs=in_specs,
                          out_specs=pl.BlockSpec((1, tn), lambda j: (0, j)), out_shape=SDS((1, n), F32))(*args)


def _me_operand(me):
    return jnp.reshape(me, (1,)).astype(jnp.int32)


def _sum_slots(name, slots, src, me):
    n_slots, r, c = slots.shape
    tr = _tile(r, 432, 16)

    def body(me_ref, own_ref, x_ref, o_ref):
        acc = own_ref[...].astype(F32)
        for sl in range(n_slots):
            acc = acc + x_ref[sl].astype(F32)
        o_ref[...] = acc

    grid_spec = pltpu.PrefetchScalarGridSpec(
        num_scalar_prefetch=1, grid=(r // tr,),
        in_specs=[pl.BlockSpec((None, tr, c), lambda i, me_ref: (me_ref[0], i, 0)),
                  pl.BlockSpec((n_slots, tr, c), lambda i, me_ref: (0, i, 0))],
        out_specs=pl.BlockSpec((tr, c), lambda i, me_ref: (i, 0)))
    return pl.pallas_call(body, name=name, grid_spec=grid_spec, out_shape=SDS((r, c), F32),
                          compiler_params=_cparams())(_me_operand(me), src, slots)


def _adamw(name, w, g, m, v):
    shape = w.shape
    cols = shape[-1]
    rows = w.size // cols
    tr = _tile(rows, 512, 8)
    bc1 = 1.0 - ADAM_B1 ** ADAM_STEP
    bc2 = 1.0 - ADAM_B2 ** ADAM_STEP

    def body(w_ref, g_ref, m_ref, v_ref, d_ref, nm_ref, nv_ref):
        gg = g_ref[...]
        nm = ADAM_B1 * m_ref[...] + (1.0 - ADAM_B1) * gg
        nv = ADAM_B2 * v_ref[...] + (1.0 - ADAM_B2) * (gg * gg)
        nm_ref[...] = nm
        nv_ref[...] = nv
        d_ref[...] = -ADAM_LR * ((nm / bc1) / (jnp.sqrt(nv / bc2) + ADAM_EPS) + ADAM_WD * w_ref[...])

    spec = pl.BlockSpec((tr, cols), lambda i: (i, 0))
    outs = pl.pallas_call(body, name=name, grid=(rows // tr,), in_specs=[spec] * 4, out_specs=[spec] * 3,
                          out_shape=[SDS((rows, cols), F32)] * 3, compiler_params=_cparams())(
        w.reshape(rows, cols), g.reshape(rows, cols), m.reshape(rows, cols), v.reshape(rows, cols))
    return tuple(t.reshape(shape) for t in outs)


def _exchange(name, x, scatter, after=None):
    blk = x.shape[1:] if scatter else x.shape
    extra = [] if after is None else [after]

    def body(x_ref, *rest):
        out_ref, send_sems, recv_sems, local_sem = rest[len(extra):]
        mx, my, mc = lax.axis_index("x"), lax.axis_index("y"), lax.axis_index("c")
        me = 4 * mx + 2 * my + mc
        own = pltpu.make_async_copy(x_ref.at[me] if scatter else x_ref, out_ref.at[me], local_sem)
        own.start()
        copies = []
        for kk in range(1, N_DEV):
            px = jnp.bitwise_xor(mx, (kk >> 2) & 1)
            py = jnp.bitwise_xor(my, (kk >> 1) & 1)
            pc = jnp.bitwise_xor(mc, kk & 1)
            peer = 4 * px + 2 * py + pc
            send = pltpu.make_async_remote_copy(
                src_ref=x_ref.at[peer] if scatter else x_ref, dst_ref=out_ref.at[me],
                send_sem=send_sems.at[kk - 1], recv_sem=recv_sems.at[kk - 1],
                device_id=(px, py, pc), device_id_type=MESH)
            send.start()
            arrival = pltpu.make_async_remote_copy(
                src_ref=x_ref.at[peer] if scatter else x_ref, dst_ref=out_ref.at[peer],
                send_sem=send_sems.at[kk - 1], recv_sem=recv_sems.at[kk - 1],
                device_id=(px, py, pc), device_id_type=MESH)
            copies.append((send, arrival))
        for send, arrival in copies:
            arrival.wait_recv()
            send.wait_send()
        own.wait()

    return pl.pallas_call(
        body, name=name, out_shape=SDS((N_DEV,) + tuple(blk), x.dtype),
        in_specs=[pl.BlockSpec(memory_space=pl.ANY)] * (1 + len(extra)), out_specs=pl.BlockSpec(memory_space=pl.ANY),
        scratch_shapes=[pltpu.SemaphoreType.DMA((N_DEV - 1,)), pltpu.SemaphoreType.DMA((N_DEV - 1,)),
                        pltpu.SemaphoreType.DMA],
    )(x, *extra)


def _rope_perm(pre, reps, post):
    half = QK_ROPE // 4
    width = reps * (pre + QK_ROPE) + post
    p = np.zeros((width, width), np.float32)
    for rep in range(reps):
        s0 = rep * (pre + QK_ROPE) + pre
        for base in (s0, s0 + 2 * half):
            for i in range(half):
                p[base + half + i, base + i] = -1.0
                p[base + i, base + half + i] = 1.0
    return p


def _rope_layout(pre, reps, post):
    width = reps * (pre + QK_ROPE) + post
    expand = np.zeros((QK_ROPE, width), np.float32)
    plain = np.ones((1, width), np.float32)
    for rep in range(reps):
        s0 = rep * (pre + QK_ROPE) + pre
        expand[np.arange(QK_ROPE), s0 + np.arange(QK_ROPE)] = 1.0
        plain[0, s0:s0 + QK_ROPE] = 0.0
    return jnp.asarray(expand, BF16), jnp.asarray(plain, F32), jnp.asarray(_rope_perm(pre, reps, post), BF16)


def _head_spread():
    spread = np.zeros((HEADS * QK_HEAD, HEADS * HEAD_PAD), np.float32)
    for hh in range(HEADS):
        spread[hh * QK_HEAD + np.arange(QK_HEAD), hh * HEAD_PAD + np.arange(QK_HEAD)] = 1.0
    return jnp.asarray(spread, BF16)


def _rope_factors(n_lat, t_rows):
    half = QK_ROPE // 4
    pos = jnp.arange(n_lat)
    freqs = jnp.power(ROPE_THETA, -jnp.arange(0, 2 * half, 2, dtype=F32) / (2 * half))
    ang_r = (pos // GRID_W).astype(F32)[:, None] * freqs
    ang_c = (pos % GRID_W).astype(F32)[:, None] * freqs
    ang = jnp.concatenate([ang_r, ang_r, ang_c, ang_c], axis=-1)
    rest = t_rows - n_lat
    return (jnp.concatenate([jnp.cos(ang), jnp.ones((rest, QK_ROPE), F32)], axis=0),
            jnp.concatenate([jnp.sin(ang), jnp.zeros((rest, QK_ROPE), F32)], axis=0))


def _ffn_half_fwd(tag, s, mg, k, feed, i, coef, n_lat):
    wg_t, wu_t = feed.weights(f"{tag}_up", [f"gate_t{i}", f"up_t{i}"], s)
    u, a, b, hid = _ffn_up(f"{tag}_up", s, mg, k, n_lat, wg_t, wu_t)
    (wd,) = feed.weights(f"{tag}_down", [f"down{i}"], hid)
    s_out, o = _mm_resid(f"{tag}_down", [(hid, wd)], s, mg, k, coef, n_lat)
    return s_out, (s, u, a, b, hid, o, wg_t, wu_t, wd)


def _ffn_half_bwd(tag, ds_out, saved, mg, k, feed, i, coef, n_lat, out_rows=None):
    s, u, a, b, hid, o, wg_t, wu_t, wd = saved
    do, da, db, dgate = _ffn_dact(f"{tag}_dact", ds_out, o, mg, k, coef, n_lat, wd, a, b)
    dwd = _mm(f"{tag}_dwd", [(hid, do)], "tn", BF16)
    dwg_t, dwu_t = _dw_pair(f"{tag}_dwgu", da, db, u)
    token = feed.grads(tag, {f"down{i}": dwd, f"gate_t{i}": dwg_t, f"up_t{i}": dwu_t})
    ds_in, (dshift, dscale, dgain) = _du_adaln(f"{tag}_du", [(da, wg_t), (db, wu_t)], s, ds_out, mg, k, n_lat,
                                               _after(token), out_rows)
    return ds_in, dict(shift=dshift, scale=dscale, gate=dgate, gain=dgain)


def _after(token):
    return jnp.zeros((1, D_MODEL), F32) + token


def _mod_grad(parts, n_groups):
    rows = []
    zero = jnp.zeros((n_groups, 1, D_MODEL), F32)
    for k in range(3):
        for nm in ("shift", "scale", "gate"):
            t = parts[k].get(nm, zero)
            if t.shape[0] < n_groups:
                t = jnp.concatenate([t, jnp.zeros((n_groups - t.shape[0], 1, D_MODEL), F32)], axis=0)
            rows.append(t)
    return jnp.concatenate(rows, axis=1).reshape(n_groups, N_MOD * D_MODEL)


def _local_step(x, ctx, target, mod_h, mod_g, norm_g, feed, pool_w, pool_scale, q_norm_g, kv_norm_g, conv_w,
                final_norm_g):
    n_lat, n_ctx = x.shape[0], ctx.shape[0]
    t_all = n_lat + n_ctx
    mg0 = jnp.stack([jnp.concatenate([mod_h[0], norm_g[0]], axis=0), jnp.concatenate([mod_g, norm_g[0]], axis=0)])
    mg1 = jnp.concatenate([mod_h[1], norm_g[1]], axis=0)[None]

    s0 = jnp.concatenate([x, ctx], axis=0)
    s1, sv_f00 = _ffn_half_fwd("l0f0", s0, mg0, 0, feed, 0, 0.5, n_lat)

    w_in, w_uq, w_ukv_t, w_ab_out = feed.weights("l0m", ["in_t", "uq", "ukv_t", "ab_out"], s1)
    kv_rows = KV_RANK + QK_ROPE
    w_in_t = jnp.concatenate([
        w_in[:POOL_DIM], jnp.zeros((PA_CQ - POOL_DIM, D_MODEL), BF16), w_in[POOL_DIM:POOL_DIM + Q_RANK],
        w_in[POOL_DIM + Q_RANK:], jnp.zeros((PA_KV_W - kv_rows, D_MODEL), BF16)], axis=0)
    ua, proj = _adaln_mm("l0m_proj", s1, mg0, 1, n_lat, w_in_t)
    pool_y, pool_p = _pool_fwd("l0m_pool", proj, n_lat, pool_w.astype(BF16), pool_scale)
    nq = _rmsnorm_fwd("l0m_qnorm", proj, Q_RANK, PA_CQ // Q_RANK, q_norm_g, n_lat)
    q_lin = _mm("l0m_q", [(nq, w_uq)], "nn", F32, 512, 768)
    cos32, sin32 = _rope_factors(n_lat, t_all)
    lay_q, lay_k = _rope_layout(QK_NOPE, HEADS, 0), _rope_layout(KV_RANK, 1, PA_KV_W - kv_rows)
    spread = _head_spread()
    q_flat = _rope("l0m_qrope", q_lin, Q_RANK, 0, cos32[:n_lat], sin32[:n_lat], lay_q, False, BF16, spread)
    kvr = _rope("l0m_krope", proj, PA_KV_W, PA_KV // PA_KV_W, cos32, sin32, lay_k, False, F32)
    nkv = _rmsnorm_fwd("l0m_kvnorm", kvr, KV_RANK, 0, kv_norm_g, t_all)
    kv = _mm("l0m_kv", [(nkv, w_ukv_t)], "nt", BF16, 768, 512)
    k_rope = jnp.pad(kvr[:, KV_RANK:KV_RANK + QK_ROPE].astype(BF16), ((0, 0), (QK_NOPE, HEAD_PAD - QK_HEAD)))
    o_flat, lse = _attn_fwd("l0m_attn", q_flat, kv, k_rope, n_lat)
    w_o_pad = jnp.pad(w_ab_out[POOL_DIM:].reshape(HEADS, V_HEAD, D_MODEL),
                      ((0, 0), (HEAD_PAD - V_HEAD, 0), (0, 0))).reshape(HEADS * HEAD_PAD, D_MODEL)
    w_o_pool = w_ab_out[:POOL_DIM]
    h1 = s1[:n_lat]
    h2, mix_o = _mm_resid("l0m_out", [(pool_y, w_o_pool), (o_flat, w_o_pad)], h1, mg0[:1], 1, 1.0, n_lat)

    h3, sv_f01 = _ffn_half_fwd("l0f1", h2, mg0[:1], 2, feed, 1, 0.5, n_lat)

    h4, sv_f10 = _ffn_half_fwd("l1f0", h3, mg1, 0, feed, 2, 0.5, n_lat)
    w_cin_t, w_c_out = feed.weights("l1m", ["cin_t", "c_out"], h4)
    uc, z3 = _adaln_mm("l1m_in", h4, mg1, 1, n_lat, w_cin_t)
    yc = _conv_fwd("l1m_conv", z3, conv_w)
    h5, conv_o = _mm_resid("l1m_out", [(yc, w_c_out)], h4, mg1, 1, 1.0, n_lat)
    h6, sv_f11 = _ffn_half_fwd("l1f1", h5, mg1, 2, feed, 3, 0.5, n_lat)

    dh6, sq_cols, d_final_g = _final_loss("loss_head", h6, target, final_norm_g)
    g = {}
    dh5, g["f11"] = _ffn_half_bwd("l1f1", dh6, sv_f11, mg1, 2, feed, 3, 0.5, n_lat)

    do_c, dyc, dgate_c = _gate_mm("l1m_dy", dh5, conv_o, mg1, 1, 1.0, n_lat, w_c_out)
    d_c_out = _mm("l1m_dwout", [(yc, do_c)], "tn", BF16)
    db_, dc_, dv_, d_conv_w = _conv_bwd("l1m_dconv", dyc, z3, conv_w)
    dz3 = jnp.concatenate([db_, dc_, dv_], axis=-1)
    d_cin_t = _mm("l1m_dwin", [(dz3, uc)], "tn", BF16)
    token = feed.grads("l1m", {"c_out": d_c_out, "cin_t": d_cin_t})
    dh4, (dsh_c, dsc_c, dgn_c) = _du_adaln("l1m_du", [(dz3, w_cin_t)], h4, dh5, mg1, 1, n_lat, _after(token))
    dh3, g["f10"] = _ffn_half_bwd("l1f0", dh4, sv_f10, mg1, 0, feed, 2, 0.5, n_lat)

    dh2, g["f01"] = _ffn_half_bwd("l0f1", dh3, sv_f01, mg0[:1], 2, feed, 1, 0.5, n_lat)

    w_back = jnp.concatenate([w_o_pool, w_o_pad], axis=0)
    do_a, dcat, dgate_a = _gate_mm("l0m_dcat", dh2, mix_o, mg0[:1], 1, 1.0, n_lat, w_back)
    d_o_pad = _mm("l0m_dwout_a", [(o_flat, do_a)], "tn", BF16)
    d_ab_out = jnp.concatenate([
        _mm("l0m_dwout_p", [(pool_y, do_a)], "tn", BF16),
        d_o_pad.reshape(HEADS, HEAD_PAD, D_MODEL)[:, HEAD_PAD - V_HEAD:].reshape(HEADS * V_HEAD, D_MODEL)], axis=0)
    dproj = jnp.zeros((t_all, PA_W), BF16)
    dproj, d_pool_w, d_pool_scale = _pool_bwd("l0m_dpool", dcat, n_lat, pool_p, pool_w.astype(BF16), pool_scale, dproj)
    dq_flat, dkv, dk_rope = _attn_bwd("l0m_dattn", q_flat, kv, k_rope, o_flat, lse, dcat, POOL_DIM // HEAD_PAD, n_lat)
    dq_lin = _rope("l0m_dqrope", dq_flat, Q_RANK, 0, cos32[:n_lat], sin32[:n_lat], lay_q, True, BF16, spread)
    d_uq = _mm("l0m_dwuq", [(nq, dq_lin)], "tn", BF16, 768, 768)
    dnq = _mm("l0m_dnq", [(dq_lin, w_uq)], "nt", F32, 512, 768)
    dproj, d_q_norm_g = _rmsnorm_bwd("l0m_dqnorm", proj, Q_RANK, PA_CQ // Q_RANK, dnq, q_norm_g, n_lat, BF16,
                                     (dproj, PA_CQ // Q_RANK))
    dnkv = _mm("l0m_dnkv", [(dkv, w_ukv_t)], "nn", F32, 768, 256)
    d_ukv_t = _mm("l0m_dwukv", [(dkv, nkv)], "tn", BF16, 512, 256)
    dckv, d_kv_norm_g = _rmsnorm_bwd("l0m_dkvnorm", kvr, KV_RANK, 0, dnkv, kv_norm_g, t_all)
    dkvr = jnp.concatenate([dckv, dk_rope[:, QK_NOPE:QK_HEAD],
                            jnp.zeros((t_all, PA_KV_W - KV_RANK - QK_ROPE), F32)], axis=-1)
    dproj = _rope("l0m_dkrope", dkvr, PA_KV_W, 0, cos32, sin32, lay_k, True, BF16, None, (dproj, PA_KV // PA_KV_W))
    d_in_pad = _mm("l0m_dwin", [(dproj, ua)], "tn", BF16, 640, 512)
    d_in_t = jnp.concatenate([d_in_pad[:POOL_DIM], d_in_pad[PA_CQ:PA_CQ + Q_RANK],
                              d_in_pad[PA_KV:PA_KV + kv_rows]], axis=0)
    token = feed.grads("l0m", {"ab_out": d_ab_out, "uq": d_uq, "ukv_t": d_ukv_t, "in_t": d_in_t})
    ds1, (dsh_a, dsc_a, dgn_a) = _du_adaln("l0m_du", [(dproj, w_in_t)], s1, dh2, mg0, 1, n_lat, _after(token))
    grad_x, g["f00"] = _ffn_half_bwd("l0f0", ds1, sv_f00, mg0, 0, feed, 0, 0.5, n_lat, out_rows=n_lat)

    dmod0 = _mod_grad([g["f00"], dict(shift=dsh_a, scale=dsc_a, gate=dgate_a), g["f01"]], 2)
    dmod1 = _mod_grad([g["f10"], dict(shift=dsh_c, scale=dsc_c, gate=dgate_c), g["f11"]], 1)
    d_norm_g = jnp.stack([
        jnp.concatenate([jnp.sum(g["f00"]["gain"], axis=0), jnp.sum(dgn_a, axis=0), g["f01"]["gain"][0]], axis=0),
        jnp.concatenate([g["f10"]["gain"][0], dgn_c[0], g["f11"]["gain"][0]], axis=0)])
    grads = dict(
        pool_w=d_pool_w, pool_scale=d_pool_scale, q_norm_g=d_q_norm_g[0], kv_norm_g=d_kv_norm_g[0],
        conv_w=d_conv_w, final_norm_g=d_final_g[0], norm_g=d_norm_g,
        mod_h=jnp.stack([dmod0[0], dmod1[0]]), mod_g=dmod0[1])
    return sq_cols, grad_x, grads


HBM_SPEC = pl.BlockSpec(memory_space=pltpu.HBM)
SEM_SPEC = pl.BlockSpec(memory_space=pltpu.SEMAPHORE)
ANY_SPEC = pl.BlockSpec(memory_space=pl.ANY)
SIDE_EFFECT = pltpu.SideEffectType.DATAFLOW_SIDE_EFFECTING
N_PEERS = N_DEV - 1


def _mesh_place():
    mx, my, mc = lax.axis_index("x"), lax.axis_index("y"), lax.axis_index("c")
    return mx, my, mc, 4 * mx + 2 * my + mc


def _peer(place, kk):
    mx, my, mc, _ = place
    px = jnp.bitwise_xor(mx, (kk >> 2) & 1)
    py = jnp.bitwise_xor(my, (kk >> 1) & 1)
    pc = jnp.bitwise_xor(mc, kk & 1)
    return (px, py, pc), 4 * px + 2 * py + pc


def _hbm(a):
    return pltpu.with_memory_space_constraint(a, pltpu.HBM)


def _landing(block, me):
    zone = lax.empty((N_DEV,) + block.shape, block.dtype)
    return lax.dynamic_update_slice(zone, block[None], (me,) + (0,) * block.ndim)


ALL_PEERS = tuple(range(1, N_DEV))
SIBLING = 1
CHIP_PEERS = (2, 4, 6)
RELAYED = (3, 5, 7)


def _exchange_start(name, srcs, lands, scatter, after, peers=ALL_PEERS):
    n = len(srcs)
    extra = [] if after is None else [after]

    def body(*refs):
        src, land = refs[:n], refs[n:2 * n]
        send_sems, recv_sems, token = refs[2 * n + len(extra)], refs[2 * n + len(extra) + 1], refs[-1]
        place = _mesh_place()
        for a in range(n):
            for kk in peers:
                dev, peer = _peer(place, kk)
                pltpu.make_async_remote_copy(
                    src_ref=src[a].at[peer] if scatter else src[a],
                    dst_ref=land[a].at[kk - 1] if scatter else land[a].at[place[3]],
                    send_sem=send_sems.at[a * N_PEERS + kk - 1], recv_sem=recv_sems.at[a * N_PEERS + kk - 1],
                    device_id=dev, device_id_type=MESH).start()
        token[...] = jnp.zeros_like(token)

    thru = [pltpu.HBM(t.shape, t.dtype) for t in (*srcs, *lands)]
    res = pl.pallas_call(
        body, name=name,
        out_shape=(pltpu.SemaphoreType.DMA((n * N_PEERS,)), pltpu.SemaphoreType.DMA((n * N_PEERS,)), *thru,
                   SDS((8, 128), F32)),
        in_specs=[HBM_SPEC] * (2 * n) + [ANY_SPEC] * len(extra),
        out_specs=(SEM_SPEC, SEM_SPEC, *([HBM_SPEC] * (2 * n)), pl.BlockSpec(memory_space=pltpu.VMEM)),
        input_output_aliases={i: 2 + i for i in range(2 * n)},
        compiler_params=pltpu.CompilerParams(has_side_effects=SIDE_EFFECT),
    )(*[_hbm(s) for s in srcs], *[_hbm(t) for t in lands], *extra)
    return res[0], res[1], list(res[2:2 + n]), list(res[2 + n:2 + 2 * n]), res[-1]


def _exchange_wait(name, send_sems, recv_sems, srcs, lands, places, scatter, after):
    n = len(srcs)

    def body(*refs):
        src, land = refs[:n], refs[n:2 * n]
        send, recv = refs[2 * n], refs[2 * n + 1]
        place = _mesh_place()
        for a in range(n):
            for kk in range(1, N_DEV):
                dev, peer = _peer(place, kk)
                cp = pltpu.make_async_remote_copy(
                    src_ref=src[a].at[peer] if scatter else src[a],
                    dst_ref=land[a].at[kk - 1] if scatter else land[a].at[peer],
                    send_sem=send.at[places[a] * N_PEERS + kk - 1], recv_sem=recv.at[places[a] * N_PEERS + kk - 1],
                    device_id=dev, device_id_type=MESH)
                cp.wait_send()
                cp.wait_recv()

    thru = [pltpu.HBM(t.shape, t.dtype) for t in (*srcs, *lands)]
    res = pl.pallas_call(
        body, name=name, out_shape=tuple(thru),
        in_specs=[HBM_SPEC] * (2 * n) + [SEM_SPEC, SEM_SPEC] + [ANY_SPEC] * len(after),
        out_specs=tuple([HBM_SPEC] * (2 * n)), input_output_aliases={i: i for i in range(2 * n)},
        compiler_params=pltpu.CompilerParams(has_side_effects=SIDE_EFFECT),
    )(*srcs, *lands, send_sems, recv_sems, *after)
    return list(res[:n]), list(res[n:])


def _gather_relay(name, send1, recv1, lands, places, after):
    n = len(lands)

    def body(*refs):
        land, s1, r1 = refs[:n], refs[n], refs[n + 1]
        s2, r2 = refs[n + 3], refs[n + 4]
        place = _mesh_place()
        sibling = _peer(place, SIBLING)[0]
        for a in range(n):
            for j, kk in enumerate(CHIP_PEERS):
                dev, origin = _peer(place, kk)
                block = land[a].at[origin]
                pltpu.make_async_remote_copy(
                    src_ref=block, dst_ref=block, send_sem=s1.at[places[a] * N_PEERS + kk - 1],
                    recv_sem=r1.at[places[a] * N_PEERS + kk - 1], device_id=dev, device_id_type=MESH).wait_recv()
                pltpu.make_async_remote_copy(
                    src_ref=block, dst_ref=block, send_sem=s2.at[a * 3 + j], recv_sem=r2.at[a * 3 + j],
                    device_id=sibling, device_id_type=MESH).start()

    res = pl.pallas_call(
        body, name=name,
        out_shape=(pltpu.SemaphoreType.DMA((3 * n,)), pltpu.SemaphoreType.DMA((3 * n,)),
                   *[pltpu.HBM(t.shape, t.dtype) for t in lands]),
        in_specs=[HBM_SPEC] * n + [SEM_SPEC, SEM_SPEC, ANY_SPEC],
        out_specs=(SEM_SPEC, SEM_SPEC, *([HBM_SPEC] * n)),
        input_output_aliases={i: 2 + i for i in range(n)},
        compiler_params=pltpu.CompilerParams(has_side_effects=SIDE_EFFECT),
    )(*lands, send1, recv1, after)
    return res[0], res[1], list(res[2:])


def _gather_wait(name, send1, recv1, send2, recv2, srcs, lands, places, after):
    n = len(lands)

    def body(*refs):
        src, land = refs[:n], refs[n:2 * n]
        s1, r1, s2, r2 = refs[2 * n:2 * n + 4]
        place = _mesh_place()
        for a in range(n):
            for kk in (SIBLING,) + CHIP_PEERS:
                dev, origin = _peer(place, kk)
                first = pltpu.make_async_remote_copy(
                    src_ref=src[a], dst_ref=land[a].at[origin], send_sem=s1.at[places[a] * N_PEERS + kk - 1],
                    recv_sem=r1.at[places[a] * N_PEERS + kk - 1], device_id=dev, device_id_type=MESH)
                first.wait_send()
                if kk == SIBLING:
                    first.wait_recv()
            for j, kk in enumerate(CHIP_PEERS):
                dev, origin = _peer(place, kk + 1)
                relay = pltpu.make_async_remote_copy(
                    src_ref=src[a], dst_ref=land[a].at[origin], send_sem=s2.at[a * 3 + j], recv_sem=r2.at[a * 3 + j],
                    device_id=dev, device_id_type=MESH)
                relay.wait_send()
                relay.wait_recv()

    arrays = (*srcs, *lands)
    res = pl.pallas_call(
        body, name=name, out_shape=tuple(pltpu.HBM(t.shape, t.dtype) for t in arrays),
        in_specs=[HBM_SPEC] * (2 * n) + [SEM_SPEC] * 4 + [ANY_SPEC], out_specs=tuple([HBM_SPEC] * (2 * n)),
        input_output_aliases={i: i for i in range(2 * n)},
        compiler_params=pltpu.CompilerParams(has_side_effects=SIDE_EFFECT),
    )(*arrays, send1, recv1, send2, recv2, after)
    return list(res[n:])


class _Feed:
    def __init__(self, shards, groups, me):
        self.shards, self.groups, self.me, self.pos = shards, groups, me, 0
        self.sems, self.srcs, self.lands = {}, {}, {}
        self.relays = {}
        self.pending = []

    def start(self, tag, names, after):
        srcs = [self.shards[nm] for nm in names]
        lands = [_landing(s, self.me) for s in srcs]
        send, recv, srcs, lands, self.token = _exchange_start(
            f"gather_start_{tag}", srcs, lands, False, after, (SIBLING,) + CHIP_PEERS)
        for i, nm in enumerate(names):
            self.sems[nm], self.srcs[nm], self.lands[nm] = (send, recv, i), srcs[i], lands[i]
        return self.token

    def _relay(self, gi, after):
        names = self.groups[gi]
        if gi not in self.relays:
            send, recv, _ = self.sems[names[0]]
            places = [self.sems[nm][2] for nm in names]
            send2, recv2, lands = _gather_relay(f"gather_relay_{gi}", send, recv, [self.lands[nm] for nm in names],
                                                places, after)
            for nm, t in zip(names, lands):
                self.lands[nm] = t
            self.relays[gi] = (send2, recv2)
            after = lands[0]
        return after

    def weights(self, tag, names, after):
        gi = self.pos
        assert names == self.groups[gi], (names, self.groups[gi])
        if gi == 0:
            after = self.token
        self._relay(gi, after)
        if 1 <= gi < len(self.groups) - 1:
            after = self._relay(gi + 1, after)
        send2, recv2 = self.relays[gi]
        send, recv, _ = self.sems[names[0]]
        got = _gather_wait(f"gather_wait_{tag}", send, recv, send2, recv2, [self.srcs[nm] for nm in names],
                           [self.lands[nm] for nm in names], [self.sems[nm][2] for nm in names], after)
        self.pos += 1
        return [t.reshape((N_DEV * t.shape[1],) + t.shape[2:]) for t in got]

    def grads(self, tag, full):
        names = list(full)
        srcs = [full[nm].reshape((N_DEV, full[nm].shape[0] // N_DEV) + full[nm].shape[1:]) for nm in names]
        lands = [lax.empty((N_PEERS,) + s.shape[1:], s.dtype) for s in srcs]
        send, recv, srcs, lands, token = _exchange_start(f"scatter_start_{tag}", srcs, lands, True, None)
        self.pending.append((tag, names, send, recv, srcs, lands))
        return token[0, 0]

    def collect(self, tags, after, keep_slots=()):
        out = {}
        for tag, names, send, recv, srcs, lands in self.pending:
            if tag not in tags:
                continue
            srcs, got = _exchange_wait(f"scatter_wait_{tag}", send, recv, srcs, lands, list(range(len(names))), True,
                                       after)
            for nm, slots, src in zip(names, got, srcs):
                out[nm] = ((slots, src) if nm.startswith(tuple(keep_slots))
                           else _sum_slots(f"reduce_{nm}", slots, src, self.me))
        return out


def _adamw_math(w, gg, m, v):
    nm = ADAM_B1 * m + (1.0 - ADAM_B1) * gg
    nv = ADAM_B2 * v + (1.0 - ADAM_B2) * (gg * gg)
    bc1 = 1.0 - ADAM_B1 ** ADAM_STEP
    bc2 = 1.0 - ADAM_B2 ** ADAM_STEP
    return -ADAM_LR * ((nm / bc1) / (jnp.sqrt(nv / bc2) + ADAM_EPS) + ADAM_WD * w), nm, nv


def _adamw_part(name, i, w, scattered, me, m, v, prev):
    n_parts, rows, cols = w.shape
    tr = _tile(rows, 256, 16)
    if prev is None:
        prev = tuple(lax.empty(w.shape, F32) for _ in range(4))

    slots, src = scattered

    def body(me_ref, w_ref, g_ref, own_ref, m_ref, v_ref, *rest):
        go_ref, d_ref, nm_ref, nv_ref = rest[4:]
        gg = own_ref[...].astype(F32)
        for sl in range(N_PEERS):
            gg = gg + g_ref[sl].astype(F32)
        d, nm, nv = _adamw_math(w_ref[...], gg, m_ref[...], v_ref[...])
        go_ref[...] = gg
        d_ref[...] = d
        nm_ref[...] = nm
        nv_ref[...] = nv

    part = pl.BlockSpec((None, tr, cols), lambda r, me_ref: (i, r, 0))
    grid_spec = pltpu.PrefetchScalarGridSpec(
        num_scalar_prefetch=1, grid=(rows // tr,),
        in_specs=[part, pl.BlockSpec((N_PEERS, tr, cols), lambda r, me_ref: (0, r, 0)),
                  pl.BlockSpec((None, tr, cols), lambda r, me_ref: (me_ref[0], r, 0)), part, part] + [ANY_SPEC] * 4,
        out_specs=[part] * 4)
    return pl.pallas_call(
        body, name=name, grid_spec=grid_spec, out_shape=[SDS(w.shape, F32)] * 4,
        input_output_aliases={6 + k: k for k in range(4)}, compiler_params=_cparams(),
    )(_me_operand(me), w, slots, src, m, v, *prev)


WEIGHT_NAMES = ("c_ctx", "norm_g", "w_mod", "b_mod", "ffn_w_gate", "ffn_w_up", "ffn_w_down", "ab_w_in", "pool_w",
                "pool_scale", "q_norm_g", "w_uq", "kv_norm_g", "w_ukv", "ab_w_out", "conv_w_in", "conv_w",
                "conv_w_out", "final_norm_g")


def kernel(x, c, ctx, c_ctx, norm_g, w_mod, b_mod, ffn_w_gate, ffn_w_up, ffn_w_down, ab_w_in, pool_w, pool_scale, q_norm_g, w_uq, kv_norm_g, w_ukv, ab_w_out, conv_w_in, conv_w, conv_w_out, final_norm_g, loss_target, m_c_ctx, m_norm_g, m_w_mod, m_b_mod, m_ffn_w_gate, m_ffn_w_up, m_ffn_w_down, m_ab_w_in, m_pool_w, m_pool_scale, m_q_norm_g, m_w_uq, m_kv_norm_g, m_w_ukv, m_ab_w_out, m_conv_w_in, m_conv_w, m_conv_w_out, m_final_norm_g, v_c_ctx, v_norm_g, v_w_mod, v_b_mod, v_ffn_w_gate, v_ffn_w_up, v_ffn_w_down, v_ab_w_in, v_pool_w, v_pool_scale, v_q_norm_g, v_w_uq, v_kv_norm_g, v_w_ukv, v_ab_w_out, v_conv_w_in, v_conv_w, v_conv_w_out, v_final_norm_g):
    weights = (c_ctx, norm_g, w_mod, b_mod, ffn_w_gate, ffn_w_up, ffn_w_down, ab_w_in, pool_w, pool_scale, q_norm_g,
               w_uq, kv_norm_g, w_ukv, ab_w_out, conv_w_in, conv_w, conv_w_out, final_norm_g)
    moms = (m_c_ctx, m_norm_g, m_w_mod, m_b_mod, m_ffn_w_gate, m_ffn_w_up, m_ffn_w_down, m_ab_w_in, m_pool_w,
            m_pool_scale, m_q_norm_g, m_w_uq, m_kv_norm_g, m_w_ukv, m_ab_w_out, m_conv_w_in, m_conv_w, m_conv_w_out,
            m_final_norm_g)
    vels = (v_c_ctx, v_norm_g, v_w_mod, v_b_mod, v_ffn_w_gate, v_ffn_w_up, v_ffn_w_down, v_ab_w_in, v_pool_w,
            v_pool_scale, v_q_norm_g, v_w_uq, v_kv_norm_g, v_w_ukv, v_ab_w_out, v_conv_w_in, v_conv_w, v_conv_w_out,
            v_final_norm_g)
    me = 4 * lax.axis_index("x") + 2 * lax.axis_index("y") + lax.axis_index("c")
    n_lat, n_ctx = x.shape[1], ctx.shape[1]
    d = D_MODEL
    mod_cols = w_mod.shape[-1]
    ng_sh, cw_sh = norm_g.shape[-1], conv_w.shape[-1]

    def ffn_shards(i):
        return {f"gate_t{i}": ffn_w_gate[i // 2, i % 2].T, f"up_t{i}": ffn_w_up[i // 2, i % 2].T,
                f"down{i}": ffn_w_down[i // 2, i % 2]}

    local = {**ffn_shards(0), "in_t": ab_w_in[0].T, "uq": w_uq[0], "ukv_t": w_ukv[0].T, "ab_out": ab_w_out[0],
             **ffn_shards(1), **ffn_shards(2), "cin_t": conv_w_in[0].T, "c_out": conv_w_out[0], **ffn_shards(3)}
    ffn_groups = [[[f"gate_t{i}", f"up_t{i}"], [f"down{i}"]] for i in range(4)]
    groups = [*ffn_groups[0], ["in_t", "uq", "ukv_t", "ab_out"], *ffn_groups[1], *ffn_groups[2], ["cin_t", "c_out"],
              *ffn_groups[3]]
    feed = _Feed({nm: a.astype(BF16) for nm, a in local.items()}, groups, me)

    small = jnp.concatenate([c.reshape(-1), norm_g.reshape(-1), conv_w.reshape(-1)])
    small_n = -(-small.shape[0] // 1024) * 1024
    small = jnp.pad(small, (0, small_n - small.shape[0])).reshape(small_n // 128, 128)
    small_all = _exchange("gather_small", small, False).reshape(N_DEV, small_n)
    c_all = small_all[:, :d]
    o1 = d + 6 * ng_sh
    norm_g_full = small_all[:, d:o1].reshape(N_DEV, 2, 3, ng_sh).transpose(1, 2, 0, 3).reshape(2, 3, d)
    conv_w_full = small_all[:, o1:o1 + 3 * cw_sh].reshape(N_DEV, 3, cw_sh).transpose(1, 0, 2).reshape(3, d)

    cond = jnp.concatenate([c_all, jnp.broadcast_to(c_ctx[None, :], (N_DEV, d))], axis=0)
    sil, dsil = _silu_rows("mod_silu", cond)
    w_mod_b = w_mod.astype(BF16)
    b_sh = lax.dynamic_slice(b_mod, (0, me * mod_cols), (2, mod_cols))
    m_part = jnp.stack([_mm(f"mod_fwd{l}", [(sil, w_mod_b[l])], "nn", F32, 16, 384, bias=b_sh[l:l + 1])
                        for l in range(2)], axis=1)
    m_all = _exchange("gather_mod", m_part.reshape(-1, 128), False).reshape(N_DEV, 2 * N_DEV, 2, mod_cols)
    m_mine = lax.dynamic_index_in_dim(m_all, me, axis=1, keepdims=False)
    mod_h = m_mine.transpose(1, 0, 2).reshape(2, N_MOD, d)
    mod_g = m_all[:, N_DEV, 0, :].reshape(N_MOD, d)

    first = feed.start("first", [nm for grp in groups[:3] for nm in grp], m_all)
    feed.start("rest", [nm for grp in groups[3:] for nm in grp], first)

    sq_cols, grad_x, g = _local_step(x[0], ctx[0], loss_target[0], mod_h, mod_g, norm_g_full, feed, pool_w[0],
                                  pool_scale, q_norm_g, kv_norm_g, conv_w_full, final_norm_g)
    w_of, m_of, v_of = (dict(zip(WEIGHT_NAMES, t)) for t in (weights, moms, vels))
    results = {}

    def update(nm, grad, view=lambda t: t):
        outs = _adamw(f"adamw_{nm}", view(w_of[nm]), grad.reshape(view(w_of[nm]).shape), view(m_of[nm]), view(v_of[nm]))
        results[nm] = tuple(view(t) for t in (grad.reshape(view(w_of[nm]).shape), *outs))

    def swap(t):
        return jnp.swapaxes(t, -1, -2)

    stacked = ("gate_t", "up_t", "down")
    early = feed.collect(["l1f1", "l1m", "l1f0", "l0f1", "l0m"], [grad_x], stacked)
    update("ab_w_in", early["in_t"], swap)
    update("w_uq", early["uq"])
    update("w_ukv", early["ukv_t"].T)
    update("ab_w_out", early["ab_out"])
    update("conv_w_in", early["cin_t"].T)
    update("conv_w_out", early["c_out"])
    ffn = {}
    for nm, prefix, view in (("ffn_w_gate", "gate_t", swap), ("ffn_w_up", "up_t", swap),
                             ("ffn_w_down", "down", lambda t: t)):
        w4, m4, v4 = (view(t).reshape((4,) + view(t).shape[-2:]) for t in (w_of[nm], m_of[nm], v_of[nm]))
        prev = None
        for i in (3, 2, 1):
            prev = _adamw_part(f"adamw_{nm}{i}", i, w4, early[f"{prefix}{i}"], me, m4, v4, prev)
        ffn[nm] = (prefix, view, w4, m4, v4, prev)
    done_early = [results[nm][1] for nm in results] + [state[5][1] for state in ffn.values()]
    late = feed.collect(["l0f0"], done_early, stacked)
    for nm, (prefix, view, w4, m4, v4, prev) in ffn.items():
        outs = _adamw_part(f"adamw_{nm}0", 0, w4, late[f"{prefix}0"], me, m4, v4, prev)
        results[nm] = tuple(view(t.reshape(view(w_of[nm]).shape)) for t in outs)

    dm = jnp.stack([g["mod_h"], jnp.stack([g["mod_g"], jnp.zeros_like(g["mod_g"])])])
    dm_all = _exchange("gather_dmod", dm.reshape(-1, 128), False, results["ffn_w_down"][1]).reshape(N_DEV, 2, 2, N_MOD * d)
    grad_b_mod = _sum_rows("dmod_bias", dm_all.reshape(2 * N_DEV, 2 * N_MOD * d)).reshape(2, N_MOD * d)
    dm_sh = lax.dynamic_slice(dm_all, (0, 0, 0, me * mod_cols), (N_DEV, 2, 2, mod_cols))
    gw_mod, cctx_parts = [], []
    for l in range(2):
        dm_l = dm_sh[:, :, l, :].transpose(1, 0, 2).reshape(2 * N_DEV, mod_cols).astype(BF16)
        gw_mod.append(_mm(f"mod_dw{l}", [(sil, dm_l)], "tn", F32, 512, 384))
        dm_ctx = jnp.concatenate([dm_l[N_DEV:], jnp.zeros((N_DEV, mod_cols), BF16)], axis=0)
        cctx_parts.append(_mm(f"mod_dcond{l}", [(dm_ctx, w_mod_b[l])], "nt", F32, 16, 512))
    cctx_part = _sum_rows("mod_dcond_sum", jnp.concatenate(cctx_parts, axis=0))
    update("w_mod", jnp.stack(gw_mod))
    update("b_mod", grad_b_mod)

    small_g = jnp.concatenate([g["pool_w"].reshape(-1), g["pool_scale"].reshape(-1), g["q_norm_g"].reshape(-1),
                               g["kv_norm_g"].reshape(-1), g["final_norm_g"].reshape(-1), g["norm_g"].reshape(-1),
                               g["conv_w"].reshape(-1), sq_cols.reshape(-1), cctx_part.reshape(-1)])
    sizes = [pool_w.size, pool_scale.size, q_norm_g.size, kv_norm_g.size, d, 6 * d, 3 * d, d, d]
    sg_n = -(-small_g.shape[0] // 1024) * 1024
    small_g = jnp.pad(small_g, (0, sg_n - small_g.shape[0]))
    sg_all = _exchange("gather_small_grads", small_g.reshape(-1, 128), False).reshape(N_DEV, sg_n)
    scale_vec = jnp.concatenate([jnp.ones((1, sum(sizes[:-1])), F32), dsil[N_DEV:N_DEV + 1],
                                 jnp.ones((1, sg_n - sum(sizes)), F32)], axis=1)
    sg = _sum_rows("small_grads_sum", sg_all, scale_vec)[0]
    cuts, pos = [], 0
    for sz in sizes:
        cuts.append(sg[pos:pos + sz])
        pos += sz
    g_pool_w, g_pool_scale, g_q_norm, g_kv_norm, g_final, g_norm_full, g_conv_full, sq_all, g_c_ctx = cuts
    loss = 0.5 * jnp.sum(sq_all) / d
    update("c_ctx", g_c_ctx)
    update("norm_g", lax.dynamic_slice(g_norm_full.reshape(2, 3, d), (0, 0, me * ng_sh), (2, 3, ng_sh)))
    update("conv_w", lax.dynamic_slice(g_conv_full.reshape(3, d), (0, me * cw_sh), (3, cw_sh)))
    update("pool_w", g_pool_w)
    update("pool_scale", g_pool_scale)
    update("q_norm_g", g_q_norm)
    update("kv_norm_g", g_kv_norm)
    update("final_norm_g", g_final)
    outs = [results[nm] for nm in WEIGHT_NAMES]
    return (loss, grad_x[None], *[o[0] for o in outs], *[o[1] for o in outs], *[o[2] for o in outs],
            *[o[3] for o in outs])
```

```python
import functools
import math

import jax
import jax.numpy as jnp
import numpy as np
from jax import lax
from jax.experimental import pallas as pl
from jax.experimental.pallas import tpu as pltpu

F32 = jnp.float32
BF16 = jnp.bfloat16
MESH = pl.DeviceIdType.MESH
SDS = jax.ShapeDtypeStruct

N_DEV = 8
D_MODEL = 1024
N_MOD = 9
D_FF = 2816
POOL_WINDOWS = (2, 4, 8, 16)
POOL_DIM = 512
POOL_GROUP_DIM = 128
HEADS = 8
QK_NOPE = 64
QK_ROPE = 32
QK_HEAD = QK_NOPE + QK_ROPE
V_HEAD = 64
Q_RANK = 768
KV_RANK = 256
GRID_W = 64
ROPE_THETA = 10000.0
RMS_EPS = 1e-6
ATTN_SCALE = 1.0 / math.sqrt(QK_HEAD)
HEAD_PAD = 128
POOL_PAD = 16
PA_POOL, PA_CQ, PA_KV = 0, 768, 1536
PA_KV_W = 384
PA_W = PA_KV + PA_KV_W

ADAM_LR, ADAM_B1, ADAM_B2, ADAM_EPS, ADAM_WD, ADAM_STEP = 0.001, 0.9, 0.999, 1e-08, 0.01, 10

VMEM_LIMIT_BYTES = 56 * 1024 * 1024

NN = ((1,), (0,))
NT = ((1,), (1,))
TN = ((0,), (0,))


def _cparams():
    return pltpu.CompilerParams(vmem_limit_bytes=VMEM_LIMIT_BYTES)


def _dot(a, b, dims):
    return lax.dot_general(a, b, (dims, ((), ())), preferred_element_type=F32)


def _tile(n, cap, mult=8):
    t = (min(cap, n) // mult) * mult
    while t >= mult:
        if n % t == 0:
            return t
        t -= mult
    return n


def _colsum(x):
    return jnp.sum(x, axis=0, keepdims=True)


def _rms(x):
    r = lax.rsqrt(jnp.mean(x * x, axis=-1, keepdims=True) + RMS_EPS)
    return x * r, r


def _rms_bwd(n, r, dn):
    return r * (dn - n * jnp.mean(dn * n, axis=-1, keepdims=True))


def _rowwise(name, fn, t_rows, tm, n_lat, rows, vecs, outs, accs, into=None):
    nt = t_rows // tm
    nlt = n_lat // tm
    n_groups = 2 if nlt < nt else 1

    def grp(i):
        return jnp.where(i >= nlt, 1, 0) if n_groups == 2 else 0

    in_specs = [pl.BlockSpec((tm, w), functools.partial(lambda i, cb: (i, cb), cb=cb)) for (_, w, cb) in rows]
    in_specs += [pl.BlockSpec((1,) + v.shape[1:], lambda i: (grp(i), 0, 0)) for v in vecs]
    out_specs = [pl.BlockSpec((tm, w), lambda i: (i, 0)) for (w, _) in outs]
    out_specs += [pl.BlockSpec((1, 1, w), lambda i: (grp(i), 0, 0)) for w in accs]
    out_shape = [SDS((t_rows, w), dt) for (w, dt) in outs] + [SDS((n_groups, 1, w), F32) for w in accs]
    n_r, n_v, n_o = len(rows), len(vecs), len(outs)
    extra, aliases = [], {}
    if into is not None:
        extra, aliases = [into[0]], {n_r + n_v: 0}
        in_specs.append(pl.BlockSpec(memory_space=pl.ANY))
        out_specs[0] = pl.BlockSpec((tm, outs[0][0]), lambda i: (i, into[1]))
        out_shape[0] = SDS(into[0].shape, into[0].dtype)
    n_in = n_r + n_v + len(extra)

    def body(*refs):
        row_vals = [r[...] for r in refs[:n_r]]
        vec_vals = [v[0] for v in refs[n_r:n_r + n_v]]
        out_refs = refs[n_in:n_in + n_o]
        acc_refs = refs[n_in + n_o:]
        out_vals, acc_vals = fn(row_vals, vec_vals)
        for o_ref, o in zip(out_refs, out_vals):
            o_ref[...] = o.astype(o_ref.dtype)
        if acc_refs:
            i = pl.program_id(0)
            first = (i == 0) | (i == nlt) if n_groups == 2 else i == 0

            @pl.when(first)
            def _():
                for a_ref, a in zip(acc_refs, acc_vals):
                    a_ref[0] = a

            @pl.when(jnp.logical_not(first))
            def _():
                for a_ref, a in zip(acc_refs, acc_vals):
                    a_ref[0] += a

    res = pl.pallas_call(
        body, name=name, grid=(nt,), in_specs=in_specs, out_specs=out_specs, out_shape=out_shape,
        input_output_aliases=aliases, compiler_params=_cparams(),
    )(*[r[0] for r in rows], *vecs, *extra)
    return res[:n_o], res[n_o:]


RESIDENT_BYTES = 12 * 1024 * 1024


def _mm(name, pairs, mode, out_dtype, tm_cap=256, tn_cap=512, bias=None):
    a0, b0 = pairs[0]
    if mode == "nn":
        m, n, dims = a0.shape[0], b0.shape[1], NN
    elif mode == "nt":
        m, n, dims = a0.shape[0], b0.shape[0], NT
    else:
        m, n, dims = a0.shape[1], b0.shape[1], TN
    b_bytes = sum(b.size * b.dtype.itemsize for _, b in pairs)
    tn = n if b_bytes <= RESIDENT_BYTES else _tile(n, tn_cap, 128)
    tm = _tile(m, tm_cap, 128 if mode == "tn" else 16)

    def a_spec(a):
        if mode == "tn":
            return pl.BlockSpec((a.shape[0], tm), lambda i, j: (0, i))
        return pl.BlockSpec((tm, a.shape[1]), lambda i, j: (i, 0))

    def b_spec(b):
        if mode == "nt":
            return pl.BlockSpec((tn, b.shape[1]), lambda i, j: (j, 0))
        return pl.BlockSpec((b.shape[0], tn), lambda i, j: (0, j))

    in_specs, flat = [], []
    for a, b in pairs:
        in_specs += [a_spec(a), b_spec(b)]
        flat += [a, b]
    if bias is not None:
        in_specs.append(pl.BlockSpec((1, tn), lambda i, j: (0, j)))
        flat.append(bias)
    n_pairs = len(pairs)

    def body(*refs):
        acc = None
        for p in range(n_pairs):
            t = _dot(refs[2 * p][...], refs[2 * p + 1][...], dims)
            acc = t if acc is None else acc + t
        if bias is not None:
            acc = acc + refs[2 * n_pairs][...]
        refs[-1][...] = acc.astype(refs[-1].dtype)

    return pl.pallas_call(
        body, name=name, grid=(m // tm, n // tn), in_specs=in_specs,
        out_specs=pl.BlockSpec((tm, tn), lambda i, j: (i, j)),
        out_shape=SDS((m, n), out_dtype), compiler_params=_cparams(),
    )(*flat)


def _mm_resid(name, pairs, s, mg, k, coef, n_lat):
    t_rows, n = pairs[0][0].shape[0], s.shape[1]
    n_pairs = len(pairs)
    tm = _tile(math.gcd(n_lat, t_rows), 256, 16)
    nlt = n_lat // tm
    n_groups = 2 if nlt < t_rows // tm else 1

    def grp(i):
        return jnp.where(i >= nlt, 1, 0) if n_groups == 2 else 0

    def body(*refs):
        s_ref, mg_ref, so_ref, o_ref = refs[2 * n_pairs:]
        o = _dot(refs[0][...], refs[n_pairs][...], NN)
        for p in range(1, n_pairs):
            o = o + _dot(refs[p][...], refs[n_pairs + p][...], NN)
        gate = mg_ref[0, 3 * k + 2:3 * k + 3, :]
        o_ref[...] = o.astype(BF16)
        so_ref[...] = s_ref[...] + (coef * gate) * o

    row = pl.BlockSpec((tm, n), lambda i: (i, 0))
    return pl.pallas_call(
        body, name=name, grid=(t_rows // tm,),
        in_specs=[pl.BlockSpec((tm, a.shape[1]), lambda i: (i, 0)) for a, _ in pairs]
        + [pl.BlockSpec(b.shape, lambda i: (0, 0)) for _, b in pairs]
        + [row, pl.BlockSpec((1, mg.shape[1], n), lambda i: (grp(i), 0, 0))],
        out_specs=[row, row], out_shape=[SDS((t_rows, n), F32), SDS((t_rows, n), BF16)], compiler_params=_cparams(),
    )(*[a for a, _ in pairs], *[b for _, b in pairs], s, mg)


def _dw_pair(name, a1, a2, b):
    kk, m = a1.shape
    n = b.shape[1]
    tm = _tile(m, 256, 128)

    def body(a1_ref, a2_ref, b_ref, o1_ref, o2_ref):
        bb = b_ref[...]
        o1_ref[...] = _dot(a1_ref[...], bb, TN).astype(BF16)
        o2_ref[...] = _dot(a2_ref[...], bb, TN).astype(BF16)

    col = pl.BlockSpec((kk, tm), lambda i: (0, i))
    out = pl.BlockSpec((tm, n), lambda i: (i, 0))
    return pl.pallas_call(
        body, name=name, grid=(m // tm,), in_specs=[col, col, pl.BlockSpec(b.shape, lambda i: (0, 0))],
        out_specs=[out, out], out_shape=[SDS((m, n), BF16)] * 2, compiler_params=_cparams(),
    )(a1, a2, b)


def _groups(t_rows, tm, n_lat):
    nlt = n_lat // tm
    if nlt < t_rows // tm:
        return 2, (lambda i: jnp.where(i >= nlt, 1, 0)), (lambda i: (i == 0) | (i == nlt))
    return 1, (lambda i: 0), (lambda i: i == 0)


def _accumulate(acc_refs, vals, first):
    @pl.when(first)
    def _():
        for r, v in zip(acc_refs, vals):
            r[0] = v

    @pl.when(jnp.logical_not(first))
    def _():
        for r, v in zip(acc_refs, vals):
            r[0] += v


def _adaln_math(s, m, k):
    n, _ = _rms(s)
    return (n * m[9 + k:10 + k]) * (1.0 + m[3 * k + 1:3 * k + 2]) + m[3 * k:3 * k + 1]


def _ffn_up(name, s, mg, k, n_lat, wg_t, wu_t):
    t_rows, f = s.shape[0], wg_t.shape[0]
    tm = _row_tm(t_rows, n_lat)
    _, grp, _ = _groups(t_rows, tm, n_lat)

    def body(s_ref, mg_ref, wg_ref, wu_ref, u_ref, a_ref, b_ref, h_ref):
        uu = _adaln_math(s_ref[...], mg_ref[0], k).astype(BF16)
        u_ref[...] = uu
        a = _dot(uu, wg_ref[...], NT)
        b = _dot(uu, wu_ref[...], NT)
        sg = jax.nn.sigmoid(a)
        act = a * sg
        a_ref[...] = (b * (sg * (1.0 + a * (1.0 - sg)))).astype(BF16)
        b_ref[...] = act.astype(BF16)
        h_ref[...] = (act * b).astype(BF16)

    w_spec = pl.BlockSpec(wg_t.shape, lambda i: (0, 0))
    o_spec = pl.BlockSpec((tm, f), lambda i: (i, 0))
    row = pl.BlockSpec((tm, s.shape[1]), lambda i: (i, 0))
    return pl.pallas_call(
        body, name=name, grid=(t_rows // tm,),
        in_specs=[row, pl.BlockSpec((1,) + mg.shape[1:], lambda i: (grp(i), 0, 0)), w_spec, w_spec],
        out_specs=[row, o_spec, o_spec, o_spec],
        out_shape=[SDS(s.shape, BF16)] + [SDS((t_rows, f), BF16)] * 3, compiler_params=_cparams(),
    )(s, mg, wg_t, wu_t)


def _ffn_dact(name, ds_out, o, mg, k, coef, n_lat, wd, a, b):
    t_rows, f = ds_out.shape[0], wd.shape[0]
    tm = _row_tm(t_rows, n_lat)
    n_groups, grp, first = _groups(t_rows, tm, n_lat)
    d = ds_out.shape[1]

    def body(ds_ref, o_ref, mg_ref, wd_ref, a_ref, b_ref, do_ref, da_ref, db_ref, dg_ref):
        dd = coef * ds_ref[...]
        do = (dd * mg_ref[0, 3 * k + 2:3 * k + 3, :]).astype(BF16)
        do_ref[...] = do
        _accumulate([dg_ref], [_colsum(dd * o_ref[...].astype(F32))], first(pl.program_id(0)))
        dh = _dot(do, wd_ref[...], NT)
        da_ref[...] = (dh * a_ref[...].astype(F32)).astype(BF16)
        db_ref[...] = (dh * b_ref[...].astype(F32)).astype(BF16)

    row = pl.BlockSpec((tm, d), lambda i: (i, 0))
    t_spec = pl.BlockSpec((tm, f), lambda i: (i, 0))
    return pl.pallas_call(
        body, name=name, grid=(t_rows // tm,),
        in_specs=[row, row, pl.BlockSpec((1,) + mg.shape[1:], lambda i: (grp(i), 0, 0)),
                  pl.BlockSpec(wd.shape, lambda i: (0, 0)), t_spec, t_spec],
        out_specs=[row, t_spec, t_spec, pl.BlockSpec((1, 1, d), lambda i: (grp(i), 0, 0))],
        out_shape=[SDS((t_rows, d), BF16), SDS((t_rows, f), BF16), SDS((t_rows, f), BF16), SDS((n_groups, 1, d), F32)],
        compiler_params=_cparams(),
    )(ds_out, o, mg, wd, a, b)


def _du_adaln(name, pairs, s, ds_out, mg, k, n_lat, after, out_rows=None):
    t_rows, d = s.shape
    tm = _row_tm(t_rows, n_lat)
    n_groups, grp, first = _groups(t_rows, tm, n_lat)
    n_pairs = len(pairs)
    nt, n_ds, n_out = t_rows // tm, ds_out.shape[0] // tm, (out_rows or t_rows) // tm

    def body(*refs):
        s_ref, ds_ref, mg_ref, z_ref, out_ref, dsh_ref, dsc_ref, dgn_ref = refs[2 * n_pairs:]
        i = pl.program_id(0)
        d_u = z_ref[...]
        for p in range(n_pairs):
            d_u = d_u + _dot(refs[p][...], refs[n_pairs + p][...], NN)
        m = mg_ref[0]
        gain, scale = m[9 + k:10 + k], m[3 * k + 1:3 * k + 2]
        n, r = _rms(s_ref[...])
        dxn = d_u * (1.0 + scale)
        ds_in = _rms_bwd(n, r, dxn * gain)
        ds_in = ds_in + (ds_ref[...] if n_ds == nt else jnp.where(i < n_ds, ds_ref[...], 0.0))
        if n_out == nt:
            out_ref[...] = ds_in
        else:
            @pl.when(i < n_out)
            def _():
                out_ref[...] = ds_in
        _accumulate([dsh_ref, dsc_ref, dgn_ref], [_colsum(d_u), _colsum(d_u * (n * gain)), _colsum(dxn * n)], first(i))

    row = pl.BlockSpec((tm, d), lambda i: (i, 0))
    acc = pl.BlockSpec((1, 1, d), lambda i: (grp(i), 0, 0))
    res = pl.pallas_call(
        body, name=name, grid=(t_rows // tm,),
        in_specs=[pl.BlockSpec((tm, a.shape[1]), lambda i: (i, 0)) for a, _ in pairs]
        + [pl.BlockSpec(w.shape, lambda i: (0, 0)) for _, w in pairs]
        + [row, pl.BlockSpec((tm, d), lambda i: (jnp.minimum(i, n_ds - 1), 0)),
           pl.BlockSpec((1,) + mg.shape[1:], lambda i: (grp(i), 0, 0)), pl.BlockSpec((1, d), lambda i: (0, 0))],
        out_specs=[pl.BlockSpec((tm, d), lambda i: (jnp.minimum(i, n_out - 1), 0)), acc, acc, acc],
        out_shape=[SDS((n_out * tm, d), F32)] + [SDS((n_groups, 1, d), F32)] * 3, compiler_params=_cparams(),
    )(*[a for a, _ in pairs], *[w for _, w in pairs], s, ds_out, mg, after)
    return res[0], res[1:]


def _adaln_mm(name, s, mg, k, n_lat, w_t):
    rows, d = s.shape
    tm = _row_tm(rows, n_lat)
    _, grp, _ = _groups(rows, tm, n_lat)
    n = w_t.shape[0]

    def body(s_ref, mg_ref, w_ref, u_ref, y_ref):
        uu = _adaln_math(s_ref[...], mg_ref[0], k).astype(BF16)
        u_ref[...] = uu
        y_ref[...] = _dot(uu, w_ref[...], NT)

    row = pl.BlockSpec((tm, d), lambda i: (i, 0))
    return pl.pallas_call(
        body, name=name, grid=(rows // tm,),
        in_specs=[row, pl.BlockSpec((1,) + mg.shape[1:], lambda i: (grp(i), 0, 0)), pl.BlockSpec(w_t.shape, lambda i: (0, 0))],
        out_specs=[row, pl.BlockSpec((tm, n), lambda i: (i, 0))],
        out_shape=[SDS((rows, d), BF16), SDS((rows, n), F32)], compiler_params=_cparams(),
    )(s, mg, w_t)


def _gate_mm(name, ds_out, o, mg, k, coef, n_lat, w):
    t_rows, d = ds_out.shape
    tm = _row_tm(t_rows, n_lat)
    n_groups, grp, first = _groups(t_rows, tm, n_lat)
    n = w.shape[0]

    def body(ds_ref, o_ref, mg_ref, w_ref, do_ref, y_ref, dg_ref):
        dd = coef * ds_ref[...]
        do = (dd * mg_ref[0, 3 * k + 2:3 * k + 3, :]).astype(BF16)
        do_ref[...] = do
        _accumulate([dg_ref], [_colsum(dd * o_ref[...].astype(F32))], first(pl.program_id(0)))
        y_ref[...] = _dot(do, w_ref[...], NT)

    row = pl.BlockSpec((tm, d), lambda i: (i, 0))
    return pl.pallas_call(
        body, name=name, grid=(t_rows // tm,),
        in_specs=[row, row, pl.BlockSpec((1,) + mg.shape[1:], lambda i: (grp(i), 0, 0)), pl.BlockSpec(w.shape, lambda i: (0, 0))],
        out_specs=[row, pl.BlockSpec((tm, n), lambda i: (i, 0)), pl.BlockSpec((1, 1, d), lambda i: (grp(i), 0, 0))],
        out_shape=[SDS((t_rows, d), BF16), SDS((t_rows, n), F32), SDS((n_groups, 1, d), F32)],
        compiler_params=_cparams(),
    )(ds_out, o, mg, w)


def _row_tm(t_rows, n_lat):
    return _tile(math.gcd(t_rows, n_lat), 256, 16)


def _rmsnorm_fwd(name, x, width, colblk, gain, t_rows):
    def fn(rv, vv):
        n, _ = _rms(rv[0])
        return [n * vv[0]], []

    (y,), _ = _rowwise(name, fn, t_rows, _tile(t_rows, 256, 16), t_rows, [(x, width, colblk)],
                       [gain.reshape(1, 1, width)], [(width, BF16)], [])
    return y


def _rmsnorm_bwd(name, x, width, colblk, dy, gain, t_rows, out_dtype=F32, into=None):
    def fn(rv, vv):
        n, r = _rms(rv[0])
        return [_rms_bwd(n, r, rv[1] * vv[0])], [_colsum(rv[1] * n)]

    (dx,), (dgain,) = _rowwise(name, fn, t_rows, _tile(t_rows, 256, 16), t_rows,
                               [(x, width, colblk), (dy, width, 0)], [gain.reshape(1, 1, width)],
                               [(width, out_dtype)], [width], into)
    return dx, dgain


def _final_loss(name, h, target, gain):
    t_rows = h.shape[0]
    inv_d = 1.0 / D_MODEL

    def fn(rv, vv):
        g = vv[0]
        n, r = _rms(rv[0])
        e = n * g - rv[1]
        dy = e * inv_d
        return [_rms_bwd(n, r, dy * g)], [_colsum(e * e), _colsum(dy * n)]

    (dh,), (sq, dgain) = _rowwise(name, fn, t_rows, _tile(t_rows, 256, 16), t_rows,
                                  [(h, D_MODEL, 0), (target, D_MODEL, 0)], [gain.reshape(1, 1, D_MODEL)],
                                  [(D_MODEL, F32)], [D_MODEL, D_MODEL])
    return dh, sq, dgain


def _exact_dot(x, m_ref):
    hi = x.astype(BF16)
    lo = (x - hi.astype(F32)).astype(BF16)
    return _dot(hi, m_ref[...], NN) + _dot(lo, m_ref[...], NN)


def _rope(name, z, width, colblk, cos32, sin32, layout, backward, out_dtype, remap=None, into=None):
    t_rows = cos32.shape[0]
    expand, plain, perm = layout
    w_in = remap.shape[1] if (remap is not None and backward) else width
    w_out = remap.shape[1] if (remap is not None and not backward) else width
    extra = [] if remap is None else [remap.T if backward else remap]
    dest = [] if into is None else [into[0]]

    def body(z_ref, c_ref, s_ref, e_ref, m_ref, p_ref, *rest):
        o_ref = rest[-1]
        zz = z_ref[...]
        if remap is not None and backward:
            zz = _exact_dot(zz, rest[0])
        cos = _exact_dot(c_ref[...], e_ref) + m_ref[...]
        sin = _exact_dot(s_ref[...], e_ref)
        rot = _exact_dot(zz * sin if backward else zz, p_ref)
        if not backward:
            rot = rot * sin
        res = zz * cos + rot
        if remap is not None and not backward:
            res = _dot(res.astype(BF16), rest[0][...], NN)
        o_ref[...] = res.astype(o_ref.dtype)

    tm = _tile(t_rows, 256, 16)
    f_spec = pl.BlockSpec((tm, QK_ROPE), lambda i: (i, 0))
    return pl.pallas_call(
        body, name=name, grid=(t_rows // tm,),
        in_specs=[pl.BlockSpec((tm, w_in), lambda i: (i, colblk)), f_spec, f_spec,
                  pl.BlockSpec((QK_ROPE, width), lambda i: (0, 0)), pl.BlockSpec((1, width), lambda i: (0, 0)),
                  pl.BlockSpec((width, width), lambda i: (0, 0))]
        + [pl.BlockSpec(e.shape, lambda i: (0, 0)) for e in extra] + [pl.BlockSpec(memory_space=pl.ANY)] * len(dest),
        out_specs=pl.BlockSpec((tm, w_out), lambda i: (i, 0 if into is None else into[1])),
        out_shape=SDS((t_rows, w_out), out_dtype) if into is None else SDS(into[0].shape, into[0].dtype),
        input_output_aliases={} if into is None else {6 + len(extra): 0}, compiler_params=_cparams(),
    )(z, cos32, sin32, expand, plain, perm.T if backward else perm, *extra, *dest)


def _window_sum(x, w, transposed):
    n_rows = x.shape[0]
    zeros = jnp.zeros((POOL_PAD, x.shape[1]), F32)
    y = jnp.concatenate([zeros, x, zeros], axis=0)
    total = n_rows + 2 * POOL_PAD
    if transposed:
        y = y + pltpu.roll(y, total - 1, 0)
    else:
        y = y + pltpu.roll(y, 1, 0)
    step = 1
    while 2 * step < w:
        y = pltpu.roll(y, step, 0) + pltpu.roll(y, total - step, 0)
        step *= 2
    return y[POOL_PAD:POOL_PAD + n_rows]


def _window_count(n_rows, w):
    t = lax.broadcasted_iota(jnp.int32, (n_rows, 1), 0)
    lo = jnp.maximum(t - w // 2, 0)
    hi = jnp.minimum(t + (w - w // 2 - 1), n_rows - 1)
    return (hi - lo + 1).astype(F32)


def _pool_fwd(name, proj, n_rows, w_grp, scale):
    def body(x_ref, w_ref, sc_ref, y_ref, p_ref):
        for g, w in enumerate(POOL_WINDOWS):
            cols = slice(g * POOL_GROUP_DIM, (g + 1) * POOL_GROUP_DIM)
            x = x_ref[:, cols]
            p = _window_sum(x, w, False) * (1.0 / _window_count(n_rows, w)) - x
            pb = p.astype(BF16)
            p_ref[:, cols] = pb
            y_ref[:, cols] = (_dot(pb, w_ref[g], NN) * sc_ref[:, cols]).astype(BF16)

    blk = pl.BlockSpec((n_rows, POOL_DIM), lambda i: (0, 0))
    return pl.pallas_call(
        body, name=name, grid=(1,),
        in_specs=[blk, pl.BlockSpec(w_grp.shape, lambda i: (0, 0, 0)), pl.BlockSpec((1, POOL_DIM), lambda i: (0, 0))],
        out_specs=[blk, blk], out_shape=[SDS((n_rows, POOL_DIM), BF16)] * 2, compiler_params=_cparams(),
    )(proj, w_grp, scale)


def _pool_bwd(name, dcat, n_rows, p, w_grp, scale, into):
    def body(dy_ref, p_ref, w_ref, sc_ref, into_ref, dx_ref, dw_ref, dsc_ref):
        for g, w in enumerate(POOL_WINDOWS):
            cols = slice(g * POOL_GROUP_DIM, (g + 1) * POOL_GROUP_DIM)
            dy = dy_ref[:, cols]
            pb = p_ref[:, cols]
            pw = _dot(pb, w_ref[g], NN)
            dsc_ref[:, cols] = _colsum(dy * pw)
            dpw = (dy * sc_ref[:, cols]).astype(BF16)
            dw_ref[g] = _dot(pb, dpw, TN)
            dp = _dot(dpw, w_ref[g], NT)
            dx_ref[:, cols] = (_window_sum(dp * (1.0 / _window_count(n_rows, w)), w, True) - dp).astype(BF16)

    blk = pl.BlockSpec((n_rows, POOL_DIM), lambda i: (0, 0))
    w_spec = pl.BlockSpec(w_grp.shape, lambda i: (0, 0, 0))
    v_spec = pl.BlockSpec((1, POOL_DIM), lambda i: (0, 0))
    return pl.pallas_call(
        body, name=name, grid=(1,), in_specs=[blk, blk, w_spec, v_spec, pl.BlockSpec(memory_space=pl.ANY)],
        out_specs=[blk, w_spec, v_spec],
        out_shape=[SDS(into.shape, into.dtype), SDS(w_grp.shape, F32), SDS((1, POOL_DIM), F32)],
        input_output_aliases={4: 0}, compiler_params=_cparams(),
    )(dcat, p, w_grp, scale, into)


def _head_keys(kv_blk, k_rope):
    lane = lax.broadcasted_iota(jnp.int32, (1, HEAD_PAD), 1)
    return jnp.where(lane < QK_NOPE, kv_blk, k_rope)


def _attn_fwd(name, q, kv, k_rope, n_q):
    n_k = kv.shape[0]
    h = kv.shape[1] // HEAD_PAD
    tq = _tile(n_q, 256, 16)

    def body(q_ref, kv_ref, kr_ref, o_ref, lse_ref):
        kvb = kv_ref[...]
        s = _dot(q_ref[...], _head_keys(kvb, kr_ref[...]), NT) * ATTN_SCALE
        m = jnp.max(s, axis=-1, keepdims=True)
        e = jnp.exp(s - m)
        l = jnp.sum(e, axis=-1, keepdims=True)
        p = (e * (1.0 / l)).astype(BF16)
        lane = lax.broadcasted_iota(jnp.int32, (1, HEAD_PAD), 1)
        o_ref[...] = jnp.where(lane >= QK_NOPE, _dot(p, kvb, NN), 0.0).astype(BF16)
        lse_ref[...] = m + jnp.log(l)

    blk = pl.BlockSpec((tq, HEAD_PAD), lambda hh, i: (i, hh))
    return pl.pallas_call(
        body, name=name, grid=(h, n_q // tq),
        in_specs=[blk, pl.BlockSpec((n_k, HEAD_PAD), lambda hh, i: (0, hh)),
                  pl.BlockSpec((n_k, HEAD_PAD), lambda hh, i: (0, 0))],
        out_specs=[blk, pl.BlockSpec((None, tq, 1), lambda hh, i: (hh, i, 0))],
        out_shape=[SDS((n_q, h * HEAD_PAD), BF16), SDS((h, n_q, 1), F32)], compiler_params=_cparams(),
    )(q, kv, k_rope)


def _attn_bwd(name, q, kv, k_rope, o, lse, dy, dy_col0, n_q):
    n_k = kv.shape[0]
    h = kv.shape[1] // HEAD_PAD
    tq = _tile(n_q, 256, 16)
    n_i = n_q // tq

    def body(q_ref, kv_ref, kr_ref, o_ref, lse_ref, do_ref, dq_ref, dkv_ref, dkr_ref, acc_k, acc_v):
        hh, i = pl.program_id(0), pl.program_id(1)
        qq, kvb = q_ref[...], kv_ref[...]
        kk = _head_keys(kvb, kr_ref[...])
        d_o = do_ref[...]
        dd = d_o.astype(BF16)
        s = _dot(qq, kk, NT) * ATTN_SCALE
        p = jnp.exp(s - lse_ref[...])
        dp = _dot(dd, kvb, NT)
        delta = jnp.sum(d_o * o_ref[...].astype(F32), axis=-1, keepdims=True)
        ds = (p * (dp - delta) * ATTN_SCALE).astype(BF16)
        dq_ref[...] = _dot(ds, kk, NN)
        dk = _dot(ds, qq, TN)
        dv = _dot(p.astype(BF16), dd, TN)

        @pl.when(i == 0)
        def _():
            acc_k[...] = dk
            acc_v[...] = dv

        @pl.when(i > 0)
        def _():
            acc_k[...] += dk
            acc_v[...] += dv

        @pl.when(i == n_i - 1)
        def _():
            lane = lax.broadcasted_iota(jnp.int32, (1, HEAD_PAD), 1)
            dkv_ref[...] = jnp.where(lane < QK_NOPE, acc_k[...], acc_v[...]).astype(BF16)
            rope = jnp.where((lane >= QK_NOPE) & (lane < QK_HEAD), acc_k[...], 0.0)

            @pl.when(hh == 0)
            def _():
                dkr_ref[...] = rope

            @pl.when(hh > 0)
            def _():
                dkr_ref[...] += rope

    blk = pl.BlockSpec((tq, HEAD_PAD), lambda hh, i: (i, hh))
    kv_spec = pl.BlockSpec((n_k, HEAD_PAD), lambda hh, i: (0, hh))
    shared = pl.BlockSpec((n_k, HEAD_PAD), lambda hh, i: (0, 0))
    return pl.pallas_call(
        body, name=name, grid=(h, n_i),
        in_specs=[blk, kv_spec, shared, blk, pl.BlockSpec((None, tq, 1), lambda hh, i: (hh, i, 0)),
                  pl.BlockSpec((tq, HEAD_PAD), lambda hh, i: (i, dy_col0 + hh))],
        out_specs=[blk, kv_spec, shared],
        out_shape=[SDS((n_q, h * HEAD_PAD), F32), SDS((n_k, h * HEAD_PAD), BF16), SDS((n_k, HEAD_PAD), F32)],
        scratch_shapes=[pltpu.VMEM((n_k, HEAD_PAD), F32), pltpu.VMEM((n_k, HEAD_PAD), F32)],
        compiler_params=_cparams(),
    )(q, kv, k_rope, o, lse, dy)


CONV_COLS = 256


def _shift_rows(x, d):
    n_rows = x.shape[0]
    t = lax.broadcasted_iota(jnp.int32, (n_rows, 1), 0)
    if d > 0:
        return jnp.where(t >= d, pltpu.roll(x, d, 0), 0.0)
    return jnp.where(t < n_rows + d, pltpu.roll(x, n_rows + d, 0), 0.0)


def _conv_fwd(name, z3, conv_w):
    n_rows = z3.shape[0]
    nb = D_MODEL // CONV_COLS

    def body(b_ref, c_ref, v_ref, w_ref, y_ref):
        z = c_ref[...] * v_ref[...]
        zc = w_ref[0:1, :] * _shift_rows(z, 1) + w_ref[1:2, :] * z + w_ref[2:3, :] * _shift_rows(z, -1)
        y_ref[...] = (b_ref[...] * zc).astype(BF16)

    def part(k):
        return pl.BlockSpec((n_rows, CONV_COLS), lambda j: (0, k * nb + j))

    return pl.pallas_call(
        body, name=name, grid=(nb,),
        in_specs=[part(0), part(1), part(2), pl.BlockSpec((3, CONV_COLS), lambda j: (0, j))],
        out_specs=pl.BlockSpec((n_rows, CONV_COLS), lambda j: (0, j)),
        out_shape=SDS((n_rows, D_MODEL), BF16), compiler_params=_cparams(),
    )(z3, z3, z3, conv_w)


def _conv_bwd(name, dy, z3, conv_w):
    n_rows = z3.shape[0]
    nb = D_MODEL // CONV_COLS

    def body(dy_ref, b_ref, c_ref, v_ref, w_ref, db_ref, dc_ref, dv_ref, dw_ref):
        c, v, d_y = c_ref[...], v_ref[...], dy_ref[...]
        z = c * v
        z_dn, z_up = _shift_rows(z, 1), _shift_rows(z, -1)
        zc = w_ref[0:1, :] * z_dn + w_ref[1:2, :] * z + w_ref[2:3, :] * z_up
        db_ref[...] = (d_y * zc).astype(BF16)
        dzc = d_y * b_ref[...]
        dz = w_ref[0:1, :] * _shift_rows(dzc, -1) + w_ref[1:2, :] * dzc + w_ref[2:3, :] * _shift_rows(dzc, 1)
        dc_ref[...] = (dz * v).astype(BF16)
        dv_ref[...] = (dz * c).astype(BF16)
        dw_ref[0:1, :] = _colsum(dzc * z_dn)
        dw_ref[1:2, :] = _colsum(dzc * z)
        dw_ref[2:3, :] = _colsum(dzc * z_up)

    def part(k):
        return pl.BlockSpec((n_rows, CONV_COLS), lambda j: (0, k * nb + j))

    col = pl.BlockSpec((n_rows, CONV_COLS), lambda j: (0, j))
    w_spec = pl.BlockSpec((3, CONV_COLS), lambda j: (0, j))
    return pl.pallas_call(
        body, name=name, grid=(nb,), in_specs=[col, part(0), part(1), part(2), w_spec],
        out_specs=[col, col, col, w_spec],
        out_shape=[SDS((n_rows, D_MODEL), BF16)] * 3 + [SDS((3, D_MODEL), F32)], compiler_params=_cparams(),
    )(dy, z3, z3, z3, conv_w)


def _silu_rows(name, x):
    def body(x_ref, s_ref, d_ref):
        xx = x_ref[...]
        sg = jax.nn.sigmoid(xx)
        s_ref[...] = (xx * sg).astype(BF16)
        d_ref[...] = sg * (1.0 + xx * (1.0 - sg))

    return pl.pallas_call(body, name=name, out_shape=[SDS(x.shape, BF16), SDS(x.shape, F32)])(x)


def _sum_rows(name, x, scale=None):
    r, n = x.shape
    tn = _tile(n, 32768, 128)

    def body(*refs):
        acc = jnp.sum(refs[0][...].astype(F32), axis=0, keepdims=True)
        if scale is not None:
            acc = acc * refs[1][...]
        refs[-1][...] = acc

    in_specs = [pl.BlockSpec((r, tn), lambda j: (0, j))]
    args = [x]
    if scale is not None:
        in_specs.append(pl.BlockSpec((1, tn), lambda j: (0, j)))
        args.append(scale)
    return pl.pallas_call(body, name=name, grid=(n // tn,), in_specs=in_specs,
                          out_specs=pl.BlockSpec((1, tn), lambda j: (0, j)), out_shape=SDS((1, n), F32))(*args)


def _me_operand(me):
    return jnp.reshape(me, (1,)).astype(jnp.int32)


def _sum_slots(name, slots, src, me):
    n_slots, r, c = slots.shape
    tr = _tile(r, 432, 16)

    def body(me_ref, own_ref, x_ref, o_ref):
        acc = own_ref[...].astype(F32)
        for sl in range(n_slots):
            acc = acc + x_ref[sl].astype(F32)
        o_ref[...] = acc

    grid_spec = pltpu.PrefetchScalarGridSpec(
        num_scalar_prefetch=1, grid=(r // tr,),
        in_specs=[pl.BlockSpec((None, tr, c), lambda i, me_ref: (me_ref[0], i, 0)),
                  pl.BlockSpec((n_slots, tr, c), lambda i, me_ref: (0, i, 0))],
        out_specs=pl.BlockSpec((tr, c), lambda i, me_ref: (i, 0)))
    return pl.pallas_call(body, name=name, grid_spec=grid_spec, out_shape=SDS((r, c), F32),
                          compiler_params=_cparams())(_me_operand(me), src, slots)


def _adamw(name, w, g, m, v):
    shape = w.shape
    cols = shape[-1]
    rows = w.size // cols
    tr = _tile(rows, 512, 8)
    bc1 = 1.0 - ADAM_B1 ** ADAM_STEP
    bc2 = 1.0 - ADAM_B2 ** ADAM_STEP

    def body(w_ref, g_ref, m_ref, v_ref, d_ref, nm_ref, nv_ref):
        gg = g_ref[...]
        nm = ADAM_B1 * m_ref[...] + (1.0 - ADAM_B1) * gg
        nv = ADAM_B2 * v_ref[...] + (1.0 - ADAM_B2) * (gg * gg)
        nm_ref[...] = nm
        nv_ref[...] = nv
        d_ref[...] = -ADAM_LR * ((nm / bc1) / (jnp.sqrt(nv / bc2) + ADAM_EPS) + ADAM_WD * w_ref[...])

    spec = pl.BlockSpec((tr, cols), lambda i: (i, 0))
    outs = pl.pallas_call(body, name=name, grid=(rows // tr,), in_specs=[spec] * 4, out_specs=[spec] * 3,
                          out_shape=[SDS((rows, cols), F32)] * 3, compiler_params=_cparams())(
        w.reshape(rows, cols), g.reshape(rows, cols), m.reshape(rows, cols), v.reshape(rows, cols))
    return tuple(t.reshape(shape) for t in outs)


def _exchange(name, x, scatter, after=None):
    blk = x.shape[1:] if scatter else x.shape
    extra = [] if after is None else [after]

    def body(x_ref, *rest):
        out_ref, send_sems, recv_sems, local_sem = rest[len(extra):]
        mx, my, mc = lax.axis_index("x"), lax.axis_index("y"), lax.axis_index("c")
        me = 4 * mx + 2 * my + mc
        own = pltpu.make_async_copy(x_ref.at[me] if scatter else x_ref, out_ref.at[me], local_sem)
        own.start()
        copies = []
        for kk in range(1, N_DEV):
            px = jnp.bitwise_xor(mx, (kk >> 2) & 1)
            py = jnp.bitwise_xor(my, (kk >> 1) & 1)
            pc = jnp.bitwise_xor(mc, kk & 1)
            peer = 4 * px + 2 * py + pc
            send = pltpu.make_async_remote_copy(
                src_ref=x_ref.at[peer] if scatter else x_ref, dst_ref=out_ref.at[me],
                send_sem=send_sems.at[kk - 1], recv_sem=recv_sems.at[kk - 1],
                device_id=(px, py, pc), device_id_type=MESH)
            send.start()
            arrival = pltpu.make_async_remote_copy(
                src_ref=x_ref.at[peer] if scatter else x_ref, dst_ref=out_ref.at[peer],
                send_sem=send_sems.at[kk - 1], recv_sem=recv_sems.at[kk - 1],
                device_id=(px, py, pc), device_id_type=MESH)
            copies.append((send, arrival))
        for send, arrival in copies:
            arrival.wait_recv()
            send.wait_send()
        own.wait()

    return pl.pallas_call(
        body, name=name, out_shape=SDS((N_DEV,) + tuple(blk), x.dtype),
        in_specs=[pl.BlockSpec(memory_space=pl.ANY)] * (1 + len(extra)), out_specs=pl.BlockSpec(memory_space=pl.ANY),
        scratch_shapes=[pltpu.SemaphoreType.DMA((N_DEV - 1,)), pltpu.SemaphoreType.DMA((N_DEV - 1,)),
                        pltpu.SemaphoreType.DMA],
    )(x, *extra)


def _rope_perm(pre, reps, post):
    half = QK_ROPE // 4
    width = reps * (pre + QK_ROPE) + post
    p = np.zeros((width, width), np.float32)
    for rep in range(reps):
        s0 = rep * (pre + QK_ROPE) + pre
        for base in (s0, s0 + 2 * half):
            for i in range(half):
                p[base + half + i, base + i] = -1.0
                p[base + i, base + half + i] = 1.0
    return p


def _rope_layout(pre, reps, post):
    width = reps * (pre + QK_ROPE) + post
    expand = np.zeros((QK_ROPE, width), np.float32)
    plain = np.ones((1, width), np.float32)
    for rep in range(reps):
        s0 = rep * (pre + QK_ROPE) + pre
        expand[np.arange(QK_ROPE), s0 + np.arange(QK_ROPE)] = 1.0
        plain[0, s0:s0 + QK_ROPE] = 0.0
    return jnp.asarray(expand, BF16), jnp.asarray(plain, F32), jnp.asarray(_rope_perm(pre, reps, post), BF16)


def _head_spread():
    spread = np.zeros((HEADS * QK_HEAD, HEADS * HEAD_PAD), np.float32)
    for hh in range(HEADS):
        spread[hh * QK_HEAD + np.arange(QK_HEAD), hh * HEAD_PAD + np.arange(QK_HEAD)] = 1.0
    return jnp.asarray(spread, BF16)


def _rope_factors(n_lat, t_rows):
    half = QK_ROPE // 4
    pos = jnp.arange(n_lat)
    freqs = jnp.power(ROPE_THETA, -jnp.arange(0, 2 * half, 2, dtype=F32) / (2 * half))
    ang_r = (pos // GRID_W).astype(F32)[:, None] * freqs
    ang_c = (pos % GRID_W).astype(F32)[:, None] * freqs
    ang = jnp.concatenate([ang_r, ang_r, ang_c, ang_c], axis=-1)
    rest = t_rows - n_lat
    return (jnp.concatenate([jnp.cos(ang), jnp.ones((rest, QK_ROPE), F32)], axis=0),
            jnp.concatenate([jnp.sin(ang), jnp.zeros((rest, QK_ROPE), F32)], axis=0))


def _ffn_half_fwd(tag, s, mg, k, feed, i, coef, n_lat):
    wg_t, wu_t = feed.weights(f"{tag}_up", [f"gate_t{i}", f"up_t{i}"], s)
    u, a, b, hid = _ffn_up(f"{tag}_up", s, mg, k, n_lat, wg_t, wu_t)
    (wd,) = feed.weights(f"{tag}_down", [f"down{i}"], hid)
    s_out, o = _mm_resid(f"{tag}_down", [(hid, wd)], s, mg, k, coef, n_lat)
    return s_out, (s, u, a, b, hid, o, wg_t, wu_t, wd)


def _ffn_half_bwd(tag, ds_out, saved, mg, k, feed, i, coef, n_lat, out_rows=None):
    s, u, a, b, hid, o, wg_t, wu_t, wd = saved
    do, da, db, dgate = _ffn_dact(f"{tag}_dact", ds_out, o, mg, k, coef, n_lat, wd, a, b)
    dwd = _mm(f"{tag}_dwd", [(hid, do)], "tn", BF16)
    dwg_t, dwu_t = _dw_pair(f"{tag}_dwgu", da, db, u)
    token = feed.grads(tag, {f"down{i}": dwd, f"gate_t{i}": dwg_t, f"up_t{i}": dwu_t})
    ds_in, (dshift, dscale, dgain) = _du_adaln(f"{tag}_du", [(da, wg_t), (db, wu_t)], s, ds_out, mg, k, n_lat,
                                               _after(token), out_rows)
    return ds_in, dict(shift=dshift, scale=dscale, gate=dgate, gain=dgain)


def _after(token):
    return jnp.zeros((1, D_MODEL), F32) + token


def _mod_grad(parts, n_groups):
    rows = []
    zero = jnp.zeros((n_groups, 1, D_MODEL), F32)
    for k in range(3):
        for nm in ("shift", "scale", "gate"):
            t = parts[k].get(nm, zero)
            if t.shape[0] < n_groups:
                t = jnp.concatenate([t, jnp.zeros((n_groups - t.shape[0], 1, D_MODEL), F32)], axis=0)
            rows.append(t)
    return jnp.concatenate(rows, axis=1).reshape(n_groups, N_MOD * D_MODEL)


def _local_step(x, ctx, target, mod_h, mod_g, norm_g, feed, pool_w, pool_scale, q_norm_g, kv_norm_g, conv_w,
                final_norm_g):
    n_lat, n_ctx = x.shape[0], ctx.shape[0]
    t_all = n_lat + n_ctx
    mg0 = jnp.stack([jnp.concatenate([mod_h[0], norm_g[0]], axis=0), jnp.concatenate([mod_g, norm_g[0]], axis=0)])
    mg1 = jnp.concatenate([mod_h[1], norm_g[1]], axis=0)[None]

    s0 = jnp.concatenate([x, ctx], axis=0)
    s1, sv_f00 = _ffn_half_fwd("l0f0", s0, mg0, 0, feed, 0, 0.5, n_lat)

    (w_in,) = feed.weights("l0m_in", ["in_t"], s1)
    kv_rows = KV_RANK + QK_ROPE
    w_in_t = jnp.concatenate([
        w_in[:POOL_DIM], jnp.zeros((PA_CQ - POOL_DIM, D_MODEL), BF16), w_in[POOL_DIM:POOL_DIM + Q_RANK],
        w_in[POOL_DIM + Q_RANK:], jnp.zeros((PA_KV_W - kv_rows, D_MODEL), BF16)], axis=0)
    ua, proj = _adaln_mm("l0m_proj", s1, mg0, 1, n_lat, w_in_t)
    w_uq, w_ukv_t, w_ab_out = feed.weights("l0m_rest", ["uq", "ukv_t", "ab_out"], proj)
    pool_y, pool_p = _pool_fwd("l0m_pool", proj, n_lat, pool_w.astype(BF16), pool_scale)
    nq = _rmsnorm_fwd("l0m_qnorm", proj, Q_RANK, PA_CQ // Q_RANK, q_norm_g, n_lat)
    q_lin = _mm("l0m_q", [(nq, w_uq)], "nn", F32, 512, 768)
    cos32, sin32 = _rope_factors(n_lat, t_all)
    lay_q, lay_k = _rope_layout(QK_NOPE, HEADS, 0), _rope_layout(KV_RANK, 1, PA_KV_W - kv_rows)
    spread = _head_spread()
    q_flat = _rope("l0m_qrope", q_lin, Q_RANK, 0, cos32[:n_lat], sin32[:n_lat], lay_q, False, BF16, spread)
    kvr = _rope("l0m_krope", proj, PA_KV_W, PA_KV // PA_KV_W, cos32, sin32, lay_k, False, F32)
    nkv = _rmsnorm_fwd("l0m_kvnorm", kvr, KV_RANK, 0, kv_norm_g, t_all)
    kv = _mm("l0m_kv", [(nkv, w_ukv_t)], "nt", BF16, 768, 512)
    k_rope = jnp.pad(kvr[:, KV_RANK:KV_RANK + QK_ROPE].astype(BF16), ((0, 0), (QK_NOPE, HEAD_PAD - QK_HEAD)))
    o_flat, lse = _attn_fwd("l0m_attn", q_flat, kv, k_rope, n_lat)
    w_o_pad = jnp.pad(w_ab_out[POOL_DIM:].reshape(HEADS, V_HEAD, D_MODEL),
                      ((0, 0), (HEAD_PAD - V_HEAD, 0), (0, 0))).reshape(HEADS * HEAD_PAD, D_MODEL)
    w_o_pool = w_ab_out[:POOL_DIM]
    h2, mix_o = _mm_resid("l0m_out", [(pool_y, w_o_pool), (o_flat, w_o_pad)], s1, mg0[:1], 1, 1.0, n_lat)

    h3, sv_f01 = _ffn_half_fwd("l0f1", h2, mg0[:1], 2, feed, 1, 0.5, n_lat)

    h4, sv_f10 = _ffn_half_fwd("l1f0", h3, mg1, 0, feed, 2, 0.5, n_lat)
    w_cin_t, w_c_out = feed.weights("l1m", ["cin_t", "c_out"], h4)
    uc, z3 = _adaln_mm("l1m_in", h4, mg1, 1, n_lat, w_cin_t)
    yc = _conv_fwd("l1m_conv", z3, conv_w)
    h5, conv_o = _mm_resid("l1m_out", [(yc, w_c_out)], h4, mg1, 1, 1.0, n_lat)
    h6, sv_f11 = _ffn_half_fwd("l1f1", h5, mg1, 2, feed, 3, 0.5, n_lat)

    dh6, sq_cols, d_final_g = _final_loss("loss_head", h6, target, final_norm_g)
    g = {}
    dh5, g["f11"] = _ffn_half_bwd("l1f1", dh6, sv_f11, mg1, 2, feed, 3, 0.5, n_lat)

    do_c, dyc, dgate_c = _gate_mm("l1m_dy", dh5, conv_o, mg1, 1, 1.0, n_lat, w_c_out)
    d_c_out = _mm("l1m_dwout", [(yc, do_c)], "tn", BF16)
    db_, dc_, dv_, d_conv_w = _conv_bwd("l1m_dconv", dyc, z3, conv_w)
    dz3 = jnp.concatenate([db_, dc_, dv_], axis=-1)
    d_cin_t = _mm("l1m_dwin", [(dz3, uc)], "tn", BF16)
    token = feed.grads("l1m", {"c_out": d_c_out, "cin_t": d_cin_t})
    dh4, (dsh_c, dsc_c, dgn_c) = _du_adaln("l1m_du", [(dz3, w_cin_t)], h4, dh5, mg1, 1, n_lat, _after(token))
    dh3, g["f10"] = _ffn_half_bwd("l1f0", dh4, sv_f10, mg1, 0, feed, 2, 0.5, n_lat)

    dh2, g["f01"] = _ffn_half_bwd("l0f1", dh3, sv_f01, mg0[:1], 2, feed, 1, 0.5, n_lat)

    w_back = jnp.concatenate([w_o_pool, w_o_pad], axis=0)
    do_a, dcat, dgate_a = _gate_mm("l0m_dcat", dh2, mix_o, mg0[:1], 1, 1.0, n_lat, w_back)
    d_o_pad = _mm("l0m_dwout_a", [(o_flat, do_a)], "tn", BF16)
    d_ab_out = jnp.concatenate([
        _mm("l0m_dwout_p", [(pool_y, do_a)], "tn", BF16),
        d_o_pad.reshape(HEADS, HEAD_PAD, D_MODEL)[:, HEAD_PAD - V_HEAD:].reshape(HEADS * V_HEAD, D_MODEL)], axis=0)
    dproj = jnp.zeros((t_all, PA_W), BF16)
    dproj, d_pool_w, d_pool_scale = _pool_bwd("l0m_dpool", dcat, n_lat, pool_p, pool_w.astype(BF16), pool_scale, dproj)
    dq_flat, dkv, dk_rope = _attn_bwd("l0m_dattn", q_flat, kv, k_rope, o_flat, lse, dcat, POOL_DIM // HEAD_PAD, n_lat)
    dq_lin = _rope("l0m_dqrope", dq_flat, Q_RANK, 0, cos32[:n_lat], sin32[:n_lat], lay_q, True, BF16, spread)
    d_uq = _mm("l0m_dwuq", [(nq, dq_lin)], "tn", BF16, 768, 768)
    dnq = _mm("l0m_dnq", [(dq_lin, w_uq)], "nt", F32, 512, 768)
    dproj, d_q_norm_g = _rmsnorm_bwd("l0m_dqnorm", proj, Q_RANK, PA_CQ // Q_RANK, dnq, q_norm_g, n_lat, BF16,
                                     (dproj, PA_CQ // Q_RANK))
    dnkv = _mm("l0m_dnkv", [(dkv, w_ukv_t)], "nn", F32, 768, 256)
    d_ukv_t = _mm("l0m_dwukv", [(dkv, nkv)], "tn", BF16, 512, 256)
    dckv, d_kv_norm_g = _rmsnorm_bwd("l0m_dkvnorm", kvr, KV_RANK, 0, dnkv, kv_norm_g, t_all)
    dkvr = jnp.concatenate([dckv, dk_rope[:, QK_NOPE:QK_HEAD],
                            jnp.zeros((t_all, PA_KV_W - KV_RANK - QK_ROPE), F32)], axis=-1)
    dproj = _rope("l0m_dkrope", dkvr, PA_KV_W, 0, cos32, sin32, lay_k, True, BF16, None, (dproj, PA_KV // PA_KV_W))
    d_in_pad = _mm("l0m_dwin", [(dproj, ua)], "tn", BF16, 640, 512)
    d_in_t = jnp.concatenate([d_in_pad[:POOL_DIM], d_in_pad[PA_CQ:PA_CQ + Q_RANK],
                              d_in_pad[PA_KV:PA_KV + kv_rows]], axis=0)
    token = feed.grads("l0m", {"ab_out": d_ab_out, "uq": d_uq, "ukv_t": d_ukv_t, "in_t": d_in_t})
    ds1, (dsh_a, dsc_a, dgn_a) = _du_adaln("l0m_du", [(dproj, w_in_t)], s1, dh2, mg0, 1, n_lat, _after(token))
    grad_x, g["f00"] = _ffn_half_bwd("l0f0", ds1, sv_f00, mg0, 0, feed, 0, 0.5, n_lat, out_rows=n_lat)

    dmod0 = _mod_grad([g["f00"], dict(shift=dsh_a, scale=dsc_a, gate=dgate_a), g["f01"]], 2)
    dmod1 = _mod_grad([g["f10"], dict(shift=dsh_c, scale=dsc_c, gate=dgate_c), g["f11"]], 1)
    d_norm_g = jnp.stack([
        jnp.concatenate([jnp.sum(g["f00"]["gain"], axis=0), jnp.sum(dgn_a, axis=0), g["f01"]["gain"][0]], axis=0),
        jnp.concatenate([g["f10"]["gain"][0], dgn_c[0], g["f11"]["gain"][0]], axis=0)])
    grads = dict(
        pool_w=d_pool_w, pool_scale=d_pool_scale, q_norm_g=d_q_norm_g[0], kv_norm_g=d_kv_norm_g[0],
        conv_w=d_conv_w, final_norm_g=d_final_g[0], norm_g=d_norm_g,
        mod_h=jnp.stack([dmod0[0], dmod1[0]]), mod_g=dmod0[1])
    return sq_cols, grad_x, grads


HBM_SPEC = pl.BlockSpec(memory_space=pltpu.HBM)
SEM_SPEC = pl.BlockSpec(memory_space=pltpu.SEMAPHORE)
ANY_SPEC = pl.BlockSpec(memory_space=pl.ANY)
SIDE_EFFECT = pltpu.SideEffectType.DATAFLOW_SIDE_EFFECTING
N_PEERS = N_DEV - 1


def _mesh_place():
    mx, my, mc = lax.axis_index("x"), lax.axis_index("y"), lax.axis_index("c")
    return mx, my, mc, 4 * mx + 2 * my + mc


def _peer(place, kk):
    mx, my, mc, _ = place
    px = jnp.bitwise_xor(mx, (kk >> 2) & 1)
    py = jnp.bitwise_xor(my, (kk >> 1) & 1)
    pc = jnp.bitwise_xor(mc, kk & 1)
    return (px, py, pc), 4 * px + 2 * py + pc


def _hbm(a):
    return pltpu.with_memory_space_constraint(a, pltpu.HBM)


def _landing(block, me):
    zone = lax.empty((N_DEV,) + block.shape, block.dtype)
    return lax.dynamic_update_slice(zone, block[None], (me,) + (0,) * block.ndim)


ALL_PEERS = tuple(range(1, N_DEV))
SIBLING = 1
CHIP_PEERS = (2, 4, 6)
RELAYED = (3, 5, 7)


def _exchange_start(name, srcs, lands, scatter, after, peers=ALL_PEERS):
    n = len(srcs)
    extra = [] if after is None else [after]

    def body(*refs):
        src, land = refs[:n], refs[n:2 * n]
        send_sems, recv_sems, token = refs[2 * n + len(extra)], refs[2 * n + len(extra) + 1], refs[-1]
        place = _mesh_place()
        for a in range(n):
            for kk in peers:
                dev, peer = _peer(place, kk)
                pltpu.make_async_remote_copy(
                    src_ref=src[a].at[peer] if scatter else src[a],
                    dst_ref=land[a].at[kk - 1] if scatter else land[a].at[place[3]],
                    send_sem=send_sems.at[a * N_PEERS + kk - 1], recv_sem=recv_sems.at[a * N_PEERS + kk - 1],
                    device_id=dev, device_id_type=MESH).start()
        token[...] = jnp.zeros_like(token)

    thru = [pltpu.HBM(t.shape, t.dtype) for t in (*srcs, *lands)]
    res = pl.pallas_call(
        body, name=name,
        out_shape=(pltpu.SemaphoreType.DMA((n * N_PEERS,)), pltpu.SemaphoreType.DMA((n * N_PEERS,)), *thru,
                   SDS((8, 128), F32)),
        in_specs=[HBM_SPEC] * (2 * n) + [ANY_SPEC] * len(extra),
        out_specs=(SEM_SPEC, SEM_SPEC, *([HBM_SPEC] * (2 * n)), pl.BlockSpec(memory_space=pltpu.VMEM)),
        input_output_aliases={i: 2 + i for i in range(2 * n)},
        compiler_params=pltpu.CompilerParams(has_side_effects=SIDE_EFFECT),
    )(*[_hbm(s) for s in srcs], *[_hbm(t) for t in lands], *extra)
    return res[0], res[1], list(res[2:2 + n]), list(res[2 + n:2 + 2 * n]), res[-1]


def _exchange_wait(name, send_sems, recv_sems, srcs, lands, places, scatter, after):
    n = len(srcs)

    def body(*refs):
        src, land = refs[:n], refs[n:2 * n]
        send, recv = refs[2 * n], refs[2 * n + 1]
        place = _mesh_place()
        for a in range(n):
            for kk in range(1, N_DEV):
                dev, peer = _peer(place, kk)
                cp = pltpu.make_async_remote_copy(
                    src_ref=src[a].at[peer] if scatter else src[a],
                    dst_ref=land[a].at[kk - 1] if scatter else land[a].at[peer],
                    send_sem=send.at[places[a] * N_PEERS + kk - 1], recv_sem=recv.at[places[a] * N_PEERS + kk - 1],
                    device_id=dev, device_id_type=MESH)
                cp.wait_send()
                cp.wait_recv()

    thru = [pltpu.HBM(t.shape, t.dtype) for t in (*srcs, *lands)]
    res = pl.pallas_call(
        body, name=name, out_shape=tuple(thru),
        in_specs=[HBM_SPEC] * (2 * n) + [SEM_SPEC, SEM_SPEC] + [ANY_SPEC] * len(after),
        out_specs=tuple([HBM_SPEC] * (2 * n)), input_output_aliases={i: i for i in range(2 * n)},
        compiler_params=pltpu.CompilerParams(has_side_effects=SIDE_EFFECT),
    )(*srcs, *lands, send_sems, recv_sems, *after)
    return list(res[:n]), list(res[n:])


def _gather_relay(name, send1, recv1, lands, places, after):
    n = len(lands)

    def body(*refs):
        land, s1, r1 = refs[:n], refs[n], refs[n + 1]
        s2, r2 = refs[n + 3], refs[n + 4]
        place = _mesh_place()
        sibling = _peer(place, SIBLING)[0]
        for a in range(n):
            for j, kk in enumerate(CHIP_PEERS):
                dev, origin = _peer(place, kk)
                block = land[a].at[origin]
                pltpu.make_async_remote_copy(
                    src_ref=block, dst_ref=block, send_sem=s1.at[places[a] * N_PEERS + kk - 1],
                    recv_sem=r1.at[places[a] * N_PEERS + kk - 1], device_id=dev, device_id_type=MESH).wait_recv()
                pltpu.make_async_remote_copy(
                    src_ref=block, dst_ref=block, send_sem=s2.at[a * 3 + j], recv_sem=r2.at[a * 3 + j],
                    device_id=sibling, device_id_type=MESH).start()

    res = pl.pallas_call(
        body, name=name,
        out_shape=(pltpu.SemaphoreType.DMA((3 * n,)), pltpu.SemaphoreType.DMA((3 * n,)),
                   *[pltpu.HBM(t.shape, t.dtype) for t in lands]),
        in_specs=[HBM_SPEC] * n + [SEM_SPEC, SEM_SPEC, ANY_SPEC],
        out_specs=(SEM_SPEC, SEM_SPEC, *([HBM_SPEC] * n)),
        input_output_aliases={i: 2 + i for i in range(n)},
        compiler_params=pltpu.CompilerParams(has_side_effects=SIDE_EFFECT),
    )(*lands, send1, recv1, after)
    return res[0], res[1], list(res[2:])


def _gather_wait(name, send1, recv1, send2, recv2, srcs, lands, places, after):
    n = len(lands)

    def body(*refs):
        src, land = refs[:n], refs[n:2 * n]
        s1, r1, s2, r2 = refs[2 * n:2 * n + 4]
        place = _mesh_place()
        for a in range(n):
            for kk in (SIBLING,) + CHIP_PEERS:
                dev, origin = _peer(place, kk)
                first = pltpu.make_async_remote_copy(
                    src_ref=src[a], dst_ref=land[a].at[origin], send_sem=s1.at[places[a] * N_PEERS + kk - 1],
                    recv_sem=r1.at[places[a] * N_PEERS + kk - 1], device_id=dev, device_id_type=MESH)
                first.wait_send()
                if kk == SIBLING:
                    first.wait_recv()
            for j, kk in enumerate(CHIP_PEERS):
                dev, origin = _peer(place, kk + 1)
                relay = pltpu.make_async_remote_copy(
                    src_ref=src[a], dst_ref=land[a].at[origin], send_sem=s2.at[a * 3 + j], recv_sem=r2.at[a * 3 + j],
                    device_id=dev, device_id_type=MESH)
                relay.wait_send()
                relay.wait_recv()

    arrays = (*srcs, *lands)
    res = pl.pallas_call(
        body, name=name, out_shape=tuple(pltpu.HBM(t.shape, t.dtype) for t in arrays),
        in_specs=[HBM_SPEC] * (2 * n) + [SEM_SPEC] * 4 + [ANY_SPEC], out_specs=tuple([HBM_SPEC] * (2 * n)),
        input_output_aliases={i: i for i in range(2 * n)},
        compiler_params=pltpu.CompilerParams(has_side_effects=SIDE_EFFECT),
    )(*arrays, send1, recv1, send2, recv2, after)
    return list(res[n:])


class _Feed:
    def __init__(self, shards, groups, me):
        self.shards, self.groups, self.me, self.pos = shards, groups, me, 0
        self.sems, self.srcs, self.lands = {}, {}, {}
        self.relays = {}
        self.pending = []

    def start(self, tag, names, after):
        srcs = [self.shards[nm] for nm in names]
        lands = [_landing(s, self.me) for s in srcs]
        send, recv, srcs, lands, self.token = _exchange_start(
            f"gather_start_{tag}", srcs, lands, False, after, (SIBLING,) + CHIP_PEERS)
        for i, nm in enumerate(names):
            self.sems[nm], self.srcs[nm], self.lands[nm] = (send, recv, i), srcs[i], lands[i]
        return self.token

    def _relay(self, gi, after):
        names = self.groups[gi]
        if gi not in self.relays:
            send, recv, _ = self.sems[names[0]]
            places = [self.sems[nm][2] for nm in names]
            send2, recv2, lands = _gather_relay(f"gather_relay_{gi}", send, recv, [self.lands[nm] for nm in names],
                                                places, after)
            for nm, t in zip(names, lands):
                self.lands[nm] = t
            self.relays[gi] = (send2, recv2)
            after = lands[0]
        return after

    def weights(self, tag, names, after):
        gi = self.pos
        assert names == self.groups[gi], (names, self.groups[gi])
        if gi == 0:
            after = self.token
        self._relay(gi, after)
        if 1 <= gi < len(self.groups) - 1:
            after = self._relay(gi + 1, after)
        send2, recv2 = self.relays[gi]
        send, recv, _ = self.sems[names[0]]
        got = _gather_wait(f"gather_wait_{tag}", send, recv, send2, recv2, [self.srcs[nm] for nm in names],
                           [self.lands[nm] for nm in names], [self.sems[nm][2] for nm in names], after)
        self.pos += 1
        return [t.reshape((N_DEV * t.shape[1],) + t.shape[2:]) for t in got]

    def grads(self, tag, full):
        names = list(full)
        srcs = [full[nm].reshape((N_DEV, full[nm].shape[0] // N_DEV) + full[nm].shape[1:]) for nm in names]
        lands = [lax.empty((N_PEERS,) + s.shape[1:], s.dtype) for s in srcs]
        send, recv, srcs, lands, token = _exchange_start(f"scatter_start_{tag}", srcs, lands, True, None)
        self.pending.append((tag, names, send, recv, srcs, lands))
        return token[0, 0]

    def collect(self, tags, after, keep_slots=()):
        out = {}
        for tag, names, send, recv, srcs, lands in self.pending:
            if tag not in tags:
                continue
            srcs, got = _exchange_wait(f"scatter_wait_{tag}", send, recv, srcs, lands, list(range(len(names))), True,
                                       after)
            for nm, slots, src in zip(names, got, srcs):
                out[nm] = ((slots, src) if nm.startswith(tuple(keep_slots))
                           else _sum_slots(f"reduce_{nm}", slots, src, self.me))
        return out


def _adamw_math(w, gg, m, v):
    nm = ADAM_B1 * m + (1.0 - ADAM_B1) * gg
    nv = ADAM_B2 * v + (1.0 - ADAM_B2) * (gg * gg)
    bc1 = 1.0 - ADAM_B1 ** ADAM_STEP
    bc2 = 1.0 - ADAM_B2 ** ADAM_STEP
    return -ADAM_LR * ((nm / bc1) / (jnp.sqrt(nv / bc2) + ADAM_EPS) + ADAM_WD * w), nm, nv


def _adamw_part(name, i, w, scattered, me, m, v, prev):
    n_parts, rows, cols = w.shape
    tr = _tile(rows, 256, 16)
    if prev is None:
        prev = tuple(lax.empty(w.shape, F32) for _ in range(4))

    slots, src = scattered

    def body(me_ref, w_ref, g_ref, own_ref, m_ref, v_ref, *rest):
        go_ref, d_ref, nm_ref, nv_ref = rest[4:]
        gg = own_ref[...].astype(F32)
        for sl in range(N_PEERS):
            gg = gg + g_ref[sl].astype(F32)
        d, nm, nv = _adamw_math(w_ref[...], gg, m_ref[...], v_ref[...])
        go_ref[...] = gg
        d_ref[...] = d
        nm_ref[...] = nm
        nv_ref[...] = nv

    part = pl.BlockSpec((None, tr, cols), lambda r, me_ref: (i, r, 0))
    grid_spec = pltpu.PrefetchScalarGridSpec(
        num_scalar_prefetch=1, grid=(rows // tr,),
        in_specs=[part, pl.BlockSpec((N_PEERS, tr, cols), lambda r, me_ref: (0, r, 0)),
                  pl.BlockSpec((None, tr, cols), lambda r, me_ref: (me_ref[0], r, 0)), part, part] + [ANY_SPEC] * 4,
        out_specs=[part] * 4)
    return pl.pallas_call(
        body, name=name, grid_spec=grid_spec, out_shape=[SDS(w.shape, F32)] * 4,
        input_output_aliases={6 + k: k for k in range(4)}, compiler_params=_cparams(),
    )(_me_operand(me), w, slots, src, m, v, *prev)


WEIGHT_NAMES = ("c_ctx", "norm_g", "w_mod", "b_mod", "ffn_w_gate", "ffn_w_up", "ffn_w_down", "ab_w_in", "pool_w",
                "pool_scale", "q_norm_g", "w_uq", "kv_norm_g", "w_ukv", "ab_w_out", "conv_w_in", "conv_w",
                "conv_w_out", "final_norm_g")


def kernel(x, c, ctx, c_ctx, norm_g, w_mod, b_mod, ffn_w_gate, ffn_w_up, ffn_w_down, ab_w_in, pool_w, pool_scale, q_norm_g, w_uq, kv_norm_g, w_ukv, ab_w_out, conv_w_in, conv_w, conv_w_out, final_norm_g, loss_target, m_c_ctx, m_norm_g, m_w_mod, m_b_mod, m_ffn_w_gate, m_ffn_w_up, m_ffn_w_down, m_ab_w_in, m_pool_w, m_pool_scale, m_q_norm_g, m_w_uq, m_kv_norm_g, m_w_ukv, m_ab_w_out, m_conv_w_in, m_conv_w, m_conv_w_out, m_final_norm_g, v_c_ctx, v_norm_g, v_w_mod, v_b_mod, v_ffn_w_gate, v_ffn_w_up, v_ffn_w_down, v_ab_w_in, v_pool_w, v_pool_scale, v_q_norm_g, v_w_uq, v_kv_norm_g, v_w_ukv, v_ab_w_out, v_conv_w_in, v_conv_w, v_conv_w_out, v_final_norm_g):
    weights = (c_ctx, norm_g, w_mod, b_mod, ffn_w_gate, ffn_w_up, ffn_w_down, ab_w_in, pool_w, pool_scale, q_norm_g,
               w_uq, kv_norm_g, w_ukv, ab_w_out, conv_w_in, conv_w, conv_w_out, final_norm_g)
    moms = (m_c_ctx, m_norm_g, m_w_mod, m_b_mod, m_ffn_w_gate, m_ffn_w_up, m_ffn_w_down, m_ab_w_in, m_pool_w,
            m_pool_scale, m_q_norm_g, m_w_uq, m_kv_norm_g, m_w_ukv, m_ab_w_out, m_conv_w_in, m_conv_w, m_conv_w_out,
            m_final_norm_g)
    vels = (v_c_ctx, v_norm_g, v_w_mod, v_b_mod, v_ffn_w_gate, v_ffn_w_up, v_ffn_w_down, v_ab_w_in, v_pool_w,
            v_pool_scale, v_q_norm_g, v_w_uq, v_kv_norm_g, v_w_ukv, v_ab_w_out, v_conv_w_in, v_conv_w, v_conv_w_out,
            v_final_norm_g)
    me = 4 * lax.axis_index("x") + 2 * lax.axis_index("y") + lax.axis_index("c")
    n_lat, n_ctx = x.shape[1], ctx.shape[1]
    d = D_MODEL
    mod_cols = w_mod.shape[-1]
    ng_sh, cw_sh = norm_g.shape[-1], conv_w.shape[-1]

    def ffn_shards(i):
        return {f"gate_t{i}": ffn_w_gate[i // 2, i % 2].T, f"up_t{i}": ffn_w_up[i // 2, i % 2].T,
                f"down{i}": ffn_w_down[i // 2, i % 2]}

    local = {**ffn_shards(0), "in_t": ab_w_in[0].T, "uq": w_uq[0], "ukv_t": w_ukv[0].T, "ab_out": ab_w_out[0],
             **ffn_shards(1), **ffn_shards(2), "cin_t": conv_w_in[0].T, "c_out": conv_w_out[0], **ffn_shards(3)}
    ffn_groups = [[[f"gate_t{i}", f"up_t{i}"], [f"down{i}"]] for i in range(4)]
    groups = [*ffn_groups[0], ["in_t"], ["uq", "ukv_t", "ab_out"], *ffn_groups[1], *ffn_groups[2],
              ["cin_t", "c_out"], *ffn_groups[3]]
    feed = _Feed({nm: a.astype(BF16) for nm, a in local.items()}, groups, me)

    small = jnp.concatenate([c.reshape(-1), norm_g.reshape(-1), conv_w.reshape(-1)])
    small_n = -(-small.shape[0] // 1024) * 1024
    small = jnp.pad(small, (0, small_n - small.shape[0])).reshape(small_n // 128, 128)
    small_all = _exchange("gather_small", small, False).reshape(N_DEV, small_n)
    c_all = small_all[:, :d]
    o1 = d + 6 * ng_sh
    norm_g_full = small_all[:, d:o1].reshape(N_DEV, 2, 3, ng_sh).transpose(1, 2, 0, 3).reshape(2, 3, d)
    conv_w_full = small_all[:, o1:o1 + 3 * cw_sh].reshape(N_DEV, 3, cw_sh).transpose(1, 0, 2).reshape(3, d)

    cond = jnp.concatenate([c_all, jnp.broadcast_to(c_ctx[None, :], (N_DEV, d))], axis=0)
    sil, dsil = _silu_rows("mod_silu", cond)
    w_mod_b = w_mod.astype(BF16)
    b_sh = lax.dynamic_slice(b_mod, (0, me * mod_cols), (2, mod_cols))
    m_part = jnp.stack([_mm(f"mod_fwd{l}", [(sil, w_mod_b[l])], "nn", F32, 16, 384, bias=b_sh[l:l + 1])
                        for l in range(2)], axis=1)
    m_all = _exchange("gather_mod", m_part.reshape(-1, 128), False).reshape(N_DEV, 2 * N_DEV, 2, mod_cols)
    m_mine = lax.dynamic_index_in_dim(m_all, me, axis=1, keepdims=False)
    mod_h = m_mine.transpose(1, 0, 2).reshape(2, N_MOD, d)
    mod_g = m_all[:, N_DEV, 0, :].reshape(N_MOD, d)

    first = feed.start("first", [nm for grp in groups[:3] for nm in grp], m_all)
    feed.start("rest", [nm for grp in groups[3:] for nm in grp], first)

    sq_cols, grad_x, g = _local_step(x[0], ctx[0], loss_target[0], mod_h, mod_g, norm_g_full, feed, pool_w[0],
                                  pool_scale, q_norm_g, kv_norm_g, conv_w_full, final_norm_g)
    w_of, m_of, v_of = (dict(zip(WEIGHT_NAMES, t)) for t in (weights, moms, vels))
    results = {}

    def update(nm, grad, view=lambda t: t):
        outs = _adamw(f"adamw_{nm}", view(w_of[nm]), grad.reshape(view(w_of[nm]).shape), view(m_of[nm]), view(v_of[nm]))
        results[nm] = tuple(view(t) for t in (grad.reshape(view(w_of[nm]).shape), *outs))

    def swap(t):
        return jnp.swapaxes(t, -1, -2)

    stacked = ("gate_t", "up_t", "down")
    early = feed.collect(["l1f1", "l1m", "l1f0", "l0f1", "l0m"], [grad_x], stacked)
    update("ab_w_in", early["in_t"], swap)
    update("w_uq", early["uq"])
    update("w_ukv", early["ukv_t"].T)
    update("ab_w_out", early["ab_out"])
    update("conv_w_in", early["cin_t"].T)
    update("conv_w_out", early["c_out"])
    ffn = {}
    for nm, prefix, view in (("ffn_w_gate", "gate_t", swap), ("ffn_w_up", "up_t", swap),
                             ("ffn_w_down", "down", lambda t: t)):
        w4, m4, v4 = (view(t).reshape((4,) + view(t).shape[-2:]) for t in (w_of[nm], m_of[nm], v_of[nm]))
        prev = None
        for i in (3, 2, 1):
            prev = _adamw_part(f"adamw_{nm}{i}", i, w4, early[f"{prefix}{i}"], me, m4, v4, prev)
        ffn[nm] = (prefix, view, w4, m4, v4, prev)
    done_early = [results[nm][1] for nm in results] + [state[5][1] for state in ffn.values()]
    late = feed.collect(["l0f0"], done_early, stacked)
    for nm, (prefix, view, w4, m4, v4, prev) in ffn.items():
        outs = _adamw_part(f"adamw_{nm}0", 0, w4, late[f"{prefix}0"], me, m4, v4, prev)
        results[nm] = tuple(view(t.reshape(view(w_of[nm]).shape)) for t in outs)

    dm = jnp.stack([g["mod_h"], jnp.stack([g["mod_g"], jnp.zeros_like(g["mod_g"])])])
    dm_all = _exchange("gather_dmod", dm.reshape(-1, 128), False, results["ffn_w_down"][1]).reshape(N_DEV, 2, 2, N_MOD * d)
    grad_b_mod = _sum_rows("dmod_bias", dm_all.reshape(2 * N_DEV, 2 * N_MOD * d)).reshape(2, N_MOD * d)
    dm_sh = lax.dynamic_slice(dm_all, (0, 0, 0, me * mod_cols), (N_DEV, 2, 2, mod_cols))
    gw_mod, cctx_parts = [], []
    for l in range(2):
        dm_l = dm_sh[:, :, l, :].transpose(1, 0, 2).reshape(2 * N_DEV, mod_cols).astype(BF16)
        gw_mod.append(_mm(f"mod_dw{l}", [(sil, dm_l)], "tn", F32, 512, 384))
        dm_ctx = jnp.concatenate([dm_l[N_DEV:], jnp.zeros((N_DEV, mod_cols), BF16)], axis=0)
        cctx_parts.append(_mm(f"mod_dcond{l}", [(dm_ctx, w_mod_b[l])], "nt", F32, 16, 512))
    cctx_part = _sum_rows("mod_dcond_sum", jnp.concatenate(cctx_parts, axis=0))
    update("w_mod", jnp.stack(gw_mod))
    update("b_mod", grad_b_mod)

    small_g = jnp.concatenate([g["pool_w"].reshape(-1), g["pool_scale"].reshape(-1), g["q_norm_g"].reshape(-1),
                               g["kv_norm_g"].reshape(-1), g["final_norm_g"].reshape(-1), g["norm_g"].reshape(-1),
                               g["conv_w"].reshape(-1), sq_cols.reshape(-1), cctx_part.reshape(-1)])
    sizes = [pool_w.size, pool_scale.size, q_norm_g.size, kv_norm_g.size, d, 6 * d, 3 * d, d, d]
    sg_n = -(-small_g.shape[0] // 1024) * 1024
    small_g = jnp.pad(small_g, (0, sg_n - small_g.shape[0]))
    sg_all = _exchange("gather_small_grads", small_g.reshape(-1, 128), False).reshape(N_DEV, sg_n)
    scale_vec = jnp.concatenate([jnp.ones((1, sum(sizes[:-1])), F32), dsil[N_DEV:N_DEV + 1],
                                 jnp.ones((1, sg_n - sum(sizes)), F32)], axis=1)
    sg = _sum_rows("small_grads_sum", sg_all, scale_vec)[0]
    cuts, pos = [], 0
    for sz in sizes:
        cuts.append(sg[pos:pos + sz])
        pos += sz
    g_pool_w, g_pool_scale, g_q_norm, g_kv_norm, g_final, g_norm_full, g_conv_full, sq_all, g_c_ctx = cuts
    loss = 0.5 * jnp.sum(sq_all) / d
    update("c_ctx", g_c_ctx)
    update("norm_g", lax.dynamic_slice(g_norm_full.reshape(2, 3, d), (0, 0, me * ng_sh), (2, 3, ng_sh)))
    update("conv_w", lax.dynamic_slice(g_conv_full.reshape(3, d), (0, me * cw_sh), (3, cw_sh)))
    update("pool_w", g_pool_w)
    update("pool_scale", g_pool_scale)
    update("q_norm_g", g_q_norm)
    update("kv_norm_g", g_kv_norm)
    update("final_norm_g", g_final)
    outs = [results[nm] for nm in WEIGHT_NAMES]
    return (loss, grad_x[None], *[o[0] for o in outs], *[o[1] for o in outs], *[o[2] for o in outs],
            *[o[3] for o in outs])
```

```python
import functools
import math

import jax
import jax.numpy as jnp
import numpy as np
from jax import lax
from jax.experimental import pallas as pl
from jax.experimental.pallas import tpu as pltpu

F32 = jnp.float32
BF16 = jnp.bfloat16
MESH = pl.DeviceIdType.MESH
SDS = jax.ShapeDtypeStruct

N_DEV = 8
D_MODEL = 1024
N_MOD = 9
D_FF = 2816
POOL_WINDOWS = (2, 4, 8, 16)
POOL_DIM = 512
POOL_GROUP_DIM = 128
HEADS = 8
QK_NOPE = 64
QK_ROPE = 32
QK_HEAD = QK_NOPE + QK_ROPE
V_HEAD = 64
Q_RANK = 768
KV_RANK = 256
GRID_W = 64
ROPE_THETA = 10000.0
RMS_EPS = 1e-6
ATTN_SCALE = 1.0 / math.sqrt(QK_HEAD)
HEAD_PAD = 128
POOL_PAD = 16
PA_POOL, PA_CQ, PA_KV = 0, 768, 1536
PA_KV_W = 384
PA_W = PA_KV + PA_KV_W

ADAM_LR, ADAM_B1, ADAM_B2, ADAM_EPS, ADAM_WD, ADAM_STEP = 0.001, 0.9, 0.999, 1e-08, 0.01, 10

VMEM_LIMIT_BYTES = 56 * 1024 * 1024

NN = ((1,), (0,))
NT = ((1,), (1,))
TN = ((0,), (0,))


def _cparams():
    return pltpu.CompilerParams(vmem_limit_bytes=VMEM_LIMIT_BYTES)


def _dot(a, b, dims):
    return lax.dot_general(a, b, (dims, ((), ())), preferred_element_type=F32)


def _tile(n, cap, mult=8):
    t = (min(cap, n) // mult) * mult
    while t >= mult:
        if n % t == 0:
            return t
        t -= mult
    return n


def _colsum(x):
    return jnp.sum(x, axis=0, keepdims=True)


def _rms(x):
    r = lax.rsqrt(jnp.mean(x * x, axis=-1, keepdims=True) + RMS_EPS)
    return x * r, r


def _rms_bwd(n, r, dn):
    return r * (dn - n * jnp.mean(dn * n, axis=-1, keepdims=True))


def _rowwise(name, fn, t_rows, tm, n_lat, rows, vecs, outs, accs, into=None):
    nt = t_rows // tm
    nlt = n_lat // tm
    n_groups = 2 if nlt < nt else 1

    def grp(i):
        return jnp.where(i >= nlt, 1, 0) if n_groups == 2 else 0

    in_specs = [pl.BlockSpec((tm, w), functools.partial(lambda i, cb: (i, cb), cb=cb)) for (_, w, cb) in rows]
    in_specs += [pl.BlockSpec((1,) + v.shape[1:], lambda i: (grp(i), 0, 0)) for v in vecs]
    out_specs = [pl.BlockSpec((tm, w), lambda i: (i, 0)) for (w, _) in outs]
    out_specs += [pl.BlockSpec((1, 1, w), lambda i: (grp(i), 0, 0)) for w in accs]
    out_shape = [SDS((t_rows, w), dt) for (w, dt) in outs] + [SDS((n_groups, 1, w), F32) for w in accs]
    n_r, n_v, n_o = len(rows), len(vecs), len(outs)
    extra, aliases = [], {}
    if into is not None:
        extra, aliases = [into[0]], {n_r + n_v: 0}
        in_specs.append(pl.BlockSpec(memory_space=pl.ANY))
        out_specs[0] = pl.BlockSpec((tm, outs[0][0]), lambda i: (i, into[1]))
        out_shape[0] = SDS(into[0].shape, into[0].dtype)
    n_in = n_r + n_v + len(extra)

    def body(*refs):
        row_vals = [r[...] for r in refs[:n_r]]
        vec_vals = [v[0] for v in refs[n_r:n_r + n_v]]
        out_refs = refs[n_in:n_in + n_o]
        acc_refs = refs[n_in + n_o:]
        out_vals, acc_vals = fn(row_vals, vec_vals)
        for o_ref, o in zip(out_refs, out_vals):
            o_ref[...] = o.astype(o_ref.dtype)
        if acc_refs:
            i = pl.program_id(0)
            first = (i == 0) | (i == nlt) if n_groups == 2 else i == 0

            @pl.when(first)
            def _():
                for a_ref, a in zip(acc_refs, acc_vals):
                    a_ref[0] = a

            @pl.when(jnp.logical_not(first))
            def _():
                for a_ref, a in zip(acc_refs, acc_vals):
                    a_ref[0] += a

    res = pl.pallas_call(
        body, name=name, grid=(nt,), in_specs=in_specs, out_specs=out_specs, out_shape=out_shape,
        input_output_aliases=aliases, compiler_params=_cparams(),
    )(*[r[0] for r in rows], *vecs, *extra)
    return res[:n_o], res[n_o:]


RESIDENT_BYTES = 12 * 1024 * 1024


def _mm(name, pairs, mode, out_dtype, tm_cap=256, tn_cap=512, bias=None):
    a0, b0 = pairs[0]
    if mode == "nn":
        m, n, dims = a0.shape[0], b0.shape[1], NN
    elif mode == "nt":
        m, n, dims = a0.shape[0], b0.shape[0], NT
    else:
        m, n, dims = a0.shape[1], b0.shape[1], TN
    b_bytes = sum(b.size * b.dtype.itemsize for _, b in pairs)
    tn = n if b_bytes <= RESIDENT_BYTES else _tile(n, tn_cap, 128)
    tm = _tile(m, tm_cap, 128 if mode == "tn" else 16)

    def a_spec(a):
        if mode == "tn":
            return pl.BlockSpec((a.shape[0], tm), lambda i, j: (0, i))
        return pl.BlockSpec((tm, a.shape[1]), lambda i, j: (i, 0))

    def b_spec(b):
        if mode == "nt":
            return pl.BlockSpec((tn, b.shape[1]), lambda i, j: (j, 0))
        return pl.BlockSpec((b.shape[0], tn), lambda i, j: (0, j))

    in_specs, flat = [], []
    for a, b in pairs:
        in_specs += [a_spec(a), b_spec(b)]
        flat += [a, b]
    if bias is not None:
        in_specs.append(pl.BlockSpec((1, tn), lambda i, j: (0, j)))
        flat.append(bias)
    n_pairs = len(pairs)

    def body(*refs):
        acc = None
        for p in range(n_pairs):
            t = _dot(refs[2 * p][...], refs[2 * p + 1][...], dims)
            acc = t if acc is None else acc + t
        if bias is not None:
            acc = acc + refs[2 * n_pairs][...]
        refs[-1][...] = acc.astype(refs[-1].dtype)

    return pl.pallas_call(
        body, name=name, grid=(m // tm, n // tn), in_specs=in_specs,
        out_specs=pl.BlockSpec((tm, tn), lambda i, j: (i, j)),
        out_shape=SDS((m, n), out_dtype), compiler_params=_cparams(),
    )(*flat)


def _mm_resid(name, pairs, s, mg, k, coef, n_lat):
    t_rows, n = pairs[0][0].shape[0], s.shape[1]
    n_pairs = len(pairs)
    tm = _tile(math.gcd(n_lat, t_rows), 256, 16)
    nlt = n_lat // tm
    n_groups = 2 if nlt < t_rows // tm else 1

    def grp(i):
        return jnp.where(i >= nlt, 1, 0) if n_groups == 2 else 0

    def body(*refs):
        s_ref, mg_ref, so_ref, o_ref = refs[2 * n_pairs:]
        o = _dot(refs[0][...], refs[n_pairs][...], NN)
        for p in range(1, n_pairs):
            o = o + _dot(refs[p][...], refs[n_pairs + p][...], NN)
        gate = mg_ref[0, 3 * k + 2:3 * k + 3, :]
        o_ref[...] = o.astype(BF16)
        so_ref[...] = s_ref[...] + (coef * gate) * o

    row = pl.BlockSpec((tm, n), lambda i: (i, 0))
    return pl.pallas_call(
        body, name=name, grid=(t_rows // tm,),
        in_specs=[pl.BlockSpec((tm, a.shape[1]), lambda i: (i, 0)) for a, _ in pairs]
        + [pl.BlockSpec(b.shape, lambda i: (0, 0)) for _, b in pairs]
        + [row, pl.BlockSpec((1, mg.shape[1], n), lambda i: (grp(i), 0, 0))],
        out_specs=[row, row], out_shape=[SDS((t_rows, n), F32), SDS((t_rows, n), BF16)], compiler_params=_cparams(),
    )(*[a for a, _ in pairs], *[b for _, b in pairs], s, mg)


def _dw_pair(name, a1, a2, b):
    kk, m = a1.shape
    n = b.shape[1]
    tm = _tile(m, 256, 128)

    def body(a1_ref, a2_ref, b_ref, o1_ref, o2_ref):
        bb = b_ref[...]
        o1_ref[...] = _dot(a1_ref[...], bb, TN).astype(BF16)
        o2_ref[...] = _dot(a2_ref[...], bb, TN).astype(BF16)

    col = pl.BlockSpec((kk, tm), lambda i: (0, i))
    out = pl.BlockSpec((tm, n), lambda i: (i, 0))
    return pl.pallas_call(
        body, name=name, grid=(m // tm,), in_specs=[col, col, pl.BlockSpec(b.shape, lambda i: (0, 0))],
        out_specs=[out, out], out_shape=[SDS((m, n), BF16)] * 2, compiler_params=_cparams(),
    )(a1, a2, b)


def _groups(t_rows, tm, n_lat):
    nlt = n_lat // tm
    if nlt < t_rows // tm:
        return 2, (lambda i: jnp.where(i >= nlt, 1, 0)), (lambda i: (i == 0) | (i == nlt))
    return 1, (lambda i: 0), (lambda i: i == 0)


def _accumulate(acc_refs, vals, first):
    @pl.when(first)
    def _():
        for r, v in zip(acc_refs, vals):
            r[0] = v

    @pl.when(jnp.logical_not(first))
    def _():
        for r, v in zip(acc_refs, vals):
            r[0] += v


def _adaln_math(s, m, k):
    n, _ = _rms(s)
    return (n * m[9 + k:10 + k]) * (1.0 + m[3 * k + 1:3 * k + 2]) + m[3 * k:3 * k + 1]


def _ffn_up(name, s, mg, k, n_lat, wg_t, wu_t):
    t_rows, f = s.shape[0], wg_t.shape[0]
    tm = _row_tm(t_rows, n_lat)
    _, grp, _ = _groups(t_rows, tm, n_lat)

    def body(s_ref, mg_ref, wg_ref, wu_ref, u_ref, a_ref, b_ref, h_ref):
        uu = _adaln_math(s_ref[...], mg_ref[0], k).astype(BF16)
        u_ref[...] = uu
        a = _dot(uu, wg_ref[...], NT)
        b = _dot(uu, wu_ref[...], NT)
        sg = jax.nn.sigmoid(a)
        act = a * sg
        a_ref[...] = (b * (sg * (1.0 + a * (1.0 - sg)))).astype(BF16)
        b_ref[...] = act.astype(BF16)
        h_ref[...] = (act * b).astype(BF16)

    w_spec = pl.BlockSpec(wg_t.shape, lambda i: (0, 0))
    o_spec = pl.BlockSpec((tm, f), lambda i: (i, 0))
    row = pl.BlockSpec((tm, s.shape[1]), lambda i: (i, 0))
    return pl.pallas_call(
        body, name=name, grid=(t_rows // tm,),
        in_specs=[row, pl.BlockSpec((1,) + mg.shape[1:], lambda i: (grp(i), 0, 0)), w_spec, w_spec],
        out_specs=[row, o_spec, o_spec, o_spec],
        out_shape=[SDS(s.shape, BF16)] + [SDS((t_rows, f), BF16)] * 3, compiler_params=_cparams(),
    )(s, mg, wg_t, wu_t)


def _ffn_dact(name, ds_out, o, mg, k, coef, n_lat, wd, a, b):
    t_rows, f = ds_out.shape[0], wd.shape[0]
    tm = _row_tm(t_rows, n_lat)
    n_groups, grp, first = _groups(t_rows, tm, n_lat)
    d = ds_out.shape[1]

    def body(ds_ref, o_ref, mg_ref, wd_ref, a_ref, b_ref, do_ref, da_ref, db_ref, dg_ref):
        dd = coef * ds_ref[...]
        do = (dd * mg_ref[0, 3 * k + 2:3 * k + 3, :]).astype(BF16)
        do_ref[...] = do
        _accumulate([dg_ref], [_colsum(dd * o_ref[...].astype(F32))], first(pl.program_id(0)))
        dh = _dot(do, wd_ref[...], NT)
        da_ref[...] = (dh * a_ref[...].astype(F32)).astype(BF16)
        db_ref[...] = (dh * b_ref[...].astype(F32)).astype(BF16)

    row = pl.BlockSpec((tm, d), lambda i: (i, 0))
    t_spec = pl.BlockSpec((tm, f), lambda i: (i, 0))
    return pl.pallas_call(
        body, name=name, grid=(t_rows // tm,),
        in_specs=[row, row, pl.BlockSpec((1,) + mg.shape[1:], lambda i: (grp(i), 0, 0)),
                  pl.BlockSpec(wd.shape, lambda i: (0, 0)), t_spec, t_spec],
        out_specs=[row, t_spec, t_spec, pl.BlockSpec((1, 1, d), lambda i: (grp(i), 0, 0))],
        out_shape=[SDS((t_rows, d), BF16), SDS((t_rows, f), BF16), SDS((t_rows, f), BF16), SDS((n_groups, 1, d), F32)],
        compiler_params=_cparams(),
    )(ds_out, o, mg, wd, a, b)


def _du_adaln(name, pairs, s, ds_out, mg, k, n_lat, after, out_rows=None):
    t_rows, d = s.shape
    tm = _row_tm(t_rows, n_lat, 512)
    n_groups, grp, first = _groups(t_rows, tm, n_lat)
    n_pairs = len(pairs)
    nt, n_ds, n_out = t_rows // tm, ds_out.shape[0] // tm, (out_rows or t_rows) // tm

    def body(*refs):
        s_ref, ds_ref, mg_ref, z_ref, out_ref, dsh_ref, dsc_ref, dgn_ref = refs[2 * n_pairs:]
        i = pl.program_id(0)
        d_u = z_ref[...]
        for p in range(n_pairs):
            d_u = d_u + _dot(refs[p][...], refs[n_pairs + p][...], NN)
        m = mg_ref[0]
        gain, scale = m[9 + k:10 + k], m[3 * k + 1:3 * k + 2]
        n, r = _rms(s_ref[...])
        dxn = d_u * (1.0 + scale)
        ds_in = _rms_bwd(n, r, dxn * gain)
        ds_in = ds_in + (ds_ref[...] if n_ds == nt else jnp.where(i < n_ds, ds_ref[...], 0.0))
        if n_out == nt:
            out_ref[...] = ds_in
        else:
            @pl.when(i < n_out)
            def _():
                out_ref[...] = ds_in
        _accumulate([dsh_ref, dsc_ref, dgn_ref], [_colsum(d_u), _colsum(d_u * (n * gain)), _colsum(dxn * n)], first(i))

    row = pl.BlockSpec((tm, d), lambda i: (i, 0))
    acc = pl.BlockSpec((1, 1, d), lambda i: (grp(i), 0, 0))
    res = pl.pallas_call(
        body, name=name, grid=(t_rows // tm,),
        in_specs=[pl.BlockSpec((tm, a.shape[1]), lambda i: (i, 0)) for a, _ in pairs]
        + [pl.BlockSpec(w.shape, lambda i: (0, 0)) for _, w in pairs]
        + [row, pl.BlockSpec((tm, d), lambda i: (jnp.minimum(i, n_ds - 1), 0)),
           pl.BlockSpec((1,) + mg.shape[1:], lambda i: (grp(i), 0, 0)), pl.BlockSpec((1, d), lambda i: (0, 0))],
        out_specs=[pl.BlockSpec((tm, d), lambda i: (jnp.minimum(i, n_out - 1), 0)), acc, acc, acc],
        out_shape=[SDS((n_out * tm, d), F32)] + [SDS((n_groups, 1, d), F32)] * 3, compiler_params=_cparams(),
    )(*[a for a, _ in pairs], *[w for _, w in pairs], s, ds_out, mg, after)
    return res[0], res[1:]


def _adaln_mm(name, s, mg, k, n_lat, w_t):
    rows, d = s.shape
    tm = _row_tm(rows, n_lat)
    _, grp, _ = _groups(rows, tm, n_lat)
    n = w_t.shape[0]

    def body(s_ref, mg_ref, w_ref, u_ref, y_ref):
        uu = _adaln_math(s_ref[...], mg_ref[0], k).astype(BF16)
        u_ref[...] = uu
        y_ref[...] = _dot(uu, w_ref[...], NT)

    row = pl.BlockSpec((tm, d), lambda i: (i, 0))
    return pl.pallas_call(
        body, name=name, grid=(rows // tm,),
        in_specs=[row, pl.BlockSpec((1,) + mg.shape[1:], lambda i: (grp(i), 0, 0)), pl.BlockSpec(w_t.shape, lambda i: (0, 0))],
        out_specs=[row, pl.BlockSpec((tm, n), lambda i: (i, 0))],
        out_shape=[SDS((rows, d), BF16), SDS((rows, n), F32)], compiler_params=_cparams(),
    )(s, mg, w_t)


def _gate_mm(name, ds_out, o, mg, k, coef, n_lat, w):
    t_rows, d = ds_out.shape
    tm = _row_tm(t_rows, n_lat)
    n_groups, grp, first = _groups(t_rows, tm, n_lat)
    n = w.shape[0]

    def body(ds_ref, o_ref, mg_ref, w_ref, do_ref, y_ref, dg_ref):
        dd = coef * ds_ref[...]
        do = (dd * mg_ref[0, 3 * k + 2:3 * k + 3, :]).astype(BF16)
        do_ref[...] = do
        _accumulate([dg_ref], [_colsum(dd * o_ref[...].astype(F32))], first(pl.program_id(0)))
        y_ref[...] = _dot(do, w_ref[...], NT)

    row = pl.BlockSpec((tm, d), lambda i: (i, 0))
    return pl.pallas_call(
        body, name=name, grid=(t_rows // tm,),
        in_specs=[row, row, pl.BlockSpec((1,) + mg.shape[1:], lambda i: (grp(i), 0, 0)), pl.BlockSpec(w.shape, lambda i: (0, 0))],
        out_specs=[row, pl.BlockSpec((tm, n), lambda i: (i, 0)), pl.BlockSpec((1, 1, d), lambda i: (grp(i), 0, 0))],
        out_shape=[SDS((t_rows, d), BF16), SDS((t_rows, n), F32), SDS((n_groups, 1, d), F32)],
        compiler_params=_cparams(),
    )(ds_out, o, mg, w)


def _row_tm(t_rows, n_lat, cap=256):
    return _tile(math.gcd(t_rows, n_lat), cap, 16)


def _rmsnorm_fwd(name, x, width, colblk, gain, t_rows):
    def fn(rv, vv):
        n, _ = _rms(rv[0])
        return [n * vv[0]], []

    (y,), _ = _rowwise(name, fn, t_rows, _tile(t_rows, 256, 16), t_rows, [(x, width, colblk)],
                       [gain.reshape(1, 1, width)], [(width, BF16)], [])
    return y


def _rmsnorm_bwd(name, x, width, colblk, dy, gain, t_rows, out_dtype=F32, into=None):
    def fn(rv, vv):
        n, r = _rms(rv[0])
        return [_rms_bwd(n, r, rv[1] * vv[0])], [_colsum(rv[1] * n)]

    (dx,), (dgain,) = _rowwise(name, fn, t_rows, _tile(t_rows, 256, 16), t_rows,
                               [(x, width, colblk), (dy, width, 0)], [gain.reshape(1, 1, width)],
                               [(width, out_dtype)], [width], into)
    return dx, dgain


def _final_loss(name, h, target, gain):
    t_rows = h.shape[0]
    inv_d = 1.0 / D_MODEL

    def fn(rv, vv):
        g = vv[0]
        n, r = _rms(rv[0])
        e = n * g - rv[1]
        dy = e * inv_d
        return [_rms_bwd(n, r, dy * g)], [_colsum(e * e), _colsum(dy * n)]

    (dh,), (sq, dgain) = _rowwise(name, fn, t_rows, _tile(t_rows, 256, 16), t_rows,
                                  [(h, D_MODEL, 0), (target, D_MODEL, 0)], [gain.reshape(1, 1, D_MODEL)],
                                  [(D_MODEL, F32)], [D_MODEL, D_MODEL])
    return dh, sq, dgain


def _exact_dot(x, m_ref):
    hi = x.astype(BF16)
    lo = (x - hi.astype(F32)).astype(BF16)
    return _dot(hi, m_ref[...], NN) + _dot(lo, m_ref[...], NN)


def _rope(name, z, width, colblk, cos32, sin32, layout, backward, out_dtype, remap=None, into=None):
    t_rows = cos32.shape[0]
    expand, plain, perm = layout
    w_in = remap.shape[1] if (remap is not None and backward) else width
    w_out = remap.shape[1] if (remap is not None and not backward) else width
    extra = [] if remap is None else [remap.T if backward else remap]
    dest = [] if into is None else [into[0]]

    def body(z_ref, c_ref, s_ref, e_ref, m_ref, p_ref, *rest):
        o_ref = rest[-1]
        zz = z_ref[...]
        if remap is not None and backward:
            zz = _exact_dot(zz, rest[0])
        cos = _exact_dot(c_ref[...], e_ref) + m_ref[...]
        sin = _exact_dot(s_ref[...], e_ref)
        rot = _exact_dot(zz * sin if backward else zz, p_ref)
        if not backward:
            rot = rot * sin
        res = zz * cos + rot
        if remap is not None and not backward:
            res = _dot(res.astype(BF16), rest[0][...], NN)
        o_ref[...] = res.astype(o_ref.dtype)

    tm = _tile(t_rows, 256, 16)
    f_spec = pl.BlockSpec((tm, QK_ROPE), lambda i: (i, 0))
    return pl.pallas_call(
        body, name=name, grid=(t_rows // tm,),
        in_specs=[pl.BlockSpec((tm, w_in), lambda i: (i, colblk)), f_spec, f_spec,
                  pl.BlockSpec((QK_ROPE, width), lambda i: (0, 0)), pl.BlockSpec((1, width), lambda i: (0, 0)),
                  pl.BlockSpec((width, width), lambda i: (0, 0))]
        + [pl.BlockSpec(e.shape, lambda i: (0, 0)) for e in extra] + [pl.BlockSpec(memory_space=pl.ANY)] * len(dest),
        out_specs=pl.BlockSpec((tm, w_out), lambda i: (i, 0 if into is None else into[1])),
        out_shape=SDS((t_rows, w_out), out_dtype) if into is None else SDS(into[0].shape, into[0].dtype),
        input_output_aliases={} if into is None else {6 + len(extra): 0}, compiler_params=_cparams(),
    )(z, cos32, sin32, expand, plain, perm.T if backward else perm, *extra, *dest)


def _window_sum(x, w, transposed):
    n_rows = x.shape[0]
    zeros = jnp.zeros((POOL_PAD, x.shape[1]), F32)
    y = jnp.concatenate([zeros, x, zeros], axis=0)
    total = n_rows + 2 * POOL_PAD
    if transposed:
        y = y + pltpu.roll(y, total - 1, 0)
    else:
        y = y + pltpu.roll(y, 1, 0)
    step = 1
    while 2 * step < w:
        y = pltpu.roll(y, step, 0) + pltpu.roll(y, total - step, 0)
        step *= 2
    return y[POOL_PAD:POOL_PAD + n_rows]


def _window_count(n_rows, w):
    t = lax.broadcasted_iota(jnp.int32, (n_rows, 1), 0)
    lo = jnp.maximum(t - w // 2, 0)
    hi = jnp.minimum(t + (w - w // 2 - 1), n_rows - 1)
    return (hi - lo + 1).astype(F32)


def _pool_fwd(name, proj, n_rows, w_grp, scale):
    def body(x_ref, w_ref, sc_ref, y_ref, p_ref):
        for g, w in enumerate(POOL_WINDOWS):
            cols = slice(g * POOL_GROUP_DIM, (g + 1) * POOL_GROUP_DIM)
            x = x_ref[:, cols]
            p = _window_sum(x, w, False) * (1.0 / _window_count(n_rows, w)) - x
            pb = p.astype(BF16)
            p_ref[:, cols] = pb
            y_ref[:, cols] = (_dot(pb, w_ref[g], NN) * sc_ref[:, cols]).astype(BF16)

    blk = pl.BlockSpec((n_rows, POOL_DIM), lambda i: (0, 0))
    return pl.pallas_call(
        body, name=name, grid=(1,),
        in_specs=[blk, pl.BlockSpec(w_grp.shape, lambda i: (0, 0, 0)), pl.BlockSpec((1, POOL_DIM), lambda i: (0, 0))],
        out_specs=[blk, blk], out_shape=[SDS((n_rows, POOL_DIM), BF16)] * 2, compiler_params=_cparams(),
    )(proj, w_grp, scale)


def _pool_bwd(name, dcat, n_rows, p, w_grp, scale, into):
    def body(dy_ref, p_ref, w_ref, sc_ref, into_ref, dx_ref, dw_ref, dsc_ref):
        for g, w in enumerate(POOL_WINDOWS):
            cols = slice(g * POOL_GROUP_DIM, (g + 1) * POOL_GROUP_DIM)
            dy = dy_ref[:, cols]
            pb = p_ref[:, cols]
            pw = _dot(pb, w_ref[g], NN)
            dsc_ref[:, cols] = _colsum(dy * pw)
            dpw = (dy * sc_ref[:, cols]).astype(BF16)
            dw_ref[g] = _dot(pb, dpw, TN)
            dp = _dot(dpw, w_ref[g], NT)
            dx_ref[:, cols] = (_window_sum(dp * (1.0 / _window_count(n_rows, w)), w, True) - dp).astype(BF16)

    blk = pl.BlockSpec((n_rows, POOL_DIM), lambda i: (0, 0))
    w_spec = pl.BlockSpec(w_grp.shape, lambda i: (0, 0, 0))
    v_spec = pl.BlockSpec((1, POOL_DIM), lambda i: (0, 0))
    return pl.pallas_call(
        body, name=name, grid=(1,), in_specs=[blk, blk, w_spec, v_spec, pl.BlockSpec(memory_space=pl.ANY)],
        out_specs=[blk, w_spec, v_spec],
        out_shape=[SDS(into.shape, into.dtype), SDS(w_grp.shape, F32), SDS((1, POOL_DIM), F32)],
        input_output_aliases={4: 0}, compiler_params=_cparams(),
    )(dcat, p, w_grp, scale, into)


def _head_keys(kv_blk, k_rope):
    lane = lax.broadcasted_iota(jnp.int32, (1, HEAD_PAD), 1)
    return jnp.where(lane < QK_NOPE, kv_blk, k_rope)


def _attn_fwd(name, q, kv, k_rope, n_q):
    n_k = kv.shape[0]
    h = kv.shape[1] // HEAD_PAD
    tq = _tile(n_q, 256, 16)

    def body(q_ref, kv_ref, kr_ref, o_ref, lse_ref):
        kvb = kv_ref[...]
        s = _dot(q_ref[...], _head_keys(kvb, kr_ref[...]), NT) * ATTN_SCALE
        m = jnp.max(s, axis=-1, keepdims=True)
        e = jnp.exp(s - m)
        l = jnp.sum(e, axis=-1, keepdims=True)
        p = (e * (1.0 / l)).astype(BF16)
        lane = lax.broadcasted_iota(jnp.int32, (1, HEAD_PAD), 1)
        o_ref[...] = jnp.where(lane >= QK_NOPE, _dot(p, kvb, NN), 0.0).astype(BF16)
        lse_ref[...] = m + jnp.log(l)

    blk = pl.BlockSpec((tq, HEAD_PAD), lambda hh, i: (i, hh))
    return pl.pallas_call(
        body, name=name, grid=(h, n_q // tq),
        in_specs=[blk, pl.BlockSpec((n_k, HEAD_PAD), lambda hh, i: (0, hh)),
                  pl.BlockSpec((n_k, HEAD_PAD), lambda hh, i: (0, 0))],
        out_specs=[blk, pl.BlockSpec((None, tq, 1), lambda hh, i: (hh, i, 0))],
        out_shape=[SDS((n_q, h * HEAD_PAD), BF16), SDS((h, n_q, 1), F32)], compiler_params=_cparams(),
    )(q, kv, k_rope)


def _attn_bwd(name, q, kv, k_rope, o, lse, dy, dy_col0, n_q):
    n_k = kv.shape[0]
    h = kv.shape[1] // HEAD_PAD
    tq = _tile(n_q, 256, 16)
    n_i = n_q // tq

    def body(q_ref, kv_ref, kr_ref, o_ref, lse_ref, do_ref, dq_ref, dkv_ref, dkr_ref, acc_k, acc_v):
        hh, i = pl.program_id(0), pl.program_id(1)
        qq, kvb = q_ref[...], kv_ref[...]
        kk = _head_keys(kvb, kr_ref[...])
        d_o = do_ref[...]
        dd = d_o.astype(BF16)
        s = _dot(qq, kk, NT) * ATTN_SCALE
        p = jnp.exp(s - lse_ref[...])
        dp = _dot(dd, kvb, NT)
        delta = jnp.sum(d_o * o_ref[...].astype(F32), axis=-1, keepdims=True)
        ds = (p * (dp - delta) * ATTN_SCALE).astype(BF16)
        dq_ref[...] = _dot(ds, kk, NN)
        dk = _dot(ds, qq, TN)
        dv = _dot(p.astype(BF16), dd, TN)

        @pl.when(i == 0)
        def _():
            acc_k[...] = dk
            acc_v[...] = dv

        @pl.when(i > 0)
        def _():
            acc_k[...] += dk
            acc_v[...] += dv

        @pl.when(i == n_i - 1)
        def _():
            lane = lax.broadcasted_iota(jnp.int32, (1, HEAD_PAD), 1)
            dkv_ref[...] = jnp.where(lane < QK_NOPE, acc_k[...], acc_v[...]).astype(BF16)
            rope = jnp.where((lane >= QK_NOPE) & (lane < QK_HEAD), acc_k[...], 0.0)

            @pl.when(hh == 0)
            def _():
                dkr_ref[...] = rope

            @pl.when(hh > 0)
            def _():
                dkr_ref[...] += rope

    blk = pl.BlockSpec((tq, HEAD_PAD), lambda hh, i: (i, hh))
    kv_spec = pl.BlockSpec((n_k, HEAD_PAD), lambda hh, i: (0, hh))
    shared = pl.BlockSpec((n_k, HEAD_PAD), lambda hh, i: (0, 0))
    return pl.pallas_call(
        body, name=name, grid=(h, n_i),
        in_specs=[blk, kv_spec, shared, blk, pl.BlockSpec((None, tq, 1), lambda hh, i: (hh, i, 0)),
                  pl.BlockSpec((tq, HEAD_PAD), lambda hh, i: (i, dy_col0 + hh))],
        out_specs=[blk, kv_spec, shared],
        out_shape=[SDS((n_q, h * HEAD_PAD), F32), SDS((n_k, h * HEAD_PAD), BF16), SDS((n_k, HEAD_PAD), F32)],
        scratch_shapes=[pltpu.VMEM((n_k, HEAD_PAD), F32), pltpu.VMEM((n_k, HEAD_PAD), F32)],
        compiler_params=_cparams(),
    )(q, kv, k_rope, o, lse, dy)


CONV_COLS = 256


def _shift_rows(x, d):
    n_rows = x.shape[0]
    t = lax.broadcasted_iota(jnp.int32, (n_rows, 1), 0)
    if d > 0:
        return jnp.where(t >= d, pltpu.roll(x, d, 0), 0.0)
    return jnp.where(t < n_rows + d, pltpu.roll(x, n_rows + d, 0), 0.0)


def _conv_fwd(name, z3, conv_w):
    n_rows = z3.shape[0]
    nb = D_MODEL // CONV_COLS

    def body(b_ref, c_ref, v_ref, w_ref, y_ref):
        z = c_ref[...] * v_ref[...]
        zc = w_ref[0:1, :] * _shift_rows(z, 1) + w_ref[1:2, :] * z + w_ref[2:3, :] * _shift_rows(z, -1)
        y_ref[...] = (b_ref[...] * zc).astype(BF16)

    def part(k):
        return pl.BlockSpec((n_rows, CONV_COLS), lambda j: (0, k * nb + j))

    return pl.pallas_call(
        body, name=name, grid=(nb,),
        in_specs=[part(0), part(1), part(2), pl.BlockSpec((3, CONV_COLS), lambda j: (0, j))],
        out_specs=pl.BlockSpec((n_rows, CONV_COLS), lambda j: (0, j)),
        out_shape=SDS((n_rows, D_MODEL), BF16), compiler_params=_cparams(),
    )(z3, z3, z3, conv_w)


def _conv_bwd(name, dy, z3, conv_w):
    n_rows = z3.shape[0]
    nb = D_MODEL // CONV_COLS

    def body(dy_ref, b_ref, c_ref, v_ref, w_ref, db_ref, dc_ref, dv_ref, dw_ref):
        c, v, d_y = c_ref[...], v_ref[...], dy_ref[...]
        z = c * v
        z_dn, z_up = _shift_rows(z, 1), _shift_rows(z, -1)
        zc = w_ref[0:1, :] * z_dn + w_ref[1:2, :] * z + w_ref[2:3, :] * z_up
        db_ref[...] = (d_y * zc).astype(BF16)
        dzc = d_y * b_ref[...]
        dz = w_ref[0:1, :] * _shift_rows(dzc, -1) + w_ref[1:2, :] * dzc + w_ref[2:3, :] * _shift_rows(dzc, 1)
        dc_ref[...] = (dz * v).astype(BF16)
        dv_ref[...] = (dz * c).astype(BF16)
        dw_ref[0:1, :] = _colsum(dzc * z_dn)
        dw_ref[1:2, :] = _colsum(dzc * z)
        dw_ref[2:3, :] = _colsum(dzc * z_up)

    def part(k):
        return pl.BlockSpec((n_rows, CONV_COLS), lambda j: (0, k * nb + j))

    col = pl.BlockSpec((n_rows, CONV_COLS), lambda j: (0, j))
    w_spec = pl.BlockSpec((3, CONV_COLS), lambda j: (0, j))
    return pl.pallas_call(
        body, name=name, grid=(nb,), in_specs=[col, part(0), part(1), part(2), w_spec],
        out_specs=[col, col, col, w_spec],
        out_shape=[SDS((n_rows, D_MODEL), BF16)] * 3 + [SDS((3, D_MODEL), F32)], compiler_params=_cparams(),
    )(dy, z3, z3, z3, conv_w)


def _silu_rows(name, x):
    def body(x_ref, s_ref, d_ref):
        xx = x_ref[...]
        sg = jax.nn.sigmoid(xx)
        s_ref[...] = (xx * sg).astype(BF16)
        d_ref[...] = sg * (1.0 + xx * (1.0 - sg))

    return pl.pallas_call(body, name=name, out_shape=[SDS(x.shape, BF16), SDS(x.shape, F32)])(x)


def _sum_rows(name, x, scale=None):
    r, n = x.shape
    tn = _tile(n, 32768, 128)

    def body(*refs):
        acc = jnp.sum(refs[0][...].astype(F32), axis=0, keepdims=True)
        if scale is not None:
            acc = acc * refs[1][...]
        refs[-1][...] = acc

    in_specs = [pl.BlockSpec((r, tn), lambda j: (0, j))]
    args = [x]
    if scale is not None:
        in_specs.append(pl.BlockSpec((1, tn), lambda j: (0, j)))
        args.append(scale)
    return pl.pallas_call(body, name=name, grid=(n // tn,), in_specs=in_specs,
                          out_specs=pl.BlockSpec((1, tn), lambda j: (0, j)), out_shape=SDS((1, n), F32))(*args)


def _me_operand(me):
    return jnp.reshape(me, (1,)).astype(jnp.int32)


def _sum_slots(name, slots, src, me):
    n_slots, r, c = slots.shape
    tr = _tile(r, 432, 16)

    def body(me_ref, own_ref, x_ref, o_ref):
        acc = own_ref[...].astype(F32)
        for sl in range(n_slots):
            acc = acc + x_ref[sl].astype(F32)
        o_ref[...] = acc

    grid_spec = pltpu.PrefetchScalarGridSpec(
        num_scalar_prefetch=1, grid=(r // tr,),
        in_specs=[pl.BlockSpec((None, tr, c), lambda i, me_ref: (me_ref[0], i, 0)),
                  pl.BlockSpec((n_slots, tr, c), lambda i, me_ref: (0, i, 0))],
        out_specs=pl.BlockSpec((tr, c), lambda i, me_ref: (i, 0)))
    return pl.pallas_call(body, name=name, grid_spec=grid_spec, out_shape=SDS((r, c), F32),
                          compiler_params=_cparams())(_me_operand(me), src, slots)


def _adamw(name, w, g, m, v):
    shape = w.shape
    cols = shape[-1]
    rows = w.size // cols
    tr = _tile(rows, 512, 8)
    bc1 = 1.0 - ADAM_B1 ** ADAM_STEP
    bc2 = 1.0 - ADAM_B2 ** ADAM_STEP

    def body(w_ref, g_ref, m_ref, v_ref, d_ref, nm_ref, nv_ref):
        gg = g_ref[...]
        nm = ADAM_B1 * m_ref[...] + (1.0 - ADAM_B1) * gg
        nv = ADAM_B2 * v_ref[...] + (1.0 - ADAM_B2) * (gg * gg)
        nm_ref[...] = nm
        nv_ref[...] = nv
        d_ref[...] = -ADAM_LR * ((nm / bc1) / (jnp.sqrt(nv / bc2) + ADAM_EPS) + ADAM_WD * w_ref[...])

    spec = pl.BlockSpec((tr, cols), lambda i: (i, 0))
    outs = pl.pallas_call(body, name=name, grid=(rows // tr,), in_specs=[spec] * 4, out_specs=[spec] * 3,
                          out_shape=[SDS((rows, cols), F32)] * 3, compiler_params=_cparams())(
        w.reshape(rows, cols), g.reshape(rows, cols), m.reshape(rows, cols), v.reshape(rows, cols))
    return tuple(t.reshape(shape) for t in outs)


def _exchange(name, x, scatter, after=None):
    blk = x.shape[1:] if scatter else x.shape
    extra = [] if after is None else [after]

    def body(x_ref, *rest):
        out_ref, send_sems, recv_sems, local_sem = rest[len(extra):]
        mx, my, mc = lax.axis_index("x"), lax.axis_index("y"), lax.axis_index("c")
        me = 4 * mx + 2 * my + mc
        own = pltpu.make_async_copy(x_ref.at[me] if scatter else x_ref, out_ref.at[me], local_sem)
        own.start()
        copies = []
        for kk in range(1, N_DEV):
            px = jnp.bitwise_xor(mx, (kk >> 2) & 1)
            py = jnp.bitwise_xor(my, (kk >> 1) & 1)
            pc = jnp.bitwise_xor(mc, kk & 1)
            peer = 4 * px + 2 * py + pc
            send = pltpu.make_async_remote_copy(
                src_ref=x_ref.at[peer] if scatter else x_ref, dst_ref=out_ref.at[me],
                send_sem=send_sems.at[kk - 1], recv_sem=recv_sems.at[kk - 1],
                device_id=(px, py, pc), device_id_type=MESH)
            send.start()
            arrival = pltpu.make_async_remote_copy(
                src_ref=x_ref.at[peer] if scatter else x_ref, dst_ref=out_ref.at[peer],
                send_sem=send_sems.at[kk - 1], recv_sem=recv_sems.at[kk - 1],
                device_id=(px, py, pc), device_id_type=MESH)
            copies.append((send, arrival))
        for send, arrival in copies:
            arrival.wait_recv()
            send.wait_send()
        own.wait()

    return pl.pallas_call(
        body, name=name, out_shape=SDS((N_DEV,) + tuple(blk), x.dtype),
        in_specs=[pl.BlockSpec(memory_space=pl.ANY)] * (1 + len(extra)), out_specs=pl.BlockSpec(memory_space=pl.ANY),
        scratch_shapes=[pltpu.SemaphoreType.DMA((N_DEV - 1,)), pltpu.SemaphoreType.DMA((N_DEV - 1,)),
                        pltpu.SemaphoreType.DMA],
    )(x, *extra)


def _rope_perm(pre, reps, post):
    half = QK_ROPE // 4
    width = reps * (pre + QK_ROPE) + post
    p = np.zeros((width, width), np.float32)
    for rep in range(reps):
        s0 = rep * (pre + QK_ROPE) + pre
        for base in (s0, s0 + 2 * half):
            for i in range(half):
                p[base + half + i, base + i] = -1.0
                p[base + i, base + half + i] = 1.0
    return p


def _rope_layout(pre, reps, post):
    width = reps * (pre + QK_ROPE) + post
    expand = np.zeros((QK_ROPE, width), np.float32)
    plain = np.ones((1, width), np.float32)
    for rep in range(reps):
        s0 = rep * (pre + QK_ROPE) + pre
        expand[np.arange(QK_ROPE), s0 + np.arange(QK_ROPE)] = 1.0
        plain[0, s0:s0 + QK_ROPE] = 0.0
    return jnp.asarray(expand, BF16), jnp.asarray(plain, F32), jnp.asarray(_rope_perm(pre, reps, post), BF16)


def _head_spread():
    spread = np.zeros((HEADS * QK_HEAD, HEADS * HEAD_PAD), np.float32)
    for hh in range(HEADS):
        spread[hh * QK_HEAD + np.arange(QK_HEAD), hh * HEAD_PAD + np.arange(QK_HEAD)] = 1.0
    return jnp.asarray(spread, BF16)


def _rope_factors(n_lat, t_rows):
    half = QK_ROPE // 4
    pos = jnp.arange(n_lat)
    freqs = jnp.power(ROPE_THETA, -jnp.arange(0, 2 * half, 2, dtype=F32) / (2 * half))
    ang_r = (pos // GRID_W).astype(F32)[:, None] * freqs
    ang_c = (pos % GRID_W).astype(F32)[:, None] * freqs
    ang = jnp.concatenate([ang_r, ang_r, ang_c, ang_c], axis=-1)
    rest = t_rows - n_lat
    return (jnp.concatenate([jnp.cos(ang), jnp.ones((rest, QK_ROPE), F32)], axis=0),
            jnp.concatenate([jnp.sin(ang), jnp.zeros((rest, QK_ROPE), F32)], axis=0))


def _ffn_half_fwd(tag, s, mg, k, feed, i, coef, n_lat):
    wg_t, wu_t = feed.weights(f"{tag}_up", [f"gate_t{i}", f"up_t{i}"], s)
    u, a, b, hid = _ffn_up(f"{tag}_up", s, mg, k, n_lat, wg_t, wu_t)
    (wd,) = feed.weights(f"{tag}_down", [f"down{i}"], hid)
    s_out, o = _mm_resid(f"{tag}_down", [(hid, wd)], s, mg, k, coef, n_lat)
    return s_out, (s, u, a, b, hid, o, wg_t, wu_t, wd)


def _ffn_half_bwd(tag, ds_out, saved, mg, k, feed, i, coef, n_lat, out_rows=None):
    s, u, a, b, hid, o, wg_t, wu_t, wd = saved
    do, da, db, dgate = _ffn_dact(f"{tag}_dact", ds_out, o, mg, k, coef, n_lat, wd, a, b)
    dwd = _mm(f"{tag}_dwd", [(hid, do)], "tn", BF16)
    dwg_t, dwu_t = _dw_pair(f"{tag}_dwgu", da, db, u)
    token = feed.grads(tag, {f"down{i}": dwd, f"gate_t{i}": dwg_t, f"up_t{i}": dwu_t})
    ds_in, (dshift, dscale, dgain) = _du_adaln(f"{tag}_du", [(da, wg_t), (db, wu_t)], s, ds_out, mg, k, n_lat,
                                               _after(token), out_rows)
    return ds_in, dict(shift=dshift, scale=dscale, gate=dgate, gain=dgain)


def _after(token):
    return jnp.zeros((1, D_MODEL), F32) + token


def _mod_grad(parts, n_groups):
    rows = []
    zero = jnp.zeros((n_groups, 1, D_MODEL), F32)
    for k in range(3):
        for nm in ("shift", "scale", "gate"):
            t = parts[k].get(nm, zero)
            if t.shape[0] < n_groups:
                t = jnp.concatenate([t, jnp.zeros((n_groups - t.shape[0], 1, D_MODEL), F32)], axis=0)
            rows.append(t)
    return jnp.concatenate(rows, axis=1).reshape(n_groups, N_MOD * D_MODEL)


def _local_step(x, ctx, target, mod_h, mod_g, norm_g, feed, pool_w, pool_scale, q_norm_g, kv_norm_g, conv_w,
                final_norm_g):
    n_lat, n_ctx = x.shape[0], ctx.shape[0]
    t_all = n_lat + n_ctx
    mg0 = jnp.stack([jnp.concatenate([mod_h[0], norm_g[0]], axis=0), jnp.concatenate([mod_g, norm_g[0]], axis=0)])
    mg1 = jnp.concatenate([mod_h[1], norm_g[1]], axis=0)[None]

    s0 = jnp.concatenate([x, ctx], axis=0)
    s1, sv_f00 = _ffn_half_fwd("l0f0", s0, mg0, 0, feed, 0, 0.5, n_lat)

    (w_in,) = feed.weights("l0m_in", ["in_t"], s1)
    kv_rows = KV_RANK + QK_ROPE
    w_in_t = jnp.concatenate([
        w_in[:POOL_DIM], jnp.zeros((PA_CQ - POOL_DIM, D_MODEL), BF16), w_in[POOL_DIM:POOL_DIM + Q_RANK],
        w_in[POOL_DIM + Q_RANK:], jnp.zeros((PA_KV_W - kv_rows, D_MODEL), BF16)], axis=0)
    ua, proj = _adaln_mm("l0m_proj", s1, mg0, 1, n_lat, w_in_t)
    w_uq, w_ukv_t, w_ab_out = feed.weights("l0m_rest", ["uq", "ukv_t", "ab_out"], proj)
    pool_y, pool_p = _pool_fwd("l0m_pool", proj, n_lat, pool_w.astype(BF16), pool_scale)
    nq = _rmsnorm_fwd("l0m_qnorm", proj, Q_RANK, PA_CQ // Q_RANK, q_norm_g, n_lat)
    q_lin = _mm("l0m_q", [(nq, w_uq)], "nn", F32, 512, 768)
    cos32, sin32 = _rope_factors(n_lat, t_all)
    lay_q, lay_k = _rope_layout(QK_NOPE, HEADS, 0), _rope_layout(KV_RANK, 1, PA_KV_W - kv_rows)
    spread = _head_spread()
    q_flat = _rope("l0m_qrope", q_lin, Q_RANK, 0, cos32[:n_lat], sin32[:n_lat], lay_q, False, BF16, spread)
    kvr = _rope("l0m_krope", proj, PA_KV_W, PA_KV // PA_KV_W, cos32, sin32, lay_k, False, F32)
    nkv = _rmsnorm_fwd("l0m_kvnorm", kvr, KV_RANK, 0, kv_norm_g, t_all)
    kv = _mm("l0m_kv", [(nkv, w_ukv_t)], "nt", BF16, 768, 512)
    k_rope = jnp.pad(kvr[:, KV_RANK:KV_RANK + QK_ROPE].astype(BF16), ((0, 0), (QK_NOPE, HEAD_PAD - QK_HEAD)))
    o_flat, lse = _attn_fwd("l0m_attn", q_flat, kv, k_rope, n_lat)
    w_o_pad = jnp.pad(w_ab_out[POOL_DIM:].reshape(HEADS, V_HEAD, D_MODEL),
                      ((0, 0), (HEAD_PAD - V_HEAD, 0), (0, 0))).reshape(HEADS * HEAD_PAD, D_MODEL)
    w_o_pool = w_ab_out[:POOL_DIM]
    h2, mix_o = _mm_resid("l0m_out", [(pool_y, w_o_pool), (o_flat, w_o_pad)], s1, mg0[:1], 1, 1.0, n_lat)

    h3, sv_f01 = _ffn_half_fwd("l0f1", h2, mg0[:1], 2, feed, 1, 0.5, n_lat)

    h4, sv_f10 = _ffn_half_fwd("l1f0", h3, mg1, 0, feed, 2, 0.5, n_lat)
    w_cin_t, w_c_out = feed.weights("l1m", ["cin_t", "c_out"], h4)
    uc, z3 = _adaln_mm("l1m_in", h4, mg1, 1, n_lat, w_cin_t)
    yc = _conv_fwd("l1m_conv", z3, conv_w)
    h5, conv_o = _mm_resid("l1m_out", [(yc, w_c_out)], h4, mg1, 1, 1.0, n_lat)
    h6, sv_f11 = _ffn_half_fwd("l1f1", h5, mg1, 2, feed, 3, 0.5, n_lat)

    dh6, sq_cols, d_final_g = _final_loss("loss_head", h6, target, final_norm_g)
    g = {}
    dh5, g["f11"] = _ffn_half_bwd("l1f1", dh6, sv_f11, mg1, 2, feed, 3, 0.5, n_lat)

    do_c, dyc, dgate_c = _gate_mm("l1m_dy", dh5, conv_o, mg1, 1, 1.0, n_lat, w_c_out)
    d_c_out = _mm("l1m_dwout", [(yc, do_c)], "tn", BF16)
    db_, dc_, dv_, d_conv_w = _conv_bwd("l1m_dconv", dyc, z3, conv_w)
    dz3 = jnp.concatenate([db_, dc_, dv_], axis=-1)
    d_cin_t = _mm("l1m_dwin", [(dz3, uc)], "tn", BF16)
    token = feed.grads("l1m", {"c_out": d_c_out, "cin_t": d_cin_t})
    dh4, (dsh_c, dsc_c, dgn_c) = _du_adaln("l1m_du", [(dz3, w_cin_t)], h4, dh5, mg1, 1, n_lat, _after(token))
    dh3, g["f10"] = _ffn_half_bwd("l1f0", dh4, sv_f10, mg1, 0, feed, 2, 0.5, n_lat)

    dh2, g["f01"] = _ffn_half_bwd("l0f1", dh3, sv_f01, mg0[:1], 2, feed, 1, 0.5, n_lat)

    w_back = jnp.concatenate([w_o_pool, w_o_pad], axis=0)
    do_a, dcat, dgate_a = _gate_mm("l0m_dcat", dh2, mix_o, mg0[:1], 1, 1.0, n_lat, w_back)
    d_o_pad = _mm("l0m_dwout_a", [(o_flat, do_a)], "tn", BF16)
    d_ab_out = jnp.concatenate([
        _mm("l0m_dwout_p", [(pool_y, do_a)], "tn", BF16),
        d_o_pad.reshape(HEADS, HEAD_PAD, D_MODEL)[:, HEAD_PAD - V_HEAD:].reshape(HEADS * V_HEAD, D_MODEL)], axis=0)
    dproj = jnp.zeros((t_all, PA_W), BF16)
    dproj, d_pool_w, d_pool_scale = _pool_bwd("l0m_dpool", dcat, n_lat, pool_p, pool_w.astype(BF16), pool_scale, dproj)
    dq_flat, dkv, dk_rope = _attn_bwd("l0m_dattn", q_flat, kv, k_rope, o_flat, lse, dcat, POOL_DIM // HEAD_PAD, n_lat)
    dq_lin = _rope("l0m_dqrope", dq_flat, Q_RANK, 0, cos32[:n_lat], sin32[:n_lat], lay_q, True, BF16, spread)
    d_uq = _mm("l0m_dwuq", [(nq, dq_lin)], "tn", BF16, 768, 768)
    dnq = _mm("l0m_dnq", [(dq_lin, w_uq)], "nt", F32, 512, 768)
    dproj, d_q_norm_g = _rmsnorm_bwd("l0m_dqnorm", proj, Q_RANK, PA_CQ // Q_RANK, dnq, q_norm_g, n_lat, BF16,
                                     (dproj, PA_CQ // Q_RANK))
    dnkv = _mm("l0m_dnkv", [(dkv, w_ukv_t)], "nn", F32, 768, 256)
    d_ukv_t = _mm("l0m_dwukv", [(dkv, nkv)], "tn", BF16, 512, 256)
    dckv, d_kv_norm_g = _rmsnorm_bwd("l0m_dkvnorm", kvr, KV_RANK, 0, dnkv, kv_norm_g, t_all)
    dkvr = jnp.concatenate([dckv, dk_rope[:, QK_NOPE:QK_HEAD],
                            jnp.zeros((t_all, PA_KV_W - KV_RANK - QK_ROPE), F32)], axis=-1)
    dproj = _rope("l0m_dkrope", dkvr, PA_KV_W, 0, cos32, sin32, lay_k, True, BF16, None, (dproj, PA_KV // PA_KV_W))
    d_in_pad = _mm("l0m_dwin", [(dproj, ua)], "tn", BF16, 640, 512)
    d_in_t = jnp.concatenate([d_in_pad[:POOL_DIM], d_in_pad[PA_CQ:PA_CQ + Q_RANK],
                              d_in_pad[PA_KV:PA_KV + kv_rows]], axis=0)
    token = feed.grads("l0m", {"ab_out": d_ab_out, "uq": d_uq, "ukv_t": d_ukv_t, "in_t": d_in_t})
    ds1, (dsh_a, dsc_a, dgn_a) = _du_adaln("l0m_du", [(dproj, w_in_t)], s1, dh2, mg0, 1, n_lat, _after(token))
    grad_x, g["f00"] = _ffn_half_bwd("l0f0", ds1, sv_f00, mg0, 0, feed, 0, 0.5, n_lat, out_rows=n_lat)

    dmod0 = _mod_grad([g["f00"], dict(shift=dsh_a, scale=dsc_a, gate=dgate_a), g["f01"]], 2)
    dmod1 = _mod_grad([g["f10"], dict(shift=dsh_c, scale=dsc_c, gate=dgate_c), g["f11"]], 1)
    d_norm_g = jnp.stack([
        jnp.concatenate([jnp.sum(g["f00"]["gain"], axis=0), jnp.sum(dgn_a, axis=0), g["f01"]["gain"][0]], axis=0),
        jnp.concatenate([g["f10"]["gain"][0], dgn_c[0], g["f11"]["gain"][0]], axis=0)])
    grads = dict(
        pool_w=d_pool_w, pool_scale=d_pool_scale, q_norm_g=d_q_norm_g[0], kv_norm_g=d_kv_norm_g[0],
        conv_w=d_conv_w, final_norm_g=d_final_g[0], norm_g=d_norm_g,
        mod_h=jnp.stack([dmod0[0], dmod1[0]]), mod_g=dmod0[1])
    return sq_cols, grad_x, grads


HBM_SPEC = pl.BlockSpec(memory_space=pltpu.HBM)
SEM_SPEC = pl.BlockSpec(memory_space=pltpu.SEMAPHORE)
ANY_SPEC = pl.BlockSpec(memory_space=pl.ANY)
SIDE_EFFECT = pltpu.SideEffectType.DATAFLOW_SIDE_EFFECTING
N_PEERS = N_DEV - 1


def _mesh_place():
    mx, my, mc = lax.axis_index("x"), lax.axis_index("y"), lax.axis_index("c")
    return mx, my, mc, 4 * mx + 2 * my + mc


def _peer(place, kk):
    mx, my, mc, _ = place
    px = jnp.bitwise_xor(mx, (kk >> 2) & 1)
    py = jnp.bitwise_xor(my, (kk >> 1) & 1)
    pc = jnp.bitwise_xor(mc, kk & 1)
    return (px, py, pc), 4 * px + 2 * py + pc


def _hbm(a):
    return pltpu.with_memory_space_constraint(a, pltpu.HBM)


def _landing(block, me):
    zone = lax.empty((N_DEV,) + block.shape, block.dtype)
    return lax.dynamic_update_slice(zone, block[None], (me,) + (0,) * block.ndim)


ALL_PEERS = tuple(range(1, N_DEV))
SIBLING = 1
CHIP_PEERS = (2, 4, 6)
RELAYED = (3, 5, 7)


def _exchange_start(name, srcs, lands, scatter, after, peers=ALL_PEERS):
    n = len(srcs)
    extra = [] if after is None else [after]

    def body(*refs):
        src, land = refs[:n], refs[n:2 * n]
        send_sems, recv_sems, token = refs[2 * n + len(extra)], refs[2 * n + len(extra) + 1], refs[-1]
        place = _mesh_place()
        for a in range(n):
            for kk in peers:
                dev, peer = _peer(place, kk)
                pltpu.make_async_remote_copy(
                    src_ref=src[a].at[peer] if scatter else src[a],
                    dst_ref=land[a].at[kk - 1] if scatter else land[a].at[place[3]],
                    send_sem=send_sems.at[a * N_PEERS + kk - 1], recv_sem=recv_sems.at[a * N_PEERS + kk - 1],
                    device_id=dev, device_id_type=MESH).start()
        token[...] = jnp.zeros_like(token)

    thru = [pltpu.HBM(t.shape, t.dtype) for t in (*srcs, *lands)]
    res = pl.pallas_call(
        body, name=name,
        out_shape=(pltpu.SemaphoreType.DMA((n * N_PEERS,)), pltpu.SemaphoreType.DMA((n * N_PEERS,)), *thru,
                   SDS((8, 128), F32)),
        in_specs=[HBM_SPEC] * (2 * n) + [ANY_SPEC] * len(extra),
        out_specs=(SEM_SPEC, SEM_SPEC, *([HBM_SPEC] * (2 * n)), pl.BlockSpec(memory_space=pltpu.VMEM)),
        input_output_aliases={i: 2 + i for i in range(2 * n)},
        compiler_params=pltpu.CompilerParams(has_side_effects=SIDE_EFFECT),
    )(*[_hbm(s) for s in srcs], *[_hbm(t) for t in lands], *extra)
    return res[0], res[1], list(res[2:2 + n]), list(res[2 + n:2 + 2 * n]), res[-1]


def _exchange_wait(name, send_sems, recv_sems, srcs, lands, places, scatter, after):
    n = len(srcs)

    def body(*refs):
        src, land = refs[:n], refs[n:2 * n]
        send, recv = refs[2 * n], refs[2 * n + 1]
        place = _mesh_place()
        for a in range(n):
            for kk in range(1, N_DEV):
                dev, peer = _peer(place, kk)
                cp = pltpu.make_async_remote_copy(
                    src_ref=src[a].at[peer] if scatter else src[a],
                    dst_ref=land[a].at[kk - 1] if scatter else land[a].at[peer],
                    send_sem=send.at[places[a] * N_PEERS + kk - 1], recv_sem=recv.at[places[a] * N_PEERS + kk - 1],
                    device_id=dev, device_id_type=MESH)
                cp.wait_send()
                cp.wait_recv()

    thru = [pltpu.HBM(t.shape, t.dtype) for t in (*srcs, *lands)]
    res = pl.pallas_call(
        body, name=name, out_shape=tuple(thru),
        in_specs=[HBM_SPEC] * (2 * n) + [SEM_SPEC, SEM_SPEC] + [ANY_SPEC] * len(after),
        out_specs=tuple([HBM_SPEC] * (2 * n)), input_output_aliases={i: i for i in range(2 * n)},
        compiler_params=pltpu.CompilerParams(has_side_effects=SIDE_EFFECT),
    )(*srcs, *lands, send_sems, recv_sems, *after)
    return list(res[:n]), list(res[n:])


def _gather_relay(name, send1, recv1, lands, places, after):
    n = len(lands)

    def body(*refs):
        land, s1, r1 = refs[:n], refs[n], refs[n + 1]
        s2, r2 = refs[n + 3], refs[n + 4]
        place = _mesh_place()
        sibling = _peer(place, SIBLING)[0]
        for a in range(n):
            for j, kk in enumerate(CHIP_PEERS):
                dev, origin = _peer(place, kk)
                block = land[a].at[origin]
                pltpu.make_async_remote_copy(
                    src_ref=block, dst_ref=block, send_sem=s1.at[places[a] * N_PEERS + kk - 1],
                    recv_sem=r1.at[places[a] * N_PEERS + kk - 1], device_id=dev, device_id_type=MESH).wait_recv()
                pltpu.make_async_remote_copy(
                    src_ref=block, dst_ref=block, send_sem=s2.at[a * 3 + j], recv_sem=r2.at[a * 3 + j],
                    device_id=sibling, device_id_type=MESH).start()

    res = pl.pallas_call(
        body, name=name,
        out_shape=(pltpu.SemaphoreType.DMA((3 * n,)), pltpu.SemaphoreType.DMA((3 * n,)),
                   *[pltpu.HBM(t.shape, t.dtype) for t in lands]),
        in_specs=[HBM_SPEC] * n + [SEM_SPEC, SEM_SPEC, ANY_SPEC],
        out_specs=(SEM_SPEC, SEM_SPEC, *([HBM_SPEC] * n)),
        input_output_aliases={i: 2 + i for i in range(n)},
        compiler_params=pltpu.CompilerParams(has_side_effects=SIDE_EFFECT),
    )(*lands, send1, recv1, after)
    return res[0], res[1], list(res[2:])


def _gather_wait(name, send1, recv1, send2, recv2, srcs, lands, places, after):
    n = len(lands)

    def body(*refs):
        src, land = refs[:n], refs[n:2 * n]
        s1, r1, s2, r2 = refs[2 * n:2 * n + 4]
        place = _mesh_place()
        for a in range(n):
            for kk in (SIBLING,) + CHIP_PEERS:
                dev, origin = _peer(place, kk)
                first = pltpu.make_async_remote_copy(
                    src_ref=src[a], dst_ref=land[a].at[origin], send_sem=s1.at[places[a] * N_PEERS + kk - 1],
                    recv_sem=r1.at[places[a] * N_PEERS + kk - 1], device_id=dev, device_id_type=MESH)
                first.wait_send()
                if kk == SIBLING:
                    first.wait_recv()
            for j, kk in enumerate(CHIP_PEERS):
                dev, origin = _peer(place, kk + 1)
                relay = pltpu.make_async_remote_copy(
                    src_ref=src[a], dst_ref=land[a].at[origin], send_sem=s2.at[a * 3 + j], recv_sem=r2.at[a * 3 + j],
                    device_id=dev, device_id_type=MESH)
                relay.wait_send()
                relay.wait_recv()

    arrays = (*srcs, *lands)
    res = pl.pallas_call(
        body, name=name, out_shape=tuple(pltpu.HBM(t.shape, t.dtype) for t in arrays),
        in_specs=[HBM_SPEC] * (2 * n) + [SEM_SPEC] * 4 + [ANY_SPEC], out_specs=tuple([HBM_SPEC] * (2 * n)),
        input_output_aliases={i: i for i in range(2 * n)},
        compiler_params=pltpu.CompilerParams(has_side_effects=SIDE_EFFECT),
    )(*arrays, send1, recv1, send2, recv2, after)
    return list(res[n:])


class _Feed:
    def __init__(self, shards, groups, me):
        self.shards, self.groups, self.me, self.pos = shards, groups, me, 0
        self.sems, self.srcs, self.lands = {}, {}, {}
        self.relays = {}
        self.pending = []

    def start(self, tag, names, after):
        srcs = [self.shards[nm] for nm in names]
        lands = [_landing(s, self.me) for s in srcs]
        send, recv, srcs, lands, self.token = _exchange_start(
            f"gather_start_{tag}", srcs, lands, False, after, (SIBLING,) + CHIP_PEERS)
        for i, nm in enumerate(names):
            self.sems[nm], self.srcs[nm], self.lands[nm] = (send, recv, i), srcs[i], lands[i]
        return self.token

    def _relay(self, gi, after):
        names = self.groups[gi]
        if gi not in self.relays:
            send, recv, _ = self.sems[names[0]]
            places = [self.sems[nm][2] for nm in names]
            send2, recv2, lands = _gather_relay(f"gather_relay_{gi}", send, recv, [self.lands[nm] for nm in names],
                                                places, after)
            for nm, t in zip(names, lands):
                self.lands[nm] = t
            self.relays[gi] = (send2, recv2)
            after = lands[0]
        return after

    def weights(self, tag, names, after):
        gi = self.pos
        assert names == self.groups[gi], (names, self.groups[gi])
        if gi == 0:
            after = self.token
        self._relay(gi, after)
        if 1 <= gi < len(self.groups) - 1:
            after = self._relay(gi + 1, after)
        send2, recv2 = self.relays[gi]
        send, recv, _ = self.sems[names[0]]
        got = _gather_wait(f"gather_wait_{tag}", send, recv, send2, recv2, [self.srcs[nm] for nm in names],
                           [self.lands[nm] for nm in names], [self.sems[nm][2] for nm in names], after)
        self.pos += 1
        return [t.reshape((N_DEV * t.shape[1],) + t.shape[2:]) for t in got]

    def grads(self, tag, full):
        names = list(full)
        srcs = [full[nm].reshape((N_DEV, full[nm].shape[0] // N_DEV) + full[nm].shape[1:]) for nm in names]
        lands = [lax.empty((N_PEERS,) + s.shape[1:], s.dtype) for s in srcs]
        send, recv, srcs, lands, token = _exchange_start(f"scatter_start_{tag}", srcs, lands, True, None)
        self.pending.append((tag, names, send, recv, srcs, lands))
        return token[0, 0]

    def collect(self, tags, after, keep_slots=()):
        out = {}
        for tag, names, send, recv, srcs, lands in self.pending:
            if tag not in tags:
                continue
            srcs, got = _exchange_wait(f"scatter_wait_{tag}", send, recv, srcs, lands, list(range(len(names))), True,
                                       after)
            for nm, slots, src in zip(names, got, srcs):
                out[nm] = ((slots, src) if nm.startswith(tuple(keep_slots))
                           else _sum_slots(f"reduce_{nm}", slots, src, self.me))
        return out


def _adamw_math(w, gg, m, v):
    nm = ADAM_B1 * m + (1.0 - ADAM_B1) * gg
    nv = ADAM_B2 * v + (1.0 - ADAM_B2) * (gg * gg)
    bc1 = 1.0 - ADAM_B1 ** ADAM_STEP
    bc2 = 1.0 - ADAM_B2 ** ADAM_STEP
    return -ADAM_LR * ((nm / bc1) / (jnp.sqrt(nv / bc2) + ADAM_EPS) + ADAM_WD * w), nm, nv


def _adamw_part(name, i, w, scattered, me, m, v, prev):
    n_parts, rows, cols = w.shape
    tr = _tile(rows, 256, 16)
    if prev is None:
        prev = tuple(lax.empty(w.shape, F32) for _ in range(4))

    slots, src = scattered

    def body(me_ref, w_ref, g_ref, own_ref, m_ref, v_ref, *rest):
        go_ref, d_ref, nm_ref, nv_ref = rest[4:]
        gg = own_ref[...].astype(F32)
        for sl in range(N_PEERS):
            gg = gg + g_ref[sl].astype(F32)
        d, nm, nv = _adamw_math(w_ref[...], gg, m_ref[...], v_ref[...])
        go_ref[...] = gg
        d_ref[...] = d
        nm_ref[...] = nm
        nv_ref[...] = nv

    part = pl.BlockSpec((None, tr, cols), lambda r, me_ref: (i, r, 0))
    grid_spec = pltpu.PrefetchScalarGridSpec(
        num_scalar_prefetch=1, grid=(rows // tr,),
        in_specs=[part, pl.BlockSpec((N_PEERS, tr, cols), lambda r, me_ref: (0, r, 0)),
                  pl.BlockSpec((None, tr, cols), lambda r, me_ref: (me_ref[0], r, 0)), part, part] + [ANY_SPEC] * 4,
        out_specs=[part] * 4)
    return pl.pallas_call(
        body, name=name, grid_spec=grid_spec, out_shape=[SDS(w.shape, F32)] * 4,
        input_output_aliases={6 + k: k for k in range(4)}, compiler_params=_cparams(),
    )(_me_operand(me), w, slots, src, m, v, *prev)


WEIGHT_NAMES = ("c_ctx", "norm_g", "w_mod", "b_mod", "ffn_w_gate", "ffn_w_up", "ffn_w_down", "ab_w_in", "pool_w",
                "pool_scale", "q_norm_g", "w_uq", "kv_norm_g", "w_ukv", "ab_w_out", "conv_w_in", "conv_w",
                "conv_w_out", "final_norm_g")


def kernel(x, c, ctx, c_ctx, norm_g, w_mod, b_mod, ffn_w_gate, ffn_w_up, ffn_w_down, ab_w_in, pool_w, pool_scale, q_norm_g, w_uq, kv_norm_g, w_ukv, ab_w_out, conv_w_in, conv_w, conv_w_out, final_norm_g, loss_target, m_c_ctx, m_norm_g, m_w_mod, m_b_mod, m_ffn_w_gate, m_ffn_w_up, m_ffn_w_down, m_ab_w_in, m_pool_w, m_pool_scale, m_q_norm_g, m_w_uq, m_kv_norm_g, m_w_ukv, m_ab_w_out, m_conv_w_in, m_conv_w, m_conv_w_out, m_final_norm_g, v_c_ctx, v_norm_g, v_w_mod, v_b_mod, v_ffn_w_gate, v_ffn_w_up, v_ffn_w_down, v_ab_w_in, v_pool_w, v_pool_scale, v_q_norm_g, v_w_uq, v_kv_norm_g, v_w_ukv, v_ab_w_out, v_conv_w_in, v_conv_w, v_conv_w_out, v_final_norm_g):
    weights = (c_ctx, norm_g, w_mod, b_mod, ffn_w_gate, ffn_w_up, ffn_w_down, ab_w_in, pool_w, pool_scale, q_norm_g,
               w_uq, kv_norm_g, w_ukv, ab_w_out, conv_w_in, conv_w, conv_w_out, final_norm_g)
    moms = (m_c_ctx, m_norm_g, m_w_mod, m_b_mod, m_ffn_w_gate, m_ffn_w_up, m_ffn_w_down, m_ab_w_in, m_pool_w,
            m_pool_scale, m_q_norm_g, m_w_uq, m_kv_norm_g, m_w_ukv, m_ab_w_out, m_conv_w_in, m_conv_w, m_conv_w_out,
            m_final_norm_g)
    vels = (v_c_ctx, v_norm_g, v_w_mod, v_b_mod, v_ffn_w_gate, v_ffn_w_up, v_ffn_w_down, v_ab_w_in, v_pool_w,
            v_pool_scale, v_q_norm_g, v_w_uq, v_kv_norm_g, v_w_ukv, v_ab_w_out, v_conv_w_in, v_conv_w, v_conv_w_out,
            v_final_norm_g)
    me = 4 * lax.axis_index("x") + 2 * lax.axis_index("y") + lax.axis_index("c")
    n_lat, n_ctx = x.shape[1], ctx.shape[1]
    d = D_MODEL
    mod_cols = w_mod.shape[-1]
    ng_sh, cw_sh = norm_g.shape[-1], conv_w.shape[-1]

    def ffn_shards(i):
        return {f"gate_t{i}": ffn_w_gate[i // 2, i % 2].T, f"up_t{i}": ffn_w_up[i // 2, i % 2].T,
                f"down{i}": ffn_w_down[i // 2, i % 2]}

    local = {**ffn_shards(0), "in_t": ab_w_in[0].T, "uq": w_uq[0], "ukv_t": w_ukv[0].T, "ab_out": ab_w_out[0],
             **ffn_shards(1), **ffn_shards(2), "cin_t": conv_w_in[0].T, "c_out": conv_w_out[0], **ffn_shards(3)}
    ffn_groups = [[[f"gate_t{i}", f"up_t{i}"], [f"down{i}"]] for i in range(4)]
    groups = [*ffn_groups[0], ["in_t"], ["uq", "ukv_t", "ab_out"], *ffn_groups[1], *ffn_groups[2],
              ["cin_t", "c_out"], *ffn_groups[3]]
    feed = _Feed({nm: a.astype(BF16) for nm, a in local.items()}, groups, me)

    small = jnp.concatenate([c.reshape(-1), norm_g.reshape(-1), conv_w.reshape(-1)])
    small_n = -(-small.shape[0] // 1024) * 1024
    small = jnp.pad(small, (0, small_n - small.shape[0])).reshape(small_n // 128, 128)
    small_all = _exchange("gather_small", small, False).reshape(N_DEV, small_n)
    c_all = small_all[:, :d]
    o1 = d + 6 * ng_sh
    norm_g_full = small_all[:, d:o1].reshape(N_DEV, 2, 3, ng_sh).transpose(1, 2, 0, 3).reshape(2, 3, d)
    conv_w_full = small_all[:, o1:o1 + 3 * cw_sh].reshape(N_DEV, 3, cw_sh).transpose(1, 0, 2).reshape(3, d)

    cond = jnp.concatenate([c_all, jnp.broadcast_to(c_ctx[None, :], (N_DEV, d))], axis=0)
    sil, dsil = _silu_rows("mod_silu", cond)
    w_mod_b = w_mod.astype(BF16)
    b_sh = lax.dynamic_slice(b_mod, (0, me * mod_cols), (2, mod_cols))
    m_part = jnp.stack([_mm(f"mod_fwd{l}", [(sil, w_mod_b[l])], "nn", F32, 16, 384, bias=b_sh[l:l + 1])
                        for l in range(2)], axis=1)
    m_all = _exchange("gather_mod", m_part.reshape(-1, 128), False).reshape(N_DEV, 2 * N_DEV, 2, mod_cols)
    m_mine = lax.dynamic_index_in_dim(m_all, me, axis=1, keepdims=False)
    mod_h = m_mine.transpose(1, 0, 2).reshape(2, N_MOD, d)
    mod_g = m_all[:, N_DEV, 0, :].reshape(N_MOD, d)

    first = feed.start("first", [nm for grp in groups[:3] for nm in grp], m_all)
    feed.start("rest", [nm for grp in groups[3:] for nm in grp], first)

    sq_cols, grad_x, g = _local_step(x[0], ctx[0], loss_target[0], mod_h, mod_g, norm_g_full, feed, pool_w[0],
                                  pool_scale, q_norm_g, kv_norm_g, conv_w_full, final_norm_g)
    w_of, m_of, v_of = (dict(zip(WEIGHT_NAMES, t)) for t in (weights, moms, vels))
    results = {}

    def update(nm, grad, view=lambda t: t):
        outs = _adamw(f"adamw_{nm}", view(w_of[nm]), grad.reshape(view(w_of[nm]).shape), view(m_of[nm]), view(v_of[nm]))
        results[nm] = tuple(view(t) for t in (grad.reshape(view(w_of[nm]).shape), *outs))

    def swap(t):
        return jnp.swapaxes(t, -1, -2)

    stacked = ("gate_t", "up_t", "down")
    early = feed.collect(["l1f1", "l1m", "l1f0", "l0f1", "l0m"], [grad_x], stacked)
    update("ab_w_in", early["in_t"], swap)
    update("w_uq", early["uq"])
    update("w_ukv", early["ukv_t"].T)
    update("ab_w_out", early["ab_out"])
    update("conv_w_in", early["cin_t"].T)
    update("conv_w_out", early["c_out"])
    ffn = {}
    for nm, prefix, view in (("ffn_w_gate", "gate_t", swap), ("ffn_w_up", "up_t", swap),
                             ("ffn_w_down", "down", lambda t: t)):
        w4, m4, v4 = (view(t).reshape((4,) + view(t).shape[-2:]) for t in (w_of[nm], m_of[nm], v_of[nm]))
        prev = None
        for i in (3, 2, 1):
            prev = _adamw_part(f"adamw_{nm}{i}", i, w4, early[f"{prefix}{i}"], me, m4, v4, prev)
        ffn[nm] = (prefix, view, w4, m4, v4, prev)
    done_early = [results[nm][1] for nm in results] + [state[5][1] for state in ffn.values()]
    late = feed.collect(["l0f0"], done_early, stacked)
    for nm, (prefix, view, w4, m4, v4, prev) in ffn.items():
        outs = _adamw_part(f"adamw_{nm}0", 0, w4, late[f"{prefix}0"], me, m4, v4, prev)
        results[nm] = tuple(view(t.reshape(view(w_of[nm]).shape)) for t in outs)

    dm = jnp.stack([g["mod_h"], jnp.stack([g["mod_g"], jnp.zeros_like(g["mod_g"])])])
    dm_all = _exchange("gather_dmod", dm.reshape(-1, 128), False, results["ffn_w_down"][1]).reshape(N_DEV, 2, 2, N_MOD * d)
    grad_b_mod = _sum_rows("dmod_bias", dm_all.reshape(2 * N_DEV, 2 * N_MOD * d)).reshape(2, N_MOD * d)
    dm_sh = lax.dynamic_slice(dm_all, (0, 0, 0, me * mod_cols), (N_DEV, 2, 2, mod_cols))
    gw_mod, cctx_parts = [], []
    for l in range(2):
        dm_l = dm_sh[:, :, l, :].transpose(1, 0, 2).reshape(2 * N_DEV, mod_cols).astype(BF16)
        gw_mod.append(_mm(f"mod_dw{l}", [(sil, dm_l)], "tn", F32, 512, 384))
        dm_ctx = jnp.concatenate([dm_l[N_DEV:], jnp.zeros((N_DEV, mod_cols), BF16)], axis=0)
        cctx_parts.append(_mm(f"mod_dcond{l}", [(dm_ctx, w_mod_b[l])], "nt", F32, 16, 512))
    cctx_part = _sum_rows("mod_dcond_sum", jnp.concatenate(cctx_parts, axis=0))
    update("w_mod", jnp.stack(gw_mod))
    update("b_mod", grad_b_mod)

    small_g = jnp.concatenate([g["pool_w"].reshape(-1), g["pool_scale"].reshape(-1), g["q_norm_g"].reshape(-1),
                               g["kv_norm_g"].reshape(-1), g["final_norm_g"].reshape(-1), g["norm_g"].reshape(-1),
                               g["conv_w"].reshape(-1), sq_cols.reshape(-1), cctx_part.reshape(-1)])
    sizes = [pool_w.size, pool_scale.size, q_norm_g.size, kv_norm_g.size, d, 6 * d, 3 * d, d, d]
    sg_n = -(-small_g.shape[0] // 1024) * 1024
    small_g = jnp.pad(small_g, (0, sg_n - small_g.shape[0]))
    sg_all = _exchange("gather_small_grads", small_g.reshape(-1, 128), False).reshape(N_DEV, sg_n)
    scale_vec = jnp.concatenate([jnp.ones((1, sum(sizes[:-1])), F32), dsil[N_DEV:N_DEV + 1],
                                 jnp.ones((1, sg_n - sum(sizes)), F32)], axis=1)
    sg = _sum_rows("small_grads_sum", sg_all, scale_vec)[0]
    cuts, pos = [], 0
    for sz in sizes:
        cuts.append(sg[pos:pos + sz])
        pos += sz
    g_pool_w, g_pool_scale, g_q_norm, g_kv_norm, g_final, g_norm_full, g_conv_full, sq_all, g_c_ctx = cuts
    loss = 0.5 * jnp.sum(sq_all) / d
    update("c_ctx", g_c_ctx)
    update("norm_g", lax.dynamic_slice(g_norm_full.reshape(2, 3, d), (0, 0, me * ng_sh), (2, 3, ng_sh)))
    update("conv_w", lax.dynamic_slice(g_conv_full.reshape(3, d), (0, me * cw_sh), (3, cw_sh)))
    update("pool_w", g_pool_w)
    update("pool_scale", g_pool_scale)
    update("q_norm_g", g_q_norm)
    update("kv_norm_g", g_kv_norm)
    update("final_norm_g", g_final)
    outs = [results[nm] for nm in WEIGHT_NAMES]
    return (loss, grad_x[None], *[o[0] for o in outs], *[o[1] for o in outs], *[o[2] for o in outs],
            *[o[3] for o in outs])
```

```python
import functools
import math

import jax
import jax.numpy as jnp
import numpy as np
from jax import lax
from jax.experimental import pallas as pl
from jax.experimental.pallas import tpu as pltpu

F32 = jnp.float32
BF16 = jnp.bfloat16
MESH = pl.DeviceIdType.MESH
SDS = jax.ShapeDtypeStruct

N_DEV = 8
D_MODEL = 1024
N_MOD = 9
D_FF = 2816
POOL_WINDOWS = (2, 4, 8, 16)
POOL_DIM = 512
POOL_GROUP_DIM = 128
HEADS = 8
QK_NOPE = 64
QK_ROPE = 32
QK_HEAD = QK_NOPE + QK_ROPE
V_HEAD = 64
Q_RANK = 768
KV_RANK = 256
GRID_W = 64
ROPE_THETA = 10000.0
RMS_EPS = 1e-6
ATTN_SCALE = 1.0 / math.sqrt(QK_HEAD)
HEAD_PAD = 128
POOL_PAD = 16
ATTN_Q_ROWS = 512
PA_POOL, PA_CQ, PA_KV = 0, 768, 1536
PA_KV_W = 384
PA_W = PA_KV + PA_KV_W

ADAM_LR, ADAM_B1, ADAM_B2, ADAM_EPS, ADAM_WD, ADAM_STEP = 0.001, 0.9, 0.999, 1e-08, 0.01, 10

VMEM_LIMIT_BYTES = 56 * 1024 * 1024

NN = ((1,), (0,))
NT = ((1,), (1,))
TN = ((0,), (0,))


def _cparams():
    return pltpu.CompilerParams(vmem_limit_bytes=VMEM_LIMIT_BYTES)


def _dot(a, b, dims):
    return lax.dot_general(a, b, (dims, ((), ())), preferred_element_type=F32)


def _tile(n, cap, mult=8):
    t = (min(cap, n) // mult) * mult
    while t >= mult:
        if n % t == 0:
            return t
        t -= mult
    return n


def _colsum(x):
    return jnp.sum(x, axis=0, keepdims=True)


def _rms(x):
    r = lax.rsqrt(jnp.mean(x * x, axis=-1, keepdims=True) + RMS_EPS)
    return x * r, r


def _rms_bwd(n, r, dn):
    return r * (dn - n * jnp.mean(dn * n, axis=-1, keepdims=True))


def _rowwise(name, fn, t_rows, tm, n_lat, rows, vecs, outs, accs, into=None):
    nt = t_rows // tm
    nlt = n_lat // tm
    n_groups = 2 if nlt < nt else 1

    def grp(i):
        return jnp.where(i >= nlt, 1, 0) if n_groups == 2 else 0

    in_specs = [pl.BlockSpec((tm, w), functools.partial(lambda i, cb: (i, cb), cb=cb)) for (_, w, cb) in rows]
    in_specs += [pl.BlockSpec((1,) + v.shape[1:], lambda i: (grp(i), 0, 0)) for v in vecs]
    out_specs = [pl.BlockSpec((tm, w), lambda i: (i, 0)) for (w, _) in outs]
    out_specs += [pl.BlockSpec((1, 1, w), lambda i: (grp(i), 0, 0)) for w in accs]
    out_shape = [SDS((t_rows, w), dt) for (w, dt) in outs] + [SDS((n_groups, 1, w), F32) for w in accs]
    n_r, n_v, n_o = len(rows), len(vecs), len(outs)
    extra, aliases = [], {}
    if into is not None:
        extra, aliases = [into[0]], {n_r + n_v: 0}
        in_specs.append(pl.BlockSpec(memory_space=pl.ANY))
        out_specs[0] = pl.BlockSpec((tm, outs[0][0]), lambda i: (i, into[1]))
        out_shape[0] = SDS(into[0].shape, into[0].dtype)
    n_in = n_r + n_v + len(extra)

    def body(*refs):
        row_vals = [r[...] for r in refs[:n_r]]
        vec_vals = [v[0] for v in refs[n_r:n_r + n_v]]
        out_refs = refs[n_in:n_in + n_o]
        acc_refs = refs[n_in + n_o:]
        out_vals, acc_vals = fn(row_vals, vec_vals)
        for o_ref, o in zip(out_refs, out_vals):
            o_ref[...] = o.astype(o_ref.dtype)
        if acc_refs:
            i = pl.program_id(0)
            first = (i == 0) | (i == nlt) if n_groups == 2 else i == 0

            @pl.when(first)
            def _():
                for a_ref, a in zip(acc_refs, acc_vals):
                    a_ref[0] = a

            @pl.when(jnp.logical_not(first))
            def _():
                for a_ref, a in zip(acc_refs, acc_vals):
                    a_ref[0] += a

    res = pl.pallas_call(
        body, name=name, grid=(nt,), in_specs=in_specs, out_specs=out_specs, out_shape=out_shape,
        input_output_aliases=aliases, compiler_params=_cparams(),
    )(*[r[0] for r in rows], *vecs, *extra)
    return res[:n_o], res[n_o:]


RESIDENT_BYTES = 12 * 1024 * 1024


def _mm(name, pairs, mode, out_dtype, tm_cap=256, tn_cap=512, bias=None):
    a0, b0 = pairs[0]
    if mode == "nn":
        m, n, dims = a0.shape[0], b0.shape[1], NN
    elif mode == "nt":
        m, n, dims = a0.shape[0], b0.shape[0], NT
    else:
        m, n, dims = a0.shape[1], b0.shape[1], TN
    b_bytes = sum(b.size * b.dtype.itemsize for _, b in pairs)
    tn = n if b_bytes <= RESIDENT_BYTES else _tile(n, tn_cap, 128)
    tm = _tile(m, tm_cap, 128 if mode == "tn" else 16)

    def a_spec(a):
        if mode == "tn":
            return pl.BlockSpec((a.shape[0], tm), lambda i, j: (0, i))
        return pl.BlockSpec((tm, a.shape[1]), lambda i, j: (i, 0))

    def b_spec(b):
        if mode == "nt":
            return pl.BlockSpec((tn, b.shape[1]), lambda i, j: (j, 0))
        return pl.BlockSpec((b.shape[0], tn), lambda i, j: (0, j))

    in_specs, flat = [], []
    for a, b in pairs:
        in_specs += [a_spec(a), b_spec(b)]
        flat += [a, b]
    if bias is not None:
        in_specs.append(pl.BlockSpec((1, tn), lambda i, j: (0, j)))
        flat.append(bias)
    n_pairs = len(pairs)

    def body(*refs):
        acc = None
        for p in range(n_pairs):
            t = _dot(refs[2 * p][...], refs[2 * p + 1][...], dims)
            acc = t if acc is None else acc + t
        if bias is not None:
            acc = acc + refs[2 * n_pairs][...]
        refs[-1][...] = acc.astype(refs[-1].dtype)

    return pl.pallas_call(
        body, name=name, grid=(m // tm, n // tn), in_specs=in_specs,
        out_specs=pl.BlockSpec((tm, tn), lambda i, j: (i, j)),
        out_shape=SDS((m, n), out_dtype), compiler_params=_cparams(),
    )(*flat)


def _mm_resid(name, pairs, s, mg, k, coef, n_lat):
    t_rows, n = pairs[0][0].shape[0], s.shape[1]
    n_pairs = len(pairs)
    tm = _tile(math.gcd(n_lat, t_rows), 256, 16)
    nlt = n_lat // tm
    n_groups = 2 if nlt < t_rows // tm else 1

    def grp(i):
        return jnp.where(i >= nlt, 1, 0) if n_groups == 2 else 0

    def body(*refs):
        s_ref, mg_ref, so_ref, o_ref = refs[2 * n_pairs:]
        o = _dot(refs[0][...], refs[n_pairs][...], NN)
        for p in range(1, n_pairs):
            o = o + _dot(refs[p][...], refs[n_pairs + p][...], NN)
        gate = mg_ref[0, 3 * k + 2:3 * k + 3, :]
        o_ref[...] = o.astype(BF16)
        so_ref[...] = s_ref[...] + (coef * gate) * o

    row = pl.BlockSpec((tm, n), lambda i: (i, 0))
    return pl.pallas_call(
        body, name=name, grid=(t_rows // tm,),
        in_specs=[pl.BlockSpec((tm, a.shape[1]), lambda i: (i, 0)) for a, _ in pairs]
        + [pl.BlockSpec(b.shape, lambda i: (0, 0)) for _, b in pairs]
        + [row, pl.BlockSpec((1, mg.shape[1], n), lambda i: (grp(i), 0, 0))],
        out_specs=[row, row], out_shape=[SDS((t_rows, n), F32), SDS((t_rows, n), BF16)], compiler_params=_cparams(),
    )(*[a for a, _ in pairs], *[b for _, b in pairs], s, mg)


def _dw_pair(name, a1, a2, b):
    kk, m = a1.shape
    n = b.shape[1]
    tm = _tile(m, 256, 128)

    def body(a1_ref, a2_ref, b_ref, o1_ref, o2_ref):
        bb = b_ref[...]
        o1_ref[...] = _dot(a1_ref[...], bb, TN).astype(BF16)
        o2_ref[...] = _dot(a2_ref[...], bb, TN).astype(BF16)

    col = pl.BlockSpec((kk, tm), lambda i: (0, i))
    out = pl.BlockSpec((tm, n), lambda i: (i, 0))
    return pl.pallas_call(
        body, name=name, grid=(m // tm,), in_specs=[col, col, pl.BlockSpec(b.shape, lambda i: (0, 0))],
        out_specs=[out, out], out_shape=[SDS((m, n), BF16)] * 2, compiler_params=_cparams(),
    )(a1, a2, b)


def _groups(t_rows, tm, n_lat):
    nlt = n_lat // tm
    if nlt < t_rows // tm:
        return 2, (lambda i: jnp.where(i >= nlt, 1, 0)), (lambda i: (i == 0) | (i == nlt))
    return 1, (lambda i: 0), (lambda i: i == 0)


def _accumulate(acc_refs, vals, first):
    @pl.when(first)
    def _():
        for r, v in zip(acc_refs, vals):
            r[0] = v

    @pl.when(jnp.logical_not(first))
    def _():
        for r, v in zip(acc_refs, vals):
            r[0] += v


def _adaln_math(s, m, k):
    n, _ = _rms(s)
    return (n * m[9 + k:10 + k]) * (1.0 + m[3 * k + 1:3 * k + 2]) + m[3 * k:3 * k + 1]


def _ffn_up(name, s, mg, k, n_lat, wg_t, wu_t):
    t_rows, f = s.shape[0], wg_t.shape[0]
    tm = _row_tm(t_rows, n_lat)
    _, grp, _ = _groups(t_rows, tm, n_lat)

    def body(s_ref, mg_ref, wg_ref, wu_ref, u_ref, a_ref, b_ref, h_ref):
        uu = _adaln_math(s_ref[...], mg_ref[0], k).astype(BF16)
        u_ref[...] = uu
        a = _dot(uu, wg_ref[...], NT)
        b = _dot(uu, wu_ref[...], NT)
        sg = jax.nn.sigmoid(a)
        act = a * sg
        a_ref[...] = (b * (sg * (1.0 + a * (1.0 - sg)))).astype(BF16)
        b_ref[...] = act.astype(BF16)
        h_ref[...] = (act * b).astype(BF16)

    w_spec = pl.BlockSpec(wg_t.shape, lambda i: (0, 0))
    o_spec = pl.BlockSpec((tm, f), lambda i: (i, 0))
    row = pl.BlockSpec((tm, s.shape[1]), lambda i: (i, 0))
    return pl.pallas_call(
        body, name=name, grid=(t_rows // tm,),
        in_specs=[row, pl.BlockSpec((1,) + mg.shape[1:], lambda i: (grp(i), 0, 0)), w_spec, w_spec],
        out_specs=[row, o_spec, o_spec, o_spec],
        out_shape=[SDS(s.shape, BF16)] + [SDS((t_rows, f), BF16)] * 3, compiler_params=_cparams(),
    )(s, mg, wg_t, wu_t)


def _ffn_dact(name, ds_out, o, mg, k, coef, n_lat, wd, a, b):
    t_rows, f = ds_out.shape[0], wd.shape[0]
    tm = _row_tm(t_rows, n_lat)
    n_groups, grp, first = _groups(t_rows, tm, n_lat)
    d = ds_out.shape[1]

    def body(ds_ref, o_ref, mg_ref, wd_ref, a_ref, b_ref, do_ref, da_ref, db_ref, dg_ref):
        dd = coef * ds_ref[...]
        do = (dd * mg_ref[0, 3 * k + 2:3 * k + 3, :]).astype(BF16)
        do_ref[...] = do
        _accumulate([dg_ref], [_colsum(dd * o_ref[...].astype(F32))], first(pl.program_id(0)))
        dh = _dot(do, wd_ref[...], NT)
        da_ref[...] = (dh * a_ref[...].astype(F32)).astype(BF16)
        db_ref[...] = (dh * b_ref[...].astype(F32)).astype(BF16)

    row = pl.BlockSpec((tm, d), lambda i: (i, 0))
    t_spec = pl.BlockSpec((tm, f), lambda i: (i, 0))
    return pl.pallas_call(
        body, name=name, grid=(t_rows // tm,),
        in_specs=[row, row, pl.BlockSpec((1,) + mg.shape[1:], lambda i: (grp(i), 0, 0)),
                  pl.BlockSpec(wd.shape, lambda i: (0, 0)), t_spec, t_spec],
        out_specs=[row, t_spec, t_spec, pl.BlockSpec((1, 1, d), lambda i: (grp(i), 0, 0))],
        out_shape=[SDS((t_rows, d), BF16), SDS((t_rows, f), BF16), SDS((t_rows, f), BF16), SDS((n_groups, 1, d), F32)],
        compiler_params=_cparams(),
    )(ds_out, o, mg, wd, a, b)


def _du_adaln(name, pairs, s, ds_out, mg, k, n_lat, after, out_rows=None):
    t_rows, d = s.shape
    tm = _row_tm(t_rows, n_lat)
    n_groups, grp, first = _groups(t_rows, tm, n_lat)
    n_pairs = len(pairs)
    nt, n_ds, n_out = t_rows // tm, ds_out.shape[0] // tm, (out_rows or t_rows) // tm

    def body(*refs):
        s_ref, ds_ref, mg_ref, z_ref, out_ref, dsh_ref, dsc_ref, dgn_ref = refs[2 * n_pairs:]
        i = pl.program_id(0)
        d_u = z_ref[...]
        for p in range(n_pairs):
            d_u = d_u + _dot(refs[p][...], refs[n_pairs + p][...], NN)
        m = mg_ref[0]
        gain, scale = m[9 + k:10 + k], m[3 * k + 1:3 * k + 2]
        n, r = _rms(s_ref[...])
        dxn = d_u * (1.0 + scale)
        ds_in = _rms_bwd(n, r, dxn * gain)
        ds_in = ds_in + (ds_ref[...] if n_ds == nt else jnp.where(i < n_ds, ds_ref[...], 0.0))
        if n_out == nt:
            out_ref[...] = ds_in
        else:
            @pl.when(i < n_out)
            def _():
                out_ref[...] = ds_in
        _accumulate([dsh_ref, dsc_ref, dgn_ref], [_colsum(d_u), _colsum(d_u * (n * gain)), _colsum(dxn * n)], first(i))

    row = pl.BlockSpec((tm, d), lambda i: (i, 0))
    acc = pl.BlockSpec((1, 1, d), lambda i: (grp(i), 0, 0))
    res = pl.pallas_call(
        body, name=name, grid=(t_rows // tm,),
        in_specs=[pl.BlockSpec((tm, a.shape[1]), lambda i: (i, 0)) for a, _ in pairs]
        + [pl.BlockSpec(w.shape, lambda i: (0, 0)) for _, w in pairs]
        + [row, pl.BlockSpec((tm, d), lambda i: (jnp.minimum(i, n_ds - 1), 0)),
           pl.BlockSpec((1,) + mg.shape[1:], lambda i: (grp(i), 0, 0)), pl.BlockSpec((1, d), lambda i: (0, 0))],
        out_specs=[pl.BlockSpec((tm, d), lambda i: (jnp.minimum(i, n_out - 1), 0)), acc, acc, acc],
        out_shape=[SDS((n_out * tm, d), F32)] + [SDS((n_groups, 1, d), F32)] * 3, compiler_params=_cparams(),
    )(*[a for a, _ in pairs], *[w for _, w in pairs], s, ds_out, mg, after)
    return res[0], res[1:]


def _adaln_mm(name, s, mg, k, n_lat, w_t):
    rows, d = s.shape
    tm = _row_tm(rows, n_lat)
    _, grp, _ = _groups(rows, tm, n_lat)
    n = w_t.shape[0]

    def body(s_ref, mg_ref, w_ref, u_ref, y_ref):
        uu = _adaln_math(s_ref[...], mg_ref[0], k).astype(BF16)
        u_ref[...] = uu
        y_ref[...] = _dot(uu, w_ref[...], NT)

    row = pl.BlockSpec((tm, d), lambda i: (i, 0))
    return pl.pallas_call(
        body, name=name, grid=(rows // tm,),
        in_specs=[row, pl.BlockSpec((1,) + mg.shape[1:], lambda i: (grp(i), 0, 0)), pl.BlockSpec(w_t.shape, lambda i: (0, 0))],
        out_specs=[row, pl.BlockSpec((tm, n), lambda i: (i, 0))],
        out_shape=[SDS((rows, d), BF16), SDS((rows, n), F32)], compiler_params=_cparams(),
    )(s, mg, w_t)


def _gate_mm(name, ds_out, o, mg, k, coef, n_lat, w):
    t_rows, d = ds_out.shape
    tm = _row_tm(t_rows, n_lat)
    n_groups, grp, first = _groups(t_rows, tm, n_lat)
    n = w.shape[0]

    def body(ds_ref, o_ref, mg_ref, w_ref, do_ref, y_ref, dg_ref):
        dd = coef * ds_ref[...]
        do = (dd * mg_ref[0, 3 * k + 2:3 * k + 3, :]).astype(BF16)
        do_ref[...] = do
        _accumulate([dg_ref], [_colsum(dd * o_ref[...].astype(F32))], first(pl.program_id(0)))
        y_ref[...] = _dot(do, w_ref[...], NT)

    row = pl.BlockSpec((tm, d), lambda i: (i, 0))
    return pl.pallas_call(
        body, name=name, grid=(t_rows // tm,),
        in_specs=[row, row, pl.BlockSpec((1,) + mg.shape[1:], lambda i: (grp(i), 0, 0)), pl.BlockSpec(w.shape, lambda i: (0, 0))],
        out_specs=[row, pl.BlockSpec((tm, n), lambda i: (i, 0)), pl.BlockSpec((1, 1, d), lambda i: (grp(i), 0, 0))],
        out_shape=[SDS((t_rows, d), BF16), SDS((t_rows, n), F32), SDS((n_groups, 1, d), F32)],
        compiler_params=_cparams(),
    )(ds_out, o, mg, w)


def _row_tm(t_rows, n_lat):
    return _tile(math.gcd(t_rows, n_lat), 256, 16)


def _rmsnorm_fwd(name, x, width, colblk, gain, t_rows):
    def fn(rv, vv):
        n, _ = _rms(rv[0])
        return [n * vv[0]], []

    (y,), _ = _rowwise(name, fn, t_rows, _tile(t_rows, 256, 16), t_rows, [(x, width, colblk)],
                       [gain.reshape(1, 1, width)], [(width, BF16)], [])
    return y


def _rmsnorm_bwd(name, x, width, colblk, dy, gain, t_rows, out_dtype=F32, into=None):
    def fn(rv, vv):
        n, r = _rms(rv[0])
        return [_rms_bwd(n, r, rv[1] * vv[0])], [_colsum(rv[1] * n)]

    (dx,), (dgain,) = _rowwise(name, fn, t_rows, _tile(t_rows, 256, 16), t_rows,
                               [(x, width, colblk), (dy, width, 0)], [gain.reshape(1, 1, width)],
                               [(width, out_dtype)], [width], into)
    return dx, dgain


def _final_loss(name, h, target, gain):
    t_rows = h.shape[0]
    inv_d = 1.0 / D_MODEL

    def fn(rv, vv):
        g = vv[0]
        n, r = _rms(rv[0])
        e = n * g - rv[1]
        dy = e * inv_d
        return [_rms_bwd(n, r, dy * g)], [_colsum(e * e), _colsum(dy * n)]

    (dh,), (sq, dgain) = _rowwise(name, fn, t_rows, _tile(t_rows, 256, 16), t_rows,
                                  [(h, D_MODEL, 0), (target, D_MODEL, 0)], [gain.reshape(1, 1, D_MODEL)],
                                  [(D_MODEL, F32)], [D_MODEL, D_MODEL])
    return dh, sq, dgain


def _exact_dot(x, m_ref):
    hi = x.astype(BF16)
    lo = (x - hi.astype(F32)).astype(BF16)
    return _dot(hi, m_ref[...], NN) + _dot(lo, m_ref[...], NN)


def _rope(name, z, width, colblk, cos32, sin32, layout, backward, out_dtype, remap=None, into=None):
    t_rows = cos32.shape[0]
    expand, plain, perm = layout
    w_in = remap.shape[1] if (remap is not None and backward) else width
    w_out = remap.shape[1] if (remap is not None and not backward) else width
    extra = [] if remap is None else [remap.T if backward else remap]
    dest = [] if into is None else [into[0]]

    def body(z_ref, c_ref, s_ref, e_ref, m_ref, p_ref, *rest):
        o_ref = rest[-1]
        zz = z_ref[...]
        if remap is not None and backward:
            zz = _exact_dot(zz, rest[0])
        cos = _exact_dot(c_ref[...], e_ref) + m_ref[...]
        sin = _exact_dot(s_ref[...], e_ref)
        rot = _exact_dot(zz * sin if backward else zz, p_ref)
        if not backward:
            rot = rot * sin
        res = zz * cos + rot
        if remap is not None and not backward:
            res = _dot(res.astype(BF16), rest[0][...], NN)
        o_ref[...] = res.astype(o_ref.dtype)

    tm = _tile(t_rows, 256, 16)
    f_spec = pl.BlockSpec((tm, QK_ROPE), lambda i: (i, 0))
    return pl.pallas_call(
        body, name=name, grid=(t_rows // tm,),
        in_specs=[pl.BlockSpec((tm, w_in), lambda i: (i, colblk)), f_spec, f_spec,
                  pl.BlockSpec((QK_ROPE, width), lambda i: (0, 0)), pl.BlockSpec((1, width), lambda i: (0, 0)),
                  pl.BlockSpec((width, width), lambda i: (0, 0))]
        + [pl.BlockSpec(e.shape, lambda i: (0, 0)) for e in extra] + [pl.BlockSpec(memory_space=pl.ANY)] * len(dest),
        out_specs=pl.BlockSpec((tm, w_out), lambda i: (i, 0 if into is None else into[1])),
        out_shape=SDS((t_rows, w_out), out_dtype) if into is None else SDS(into[0].shape, into[0].dtype),
        input_output_aliases={} if into is None else {6 + len(extra): 0}, compiler_params=_cparams(),
    )(z, cos32, sin32, expand, plain, perm.T if backward else perm, *extra, *dest)


def _window_sum(x, w, transposed):
    n_rows = x.shape[0]
    zeros = jnp.zeros((POOL_PAD, x.shape[1]), F32)
    y = jnp.concatenate([zeros, x, zeros], axis=0)
    total = n_rows + 2 * POOL_PAD
    if transposed:
        y = y + pltpu.roll(y, total - 1, 0)
    else:
        y = y + pltpu.roll(y, 1, 0)
    step = 1
    while 2 * step < w:
        y = pltpu.roll(y, step, 0) + pltpu.roll(y, total - step, 0)
        step *= 2
    return y[POOL_PAD:POOL_PAD + n_rows]


def _window_count(n_rows, w):
    t = lax.broadcasted_iota(jnp.int32, (n_rows, 1), 0)
    lo = jnp.maximum(t - w // 2, 0)
    hi = jnp.minimum(t + (w - w // 2 - 1), n_rows - 1)
    return (hi - lo + 1).astype(F32)


def _pool_fwd(name, proj, n_rows, w_grp, scale):
    def body(x_ref, w_ref, sc_ref, y_ref, p_ref):
        for g, w in enumerate(POOL_WINDOWS):
            cols = slice(g * POOL_GROUP_DIM, (g + 1) * POOL_GROUP_DIM)
            x = x_ref[:, cols]
            p = _window_sum(x, w, False) * (1.0 / _window_count(n_rows, w)) - x
            pb = p.astype(BF16)
            p_ref[:, cols] = pb
            y_ref[:, cols] = (_dot(pb, w_ref[g], NN) * sc_ref[:, cols]).astype(BF16)

    blk = pl.BlockSpec((n_rows, POOL_DIM), lambda i: (0, 0))
    return pl.pallas_call(
        body, name=name, grid=(1,),
        in_specs=[blk, pl.BlockSpec(w_grp.shape, lambda i: (0, 0, 0)), pl.BlockSpec((1, POOL_DIM), lambda i: (0, 0))],
        out_specs=[blk, blk], out_shape=[SDS((n_rows, POOL_DIM), BF16)] * 2, compiler_params=_cparams(),
    )(proj, w_grp, scale)


def _pool_bwd(name, dcat, n_rows, p, w_grp, scale, into):
    def body(dy_ref, p_ref, w_ref, sc_ref, into_ref, dx_ref, dw_ref, dsc_ref):
        for g, w in enumerate(POOL_WINDOWS):
            cols = slice(g * POOL_GROUP_DIM, (g + 1) * POOL_GROUP_DIM)
            dy = dy_ref[:, cols]
            pb = p_ref[:, cols]
            pw = _dot(pb, w_ref[g], NN)
            dsc_ref[:, cols] = _colsum(dy * pw)
            dpw = (dy * sc_ref[:, cols]).astype(BF16)
            dw_ref[g] = _dot(pb, dpw, TN)
            dp = _dot(dpw, w_ref[g], NT)
            dx_ref[:, cols] = (_window_sum(dp * (1.0 / _window_count(n_rows, w)), w, True) - dp).astype(BF16)

    blk = pl.BlockSpec((n_rows, POOL_DIM), lambda i: (0, 0))
    w_spec = pl.BlockSpec(w_grp.shape, lambda i: (0, 0, 0))
    v_spec = pl.BlockSpec((1, POOL_DIM), lambda i: (0, 0))
    return pl.pallas_call(
        body, name=name, grid=(1,), in_specs=[blk, blk, w_spec, v_spec, pl.BlockSpec(memory_space=pl.ANY)],
        out_specs=[blk, w_spec, v_spec],
        out_shape=[SDS(into.shape, into.dtype), SDS(w_grp.shape, F32), SDS((1, POOL_DIM), F32)],
        input_output_aliases={4: 0}, compiler_params=_cparams(),
    )(dcat, p, w_grp, scale, into)


def _head_keys(kv_blk, k_rope):
    lane = lax.broadcasted_iota(jnp.int32, (1, HEAD_PAD), 1)
    return jnp.where(lane < QK_NOPE, kv_blk, k_rope)


def _attn_fwd(name, q, kv, k_rope, n_q):
    n_k = kv.shape[0]
    h = kv.shape[1] // HEAD_PAD
    tq = _tile(n_q, ATTN_Q_ROWS, 16)

    def body(q_ref, kv_ref, kr_ref, o_ref, lse_ref):
        kvb = kv_ref[...]
        s = _dot(q_ref[...], _head_keys(kvb, kr_ref[...]), NT) * ATTN_SCALE
        m = jnp.max(s, axis=-1, keepdims=True)
        e = jnp.exp(s - m)
        l = jnp.sum(e, axis=-1, keepdims=True)
        p = (e * (1.0 / l)).astype(BF16)
        lane = lax.broadcasted_iota(jnp.int32, (1, HEAD_PAD), 1)
        o_ref[...] = jnp.where(lane >= QK_NOPE, _dot(p, kvb, NN), 0.0).astype(BF16)
        lse_ref[...] = m + jnp.log(l)

    blk = pl.BlockSpec((tq, HEAD_PAD), lambda hh, i: (i, hh))
    return pl.pallas_call(
        body, name=name, grid=(h, n_q // tq),
        in_specs=[blk, pl.BlockSpec((n_k, HEAD_PAD), lambda hh, i: (0, hh)),
                  pl.BlockSpec((n_k, HEAD_PAD), lambda hh, i: (0, 0))],
        out_specs=[blk, pl.BlockSpec((None, tq, 1), lambda hh, i: (hh, i, 0))],
        out_shape=[SDS((n_q, h * HEAD_PAD), BF16), SDS((h, n_q, 1), F32)], compiler_params=_cparams(),
    )(q, kv, k_rope)


def _attn_bwd(name, q, kv, k_rope, o, lse, dy, dy_col0, n_q):
    n_k = kv.shape[0]
    h = kv.shape[1] // HEAD_PAD
    tq = _tile(n_q, ATTN_Q_ROWS, 16)
    n_i = n_q // tq

    def body(q_ref, kv_ref, kr_ref, o_ref, lse_ref, do_ref, dq_ref, dkv_ref, dkr_ref, acc_k, acc_v):
        hh, i = pl.program_id(0), pl.program_id(1)
        qq, kvb = q_ref[...], kv_ref[...]
        kk = _head_keys(kvb, kr_ref[...])
        d_o = do_ref[...]
        dd = d_o.astype(BF16)
        s = _dot(qq, kk, NT) * ATTN_SCALE
        p = jnp.exp(s - lse_ref[...])
        dp = _dot(dd, kvb, NT)
        delta = jnp.sum(d_o * o_ref[...].astype(F32), axis=-1, keepdims=True)
        ds = (p * (dp - delta) * ATTN_SCALE).astype(BF16)
        dq_ref[...] = _dot(ds, kk, NN)
        dk = _dot(ds, qq, TN)
        dv = _dot(p.astype(BF16), dd, TN)

        @pl.when(i == 0)
        def _():
            acc_k[...] = dk
            acc_v[...] = dv

        @pl.when(i > 0)
        def _():
            acc_k[...] += dk
            acc_v[...] += dv

        @pl.when(i == n_i - 1)
        def _():
            lane = lax.broadcasted_iota(jnp.int32, (1, HEAD_PAD), 1)
            dkv_ref[...] = jnp.where(lane < QK_NOPE, acc_k[...], acc_v[...]).astype(BF16)
            rope = jnp.where((lane >= QK_NOPE) & (lane < QK_HEAD), acc_k[...], 0.0)

            @pl.when(hh == 0)
            def _():
                dkr_ref[...] = rope

            @pl.when(hh > 0)
            def _():
                dkr_ref[...] += rope

    blk = pl.BlockSpec((tq, HEAD_PAD), lambda hh, i: (i, hh))
    kv_spec = pl.BlockSpec((n_k, HEAD_PAD), lambda hh, i: (0, hh))
    shared = pl.BlockSpec((n_k, HEAD_PAD), lambda hh, i: (0, 0))
    return pl.pallas_call(
        body, name=name, grid=(h, n_i),
        in_specs=[blk, kv_spec, shared, blk, pl.BlockSpec((None, tq, 1), lambda hh, i: (hh, i, 0)),
                  pl.BlockSpec((tq, HEAD_PAD), lambda hh, i: (i, dy_col0 + hh))],
        out_specs=[blk, kv_spec, shared],
        out_shape=[SDS((n_q, h * HEAD_PAD), F32), SDS((n_k, h * HEAD_PAD), BF16), SDS((n_k, HEAD_PAD), F32)],
        scratch_shapes=[pltpu.VMEM((n_k, HEAD_PAD), F32), pltpu.VMEM((n_k, HEAD_PAD), F32)],
        compiler_params=_cparams(),
    )(q, kv, k_rope, o, lse, dy)


CONV_COLS = 256


def _shift_rows(x, d):
    n_rows = x.shape[0]
    t = lax.broadcasted_iota(jnp.int32, (n_rows, 1), 0)
    if d > 0:
        return jnp.where(t >= d, pltpu.roll(x, d, 0), 0.0)
    return jnp.where(t < n_rows + d, pltpu.roll(x, n_rows + d, 0), 0.0)


def _conv_fwd(name, z3, conv_w):
    n_rows = z3.shape[0]
    nb = D_MODEL // CONV_COLS

    def body(b_ref, c_ref, v_ref, w_ref, y_ref):
        z = c_ref[...] * v_ref[...]
        zc = w_ref[0:1, :] * _shift_rows(z, 1) + w_ref[1:2, :] * z + w_ref[2:3, :] * _shift_rows(z, -1)
        y_ref[...] = (b_ref[...] * zc).astype(BF16)

    def part(k):
        return pl.BlockSpec((n_rows, CONV_COLS), lambda j: (0, k * nb + j))

    return pl.pallas_call(
        body, name=name, grid=(nb,),
        in_specs=[part(0), part(1), part(2), pl.BlockSpec((3, CONV_COLS), lambda j: (0, j))],
        out_specs=pl.BlockSpec((n_rows, CONV_COLS), lambda j: (0, j)),
        out_shape=SDS((n_rows, D_MODEL), BF16), compiler_params=_cparams(),
    )(z3, z3, z3, conv_w)


def _conv_bwd(name, dy, z3, conv_w):
    n_rows = z3.shape[0]
    nb = D_MODEL // CONV_COLS

    def body(dy_ref, b_ref, c_ref, v_ref, w_ref, db_ref, dc_ref, dv_ref, dw_ref):
        c, v, d_y = c_ref[...], v_ref[...], dy_ref[...]
        z = c * v
        z_dn, z_up = _shift_rows(z, 1), _shift_rows(z, -1)
        zc = w_ref[0:1, :] * z_dn + w_ref[1:2, :] * z + w_ref[2:3, :] * z_up
        db_ref[...] = (d_y * zc).astype(BF16)
        dzc = d_y * b_ref[...]
        dz = w_ref[0:1, :] * _shift_rows(dzc, -1) + w_ref[1:2, :] * dzc + w_ref[2:3, :] * _shift_rows(dzc, 1)
        dc_ref[...] = (dz * v).astype(BF16)
        dv_ref[...] = (dz * c).astype(BF16)
        dw_ref[0:1, :] = _colsum(dzc * z_dn)
        dw_ref[1:2, :] = _colsum(dzc * z)
        dw_ref[2:3, :] = _colsum(dzc * z_up)

    def part(k):
        return pl.BlockSpec((n_rows, CONV_COLS), lambda j: (0, k * nb + j))

    col = pl.BlockSpec((n_rows, CONV_COLS), lambda j: (0, j))
    w_spec = pl.BlockSpec((3, CONV_COLS), lambda j: (0, j))
    return pl.pallas_call(
        body, name=name, grid=(nb,), in_specs=[col, part(0), part(1), part(2), w_spec],
        out_specs=[col, col, col, w_spec],
        out_shape=[SDS((n_rows, D_MODEL), BF16)] * 3 + [SDS((3, D_MODEL), F32)], compiler_params=_cparams(),
    )(dy, z3, z3, z3, conv_w)


def _silu_rows(name, x):
    def body(x_ref, s_ref, d_ref):
        xx = x_ref[...]
        sg = jax.nn.sigmoid(xx)
        s_ref[...] = (xx * sg).astype(BF16)
        d_ref[...] = sg * (1.0 + xx * (1.0 - sg))

    return pl.pallas_call(body, name=name, out_shape=[SDS(x.shape, BF16), SDS(x.shape, F32)])(x)


def _sum_rows(name, x, scale=None):
    r, n = x.shape
    tn = _tile(n, 32768, 128)

    def body(*refs):
        acc = jnp.sum(refs[0][...].astype(F32), axis=0, keepdims=True)
        if scale is not None:
            acc = acc * refs[1][...]
        refs[-1][...] = acc

    in_specs = [pl.BlockSpec((r, tn), lambda j: (0, j))]
    args = [x]
    if scale is not None:
        in_specs.append(pl.BlockSpec((1, tn), lambda j: (0, j)))
        args.append(scale)
    return pl.pallas_call(body, name=name, grid=(n // tn,), in_specs=in_specs,
                          out_specs=pl.BlockSpec((1, tn), lambda j: (0, j)), out_shape=SDS((1, n), F32))(*args)


def _me_operand(me):
    return jnp.reshape(me, (1,)).astype(jnp.int32)


def _sum_slots(name, slots, src, me):
    n_slots, r, c = slots.shape
    tr = _tile(r, 432, 16)

    def body(me_ref, own_ref, x_ref, o_ref):
        acc = own_ref[...].astype(F32)
        for sl in range(n_slots):
            acc = acc + x_ref[sl].astype(F32)
        o_ref[...] = acc

    grid_spec = pltpu.PrefetchScalarGridSpec(
        num_scalar_prefetch=1, grid=(r // tr,),
        in_specs=[pl.BlockSpec((None, tr, c), lambda i, me_ref: (me_ref[0], i, 0)),
                  pl.BlockSpec((n_slots, tr, c), lambda i, me_ref: (0, i, 0))],
        out_specs=pl.BlockSpec((tr, c), lambda i, me_ref: (i, 0)))
    return pl.pallas_call(body, name=name, grid_spec=grid_spec, out_shape=SDS((r, c), F32),
                          compiler_params=_cparams())(_me_operand(me), src, slots)


def _adamw(name, w, g, m, v):
    shape = w.shape
    cols = shape[-1]
    rows = w.size // cols
    tr = _tile(rows, 512, 8)
    bc1 = 1.0 - ADAM_B1 ** ADAM_STEP
    bc2 = 1.0 - ADAM_B2 ** ADAM_STEP

    def body(w_ref, g_ref, m_ref, v_ref, d_ref, nm_ref, nv_ref):
        gg = g_ref[...]
        nm = ADAM_B1 * m_ref[...] + (1.0 - ADAM_B1) * gg
        nv = ADAM_B2 * v_ref[...] + (1.0 - ADAM_B2) * (gg * gg)
        nm_ref[...] = nm
        nv_ref[...] = nv
        d_ref[...] = -ADAM_LR * ((nm / bc1) / (jnp.sqrt(nv / bc2) + ADAM_EPS) + ADAM_WD * w_ref[...])

    spec = pl.BlockSpec((tr, cols), lambda i: (i, 0))
    outs = pl.pallas_call(body, name=name, grid=(rows // tr,), in_specs=[spec] * 4, out_specs=[spec] * 3,
                          out_shape=[SDS((rows, cols), F32)] * 3, compiler_params=_cparams())(
        w.reshape(rows, cols), g.reshape(rows, cols), m.reshape(rows, cols), v.reshape(rows, cols))
    return tuple(t.reshape(shape) for t in outs)


def _exchange(name, x, scatter, after=None):
    blk = x.shape[1:] if scatter else x.shape
    extra = [] if after is None else [after]

    def body(x_ref, *rest):
        out_ref, send_sems, recv_sems, local_sem = rest[len(extra):]
        mx, my, mc = lax.axis_index("x"), lax.axis_index("y"), lax.axis_index("c")
        me = 4 * mx + 2 * my + mc
        own = pltpu.make_async_copy(x_ref.at[me] if scatter else x_ref, out_ref.at[me], local_sem)
        own.start()
        copies = []
        for kk in range(1, N_DEV):
            px = jnp.bitwise_xor(mx, (kk >> 2) & 1)
            py = jnp.bitwise_xor(my, (kk >> 1) & 1)
            pc = jnp.bitwise_xor(mc, kk & 1)
            peer = 4 * px + 2 * py + pc
            send = pltpu.make_async_remote_copy(
                src_ref=x_ref.at[peer] if scatter else x_ref, dst_ref=out_ref.at[me],
                send_sem=send_sems.at[kk - 1], recv_sem=recv_sems.at[kk - 1],
                device_id=(px, py, pc), device_id_type=MESH)
            send.start()
            arrival = pltpu.make_async_remote_copy(
                src_ref=x_ref.at[peer] if scatter else x_ref, dst_ref=out_ref.at[peer],
                send_sem=send_sems.at[kk - 1], recv_sem=recv_sems.at[kk - 1],
                device_id=(px, py, pc), device_id_type=MESH)
            copies.append((send, arrival))
        for send, arrival in copies:
            arrival.wait_recv()
            send.wait_send()
        own.wait()

    return pl.pallas_call(
        body, name=name, out_shape=SDS((N_DEV,) + tuple(blk), x.dtype),
        in_specs=[pl.BlockSpec(memory_space=pl.ANY)] * (1 + len(extra)), out_specs=pl.BlockSpec(memory_space=pl.ANY),
        scratch_shapes=[pltpu.SemaphoreType.DMA((N_DEV - 1,)), pltpu.SemaphoreType.DMA((N_DEV - 1,)),
                        pltpu.SemaphoreType.DMA],
    )(x, *extra)


def _rope_perm(pre, reps, post):
    half = QK_ROPE // 4
    width = reps * (pre + QK_ROPE) + post
    p = np.zeros((width, width), np.float32)
    for rep in range(reps):
        s0 = rep * (pre + QK_ROPE) + pre
        for base in (s0, s0 + 2 * half):
            for i in range(half):
                p[base + half + i, base + i] = -1.0
                p[base + i, base + half + i] = 1.0
    return p


def _rope_layout(pre, reps, post):
    width = reps * (pre + QK_ROPE) + post
    expand = np.zeros((QK_ROPE, width), np.float32)
    plain = np.ones((1, width), np.float32)
    for rep in range(reps):
        s0 = rep * (pre + QK_ROPE) + pre
        expand[np.arange(QK_ROPE), s0 + np.arange(QK_ROPE)] = 1.0
        plain[0, s0:s0 + QK_ROPE] = 0.0
    return jnp.asarray(expand, BF16), jnp.asarray(plain, F32), jnp.asarray(_rope_perm(pre, reps, post), BF16)


def _head_spread():
    spread = np.zeros((HEADS * QK_HEAD, HEADS * HEAD_PAD), np.float32)
    for hh in range(HEADS):
        spread[hh * QK_HEAD + np.arange(QK_HEAD), hh * HEAD_PAD + np.arange(QK_HEAD)] = 1.0
    return jnp.asarray(spread, BF16)


def _rope_factors(n_lat, t_rows):
    half = QK_ROPE // 4
    pos = jnp.arange(n_lat)
    freqs = jnp.power(ROPE_THETA, -jnp.arange(0, 2 * half, 2, dtype=F32) / (2 * half))
    ang_r = (pos // GRID_W).astype(F32)[:, None] * freqs
    ang_c = (pos % GRID_W).astype(F32)[:, None] * freqs
    ang = jnp.concatenate([ang_r, ang_r, ang_c, ang_c], axis=-1)
    rest = t_rows - n_lat
    return (jnp.concatenate([jnp.cos(ang), jnp.ones((rest, QK_ROPE), F32)], axis=0),
            jnp.concatenate([jnp.sin(ang), jnp.zeros((rest, QK_ROPE), F32)], axis=0))


def _ffn_half_fwd(tag, s, mg, k, feed, i, coef, n_lat):
    wg_t, wu_t = feed.weights(f"{tag}_up", [f"gate_t{i}", f"up_t{i}"], s)
    u, a, b, hid = _ffn_up(f"{tag}_up", s, mg, k, n_lat, wg_t, wu_t)
    (wd,) = feed.weights(f"{tag}_down", [f"down{i}"], hid)
    s_out, o = _mm_resid(f"{tag}_down", [(hid, wd)], s, mg, k, coef, n_lat)
    return s_out, (s, u, a, b, hid, o, wg_t, wu_t, wd)


def _ffn_half_bwd(tag, ds_out, saved, mg, k, feed, i, coef, n_lat, out_rows=None):
    s, u, a, b, hid, o, wg_t, wu_t, wd = saved
    do, da, db, dgate = _ffn_dact(f"{tag}_dact", ds_out, o, mg, k, coef, n_lat, wd, a, b)
    dwd = _mm(f"{tag}_dwd", [(hid, do)], "tn", BF16)
    dwg_t, dwu_t = _dw_pair(f"{tag}_dwgu", da, db, u)
    token = feed.grads(tag, {f"down{i}": dwd, f"gate_t{i}": dwg_t, f"up_t{i}": dwu_t})
    ds_in, (dshift, dscale, dgain) = _du_adaln(f"{tag}_du", [(da, wg_t), (db, wu_t)], s, ds_out, mg, k, n_lat,
                                               _after(token), out_rows)
    return ds_in, dict(shift=dshift, scale=dscale, gate=dgate, gain=dgain)


def _after(token):
    return jnp.zeros((1, D_MODEL), F32) + token


def _mod_grad(parts, n_groups):
    rows = []
    zero = jnp.zeros((n_groups, 1, D_MODEL), F32)
    for k in range(3):
        for nm in ("shift", "scale", "gate"):
            t = parts[k].get(nm, zero)
            if t.shape[0] < n_groups:
                t = jnp.concatenate([t, jnp.zeros((n_groups - t.shape[0], 1, D_MODEL), F32)], axis=0)
            rows.append(t)
    return jnp.concatenate(rows, axis=1).reshape(n_groups, N_MOD * D_MODEL)


def _local_step(x, ctx, target, mod_h, mod_g, norm_g, feed, pool_w, pool_scale, q_norm_g, kv_norm_g, conv_w,
                final_norm_g):
    n_lat, n_ctx = x.shape[0], ctx.shape[0]
    t_all = n_lat + n_ctx
    mg0 = jnp.stack([jnp.concatenate([mod_h[0], norm_g[0]], axis=0), jnp.concatenate([mod_g, norm_g[0]], axis=0)])
    mg1 = jnp.concatenate([mod_h[1], norm_g[1]], axis=0)[None]

    s0 = jnp.concatenate([x, ctx], axis=0)
    s1, sv_f00 = _ffn_half_fwd("l0f0", s0, mg0, 0, feed, 0, 0.5, n_lat)

    (w_in,) = feed.weights("l0m_in", ["in_t"], s1)
    kv_rows = KV_RANK + QK_ROPE
    w_in_t = jnp.concatenate([
        w_in[:POOL_DIM], jnp.zeros((PA_CQ - POOL_DIM, D_MODEL), BF16), w_in[POOL_DIM:POOL_DIM + Q_RANK],
        w_in[POOL_DIM + Q_RANK:], jnp.zeros((PA_KV_W - kv_rows, D_MODEL), BF16)], axis=0)
    ua, proj = _adaln_mm("l0m_proj", s1, mg0, 1, n_lat, w_in_t)
    w_uq, w_ukv_t, w_ab_out = feed.weights("l0m_rest", ["uq", "ukv_t", "ab_out"], proj)
    pool_y, pool_p = _pool_fwd("l0m_pool", proj, n_lat, pool_w.astype(BF16), pool_scale)
    nq = _rmsnorm_fwd("l0m_qnorm", proj, Q_RANK, PA_CQ // Q_RANK, q_norm_g, n_lat)
    q_lin = _mm("l0m_q", [(nq, w_uq)], "nn", F32, 512, 768)
    cos32, sin32 = _rope_factors(n_lat, t_all)
    lay_q, lay_k = _rope_layout(QK_NOPE, HEADS, 0), _rope_layout(KV_RANK, 1, PA_KV_W - kv_rows)
    spread = _head_spread()
    q_flat = _rope("l0m_qrope", q_lin, Q_RANK, 0, cos32[:n_lat], sin32[:n_lat], lay_q, False, BF16, spread)
    kvr = _rope("l0m_krope", proj, PA_KV_W, PA_KV // PA_KV_W, cos32, sin32, lay_k, False, F32)
    nkv = _rmsnorm_fwd("l0m_kvnorm", kvr, KV_RANK, 0, kv_norm_g, t_all)
    kv = _mm("l0m_kv", [(nkv, w_ukv_t)], "nt", BF16, 768, 512)
    k_rope = jnp.pad(kvr[:, KV_RANK:KV_RANK + QK_ROPE].astype(BF16), ((0, 0), (QK_NOPE, HEAD_PAD - QK_HEAD)))
    o_flat, lse = _attn_fwd("l0m_attn", q_flat, kv, k_rope, n_lat)
    w_o_pad = jnp.pad(w_ab_out[POOL_DIM:].reshape(HEADS, V_HEAD, D_MODEL),
                      ((0, 0), (HEAD_PAD - V_HEAD, 0), (0, 0))).reshape(HEADS * HEAD_PAD, D_MODEL)
    w_o_pool = w_ab_out[:POOL_DIM]
    h2, mix_o = _mm_resid("l0m_out", [(pool_y, w_o_pool), (o_flat, w_o_pad)], s1, mg0[:1], 1, 1.0, n_lat)

    h3, sv_f01 = _ffn_half_fwd("l0f1", h2, mg0[:1], 2, feed, 1, 0.5, n_lat)

    h4, sv_f10 = _ffn_half_fwd("l1f0", h3, mg1, 0, feed, 2, 0.5, n_lat)
    w_cin_t, w_c_out = feed.weights("l1m", ["cin_t", "c_out"], h4)
    uc, z3 = _adaln_mm("l1m_in", h4, mg1, 1, n_lat, w_cin_t)
    yc = _conv_fwd("l1m_conv", z3, conv_w)
    h5, conv_o = _mm_resid("l1m_out", [(yc, w_c_out)], h4, mg1, 1, 1.0, n_lat)
    h6, sv_f11 = _ffn_half_fwd("l1f1", h5, mg1, 2, feed, 3, 0.5, n_lat)

    dh6, sq_cols, d_final_g = _final_loss("loss_head", h6, target, final_norm_g)
    g = {}
    dh5, g["f11"] = _ffn_half_bwd("l1f1", dh6, sv_f11, mg1, 2, feed, 3, 0.5, n_lat)

    do_c, dyc, dgate_c = _gate_mm("l1m_dy", dh5, conv_o, mg1, 1, 1.0, n_lat, w_c_out)
    d_c_out = _mm("l1m_dwout", [(yc, do_c)], "tn", BF16)
    db_, dc_, dv_, d_conv_w = _conv_bwd("l1m_dconv", dyc, z3, conv_w)
    dz3 = jnp.concatenate([db_, dc_, dv_], axis=-1)
    d_cin_t = _mm("l1m_dwin", [(dz3, uc)], "tn", BF16)
    token = feed.grads("l1m", {"c_out": d_c_out, "cin_t": d_cin_t})
    dh4, (dsh_c, dsc_c, dgn_c) = _du_adaln("l1m_du", [(dz3, w_cin_t)], h4, dh5, mg1, 1, n_lat, _after(token))
    dh3, g["f10"] = _ffn_half_bwd("l1f0", dh4, sv_f10, mg1, 0, feed, 2, 0.5, n_lat)

    dh2, g["f01"] = _ffn_half_bwd("l0f1", dh3, sv_f01, mg0[:1], 2, feed, 1, 0.5, n_lat)

    w_back = jnp.concatenate([w_o_pool, w_o_pad], axis=0)
    do_a, dcat, dgate_a = _gate_mm("l0m_dcat", dh2, mix_o, mg0[:1], 1, 1.0, n_lat, w_back)
    d_o_pad = _mm("l0m_dwout_a", [(o_flat, do_a)], "tn", BF16)
    d_ab_out = jnp.concatenate([
        _mm("l0m_dwout_p", [(pool_y, do_a)], "tn", BF16),
        d_o_pad.reshape(HEADS, HEAD_PAD, D_MODEL)[:, HEAD_PAD - V_HEAD:].reshape(HEADS * V_HEAD, D_MODEL)], axis=0)
    dproj = jnp.zeros((t_all, PA_W), BF16)
    dproj, d_pool_w, d_pool_scale = _pool_bwd("l0m_dpool", dcat, n_lat, pool_p, pool_w.astype(BF16), pool_scale, dproj)
    dq_flat, dkv, dk_rope = _attn_bwd("l0m_dattn", q_flat, kv, k_rope, o_flat, lse, dcat, POOL_DIM // HEAD_PAD, n_lat)
    dq_lin = _rope("l0m_dqrope", dq_flat, Q_RANK, 0, cos32[:n_lat], sin32[:n_lat], lay_q, True, BF16, spread)
    d_uq = _mm("l0m_dwuq", [(nq, dq_lin)], "tn", BF16, 768, 768)
    dnq = _mm("l0m_dnq", [(dq_lin, w_uq)], "nt", F32, 512, 768)
    dproj, d_q_norm_g = _rmsnorm_bwd("l0m_dqnorm", proj, Q_RANK, PA_CQ // Q_RANK, dnq, q_norm_g, n_lat, BF16,
                                     (dproj, PA_CQ // Q_RANK))
    dnkv = _mm("l0m_dnkv", [(dkv, w_ukv_t)], "nn", F32, 768, 256)
    d_ukv_t = _mm("l0m_dwukv", [(dkv, nkv)], "tn", BF16, 512, 256)
    dckv, d_kv_norm_g = _rmsnorm_bwd("l0m_dkvnorm", kvr, KV_RANK, 0, dnkv, kv_norm_g, t_all)
    dkvr = jnp.concatenate([dckv, dk_rope[:, QK_NOPE:QK_HEAD],
                            jnp.zeros((t_all, PA_KV_W - KV_RANK - QK_ROPE), F32)], axis=-1)
    dproj = _rope("l0m_dkrope", dkvr, PA_KV_W, 0, cos32, sin32, lay_k, True, BF16, None, (dproj, PA_KV // PA_KV_W))
    d_in_pad = _mm("l0m_dwin", [(dproj, ua)], "tn", BF16, 640, 512)
    d_in_t = jnp.concatenate([d_in_pad[:POOL_DIM], d_in_pad[PA_CQ:PA_CQ + Q_RANK],
                              d_in_pad[PA_KV:PA_KV + kv_rows]], axis=0)
    token = feed.grads("l0m", {"ab_out": d_ab_out, "uq": d_uq, "ukv_t": d_ukv_t, "in_t": d_in_t})
    ds1, (dsh_a, dsc_a, dgn_a) = _du_adaln("l0m_du", [(dproj, w_in_t)], s1, dh2, mg0, 1, n_lat, _after(token))
    grad_x, g["f00"] = _ffn_half_bwd("l0f0", ds1, sv_f00, mg0, 0, feed, 0, 0.5, n_lat, out_rows=n_lat)

    dmod0 = _mod_grad([g["f00"], dict(shift=dsh_a, scale=dsc_a, gate=dgate_a), g["f01"]], 2)
    dmod1 = _mod_grad([g["f10"], dict(shift=dsh_c, scale=dsc_c, gate=dgate_c), g["f11"]], 1)
    d_norm_g = jnp.stack([
        jnp.concatenate([jnp.sum(g["f00"]["gain"], axis=0), jnp.sum(dgn_a, axis=0), g["f01"]["gain"][0]], axis=0),
        jnp.concatenate([g["f10"]["gain"][0], dgn_c[0], g["f11"]["gain"][0]], axis=0)])
    grads = dict(
        pool_w=d_pool_w, pool_scale=d_pool_scale, q_norm_g=d_q_norm_g[0], kv_norm_g=d_kv_norm_g[0],
        conv_w=d_conv_w, final_norm_g=d_final_g[0], norm_g=d_norm_g,
        mod_h=jnp.stack([dmod0[0], dmod1[0]]), mod_g=dmod0[1])
    return sq_cols, grad_x, grads


HBM_SPEC = pl.BlockSpec(memory_space=pltpu.HBM)
SEM_SPEC = pl.BlockSpec(memory_space=pltpu.SEMAPHORE)
ANY_SPEC = pl.BlockSpec(memory_space=pl.ANY)
SIDE_EFFECT = pltpu.SideEffectType.DATAFLOW_SIDE_EFFECTING
N_PEERS = N_DEV - 1


def _mesh_place():
    mx, my, mc = lax.axis_index("x"), lax.axis_index("y"), lax.axis_index("c")
    return mx, my, mc, 4 * mx + 2 * my + mc


def _peer(place, kk):
    mx, my, mc, _ = place
    px = jnp.bitwise_xor(mx, (kk >> 2) & 1)
    py = jnp.bitwise_xor(my, (kk >> 1) & 1)
    pc = jnp.bitwise_xor(mc, kk & 1)
    return (px, py, pc), 4 * px + 2 * py + pc


def _hbm(a):
    return pltpu.with_memory_space_constraint(a, pltpu.HBM)


def _landing(block, me):
    zone = lax.empty((N_DEV,) + block.shape, block.dtype)
    return lax.dynamic_update_slice(zone, block[None], (me,) + (0,) * block.ndim)


ALL_PEERS = tuple(range(1, N_DEV))
SIBLING = 1
CHIP_PEERS = (2, 4, 6)
RELAYED = (3, 5, 7)


def _exchange_start(name, srcs, lands, scatter, after, peers=ALL_PEERS):
    n = len(srcs)
    extra = [] if after is None else [after]

    def body(*refs):
        src, land = refs[:n], refs[n:2 * n]
        send_sems, recv_sems, token = refs[2 * n + len(extra)], refs[2 * n + len(extra) + 1], refs[-1]
        place = _mesh_place()
        for a in range(n):
            for kk in peers:
                dev, peer = _peer(place, kk)
                pltpu.make_async_remote_copy(
                    src_ref=src[a].at[peer] if scatter else src[a],
                    dst_ref=land[a].at[kk - 1] if scatter else land[a].at[place[3]],
                    send_sem=send_sems.at[a * N_PEERS + kk - 1], recv_sem=recv_sems.at[a * N_PEERS + kk - 1],
                    device_id=dev, device_id_type=MESH).start()
        token[...] = jnp.zeros_like(token)

    thru = [pltpu.HBM(t.shape, t.dtype) for t in (*srcs, *lands)]
    res = pl.pallas_call(
        body, name=name,
        out_shape=(pltpu.SemaphoreType.DMA((n * N_PEERS,)), pltpu.SemaphoreType.DMA((n * N_PEERS,)), *thru,
                   SDS((8, 128), F32)),
        in_specs=[HBM_SPEC] * (2 * n) + [ANY_SPEC] * len(extra),
        out_specs=(SEM_SPEC, SEM_SPEC, *([HBM_SPEC] * (2 * n)), pl.BlockSpec(memory_space=pltpu.VMEM)),
        input_output_aliases={i: 2 + i for i in range(2 * n)},
        compiler_params=pltpu.CompilerParams(has_side_effects=SIDE_EFFECT),
    )(*[_hbm(s) for s in srcs], *[_hbm(t) for t in lands], *extra)
    return res[0], res[1], list(res[2:2 + n]), list(res[2 + n:2 + 2 * n]), res[-1]


def _exchange_wait(name, send_sems, recv_sems, srcs, lands, places, scatter, after):
    n = len(srcs)

    def body(*refs):
        src, land = refs[:n], refs[n:2 * n]
        send, recv = refs[2 * n], refs[2 * n + 1]
        place = _mesh_place()
        for a in range(n):
            for kk in range(1, N_DEV):
                dev, peer = _peer(place, kk)
                cp = pltpu.make_async_remote_copy(
                    src_ref=src[a].at[peer] if scatter else src[a],
                    dst_ref=land[a].at[kk - 1] if scatter else land[a].at[peer],
                    send_sem=send.at[places[a] * N_PEERS + kk - 1], recv_sem=recv.at[places[a] * N_PEERS + kk - 1],
                    device_id=dev, device_id_type=MESH)
                cp.wait_send()
                cp.wait_recv()

    thru = [pltpu.HBM(t.shape, t.dtype) for t in (*srcs, *lands)]
    res = pl.pallas_call(
        body, name=name, out_shape=tuple(thru),
        in_specs=[HBM_SPEC] * (2 * n) + [SEM_SPEC, SEM_SPEC] + [ANY_SPEC] * len(after),
        out_specs=tuple([HBM_SPEC] * (2 * n)), input_output_aliases={i: i for i in range(2 * n)},
        compiler_params=pltpu.CompilerParams(has_side_effects=SIDE_EFFECT),
    )(*srcs, *lands, send_sems, recv_sems, *after)
    return list(res[:n]), list(res[n:])


def _gather_relay(name, send1, recv1, lands, places, after):
    n = len(lands)

    def body(*refs):
        land, s1, r1 = refs[:n], refs[n], refs[n + 1]
        s2, r2 = refs[n + 3], refs[n + 4]
        place = _mesh_place()
        sibling = _peer(place, SIBLING)[0]
        for a in range(n):
            for j, kk in enumerate(CHIP_PEERS):
                dev, origin = _peer(place, kk)
                block = land[a].at[origin]
                pltpu.make_async_remote_copy(
                    src_ref=block, dst_ref=block, send_sem=s1.at[places[a] * N_PEERS + kk - 1],
                    recv_sem=r1.at[places[a] * N_PEERS + kk - 1], device_id=dev, device_id_type=MESH).wait_recv()
                pltpu.make_async_remote_copy(
                    src_ref=block, dst_ref=block, send_sem=s2.at[a * 3 + j], recv_sem=r2.at[a * 3 + j],
                    device_id=sibling, device_id_type=MESH).start()

    res = pl.pallas_call(
        body, name=name,
        out_shape=(pltpu.SemaphoreType.DMA((3 * n,)), pltpu.SemaphoreType.DMA((3 * n,)),
                   *[pltpu.HBM(t.shape, t.dtype) for t in lands]),
        in_specs=[HBM_SPEC] * n + [SEM_SPEC, SEM_SPEC, ANY_SPEC],
        out_specs=(SEM_SPEC, SEM_SPEC, *([HBM_SPEC] * n)),
        input_output_aliases={i: 2 + i for i in range(n)},
        compiler_params=pltpu.CompilerParams(has_side_effects=SIDE_EFFECT),
    )(*lands, send1, recv1, after)
    return res[0], res[1], list(res[2:])


def _gather_wait(name, send1, recv1, send2, recv2, srcs, lands, places, after):
    n = len(lands)

    def body(*refs):
        src, land = refs[:n], refs[n:2 * n]
        s1, r1, s2, r2 = refs[2 * n:2 * n + 4]
        place = _mesh_place()
        for a in range(n):
            for kk in (SIBLING,) + CHIP_PEERS:
                dev, origin = _peer(place, kk)
                first = pltpu.make_async_remote_copy(
                    src_ref=src[a], dst_ref=land[a].at[origin], send_sem=s1.at[places[a] * N_PEERS + kk - 1],
                    recv_sem=r1.at[places[a] * N_PEERS + kk - 1], device_id=dev, device_id_type=MESH)
                first.wait_send()
                if kk == SIBLING:
                    first.wait_recv()
            for j, kk in enumerate(CHIP_PEERS):
                dev, origin = _peer(place, kk + 1)
                relay = pltpu.make_async_remote_copy(
                    src_ref=src[a], dst_ref=land[a].at[origin], send_sem=s2.at[a * 3 + j], recv_sem=r2.at[a * 3 + j],
                    device_id=dev, device_id_type=MESH)
                relay.wait_send()
                relay.wait_recv()

    arrays = (*srcs, *lands)
    res = pl.pallas_call(
        body, name=name, out_shape=tuple(pltpu.HBM(t.shape, t.dtype) for t in arrays),
        in_specs=[HBM_SPEC] * (2 * n) + [SEM_SPEC] * 4 + [ANY_SPEC], out_specs=tuple([HBM_SPEC] * (2 * n)),
        input_output_aliases={i: i for i in range(2 * n)},
        compiler_params=pltpu.CompilerParams(has_side_effects=SIDE_EFFECT),
    )(*arrays, send1, recv1, send2, recv2, after)
    return list(res[n:])


class _Feed:
    def __init__(self, shards, groups, me):
        self.shards, self.groups, self.me, self.pos = shards, groups, me, 0
        self.sems, self.srcs, self.lands = {}, {}, {}
        self.relays = {}
        self.pending = []

    def start(self, tag, names, after):
        srcs = [self.shards[nm] for nm in names]
        lands = [_landing(s, self.me) for s in srcs]
        send, recv, srcs, lands, self.token = _exchange_start(
            f"gather_start_{tag}", srcs, lands, False, after, (SIBLING,) + CHIP_PEERS)
        for i, nm in enumerate(names):
            self.sems[nm], self.srcs[nm], self.lands[nm] = (send, recv, i), srcs[i], lands[i]
        return self.token

    def _relay(self, gi, after):
        names = self.groups[gi]
        if gi not in self.relays:
            send, recv, _ = self.sems[names[0]]
            places = [self.sems[nm][2] for nm in names]
            send2, recv2, lands = _gather_relay(f"gather_relay_{gi}", send, recv, [self.lands[nm] for nm in names],
                                                places, after)
            for nm, t in zip(names, lands):
                self.lands[nm] = t
            self.relays[gi] = (send2, recv2)
            after = lands[0]
        return after

    def weights(self, tag, names, after):
        gi = self.pos
        assert names == self.groups[gi], (names, self.groups[gi])
        if gi == 0:
            after = self.token
        self._relay(gi, after)
        if 1 <= gi < len(self.groups) - 1:
            after = self._relay(gi + 1, after)
        send2, recv2 = self.relays[gi]
        send, recv, _ = self.sems[names[0]]
        got = _gather_wait(f"gather_wait_{tag}", send, recv, send2, recv2, [self.srcs[nm] for nm in names],
                           [self.lands[nm] for nm in names], [self.sems[nm][2] for nm in names], after)
        self.pos += 1
        return [t.reshape((N_DEV * t.shape[1],) + t.shape[2:]) for t in got]

    def grads(self, tag, full):
        names = list(full)
        srcs = [full[nm].reshape((N_DEV, full[nm].shape[0] // N_DEV) + full[nm].shape[1:]) for nm in names]
        lands = [lax.empty((N_PEERS,) + s.shape[1:], s.dtype) for s in srcs]
        send, recv, srcs, lands, token = _exchange_start(f"scatter_start_{tag}", srcs, lands, True, None)
        self.pending.append((tag, names, send, recv, srcs, lands))
        return token[0, 0]

    def collect(self, tags, after, keep_slots=()):
        out = {}
        for tag, names, send, recv, srcs, lands in self.pending:
            if tag not in tags:
                continue
            srcs, got = _exchange_wait(f"scatter_wait_{tag}", send, recv, srcs, lands, list(range(len(names))), True,
                                       after)
            for nm, slots, src in zip(names, got, srcs):
                out[nm] = ((slots, src) if nm.startswith(tuple(keep_slots))
                           else _sum_slots(f"reduce_{nm}", slots, src, self.me))
        return out


def _adamw_math(w, gg, m, v):
    nm = ADAM_B1 * m + (1.0 - ADAM_B1) * gg
    nv = ADAM_B2 * v + (1.0 - ADAM_B2) * (gg * gg)
    bc1 = 1.0 - ADAM_B1 ** ADAM_STEP
    bc2 = 1.0 - ADAM_B2 ** ADAM_STEP
    return -ADAM_LR * ((nm / bc1) / (jnp.sqrt(nv / bc2) + ADAM_EPS) + ADAM_WD * w), nm, nv


def _adamw_part(name, i, w, scattered, me, m, v, prev):
    n_parts, rows, cols = w.shape
    tr = _tile(rows, 256, 16)
    if prev is None:
        prev = tuple(lax.empty(w.shape, F32) for _ in range(4))

    slots, src = scattered

    def body(me_ref, w_ref, g_ref, own_ref, m_ref, v_ref, *rest):
        go_ref, d_ref, nm_ref, nv_ref = rest[4:]
        gg = own_ref[...].astype(F32)
        for sl in range(N_PEERS):
            gg = gg + g_ref[sl].astype(F32)
        d, nm, nv = _adamw_math(w_ref[...], gg, m_ref[...], v_ref[...])
        go_ref[...] = gg
        d_ref[...] = d
        nm_ref[...] = nm
        nv_ref[...] = nv

    part = pl.BlockSpec((None, tr, cols), lambda r, me_ref: (i, r, 0))
    grid_spec = pltpu.PrefetchScalarGridSpec(
        num_scalar_prefetch=1, grid=(rows // tr,),
        in_specs=[part, pl.BlockSpec((N_PEERS, tr, cols), lambda r, me_ref: (0, r, 0)),
                  pl.BlockSpec((None, tr, cols), lambda r, me_ref: (me_ref[0], r, 0)), part, part] + [ANY_SPEC] * 4,
        out_specs=[part] * 4)
    return pl.pallas_call(
        body, name=name, grid_spec=grid_spec, out_shape=[SDS(w.shape, F32)] * 4,
        input_output_aliases={6 + k: k for k in range(4)}, compiler_params=_cparams(),
    )(_me_operand(me), w, slots, src, m, v, *prev)


WEIGHT_NAMES = ("c_ctx", "norm_g", "w_mod", "b_mod", "ffn_w_gate", "ffn_w_up", "ffn_w_down", "ab_w_in", "pool_w",
                "pool_scale", "q_norm_g", "w_uq", "kv_norm_g", "w_ukv", "ab_w_out", "conv_w_in", "conv_w",
                "conv_w_out", "final_norm_g")


def kernel(x, c, ctx, c_ctx, norm_g, w_mod, b_mod, ffn_w_gate, ffn_w_up, ffn_w_down, ab_w_in, pool_w, pool_scale, q_norm_g, w_uq, kv_norm_g, w_ukv, ab_w_out, conv_w_in, conv_w, conv_w_out, final_norm_g, loss_target, m_c_ctx, m_norm_g, m_w_mod, m_b_mod, m_ffn_w_gate, m_ffn_w_up, m_ffn_w_down, m_ab_w_in, m_pool_w, m_pool_scale, m_q_norm_g, m_w_uq, m_kv_norm_g, m_w_ukv, m_ab_w_out, m_conv_w_in, m_conv_w, m_conv_w_out, m_final_norm_g, v_c_ctx, v_norm_g, v_w_mod, v_b_mod, v_ffn_w_gate, v_ffn_w_up, v_ffn_w_down, v_ab_w_in, v_pool_w, v_pool_scale, v_q_norm_g, v_w_uq, v_kv_norm_g, v_w_ukv, v_ab_w_out, v_conv_w_in, v_conv_w, v_conv_w_out, v_final_norm_g):
    weights = (c_ctx, norm_g, w_mod, b_mod, ffn_w_gate, ffn_w_up, ffn_w_down, ab_w_in, pool_w, pool_scale, q_norm_g,
               w_uq, kv_norm_g, w_ukv, ab_w_out, conv_w_in, conv_w, conv_w_out, final_norm_g)
    moms = (m_c_ctx, m_norm_g, m_w_mod, m_b_mod, m_ffn_w_gate, m_ffn_w_up, m_ffn_w_down, m_ab_w_in, m_pool_w,
            m_pool_scale, m_q_norm_g, m_w_uq, m_kv_norm_g, m_w_ukv, m_ab_w_out, m_conv_w_in, m_conv_w, m_conv_w_out,
            m_final_norm_g)
    vels = (v_c_ctx, v_norm_g, v_w_mod, v_b_mod, v_ffn_w_gate, v_ffn_w_up, v_ffn_w_down, v_ab_w_in, v_pool_w,
            v_pool_scale, v_q_norm_g, v_w_uq, v_kv_norm_g, v_w_ukv, v_ab_w_out, v_conv_w_in, v_conv_w, v_conv_w_out,
            v_final_norm_g)
    me = 4 * lax.axis_index("x") + 2 * lax.axis_index("y") + lax.axis_index("c")
    n_lat, n_ctx = x.shape[1], ctx.shape[1]
    d = D_MODEL
    mod_cols = w_mod.shape[-1]
    ng_sh, cw_sh = norm_g.shape[-1], conv_w.shape[-1]

    def ffn_shards(i):
        return {f"gate_t{i}": ffn_w_gate[i // 2, i % 2].T, f"up_t{i}": ffn_w_up[i // 2, i % 2].T,
                f"down{i}": ffn_w_down[i // 2, i % 2]}

    local = {**ffn_shards(0), "in_t": ab_w_in[0].T, "uq": w_uq[0], "ukv_t": w_ukv[0].T, "ab_out": ab_w_out[0],
             **ffn_shards(1), **ffn_shards(2), "cin_t": conv_w_in[0].T, "c_out": conv_w_out[0], **ffn_shards(3)}
    ffn_groups = [[[f"gate_t{i}", f"up_t{i}"], [f"down{i}"]] for i in range(4)]
    groups = [*ffn_groups[0], ["in_t"], ["uq", "ukv_t", "ab_out"], *ffn_groups[1], *ffn_groups[2],
              ["cin_t", "c_out"], *ffn_groups[3]]
    feed = _Feed({nm: a.astype(BF16) for nm, a in local.items()}, groups, me)

    small = jnp.concatenate([c.reshape(-1), norm_g.reshape(-1), conv_w.reshape(-1)])
    small_n = -(-small.shape[0] // 1024) * 1024
    small = jnp.pad(small, (0, small_n - small.shape[0])).reshape(small_n // 128, 128)
    small_all = _exchange("gather_small", small, False).reshape(N_DEV, small_n)
    c_all = small_all[:, :d]
    o1 = d + 6 * ng_sh
    norm_g_full = small_all[:, d:o1].reshape(N_DEV, 2, 3, ng_sh).transpose(1, 2, 0, 3).reshape(2, 3, d)
    conv_w_full = small_all[:, o1:o1 + 3 * cw_sh].reshape(N_DEV, 3, cw_sh).transpose(1, 0, 2).reshape(3, d)

    cond = jnp.concatenate([c_all, jnp.broadcast_to(c_ctx[None, :], (N_DEV, d))], axis=0)
    sil, dsil = _silu_rows("mod_silu", cond)
    w_mod_b = w_mod.astype(BF16)
    b_sh = lax.dynamic_slice(b_mod, (0, me * mod_cols), (2, mod_cols))
    m_part = jnp.stack([_mm(f"mod_fwd{l}", [(sil, w_mod_b[l])], "nn", F32, 16, 384, bias=b_sh[l:l + 1])
                        for l in range(2)], axis=1)
    m_all = _exchange("gather_mod", m_part.reshape(-1, 128), False).reshape(N_DEV, 2 * N_DEV, 2, mod_cols)
    m_mine = lax.dynamic_index_in_dim(m_all, me, axis=1, keepdims=False)
    mod_h = m_mine.transpose(1, 0, 2).reshape(2, N_MOD, d)
    mod_g = m_all[:, N_DEV, 0, :].reshape(N_MOD, d)

    first = feed.start("first", [nm for grp in groups[:3] for nm in grp], m_all)
    feed.start("rest", [nm for grp in groups[3:] for nm in grp], first)

    sq_cols, grad_x, g = _local_step(x[0], ctx[0], loss_target[0], mod_h, mod_g, norm_g_full, feed, pool_w[0],
                                  pool_scale, q_norm_g, kv_norm_g, conv_w_full, final_norm_g)
    w_of, m_of, v_of = (dict(zip(WEIGHT_NAMES, t)) for t in (weights, moms, vels))
    results = {}

    def update(nm, grad, view=lambda t: t):
        outs = _adamw(f"adamw_{nm}", view(w_of[nm]), grad.reshape(view(w_of[nm]).shape), view(m_of[nm]), view(v_of[nm]))
        results[nm] = tuple(view(t) for t in (grad.reshape(view(w_of[nm]).shape), *outs))

    def swap(t):
        return jnp.swapaxes(t, -1, -2)

    stacked = ("gate_t", "up_t", "down")
    early = feed.collect(["l1f1", "l1m", "l1f0", "l0f1", "l0m"], [grad_x], stacked)
    update("ab_w_in", early["in_t"], swap)
    update("w_uq", early["uq"])
    update("w_ukv", early["ukv_t"].T)
    update("ab_w_out", early["ab_out"])
    update("conv_w_in", early["cin_t"].T)
    update("conv_w_out", early["c_out"])
    ffn = {}
    for nm, prefix, view in (("ffn_w_gate", "gate_t", swap), ("ffn_w_up", "up_t", swap),
                             ("ffn_w_down", "down", lambda t: t)):
        w4, m4, v4 = (view(t).reshape((4,) + view(t).shape[-2:]) for t in (w_of[nm], m_of[nm], v_of[nm]))
        prev = None
        for i in (3, 2, 1):
            prev = _adamw_part(f"adamw_{nm}{i}", i, w4, early[f"{prefix}{i}"], me, m4, v4, prev)
        ffn[nm] = (prefix, view, w4, m4, v4, prev)
    done_early = [results[nm][1] for nm in results] + [state[5][1] for state in ffn.values()]
    late = feed.collect(["l0f0"], done_early, stacked)
    for nm, (prefix, view, w4, m4, v4, prev) in ffn.items():
        outs = _adamw_part(f"adamw_{nm}0", 0, w4, late[f"{prefix}0"], me, m4, v4, prev)
        results[nm] = tuple(view(t.reshape(view(w_of[nm]).shape)) for t in outs)

    dm = jnp.stack([g["mod_h"], jnp.stack([g["mod_g"], jnp.zeros_like(g["mod_g"])])])
    dm_all = _exchange("gather_dmod", dm.reshape(-1, 128), False, results["ffn_w_down"][1]).reshape(N_DEV, 2, 2, N_MOD * d)
    grad_b_mod = _sum_rows("dmod_bias", dm_all.reshape(2 * N_DEV, 2 * N_MOD * d)).reshape(2, N_MOD * d)
    dm_sh = lax.dynamic_slice(dm_all, (0, 0, 0, me * mod_cols), (N_DEV, 2, 2, mod_cols))
    gw_mod, cctx_parts = [], []
    for l in range(2):
        dm_l = dm_sh[:, :, l, :].transpose(1, 0, 2).reshape(2 * N_DEV, mod_cols).astype(BF16)
        gw_mod.append(_mm(f"mod_dw{l}", [(sil, dm_l)], "tn", F32, 512, 384))
        dm_ctx = jnp.concatenate([dm_l[N_DEV:], jnp.zeros((N_DEV, mod_cols), BF16)], axis=0)
        cctx_parts.append(_mm(f"mod_dcond{l}", [(dm_ctx, w_mod_b[l])], "nt", F32, 16, 512))
    cctx_part = _sum_rows("mod_dcond_sum", jnp.concatenate(cctx_parts, axis=0))
    update("w_mod", jnp.stack(gw_mod))
    update("b_mod", grad_b_mod)

    small_g = jnp.concatenate([g["pool_w"].reshape(-1), g["pool_scale"].reshape(-1), g["q_norm_g"].reshape(-1),
                               g["kv_norm_g"].reshape(-1), g["final_norm_g"].reshape(-1), g["norm_g"].reshape(-1),
                               g["conv_w"].reshape(-1), sq_cols.reshape(-1), cctx_part.reshape(-1)])
    sizes = [pool_w.size, pool_scale.size, q_norm_g.size, kv_norm_g.size, d, 6 * d, 3 * d, d, d]
    sg_n = -(-small_g.shape[0] // 1024) * 1024
    small_g = jnp.pad(small_g, (0, sg_n - small_g.shape[0]))
    sg_all = _exchange("gather_small_grads", small_g.reshape(-1, 128), False).reshape(N_DEV, sg_n)
    scale_vec = jnp.concatenate([jnp.ones((1, sum(sizes[:-1])), F32), dsil[N_DEV:N_DEV + 1],
                                 jnp.ones((1, sg_n - sum(sizes)), F32)], axis=1)
    sg = _sum_rows("small_grads_sum", sg_all, scale_vec)[0]
    cuts, pos = [], 0
    for sz in sizes:
        cuts.append(sg[pos:pos + sz])
        pos += sz
    g_pool_w, g_pool_scale, g_q_norm, g_kv_norm, g_final, g_norm_full, g_conv_full, sq_all, g_c_ctx = cuts
    loss = 0.5 * jnp.sum(sq_all) / d
    update("c_ctx", g_c_ctx)
    update("norm_g", lax.dynamic_slice(g_norm_full.reshape(2, 3, d), (0, 0, me * ng_sh), (2, 3, ng_sh)))
    update("conv_w", lax.dynamic_slice(g_conv_full.reshape(3, d), (0, me * cw_sh), (3, cw_sh)))
    update("pool_w", g_pool_w)
    update("pool_scale", g_pool_scale)
    update("q_norm_g", g_q_norm)
    update("kv_norm_g", g_kv_norm)
    update("final_norm_g", g_final)
    outs = [results[nm] for nm in WEIGHT_NAMES]
    return (loss, grad_x[None], *[o[0] for o in outs], *[o[1] for o in outs], *[o[2] for o in outs],
            *[o[3] for o in outs])
```

```python
import functools
import math

import jax
import jax.numpy as jnp
import numpy as np
from jax import lax
from jax.experimental import pallas as pl
from jax.experimental.pallas import tpu as pltpu

F32 = jnp.float32
BF16 = jnp.bfloat16
MESH = pl.DeviceIdType.MESH
SDS = jax.ShapeDtypeStruct

N_DEV = 8
D_MODEL = 1024
N_MOD = 9
D_FF = 2816
POOL_WINDOWS = (2, 4, 8, 16)
POOL_DIM = 512
POOL_GROUP_DIM = 128
HEADS = 8
QK_NOPE = 64
QK_ROPE = 32
QK_HEAD = QK_NOPE + QK_ROPE
V_HEAD = 64
Q_RANK = 768
KV_RANK = 256
GRID_W = 64
ROPE_THETA = 10000.0
RMS_EPS = 1e-6
ATTN_SCALE = 1.0 / math.sqrt(QK_HEAD)
HEAD_PAD = 128
POOL_PAD = 16
ATTN_Q_ROWS_FWD = 256
ATTN_Q_ROWS_BWD = 512
PA_POOL, PA_CQ, PA_KV = 0, 768, 1536
PA_KV_W = 384
PA_W = PA_KV + PA_KV_W

ADAM_LR, ADAM_B1, ADAM_B2, ADAM_EPS, ADAM_WD, ADAM_STEP = 0.001, 0.9, 0.999, 1e-08, 0.01, 10

VMEM_LIMIT_BYTES = 56 * 1024 * 1024

NN = ((1,), (0,))
NT = ((1,), (1,))
TN = ((0,), (0,))


def _cparams():
    return pltpu.CompilerParams(vmem_limit_bytes=VMEM_LIMIT_BYTES)


def _dot(a, b, dims):
    return lax.dot_general(a, b, (dims, ((), ())), preferred_element_type=F32)


def _tile(n, cap, mult=8):
    t = (min(cap, n) // mult) * mult
    while t >= mult:
        if n % t == 0:
            return t
        t -= mult
    return n


def _colsum(x):
    return jnp.sum(x, axis=0, keepdims=True)


def _rms(x):
    r = lax.rsqrt(jnp.mean(x * x, axis=-1, keepdims=True) + RMS_EPS)
    return x * r, r


def _rms_bwd(n, r, dn):
    return r * (dn - n * jnp.mean(dn * n, axis=-1, keepdims=True))


def _rowwise(name, fn, t_rows, tm, n_lat, rows, vecs, outs, accs, into=None):
    nt = t_rows // tm
    nlt = n_lat // tm
    n_groups = 2 if nlt < nt else 1

    def grp(i):
        return jnp.where(i >= nlt, 1, 0) if n_groups == 2 else 0

    in_specs = [pl.BlockSpec((tm, w), functools.partial(lambda i, cb: (i, cb), cb=cb)) for (_, w, cb) in rows]
    in_specs += [pl.BlockSpec((1,) + v.shape[1:], lambda i: (grp(i), 0, 0)) for v in vecs]
    out_specs = [pl.BlockSpec((tm, w), lambda i: (i, 0)) for (w, _) in outs]
    out_specs += [pl.BlockSpec((1, 1, w), lambda i: (grp(i), 0, 0)) for w in accs]
    out_shape = [SDS((t_rows, w), dt) for (w, dt) in outs] + [SDS((n_groups, 1, w), F32) for w in accs]
    n_r, n_v, n_o = len(rows), len(vecs), len(outs)
    extra, aliases = [], {}
    if into is not None:
        extra, aliases = [into[0]], {n_r + n_v: 0}
        in_specs.append(pl.BlockSpec(memory_space=pl.ANY))
        out_specs[0] = pl.BlockSpec((tm, outs[0][0]), lambda i: (i, into[1]))
        out_shape[0] = SDS(into[0].shape, into[0].dtype)
    n_in = n_r + n_v + len(extra)

    def body(*refs):
        row_vals = [r[...] for r in refs[:n_r]]
        vec_vals = [v[0] for v in refs[n_r:n_r + n_v]]
        out_refs = refs[n_in:n_in + n_o]
        acc_refs = refs[n_in + n_o:]
        out_vals, acc_vals = fn(row_vals, vec_vals)
        for o_ref, o in zip(out_refs, out_vals):
            o_ref[...] = o.astype(o_ref.dtype)
        if acc_refs:
            i = pl.program_id(0)
            first = (i == 0) | (i == nlt) if n_groups == 2 else i == 0

            @pl.when(first)
            def _():
                for a_ref, a in zip(acc_refs, acc_vals):
                    a_ref[0] = a

            @pl.when(jnp.logical_not(first))
            def _():
                for a_ref, a in zip(acc_refs, acc_vals):
                    a_ref[0] += a

    res = pl.pallas_call(
        body, name=name, grid=(nt,), in_specs=in_specs, out_specs=out_specs, out_shape=out_shape,
        input_output_aliases=aliases, compiler_params=_cparams(),
    )(*[r[0] for r in rows], *vecs, *extra)
    return res[:n_o], res[n_o:]


RESIDENT_BYTES = 12 * 1024 * 1024


def _mm(name, pairs, mode, out_dtype, tm_cap=256, tn_cap=512, bias=None):
    a0, b0 = pairs[0]
    if mode == "nn":
        m, n, dims = a0.shape[0], b0.shape[1], NN
    elif mode == "nt":
        m, n, dims = a0.shape[0], b0.shape[0], NT
    else:
        m, n, dims = a0.shape[1], b0.shape[1], TN
    b_bytes = sum(b.size * b.dtype.itemsize for _, b in pairs)
    tn = n if b_bytes <= RESIDENT_BYTES else _tile(n, tn_cap, 128)
    tm = _tile(m, tm_cap, 128 if mode == "tn" else 16)

    def a_spec(a):
        if mode == "tn":
            return pl.BlockSpec((a.shape[0], tm), lambda i, j: (0, i))
        return pl.BlockSpec((tm, a.shape[1]), lambda i, j: (i, 0))

    def b_spec(b):
        if mode == "nt":
            return pl.BlockSpec((tn, b.shape[1]), lambda i, j: (j, 0))
        return pl.BlockSpec((b.shape[0], tn), lambda i, j: (0, j))

    in_specs, flat = [], []
    for a, b in pairs:
        in_specs += [a_spec(a), b_spec(b)]
        flat += [a, b]
    if bias is not None:
        in_specs.append(pl.BlockSpec((1, tn), lambda i, j: (0, j)))
        flat.append(bias)
    n_pairs = len(pairs)

    def body(*refs):
        acc = None
        for p in range(n_pairs):
            t = _dot(refs[2 * p][...], refs[2 * p + 1][...], dims)
            acc = t if acc is None else acc + t
        if bias is not None:
            acc = acc + refs[2 * n_pairs][...]
        refs[-1][...] = acc.astype(refs[-1].dtype)

    return pl.pallas_call(
        body, name=name, grid=(m // tm, n // tn), in_specs=in_specs,
        out_specs=pl.BlockSpec((tm, tn), lambda i, j: (i, j)),
        out_shape=SDS((m, n), out_dtype), compiler_params=_cparams(),
    )(*flat)


def _mm_resid(name, pairs, s, mg, k, coef, n_lat):
    t_rows, n = pairs[0][0].shape[0], s.shape[1]
    n_pairs = len(pairs)
    tm = _tile(math.gcd(n_lat, t_rows), 256, 16)
    nlt = n_lat // tm
    n_groups = 2 if nlt < t_rows // tm else 1

    def grp(i):
        return jnp.where(i >= nlt, 1, 0) if n_groups == 2 else 0

    def body(*refs):
        s_ref, mg_ref, so_ref, o_ref = refs[2 * n_pairs:]
        o = _dot(refs[0][...], refs[n_pairs][...], NN)
        for p in range(1, n_pairs):
            o = o + _dot(refs[p][...], refs[n_pairs + p][...], NN)
        gate = mg_ref[0, 3 * k + 2:3 * k + 3, :]
        o_ref[...] = o.astype(BF16)
        so_ref[...] = s_ref[...] + (coef * gate) * o

    row = pl.BlockSpec((tm, n), lambda i: (i, 0))
    return pl.pallas_call(
        body, name=name, grid=(t_rows // tm,),
        in_specs=[pl.BlockSpec((tm, a.shape[1]), lambda i: (i, 0)) for a, _ in pairs]
        + [pl.BlockSpec(b.shape, lambda i: (0, 0)) for _, b in pairs]
        + [row, pl.BlockSpec((1, mg.shape[1], n), lambda i: (grp(i), 0, 0))],
        out_specs=[row, row], out_shape=[SDS((t_rows, n), F32), SDS((t_rows, n), BF16)], compiler_params=_cparams(),
    )(*[a for a, _ in pairs], *[b for _, b in pairs], s, mg)


def _dw_pair(name, a1, a2, b):
    kk, m = a1.shape
    n = b.shape[1]
    tm = _tile(m, 256, 128)

    def body(a1_ref, a2_ref, b_ref, o1_ref, o2_ref):
        bb = b_ref[...]
        o1_ref[...] = _dot(a1_ref[...], bb, TN).astype(BF16)
        o2_ref[...] = _dot(a2_ref[...], bb, TN).astype(BF16)

    col = pl.BlockSpec((kk, tm), lambda i: (0, i))
    out = pl.BlockSpec((tm, n), lambda i: (i, 0))
    return pl.pallas_call(
        body, name=name, grid=(m // tm,), in_specs=[col, col, pl.BlockSpec(b.shape, lambda i: (0, 0))],
        out_specs=[out, out], out_shape=[SDS((m, n), BF16)] * 2, compiler_params=_cparams(),
    )(a1, a2, b)


def _groups(t_rows, tm, n_lat):
    nlt = n_lat // tm
    if nlt < t_rows // tm:
        return 2, (lambda i: jnp.where(i >= nlt, 1, 0)), (lambda i: (i == 0) | (i == nlt))
    return 1, (lambda i: 0), (lambda i: i == 0)


def _accumulate(acc_refs, vals, first):
    @pl.when(first)
    def _():
        for r, v in zip(acc_refs, vals):
            r[0] = v

    @pl.when(jnp.logical_not(first))
    def _():
        for r, v in zip(acc_refs, vals):
            r[0] += v


def _adaln_math(s, m, k):
    n, _ = _rms(s)
    return (n * m[9 + k:10 + k]) * (1.0 + m[3 * k + 1:3 * k + 2]) + m[3 * k:3 * k + 1]


def _ffn_up(name, s, mg, k, n_lat, wg_t, wu_t):
    t_rows, f = s.shape[0], wg_t.shape[0]
    tm = _row_tm(t_rows, n_lat)
    _, grp, _ = _groups(t_rows, tm, n_lat)

    def body(s_ref, mg_ref, wg_ref, wu_ref, u_ref, a_ref, b_ref, h_ref):
        uu = _adaln_math(s_ref[...], mg_ref[0], k).astype(BF16)
        u_ref[...] = uu
        a = _dot(uu, wg_ref[...], NT)
        b = _dot(uu, wu_ref[...], NT)
        sg = jax.nn.sigmoid(a)
        act = a * sg
        a_ref[...] = (b * (sg * (1.0 + a * (1.0 - sg)))).astype(BF16)
        b_ref[...] = act.astype(BF16)
        h_ref[...] = (act * b).astype(BF16)

    w_spec = pl.BlockSpec(wg_t.shape, lambda i: (0, 0))
    o_spec = pl.BlockSpec((tm, f), lambda i: (i, 0))
    row = pl.BlockSpec((tm, s.shape[1]), lambda i: (i, 0))
    return pl.pallas_call(
        body, name=name, grid=(t_rows // tm,),
        in_specs=[row, pl.BlockSpec((1,) + mg.shape[1:], lambda i: (grp(i), 0, 0)), w_spec, w_spec],
        out_specs=[row, o_spec, o_spec, o_spec],
        out_shape=[SDS(s.shape, BF16)] + [SDS((t_rows, f), BF16)] * 3, compiler_params=_cparams(),
    )(s, mg, wg_t, wu_t)


def _ffn_dact(name, ds_out, o, mg, k, coef, n_lat, wd, a, b):
    t_rows, f = ds_out.shape[0], wd.shape[0]
    tm = _row_tm(t_rows, n_lat)
    n_groups, grp, first = _groups(t_rows, tm, n_lat)
    d = ds_out.shape[1]

    def body(ds_ref, o_ref, mg_ref, wd_ref, a_ref, b_ref, do_ref, da_ref, db_ref, dg_ref):
        dd = coef * ds_ref[...]
        do = (dd * mg_ref[0, 3 * k + 2:3 * k + 3, :]).astype(BF16)
        do_ref[...] = do
        _accumulate([dg_ref], [_colsum(dd * o_ref[...].astype(F32))], first(pl.program_id(0)))
        dh = _dot(do, wd_ref[...], NT)
        da_ref[...] = (dh * a_ref[...].astype(F32)).astype(BF16)
        db_ref[...] = (dh * b_ref[...].astype(F32)).astype(BF16)

    row = pl.BlockSpec((tm, d), lambda i: (i, 0))
    t_spec = pl.BlockSpec((tm, f), lambda i: (i, 0))
    return pl.pallas_call(
        body, name=name, grid=(t_rows // tm,),
        in_specs=[row, row, pl.BlockSpec((1,) + mg.shape[1:], lambda i: (grp(i), 0, 0)),
                  pl.BlockSpec(wd.shape, lambda i: (0, 0)), t_spec, t_spec],
        out_specs=[row, t_spec, t_spec, pl.BlockSpec((1, 1, d), lambda i: (grp(i), 0, 0))],
        out_shape=[SDS((t_rows, d), BF16), SDS((t_rows, f), BF16), SDS((t_rows, f), BF16), SDS((n_groups, 1, d), F32)],
        compiler_params=_cparams(),
    )(ds_out, o, mg, wd, a, b)


def _du_adaln(name, pairs, s, ds_out, mg, k, n_lat, after, out_rows=None):
    t_rows, d = s.shape
    tm = _row_tm(t_rows, n_lat)
    n_groups, grp, first = _groups(t_rows, tm, n_lat)
    n_pairs = len(pairs)
    nt, n_ds, n_out = t_rows // tm, ds_out.shape[0] // tm, (out_rows or t_rows) // tm

    def body(*refs):
        s_ref, ds_ref, mg_ref, z_ref, out_ref, dsh_ref, dsc_ref, dgn_ref = refs[2 * n_pairs:]
        i = pl.program_id(0)
        d_u = z_ref[...]
        for p in range(n_pairs):
            d_u = d_u + _dot(refs[p][...], refs[n_pairs + p][...], NN)
        m = mg_ref[0]
        gain, scale = m[9 + k:10 + k], m[3 * k + 1:3 * k + 2]
        n, r = _rms(s_ref[...])
        dxn = d_u * (1.0 + scale)
        ds_in = _rms_bwd(n, r, dxn * gain)
        ds_in = ds_in + (ds_ref[...] if n_ds == nt else jnp.where(i < n_ds, ds_ref[...], 0.0))
        if n_out == nt:
            out_ref[...] = ds_in
        else:
            @pl.when(i < n_out)
            def _():
                out_ref[...] = ds_in
        _accumulate([dsh_ref, dsc_ref, dgn_ref], [_colsum(d_u), _colsum(d_u * (n * gain)), _colsum(dxn * n)], first(i))

    row = pl.BlockSpec((tm, d), lambda i: (i, 0))
    acc = pl.BlockSpec((1, 1, d), lambda i: (grp(i), 0, 0))
    res = pl.pallas_call(
        body, name=name, grid=(t_rows // tm,),
        in_specs=[pl.BlockSpec((tm, a.shape[1]), lambda i: (i, 0)) for a, _ in pairs]
        + [pl.BlockSpec(w.shape, lambda i: (0, 0)) for _, w in pairs]
        + [row, pl.BlockSpec((tm, d), lambda i: (jnp.minimum(i, n_ds - 1), 0)),
           pl.BlockSpec((1,) + mg.shape[1:], lambda i: (grp(i), 0, 0)), pl.BlockSpec((1, d), lambda i: (0, 0))],
        out_specs=[pl.BlockSpec((tm, d), lambda i: (jnp.minimum(i, n_out - 1), 0)), acc, acc, acc],
        out_shape=[SDS((n_out * tm, d), F32)] + [SDS((n_groups, 1, d), F32)] * 3, compiler_params=_cparams(),
    )(*[a for a, _ in pairs], *[w for _, w in pairs], s, ds_out, mg, after)
    return res[0], res[1:]


def _adaln_mm(name, s, mg, k, n_lat, w_t):
    rows, d = s.shape
    tm = _row_tm(rows, n_lat)
    _, grp, _ = _groups(rows, tm, n_lat)
    n = w_t.shape[0]

    def body(s_ref, mg_ref, w_ref, u_ref, y_ref):
        uu = _adaln_math(s_ref[...], mg_ref[0], k).astype(BF16)
        u_ref[...] = uu
        y_ref[...] = _dot(uu, w_ref[...], NT)

    row = pl.BlockSpec((tm, d), lambda i: (i, 0))
    return pl.pallas_call(
        body, name=name, grid=(rows // tm,),
        in_specs=[row, pl.BlockSpec((1,) + mg.shape[1:], lambda i: (grp(i), 0, 0)), pl.BlockSpec(w_t.shape, lambda i: (0, 0))],
        out_specs=[row, pl.BlockSpec((tm, n), lambda i: (i, 0))],
        out_shape=[SDS((rows, d), BF16), SDS((rows, n), F32)], compiler_params=_cparams(),
    )(s, mg, w_t)


def _gate_mm(name, ds_out, o, mg, k, coef, n_lat, w):
    t_rows, d = ds_out.shape
    tm = _row_tm(t_rows, n_lat)
    n_groups, grp, first = _groups(t_rows, tm, n_lat)
    n = w.shape[0]

    def body(ds_ref, o_ref, mg_ref, w_ref, do_ref, y_ref, dg_ref):
        dd = coef * ds_ref[...]
        do = (dd * mg_ref[0, 3 * k + 2:3 * k + 3, :]).astype(BF16)
        do_ref[...] = do
        _accumulate([dg_ref], [_colsum(dd * o_ref[...].astype(F32))], first(pl.program_id(0)))
        y_ref[...] = _dot(do, w_ref[...], NT)

    row = pl.BlockSpec((tm, d), lambda i: (i, 0))
    return pl.pallas_call(
        body, name=name, grid=(t_rows // tm,),
        in_specs=[row, row, pl.BlockSpec((1,) + mg.shape[1:], lambda i: (grp(i), 0, 0)), pl.BlockSpec(w.shape, lambda i: (0, 0))],
        out_specs=[row, pl.BlockSpec((tm, n), lambda i: (i, 0)), pl.BlockSpec((1, 1, d), lambda i: (grp(i), 0, 0))],
        out_shape=[SDS((t_rows, d), BF16), SDS((t_rows, n), F32), SDS((n_groups, 1, d), F32)],
        compiler_params=_cparams(),
    )(ds_out, o, mg, w)


def _row_tm(t_rows, n_lat):
    return _tile(math.gcd(t_rows, n_lat), 256, 16)


def _rmsnorm_fwd(name, x, width, colblk, gain, t_rows):
    def fn(rv, vv):
        n, _ = _rms(rv[0])
        return [n * vv[0]], []

    (y,), _ = _rowwise(name, fn, t_rows, _tile(t_rows, 256, 16), t_rows, [(x, width, colblk)],
                       [gain.reshape(1, 1, width)], [(width, BF16)], [])
    return y


def _rmsnorm_bwd(name, x, width, colblk, dy, gain, t_rows, out_dtype=F32, into=None):
    def fn(rv, vv):
        n, r = _rms(rv[0])
        return [_rms_bwd(n, r, rv[1] * vv[0])], [_colsum(rv[1] * n)]

    (dx,), (dgain,) = _rowwise(name, fn, t_rows, _tile(t_rows, 256, 16), t_rows,
                               [(x, width, colblk), (dy, width, 0)], [gain.reshape(1, 1, width)],
                               [(width, out_dtype)], [width], into)
    return dx, dgain


def _final_loss(name, h, target, gain):
    t_rows = h.shape[0]
    inv_d = 1.0 / D_MODEL

    def fn(rv, vv):
        g = vv[0]
        n, r = _rms(rv[0])
        e = n * g - rv[1]
        dy = e * inv_d
        return [_rms_bwd(n, r, dy * g)], [_colsum(e * e), _colsum(dy * n)]

    (dh,), (sq, dgain) = _rowwise(name, fn, t_rows, _tile(t_rows, 256, 16), t_rows,
                                  [(h, D_MODEL, 0), (target, D_MODEL, 0)], [gain.reshape(1, 1, D_MODEL)],
                                  [(D_MODEL, F32)], [D_MODEL, D_MODEL])
    return dh, sq, dgain


def _exact_dot(x, m_ref):
    hi = x.astype(BF16)
    lo = (x - hi.astype(F32)).astype(BF16)
    return _dot(hi, m_ref[...], NN) + _dot(lo, m_ref[...], NN)


def _rope(name, z, width, colblk, cos32, sin32, layout, backward, out_dtype, remap=None, into=None):
    t_rows = cos32.shape[0]
    expand, plain, perm = layout
    w_in = remap.shape[1] if (remap is not None and backward) else width
    w_out = remap.shape[1] if (remap is not None and not backward) else width
    extra = [] if remap is None else [remap.T if backward else remap]
    dest = [] if into is None else [into[0]]

    def body(z_ref, c_ref, s_ref, e_ref, m_ref, p_ref, *rest):
        o_ref = rest[-1]
        zz = z_ref[...]
        if remap is not None and backward:
            zz = _exact_dot(zz, rest[0])
        cos = _exact_dot(c_ref[...], e_ref) + m_ref[...]
        sin = _exact_dot(s_ref[...], e_ref)
        rot = _exact_dot(zz * sin if backward else zz, p_ref)
        if not backward:
            rot = rot * sin
        res = zz * cos + rot
        if remap is not None and not backward:
            res = _dot(res.astype(BF16), rest[0][...], NN)
        o_ref[...] = res.astype(o_ref.dtype)

    tm = _tile(t_rows, 256, 16)
    f_spec = pl.BlockSpec((tm, QK_ROPE), lambda i: (i, 0))
    return pl.pallas_call(
        body, name=name, grid=(t_rows // tm,),
        in_specs=[pl.BlockSpec((tm, w_in), lambda i: (i, colblk)), f_spec, f_spec,
                  pl.BlockSpec((QK_ROPE, width), lambda i: (0, 0)), pl.BlockSpec((1, width), lambda i: (0, 0)),
                  pl.BlockSpec((width, width), lambda i: (0, 0))]
        + [pl.BlockSpec(e.shape, lambda i: (0, 0)) for e in extra] + [pl.BlockSpec(memory_space=pl.ANY)] * len(dest),
        out_specs=pl.BlockSpec((tm, w_out), lambda i: (i, 0 if into is None else into[1])),
        out_shape=SDS((t_rows, w_out), out_dtype) if into is None else SDS(into[0].shape, into[0].dtype),
        input_output_aliases={} if into is None else {6 + len(extra): 0}, compiler_params=_cparams(),
    )(z, cos32, sin32, expand, plain, perm.T if backward else perm, *extra, *dest)


def _window_sum(x, w, transposed):
    n_rows = x.shape[0]
    zeros = jnp.zeros((POOL_PAD, x.shape[1]), F32)
    y = jnp.concatenate([zeros, x, zeros], axis=0)
    total = n_rows + 2 * POOL_PAD
    if transposed:
        y = y + pltpu.roll(y, total - 1, 0)
    else:
        y = y + pltpu.roll(y, 1, 0)
    step = 1
    while 2 * step < w:
        y = pltpu.roll(y, step, 0) + pltpu.roll(y, total - step, 0)
        step *= 2
    return y[POOL_PAD:POOL_PAD + n_rows]


def _window_count(n_rows, w):
    t = lax.broadcasted_iota(jnp.int32, (n_rows, 1), 0)
    lo = jnp.maximum(t - w // 2, 0)
    hi = jnp.minimum(t + (w - w // 2 - 1), n_rows - 1)
    return (hi - lo + 1).astype(F32)


def _pool_fwd(name, proj, n_rows, w_grp, scale):
    def body(x_ref, w_ref, sc_ref, y_ref, p_ref):
        for g, w in enumerate(POOL_WINDOWS):
            cols = slice(g * POOL_GROUP_DIM, (g + 1) * POOL_GROUP_DIM)
            x = x_ref[:, cols]
            p = _window_sum(x, w, False) * (1.0 / _window_count(n_rows, w)) - x
            pb = p.astype(BF16)
            p_ref[:, cols] = pb
            y_ref[:, cols] = (_dot(pb, w_ref[g], NN) * sc_ref[:, cols]).astype(BF16)

    blk = pl.BlockSpec((n_rows, POOL_DIM), lambda i: (0, 0))
    return pl.pallas_call(
        body, name=name, grid=(1,),
        in_specs=[blk, pl.BlockSpec(w_grp.shape, lambda i: (0, 0, 0)), pl.BlockSpec((1, POOL_DIM), lambda i: (0, 0))],
        out_specs=[blk, blk], out_shape=[SDS((n_rows, POOL_DIM), BF16)] * 2, compiler_params=_cparams(),
    )(proj, w_grp, scale)


def _pool_bwd(name, dcat, n_rows, p, w_grp, scale, into):
    def body(dy_ref, p_ref, w_ref, sc_ref, into_ref, dx_ref, dw_ref, dsc_ref):
        for g, w in enumerate(POOL_WINDOWS):
            cols = slice(g * POOL_GROUP_DIM, (g + 1) * POOL_GROUP_DIM)
            dy = dy_ref[:, cols]
            pb = p_ref[:, cols]
            pw = _dot(pb, w_ref[g], NN)
            dsc_ref[:, cols] = _colsum(dy * pw)
            dpw = (dy * sc_ref[:, cols]).astype(BF16)
            dw_ref[g] = _dot(pb, dpw, TN)
            dp = _dot(dpw, w_ref[g], NT)
            dx_ref[:, cols] = (_window_sum(dp * (1.0 / _window_count(n_rows, w)), w, True) - dp).astype(BF16)

    blk = pl.BlockSpec((n_rows, POOL_DIM), lambda i: (0, 0))
    w_spec = pl.BlockSpec(w_grp.shape, lambda i: (0, 0, 0))
    v_spec = pl.BlockSpec((1, POOL_DIM), lambda i: (0, 0))
    return pl.pallas_call(
        body, name=name, grid=(1,), in_specs=[blk, blk, w_spec, v_spec, pl.BlockSpec(memory_space=pl.ANY)],
        out_specs=[blk, w_spec, v_spec],
        out_shape=[SDS(into.shape, into.dtype), SDS(w_grp.shape, F32), SDS((1, POOL_DIM), F32)],
        input_output_aliases={4: 0}, compiler_params=_cparams(),
    )(dcat, p, w_grp, scale, into)


def _head_keys(kv_blk, k_rope):
    lane = lax.broadcasted_iota(jnp.int32, (1, HEAD_PAD), 1)
    return jnp.where(lane < QK_NOPE, kv_blk, k_rope)


def _attn_fwd(name, q, kv, k_rope, n_q):
    n_k = kv.shape[0]
    h = kv.shape[1] // HEAD_PAD
    tq = _tile(n_q, ATTN_Q_ROWS_FWD, 16)

    def body(q_ref, kv_ref, kr_ref, o_ref, lse_ref):
        kvb = kv_ref[...]
        s = _dot(q_ref[...], _head_keys(kvb, kr_ref[...]), NT) * ATTN_SCALE
        m = jnp.max(s, axis=-1, keepdims=True)
        e = jnp.exp(s - m)
        l = jnp.sum(e, axis=-1, keepdims=True)
        p = (e * (1.0 / l)).astype(BF16)
        lane = lax.broadcasted_iota(jnp.int32, (1, HEAD_PAD), 1)
        o_ref[...] = jnp.where(lane >= QK_NOPE, _dot(p, kvb, NN), 0.0).astype(BF16)
        lse_ref[...] = m + jnp.log(l)

    blk = pl.BlockSpec((tq, HEAD_PAD), lambda hh, i: (i, hh))
    return pl.pallas_call(
        body, name=name, grid=(h, n_q // tq),
        in_specs=[blk, pl.BlockSpec((n_k, HEAD_PAD), lambda hh, i: (0, hh)),
                  pl.BlockSpec((n_k, HEAD_PAD), lambda hh, i: (0, 0))],
        out_specs=[blk, pl.BlockSpec((None, tq, 1), lambda hh, i: (hh, i, 0))],
        out_shape=[SDS((n_q, h * HEAD_PAD), BF16), SDS((h, n_q, 1), F32)], compiler_params=_cparams(),
    )(q, kv, k_rope)


def _attn_bwd(name, q, kv, k_rope, o, lse, dy, dy_col0, n_q):
    n_k = kv.shape[0]
    h = kv.shape[1] // HEAD_PAD
    tq = _tile(n_q, ATTN_Q_ROWS_BWD, 16)
    n_i = n_q // tq

    def body(q_ref, kv_ref, kr_ref, o_ref, lse_ref, do_ref, dq_ref, dkv_ref, dkr_ref, acc_k, acc_v):
        hh, i = pl.program_id(0), pl.program_id(1)
        qq, kvb = q_ref[...], kv_ref[...]
        kk = _head_keys(kvb, kr_ref[...])
        d_o = do_ref[...]
        dd = d_o.astype(BF16)
        s = _dot(qq, kk, NT) * ATTN_SCALE
        p = jnp.exp(s - lse_ref[...])
        dp = _dot(dd, kvb, NT)
        delta = jnp.sum(d_o * o_ref[...].astype(F32), axis=-1, keepdims=True)
        ds = (p * (dp - delta) * ATTN_SCALE).astype(BF16)
        dq_ref[...] = _dot(ds, kk, NN)
        dk = _dot(ds, qq, TN)
        dv = _dot(p.astype(BF16), dd, TN)

        @pl.when(i == 0)
        def _():
            acc_k[...] = dk
            acc_v[...] = dv

        @pl.when(i > 0)
        def _():
            acc_k[...] += dk
            acc_v[...] += dv

        @pl.when(i == n_i - 1)
        def _():
            lane = lax.broadcasted_iota(jnp.int32, (1, HEAD_PAD), 1)
            dkv_ref[...] = jnp.where(lane < QK_NOPE, acc_k[...], acc_v[...]).astype(BF16)
            rope = jnp.where((lane >= QK_NOPE) & (lane < QK_HEAD), acc_k[...], 0.0)

            @pl.when(hh == 0)
            def _():
                dkr_ref[...] = rope

            @pl.when(hh > 0)
            def _():
                dkr_ref[...] += rope

    blk = pl.BlockSpec((tq, HEAD_PAD), lambda hh, i: (i, hh))
    kv_spec = pl.BlockSpec((n_k, HEAD_PAD), lambda hh, i: (0, hh))
    shared = pl.BlockSpec((n_k, HEAD_PAD), lambda hh, i: (0, 0))
    return pl.pallas_call(
        body, name=name, grid=(h, n_i),
        in_specs=[blk, kv_spec, shared, blk, pl.BlockSpec((None, tq, 1), lambda hh, i: (hh, i, 0)),
                  pl.BlockSpec((tq, HEAD_PAD), lambda hh, i: (i, dy_col0 + hh))],
        out_specs=[blk, kv_spec, shared],
        out_shape=[SDS((n_q, h * HEAD_PAD), F32), SDS((n_k, h * HEAD_PAD), BF16), SDS((n_k, HEAD_PAD), F32)],
        scratch_shapes=[pltpu.VMEM((n_k, HEAD_PAD), F32), pltpu.VMEM((n_k, HEAD_PAD), F32)],
        compiler_params=_cparams(),
    )(q, kv, k_rope, o, lse, dy)


CONV_COLS = 256


def _shift_rows(x, d):
    n_rows = x.shape[0]
    t = lax.broadcasted_iota(jnp.int32, (n_rows, 1), 0)
    if d > 0:
        return jnp.where(t >= d, pltpu.roll(x, d, 0), 0.0)
    return jnp.where(t < n_rows + d, pltpu.roll(x, n_rows + d, 0), 0.0)


def _conv_fwd(name, z3, conv_w):
    n_rows = z3.shape[0]
    nb = D_MODEL // CONV_COLS

    def body(b_ref, c_ref, v_ref, w_ref, y_ref):
        z = c_ref[...] * v_ref[...]
        zc = w_ref[0:1, :] * _shift_rows(z, 1) + w_ref[1:2, :] * z + w_ref[2:3, :] * _shift_rows(z, -1)
        y_ref[...] = (b_ref[...] * zc).astype(BF16)

    def part(k):
        return pl.BlockSpec((n_rows, CONV_COLS), lambda j: (0, k * nb + j))

    return pl.pallas_call(
        body, name=name, grid=(nb,),
        in_specs=[part(0), part(1), part(2), pl.BlockSpec((3, CONV_COLS), lambda j: (0, j))],
        out_specs=pl.BlockSpec((n_rows, CONV_COLS), lambda j: (0, j)),
        out_shape=SDS((n_rows, D_MODEL), BF16), compiler_params=_cparams(),
    )(z3, z3, z3, conv_w)


def _conv_bwd(name, dy, z3, conv_w):
    n_rows = z3.shape[0]
    nb = D_MODEL // CONV_COLS

    def body(dy_ref, b_ref, c_ref, v_ref, w_ref, db_ref, dc_ref, dv_ref, dw_ref):
        c, v, d_y = c_ref[...], v_ref[...], dy_ref[...]
        z = c * v
        z_dn, z_up = _shift_rows(z, 1), _shift_rows(z, -1)
        zc = w_ref[0:1, :] * z_dn + w_ref[1:2, :] * z + w_ref[2:3, :] * z_up
        db_ref[...] = (d_y * zc).astype(BF16)
        dzc = d_y * b_ref[...]
        dz = w_ref[0:1, :] * _shift_rows(dzc, -1) + w_ref[1:2, :] * dzc + w_ref[2:3, :] * _shift_rows(dzc, 1)
        dc_ref[...] = (dz * v).astype(BF16)
        dv_ref[...] = (dz * c).astype(BF16)
        dw_ref[0:1, :] = _colsum(dzc * z_dn)
        dw_ref[1:2, :] = _colsum(dzc * z)
        dw_ref[2:3, :] = _colsum(dzc * z_up)

    def part(k):
        return pl.BlockSpec((n_rows, CONV_COLS), lambda j: (0, k * nb + j))

    col = pl.BlockSpec((n_rows, CONV_COLS), lambda j: (0, j))
    w_spec = pl.BlockSpec((3, CONV_COLS), lambda j: (0, j))
    return pl.pallas_call(
        body, name=name, grid=(nb,), in_specs=[col, part(0), part(1), part(2), w_spec],
        out_specs=[col, col, col, w_spec],
        out_shape=[SDS((n_rows, D_MODEL), BF16)] * 3 + [SDS((3, D_MODEL), F32)], compiler_params=_cparams(),
    )(dy, z3, z3, z3, conv_w)


def _silu_rows(name, x):
    def body(x_ref, s_ref, d_ref):
        xx = x_ref[...]
        sg = jax.nn.sigmoid(xx)
        s_ref[...] = (xx * sg).astype(BF16)
        d_ref[...] = sg * (1.0 + xx * (1.0 - sg))

    return pl.pallas_call(body, name=name, out_shape=[SDS(x.shape, BF16), SDS(x.shape, F32)])(x)


def _sum_rows(name, x, scale=None):
    r, n = x.shape
    tn = _tile(n, 32768, 128)

    def body(*refs):
        acc = jnp.sum(refs[0][...].astype(F32), axis=0, keepdims=True)
        if scale is not None:
            acc = acc * refs[1][...]
        refs[-1][...] = acc

    in_specs = [pl.BlockSpec((r, tn), lambda j: (0, j))]
    args = [x]
    if scale is not None:
        in_specs.append(pl.BlockSpec((1, tn), lambda j: (0, j)))
        args.append(scale)
    return pl.pallas_call(body, name=name, grid=(n // tn,), in_specs=in_specs,
                          out_specs=pl.BlockSpec((1, tn), lambda j: (0, j)), out_shape=SDS((1, n), F32))(*args)


def _me_operand(me):
    return jnp.reshape(me, (1,)).astype(jnp.int32)


def _sum_slots(name, slots, src, me):
    n_slots, r, c = slots.shape
    tr = _tile(r, 432, 16)

    def body(me_ref, own_ref, x_ref, o_ref):
        acc = own_ref[...].astype(F32)
        for sl in range(n_slots):
            acc = acc + x_ref[sl].astype(F32)
        o_ref[...] = acc

    grid_spec = pltpu.PrefetchScalarGridSpec(
        num_scalar_prefetch=1, grid=(r // tr,),
        in_specs=[pl.BlockSpec((None, tr, c), lambda i, me_ref: (me_ref[0], i, 0)),
                  pl.BlockSpec((n_slots, tr, c), lambda i, me_ref: (0, i, 0))],
        out_specs=pl.BlockSpec((tr, c), lambda i, me_ref: (i, 0)))
    return pl.pallas_call(body, name=name, grid_spec=grid_spec, out_shape=SDS((r, c), F32),
                          compiler_params=_cparams())(_me_operand(me), src, slots)


def _adamw(name, w, g, m, v):
    shape = w.shape
    cols = shape[-1]
    rows = w.size // cols
    tr = _tile(rows, 512, 8)
    bc1 = 1.0 - ADAM_B1 ** ADAM_STEP
    bc2 = 1.0 - ADAM_B2 ** ADAM_STEP

    def body(w_ref, g_ref, m_ref, v_ref, d_ref, nm_ref, nv_ref):
        gg = g_ref[...]
        nm = ADAM_B1 * m_ref[...] + (1.0 - ADAM_B1) * gg
        nv = ADAM_B2 * v_ref[...] + (1.0 - ADAM_B2) * (gg * gg)
        nm_ref[...] = nm
        nv_ref[...] = nv
        d_ref[...] = -ADAM_LR * ((nm / bc1) / (jnp.sqrt(nv / bc2) + ADAM_EPS) + ADAM_WD * w_ref[...])

    spec = pl.BlockSpec((tr, cols), lambda i: (i, 0))
    outs = pl.pallas_call(body, name=name, grid=(rows // tr,), in_specs=[spec] * 4, out_specs=[spec] * 3,
                          out_shape=[SDS((rows, cols), F32)] * 3, compiler_params=_cparams())(
        w.reshape(rows, cols), g.reshape(rows, cols), m.reshape(rows, cols), v.reshape(rows, cols))
    return tuple(t.reshape(shape) for t in outs)


def _exchange(name, x, scatter, after=None):
    blk = x.shape[1:] if scatter else x.shape
    extra = [] if after is None else [after]

    def body(x_ref, *rest):
        out_ref, send_sems, recv_sems, local_sem = rest[len(extra):]
        mx, my, mc = lax.axis_index("x"), lax.axis_index("y"), lax.axis_index("c")
        me = 4 * mx + 2 * my + mc
        own = pltpu.make_async_copy(x_ref.at[me] if scatter else x_ref, out_ref.at[me], local_sem)
        own.start()
        copies = []
        for kk in range(1, N_DEV):
            px = jnp.bitwise_xor(mx, (kk >> 2) & 1)
            py = jnp.bitwise_xor(my, (kk >> 1) & 1)
            pc = jnp.bitwise_xor(mc, kk & 1)
            peer = 4 * px + 2 * py + pc
            send = pltpu.make_async_remote_copy(
                src_ref=x_ref.at[peer] if scatter else x_ref, dst_ref=out_ref.at[me],
                send_sem=send_sems.at[kk - 1], recv_sem=recv_sems.at[kk - 1],
                device_id=(px, py, pc), device_id_type=MESH)
            send.start()
            arrival = pltpu.make_async_remote_copy(
                src_ref=x_ref.at[peer] if scatter else x_ref, dst_ref=out_ref.at[peer],
                send_sem=send_sems.at[kk - 1], recv_sem=recv_sems.at[kk - 1],
                device_id=(px, py, pc), device_id_type=MESH)
            copies.append((send, arrival))
        for send, arrival in copies:
            arrival.wait_recv()
            send.wait_send()
        own.wait()

    return pl.pallas_call(
        body, name=name, out_shape=SDS((N_DEV,) + tuple(blk), x.dtype),
        in_specs=[pl.BlockSpec(memory_space=pl.ANY)] * (1 + len(extra)), out_specs=pl.BlockSpec(memory_space=pl.ANY),
        scratch_shapes=[pltpu.SemaphoreType.DMA((N_DEV - 1,)), pltpu.SemaphoreType.DMA((N_DEV - 1,)),
                        pltpu.SemaphoreType.DMA],
    )(x, *extra)


def _rope_perm(pre, reps, post):
    half = QK_ROPE // 4
    width = reps * (pre + QK_ROPE) + post
    p = np.zeros((width, width), np.float32)
    for rep in range(reps):
        s0 = rep * (pre + QK_ROPE) + pre
        for base in (s0, s0 + 2 * half):
            for i in range(half):
                p[base + half + i, base + i] = -1.0
                p[base + i, base + half + i] = 1.0
    return p


def _rope_layout(pre, reps, post):
    width = reps * (pre + QK_ROPE) + post
    expand = np.zeros((QK_ROPE, width), np.float32)
    plain = np.ones((1, width), np.float32)
    for rep in range(reps):
        s0 = rep * (pre + QK_ROPE) + pre
        expand[np.arange(QK_ROPE), s0 + np.arange(QK_ROPE)] = 1.0
        plain[0, s0:s0 + QK_ROPE] = 0.0
    return jnp.asarray(expand, BF16), jnp.asarray(plain, F32), jnp.asarray(_rope_perm(pre, reps, post), BF16)


def _head_spread():
    spread = np.zeros((HEADS * QK_HEAD, HEADS * HEAD_PAD), np.float32)
    for hh in range(HEADS):
        spread[hh * QK_HEAD + np.arange(QK_HEAD), hh * HEAD_PAD + np.arange(QK_HEAD)] = 1.0
    return jnp.asarray(spread, BF16)


def _rope_factors(n_lat, t_rows):
    half = QK_ROPE // 4
    pos = jnp.arange(n_lat)
    freqs = jnp.power(ROPE_THETA, -jnp.arange(0, 2 * half, 2, dtype=F32) / (2 * half))
    ang_r = (pos // GRID_W).astype(F32)[:, None] * freqs
    ang_c = (pos % GRID_W).astype(F32)[:, None] * freqs
    ang = jnp.concatenate([ang_r, ang_r, ang_c, ang_c], axis=-1)
    rest = t_rows - n_lat
    return (jnp.concatenate([jnp.cos(ang), jnp.ones((rest, QK_ROPE), F32)], axis=0),
            jnp.concatenate([jnp.sin(ang), jnp.zeros((rest, QK_ROPE), F32)], axis=0))


def _ffn_half_fwd(tag, s, mg, k, feed, i, coef, n_lat):
    wg_t, wu_t = feed.weights(f"{tag}_up", [f"gate_t{i}", f"up_t{i}"], s)
    u, a, b, hid = _ffn_up(f"{tag}_up", s, mg, k, n_lat, wg_t, wu_t)
    (wd,) = feed.weights(f"{tag}_down", [f"down{i}"], hid)
    s_out, o = _mm_resid(f"{tag}_down", [(hid, wd)], s, mg, k, coef, n_lat)
    return s_out, (s, u, a, b, hid, o, wg_t, wu_t, wd)


def _ffn_half_bwd(tag, ds_out, saved, mg, k, feed, i, coef, n_lat, out_rows=None):
    s, u, a, b, hid, o, wg_t, wu_t, wd = saved
    do, da, db, dgate = _ffn_dact(f"{tag}_dact", ds_out, o, mg, k, coef, n_lat, wd, a, b)
    dwd = _mm(f"{tag}_dwd", [(hid, do)], "tn", BF16)
    dwg_t, dwu_t = _dw_pair(f"{tag}_dwgu", da, db, u)
    token = feed.grads(tag, {f"down{i}": dwd, f"gate_t{i}": dwg_t, f"up_t{i}": dwu_t})
    ds_in, (dshift, dscale, dgain) = _du_adaln(f"{tag}_du", [(da, wg_t), (db, wu_t)], s, ds_out, mg, k, n_lat,
                                               _after(token), out_rows)
    return ds_in, dict(shift=dshift, scale=dscale, gate=dgate, gain=dgain)


def _after(token):
    return jnp.zeros((1, D_MODEL), F32) + token


def _mod_grad(parts, n_groups):
    rows = []
    zero = jnp.zeros((n_groups, 1, D_MODEL), F32)
    for k in range(3):
        for nm in ("shift", "scale", "gate"):
            t = parts[k].get(nm, zero)
            if t.shape[0] < n_groups:
                t = jnp.concatenate([t, jnp.zeros((n_groups - t.shape[0], 1, D_MODEL), F32)], axis=0)
            rows.append(t)
    return jnp.concatenate(rows, axis=1).reshape(n_groups, N_MOD * D_MODEL)


def _local_step(x, ctx, target, mod_h, mod_g, norm_g, feed, pool_w, pool_scale, q_norm_g, kv_norm_g, conv_w,
                final_norm_g):
    n_lat, n_ctx = x.shape[0], ctx.shape[0]
    t_all = n_lat + n_ctx
    mg0 = jnp.stack([jnp.concatenate([mod_h[0], norm_g[0]], axis=0), jnp.concatenate([mod_g, norm_g[0]], axis=0)])
    mg1 = jnp.concatenate([mod_h[1], norm_g[1]], axis=0)[None]

    s0 = jnp.concatenate([x, ctx], axis=0)
    s1, sv_f00 = _ffn_half_fwd("l0f0", s0, mg0, 0, feed, 0, 0.5, n_lat)

    (w_in,) = feed.weights("l0m_in", ["in_t"], s1)
    kv_rows = KV_RANK + QK_ROPE
    w_in_t = jnp.concatenate([
        w_in[:POOL_DIM], jnp.zeros((PA_CQ - POOL_DIM, D_MODEL), BF16), w_in[POOL_DIM:POOL_DIM + Q_RANK],
        w_in[POOL_DIM + Q_RANK:], jnp.zeros((PA_KV_W - kv_rows, D_MODEL), BF16)], axis=0)
    ua, proj = _adaln_mm("l0m_proj", s1, mg0, 1, n_lat, w_in_t)
    w_uq, w_ukv_t, w_ab_out = feed.weights("l0m_rest", ["uq", "ukv_t", "ab_out"], proj)
    pool_y, pool_p = _pool_fwd("l0m_pool", proj, n_lat, pool_w.astype(BF16), pool_scale)
    nq = _rmsnorm_fwd("l0m_qnorm", proj, Q_RANK, PA_CQ // Q_RANK, q_norm_g, n_lat)
    q_lin = _mm("l0m_q", [(nq, w_uq)], "nn", F32, 512, 768)
    cos32, sin32 = _rope_factors(n_lat, t_all)
    lay_q, lay_k = _rope_layout(QK_NOPE, HEADS, 0), _rope_layout(KV_RANK, 1, PA_KV_W - kv_rows)
    spread = _head_spread()
    q_flat = _rope("l0m_qrope", q_lin, Q_RANK, 0, cos32[:n_lat], sin32[:n_lat], lay_q, False, BF16, spread)
    kvr = _rope("l0m_krope", proj, PA_KV_W, PA_KV // PA_KV_W, cos32, sin32, lay_k, False, F32)
    nkv = _rmsnorm_fwd("l0m_kvnorm", kvr, KV_RANK, 0, kv_norm_g, t_all)
    kv = _mm("l0m_kv", [(nkv, w_ukv_t)], "nt", BF16, 768, 512)
    k_rope = jnp.pad(kvr[:, KV_RANK:KV_RANK + QK_ROPE].astype(BF16), ((0, 0), (QK_NOPE, HEAD_PAD - QK_HEAD)))
    o_flat, lse = _attn_fwd("l0m_attn", q_flat, kv, k_rope, n_lat)
    w_o_pad = jnp.pad(w_ab_out[POOL_DIM:].reshape(HEADS, V_HEAD, D_MODEL),
                      ((0, 0), (HEAD_PAD - V_HEAD, 0), (0, 0))).reshape(HEADS * HEAD_PAD, D_MODEL)
    w_o_pool = w_ab_out[:POOL_DIM]
    h2, mix_o = _mm_resid("l0m_out", [(pool_y, w_o_pool), (o_flat, w_o_pad)], s1, mg0[:1], 1, 1.0, n_lat)

    h3, sv_f01 = _ffn_half_fwd("l0f1", h2, mg0[:1], 2, feed, 1, 0.5, n_lat)

    h4, sv_f10 = _ffn_half_fwd("l1f0", h3, mg1, 0, feed, 2, 0.5, n_lat)
    w_cin_t, w_c_out = feed.weights("l1m", ["cin_t", "c_out"], h4)
    uc, z3 = _adaln_mm("l1m_in", h4, mg1, 1, n_lat, w_cin_t)
    yc = _conv_fwd("l1m_conv", z3, conv_w)
    h5, conv_o = _mm_resid("l1m_out", [(yc, w_c_out)], h4, mg1, 1, 1.0, n_lat)
    h6, sv_f11 = _ffn_half_fwd("l1f1", h5, mg1, 2, feed, 3, 0.5, n_lat)

    dh6, sq_cols, d_final_g = _final_loss("loss_head", h6, target, final_norm_g)
    g = {}
    dh5, g["f11"] = _ffn_half_bwd("l1f1", dh6, sv_f11, mg1, 2, feed, 3, 0.5, n_lat)

    do_c, dyc, dgate_c = _gate_mm("l1m_dy", dh5, conv_o, mg1, 1, 1.0, n_lat, w_c_out)
    d_c_out = _mm("l1m_dwout", [(yc, do_c)], "tn", BF16)
    db_, dc_, dv_, d_conv_w = _conv_bwd("l1m_dconv", dyc, z3, conv_w)
    dz3 = jnp.concatenate([db_, dc_, dv_], axis=-1)
    d_cin_t = _mm("l1m_dwin", [(dz3, uc)], "tn", BF16)
    token = feed.grads("l1m", {"c_out": d_c_out, "cin_t": d_cin_t})
    dh4, (dsh_c, dsc_c, dgn_c) = _du_adaln("l1m_du", [(dz3, w_cin_t)], h4, dh5, mg1, 1, n_lat, _after(token))
    dh3, g["f10"] = _ffn_half_bwd("l1f0", dh4, sv_f10, mg1, 0, feed, 2, 0.5, n_lat)

    dh2, g["f01"] = _ffn_half_bwd("l0f1", dh3, sv_f01, mg0[:1], 2, feed, 1, 0.5, n_lat)

    w_back = jnp.concatenate([w_o_pool, w_o_pad], axis=0)
    do_a, dcat, dgate_a = _gate_mm("l0m_dcat", dh2, mix_o, mg0[:1], 1, 1.0, n_lat, w_back)
    d_o_pad = _mm("l0m_dwout_a", [(o_flat, do_a)], "tn", BF16)
    d_ab_out = jnp.concatenate([
        _mm("l0m_dwout_p", [(pool_y, do_a)], "tn", BF16),
        d_o_pad.reshape(HEADS, HEAD_PAD, D_MODEL)[:, HEAD_PAD - V_HEAD:].reshape(HEADS * V_HEAD, D_MODEL)], axis=0)
    dproj = jnp.zeros((t_all, PA_W), BF16)
    dproj, d_pool_w, d_pool_scale = _pool_bwd("l0m_dpool", dcat, n_lat, pool_p, pool_w.astype(BF16), pool_scale, dproj)
    dq_flat, dkv, dk_rope = _attn_bwd("l0m_dattn", q_flat, kv, k_rope, o_flat, lse, dcat, POOL_DIM // HEAD_PAD, n_lat)
    dq_lin = _rope("l0m_dqrope", dq_flat, Q_RANK, 0, cos32[:n_lat], sin32[:n_lat], lay_q, True, BF16, spread)
    d_uq = _mm("l0m_dwuq", [(nq, dq_lin)], "tn", BF16, 768, 768)
    dnq = _mm("l0m_dnq", [(dq_lin, w_uq)], "nt", F32, 512, 768)
    dproj, d_q_norm_g = _rmsnorm_bwd("l0m_dqnorm", proj, Q_RANK, PA_CQ // Q_RANK, dnq, q_norm_g, n_lat, BF16,
                                     (dproj, PA_CQ // Q_RANK))
    dnkv = _mm("l0m_dnkv", [(dkv, w_ukv_t)], "nn", F32, 768, 256)
    d_ukv_t = _mm("l0m_dwukv", [(dkv, nkv)], "tn", BF16, 512, 256)
    dckv, d_kv_norm_g = _rmsnorm_bwd("l0m_dkvnorm", kvr, KV_RANK, 0, dnkv, kv_norm_g, t_all)
    dkvr = jnp.concatenate([dckv, dk_rope[:, QK_NOPE:QK_HEAD],
                            jnp.zeros((t_all, PA_KV_W - KV_RANK - QK_ROPE), F32)], axis=-1)
    dproj = _rope("l0m_dkrope", dkvr, PA_KV_W, 0, cos32, sin32, lay_k, True, BF16, None, (dproj, PA_KV // PA_KV_W))
    d_in_pad = _mm("l0m_dwin", [(dproj, ua)], "tn", BF16, 640, 512)
    d_in_t = jnp.concatenate([d_in_pad[:POOL_DIM], d_in_pad[PA_CQ:PA_CQ + Q_RANK],
                              d_in_pad[PA_KV:PA_KV + kv_rows]], axis=0)
    token = feed.grads("l0m", {"ab_out": d_ab_out, "uq": d_uq, "ukv_t": d_ukv_t, "in_t": d_in_t})
    ds1, (dsh_a, dsc_a, dgn_a) = _du_adaln("l0m_du", [(dproj, w_in_t)], s1, dh2, mg0, 1, n_lat, _after(token))
    grad_x, g["f00"] = _ffn_half_bwd("l0f0", ds1, sv_f00, mg0, 0, feed, 0, 0.5, n_lat, out_rows=n_lat)

    dmod0 = _mod_grad([g["f00"], dict(shift=dsh_a, scale=dsc_a, gate=dgate_a), g["f01"]], 2)
    dmod1 = _mod_grad([g["f10"], dict(shift=dsh_c, scale=dsc_c, gate=dgate_c), g["f11"]], 1)
    d_norm_g = jnp.stack([
        jnp.concatenate([jnp.sum(g["f00"]["gain"], axis=0), jnp.sum(dgn_a, axis=0), g["f01"]["gain"][0]], axis=0),
        jnp.concatenate([g["f10"]["gain"][0], dgn_c[0], g["f11"]["gain"][0]], axis=0)])
    grads = dict(
        pool_w=d_pool_w, pool_scale=d_pool_scale, q_norm_g=d_q_norm_g[0], kv_norm_g=d_kv_norm_g[0],
        conv_w=d_conv_w, final_norm_g=d_final_g[0], norm_g=d_norm_g,
        mod_h=jnp.stack([dmod0[0], dmod1[0]]), mod_g=dmod0[1])
    return sq_cols, grad_x, grads


HBM_SPEC = pl.BlockSpec(memory_space=pltpu.HBM)
SEM_SPEC = pl.BlockSpec(memory_space=pltpu.SEMAPHORE)
ANY_SPEC = pl.BlockSpec(memory_space=pl.ANY)
SIDE_EFFECT = pltpu.SideEffectType.DATAFLOW_SIDE_EFFECTING
N_PEERS = N_DEV - 1


def _mesh_place():
    mx, my, mc = lax.axis_index("x"), lax.axis_index("y"), lax.axis_index("c")
    return mx, my, mc, 4 * mx + 2 * my + mc


def _peer(place, kk):
    mx, my, mc, _ = place
    px = jnp.bitwise_xor(mx, (kk >> 2) & 1)
    py = jnp.bitwise_xor(my, (kk >> 1) & 1)
    pc = jnp.bitwise_xor(mc, kk & 1)
    return (px, py, pc), 4 * px + 2 * py + pc


def _hbm(a):
    return pltpu.with_memory_space_constraint(a, pltpu.HBM)


def _landing(block, me):
    zone = lax.empty((N_DEV,) + block.shape, block.dtype)
    return lax.dynamic_update_slice(zone, block[None], (me,) + (0,) * block.ndim)


ALL_PEERS = tuple(range(1, N_DEV))
SIBLING = 1
CHIP_PEERS = (2, 4, 6)
RELAYED = (3, 5, 7)


def _exchange_start(name, srcs, lands, scatter, after, peers=ALL_PEERS):
    n = len(srcs)
    extra = [] if after is None else [after]

    def body(*refs):
        src, land = refs[:n], refs[n:2 * n]
        send_sems, recv_sems, token = refs[2 * n + len(extra)], refs[2 * n + len(extra) + 1], refs[-1]
        place = _mesh_place()
        for a in range(n):
            for kk in peers:
                dev, peer = _peer(place, kk)
                pltpu.make_async_remote_copy(
                    src_ref=src[a].at[peer] if scatter else src[a],
                    dst_ref=land[a].at[kk - 1] if scatter else land[a].at[place[3]],
                    send_sem=send_sems.at[a * N_PEERS + kk - 1], recv_sem=recv_sems.at[a * N_PEERS + kk - 1],
                    device_id=dev, device_id_type=MESH).start()
        token[...] = jnp.zeros_like(token)

    thru = [pltpu.HBM(t.shape, t.dtype) for t in (*srcs, *lands)]
    res = pl.pallas_call(
        body, name=name,
        out_shape=(pltpu.SemaphoreType.DMA((n * N_PEERS,)), pltpu.SemaphoreType.DMA((n * N_PEERS,)), *thru,
                   SDS((8, 128), F32)),
        in_specs=[HBM_SPEC] * (2 * n) + [ANY_SPEC] * len(extra),
        out_specs=(SEM_SPEC, SEM_SPEC, *([HBM_SPEC] * (2 * n)), pl.BlockSpec(memory_space=pltpu.VMEM)),
        input_output_aliases={i: 2 + i for i in range(2 * n)},
        compiler_params=pltpu.CompilerParams(has_side_effects=SIDE_EFFECT),
    )(*[_hbm(s) for s in srcs], *[_hbm(t) for t in lands], *extra)
    return res[0], res[1], list(res[2:2 + n]), list(res[2 + n:2 + 2 * n]), res[-1]


def _exchange_wait(name, send_sems, recv_sems, srcs, lands, places, scatter, after):
    n = len(srcs)

    def body(*refs):
        src, land = refs[:n], refs[n:2 * n]
        send, recv = refs[2 * n], refs[2 * n + 1]
        place = _mesh_place()
        for a in range(n):
            for kk in range(1, N_DEV):
                dev, peer = _peer(place, kk)
                cp = pltpu.make_async_remote_copy(
                    src_ref=src[a].at[peer] if scatter else src[a],
                    dst_ref=land[a].at[kk - 1] if scatter else land[a].at[peer],
                    send_sem=send.at[places[a] * N_PEERS + kk - 1], recv_sem=recv.at[places[a] * N_PEERS + kk - 1],
                    device_id=dev, device_id_type=MESH)
                cp.wait_send()
                cp.wait_recv()

    thru = [pltpu.HBM(t.shape, t.dtype) for t in (*srcs, *lands)]
    res = pl.pallas_call(
        body, name=name, out_shape=tuple(thru),
        in_specs=[HBM_SPEC] * (2 * n) + [SEM_SPEC, SEM_SPEC] + [ANY_SPEC] * len(after),
        out_specs=tuple([HBM_SPEC] * (2 * n)), input_output_aliases={i: i for i in range(2 * n)},
        compiler_params=pltpu.CompilerParams(has_side_effects=SIDE_EFFECT),
    )(*srcs, *lands, send_sems, recv_sems, *after)
    return list(res[:n]), list(res[n:])


def _gather_relay(name, send1, recv1, lands, places, after):
    n = len(lands)

    def body(*refs):
        land, s1, r1 = refs[:n], refs[n], refs[n + 1]
        s2, r2 = refs[n + 3], refs[n + 4]
        place = _mesh_place()
        sibling = _peer(place, SIBLING)[0]
        for a in range(n):
            for j, kk in enumerate(CHIP_PEERS):
                dev, origin = _peer(place, kk)
                block = land[a].at[origin]
                pltpu.make_async_remote_copy(
                    src_ref=block, dst_ref=block, send_sem=s1.at[places[a] * N_PEERS + kk - 1],
                    recv_sem=r1.at[places[a] * N_PEERS + kk - 1], device_id=dev, device_id_type=MESH).wait_recv()
                pltpu.make_async_remote_copy(
                    src_ref=block, dst_ref=block, send_sem=s2.at[a * 3 + j], recv_sem=r2.at[a * 3 + j],
                    device_id=sibling, device_id_type=MESH).start()

    res = pl.pallas_call(
        body, name=name,
        out_shape=(pltpu.SemaphoreType.DMA((3 * n,)), pltpu.SemaphoreType.DMA((3 * n,)),
                   *[pltpu.HBM(t.shape, t.dtype) for t in lands]),
        in_specs=[HBM_SPEC] * n + [SEM_SPEC, SEM_SPEC, ANY_SPEC],
        out_specs=(SEM_SPEC, SEM_SPEC, *([HBM_SPEC] * n)),
        input_output_aliases={i: 2 + i for i in range(n)},
        compiler_params=pltpu.CompilerParams(has_side_effects=SIDE_EFFECT),
    )(*lands, send1, recv1, after)
    return res[0], res[1], list(res[2:])


def _gather_wait(name, send1, recv1, send2, recv2, srcs, lands, places, after):
    n = len(lands)

    def body(*refs):
        src, land = refs[:n], refs[n:2 * n]
        s1, r1, s2, r2 = refs[2 * n:2 * n + 4]
        place = _mesh_place()
        for a in range(n):
            for kk in (SIBLING,) + CHIP_PEERS:
                dev, origin = _peer(place, kk)
                first = pltpu.make_async_remote_copy(
                    src_ref=src[a], dst_ref=land[a].at[origin], send_sem=s1.at[places[a] * N_PEERS + kk - 1],
                    recv_sem=r1.at[places[a] * N_PEERS + kk - 1], device_id=dev, device_id_type=MESH)
                first.wait_send()
                if kk == SIBLING:
                    first.wait_recv()
            for j, kk in enumerate(CHIP_PEERS):
                dev, origin = _peer(place, kk + 1)
                relay = pltpu.make_async_remote_copy(
                    src_ref=src[a], dst_ref=land[a].at[origin], send_sem=s2.at[a * 3 + j], recv_sem=r2.at[a * 3 + j],
                    device_id=dev, device_id_type=MESH)
                relay.wait_send()
                relay.wait_recv()

    arrays = (*srcs, *lands)
    res = pl.pallas_call(
        body, name=name, out_shape=tuple(pltpu.HBM(t.shape, t.dtype) for t in arrays),
        in_specs=[HBM_SPEC] * (2 * n) + [SEM_SPEC] * 4 + [ANY_SPEC], out_specs=tuple([HBM_SPEC] * (2 * n)),
        input_output_aliases={i: i for i in range(2 * n)},
        compiler_params=pltpu.CompilerParams(has_side_effects=SIDE_EFFECT),
    )(*arrays, send1, recv1, send2, recv2, after)
    return list(res[n:])


class _Feed:
    def __init__(self, shards, groups, me):
        self.shards, self.groups, self.me, self.pos = shards, groups, me, 0
        self.sems, self.srcs, self.lands = {}, {}, {}
        self.relays = {}
        self.pending = []

    def start(self, tag, names, after):
        srcs = [self.shards[nm] for nm in names]
        lands = [_landing(s, self.me) for s in srcs]
        send, recv, srcs, lands, self.token = _exchange_start(
            f"gather_start_{tag}", srcs, lands, False, after, (SIBLING,) + CHIP_PEERS)
        for i, nm in enumerate(names):
            self.sems[nm], self.srcs[nm], self.lands[nm] = (send, recv, i), srcs[i], lands[i]
        return self.token

    def _relay(self, gi, after):
        names = self.groups[gi]
        if gi not in self.relays:
            send, recv, _ = self.sems[names[0]]
            places = [self.sems[nm][2] for nm in names]
            send2, recv2, lands = _gather_relay(f"gather_relay_{gi}", send, recv, [self.lands[nm] for nm in names],
                                                places, after)
            for nm, t in zip(names, lands):
                self.lands[nm] = t
            self.relays[gi] = (send2, recv2)
            after = lands[0]
        return after

    def weights(self, tag, names, after):
        gi = self.pos
        assert names == self.groups[gi], (names, self.groups[gi])
        if gi == 0:
            after = self.token
        self._relay(gi, after)
        if 1 <= gi < len(self.groups) - 1:
            after = self._relay(gi + 1, after)
        send2, recv2 = self.relays[gi]
        send, recv, _ = self.sems[names[0]]
        got = _gather_wait(f"gather_wait_{tag}", send, recv, send2, recv2, [self.srcs[nm] for nm in names],
                           [self.lands[nm] for nm in names], [self.sems[nm][2] for nm in names], after)
        self.pos += 1
        return [t.reshape((N_DEV * t.shape[1],) + t.shape[2:]) for t in got]

    def grads(self, tag, full):
        names = list(full)
        srcs = [full[nm].reshape((N_DEV, full[nm].shape[0] // N_DEV) + full[nm].shape[1:]) for nm in names]
        lands = [lax.empty((N_PEERS,) + s.shape[1:], s.dtype) for s in srcs]
        send, recv, srcs, lands, token = _exchange_start(f"scatter_start_{tag}", srcs, lands, True, None)
        self.pending.append((tag, names, send, recv, srcs, lands))
        return token[0, 0]

    def collect(self, tags, after, keep_slots=()):
        out = {}
        for tag, names, send, recv, srcs, lands in self.pending:
            if tag not in tags:
                continue
            srcs, got = _exchange_wait(f"scatter_wait_{tag}", send, recv, srcs, lands, list(range(len(names))), True,
                                       after)
            for nm, slots, src in zip(names, got, srcs):
                out[nm] = ((slots, src) if nm.startswith(tuple(keep_slots))
                           else _sum_slots(f"reduce_{nm}", slots, src, self.me))
        return out


def _adamw_math(w, gg, m, v):
    nm = ADAM_B1 * m + (1.0 - ADAM_B1) * gg
    nv = ADAM_B2 * v + (1.0 - ADAM_B2) * (gg * gg)
    bc1 = 1.0 - ADAM_B1 ** ADAM_STEP
    bc2 = 1.0 - ADAM_B2 ** ADAM_STEP
    return -ADAM_LR * ((nm / bc1) / (jnp.sqrt(nv / bc2) + ADAM_EPS) + ADAM_WD * w), nm, nv


def _adamw_part(name, i, w, scattered, me, m, v, prev):
    n_parts, rows, cols = w.shape
    tr = _tile(rows, 256, 16)
    if prev is None:
        prev = tuple(lax.empty(w.shape, F32) for _ in range(4))

    slots, src = scattered

    def body(me_ref, w_ref, g_ref, own_ref, m_ref, v_ref, *rest):
        go_ref, d_ref, nm_ref, nv_ref = rest[4:]
        gg = own_ref[...].astype(F32)
        for sl in range(N_PEERS):
            gg = gg + g_ref[sl].astype(F32)
        d, nm, nv = _adamw_math(w_ref[...], gg, m_ref[...], v_ref[...])
        go_ref[...] = gg
        d_ref[...] = d
        nm_ref[...] = nm
        nv_ref[...] = nv

    part = pl.BlockSpec((None, tr, cols), lambda r, me_ref: (i, r, 0))
    grid_spec = pltpu.PrefetchScalarGridSpec(
        num_scalar_prefetch=1, grid=(rows // tr,),
        in_specs=[part, pl.BlockSpec((N_PEERS, tr, cols), lambda r, me_ref: (0, r, 0)),
                  pl.BlockSpec((None, tr, cols), lambda r, me_ref: (me_ref[0], r, 0)), part, part] + [ANY_SPEC] * 4,
        out_specs=[part] * 4)
    return pl.pallas_call(
        body, name=name, grid_spec=grid_spec, out_shape=[SDS(w.shape, F32)] * 4,
        input_output_aliases={6 + k: k for k in range(4)}, compiler_params=_cparams(),
    )(_me_operand(me), w, slots, src, m, v, *prev)


WEIGHT_NAMES = ("c_ctx", "norm_g", "w_mod", "b_mod", "ffn_w_gate", "ffn_w_up", "ffn_w_down", "ab_w_in", "pool_w",
                "pool_scale", "q_norm_g", "w_uq", "kv_norm_g", "w_ukv", "ab_w_out", "conv_w_in", "conv_w",
                "conv_w_out", "final_norm_g")


def kernel(x, c, ctx, c_ctx, norm_g, w_mod, b_mod, ffn_w_gate, ffn_w_up, ffn_w_down, ab_w_in, pool_w, pool_scale, q_norm_g, w_uq, kv_norm_g, w_ukv, ab_w_out, conv_w_in, conv_w, conv_w_out, final_norm_g, loss_target, m_c_ctx, m_norm_g, m_w_mod, m_b_mod, m_ffn_w_gate, m_ffn_w_up, m_ffn_w_down, m_ab_w_in, m_pool_w, m_pool_scale, m_q_norm_g, m_w_uq, m_kv_norm_g, m_w_ukv, m_ab_w_out, m_conv_w_in, m_conv_w, m_conv_w_out, m_final_norm_g, v_c_ctx, v_norm_g, v_w_mod, v_b_mod, v_ffn_w_gate, v_ffn_w_up, v_ffn_w_down, v_ab_w_in, v_pool_w, v_pool_scale, v_q_norm_g, v_w_uq, v_kv_norm_g, v_w_ukv, v_ab_w_out, v_conv_w_in, v_conv_w, v_conv_w_out, v_final_norm_g):
    weights = (c_ctx, norm_g, w_mod, b_mod, ffn_w_gate, ffn_w_up, ffn_w_down, ab_w_in, pool_w, pool_scale, q_norm_g,
               w_uq, kv_norm_g, w_ukv, ab_w_out, conv_w_in, conv_w, conv_w_out, final_norm_g)
    moms = (m_c_ctx, m_norm_g, m_w_mod, m_b_mod, m_ffn_w_gate, m_ffn_w_up, m_ffn_w_down, m_ab_w_in, m_pool_w,
            m_pool_scale, m_q_norm_g, m_w_uq, m_kv_norm_g, m_w_ukv, m_ab_w_out, m_conv_w_in, m_conv_w, m_conv_w_out,
            m_final_norm_g)
    vels = (v_c_ctx, v_norm_g, v_w_mod, v_b_mod, v_ffn_w_gate, v_ffn_w_up, v_ffn_w_down, v_ab_w_in, v_pool_w,
            v_pool_scale, v_q_norm_g, v_w_uq, v_kv_norm_g, v_w_ukv, v_ab_w_out, v_conv_w_in, v_conv_w, v_conv_w_out,
            v_final_norm_g)
    me = 4 * lax.axis_index("x") + 2 * lax.axis_index("y") + lax.axis_index("c")
    n_lat, n_ctx = x.shape[1], ctx.shape[1]
    d = D_MODEL
    mod_cols = w_mod.shape[-1]
    ng_sh, cw_sh = norm_g.shape[-1], conv_w.shape[-1]

    def ffn_shards(i):
        return {f"gate_t{i}": ffn_w_gate[i // 2, i % 2].T, f"up_t{i}": ffn_w_up[i // 2, i % 2].T,
                f"down{i}": ffn_w_down[i // 2, i % 2]}

    local = {**ffn_shards(0), "in_t": ab_w_in[0].T, "uq": w_uq[0], "ukv_t": w_ukv[0].T, "ab_out": ab_w_out[0],
             **ffn_shards(1), **ffn_shards(2), "cin_t": conv_w_in[0].T, "c_out": conv_w_out[0], **ffn_shards(3)}
    ffn_groups = [[[f"gate_t{i}", f"up_t{i}"], [f"down{i}"]] for i in range(4)]
    groups = [*ffn_groups[0], ["in_t"], ["uq", "ukv_t", "ab_out"], *ffn_groups[1], *ffn_groups[2],
              ["cin_t", "c_out"], *ffn_groups[3]]
    feed = _Feed({nm: a.astype(BF16) for nm, a in local.items()}, groups, me)

    small = jnp.concatenate([c.reshape(-1), norm_g.reshape(-1), conv_w.reshape(-1)])
    small_n = -(-small.shape[0] // 1024) * 1024
    small = jnp.pad(small, (0, small_n - small.shape[0])).reshape(small_n // 128, 128)
    small_all = _exchange("gather_small", small, False).reshape(N_DEV, small_n)
    c_all = small_all[:, :d]
    o1 = d + 6 * ng_sh
    norm_g_full = small_all[:, d:o1].reshape(N_DEV, 2, 3, ng_sh).transpose(1, 2, 0, 3).reshape(2, 3, d)
    conv_w_full = small_all[:, o1:o1 + 3 * cw_sh].reshape(N_DEV, 3, cw_sh).transpose(1, 0, 2).reshape(3, d)

    cond = jnp.concatenate([c_all, jnp.broadcast_to(c_ctx[None, :], (N_DEV, d))], axis=0)
    sil, dsil = _silu_rows("mod_silu", cond)
    w_mod_b = w_mod.astype(BF16)
    b_sh = lax.dynamic_slice(b_mod, (0, me * mod_cols), (2, mod_cols))
    m_part = jnp.stack([_mm(f"mod_fwd{l}", [(sil, w_mod_b[l])], "nn", F32, 16, 384, bias=b_sh[l:l + 1])
                        for l in range(2)], axis=1)
    m_all = _exchange("gather_mod", m_part.reshape(-1, 128), False).reshape(N_DEV, 2 * N_DEV, 2, mod_cols)
    m_mine = lax.dynamic_index_in_dim(m_all, me, axis=1, keepdims=False)
    mod_h = m_mine.transpose(1, 0, 2).reshape(2, N_MOD, d)
    mod_g = m_all[:, N_DEV, 0, :].reshape(N_MOD, d)

    first = feed.start("first", [nm for grp in groups[:3] for nm in grp], m_all)
    feed.start("rest", [nm for grp in groups[3:] for nm in grp], first)

    sq_cols, grad_x, g = _local_step(x[0], ctx[0], loss_target[0], mod_h, mod_g, norm_g_full, feed, pool_w[0],
                                  pool_scale, q_norm_g, kv_norm_g, conv_w_full, final_norm_g)
    w_of, m_of, v_of = (dict(zip(WEIGHT_NAMES, t)) for t in (weights, moms, vels))
    results = {}

    def update(nm, grad, view=lambda t: t):
        outs = _adamw(f"adamw_{nm}", view(w_of[nm]), grad.reshape(view(w_of[nm]).shape), view(m_of[nm]), view(v_of[nm]))
        results[nm] = tuple(view(t) for t in (grad.reshape(view(w_of[nm]).shape), *outs))

    def swap(t):
        return jnp.swapaxes(t, -1, -2)

    stacked = ("gate_t", "up_t", "down")
    early = feed.collect(["l1f1", "l1m", "l1f0", "l0f1", "l0m"], [grad_x], stacked)
    update("ab_w_in", early["in_t"], swap)
    update("w_uq", early["uq"])
    update("w_ukv", early["ukv_t"].T)
    update("ab_w_out", early["ab_out"])
    update("conv_w_in", early["cin_t"].T)
    update("conv_w_out", early["c_out"])
    ffn = {}
    for nm, prefix, view in (("ffn_w_gate", "gate_t", swap), ("ffn_w_up", "up_t", swap),
                             ("ffn_w_down", "down", lambda t: t)):
        w4, m4, v4 = (view(t).reshape((4,) + view(t).shape[-2:]) for t in (w_of[nm], m_of[nm], v_of[nm]))
        prev = None
        for i in (3, 2, 1):
            prev = _adamw_part(f"adamw_{nm}{i}", i, w4, early[f"{prefix}{i}"], me, m4, v4, prev)
        ffn[nm] = (prefix, view, w4, m4, v4, prev)
    done_early = [results[nm][1] for nm in results] + [state[5][1] for state in ffn.values()]
    late = feed.collect(["l0f0"], done_early, stacked)
    for nm, (prefix, view, w4, m4, v4, prev) in ffn.items():
        outs = _adamw_part(f"adamw_{nm}0", 0, w4, late[f"{prefix}0"], me, m4, v4, prev)
        results[nm] = tuple(view(t.reshape(view(w_of[nm]).shape)) for t in outs)

    dm = jnp.stack([g["mod_h"], jnp.stack([g["mod_g"], jnp.zeros_like(g["mod_g"])])])
    dm_all = _exchange("gather_dmod", dm.reshape(-1, 128), False, results["ffn_w_down"][1]).reshape(N_DEV, 2, 2, N_MOD * d)
    grad_b_mod = _sum_rows("dmod_bias", dm_all.reshape(2 * N_DEV, 2 * N_MOD * d)).reshape(2, N_MOD * d)
    dm_sh = lax.dynamic_slice(dm_all, (0, 0, 0, me * mod_cols), (N_DEV, 2, 2, mod_cols))
    gw_mod, cctx_parts = [], []
    for l in range(2):
        dm_l = dm_sh[:, :, l, :].transpose(1, 0, 2).reshape(2 * N_DEV, mod_cols).astype(BF16)
        gw_mod.append(_mm(f"mod_dw{l}", [(sil, dm_l)], "tn", F32, 512, 384))
        dm_ctx = jnp.concatenate([dm_l[N_DEV:], jnp.zeros((N_DEV, mod_cols), BF16)], axis=0)
        cctx_parts.append(_mm(f"mod_dcond{l}", [(dm_ctx, w_mod_b[l])], "nt", F32, 16, 512))
    cctx_part = _sum_rows("mod_dcond_sum", jnp.concatenate(cctx_parts, axis=0))
    update("w_mod", jnp.stack(gw_mod))
    update("b_mod", grad_b_mod)

    small_g = jnp.concatenate([g["pool_w"].reshape(-1), g["pool_scale"].reshape(-1), g["q_norm_g"].reshape(-1),
                               g["kv_norm_g"].reshape(-1), g["final_norm_g"].reshape(-1), g["norm_g"].reshape(-1),
                               g["conv_w"].reshape(-1), sq_cols.reshape(-1), cctx_part.reshape(-1)])
    sizes = [pool_w.size, pool_scale.size, q_norm_g.size, kv_norm_g.size, d, 6 * d, 3 * d, d, d]
    sg_n = -(-small_g.shape[0] // 1024) * 1024
    small_g = jnp.pad(small_g, (0, sg_n - small_g.shape[0]))
    sg_all = _exchange("gather_small_grads", small_g.reshape(-1, 128), False).reshape(N_DEV, sg_n)
    scale_vec = jnp.concatenate([jnp.ones((1, sum(sizes[:-1])), F32), dsil[N_DEV:N_DEV + 1],
                                 jnp.ones((1, sg_n - sum(sizes)), F32)], axis=1)
    sg = _sum_rows("small_grads_sum", sg_all, scale_vec)[0]
    cuts, pos = [], 0
    for sz in sizes:
        cuts.append(sg[pos:pos + sz])
        pos += sz
    g_pool_w, g_pool_scale, g_q_norm, g_kv_norm, g_final, g_norm_full, g_conv_full, sq_all, g_c_ctx = cuts
    loss = 0.5 * jnp.sum(sq_all) / d
    update("c_ctx", g_c_ctx)
    update("norm_g", lax.dynamic_slice(g_norm_full.reshape(2, 3, d), (0, 0, me * ng_sh), (2, 3, ng_sh)))
    update("conv_w", lax.dynamic_slice(g_conv_full.reshape(3, d), (0, me * cw_sh), (3, cw_sh)))
    update("pool_w", g_pool_w)
    update("pool_scale", g_pool_scale)
    update("q_norm_g", g_q_norm)
    update("kv_norm_g", g_kv_norm)
    update("final_norm_g", g_final)
    outs = [results[nm] for nm in WEIGHT_NAMES]
    return (loss, grad_x[None], *[o[0] for o in outs], *[o[1] for o in outs], *[o[2] for o in outs],
            *[o[3] for o in outs])
```

```python
import functools
import math

import jax
import jax.numpy as jnp
import numpy as np
from jax import lax
from jax.experimental import pallas as pl
from jax.experimental.pallas import tpu as pltpu

F32 = jnp.float32
BF16 = jnp.bfloat16
MESH = pl.DeviceIdType.MESH
SDS = jax.ShapeDtypeStruct

N_DEV = 8
D_MODEL = 1024
N_MOD = 9
D_FF = 2816
POOL_WINDOWS = (2, 4, 8, 16)
POOL_DIM = 512
POOL_GROUP_DIM = 128
HEADS = 8
QK_NOPE = 64
QK_ROPE = 32
QK_HEAD = QK_NOPE + QK_ROPE
V_HEAD = 64
Q_RANK = 768
KV_RANK = 256
GRID_W = 64
ROPE_THETA = 10000.0
RMS_EPS = 1e-6
ATTN_SCALE = 1.0 / math.sqrt(QK_HEAD)
HEAD_PAD = 128
POOL_PAD = 16
ATTN_Q_ROWS_FWD = 256
ATTN_Q_ROWS_BWD = 1024
PA_POOL, PA_CQ, PA_KV = 0, 768, 1536
PA_KV_W = 384
PA_W = PA_KV + PA_KV_W

ADAM_LR, ADAM_B1, ADAM_B2, ADAM_EPS, ADAM_WD, ADAM_STEP = 0.001, 0.9, 0.999, 1e-08, 0.01, 10

VMEM_LIMIT_BYTES = 56 * 1024 * 1024

NN = ((1,), (0,))
NT = ((1,), (1,))
TN = ((0,), (0,))


def _cparams():
    return pltpu.CompilerParams(vmem_limit_bytes=VMEM_LIMIT_BYTES)


def _dot(a, b, dims):
    return lax.dot_general(a, b, (dims, ((), ())), preferred_element_type=F32)


def _tile(n, cap, mult=8):
    t = (min(cap, n) // mult) * mult
    while t >= mult:
        if n % t == 0:
            return t
        t -= mult
    return n


def _colsum(x):
    return jnp.sum(x, axis=0, keepdims=True)


def _rms(x):
    r = lax.rsqrt(jnp.mean(x * x, axis=-1, keepdims=True) + RMS_EPS)
    return x * r, r


def _rms_bwd(n, r, dn):
    return r * (dn - n * jnp.mean(dn * n, axis=-1, keepdims=True))


def _rowwise(name, fn, t_rows, tm, n_lat, rows, vecs, outs, accs, into=None):
    nt = t_rows // tm
    nlt = n_lat // tm
    n_groups = 2 if nlt < nt else 1

    def grp(i):
        return jnp.where(i >= nlt, 1, 0) if n_groups == 2 else 0

    in_specs = [pl.BlockSpec((tm, w), functools.partial(lambda i, cb: (i, cb), cb=cb)) for (_, w, cb) in rows]
    in_specs += [pl.BlockSpec((1,) + v.shape[1:], lambda i: (grp(i), 0, 0)) for v in vecs]
    out_specs = [pl.BlockSpec((tm, w), lambda i: (i, 0)) for (w, _) in outs]
    out_specs += [pl.BlockSpec((1, 1, w), lambda i: (grp(i), 0, 0)) for w in accs]
    out_shape = [SDS((t_rows, w), dt) for (w, dt) in outs] + [SDS((n_groups, 1, w), F32) for w in accs]
    n_r, n_v, n_o = len(rows), len(vecs), len(outs)
    extra, aliases = [], {}
    if into is not None:
        extra, aliases = [into[0]], {n_r + n_v: 0}
        in_specs.append(pl.BlockSpec(memory_space=pl.ANY))
        out_specs[0] = pl.BlockSpec((tm, outs[0][0]), lambda i: (i, into[1]))
        out_shape[0] = SDS(into[0].shape, into[0].dtype)
    n_in = n_r + n_v + len(extra)

    def body(*refs):
        row_vals = [r[...] for r in refs[:n_r]]
        vec_vals = [v[0] for v in refs[n_r:n_r + n_v]]
        out_refs = refs[n_in:n_in + n_o]
        acc_refs = refs[n_in + n_o:]
        out_vals, acc_vals = fn(row_vals, vec_vals)
        for o_ref, o in zip(out_refs, out_vals):
            o_ref[...] = o.astype(o_ref.dtype)
        if acc_refs:
            i = pl.program_id(0)
            first = (i == 0) | (i == nlt) if n_groups == 2 else i == 0

            @pl.when(first)
            def _():
                for a_ref, a in zip(acc_refs, acc_vals):
                    a_ref[0] = a

            @pl.when(jnp.logical_not(first))
            def _():
                for a_ref, a in zip(acc_refs, acc_vals):
                    a_ref[0] += a

    res = pl.pallas_call(
        body, name=name, grid=(nt,), in_specs=in_specs, out_specs=out_specs, out_shape=out_shape,
        input_output_aliases=aliases, compiler_params=_cparams(),
    )(*[r[0] for r in rows], *vecs, *extra)
    return res[:n_o], res[n_o:]


RESIDENT_BYTES = 12 * 1024 * 1024


def _mm(name, pairs, mode, out_dtype, tm_cap=256, tn_cap=512, bias=None):
    a0, b0 = pairs[0]
    if mode == "nn":
        m, n, dims = a0.shape[0], b0.shape[1], NN
    elif mode == "nt":
        m, n, dims = a0.shape[0], b0.shape[0], NT
    else:
        m, n, dims = a0.shape[1], b0.shape[1], TN
    b_bytes = sum(b.size * b.dtype.itemsize for _, b in pairs)
    tn = n if b_bytes <= RESIDENT_BYTES else _tile(n, tn_cap, 128)
    tm = _tile(m, tm_cap, 128 if mode == "tn" else 16)

    def a_spec(a):
        if mode == "tn":
            return pl.BlockSpec((a.shape[0], tm), lambda i, j: (0, i))
        return pl.BlockSpec((tm, a.shape[1]), lambda i, j: (i, 0))

    def b_spec(b):
        if mode == "nt":
            return pl.BlockSpec((tn, b.shape[1]), lambda i, j: (j, 0))
        return pl.BlockSpec((b.shape[0], tn), lambda i, j: (0, j))

    in_specs, flat = [], []
    for a, b in pairs:
        in_specs += [a_spec(a), b_spec(b)]
        flat += [a, b]
    if bias is not None:
        in_specs.append(pl.BlockSpec((1, tn), lambda i, j: (0, j)))
        flat.append(bias)
    n_pairs = len(pairs)

    def body(*refs):
        acc = None
        for p in range(n_pairs):
            t = _dot(refs[2 * p][...], refs[2 * p + 1][...], dims)
            acc = t if acc is None else acc + t
        if bias is not None:
            acc = acc + refs[2 * n_pairs][...]
        refs[-1][...] = acc.astype(refs[-1].dtype)

    return pl.pallas_call(
        body, name=name, grid=(m // tm, n // tn), in_specs=in_specs,
        out_specs=pl.BlockSpec((tm, tn), lambda i, j: (i, j)),
        out_shape=SDS((m, n), out_dtype), compiler_params=_cparams(),
    )(*flat)


def _mm_resid(name, pairs, s, mg, k, coef, n_lat):
    t_rows, n = pairs[0][0].shape[0], s.shape[1]
    n_pairs = len(pairs)
    tm = _tile(math.gcd(n_lat, t_rows), 256, 16)
    nlt = n_lat // tm
    n_groups = 2 if nlt < t_rows // tm else 1

    def grp(i):
        return jnp.where(i >= nlt, 1, 0) if n_groups == 2 else 0

    def body(*refs):
        s_ref, mg_ref, so_ref, o_ref = refs[2 * n_pairs:]
        o = _dot(refs[0][...], refs[n_pairs][...], NN)
        for p in range(1, n_pairs):
            o = o + _dot(refs[p][...], refs[n_pairs + p][...], NN)
        gate = mg_ref[0, 3 * k + 2:3 * k + 3, :]
        o_ref[...] = o.astype(BF16)
        so_ref[...] = s_ref[...] + (coef * gate) * o

    row = pl.BlockSpec((tm, n), lambda i: (i, 0))
    return pl.pallas_call(
        body, name=name, grid=(t_rows // tm,),
        in_specs=[pl.BlockSpec((tm, a.shape[1]), lambda i: (i, 0)) for a, _ in pairs]
        + [pl.BlockSpec(b.shape, lambda i: (0, 0)) for _, b in pairs]
        + [row, pl.BlockSpec((1, mg.shape[1], n), lambda i: (grp(i), 0, 0))],
        out_specs=[row, row], out_shape=[SDS((t_rows, n), F32), SDS((t_rows, n), BF16)], compiler_params=_cparams(),
    )(*[a for a, _ in pairs], *[b for _, b in pairs], s, mg)


def _dw_pair(name, a1, a2, b):
    kk, m = a1.shape
    n = b.shape[1]
    tm = _tile(m, 256, 128)

    def body(a1_ref, a2_ref, b_ref, o1_ref, o2_ref):
        bb = b_ref[...]
        o1_ref[...] = _dot(a1_ref[...], bb, TN).astype(BF16)
        o2_ref[...] = _dot(a2_ref[...], bb, TN).astype(BF16)

    col = pl.BlockSpec((kk, tm), lambda i: (0, i))
    out = pl.BlockSpec((tm, n), lambda i: (i, 0))
    return pl.pallas_call(
        body, name=name, grid=(m // tm,), in_specs=[col, col, pl.BlockSpec(b.shape, lambda i: (0, 0))],
        out_specs=[out, out], out_shape=[SDS((m, n), BF16)] * 2, compiler_params=_cparams(),
    )(a1, a2, b)


def _groups(t_rows, tm, n_lat):
    nlt = n_lat // tm
    if nlt < t_rows // tm:
        return 2, (lambda i: jnp.where(i >= nlt, 1, 0)), (lambda i: (i == 0) | (i == nlt))
    return 1, (lambda i: 0), (lambda i: i == 0)


def _accumulate(acc_refs, vals, first):
    @pl.when(first)
    def _():
        for r, v in zip(acc_refs, vals):
            r[0] = v

    @pl.when(jnp.logical_not(first))
    def _():
        for r, v in zip(acc_refs, vals):
            r[0] += v


def _adaln_math(s, m, k):
    n, _ = _rms(s)
    return (n * m[9 + k:10 + k]) * (1.0 + m[3 * k + 1:3 * k + 2]) + m[3 * k:3 * k + 1]


def _ffn_up(name, s, mg, k, n_lat, wg_t, wu_t):
    t_rows, f = s.shape[0], wg_t.shape[0]
    tm = _row_tm(t_rows, n_lat)
    _, grp, _ = _groups(t_rows, tm, n_lat)

    def body(s_ref, mg_ref, wg_ref, wu_ref, u_ref, a_ref, b_ref, h_ref):
        uu = _adaln_math(s_ref[...], mg_ref[0], k).astype(BF16)
        u_ref[...] = uu
        a = _dot(uu, wg_ref[...], NT)
        b = _dot(uu, wu_ref[...], NT)
        sg = jax.nn.sigmoid(a)
        act = a * sg
        a_ref[...] = (b * (sg * (1.0 + a * (1.0 - sg)))).astype(BF16)
        b_ref[...] = act.astype(BF16)
        h_ref[...] = (act * b).astype(BF16)

    w_spec = pl.BlockSpec(wg_t.shape, lambda i: (0, 0))
    o_spec = pl.BlockSpec((tm, f), lambda i: (i, 0))
    row = pl.BlockSpec((tm, s.shape[1]), lambda i: (i, 0))
    return pl.pallas_call(
        body, name=name, grid=(t_rows // tm,),
        in_specs=[row, pl.BlockSpec((1,) + mg.shape[1:], lambda i: (grp(i), 0, 0)), w_spec, w_spec],
        out_specs=[row, o_spec, o_spec, o_spec],
        out_shape=[SDS(s.shape, BF16)] + [SDS((t_rows, f), BF16)] * 3, compiler_params=_cparams(),
    )(s, mg, wg_t, wu_t)


def _ffn_dact(name, ds_out, o, mg, k, coef, n_lat, wd, a, b):
    t_rows, f = ds_out.shape[0], wd.shape[0]
    tm = _row_tm(t_rows, n_lat)
    n_groups, grp, first = _groups(t_rows, tm, n_lat)
    d = ds_out.shape[1]

    def body(ds_ref, o_ref, mg_ref, wd_ref, a_ref, b_ref, do_ref, da_ref, db_ref, dg_ref):
        dd = coef * ds_ref[...]
        do = (dd * mg_ref[0, 3 * k + 2:3 * k + 3, :]).astype(BF16)
        do_ref[...] = do
        _accumulate([dg_ref], [_colsum(dd * o_ref[...].astype(F32))], first(pl.program_id(0)))
        dh = _dot(do, wd_ref[...], NT)
        da_ref[...] = (dh * a_ref[...].astype(F32)).astype(BF16)
        db_ref[...] = (dh * b_ref[...].astype(F32)).astype(BF16)

    row = pl.BlockSpec((tm, d), lambda i: (i, 0))
    t_spec = pl.BlockSpec((tm, f), lambda i: (i, 0))
    return pl.pallas_call(
        body, name=name, grid=(t_rows // tm,),
        in_specs=[row, row, pl.BlockSpec((1,) + mg.shape[1:], lambda i: (grp(i), 0, 0)),
                  pl.BlockSpec(wd.shape, lambda i: (0, 0)), t_spec, t_spec],
        out_specs=[row, t_spec, t_spec, pl.BlockSpec((1, 1, d), lambda i: (grp(i), 0, 0))],
        out_shape=[SDS((t_rows, d), BF16), SDS((t_rows, f), BF16), SDS((t_rows, f), BF16), SDS((n_groups, 1, d), F32)],
        compiler_params=_cparams(),
    )(ds_out, o, mg, wd, a, b)


def _du_adaln(name, pairs, s, ds_out, mg, k, n_lat, after, out_rows=None):
    t_rows, d = s.shape
    tm = _row_tm(t_rows, n_lat)
    n_groups, grp, first = _groups(t_rows, tm, n_lat)
    n_pairs = len(pairs)
    nt, n_ds, n_out = t_rows // tm, ds_out.shape[0] // tm, (out_rows or t_rows) // tm

    def body(*refs):
        s_ref, ds_ref, mg_ref, z_ref, out_ref, dsh_ref, dsc_ref, dgn_ref = refs[2 * n_pairs:]
        i = pl.program_id(0)
        d_u = z_ref[...]
        for p in range(n_pairs):
            d_u = d_u + _dot(refs[p][...], refs[n_pairs + p][...], NN)
        m = mg_ref[0]
        gain, scale = m[9 + k:10 + k], m[3 * k + 1:3 * k + 2]
        n, r = _rms(s_ref[...])
        dxn = d_u * (1.0 + scale)
        ds_in = _rms_bwd(n, r, dxn * gain)
        ds_in = ds_in + (ds_ref[...] if n_ds == nt else jnp.where(i < n_ds, ds_ref[...], 0.0))
        if n_out == nt:
            out_ref[...] = ds_in
        else:
            @pl.when(i < n_out)
            def _():
                out_ref[...] = ds_in
        _accumulate([dsh_ref, dsc_ref, dgn_ref], [_colsum(d_u), _colsum(d_u * (n * gain)), _colsum(dxn * n)], first(i))

    row = pl.BlockSpec((tm, d), lambda i: (i, 0))
    acc = pl.BlockSpec((1, 1, d), lambda i: (grp(i), 0, 0))
    res = pl.pallas_call(
        body, name=name, grid=(t_rows // tm,),
        in_specs=[pl.BlockSpec((tm, a.shape[1]), lambda i: (i, 0)) for a, _ in pairs]
        + [pl.BlockSpec(w.shape, lambda i: (0, 0)) for _, w in pairs]
        + [row, pl.BlockSpec((tm, d), lambda i: (jnp.minimum(i, n_ds - 1), 0)),
           pl.BlockSpec((1,) + mg.shape[1:], lambda i: (grp(i), 0, 0)), pl.BlockSpec((1, d), lambda i: (0, 0))],
        out_specs=[pl.BlockSpec((tm, d), lambda i: (jnp.minimum(i, n_out - 1), 0)), acc, acc, acc],
        out_shape=[SDS((n_out * tm, d), F32)] + [SDS((n_groups, 1, d), F32)] * 3, compiler_params=_cparams(),
    )(*[a for a, _ in pairs], *[w for _, w in pairs], s, ds_out, mg, after)
    return res[0], res[1:]


def _adaln_mm(name, s, mg, k, n_lat, w_t):
    rows, d = s.shape
    tm = _row_tm(rows, n_lat)
    _, grp, _ = _groups(rows, tm, n_lat)
    n = w_t.shape[0]

    def body(s_ref, mg_ref, w_ref, u_ref, y_ref):
        uu = _adaln_math(s_ref[...], mg_ref[0], k).astype(BF16)
        u_ref[...] = uu
        y_ref[...] = _dot(uu, w_ref[...], NT)

    row = pl.BlockSpec((tm, d), lambda i: (i, 0))
    return pl.pallas_call(
        body, name=name, grid=(rows // tm,),
        in_specs=[row, pl.BlockSpec((1,) + mg.shape[1:], lambda i: (grp(i), 0, 0)), pl.BlockSpec(w_t.shape, lambda i: (0, 0))],
        out_specs=[row, pl.BlockSpec((tm, n), lambda i: (i, 0))],
        out_shape=[SDS((rows, d), BF16), SDS((rows, n), F32)], compiler_params=_cparams(),
    )(s, mg, w_t)


def _gate_mm(name, ds_out, o, mg, k, coef, n_lat, w):
    t_rows, d = ds_out.shape
    tm = _row_tm(t_rows, n_lat)
    n_groups, grp, first = _groups(t_rows, tm, n_lat)
    n = w.shape[0]

    def body(ds_ref, o_ref, mg_ref, w_ref, do_ref, y_ref, dg_ref):
        dd = coef * ds_ref[...]
        do = (dd * mg_ref[0, 3 * k + 2:3 * k + 3, :]).astype(BF16)
        do_ref[...] = do
        _accumulate([dg_ref], [_colsum(dd * o_ref[...].astype(F32))], first(pl.program_id(0)))
        y_ref[...] = _dot(do, w_ref[...], NT)

    row = pl.BlockSpec((tm, d), lambda i: (i, 0))
    return pl.pallas_call(
        body, name=name, grid=(t_rows // tm,),
        in_specs=[row, row, pl.BlockSpec((1,) + mg.shape[1:], lambda i: (grp(i), 0, 0)), pl.BlockSpec(w.shape, lambda i: (0, 0))],
        out_specs=[row, pl.BlockSpec((tm, n), lambda i: (i, 0)), pl.BlockSpec((1, 1, d), lambda i: (grp(i), 0, 0))],
        out_shape=[SDS((t_rows, d), BF16), SDS((t_rows, n), F32), SDS((n_groups, 1, d), F32)],
        compiler_params=_cparams(),
    )(ds_out, o, mg, w)


def _row_tm(t_rows, n_lat):
    return _tile(math.gcd(t_rows, n_lat), 256, 16)


def _rmsnorm_fwd(name, x, width, colblk, gain, t_rows):
    def fn(rv, vv):
        n, _ = _rms(rv[0])
        return [n * vv[0]], []

    (y,), _ = _rowwise(name, fn, t_rows, _tile(t_rows, 256, 16), t_rows, [(x, width, colblk)],
                       [gain.reshape(1, 1, width)], [(width, BF16)], [])
    return y


def _rmsnorm_bwd(name, x, width, colblk, dy, gain, t_rows, out_dtype=F32, into=None):
    def fn(rv, vv):
        n, r = _rms(rv[0])
        return [_rms_bwd(n, r, rv[1] * vv[0])], [_colsum(rv[1] * n)]

    (dx,), (dgain,) = _rowwise(name, fn, t_rows, _tile(t_rows, 256, 16), t_rows,
                               [(x, width, colblk), (dy, width, 0)], [gain.reshape(1, 1, width)],
                               [(width, out_dtype)], [width], into)
    return dx, dgain


def _final_loss(name, h, target, gain):
    t_rows = h.shape[0]
    inv_d = 1.0 / D_MODEL

    def fn(rv, vv):
        g = vv[0]
        n, r = _rms(rv[0])
        e = n * g - rv[1]
        dy = e * inv_d
        return [_rms_bwd(n, r, dy * g)], [_colsum(e * e), _colsum(dy * n)]

    (dh,), (sq, dgain) = _rowwise(name, fn, t_rows, _tile(t_rows, 256, 16), t_rows,
                                  [(h, D_MODEL, 0), (target, D_MODEL, 0)], [gain.reshape(1, 1, D_MODEL)],
                                  [(D_MODEL, F32)], [D_MODEL, D_MODEL])
    return dh, sq, dgain


def _exact_dot(x, m_ref):
    hi = x.astype(BF16)
    lo = (x - hi.astype(F32)).astype(BF16)
    return _dot(hi, m_ref[...], NN) + _dot(lo, m_ref[...], NN)


def _rope(name, z, width, colblk, cos32, sin32, layout, backward, out_dtype, remap=None, into=None):
    t_rows = cos32.shape[0]
    expand, plain, perm = layout
    w_in = remap.shape[1] if (remap is not None and backward) else width
    w_out = remap.shape[1] if (remap is not None and not backward) else width
    extra = [] if remap is None else [remap.T if backward else remap]
    dest = [] if into is None else [into[0]]

    def body(z_ref, c_ref, s_ref, e_ref, m_ref, p_ref, *rest):
        o_ref = rest[-1]
        zz = z_ref[...]
        if remap is not None and backward:
            zz = _exact_dot(zz, rest[0])
        cos = _exact_dot(c_ref[...], e_ref) + m_ref[...]
        sin = _exact_dot(s_ref[...], e_ref)
        rot = _exact_dot(zz * sin if backward else zz, p_ref)
        if not backward:
            rot = rot * sin
        res = zz * cos + rot
        if remap is not None and not backward:
            res = _dot(res.astype(BF16), rest[0][...], NN)
        o_ref[...] = res.astype(o_ref.dtype)

    tm = _tile(t_rows, 256, 16)
    f_spec = pl.BlockSpec((tm, QK_ROPE), lambda i: (i, 0))
    return pl.pallas_call(
        body, name=name, grid=(t_rows // tm,),
        in_specs=[pl.BlockSpec((tm, w_in), lambda i: (i, colblk)), f_spec, f_spec,
                  pl.BlockSpec((QK_ROPE, width), lambda i: (0, 0)), pl.BlockSpec((1, width), lambda i: (0, 0)),
                  pl.BlockSpec((width, width), lambda i: (0, 0))]
        + [pl.BlockSpec(e.shape, lambda i: (0, 0)) for e in extra] + [pl.BlockSpec(memory_space=pl.ANY)] * len(dest),
        out_specs=pl.BlockSpec((tm, w_out), lambda i: (i, 0 if into is None else into[1])),
        out_shape=SDS((t_rows, w_out), out_dtype) if into is None else SDS(into[0].shape, into[0].dtype),
        input_output_aliases={} if into is None else {6 + len(extra): 0}, compiler_params=_cparams(),
    )(z, cos32, sin32, expand, plain, perm.T if backward else perm, *extra, *dest)


def _window_sum(x, w, transposed):
    n_rows = x.shape[0]
    zeros = jnp.zeros((POOL_PAD, x.shape[1]), F32)
    y = jnp.concatenate([zeros, x, zeros], axis=0)
    total = n_rows + 2 * POOL_PAD
    if transposed:
        y = y + pltpu.roll(y, total - 1, 0)
    else:
        y = y + pltpu.roll(y, 1, 0)
    step = 1
    while 2 * step < w:
        y = pltpu.roll(y, step, 0) + pltpu.roll(y, total - step, 0)
        step *= 2
    return y[POOL_PAD:POOL_PAD + n_rows]


def _window_count(n_rows, w):
    t = lax.broadcasted_iota(jnp.int32, (n_rows, 1), 0)
    lo = jnp.maximum(t - w // 2, 0)
    hi = jnp.minimum(t + (w - w // 2 - 1), n_rows - 1)
    return (hi - lo + 1).astype(F32)


def _pool_fwd(name, proj, n_rows, w_grp, scale):
    def body(x_ref, w_ref, sc_ref, y_ref, p_ref):
        for g, w in enumerate(POOL_WINDOWS):
            cols = slice(g * POOL_GROUP_DIM, (g + 1) * POOL_GROUP_DIM)
            x = x_ref[:, cols]
            p = _window_sum(x, w, False) * (1.0 / _window_count(n_rows, w)) - x
            pb = p.astype(BF16)
            p_ref[:, cols] = pb
            y_ref[:, cols] = (_dot(pb, w_ref[g], NN) * sc_ref[:, cols]).astype(BF16)

    blk = pl.BlockSpec((n_rows, POOL_DIM), lambda i: (0, 0))
    return pl.pallas_call(
        body, name=name, grid=(1,),
        in_specs=[blk, pl.BlockSpec(w_grp.shape, lambda i: (0, 0, 0)), pl.BlockSpec((1, POOL_DIM), lambda i: (0, 0))],
        out_specs=[blk, blk], out_shape=[SDS((n_rows, POOL_DIM), BF16)] * 2, compiler_params=_cparams(),
    )(proj, w_grp, scale)


def _pool_bwd(name, dcat, n_rows, p, w_grp, scale, into):
    def body(dy_ref, p_ref, w_ref, sc_ref, into_ref, dx_ref, dw_ref, dsc_ref):
        for g, w in enumerate(POOL_WINDOWS):
            cols = slice(g * POOL_GROUP_DIM, (g + 1) * POOL_GROUP_DIM)
            dy = dy_ref[:, cols]
            pb = p_ref[:, cols]
            pw = _dot(pb, w_ref[g], NN)
            dsc_ref[:, cols] = _colsum(dy * pw)
            dpw = (dy * sc_ref[:, cols]).astype(BF16)
            dw_ref[g] = _dot(pb, dpw, TN)
            dp = _dot(dpw, w_ref[g], NT)
            dx_ref[:, cols] = (_window_sum(dp * (1.0 / _window_count(n_rows, w)), w, True) - dp).astype(BF16)

    blk = pl.BlockSpec((n_rows, POOL_DIM), lambda i: (0, 0))
    w_spec = pl.BlockSpec(w_grp.shape, lambda i: (0, 0, 0))
    v_spec = pl.BlockSpec((1, POOL_DIM), lambda i: (0, 0))
    return pl.pallas_call(
        body, name=name, grid=(1,), in_specs=[blk, blk, w_spec, v_spec, pl.BlockSpec(memory_space=pl.ANY)],
        out_specs=[blk, w_spec, v_spec],
        out_shape=[SDS(into.shape, into.dtype), SDS(w_grp.shape, F32), SDS((1, POOL_DIM), F32)],
        input_output_aliases={4: 0}, compiler_params=_cparams(),
    )(dcat, p, w_grp, scale, into)


def _head_keys(kv_blk, k_rope):
    lane = lax.broadcasted_iota(jnp.int32, (1, HEAD_PAD), 1)
    return jnp.where(lane < QK_NOPE, kv_blk, k_rope)


def _attn_fwd(name, q, kv, k_rope, n_q):
    n_k = kv.shape[0]
    h = kv.shape[1] // HEAD_PAD
    tq = _tile(n_q, ATTN_Q_ROWS_FWD, 16)

    def body(q_ref, kv_ref, kr_ref, o_ref, lse_ref):
        kvb = kv_ref[...]
        s = _dot(q_ref[...], _head_keys(kvb, kr_ref[...]), NT) * ATTN_SCALE
        m = jnp.max(s, axis=-1, keepdims=True)
        e = jnp.exp(s - m)
        l = jnp.sum(e, axis=-1, keepdims=True)
        p = (e * (1.0 / l)).astype(BF16)
        lane = lax.broadcasted_iota(jnp.int32, (1, HEAD_PAD), 1)
        o_ref[...] = jnp.where(lane >= QK_NOPE, _dot(p, kvb, NN), 0.0).astype(BF16)
        lse_ref[...] = m + jnp.log(l)

    blk = pl.BlockSpec((tq, HEAD_PAD), lambda hh, i: (i, hh))
    return pl.pallas_call(
        body, name=name, grid=(h, n_q // tq),
        in_specs=[blk, pl.BlockSpec((n_k, HEAD_PAD), lambda hh, i: (0, hh)),
                  pl.BlockSpec((n_k, HEAD_PAD), lambda hh, i: (0, 0))],
        out_specs=[blk, pl.BlockSpec((None, tq, 1), lambda hh, i: (hh, i, 0))],
        out_shape=[SDS((n_q, h * HEAD_PAD), BF16), SDS((h, n_q, 1), F32)], compiler_params=_cparams(),
    )(q, kv, k_rope)


def _attn_bwd(name, q, kv, k_rope, o, lse, dy, dy_col0, n_q):
    n_k = kv.shape[0]
    h = kv.shape[1] // HEAD_PAD
    tq = _tile(n_q, ATTN_Q_ROWS_BWD, 16)
    n_i = n_q // tq

    def body(q_ref, kv_ref, kr_ref, o_ref, lse_ref, do_ref, dq_ref, dkv_ref, dkr_ref, acc_k, acc_v):
        hh, i = pl.program_id(0), pl.program_id(1)
        qq, kvb = q_ref[...], kv_ref[...]
        kk = _head_keys(kvb, kr_ref[...])
        d_o = do_ref[...]
        dd = d_o.astype(BF16)
        s = _dot(qq, kk, NT) * ATTN_SCALE
        p = jnp.exp(s - lse_ref[...])
        dp = _dot(dd, kvb, NT)
        delta = jnp.sum(d_o * o_ref[...].astype(F32), axis=-1, keepdims=True)
        ds = (p * (dp - delta) * ATTN_SCALE).astype(BF16)
        dq_ref[...] = _dot(ds, kk, NN)
        dk = _dot(ds, qq, TN)
        dv = _dot(p.astype(BF16), dd, TN)

        @pl.when(i == 0)
        def _():
            acc_k[...] = dk
            acc_v[...] = dv

        @pl.when(i > 0)
        def _():
            acc_k[...] += dk
            acc_v[...] += dv

        @pl.when(i == n_i - 1)
        def _():
            lane = lax.broadcasted_iota(jnp.int32, (1, HEAD_PAD), 1)
            dkv_ref[...] = jnp.where(lane < QK_NOPE, acc_k[...], acc_v[...]).astype(BF16)
            rope = jnp.where((lane >= QK_NOPE) & (lane < QK_HEAD), acc_k[...], 0.0)

            @pl.when(hh == 0)
            def _():
                dkr_ref[...] = rope

            @pl.when(hh > 0)
            def _():
                dkr_ref[...] += rope

    blk = pl.BlockSpec((tq, HEAD_PAD), lambda hh, i: (i, hh))
    kv_spec = pl.BlockSpec((n_k, HEAD_PAD), lambda hh, i: (0, hh))
    shared = pl.BlockSpec((n_k, HEAD_PAD), lambda hh, i: (0, 0))
    return pl.pallas_call(
        body, name=name, grid=(h, n_i),
        in_specs=[blk, kv_spec, shared, blk, pl.BlockSpec((None, tq, 1), lambda hh, i: (hh, i, 0)),
                  pl.BlockSpec((tq, HEAD_PAD), lambda hh, i: (i, dy_col0 + hh))],
        out_specs=[blk, kv_spec, shared],
        out_shape=[SDS((n_q, h * HEAD_PAD), F32), SDS((n_k, h * HEAD_PAD), BF16), SDS((n_k, HEAD_PAD), F32)],
        scratch_shapes=[pltpu.VMEM((n_k, HEAD_PAD), F32), pltpu.VMEM((n_k, HEAD_PAD), F32)],
        compiler_params=_cparams(),
    )(q, kv, k_rope, o, lse, dy)


CONV_COLS = 256


def _shift_rows(x, d):
    n_rows = x.shape[0]
    t = lax.broadcasted_iota(jnp.int32, (n_rows, 1), 0)
    if d > 0:
        return jnp.where(t >= d, pltpu.roll(x, d, 0), 0.0)
    return jnp.where(t < n_rows + d, pltpu.roll(x, n_rows + d, 0), 0.0)


def _conv_fwd(name, z3, conv_w):
    n_rows = z3.shape[0]
    nb = D_MODEL // CONV_COLS

    def body(b_ref, c_ref, v_ref, w_ref, y_ref):
        z = c_ref[...] * v_ref[...]
        zc = w_ref[0:1, :] * _shift_rows(z, 1) + w_ref[1:2, :] * z + w_ref[2:3, :] * _shift_rows(z, -1)
        y_ref[...] = (b_ref[...] * zc).astype(BF16)

    def part(k):
        return pl.BlockSpec((n_rows, CONV_COLS), lambda j: (0, k * nb + j))

    return pl.pallas_call(
        body, name=name, grid=(nb,),
        in_specs=[part(0), part(1), part(2), pl.BlockSpec((3, CONV_COLS), lambda j: (0, j))],
        out_specs=pl.BlockSpec((n_rows, CONV_COLS), lambda j: (0, j)),
        out_shape=SDS((n_rows, D_MODEL), BF16), compiler_params=_cparams(),
    )(z3, z3, z3, conv_w)


def _conv_bwd(name, dy, z3, conv_w):
    n_rows = z3.shape[0]
    nb = D_MODEL // CONV_COLS

    def body(dy_ref, b_ref, c_ref, v_ref, w_ref, db_ref, dc_ref, dv_ref, dw_ref):
        c, v, d_y = c_ref[...], v_ref[...], dy_ref[...]
        z = c * v
        z_dn, z_up = _shift_rows(z, 1), _shift_rows(z, -1)
        zc = w_ref[0:1, :] * z_dn + w_ref[1:2, :] * z + w_ref[2:3, :] * z_up
        db_ref[...] = (d_y * zc).astype(BF16)
        dzc = d_y * b_ref[...]
        dz = w_ref[0:1, :] * _shift_rows(dzc, -1) + w_ref[1:2, :] * dzc + w_ref[2:3, :] * _shift_rows(dzc, 1)
        dc_ref[...] = (dz * v).astype(BF16)
        dv_ref[...] = (dz * c).astype(BF16)
        dw_ref[0:1, :] = _colsum(dzc * z_dn)
        dw_ref[1:2, :] = _colsum(dzc * z)
        dw_ref[2:3, :] = _colsum(dzc * z_up)

    def part(k):
        return pl.BlockSpec((n_rows, CONV_COLS), lambda j: (0, k * nb + j))

    col = pl.BlockSpec((n_rows, CONV_COLS), lambda j: (0, j))
    w_spec = pl.BlockSpec((3, CONV_COLS), lambda j: (0, j))
    return pl.pallas_call(
        body, name=name, grid=(nb,), in_specs=[col, part(0), part(1), part(2), w_spec],
        out_specs=[col, col, col, w_spec],
        out_shape=[SDS((n_rows, D_MODEL), BF16)] * 3 + [SDS((3, D_MODEL), F32)], compiler_params=_cparams(),
    )(dy, z3, z3, z3, conv_w)


def _silu_rows(name, x):
    def body(x_ref, s_ref, d_ref):
        xx = x_ref[...]
        sg = jax.nn.sigmoid(xx)
        s_ref[...] = (xx * sg).astype(BF16)
        d_ref[...] = sg * (1.0 + xx * (1.0 - sg))

    return pl.pallas_call(body, name=name, out_shape=[SDS(x.shape, BF16), SDS(x.shape, F32)])(x)


def _sum_rows(name, x, scale=None):
    r, n = x.shape
    tn = _tile(n, 32768, 128)

    def body(*refs):
        acc = jnp.sum(refs[0][...].astype(F32), axis=0, keepdims=True)
        if scale is not None:
            acc = acc * refs[1][...]
        refs[-1][...] = acc

    in_specs = [pl.BlockSpec((r, tn), lambda j: (0, j))]
    args = [x]
    if scale is not None:
        in_specs.append(pl.BlockSpec((1, tn), lambda j: (0, j)))
        args.append(scale)
    return pl.pallas_call(body, name=name, grid=(n // tn,), in_specs=in_specs,
                          out_specs=pl.BlockSpec((1, tn), lambda j: (0, j)), out_shape=SDS((1, n), F32))(*args)


def _me_operand(me):
    return jnp.reshape(me, (1,)).astype(jnp.int32)


def _sum_slots(name, slots, src, me):
    n_slots, r, c = slots.shape
    tr = _tile(r, 432, 16)

    def body(me_ref, own_ref, x_ref, o_ref):
        acc = own_ref[...].astype(F32)
        for sl in range(n_slots):
            acc = acc + x_ref[sl].astype(F32)
        o_ref[...] = acc

    grid_spec = pltpu.PrefetchScalarGridSpec(
        num_scalar_prefetch=1, grid=(r // tr,),
        in_specs=[pl.BlockSpec((None, tr, c), lambda i, me_ref: (me_ref[0], i, 0)),
                  pl.BlockSpec((n_slots, tr, c), lambda i, me_ref: (0, i, 0))],
        out_specs=pl.BlockSpec((tr, c), lambda i, me_ref: (i, 0)))
    return pl.pallas_call(body, name=name, grid_spec=grid_spec, out_shape=SDS((r, c), F32),
                          compiler_params=_cparams())(_me_operand(me), src, slots)


def _adamw(name, w, g, m, v):
    shape = w.shape
    cols = shape[-1]
    rows = w.size // cols
    tr = _tile(rows, 512, 8)
    bc1 = 1.0 - ADAM_B1 ** ADAM_STEP
    bc2 = 1.0 - ADAM_B2 ** ADAM_STEP

    def body(w_ref, g_ref, m_ref, v_ref, d_ref, nm_ref, nv_ref):
        gg = g_ref[...]
        nm = ADAM_B1 * m_ref[...] + (1.0 - ADAM_B1) * gg
        nv = ADAM_B2 * v_ref[...] + (1.0 - ADAM_B2) * (gg * gg)
        nm_ref[...] = nm
        nv_ref[...] = nv
        d_ref[...] = -ADAM_LR * ((nm / bc1) / (jnp.sqrt(nv / bc2) + ADAM_EPS) + ADAM_WD * w_ref[...])

    spec = pl.BlockSpec((tr, cols), lambda i: (i, 0))
    outs = pl.pallas_call(body, name=name, grid=(rows // tr,), in_specs=[spec] * 4, out_specs=[spec] * 3,
                          out_shape=[SDS((rows, cols), F32)] * 3, compiler_params=_cparams())(
        w.reshape(rows, cols), g.reshape(rows, cols), m.reshape(rows, cols), v.reshape(rows, cols))
    return tuple(t.reshape(shape) for t in outs)


def _exchange(name, x, scatter, after=None):
    blk = x.shape[1:] if scatter else x.shape
    extra = [] if after is None else [after]

    def body(x_ref, *rest):
        out_ref, send_sems, recv_sems, local_sem = rest[len(extra):]
        mx, my, mc = lax.axis_index("x"), lax.axis_index("y"), lax.axis_index("c")
        me = 4 * mx + 2 * my + mc
        own = pltpu.make_async_copy(x_ref.at[me] if scatter else x_ref, out_ref.at[me], local_sem)
        own.start()
        copies = []
        for kk in range(1, N_DEV):
            px = jnp.bitwise_xor(mx, (kk >> 2) & 1)
            py = jnp.bitwise_xor(my, (kk >> 1) & 1)
            pc = jnp.bitwise_xor(mc, kk & 1)
            peer = 4 * px + 2 * py + pc
            send = pltpu.make_async_remote_copy(
                src_ref=x_ref.at[peer] if scatter else x_ref, dst_ref=out_ref.at[me],
                send_sem=send_sems.at[kk - 1], recv_sem=recv_sems.at[kk - 1],
                device_id=(px, py, pc), device_id_type=MESH)
            send.start()
            arrival = pltpu.make_async_remote_copy(
                src_ref=x_ref.at[peer] if scatter else x_ref, dst_ref=out_ref.at[peer],
                send_sem=send_sems.at[kk - 1], recv_sem=recv_sems.at[kk - 1],
                device_id=(px, py, pc), device_id_type=MESH)
            copies.append((send, arrival))
        for send, arrival in copies:
            arrival.wait_recv()
            send.wait_send()
        own.wait()

    return pl.pallas_call(
        body, name=name, out_shape=SDS((N_DEV,) + tuple(blk), x.dtype),
        in_specs=[pl.BlockSpec(memory_space=pl.ANY)] * (1 + len(extra)), out_specs=pl.BlockSpec(memory_space=pl.ANY),
        scratch_shapes=[pltpu.SemaphoreType.DMA((N_DEV - 1,)), pltpu.SemaphoreType.DMA((N_DEV - 1,)),
                        pltpu.SemaphoreType.DMA],
    )(x, *extra)


def _rope_perm(pre, reps, post):
    half = QK_ROPE // 4
    width = reps * (pre + QK_ROPE) + post
    p = np.zeros((width, width), np.float32)
    for rep in range(reps):
        s0 = rep * (pre + QK_ROPE) + pre
        for base in (s0, s0 + 2 * half):
            for i in range(half):
                p[base + half + i, base + i] = -1.0
                p[base + i, base + half + i] = 1.0
    return p


def _rope_layout(pre, reps, post):
    width = reps * (pre + QK_ROPE) + post
    expand = np.zeros((QK_ROPE, width), np.float32)
    plain = np.ones((1, width), np.float32)
    for rep in range(reps):
        s0 = rep * (pre + QK_ROPE) + pre
        expand[np.arange(QK_ROPE), s0 + np.arange(QK_ROPE)] = 1.0
        plain[0, s0:s0 + QK_ROPE] = 0.0
    return jnp.asarray(expand, BF16), jnp.asarray(plain, F32), jnp.asarray(_rope_perm(pre, reps, post), BF16)


def _head_spread():
    spread = np.zeros((HEADS * QK_HEAD, HEADS * HEAD_PAD), np.float32)
    for hh in range(HEADS):
        spread[hh * QK_HEAD + np.arange(QK_HEAD), hh * HEAD_PAD + np.arange(QK_HEAD)] = 1.0
    return jnp.asarray(spread, BF16)


def _rope_factors(n_lat, t_rows):
    half = QK_ROPE // 4
    pos = jnp.arange(n_lat)
    freqs = jnp.power(ROPE_THETA, -jnp.arange(0, 2 * half, 2, dtype=F32) / (2 * half))
    ang_r = (pos // GRID_W).astype(F32)[:, None] * freqs
    ang_c = (pos % GRID_W).astype(F32)[:, None] * freqs
    ang = jnp.concatenate([ang_r, ang_r, ang_c, ang_c], axis=-1)
    rest = t_rows - n_lat
    return (jnp.concatenate([jnp.cos(ang), jnp.ones((rest, QK_ROPE), F32)], axis=0),
            jnp.concatenate([jnp.sin(ang), jnp.zeros((rest, QK_ROPE), F32)], axis=0))


def _ffn_half_fwd(tag, s, mg, k, feed, i, coef, n_lat):
    wg_t, wu_t = feed.weights(f"{tag}_up", [f"gate_t{i}", f"up_t{i}"], s)
    u, a, b, hid = _ffn_up(f"{tag}_up", s, mg, k, n_lat, wg_t, wu_t)
    (wd,) = feed.weights(f"{tag}_down", [f"down{i}"], hid)
    s_out, o = _mm_resid(f"{tag}_down", [(hid, wd)], s, mg, k, coef, n_lat)
    return s_out, (s, u, a, b, hid, o, wg_t, wu_t, wd)


def _ffn_half_bwd(tag, ds_out, saved, mg, k, feed, i, coef, n_lat, out_rows=None):
    s, u, a, b, hid, o, wg_t, wu_t, wd = saved
    do, da, db, dgate = _ffn_dact(f"{tag}_dact", ds_out, o, mg, k, coef, n_lat, wd, a, b)
    dwd = _mm(f"{tag}_dwd", [(hid, do)], "tn", BF16)
    dwg_t, dwu_t = _dw_pair(f"{tag}_dwgu", da, db, u)
    token = feed.grads(tag, {f"down{i}": dwd, f"gate_t{i}": dwg_t, f"up_t{i}": dwu_t})
    ds_in, (dshift, dscale, dgain) = _du_adaln(f"{tag}_du", [(da, wg_t), (db, wu_t)], s, ds_out, mg, k, n_lat,
                                               _after(token), out_rows)
    return ds_in, dict(shift=dshift, scale=dscale, gate=dgate, gain=dgain)


def _after(token):
    return jnp.zeros((1, D_MODEL), F32) + token


def _mod_grad(parts, n_groups):
    rows = []
    zero = jnp.zeros((n_groups, 1, D_MODEL), F32)
    for k in range(3):
        for nm in ("shift", "scale", "gate"):
            t = parts[k].get(nm, zero)
            if t.shape[0] < n_groups:
                t = jnp.concatenate([t, jnp.zeros((n_groups - t.shape[0], 1, D_MODEL), F32)], axis=0)
            rows.append(t)
    return jnp.concatenate(rows, axis=1).reshape(n_groups, N_MOD * D_MODEL)


def _local_step(x, ctx, target, mod_h, mod_g, norm_g, feed, pool_w, pool_scale, q_norm_g, kv_norm_g, conv_w,
                final_norm_g):
    n_lat, n_ctx = x.shape[0], ctx.shape[0]
    t_all = n_lat + n_ctx
    mg0 = jnp.stack([jnp.concatenate([mod_h[0], norm_g[0]], axis=0), jnp.concatenate([mod_g, norm_g[0]], axis=0)])
    mg1 = jnp.concatenate([mod_h[1], norm_g[1]], axis=0)[None]

    s0 = jnp.concatenate([x, ctx], axis=0)
    s1, sv_f00 = _ffn_half_fwd("l0f0", s0, mg0, 0, feed, 0, 0.5, n_lat)

    (w_in,) = feed.weights("l0m_in", ["in_t"], s1)
    kv_rows = KV_RANK + QK_ROPE
    w_in_t = jnp.concatenate([
        w_in[:POOL_DIM], jnp.zeros((PA_CQ - POOL_DIM, D_MODEL), BF16), w_in[POOL_DIM:POOL_DIM + Q_RANK],
        w_in[POOL_DIM + Q_RANK:], jnp.zeros((PA_KV_W - kv_rows, D_MODEL), BF16)], axis=0)
    ua, proj = _adaln_mm("l0m_proj", s1, mg0, 1, n_lat, w_in_t)
    w_uq, w_ukv_t, w_ab_out = feed.weights("l0m_rest", ["uq", "ukv_t", "ab_out"], proj)
    pool_y, pool_p = _pool_fwd("l0m_pool", proj, n_lat, pool_w.astype(BF16), pool_scale)
    nq = _rmsnorm_fwd("l0m_qnorm", proj, Q_RANK, PA_CQ // Q_RANK, q_norm_g, n_lat)
    q_lin = _mm("l0m_q", [(nq, w_uq)], "nn", F32, 512, 768)
    cos32, sin32 = _rope_factors(n_lat, t_all)
    lay_q, lay_k = _rope_layout(QK_NOPE, HEADS, 0), _rope_layout(KV_RANK, 1, PA_KV_W - kv_rows)
    spread = _head_spread()
    q_flat = _rope("l0m_qrope", q_lin, Q_RANK, 0, cos32[:n_lat], sin32[:n_lat], lay_q, False, BF16, spread)
    kvr = _rope("l0m_krope", proj, PA_KV_W, PA_KV // PA_KV_W, cos32, sin32, lay_k, False, F32)
    nkv = _rmsnorm_fwd("l0m_kvnorm", kvr, KV_RANK, 0, kv_norm_g, t_all)
    kv = _mm("l0m_kv", [(nkv, w_ukv_t)], "nt", BF16, 768, 512)
    k_rope = jnp.pad(kvr[:, KV_RANK:KV_RANK + QK_ROPE].astype(BF16), ((0, 0), (QK_NOPE, HEAD_PAD - QK_HEAD)))
    o_flat, lse = _attn_fwd("l0m_attn", q_flat, kv, k_rope, n_lat)
    w_o_pad = jnp.pad(w_ab_out[POOL_DIM:].reshape(HEADS, V_HEAD, D_MODEL),
                      ((0, 0), (HEAD_PAD - V_HEAD, 0), (0, 0))).reshape(HEADS * HEAD_PAD, D_MODEL)
    w_o_pool = w_ab_out[:POOL_DIM]
    h2, mix_o = _mm_resid("l0m_out", [(pool_y, w_o_pool), (o_flat, w_o_pad)], s1, mg0[:1], 1, 1.0, n_lat)

    h3, sv_f01 = _ffn_half_fwd("l0f1", h2, mg0[:1], 2, feed, 1, 0.5, n_lat)

    h4, sv_f10 = _ffn_half_fwd("l1f0", h3, mg1, 0, feed, 2, 0.5, n_lat)
    w_cin_t, w_c_out = feed.weights("l1m", ["cin_t", "c_out"], h4)
    uc, z3 = _adaln_mm("l1m_in", h4, mg1, 1, n_lat, w_cin_t)
    yc = _conv_fwd("l1m_conv", z3, conv_w)
    h5, conv_o = _mm_resid("l1m_out", [(yc, w_c_out)], h4, mg1, 1, 1.0, n_lat)
    h6, sv_f11 = _ffn_half_fwd("l1f1", h5, mg1, 2, feed, 3, 0.5, n_lat)

    dh6, sq_cols, d_final_g = _final_loss("loss_head", h6, target, final_norm_g)
    g = {}
    dh5, g["f11"] = _ffn_half_bwd("l1f1", dh6, sv_f11, mg1, 2, feed, 3, 0.5, n_lat)

    do_c, dyc, dgate_c = _gate_mm("l1m_dy", dh5, conv_o, mg1, 1, 1.0, n_lat, w_c_out)
    d_c_out = _mm("l1m_dwout", [(yc, do_c)], "tn", BF16)
    db_, dc_, dv_, d_conv_w = _conv_bwd("l1m_dconv", dyc, z3, conv_w)
    dz3 = jnp.concatenate([db_, dc_, dv_], axis=-1)
    d_cin_t = _mm("l1m_dwin", [(dz3, uc)], "tn", BF16)
    token = feed.grads("l1m", {"c_out": d_c_out, "cin_t": d_cin_t})
    dh4, (dsh_c, dsc_c, dgn_c) = _du_adaln("l1m_du", [(dz3, w_cin_t)], h4, dh5, mg1, 1, n_lat, _after(token))
    dh3, g["f10"] = _ffn_half_bwd("l1f0", dh4, sv_f10, mg1, 0, feed, 2, 0.5, n_lat)

    dh2, g["f01"] = _ffn_half_bwd("l0f1", dh3, sv_f01, mg0[:1], 2, feed, 1, 0.5, n_lat)

    w_back = jnp.concatenate([w_o_pool, w_o_pad], axis=0)
    do_a, dcat, dgate_a = _gate_mm("l0m_dcat", dh2, mix_o, mg0[:1], 1, 1.0, n_lat, w_back)
    d_o_pad = _mm("l0m_dwout_a", [(o_flat, do_a)], "tn", BF16)
    d_ab_out = jnp.concatenate([
        _mm("l0m_dwout_p", [(pool_y, do_a)], "tn", BF16),
        d_o_pad.reshape(HEADS, HEAD_PAD, D_MODEL)[:, HEAD_PAD - V_HEAD:].reshape(HEADS * V_HEAD, D_MODEL)], axis=0)
    dproj = jnp.zeros((t_all, PA_W), BF16)
    dproj, d_pool_w, d_pool_scale = _pool_bwd("l0m_dpool", dcat, n_lat, pool_p, pool_w.astype(BF16), pool_scale, dproj)
    dq_flat, dkv, dk_rope = _attn_bwd("l0m_dattn", q_flat, kv, k_rope, o_flat, lse, dcat, POOL_DIM // HEAD_PAD, n_lat)
    dq_lin = _rope("l0m_dqrope", dq_flat, Q_RANK, 0, cos32[:n_lat], sin32[:n_lat], lay_q, True, BF16, spread)
    d_uq = _mm("l0m_dwuq", [(nq, dq_lin)], "tn", BF16, 768, 768)
    dnq = _mm("l0m_dnq", [(dq_lin, w_uq)], "nt", F32, 512, 768)
    dproj, d_q_norm_g = _rmsnorm_bwd("l0m_dqnorm", proj, Q_RANK, PA_CQ // Q_RANK, dnq, q_norm_g, n_lat, BF16,
                                     (dproj, PA_CQ // Q_RANK))
    dnkv = _mm("l0m_dnkv", [(dkv, w_ukv_t)], "nn", F32, 768, 256)
    d_ukv_t = _mm("l0m_dwukv", [(dkv, nkv)], "tn", BF16, 512, 256)
    dckv, d_kv_norm_g = _rmsnorm_bwd("l0m_dkvnorm", kvr, KV_RANK, 0, dnkv, kv_norm_g, t_all)
    dkvr = jnp.concatenate([dckv, dk_rope[:, QK_NOPE:QK_HEAD],
                            jnp.zeros((t_all, PA_KV_W - KV_RANK - QK_ROPE), F32)], axis=-1)
    dproj = _rope("l0m_dkrope", dkvr, PA_KV_W, 0, cos32, sin32, lay_k, True, BF16, None, (dproj, PA_KV // PA_KV_W))
    d_in_pad = _mm("l0m_dwin", [(dproj, ua)], "tn", BF16, 640, 512)
    d_in_t = jnp.concatenate([d_in_pad[:POOL_DIM], d_in_pad[PA_CQ:PA_CQ + Q_RANK],
                              d_in_pad[PA_KV:PA_KV + kv_rows]], axis=0)
    token = feed.grads("l0m", {"ab_out": d_ab_out, "uq": d_uq, "ukv_t": d_ukv_t, "in_t": d_in_t})
    ds1, (dsh_a, dsc_a, dgn_a) = _du_adaln("l0m_du", [(dproj, w_in_t)], s1, dh2, mg0, 1, n_lat, _after(token))
    grad_x, g["f00"] = _ffn_half_bwd("l0f0", ds1, sv_f00, mg0, 0, feed, 0, 0.5, n_lat, out_rows=n_lat)

    dmod0 = _mod_grad([g["f00"], dict(shift=dsh_a, scale=dsc_a, gate=dgate_a), g["f01"]], 2)
    dmod1 = _mod_grad([g["f10"], dict(shift=dsh_c, scale=dsc_c, gate=dgate_c), g["f11"]], 1)
    d_norm_g = jnp.stack([
        jnp.concatenate([jnp.sum(g["f00"]["gain"], axis=0), jnp.sum(dgn_a, axis=0), g["f01"]["gain"][0]], axis=0),
        jnp.concatenate([g["f10"]["gain"][0], dgn_c[0], g["f11"]["gain"][0]], axis=0)])
    grads = dict(
        pool_w=d_pool_w, pool_scale=d_pool_scale, q_norm_g=d_q_norm_g[0], kv_norm_g=d_kv_norm_g[0],
        conv_w=d_conv_w, final_norm_g=d_final_g[0], norm_g=d_norm_g,
        mod_h=jnp.stack([dmod0[0], dmod1[0]]), mod_g=dmod0[1])
    return sq_cols, grad_x, grads


HBM_SPEC = pl.BlockSpec(memory_space=pltpu.HBM)
SEM_SPEC = pl.BlockSpec(memory_space=pltpu.SEMAPHORE)
ANY_SPEC = pl.BlockSpec(memory_space=pl.ANY)
SIDE_EFFECT = pltpu.SideEffectType.DATAFLOW_SIDE_EFFECTING
N_PEERS = N_DEV - 1


def _mesh_place():
    mx, my, mc = lax.axis_index("x"), lax.axis_index("y"), lax.axis_index("c")
    return mx, my, mc, 4 * mx + 2 * my + mc


def _peer(place, kk):
    mx, my, mc, _ = place
    px = jnp.bitwise_xor(mx, (kk >> 2) & 1)
    py = jnp.bitwise_xor(my, (kk >> 1) & 1)
    pc = jnp.bitwise_xor(mc, kk & 1)
    return (px, py, pc), 4 * px + 2 * py + pc


def _hbm(a):
    return pltpu.with_memory_space_constraint(a, pltpu.HBM)


def _landing(block, me):
    zone = lax.empty((N_DEV,) + block.shape, block.dtype)
    return lax.dynamic_update_slice(zone, block[None], (me,) + (0,) * block.ndim)


ALL_PEERS = tuple(range(1, N_DEV))
SIBLING = 1
CHIP_PEERS = (2, 4, 6)
RELAYED = (3, 5, 7)


def _exchange_start(name, srcs, lands, scatter, after, peers=ALL_PEERS):
    n = len(srcs)
    extra = [] if after is None else [after]

    def body(*refs):
        src, land = refs[:n], refs[n:2 * n]
        send_sems, recv_sems, token = refs[2 * n + len(extra)], refs[2 * n + len(extra) + 1], refs[-1]
        place = _mesh_place()
        for a in range(n):
            for kk in peers:
                dev, peer = _peer(place, kk)
                pltpu.make_async_remote_copy(
                    src_ref=src[a].at[peer] if scatter else src[a],
                    dst_ref=land[a].at[kk - 1] if scatter else land[a].at[place[3]],
                    send_sem=send_sems.at[a * N_PEERS + kk - 1], recv_sem=recv_sems.at[a * N_PEERS + kk - 1],
                    device_id=dev, device_id_type=MESH).start()
        token[...] = jnp.zeros_like(token)

    thru = [pltpu.HBM(t.shape, t.dtype) for t in (*srcs, *lands)]
    res = pl.pallas_call(
        body, name=name,
        out_shape=(pltpu.SemaphoreType.DMA((n * N_PEERS,)), pltpu.SemaphoreType.DMA((n * N_PEERS,)), *thru,
                   SDS((8, 128), F32)),
        in_specs=[HBM_SPEC] * (2 * n) + [ANY_SPEC] * len(extra),
        out_specs=(SEM_SPEC, SEM_SPEC, *([HBM_SPEC] * (2 * n)), pl.BlockSpec(memory_space=pltpu.VMEM)),
        input_output_aliases={i: 2 + i for i in range(2 * n)},
        compiler_params=pltpu.CompilerParams(has_side_effects=SIDE_EFFECT),
    )(*[_hbm(s) for s in srcs], *[_hbm(t) for t in lands], *extra)
    return res[0], res[1], list(res[2:2 + n]), list(res[2 + n:2 + 2 * n]), res[-1]


def _exchange_wait(name, send_sems, recv_sems, srcs, lands, places, scatter, after):
    n = len(srcs)

    def body(*refs):
        src, land = refs[:n], refs[n:2 * n]
        send, recv = refs[2 * n], refs[2 * n + 1]
        place = _mesh_place()
        for a in range(n):
            for kk in range(1, N_DEV):
                dev, peer = _peer(place, kk)
                cp = pltpu.make_async_remote_copy(
                    src_ref=src[a].at[peer] if scatter else src[a],
                    dst_ref=land[a].at[kk - 1] if scatter else land[a].at[peer],
                    send_sem=send.at[places[a] * N_PEERS + kk - 1], recv_sem=recv.at[places[a] * N_PEERS + kk - 1],
                    device_id=dev, device_id_type=MESH)
                cp.wait_send()
                cp.wait_recv()

    thru = [pltpu.HBM(t.shape, t.dtype) for t in (*srcs, *lands)]
    res = pl.pallas_call(
        body, name=name, out_shape=tuple(thru),
        in_specs=[HBM_SPEC] * (2 * n) + [SEM_SPEC, SEM_SPEC] + [ANY_SPEC] * len(after),
        out_specs=tuple([HBM_SPEC] * (2 * n)), input_output_aliases={i: i for i in range(2 * n)},
        compiler_params=pltpu.CompilerParams(has_side_effects=SIDE_EFFECT),
    )(*srcs, *lands, send_sems, recv_sems, *after)
    return list(res[:n]), list(res[n:])


def _gather_relay(name, send1, recv1, lands, places, after):
    n = len(lands)

    def body(*refs):
        land, s1, r1 = refs[:n], refs[n], refs[n + 1]
        s2, r2 = refs[n + 3], refs[n + 4]
        place = _mesh_place()
        sibling = _peer(place, SIBLING)[0]
        for a in range(n):
            for j, kk in enumerate(CHIP_PEERS):
                dev, origin = _peer(place, kk)
                block = land[a].at[origin]
                pltpu.make_async_remote_copy(
                    src_ref=block, dst_ref=block, send_sem=s1.at[places[a] * N_PEERS + kk - 1],
                    recv_sem=r1.at[places[a] * N_PEERS + kk - 1], device_id=dev, device_id_type=MESH).wait_recv()
                pltpu.make_async_remote_copy(
                    src_ref=block, dst_ref=block, send_sem=s2.at[a * 3 + j], recv_sem=r2.at[a * 3 + j],
                    device_id=sibling, device_id_type=MESH).start()

    res = pl.pallas_call(
        body, name=name,
        out_shape=(pltpu.SemaphoreType.DMA((3 * n,)), pltpu.SemaphoreType.DMA((3 * n,)),
                   *[pltpu.HBM(t.shape, t.dtype) for t in lands]),
        in_specs=[HBM_SPEC] * n + [SEM_SPEC, SEM_SPEC, ANY_SPEC],
        out_specs=(SEM_SPEC, SEM_SPEC, *([HBM_SPEC] * n)),
        input_output_aliases={i: 2 + i for i in range(n)},
        compiler_params=pltpu.CompilerParams(has_side_effects=SIDE_EFFECT),
    )(*lands, send1, recv1, after)
    return res[0], res[1], list(res[2:])


def _gather_wait(name, send1, recv1, send2, recv2, srcs, lands, places, after):
    n = len(lands)

    def body(*refs):
        src, land = refs[:n], refs[n:2 * n]
        s1, r1, s2, r2 = refs[2 * n:2 * n + 4]
        place = _mesh_place()
        for a in range(n):
            for kk in (SIBLING,) + CHIP_PEERS:
                dev, origin = _peer(place, kk)
                first = pltpu.make_async_remote_copy(
                    src_ref=src[a], dst_ref=land[a].at[origin], send_sem=s1.at[places[a] * N_PEERS + kk - 1],
                    recv_sem=r1.at[places[a] * N_PEERS + kk - 1], device_id=dev, device_id_type=MESH)
                first.wait_send()
                if kk == SIBLING:
                    first.wait_recv()
            for j, kk in enumerate(CHIP_PEERS):
                dev, origin = _peer(place, kk + 1)
                relay = pltpu.make_async_remote_copy(
                    src_ref=src[a], dst_ref=land[a].at[origin], send_sem=s2.at[a * 3 + j], recv_sem=r2.at[a * 3 + j],
                    device_id=dev, device_id_type=MESH)
                relay.wait_send()
                relay.wait_recv()

    arrays = (*srcs, *lands)
    res = pl.pallas_call(
        body, name=name, out_shape=tuple(pltpu.HBM(t.shape, t.dtype) for t in arrays),
        in_specs=[HBM_SPEC] * (2 * n) + [SEM_SPEC] * 4 + [ANY_SPEC], out_specs=tuple([HBM_SPEC] * (2 * n)),
        input_output_aliases={i: i for i in range(2 * n)},
        compiler_params=pltpu.CompilerParams(has_side_effects=SIDE_EFFECT),
    )(*arrays, send1, recv1, send2, recv2, after)
    return list(res[n:])


class _Feed:
    def __init__(self, shards, groups, me):
        self.shards, self.groups, self.me, self.pos = shards, groups, me, 0
        self.sems, self.srcs, self.lands = {}, {}, {}
        self.relays = {}
        self.pending = []

    def start(self, tag, names, after):
        srcs = [self.shards[nm] for nm in names]
        lands = [_landing(s, self.me) for s in srcs]
        send, recv, srcs, lands, self.token = _exchange_start(
            f"gather_start_{tag}", srcs, lands, False, after, (SIBLING,) + CHIP_PEERS)
        for i, nm in enumerate(names):
            self.sems[nm], self.srcs[nm], self.lands[nm] = (send, recv, i), srcs[i], lands[i]
        return self.token

    def _relay(self, gi, after):
        names = self.groups[gi]
        if gi not in self.relays:
            send, recv, _ = self.sems[names[0]]
            places = [self.sems[nm][2] for nm in names]
            send2, recv2, lands = _gather_relay(f"gather_relay_{gi}", send, recv, [self.lands[nm] for nm in names],
                                                places, after)
            for nm, t in zip(names, lands):
                self.lands[nm] = t
            self.relays[gi] = (send2, recv2)
            after = lands[0]
        return after

    def weights(self, tag, names, after):
        gi = self.pos
        assert names == self.groups[gi], (names, self.groups[gi])
        if gi == 0:
            after = self.token
        self._relay(gi, after)
        if 1 <= gi < len(self.groups) - 1:
            after = self._relay(gi + 1, after)
        send2, recv2 = self.relays[gi]
        send, recv, _ = self.sems[names[0]]
        got = _gather_wait(f"gather_wait_{tag}", send, recv, send2, recv2, [self.srcs[nm] for nm in names],
                           [self.lands[nm] for nm in names], [self.sems[nm][2] for nm in names], after)
        self.pos += 1
        return [t.reshape((N_DEV * t.shape[1],) + t.shape[2:]) for t in got]

    def grads(self, tag, full):
        names = list(full)
        srcs = [full[nm].reshape((N_DEV, full[nm].shape[0] // N_DEV) + full[nm].shape[1:]) for nm in names]
        lands = [lax.empty((N_PEERS,) + s.shape[1:], s.dtype) for s in srcs]
        send, recv, srcs, lands, token = _exchange_start(f"scatter_start_{tag}", srcs, lands, True, None)
        self.pending.append((tag, names, send, recv, srcs, lands))
        return token[0, 0]

    def collect(self, tags, after, keep_slots=()):
        out = {}
        for tag, names, send, recv, srcs, lands in self.pending:
            if tag not in tags:
                continue
            srcs, got = _exchange_wait(f"scatter_wait_{tag}", send, recv, srcs, lands, list(range(len(names))), True,
                                       after)
            for nm, slots, src in zip(names, got, srcs):
                out[nm] = ((slots, src) if nm.startswith(tuple(keep_slots))
                           else _sum_slots(f"reduce_{nm}", slots, src, self.me))
        return out


def _adamw_math(w, gg, m, v):
    nm = ADAM_B1 * m + (1.0 - ADAM_B1) * gg
    nv = ADAM_B2 * v + (1.0 - ADAM_B2) * (gg * gg)
    bc1 = 1.0 - ADAM_B1 ** ADAM_STEP
    bc2 = 1.0 - ADAM_B2 ** ADAM_STEP
    return -ADAM_LR * ((nm / bc1) / (jnp.sqrt(nv / bc2) + ADAM_EPS) + ADAM_WD * w), nm, nv


def _adamw_part(name, i, w, scattered, me, m, v, prev):
    n_parts, rows, cols = w.shape
    tr = _tile(rows, 256, 16)
    if prev is None:
        prev = tuple(lax.empty(w.shape, F32) for _ in range(4))

    slots, src = scattered

    def body(me_ref, w_ref, g_ref, own_ref, m_ref, v_ref, *rest):
        go_ref, d_ref, nm_ref, nv_ref = rest[4:]
        gg = own_ref[...].astype(F32)
        for sl in range(N_PEERS):
            gg = gg + g_ref[sl].astype(F32)
        d, nm, nv = _adamw_math(w_ref[...], gg, m_ref[...], v_ref[...])
        go_ref[...] = gg
        d_ref[...] = d
        nm_ref[...] = nm
        nv_ref[...] = nv

    part = pl.BlockSpec((None, tr, cols), lambda r, me_ref: (i, r, 0))
    grid_spec = pltpu.PrefetchScalarGridSpec(
        num_scalar_prefetch=1, grid=(rows // tr,),
        in_specs=[part, pl.BlockSpec((N_PEERS, tr, cols), lambda r, me_ref: (0, r, 0)),
                  pl.BlockSpec((None, tr, cols), lambda r, me_ref: (me_ref[0], r, 0)), part, part] + [ANY_SPEC] * 4,
        out_specs=[part] * 4)
    return pl.pallas_call(
        body, name=name, grid_spec=grid_spec, out_shape=[SDS(w.shape, F32)] * 4,
        input_output_aliases={6 + k: k for k in range(4)}, compiler_params=_cparams(),
    )(_me_operand(me), w, slots, src, m, v, *prev)


WEIGHT_NAMES = ("c_ctx", "norm_g", "w_mod", "b_mod", "ffn_w_gate", "ffn_w_up", "ffn_w_down", "ab_w_in", "pool_w",
                "pool_scale", "q_norm_g", "w_uq", "kv_norm_g", "w_ukv", "ab_w_out", "conv_w_in", "conv_w",
                "conv_w_out", "final_norm_g")


def kernel(x, c, ctx, c_ctx, norm_g, w_mod, b_mod, ffn_w_gate, ffn_w_up, ffn_w_down, ab_w_in, pool_w, pool_scale, q_norm_g, w_uq, kv_norm_g, w_ukv, ab_w_out, conv_w_in, conv_w, conv_w_out, final_norm_g, loss_target, m_c_ctx, m_norm_g, m_w_mod, m_b_mod, m_ffn_w_gate, m_ffn_w_up, m_ffn_w_down, m_ab_w_in, m_pool_w, m_pool_scale, m_q_norm_g, m_w_uq, m_kv_norm_g, m_w_ukv, m_ab_w_out, m_conv_w_in, m_conv_w, m_conv_w_out, m_final_norm_g, v_c_ctx, v_norm_g, v_w_mod, v_b_mod, v_ffn_w_gate, v_ffn_w_up, v_ffn_w_down, v_ab_w_in, v_pool_w, v_pool_scale, v_q_norm_g, v_w_uq, v_kv_norm_g, v_w_ukv, v_ab_w_out, v_conv_w_in, v_conv_w, v_conv_w_out, v_final_norm_g):
    weights = (c_ctx, norm_g, w_mod, b_mod, ffn_w_gate, ffn_w_up, ffn_w_down, ab_w_in, pool_w, pool_scale, q_norm_g,
               w_uq, kv_norm_g, w_ukv, ab_w_out, conv_w_in, conv_w, conv_w_out, final_norm_g)
    moms = (m_c_ctx, m_norm_g, m_w_mod, m_b_mod, m_ffn_w_gate, m_ffn_w_up, m_ffn_w_down, m_ab_w_in, m_pool_w,
            m_pool_scale, m_q_norm_g, m_w_uq, m_kv_norm_g, m_w_ukv, m_ab_w_out, m_conv_w_in, m_conv_w, m_conv_w_out,
            m_final_norm_g)
    vels = (v_c_ctx, v_norm_g, v_w_mod, v_b_mod, v_ffn_w_gate, v_ffn_w_up, v_ffn_w_down, v_ab_w_in, v_pool_w,
            v_pool_scale, v_q_norm_g, v_w_uq, v_kv_norm_g, v_w_ukv, v_ab_w_out, v_conv_w_in, v_conv_w, v_conv_w_out,
            v_final_norm_g)
    me = 4 * lax.axis_index("x") + 2 * lax.axis_index("y") + lax.axis_index("c")
    n_lat, n_ctx = x.shape[1], ctx.shape[1]
    d = D_MODEL
    mod_cols = w_mod.shape[-1]
    ng_sh, cw_sh = norm_g.shape[-1], conv_w.shape[-1]

    def ffn_shards(i):
        return {f"gate_t{i}": ffn_w_gate[i // 2, i % 2].T, f"up_t{i}": ffn_w_up[i // 2, i % 2].T,
                f"down{i}": ffn_w_down[i // 2, i % 2]}

    local = {**ffn_shards(0), "in_t": ab_w_in[0].T, "uq": w_uq[0], "ukv_t": w_ukv[0].T, "ab_out": ab_w_out[0],
             **ffn_shards(1), **ffn_shards(2), "cin_t": conv_w_in[0].T, "c_out": conv_w_out[0], **ffn_shards(3)}
    ffn_groups = [[[f"gate_t{i}", f"up_t{i}"], [f"down{i}"]] for i in range(4)]
    groups = [*ffn_groups[0], ["in_t"], ["uq", "ukv_t", "ab_out"], *ffn_groups[1], *ffn_groups[2],
              ["cin_t", "c_out"], *ffn_groups[3]]
    feed = _Feed({nm: a.astype(BF16) for nm, a in local.items()}, groups, me)

    small = jnp.concatenate([c.reshape(-1), norm_g.reshape(-1), conv_w.reshape(-1)])
    small_n = -(-small.shape[0] // 1024) * 1024
    small = jnp.pad(small, (0, small_n - small.shape[0])).reshape(small_n // 128, 128)
    small_all = _exchange("gather_small", small, False).reshape(N_DEV, small_n)
    c_all = small_all[:, :d]
    o1 = d + 6 * ng_sh
    norm_g_full = small_all[:, d:o1].reshape(N_DEV, 2, 3, ng_sh).transpose(1, 2, 0, 3).reshape(2, 3, d)
    conv_w_full = small_all[:, o1:o1 + 3 * cw_sh].reshape(N_DEV, 3, cw_sh).transpose(1, 0, 2).reshape(3, d)

    cond = jnp.concatenate([c_all, jnp.broadcast_to(c_ctx[None, :], (N_DEV, d))], axis=0)
    sil, dsil = _silu_rows("mod_silu", cond)
    w_mod_b = w_mod.astype(BF16)
    b_sh = lax.dynamic_slice(b_mod, (0, me * mod_cols), (2, mod_cols))
    m_part = jnp.stack([_mm(f"mod_fwd{l}", [(sil, w_mod_b[l])], "nn", F32, 16, 384, bias=b_sh[l:l + 1])
                        for l in range(2)], axis=1)
    m_all = _exchange("gather_mod", m_part.reshape(-1, 128), False).reshape(N_DEV, 2 * N_DEV, 2, mod_cols)
    m_mine = lax.dynamic_index_in_dim(m_all, me, axis=1, keepdims=False)
    mod_h = m_mine.transpose(1, 0, 2).reshape(2, N_MOD, d)
    mod_g = m_all[:, N_DEV, 0, :].reshape(N_MOD, d)

    first = feed.start("first", [nm for grp in groups[:3] for nm in grp], m_all)
    feed.start("rest", [nm for grp in groups[3:] for nm in grp], first)

    sq_cols, grad_x, g = _local_step(x[0], ctx[0], loss_target[0], mod_h, mod_g, norm_g_full, feed, pool_w[0],
                                  pool_scale, q_norm_g, kv_norm_g, conv_w_full, final_norm_g)
    w_of, m_of, v_of = (dict(zip(WEIGHT_NAMES, t)) for t in (weights, moms, vels))
    results = {}

    def update(nm, grad, view=lambda t: t):
        outs = _adamw(f"adamw_{nm}", view(w_of[nm]), grad.reshape(view(w_of[nm]).shape), view(m_of[nm]), view(v_of[nm]))
        results[nm] = tuple(view(t) for t in (grad.reshape(view(w_of[nm]).shape), *outs))

    def swap(t):
        return jnp.swapaxes(t, -1, -2)

    stacked = ("gate_t", "up_t", "down")
    early = feed.collect(["l1f1", "l1m", "l1f0", "l0f1", "l0m"], [grad_x], stacked)
    update("ab_w_in", early["in_t"], swap)
    update("w_uq", early["uq"])
    update("w_ukv", early["ukv_t"].T)
    update("ab_w_out", early["ab_out"])
    update("conv_w_in", early["cin_t"].T)
    update("conv_w_out", early["c_out"])
    ffn = {}
    for nm, prefix, view in (("ffn_w_gate", "gate_t", swap), ("ffn_w_up", "up_t", swap),
                             ("ffn_w_down", "down", lambda t: t)):
        w4, m4, v4 = (view(t).reshape((4,) + view(t).shape[-2:]) for t in (w_of[nm], m_of[nm], v_of[nm]))
        prev = None
        for i in (3, 2, 1):
            prev = _adamw_part(f"adamw_{nm}{i}", i, w4, early[f"{prefix}{i}"], me, m4, v4, prev)
        ffn[nm] = (prefix, view, w4, m4, v4, prev)
    done_early = [results[nm][1] for nm in results] + [state[5][1] for state in ffn.values()]
    late = feed.collect(["l0f0"], done_early, stacked)
    for nm, (prefix, view, w4, m4, v4, prev) in ffn.items():
        outs = _adamw_part(f"adamw_{nm}0", 0, w4, late[f"{prefix}0"], me, m4, v4, prev)
        results[nm] = tuple(view(t.reshape(view(w_of[nm]).shape)) for t in outs)

    dm = jnp.stack([g["mod_h"], jnp.stack([g["mod_g"], jnp.zeros_like(g["mod_g"])])])
    dm_all = _exchange("gather_dmod", dm.reshape(-1, 128), False, results["ffn_w_down"][1]).reshape(N_DEV, 2, 2, N_MOD * d)
    grad_b_mod = _sum_rows("dmod_bias", dm_all.reshape(2 * N_DEV, 2 * N_MOD * d)).reshape(2, N_MOD * d)
    dm_sh = lax.dynamic_slice(dm_all, (0, 0, 0, me * mod_cols), (N_DEV, 2, 2, mod_cols))
    gw_mod, cctx_parts = [], []
    for l in range(2):
        dm_l = dm_sh[:, :, l, :].transpose(1, 0, 2).reshape(2 * N_DEV, mod_cols).astype(BF16)
        gw_mod.append(_mm(f"mod_dw{l}", [(sil, dm_l)], "tn", F32, 512, 384))
        dm_ctx = jnp.concatenate([dm_l[N_DEV:], jnp.zeros((N_DEV, mod_cols), BF16)], axis=0)
        cctx_parts.append(_mm(f"mod_dcond{l}", [(dm_ctx, w_mod_b[l])], "nt", F32, 16, 512))
    cctx_part = _sum_rows("mod_dcond_sum", jnp.concatenate(cctx_parts, axis=0))
    update("w_mod", jnp.stack(gw_mod))
    update("b_mod", grad_b_mod)

    small_g = jnp.concatenate([g["pool_w"].reshape(-1), g["pool_scale"].reshape(-1), g["q_norm_g"].reshape(-1),
                               g["kv_norm_g"].reshape(-1), g["final_norm_g"].reshape(-1), g["norm_g"].reshape(-1),
                               g["conv_w"].reshape(-1), sq_cols.reshape(-1), cctx_part.reshape(-1)])
    sizes = [pool_w.size, pool_scale.size, q_norm_g.size, kv_norm_g.size, d, 6 * d, 3 * d, d, d]
    sg_n = -(-small_g.shape[0] // 1024) * 1024
    small_g = jnp.pad(small_g, (0, sg_n - small_g.shape[0]))
    sg_all = _exchange("gather_small_grads", small_g.reshape(-1, 128), False).reshape(N_DEV, sg_n)
    scale_vec = jnp.concatenate([jnp.ones((1, sum(sizes[:-1])), F32), dsil[N_DEV:N_DEV + 1],
                                 jnp.ones((1, sg_n - sum(sizes)), F32)], axis=1)
    sg = _sum_rows("small_grads_sum", sg_all, scale_vec)[0]
    cuts, pos = [], 0
    for sz in sizes:
        cuts.append(sg[pos:pos + sz])
        pos += sz
    g_pool_w, g_pool_scale, g_q_norm, g_kv_norm, g_final, g_norm_full, g_conv_full, sq_all, g_c_ctx = cuts
    loss = 0.5 * jnp.sum(sq_all) / d
    update("c_ctx", g_c_ctx)
    update("norm_g", lax.dynamic_slice(g_norm_full.reshape(2, 3, d), (0, 0, me * ng_sh), (2, 3, ng_sh)))
    update("conv_w", lax.dynamic_slice(g_conv_full.reshape(3, d), (0, me * cw_sh), (3, cw_sh)))
    update("pool_w", g_pool_w)
    update("pool_scale", g_pool_scale)
    update("q_norm_g", g_q_norm)
    update("kv_norm_g", g_kv_norm)
    update("final_norm_g", g_final)
    outs = [results[nm] for nm in WEIGHT_NAMES]
    return (loss, grad_x[None], *[o[0] for o in outs], *[o[1] for o in outs], *[o[2] for o in outs],
            *[o[3] for o in outs])
```

```python
import functools
import math

import jax
import jax.numpy as jnp
import numpy as np
from jax import lax
from jax.experimental import pallas as pl
from jax.experimental.pallas import tpu as pltpu

F32 = jnp.float32
BF16 = jnp.bfloat16
MESH = pl.DeviceIdType.MESH
SDS = jax.ShapeDtypeStruct

N_DEV = 8
D_MODEL = 1024
N_MOD = 9
D_FF = 2816
POOL_WINDOWS = (2, 4, 8, 16)
POOL_DIM = 512
POOL_GROUP_DIM = 128
HEADS = 8
QK_NOPE = 64
QK_ROPE = 32
QK_HEAD = QK_NOPE + QK_ROPE
V_HEAD = 64
Q_RANK = 768
KV_RANK = 256
GRID_W = 64
ROPE_THETA = 10000.0
RMS_EPS = 1e-6
ATTN_SCALE = 1.0 / math.sqrt(QK_HEAD)
HEAD_PAD = 128
POOL_PAD = 16
ATTN_Q_ROWS_FWD = 128
ATTN_Q_ROWS_BWD = 1024
PA_POOL, PA_CQ, PA_KV = 0, 768, 1536
PA_KV_W = 384
PA_W = PA_KV + PA_KV_W

ADAM_LR, ADAM_B1, ADAM_B2, ADAM_EPS, ADAM_WD, ADAM_STEP = 0.001, 0.9, 0.999, 1e-08, 0.01, 10

VMEM_LIMIT_BYTES = 56 * 1024 * 1024

NN = ((1,), (0,))
NT = ((1,), (1,))
TN = ((0,), (0,))


def _cparams():
    return pltpu.CompilerParams(vmem_limit_bytes=VMEM_LIMIT_BYTES)


def _dot(a, b, dims):
    return lax.dot_general(a, b, (dims, ((), ())), preferred_element_type=F32)


def _tile(n, cap, mult=8):
    t = (min(cap, n) // mult) * mult
    while t >= mult:
        if n % t == 0:
            return t
        t -= mult
    return n


def _colsum(x):
    return jnp.sum(x, axis=0, keepdims=True)


def _rms(x):
    r = lax.rsqrt(jnp.mean(x * x, axis=-1, keepdims=True) + RMS_EPS)
    return x * r, r


def _rms_bwd(n, r, dn):
    return r * (dn - n * jnp.mean(dn * n, axis=-1, keepdims=True))


def _rowwise(name, fn, t_rows, tm, n_lat, rows, vecs, outs, accs, into=None):
    nt = t_rows // tm
    nlt = n_lat // tm
    n_groups = 2 if nlt < nt else 1

    def grp(i):
        return jnp.where(i >= nlt, 1, 0) if n_groups == 2 else 0

    in_specs = [pl.BlockSpec((tm, w), functools.partial(lambda i, cb: (i, cb), cb=cb)) for (_, w, cb) in rows]
    in_specs += [pl.BlockSpec((1,) + v.shape[1:], lambda i: (grp(i), 0, 0)) for v in vecs]
    out_specs = [pl.BlockSpec((tm, w), lambda i: (i, 0)) for (w, _) in outs]
    out_specs += [pl.BlockSpec((1, 1, w), lambda i: (grp(i), 0, 0)) for w in accs]
    out_shape = [SDS((t_rows, w), dt) for (w, dt) in outs] + [SDS((n_groups, 1, w), F32) for w in accs]
    n_r, n_v, n_o = len(rows), len(vecs), len(outs)
    extra, aliases = [], {}
    if into is not None:
        extra, aliases = [into[0]], {n_r + n_v: 0}
        in_specs.append(pl.BlockSpec(memory_space=pl.ANY))
        out_specs[0] = pl.BlockSpec((tm, outs[0][0]), lambda i: (i, into[1]))
        out_shape[0] = SDS(into[0].shape, into[0].dtype)
    n_in = n_r + n_v + len(extra)

    def body(*refs):
        row_vals = [r[...] for r in refs[:n_r]]
        vec_vals = [v[0] for v in refs[n_r:n_r + n_v]]
        out_refs = refs[n_in:n_in + n_o]
        acc_refs = refs[n_in + n_o:]
        out_vals, acc_vals = fn(row_vals, vec_vals)
        for o_ref, o in zip(out_refs, out_vals):
            o_ref[...] = o.astype(o_ref.dtype)
        if acc_refs:
            i = pl.program_id(0)
            first = (i == 0) | (i == nlt) if n_groups == 2 else i == 0

            @pl.when(first)
            def _():
                for a_ref, a in zip(acc_refs, acc_vals):
                    a_ref[0] = a

            @pl.when(jnp.logical_not(first))
            def _():
                for a_ref, a in zip(acc_refs, acc_vals):
                    a_ref[0] += a

    res = pl.pallas_call(
        body, name=name, grid=(nt,), in_specs=in_specs, out_specs=out_specs, out_shape=out_shape,
        input_output_aliases=aliases, compiler_params=_cparams(),
    )(*[r[0] for r in rows], *vecs, *extra)
    return res[:n_o], res[n_o:]


RESIDENT_BYTES = 12 * 1024 * 1024


def _mm(name, pairs, mode, out_dtype, tm_cap=256, tn_cap=512, bias=None):
    a0, b0 = pairs[0]
    if mode == "nn":
        m, n, dims = a0.shape[0], b0.shape[1], NN
    elif mode == "nt":
        m, n, dims = a0.shape[0], b0.shape[0], NT
    else:
        m, n, dims = a0.shape[1], b0.shape[1], TN
    b_bytes = sum(b.size * b.dtype.itemsize for _, b in pairs)
    tn = n if b_bytes <= RESIDENT_BYTES else _tile(n, tn_cap, 128)
    tm = _tile(m, tm_cap, 128 if mode == "tn" else 16)

    def a_spec(a):
        if mode == "tn":
            return pl.BlockSpec((a.shape[0], tm), lambda i, j: (0, i))
        return pl.BlockSpec((tm, a.shape[1]), lambda i, j: (i, 0))

    def b_spec(b):
        if mode == "nt":
            return pl.BlockSpec((tn, b.shape[1]), lambda i, j: (j, 0))
        return pl.BlockSpec((b.shape[0], tn), lambda i, j: (0, j))

    in_specs, flat = [], []
    for a, b in pairs:
        in_specs += [a_spec(a), b_spec(b)]
        flat += [a, b]
    if bias is not None:
        in_specs.append(pl.BlockSpec((1, tn), lambda i, j: (0, j)))
        flat.append(bias)
    n_pairs = len(pairs)

    def body(*refs):
        acc = None
        for p in range(n_pairs):
            t = _dot(refs[2 * p][...], refs[2 * p + 1][...], dims)
            acc = t if acc is None else acc + t
        if bias is not None:
            acc = acc + refs[2 * n_pairs][...]
        refs[-1][...] = acc.astype(refs[-1].dtype)

    return pl.pallas_call(
        body, name=name, grid=(m // tm, n // tn), in_specs=in_specs,
        out_specs=pl.BlockSpec((tm, tn), lambda i, j: (i, j)),
        out_shape=SDS((m, n), out_dtype), compiler_params=_cparams(),
    )(*flat)


def _mm_resid(name, pairs, s, mg, k, coef, n_lat):
    t_rows, n = pairs[0][0].shape[0], s.shape[1]
    n_pairs = len(pairs)
    tm = _tile(math.gcd(n_lat, t_rows), 256, 16)
    nlt = n_lat // tm
    n_groups = 2 if nlt < t_rows // tm else 1

    def grp(i):
        return jnp.where(i >= nlt, 1, 0) if n_groups == 2 else 0

    def body(*refs):
        s_ref, mg_ref, so_ref, o_ref = refs[2 * n_pairs:]
        o = _dot(refs[0][...], refs[n_pairs][...], NN)
        for p in range(1, n_pairs):
            o = o + _dot(refs[p][...], refs[n_pairs + p][...], NN)
        gate = mg_ref[0, 3 * k + 2:3 * k + 3, :]
        o_ref[...] = o.astype(BF16)
        so_ref[...] = s_ref[...] + (coef * gate) * o

    row = pl.BlockSpec((tm, n), lambda i: (i, 0))
    return pl.pallas_call(
        body, name=name, grid=(t_rows // tm,),
        in_specs=[pl.BlockSpec((tm, a.shape[1]), lambda i: (i, 0)) for a, _ in pairs]
        + [pl.BlockSpec(b.shape, lambda i: (0, 0)) for _, b in pairs]
        + [row, pl.BlockSpec((1, mg.shape[1], n), lambda i: (grp(i), 0, 0))],
        out_specs=[row, row], out_shape=[SDS((t_rows, n), F32), SDS((t_rows, n), BF16)], compiler_params=_cparams(),
    )(*[a for a, _ in pairs], *[b for _, b in pairs], s, mg)


def _dw_pair(name, a1, a2, b):
    kk, m = a1.shape
    n = b.shape[1]
    tm = _tile(m, 256, 128)

    def body(a1_ref, a2_ref, b_ref, o1_ref, o2_ref):
        bb = b_ref[...]
        o1_ref[...] = _dot(a1_ref[...], bb, TN).astype(BF16)
        o2_ref[...] = _dot(a2_ref[...], bb, TN).astype(BF16)

    col = pl.BlockSpec((kk, tm), lambda i: (0, i))
    out = pl.BlockSpec((tm, n), lambda i: (i, 0))
    return pl.pallas_call(
        body, name=name, grid=(m // tm,), in_specs=[col, col, pl.BlockSpec(b.shape, lambda i: (0, 0))],
        out_specs=[out, out], out_shape=[SDS((m, n), BF16)] * 2, compiler_params=_cparams(),
    )(a1, a2, b)


def _groups(t_rows, tm, n_lat):
    nlt = n_lat // tm
    if nlt < t_rows // tm:
        return 2, (lambda i: jnp.where(i >= nlt, 1, 0)), (lambda i: (i == 0) | (i == nlt))
    return 1, (lambda i: 0), (lambda i: i == 0)


def _accumulate(acc_refs, vals, first):
    @pl.when(first)
    def _():
        for r, v in zip(acc_refs, vals):
            r[0] = v

    @pl.when(jnp.logical_not(first))
    def _():
        for r, v in zip(acc_refs, vals):
            r[0] += v


def _adaln_math(s, m, k):
    n, _ = _rms(s)
    return (n * m[9 + k:10 + k]) * (1.0 + m[3 * k + 1:3 * k + 2]) + m[3 * k:3 * k + 1]


def _ffn_up(name, s, mg, k, n_lat, wg_t, wu_t):
    t_rows, f = s.shape[0], wg_t.shape[0]
    tm = _row_tm(t_rows, n_lat)
    _, grp, _ = _groups(t_rows, tm, n_lat)

    def body(s_ref, mg_ref, wg_ref, wu_ref, u_ref, a_ref, b_ref, h_ref):
        uu = _adaln_math(s_ref[...], mg_ref[0], k).astype(BF16)
        u_ref[...] = uu
        a = _dot(uu, wg_ref[...], NT)
        b = _dot(uu, wu_ref[...], NT)
        sg = jax.nn.sigmoid(a)
        act = a * sg
        a_ref[...] = (b * (sg * (1.0 + a * (1.0 - sg)))).astype(BF16)
        b_ref[...] = act.astype(BF16)
        h_ref[...] = (act * b).astype(BF16)

    w_spec = pl.BlockSpec(wg_t.shape, lambda i: (0, 0))
    o_spec = pl.BlockSpec((tm, f), lambda i: (i, 0))
    row = pl.BlockSpec((tm, s.shape[1]), lambda i: (i, 0))
    return pl.pallas_call(
        body, name=name, grid=(t_rows // tm,),
        in_specs=[row, pl.BlockSpec((1,) + mg.shape[1:], lambda i: (grp(i), 0, 0)), w_spec, w_spec],
        out_specs=[row, o_spec, o_spec, o_spec],
        out_shape=[SDS(s.shape, BF16)] + [SDS((t_rows, f), BF16)] * 3, compiler_params=_cparams(),
    )(s, mg, wg_t, wu_t)


def _ffn_dact(name, ds_out, o, mg, k, coef, n_lat, wd, a, b):
    t_rows, f = ds_out.shape[0], wd.shape[0]
    tm = _row_tm(t_rows, n_lat)
    n_groups, grp, first = _groups(t_rows, tm, n_lat)
    d = ds_out.shape[1]

    def body(ds_ref, o_ref, mg_ref, wd_ref, a_ref, b_ref, do_ref, da_ref, db_ref, dg_ref):
        dd = coef * ds_ref[...]
        do = (dd * mg_ref[0, 3 * k + 2:3 * k + 3, :]).astype(BF16)
        do_ref[...] = do
        _accumulate([dg_ref], [_colsum(dd * o_ref[...].astype(F32))], first(pl.program_id(0)))
        dh = _dot(do, wd_ref[...], NT)
        da_ref[...] = (dh * a_ref[...].astype(F32)).astype(BF16)
        db_ref[...] = (dh * b_ref[...].astype(F32)).astype(BF16)

    row = pl.BlockSpec((tm, d), lambda i: (i, 0))
    t_spec = pl.BlockSpec((tm, f), lambda i: (i, 0))
    return pl.pallas_call(
        body, name=name, grid=(t_rows // tm,),
        in_specs=[row, row, pl.BlockSpec((1,) + mg.shape[1:], lambda i: (grp(i), 0, 0)),
                  pl.BlockSpec(wd.shape, lambda i: (0, 0)), t_spec, t_spec],
        out_specs=[row, t_spec, t_spec, pl.BlockSpec((1, 1, d), lambda i: (grp(i), 0, 0))],
        out_shape=[SDS((t_rows, d), BF16), SDS((t_rows, f), BF16), SDS((t_rows, f), BF16), SDS((n_groups, 1, d), F32)],
        compiler_params=_cparams(),
    )(ds_out, o, mg, wd, a, b)


def _du_adaln(name, pairs, s, ds_out, mg, k, n_lat, after, out_rows=None):
    t_rows, d = s.shape
    tm = _row_tm(t_rows, n_lat)
    n_groups, grp, first = _groups(t_rows, tm, n_lat)
    n_pairs = len(pairs)
    nt, n_ds, n_out = t_rows // tm, ds_out.shape[0] // tm, (out_rows or t_rows) // tm

    def body(*refs):
        s_ref, ds_ref, mg_ref, z_ref, out_ref, dsh_ref, dsc_ref, dgn_ref = refs[2 * n_pairs:]
        i = pl.program_id(0)
        d_u = z_ref[...]
        for p in range(n_pairs):
            d_u = d_u + _dot(refs[p][...], refs[n_pairs + p][...], NN)
        m = mg_ref[0]
        gain, scale = m[9 + k:10 + k], m[3 * k + 1:3 * k + 2]
        n, r = _rms(s_ref[...])
        dxn = d_u * (1.0 + scale)
        ds_in = _rms_bwd(n, r, dxn * gain)
        ds_in = ds_in + (ds_ref[...] if n_ds == nt else jnp.where(i < n_ds, ds_ref[...], 0.0))
        if n_out == nt:
            out_ref[...] = ds_in
        else:
            @pl.when(i < n_out)
            def _():
                out_ref[...] = ds_in
        _accumulate([dsh_ref, dsc_ref, dgn_ref], [_colsum(d_u), _colsum(d_u * (n * gain)), _colsum(dxn * n)], first(i))

    row = pl.BlockSpec((tm, d), lambda i: (i, 0))
    acc = pl.BlockSpec((1, 1, d), lambda i: (grp(i), 0, 0))
    res = pl.pallas_call(
        body, name=name, grid=(t_rows // tm,),
        in_specs=[pl.BlockSpec((tm, a.shape[1]), lambda i: (i, 0)) for a, _ in pairs]
        + [pl.BlockSpec(w.shape, lambda i: (0, 0)) for _, w in pairs]
        + [row, pl.BlockSpec((tm, d), lambda i: (jnp.minimum(i, n_ds - 1), 0)),
           pl.BlockSpec((1,) + mg.shape[1:], lambda i: (grp(i), 0, 0)), pl.BlockSpec((1, d), lambda i: (0, 0))],
        out_specs=[pl.BlockSpec((tm, d), lambda i: (jnp.minimum(i, n_out - 1), 0)), acc, acc, acc],
        out_shape=[SDS((n_out * tm, d), F32)] + [SDS((n_groups, 1, d), F32)] * 3, compiler_params=_cparams(),
    )(*[a for a, _ in pairs], *[w for _, w in pairs], s, ds_out, mg, after)
    return res[0], res[1:]


def _adaln_mm(name, s, mg, k, n_lat, w_t):
    rows, d = s.shape
    tm = _row_tm(rows, n_lat)
    _, grp, _ = _groups(rows, tm, n_lat)
    n = w_t.shape[0]

    def body(s_ref, mg_ref, w_ref, u_ref, y_ref):
        uu = _adaln_math(s_ref[...], mg_ref[0], k).astype(BF16)
        u_ref[...] = uu
        y_ref[...] = _dot(uu, w_ref[...], NT)

    row = pl.BlockSpec((tm, d), lambda i: (i, 0))
    return pl.pallas_call(
        body, name=name, grid=(rows // tm,),
        in_specs=[row, pl.BlockSpec((1,) + mg.shape[1:], lambda i: (grp(i), 0, 0)), pl.BlockSpec(w_t.shape, lambda i: (0, 0))],
        out_specs=[row, pl.BlockSpec((tm, n), lambda i: (i, 0))],
        out_shape=[SDS((rows, d), BF16), SDS((rows, n), F32)], compiler_params=_cparams(),
    )(s, mg, w_t)


def _gate_mm(name, ds_out, o, mg, k, coef, n_lat, w):
    t_rows, d = ds_out.shape
    tm = _row_tm(t_rows, n_lat)
    n_groups, grp, first = _groups(t_rows, tm, n_lat)
    n = w.shape[0]

    def body(ds_ref, o_ref, mg_ref, w_ref, do_ref, y_ref, dg_ref):
        dd = coef * ds_ref[...]
        do = (dd * mg_ref[0, 3 * k + 2:3 * k + 3, :]).astype(BF16)
        do_ref[...] = do
        _accumulate([dg_ref], [_colsum(dd * o_ref[...].astype(F32))], first(pl.program_id(0)))
        y_ref[...] = _dot(do, w_ref[...], NT)

    row = pl.BlockSpec((tm, d), lambda i: (i, 0))
    return pl.pallas_call(
        body, name=name, grid=(t_rows // tm,),
        in_specs=[row, row, pl.BlockSpec((1,) + mg.shape[1:], lambda i: (grp(i), 0, 0)), pl.BlockSpec(w.shape, lambda i: (0, 0))],
        out_specs=[row, pl.BlockSpec((tm, n), lambda i: (i, 0)), pl.BlockSpec((1, 1, d), lambda i: (grp(i), 0, 0))],
        out_shape=[SDS((t_rows, d), BF16), SDS((t_rows, n), F32), SDS((n_groups, 1, d), F32)],
        compiler_params=_cparams(),
    )(ds_out, o, mg, w)


def _row_tm(t_rows, n_lat):
    return _tile(math.gcd(t_rows, n_lat), 256, 16)


def _rmsnorm_fwd(name, x, width, colblk, gain, t_rows):
    def fn(rv, vv):
        n, _ = _rms(rv[0])
        return [n * vv[0]], []

    (y,), _ = _rowwise(name, fn, t_rows, _tile(t_rows, 256, 16), t_rows, [(x, width, colblk)],
                       [gain.reshape(1, 1, width)], [(width, BF16)], [])
    return y


def _rmsnorm_bwd(name, x, width, colblk, dy, gain, t_rows, out_dtype=F32, into=None):
    def fn(rv, vv):
        n, r = _rms(rv[0])
        return [_rms_bwd(n, r, rv[1] * vv[0])], [_colsum(rv[1] * n)]

    (dx,), (dgain,) = _rowwise(name, fn, t_rows, _tile(t_rows, 256, 16), t_rows,
                               [(x, width, colblk), (dy, width, 0)], [gain.reshape(1, 1, width)],
                               [(width, out_dtype)], [width], into)
    return dx, dgain


def _final_loss(name, h, target, gain):
    t_rows = h.shape[0]
    inv_d = 1.0 / D_MODEL

    def fn(rv, vv):
        g = vv[0]
        n, r = _rms(rv[0])
        e = n * g - rv[1]
        dy = e * inv_d
        return [_rms_bwd(n, r, dy * g)], [_colsum(e * e), _colsum(dy * n)]

    (dh,), (sq, dgain) = _rowwise(name, fn, t_rows, _tile(t_rows, 256, 16), t_rows,
                                  [(h, D_MODEL, 0), (target, D_MODEL, 0)], [gain.reshape(1, 1, D_MODEL)],
                                  [(D_MODEL, F32)], [D_MODEL, D_MODEL])
    return dh, sq, dgain


def _exact_dot(x, m_ref):
    hi = x.astype(BF16)
    lo = (x - hi.astype(F32)).astype(BF16)
    return _dot(hi, m_ref[...], NN) + _dot(lo, m_ref[...], NN)


def _rope(name, z, width, colblk, cos32, sin32, layout, backward, out_dtype, remap=None, into=None):
    t_rows = cos32.shape[0]
    expand, plain, perm = layout
    w_in = remap.shape[1] if (remap is not None and backward) else width
    w_out = remap.shape[1] if (remap is not None and not backward) else width
    extra = [] if remap is None else [remap.T if backward else remap]
    dest = [] if into is None else [into[0]]

    def body(z_ref, c_ref, s_ref, e_ref, m_ref, p_ref, *rest):
        o_ref = rest[-1]
        zz = z_ref[...]
        if remap is not None and backward:
            zz = _exact_dot(zz, rest[0])
        cos = _exact_dot(c_ref[...], e_ref) + m_ref[...]
        sin = _exact_dot(s_ref[...], e_ref)
        rot = _exact_dot(zz * sin if backward else zz, p_ref)
        if not backward:
            rot = rot * sin
        res = zz * cos + rot
        if remap is not None and not backward:
            res = _dot(res.astype(BF16), rest[0][...], NN)
        o_ref[...] = res.astype(o_ref.dtype)

    tm = _tile(t_rows, 256, 16)
    f_spec = pl.BlockSpec((tm, QK_ROPE), lambda i: (i, 0))
    return pl.pallas_call(
        body, name=name, grid=(t_rows // tm,),
        in_specs=[pl.BlockSpec((tm, w_in), lambda i: (i, colblk)), f_spec, f_spec,
                  pl.BlockSpec((QK_ROPE, width), lambda i: (0, 0)), pl.BlockSpec((1, width), lambda i: (0, 0)),
                  pl.BlockSpec((width, width), lambda i: (0, 0))]
        + [pl.BlockSpec(e.shape, lambda i: (0, 0)) for e in extra] + [pl.BlockSpec(memory_space=pl.ANY)] * len(dest),
        out_specs=pl.BlockSpec((tm, w_out), lambda i: (i, 0 if into is None else into[1])),
        out_shape=SDS((t_rows, w_out), out_dtype) if into is None else SDS(into[0].shape, into[0].dtype),
        input_output_aliases={} if into is None else {6 + len(extra): 0}, compiler_params=_cparams(),
    )(z, cos32, sin32, expand, plain, perm.T if backward else perm, *extra, *dest)


def _window_sum(x, w, transposed):
    n_rows = x.shape[0]
    zeros = jnp.zeros((POOL_PAD, x.shape[1]), F32)
    y = jnp.concatenate([zeros, x, zeros], axis=0)
    total = n_rows + 2 * POOL_PAD
    if transposed:
        y = y + pltpu.roll(y, total - 1, 0)
    else:
        y = y + pltpu.roll(y, 1, 0)
    step = 1
    while 2 * step < w:
        y = pltpu.roll(y, step, 0) + pltpu.roll(y, total - step, 0)
        step *= 2
    return y[POOL_PAD:POOL_PAD + n_rows]


def _window_count(n_rows, w):
    t = lax.broadcasted_iota(jnp.int32, (n_rows, 1), 0)
    lo = jnp.maximum(t - w // 2, 0)
    hi = jnp.minimum(t + (w - w // 2 - 1), n_rows - 1)
    return (hi - lo + 1).astype(F32)


def _pool_fwd(name, proj, n_rows, w_grp, scale):
    def body(x_ref, w_ref, sc_ref, y_ref, p_ref):
        for g, w in enumerate(POOL_WINDOWS):
            cols = slice(g * POOL_GROUP_DIM, (g + 1) * POOL_GROUP_DIM)
            x = x_ref[:, cols]
            p = _window_sum(x, w, False) * (1.0 / _window_count(n_rows, w)) - x
            pb = p.astype(BF16)
            p_ref[:, cols] = pb
            y_ref[:, cols] = (_dot(pb, w_ref[g], NN) * sc_ref[:, cols]).astype(BF16)

    blk = pl.BlockSpec((n_rows, POOL_DIM), lambda i: (0, 0))
    return pl.pallas_call(
        body, name=name, grid=(1,),
        in_specs=[blk, pl.BlockSpec(w_grp.shape, lambda i: (0, 0, 0)), pl.BlockSpec((1, POOL_DIM), lambda i: (0, 0))],
        out_specs=[blk, blk], out_shape=[SDS((n_rows, POOL_DIM), BF16)] * 2, compiler_params=_cparams(),
    )(proj, w_grp, scale)


def _pool_bwd(name, dcat, n_rows, p, w_grp, scale, into):
    def body(dy_ref, p_ref, w_ref, sc_ref, into_ref, dx_ref, dw_ref, dsc_ref):
        for g, w in enumerate(POOL_WINDOWS):
            cols = slice(g * POOL_GROUP_DIM, (g + 1) * POOL_GROUP_DIM)
            dy = dy_ref[:, cols]
            pb = p_ref[:, cols]
            pw = _dot(pb, w_ref[g], NN)
            dsc_ref[:, cols] = _colsum(dy * pw)
            dpw = (dy * sc_ref[:, cols]).astype(BF16)
            dw_ref[g] = _dot(pb, dpw, TN)
            dp = _dot(dpw, w_ref[g], NT)
            dx_ref[:, cols] = (_window_sum(dp * (1.0 / _window_count(n_rows, w)), w, True) - dp).astype(BF16)

    blk = pl.BlockSpec((n_rows, POOL_DIM), lambda i: (0, 0))
    w_spec = pl.BlockSpec(w_grp.shape, lambda i: (0, 0, 0))
    v_spec = pl.BlockSpec((1, POOL_DIM), lambda i: (0, 0))
    return pl.pallas_call(
        body, name=name, grid=(1,), in_specs=[blk, blk, w_spec, v_spec, pl.BlockSpec(memory_space=pl.ANY)],
        out_specs=[blk, w_spec, v_spec],
        out_shape=[SDS(into.shape, into.dtype), SDS(w_grp.shape, F32), SDS((1, POOL_DIM), F32)],
        input_output_aliases={4: 0}, compiler_params=_cparams(),
    )(dcat, p, w_grp, scale, into)


def _head_keys(kv_blk, k_rope):
    lane = lax.broadcasted_iota(jnp.int32, (1, HEAD_PAD), 1)
    return jnp.where(lane < QK_NOPE, kv_blk, k_rope)


def _attn_fwd(name, q, kv, k_rope, n_q):
    n_k = kv.shape[0]
    h = kv.shape[1] // HEAD_PAD
    tq = _tile(n_q, ATTN_Q_ROWS_FWD, 16)

    def body(q_ref, kv_ref, kr_ref, o_ref, lse_ref):
        kvb = kv_ref[...]
        s = _dot(q_ref[...], _head_keys(kvb, kr_ref[...]), NT) * ATTN_SCALE
        m = jnp.max(s, axis=-1, keepdims=True)
        e = jnp.exp(s - m)
        l = jnp.sum(e, axis=-1, keepdims=True)
        p = (e * (1.0 / l)).astype(BF16)
        lane = lax.broadcasted_iota(jnp.int32, (1, HEAD_PAD), 1)
        o_ref[...] = jnp.where(lane >= QK_NOPE, _dot(p, kvb, NN), 0.0).astype(BF16)
        lse_ref[...] = m + jnp.log(l)

    blk = pl.BlockSpec((tq, HEAD_PAD), lambda hh, i: (i, hh))
    return pl.pallas_call(
        body, name=name, grid=(h, n_q // tq),
        in_specs=[blk, pl.BlockSpec((n_k, HEAD_PAD), lambda hh, i: (0, hh)),
                  pl.BlockSpec((n_k, HEAD_PAD), lambda hh, i: (0, 0))],
        out_specs=[blk, pl.BlockSpec((None, tq, 1), lambda hh, i: (hh, i, 0))],
        out_shape=[SDS((n_q, h * HEAD_PAD), BF16), SDS((h, n_q, 1), F32)], compiler_params=_cparams(),
    )(q, kv, k_rope)


def _attn_bwd(name, q, kv, k_rope, o, lse, dy, dy_col0, n_q):
    n_k = kv.shape[0]
    h = kv.shape[1] // HEAD_PAD
    tq = _tile(n_q, ATTN_Q_ROWS_BWD, 16)
    n_i = n_q // tq

    def body(q_ref, kv_ref, kr_ref, o_ref, lse_ref, do_ref, dq_ref, dkv_ref, dkr_ref, acc_k, acc_v):
        hh, i = pl.program_id(0), pl.program_id(1)
        qq, kvb = q_ref[...], kv_ref[...]
        kk = _head_keys(kvb, kr_ref[...])
        d_o = do_ref[...]
        dd = d_o.astype(BF16)
        s = _dot(qq, kk, NT) * ATTN_SCALE
        p = jnp.exp(s - lse_ref[...])
        dp = _dot(dd, kvb, NT)
        delta = jnp.sum(d_o * o_ref[...].astype(F32), axis=-1, keepdims=True)
        ds = (p * (dp - delta) * ATTN_SCALE).astype(BF16)
        dq_ref[...] = _dot(ds, kk, NN)
        dk = _dot(ds, qq, TN)
        dv = _dot(p.astype(BF16), dd, TN)

        @pl.when(i == 0)
        def _():
            acc_k[...] = dk
            acc_v[...] = dv

        @pl.when(i > 0)
        def _():
            acc_k[...] += dk
            acc_v[...] += dv

        @pl.when(i == n_i - 1)
        def _():
            lane = lax.broadcasted_iota(jnp.int32, (1, HEAD_PAD), 1)
            dkv_ref[...] = jnp.where(lane < QK_NOPE, acc_k[...], acc_v[...]).astype(BF16)
            rope = jnp.where((lane >= QK_NOPE) & (lane < QK_HEAD), acc_k[...], 0.0)

            @pl.when(hh == 0)
            def _():
                dkr_ref[...] = rope

            @pl.when(hh > 0)
            def _():
                dkr_ref[...] += rope

    blk = pl.BlockSpec((tq, HEAD_PAD), lambda hh, i: (i, hh))
    kv_spec = pl.BlockSpec((n_k, HEAD_PAD), lambda hh, i: (0, hh))
    shared = pl.BlockSpec((n_k, HEAD_PAD), lambda hh, i: (0, 0))
    return pl.pallas_call(
        body, name=name, grid=(h, n_i),
        in_specs=[blk, kv_spec, shared, blk, pl.BlockSpec((None, tq, 1), lambda hh, i: (hh, i, 0)),
                  pl.BlockSpec((tq, HEAD_PAD), lambda hh, i: (i, dy_col0 + hh))],
        out_specs=[blk, kv_spec, shared],
        out_shape=[SDS((n_q, h * HEAD_PAD), F32), SDS((n_k, h * HEAD_PAD), BF16), SDS((n_k, HEAD_PAD), F32)],
        scratch_shapes=[pltpu.VMEM((n_k, HEAD_PAD), F32), pltpu.VMEM((n_k, HEAD_PAD), F32)],
        compiler_params=_cparams(),
    )(q, kv, k_rope, o, lse, dy)


CONV_COLS = 256


def _shift_rows(x, d):
    n_rows = x.shape[0]
    t = lax.broadcasted_iota(jnp.int32, (n_rows, 1), 0)
    if d > 0:
        return jnp.where(t >= d, pltpu.roll(x, d, 0), 0.0)
    return jnp.where(t < n_rows + d, pltpu.roll(x, n_rows + d, 0), 0.0)


def _conv_fwd(name, z3, conv_w):
    n_rows = z3.shape[0]
    nb = D_MODEL // CONV_COLS

    def body(b_ref, c_ref, v_ref, w_ref, y_ref):
        z = c_ref[...] * v_ref[...]
        zc = w_ref[0:1, :] * _shift_rows(z, 1) + w_ref[1:2, :] * z + w_ref[2:3, :] * _shift_rows(z, -1)
        y_ref[...] = (b_ref[...] * zc).astype(BF16)

    def part(k):
        return pl.BlockSpec((n_rows, CONV_COLS), lambda j: (0, k * nb + j))

    return pl.pallas_call(
        body, name=name, grid=(nb,),
        in_specs=[part(0), part(1), part(2), pl.BlockSpec((3, CONV_COLS), lambda j: (0, j))],
        out_specs=pl.BlockSpec((n_rows, CONV_COLS), lambda j: (0, j)),
        out_shape=SDS((n_rows, D_MODEL), BF16), compiler_params=_cparams(),
    )(z3, z3, z3, conv_w)


def _conv_bwd(name, dy, z3, conv_w):
    n_rows = z3.shape[0]
    nb = D_MODEL // CONV_COLS

    def body(dy_ref, b_ref, c_ref, v_ref, w_ref, db_ref, dc_ref, dv_ref, dw_ref):
        c, v, d_y = c_ref[...], v_ref[...], dy_ref[...]
        z = c * v
        z_dn, z_up = _shift_rows(z, 1), _shift_rows(z, -1)
        zc = w_ref[0:1, :] * z_dn + w_ref[1:2, :] * z + w_ref[2:3, :] * z_up
        db_ref[...] = (d_y * zc).astype(BF16)
        dzc = d_y * b_ref[...]
        dz = w_ref[0:1, :] * _shift_rows(dzc, -1) + w_ref[1:2, :] * dzc + w_ref[2:3, :] * _shift_rows(dzc, 1)
        dc_ref[...] = (dz * v).astype(BF16)
        dv_ref[...] = (dz * c).astype(BF16)
        dw_ref[0:1, :] = _colsum(dzc * z_dn)
        dw_ref[1:2, :] = _colsum(dzc * z)
        dw_ref[2:3, :] = _colsum(dzc * z_up)

    def part(k):
        return pl.BlockSpec((n_rows, CONV_COLS), lambda j: (0, k * nb + j))

    col = pl.BlockSpec((n_rows, CONV_COLS), lambda j: (0, j))
    w_spec = pl.BlockSpec((3, CONV_COLS), lambda j: (0, j))
    return pl.pallas_call(
        body, name=name, grid=(nb,), in_specs=[col, part(0), part(1), part(2), w_spec],
        out_specs=[col, col, col, w_spec],
        out_shape=[SDS((n_rows, D_MODEL), BF16)] * 3 + [SDS((3, D_MODEL), F32)], compiler_params=_cparams(),
    )(dy, z3, z3, z3, conv_w)


def _silu_rows(name, x):
    def body(x_ref, s_ref, d_ref):
        xx = x_ref[...]
        sg = jax.nn.sigmoid(xx)
        s_ref[...] = (xx * sg).astype(BF16)
        d_ref[...] = sg * (1.0 + xx * (1.0 - sg))

    return pl.pallas_call(body, name=name, out_shape=[SDS(x.shape, BF16), SDS(x.shape, F32)])(x)


def _sum_rows(name, x, scale=None):
    r, n = x.shape
    tn = _tile(n, 32768, 128)

    def body(*refs):
        acc = jnp.sum(refs[0][...].astype(F32), axis=0, keepdims=True)
        if scale is not None:
            acc = acc * refs[1][...]
        refs[-1][...] = acc

    in_specs = [pl.BlockSpec((r, tn), lambda j: (0, j))]
    args = [x]
    if scale is not None:
        in_specs.append(pl.BlockSpec((1, tn), lambda j: (0, j)))
        args.append(scale)
    return pl.pallas_call(body, name=name, grid=(n // tn,), in_specs=in_specs,
                          out_specs=pl.BlockSpec((1, tn), lambda j: (0, j)), out_shape=SDS((1, n), F32))(*args)


def _me_operand(me):
    return jnp.reshape(me, (1,)).astype(jnp.int32)


def _sum_slots(name, slots, src, me):
    n_slots, r, c = slots.shape
    tr = _tile(r, 432, 16)

    def body(me_ref, own_ref, x_ref, o_ref):
        acc = own_ref[...].astype(F32)
        for sl in range(n_slots):
            acc = acc + x_ref[sl].astype(F32)
        o_ref[...] = acc

    grid_spec = pltpu.PrefetchScalarGridSpec(
        num_scalar_prefetch=1, grid=(r // tr,),
        in_specs=[pl.BlockSpec((None, tr, c), lambda i, me_ref: (me_ref[0], i, 0)),
                  pl.BlockSpec((n_slots, tr, c), lambda i, me_ref: (0, i, 0))],
        out_specs=pl.BlockSpec((tr, c), lambda i, me_ref: (i, 0)))
    return pl.pallas_call(body, name=name, grid_spec=grid_spec, out_shape=SDS((r, c), F32),
                          compiler_params=_cparams())(_me_operand(me), src, slots)


def _adamw(name, w, g, m, v):
    shape = w.shape
    cols = shape[-1]
    rows = w.size // cols
    tr = _tile(rows, 512, 8)
    bc1 = 1.0 - ADAM_B1 ** ADAM_STEP
    bc2 = 1.0 - ADAM_B2 ** ADAM_STEP

    def body(w_ref, g_ref, m_ref, v_ref, d_ref, nm_ref, nv_ref):
        gg = g_ref[...]
        nm = ADAM_B1 * m_ref[...] + (1.0 - ADAM_B1) * gg
        nv = ADAM_B2 * v_ref[...] + (1.0 - ADAM_B2) * (gg * gg)
        nm_ref[...] = nm
        nv_ref[...] = nv
        d_ref[...] = -ADAM_LR * ((nm / bc1) / (jnp.sqrt(nv / bc2) + ADAM_EPS) + ADAM_WD * w_ref[...])

    spec = pl.BlockSpec((tr, cols), lambda i: (i, 0))
    outs = pl.pallas_call(body, name=name, grid=(rows // tr,), in_specs=[spec] * 4, out_specs=[spec] * 3,
                          out_shape=[SDS((rows, cols), F32)] * 3, compiler_params=_cparams())(
        w.reshape(rows, cols), g.reshape(rows, cols), m.reshape(rows, cols), v.reshape(rows, cols))
    return tuple(t.reshape(shape) for t in outs)


def _exchange(name, x, scatter, after=None):
    blk = x.shape[1:] if scatter else x.shape
    extra = [] if after is None else [after]

    def body(x_ref, *rest):
        out_ref, send_sems, recv_sems, local_sem = rest[len(extra):]
        mx, my, mc = lax.axis_index("x"), lax.axis_index("y"), lax.axis_index("c")
        me = 4 * mx + 2 * my + mc
        own = pltpu.make_async_copy(x_ref.at[me] if scatter else x_ref, out_ref.at[me], local_sem)
        own.start()
        copies = []
        for kk in range(1, N_DEV):
            px = jnp.bitwise_xor(mx, (kk >> 2) & 1)
            py = jnp.bitwise_xor(my, (kk >> 1) & 1)
            pc = jnp.bitwise_xor(mc, kk & 1)
            peer = 4 * px + 2 * py + pc
            send = pltpu.make_async_remote_copy(
                src_ref=x_ref.at[peer] if scatter else x_ref, dst_ref=out_ref.at[me],
                send_sem=send_sems.at[kk - 1], recv_sem=recv_sems.at[kk - 1],
                device_id=(px, py, pc), device_id_type=MESH)
            send.start()
            arrival = pltpu.make_async_remote_copy(
                src_ref=x_ref.at[peer] if scatter else x_ref, dst_ref=out_ref.at[peer],
                send_sem=send_sems.at[kk - 1], recv_sem=recv_sems.at[kk - 1],
                device_id=(px, py, pc), device_id_type=MESH)
            copies.append((send, arrival))
        for send, arrival in copies:
            arrival.wait_recv()
            send.wait_send()
        own.wait()

    return pl.pallas_call(
        body, name=name, out_shape=SDS((N_DEV,) + tuple(blk), x.dtype),
        in_specs=[pl.BlockSpec(memory_space=pl.ANY)] * (1 + len(extra)), out_specs=pl.BlockSpec(memory_space=pl.ANY),
        scratch_shapes=[pltpu.SemaphoreType.DMA((N_DEV - 1,)), pltpu.SemaphoreType.DMA((N_DEV - 1,)),
                        pltpu.SemaphoreType.DMA],
    )(x, *extra)


def _rope_perm(pre, reps, post):
    half = QK_ROPE // 4
    width = reps * (pre + QK_ROPE) + post
    p = np.zeros((width, width), np.float32)
    for rep in range(reps):
        s0 = rep * (pre + QK_ROPE) + pre
        for base in (s0, s0 + 2 * half):
            for i in range(half):
                p[base + half + i, base + i] = -1.0
                p[base + i, base + half + i] = 1.0
    return p


def _rope_layout(pre, reps, post):
    width = reps * (pre + QK_ROPE) + post
    expand = np.zeros((QK_ROPE, width), np.float32)
    plain = np.ones((1, width), np.float32)
    for rep in range(reps):
        s0 = rep * (pre + QK_ROPE) + pre
        expand[np.arange(QK_ROPE), s0 + np.arange(QK_ROPE)] = 1.0
        plain[0, s0:s0 + QK_ROPE] = 0.0
    return jnp.asarray(expand, BF16), jnp.asarray(plain, F32), jnp.asarray(_rope_perm(pre, reps, post), BF16)


def _head_spread():
    spread = np.zeros((HEADS * QK_HEAD, HEADS * HEAD_PAD), np.float32)
    for hh in range(HEADS):
        spread[hh * QK_HEAD + np.arange(QK_HEAD), hh * HEAD_PAD + np.arange(QK_HEAD)] = 1.0
    return jnp.asarray(spread, BF16)


def _rope_factors(n_lat, t_rows):
    half = QK_ROPE // 4
    pos = jnp.arange(n_lat)
    freqs = jnp.power(ROPE_THETA, -jnp.arange(0, 2 * half, 2, dtype=F32) / (2 * half))
    ang_r = (pos // GRID_W).astype(F32)[:, None] * freqs
    ang_c = (pos % GRID_W).astype(F32)[:, None] * freqs
    ang = jnp.concatenate([ang_r, ang_r, ang_c, ang_c], axis=-1)
    rest = t_rows - n_lat
    return (jnp.concatenate([jnp.cos(ang), jnp.ones((rest, QK_ROPE), F32)], axis=0),
            jnp.concatenate([jnp.sin(ang), jnp.zeros((rest, QK_ROPE), F32)], axis=0))


def _ffn_half_fwd(tag, s, mg, k, feed, i, coef, n_lat):
    wg_t, wu_t = feed.weights(f"{tag}_up", [f"gate_t{i}", f"up_t{i}"], s)
    u, a, b, hid = _ffn_up(f"{tag}_up", s, mg, k, n_lat, wg_t, wu_t)
    (wd,) = feed.weights(f"{tag}_down", [f"down{i}"], hid)
    s_out, o = _mm_resid(f"{tag}_down", [(hid, wd)], s, mg, k, coef, n_lat)
    return s_out, (s, u, a, b, hid, o, wg_t, wu_t, wd)


def _ffn_half_bwd(tag, ds_out, saved, mg, k, feed, i, coef, n_lat, out_rows=None):
    s, u, a, b, hid, o, wg_t, wu_t, wd = saved
    do, da, db, dgate = _ffn_dact(f"{tag}_dact", ds_out, o, mg, k, coef, n_lat, wd, a, b)
    dwd = _mm(f"{tag}_dwd", [(hid, do)], "tn", BF16)
    dwg_t, dwu_t = _dw_pair(f"{tag}_dwgu", da, db, u)
    token = feed.grads(tag, {f"down{i}": dwd, f"gate_t{i}": dwg_t, f"up_t{i}": dwu_t})
    ds_in, (dshift, dscale, dgain) = _du_adaln(f"{tag}_du", [(da, wg_t), (db, wu_t)], s, ds_out, mg, k, n_lat,
                                               _after(token), out_rows)
    return ds_in, dict(shift=dshift, scale=dscale, gate=dgate, gain=dgain)


def _after(token):
    return jnp.zeros((1, D_MODEL), F32) + token


def _mod_grad(parts, n_groups):
    rows = []
    zero = jnp.zeros((n_groups, 1, D_MODEL), F32)
    for k in range(3):
        for nm in ("shift", "scale", "gate"):
            t = parts[k].get(nm, zero)
            if t.shape[0] < n_groups:
                t = jnp.concatenate([t, jnp.zeros((n_groups - t.shape[0], 1, D_MODEL), F32)], axis=0)
            rows.append(t)
    return jnp.concatenate(rows, axis=1).reshape(n_groups, N_MOD * D_MODEL)


def _local_step(x, ctx, target, mod_h, mod_g, norm_g, feed, pool_w, pool_scale, q_norm_g, kv_norm_g, conv_w,
                final_norm_g):
    n_lat, n_ctx = x.shape[0], ctx.shape[0]
    t_all = n_lat + n_ctx
    mg0 = jnp.stack([jnp.concatenate([mod_h[0], norm_g[0]], axis=0), jnp.concatenate([mod_g, norm_g[0]], axis=0)])
    mg1 = jnp.concatenate([mod_h[1], norm_g[1]], axis=0)[None]

    s0 = jnp.concatenate([x, ctx], axis=0)
    s1, sv_f00 = _ffn_half_fwd("l0f0", s0, mg0, 0, feed, 0, 0.5, n_lat)

    (w_in,) = feed.weights("l0m_in", ["in_t"], s1)
    kv_rows = KV_RANK + QK_ROPE
    w_in_t = jnp.concatenate([
        w_in[:POOL_DIM], jnp.zeros((PA_CQ - POOL_DIM, D_MODEL), BF16), w_in[POOL_DIM:POOL_DIM + Q_RANK],
        w_in[POOL_DIM + Q_RANK:], jnp.zeros((PA_KV_W - kv_rows, D_MODEL), BF16)], axis=0)
    ua, proj = _adaln_mm("l0m_proj", s1, mg0, 1, n_lat, w_in_t)
    w_uq, w_ukv_t, w_ab_out = feed.weights("l0m_rest", ["uq", "ukv_t", "ab_out"], proj)
    pool_y, pool_p = _pool_fwd("l0m_pool", proj, n_lat, pool_w.astype(BF16), pool_scale)
    nq = _rmsnorm_fwd("l0m_qnorm", proj, Q_RANK, PA_CQ // Q_RANK, q_norm_g, n_lat)
    q_lin = _mm("l0m_q", [(nq, w_uq)], "nn", F32, 512, 768)
    cos32, sin32 = _rope_factors(n_lat, t_all)
    lay_q, lay_k = _rope_layout(QK_NOPE, HEADS, 0), _rope_layout(KV_RANK, 1, PA_KV_W - kv_rows)
    spread = _head_spread()
    q_flat = _rope("l0m_qrope", q_lin, Q_RANK, 0, cos32[:n_lat], sin32[:n_lat], lay_q, False, BF16, spread)
    kvr = _rope("l0m_krope", proj, PA_KV_W, PA_KV // PA_KV_W, cos32, sin32, lay_k, False, F32)
    nkv = _rmsnorm_fwd("l0m_kvnorm", kvr, KV_RANK, 0, kv_norm_g, t_all)
    kv = _mm("l0m_kv", [(nkv, w_ukv_t)], "nt", BF16, 768, 512)
    k_rope = jnp.pad(kvr[:, KV_RANK:KV_RANK + QK_ROPE].astype(BF16), ((0, 0), (QK_NOPE, HEAD_PAD - QK_HEAD)))
    o_flat, lse = _attn_fwd("l0m_attn", q_flat, kv, k_rope, n_lat)
    w_o_pad = jnp.pad(w_ab_out[POOL_DIM:].reshape(HEADS, V_HEAD, D_MODEL),
                      ((0, 0), (HEAD_PAD - V_HEAD, 0), (0, 0))).reshape(HEADS * HEAD_PAD, D_MODEL)
    w_o_pool = w_ab_out[:POOL_DIM]
    h2, mix_o = _mm_resid("l0m_out", [(pool_y, w_o_pool), (o_flat, w_o_pad)], s1, mg0[:1], 1, 1.0, n_lat)

    h3, sv_f01 = _ffn_half_fwd("l0f1", h2, mg0[:1], 2, feed, 1, 0.5, n_lat)

    h4, sv_f10 = _ffn_half_fwd("l1f0", h3, mg1, 0, feed, 2, 0.5, n_lat)
    w_cin_t, w_c_out = feed.weights("l1m", ["cin_t", "c_out"], h4)
    uc, z3 = _adaln_mm("l1m_in", h4, mg1, 1, n_lat, w_cin_t)
    yc = _conv_fwd("l1m_conv", z3, conv_w)
    h5, conv_o = _mm_resid("l1m_out", [(yc, w_c_out)], h4, mg1, 1, 1.0, n_lat)
    h6, sv_f11 = _ffn_half_fwd("l1f1", h5, mg1, 2, feed, 3, 0.5, n_lat)

    dh6, sq_cols, d_final_g = _final_loss("loss_head", h6, target, final_norm_g)
    g = {}
    dh5, g["f11"] = _ffn_half_bwd("l1f1", dh6, sv_f11, mg1, 2, feed, 3, 0.5, n_lat)

    do_c, dyc, dgate_c = _gate_mm("l1m_dy", dh5, conv_o, mg1, 1, 1.0, n_lat, w_c_out)
    d_c_out = _mm("l1m_dwout", [(yc, do_c)], "tn", BF16)
    db_, dc_, dv_, d_conv_w = _conv_bwd("l1m_dconv", dyc, z3, conv_w)
    dz3 = jnp.concatenate([db_, dc_, dv_], axis=-1)
    d_cin_t = _mm("l1m_dwin", [(dz3, uc)], "tn", BF16)
    token = feed.grads("l1m", {"c_out": d_c_out, "cin_t": d_cin_t})
    dh4, (dsh_c, dsc_c, dgn_c) = _du_adaln("l1m_du", [(dz3, w_cin_t)], h4, dh5, mg1, 1, n_lat, _after(token))
    dh3, g["f10"] = _ffn_half_bwd("l1f0", dh4, sv_f10, mg1, 0, feed, 2, 0.5, n_lat)

    dh2, g["f01"] = _ffn_half_bwd("l0f1", dh3, sv_f01, mg0[:1], 2, feed, 1, 0.5, n_lat)

    w_back = jnp.concatenate([w_o_pool, w_o_pad], axis=0)
    do_a, dcat, dgate_a = _gate_mm("l0m_dcat", dh2, mix_o, mg0[:1], 1, 1.0, n_lat, w_back)
    d_o_pad = _mm("l0m_dwout_a", [(o_flat, do_a)], "tn", BF16)
    d_ab_out = jnp.concatenate([
        _mm("l0m_dwout_p", [(pool_y, do_a)], "tn", BF16),
        d_o_pad.reshape(HEADS, HEAD_PAD, D_MODEL)[:, HEAD_PAD - V_HEAD:].reshape(HEADS * V_HEAD, D_MODEL)], axis=0)
    dproj = jnp.zeros((t_all, PA_W), BF16)
    dproj, d_pool_w, d_pool_scale = _pool_bwd("l0m_dpool", dcat, n_lat, pool_p, pool_w.astype(BF16), pool_scale, dproj)
    dq_flat, dkv, dk_rope = _attn_bwd("l0m_dattn", q_flat, kv, k_rope, o_flat, lse, dcat, POOL_DIM // HEAD_PAD, n_lat)
    dq_lin = _rope("l0m_dqrope", dq_flat, Q_RANK, 0, cos32[:n_lat], sin32[:n_lat], lay_q, True, BF16, spread)
    d_uq = _mm("l0m_dwuq", [(nq, dq_lin)], "tn", BF16, 768, 768)
    dnq = _mm("l0m_dnq", [(dq_lin, w_uq)], "nt", F32, 512, 768)
    dproj, d_q_norm_g = _rmsnorm_bwd("l0m_dqnorm", proj, Q_RANK, PA_CQ // Q_RANK, dnq, q_norm_g, n_lat, BF16,
                                     (dproj, PA_CQ // Q_RANK))
    dnkv = _mm("l0m_dnkv", [(dkv, w_ukv_t)], "nn", F32, 768, 256)
    d_ukv_t = _mm("l0m_dwukv", [(dkv, nkv)], "tn", BF16, 512, 256)
    dckv, d_kv_norm_g = _rmsnorm_bwd("l0m_dkvnorm", kvr, KV_RANK, 0, dnkv, kv_norm_g, t_all)
    dkvr = jnp.concatenate([dckv, dk_rope[:, QK_NOPE:QK_HEAD],
                            jnp.zeros((t_all, PA_KV_W - KV_RANK - QK_ROPE), F32)], axis=-1)
    dproj = _rope("l0m_dkrope", dkvr, PA_KV_W, 0, cos32, sin32, lay_k, True, BF16, None, (dproj, PA_KV // PA_KV_W))
    d_in_pad = _mm("l0m_dwin", [(dproj, ua)], "tn", BF16, 640, 512)
    d_in_t = jnp.concatenate([d_in_pad[:POOL_DIM], d_in_pad[PA_CQ:PA_CQ + Q_RANK],
                              d_in_pad[PA_KV:PA_KV + kv_rows]], axis=0)
    token = feed.grads("l0m", {"ab_out": d_ab_out, "uq": d_uq, "ukv_t": d_ukv_t, "in_t": d_in_t})
    ds1, (dsh_a, dsc_a, dgn_a) = _du_adaln("l0m_du", [(dproj, w_in_t)], s1, dh2, mg0, 1, n_lat, _after(token))
    grad_x, g["f00"] = _ffn_half_bwd("l0f0", ds1, sv_f00, mg0, 0, feed, 0, 0.5, n_lat, out_rows=n_lat)

    dmod0 = _mod_grad([g["f00"], dict(shift=dsh_a, scale=dsc_a, gate=dgate_a), g["f01"]], 2)
    dmod1 = _mod_grad([g["f10"], dict(shift=dsh_c, scale=dsc_c, gate=dgate_c), g["f11"]], 1)
    d_norm_g = jnp.stack([
        jnp.concatenate([jnp.sum(g["f00"]["gain"], axis=0), jnp.sum(dgn_a, axis=0), g["f01"]["gain"][0]], axis=0),
        jnp.concatenate([g["f10"]["gain"][0], dgn_c[0], g["f11"]["gain"][0]], axis=0)])
    grads = dict(
        pool_w=d_pool_w, pool_scale=d_pool_scale, q_norm_g=d_q_norm_g[0], kv_norm_g=d_kv_norm_g[0],
        conv_w=d_conv_w, final_norm_g=d_final_g[0], norm_g=d_norm_g,
        mod_h=jnp.stack([dmod0[0], dmod1[0]]), mod_g=dmod0[1])
    return sq_cols, grad_x, grads


HBM_SPEC = pl.BlockSpec(memory_space=pltpu.HBM)
SEM_SPEC = pl.BlockSpec(memory_space=pltpu.SEMAPHORE)
ANY_SPEC = pl.BlockSpec(memory_space=pl.ANY)
SIDE_EFFECT = pltpu.SideEffectType.DATAFLOW_SIDE_EFFECTING
N_PEERS = N_DEV - 1


def _mesh_place():
    mx, my, mc = lax.axis_index("x"), lax.axis_index("y"), lax.axis_index("c")
    return mx, my, mc, 4 * mx + 2 * my + mc


def _peer(place, kk):
    mx, my, mc, _ = place
    px = jnp.bitwise_xor(mx, (kk >> 2) & 1)
    py = jnp.bitwise_xor(my, (kk >> 1) & 1)
    pc = jnp.bitwise_xor(mc, kk & 1)
    return (px, py, pc), 4 * px + 2 * py + pc


def _hbm(a):
    return pltpu.with_memory_space_constraint(a, pltpu.HBM)


def _landing(block, me):
    zone = lax.empty((N_DEV,) + block.shape, block.dtype)
    return lax.dynamic_update_slice(zone, block[None], (me,) + (0,) * block.ndim)


ALL_PEERS = tuple(range(1, N_DEV))
SIBLING = 1
CHIP_PEERS = (2, 4, 6)
RELAYED = (3, 5, 7)


def _exchange_start(name, srcs, lands, scatter, after, peers=ALL_PEERS):
    n = len(srcs)
    extra = [] if after is None else [after]

    def body(*refs):
        src, land = refs[:n], refs[n:2 * n]
        send_sems, recv_sems, token = refs[2 * n + len(extra)], refs[2 * n + len(extra) + 1], refs[-1]
        place = _mesh_place()
        for a in range(n):
            for kk in peers:
                dev, peer = _peer(place, kk)
                pltpu.make_async_remote_copy(
                    src_ref=src[a].at[peer] if scatter else src[a],
                    dst_ref=land[a].at[kk - 1] if scatter else land[a].at[place[3]],
                    send_sem=send_sems.at[a * N_PEERS + kk - 1], recv_sem=recv_sems.at[a * N_PEERS + kk - 1],
                    device_id=dev, device_id_type=MESH).start()
        token[...] = jnp.zeros_like(token)

    thru = [pltpu.HBM(t.shape, t.dtype) for t in (*srcs, *lands)]
    res = pl.pallas_call(
        body, name=name,
        out_shape=(pltpu.SemaphoreType.DMA((n * N_PEERS,)), pltpu.SemaphoreType.DMA((n * N_PEERS,)), *thru,
                   SDS((8, 128), F32)),
        in_specs=[HBM_SPEC] * (2 * n) + [ANY_SPEC] * len(extra),
        out_specs=(SEM_SPEC, SEM_SPEC, *([HBM_SPEC] * (2 * n)), pl.BlockSpec(memory_space=pltpu.VMEM)),
        input_output_aliases={i: 2 + i for i in range(2 * n)},
        compiler_params=pltpu.CompilerParams(has_side_effects=SIDE_EFFECT),
    )(*[_hbm(s) for s in srcs], *[_hbm(t) for t in lands], *extra)
    return res[0], res[1], list(res[2:2 + n]), list(res[2 + n:2 + 2 * n]), res[-1]


def _exchange_wait(name, send_sems, recv_sems, srcs, lands, places, scatter, after):
    n = len(srcs)

    def body(*refs):
        src, land = refs[:n], refs[n:2 * n]
        send, recv = refs[2 * n], refs[2 * n + 1]
        place = _mesh_place()
        for a in range(n):
            for kk in range(1, N_DEV):
                dev, peer = _peer(place, kk)
                cp = pltpu.make_async_remote_copy(
                    src_ref=src[a].at[peer] if scatter else src[a],
                    dst_ref=land[a].at[kk - 1] if scatter else land[a].at[peer],
                    send_sem=send.at[places[a] * N_PEERS + kk - 1], recv_sem=recv.at[places[a] * N_PEERS + kk - 1],
                    device_id=dev, device_id_type=MESH)
                cp.wait_send()
                cp.wait_recv()

    thru = [pltpu.HBM(t.shape, t.dtype) for t in (*srcs, *lands)]
    res = pl.pallas_call(
        body, name=name, out_shape=tuple(thru),
        in_specs=[HBM_SPEC] * (2 * n) + [SEM_SPEC, SEM_SPEC] + [ANY_SPEC] * len(after),
        out_specs=tuple([HBM_SPEC] * (2 * n)), input_output_aliases={i: i for i in range(2 * n)},
        compiler_params=pltpu.CompilerParams(has_side_effects=SIDE_EFFECT),
    )(*srcs, *lands, send_sems, recv_sems, *after)
    return list(res[:n]), list(res[n:])


def _gather_relay(name, send1, recv1, lands, places, after):
    n = len(lands)

    def body(*refs):
        land, s1, r1 = refs[:n], refs[n], refs[n + 1]
        s2, r2 = refs[n + 3], refs[n + 4]
        place = _mesh_place()
        sibling = _peer(place, SIBLING)[0]
        for a in range(n):
            for j, kk in enumerate(CHIP_PEERS):
                dev, origin = _peer(place, kk)
                block = land[a].at[origin]
                pltpu.make_async_remote_copy(
                    src_ref=block, dst_ref=block, send_sem=s1.at[places[a] * N_PEERS + kk - 1],
                    recv_sem=r1.at[places[a] * N_PEERS + kk - 1], device_id=dev, device_id_type=MESH).wait_recv()
                pltpu.make_async_remote_copy(
                    src_ref=block, dst_ref=block, send_sem=s2.at[a * 3 + j], recv_sem=r2.at[a * 3 + j],
                    device_id=sibling, device_id_type=MESH).start()

    res = pl.pallas_call(
        body, name=name,
        out_shape=(pltpu.SemaphoreType.DMA((3 * n,)), pltpu.SemaphoreType.DMA((3 * n,)),
                   *[pltpu.HBM(t.shape, t.dtype) for t in lands]),
        in_specs=[HBM_SPEC] * n + [SEM_SPEC, SEM_SPEC, ANY_SPEC],
        out_specs=(SEM_SPEC, SEM_SPEC, *([HBM_SPEC] * n)),
        input_output_aliases={i: 2 + i for i in range(n)},
        compiler_params=pltpu.CompilerParams(has_side_effects=SIDE_EFFECT),
    )(*lands, send1, recv1, after)
    return res[0], res[1], list(res[2:])


def _gather_wait(name, send1, recv1, send2, recv2, srcs, lands, places, after):
    n = len(lands)

    def body(*refs):
        src, land = refs[:n], refs[n:2 * n]
        s1, r1, s2, r2 = refs[2 * n:2 * n + 4]
        place = _mesh_place()
        for a in range(n):
            for kk in (SIBLING,) + CHIP_PEERS:
                dev, origin = _peer(place, kk)
                first = pltpu.make_async_remote_copy(
                    src_ref=src[a], dst_ref=land[a].at[origin], send_sem=s1.at[places[a] * N_PEERS + kk - 1],
                    recv_sem=r1.at[places[a] * N_PEERS + kk - 1], device_id=dev, device_id_type=MESH)
                first.wait_send()
                if kk == SIBLING:
                    first.wait_recv()
            for j, kk in enumerate(CHIP_PEERS):
                dev, origin = _peer(place, kk + 1)
                relay = pltpu.make_async_remote_copy(
                    src_ref=src[a], dst_ref=land[a].at[origin], send_sem=s2.at[a * 3 + j], recv_sem=r2.at[a * 3 + j],
                    device_id=dev, device_id_type=MESH)
                relay.wait_send()
                relay.wait_recv()

    arrays = (*srcs, *lands)
    res = pl.pallas_call(
        body, name=name, out_shape=tuple(pltpu.HBM(t.shape, t.dtype) for t in arrays),
        in_specs=[HBM_SPEC] * (2 * n) + [SEM_SPEC] * 4 + [ANY_SPEC], out_specs=tuple([HBM_SPEC] * (2 * n)),
        input_output_aliases={i: i for i in range(2 * n)},
        compiler_params=pltpu.CompilerParams(has_side_effects=SIDE_EFFECT),
    )(*arrays, send1, recv1, send2, recv2, after)
    return list(res[n:])


class _Feed:
    def __init__(self, shards, groups, me):
        self.shards, self.groups, self.me, self.pos = shards, groups, me, 0
        self.sems, self.srcs, self.lands = {}, {}, {}
        self.relays = {}
        self.pending = []

    def start(self, tag, names, after):
        srcs = [self.shards[nm] for nm in names]
        lands = [_landing(s, self.me) for s in srcs]
        send, recv, srcs, lands, self.token = _exchange_start(
            f"gather_start_{tag}", srcs, lands, False, after, (SIBLING,) + CHIP_PEERS)
        for i, nm in enumerate(names):
            self.sems[nm], self.srcs[nm], self.lands[nm] = (send, recv, i), srcs[i], lands[i]
        return self.token

    def _relay(self, gi, after):
        names = self.groups[gi]
        if gi not in self.relays:
            send, recv, _ = self.sems[names[0]]
            places = [self.sems[nm][2] for nm in names]
            send2, recv2, lands = _gather_relay(f"gather_relay_{gi}", send, recv, [self.lands[nm] for nm in names],
                                                places, after)
            for nm, t in zip(names, lands):
                self.lands[nm] = t
            self.relays[gi] = (send2, recv2)
            after = lands[0]
        return after

    def weights(self, tag, names, after):
        gi = self.pos
        assert names == self.groups[gi], (names, self.groups[gi])
        if gi == 0:
            after = self.token
        self._relay(gi, after)
        if 1 <= gi < len(self.groups) - 1:
            after = self._relay(gi + 1, after)
        send2, recv2 = self.relays[gi]
        send, recv, _ = self.sems[names[0]]
        got = _gather_wait(f"gather_wait_{tag}", send, recv, send2, recv2, [self.srcs[nm] for nm in names],
                           [self.lands[nm] for nm in names], [self.sems[nm][2] for nm in names], after)
        self.pos += 1
        return [t.reshape((N_DEV * t.shape[1],) + t.shape[2:]) for t in got]

    def grads(self, tag, full):
        names = list(full)
        srcs = [full[nm].reshape((N_DEV, full[nm].shape[0] // N_DEV) + full[nm].shape[1:]) for nm in names]
        lands = [lax.empty((N_PEERS,) + s.shape[1:], s.dtype) for s in srcs]
        send, recv, srcs, lands, token = _exchange_start(f"scatter_start_{tag}", srcs, lands, True, None)
        self.pending.append((tag, names, send, recv, srcs, lands))
        return token[0, 0]

    def collect(self, tags, after, keep_slots=()):
        out = {}
        for tag, names, send, recv, srcs, lands in self.pending:
            if tag not in tags:
                continue
            srcs, got = _exchange_wait(f"scatter_wait_{tag}", send, recv, srcs, lands, list(range(len(names))), True,
                                       after)
            for nm, slots, src in zip(names, got, srcs):
                out[nm] = ((slots, src) if nm.startswith(tuple(keep_slots))
                           else _sum_slots(f"reduce_{nm}", slots, src, self.me))
        return out


def _adamw_math(w, gg, m, v):
    nm = ADAM_B1 * m + (1.0 - ADAM_B1) * gg
    nv = ADAM_B2 * v + (1.0 - ADAM_B2) * (gg * gg)
    bc1 = 1.0 - ADAM_B1 ** ADAM_STEP
    bc2 = 1.0 - ADAM_B2 ** ADAM_STEP
    return -ADAM_LR * ((nm / bc1) / (jnp.sqrt(nv / bc2) + ADAM_EPS) + ADAM_WD * w), nm, nv


def _adamw_part(name, i, w, scattered, me, m, v, prev):
    n_parts, rows, cols = w.shape
    tr = _tile(rows, 256, 16)
    if prev is None:
        prev = tuple(lax.empty(w.shape, F32) for _ in range(4))

    slots, src = scattered

    def body(me_ref, w_ref, g_ref, own_ref, m_ref, v_ref, *rest):
        go_ref, d_ref, nm_ref, nv_ref = rest[4:]
        gg = own_ref[...].astype(F32)
        for sl in range(N_PEERS):
            gg = gg + g_ref[sl].astype(F32)
        d, nm, nv = _adamw_math(w_ref[...], gg, m_ref[...], v_ref[...])
        go_ref[...] = gg
        d_ref[...] = d
        nm_ref[...] = nm
        nv_ref[...] = nv

    part = pl.BlockSpec((None, tr, cols), lambda r, me_ref: (i, r, 0))
    grid_spec = pltpu.PrefetchScalarGridSpec(
        num_scalar_prefetch=1, grid=(rows // tr,),
        in_specs=[part, pl.BlockSpec((N_PEERS, tr, cols), lambda r, me_ref: (0, r, 0)),
                  pl.BlockSpec((None, tr, cols), lambda r, me_ref: (me_ref[0], r, 0)), part, part] + [ANY_SPEC] * 4,
        out_specs=[part] * 4)
    return pl.pallas_call(
        body, name=name, grid_spec=grid_spec, out_shape=[SDS(w.shape, F32)] * 4,
        input_output_aliases={6 + k: k for k in range(4)}, compiler_params=_cparams(),
    )(_me_operand(me), w, slots, src, m, v, *prev)


WEIGHT_NAMES = ("c_ctx", "norm_g", "w_mod", "b_mod", "ffn_w_gate", "ffn_w_up", "ffn_w_down", "ab_w_in", "pool_w",
                "pool_scale", "q_norm_g", "w_uq", "kv_norm_g", "w_ukv", "ab_w_out", "conv_w_in", "conv_w",
                "conv_w_out", "final_norm_g")


def kernel(x, c, ctx, c_ctx, norm_g, w_mod, b_mod, ffn_w_gate, ffn_w_up, ffn_w_down, ab_w_in, pool_w, pool_scale, q_norm_g, w_uq, kv_norm_g, w_ukv, ab_w_out, conv_w_in, conv_w, conv_w_out, final_norm_g, loss_target, m_c_ctx, m_norm_g, m_w_mod, m_b_mod, m_ffn_w_gate, m_ffn_w_up, m_ffn_w_down, m_ab_w_in, m_pool_w, m_pool_scale, m_q_norm_g, m_w_uq, m_kv_norm_g, m_w_ukv, m_ab_w_out, m_conv_w_in, m_conv_w, m_conv_w_out, m_final_norm_g, v_c_ctx, v_norm_g, v_w_mod, v_b_mod, v_ffn_w_gate, v_ffn_w_up, v_ffn_w_down, v_ab_w_in, v_pool_w, v_pool_scale, v_q_norm_g, v_w_uq, v_kv_norm_g, v_w_ukv, v_ab_w_out, v_conv_w_in, v_conv_w, v_conv_w_out, v_final_norm_g):
    weights = (c_ctx, norm_g, w_mod, b_mod, ffn_w_gate, ffn_w_up, ffn_w_down, ab_w_in, pool_w, pool_scale, q_norm_g,
               w_uq, kv_norm_g, w_ukv, ab_w_out, conv_w_in, conv_w, conv_w_out, final_norm_g)
    moms = (m_c_ctx, m_norm_g, m_w_mod, m_b_mod, m_ffn_w_gate, m_ffn_w_up, m_ffn_w_down, m_ab_w_in, m_pool_w,
            m_pool_scale, m_q_norm_g, m_w_uq, m_kv_norm_g, m_w_ukv, m_ab_w_out, m_conv_w_in, m_conv_w, m_conv_w_out,
            m_final_norm_g)
    vels = (v_c_ctx, v_norm_g, v_w_mod, v_b_mod, v_ffn_w_gate, v_ffn_w_up, v_ffn_w_down, v_ab_w_in, v_pool_w,
            v_pool_scale, v_q_norm_g, v_w_uq, v_kv_norm_g, v_w_ukv, v_ab_w_out, v_conv_w_in, v_conv_w, v_conv_w_out,
            v_final_norm_g)
    me = 4 * lax.axis_index("x") + 2 * lax.axis_index("y") + lax.axis_index("c")
    n_lat, n_ctx = x.shape[1], ctx.shape[1]
    d = D_MODEL
    mod_cols = w_mod.shape[-1]
    ng_sh, cw_sh = norm_g.shape[-1], conv_w.shape[-1]

    def ffn_shards(i):
        return {f"gate_t{i}": ffn_w_gate[i // 2, i % 2].T, f"up_t{i}": ffn_w_up[i // 2, i % 2].T,
                f"down{i}": ffn_w_down[i // 2, i % 2]}

    local = {**ffn_shards(0), "in_t": ab_w_in[0].T, "uq": w_uq[0], "ukv_t": w_ukv[0].T, "ab_out": ab_w_out[0],
             **ffn_shards(1), **ffn_shards(2), "cin_t": conv_w_in[0].T, "c_out": conv_w_out[0], **ffn_shards(3)}
    ffn_groups = [[[f"gate_t{i}", f"up_t{i}"], [f"down{i}"]] for i in range(4)]
    groups = [*ffn_groups[0], ["in_t"], ["uq", "ukv_t", "ab_out"], *ffn_groups[1], *ffn_groups[2],
              ["cin_t", "c_out"], *ffn_groups[3]]
    feed = _Feed({nm: a.astype(BF16) for nm, a in local.items()}, groups, me)

    small = jnp.concatenate([c.reshape(-1), norm_g.reshape(-1), conv_w.reshape(-1)])
    small_n = -(-small.shape[0] // 1024) * 1024
    small = jnp.pad(small, (0, small_n - small.shape[0])).reshape(small_n // 128, 128)
    small_all = _exchange("gather_small", small, False).reshape(N_DEV, small_n)
    c_all = small_all[:, :d]
    o1 = d + 6 * ng_sh
    norm_g_full = small_all[:, d:o1].reshape(N_DEV, 2, 3, ng_sh).transpose(1, 2, 0, 3).reshape(2, 3, d)
    conv_w_full = small_all[:, o1:o1 + 3 * cw_sh].reshape(N_DEV, 3, cw_sh).transpose(1, 0, 2).reshape(3, d)

    cond = jnp.concatenate([c_all, jnp.broadcast_to(c_ctx[None, :], (N_DEV, d))], axis=0)
    sil, dsil = _silu_rows("mod_silu", cond)
    w_mod_b = w_mod.astype(BF16)
    b_sh = lax.dynamic_slice(b_mod, (0, me * mod_cols), (2, mod_cols))
    m_part = jnp.stack([_mm(f"mod_fwd{l}", [(sil, w_mod_b[l])], "nn", F32, 16, 384, bias=b_sh[l:l + 1])
                        for l in range(2)], axis=1)
    m_all = _exchange("gather_mod", m_part.reshape(-1, 128), False).reshape(N_DEV, 2 * N_DEV, 2, mod_cols)
    m_mine = lax.dynamic_index_in_dim(m_all, me, axis=1, keepdims=False)
    mod_h = m_mine.transpose(1, 0, 2).reshape(2, N_MOD, d)
    mod_g = m_all[:, N_DEV, 0, :].reshape(N_MOD, d)

    first = feed.start("first", [nm for grp in groups[:3] for nm in grp], m_all)
    feed.start("rest", [nm for grp in groups[3:] for nm in grp], first)

    sq_cols, grad_x, g = _local_step(x[0], ctx[0], loss_target[0], mod_h, mod_g, norm_g_full, feed, pool_w[0],
                                  pool_scale, q_norm_g, kv_norm_g, conv_w_full, final_norm_g)
    w_of, m_of, v_of = (dict(zip(WEIGHT_NAMES, t)) for t in (weights, moms, vels))
    results = {}

    def update(nm, grad, view=lambda t: t):
        outs = _adamw(f"adamw_{nm}", view(w_of[nm]), grad.reshape(view(w_of[nm]).shape), view(m_of[nm]), view(v_of[nm]))
        results[nm] = tuple(view(t) for t in (grad.reshape(view(w_of[nm]).shape), *outs))

    def swap(t):
        return jnp.swapaxes(t, -1, -2)

    stacked = ("gate_t", "up_t", "down")
    early = feed.collect(["l1f1", "l1m", "l1f0", "l0f1", "l0m"], [grad_x], stacked)
    update("ab_w_in", early["in_t"], swap)
    update("w_uq", early["uq"])
    update("w_ukv", early["ukv_t"].T)
    update("ab_w_out", early["ab_out"])
    update("conv_w_in", early["cin_t"].T)
    update("conv_w_out", early["c_out"])
    ffn = {}
    for nm, prefix, view in (("ffn_w_gate", "gate_t", swap), ("ffn_w_up", "up_t", swap),
                             ("ffn_w_down", "down", lambda t: t)):
        w4, m4, v4 = (view(t).reshape((4,) + view(t).shape[-2:]) for t in (w_of[nm], m_of[nm], v_of[nm]))
        prev = None
        for i in (3, 2, 1):
            prev = _adamw_part(f"adamw_{nm}{i}", i, w4, early[f"{prefix}{i}"], me, m4, v4, prev)
        ffn[nm] = (prefix, view, w4, m4, v4, prev)
    done_early = [results[nm][1] for nm in results] + [state[5][1] for state in ffn.values()]
    late = feed.collect(["l0f0"], done_early, stacked)
    for nm, (prefix, view, w4, m4, v4, prev) in ffn.items():
        outs = _adamw_part(f"adamw_{nm}0", 0, w4, late[f"{prefix}0"], me, m4, v4, prev)
        results[nm] = tuple(view(t.reshape(view(w_of[nm]).shape)) for t in outs)

    dm = jnp.stack([g["mod_h"], jnp.stack([g["mod_g"], jnp.zeros_like(g["mod_g"])])])
    dm_all = _exchange("gather_dmod", dm.reshape(-1, 128), False, results["ffn_w_down"][1]).reshape(N_DEV, 2, 2, N_MOD * d)
    grad_b_mod = _sum_rows("dmod_bias", dm_all.reshape(2 * N_DEV, 2 * N_MOD * d)).reshape(2, N_MOD * d)
    dm_sh = lax.dynamic_slice(dm_all, (0, 0, 0, me * mod_cols), (N_DEV, 2, 2, mod_cols))
    gw_mod, cctx_parts = [], []
    for l in range(2):
        dm_l = dm_sh[:, :, l, :].transpose(1, 0, 2).reshape(2 * N_DEV, mod_cols).astype(BF16)
        gw_mod.append(_mm(f"mod_dw{l}", [(sil, dm_l)], "tn", F32, 512, 384))
        dm_ctx = jnp.concatenate([dm_l[N_DEV:], jnp.zeros((N_DEV, mod_cols), BF16)], axis=0)
        cctx_parts.append(_mm(f"mod_dcond{l}", [(dm_ctx, w_mod_b[l])], "nt", F32, 16, 512))
    cctx_part = _sum_rows("mod_dcond_sum", jnp.concatenate(cctx_parts, axis=0))
    update("w_mod", jnp.stack(gw_mod))
    update("b_mod", grad_b_mod)

    small_g = jnp.concatenate([g["pool_w"].reshape(-1), g["pool_scale"].reshape(-1), g["q_norm_g"].reshape(-1),
                               g["kv_norm_g"].reshape(-1), g["final_norm_g"].reshape(-1), g["norm_g"].reshape(-1),
                               g["conv_w"].reshape(-1), sq_cols.reshape(-1), cctx_part.reshape(-1)])
    sizes = [pool_w.size, pool_scale.size, q_norm_g.size, kv_norm_g.size, d, 6 * d, 3 * d, d, d]
    sg_n = -(-small_g.shape[0] // 1024) * 1024
    small_g = jnp.pad(small_g, (0, sg_n - small_g.shape[0]))
    sg_all = _exchange("gather_small_grads", small_g.reshape(-1, 128), False).reshape(N_DEV, sg_n)
    scale_vec = jnp.concatenate([jnp.ones((1, sum(sizes[:-1])), F32), dsil[N_DEV:N_DEV + 1],
                                 jnp.ones((1, sg_n - sum(sizes)), F32)], axis=1)
    sg = _sum_rows("small_grads_sum", sg_all, scale_vec)[0]
    cuts, pos = [], 0
    for sz in sizes:
        cuts.append(sg[pos:pos + sz])
        pos += sz
    g_pool_w, g_pool_scale, g_q_norm, g_kv_norm, g_final, g_norm_full, g_conv_full, sq_all, g_c_ctx = cuts
    loss = 0.5 * jnp.sum(sq_all) / d
    update("c_ctx", g_c_ctx)
    update("norm_g", lax.dynamic_slice(g_norm_full.reshape(2, 3, d), (0, 0, me * ng_sh), (2, 3, ng_sh)))
    update("conv_w", lax.dynamic_slice(g_conv_full.reshape(3, d), (0, me * cw_sh), (3, cw_sh)))
    update("pool_w", g_pool_w)
    update("pool_scale", g_pool_scale)
    update("q_norm_g", g_q_norm)
    update("kv_norm_g", g_kv_norm)
    update("final_norm_g", g_final)
    outs = [results[nm] for nm in WEIGHT_NAMES]
    return (loss, grad_x[None], *[o[0] for o in outs], *[o[1] for o in outs], *[o[2] for o in outs],
            *[o[3] for o in outs])
```

```python
import functools
import math

import jax
import jax.numpy as jnp
import numpy as np
from jax import lax
from jax.experimental import pallas as pl
from jax.experimental.pallas import tpu as pltpu

F32 = jnp.float32
BF16 = jnp.bfloat16
MESH = pl.DeviceIdType.MESH
SDS = jax.ShapeDtypeStruct

N_DEV = 8
D_MODEL = 1024
N_MOD = 9
D_FF = 2816
POOL_WINDOWS = (2, 4, 8, 16)
POOL_DIM = 512
POOL_GROUP_DIM = 128
HEADS = 8
QK_NOPE = 64
QK_ROPE = 32
QK_HEAD = QK_NOPE + QK_ROPE
V_HEAD = 64
Q_RANK = 768
KV_RANK = 256
GRID_W = 64
ROPE_THETA = 10000.0
RMS_EPS = 1e-6
ATTN_SCALE = 1.0 / math.sqrt(QK_HEAD)
HEAD_PAD = 128
POOL_PAD = 16
ATTN_Q_ROWS_FWD = 256
ATTN_Q_ROWS_BWD = 1024
PA_POOL, PA_CQ, PA_KV = 0, 768, 1536
PA_KV_W = 384
PA_W = PA_KV + PA_KV_W

ADAM_LR, ADAM_B1, ADAM_B2, ADAM_EPS, ADAM_WD, ADAM_STEP = 0.001, 0.9, 0.999, 1e-08, 0.01, 10

VMEM_LIMIT_BYTES = 56 * 1024 * 1024

NN = ((1,), (0,))
NT = ((1,), (1,))
TN = ((0,), (0,))


def _cparams():
    return pltpu.CompilerParams(vmem_limit_bytes=VMEM_LIMIT_BYTES)


def _dot(a, b, dims):
    return lax.dot_general(a, b, (dims, ((), ())), preferred_element_type=F32)


def _tile(n, cap, mult=8):
    t = (min(cap, n) // mult) * mult
    while t >= mult:
        if n % t == 0:
            return t
        t -= mult
    return n


def _colsum(x):
    return jnp.sum(x, axis=0, keepdims=True)


def _rms(x):
    r = lax.rsqrt(jnp.mean(x * x, axis=-1, keepdims=True) + RMS_EPS)
    return x * r, r


def _rms_bwd(n, r, dn):
    return r * (dn - n * jnp.mean(dn * n, axis=-1, keepdims=True))


def _rowwise(name, fn, t_rows, tm, n_lat, rows, vecs, outs, accs, into=None):
    nt = t_rows // tm
    nlt = n_lat // tm
    n_groups = 2 if nlt < nt else 1

    def grp(i):
        return jnp.where(i >= nlt, 1, 0) if n_groups == 2 else 0

    in_specs = [pl.BlockSpec((tm, w), functools.partial(lambda i, cb: (i, cb), cb=cb)) for (_, w, cb) in rows]
    in_specs += [pl.BlockSpec((1,) + v.shape[1:], lambda i: (grp(i), 0, 0)) for v in vecs]
    out_specs = [pl.BlockSpec((tm, w), lambda i: (i, 0)) for (w, _) in outs]
    out_specs += [pl.BlockSpec((1, 1, w), lambda i: (grp(i), 0, 0)) for w in accs]
    out_shape = [SDS((t_rows, w), dt) for (w, dt) in outs] + [SDS((n_groups, 1, w), F32) for w in accs]
    n_r, n_v, n_o = len(rows), len(vecs), len(outs)
    extra, aliases = [], {}
    if into is not None:
        extra, aliases = [into[0]], {n_r + n_v: 0}
        in_specs.append(pl.BlockSpec(memory_space=pl.ANY))
        out_specs[0] = pl.BlockSpec((tm, outs[0][0]), lambda i: (i, into[1]))
        out_shape[0] = SDS(into[0].shape, into[0].dtype)
    n_in = n_r + n_v + len(extra)

    def body(*refs):
        row_vals = [r[...] for r in refs[:n_r]]
        vec_vals = [v[0] for v in refs[n_r:n_r + n_v]]
        out_refs = refs[n_in:n_in + n_o]
        acc_refs = refs[n_in + n_o:]
        out_vals, acc_vals = fn(row_vals, vec_vals)
        for o_ref, o in zip(out_refs, out_vals):
            o_ref[...] = o.astype(o_ref.dtype)
        if acc_refs:
            i = pl.program_id(0)
            first = (i == 0) | (i == nlt) if n_groups == 2 else i == 0

            @pl.when(first)
            def _():
                for a_ref, a in zip(acc_refs, acc_vals):
                    a_ref[0] = a

            @pl.when(jnp.logical_not(first))
            def _():
                for a_ref, a in zip(acc_refs, acc_vals):
                    a_ref[0] += a

    res = pl.pallas_call(
        body, name=name, grid=(nt,), in_specs=in_specs, out_specs=out_specs, out_shape=out_shape,
        input_output_aliases=aliases, compiler_params=_cparams(),
    )(*[r[0] for r in rows], *vecs, *extra)
    return res[:n_o], res[n_o:]


RESIDENT_BYTES = 12 * 1024 * 1024


def _mm(name, pairs, mode, out_dtype, tm_cap=256, tn_cap=512, bias=None):
    a0, b0 = pairs[0]
    if mode == "nn":
        m, n, dims = a0.shape[0], b0.shape[1], NN
    elif mode == "nt":
        m, n, dims = a0.shape[0], b0.shape[0], NT
    else:
        m, n, dims = a0.shape[1], b0.shape[1], TN
    b_bytes = sum(b.size * b.dtype.itemsize for _, b in pairs)
    tn = n if b_bytes <= RESIDENT_BYTES else _tile(n, tn_cap, 128)
    tm = _tile(m, tm_cap, 128 if mode == "tn" else 16)

    def a_spec(a):
        if mode == "tn":
            return pl.BlockSpec((a.shape[0], tm), lambda i, j: (0, i))
        return pl.BlockSpec((tm, a.shape[1]), lambda i, j: (i, 0))

    def b_spec(b):
        if mode == "nt":
            return pl.BlockSpec((tn, b.shape[1]), lambda i, j: (j, 0))
        return pl.BlockSpec((b.shape[0], tn), lambda i, j: (0, j))

    in_specs, flat = [], []
    for a, b in pairs:
        in_specs += [a_spec(a), b_spec(b)]
        flat += [a, b]
    if bias is not None:
        in_specs.append(pl.BlockSpec((1, tn), lambda i, j: (0, j)))
        flat.append(bias)
    n_pairs = len(pairs)

    def body(*refs):
        acc = None
        for p in range(n_pairs):
            t = _dot(refs[2 * p][...], refs[2 * p + 1][...], dims)
            acc = t if acc is None else acc + t
        if bias is not None:
            acc = acc + refs[2 * n_pairs][...]
        refs[-1][...] = acc.astype(refs[-1].dtype)

    return pl.pallas_call(
        body, name=name, grid=(m // tm, n // tn), in_specs=in_specs,
        out_specs=pl.BlockSpec((tm, tn), lambda i, j: (i, j)),
        out_shape=SDS((m, n), out_dtype), compiler_params=_cparams(),
    )(*flat)


def _mm_resid(name, pairs, s, mg, k, coef, n_lat):
    t_rows, n = pairs[0][0].shape[0], s.shape[1]
    n_pairs = len(pairs)
    tm = _tile(math.gcd(n_lat, t_rows), 256, 16)
    nlt = n_lat // tm
    n_groups = 2 if nlt < t_rows // tm else 1

    def grp(i):
        return jnp.where(i >= nlt, 1, 0) if n_groups == 2 else 0

    def body(*refs):
        s_ref, mg_ref, so_ref, o_ref = refs[2 * n_pairs:]
        o = _dot(refs[0][...], refs[n_pairs][...], NN)
        for p in range(1, n_pairs):
            o = o + _dot(refs[p][...], refs[n_pairs + p][...], NN)
        gate = mg_ref[0, 3 * k + 2:3 * k + 3, :]
        o_ref[...] = o.astype(BF16)
        so_ref[...] = s_ref[...] + (coef * gate) * o

    row = pl.BlockSpec((tm, n), lambda i: (i, 0))
    return pl.pallas_call(
        body, name=name, grid=(t_rows // tm,),
        in_specs=[pl.BlockSpec((tm, a.shape[1]), lambda i: (i, 0)) for a, _ in pairs]
        + [pl.BlockSpec(b.shape, lambda i: (0, 0)) for _, b in pairs]
        + [row, pl.BlockSpec((1, mg.shape[1], n), lambda i: (grp(i), 0, 0))],
        out_specs=[row, row], out_shape=[SDS((t_rows, n), F32), SDS((t_rows, n), BF16)], compiler_params=_cparams(),
    )(*[a for a, _ in pairs], *[b for _, b in pairs], s, mg)


def _dw_pair(name, a1, a2, b):
    kk, m = a1.shape
    n = b.shape[1]
    tm = _tile(m, 256, 128)

    def body(a1_ref, a2_ref, b_ref, o1_ref, o2_ref):
        bb = b_ref[...]
        o1_ref[...] = _dot(a1_ref[...], bb, TN).astype(BF16)
        o2_ref[...] = _dot(a2_ref[...], bb, TN).astype(BF16)

    col = pl.BlockSpec((kk, tm), lambda i: (0, i))
    out = pl.BlockSpec((tm, n), lambda i: (i, 0))
    return pl.pallas_call(
        body, name=name, grid=(m // tm,), in_specs=[col, col, pl.BlockSpec(b.shape, lambda i: (0, 0))],
        out_specs=[out, out], out_shape=[SDS((m, n), BF16)] * 2, compiler_params=_cparams(),
    )(a1, a2, b)


def _groups(t_rows, tm, n_lat):
    nlt = n_lat // tm
    if nlt < t_rows // tm:
        return 2, (lambda i: jnp.where(i >= nlt, 1, 0)), (lambda i: (i == 0) | (i == nlt))
    return 1, (lambda i: 0), (lambda i: i == 0)


def _accumulate(acc_refs, vals, first):
    @pl.when(first)
    def _():
        for r, v in zip(acc_refs, vals):
            r[0] = v

    @pl.when(jnp.logical_not(first))
    def _():
        for r, v in zip(acc_refs, vals):
            r[0] += v


def _adaln_math(s, m, k):
    n, _ = _rms(s)
    return (n * m[9 + k:10 + k]) * (1.0 + m[3 * k + 1:3 * k + 2]) + m[3 * k:3 * k + 1]


def _ffn_up(name, s, mg, k, n_lat, wg_t, wu_t):
    t_rows, f = s.shape[0], wg_t.shape[0]
    tm = _row_tm(t_rows, n_lat)
    _, grp, _ = _groups(t_rows, tm, n_lat)

    def body(s_ref, mg_ref, wg_ref, wu_ref, u_ref, a_ref, b_ref, h_ref):
        uu = _adaln_math(s_ref[...], mg_ref[0], k).astype(BF16)
        u_ref[...] = uu
        a = _dot(uu, wg_ref[...], NT)
        b = _dot(uu, wu_ref[...], NT)
        sg = jax.nn.sigmoid(a)
        act = a * sg
        a_ref[...] = (b * (sg * (1.0 + a * (1.0 - sg)))).astype(BF16)
        b_ref[...] = act.astype(BF16)
        h_ref[...] = (act * b).astype(BF16)

    w_spec = pl.BlockSpec(wg_t.shape, lambda i: (0, 0))
    o_spec = pl.BlockSpec((tm, f), lambda i: (i, 0))
    row = pl.BlockSpec((tm, s.shape[1]), lambda i: (i, 0))
    return pl.pallas_call(
        body, name=name, grid=(t_rows // tm,),
        in_specs=[row, pl.BlockSpec((1,) + mg.shape[1:], lambda i: (grp(i), 0, 0)), w_spec, w_spec],
        out_specs=[row, o_spec, o_spec, o_spec],
        out_shape=[SDS(s.shape, BF16)] + [SDS((t_rows, f), BF16)] * 3, compiler_params=_cparams(),
    )(s, mg, wg_t, wu_t)


def _ffn_dact(name, ds_out, o, mg, k, coef, n_lat, wd, a, b):
    t_rows, f = ds_out.shape[0], wd.shape[0]
    tm = _row_tm(t_rows, n_lat)
    n_groups, grp, first = _groups(t_rows, tm, n_lat)
    d = ds_out.shape[1]

    def body(ds_ref, o_ref, mg_ref, wd_ref, a_ref, b_ref, do_ref, da_ref, db_ref, dg_ref):
        dd = coef * ds_ref[...]
        do = (dd * mg_ref[0, 3 * k + 2:3 * k + 3, :]).astype(BF16)
        do_ref[...] = do
        _accumulate([dg_ref], [_colsum(dd * o_ref[...].astype(F32))], first(pl.program_id(0)))
        dh = _dot(do, wd_ref[...], NT)
        da_ref[...] = (dh * a_ref[...].astype(F32)).astype(BF16)
        db_ref[...] = (dh * b_ref[...].astype(F32)).astype(BF16)

    row = pl.BlockSpec((tm, d), lambda i: (i, 0))
    t_spec = pl.BlockSpec((tm, f), lambda i: (i, 0))
    return pl.pallas_call(
        body, name=name, grid=(t_rows // tm,),
        in_specs=[row, row, pl.BlockSpec((1,) + mg.shape[1:], lambda i: (grp(i), 0, 0)),
                  pl.BlockSpec(wd.shape, lambda i: (0, 0)), t_spec, t_spec],
        out_specs=[row, t_spec, t_spec, pl.BlockSpec((1, 1, d), lambda i: (grp(i), 0, 0))],
        out_shape=[SDS((t_rows, d), BF16), SDS((t_rows, f), BF16), SDS((t_rows, f), BF16), SDS((n_groups, 1, d), F32)],
        compiler_params=_cparams(),
    )(ds_out, o, mg, wd, a, b)


def _du_adaln(name, pairs, s, ds_out, mg, k, n_lat, after, out_rows=None):
    t_rows, d = s.shape
    tm = _row_tm(t_rows, n_lat)
    n_groups, grp, first = _groups(t_rows, tm, n_lat)
    n_pairs = len(pairs)
    nt, n_ds, n_out = t_rows // tm, ds_out.shape[0] // tm, (out_rows or t_rows) // tm

    def body(*refs):
        s_ref, ds_ref, mg_ref, z_ref, out_ref, dsh_ref, dsc_ref, dgn_ref = refs[2 * n_pairs:]
        i = pl.program_id(0)
        d_u = z_ref[...]
        for p in range(n_pairs):
            d_u = d_u + _dot(refs[p][...], refs[n_pairs + p][...], NN)
        m = mg_ref[0]
        gain, scale = m[9 + k:10 + k], m[3 * k + 1:3 * k + 2]
        n, r = _rms(s_ref[...])
        dxn = d_u * (1.0 + scale)
        ds_in = _rms_bwd(n, r, dxn * gain)
        ds_in = ds_in + (ds_ref[...] if n_ds == nt else jnp.where(i < n_ds, ds_ref[...], 0.0))
        if n_out == nt:
            out_ref[...] = ds_in
        else:
            @pl.when(i < n_out)
            def _():
                out_ref[...] = ds_in
        _accumulate([dsh_ref, dsc_ref, dgn_ref], [_colsum(d_u), _colsum(d_u * (n * gain)), _colsum(dxn * n)], first(i))

    row = pl.BlockSpec((tm, d), lambda i: (i, 0))
    acc = pl.BlockSpec((1, 1, d), lambda i: (grp(i), 0, 0))
    res = pl.pallas_call(
        body, name=name, grid=(t_rows // tm,),
        in_specs=[pl.BlockSpec((tm, a.shape[1]), lambda i: (i, 0)) for a, _ in pairs]
        + [pl.BlockSpec(w.shape, lambda i: (0, 0)) for _, w in pairs]
        + [row, pl.BlockSpec((tm, d), lambda i: (jnp.minimum(i, n_ds - 1), 0)),
           pl.BlockSpec((1,) + mg.shape[1:], lambda i: (grp(i), 0, 0)), pl.BlockSpec((1, d), lambda i: (0, 0))],
        out_specs=[pl.BlockSpec((tm, d), lambda i: (jnp.minimum(i, n_out - 1), 0)), acc, acc, acc],
        out_shape=[SDS((n_out * tm, d), F32)] + [SDS((n_groups, 1, d), F32)] * 3, compiler_params=_cparams(),
    )(*[a for a, _ in pairs], *[w for _, w in pairs], s, ds_out, mg, after)
    return res[0], res[1:]


def _adaln_mm(name, s, mg, k, n_lat, w_t):
    rows, d = s.shape
    tm = _row_tm(rows, n_lat)
    _, grp, _ = _groups(rows, tm, n_lat)
    n = w_t.shape[0]

    def body(s_ref, mg_ref, w_ref, u_ref, y_ref):
        uu = _adaln_math(s_ref[...], mg_ref[0], k).astype(BF16)
        u_ref[...] = uu
        y_ref[...] = _dot(uu, w_ref[...], NT)

    row = pl.BlockSpec((tm, d), lambda i: (i, 0))
    return pl.pallas_call(
        body, name=name, grid=(rows // tm,),
        in_specs=[row, pl.BlockSpec((1,) + mg.shape[1:], lambda i: (grp(i), 0, 0)), pl.BlockSpec(w_t.shape, lambda i: (0, 0))],
        out_specs=[row, pl.BlockSpec((tm, n), lambda i: (i, 0))],
        out_shape=[SDS((rows, d), BF16), SDS((rows, n), F32)], compiler_params=_cparams(),
    )(s, mg, w_t)


def _gate_mm(name, ds_out, o, mg, k, coef, n_lat, w):
    t_rows, d = ds_out.shape
    tm = _row_tm(t_rows, n_lat)
    n_groups, grp, first = _groups(t_rows, tm, n_lat)
    n = w.shape[0]

    def body(ds_ref, o_ref, mg_ref, w_ref, do_ref, y_ref, dg_ref):
        dd = coef * ds_ref[...]
        do = (dd * mg_ref[0, 3 * k + 2:3 * k + 3, :]).astype(BF16)
        do_ref[...] = do
        _accumulate([dg_ref], [_colsum(dd * o_ref[...].astype(F32))], first(pl.program_id(0)))
        y_ref[...] = _dot(do, w_ref[...], NT)

    row = pl.BlockSpec((tm, d), lambda i: (i, 0))
    return pl.pallas_call(
        body, name=name, grid=(t_rows // tm,),
        in_specs=[row, row, pl.BlockSpec((1,) + mg.shape[1:], lambda i: (grp(i), 0, 0)), pl.BlockSpec(w.shape, lambda i: (0, 0))],
        out_specs=[row, pl.BlockSpec((tm, n), lambda i: (i, 0)), pl.BlockSpec((1, 1, d), lambda i: (grp(i), 0, 0))],
        out_shape=[SDS((t_rows, d), BF16), SDS((t_rows, n), F32), SDS((n_groups, 1, d), F32)],
        compiler_params=_cparams(),
    )(ds_out, o, mg, w)


def _row_tm(t_rows, n_lat):
    return _tile(math.gcd(t_rows, n_lat), 256, 16)


def _rmsnorm_fwd(name, x, width, colblk, gain, t_rows):
    def fn(rv, vv):
        n, _ = _rms(rv[0])
        return [n * vv[0]], []

    (y,), _ = _rowwise(name, fn, t_rows, _tile(t_rows, 256, 16), t_rows, [(x, width, colblk)],
                       [gain.reshape(1, 1, width)], [(width, BF16)], [])
    return y


def _rmsnorm_bwd(name, x, width, colblk, dy, gain, t_rows, out_dtype=F32, into=None):
    def fn(rv, vv):
        n, r = _rms(rv[0])
        return [_rms_bwd(n, r, rv[1] * vv[0])], [_colsum(rv[1] * n)]

    (dx,), (dgain,) = _rowwise(name, fn, t_rows, _tile(t_rows, 256, 16), t_rows,
                               [(x, width, colblk), (dy, width, 0)], [gain.reshape(1, 1, width)],
                               [(width, out_dtype)], [width], into)
    return dx, dgain


def _final_loss(name, h, target, gain):
    t_rows = h.shape[0]
    inv_d = 1.0 / D_MODEL

    def fn(rv, vv):
        g = vv[0]
        n, r = _rms(rv[0])
        e = n * g - rv[1]
        dy = e * inv_d
        return [_rms_bwd(n, r, dy * g)], [_colsum(e * e), _colsum(dy * n)]

    (dh,), (sq, dgain) = _rowwise(name, fn, t_rows, _tile(t_rows, 256, 16), t_rows,
                                  [(h, D_MODEL, 0), (target, D_MODEL, 0)], [gain.reshape(1, 1, D_MODEL)],
                                  [(D_MODEL, F32)], [D_MODEL, D_MODEL])
    return dh, sq, dgain


def _exact_dot(x, m_ref):
    hi = x.astype(BF16)
    lo = (x - hi.astype(F32)).astype(BF16)
    return _dot(hi, m_ref[...], NN) + _dot(lo, m_ref[...], NN)


def _rope(name, z, width, colblk, cos32, sin32, layout, backward, out_dtype, remap=None, into=None):
    t_rows = cos32.shape[0]
    expand, plain, perm = layout
    w_in = remap.shape[1] if (remap is not None and backward) else width
    w_out = remap.shape[1] if (remap is not None and not backward) else width
    extra = [] if remap is None else [remap.T if backward else remap]
    dest = [] if into is None else [into[0]]

    def body(z_ref, c_ref, s_ref, e_ref, m_ref, p_ref, *rest):
        o_ref = rest[-1]
        zz = z_ref[...]
        if remap is not None and backward:
            zz = _exact_dot(zz, rest[0])
        cos = _exact_dot(c_ref[...], e_ref) + m_ref[...]
        sin = _exact_dot(s_ref[...], e_ref)
        rot = _exact_dot(zz * sin if backward else zz, p_ref)
        if not backward:
            rot = rot * sin
        res = zz * cos + rot
        if remap is not None and not backward:
            res = _dot(res.astype(BF16), rest[0][...], NN)
        o_ref[...] = res.astype(o_ref.dtype)

    tm = _tile(t_rows, 256, 16)
    f_spec = pl.BlockSpec((tm, QK_ROPE), lambda i: (i, 0))
    return pl.pallas_call(
        body, name=name, grid=(t_rows // tm,),
        in_specs=[pl.BlockSpec((tm, w_in), lambda i: (i, colblk)), f_spec, f_spec,
                  pl.BlockSpec((QK_ROPE, width), lambda i: (0, 0)), pl.BlockSpec((1, width), lambda i: (0, 0)),
                  pl.BlockSpec((width, width), lambda i: (0, 0))]
        + [pl.BlockSpec(e.shape, lambda i: (0, 0)) for e in extra] + [pl.BlockSpec(memory_space=pl.ANY)] * len(dest),
        out_specs=pl.BlockSpec((tm, w_out), lambda i: (i, 0 if into is None else into[1])),
        out_shape=SDS((t_rows, w_out), out_dtype) if into is None else SDS(into[0].shape, into[0].dtype),
        input_output_aliases={} if into is None else {6 + len(extra): 0}, compiler_params=_cparams(),
    )(z, cos32, sin32, expand, plain, perm.T if backward else perm, *extra, *dest)


def _window_sum(x, w, transposed):
    n_rows = x.shape[0]
    zeros = jnp.zeros((POOL_PAD, x.shape[1]), F32)
    y = jnp.concatenate([zeros, x, zeros], axis=0)
    total = n_rows + 2 * POOL_PAD
    if transposed:
        y = y + pltpu.roll(y, total - 1, 0)
    else:
        y = y + pltpu.roll(y, 1, 0)
    step = 1
    while 2 * step < w:
        y = pltpu.roll(y, step, 0) + pltpu.roll(y, total - step, 0)
        step *= 2
    return y[POOL_PAD:POOL_PAD + n_rows]


def _window_count(n_rows, w):
    t = lax.broadcasted_iota(jnp.int32, (n_rows, 1), 0)
    lo = jnp.maximum(t - w // 2, 0)
    hi = jnp.minimum(t + (w - w // 2 - 1), n_rows - 1)
    return (hi - lo + 1).astype(F32)


def _pool_fwd(name, proj, n_rows, w_grp, scale):
    def body(x_ref, w_ref, sc_ref, y_ref, p_ref):
        for g, w in enumerate(POOL_WINDOWS):
            cols = slice(g * POOL_GROUP_DIM, (g + 1) * POOL_GROUP_DIM)
            x = x_ref[:, cols]
            p = _window_sum(x, w, False) * (1.0 / _window_count(n_rows, w)) - x
            pb = p.astype(BF16)
            p_ref[:, cols] = pb
            y_ref[:, cols] = (_dot(pb, w_ref[g], NN) * sc_ref[:, cols]).astype(BF16)

    blk = pl.BlockSpec((n_rows, POOL_DIM), lambda i: (0, 0))
    return pl.pallas_call(
        body, name=name, grid=(1,),
        in_specs=[blk, pl.BlockSpec(w_grp.shape, lambda i: (0, 0, 0)), pl.BlockSpec((1, POOL_DIM), lambda i: (0, 0))],
        out_specs=[blk, blk], out_shape=[SDS((n_rows, POOL_DIM), BF16)] * 2, compiler_params=_cparams(),
    )(proj, w_grp, scale)


def _pool_bwd(name, dcat, n_rows, p, w_grp, scale, into):
    def body(dy_ref, p_ref, w_ref, sc_ref, into_ref, dx_ref, dw_ref, dsc_ref):
        for g, w in enumerate(POOL_WINDOWS):
            cols = slice(g * POOL_GROUP_DIM, (g + 1) * POOL_GROUP_DIM)
            dy = dy_ref[:, cols]
            pb = p_ref[:, cols]
            pw = _dot(pb, w_ref[g], NN)
            dsc_ref[:, cols] = _colsum(dy * pw)
            dpw = (dy * sc_ref[:, cols]).astype(BF16)
            dw_ref[g] = _dot(pb, dpw, TN)
            dp = _dot(dpw, w_ref[g], NT)
            dx_ref[:, cols] = (_window_sum(dp * (1.0 / _window_count(n_rows, w)), w, True) - dp).astype(BF16)

    blk = pl.BlockSpec((n_rows, POOL_DIM), lambda i: (0, 0))
    w_spec = pl.BlockSpec(w_grp.shape, lambda i: (0, 0, 0))
    v_spec = pl.BlockSpec((1, POOL_DIM), lambda i: (0, 0))
    return pl.pallas_call(
        body, name=name, grid=(1,), in_specs=[blk, blk, w_spec, v_spec, pl.BlockSpec(memory_space=pl.ANY)],
        out_specs=[blk, w_spec, v_spec],
        out_shape=[SDS(into.shape, into.dtype), SDS(w_grp.shape, F32), SDS((1, POOL_DIM), F32)],
        input_output_aliases={4: 0}, compiler_params=_cparams(),
    )(dcat, p, w_grp, scale, into)


def _head_keys(kv_blk, k_rope):
    lane = lax.broadcasted_iota(jnp.int32, (1, HEAD_PAD), 1)
    return jnp.where(lane < QK_NOPE, kv_blk, k_rope)


def _attn_fwd(name, q, kv, k_rope, n_q):
    n_k = kv.shape[0]
    h = kv.shape[1] // HEAD_PAD
    tq = _tile(n_q, ATTN_Q_ROWS_FWD, 16)

    def body(q_ref, kv_ref, kr_ref, o_ref, lse_ref):
        kvb = kv_ref[...]
        s = _dot(q_ref[...], _head_keys(kvb, kr_ref[...]), NT) * ATTN_SCALE
        m = jnp.max(s, axis=-1, keepdims=True)
        e = jnp.exp(s - m)
        l = jnp.sum(e, axis=-1, keepdims=True)
        p = (e * (1.0 / l)).astype(BF16)
        lane = lax.broadcasted_iota(jnp.int32, (1, HEAD_PAD), 1)
        o_ref[...] = jnp.where(lane >= QK_NOPE, _dot(p, kvb, NN), 0.0).astype(BF16)
        lse_ref[...] = m + jnp.log(l)

    blk = pl.BlockSpec((tq, HEAD_PAD), lambda hh, i: (i, hh))
    return pl.pallas_call(
        body, name=name, grid=(h, n_q // tq),
        in_specs=[blk, pl.BlockSpec((n_k, HEAD_PAD), lambda hh, i: (0, hh)),
                  pl.BlockSpec((n_k, HEAD_PAD), lambda hh, i: (0, 0))],
        out_specs=[blk, pl.BlockSpec((None, tq, 1), lambda hh, i: (hh, i, 0))],
        out_shape=[SDS((n_q, h * HEAD_PAD), BF16), SDS((h, n_q, 1), F32)], compiler_params=_cparams(),
    )(q, kv, k_rope)


def _attn_bwd(name, q, kv, k_rope, o, lse, dy, dy_col0, n_q):
    n_k = kv.shape[0]
    h = kv.shape[1] // HEAD_PAD
    tq = _tile(n_q, ATTN_Q_ROWS_BWD, 16)
    n_i = n_q // tq

    def body(q_ref, kv_ref, kr_ref, o_ref, lse_ref, do_ref, dq_ref, dkv_ref, dkr_ref, acc_k, acc_v):
        hh, i = pl.program_id(0), pl.program_id(1)
        qq, kvb = q_ref[...], kv_ref[...]
        kk = _head_keys(kvb, kr_ref[...])
        d_o = do_ref[...]
        dd = d_o.astype(BF16)
        s = _dot(qq, kk, NT) * ATTN_SCALE
        p = jnp.exp(s - lse_ref[...])
        dp = _dot(dd, kvb, NT)
        delta = jnp.sum(d_o * o_ref[...].astype(F32), axis=-1, keepdims=True)
        ds = (p * (dp - delta) * ATTN_SCALE).astype(BF16)
        dq_ref[...] = _dot(ds, kk, NN)
        dk = _dot(ds, qq, TN)
        dv = _dot(p.astype(BF16), dd, TN)

        @pl.when(i == 0)
        def _():
            acc_k[...] = dk
            acc_v[...] = dv

        @pl.when(i > 0)
        def _():
            acc_k[...] += dk
            acc_v[...] += dv

        @pl.when(i == n_i - 1)
        def _():
            lane = lax.broadcasted_iota(jnp.int32, (1, HEAD_PAD), 1)
            dkv_ref[...] = jnp.where(lane < QK_NOPE, acc_k[...], acc_v[...]).astype(BF16)
            rope = jnp.where((lane >= QK_NOPE) & (lane < QK_HEAD), acc_k[...], 0.0)

            @pl.when(hh == 0)
            def _():
                dkr_ref[...] = rope

            @pl.when(hh > 0)
            def _():
                dkr_ref[...] += rope

    blk = pl.BlockSpec((tq, HEAD_PAD), lambda hh, i: (i, hh))
    kv_spec = pl.BlockSpec((n_k, HEAD_PAD), lambda hh, i: (0, hh))
    shared = pl.BlockSpec((n_k, HEAD_PAD), lambda hh, i: (0, 0))
    return pl.pallas_call(
        body, name=name, grid=(h, n_i),
        in_specs=[blk, kv_spec, shared, blk, pl.BlockSpec((None, tq, 1), lambda hh, i: (hh, i, 0)),
                  pl.BlockSpec((tq, HEAD_PAD), lambda hh, i: (i, dy_col0 + hh))],
        out_specs=[blk, kv_spec, shared],
        out_shape=[SDS((n_q, h * HEAD_PAD), F32), SDS((n_k, h * HEAD_PAD), BF16), SDS((n_k, HEAD_PAD), F32)],
        scratch_shapes=[pltpu.VMEM((n_k, HEAD_PAD), F32), pltpu.VMEM((n_k, HEAD_PAD), F32)],
        compiler_params=_cparams(),
    )(q, kv, k_rope, o, lse, dy)


CONV_COLS = 256


def _shift_rows(x, d):
    n_rows = x.shape[0]
    t = lax.broadcasted_iota(jnp.int32, (n_rows, 1), 0)
    if d > 0:
        return jnp.where(t >= d, pltpu.roll(x, d, 0), 0.0)
    return jnp.where(t < n_rows + d, pltpu.roll(x, n_rows + d, 0), 0.0)


def _conv_fwd(name, z3, conv_w):
    n_rows = z3.shape[0]
    nb = D_MODEL // CONV_COLS

    def body(b_ref, c_ref, v_ref, w_ref, y_ref):
        z = c_ref[...] * v_ref[...]
        zc = w_ref[0:1, :] * _shift_rows(z, 1) + w_ref[1:2, :] * z + w_ref[2:3, :] * _shift_rows(z, -1)
        y_ref[...] = (b_ref[...] * zc).astype(BF16)

    def part(k):
        return pl.BlockSpec((n_rows, CONV_COLS), lambda j: (0, k * nb + j))

    return pl.pallas_call(
        body, name=name, grid=(nb,),
        in_specs=[part(0), part(1), part(2), pl.BlockSpec((3, CONV_COLS), lambda j: (0, j))],
        out_specs=pl.BlockSpec((n_rows, CONV_COLS), lambda j: (0, j)),
        out_shape=SDS((n_rows, D_MODEL), BF16), compiler_params=_cparams(),
    )(z3, z3, z3, conv_w)


def _conv_bwd(name, dy, z3, conv_w):
    n_rows = z3.shape[0]
    nb = D_MODEL // CONV_COLS

    def body(dy_ref, b_ref, c_ref, v_ref, w_ref, db_ref, dc_ref, dv_ref, dw_ref):
        c, v, d_y = c_ref[...], v_ref[...], dy_ref[...]
        z = c * v
        z_dn, z_up = _shift_rows(z, 1), _shift_rows(z, -1)
        zc = w_ref[0:1, :] * z_dn + w_ref[1:2, :] * z + w_ref[2:3, :] * z_up
        db_ref[...] = (d_y * zc).astype(BF16)
        dzc = d_y * b_ref[...]
        dz = w_ref[0:1, :] * _shift_rows(dzc, -1) + w_ref[1:2, :] * dzc + w_ref[2:3, :] * _shift_rows(dzc, 1)
        dc_ref[...] = (dz * v).astype(BF16)
        dv_ref[...] = (dz * c).astype(BF16)
        dw_ref[0:1, :] = _colsum(dzc * z_dn)
        dw_ref[1:2, :] = _colsum(dzc * z)
        dw_ref[2:3, :] = _colsum(dzc * z_up)

    def part(k):
        return pl.BlockSpec((n_rows, CONV_COLS), lambda j: (0, k * nb + j))

    col = pl.BlockSpec((n_rows, CONV_COLS), lambda j: (0, j))
    w_spec = pl.BlockSpec((3, CONV_COLS), lambda j: (0, j))
    return pl.pallas_call(
        body, name=name, grid=(nb,), in_specs=[col, part(0), part(1), part(2), w_spec],
        out_specs=[col, col, col, w_spec],
        out_shape=[SDS((n_rows, D_MODEL), BF16)] * 3 + [SDS((3, D_MODEL), F32)], compiler_params=_cparams(),
    )(dy, z3, z3, z3, conv_w)


def _silu_rows(name, x):
    def body(x_ref, s_ref, d_ref):
        xx = x_ref[...]
        sg = jax.nn.sigmoid(xx)
        s_ref[...] = (xx * sg).astype(BF16)
        d_ref[...] = sg * (1.0 + xx * (1.0 - sg))

    return pl.pallas_call(body, name=name, out_shape=[SDS(x.shape, BF16), SDS(x.shape, F32)])(x)


def _sum_rows(name, x, scale=None):
    r, n = x.shape
    tn = _tile(n, 32768, 128)

    def body(*refs):
        acc = jnp.sum(refs[0][...].astype(F32), axis=0, keepdims=True)
        if scale is not None:
            acc = acc * refs[1][...]
        refs[-1][...] = acc

    in_specs = [pl.BlockSpec((r, tn), lambda j: (0, j))]
    args = [x]
    if scale is not None:
        in_specs.append(pl.BlockSpec((1, tn), lambda j: (0, j)))
        args.append(scale)
    return pl.pallas_call(body, name=name, grid=(n // tn,), in_specs=in_specs,
                          out_specs=pl.BlockSpec((1, tn), lambda j: (0, j)), out_shape=SDS((1, n), F32))(*args)


def _me_operand(me):
    return jnp.reshape(me, (1,)).astype(jnp.int32)


def _sum_slots(name, slots, src, me):
    n_slots, r, c = slots.shape
    tr = _tile(r, 432, 16)

    def body(me_ref, own_ref, x_ref, o_ref):
        acc = own_ref[...].astype(F32)
        for sl in range(n_slots):
            acc = acc + x_ref[sl].astype(F32)
        o_ref[...] = acc

    grid_spec = pltpu.PrefetchScalarGridSpec(
        num_scalar_prefetch=1, grid=(r // tr,),
        in_specs=[pl.BlockSpec((None, tr, c), lambda i, me_ref: (me_ref[0], i, 0)),
                  pl.BlockSpec((n_slots, tr, c), lambda i, me_ref: (0, i, 0))],
        out_specs=pl.BlockSpec((tr, c), lambda i, me_ref: (i, 0)))
    return pl.pallas_call(body, name=name, grid_spec=grid_spec, out_shape=SDS((r, c), F32),
                          compiler_params=_cparams())(_me_operand(me), src, slots)


def _adamw(name, w, g, m, v):
    shape = w.shape
    cols = shape[-1]
    rows = w.size // cols
    tr = _tile(rows, 512, 8)
    bc1 = 1.0 - ADAM_B1 ** ADAM_STEP
    bc2 = 1.0 - ADAM_B2 ** ADAM_STEP

    def body(w_ref, g_ref, m_ref, v_ref, d_ref, nm_ref, nv_ref):
        gg = g_ref[...]
        nm = ADAM_B1 * m_ref[...] + (1.0 - ADAM_B1) * gg
        nv = ADAM_B2 * v_ref[...] + (1.0 - ADAM_B2) * (gg * gg)
        nm_ref[...] = nm
        nv_ref[...] = nv
        d_ref[...] = -ADAM_LR * ((nm / bc1) / (jnp.sqrt(nv / bc2) + ADAM_EPS) + ADAM_WD * w_ref[...])

    spec = pl.BlockSpec((tr, cols), lambda i: (i, 0))
    outs = pl.pallas_call(body, name=name, grid=(rows // tr,), in_specs=[spec] * 4, out_specs=[spec] * 3,
                          out_shape=[SDS((rows, cols), F32)] * 3, compiler_params=_cparams())(
        w.reshape(rows, cols), g.reshape(rows, cols), m.reshape(rows, cols), v.reshape(rows, cols))
    return tuple(t.reshape(shape) for t in outs)


def _exchange(name, x, scatter, after=None):
    blk = x.shape[1:] if scatter else x.shape
    extra = [] if after is None else [after]

    def body(x_ref, *rest):
        out_ref, send_sems, recv_sems, local_sem = rest[len(extra):]
        mx, my, mc = lax.axis_index("x"), lax.axis_index("y"), lax.axis_index("c")
        me = 4 * mx + 2 * my + mc
        own = pltpu.make_async_copy(x_ref.at[me] if scatter else x_ref, out_ref.at[me], local_sem)
        own.start()
        copies = []
        for kk in range(1, N_DEV):
            px = jnp.bitwise_xor(mx, (kk >> 2) & 1)
            py = jnp.bitwise_xor(my, (kk >> 1) & 1)
            pc = jnp.bitwise_xor(mc, kk & 1)
            peer = 4 * px + 2 * py + pc
            send = pltpu.make_async_remote_copy(
                src_ref=x_ref.at[peer] if scatter else x_ref, dst_ref=out_ref.at[me],
                send_sem=send_sems.at[kk - 1], recv_sem=recv_sems.at[kk - 1],
                device_id=(px, py, pc), device_id_type=MESH)
            send.start()
            arrival = pltpu.make_async_remote_copy(
                src_ref=x_ref.at[peer] if scatter else x_ref, dst_ref=out_ref.at[peer],
                send_sem=send_sems.at[kk - 1], recv_sem=recv_sems.at[kk - 1],
                device_id=(px, py, pc), device_id_type=MESH)
            copies.append((send, arrival))
        for send, arrival in copies:
            arrival.wait_recv()
            send.wait_send()
        own.wait()

    return pl.pallas_call(
        body, name=name, out_shape=SDS((N_DEV,) + tuple(blk), x.dtype),
        in_specs=[pl.BlockSpec(memory_space=pl.ANY)] * (1 + len(extra)), out_specs=pl.BlockSpec(memory_space=pl.ANY),
        scratch_shapes=[pltpu.SemaphoreType.DMA((N_DEV - 1,)), pltpu.SemaphoreType.DMA((N_DEV - 1,)),
                        pltpu.SemaphoreType.DMA],
    )(x, *extra)


def _rope_perm(pre, reps, post):
    half = QK_ROPE // 4
    width = reps * (pre + QK_ROPE) + post
    p = np.zeros((width, width), np.float32)
    for rep in range(reps):
        s0 = rep * (pre + QK_ROPE) + pre
        for base in (s0, s0 + 2 * half):
            for i in range(half):
                p[base + half + i, base + i] = -1.0
                p[base + i, base + half + i] = 1.0
    return p


def _rope_layout(pre, reps, post):
    width = reps * (pre + QK_ROPE) + post
    expand = np.zeros((QK_ROPE, width), np.float32)
    plain = np.ones((1, width), np.float32)
    for rep in range(reps):
        s0 = rep * (pre + QK_ROPE) + pre
        expand[np.arange(QK_ROPE), s0 + np.arange(QK_ROPE)] = 1.0
        plain[0, s0:s0 + QK_ROPE] = 0.0
    return jnp.asarray(expand, BF16), jnp.asarray(plain, F32), jnp.asarray(_rope_perm(pre, reps, post), BF16)


def _head_spread():
    spread = np.zeros((HEADS * QK_HEAD, HEADS * HEAD_PAD), np.float32)
    for hh in range(HEADS):
        spread[hh * QK_HEAD + np.arange(QK_HEAD), hh * HEAD_PAD + np.arange(QK_HEAD)] = 1.0
    return jnp.asarray(spread, BF16)


def _rope_factors(n_lat, t_rows):
    half = QK_ROPE // 4
    pos = jnp.arange(n_lat)
    freqs = jnp.power(ROPE_THETA, -jnp.arange(0, 2 * half, 2, dtype=F32) / (2 * half))
    ang_r = (pos // GRID_W).astype(F32)[:, None] * freqs
    ang_c = (pos % GRID_W).astype(F32)[:, None] * freqs
    ang = jnp.concatenate([ang_r, ang_r, ang_c, ang_c], axis=-1)
    rest = t_rows - n_lat
    return (jnp.concatenate([jnp.cos(ang), jnp.ones((rest, QK_ROPE), F32)], axis=0),
            jnp.concatenate([jnp.sin(ang), jnp.zeros((rest, QK_ROPE), F32)], axis=0))


def _ffn_half_fwd(tag, s, mg, k, feed, i, coef, n_lat):
    wg_t, wu_t = feed.weights(f"{tag}_up", [f"gate_t{i}", f"up_t{i}"], s)
    u, a, b, hid = _ffn_up(f"{tag}_up", s, mg, k, n_lat, wg_t, wu_t)
    (wd,) = feed.weights(f"{tag}_down", [f"down{i}"], hid)
    s_out, o = _mm_resid(f"{tag}_down", [(hid, wd)], s, mg, k, coef, n_lat)
    return s_out, (s, u, a, b, hid, o, wg_t, wu_t, wd)


def _ffn_half_bwd(tag, ds_out, saved, mg, k, feed, i, coef, n_lat, out_rows=None):
    s, u, a, b, hid, o, wg_t, wu_t, wd = saved
    do, da, db, dgate = _ffn_dact(f"{tag}_dact", ds_out, o, mg, k, coef, n_lat, wd, a, b)
    dwd = _mm(f"{tag}_dwd", [(hid, do)], "tn", BF16)
    dwg_t, dwu_t = _dw_pair(f"{tag}_dwgu", da, db, u)
    token = feed.grads(tag, {f"down{i}": dwd, f"gate_t{i}": dwg_t, f"up_t{i}": dwu_t})
    ds_in, (dshift, dscale, dgain) = _du_adaln(f"{tag}_du", [(da, wg_t), (db, wu_t)], s, ds_out, mg, k, n_lat,
                                               _after(token), out_rows)
    return ds_in, dict(shift=dshift, scale=dscale, gate=dgate, gain=dgain)


def _after(token):
    return jnp.zeros((1, D_MODEL), F32) + token


def _mod_grad(parts, n_groups):
    rows = []
    zero = jnp.zeros((n_groups, 1, D_MODEL), F32)
    for k in range(3):
        for nm in ("shift", "scale", "gate"):
            t = parts[k].get(nm, zero)
            if t.shape[0] < n_groups:
                t = jnp.concatenate([t, jnp.zeros((n_groups - t.shape[0], 1, D_MODEL), F32)], axis=0)
            rows.append(t)
    return jnp.concatenate(rows, axis=1).reshape(n_groups, N_MOD * D_MODEL)


def _local_step(x, ctx, target, mod_h, mod_g, norm_g, feed, pool_w, pool_scale, q_norm_g, kv_norm_g, conv_w,
                final_norm_g):
    n_lat, n_ctx = x.shape[0], ctx.shape[0]
    t_all = n_lat + n_ctx
    mg0 = jnp.stack([jnp.concatenate([mod_h[0], norm_g[0]], axis=0), jnp.concatenate([mod_g, norm_g[0]], axis=0)])
    mg1 = jnp.concatenate([mod_h[1], norm_g[1]], axis=0)[None]

    s0 = jnp.concatenate([x, ctx], axis=0)
    s1, sv_f00 = _ffn_half_fwd("l0f0", s0, mg0, 0, feed, 0, 0.5, n_lat)

    (w_in,) = feed.weights("l0m_in", ["in_t"], s1)
    kv_rows = KV_RANK + QK_ROPE
    w_in_t = jnp.concatenate([
        w_in[:POOL_DIM], jnp.zeros((PA_CQ - POOL_DIM, D_MODEL), BF16), w_in[POOL_DIM:POOL_DIM + Q_RANK],
        w_in[POOL_DIM + Q_RANK:], jnp.zeros((PA_KV_W - kv_rows, D_MODEL), BF16)], axis=0)
    ua, proj = _adaln_mm("l0m_proj", s1, mg0, 1, n_lat, w_in_t)
    w_uq, w_ukv_t, w_ab_out = feed.weights("l0m_rest", ["uq", "ukv_t", "ab_out"], proj)
    pool_y, pool_p = _pool_fwd("l0m_pool", proj, n_lat, pool_w.astype(BF16), pool_scale)
    nq = _rmsnorm_fwd("l0m_qnorm", proj, Q_RANK, PA_CQ // Q_RANK, q_norm_g, n_lat)
    q_lin = _mm("l0m_q", [(nq, w_uq)], "nn", F32, 512, 768)
    cos32, sin32 = _rope_factors(n_lat, t_all)
    lay_q, lay_k = _rope_layout(QK_NOPE, HEADS, 0), _rope_layout(KV_RANK, 1, PA_KV_W - kv_rows)
    spread = _head_spread()
    q_flat = _rope("l0m_qrope", q_lin, Q_RANK, 0, cos32[:n_lat], sin32[:n_lat], lay_q, False, BF16, spread)
    kvr = _rope("l0m_krope", proj, PA_KV_W, PA_KV // PA_KV_W, cos32, sin32, lay_k, False, F32)
    nkv = _rmsnorm_fwd("l0m_kvnorm", kvr, KV_RANK, 0, kv_norm_g, t_all)
    kv = _mm("l0m_kv", [(nkv, w_ukv_t)], "nt", BF16, 768, 512)
    k_rope = jnp.pad(kvr[:, KV_RANK:KV_RANK + QK_ROPE].astype(BF16), ((0, 0), (QK_NOPE, HEAD_PAD - QK_HEAD)))
    o_flat, lse = _attn_fwd("l0m_attn", q_flat, kv, k_rope, n_lat)
    w_o_pad = jnp.pad(w_ab_out[POOL_DIM:].reshape(HEADS, V_HEAD, D_MODEL),
                      ((0, 0), (HEAD_PAD - V_HEAD, 0), (0, 0))).reshape(HEADS * HEAD_PAD, D_MODEL)
    w_o_pool = w_ab_out[:POOL_DIM]
    h2, mix_o = _mm_resid("l0m_out", [(pool_y, w_o_pool), (o_flat, w_o_pad)], s1, mg0[:1], 1, 1.0, n_lat)

    h3, sv_f01 = _ffn_half_fwd("l0f1", h2, mg0[:1], 2, feed, 1, 0.5, n_lat)

    h4, sv_f10 = _ffn_half_fwd("l1f0", h3, mg1, 0, feed, 2, 0.5, n_lat)
    w_cin_t, w_c_out = feed.weights("l1m", ["cin_t", "c_out"], h4)
    uc, z3 = _adaln_mm("l1m_in", h4, mg1, 1, n_lat, w_cin_t)
    yc = _conv_fwd("l1m_conv", z3, conv_w)
    h5, conv_o = _mm_resid("l1m_out", [(yc, w_c_out)], h4, mg1, 1, 1.0, n_lat)
    h6, sv_f11 = _ffn_half_fwd("l1f1", h5, mg1, 2, feed, 3, 0.5, n_lat)

    dh6, sq_cols, d_final_g = _final_loss("loss_head", h6, target, final_norm_g)
    g = {}
    dh5, g["f11"] = _ffn_half_bwd("l1f1", dh6, sv_f11, mg1, 2, feed, 3, 0.5, n_lat)

    do_c, dyc, dgate_c = _gate_mm("l1m_dy", dh5, conv_o, mg1, 1, 1.0, n_lat, w_c_out)
    d_c_out = _mm("l1m_dwout", [(yc, do_c)], "tn", BF16)
    db_, dc_, dv_, d_conv_w = _conv_bwd("l1m_dconv", dyc, z3, conv_w)
    dz3 = jnp.concatenate([db_, dc_, dv_], axis=-1)
    d_cin_t = _mm("l1m_dwin", [(dz3, uc)], "tn", BF16)
    token = feed.grads("l1m", {"c_out": d_c_out, "cin_t": d_cin_t})
    dh4, (dsh_c, dsc_c, dgn_c) = _du_adaln("l1m_du", [(dz3, w_cin_t)], h4, dh5, mg1, 1, n_lat, _after(token))
    dh3, g["f10"] = _ffn_half_bwd("l1f0", dh4, sv_f10, mg1, 0, feed, 2, 0.5, n_lat)

    dh2, g["f01"] = _ffn_half_bwd("l0f1", dh3, sv_f01, mg0[:1], 2, feed, 1, 0.5, n_lat)

    w_back = jnp.concatenate([w_o_pool, w_o_pad], axis=0)
    do_a, dcat, dgate_a = _gate_mm("l0m_dcat", dh2, mix_o, mg0[:1], 1, 1.0, n_lat, w_back)
    d_o_pad = _mm("l0m_dwout_a", [(o_flat, do_a)], "tn", BF16)
    d_ab_out = jnp.concatenate([
        _mm("l0m_dwout_p", [(pool_y, do_a)], "tn", BF16),
        d_o_pad.reshape(HEADS, HEAD_PAD, D_MODEL)[:, HEAD_PAD - V_HEAD:].reshape(HEADS * V_HEAD, D_MODEL)], axis=0)
    dproj = jnp.zeros((t_all, PA_W), BF16)
    dproj, d_pool_w, d_pool_scale = _pool_bwd("l0m_dpool", dcat, n_lat, pool_p, pool_w.astype(BF16), pool_scale, dproj)
    dq_flat, dkv, dk_rope = _attn_bwd("l0m_dattn", q_flat, kv, k_rope, o_flat, lse, dcat, POOL_DIM // HEAD_PAD, n_lat)
    dq_lin = _rope("l0m_dqrope", dq_flat, Q_RANK, 0, cos32[:n_lat], sin32[:n_lat], lay_q, True, BF16, spread)
    d_uq = _mm("l0m_dwuq", [(nq, dq_lin)], "tn", BF16, 768, 768)
    dnq = _mm("l0m_dnq", [(dq_lin, w_uq)], "nt", F32, 512, 768)
    dproj, d_q_norm_g = _rmsnorm_bwd("l0m_dqnorm", proj, Q_RANK, PA_CQ // Q_RANK, dnq, q_norm_g, n_lat, BF16,
                                     (dproj, PA_CQ // Q_RANK))
    dnkv = _mm("l0m_dnkv", [(dkv, w_ukv_t)], "nn", F32, 768, 256)
    d_ukv_t = _mm("l0m_dwukv", [(dkv, nkv)], "tn", BF16, 512, 256)
    dckv, d_kv_norm_g = _rmsnorm_bwd("l0m_dkvnorm", kvr, KV_RANK, 0, dnkv, kv_norm_g, t_all)
    dkvr = jnp.concatenate([dckv, dk_rope[:, QK_NOPE:QK_HEAD],
                            jnp.zeros((t_all, PA_KV_W - KV_RANK - QK_ROPE), F32)], axis=-1)
    dproj = _rope("l0m_dkrope", dkvr, PA_KV_W, 0, cos32, sin32, lay_k, True, BF16, None, (dproj, PA_KV // PA_KV_W))
    d_in_pad = _mm("l0m_dwin", [(dproj, ua)], "tn", BF16, 640, 512)
    d_in_t = jnp.concatenate([d_in_pad[:POOL_DIM], d_in_pad[PA_CQ:PA_CQ + Q_RANK],
                              d_in_pad[PA_KV:PA_KV + kv_rows]], axis=0)
    token = feed.grads("l0m", {"ab_out": d_ab_out, "uq": d_uq, "ukv_t": d_ukv_t, "in_t": d_in_t})
    ds1, (dsh_a, dsc_a, dgn_a) = _du_adaln("l0m_du", [(dproj, w_in_t)], s1, dh2, mg0, 1, n_lat, _after(token))
    grad_x, g["f00"] = _ffn_half_bwd("l0f0", ds1, sv_f00, mg0, 0, feed, 0, 0.5, n_lat, out_rows=n_lat)

    dmod0 = _mod_grad([g["f00"], dict(shift=dsh_a, scale=dsc_a, gate=dgate_a), g["f01"]], 2)
    dmod1 = _mod_grad([g["f10"], dict(shift=dsh_c, scale=dsc_c, gate=dgate_c), g["f11"]], 1)
    d_norm_g = jnp.stack([
        jnp.concatenate([jnp.sum(g["f00"]["gain"], axis=0), jnp.sum(dgn_a, axis=0), g["f01"]["gain"][0]], axis=0),
        jnp.concatenate([g["f10"]["gain"][0], dgn_c[0], g["f11"]["gain"][0]], axis=0)])
    grads = dict(
        pool_w=d_pool_w, pool_scale=d_pool_scale, q_norm_g=d_q_norm_g[0], kv_norm_g=d_kv_norm_g[0],
        conv_w=d_conv_w, final_norm_g=d_final_g[0], norm_g=d_norm_g,
        mod_h=jnp.stack([dmod0[0], dmod1[0]]), mod_g=dmod0[1])
    return sq_cols, grad_x, grads


HBM_SPEC = pl.BlockSpec(memory_space=pltpu.HBM)
SEM_SPEC = pl.BlockSpec(memory_space=pltpu.SEMAPHORE)
ANY_SPEC = pl.BlockSpec(memory_space=pl.ANY)
SIDE_EFFECT = pltpu.SideEffectType.DATAFLOW_SIDE_EFFECTING
N_PEERS = N_DEV - 1


def _mesh_place():
    mx, my, mc = lax.axis_index("x"), lax.axis_index("y"), lax.axis_index("c")
    return mx, my, mc, 4 * mx + 2 * my + mc


def _peer(place, kk):
    mx, my, mc, _ = place
    px = jnp.bitwise_xor(mx, (kk >> 2) & 1)
    py = jnp.bitwise_xor(my, (kk >> 1) & 1)
    pc = jnp.bitwise_xor(mc, kk & 1)
    return (px, py, pc), 4 * px + 2 * py + pc


def _hbm(a):
    return pltpu.with_memory_space_constraint(a, pltpu.HBM)


def _landing(block, me):
    zone = lax.empty((N_DEV,) + block.shape, block.dtype)
    return lax.dynamic_update_slice(zone, block[None], (me,) + (0,) * block.ndim)


ALL_PEERS = tuple(range(1, N_DEV))
SIBLING = 1
CHIP_PEERS = (2, 4, 6)
RELAYED = (3, 5, 7)


def _exchange_start(name, srcs, lands, scatter, after, peers=ALL_PEERS):
    n = len(srcs)
    extra = [] if after is None else [after]

    def body(*refs):
        src, land = refs[:n], refs[n:2 * n]
        send_sems, recv_sems, token = refs[2 * n + len(extra)], refs[2 * n + len(extra) + 1], refs[-1]
        place = _mesh_place()
        for a in range(n):
            for kk in peers:
                dev, peer = _peer(place, kk)
                pltpu.make_async_remote_copy(
                    src_ref=src[a].at[peer] if scatter else src[a],
                    dst_ref=land[a].at[kk - 1] if scatter else land[a].at[place[3]],
                    send_sem=send_sems.at[a * N_PEERS + kk - 1], recv_sem=recv_sems.at[a * N_PEERS + kk - 1],
                    device_id=dev, device_id_type=MESH).start()
        token[...] = jnp.zeros_like(token)

    thru = [pltpu.HBM(t.shape, t.dtype) for t in (*srcs, *lands)]
    res = pl.pallas_call(
        body, name=name,
        out_shape=(pltpu.SemaphoreType.DMA((n * N_PEERS,)), pltpu.SemaphoreType.DMA((n * N_PEERS,)), *thru,
                   SDS((8, 128), F32)),
        in_specs=[HBM_SPEC] * (2 * n) + [ANY_SPEC] * len(extra),
        out_specs=(SEM_SPEC, SEM_SPEC, *([HBM_SPEC] * (2 * n)), pl.BlockSpec(memory_space=pltpu.VMEM)),
        input_output_aliases={i: 2 + i for i in range(2 * n)},
        compiler_params=pltpu.CompilerParams(has_side_effects=SIDE_EFFECT),
    )(*[_hbm(s) for s in srcs], *[_hbm(t) for t in lands], *extra)
    return res[0], res[1], list(res[2:2 + n]), list(res[2 + n:2 + 2 * n]), res[-1]


def _exchange_wait(name, send_sems, recv_sems, srcs, lands, places, scatter, after):
    n = len(srcs)

    def body(*refs):
        src, land = refs[:n], refs[n:2 * n]
        send, recv = refs[2 * n], refs[2 * n + 1]
        place = _mesh_place()
        for a in range(n):
            for kk in range(1, N_DEV):
                dev, peer = _peer(place, kk)
                cp = pltpu.make_async_remote_copy(
                    src_ref=src[a].at[peer] if scatter else src[a],
                    dst_ref=land[a].at[kk - 1] if scatter else land[a].at[peer],
                    send_sem=send.at[places[a] * N_PEERS + kk - 1], recv_sem=recv.at[places[a] * N_PEERS + kk - 1],
                    device_id=dev, device_id_type=MESH)
                cp.wait_send()
                cp.wait_recv()

    thru = [pltpu.HBM(t.shape, t.dtype) for t in (*srcs, *lands)]
    res = pl.pallas_call(
        body, name=name, out_shape=tuple(thru),
        in_specs=[HBM_SPEC] * (2 * n) + [SEM_SPEC, SEM_SPEC] + [ANY_SPEC] * len(after),
        out_specs=tuple([HBM_SPEC] * (2 * n)), input_output_aliases={i: i for i in range(2 * n)},
        compiler_params=pltpu.CompilerParams(has_side_effects=SIDE_EFFECT),
    )(*srcs, *lands, send_sems, recv_sems, *after)
    return list(res[:n]), list(res[n:])


def _gather_relay(name, send1, recv1, lands, places, after):
    n = len(lands)

    def body(*refs):
        land, s1, r1 = refs[:n], refs[n], refs[n + 1]
        s2, r2 = refs[n + 3], refs[n + 4]
        place = _mesh_place()
        sibling = _peer(place, SIBLING)[0]
        for a in range(n):
            for j, kk in enumerate(CHIP_PEERS):
                dev, origin = _peer(place, kk)
                block = land[a].at[origin]
                pltpu.make_async_remote_copy(
                    src_ref=block, dst_ref=block, send_sem=s1.at[places[a] * N_PEERS + kk - 1],
                    recv_sem=r1.at[places[a] * N_PEERS + kk - 1], device_id=dev, device_id_type=MESH).wait_recv()
                pltpu.make_async_remote_copy(
                    src_ref=block, dst_ref=block, send_sem=s2.at[a * 3 + j], recv_sem=r2.at[a * 3 + j],
                    device_id=sibling, device_id_type=MESH).start()

    res = pl.pallas_call(
        body, name=name,
        out_shape=(pltpu.SemaphoreType.DMA((3 * n,)), pltpu.SemaphoreType.DMA((3 * n,)),
                   *[pltpu.HBM(t.shape, t.dtype) for t in lands]),
        in_specs=[HBM_SPEC] * n + [SEM_SPEC, SEM_SPEC, ANY_SPEC],
        out_specs=(SEM_SPEC, SEM_SPEC, *([HBM_SPEC] * n)),
        input_output_aliases={i: 2 + i for i in range(n)},
        compiler_params=pltpu.CompilerParams(has_side_effects=SIDE_EFFECT),
    )(*lands, send1, recv1, after)
    return res[0], res[1], list(res[2:])


def _gather_wait(name, send1, recv1, send2, recv2, srcs, lands, places, after):
    n = len(lands)

    def body(*refs):
        src, land = refs[:n], refs[n:2 * n]
        s1, r1, s2, r2 = refs[2 * n:2 * n + 4]
        place = _mesh_place()
        for a in range(n):
            for kk in (SIBLING,) + CHIP_PEERS:
                dev, origin = _peer(place, kk)
                first = pltpu.make_async_remote_copy(
                    src_ref=src[a], dst_ref=land[a].at[origin], send_sem=s1.at[places[a] * N_PEERS + kk - 1],
                    recv_sem=r1.at[places[a] * N_PEERS + kk - 1], device_id=dev, device_id_type=MESH)
                first.wait_send()
                if kk == SIBLING:
                    first.wait_recv()
            for j, kk in enumerate(CHIP_PEERS):
                dev, origin = _peer(place, kk + 1)
                relay = pltpu.make_async_remote_copy(
                    src_ref=src[a], dst_ref=land[a].at[origin], send_sem=s2.at[a * 3 + j], recv_sem=r2.at[a * 3 + j],
                    device_id=dev, device_id_type=MESH)
                relay.wait_send()
                relay.wait_recv()

    arrays = (*srcs, *lands)
    res = pl.pallas_call(
        body, name=name, out_shape=tuple(pltpu.HBM(t.shape, t.dtype) for t in arrays),
        in_specs=[HBM_SPEC] * (2 * n) + [SEM_SPEC] * 4 + [ANY_SPEC], out_specs=tuple([HBM_SPEC] * (2 * n)),
        input_output_aliases={i: i for i in range(2 * n)},
        compiler_params=pltpu.CompilerParams(has_side_effects=SIDE_EFFECT),
    )(*arrays, send1, recv1, send2, recv2, after)
    return list(res[n:])


class _Feed:
    def __init__(self, shards, groups, me):
        self.shards, self.groups, self.me, self.pos = shards, groups, me, 0
        self.sems, self.srcs, self.lands = {}, {}, {}
        self.relays = {}
        self.pending = []

    def start(self, tag, names, after):
        srcs = [self.shards[nm] for nm in names]
        lands = [_landing(s, self.me) for s in srcs]
        send, recv, srcs, lands, self.token = _exchange_start(
            f"gather_start_{tag}", srcs, lands, False, after, (SIBLING,) + CHIP_PEERS)
        for i, nm in enumerate(names):
            self.sems[nm], self.srcs[nm], self.lands[nm] = (send, recv, i), srcs[i], lands[i]
        return self.token

    def _relay(self, gi, after):
        names = self.groups[gi]
        if gi not in self.relays:
            send, recv, _ = self.sems[names[0]]
            places = [self.sems[nm][2] for nm in names]
            send2, recv2, lands = _gather_relay(f"gather_relay_{gi}", send, recv, [self.lands[nm] for nm in names],
                                                places, after)
            for nm, t in zip(names, lands):
                self.lands[nm] = t
            self.relays[gi] = (send2, recv2)
            after = lands[0]
        return after

    def weights(self, tag, names, after):
        gi = self.pos
        assert names == self.groups[gi], (names, self.groups[gi])
        if gi == 0:
            after = self.token
        self._relay(gi, after)
        if 1 <= gi < len(self.groups) - 1:
            after = self._relay(gi + 1, after)
        send2, recv2 = self.relays[gi]
        send, recv, _ = self.sems[names[0]]
        got = _gather_wait(f"gather_wait_{tag}", send, recv, send2, recv2, [self.srcs[nm] for nm in names],
                           [self.lands[nm] for nm in names], [self.sems[nm][2] for nm in names], after)
        self.pos += 1
        return [t.reshape((N_DEV * t.shape[1],) + t.shape[2:]) for t in got]

    def grads(self, tag, full):
        names = list(full)
        srcs = [full[nm].reshape((N_DEV, full[nm].shape[0] // N_DEV) + full[nm].shape[1:]) for nm in names]
        lands = [lax.empty((N_PEERS,) + s.shape[1:], s.dtype) for s in srcs]
        send, recv, srcs, lands, token = _exchange_start(f"scatter_start_{tag}", srcs, lands, True, None)
        self.pending.append((tag, names, send, recv, srcs, lands))
        return token[0, 0]

    def collect(self, tags, after, keep_slots=()):
        out = {}
        for tag, names, send, recv, srcs, lands in self.pending:
            if tag not in tags:
                continue
            srcs, got = _exchange_wait(f"scatter_wait_{tag}", send, recv, srcs, lands, list(range(len(names))), True,
                                       after)
            for nm, slots, src in zip(names, got, srcs):
                out[nm] = ((slots, src) if nm.startswith(tuple(keep_slots))
                           else _sum_slots(f"reduce_{nm}", slots, src, self.me))
        return out


def _adamw_math(w, gg, m, v):
    nm = ADAM_B1 * m + (1.0 - ADAM_B1) * gg
    nv = ADAM_B2 * v + (1.0 - ADAM_B2) * (gg * gg)
    bc1 = 1.0 - ADAM_B1 ** ADAM_STEP
    bc2 = 1.0 - ADAM_B2 ** ADAM_STEP
    return -ADAM_LR * ((nm / bc1) / (jnp.sqrt(nv / bc2) + ADAM_EPS) + ADAM_WD * w), nm, nv


def _adamw_part(name, i, w, scattered, me, m, v, prev):
    n_parts, rows, cols = w.shape
    tr = _tile(rows, 256, 16)
    if prev is None:
        prev = tuple(lax.empty(w.shape, F32) for _ in range(4))

    slots, src = scattered

    def body(me_ref, w_ref, g_ref, own_ref, m_ref, v_ref, *rest):
        go_ref, d_ref, nm_ref, nv_ref = rest[4:]
        gg = own_ref[...].astype(F32)
        for sl in range(N_PEERS):
            gg = gg + g_ref[sl].astype(F32)
        d, nm, nv = _adamw_math(w_ref[...], gg, m_ref[...], v_ref[...])
        go_ref[...] = gg
        d_ref[...] = d
        nm_ref[...] = nm
        nv_ref[...] = nv

    part = pl.BlockSpec((None, tr, cols), lambda r, me_ref: (i, r, 0))
    grid_spec = pltpu.PrefetchScalarGridSpec(
        num_scalar_prefetch=1, grid=(rows // tr,),
        in_specs=[part, pl.BlockSpec((N_PEERS, tr, cols), lambda r, me_ref: (0, r, 0)),
                  pl.BlockSpec((None, tr, cols), lambda r, me_ref: (me_ref[0], r, 0)), part, part] + [ANY_SPEC] * 4,
        out_specs=[part] * 4)
    return pl.pallas_call(
        body, name=name, grid_spec=grid_spec, out_shape=[SDS(w.shape, F32)] * 4,
        input_output_aliases={6 + k: k for k in range(4)}, compiler_params=_cparams(),
    )(_me_operand(me), w, slots, src, m, v, *prev)


WEIGHT_NAMES = ("c_ctx", "norm_g", "w_mod", "b_mod", "ffn_w_gate", "ffn_w_up", "ffn_w_down", "ab_w_in", "pool_w",
                "pool_scale", "q_norm_g", "w_uq", "kv_norm_g", "w_ukv", "ab_w_out", "conv_w_in", "conv_w",
                "conv_w_out", "final_norm_g")


def kernel(x, c, ctx, c_ctx, norm_g, w_mod, b_mod, ffn_w_gate, ffn_w_up, ffn_w_down, ab_w_in, pool_w, pool_scale, q_norm_g, w_uq, kv_norm_g, w_ukv, ab_w_out, conv_w_in, conv_w, conv_w_out, final_norm_g, loss_target, m_c_ctx, m_norm_g, m_w_mod, m_b_mod, m_ffn_w_gate, m_ffn_w_up, m_ffn_w_down, m_ab_w_in, m_pool_w, m_pool_scale, m_q_norm_g, m_w_uq, m_kv_norm_g, m_w_ukv, m_ab_w_out, m_conv_w_in, m_conv_w, m_conv_w_out, m_final_norm_g, v_c_ctx, v_norm_g, v_w_mod, v_b_mod, v_ffn_w_gate, v_ffn_w_up, v_ffn_w_down, v_ab_w_in, v_pool_w, v_pool_scale, v_q_norm_g, v_w_uq, v_kv_norm_g, v_w_ukv, v_ab_w_out, v_conv_w_in, v_conv_w, v_conv_w_out, v_final_norm_g):
    weights = (c_ctx, norm_g, w_mod, b_mod, ffn_w_gate, ffn_w_up, ffn_w_down, ab_w_in, pool_w, pool_scale, q_norm_g,
               w_uq, kv_norm_g, w_ukv, ab_w_out, conv_w_in, conv_w, conv_w_out, final_norm_g)
    moms = (m_c_ctx, m_norm_g, m_w_mod, m_b_mod, m_ffn_w_gate, m_ffn_w_up, m_ffn_w_down, m_ab_w_in, m_pool_w,
            m_pool_scale, m_q_norm_g, m_w_uq, m_kv_norm_g, m_w_ukv, m_ab_w_out, m_conv_w_in, m_conv_w, m_conv_w_out,
            m_final_norm_g)
    vels = (v_c_ctx, v_norm_g, v_w_mod, v_b_mod, v_ffn_w_gate, v_ffn_w_up, v_ffn_w_down, v_ab_w_in, v_pool_w,
            v_pool_scale, v_q_norm_g, v_w_uq, v_kv_norm_g, v_w_ukv, v_ab_w_out, v_conv_w_in, v_conv_w, v_conv_w_out,
            v_final_norm_g)
    me = 4 * lax.axis_index("x") + 2 * lax.axis_index("y") + lax.axis_index("c")
    n_lat, n_ctx = x.shape[1], ctx.shape[1]
    d = D_MODEL
    mod_cols = w_mod.shape[-1]
    ng_sh, cw_sh = norm_g.shape[-1], conv_w.shape[-1]

    def ffn_shards(i):
        return {f"gate_t{i}": ffn_w_gate[i // 2, i % 2].T, f"up_t{i}": ffn_w_up[i // 2, i % 2].T,
                f"down{i}": ffn_w_down[i // 2, i % 2]}

    local = {**ffn_shards(0), "in_t": ab_w_in[0].T, "uq": w_uq[0], "ukv_t": w_ukv[0].T, "ab_out": ab_w_out[0],
             **ffn_shards(1), **ffn_shards(2), "cin_t": conv_w_in[0].T, "c_out": conv_w_out[0], **ffn_shards(3)}
    ffn_groups = [[[f"gate_t{i}", f"up_t{i}"], [f"down{i}"]] for i in range(4)]
    groups = [*ffn_groups[0], ["in_t"], ["uq", "ukv_t", "ab_out"], *ffn_groups[1], *ffn_groups[2],
              ["cin_t", "c_out"], *ffn_groups[3]]
    feed = _Feed({nm: a.astype(BF16) for nm, a in local.items()}, groups, me)

    small = jnp.concatenate([c.reshape(-1), norm_g.reshape(-1), conv_w.reshape(-1)])
    small_n = -(-small.shape[0] // 1024) * 1024
    small = jnp.pad(small, (0, small_n - small.shape[0])).reshape(small_n // 128, 128)
    small_all = _exchange("gather_small", small, False).reshape(N_DEV, small_n)
    c_all = small_all[:, :d]
    o1 = d + 6 * ng_sh
    norm_g_full = small_all[:, d:o1].reshape(N_DEV, 2, 3, ng_sh).transpose(1, 2, 0, 3).reshape(2, 3, d)
    conv_w_full = small_all[:, o1:o1 + 3 * cw_sh].reshape(N_DEV, 3, cw_sh).transpose(1, 0, 2).reshape(3, d)

    cond = jnp.concatenate([c_all, jnp.broadcast_to(c_ctx[None, :], (N_DEV, d))], axis=0)
    sil, dsil = _silu_rows("mod_silu", cond)
    w_mod_b = w_mod.astype(BF16)
    b_sh = lax.dynamic_slice(b_mod, (0, me * mod_cols), (2, mod_cols))
    m_part = jnp.stack([_mm(f"mod_fwd{l}", [(sil, w_mod_b[l])], "nn", F32, 16, 384, bias=b_sh[l:l + 1])
                        for l in range(2)], axis=1)
    m_all = _exchange("gather_mod", m_part.reshape(-1, 128), False).reshape(N_DEV, 2 * N_DEV, 2, mod_cols)
    m_mine = lax.dynamic_index_in_dim(m_all, me, axis=1, keepdims=False)
    mod_h = m_mine.transpose(1, 0, 2).reshape(2, N_MOD, d)
    mod_g = m_all[:, N_DEV, 0, :].reshape(N_MOD, d)

    first = feed.start("first", [nm for grp in groups[:3] for nm in grp], m_all)
    feed.start("rest", [nm for grp in groups[3:] for nm in grp], first)

    sq_cols, grad_x, g = _local_step(x[0], ctx[0], loss_target[0], mod_h, mod_g, norm_g_full, feed, pool_w[0],
                                  pool_scale, q_norm_g, kv_norm_g, conv_w_full, final_norm_g)
    w_of, m_of, v_of = (dict(zip(WEIGHT_NAMES, t)) for t in (weights, moms, vels))
    results = {}

    def update(nm, grad, view=lambda t: t):
        outs = _adamw(f"adamw_{nm}", view(w_of[nm]), grad.reshape(view(w_of[nm]).shape), view(m_of[nm]), view(v_of[nm]))
        results[nm] = tuple(view(t) for t in (grad.reshape(view(w_of[nm]).shape), *outs))

    def swap(t):
        return jnp.swapaxes(t, -1, -2)

    stacked = ("gate_t", "up_t", "down")
    dm = jnp.stack([g["mod_h"], jnp.stack([g["mod_g"], jnp.zeros_like(g["mod_g"])])])
    dm_rows = dm.reshape(-1, 128)
    dm_send, dm_recv, dm_srcs, dm_lands, dm_started = _exchange_start(
        "dmod_start", [dm_rows], [_landing(dm_rows, me)], False, None)
    early = feed.collect(["l1f1", "l1m", "l1f0", "l0f1", "l0m"], [grad_x, dm_started], stacked)
    update("ab_w_in", early["in_t"], swap)
    update("w_uq", early["uq"])
    update("w_ukv", early["ukv_t"].T)
    update("ab_w_out", early["ab_out"])
    update("conv_w_in", early["cin_t"].T)
    update("conv_w_out", early["c_out"])
    ffn = {}
    for nm, prefix, view in (("ffn_w_gate", "gate_t", swap), ("ffn_w_up", "up_t", swap),
                             ("ffn_w_down", "down", lambda t: t)):
        w4, m4, v4 = (view(t).reshape((4,) + view(t).shape[-2:]) for t in (w_of[nm], m_of[nm], v_of[nm]))
        prev = None
        for i in (3, 2, 1):
            prev = _adamw_part(f"adamw_{nm}{i}", i, w4, early[f"{prefix}{i}"], me, m4, v4, prev)
        ffn[nm] = (prefix, view, w4, m4, v4, prev)
    done_early = [results[nm][1] for nm in results] + [state[5][1] for state in ffn.values()]
    late = feed.collect(["l0f0"], done_early, stacked)
    for nm, (prefix, view, w4, m4, v4, prev) in ffn.items():
        outs = _adamw_part(f"adamw_{nm}0", 0, w4, late[f"{prefix}0"], me, m4, v4, prev)
        results[nm] = tuple(view(t.reshape(view(w_of[nm]).shape)) for t in outs)

    _, dm_got = _exchange_wait("dmod_wait", dm_send, dm_recv, dm_srcs, dm_lands, [0], False,
                               [results["ffn_w_down"][1]])
    dm_all = dm_got[0].reshape(N_DEV, 2, 2, N_MOD * d)
    grad_b_mod = _sum_rows("dmod_bias", dm_all.reshape(2 * N_DEV, 2 * N_MOD * d)).reshape(2, N_MOD * d)
    dm_sh = lax.dynamic_slice(dm_all, (0, 0, 0, me * mod_cols), (N_DEV, 2, 2, mod_cols))
    gw_mod, cctx_parts = [], []
    for l in range(2):
        dm_l = dm_sh[:, :, l, :].transpose(1, 0, 2).reshape(2 * N_DEV, mod_cols).astype(BF16)
        gw_mod.append(_mm(f"mod_dw{l}", [(sil, dm_l)], "tn", F32, 512, 384))
        dm_ctx = jnp.concatenate([dm_l[N_DEV:], jnp.zeros((N_DEV, mod_cols), BF16)], axis=0)
        cctx_parts.append(_mm(f"mod_dcond{l}", [(dm_ctx, w_mod_b[l])], "nt", F32, 16, 512))
    cctx_part = _sum_rows("mod_dcond_sum", jnp.concatenate(cctx_parts, axis=0))
    update("w_mod", jnp.stack(gw_mod))
    update("b_mod", grad_b_mod)

    small_g = jnp.concatenate([g["pool_w"].reshape(-1), g["pool_scale"].reshape(-1), g["q_norm_g"].reshape(-1),
                               g["kv_norm_g"].reshape(-1), g["final_norm_g"].reshape(-1), g["norm_g"].reshape(-1),
                               g["conv_w"].reshape(-1), sq_cols.reshape(-1), cctx_part.reshape(-1)])
    sizes = [pool_w.size, pool_scale.size, q_norm_g.size, kv_norm_g.size, d, 6 * d, 3 * d, d, d]
    sg_n = -(-small_g.shape[0] // 1024) * 1024
    small_g = jnp.pad(small_g, (0, sg_n - small_g.shape[0]))
    sg_all = _exchange("gather_small_grads", small_g.reshape(-1, 128), False).reshape(N_DEV, sg_n)
    scale_vec = jnp.concatenate([jnp.ones((1, sum(sizes[:-1])), F32), dsil[N_DEV:N_DEV + 1],
                                 jnp.ones((1, sg_n - sum(sizes)), F32)], axis=1)
    sg = _sum_rows("small_grads_sum", sg_all, scale_vec)[0]
    cuts, pos = [], 0
    for sz in sizes:
        cuts.append(sg[pos:pos + sz])
        pos += sz
    g_pool_w, g_pool_scale, g_q_norm, g_kv_norm, g_final, g_norm_full, g_conv_full, sq_all, g_c_ctx = cuts
    loss = 0.5 * jnp.sum(sq_all) / d
    update("c_ctx", g_c_ctx)
    update("norm_g", lax.dynamic_slice(g_norm_full.reshape(2, 3, d), (0, 0, me * ng_sh), (2, 3, ng_sh)))
    update("conv_w", lax.dynamic_slice(g_conv_full.reshape(3, d), (0, me * cw_sh), (3, cw_sh)))
    update("pool_w", g_pool_w)
    update("pool_scale", g_pool_scale)
    update("q_norm_g", g_q_norm)
    update("kv_norm_g", g_kv_norm)
    update("final_norm_g", g_final)
    outs = [results[nm] for nm in WEIGHT_NAMES]
    return (loss, grad_x[None], *[o[0] for o in outs], *[o[1] for o in outs], *[o[2] for o in outs],
            *[o[3] for o in outs])
```

```python
import functools
import math

import jax
import jax.numpy as jnp
import numpy as np
from jax import lax
from jax.experimental import pallas as pl
from jax.experimental.pallas import tpu as pltpu

F32 = jnp.float32
BF16 = jnp.bfloat16
MESH = pl.DeviceIdType.MESH
SDS = jax.ShapeDtypeStruct

N_DEV = 8
D_MODEL = 1024
N_MOD = 9
D_FF = 2816
POOL_WINDOWS = (2, 4, 8, 16)
POOL_DIM = 512
POOL_GROUP_DIM = 128
HEADS = 8
QK_NOPE = 64
QK_ROPE = 32
QK_HEAD = QK_NOPE + QK_ROPE
V_HEAD = 64
Q_RANK = 768
KV_RANK = 256
GRID_W = 64
ROPE_THETA = 10000.0
RMS_EPS = 1e-6
ATTN_SCALE = 1.0 / math.sqrt(QK_HEAD)
HEAD_PAD = 128
POOL_PAD = 16
ATTN_Q_ROWS_FWD = 256
ATTN_Q_ROWS_BWD = 1024
PA_POOL, PA_CQ, PA_KV = 0, 768, 1536
PA_KV_W = 384
PA_W = PA_KV + PA_KV_W

ADAM_LR, ADAM_B1, ADAM_B2, ADAM_EPS, ADAM_WD, ADAM_STEP = 0.001, 0.9, 0.999, 1e-08, 0.01, 10

VMEM_LIMIT_BYTES = 56 * 1024 * 1024

NN = ((1,), (0,))
NT = ((1,), (1,))
TN = ((0,), (0,))


def _cparams():
    return pltpu.CompilerParams(vmem_limit_bytes=VMEM_LIMIT_BYTES)


def _dot(a, b, dims):
    return lax.dot_general(a, b, (dims, ((), ())), preferred_element_type=F32)


def _tile(n, cap, mult=8):
    t = (min(cap, n) // mult) * mult
    while t >= mult:
        if n % t == 0:
            return t
        t -= mult
    return n


def _colsum(x):
    return jnp.sum(x, axis=0, keepdims=True)


def _rms(x):
    r = lax.rsqrt(jnp.mean(x * x, axis=-1, keepdims=True) + RMS_EPS)
    return x * r, r


def _rms_bwd(n, r, dn):
    return r * (dn - n * jnp.mean(dn * n, axis=-1, keepdims=True))


def _rowwise(name, fn, t_rows, tm, n_lat, rows, vecs, outs, accs, into=None):
    nt = t_rows // tm
    nlt = n_lat // tm
    n_groups = 2 if nlt < nt else 1

    def grp(i):
        return jnp.where(i >= nlt, 1, 0) if n_groups == 2 else 0

    in_specs = [pl.BlockSpec((tm, w), functools.partial(lambda i, cb: (i, cb), cb=cb)) for (_, w, cb) in rows]
    in_specs += [pl.BlockSpec((1,) + v.shape[1:], lambda i: (grp(i), 0, 0)) for v in vecs]
    out_specs = [pl.BlockSpec((tm, w), lambda i: (i, 0)) for (w, _) in outs]
    out_specs += [pl.BlockSpec((1, 1, w), lambda i: (grp(i), 0, 0)) for w in accs]
    out_shape = [SDS((t_rows, w), dt) for (w, dt) in outs] + [SDS((n_groups, 1, w), F32) for w in accs]
    n_r, n_v, n_o = len(rows), len(vecs), len(outs)
    extra, aliases = [], {}
    if into is not None:
        extra, aliases = [into[0]], {n_r + n_v: 0}
        in_specs.append(pl.BlockSpec(memory_space=pl.ANY))
        out_specs[0] = pl.BlockSpec((tm, outs[0][0]), lambda i: (i, into[1]))
        out_shape[0] = SDS(into[0].shape, into[0].dtype)
    n_in = n_r + n_v + len(extra)

    def body(*refs):
        row_vals = [r[...] for r in refs[:n_r]]
        vec_vals = [v[0] for v in refs[n_r:n_r + n_v]]
        out_refs = refs[n_in:n_in + n_o]
        acc_refs = refs[n_in + n_o:]
        out_vals, acc_vals = fn(row_vals, vec_vals)
        for o_ref, o in zip(out_refs, out_vals):
            o_ref[...] = o.astype(o_ref.dtype)
        if acc_refs:
            i = pl.program_id(0)
            first = (i == 0) | (i == nlt) if n_groups == 2 else i == 0

            @pl.when(first)
            def _():
                for a_ref, a in zip(acc_refs, acc_vals):
                    a_ref[0] = a

            @pl.when(jnp.logical_not(first))
            def _():
                for a_ref, a in zip(acc_refs, acc_vals):
                    a_ref[0] += a

    res = pl.pallas_call(
        body, name=name, grid=(nt,), in_specs=in_specs, out_specs=out_specs, out_shape=out_shape,
        input_output_aliases=aliases, compiler_params=_cparams(),
    )(*[r[0] for r in rows], *vecs, *extra)
    return res[:n_o], res[n_o:]


RESIDENT_BYTES = 12 * 1024 * 1024


def _mm(name, pairs, mode, out_dtype, tm_cap=256, tn_cap=512, bias=None):
    a0, b0 = pairs[0]
    if mode == "nn":
        m, n, dims = a0.shape[0], b0.shape[1], NN
    elif mode == "nt":
        m, n, dims = a0.shape[0], b0.shape[0], NT
    else:
        m, n, dims = a0.shape[1], b0.shape[1], TN
    b_bytes = sum(b.size * b.dtype.itemsize for _, b in pairs)
    tn = n if b_bytes <= RESIDENT_BYTES else _tile(n, tn_cap, 128)
    tm = _tile(m, tm_cap, 128 if mode == "tn" else 16)

    def a_spec(a):
        if mode == "tn":
            return pl.BlockSpec((a.shape[0], tm), lambda i, j: (0, i))
        return pl.BlockSpec((tm, a.shape[1]), lambda i, j: (i, 0))

    def b_spec(b):
        if mode == "nt":
            return pl.BlockSpec((tn, b.shape[1]), lambda i, j: (j, 0))
        return pl.BlockSpec((b.shape[0], tn), lambda i, j: (0, j))

    in_specs, flat = [], []
    for a, b in pairs:
        in_specs += [a_spec(a), b_spec(b)]
        flat += [a, b]
    if bias is not None:
        in_specs.append(pl.BlockSpec((1, tn), lambda i, j: (0, j)))
        flat.append(bias)
    n_pairs = len(pairs)

    def body(*refs):
        acc = None
        for p in range(n_pairs):
            t = _dot(refs[2 * p][...], refs[2 * p + 1][...], dims)
            acc = t if acc is None else acc + t
        if bias is not None:
            acc = acc + refs[2 * n_pairs][...]
        refs[-1][...] = acc.astype(refs[-1].dtype)

    return pl.pallas_call(
        body, name=name, grid=(m // tm, n // tn), in_specs=in_specs,
        out_specs=pl.BlockSpec((tm, tn), lambda i, j: (i, j)),
        out_shape=SDS((m, n), out_dtype), compiler_params=_cparams(),
    )(*flat)


def _mm_resid(name, pairs, s, mg, k, coef, n_lat):
    t_rows, n = pairs[0][0].shape[0], s.shape[1]
    n_pairs = len(pairs)
    tm = _tile(math.gcd(n_lat, t_rows), 256, 16)
    nlt = n_lat // tm
    n_groups = 2 if nlt < t_rows // tm else 1

    def grp(i):
        return jnp.where(i >= nlt, 1, 0) if n_groups == 2 else 0

    def body(*refs):
        s_ref, mg_ref, so_ref, o_ref = refs[2 * n_pairs:]
        o = _dot(refs[0][...], refs[n_pairs][...], NN)
        for p in range(1, n_pairs):
            o = o + _dot(refs[p][...], refs[n_pairs + p][...], NN)
        gate = mg_ref[0, 3 * k + 2:3 * k + 3, :]
        o_ref[...] = o.astype(BF16)
        so_ref[...] = s_ref[...] + (coef * gate) * o

    row = pl.BlockSpec((tm, n), lambda i: (i, 0))
    return pl.pallas_call(
        body, name=name, grid=(t_rows // tm,),
        in_specs=[pl.BlockSpec((tm, a.shape[1]), lambda i: (i, 0)) for a, _ in pairs]
        + [pl.BlockSpec(b.shape, lambda i: (0, 0)) for _, b in pairs]
        + [row, pl.BlockSpec((1, mg.shape[1], n), lambda i: (grp(i), 0, 0))],
        out_specs=[row, row], out_shape=[SDS((t_rows, n), F32), SDS((t_rows, n), BF16)], compiler_params=_cparams(),
    )(*[a for a, _ in pairs], *[b for _, b in pairs], s, mg)


def _dw_pair(name, a1, a2, b):
    kk, m = a1.shape
    n = b.shape[1]
    tm = _tile(m, 256, 128)

    def body(a1_ref, a2_ref, b_ref, o1_ref, o2_ref):
        bb = b_ref[...]
        o1_ref[...] = _dot(a1_ref[...], bb, TN).astype(BF16)
        o2_ref[...] = _dot(a2_ref[...], bb, TN).astype(BF16)

    col = pl.BlockSpec((kk, tm), lambda i: (0, i))
    out = pl.BlockSpec((tm, n), lambda i: (i, 0))
    return pl.pallas_call(
        body, name=name, grid=(m // tm,), in_specs=[col, col, pl.BlockSpec(b.shape, lambda i: (0, 0))],
        out_specs=[out, out], out_shape=[SDS((m, n), BF16)] * 2, compiler_params=_cparams(),
    )(a1, a2, b)


def _groups(t_rows, tm, n_lat):
    nlt = n_lat // tm
    if nlt < t_rows // tm:
        return 2, (lambda i: jnp.where(i >= nlt, 1, 0)), (lambda i: (i == 0) | (i == nlt))
    return 1, (lambda i: 0), (lambda i: i == 0)


def _accumulate(acc_refs, vals, first):
    @pl.when(first)
    def _():
        for r, v in zip(acc_refs, vals):
            r[0] = v

    @pl.when(jnp.logical_not(first))
    def _():
        for r, v in zip(acc_refs, vals):
            r[0] += v


def _adaln_math(s, m, k):
    n, _ = _rms(s)
    return (n * m[9 + k:10 + k]) * (1.0 + m[3 * k + 1:3 * k + 2]) + m[3 * k:3 * k + 1]


def _ffn_up(name, s, mg, k, n_lat, wg_t, wu_t):
    t_rows, f = s.shape[0], wg_t.shape[0]
    tm = _row_tm(t_rows, n_lat)
    _, grp, _ = _groups(t_rows, tm, n_lat)

    def body(s_ref, mg_ref, wg_ref, wu_ref, u_ref, a_ref, b_ref, h_ref):
        uu = _adaln_math(s_ref[...], mg_ref[0], k).astype(BF16)
        u_ref[...] = uu
        a = _dot(uu, wg_ref[...], NT)
        b = _dot(uu, wu_ref[...], NT)
        sg = jax.nn.sigmoid(a)
        act = a * sg
        a_ref[...] = (b * (sg * (1.0 + a * (1.0 - sg)))).astype(BF16)
        b_ref[...] = act.astype(BF16)
        h_ref[...] = (act * b).astype(BF16)

    w_spec = pl.BlockSpec(wg_t.shape, lambda i: (0, 0))
    o_spec = pl.BlockSpec((tm, f), lambda i: (i, 0))
    row = pl.BlockSpec((tm, s.shape[1]), lambda i: (i, 0))
    return pl.pallas_call(
        body, name=name, grid=(t_rows // tm,),
        in_specs=[row, pl.BlockSpec((1,) + mg.shape[1:], lambda i: (grp(i), 0, 0)), w_spec, w_spec],
        out_specs=[row, o_spec, o_spec, o_spec],
        out_shape=[SDS(s.shape, BF16)] + [SDS((t_rows, f), BF16)] * 3, compiler_params=_cparams(),
    )(s, mg, wg_t, wu_t)


def _ffn_dact(name, ds_out, o, mg, k, coef, n_lat, wd, a, b):
    t_rows, f = ds_out.shape[0], wd.shape[0]
    tm = _row_tm(t_rows, n_lat)
    n_groups, grp, first = _groups(t_rows, tm, n_lat)
    d = ds_out.shape[1]

    def body(ds_ref, o_ref, mg_ref, wd_ref, a_ref, b_ref, do_ref, da_ref, db_ref, dg_ref):
        dd = coef * ds_ref[...]
        do = (dd * mg_ref[0, 3 * k + 2:3 * k + 3, :]).astype(BF16)
        do_ref[...] = do
        _accumulate([dg_ref], [_colsum(dd * o_ref[...].astype(F32))], first(pl.program_id(0)))
        dh = _dot(do, wd_ref[...], NT)
        da_ref[...] = (dh * a_ref[...].astype(F32)).astype(BF16)
        db_ref[...] = (dh * b_ref[...].astype(F32)).astype(BF16)

    row = pl.BlockSpec((tm, d), lambda i: (i, 0))
    t_spec = pl.BlockSpec((tm, f), lambda i: (i, 0))
    return pl.pallas_call(
        body, name=name, grid=(t_rows // tm,),
        in_specs=[row, row, pl.BlockSpec((1,) + mg.shape[1:], lambda i: (grp(i), 0, 0)),
                  pl.BlockSpec(wd.shape, lambda i: (0, 0)), t_spec, t_spec],
        out_specs=[row, t_spec, t_spec, pl.BlockSpec((1, 1, d), lambda i: (grp(i), 0, 0))],
        out_shape=[SDS((t_rows, d), BF16), SDS((t_rows, f), BF16), SDS((t_rows, f), BF16), SDS((n_groups, 1, d), F32)],
        compiler_params=_cparams(),
    )(ds_out, o, mg, wd, a, b)


def _du_adaln(name, pairs, s, ds_out, mg, k, n_lat, after, out_rows=None):
    t_rows, d = s.shape
    tm = _row_tm(t_rows, n_lat)
    n_groups, grp, first = _groups(t_rows, tm, n_lat)
    n_pairs = len(pairs)
    nt, n_ds, n_out = t_rows // tm, ds_out.shape[0] // tm, (out_rows or t_rows) // tm

    def body(*refs):
        s_ref, ds_ref, mg_ref, z_ref, out_ref, dsh_ref, dsc_ref, dgn_ref = refs[2 * n_pairs:]
        i = pl.program_id(0)
        d_u = z_ref[...]
        for p in range(n_pairs):
            d_u = d_u + _dot(refs[p][...], refs[n_pairs + p][...], NN)
        m = mg_ref[0]
        gain, scale = m[9 + k:10 + k], m[3 * k + 1:3 * k + 2]
        n, r = _rms(s_ref[...])
        dxn = d_u * (1.0 + scale)
        ds_in = _rms_bwd(n, r, dxn * gain)
        ds_in = ds_in + (ds_ref[...] if n_ds == nt else jnp.where(i < n_ds, ds_ref[...], 0.0))
        if n_out == nt:
            out_ref[...] = ds_in
        else:
            @pl.when(i < n_out)
            def _():
                out_ref[...] = ds_in
        _accumulate([dsh_ref, dsc_ref, dgn_ref], [_colsum(d_u), _colsum(d_u * (n * gain)), _colsum(dxn * n)], first(i))

    row = pl.BlockSpec((tm, d), lambda i: (i, 0))
    acc = pl.BlockSpec((1, 1, d), lambda i: (grp(i), 0, 0))
    res = pl.pallas_call(
        body, name=name, grid=(t_rows // tm,),
        in_specs=[pl.BlockSpec((tm, a.shape[1]), lambda i: (i, 0)) for a, _ in pairs]
        + [pl.BlockSpec(w.shape, lambda i: (0, 0)) for _, w in pairs]
        + [row, pl.BlockSpec((tm, d), lambda i: (jnp.minimum(i, n_ds - 1), 0)),
           pl.BlockSpec((1,) + mg.shape[1:], lambda i: (grp(i), 0, 0)), pl.BlockSpec((1, d), lambda i: (0, 0))],
        out_specs=[pl.BlockSpec((tm, d), lambda i: (jnp.minimum(i, n_out - 1), 0)), acc, acc, acc],
        out_shape=[SDS((n_out * tm, d), F32)] + [SDS((n_groups, 1, d), F32)] * 3, compiler_params=_cparams(),
    )(*[a for a, _ in pairs], *[w for _, w in pairs], s, ds_out, mg, after)
    return res[0], res[1:]


def _adaln_mm(name, s, mg, k, n_lat, w_t):
    rows, d = s.shape
    tm = _row_tm(rows, n_lat)
    _, grp, _ = _groups(rows, tm, n_lat)
    n = w_t.shape[0]

    def body(s_ref, mg_ref, w_ref, u_ref, y_ref):
        uu = _adaln_math(s_ref[...], mg_ref[0], k).astype(BF16)
        u_ref[...] = uu
        y_ref[...] = _dot(uu, w_ref[...], NT)

    row = pl.BlockSpec((tm, d), lambda i: (i, 0))
    return pl.pallas_call(
        body, name=name, grid=(rows // tm,),
        in_specs=[row, pl.BlockSpec((1,) + mg.shape[1:], lambda i: (grp(i), 0, 0)), pl.BlockSpec(w_t.shape, lambda i: (0, 0))],
        out_specs=[row, pl.BlockSpec((tm, n), lambda i: (i, 0))],
        out_shape=[SDS((rows, d), BF16), SDS((rows, n), F32)], compiler_params=_cparams(),
    )(s, mg, w_t)


def _gate_mm(name, ds_out, o, mg, k, coef, n_lat, w):
    t_rows, d = ds_out.shape
    tm = _row_tm(t_rows, n_lat)
    n_groups, grp, first = _groups(t_rows, tm, n_lat)
    n = w.shape[0]

    def body(ds_ref, o_ref, mg_ref, w_ref, do_ref, y_ref, dg_ref):
        dd = coef * ds_ref[...]
        do = (dd * mg_ref[0, 3 * k + 2:3 * k + 3, :]).astype(BF16)
        do_ref[...] = do
        _accumulate([dg_ref], [_colsum(dd * o_ref[...].astype(F32))], first(pl.program_id(0)))
        y_ref[...] = _dot(do, w_ref[...], NT)

    row = pl.BlockSpec((tm, d), lambda i: (i, 0))
    return pl.pallas_call(
        body, name=name, grid=(t_rows // tm,),
        in_specs=[row, row, pl.BlockSpec((1,) + mg.shape[1:], lambda i: (grp(i), 0, 0)), pl.BlockSpec(w.shape, lambda i: (0, 0))],
        out_specs=[row, pl.BlockSpec((tm, n), lambda i: (i, 0)), pl.BlockSpec((1, 1, d), lambda i: (grp(i), 0, 0))],
        out_shape=[SDS((t_rows, d), BF16), SDS((t_rows, n), F32), SDS((n_groups, 1, d), F32)],
        compiler_params=_cparams(),
    )(ds_out, o, mg, w)


def _row_tm(t_rows, n_lat):
    return _tile(math.gcd(t_rows, n_lat), 256, 16)


def _rmsnorm_fwd(name, x, width, colblk, gain, t_rows):
    def fn(rv, vv):
        n, _ = _rms(rv[0])
        return [n * vv[0]], []

    (y,), _ = _rowwise(name, fn, t_rows, _tile(t_rows, 256, 16), t_rows, [(x, width, colblk)],
                       [gain.reshape(1, 1, width)], [(width, BF16)], [])
    return y


def _rmsnorm_bwd(name, x, width, colblk, dy, gain, t_rows, out_dtype=F32, into=None):
    def fn(rv, vv):
        n, r = _rms(rv[0])
        return [_rms_bwd(n, r, rv[1] * vv[0])], [_colsum(rv[1] * n)]

    (dx,), (dgain,) = _rowwise(name, fn, t_rows, _tile(t_rows, 256, 16), t_rows,
                               [(x, width, colblk), (dy, width, 0)], [gain.reshape(1, 1, width)],
                               [(width, out_dtype)], [width], into)
    return dx, dgain


def _final_loss(name, h, target, gain):
    t_rows = h.shape[0]
    inv_d = 1.0 / D_MODEL

    def fn(rv, vv):
        g = vv[0]
        n, r = _rms(rv[0])
        e = n * g - rv[1]
        dy = e * inv_d
        return [_rms_bwd(n, r, dy * g)], [_colsum(e * e), _colsum(dy * n)]

    (dh,), (sq, dgain) = _rowwise(name, fn, t_rows, _tile(t_rows, 256, 16), t_rows,
                                  [(h, D_MODEL, 0), (target, D_MODEL, 0)], [gain.reshape(1, 1, D_MODEL)],
                                  [(D_MODEL, F32)], [D_MODEL, D_MODEL])
    return dh, sq, dgain


def _exact_dot(x, m_ref):
    hi = x.astype(BF16)
    lo = (x - hi.astype(F32)).astype(BF16)
    return _dot(hi, m_ref[...], NN) + _dot(lo, m_ref[...], NN)


def _rope(name, z, width, colblk, cos32, sin32, layout, backward, out_dtype, remap=None, into=None):
    t_rows = cos32.shape[0]
    expand, plain, perm = layout
    w_in = remap.shape[1] if (remap is not None and backward) else width
    w_out = remap.shape[1] if (remap is not None and not backward) else width
    extra = [] if remap is None else [remap.T if backward else remap]
    dest = [] if into is None else [into[0]]

    def body(z_ref, c_ref, s_ref, e_ref, m_ref, p_ref, *rest):
        o_ref = rest[-1]
        zz = z_ref[...]
        if remap is not None and backward:
            zz = _exact_dot(zz, rest[0])
        cos = _exact_dot(c_ref[...], e_ref) + m_ref[...]
        sin = _exact_dot(s_ref[...], e_ref)
        rot = _exact_dot(zz * sin if backward else zz, p_ref)
        if not backward:
            rot = rot * sin
        res = zz * cos + rot
        if remap is not None and not backward:
            res = _dot(res.astype(BF16), rest[0][...], NN)
        o_ref[...] = res.astype(o_ref.dtype)

    tm = _tile(t_rows, 256, 16)
    f_spec = pl.BlockSpec((tm, QK_ROPE), lambda i: (i, 0))
    return pl.pallas_call(
        body, name=name, grid=(t_rows // tm,),
        in_specs=[pl.BlockSpec((tm, w_in), lambda i: (i, colblk)), f_spec, f_spec,
                  pl.BlockSpec((QK_ROPE, width), lambda i: (0, 0)), pl.BlockSpec((1, width), lambda i: (0, 0)),
                  pl.BlockSpec((width, width), lambda i: (0, 0))]
        + [pl.BlockSpec(e.shape, lambda i: (0, 0)) for e in extra] + [pl.BlockSpec(memory_space=pl.ANY)] * len(dest),
        out_specs=pl.BlockSpec((tm, w_out), lambda i: (i, 0 if into is None else into[1])),
        out_shape=SDS((t_rows, w_out), out_dtype) if into is None else SDS(into[0].shape, into[0].dtype),
        input_output_aliases={} if into is None else {6 + len(extra): 0}, compiler_params=_cparams(),
    )(z, cos32, sin32, expand, plain, perm.T if backward else perm, *extra, *dest)


def _window_sum(x, w, transposed):
    n_rows = x.shape[0]
    zeros = jnp.zeros((POOL_PAD, x.shape[1]), F32)
    y = jnp.concatenate([zeros, x, zeros], axis=0)
    total = n_rows + 2 * POOL_PAD
    if transposed:
        y = y + pltpu.roll(y, total - 1, 0)
    else:
        y = y + pltpu.roll(y, 1, 0)
    step = 1
    while 2 * step < w:
        y = pltpu.roll(y, step, 0) + pltpu.roll(y, total - step, 0)
        step *= 2
    return y[POOL_PAD:POOL_PAD + n_rows]


def _window_count(n_rows, w):
    t = lax.broadcasted_iota(jnp.int32, (n_rows, 1), 0)
    lo = jnp.maximum(t - w // 2, 0)
    hi = jnp.minimum(t + (w - w // 2 - 1), n_rows - 1)
    return (hi - lo + 1).astype(F32)


def _pool_fwd(name, proj, n_rows, w_grp, scale):
    def body(x_ref, w_ref, sc_ref, y_ref, p_ref):
        for g, w in enumerate(POOL_WINDOWS):
            cols = slice(g * POOL_GROUP_DIM, (g + 1) * POOL_GROUP_DIM)
            x = x_ref[:, cols]
            p = _window_sum(x, w, False) * (1.0 / _window_count(n_rows, w)) - x
            pb = p.astype(BF16)
            p_ref[:, cols] = pb
            y_ref[:, cols] = (_dot(pb, w_ref[g], NN) * sc_ref[:, cols]).astype(BF16)

    blk = pl.BlockSpec((n_rows, POOL_DIM), lambda i: (0, 0))
    return pl.pallas_call(
        body, name=name, grid=(1,),
        in_specs=[blk, pl.BlockSpec(w_grp.shape, lambda i: (0, 0, 0)), pl.BlockSpec((1, POOL_DIM), lambda i: (0, 0))],
        out_specs=[blk, blk], out_shape=[SDS((n_rows, POOL_DIM), BF16)] * 2, compiler_params=_cparams(),
    )(proj, w_grp, scale)


def _pool_bwd(name, dcat, n_rows, p, w_grp, scale, into):
    def body(dy_ref, p_ref, w_ref, sc_ref, into_ref, dx_ref, dw_ref, dsc_ref):
        for g, w in enumerate(POOL_WINDOWS):
            cols = slice(g * POOL_GROUP_DIM, (g + 1) * POOL_GROUP_DIM)
            dy = dy_ref[:, cols]
            pb = p_ref[:, cols]
            pw = _dot(pb, w_ref[g], NN)
            dsc_ref[:, cols] = _colsum(dy * pw)
            dpw = (dy * sc_ref[:, cols]).astype(BF16)
            dw_ref[g] = _dot(pb, dpw, TN)
            dp = _dot(dpw, w_ref[g], NT)
            dx_ref[:, cols] = (_window_sum(dp * (1.0 / _window_count(n_rows, w)), w, True) - dp).astype(BF16)

    blk = pl.BlockSpec((n_rows, POOL_DIM), lambda i: (0, 0))
    w_spec = pl.BlockSpec(w_grp.shape, lambda i: (0, 0, 0))
    v_spec = pl.BlockSpec((1, POOL_DIM), lambda i: (0, 0))
    return pl.pallas_call(
        body, name=name, grid=(1,), in_specs=[blk, blk, w_spec, v_spec, pl.BlockSpec(memory_space=pl.ANY)],
        out_specs=[blk, w_spec, v_spec],
        out_shape=[SDS(into.shape, into.dtype), SDS(w_grp.shape, F32), SDS((1, POOL_DIM), F32)],
        input_output_aliases={4: 0}, compiler_params=_cparams(),
    )(dcat, p, w_grp, scale, into)


def _head_keys(kv_blk, k_rope):
    lane = lax.broadcasted_iota(jnp.int32, (1, HEAD_PAD), 1)
    return jnp.where(lane < QK_NOPE, kv_blk, k_rope)


def _attn_fwd(name, q, kv, k_rope, n_q):
    n_k = kv.shape[0]
    h = kv.shape[1] // HEAD_PAD
    tq = _tile(n_q, ATTN_Q_ROWS_FWD, 16)

    def body(q_ref, kv_ref, kr_ref, o_ref, lse_ref):
        kvb = kv_ref[...]
        s = _dot(q_ref[...], _head_keys(kvb, kr_ref[...]), NT) * ATTN_SCALE
        m = jnp.max(s, axis=-1, keepdims=True)
        e = jnp.exp(s - m)
        l = jnp.sum(e, axis=-1, keepdims=True)
        p = (e * (1.0 / l)).astype(BF16)
        lane = lax.broadcasted_iota(jnp.int32, (1, HEAD_PAD), 1)
        o_ref[...] = jnp.where(lane >= QK_NOPE, _dot(p, kvb, NN), 0.0).astype(BF16)
        lse_ref[...] = m + jnp.log(l)

    blk = pl.BlockSpec((tq, HEAD_PAD), lambda hh, i: (i, hh))
    return pl.pallas_call(
        body, name=name, grid=(h, n_q // tq),
        in_specs=[blk, pl.BlockSpec((n_k, HEAD_PAD), lambda hh, i: (0, hh)),
                  pl.BlockSpec((n_k, HEAD_PAD), lambda hh, i: (0, 0))],
        out_specs=[blk, pl.BlockSpec((None, tq, 1), lambda hh, i: (hh, i, 0))],
        out_shape=[SDS((n_q, h * HEAD_PAD), BF16), SDS((h, n_q, 1), F32)], compiler_params=_cparams(),
    )(q, kv, k_rope)


def _attn_bwd(name, q, kv, k_rope, o, lse, dy, dy_col0, n_q):
    n_k = kv.shape[0]
    h = kv.shape[1] // HEAD_PAD
    tq = _tile(n_q, ATTN_Q_ROWS_BWD, 16)
    n_i = n_q // tq

    def body(q_ref, kv_ref, kr_ref, o_ref, lse_ref, do_ref, dq_ref, dkv_ref, dkr_ref, acc_k, acc_v):
        hh, i = pl.program_id(0), pl.program_id(1)
        qq, kvb = q_ref[...], kv_ref[...]
        kk = _head_keys(kvb, kr_ref[...])
        d_o = do_ref[...]
        dd = d_o.astype(BF16)
        s = _dot(qq, kk, NT) * ATTN_SCALE
        p = jnp.exp(s - lse_ref[...])
        dp = _dot(dd, kvb, NT)
        delta = jnp.sum(d_o * o_ref[...].astype(F32), axis=-1, keepdims=True)
        ds = (p * (dp - delta) * ATTN_SCALE).astype(BF16)
        dq_ref[...] = _dot(ds, kk, NN)
        dk = _dot(ds, qq, TN)
        dv = _dot(p.astype(BF16), dd, TN)

        @pl.when(i == 0)
        def _():
            acc_k[...] = dk
            acc_v[...] = dv

        @pl.when(i > 0)
        def _():
            acc_k[...] += dk
            acc_v[...] += dv

        @pl.when(i == n_i - 1)
        def _():
            lane = lax.broadcasted_iota(jnp.int32, (1, HEAD_PAD), 1)
            dkv_ref[...] = jnp.where(lane < QK_NOPE, acc_k[...], acc_v[...]).astype(BF16)
            rope = jnp.where((lane >= QK_NOPE) & (lane < QK_HEAD), acc_k[...], 0.0)

            @pl.when(hh == 0)
            def _():
                dkr_ref[...] = rope

            @pl.when(hh > 0)
            def _():
                dkr_ref[...] += rope

    blk = pl.BlockSpec((tq, HEAD_PAD), lambda hh, i: (i, hh))
    kv_spec = pl.BlockSpec((n_k, HEAD_PAD), lambda hh, i: (0, hh))
    shared = pl.BlockSpec((n_k, HEAD_PAD), lambda hh, i: (0, 0))
    return pl.pallas_call(
        body, name=name, grid=(h, n_i),
        in_specs=[blk, kv_spec, shared, blk, pl.BlockSpec((None, tq, 1), lambda hh, i: (hh, i, 0)),
                  pl.BlockSpec((tq, HEAD_PAD), lambda hh, i: (i, dy_col0 + hh))],
        out_specs=[blk, kv_spec, shared],
        out_shape=[SDS((n_q, h * HEAD_PAD), F32), SDS((n_k, h * HEAD_PAD), BF16), SDS((n_k, HEAD_PAD), F32)],
        scratch_shapes=[pltpu.VMEM((n_k, HEAD_PAD), F32), pltpu.VMEM((n_k, HEAD_PAD), F32)],
        compiler_params=_cparams(),
    )(q, kv, k_rope, o, lse, dy)


CONV_COLS = 256


def _shift_rows(x, d):
    n_rows = x.shape[0]
    t = lax.broadcasted_iota(jnp.int32, (n_rows, 1), 0)
    if d > 0:
        return jnp.where(t >= d, pltpu.roll(x, d, 0), 0.0)
    return jnp.where(t < n_rows + d, pltpu.roll(x, n_rows + d, 0), 0.0)


def _conv_fwd(name, z3, conv_w):
    n_rows = z3.shape[0]
    nb = D_MODEL // CONV_COLS

    def body(b_ref, c_ref, v_ref, w_ref, y_ref):
        z = c_ref[...] * v_ref[...]
        zc = w_ref[0:1, :] * _shift_rows(z, 1) + w_ref[1:2, :] * z + w_ref[2:3, :] * _shift_rows(z, -1)
        y_ref[...] = (b_ref[...] * zc).astype(BF16)

    def part(k):
        return pl.BlockSpec((n_rows, CONV_COLS), lambda j: (0, k * nb + j))

    return pl.pallas_call(
        body, name=name, grid=(nb,),
        in_specs=[part(0), part(1), part(2), pl.BlockSpec((3, CONV_COLS), lambda j: (0, j))],
        out_specs=pl.BlockSpec((n_rows, CONV_COLS), lambda j: (0, j)),
        out_shape=SDS((n_rows, D_MODEL), BF16), compiler_params=_cparams(),
    )(z3, z3, z3, conv_w)


def _conv_bwd(name, dy, z3, conv_w):
    n_rows = z3.shape[0]
    nb = D_MODEL // CONV_COLS

    def body(dy_ref, b_ref, c_ref, v_ref, w_ref, db_ref, dc_ref, dv_ref, dw_ref):
        c, v, d_y = c_ref[...], v_ref[...], dy_ref[...]
        z = c * v
        z_dn, z_up = _shift_rows(z, 1), _shift_rows(z, -1)
        zc = w_ref[0:1, :] * z_dn + w_ref[1:2, :] * z + w_ref[2:3, :] * z_up
        db_ref[...] = (d_y * zc).astype(BF16)
        dzc = d_y * b_ref[...]
        dz = w_ref[0:1, :] * _shift_rows(dzc, -1) + w_ref[1:2, :] * dzc + w_ref[2:3, :] * _shift_rows(dzc, 1)
        dc_ref[...] = (dz * v).astype(BF16)
        dv_ref[...] = (dz * c).astype(BF16)
        dw_ref[0:1, :] = _colsum(dzc * z_dn)
        dw_ref[1:2, :] = _colsum(dzc * z)
        dw_ref[2:3, :] = _colsum(dzc * z_up)

    def part(k):
        return pl.BlockSpec((n_rows, CONV_COLS), lambda j: (0, k * nb + j))

    col = pl.BlockSpec((n_rows, CONV_COLS), lambda j: (0, j))
    w_spec = pl.BlockSpec((3, CONV_COLS), lambda j: (0, j))
    return pl.pallas_call(
        body, name=name, grid=(nb,), in_specs=[col, part(0), part(1), part(2), w_spec],
        out_specs=[col, col, col, w_spec],
        out_shape=[SDS((n_rows, D_MODEL), BF16)] * 3 + [SDS((3, D_MODEL), F32)], compiler_params=_cparams(),
    )(dy, z3, z3, z3, conv_w)


def _silu_rows(name, x):
    def body(x_ref, s_ref, d_ref):
        xx = x_ref[...]
        sg = jax.nn.sigmoid(xx)
        s_ref[...] = (xx * sg).astype(BF16)
        d_ref[...] = sg * (1.0 + xx * (1.0 - sg))

    return pl.pallas_call(body, name=name, out_shape=[SDS(x.shape, BF16), SDS(x.shape, F32)])(x)


def _sum_rows(name, x, scale=None):
    r, n = x.shape
    tn = _tile(n, 32768, 128)

    def body(*refs):
        acc = jnp.sum(refs[0][...].astype(F32), axis=0, keepdims=True)
        if scale is not None:
            acc = acc * refs[1][...]
        refs[-1][...] = acc

    in_specs = [pl.BlockSpec((r, tn), lambda j: (0, j))]
    args = [x]
    if scale is not None:
        in_specs.append(pl.BlockSpec((1, tn), lambda j: (0, j)))
        args.append(scale)
    return pl.pallas_call(body, name=name, grid=(n // tn,), in_specs=in_specs,
                          out_specs=pl.BlockSpec((1, tn), lambda j: (0, j)), out_shape=SDS((1, n), F32))(*args)


def _me_operand(me):
    return jnp.reshape(me, (1,)).astype(jnp.int32)


def _sum_slots(name, slots, src, me):
    n_slots, r, c = slots.shape
    tr = _tile(r, 432, 16)

    def body(me_ref, own_ref, x_ref, o_ref):
        acc = own_ref[...].astype(F32)
        for sl in range(n_slots):
            acc = acc + x_ref[sl].astype(F32)
        o_ref[...] = acc

    grid_spec = pltpu.PrefetchScalarGridSpec(
        num_scalar_prefetch=1, grid=(r // tr,),
        in_specs=[pl.BlockSpec((None, tr, c), lambda i, me_ref: (me_ref[0], i, 0)),
                  pl.BlockSpec((n_slots, tr, c), lambda i, me_ref: (0, i, 0))],
        out_specs=pl.BlockSpec((tr, c), lambda i, me_ref: (i, 0)))
    return pl.pallas_call(body, name=name, grid_spec=grid_spec, out_shape=SDS((r, c), F32),
                          compiler_params=_cparams())(_me_operand(me), src, slots)


def _adamw(name, w, g, m, v):
    shape = w.shape
    cols = shape[-1]
    rows = w.size // cols
    tr = _tile(rows, 512, 8)
    bc1 = 1.0 - ADAM_B1 ** ADAM_STEP
    bc2 = 1.0 - ADAM_B2 ** ADAM_STEP

    def body(w_ref, g_ref, m_ref, v_ref, d_ref, nm_ref, nv_ref):
        gg = g_ref[...]
        nm = ADAM_B1 * m_ref[...] + (1.0 - ADAM_B1) * gg
        nv = ADAM_B2 * v_ref[...] + (1.0 - ADAM_B2) * (gg * gg)
        nm_ref[...] = nm
        nv_ref[...] = nv
        d_ref[...] = -ADAM_LR * ((nm / bc1) / (jnp.sqrt(nv / bc2) + ADAM_EPS) + ADAM_WD * w_ref[...])

    spec = pl.BlockSpec((tr, cols), lambda i: (i, 0))
    outs = pl.pallas_call(body, name=name, grid=(rows // tr,), in_specs=[spec] * 4, out_specs=[spec] * 3,
                          out_shape=[SDS((rows, cols), F32)] * 3, compiler_params=_cparams())(
        w.reshape(rows, cols), g.reshape(rows, cols), m.reshape(rows, cols), v.reshape(rows, cols))
    return tuple(t.reshape(shape) for t in outs)


def _exchange(name, x, scatter, after=None):
    blk = x.shape[1:] if scatter else x.shape
    extra = [] if after is None else [after]

    def body(x_ref, *rest):
        out_ref, send_sems, recv_sems, local_sem = rest[len(extra):]
        mx, my, mc = lax.axis_index("x"), lax.axis_index("y"), lax.axis_index("c")
        me = 4 * mx + 2 * my + mc
        own = pltpu.make_async_copy(x_ref.at[me] if scatter else x_ref, out_ref.at[me], local_sem)
        own.start()
        copies = []
        for kk in range(1, N_DEV):
            px = jnp.bitwise_xor(mx, (kk >> 2) & 1)
            py = jnp.bitwise_xor(my, (kk >> 1) & 1)
            pc = jnp.bitwise_xor(mc, kk & 1)
            peer = 4 * px + 2 * py + pc
            send = pltpu.make_async_remote_copy(
                src_ref=x_ref.at[peer] if scatter else x_ref, dst_ref=out_ref.at[me],
                send_sem=send_sems.at[kk - 1], recv_sem=recv_sems.at[kk - 1],
                device_id=(px, py, pc), device_id_type=MESH)
            send.start()
            arrival = pltpu.make_async_remote_copy(
                src_ref=x_ref.at[peer] if scatter else x_ref, dst_ref=out_ref.at[peer],
                send_sem=send_sems.at[kk - 1], recv_sem=recv_sems.at[kk - 1],
                device_id=(px, py, pc), device_id_type=MESH)
            copies.append((send, arrival))
        for send, arrival in copies:
            arrival.wait_recv()
            send.wait_send()
        own.wait()

    return pl.pallas_call(
        body, name=name, out_shape=SDS((N_DEV,) + tuple(blk), x.dtype),
        in_specs=[pl.BlockSpec(memory_space=pl.ANY)] * (1 + len(extra)), out_specs=pl.BlockSpec(memory_space=pl.ANY),
        scratch_shapes=[pltpu.SemaphoreType.DMA((N_DEV - 1,)), pltpu.SemaphoreType.DMA((N_DEV - 1,)),
                        pltpu.SemaphoreType.DMA],
    )(x, *extra)


def _rope_perm(pre, reps, post):
    half = QK_ROPE // 4
    width = reps * (pre + QK_ROPE) + post
    p = np.zeros((width, width), np.float32)
    for rep in range(reps):
        s0 = rep * (pre + QK_ROPE) + pre
        for base in (s0, s0 + 2 * half):
            for i in range(half):
                p[base + half + i, base + i] = -1.0
                p[base + i, base + half + i] = 1.0
    return p


def _rope_layout(pre, reps, post):
    width = reps * (pre + QK_ROPE) + post
    expand = np.zeros((QK_ROPE, width), np.float32)
    plain = np.ones((1, width), np.float32)
    for rep in range(reps):
        s0 = rep * (pre + QK_ROPE) + pre
        expand[np.arange(QK_ROPE), s0 + np.arange(QK_ROPE)] = 1.0
        plain[0, s0:s0 + QK_ROPE] = 0.0
    return jnp.asarray(expand, BF16), jnp.asarray(plain, F32), jnp.asarray(_rope_perm(pre, reps, post), BF16)


def _head_spread():
    spread = np.zeros((HEADS * QK_HEAD, HEADS * HEAD_PAD), np.float32)
    for hh in range(HEADS):
        spread[hh * QK_HEAD + np.arange(QK_HEAD), hh * HEAD_PAD + np.arange(QK_HEAD)] = 1.0
    return jnp.asarray(spread, BF16)


def _rope_factors(n_lat, t_rows):
    half = QK_ROPE // 4
    pos = jnp.arange(n_lat)
    freqs = jnp.power(ROPE_THETA, -jnp.arange(0, 2 * half, 2, dtype=F32) / (2 * half))
    ang_r = (pos // GRID_W).astype(F32)[:, None] * freqs
    ang_c = (pos % GRID_W).astype(F32)[:, None] * freqs
    ang = jnp.concatenate([ang_r, ang_r, ang_c, ang_c], axis=-1)
    rest = t_rows - n_lat
    return (jnp.concatenate([jnp.cos(ang), jnp.ones((rest, QK_ROPE), F32)], axis=0),
            jnp.concatenate([jnp.sin(ang), jnp.zeros((rest, QK_ROPE), F32)], axis=0))


def _ffn_half_fwd(tag, s, mg, k, feed, i, coef, n_lat):
    wg_t, wu_t = feed.weights(f"{tag}_up", [f"gate_t{i}", f"up_t{i}"], s)
    u, a, b, hid = _ffn_up(f"{tag}_up", s, mg, k, n_lat, wg_t, wu_t)
    (wd,) = feed.weights(f"{tag}_down", [f"down{i}"], hid)
    s_out, o = _mm_resid(f"{tag}_down", [(hid, wd)], s, mg, k, coef, n_lat)
    return s_out, (s, u, a, b, hid, o, wg_t, wu_t, wd)


def _ffn_half_bwd(tag, ds_out, saved, mg, k, feed, i, coef, n_lat, out_rows=None):
    s, u, a, b, hid, o, wg_t, wu_t, wd = saved
    do, da, db, dgate = _ffn_dact(f"{tag}_dact", ds_out, o, mg, k, coef, n_lat, wd, a, b)
    dwd = _mm(f"{tag}_dwd", [(hid, do)], "tn", BF16)
    dwg_t, dwu_t = _dw_pair(f"{tag}_dwgu", da, db, u)
    token = feed.grads(tag, {f"down{i}": dwd, f"gate_t{i}": dwg_t, f"up_t{i}": dwu_t})
    ds_in, (dshift, dscale, dgain) = _du_adaln(f"{tag}_du", [(da, wg_t), (db, wu_t)], s, ds_out, mg, k, n_lat,
                                               _after(token), out_rows)
    return ds_in, dict(shift=dshift, scale=dscale, gate=dgate, gain=dgain)


def _after(token):
    return jnp.zeros((1, D_MODEL), F32) + token


def _mod_grad(parts, n_groups):
    rows = []
    zero = jnp.zeros((n_groups, 1, D_MODEL), F32)
    for k in range(3):
        for nm in ("shift", "scale", "gate"):
            t = parts[k].get(nm, zero)
            if t.shape[0] < n_groups:
                t = jnp.concatenate([t, jnp.zeros((n_groups - t.shape[0], 1, D_MODEL), F32)], axis=0)
            rows.append(t)
    return jnp.concatenate(rows, axis=1).reshape(n_groups, N_MOD * D_MODEL)


def _local_step(x, ctx, target, mod_h, mod_g, norm_g, feed, pool_w, pool_scale, q_norm_g, kv_norm_g, conv_w,
                final_norm_g):
    n_lat, n_ctx = x.shape[0], ctx.shape[0]
    t_all = n_lat + n_ctx
    mg0 = jnp.stack([jnp.concatenate([mod_h[0], norm_g[0]], axis=0), jnp.concatenate([mod_g, norm_g[0]], axis=0)])
    mg1 = jnp.concatenate([mod_h[1], norm_g[1]], axis=0)[None]

    s0 = jnp.concatenate([x, ctx], axis=0)
    s1, sv_f00 = _ffn_half_fwd("l0f0", s0, mg0, 0, feed, 0, 0.5, n_lat)

    (w_in,) = feed.weights("l0m_in", ["in_t"], s1)
    kv_rows = KV_RANK + QK_ROPE
    w_in_t = jnp.concatenate([
        w_in[:POOL_DIM], jnp.zeros((PA_CQ - POOL_DIM, D_MODEL), BF16), w_in[POOL_DIM:POOL_DIM + Q_RANK],
        w_in[POOL_DIM + Q_RANK:], jnp.zeros((PA_KV_W - kv_rows, D_MODEL), BF16)], axis=0)
    ua, proj = _adaln_mm("l0m_proj", s1, mg0, 1, n_lat, w_in_t)
    w_uq, w_ukv_t, w_ab_out = feed.weights("l0m_rest", ["uq", "ukv_t", "ab_out"], proj)
    pool_y, pool_p = _pool_fwd("l0m_pool", proj, n_lat, pool_w.astype(BF16), pool_scale)
    nq = _rmsnorm_fwd("l0m_qnorm", proj, Q_RANK, PA_CQ // Q_RANK, q_norm_g, n_lat)
    q_lin = _mm("l0m_q", [(nq, w_uq)], "nn", F32, 512, 768)
    cos32, sin32 = _rope_factors(n_lat, t_all)
    lay_q, lay_k = _rope_layout(QK_NOPE, HEADS, 0), _rope_layout(KV_RANK, 1, PA_KV_W - kv_rows)
    spread = _head_spread()
    q_flat = _rope("l0m_qrope", q_lin, Q_RANK, 0, cos32[:n_lat], sin32[:n_lat], lay_q, False, BF16, spread)
    kvr = _rope("l0m_krope", proj, PA_KV_W, PA_KV // PA_KV_W, cos32, sin32, lay_k, False, F32)
    nkv = _rmsnorm_fwd("l0m_kvnorm", kvr, KV_RANK, 0, kv_norm_g, t_all)
    kv = _mm("l0m_kv", [(nkv, w_ukv_t)], "nt", BF16, 768, 512)
    k_rope = jnp.pad(kvr[:, KV_RANK:KV_RANK + QK_ROPE].astype(BF16), ((0, 0), (QK_NOPE, HEAD_PAD - QK_HEAD)))
    o_flat, lse = _attn_fwd("l0m_attn", q_flat, kv, k_rope, n_lat)
    w_o_pad = jnp.pad(w_ab_out[POOL_DIM:].reshape(HEADS, V_HEAD, D_MODEL),
                      ((0, 0), (HEAD_PAD - V_HEAD, 0), (0, 0))).reshape(HEADS * HEAD_PAD, D_MODEL)
    w_o_pool = w_ab_out[:POOL_DIM]
    h2, mix_o = _mm_resid("l0m_out", [(pool_y, w_o_pool), (o_flat, w_o_pad)], s1, mg0[:1], 1, 1.0, n_lat)

    h3, sv_f01 = _ffn_half_fwd("l0f1", h2, mg0[:1], 2, feed, 1, 0.5, n_lat)

    h4, sv_f10 = _ffn_half_fwd("l1f0", h3, mg1, 0, feed, 2, 0.5, n_lat)
    w_cin_t, w_c_out = feed.weights("l1m", ["cin_t", "c_out"], h4)
    uc, z3 = _adaln_mm("l1m_in", h4, mg1, 1, n_lat, w_cin_t)
    yc = _conv_fwd("l1m_conv", z3, conv_w)
    h5, conv_o = _mm_resid("l1m_out", [(yc, w_c_out)], h4, mg1, 1, 1.0, n_lat)
    h6, sv_f11 = _ffn_half_fwd("l1f1", h5, mg1, 2, feed, 3, 0.5, n_lat)

    dh6, sq_cols, d_final_g = _final_loss("loss_head", h6, target, final_norm_g)
    g = {}
    dh5, g["f11"] = _ffn_half_bwd("l1f1", dh6, sv_f11, mg1, 2, feed, 3, 0.5, n_lat)

    do_c, dyc, dgate_c = _gate_mm("l1m_dy", dh5, conv_o, mg1, 1, 1.0, n_lat, w_c_out)
    d_c_out = _mm("l1m_dwout", [(yc, do_c)], "tn", BF16)
    db_, dc_, dv_, d_conv_w = _conv_bwd("l1m_dconv", dyc, z3, conv_w)
    dz3 = jnp.concatenate([db_, dc_, dv_], axis=-1)
    d_cin_t = _mm("l1m_dwin", [(dz3, uc)], "tn", BF16)
    token = feed.grads("l1m", {"c_out": d_c_out, "cin_t": d_cin_t})
    dh4, (dsh_c, dsc_c, dgn_c) = _du_adaln("l1m_du", [(dz3, w_cin_t)], h4, dh5, mg1, 1, n_lat, _after(token))
    dh3, g["f10"] = _ffn_half_bwd("l1f0", dh4, sv_f10, mg1, 0, feed, 2, 0.5, n_lat)

    dh2, g["f01"] = _ffn_half_bwd("l0f1", dh3, sv_f01, mg0[:1], 2, feed, 1, 0.5, n_lat)

    w_back = jnp.concatenate([w_o_pool, w_o_pad], axis=0)
    do_a, dcat, dgate_a = _gate_mm("l0m_dcat", dh2, mix_o, mg0[:1], 1, 1.0, n_lat, w_back)
    d_o_pad = _mm("l0m_dwout_a", [(o_flat, do_a)], "tn", BF16)
    d_ab_out = jnp.concatenate([
        _mm("l0m_dwout_p", [(pool_y, do_a)], "tn", BF16),
        d_o_pad.reshape(HEADS, HEAD_PAD, D_MODEL)[:, HEAD_PAD - V_HEAD:].reshape(HEADS * V_HEAD, D_MODEL)], axis=0)
    dproj = jnp.zeros((t_all, PA_W), BF16)
    dproj, d_pool_w, d_pool_scale = _pool_bwd("l0m_dpool", dcat, n_lat, pool_p, pool_w.astype(BF16), pool_scale, dproj)
    dq_flat, dkv, dk_rope = _attn_bwd("l0m_dattn", q_flat, kv, k_rope, o_flat, lse, dcat, POOL_DIM // HEAD_PAD, n_lat)
    dq_lin = _rope("l0m_dqrope", dq_flat, Q_RANK, 0, cos32[:n_lat], sin32[:n_lat], lay_q, True, BF16, spread)
    d_uq = _mm("l0m_dwuq", [(nq, dq_lin)], "tn", BF16, 768, 768)
    dnq = _mm("l0m_dnq", [(dq_lin, w_uq)], "nt", F32, 512, 768)
    dproj, d_q_norm_g = _rmsnorm_bwd("l0m_dqnorm", proj, Q_RANK, PA_CQ // Q_RANK, dnq, q_norm_g, n_lat, BF16,
                                     (dproj, PA_CQ // Q_RANK))
    dnkv = _mm("l0m_dnkv", [(dkv, w_ukv_t)], "nn", F32, 768, 256)
    d_ukv_t = _mm("l0m_dwukv", [(dkv, nkv)], "tn", BF16, 512, 256)
    dckv, d_kv_norm_g = _rmsnorm_bwd("l0m_dkvnorm", kvr, KV_RANK, 0, dnkv, kv_norm_g, t_all)
    dkvr = jnp.concatenate([dckv, dk_rope[:, QK_NOPE:QK_HEAD],
                            jnp.zeros((t_all, PA_KV_W - KV_RANK - QK_ROPE), F32)], axis=-1)
    dproj = _rope("l0m_dkrope", dkvr, PA_KV_W, 0, cos32, sin32, lay_k, True, BF16, None, (dproj, PA_KV // PA_KV_W))
    d_in_pad = _mm("l0m_dwin", [(dproj, ua)], "tn", BF16, 640, 512)
    d_in_t = jnp.concatenate([d_in_pad[:POOL_DIM], d_in_pad[PA_CQ:PA_CQ + Q_RANK],
                              d_in_pad[PA_KV:PA_KV + kv_rows]], axis=0)
    token = feed.grads("l0m", {"ab_out": d_ab_out, "uq": d_uq, "ukv_t": d_ukv_t, "in_t": d_in_t})
    ds1, (dsh_a, dsc_a, dgn_a) = _du_adaln("l0m_du", [(dproj, w_in_t)], s1, dh2, mg0, 1, n_lat, _after(token))
    grad_x, g["f00"] = _ffn_half_bwd("l0f0", ds1, sv_f00, mg0, 0, feed, 0, 0.5, n_lat, out_rows=n_lat)

    dmod0 = _mod_grad([g["f00"], dict(shift=dsh_a, scale=dsc_a, gate=dgate_a), g["f01"]], 2)
    dmod1 = _mod_grad([g["f10"], dict(shift=dsh_c, scale=dsc_c, gate=dgate_c), g["f11"]], 1)
    d_norm_g = jnp.stack([
        jnp.concatenate([jnp.sum(g["f00"]["gain"], axis=0), jnp.sum(dgn_a, axis=0), g["f01"]["gain"][0]], axis=0),
        jnp.concatenate([g["f10"]["gain"][0], dgn_c[0], g["f11"]["gain"][0]], axis=0)])
    grads = dict(
        pool_w=d_pool_w, pool_scale=d_pool_scale, q_norm_g=d_q_norm_g[0], kv_norm_g=d_kv_norm_g[0],
        conv_w=d_conv_w, final_norm_g=d_final_g[0], norm_g=d_norm_g,
        mod_h=jnp.stack([dmod0[0], dmod1[0]]), mod_g=dmod0[1])
    return sq_cols, grad_x, grads


HBM_SPEC = pl.BlockSpec(memory_space=pltpu.HBM)
SEM_SPEC = pl.BlockSpec(memory_space=pltpu.SEMAPHORE)
ANY_SPEC = pl.BlockSpec(memory_space=pl.ANY)
SIDE_EFFECT = pltpu.SideEffectType.DATAFLOW_SIDE_EFFECTING
N_PEERS = N_DEV - 1


def _mesh_place():
    mx, my, mc = lax.axis_index("x"), lax.axis_index("y"), lax.axis_index("c")
    return mx, my, mc, 4 * mx + 2 * my + mc


def _peer(place, kk):
    mx, my, mc, _ = place
    px = jnp.bitwise_xor(mx, (kk >> 2) & 1)
    py = jnp.bitwise_xor(my, (kk >> 1) & 1)
    pc = jnp.bitwise_xor(mc, kk & 1)
    return (px, py, pc), 4 * px + 2 * py + pc


def _hbm(a):
    return pltpu.with_memory_space_constraint(a, pltpu.HBM)


def _landing(block, me):
    zone = lax.empty((N_DEV,) + block.shape, block.dtype)
    return lax.dynamic_update_slice(zone, block[None], (me,) + (0,) * block.ndim)


ALL_PEERS = tuple(range(1, N_DEV))
SIBLING = 1
CHIP_PEERS = (2, 4, 6)
RELAYED = (3, 5, 7)


def _exchange_start(name, srcs, lands, scatter, after, peers=ALL_PEERS):
    n = len(srcs)
    extra = [] if after is None else [after]

    def body(*refs):
        src, land = refs[:n], refs[n:2 * n]
        send_sems, recv_sems, token = refs[2 * n + len(extra)], refs[2 * n + len(extra) + 1], refs[-1]
        place = _mesh_place()
        for a in range(n):
            for kk in peers:
                dev, peer = _peer(place, kk)
                pltpu.make_async_remote_copy(
                    src_ref=src[a].at[peer] if scatter else src[a],
                    dst_ref=land[a].at[kk - 1] if scatter else land[a].at[place[3]],
                    send_sem=send_sems.at[a * N_PEERS + kk - 1], recv_sem=recv_sems.at[a * N_PEERS + kk - 1],
                    device_id=dev, device_id_type=MESH).start()
        token[...] = jnp.zeros_like(token)

    thru = [pltpu.HBM(t.shape, t.dtype) for t in (*srcs, *lands)]
    res = pl.pallas_call(
        body, name=name,
        out_shape=(pltpu.SemaphoreType.DMA((n * N_PEERS,)), pltpu.SemaphoreType.DMA((n * N_PEERS,)), *thru,
                   SDS((8, 128), F32)),
        in_specs=[HBM_SPEC] * (2 * n) + [ANY_SPEC] * len(extra),
        out_specs=(SEM_SPEC, SEM_SPEC, *([HBM_SPEC] * (2 * n)), pl.BlockSpec(memory_space=pltpu.VMEM)),
        input_output_aliases={i: 2 + i for i in range(2 * n)},
        compiler_params=pltpu.CompilerParams(has_side_effects=SIDE_EFFECT),
    )(*[_hbm(s) for s in srcs], *[_hbm(t) for t in lands], *extra)
    return res[0], res[1], list(res[2:2 + n]), list(res[2 + n:2 + 2 * n]), res[-1]


def _exchange_wait(name, send_sems, recv_sems, srcs, lands, places, scatter, after):
    n = len(srcs)

    def body(*refs):
        src, land = refs[:n], refs[n:2 * n]
        send, recv = refs[2 * n], refs[2 * n + 1]
        place = _mesh_place()
        for a in range(n):
            for kk in range(1, N_DEV):
                dev, peer = _peer(place, kk)
                cp = pltpu.make_async_remote_copy(
                    src_ref=src[a].at[peer] if scatter else src[a],
                    dst_ref=land[a].at[kk - 1] if scatter else land[a].at[peer],
                    send_sem=send.at[places[a] * N_PEERS + kk - 1], recv_sem=recv.at[places[a] * N_PEERS + kk - 1],
                    device_id=dev, device_id_type=MESH)
                cp.wait_send()
                cp.wait_recv()

    thru = [pltpu.HBM(t.shape, t.dtype) for t in (*srcs, *lands)]
    res = pl.pallas_call(
        body, name=name, out_shape=tuple(thru),
        in_specs=[HBM_SPEC] * (2 * n) + [SEM_SPEC, SEM_SPEC] + [ANY_SPEC] * len(after),
        out_specs=tuple([HBM_SPEC] * (2 * n)), input_output_aliases={i: i for i in range(2 * n)},
        compiler_params=pltpu.CompilerParams(has_side_effects=SIDE_EFFECT),
    )(*srcs, *lands, send_sems, recv_sems, *after)
    return list(res[:n]), list(res[n:])


def _gather_relay(name, send1, recv1, lands, places, after):
    n = len(lands)

    def body(*refs):
        land, s1, r1 = refs[:n], refs[n], refs[n + 1]
        s2, r2 = refs[n + 3], refs[n + 4]
        place = _mesh_place()
        sibling = _peer(place, SIBLING)[0]
        for a in range(n):
            for j, kk in enumerate(CHIP_PEERS):
                dev, origin = _peer(place, kk)
                block = land[a].at[origin]
                pltpu.make_async_remote_copy(
                    src_ref=block, dst_ref=block, send_sem=s1.at[places[a] * N_PEERS + kk - 1],
                    recv_sem=r1.at[places[a] * N_PEERS + kk - 1], device_id=dev, device_id_type=MESH).wait_recv()
                pltpu.make_async_remote_copy(
                    src_ref=block, dst_ref=block, send_sem=s2.at[a * 3 + j], recv_sem=r2.at[a * 3 + j],
                    device_id=sibling, device_id_type=MESH).start()

    res = pl.pallas_call(
        body, name=name,
        out_shape=(pltpu.SemaphoreType.DMA((3 * n,)), pltpu.SemaphoreType.DMA((3 * n,)),
                   *[pltpu.HBM(t.shape, t.dtype) for t in lands]),
        in_specs=[HBM_SPEC] * n + [SEM_SPEC, SEM_SPEC, ANY_SPEC],
        out_specs=(SEM_SPEC, SEM_SPEC, *([HBM_SPEC] * n)),
        input_output_aliases={i: 2 + i for i in range(n)},
        compiler_params=pltpu.CompilerParams(has_side_effects=SIDE_EFFECT),
    )(*lands, send1, recv1, after)
    return res[0], res[1], list(res[2:])


def _gather_wait(name, send1, recv1, send2, recv2, srcs, lands, places, after):
    n = len(lands)

    def body(*refs):
        src, land = refs[:n], refs[n:2 * n]
        s1, r1, s2, r2 = refs[2 * n:2 * n + 4]
        place = _mesh_place()
        for a in range(n):
            for kk in (SIBLING,) + CHIP_PEERS:
                dev, origin = _peer(place, kk)
                first = pltpu.make_async_remote_copy(
                    src_ref=src[a], dst_ref=land[a].at[origin], send_sem=s1.at[places[a] * N_PEERS + kk - 1],
                    recv_sem=r1.at[places[a] * N_PEERS + kk - 1], device_id=dev, device_id_type=MESH)
                first.wait_send()
                if kk == SIBLING:
                    first.wait_recv()
            for j, kk in enumerate(CHIP_PEERS):
                dev, origin = _peer(place, kk + 1)
                relay = pltpu.make_async_remote_copy(
                    src_ref=src[a], dst_ref=land[a].at[origin], send_sem=s2.at[a * 3 + j], recv_sem=r2.at[a * 3 + j],
                    device_id=dev, device_id_type=MESH)
                relay.wait_send()
                relay.wait_recv()

    arrays = (*srcs, *lands)
    res = pl.pallas_call(
        body, name=name, out_shape=tuple(pltpu.HBM(t.shape, t.dtype) for t in arrays),
        in_specs=[HBM_SPEC] * (2 * n) + [SEM_SPEC] * 4 + [ANY_SPEC], out_specs=tuple([HBM_SPEC] * (2 * n)),
        input_output_aliases={i: i for i in range(2 * n)},
        compiler_params=pltpu.CompilerParams(has_side_effects=SIDE_EFFECT),
    )(*arrays, send1, recv1, send2, recv2, after)
    return list(res[n:])


class _Feed:
    def __init__(self, shards, groups, me):
        self.shards, self.groups, self.me, self.pos = shards, groups, me, 0
        self.sems, self.srcs, self.lands = {}, {}, {}
        self.relays = {}
        self.pending = []

    def start(self, tag, names, after):
        srcs = [self.shards[nm] for nm in names]
        lands = [_landing(s, self.me) for s in srcs]
        send, recv, srcs, lands, self.token = _exchange_start(
            f"gather_start_{tag}", srcs, lands, False, after, (SIBLING,) + CHIP_PEERS)
        for i, nm in enumerate(names):
            self.sems[nm], self.srcs[nm], self.lands[nm] = (send, recv, i), srcs[i], lands[i]
        return self.token

    def _relay(self, gi, after):
        names = self.groups[gi]
        if gi not in self.relays:
            send, recv, _ = self.sems[names[0]]
            places = [self.sems[nm][2] for nm in names]
            send2, recv2, lands = _gather_relay(f"gather_relay_{gi}", send, recv, [self.lands[nm] for nm in names],
                                                places, after)
            for nm, t in zip(names, lands):
                self.lands[nm] = t
            self.relays[gi] = (send2, recv2)
            after = lands[0]
        return after

    def weights(self, tag, names, after):
        gi = self.pos
        assert names == self.groups[gi], (names, self.groups[gi])
        if gi == 0:
            after = self.token
        self._relay(gi, after)
        if 1 <= gi < len(self.groups) - 1:
            after = self._relay(gi + 1, after)
        send2, recv2 = self.relays[gi]
        send, recv, _ = self.sems[names[0]]
        got = _gather_wait(f"gather_wait_{tag}", send, recv, send2, recv2, [self.srcs[nm] for nm in names],
                           [self.lands[nm] for nm in names], [self.sems[nm][2] for nm in names], after)
        self.pos += 1
        return [t.reshape((N_DEV * t.shape[1],) + t.shape[2:]) for t in got]

    def grads(self, tag, full):
        names = list(full)
        srcs = [full[nm].reshape((N_DEV, full[nm].shape[0] // N_DEV) + full[nm].shape[1:]) for nm in names]
        lands = [lax.empty((N_PEERS,) + s.shape[1:], s.dtype) for s in srcs]
        send, recv, srcs, lands, token = _exchange_start(f"scatter_start_{tag}", srcs, lands, True, None)
        self.pending.append((tag, names, send, recv, srcs, lands))
        return token[0, 0]

    def collect(self, tags, after, keep_slots=()):
        out = {}
        for tag, names, send, recv, srcs, lands in self.pending:
            if tag not in tags:
                continue
            srcs, got = _exchange_wait(f"scatter_wait_{tag}", send, recv, srcs, lands, list(range(len(names))), True,
                                       after)
            for nm, slots, src in zip(names, got, srcs):
                out[nm] = ((slots, src) if nm.startswith(tuple(keep_slots))
                           else _sum_slots(f"reduce_{nm}", slots, src, self.me))
        return out


def _adamw_math(w, gg, m, v):
    nm = ADAM_B1 * m + (1.0 - ADAM_B1) * gg
    nv = ADAM_B2 * v + (1.0 - ADAM_B2) * (gg * gg)
    bc1 = 1.0 - ADAM_B1 ** ADAM_STEP
    bc2 = 1.0 - ADAM_B2 ** ADAM_STEP
    return -ADAM_LR * ((nm / bc1) / (jnp.sqrt(nv / bc2) + ADAM_EPS) + ADAM_WD * w), nm, nv


def _adamw_part(name, i, w, scattered, me, m, v, prev):
    n_parts, rows, cols = w.shape
    tr = _tile(rows, 256, 16)
    if prev is None:
        prev = tuple(lax.empty(w.shape, F32) for _ in range(4))

    slots, src = scattered

    def body(me_ref, w_ref, g_ref, own_ref, m_ref, v_ref, *rest):
        go_ref, d_ref, nm_ref, nv_ref = rest[4:]
        gg = own_ref[...].astype(F32)
        for sl in range(N_PEERS):
            gg = gg + g_ref[sl].astype(F32)
        d, nm, nv = _adamw_math(w_ref[...], gg, m_ref[...], v_ref[...])
        go_ref[...] = gg
        d_ref[...] = d
        nm_ref[...] = nm
        nv_ref[...] = nv

    part = pl.BlockSpec((None, tr, cols), lambda r, me_ref: (i, r, 0))
    grid_spec = pltpu.PrefetchScalarGridSpec(
        num_scalar_prefetch=1, grid=(rows // tr,),
        in_specs=[part, pl.BlockSpec((N_PEERS, tr, cols), lambda r, me_ref: (0, r, 0)),
                  pl.BlockSpec((None, tr, cols), lambda r, me_ref: (me_ref[0], r, 0)), part, part] + [ANY_SPEC] * 4,
        out_specs=[part] * 4)
    return pl.pallas_call(
        body, name=name, grid_spec=grid_spec, out_shape=[SDS(w.shape, F32)] * 4,
        input_output_aliases={6 + k: k for k in range(4)}, compiler_params=_cparams(),
    )(_me_operand(me), w, slots, src, m, v, *prev)


WEIGHT_NAMES = ("c_ctx", "norm_g", "w_mod", "b_mod", "ffn_w_gate", "ffn_w_up", "ffn_w_down", "ab_w_in", "pool_w",
                "pool_scale", "q_norm_g", "w_uq", "kv_norm_g", "w_ukv", "ab_w_out", "conv_w_in", "conv_w",
                "conv_w_out", "final_norm_g")


def kernel(x, c, ctx, c_ctx, norm_g, w_mod, b_mod, ffn_w_gate, ffn_w_up, ffn_w_down, ab_w_in, pool_w, pool_scale, q_norm_g, w_uq, kv_norm_g, w_ukv, ab_w_out, conv_w_in, conv_w, conv_w_out, final_norm_g, loss_target, m_c_ctx, m_norm_g, m_w_mod, m_b_mod, m_ffn_w_gate, m_ffn_w_up, m_ffn_w_down, m_ab_w_in, m_pool_w, m_pool_scale, m_q_norm_g, m_w_uq, m_kv_norm_g, m_w_ukv, m_ab_w_out, m_conv_w_in, m_conv_w, m_conv_w_out, m_final_norm_g, v_c_ctx, v_norm_g, v_w_mod, v_b_mod, v_ffn_w_gate, v_ffn_w_up, v_ffn_w_down, v_ab_w_in, v_pool_w, v_pool_scale, v_q_norm_g, v_w_uq, v_kv_norm_g, v_w_ukv, v_ab_w_out, v_conv_w_in, v_conv_w, v_conv_w_out, v_final_norm_g):
    weights = (c_ctx, norm_g, w_mod, b_mod, ffn_w_gate, ffn_w_up, ffn_w_down, ab_w_in, pool_w, pool_scale, q_norm_g,
               w_uq, kv_norm_g, w_ukv, ab_w_out, conv_w_in, conv_w, conv_w_out, final_norm_g)
    moms = (m_c_ctx, m_norm_g, m_w_mod, m_b_mod, m_ffn_w_gate, m_ffn_w_up, m_ffn_w_down, m_ab_w_in, m_pool_w,
            m_pool_scale, m_q_norm_g, m_w_uq, m_kv_norm_g, m_w_ukv, m_ab_w_out, m_conv_w_in, m_conv_w, m_conv_w_out,
            m_final_norm_g)
    vels = (v_c_ctx, v_norm_g, v_w_mod, v_b_mod, v_ffn_w_gate, v_ffn_w_up, v_ffn_w_down, v_ab_w_in, v_pool_w,
            v_pool_scale, v_q_norm_g, v_w_uq, v_kv_norm_g, v_w_ukv, v_ab_w_out, v_conv_w_in, v_conv_w, v_conv_w_out,
            v_final_norm_g)
    me = 4 * lax.axis_index("x") + 2 * lax.axis_index("y") + lax.axis_index("c")
    n_lat, n_ctx = x.shape[1], ctx.shape[1]
    d = D_MODEL
    mod_cols = w_mod.shape[-1]
    ng_sh, cw_sh = norm_g.shape[-1], conv_w.shape[-1]

    def ffn_shards(i):
        return {f"gate_t{i}": ffn_w_gate[i // 2, i % 2].T, f"up_t{i}": ffn_w_up[i // 2, i % 2].T,
                f"down{i}": ffn_w_down[i // 2, i % 2]}

    local = {**ffn_shards(0), "in_t": ab_w_in[0].T, "uq": w_uq[0], "ukv_t": w_ukv[0].T, "ab_out": ab_w_out[0],
             **ffn_shards(1), **ffn_shards(2), "cin_t": conv_w_in[0].T, "c_out": conv_w_out[0], **ffn_shards(3)}
    ffn_groups = [[[f"gate_t{i}", f"up_t{i}"], [f"down{i}"]] for i in range(4)]
    groups = [*ffn_groups[0], ["in_t"], ["uq", "ukv_t", "ab_out"], *ffn_groups[1], *ffn_groups[2],
              ["cin_t", "c_out"], *ffn_groups[3]]
    feed = _Feed({nm: a.astype(BF16) for nm, a in local.items()}, groups, me)

    small = jnp.concatenate([c.reshape(-1), norm_g.reshape(-1), conv_w.reshape(-1)])
    small_n = -(-small.shape[0] // 1024) * 1024
    small = jnp.pad(small, (0, small_n - small.shape[0])).reshape(small_n // 128, 128)
    small_all = _exchange("gather_small", small, False).reshape(N_DEV, small_n)
    c_all = small_all[:, :d]
    o1 = d + 6 * ng_sh
    norm_g_full = small_all[:, d:o1].reshape(N_DEV, 2, 3, ng_sh).transpose(1, 2, 0, 3).reshape(2, 3, d)
    conv_w_full = small_all[:, o1:o1 + 3 * cw_sh].reshape(N_DEV, 3, cw_sh).transpose(1, 0, 2).reshape(3, d)

    cond = jnp.concatenate([c_all, jnp.broadcast_to(c_ctx[None, :], (N_DEV, d))], axis=0)
    sil, dsil = _silu_rows("mod_silu", cond)
    w_mod_b = w_mod.astype(BF16)
    b_sh = lax.dynamic_slice(b_mod, (0, me * mod_cols), (2, mod_cols))
    m_part = jnp.stack([_mm(f"mod_fwd{l}", [(sil, w_mod_b[l])], "nn", F32, 16, 384, bias=b_sh[l:l + 1])
                        for l in range(2)], axis=1)
    m_all = _exchange("gather_mod", m_part.reshape(-1, 128), False).reshape(N_DEV, 2 * N_DEV, 2, mod_cols)
    m_mine = lax.dynamic_index_in_dim(m_all, me, axis=1, keepdims=False)
    mod_h = m_mine.transpose(1, 0, 2).reshape(2, N_MOD, d)
    mod_g = m_all[:, N_DEV, 0, :].reshape(N_MOD, d)

    first = feed.start("first", [nm for grp in groups[:3] for nm in grp], m_all)
    feed.start("rest", [nm for grp in groups[3:] for nm in grp], first)

    sq_cols, grad_x, g = _local_step(x[0], ctx[0], loss_target[0], mod_h, mod_g, norm_g_full, feed, pool_w[0],
                                  pool_scale, q_norm_g, kv_norm_g, conv_w_full, final_norm_g)
    w_of, m_of, v_of = (dict(zip(WEIGHT_NAMES, t)) for t in (weights, moms, vels))
    results = {}

    def update(nm, grad, view=lambda t: t):
        outs = _adamw(f"adamw_{nm}", view(w_of[nm]), grad.reshape(view(w_of[nm]).shape), view(m_of[nm]), view(v_of[nm]))
        results[nm] = tuple(view(t) for t in (grad.reshape(view(w_of[nm]).shape), *outs))

    def swap(t):
        return jnp.swapaxes(t, -1, -2)

    stacked = ("gate_t", "up_t", "down")
    dm = jnp.stack([g["mod_h"], jnp.stack([g["mod_g"], jnp.zeros_like(g["mod_g"])])])
    dm_rows = dm.reshape(-1, 128)
    dm_send, dm_recv, dm_srcs, dm_lands, dm_started = _exchange_start(
        "dmod_start", [dm_rows], [_landing(dm_rows, me)], False, None)
    small_g = jnp.concatenate([g["pool_w"].reshape(-1), g["pool_scale"].reshape(-1), g["q_norm_g"].reshape(-1),
                               g["kv_norm_g"].reshape(-1), g["final_norm_g"].reshape(-1), g["norm_g"].reshape(-1),
                               g["conv_w"].reshape(-1), sq_cols.reshape(-1)])
    sizes = [pool_w.size, pool_scale.size, q_norm_g.size, kv_norm_g.size, d, 6 * d, 3 * d, d]
    sg_n = -(-small_g.shape[0] // 1024) * 1024
    small_rows = jnp.pad(small_g, (0, sg_n - small_g.shape[0])).reshape(-1, 128)
    sg_send, sg_recv, sg_srcs, sg_lands, sg_started = _exchange_start(
        "small_grads_start", [small_rows], [_landing(small_rows, me)], False, dm_started)
    early = feed.collect(["l1f1", "l1m", "l1f0", "l0f1", "l0m"], [grad_x, sg_started], stacked)
    update("ab_w_in", early["in_t"], swap)
    update("w_uq", early["uq"])
    update("w_ukv", early["ukv_t"].T)
    update("ab_w_out", early["ab_out"])
    update("conv_w_in", early["cin_t"].T)
    update("conv_w_out", early["c_out"])
    ffn = {}
    for nm, prefix, view in (("ffn_w_gate", "gate_t", swap), ("ffn_w_up", "up_t", swap),
                             ("ffn_w_down", "down", lambda t: t)):
        w4, m4, v4 = (view(t).reshape((4,) + view(t).shape[-2:]) for t in (w_of[nm], m_of[nm], v_of[nm]))
        prev = None
        for i in (3, 2, 1):
            prev = _adamw_part(f"adamw_{nm}{i}", i, w4, early[f"{prefix}{i}"], me, m4, v4, prev)
        ffn[nm] = (prefix, view, w4, m4, v4, prev)
    done_early = [results[nm][1] for nm in results] + [state[5][1] for state in ffn.values()]
    late = feed.collect(["l0f0"], done_early, stacked)
    for nm, (prefix, view, w4, m4, v4, prev) in ffn.items():
        outs = _adamw_part(f"adamw_{nm}0", 0, w4, late[f"{prefix}0"], me, m4, v4, prev)
        results[nm] = tuple(view(t.reshape(view(w_of[nm]).shape)) for t in outs)

    _, dm_got = _exchange_wait("dmod_wait", dm_send, dm_recv, dm_srcs, dm_lands, [0], False,
                               [results["ffn_w_down"][1]])
    dm_all = dm_got[0].reshape(N_DEV, 2, 2, N_MOD * d)
    grad_b_mod = _sum_rows("dmod_bias", dm_all.reshape(2 * N_DEV, 2 * N_MOD * d)).reshape(2, N_MOD * d)
    dm_sh = lax.dynamic_slice(dm_all, (0, 0, 0, me * mod_cols), (N_DEV, 2, 2, mod_cols))
    gw_mod, cctx_parts = [], []
    for l in range(2):
        dm_l = dm_sh[:, :, l, :].transpose(1, 0, 2).reshape(2 * N_DEV, mod_cols).astype(BF16)
        gw_mod.append(_mm(f"mod_dw{l}", [(sil, dm_l)], "tn", F32, 512, 384))
        dm_ctx = jnp.concatenate([dm_l[N_DEV:], jnp.zeros((N_DEV, mod_cols), BF16)], axis=0)
        cctx_parts.append(_mm(f"mod_dcond{l}", [(dm_ctx, w_mod_b[l])], "nt", F32, 16, 512))
    cctx_part = _sum_rows("mod_dcond_sum", jnp.concatenate(cctx_parts, axis=0))
    update("w_mod", jnp.stack(gw_mod))
    update("b_mod", grad_b_mod)

    _, sg_got = _exchange_wait("small_grads_wait", sg_send, sg_recv, sg_srcs, sg_lands, [0], False, [cctx_part])
    sg = _sum_rows("small_grads_sum", sg_got[0].reshape(N_DEV, sg_n))[0]
    cctx_all = _exchange("gather_cctx", cctx_part.reshape(-1, 128), False).reshape(N_DEV, d)
    g_c_ctx = _sum_rows("cctx_sum", cctx_all, dsil[N_DEV:N_DEV + 1])[0]
    cuts, pos = [], 0
    for sz in sizes:
        cuts.append(sg[pos:pos + sz])
        pos += sz
    g_pool_w, g_pool_scale, g_q_norm, g_kv_norm, g_final, g_norm_full, g_conv_full, sq_all = cuts
    loss = 0.5 * jnp.sum(sq_all) / d
    update("c_ctx", g_c_ctx)
    update("norm_g", lax.dynamic_slice(g_norm_full.reshape(2, 3, d), (0, 0, me * ng_sh), (2, 3, ng_sh)))
    update("conv_w", lax.dynamic_slice(g_conv_full.reshape(3, d), (0, me * cw_sh), (3, cw_sh)))
    update("pool_w", g_pool_w)
    update("pool_scale", g_pool_scale)
    update("q_norm_g", g_q_norm)
    update("kv_norm_g", g_kv_norm)
    update("final_norm_g", g_final)
    outs = [results[nm] for nm in WEIGHT_NAMES]
    return (loss, grad_x[None], *[o[0] for o in outs], *[o[1] for o in outs], *[o[2] for o in outs],
            *[o[3] for o in outs])
```
